```python
import math
import jax, jax.numpy as jnp
from jax import lax
import numpy as np

D_MODEL = 1024
BATCH = 8
SEQ = 4096
DEPTH = 2

N_EVEN = (DEPTH + 1) // 2
N_ODD = DEPTH // 2

POOL_WINDOWS = (2, 4, 8, 16)
N_POOL_GROUPS = len(POOL_WINDOWS)
POOL_DIM = D_MODEL // 2
POOL_GROUP = POOL_DIM // N_POOL_GROUPS

HEAD_DIM = 64
N_HEADS = (D_MODEL // 2) // HEAD_DIM
N_KV_HEADS = 2
GQ = N_HEADS // N_KV_HEADS
Q_DIM = N_HEADS * HEAD_DIM
KV_DIM = N_KV_HEADS * HEAD_DIM
WINDOW = 128
BLOCK = 128
ROPE_THETA = 10000.0
MIX_IN_DIM = POOL_DIM + Q_DIM + 2 * KV_DIM
MIX_OUT_DIM = POOL_DIM + Q_DIM
MAX_POS_OFFSET = 1024

SSM_EXPAND = 2
SSM_D_INNER = SSM_EXPAND * D_MODEL
SSM_HEAD_DIM = 64
SSM_HEADS = SSM_D_INNER // SSM_HEAD_DIM
SSM_GROUPS = 8
SSM_STATE = 128
SSM_CONV = 4
SSM_CHUNK = 128
SSM_CONV_DIM = SSM_D_INNER + 2 * SSM_GROUPS * SSM_STATE
SSM_IN_DIM = SSM_D_INNER + SSM_CONV_DIM + SSM_HEADS

D_FF = 2816
FFN_CONV = 3

NORM_EPS = 1e-6
SSM_NORM_EPS = 1e-5

kernel_name = "hybrid_pool_swa_ssd_convffn"


def rms_norm(x, w, eps=NORM_EPS):
    xf = x.astype(jnp.float32)
    y = xf * lax.rsqrt(jnp.mean(xf * xf, axis=-1, keepdims=True) + eps)
    return (y * w.astype(jnp.float32)).astype(x.dtype)


def causal_dwconv(x, w, b):
    K, C = w.shape
    y = lax.conv_general_dilated(
        x, w.astype(x.dtype)[:, None, :], window_strides=(1,), padding=[(K - 1, 0)],
        dimension_numbers=("NWC", "WIO", "NWC"), feature_group_count=C)
    return y + b.astype(x.dtype)


def rope_tables(positions):
    inv_freq = ROPE_THETA ** (-jnp.arange(0, HEAD_DIM, 2, dtype=jnp.float32) / HEAD_DIM)
    ang = positions.astype(jnp.float32)[..., None] * inv_freq
    ang = jnp.concatenate([ang, ang], axis=-1)
    return jnp.cos(ang)[:, :, None, :], jnp.sin(ang)[:, :, None, :]


def apply_rope(t, cos, sin):
    tf = t.astype(jnp.float32)
    half = HEAD_DIM // 2
    rot = jnp.concatenate([-tf[..., half:], tf[..., :half]], axis=-1)
    return (tf * cos + rot * sin).astype(t.dtype)


def multiscale_pool(u):
    B, S, _ = u.shape
    ug = u.reshape(B, S, N_POOL_GROUPS, POOL_GROUP).astype(jnp.float32)
    cs = jnp.pad(jnp.cumsum(ug, axis=1), ((0, 0), (1, 0), (0, 0), (0, 0)))
    t1 = jnp.arange(1, S + 1)
    means = []
    for g, w in enumerate(POOL_WINDOWS):
        upper = cs[:, 1:, g]
        lower = jnp.pad(cs[:, :S + 1 - w, g], ((0, 0), (w - 1, 0), (0, 0)))
        cnt = jnp.minimum(t1, w).astype(jnp.float32)[None, :, None]
        means.append((upper - lower) / cnt)
    return jnp.stack(means, axis=2) - ug


def sliding_window_attention(q, k, v, sinks):
    B, S, _, _ = q.shape
    nb = S // BLOCK
    qb = q.reshape(B, nb, BLOCK, N_KV_HEADS, GQ, HEAD_DIM)

    def with_prev(t):
        t = t.reshape(B, nb, BLOCK, N_KV_HEADS, HEAD_DIM)
        prev = jnp.pad(t[:, :-1], ((0, 0), (1, 0), (0, 0), (0, 0), (0, 0)))
        return jnp.concatenate([prev, t], axis=2)

    kk, vv = with_prev(k), with_prev(v)
    s = jnp.einsum("bnqkgd,bnskd->bkgnqs", qb, kk).astype(jnp.float32) * (HEAD_DIM ** -0.5)
    qi = jnp.arange(BLOCK)[:, None]
    kj = jnp.arange(2 * BLOCK)[None, :]
    rel = qi + BLOCK - kj
    band = (rel >= 0) & (rel < WINDOW)
    valid = (jnp.arange(nb)[:, None, None] > 0) | (kj[None] >= BLOCK)
    mask = band[None] & valid
    s = jnp.where(mask, s, -jnp.inf)
    sink = jnp.broadcast_to(sinks.astype(jnp.float32).reshape(1, N_KV_HEADS, GQ, 1, 1, 1),
                            s.shape[:-1] + (1,))
    p = jax.nn.softmax(jnp.concatenate([s, sink], axis=-1), axis=-1)[..., :-1]
    o = jnp.einsum("bkgnqs,bnskd->bnqkgd", p.astype(v.dtype), vv)
    return o.reshape(B, S, Q_DIM)


def pool_attention_mixer(h, cos, sin, w_in, pool_w, pool_scale, sinks, w_out):
    B, S, _ = h.shape
    proj = h @ w_in
    u, q, k, v = jnp.split(proj, [POOL_DIM, POOL_DIM + Q_DIM, POOL_DIM + Q_DIM + KV_DIM], axis=-1)
    pooled = multiscale_pool(u).astype(h.dtype)
    pooled = jnp.einsum("bsgc,gcd->bsgd", pooled, pool_w).reshape(B, S, POOL_DIM) * pool_scale
    q = apply_rope(q.reshape(B, S, N_HEADS, HEAD_DIM), cos, sin)
    k = apply_rope(k.reshape(B, S, N_KV_HEADS, HEAD_DIM), cos, sin)
    v = v.reshape(B, S, N_KV_HEADS, HEAD_DIM)
    attn = sliding_window_attention(q, k, v, sinks)
    return jnp.concatenate([pooled, attn], axis=-1) @ w_out


def ssd_scan(x, dt, A, Bm, Cm):
    b, s, h, p = x.shape
    g, n = Bm.shape[2:]
    r = h // g
    c = s // SSM_CHUNK
    X = (x * dt[..., None]).reshape(b, c, SSM_CHUNK, g, r, p)
    a = (dt * A).reshape(b, c, SSM_CHUNK, g, r).transpose(0, 3, 4, 1, 2)
    a_cs = jnp.cumsum(a, axis=-1)
    Bc = Bm.reshape(b, c, SSM_CHUNK, g, n)
    Cc = Cm.reshape(b, c, SSM_CHUNK, g, n)
    tril = jnp.tril(jnp.ones((SSM_CHUNK, SSM_CHUNK), dtype=bool))
    seg = a_cs[..., :, None] - a_cs[..., None, :]
    Lmat = jnp.exp(jnp.where(tril, seg, -jnp.inf))
    CB = jnp.einsum("bclgn,bcsgn->bgcls", Cc, Bc)
    y_diag = jnp.einsum("bgcls,bgrcls,bcsgrp->bclgrp", CB, Lmat, X)
    decay = jnp.exp(a_cs[..., -1:] - a_cs)
    states = jnp.einsum("bclgn,bgrcl,bclgrp->bcgrpn", Bc, decay, X)
    chunk_decay = jnp.exp(a_cs[..., -1])

    def step(state, inp):
        st, dec = inp
        return state * dec[..., None, None] + st, state

    h0 = jnp.zeros((b, g, r, p, n), jnp.float32)
    _, prev = lax.scan(step, h0, (jnp.moveaxis(states, 1, 0), jnp.moveaxis(chunk_decay, 3, 0)))
    y_off = jnp.einsum("bclgn,cbgrpn,bgrcl->bclgrp", Cc, prev, jnp.exp(a_cs))
    return (y_diag + y_off).reshape(b, s, h, p)


def ssd_mixer(h, w_in, conv_w, conv_b, dt_bias, A_log, D_skip, norm_w, w_out):
    B, S, _ = h.shape
    proj = h @ w_in
    z, xbc, dt = jnp.split(proj, [SSM_D_INNER, SSM_D_INNER + SSM_CONV_DIM], axis=-1)
    xbc = jax.nn.silu(causal_dwconv(xbc, conv_w, conv_b))
    xs, Bm, Cm = jnp.split(xbc, [SSM_D_INNER, SSM_D_INNER + SSM_GROUPS * SSM_STATE], axis=-1)
    xs = xs.reshape(B, S, SSM_HEADS, SSM_HEAD_DIM).astype(jnp.float32)
    Bm = Bm.reshape(B, S, SSM_GROUPS, SSM_STATE).astype(jnp.float32)
    Cm = Cm.reshape(B, S, SSM_GROUPS, SSM_STATE).astype(jnp.float32)
    dt = jax.nn.softplus(dt.astype(jnp.float32) + dt_bias.astype(jnp.float32))
    A = -jnp.exp(A_log.astype(jnp.float32))
    y = ssd_scan(xs, dt, A, Bm, Cm) + D_skip.astype(jnp.float32)[:, None] * xs
    y = y.reshape(B, S, SSM_D_INNER) * jax.nn.silu(z.astype(jnp.float32))
    y = rms_norm(y, norm_w, SSM_NORM_EPS).astype(h.dtype)
    return y @ w_out


def conv_ffn(h, w_up, conv_w, conv_b, w_down):
    hid = causal_dwconv(h @ w_up, conv_w, conv_b)
    u, g = jnp.split(hid, 2, axis=-1)
    return (jax.nn.silu(g) * u) @ w_down


def _fwd_setup_inputs(seed: int = 0) -> dict:
    key = jax.random.key(seed)
    ks = jax.random.split(key, 24)
    f32 = jnp.float32
    nrm = lambda k, shape, scale: jax.random.normal(k, shape, f32) * scale
    x = jax.random.normal(ks[0], (BATCH, SEQ, D_MODEL), f32)
    offsets = jax.random.randint(ks[1], (BATCH, 1), 0, MAX_POS_OFFSET, dtype=jnp.int32)
    positions = (offsets + jnp.arange(SEQ, dtype=jnp.int32)[None, :]).astype(jnp.int32)
    dt0 = jnp.exp(jax.random.uniform(ks[13], (N_ODD, SSM_HEADS), f32, math.log(1e-3), math.log(1e-1)))
    return {
        "x": x,
        "positions": positions,
        "norm_mix": 1.0 + nrm(ks[2], (DEPTH, D_MODEL), 0.02),
        "norm_ffn": 1.0 + nrm(ks[3], (DEPTH, D_MODEL), 0.02),
        "norm_final": 1.0 + nrm(ks[4], (D_MODEL,), 0.02),
        "mix_w_in": nrm(ks[5], (N_EVEN, D_MODEL, MIX_IN_DIM), D_MODEL ** -0.5),
        "pool_w": nrm(ks[6], (N_EVEN, N_POOL_GROUPS, POOL_GROUP, POOL_GROUP), POOL_GROUP ** -0.5),
        "pool_scale": 1.0 + nrm(ks[7], (N_EVEN, POOL_DIM), 0.1),
        "attn_sinks": nrm(ks[8], (N_EVEN, N_HEADS), 1.0),
        "mix_w_out": nrm(ks[9], (N_EVEN, MIX_OUT_DIM, D_MODEL), MIX_OUT_DIM ** -0.5),
        "ssm_w_in": nrm(ks[10], (N_ODD, D_MODEL, SSM_IN_DIM), D_MODEL ** -0.5),
        "ssm_conv_w": nrm(ks[11], (N_ODD, SSM_CONV, SSM_CONV_DIM), SSM_CONV ** -0.5),
        "ssm_conv_b": nrm(ks[12], (N_ODD, SSM_CONV_DIM), 0.02),
        "ssm_dt_bias": dt0 + jnp.log(-jnp.expm1(-dt0)),
        "ssm_A_log": jnp.log(jax.random.uniform(ks[14], (N_ODD, SSM_HEADS), f32, 1.0, 16.0)),
        "ssm_D": 1.0 + nrm(ks[15], (N_ODD, SSM_HEADS), 0.1),
        "ssm_norm": 1.0 + nrm(ks[16], (N_ODD, SSM_D_INNER), 0.02),
        "ssm_w_out": nrm(ks[17], (N_ODD, SSM_D_INNER, D_MODEL), SSM_D_INNER ** -0.5),
        "ffn_w_up": nrm(ks[18], (DEPTH, D_MODEL, 2 * D_FF), D_MODEL ** -0.5),
        "ffn_conv_w": nrm(ks[19], (DEPTH, FFN_CONV, 2 * D_FF), FFN_CONV ** -0.5),
        "ffn_conv_b": nrm(ks[20], (DEPTH, 2 * D_FF), 0.02),
        "ffn_w_down": nrm(ks[21], (DEPTH, D_FF, D_MODEL), D_FF ** -0.5),
    }


def _fwd_reference(x, positions, norm_mix, norm_ffn, norm_final, mix_w_in, pool_w, pool_scale,
              attn_sinks, mix_w_out, ssm_w_in, ssm_conv_w, ssm_conv_b, ssm_dt_bias, ssm_A_log,
              ssm_D, ssm_norm, ssm_w_out, ffn_w_up, ffn_conv_w, ffn_conv_b, ffn_w_down):
    cos, sin = rope_tables(positions)
    for i in range(DEPTH):
        j = i // 2
        h = rms_norm(x, norm_mix[i])
        if i % 2 == 0:
            x = x + pool_attention_mixer(h, cos, sin, mix_w_in[j], pool_w[j], pool_scale[j],
                                         attn_sinks[j], mix_w_out[j])
        else:
            x = x + ssd_mixer(h, ssm_w_in[j], ssm_conv_w[j], ssm_conv_b[j], ssm_dt_bias[j],
                              ssm_A_log[j], ssm_D[j], ssm_norm[j], ssm_w_out[j])
        x = x + conv_ffn(rms_norm(x, norm_ffn[i]), ffn_w_up[i], ffn_conv_w[i], ffn_conv_b[i],
                         ffn_w_down[i])
    return rms_norm(x, norm_final)


import jax as _jax
import jax.numpy as _jnp

TWIN_FORMAT = 'train_step'
FWD_PARAMS = ['x', 'positions', 'norm_mix', 'norm_ffn', 'norm_final', 'mix_w_in', 'pool_w', 'pool_scale', 'attn_sinks', 'mix_w_out', 'ssm_w_in', 'ssm_conv_w', 'ssm_conv_b', 'ssm_dt_bias', 'ssm_A_log', 'ssm_D', 'ssm_norm', 'ssm_w_out', 'ffn_w_up', 'ffn_conv_w', 'ffn_conv_b', 'ffn_w_down']
TWIN_WEIGHTS = ['norm_mix', 'norm_ffn', 'norm_final', 'mix_w_in', 'pool_w', 'pool_scale', 'attn_sinks', 'mix_w_out', 'ssm_w_in', 'ssm_conv_w', 'ssm_conv_b', 'ssm_dt_bias', 'ssm_A_log', 'ssm_D', 'ssm_norm', 'ssm_w_out', 'ffn_w_up', 'ffn_conv_w', 'ffn_conv_b', 'ffn_w_down']
TWIN_DIFF_INPUT = 'x'
TWIN_INPUTS = ['x', 'positions', 'norm_mix', 'norm_ffn', 'norm_final', 'mix_w_in', 'pool_w', 'pool_scale', 'attn_sinks', 'mix_w_out', 'ssm_w_in', 'ssm_conv_w', 'ssm_conv_b', 'ssm_dt_bias', 'ssm_A_log', 'ssm_D', 'ssm_norm', 'ssm_w_out', 'ffn_w_up', 'ffn_conv_w', 'ffn_conv_b', 'ffn_w_down', 'loss_target', 'm_norm_mix', 'm_norm_ffn', 'm_norm_final', 'm_mix_w_in', 'm_pool_w', 'm_pool_scale', 'm_attn_sinks', 'm_mix_w_out', 'm_ssm_w_in', 'm_ssm_conv_w', 'm_ssm_conv_b', 'm_ssm_dt_bias', 'm_ssm_A_log', 'm_ssm_D', 'm_ssm_norm', 'm_ssm_w_out', 'm_ffn_w_up', 'm_ffn_conv_w', 'm_ffn_conv_b', 'm_ffn_w_down', 'v_norm_mix', 'v_norm_ffn', 'v_norm_final', 'v_mix_w_in', 'v_pool_w', 'v_pool_scale', 'v_attn_sinks', 'v_mix_w_out', 'v_ssm_w_in', 'v_ssm_conv_w', 'v_ssm_conv_b', 'v_ssm_dt_bias', 'v_ssm_A_log', 'v_ssm_D', 'v_ssm_norm', 'v_ssm_w_out', 'v_ffn_w_up', 'v_ffn_conv_w', 'v_ffn_conv_b', 'v_ffn_w_down']
TWIN_OUTPUTS = ['loss', 'grad_x', 'grad_norm_mix', 'grad_norm_ffn', 'grad_norm_final', 'grad_mix_w_in', 'grad_pool_w', 'grad_pool_scale', 'grad_attn_sinks', 'grad_mix_w_out', 'grad_ssm_w_in', 'grad_ssm_conv_w', 'grad_ssm_conv_b', 'grad_ssm_dt_bias', 'grad_ssm_A_log', 'grad_ssm_D', 'grad_ssm_norm', 'grad_ssm_w_out', 'grad_ffn_w_up', 'grad_ffn_conv_w', 'grad_ffn_conv_b', 'grad_ffn_w_down', 'delta_norm_mix', 'delta_norm_ffn', 'delta_norm_final', 'delta_mix_w_in', 'delta_pool_w', 'delta_pool_scale', 'delta_attn_sinks', 'delta_mix_w_out', 'delta_ssm_w_in', 'delta_ssm_conv_w', 'delta_ssm_conv_b', 'delta_ssm_dt_bias', 'delta_ssm_A_log', 'delta_ssm_D', 'delta_ssm_norm', 'delta_ssm_w_out', 'delta_ffn_w_up', 'delta_ffn_conv_w', 'delta_ffn_conv_b', 'delta_ffn_w_down', 'new_m_norm_mix', 'new_m_norm_ffn', 'new_m_norm_final', 'new_m_mix_w_in', 'new_m_pool_w', 'new_m_pool_scale', 'new_m_attn_sinks', 'new_m_mix_w_out', 'new_m_ssm_w_in', 'new_m_ssm_conv_w', 'new_m_ssm_conv_b', 'new_m_ssm_dt_bias', 'new_m_ssm_A_log', 'new_m_ssm_D', 'new_m_ssm_norm', 'new_m_ssm_w_out', 'new_m_ffn_w_up', 'new_m_ffn_conv_w', 'new_m_ffn_conv_b', 'new_m_ffn_w_down', 'new_v_norm_mix', 'new_v_norm_ffn', 'new_v_norm_final', 'new_v_mix_w_in', 'new_v_pool_w', 'new_v_pool_scale', 'new_v_attn_sinks', 'new_v_mix_w_out', 'new_v_ssm_w_in', 'new_v_ssm_conv_w', 'new_v_ssm_conv_b', 'new_v_ssm_dt_bias', 'new_v_ssm_A_log', 'new_v_ssm_D', 'new_v_ssm_norm', 'new_v_ssm_w_out', 'new_v_ffn_w_up', 'new_v_ffn_conv_w', 'new_v_ffn_conv_b', 'new_v_ffn_w_down']
TWIN_LEAF_KINDS = {'loss': 'loss', 'grad_x': 'grad_x', 'grad_norm_mix': 'grad_w', 'grad_norm_ffn': 'grad_w', 'grad_norm_final': 'grad_w', 'grad_mix_w_in': 'grad_w', 'grad_pool_w': 'grad_w', 'grad_pool_scale': 'grad_w', 'grad_attn_sinks': 'grad_w', 'grad_mix_w_out': 'grad_w', 'grad_ssm_w_in': 'grad_w', 'grad_ssm_conv_w': 'grad_w', 'grad_ssm_conv_b': 'grad_w', 'grad_ssm_dt_bias': 'grad_w', 'grad_ssm_A_log': 'grad_w', 'grad_ssm_D': 'grad_w', 'grad_ssm_norm': 'grad_w', 'grad_ssm_w_out': 'grad_w', 'grad_ffn_w_up': 'grad_w', 'grad_ffn_conv_w': 'grad_w', 'grad_ffn_conv_b': 'grad_w', 'grad_ffn_w_down': 'grad_w', 'delta_norm_mix': 'delta_w', 'delta_norm_ffn': 'delta_w', 'delta_norm_final': 'delta_w', 'delta_mix_w_in': 'delta_w', 'delta_pool_w': 'delta_w', 'delta_pool_scale': 'delta_w', 'delta_attn_sinks': 'delta_w', 'delta_mix_w_out': 'delta_w', 'delta_ssm_w_in': 'delta_w', 'delta_ssm_conv_w': 'delta_w', 'delta_ssm_conv_b': 'delta_w', 'delta_ssm_dt_bias': 'delta_w', 'delta_ssm_A_log': 'delta_w', 'delta_ssm_D': 'delta_w', 'delta_ssm_norm': 'delta_w', 'delta_ssm_w_out': 'delta_w', 'delta_ffn_w_up': 'delta_w', 'delta_ffn_conv_w': 'delta_w', 'delta_ffn_conv_b': 'delta_w', 'delta_ffn_w_down': 'delta_w', 'new_m_norm_mix': 'new_m', 'new_m_norm_ffn': 'new_m', 'new_m_norm_final': 'new_m', 'new_m_mix_w_in': 'new_m', 'new_m_pool_w': 'new_m', 'new_m_pool_scale': 'new_m', 'new_m_attn_sinks': 'new_m', 'new_m_mix_w_out': 'new_m', 'new_m_ssm_w_in': 'new_m', 'new_m_ssm_conv_w': 'new_m', 'new_m_ssm_conv_b': 'new_m', 'new_m_ssm_dt_bias': 'new_m', 'new_m_ssm_A_log': 'new_m', 'new_m_ssm_D': 'new_m', 'new_m_ssm_norm': 'new_m', 'new_m_ssm_w_out': 'new_m', 'new_m_ffn_w_up': 'new_m', 'new_m_ffn_conv_w': 'new_m', 'new_m_ffn_conv_b': 'new_m', 'new_m_ffn_w_down': 'new_m', 'new_v_norm_mix': 'new_v', 'new_v_norm_ffn': 'new_v', 'new_v_norm_final': 'new_v', 'new_v_mix_w_in': 'new_v', 'new_v_pool_w': 'new_v', 'new_v_pool_scale': 'new_v', 'new_v_attn_sinks': 'new_v', 'new_v_mix_w_out': 'new_v', 'new_v_ssm_w_in': 'new_v', 'new_v_ssm_conv_w': 'new_v', 'new_v_ssm_conv_b': 'new_v', 'new_v_ssm_dt_bias': 'new_v', 'new_v_ssm_A_log': 'new_v', 'new_v_ssm_D': 'new_v', 'new_v_ssm_norm': 'new_v', 'new_v_ssm_w_out': 'new_v', 'new_v_ffn_w_up': 'new_v', 'new_v_ffn_conv_w': 'new_v', 'new_v_ffn_conv_b': 'new_v', 'new_v_ffn_w_down': 'new_v'}


def _forward(args):
    return _fwd_reference(*[args[k] for k in FWD_PARAMS])


def _output_shape():
    def fwd():
        inp = _fwd_setup_inputs(0)
        return _fwd_reference(*[inp[k] for k in FWD_PARAMS])
    out = _jax.eval_shape(fwd)
    return out.shape, out.dtype

N_MICROBATCH = 1
ADAM_LR = 0.001
ADAM_B1 = 0.9
ADAM_B2 = 0.999
ADAM_EPS = 1e-08
ADAM_WD = 0.01
ADAM_STEP = 10
PER_EXAMPLE_BATCH_AXIS = {'x': 0, 'positions': 0, 'loss_target': 0}
SHARED_INPUTS = []
_WEIGHT_DTYPES = {'norm_mix': _jnp.float32, 'norm_ffn': _jnp.float32, 'norm_final': _jnp.float32, 'mix_w_in': _jnp.float32, 'pool_w': _jnp.float32, 'pool_scale': _jnp.float32, 'attn_sinks': _jnp.float32, 'mix_w_out': _jnp.float32, 'ssm_w_in': _jnp.float32, 'ssm_conv_w': _jnp.float32, 'ssm_conv_b': _jnp.float32, 'ssm_dt_bias': _jnp.float32, 'ssm_A_log': _jnp.float32, 'ssm_D': _jnp.float32, 'ssm_norm': _jnp.float32, 'ssm_w_out': _jnp.float32, 'ffn_w_up': _jnp.float32, 'ffn_conv_w': _jnp.float32, 'ffn_conv_b': _jnp.float32, 'ffn_w_down': _jnp.float32}
MOMENT_SCALE = {'norm_mix': 1.636311e-01, 'norm_ffn': 1.231859e-01, 'norm_final': 3.201274e+01, 'mix_w_in': 1.273076e-01, 'pool_w': 1.893070e-01, 'pool_scale': 2.130817e-01, 'attn_sinks': 4.260492e-02, 'mix_w_out': 1.359747e-01, 'ssm_w_in': 7.156801e-02, 'ssm_conv_w': 6.312421e-02, 'ssm_conv_b': 8.182030e-02, 'ssm_dt_bias': 2.201204e-01, 'ssm_A_log': 1.897572e-01, 'ssm_D': 3.862165e-01, 'ssm_norm': 8.293236e-02, 'ssm_w_out': 1.140060e-01, 'ffn_w_up': 5.303187e-02, 'ffn_conv_w': 5.367721e-02, 'ffn_conv_b': 5.400095e-02, 'ffn_w_down': 8.644645e-02}


def _to_microbatches(a, axis):
    t = _jnp.moveaxis(a, axis, 0)
    t = t.reshape((N_MICROBATCH, t.shape[0] // N_MICROBATCH) + t.shape[1:])
    return _jnp.moveaxis(t, 1, axis + 1)


def setup_inputs(seed: int = 0) -> dict:
    inp = _fwd_setup_inputs(seed)
    key = _jax.random.fold_in(_jax.random.key(seed), 7919)
    shape, _ = _output_shape()
    out = dict(inp)
    out["loss_target"] = _jax.random.normal(_jax.random.fold_in(key, 0), shape, _jnp.float32)
    for i, name in enumerate(TWIN_WEIGHTS):
        w = inp[name].astype(_jnp.float32)
        if MOMENT_SCALE is None:
            s = _jnp.sqrt(_jnp.mean(_jnp.square(w)) + 1e-30)
        else:
            s = MOMENT_SCALE[name]
        km, kv = _jax.random.split(_jax.random.fold_in(key, i + 1))
        out[name] = w
        out["m_" + name] = s * _jax.random.normal(km, w.shape, _jnp.float32)
        out["v_" + name] = (s * s) * _jax.random.uniform(kv, w.shape, _jnp.float32, 0.5, 1.5)
    if N_MICROBATCH > 1:
        for name, axis in PER_EXAMPLE_BATCH_AXIS.items():
            out[name] = _to_microbatches(out[name], axis)
    return {'x': out['x'], 'positions': out['positions'], 'norm_mix': out['norm_mix'], 'norm_ffn': out['norm_ffn'], 'norm_final': out['norm_final'], 'mix_w_in': out['mix_w_in'], 'pool_w': out['pool_w'], 'pool_scale': out['pool_scale'], 'attn_sinks': out['attn_sinks'], 'mix_w_out': out['mix_w_out'], 'ssm_w_in': out['ssm_w_in'], 'ssm_conv_w': out['ssm_conv_w'], 'ssm_conv_b': out['ssm_conv_b'], 'ssm_dt_bias': out['ssm_dt_bias'], 'ssm_A_log': out['ssm_A_log'], 'ssm_D': out['ssm_D'], 'ssm_norm': out['ssm_norm'], 'ssm_w_out': out['ssm_w_out'], 'ffn_w_up': out['ffn_w_up'], 'ffn_conv_w': out['ffn_conv_w'], 'ffn_conv_b': out['ffn_conv_b'], 'ffn_w_down': out['ffn_w_down'], 'loss_target': out['loss_target'], 'm_norm_mix': out['m_norm_mix'], 'm_norm_ffn': out['m_norm_ffn'], 'm_norm_final': out['m_norm_final'], 'm_mix_w_in': out['m_mix_w_in'], 'm_pool_w': out['m_pool_w'], 'm_pool_scale': out['m_pool_scale'], 'm_attn_sinks': out['m_attn_sinks'], 'm_mix_w_out': out['m_mix_w_out'], 'm_ssm_w_in': out['m_ssm_w_in'], 'm_ssm_conv_w': out['m_ssm_conv_w'], 'm_ssm_conv_b': out['m_ssm_conv_b'], 'm_ssm_dt_bias': out['m_ssm_dt_bias'], 'm_ssm_A_log': out['m_ssm_A_log'], 'm_ssm_D': out['m_ssm_D'], 'm_ssm_norm': out['m_ssm_norm'], 'm_ssm_w_out': out['m_ssm_w_out'], 'm_ffn_w_up': out['m_ffn_w_up'], 'm_ffn_conv_w': out['m_ffn_conv_w'], 'm_ffn_conv_b': out['m_ffn_conv_b'], 'm_ffn_w_down': out['m_ffn_w_down'], 'v_norm_mix': out['v_norm_mix'], 'v_norm_ffn': out['v_norm_ffn'], 'v_norm_final': out['v_norm_final'], 'v_mix_w_in': out['v_mix_w_in'], 'v_pool_w': out['v_pool_w'], 'v_pool_scale': out['v_pool_scale'], 'v_attn_sinks': out['v_attn_sinks'], 'v_mix_w_out': out['v_mix_w_out'], 'v_ssm_w_in': out['v_ssm_w_in'], 'v_ssm_conv_w': out['v_ssm_conv_w'], 'v_ssm_conv_b': out['v_ssm_conv_b'], 'v_ssm_dt_bias': out['v_ssm_dt_bias'], 'v_ssm_A_log': out['v_ssm_A_log'], 'v_ssm_D': out['v_ssm_D'], 'v_ssm_norm': out['v_ssm_norm'], 'v_ssm_w_out': out['v_ssm_w_out'], 'v_ffn_w_up': out['v_ffn_w_up'], 'v_ffn_conv_w': out['v_ffn_conv_w'], 'v_ffn_conv_b': out['v_ffn_conv_b'], 'v_ffn_w_down': out['v_ffn_w_down']}


def _loss(weights, diff, rest, loss_target):
    with _jax.named_scope("forward"):
        args = {**rest, TWIN_DIFF_INPUT: diff, **{k: w.astype(_WEIGHT_DTYPES[k]) for k, w in weights.items()}}
        y = _forward(args)
    with _jax.named_scope("loss_head"):
        err = _jnp.square(y.astype(_jnp.float32) - loss_target)
        return 0.5 * _jnp.sum(_jnp.mean(err, axis=-1)) if err.ndim else 0.5 * err


def _adamw(w, g, m, v):
    m = ADAM_B1 * m + (1.0 - ADAM_B1) * g
    v = ADAM_B2 * v + (1.0 - ADAM_B2) * _jnp.square(g)
    m_hat = m / (1.0 - ADAM_B1 ** ADAM_STEP)
    v_hat = v / (1.0 - ADAM_B2 ** ADAM_STEP)
    delta = -ADAM_LR * (m_hat / (_jnp.sqrt(v_hat) + ADAM_EPS) + ADAM_WD * w)
    return delta, m, v


def reference(x, positions, norm_mix, norm_ffn, norm_final, mix_w_in, pool_w, pool_scale, attn_sinks, mix_w_out, ssm_w_in, ssm_conv_w, ssm_conv_b, ssm_dt_bias, ssm_A_log, ssm_D, ssm_norm, ssm_w_out, ffn_w_up, ffn_conv_w, ffn_conv_b, ffn_w_down, loss_target, m_norm_mix, m_norm_ffn, m_norm_final, m_mix_w_in, m_pool_w, m_pool_scale, m_attn_sinks, m_mix_w_out, m_ssm_w_in, m_ssm_conv_w, m_ssm_conv_b, m_ssm_dt_bias, m_ssm_A_log, m_ssm_D, m_ssm_norm, m_ssm_w_out, m_ffn_w_up, m_ffn_conv_w, m_ffn_conv_b, m_ffn_w_down, v_norm_mix, v_norm_ffn, v_norm_final, v_mix_w_in, v_pool_w, v_pool_scale, v_attn_sinks, v_mix_w_out, v_ssm_w_in, v_ssm_conv_w, v_ssm_conv_b, v_ssm_dt_bias, v_ssm_A_log, v_ssm_D, v_ssm_norm, v_ssm_w_out, v_ffn_w_up, v_ffn_conv_w, v_ffn_conv_b, v_ffn_w_down):
    given = dict(x=x, positions=positions, norm_mix=norm_mix, norm_ffn=norm_ffn, norm_final=norm_final, mix_w_in=mix_w_in, pool_w=pool_w, pool_scale=pool_scale, attn_sinks=attn_sinks, mix_w_out=mix_w_out, ssm_w_in=ssm_w_in, ssm_conv_w=ssm_conv_w, ssm_conv_b=ssm_conv_b, ssm_dt_bias=ssm_dt_bias, ssm_A_log=ssm_A_log, ssm_D=ssm_D, ssm_norm=ssm_norm, ssm_w_out=ssm_w_out, ffn_w_up=ffn_w_up, ffn_conv_w=ffn_conv_w, ffn_conv_b=ffn_conv_b, ffn_w_down=ffn_w_down, loss_target=loss_target, m_norm_mix=m_norm_mix, m_norm_ffn=m_norm_ffn, m_norm_final=m_norm_final, m_mix_w_in=m_mix_w_in, m_pool_w=m_pool_w, m_pool_scale=m_pool_scale, m_attn_sinks=m_attn_sinks, m_mix_w_out=m_mix_w_out, m_ssm_w_in=m_ssm_w_in, m_ssm_conv_w=m_ssm_conv_w, m_ssm_conv_b=m_ssm_conv_b, m_ssm_dt_bias=m_ssm_dt_bias, m_ssm_A_log=m_ssm_A_log, m_ssm_D=m_ssm_D, m_ssm_norm=m_ssm_norm, m_ssm_w_out=m_ssm_w_out, m_ffn_w_up=m_ffn_w_up, m_ffn_conv_w=m_ffn_conv_w, m_ffn_conv_b=m_ffn_conv_b, m_ffn_w_down=m_ffn_w_down, v_norm_mix=v_norm_mix, v_norm_ffn=v_norm_ffn, v_norm_final=v_norm_final, v_mix_w_in=v_mix_w_in, v_pool_w=v_pool_w, v_pool_scale=v_pool_scale, v_attn_sinks=v_attn_sinks, v_mix_w_out=v_mix_w_out, v_ssm_w_in=v_ssm_w_in, v_ssm_conv_w=v_ssm_conv_w, v_ssm_conv_b=v_ssm_conv_b, v_ssm_dt_bias=v_ssm_dt_bias, v_ssm_A_log=v_ssm_A_log, v_ssm_D=v_ssm_D, v_ssm_norm=v_ssm_norm, v_ssm_w_out=v_ssm_w_out, v_ffn_w_up=v_ffn_w_up, v_ffn_conv_w=v_ffn_conv_w, v_ffn_conv_b=v_ffn_conv_b, v_ffn_w_down=v_ffn_w_down)
    weights = {n: given[n] for n in TWIN_WEIGHTS}
    shared = {n: given[n] for n in SHARED_INPUTS}
    per_example = {n: given[n] for n in ['x', 'positions']}
    grad_fn = _jax.value_and_grad(_loss, argnums=(0, 1))

    def one_microbatch(ex, loss_target):
        ex = dict(ex)
        diff = ex.pop(TWIN_DIFF_INPUT)
        return grad_fn(weights, diff, {**shared, **ex}, loss_target)

    if N_MICROBATCH == 1:
        loss, (grad_w, grad_x) = one_microbatch(per_example, given["loss_target"])
    else:
        def body(carry, xs):
            loss_sum, grad_sum = carry
            l_k, (gw_k, gx_k) = one_microbatch(xs[0], xs[1])
            with _jax.named_scope("update"):
                return (loss_sum + l_k, _jax.tree.map(_jnp.add, grad_sum, gw_k)), gx_k

        init = (_jnp.zeros((), _jnp.float32), _jax.tree.map(_jnp.zeros_like, weights))
        (loss, grad_w), grad_x = _jax.lax.scan(body, init, (per_example, given["loss_target"]))
    with _jax.named_scope("update"):
        delta_w, new_m, new_v = {}, {}, {}
        for n in TWIN_WEIGHTS:
            delta_w[n], new_m[n], new_v[n] = _adamw(weights[n], grad_w[n], given["m_" + n], given["v_" + n])
    return (loss, grad_x, *[grad_w[n] for n in TWIN_WEIGHTS], *[delta_w[n] for n in TWIN_WEIGHTS],
            *[new_m[n] for n in TWIN_WEIGHTS], *[new_v[n] for n in TWIN_WEIGHTS])
```

```python
import functools
import math

import jax
import jax.numpy as jnp
from jax import lax
from jax.experimental import pallas as pl
from jax.experimental.pallas import tpu as pltpu

F32 = jnp.float32
BF16 = jnp.bfloat16

N_DEV = 8
LANES = 128
HEAD_DIM = 64
N_KV_HEADS = 2
GQ = 4
N_HEADS = N_KV_HEADS * GQ
BLOCK = 128
POOL_GROUPS = 4
ROPE_THETA = 10000.0
SSM_P = 64
SSM_G = 8
SSM_R = 4
SSM_N = 128
SSM_L = 128
NORM_EPS = 1e-6
SSM_NORM_EPS = 1e-5
ADAM_LR, ADAM_B1, ADAM_B2, ADAM_EPS, ADAM_WD, ADAM_STEP = 0.001, 0.9, 0.999, 1e-08, 0.01, 10
FLAT_W = 1024
VMEM_LIMIT = 56 * 2 ** 20

SHARDED = {
    "mix_w_in": 2, "mix_w_out": 1, "ssm_w_in": 2, "ssm_w_out": 1, "ffn_w_up": 2, "ffn_w_down": 1,
    "ssm_conv_w": 2, "ssm_conv_b": 1, "ssm_norm": 1, "ffn_conv_w": 2,
}
BIG = ("mix_w_in", "mix_w_out", "ssm_w_in", "ssm_w_out", "ffn_w_up", "ffn_w_down")
SMALL_SHARDED = ("ssm_conv_w", "ssm_conv_b", "ssm_norm", "ffn_conv_w")
REPLICATED = ("norm_mix", "norm_ffn", "norm_final", "pool_w", "pool_scale", "attn_sinks",
              "ssm_dt_bias", "ssm_A_log", "ssm_D", "ffn_conv_b")
WEIGHTS = ("norm_mix", "norm_ffn", "norm_final", "mix_w_in", "pool_w", "pool_scale", "attn_sinks", "mix_w_out",
           "ssm_w_in", "ssm_conv_w", "ssm_conv_b", "ssm_dt_bias", "ssm_A_log", "ssm_D", "ssm_norm", "ssm_w_out",
           "ffn_w_up", "ffn_conv_w", "ffn_conv_b", "ffn_w_down")


def _tile(n, cap):
    if n <= cap:
        return n
    best = None
    for d in range(LANES, cap + 1, LANES):
        if n % d == 0:
            best = d
    assert best is not None, (n, cap)
    return best


def _call(body, *, name, out_shape, grid=None, in_specs=None, out_specs=None, scratch=()):
    kw = {}
    if grid is not None:
        kw = dict(grid=grid, in_specs=in_specs, out_specs=out_specs)
    return pl.pallas_call(
        body, name=name, out_shape=out_shape, scratch_shapes=list(scratch),
        compiler_params=pltpu.CompilerParams(vmem_limit_bytes=VMEM_LIMIT), **kw)


def _sds(shape, dtype=F32):
    return jax.ShapeDtypeStruct(tuple(shape), dtype)


def _sigmoid(x):
    return 1.0 / (1.0 + jnp.exp(-x))


def _shift_dn(x, d, t):
    if d == 0:
        return x
    return jnp.where(t >= d, pltpu.roll(x, d, axis=0), 0.0)


def _shift_up(x, d, t):
    if d == 0:
        return x
    n = x.shape[0]
    return jnp.where(t < n - d, pltpu.roll(x, n - d, axis=0), 0.0)


def _mm(a, b, *, name, ta=False, tb=False, res=None, out_dtype=F32):
    M, K = (a.shape[1], a.shape[0]) if ta else a.shape
    N = b.shape[0] if tb else b.shape[1]
    assert (b.shape[1] if tb else b.shape[0]) == K, (a.shape, b.shape, ta, tb)
    tm, tn, tk = _tile(M, 1024), _tile(N, 1408), _tile(K, 1408)
    nk = K // tk
    dims = (((0 if ta else 1,), (1 if tb else 0,)), ((), ()))

    def body(*refs):
        if res is None:
            a_ref, b_ref, o_ref, acc = refs
        else:
            a_ref, b_ref, r_ref, o_ref, acc = refs
        k = pl.program_id(2)

        @pl.when(k == 0)
        def _():
            acc[...] = jnp.zeros_like(acc)

        acc[...] += lax.dot_general(a_ref[...].astype(BF16), b_ref[...].astype(BF16), dims,
                                    preferred_element_type=F32)

        @pl.when(k == nk - 1)
        def _():
            out = acc[...]
            if res is not None:
                out = out + r_ref[...]
            o_ref[...] = out.astype(out_dtype)

    a_spec = pl.BlockSpec((tk, tm), lambda i, j, k: (k, i)) if ta else pl.BlockSpec((tm, tk), lambda i, j, k: (i, k))
    b_spec = pl.BlockSpec((tn, tk), lambda i, j, k: (j, k)) if tb else pl.BlockSpec((tk, tn), lambda i, j, k: (k, j))
    o_spec = pl.BlockSpec((tm, tn), lambda i, j, k: (i, j))
    ins, specs = [a, b], [a_spec, b_spec]
    if res is not None:
        ins.append(res)
        specs.append(o_spec)
    return _call(body, name=name, out_shape=_sds((M, N), out_dtype), grid=(M // tm, N // tn, nk),
                 in_specs=specs, out_specs=o_spec, scratch=[pltpu.VMEM((tm, tn), F32)])(*ins)


def _rmsnorm(x, w, *, name, eps=NORM_EPS):
    S, D = x.shape
    tm = _tile(S, 512)

    def body(x_ref, w_ref, o_ref):
        xf = x_ref[...]
        r = lax.rsqrt(jnp.mean(xf * xf, axis=-1, keepdims=True) + eps)
        o_ref[...] = (xf * r * w_ref[...]).astype(BF16)

    return _call(body, name=name, out_shape=_sds((S, D), BF16), grid=(S // tm,),
                 in_specs=[pl.BlockSpec((tm, D), lambda i: (i, 0)), pl.BlockSpec((1, D), lambda i: (0, 0))],
                 out_specs=pl.BlockSpec((tm, D), lambda i: (i, 0)))(x, w)


def _norm_bwd_math(xf, w, dh, eps):
    r = lax.rsqrt(jnp.mean(xf * xf, axis=-1, keepdims=True) + eps)
    xhat = xf * r
    dxh = dh * w
    dx = r * (dxh - xhat * jnp.mean(dxh * xhat, axis=-1, keepdims=True))
    dw = jnp.sum(dh * xhat, axis=0, keepdims=True)
    return dx, dw


def _rmsnorm_bwd(x, w, dh, dres, *, name, eps=NORM_EPS):
    S, D = x.shape
    tm = _tile(S, 512)

    def body(x_ref, w_ref, dh_ref, dr_ref, dx_ref, dw_ref):
        dx, dw = _norm_bwd_math(x_ref[...], w_ref[...], dh_ref[...].astype(F32), eps)
        dx_ref[...] = dr_ref[...] + dx

        @pl.when(pl.program_id(0) == 0)
        def _():
            dw_ref[...] = jnp.zeros_like(dw_ref)

        dw_ref[...] += dw

    row = pl.BlockSpec((tm, D), lambda i: (i, 0))
    vec = pl.BlockSpec((1, D), lambda i: (0, 0))
    return _call(body, name=name, out_shape=(_sds((S, D)), _sds((1, D))), grid=(S // tm,),
                 in_specs=[row, vec, row, row], out_specs=(row, vec))(x, w, dh, dres)


def _final_loss(x, w, target, *, name):
    S, D = x.shape
    tm = _tile(S, 512)

    def body(x_ref, w_ref, t_ref, loss_ref, dx_ref, dw_ref):
        xf, wv = x_ref[...], w_ref[...]
        r = lax.rsqrt(jnp.mean(xf * xf, axis=-1, keepdims=True) + NORM_EPS)
        err = xf * r * wv - t_ref[...]
        part = 0.5 * jnp.sum(jnp.mean(err * err, axis=-1, keepdims=True), axis=0, keepdims=True)
        dx, dw = _norm_bwd_math(xf, wv, err * (1.0 / D), NORM_EPS)
        dx_ref[...] = dx

        @pl.when(pl.program_id(0) == 0)
        def _():
            dw_ref[...] = jnp.zeros_like(dw_ref)
            loss_ref[...] = jnp.zeros_like(loss_ref)

        dw_ref[...] += dw
        loss_ref[...] += jnp.broadcast_to(part, loss_ref.shape)

    row = pl.BlockSpec((tm, D), lambda i: (i, 0))
    vec = pl.BlockSpec((1, D), lambda i: (0, 0))
    return _call(body, name=name, out_shape=(_sds((1, LANES)), _sds((S, D)), _sds((1, D))), grid=(S // tm,),
                 in_specs=[row, vec, row], out_specs=(pl.BlockSpec((1, LANES), lambda i: (0, 0)), row, vec))(x, w, target)


def _rope_tables(pos, inv_freq):
    S = pos.shape[0]
    tm = _tile(S, 512)

    def body(p_ref, f_ref, c_ref, s_ref):
        ang = p_ref[...].astype(F32) * f_ref[...]
        c_ref[...] = jnp.cos(ang)
        s_ref[...] = jnp.sin(ang)

    blk = pl.BlockSpec((tm, LANES), lambda i: (i, 0))
    return _call(body, name="rope_tables", out_shape=(_sds((S, LANES)), _sds((S, LANES))), grid=(S // tm,),
                 in_specs=[pl.BlockSpec((tm, 1), lambda i: (i, 0)), pl.BlockSpec((1, LANES), lambda i: (0, 0))],
                 out_specs=(blk, blk))(pos, inv_freq)


def _rot_half(t):
    lane = lax.broadcasted_iota(jnp.int32, t.shape, 1)
    lo = (lane % HEAD_DIM) < (HEAD_DIM // 2)
    return jnp.where(lo, -pltpu.roll(t, LANES - HEAD_DIM // 2, axis=1), pltpu.roll(t, HEAD_DIM // 2, axis=1))


def _rope_fwd(proj, cos, sin, col0, ncols):
    S = proj.shape[0]
    tm = _tile(S, 512)
    b0 = col0 // LANES

    def body(t_ref, c_ref, s_ref, o_ref):
        t = t_ref[...]
        o_ref[...] = t * c_ref[...] + _rot_half(t) * s_ref[...]

    tab = pl.BlockSpec((tm, LANES), lambda i, j: (i, 0))
    return _call(body, name="rope_fwd", out_shape=_sds((S, ncols)), grid=(S // tm, ncols // LANES),
                 in_specs=[pl.BlockSpec((tm, LANES), lambda i, j: (i, b0 + j)), tab, tab],
                 out_specs=pl.BlockSpec((tm, LANES), lambda i, j: (i, j)))(proj, cos, sin)


def _qkv_bwd_assemble(dq, dka, dkb, dva, dvb, cos, sin):
    S, QD = dq.shape
    nb = S // BLOCK
    KD = dka.shape[1]

    def body(dq_ref, ka_ref, kb_ref, va_ref, vb_ref, c_ref, s_ref, o_ref):
        n = pl.program_id(0)
        c, s = c_ref[...], s_ref[...]
        keep = jnp.where(n < nb - 1, 1.0, 0.0)

        def unrot(dy):
            return dy * c - _rot_half(dy * s)

        for j in range(QD // LANES):
            o_ref[:, j * LANES:(j + 1) * LANES] = unrot(dq_ref[:, j * LANES:(j + 1) * LANES])
        o_ref[:, QD:QD + KD] = unrot(ka_ref[...] + keep * kb_ref[...])
        o_ref[:, QD + KD:QD + 2 * KD] = va_ref[...] + keep * vb_ref[...]

    cur = pl.BlockSpec((BLOCK, KD), lambda n: (n, 0))
    nxt = pl.BlockSpec((BLOCK, KD), lambda n: (jnp.minimum(n + 1, nb - 1), 0))
    tab = pl.BlockSpec((BLOCK, LANES), lambda n: (n, 0))
    return _call(body, name="qkv_bwd_assemble", out_shape=_sds((S, QD + 2 * KD)), grid=(nb,),
                 in_specs=[pl.BlockSpec((BLOCK, QD), lambda n: (n, 0)), cur, nxt, cur, nxt, tab, tab],
                 out_specs=pl.BlockSpec((BLOCK, QD + 2 * KD), lambda n: (n, 0)))(dq, dka, dkb, dva, dvb, cos, sin)


def _attn_probs(q, kcat, sink, mask):
    s = lax.dot_general(q.astype(BF16), kcat, (((1,), (1,)), ((), ())), preferred_element_type=F32) * (HEAD_DIM ** -0.5)
    s = jnp.where(mask, s, -jnp.inf)
    m = jnp.maximum(jnp.max(s, axis=1, keepdims=True), sink)
    p = jnp.exp(s - m)
    ps = jnp.exp(sink - m)
    inv = 1.0 / (jnp.sum(p, axis=1, keepdims=True) + ps)
    return p * inv, ps * inv


def _attn_mask(n):
    qi = lax.broadcasted_iota(jnp.int32, (BLOCK, 2 * BLOCK), 0)
    kj = lax.broadcasted_iota(jnp.int32, (BLOCK, 2 * BLOCK), 1)
    rel = qi + BLOCK - kj
    return (rel >= 0) & (rel < BLOCK) & ((n > 0) | (kj >= BLOCK))


def _attn_specs(nb):
    qs = pl.BlockSpec((GQ, BLOCK, HEAD_DIM), lambda k, n: (k, n, 0))
    cur = pl.BlockSpec((1, BLOCK, HEAD_DIM), lambda k, n: (k, n, 0))
    prev = pl.BlockSpec((1, BLOCK, HEAD_DIM), lambda k, n: (k, jnp.maximum(n - 1, 0), 0))
    smem = pl.BlockSpec(memory_space=pltpu.SMEM)
    return qs, cur, prev, smem


def _attn_fwd(q, k, v, sinks):
    H, S, _ = q.shape
    nb = S // BLOCK
    qs, cur, prev, smem = _attn_specs(nb)

    def body(sink_ref, q_ref, kc_ref, kp_ref, vc_ref, vp_ref, o_ref):
        kk, n = pl.program_id(0), pl.program_id(1)
        kcat = jnp.concatenate([kp_ref[0], kc_ref[0]], axis=0).astype(BF16)
        vcat = jnp.concatenate([vp_ref[0], vc_ref[0]], axis=0).astype(BF16)
        mask = _attn_mask(n)
        for g in range(GQ):
            pn, _ = _attn_probs(q_ref[g], kcat, sink_ref[0, kk * GQ + g], mask)
            o_ref[g] = jnp.dot(pn.astype(BF16), vcat, preferred_element_type=F32)

    return _call(body, name="attn_fwd", out_shape=_sds((H, S, HEAD_DIM)), grid=(N_KV_HEADS, nb),
                 in_specs=[smem, qs, cur, prev, cur, prev], out_specs=qs)(sinks, q, k, k, v, v)


def _attn_bwd(q, k, v, sinks, do):
    H, S, _ = q.shape
    nb = S // BLOCK
    qs, cur, prev, smem = _attn_specs(nb)
    TN = (((0,), (0,)), ((), ()))
    NT = (((1,), (1,)), ((), ()))

    def body(sink_ref, q_ref, kc_ref, kp_ref, vc_ref, vp_ref, do_ref, dq_ref, dka_ref, dkb_ref, dva_ref, dvb_ref, ds_ref):
        kk, n = pl.program_id(0), pl.program_id(1)
        kcat = jnp.concatenate([kp_ref[0], kc_ref[0]], axis=0).astype(BF16)
        vcat = jnp.concatenate([vp_ref[0], vc_ref[0]], axis=0).astype(BF16)
        mask = _attn_mask(n)
        dk = jnp.zeros((2 * BLOCK, HEAD_DIM), F32)
        dv = jnp.zeros((2 * BLOCK, HEAD_DIM), F32)
        row = lax.broadcasted_iota(jnp.int32, (8, LANES), 0)
        dsk = jnp.zeros((8, LANES), F32)
        for g in range(GQ):
            qg = q_ref[g]
            pn, psn = _attn_probs(qg, kcat, sink_ref[0, kk * GQ + g], mask)
            dog = do_ref[g].astype(BF16)
            dp = lax.dot_general(dog, vcat, NT, preferred_element_type=F32)
            delta = jnp.sum(pn * dp, axis=1, keepdims=True)
            ds = (pn * (dp - delta) * (HEAD_DIM ** -0.5)).astype(BF16)
            dq_ref[g] = jnp.dot(ds, kcat, preferred_element_type=F32)
            dk = dk + lax.dot_general(ds, qg.astype(BF16), TN, preferred_element_type=F32)
            dv = dv + lax.dot_general(pn.astype(BF16), dog, TN, preferred_element_type=F32)
            dsk = dsk + jnp.where(row == g, -jnp.sum(psn * delta), 0.0)
        dkb_ref[0] = dk[:BLOCK]
        dka_ref[0] = dk[BLOCK:]
        dvb_ref[0] = dv[:BLOCK]
        dva_ref[0] = dv[BLOCK:]

        @pl.when(n == 0)
        def _():
            ds_ref[...] = jnp.zeros_like(ds_ref)

        ds_ref[0] += dsk

    kv_shape = _sds((N_KV_HEADS, S, HEAD_DIM))
    return _call(body, name="attn_bwd",
                 out_shape=(_sds((H, S, HEAD_DIM)), kv_shape, kv_shape, kv_shape, kv_shape, _sds((N_KV_HEADS, 8, LANES))),
                 grid=(N_KV_HEADS, nb), in_specs=[smem, qs, cur, prev, cur, prev, qs],
                 out_specs=(qs, cur, cur, cur, cur, pl.BlockSpec((1, 8, LANES), lambda k, n: (k, 0, 0))))(sinks, q, k, k, v, v, do)


def _pool_sums(u, g, t, shift):
    s2 = u + shift(u, 1, t)
    s4 = s2 + shift(s2, 2, t)
    s8 = s4 + shift(s4, 4, t)
    s16 = s8 + shift(s8, 8, t)
    return jnp.where(g == 0, s2, jnp.where(g == 1, s4, jnp.where(g == 2, s8, s16)))


def _pool_specs(S):
    col = pl.BlockSpec((S, LANES), lambda g: (0, g))
    wsp = pl.BlockSpec((1, LANES, LANES), lambda g: (g, 0, 0))
    vec = pl.BlockSpec((1, LANES), lambda g: (0, g))
    return col, wsp, vec


def _pool_fwd(proj, pool_w, scale):
    S = proj.shape[0]
    col, wsp, vec = _pool_specs(S)

    def body(u_ref, w_ref, s_ref, o_ref):
        g = pl.program_id(0)
        u = u_ref[...]
        t = lax.broadcasted_iota(jnp.int32, u.shape, 0)
        cnt = jnp.minimum(t + 1, 2 << g).astype(F32)
        pm = _pool_sums(u, g, t, _shift_dn) / cnt - u
        o_ref[...] = jnp.dot(pm.astype(BF16), w_ref[0].astype(BF16), preferred_element_type=F32) * s_ref[...]

    return _call(body, name="pool_fwd", out_shape=_sds((S, POOL_GROUPS * LANES)), grid=(POOL_GROUPS,),
                 in_specs=[col, wsp, vec], out_specs=col)(proj, pool_w, scale)


def _pool_bwd(proj, pool_w, scale, dcat):
    S = proj.shape[0]
    col, wsp, vec = _pool_specs(S)

    def body(u_ref, w_ref, s_ref, d_ref, du_ref, dw_ref, dsc_ref):
        g = pl.program_id(0)
        u = u_ref[...]
        t = lax.broadcasted_iota(jnp.int32, u.shape, 0)
        cnt = jnp.minimum(t + 1, 2 << g).astype(F32)
        pm = (_pool_sums(u, g, t, _shift_dn) / cnt - u).astype(BF16)
        wv = w_ref[0].astype(BF16)
        d = d_ref[...]
        pw = jnp.dot(pm, wv, preferred_element_type=F32)
        dsc_ref[...] = jnp.sum(pw * d, axis=0, keepdims=True)
        dpw = (d * s_ref[...]).astype(BF16)
        dw_ref[0] = lax.dot_general(pm, dpw, (((0,), (0,)), ((), ())), preferred_element_type=F32)
        dpm = lax.dot_general(dpw, wv, (((1,), (1,)), ((), ())), preferred_element_type=F32)
        du_ref[...] = _pool_sums(dpm / cnt, g, t, _shift_up) - dpm

    return _call(body, name="pool_bwd",
                 out_shape=(_sds((S, POOL_GROUPS * LANES)), _sds((POOL_GROUPS, LANES, LANES)), _sds((1, POOL_GROUPS * LANES))),
                 grid=(POOL_GROUPS,), in_specs=[col, wsp, vec, col], out_specs=(col, wsp, vec))(proj, pool_w, scale, dcat)


def _conv(x, w_ref, b_ref, t):
    K = w_ref.shape[0]
    y = b_ref[...] + jnp.zeros_like(x)
    for k in range(K):
        y = y + w_ref[k:k + 1, :] * _shift_dn(x, K - 1 - k, t)
    return y


def _conv_bwd(x, dy, w_ref, t):
    K = w_ref.shape[0]
    dx = jnp.zeros_like(x)
    dws = []
    for k in range(K):
        dx = dx + w_ref[k:k + 1, :] * _shift_up(dy, K - 1 - k, t)
        dws.append(jnp.sum(dy * _shift_dn(x, K - 1 - k, t), axis=0, keepdims=True))
    return dx, dws, jnp.sum(dy, axis=0, keepdims=True)


def _silu_grad(y):
    sg = _sigmoid(y)
    return sg * (1.0 + y * (1.0 - sg))


def _ffn_mid_fwd(a, cw, cb):
    S, F2 = a.shape
    F = F2 // 2
    nf = F // LANES
    K = cw.shape[0]

    def body(au_ref, ag_ref, wu_ref, wg_ref, bu_ref, bg_ref, o_ref):
        t = lax.broadcasted_iota(jnp.int32, (S, LANES), 0)
        hu = _conv(au_ref[...], wu_ref, bu_ref, t)
        hg = _conv(ag_ref[...], wg_ref, bg_ref, t)
        o_ref[...] = (hg * _sigmoid(hg) * hu).astype(BF16)

    cu = pl.BlockSpec((S, LANES), lambda j: (0, j))
    cg = pl.BlockSpec((S, LANES), lambda j: (0, j + nf))
    return _call(body, name="ffn_mid_fwd", out_shape=_sds((S, F), BF16), grid=(nf,),
                 in_specs=[cu, cg, pl.BlockSpec((K, LANES), lambda j: (0, j)), pl.BlockSpec((K, LANES), lambda j: (0, j + nf)),
                           pl.BlockSpec((1, LANES), lambda j: (0, j)), pl.BlockSpec((1, LANES), lambda j: (0, j + nf))],
                 out_specs=cu)(a, a, cw, cw, cb, cb)


def _ffn_mid_bwd(a, cw, cb, dact):
    S, F2 = a.shape
    F = F2 // 2
    nf = F // LANES
    K = cw.shape[0]

    def body(au_ref, ag_ref, wu_ref, wg_ref, bu_ref, bg_ref, d_ref, dau_ref, dag_ref, dwu_ref, dwg_ref, dbu_ref, dbg_ref):
        t = lax.broadcasted_iota(jnp.int32, (S, LANES), 0)
        au, ag = au_ref[...], ag_ref[...]
        hu = _conv(au, wu_ref, bu_ref, t)
        hg = _conv(ag, wg_ref, bg_ref, t)
        d = d_ref[...].astype(F32)
        dhu = d * hg * _sigmoid(hg)
        dhg = d * hu * _silu_grad(hg)
        dau, dwu, dbu = _conv_bwd(au, dhu, wu_ref, t)
        dag, dwg, dbg = _conv_bwd(ag, dhg, wg_ref, t)
        dau_ref[...] = dau.astype(BF16)
        dag_ref[...] = dag.astype(BF16)
        for k in range(K):
            dwu_ref[k:k + 1, :] = dwu[k]
            dwg_ref[k:k + 1, :] = dwg[k]
        dbu_ref[...] = dbu
        dbg_ref[...] = dbg

    cu = pl.BlockSpec((S, LANES), lambda j: (0, j))
    cg = pl.BlockSpec((S, LANES), lambda j: (0, j + nf))
    wu = pl.BlockSpec((K, LANES), lambda j: (0, j))
    wg = pl.BlockSpec((K, LANES), lambda j: (0, j + nf))
    bu = pl.BlockSpec((1, LANES), lambda j: (0, j))
    bg = pl.BlockSpec((1, LANES), lambda j: (0, j + nf))
    outs = _call(body, name="ffn_mid_bwd",
                 out_shape=(_sds((S, F), BF16), _sds((S, F), BF16), _sds((K, F)), _sds((K, F)), _sds((1, F)), _sds((1, F))),
                 grid=(nf,), in_specs=[cu, cg, wu, wg, bu, bg, cu], out_specs=(cu, cu, wu, wu, bu, bu))(a, a, cw, cw, cb, cb, dact)
    dau, dag, dwu, dwg, dbu, dbg = outs
    return jnp.concatenate([dau, dag], axis=1), jnp.concatenate([dwu, dwg], axis=1), jnp.concatenate([dbu, dbg], axis=1)


def _conv_silu_fwd(proj, col0, cw, cb):
    S = proj.shape[0]
    K, C = cw.shape
    b0 = col0 // LANES

    def body(x_ref, w_ref, b_ref, o_ref):
        t = lax.broadcasted_iota(jnp.int32, (S, LANES), 0)
        y = _conv(x_ref[...], w_ref, b_ref, t)
        o_ref[...] = y * _sigmoid(y)

    return _call(body, name="conv_silu_fwd", out_shape=_sds((S, C)), grid=(C // LANES,),
                 in_specs=[pl.BlockSpec((S, LANES), lambda j: (0, b0 + j)), pl.BlockSpec((K, LANES), lambda j: (0, j)),
                           pl.BlockSpec((1, LANES), lambda j: (0, j))],
                 out_specs=pl.BlockSpec((S, LANES), lambda j: (0, j)))(proj, cw, cb)


def _conv_silu_bwd(proj, col0, cw, cb, dout):
    S = proj.shape[0]
    K, C = cw.shape
    b0 = col0 // LANES

    def body(x_ref, w_ref, b_ref, d_ref, dx_ref, dw_ref, db_ref):
        t = lax.broadcasted_iota(jnp.int32, (S, LANES), 0)
        x = x_ref[...]
        y = _conv(x, w_ref, b_ref, t)
        dy = d_ref[...] * _silu_grad(y)
        dx, dw, db = _conv_bwd(x, dy, w_ref, t)
        dx_ref[...] = dx.astype(BF16)
        for k in range(K):
            dw_ref[k:k + 1, :] = dw[k]
        db_ref[...] = db

    col = pl.BlockSpec((S, LANES), lambda j: (0, j))
    wsp = pl.BlockSpec((K, LANES), lambda j: (0, j))
    bsp = pl.BlockSpec((1, LANES), lambda j: (0, j))
    return _call(body, name="conv_silu_bwd", out_shape=(_sds((S, C), BF16), _sds((K, C)), _sds((1, C))), grid=(C // LANES,),
                 in_specs=[pl.BlockSpec((S, LANES), lambda j: (0, b0 + j)), wsp, bsp, col],
                 out_specs=(col, wsp, bsp))(proj, cw, cb, dout)


HI = lax.Precision.HIGHEST


def _ssd_prep_fwd(proj, col0, bias_g, alog_g):
    S = proj.shape[0]
    nc = S // SSM_L
    b0 = col0 // LANES

    def body(raw_ref, b_ref, al_ref, pre_ref, dt_ref, acs_ref, acst_ref):
        r_i = lax.broadcasted_iota(jnp.int32, (LANES, LANES), 0)
        c_i = lax.broadcasted_iota(jnp.int32, (LANES, LANES), 1)
        live = c_i < SSM_R
        tril = jnp.where(r_i >= c_i, 1.0, 0.0)
        raw = raw_ref[...]
        for g in range(SSM_G):
            sel = jnp.where((r_i == SSM_R * g + c_i) & live, 1.0, 0.0)
            pre = jnp.dot(raw, sel, preferred_element_type=F32, precision=HI) + b_ref[g]
            dt = jnp.where(live, jnp.logaddexp(pre, 0.0), 0.0)
            a = dt * (-jnp.exp(al_ref[g]))
            acs = jnp.dot(tril, a, preferred_element_type=F32, precision=HI)
            pre_ref[g] = pre
            dt_ref[g] = dt
            acs_ref[g] = acs
            acst_ref[g] = acs.T

    gsp = pl.BlockSpec((SSM_G, 1, LANES), lambda c: (0, 0, 0))
    blk = pl.BlockSpec((SSM_G, SSM_L, LANES), lambda c: (0, c, 0))
    big = _sds((SSM_G, S, LANES))
    return _call(body, name="ssd_prep_fwd", out_shape=(big, big, big, _sds((SSM_G, LANES, S))), grid=(nc,),
                 in_specs=[pl.BlockSpec((SSM_L, LANES), lambda c: (c, b0)), gsp, gsp],
                 out_specs=(blk, blk, blk, pl.BlockSpec((SSM_G, LANES, SSM_L), lambda c: (0, 0, c))))(proj, bias_g, alog_g)


def _ssd_prep_bwd(pre_g, dt_g, alog_g, ddt_g, dacs_g, dacst_g):
    S = pre_g.shape[1]
    nc = S // SSM_L

    def body(pre_ref, dt_ref, al_ref, ddt_ref, dacs_ref, dacst_ref, draw_ref, db_ref, dal_ref):
        c = pl.program_id(0)
        r_i = lax.broadcasted_iota(jnp.int32, (LANES, LANES), 0)
        c_i = lax.broadcasted_iota(jnp.int32, (LANES, LANES), 1)
        live = c_i < SSM_R
        triu = jnp.where(r_i <= c_i, 1.0, 0.0)

        @pl.when(c == 0)
        def _():
            db_ref[...] = jnp.zeros_like(db_ref)
            dal_ref[...] = jnp.zeros_like(dal_ref)

        draw = jnp.zeros((SSM_L, LANES), F32)
        for g in range(SSM_G):
            dacs = dacs_ref[g] + dacst_ref[g].T
            da = jnp.dot(triu, dacs, preferred_element_type=F32, precision=HI)
            A = -jnp.exp(al_ref[g])
            ddt = ddt_ref[g] + da * A
            dpre = jnp.where(live, ddt * _sigmoid(pre_ref[g]), 0.0)
            unsel = jnp.where((c_i == SSM_R * g + r_i) & (r_i < SSM_R), 1.0, 0.0)
            draw = draw + jnp.dot(dpre, unsel, preferred_element_type=F32, precision=HI)
            db_ref[g] += jnp.sum(dpre, axis=0, keepdims=True)
            dal_ref[g] += jnp.where(live[:1], jnp.sum(da * dt_ref[g], axis=0, keepdims=True) * A, 0.0)
        draw_ref[...] = draw

    gsp = pl.BlockSpec((SSM_G, 1, LANES), lambda c: (0, 0, 0))
    blk = pl.BlockSpec((SSM_G, SSM_L, LANES), lambda c: (0, c, 0))
    gout = _sds((SSM_G, 1, LANES))
    return _call(body, name="ssd_prep_bwd", out_shape=(_sds((S, LANES)), gout, gout), grid=(nc,),
                 in_specs=[blk, blk, gsp, blk, blk, pl.BlockSpec((SSM_G, LANES, SSM_L), lambda c: (0, 0, c))],
                 out_specs=(pl.BlockSpec((SSM_L, LANES), lambda c: (c, 0)), gsp, gsp))(pre_g, dt_g, alog_g, ddt_g, dacs_g, dacst_g)


def _ssd_chunk_terms(xs_h, dt_col, acs_col, acs_row, acs_last, Gm, tril):
    X = xs_h * dt_col
    Lm = jnp.exp(jnp.where(tril, acs_col - acs_row, -jnp.inf))
    M = Gm * Lm
    e_col = jnp.exp(acs_col)
    decay = jnp.exp(acs_last - acs_col)
    cd = jnp.exp(acs_last)
    return X, Lm, M, e_col, decay, cd


NT = (((1,), (1,)), ((), ()))
TN = (((0,), (0,)), ((), ()))


def _ssd_specs(rev, nc):
    def cc(c):
        return nc - 1 - c if rev else c
    xs = pl.BlockSpec((SSM_L, SSM_R * SSM_P), lambda g, c: (cc(c), g))
    bsp = pl.BlockSpec((SSM_L, SSM_N), lambda g, c: (cc(c), (SSM_G * SSM_R * SSM_P) // SSM_N + g))
    csp = pl.BlockSpec((SSM_L, SSM_N), lambda g, c: (cc(c), (SSM_G * SSM_R * SSM_P) // SSM_N + SSM_G + g))
    sc = pl.BlockSpec((1, SSM_L, LANES), lambda g, c: (g, cc(c), 0))
    sct = pl.BlockSpec((1, LANES, SSM_L), lambda g, c: (g, 0, cc(c)))
    gsp = pl.BlockSpec((1, 1, LANES), lambda g, c: (g, 0, 0))
    st = pl.BlockSpec((1, SSM_R, SSM_P, SSM_N), lambda g, c: (cc(c), g, 0, 0))
    return xs, bsp, csp, sc, sct, gsp, st


def _ssd_fwd(xbc, dt_g, acs_g, acst_g, d_g):
    S = xbc.shape[0]
    nc = S // SSM_L
    xs_s, b_s, c_s, sc, sct, gsp, st = _ssd_specs(False, nc)

    def body(xs_ref, b_ref, c_ref, dt_ref, acs_ref, acst_ref, d_ref, y_ref, st_ref, state):
        c = pl.program_id(1)

        @pl.when(c == 0)
        def _():
            state[...] = jnp.zeros_like(state)

        Bm, Cm = b_ref[...], c_ref[...]
        Bb, Cb = Bm.astype(BF16), Cm.astype(BF16)
        Gm = lax.dot_general(Cb, Bb, NT, preferred_element_type=F32)
        r_i = lax.broadcasted_iota(jnp.int32, (SSM_L, SSM_L), 0)
        c_i = lax.broadcasted_iota(jnp.int32, (SSM_L, SSM_L), 1)
        tril = r_i >= c_i
        for r in range(SSM_R):
            xs_h = xs_ref[:, r * SSM_P:(r + 1) * SSM_P]
            X, Lm, M, e_col, decay, cd = _ssd_chunk_terms(
                xs_h, dt_ref[0, :, r:r + 1], acs_ref[0, :, r:r + 1], acst_ref[0, r:r + 1, :],
                acs_ref[0, SSM_L - 1:SSM_L, r:r + 1], Gm, tril)
            Xb = X.astype(BF16)
            Sh = state[r]
            st_ref[0, r] = Sh
            yd = jnp.dot(M.astype(BF16), Xb, preferred_element_type=F32)
            yo = e_col * lax.dot_general(Cb, Sh.astype(BF16), NT, preferred_element_type=F32)
            y_ref[:, r * SSM_P:(r + 1) * SSM_P] = yd + yo + d_ref[0, :, r:r + 1] * xs_h
            state[r] = Sh * cd + lax.dot_general(Xb, (Bm * decay).astype(BF16), TN, preferred_element_type=F32)

    return _call(body, name="ssd_fwd",
                 out_shape=(_sds((S, SSM_G * SSM_R * SSM_P)), _sds((nc, SSM_G * SSM_R, SSM_P, SSM_N))),
                 grid=(SSM_G, nc), in_specs=[xs_s, b_s, c_s, sc, sc, sct, gsp],
                 out_specs=(xs_s, pl.BlockSpec((1, SSM_R, SSM_P, SSM_N), lambda g, c: (c, g, 0, 0))),
                 scratch=[pltpu.VMEM((SSM_R, SSM_P, SSM_N), F32)])(xbc, xbc, xbc, dt_g, acs_g, acst_g, d_g)


def _ssd_bwd(xbc, dt_g, acs_g, acst_g, d_g, states, dy):
    S = xbc.shape[0]
    nc = S // SSM_L
    xs_s, b_s, c_s, sc, sct, gsp, st = _ssd_specs(True, nc)
    bc_out = pl.BlockSpec((SSM_L, SSM_N), lambda g, c: (nc - 1 - c, g))

    def body(xs_ref, b_ref, c_ref, dt_ref, acs_ref, acst_ref, d_ref, st_ref, dy_ref,
             dxs_ref, db_ref, dc_ref, ddt_ref, dacs_ref, dacst_ref, dd_ref, dstate):
        c = pl.program_id(1)

        @pl.when(c == 0)
        def _():
            dstate[...] = jnp.zeros_like(dstate)
            dd_ref[...] = jnp.zeros_like(dd_ref)

        Bm, Cm = b_ref[...], c_ref[...]
        Bb, Cb = Bm.astype(BF16), Cm.astype(BF16)
        Gm = lax.dot_general(Cb, Bb, NT, preferred_element_type=F32)
        r_i = lax.broadcasted_iota(jnp.int32, (SSM_L, SSM_L), 0)
        c_i = lax.broadcasted_iota(jnp.int32, (SSM_L, SSM_L), 1)
        tril = r_i >= c_i
        lane = lax.broadcasted_iota(jnp.int32, (1, LANES), 1)
        subl = lax.broadcasted_iota(jnp.int32, (LANES, 1), 0)
        last_row = lax.broadcasted_iota(jnp.int32, (SSM_L, 1), 0) == SSM_L - 1
        dG = jnp.zeros((SSM_L, SSM_L), F32)
        dB = jnp.zeros((SSM_L, SSM_N), F32)
        dC = jnp.zeros((SSM_L, SSM_N), F32)
        ddt_blk = jnp.zeros((SSM_L, LANES), F32)
        dacs_blk = jnp.zeros((SSM_L, LANES), F32)
        dacst_blk = jnp.zeros((LANES, SSM_L), F32)
        dd_row = jnp.zeros((1, LANES), F32)
        for r in range(SSM_R):
            xs_h = xs_ref[:, r * SSM_P:(r + 1) * SSM_P]
            dt_col = dt_ref[0, :, r:r + 1]
            X, Lm, M, e_col, decay, cd = _ssd_chunk_terms(
                xs_h, dt_col, acs_ref[0, :, r:r + 1], acst_ref[0, r:r + 1, :],
                acs_ref[0, SSM_L - 1:SSM_L, r:r + 1], Gm, tril)
            Xb, Mb = X.astype(BF16), M.astype(BF16)
            Sh = st_ref[0, r]
            Shb = Sh.astype(BF16)
            dY = dy_ref[:, r * SSM_P:(r + 1) * SSM_P]
            dYb = dY.astype(BF16)
            dSn = dstate[r]
            dSnb = dSn.astype(BF16)
            dM = lax.dot_general(dYb, Xb, NT, preferred_element_type=F32)
            dX = lax.dot_general(Mb, dYb, TN, preferred_element_type=F32)
            dG = dG + dM * Lm
            dseg = dM * M
            dacs_col = jnp.sum(dseg, axis=1, keepdims=True)
            dacs_row = -jnp.sum(dseg, axis=0, keepdims=True)
            T = lax.dot_general(Cb, Shb, NT, preferred_element_type=F32)
            dT = (dY * e_col).astype(BF16)
            dC = dC + jnp.dot(dT, Shb, preferred_element_type=F32)
            dS_prev = lax.dot_general(dT, Cb, TN, preferred_element_type=F32)
            dacs_col = dacs_col + jnp.sum(dY * (e_col * T), axis=1, keepdims=True)
            Bd = (Bm * decay).astype(BF16)
            dS_prev = dS_prev + dSn * cd
            dcd = jnp.sum(dSn * Sh)
            dX = dX + lax.dot_general(Bd, dSnb, NT, preferred_element_type=F32)
            dBd = jnp.dot(Xb, dSnb, preferred_element_type=F32)
            dB = dB + dBd * decay
            dd = jnp.sum(dBd * Bm, axis=1, keepdims=True) * decay
            dacs_col = dacs_col - dd + jnp.where(last_row, dcd * cd + jnp.sum(dd), 0.0)
            dstate[r] = dS_prev
            dxs_ref[:, r * SSM_P:(r + 1) * SSM_P] = dX * dt_col + d_ref[0, :, r:r + 1] * dY
            ddt_blk = ddt_blk + jnp.where(lane == r, jnp.sum(dX * xs_h, axis=1, keepdims=True), 0.0)
            dacs_blk = dacs_blk + jnp.where(lane == r, dacs_col, 0.0)
            dacst_blk = dacst_blk + jnp.where(subl == r, dacs_row, 0.0)
            dd_row = dd_row + jnp.where(lane == r, jnp.sum(dY * xs_h), 0.0)
        dGb = dG.astype(BF16)
        dc_ref[...] = dC + jnp.dot(dGb, Bb, preferred_element_type=F32)
        db_ref[...] = dB + lax.dot_general(dGb, Cb, TN, preferred_element_type=F32)
        ddt_ref[0] = ddt_blk
        dacs_ref[0] = dacs_blk
        dacst_ref[0] = dacst_blk
        dd_ref[0] += dd_row

    big = _sds((SSM_G, S, LANES))
    return _call(body, name="ssd_bwd",
                 out_shape=(_sds((S, SSM_G * SSM_R * SSM_P)), _sds((S, SSM_G * SSM_N)), _sds((S, SSM_G * SSM_N)),
                            big, big, _sds((SSM_G, LANES, S)), _sds((SSM_G, 1, LANES))),
                 grid=(SSM_G, nc), in_specs=[xs_s, b_s, c_s, sc, sc, sct, gsp, st, xs_s],
                 out_specs=(xs_s, bc_out, bc_out, sc, sc, sct, gsp),
                 scratch=[pltpu.VMEM((SSM_R, SSM_P, SSM_N), F32)])(xbc, xbc, xbc, dt_g, acs_g, acst_g, d_g, states, dy)


def _gate_norm_fwd(y, proj, w):
    S, DI = y.shape
    tm = _tile(S, 256)

    def body(y_ref, z_ref, w_ref, o_ref):
        z = z_ref[...]
        gn = y_ref[...] * (z * _sigmoid(z))
        r = lax.rsqrt(jnp.mean(gn * gn, axis=-1, keepdims=True) + SSM_NORM_EPS)
        o_ref[...] = (gn * r * w_ref[...]).astype(BF16)

    row = pl.BlockSpec((tm, DI), lambda i: (i, 0))
    return _call(body, name="gate_norm_fwd", out_shape=_sds((S, DI), BF16), grid=(S // tm,),
                 in_specs=[row, row, pl.BlockSpec((1, DI), lambda i: (0, 0))], out_specs=row)(y, proj, w)


def _gate_norm_bwd(y, proj, w, dout):
    S, DI = y.shape
    tm = _tile(S, 256)

    def body(y_ref, z_ref, w_ref, d_ref, dy_ref, dz_ref, dw_ref):
        z, yv = z_ref[...], y_ref[...]
        sz = z * _sigmoid(z)
        dgn, dw = _norm_bwd_math(yv * sz, w_ref[...], d_ref[...].astype(F32), SSM_NORM_EPS)
        dy_ref[...] = dgn * sz
        dz_ref[...] = (dgn * yv * _silu_grad(z)).astype(BF16)

        @pl.when(pl.program_id(0) == 0)
        def _():
            dw_ref[...] = jnp.zeros_like(dw_ref)

        dw_ref[...] += dw

    row = pl.BlockSpec((tm, DI), lambda i: (i, 0))
    vec = pl.BlockSpec((1, DI), lambda i: (0, 0))
    return _call(body, name="gate_norm_bwd", out_shape=(_sds((S, DI)), _sds((S, DI), BF16), _sds((1, DI))), grid=(S // tm,),
                 in_specs=[row, row, vec, row], out_specs=(row, row, vec))(y, proj, w, dout)


def _heads_major(t, n_heads):
    S = t.shape[0]
    return t.reshape(S, n_heads, HEAD_DIM).transpose(1, 0, 2)


def _heads_minor(t):
    H, S, _ = t.shape
    return t.transpose(1, 0, 2).reshape(S, H * HEAD_DIM)


def _group_major(v):
    return jnp.pad(v.reshape(SSM_G, 1, SSM_R), ((0, 0), (0, 0), (0, LANES - SSM_R)))


def _ffn_fwd(x, nw, w_up, cw, cb, w_down, tag):
    h = _rmsnorm(x, nw, name=f"ffn{tag}_norm")
    a = _mm(h, w_up, name=f"ffn{tag}_up")
    act = _ffn_mid_fwd(a, cw, cb)
    out = _mm(act, w_down, res=x, name=f"ffn{tag}_down")
    return out, (x, h, a, act)


def _ffn_bwd(saved, nw, w_up, cw, cb, w_down, dx, tag):
    x, h, a, act = saved
    dact = _mm(dx, w_down, tb=True, out_dtype=BF16, name=f"ffn{tag}_down_dx")
    dw_down = _mm(act, dx, ta=True, name=f"ffn{tag}_down_dw")
    da, dcw, dcb = _ffn_mid_bwd(a, cw, cb, dact)
    dw_up = _mm(h, da, ta=True, name=f"ffn{tag}_up_dw")
    dh = _mm(da, w_up, tb=True, name=f"ffn{tag}_up_dx")
    dx_in, dnw = _rmsnorm_bwd(x, nw, dh, dx, name=f"ffn{tag}_norm_bwd")
    return dx_in, dnw, dw_up, dcw, dcb, dw_down


def _local_step(x, positions, target, W):
    S, D = x.shape
    PD = POOL_GROUPS * LANES
    QD = N_HEADS * HEAD_DIM
    KD = N_KV_HEADS * HEAD_DIM
    inv_freq = ROPE_THETA ** (-jnp.arange(0, HEAD_DIM, 2, dtype=F32) / HEAD_DIM)
    inv_freq = jnp.tile(inv_freq, LANES // (HEAD_DIM // 2)).reshape(1, LANES)
    cos, sin = _rope_tables(positions, inv_freq)

    w_in0, w_out0 = W["mix_w_in"][0], W["mix_w_out"][0]
    nm0 = W["norm_mix"][0:1]
    h0 = _rmsnorm(x, nm0, name="mix_norm")
    proj0 = _mm(h0, w_in0, name="mix_in")
    pooled = _pool_fwd(proj0, W["pool_w"][0], W["pool_scale"])
    qk = _rope_fwd(proj0, cos, sin, PD, QD + KD)
    q_hm = _heads_major(qk[:, :QD], N_HEADS)
    k_hm = _heads_major(qk[:, QD:], N_KV_HEADS)
    v_hm = _heads_major(proj0[:, PD + QD + KD:], N_KV_HEADS)
    o_hm = _attn_fwd(q_hm, k_hm, v_hm, W["attn_sinks"])
    cat0 = jnp.concatenate([pooled, _heads_minor(o_hm)], axis=1)
    x1 = _mm(cat0, w_out0, res=x, name="mix_out")
    x2, ffn0 = _ffn_fwd(x1, W["norm_ffn"][0:1], W["ffn_w_up"][0], W["ffn_conv_w"][0], W["ffn_conv_b"][0:1], W["ffn_w_down"][0], 0)

    DI = W["ssm_norm"].shape[1]
    CD = W["ssm_conv_w"].shape[2]
    NH = W["ssm_dt_bias"].shape[1]
    w_in1 = jnp.pad(W["ssm_w_in"][0], ((0, 0), (0, LANES - NH)))
    w_out1 = W["ssm_w_out"][0]
    nm1 = W["norm_mix"][1:2]
    h1 = _rmsnorm(x2, nm1, name="ssm_norm_in")
    proj1 = _mm(h1, w_in1, name="ssm_in")
    xbc = _conv_silu_fwd(proj1, DI, W["ssm_conv_w"][0], W["ssm_conv_b"])
    bias_g, alog_g, d_g = _group_major(W["ssm_dt_bias"]), _group_major(W["ssm_A_log"]), _group_major(W["ssm_D"])
    pre_g, dt_g, acs_g, acst_g = _ssd_prep_fwd(proj1, DI + CD, bias_g, alog_g)
    y, states = _ssd_fwd(xbc, dt_g, acs_g, acst_g, d_g)
    yn = _gate_norm_fwd(y, proj1, W["ssm_norm"])
    x3 = _mm(yn, w_out1, res=x2, name="ssm_out")
    x4, ffn1 = _ffn_fwd(x3, W["norm_ffn"][1:2], W["ffn_w_up"][1], W["ffn_conv_w"][1], W["ffn_conv_b"][1:2], W["ffn_w_down"][1], 1)

    loss, dx, d_norm_final = _final_loss(x4, W["norm_final"].reshape(1, D), target, name="final_loss")
    dx, dnf1, dwu1, dcw1, dcb1, dwd1 = _ffn_bwd(ffn1, W["norm_ffn"][1:2], W["ffn_w_up"][1], W["ffn_conv_w"][1],
                                                W["ffn_conv_b"][1:2], W["ffn_w_down"][1], dx, 1)
    dyn = _mm(dx, w_out1, tb=True, out_dtype=BF16, name="ssm_out_dx")
    d_w_out1 = _mm(yn, dx, ta=True, name="ssm_out_dw")
    dy, dz, d_ssm_norm = _gate_norm_bwd(y, proj1, W["ssm_norm"], dyn)
    dxs, dB, dC, ddt_g, dacs_g, dacst_g, dd_g = _ssd_bwd(xbc, dt_g, acs_g, acst_g, d_g, states, dy)
    draw, dbias_g, dalog_g = _ssd_prep_bwd(pre_g, dt_g, alog_g, ddt_g, dacs_g, dacst_g)
    dxbc, d_conv_w1, d_conv_b1 = _conv_silu_bwd(proj1, DI, W["ssm_conv_w"][0], W["ssm_conv_b"], jnp.concatenate([dxs, dB, dC], axis=1))
    dproj1 = jnp.concatenate([dz, dxbc, draw.astype(BF16)], axis=1)
    d_w_in1 = _mm(h1, dproj1, ta=True, name="ssm_in_dw")[:, :DI + CD + NH]
    dh1 = _mm(dproj1, w_in1, tb=True, name="ssm_in_dx")
    dx, dnm1 = _rmsnorm_bwd(x2, nm1, dh1, dx, name="ssm_norm_in_bwd")
    dx, dnf0, dwu0, dcw0, dcb0, dwd0 = _ffn_bwd(ffn0, W["norm_ffn"][0:1], W["ffn_w_up"][0], W["ffn_conv_w"][0],
                                                W["ffn_conv_b"][0:1], W["ffn_w_down"][0], dx, 0)
    dcat = _mm(dx, w_out0, tb=True, name="mix_out_dx")
    d_w_out0 = _mm(cat0, dx, ta=True, name="mix_out_dw")
    du, d_pool_w, d_pool_scale = _pool_bwd(proj0, W["pool_w"][0], W["pool_scale"], dcat)
    do_hm = _heads_major(dcat[:, PD:], N_HEADS)
    dq_hm, dka, dkb, dva, dvb, dsk = _attn_bwd(q_hm, k_hm, v_hm, W["attn_sinks"], do_hm)
    dqkv = _qkv_bwd_assemble(_heads_minor(dq_hm), _heads_minor(dka), _heads_minor(dkb), _heads_minor(dva), _heads_minor(dvb), cos, sin)
    dproj0 = jnp.concatenate([du, dqkv], axis=1)
    d_w_in0 = _mm(h0, dproj0, ta=True, name="mix_in_dw")
    dh0 = _mm(dproj0, w_in0, tb=True, name="mix_in_dx")
    grad_x, dnm0 = _rmsnorm_bwd(x, nm0, dh0, dx, name="mix_norm_bwd")

    ungroup = lambda t: t[:, 0, :SSM_R].reshape(1, SSM_G * SSM_R)
    grads = {
        "norm_mix": jnp.concatenate([dnm0, dnm1], axis=0),
        "norm_ffn": jnp.concatenate([dnf0, dnf1], axis=0),
        "norm_final": d_norm_final.reshape(D),
        "mix_w_in": d_w_in0[None],
        "pool_w": d_pool_w[None],
        "pool_scale": d_pool_scale,
        "attn_sinks": dsk[:, :GQ, 0].reshape(1, N_HEADS),
        "mix_w_out": d_w_out0[None],
        "ssm_w_in": d_w_in1[None],
        "ssm_conv_w": d_conv_w1[None],
        "ssm_conv_b": d_conv_b1,
        "ssm_dt_bias": ungroup(dbias_g),
        "ssm_A_log": ungroup(dalog_g),
        "ssm_D": ungroup(dd_g),
        "ssm_norm": d_ssm_norm,
        "ssm_w_out": d_w_out1[None],
        "ffn_w_up": jnp.stack([dwu0, dwu1]),
        "ffn_conv_w": jnp.stack([dcw0, dcw1]),
        "ffn_conv_b": jnp.concatenate([dcb0, dcb1], axis=0),
        "ffn_w_down": jnp.stack([dwd0, dwd1]),
    }
    return loss, grad_x, grads


def _peer(k):
    x, y, c = lax.axis_index("x"), lax.axis_index("y"), lax.axis_index("c")
    px = 1 - x if k & 4 else x
    py = 1 - y if k & 2 else y
    pc = 1 - c if k & 1 else c
    return (px, py, pc), 4 * px + 2 * py + pc


def _my_index():
    return 4 * lax.axis_index("x") + 2 * lax.axis_index("y") + lax.axis_index("c")


def _all_gather(v, *, name):
    def body(v_ref, o_ref, send_sems, recv_sems, local_sem):
        me = _my_index()
        mine = pltpu.make_async_copy(v_ref, o_ref.at[me], local_sem)
        mine.start()
        copies = []
        for k in range(1, N_DEV):
            dev, _ = _peer(k)
            cp = pltpu.make_async_remote_copy(src_ref=v_ref, dst_ref=o_ref.at[me], send_sem=send_sems.at[k - 1],
                                              recv_sem=recv_sems.at[k - 1], device_id=dev, device_id_type=pl.DeviceIdType.MESH)
            cp.start()
            copies.append(cp)
        for cp in copies:
            cp.wait()
        mine.wait()

    return pl.pallas_call(
        body, name=name, out_shape=_sds((N_DEV,) + v.shape, v.dtype),
        in_specs=[pl.BlockSpec(memory_space=pl.ANY)], out_specs=pl.BlockSpec(memory_space=pl.ANY),
        scratch_shapes=[pltpu.SemaphoreType.DMA((N_DEV - 1,)), pltpu.SemaphoreType.DMA((N_DEV - 1,)), pltpu.SemaphoreType.DMA(())],
    )(v)


def _all_to_all(v, *, name):
    def body(v_ref, o_ref, send_sems, recv_sems, local_sem):
        me = _my_index()
        mine = pltpu.make_async_copy(v_ref.at[me], o_ref.at[me], local_sem)
        mine.start()
        copies = []
        for k in range(1, N_DEV):
            dev, idx = _peer(k)
            cp = pltpu.make_async_remote_copy(src_ref=v_ref.at[idx], dst_ref=o_ref.at[me], send_sem=send_sems.at[k - 1],
                                              recv_sem=recv_sems.at[k - 1], device_id=dev, device_id_type=pl.DeviceIdType.MESH)
            cp.start()
            copies.append(cp)
        for cp in copies:
            cp.wait()
        mine.wait()

    return pl.pallas_call(
        body, name=name, out_shape=_sds(v.shape, v.dtype),
        in_specs=[pl.BlockSpec(memory_space=pl.ANY)], out_specs=pl.BlockSpec(memory_space=pl.ANY),
        scratch_shapes=[pltpu.SemaphoreType.DMA((N_DEV - 1,)), pltpu.SemaphoreType.DMA((N_DEV - 1,)), pltpu.SemaphoreType.DMA(())],
    )(v)


ADAM_ROWS = 256


def _adamw(g_slabs, w, m, v, *, name):
    J, R, Wd = g_slabs.shape
    tr = _tile(R, ADAM_ROWS) if R % ADAM_ROWS == 0 else R
    c1 = 1.0 / (1.0 - ADAM_B1 ** ADAM_STEP)
    c2 = 1.0 / (1.0 - ADAM_B2 ** ADAM_STEP)

    def body(g_ref, w_ref, m_ref, v_ref, go_ref, d_ref, mo_ref, vo_ref):
        g = g_ref[0].astype(F32)
        for j in range(1, J):
            g = g + g_ref[j].astype(F32)
        mn = ADAM_B1 * m_ref[...] + (1.0 - ADAM_B1) * g
        vn = ADAM_B2 * v_ref[...] + (1.0 - ADAM_B2) * (g * g)
        go_ref[...] = g
        mo_ref[...] = mn
        vo_ref[...] = vn
        d_ref[...] = -ADAM_LR * ((mn * c1) / (jnp.sqrt(vn * c2) + ADAM_EPS) + ADAM_WD * w_ref[...])

    row = pl.BlockSpec((tr, Wd), lambda i: (i, 0))
    out = _sds((R, Wd))
    return _call(body, name=name, out_shape=(out, out, out, out), grid=(R // tr,),
                 in_specs=[pl.BlockSpec((J, tr, Wd), lambda i: (0, i, 0)), row, row, row], out_specs=(row, row, row, row))(g_slabs, w, m, v)


def _sum_slabs(g_slabs, *, name):
    J, R, Wd = g_slabs.shape

    def body(g_ref, o_ref):
        g = g_ref[0]
        for j in range(1, J):
            g = g + g_ref[j]
        o_ref[...] = g

    return _call(body, name=name, out_shape=_sds((R, Wd)))(g_slabs)


def _pack(arrs, row_mult):
    flat = jnp.concatenate([a.reshape(-1) for a in arrs])
    n = flat.shape[0]
    chunk = FLAT_W * row_mult
    total = -(-n // chunk) * chunk
    return jnp.pad(flat, (0, total - n)).reshape(total // FLAT_W, FLAT_W)


def _unpack(flat2d, shapes, lead=()):
    flat = flat2d.reshape(lead + (-1,))
    out, off = [], 0
    for s in shapes:
        n = math.prod(s)
        out.append(flat[..., off:off + n].reshape(lead + tuple(s)))
        off += n
    return out


def _to_blocks(full, ax, lshape):
    shp = list(full.shape)
    shp[ax:ax + 1] = [N_DEV, lshape[ax]]
    return jnp.moveaxis(full.reshape(shp), ax, 0)


def _from_blocks(blocks, ax):
    t = jnp.moveaxis(blocks, 0, ax)
    shp = list(t.shape)
    shp[ax:ax + 2] = [shp[ax] * shp[ax + 1]]
    return t.reshape(shp)


PAYLOAD = jnp.bfloat16


def _f32_as_bf16(a):
    return lax.bitcast_convert_type(a, PAYLOAD).reshape(-1)


def _bf16_as_f32(a, shape):
    return lax.bitcast_convert_type(a.reshape(a.shape[:-1] + (-1, 2)), F32).reshape(a.shape[:-1] + tuple(shape))


def kernel(x, positions, norm_mix, norm_ffn, norm_final, mix_w_in, pool_w, pool_scale, attn_sinks, mix_w_out, ssm_w_in, ssm_conv_w, ssm_conv_b, ssm_dt_bias, ssm_A_log, ssm_D, ssm_norm, ssm_w_out, ffn_w_up, ffn_conv_w, ffn_conv_b, ffn_w_down, loss_target, m_norm_mix, m_norm_ffn, m_norm_final, m_mix_w_in, m_pool_w, m_pool_scale, m_attn_sinks, m_mix_w_out, m_ssm_w_in, m_ssm_conv_w, m_ssm_conv_b, m_ssm_dt_bias, m_ssm_A_log, m_ssm_D, m_ssm_norm, m_ssm_w_out, m_ffn_w_up, m_ffn_conv_w, m_ffn_conv_b, m_ffn_w_down, v_norm_mix, v_norm_ffn, v_norm_final, v_mix_w_in, v_pool_w, v_pool_scale, v_attn_sinks, v_mix_w_out, v_ssm_w_in, v_ssm_conv_w, v_ssm_conv_b, v_ssm_dt_bias, v_ssm_A_log, v_ssm_D, v_ssm_norm, v_ssm_w_out, v_ffn_w_up, v_ffn_conv_w, v_ffn_conv_b, v_ffn_w_down):
    args = dict(locals())
    wl = {n: args[n] for n in WEIGHTS}
    ml = {n: args["m_" + n] for n in WEIGHTS}
    vl = {n: args["v_" + n] for n in WEIGHTS}
    me = _my_index()

    send = _pack([wl[n].astype(PAYLOAD) for n in BIG] + [_f32_as_bf16(wl[n]) for n in SMALL_SHARDED], 16)
    got = _all_gather(send, name="gather_weights")
    pieces = _unpack(got, [wl[n].shape for n in BIG] + [(wl[n].size * 2,) for n in SMALL_SHARDED], lead=(N_DEV,))
    W = {n: wl[n] for n in REPLICATED}
    for n, p in zip(BIG, pieces[:len(BIG)]):
        W[n] = _from_blocks(p, SHARDED[n] + 1 - 1)
    for n, p in zip(SMALL_SHARDED, pieces[len(BIG):]):
        W[n] = _from_blocks(_bf16_as_f32(p, wl[n].shape), SHARDED[n])

    loss_lanes, grad_x, G = _local_step(x[0], positions.reshape(-1, 1), loss_target[0], W)
    loss = lax.psum(loss_lanes[0, 0], ("x", "y", "c"))

    gsend = jnp.concatenate([_to_blocks(G[n], SHARDED[n], wl[n].shape).reshape(N_DEV, -1) for n in BIG], axis=1).astype(PAYLOAD)
    nbig = gsend.shape[1]
    chunk = FLAT_W * ADAM_ROWS
    tot = -(-nbig // chunk) * chunk
    gsend = jnp.pad(gsend, ((0, 0), (0, tot - nbig))).reshape(N_DEV, tot // FLAT_W, FLAT_W)
    grecv = _all_to_all(gsend, name="exchange_grads")
    big_shapes = [wl[n].shape for n in BIG]
    outs = _adamw(grecv, _pack([wl[n] for n in BIG], ADAM_ROWS), _pack([ml[n] for n in BIG], ADAM_ROWS),
                  _pack([vl[n] for n in BIG], ADAM_ROWS), name="adamw_big")
    res = {}
    for kind, flat in zip(("grad", "delta", "new_m", "new_v"), outs):
        for n, a in zip(BIG, _unpack(flat, big_shapes)):
            res[kind, n] = a

    small = REPLICATED + SMALL_SHARDED
    sg = _pack([G[n] for n in small], 8)
    ssum = _sum_slabs(_all_gather(sg, name="gather_small_grads"), name="sum_small_grads")
    full_g = dict(zip(small, _unpack(ssum, [G[n].shape for n in small])))
    local_g = [full_g[n] for n in REPLICATED]
    for n in SMALL_SHARDED:
        blocks = _to_blocks(full_g[n], SHARDED[n], wl[n].shape)
        local_g.append(lax.dynamic_index_in_dim(blocks, me, 0, keepdims=False))
    small_shapes = [wl[n].shape for n in small]
    outs = _adamw(_pack(local_g, 8)[None], _pack([wl[n] for n in small], 8), _pack([ml[n] for n in small], 8),
                  _pack([vl[n] for n in small], 8), name="adamw_small")
    for kind, flat in zip(("grad", "delta", "new_m", "new_v"), outs):
        for n, a in zip(small, _unpack(flat, small_shapes)):
            res[kind, n] = a

    return (loss, grad_x[None], *[res[k, n] for k in ("grad", "delta", "new_m", "new_v") for n in WEIGHTS])
```

```python
import functools
import math

import jax
import jax.numpy as jnp
from jax import lax
from jax.experimental import pallas as pl
from jax.experimental.pallas import tpu as pltpu

F32 = jnp.float32
BF16 = jnp.bfloat16

N_DEV = 8
LANES = 128
HEAD_DIM = 64
N_KV_HEADS = 2
GQ = 4
N_HEADS = N_KV_HEADS * GQ
BLOCK = 128
POOL_GROUPS = 4
ROPE_THETA = 10000.0
SSM_P = 64
SSM_G = 8
SSM_R = 4
SSM_N = 128
SSM_L = 128
NORM_EPS = 1e-6
SSM_NORM_EPS = 1e-5
ADAM_LR, ADAM_B1, ADAM_B2, ADAM_EPS, ADAM_WD, ADAM_STEP = 0.001, 0.9, 0.999, 1e-08, 0.01, 10
VMEM_LIMIT = 56 * 2 ** 20
PAYLOAD = jnp.bfloat16

REPLICATED = ("norm_mix", "norm_ffn", "norm_final", "pool_w", "pool_scale", "attn_sinks",
              "ssm_dt_bias", "ssm_A_log", "ssm_D", "ffn_conv_b")
WEIGHTS = ("norm_mix", "norm_ffn", "norm_final", "mix_w_in", "pool_w", "pool_scale", "attn_sinks", "mix_w_out",
           "ssm_w_in", "ssm_conv_w", "ssm_conv_b", "ssm_dt_bias", "ssm_A_log", "ssm_D", "ssm_norm", "ssm_w_out",
           "ffn_w_up", "ffn_conv_w", "ffn_conv_b", "ffn_w_down")


def _tile(n, cap):
    if n <= cap:
        return n
    best = None
    for d in range(LANES, cap + 1, LANES):
        if n % d == 0:
            best = d
    assert best is not None, (n, cap)
    return best


def _call(body, *, name, out_shape, grid=None, in_specs=None, out_specs=None, scratch=(), aliases=None):
    kw = {}
    if grid is not None:
        kw = dict(grid=grid, in_specs=in_specs, out_specs=out_specs)
    if aliases:
        kw["input_output_aliases"] = aliases
    return pl.pallas_call(
        body, name=name, out_shape=out_shape, scratch_shapes=list(scratch),
        compiler_params=pltpu.CompilerParams(vmem_limit_bytes=VMEM_LIMIT), **kw)


ANY = pl.BlockSpec(memory_space=pl.ANY)


def _sds(shape, dtype=F32):
    return jax.ShapeDtypeStruct(tuple(shape), dtype)


def _sigmoid(x):
    return 1.0 / (1.0 + jnp.exp(-x))


def _shift_dn(x, d, t):
    if d == 0:
        return x
    return jnp.where(t >= d, pltpu.roll(x, d, axis=0), 0.0)


def _shift_up(x, d, t):
    if d == 0:
        return x
    n = x.shape[0]
    return jnp.where(t < n - d, pltpu.roll(x, n - d, axis=0), 0.0)


def _mm(a, b, *, name, ta=False, tb=False, res=None, out_dtype=F32, b_layer=None, out_layer=None):
    M, K = (a.shape[1], a.shape[0]) if ta else a.shape
    bs = b.shape if b_layer is None else b.shape[1:]
    N = bs[0] if tb else bs[1]
    assert (bs[1] if tb else bs[0]) == K, (a.shape, b.shape, ta, tb)
    tm, tn, tk = _tile(M, 1024), _tile(N, 1408), _tile(K, 1408)
    nk = K // tk
    dims = (((0 if ta else 1,), (1 if tb else 0,)), ((), ()))
    n_in = 2 + (res is not None) + (out_layer is not None and out_layer[2] is not None)

    def body(*refs):
        a_ref, b_ref = refs[:2]
        r_ref = refs[2] if res is not None else None
        o_ref, acc = refs[n_in], refs[n_in + 1]
        k = pl.program_id(2)

        @pl.when(k == 0)
        def _():
            acc[...] = jnp.zeros_like(acc)

        acc[...] += lax.dot_general(a_ref[...].astype(BF16), b_ref[...].astype(BF16), dims,
                                    preferred_element_type=F32)

        @pl.when(k == nk - 1)
        def _():
            out = acc[...]
            if res is not None:
                out = out + r_ref[...]
            o_ref[...] = out.astype(out_dtype)

    a_spec = pl.BlockSpec((tk, tm), lambda i, j, k: (k, i)) if ta else pl.BlockSpec((tm, tk), lambda i, j, k: (i, k))
    if b_layer is None:
        b_spec = pl.BlockSpec((tn, tk), lambda i, j, k: (j, k)) if tb else pl.BlockSpec((tk, tn), lambda i, j, k: (k, j))
    elif tb:
        b_spec = pl.BlockSpec((None, tn, tk), lambda i, j, k: (b_layer, j, k))
    else:
        b_spec = pl.BlockSpec((None, tk, tn), lambda i, j, k: (b_layer, k, j))
    ins, specs = [a, b], [a_spec, b_spec]
    if res is not None:
        ins.append(res)
        specs.append(pl.BlockSpec((tm, tn), lambda i, j, k: (i, j)))
    aliases = None
    if out_layer is None:
        o_spec = pl.BlockSpec((tm, tn), lambda i, j, k: (i, j))
        out_shape = _sds((M, N), out_dtype)
    else:
        L, l, prev = out_layer
        o_spec = pl.BlockSpec((None, tm, tn), lambda i, j, k: (l, i, j))
        out_shape = _sds((L, M, N), out_dtype)
        if prev is not None:
            aliases = {len(ins): 0}
            ins.append(prev)
            specs.append(ANY)
    return _call(body, name=name, out_shape=out_shape, grid=(M // tm, N // tn, nk), in_specs=specs, out_specs=o_spec,
                 scratch=[pltpu.VMEM((tm, tn), F32)], aliases=aliases)(*ins)


def _rmsnorm(x, w, *, name, eps=NORM_EPS):
    S, D = x.shape
    tm = _tile(S, 512)

    def body(x_ref, w_ref, o_ref):
        xf = x_ref[...]
        r = lax.rsqrt(jnp.mean(xf * xf, axis=-1, keepdims=True) + eps)
        o_ref[...] = (xf * r * w_ref[...]).astype(BF16)

    return _call(body, name=name, out_shape=_sds((S, D), BF16), grid=(S // tm,),
                 in_specs=[pl.BlockSpec((tm, D), lambda i: (i, 0)), pl.BlockSpec((1, D), lambda i: (0, 0))],
                 out_specs=pl.BlockSpec((tm, D), lambda i: (i, 0)))(x, w)


def _norm_bwd_math(xf, w, dh, eps):
    r = lax.rsqrt(jnp.mean(xf * xf, axis=-1, keepdims=True) + eps)
    xhat = xf * r
    dxh = dh * w
    dx = r * (dxh - xhat * jnp.mean(dxh * xhat, axis=-1, keepdims=True))
    dw = jnp.sum(dh * xhat, axis=0, keepdims=True)
    return dx, dw


def _rmsnorm_bwd(x, w, dh, dres, *, name, eps=NORM_EPS):
    S, D = x.shape
    tm = _tile(S, 512)

    def body(x_ref, w_ref, dh_ref, dr_ref, dx_ref, dw_ref):
        dx, dw = _norm_bwd_math(x_ref[...], w_ref[...], dh_ref[...].astype(F32), eps)
        dx_ref[...] = dr_ref[...] + dx

        @pl.when(pl.program_id(0) == 0)
        def _():
            dw_ref[...] = jnp.zeros_like(dw_ref)

        dw_ref[...] += dw

    row = pl.BlockSpec((tm, D), lambda i: (i, 0))
    vec = pl.BlockSpec((1, D), lambda i: (0, 0))
    return _call(body, name=name, out_shape=(_sds((S, D)), _sds((1, D))), grid=(S // tm,),
                 in_specs=[row, vec, row, row], out_specs=(row, vec))(x, w, dh, dres)


def _final_loss(x, w, target, *, name):
    S, D = x.shape
    tm = _tile(S, 512)

    def body(x_ref, w_ref, t_ref, loss_ref, dx_ref, dw_ref):
        xf, wv = x_ref[...], w_ref[...]
        r = lax.rsqrt(jnp.mean(xf * xf, axis=-1, keepdims=True) + NORM_EPS)
        err = xf * r * wv - t_ref[...]
        part = 0.5 * jnp.sum(jnp.mean(err * err, axis=-1, keepdims=True), axis=0, keepdims=True)
        dx, dw = _norm_bwd_math(xf, wv, err * (1.0 / D), NORM_EPS)
        dx_ref[...] = dx

        @pl.when(pl.program_id(0) == 0)
        def _():
            dw_ref[...] = jnp.zeros_like(dw_ref)
            loss_ref[...] = jnp.zeros_like(loss_ref)

        dw_ref[...] += dw
        loss_ref[...] += jnp.broadcast_to(part, loss_ref.shape)

    row = pl.BlockSpec((tm, D), lambda i: (i, 0))
    vec = pl.BlockSpec((1, D), lambda i: (0, 0))
    return _call(body, name=name, out_shape=(_sds((1, LANES)), _sds((S, D)), _sds((1, D))), grid=(S // tm,),
                 in_specs=[row, vec, row], out_specs=(pl.BlockSpec((1, LANES), lambda i: (0, 0)), row, vec))(x, w, target)


def _rope_tables(pos, inv_freq):
    S = pos.shape[0]
    tm = _tile(S, 512)

    def body(p_ref, f_ref, c_ref, s_ref):
        ang = p_ref[...].astype(F32) * f_ref[...]
        c_ref[...] = jnp.cos(ang)
        s_ref[...] = jnp.sin(ang)

    blk = pl.BlockSpec((tm, LANES), lambda i: (i, 0))
    return _call(body, name="rope_tables", out_shape=(_sds((S, LANES)), _sds((S, LANES))), grid=(S // tm,),
                 in_specs=[pl.BlockSpec((tm, 1), lambda i: (i, 0)), pl.BlockSpec((1, LANES), lambda i: (0, 0))],
                 out_specs=(blk, blk))(pos, inv_freq)


def _rot_half(t):
    lane = lax.broadcasted_iota(jnp.int32, t.shape, 1)
    lo = (lane % HEAD_DIM) < (HEAD_DIM // 2)
    return jnp.where(lo, -pltpu.roll(t, LANES - HEAD_DIM // 2, axis=1), pltpu.roll(t, HEAD_DIM // 2, axis=1))


def _rope(t, c, s):
    return t * c + _rot_half(t) * s


def _unrope(dy, c, s):
    return dy * c - _rot_half(dy * s)


PD = POOL_GROUPS * LANES
QD = N_HEADS * HEAD_DIM
KD = N_KV_HEADS * HEAD_DIM
assert PD % QD == 0 and (PD + QD) % (2 * KD) == 0 and KD == LANES
def _attn_probs(q, kcat, sink, mask):
    s = lax.dot_general(q.astype(BF16), kcat, (((1,), (1,)), ((), ())), preferred_element_type=F32) * (HEAD_DIM ** -0.5)
    s = jnp.where(mask, s, -jnp.inf)
    m = jnp.maximum(jnp.max(s, axis=1, keepdims=True), sink)
    p = jnp.exp(s - m)
    ps = jnp.exp(sink - m)
    inv = 1.0 / (jnp.sum(p, axis=1, keepdims=True) + ps)
    return p * inv, ps * inv


def _attn_mask(n):
    qi = lax.broadcasted_iota(jnp.int32, (BLOCK, 2 * BLOCK), 0)
    kj = lax.broadcasted_iota(jnp.int32, (BLOCK, 2 * BLOCK), 1)
    rel = qi + BLOCK - kj
    return (rel >= 0) & (rel < BLOCK) & ((n > 0) | (kj >= BLOCK))


def _attn_in_specs(nb):
    def cur(n):
        return jnp.minimum(n, nb - 1)

    def prev(n):
        return jnp.clip(n - 1, 0, nb - 1)

    kvb = (PD + QD) // (2 * KD)
    return [pl.BlockSpec(memory_space=pltpu.SMEM),
            pl.BlockSpec((BLOCK, QD), lambda n: (cur(n), PD // QD)),
            pl.BlockSpec((BLOCK, 2 * KD), lambda n: (cur(n), kvb)),
            pl.BlockSpec((BLOCK, 2 * KD), lambda n: (prev(n), kvb)),
            pl.BlockSpec((BLOCK, LANES), lambda n: (cur(n), 0)), pl.BlockSpec((BLOCK, LANES), lambda n: (cur(n), 0)),
            pl.BlockSpec((BLOCK, LANES), lambda n: (prev(n), 0)), pl.BlockSpec((BLOCK, LANES), lambda n: (prev(n), 0))]


def _attn_keys(kvc_ref, kvp_ref, cc, sc, cp, sp):
    kc = _rope(kvc_ref[:, :KD], cc, sc)
    kp = _rope(kvp_ref[:, :KD], cp, sp)
    vc, vp = kvc_ref[:, KD:], kvp_ref[:, KD:]
    kcat, vcat = [], []
    for kk in range(N_KV_HEADS):
        sl = slice(kk * HEAD_DIM, (kk + 1) * HEAD_DIM)
        kcat.append(jnp.concatenate([kp[:, sl], kc[:, sl]], axis=0).astype(BF16))
        vcat.append(jnp.concatenate([vp[:, sl], vc[:, sl]], axis=0).astype(BF16))
    return kcat, vcat


def _attn_fwd(proj, cos, sin, sinks, cat):
    S = proj.shape[0]
    nb = S // BLOCK

    def body(sink_ref, q_ref, kvc_ref, kvp_ref, cc_ref, sc_ref, cp_ref, sp_ref, cat_ref, o_ref):
        n = pl.program_id(0)
        cc, sc = cc_ref[...], sc_ref[...]
        kcat, vcat = _attn_keys(kvc_ref, kvp_ref, cc, sc, cp_ref[...], sp_ref[...])
        mask = _attn_mask(n)
        for j in range(QD // LANES):
            qr = _rope(q_ref[:, j * LANES:(j + 1) * LANES], cc, sc)
            for e in range(LANES // HEAD_DIM):
                h = j * (LANES // HEAD_DIM) + e
                pn, _ = _attn_probs(qr[:, e * HEAD_DIM:(e + 1) * HEAD_DIM], kcat[h // GQ], sink_ref[0, h], mask)
                o_ref[:, h * HEAD_DIM:(h + 1) * HEAD_DIM] = jnp.dot(pn.astype(BF16), vcat[h // GQ], preferred_element_type=F32)

    return _call(body, name="attn_fwd", out_shape=_sds(cat.shape), grid=(nb,),
                 in_specs=_attn_in_specs(nb) + [ANY], out_specs=pl.BlockSpec((BLOCK, QD), lambda n: (n, PD // QD)),
                 aliases={8: 0})(sinks, proj, proj, proj, cos, sin, cos, sin, cat)


def _attn_bwd(proj, cos, sin, sinks, dcat):
    S = proj.shape[0]
    nb = S // BLOCK
    scale = HEAD_DIM ** -0.5
    per = LANES // HEAD_DIM

    def body(sink_ref, q_ref, kvc_ref, kvp_ref, cc_ref, sc_ref, cp_ref, sp_ref, do_ref, o_ref, ds_ref, hold, carry, part, pair):
        n = pl.program_id(0)

        @pl.when(n == 0)
        def _():
            hold[...] = jnp.zeros_like(hold)
            carry[...] = jnp.zeros_like(carry)
            ds_ref[...] = jnp.zeros_like(ds_ref)

        live = jnp.where(n < nb, 1.0, 0.0)
        cc, sc, cp, sp = cc_ref[...], sc_ref[...], cp_ref[...], sp_ref[...]
        kcat, vcat = _attn_keys(kvc_ref, kvp_ref, cc, sc, cp, sp)
        mask = _attn_mask(n)
        o_ref[:, :PD] = jnp.zeros((BLOCK, PD), F32)
        o_ref[:, PD:PD + QD] = hold[...]
        dk = [jnp.zeros((2 * BLOCK, HEAD_DIM), F32) for _ in range(N_KV_HEADS)]
        dv = [jnp.zeros((2 * BLOCK, HEAD_DIM), F32) for _ in range(N_KV_HEADS)]
        row = lax.broadcasted_iota(jnp.int32, (8, LANES), 0)
        dsk = jnp.zeros((8, LANES), F32)
        for j in range(QD // LANES):
            qr = _rope(q_ref[:, j * LANES:(j + 1) * LANES], cc, sc)
            for e in range(per):
                h = j * per + e
                kk = h // GQ
                qh = qr[:, e * HEAD_DIM:(e + 1) * HEAD_DIM]
                pn, psn = _attn_probs(qh, kcat[kk], sink_ref[0, h], mask)
                doh = (do_ref[:, h * HEAD_DIM:(h + 1) * HEAD_DIM] * live).astype(BF16)
                dp = lax.dot_general(doh, vcat[kk], NT, preferred_element_type=F32)
                delta = jnp.sum(pn * dp, axis=1, keepdims=True)
                ds = (pn * (dp - delta) * scale).astype(BF16)
                pair[:, e * HEAD_DIM:(e + 1) * HEAD_DIM] = jnp.dot(ds, kcat[kk], preferred_element_type=F32)
                dk[kk] = dk[kk] + lax.dot_general(ds, qh.astype(BF16), TN, preferred_element_type=F32)
                dv[kk] = dv[kk] + lax.dot_general(pn.astype(BF16), doh, TN, preferred_element_type=F32)
                dsk = dsk + jnp.where(row == h, -jnp.sum(psn * delta), 0.0)
            hold[:, j * LANES:(j + 1) * LANES] = _unrope(pair[...], cc, sc)
        for kk in range(N_KV_HEADS):
            sl = slice(kk * HEAD_DIM, (kk + 1) * HEAD_DIM)
            sv = slice(KD + kk * HEAD_DIM, KD + (kk + 1) * HEAD_DIM)
            part[0, :, sl] = dk[kk][:BLOCK]
            part[0, :, sv] = dv[kk][:BLOCK]
            part[1, :, sl] = dk[kk][BLOCK:]
            part[1, :, sv] = dv[kk][BLOCK:]
        done = carry[...] + part[0]
        o_ref[:, PD + QD:PD + QD + KD] = _unrope(done[:, :KD], cp, sp)
        o_ref[:, PD + QD + KD:] = done[:, KD:]
        carry[...] = part[1]
        ds_ref[...] += dsk

    return _call(body, name="attn_bwd", out_shape=(_sds((S, PD + QD + 2 * KD)), _sds((8, LANES))), grid=(nb + 1,),
                 in_specs=_attn_in_specs(nb) + [pl.BlockSpec((BLOCK, QD), lambda n: (jnp.minimum(n, nb - 1), PD // QD))],
                 out_specs=(pl.BlockSpec((BLOCK, PD + QD + 2 * KD), lambda n: (jnp.maximum(n - 1, 0), 0)),
                            pl.BlockSpec((8, LANES), lambda n: (0, 0))),
                 scratch=[pltpu.VMEM((BLOCK, QD), F32), pltpu.VMEM((BLOCK, 2 * KD), F32),
                          pltpu.VMEM((2, BLOCK, 2 * KD), F32), pltpu.VMEM((BLOCK, LANES), F32)])(
                     sinks, proj, proj, proj, cos, sin, cos, sin, dcat)


def _pool_sums(u, g, t, shift):
    s2 = u + shift(u, 1, t)
    s4 = s2 + shift(s2, 2, t)
    s8 = s4 + shift(s4, 4, t)
    s16 = s8 + shift(s8, 8, t)
    return jnp.where(g == 0, s2, jnp.where(g == 1, s4, jnp.where(g == 2, s8, s16)))


def _pool_specs(S):
    col = pl.BlockSpec((S, LANES), lambda g: (0, g))
    wsp = pl.BlockSpec((1, LANES, LANES), lambda g: (g, 0, 0))
    vec = pl.BlockSpec((1, LANES), lambda g: (0, g))
    return col, wsp, vec


def _pool_fwd(proj, pool_w, scale):
    S = proj.shape[0]
    col, wsp, vec = _pool_specs(S)

    def body(u_ref, w_ref, s_ref, o_ref):
        g = pl.program_id(0)
        u = u_ref[...]
        t = lax.broadcasted_iota(jnp.int32, u.shape, 0)
        cnt = jnp.minimum(t + 1, 2 << g).astype(F32)
        pm = _pool_sums(u, g, t, _shift_dn) / cnt - u
        o_ref[...] = jnp.dot(pm.astype(BF16), w_ref[0].astype(BF16), preferred_element_type=F32) * s_ref[...]

    return _call(body, name="pool_fwd", out_shape=_sds((S, PD + QD)), grid=(POOL_GROUPS,),
                 in_specs=[col, wsp, vec], out_specs=col)(proj, pool_w, scale)


def _pool_bwd(proj, pool_w, scale, dcat, dproj):
    S = proj.shape[0]
    col, wsp, vec = _pool_specs(S)

    def body(u_ref, w_ref, s_ref, d_ref, dproj_ref, du_ref, dw_ref, dsc_ref):
        g = pl.program_id(0)
        u = u_ref[...]
        t = lax.broadcasted_iota(jnp.int32, u.shape, 0)
        cnt = jnp.minimum(t + 1, 2 << g).astype(F32)
        pm = (_pool_sums(u, g, t, _shift_dn) / cnt - u).astype(BF16)
        wv = w_ref[0].astype(BF16)
        d = d_ref[...]
        pw = jnp.dot(pm, wv, preferred_element_type=F32)
        dsc_ref[...] = jnp.sum(pw * d, axis=0, keepdims=True)
        dpw = (d * s_ref[...]).astype(BF16)
        dw_ref[0] = lax.dot_general(pm, dpw, (((0,), (0,)), ((), ())), preferred_element_type=F32)
        dpm = lax.dot_general(dpw, wv, (((1,), (1,)), ((), ())), preferred_element_type=F32)
        du_ref[...] = _pool_sums(dpm / cnt, g, t, _shift_up) - dpm

    return _call(body, name="pool_bwd",
                 out_shape=(_sds(dproj.shape), _sds((POOL_GROUPS, LANES, LANES)), _sds((1, POOL_GROUPS * LANES))),
                 grid=(POOL_GROUPS,), in_specs=[col, wsp, vec, col, ANY], out_specs=(col, wsp, vec),
                 aliases={4: 0})(proj, pool_w, scale, dcat, dproj)


def _conv(x, w_ref, b_ref, t):
    K = w_ref.shape[0]
    y = b_ref[...] + jnp.zeros_like(x)
    for k in range(K):
        y = y + w_ref[k:k + 1, :] * _shift_dn(x, K - 1 - k, t)
    return y


def _conv_bwd(x, dy, w_ref, t):
    K = w_ref.shape[0]
    dx = jnp.zeros_like(x)
    dws = []
    for k in range(K):
        dx = dx + w_ref[k:k + 1, :] * _shift_up(dy, K - 1 - k, t)
        dws.append(jnp.sum(dy * _shift_dn(x, K - 1 - k, t), axis=0, keepdims=True))
    return dx, dws, jnp.sum(dy, axis=0, keepdims=True)


def _silu_grad(y):
    sg = _sigmoid(y)
    return sg * (1.0 + y * (1.0 - sg))


def _ffn_mid_specs(S, K, layer):
    return [pl.BlockSpec((S, LANES), lambda j: (0, 2 * j)), pl.BlockSpec((S, LANES), lambda j: (0, 2 * j + 1)),
            pl.BlockSpec((None, K, LANES), lambda j: (layer, 0, 2 * j)), pl.BlockSpec((None, K, LANES), lambda j: (layer, 0, 2 * j + 1)),
            pl.BlockSpec((None, 1, LANES), lambda j: (layer, 0, 2 * j)), pl.BlockSpec((None, 1, LANES), lambda j: (layer, 0, 2 * j + 1))]


def _ffn_mid_fwd(a, cw, cb, layer):
    S, F2 = a.shape
    nf = F2 // (2 * LANES)
    K = cw.shape[1]

    def body(au_ref, ag_ref, wu_ref, wg_ref, bu_ref, bg_ref, o_ref):
        t = lax.broadcasted_iota(jnp.int32, (S, LANES), 0)
        hu = _conv(au_ref[...], wu_ref, bu_ref, t)
        hg = _conv(ag_ref[...], wg_ref, bg_ref, t)
        o_ref[...] = (hg * _sigmoid(hg) * hu).astype(BF16)

    return _call(body, name="ffn_mid_fwd", out_shape=_sds((S, F2 // 2), BF16), grid=(nf,),
                 in_specs=_ffn_mid_specs(S, K, layer), out_specs=pl.BlockSpec((S, LANES), lambda j: (0, j)))(
                     a, a, cw, cw, cb[:, None], cb[:, None])


def _ffn_mid_bwd(a, cw, cb, layer, dact):
    S, F2 = a.shape
    nf = F2 // (2 * LANES)
    K = cw.shape[1]

    def body(au_ref, ag_ref, wu_ref, wg_ref, bu_ref, bg_ref, d_ref, da_ref, dw_ref, db_ref):
        t = lax.broadcasted_iota(jnp.int32, (S, LANES), 0)
        au, ag = au_ref[...], ag_ref[...]
        hu = _conv(au, wu_ref, bu_ref, t)
        hg = _conv(ag, wg_ref, bg_ref, t)
        d = d_ref[...].astype(F32)
        dhu = d * hg * _sigmoid(hg)
        dhg = d * hu * _silu_grad(hg)
        dau, dwu, dbu = _conv_bwd(au, dhu, wu_ref, t)
        dag, dwg, dbg = _conv_bwd(ag, dhg, wg_ref, t)
        da_ref[:, :LANES] = dau.astype(BF16)
        da_ref[:, LANES:] = dag.astype(BF16)
        for k in range(K):
            dw_ref[k:k + 1, :LANES] = dwu[k]
            dw_ref[k:k + 1, LANES:] = dwg[k]
        db_ref[:, :LANES] = dbu
        db_ref[:, LANES:] = dbg

    return _call(body, name="ffn_mid_bwd", out_shape=(_sds((S, F2), BF16), _sds((K, F2)), _sds((1, F2))), grid=(nf,),
                 in_specs=_ffn_mid_specs(S, K, layer) + [pl.BlockSpec((S, LANES), lambda j: (0, j))],
                 out_specs=(pl.BlockSpec((S, 2 * LANES), lambda j: (0, j)), pl.BlockSpec((K, 2 * LANES), lambda j: (0, j)),
                            pl.BlockSpec((1, 2 * LANES), lambda j: (0, j))))(a, a, cw, cw, cb[:, None], cb[:, None], dact)


def _conv_silu_fwd(x, cw, cb):
    S = x.shape[0]
    K, C = cw.shape

    def body(x_ref, w_ref, b_ref, o_ref):
        t = lax.broadcasted_iota(jnp.int32, (S, LANES), 0)
        y = _conv(x_ref[...], w_ref, b_ref, t)
        o_ref[...] = y * _sigmoid(y)

    col = pl.BlockSpec((S, LANES), lambda j: (0, j))
    return _call(body, name="conv_silu_fwd", out_shape=_sds((S, C)), grid=(C // LANES,),
                 in_specs=[col, pl.BlockSpec((K, LANES), lambda j: (0, j)), pl.BlockSpec((1, LANES), lambda j: (0, j))],
                 out_specs=col)(x, cw, cb)


def _conv_silu_bwd(x, cw, cb, douts):
    S = x.shape[0]
    K, C = cw.shape
    starts, off = [], 0
    for d in douts:
        starts.append(off)
        off += d.shape[1] // LANES
    assert off == C // LANES

    def body(x_ref, w_ref, b_ref, *rest):
        d_refs, (dx_ref, dw_ref, db_ref) = rest[:len(douts)], rest[len(douts):]
        j = pl.program_id(0)
        t = lax.broadcasted_iota(jnp.int32, (S, LANES), 0)
        x = x_ref[...]
        y = _conv(x, w_ref, b_ref, t)
        d = d_refs[0][...]
        for i in range(1, len(douts)):
            d = jnp.where(j >= starts[i], d_refs[i][...], d)
        dy = d * _silu_grad(y)
        dx, dw, db = _conv_bwd(x, dy, w_ref, t)
        dx_ref[...] = dx.astype(BF16)
        for k in range(K):
            dw_ref[k:k + 1, :] = dw[k]
        db_ref[...] = db

    col = pl.BlockSpec((S, LANES), lambda j: (0, j))
    wsp = pl.BlockSpec((K, LANES), lambda j: (0, j))
    bsp = pl.BlockSpec((1, LANES), lambda j: (0, j))

    def dspec(i):
        nblk = douts[i].shape[1] // LANES
        return pl.BlockSpec((S, LANES), lambda j: (0, jnp.clip(j - starts[i], 0, nblk - 1)))

    return _call(body, name="conv_silu_bwd", out_shape=(_sds((S, C), BF16), _sds((K, C)), _sds((1, C))), grid=(C // LANES,),
                 in_specs=[col, wsp, bsp] + [dspec(i) for i in range(len(douts))],
                 out_specs=(col, wsp, bsp))(x, cw, cb, *douts)


HI = lax.Precision.HIGHEST


def _ssd_prep_fwd(proj, col0, bias_g, alog_g):
    S = proj.shape[0]
    nc = S // SSM_L
    b0 = col0 // LANES

    def body(raw_ref, b_ref, al_ref, pre_ref, dt_ref, acs_ref, acst_ref):
        r_i = lax.broadcasted_iota(jnp.int32, (LANES, LANES), 0)
        c_i = lax.broadcasted_iota(jnp.int32, (LANES, LANES), 1)
        live = c_i < SSM_R
        tril = jnp.where(r_i >= c_i, 1.0, 0.0)
        raw = raw_ref[...]
        for g in range(SSM_G):
            sel = jnp.where((r_i == SSM_R * g + c_i) & live, 1.0, 0.0)
            pre = jnp.dot(raw, sel, preferred_element_type=F32, precision=HI) + b_ref[g]
            dt = jnp.where(live, jnp.logaddexp(pre, 0.0), 0.0)
            a = dt * (-jnp.exp(al_ref[g]))
            acs = jnp.dot(tril, a, preferred_element_type=F32, precision=HI)
            pre_ref[g] = pre
            dt_ref[g] = dt
            acs_ref[g] = acs
            acst_ref[g] = acs.T

    gsp = pl.BlockSpec((SSM_G, 1, LANES), lambda c: (0, 0, 0))
    blk = pl.BlockSpec((SSM_G, SSM_L, LANES), lambda c: (0, c, 0))
    big = _sds((SSM_G, S, LANES))
    return _call(body, name="ssd_prep_fwd", out_shape=(big, big, big, _sds((SSM_G, LANES, S))), grid=(nc,),
                 in_specs=[pl.BlockSpec((SSM_L, LANES), lambda c: (c, b0)), gsp, gsp],
                 out_specs=(blk, blk, blk, pl.BlockSpec((SSM_G, LANES, SSM_L), lambda c: (0, 0, c))))(proj, bias_g, alog_g)


def _ssd_prep_bwd(pre_g, dt_g, alog_g, ddt_g, dacs_g, dacst_g):
    S = pre_g.shape[1]
    nc = S // SSM_L

    def body(pre_ref, dt_ref, al_ref, ddt_ref, dacs_ref, dacst_ref, draw_ref, db_ref, dal_ref):
        c = pl.program_id(0)
        r_i = lax.broadcasted_iota(jnp.int32, (LANES, LANES), 0)
        c_i = lax.broadcasted_iota(jnp.int32, (LANES, LANES), 1)
        live = c_i < SSM_R
        triu = jnp.where(r_i <= c_i, 1.0, 0.0)

        @pl.when(c == 0)
        def _():
            db_ref[...] = jnp.zeros_like(db_ref)
            dal_ref[...] = jnp.zeros_like(dal_ref)

        draw = jnp.zeros((SSM_L, LANES), F32)
        for g in range(SSM_G):
            dacs = dacs_ref[g] + dacst_ref[g].T
            da = jnp.dot(triu, dacs, preferred_element_type=F32, precision=HI)
            A = -jnp.exp(al_ref[g])
            ddt = ddt_ref[g] + da * A
            dpre = jnp.where(live, ddt * _sigmoid(pre_ref[g]), 0.0)
            unsel = jnp.where((c_i == SSM_R * g + r_i) & (r_i < SSM_R), 1.0, 0.0)
            draw = draw + jnp.dot(dpre, unsel, preferred_element_type=F32, precision=HI)
            db_ref[g] += jnp.sum(dpre, axis=0, keepdims=True)
            dal_ref[g] += jnp.where(live[:1], jnp.sum(da * dt_ref[g], axis=0, keepdims=True) * A, 0.0)
        draw_ref[...] = draw

    gsp = pl.BlockSpec((SSM_G, 1, LANES), lambda c: (0, 0, 0))
    blk = pl.BlockSpec((SSM_G, SSM_L, LANES), lambda c: (0, c, 0))
    gout = _sds((SSM_G, 1, LANES))
    return _call(body, name="ssd_prep_bwd", out_shape=(_sds((S, LANES)), gout, gout), grid=(nc,),
                 in_specs=[blk, blk, gsp, blk, blk, pl.BlockSpec((SSM_G, LANES, SSM_L), lambda c: (0, 0, c))],
                 out_specs=(pl.BlockSpec((SSM_L, LANES), lambda c: (c, 0)), gsp, gsp))(pre_g, dt_g, alog_g, ddt_g, dacs_g, dacst_g)


def _ssd_chunk_terms(xs_h, dt_col, acs_col, acs_row, acs_last, Gm, tril):
    X = xs_h * dt_col
    Lm = jnp.exp(jnp.where(tril, acs_col - acs_row, -jnp.inf))
    M = Gm * Lm
    e_col = jnp.exp(acs_col)
    decay = jnp.exp(acs_last - acs_col)
    cd = jnp.exp(acs_last)
    return X, Lm, M, e_col, decay, cd


NT = (((1,), (1,)), ((), ()))
TN = (((0,), (0,)), ((), ()))


def _ssd_specs(rev, nc):
    def cc(c):
        return nc - 1 - c if rev else c
    xs = pl.BlockSpec((SSM_L, SSM_R * SSM_P), lambda g, c: (cc(c), g))
    bsp = pl.BlockSpec((SSM_L, SSM_N), lambda g, c: (cc(c), (SSM_G * SSM_R * SSM_P) // SSM_N + g))
    csp = pl.BlockSpec((SSM_L, SSM_N), lambda g, c: (cc(c), (SSM_G * SSM_R * SSM_P) // SSM_N + SSM_G + g))
    sc = pl.BlockSpec((1, SSM_L, LANES), lambda g, c: (g, cc(c), 0))
    sct = pl.BlockSpec((1, LANES, SSM_L), lambda g, c: (g, 0, cc(c)))
    gsp = pl.BlockSpec((1, 1, LANES), lambda g, c: (g, 0, 0))
    st = pl.BlockSpec((1, SSM_R, SSM_P, SSM_N), lambda g, c: (cc(c), g, 0, 0))
    return xs, bsp, csp, sc, sct, gsp, st


def _ssd_fwd(xbc, dt_g, acs_g, acst_g, d_g):
    S = xbc.shape[0]
    nc = S // SSM_L
    xs_s, b_s, c_s, sc, sct, gsp, st = _ssd_specs(False, nc)

    def body(xs_ref, b_ref, c_ref, dt_ref, acs_ref, acst_ref, d_ref, y_ref, st_ref, state):
        c = pl.program_id(1)

        @pl.when(c == 0)
        def _():
            state[...] = jnp.zeros_like(state)

        Bm, Cm = b_ref[...], c_ref[...]
        Bb, Cb = Bm.astype(BF16), Cm.astype(BF16)
        Gm = lax.dot_general(Cb, Bb, NT, preferred_element_type=F32)
        r_i = lax.broadcasted_iota(jnp.int32, (SSM_L, SSM_L), 0)
        c_i = lax.broadcasted_iota(jnp.int32, (SSM_L, SSM_L), 1)
        tril = r_i >= c_i
        for r in range(SSM_R):
            xs_h = xs_ref[:, r * SSM_P:(r + 1) * SSM_P]
            X, Lm, M, e_col, decay, cd = _ssd_chunk_terms(
                xs_h, dt_ref[0, :, r:r + 1], acs_ref[0, :, r:r + 1], acst_ref[0, r:r + 1, :],
                acs_ref[0, SSM_L - 1:SSM_L, r:r + 1], Gm, tril)
            Xb = X.astype(BF16)
            Sh = state[r]
            st_ref[0, r] = Sh
            yd = jnp.dot(M.astype(BF16), Xb, preferred_element_type=F32)
            yo = e_col * lax.dot_general(Cb, Sh.astype(BF16), NT, preferred_element_type=F32)
            y_ref[:, r * SSM_P:(r + 1) * SSM_P] = yd + yo + d_ref[0, :, r:r + 1] * xs_h
            state[r] = Sh * cd + lax.dot_general(Xb, (Bm * decay).astype(BF16), TN, preferred_element_type=F32)

    return _call(body, name="ssd_fwd",
                 out_shape=(_sds((S, SSM_G * SSM_R * SSM_P)), _sds((nc, SSM_G * SSM_R, SSM_P, SSM_N))),
                 grid=(SSM_G, nc), in_specs=[xs_s, b_s, c_s, sc, sc, sct, gsp],
                 out_specs=(xs_s, pl.BlockSpec((1, SSM_R, SSM_P, SSM_N), lambda g, c: (c, g, 0, 0))),
                 scratch=[pltpu.VMEM((SSM_R, SSM_P, SSM_N), F32)])(xbc, xbc, xbc, dt_g, acs_g, acst_g, d_g)


def _ssd_bwd(xbc, dt_g, acs_g, acst_g, d_g, states, dy):
    S = xbc.shape[0]
    nc = S // SSM_L
    xs_s, b_s, c_s, sc, sct, gsp, st = _ssd_specs(True, nc)
    bc_out = pl.BlockSpec((SSM_L, SSM_N), lambda g, c: (nc - 1 - c, g))

    def body(xs_ref, b_ref, c_ref, dt_ref, acs_ref, acst_ref, d_ref, st_ref, dy_ref,
             dxs_ref, db_ref, dc_ref, ddt_ref, dacs_ref, dacst_ref, dd_ref, dstate):
        c = pl.program_id(1)

        @pl.when(c == 0)
        def _():
            dstate[...] = jnp.zeros_like(dstate)
            dd_ref[...] = jnp.zeros_like(dd_ref)

        Bm, Cm = b_ref[...], c_ref[...]
        Bb, Cb = Bm.astype(BF16), Cm.astype(BF16)
        Gm = lax.dot_general(Cb, Bb, NT, preferred_element_type=F32)
        r_i = lax.broadcasted_iota(jnp.int32, (SSM_L, SSM_L), 0)
        c_i = lax.broadcasted_iota(jnp.int32, (SSM_L, SSM_L), 1)
        tril = r_i >= c_i
        lane = lax.broadcasted_iota(jnp.int32, (1, LANES), 1)
        subl = lax.broadcasted_iota(jnp.int32, (LANES, 1), 0)
        last_row = lax.broadcasted_iota(jnp.int32, (SSM_L, 1), 0) == SSM_L - 1
        dG = jnp.zeros((SSM_L, SSM_L), F32)
        dB = jnp.zeros((SSM_L, SSM_N), F32)
        dC = jnp.zeros((SSM_L, SSM_N), F32)
        ddt_blk = jnp.zeros((SSM_L, LANES), F32)
        dacs_blk = jnp.zeros((SSM_L, LANES), F32)
        dacst_blk = jnp.zeros((LANES, SSM_L), F32)
        dd_row = jnp.zeros((1, LANES), F32)
        for r in range(SSM_R):
            xs_h = xs_ref[:, r * SSM_P:(r + 1) * SSM_P]
            dt_col = dt_ref[0, :, r:r + 1]
            X, Lm, M, e_col, decay, cd = _ssd_chunk_terms(
                xs_h, dt_col, acs_ref[0, :, r:r + 1], acst_ref[0, r:r + 1, :],
                acs_ref[0, SSM_L - 1:SSM_L, r:r + 1], Gm, tril)
            Xb, Mb = X.astype(BF16), M.astype(BF16)
            Sh = st_ref[0, r]
            Shb = Sh.astype(BF16)
            dY = dy_ref[:, r * SSM_P:(r + 1) * SSM_P]
            dYb = dY.astype(BF16)
            dSn = dstate[r]
            dSnb = dSn.astype(BF16)
            dM = lax.dot_general(dYb, Xb, NT, preferred_element_type=F32)
            dX = lax.dot_general(Mb, dYb, TN, preferred_element_type=F32)
            dG = dG + dM * Lm
            dseg = dM * M
            dacs_col = jnp.sum(dseg, axis=1, keepdims=True)
            dacs_row = -jnp.sum(dseg, axis=0, keepdims=True)
            T = lax.dot_general(Cb, Shb, NT, preferred_element_type=F32)
            dT = (dY * e_col).astype(BF16)
            dC = dC + jnp.dot(dT, Shb, preferred_element_type=F32)
            dS_prev = lax.dot_general(dT, Cb, TN, preferred_element_type=F32)
            dacs_col = dacs_col + jnp.sum(dY * (e_col * T), axis=1, keepdims=True)
            Bd = (Bm * decay).astype(BF16)
            dS_prev = dS_prev + dSn * cd
            dcd = jnp.sum(dSn * Sh)
            dX = dX + lax.dot_general(Bd, dSnb, NT, preferred_element_type=F32)
            dBd = jnp.dot(Xb, dSnb, preferred_element_type=F32)
            dB = dB + dBd * decay
            dd = jnp.sum(dBd * Bm, axis=1, keepdims=True) * decay
            dacs_col = dacs_col - dd + jnp.where(last_row, dcd * cd + jnp.sum(dd), 0.0)
            dstate[r] = dS_prev
            dxs_ref[:, r * SSM_P:(r + 1) * SSM_P] = dX * dt_col + d_ref[0, :, r:r + 1] * dY
            ddt_blk = ddt_blk + jnp.where(lane == r, jnp.sum(dX * xs_h, axis=1, keepdims=True), 0.0)
            dacs_blk = dacs_blk + jnp.where(lane == r, dacs_col, 0.0)
            dacst_blk = dacst_blk + jnp.where(subl == r, dacs_row, 0.0)
            dd_row = dd_row + jnp.where(lane == r, jnp.sum(dY * xs_h), 0.0)
        dGb = dG.astype(BF16)
        dc_ref[...] = dC + jnp.dot(dGb, Bb, preferred_element_type=F32)
        db_ref[...] = dB + lax.dot_general(dGb, Cb, TN, preferred_element_type=F32)
        ddt_ref[0] = ddt_blk
        dacs_ref[0] = dacs_blk
        dacst_ref[0] = dacst_blk
        dd_ref[0] += dd_row

    big = _sds((SSM_G, S, LANES))
    return _call(body, name="ssd_bwd",
                 out_shape=(_sds((S, SSM_G * SSM_R * SSM_P)), _sds((S, SSM_G * SSM_N)), _sds((S, SSM_G * SSM_N)),
                            big, big, _sds((SSM_G, LANES, S)), _sds((SSM_G, 1, LANES))),
                 grid=(SSM_G, nc), in_specs=[xs_s, b_s, c_s, sc, sc, sct, gsp, st, xs_s],
                 out_specs=(xs_s, bc_out, bc_out, sc, sc, sct, gsp),
                 scratch=[pltpu.VMEM((SSM_R, SSM_P, SSM_N), F32)])(xbc, xbc, xbc, dt_g, acs_g, acst_g, d_g, states, dy)


def _gate_norm_fwd(y, proj, w):
    S, DI = y.shape
    tm = _tile(S, 256)

    def body(y_ref, z_ref, w_ref, o_ref):
        z = z_ref[...]
        gn = y_ref[...] * (z * _sigmoid(z))
        r = lax.rsqrt(jnp.mean(gn * gn, axis=-1, keepdims=True) + SSM_NORM_EPS)
        o_ref[...] = (gn * r * w_ref[...]).astype(BF16)

    row = pl.BlockSpec((tm, DI), lambda i: (i, 0))
    return _call(body, name="gate_norm_fwd", out_shape=_sds((S, DI), BF16), grid=(S // tm,),
                 in_specs=[row, row, pl.BlockSpec((1, DI), lambda i: (0, 0))], out_specs=row)(y, proj, w)


def _gate_norm_bwd(y, proj, w, dout):
    S, DI = y.shape
    tm = _tile(S, 256)

    def body(y_ref, z_ref, w_ref, d_ref, dy_ref, dz_ref, dw_ref):
        z, yv = z_ref[...], y_ref[...]
        sz = z * _sigmoid(z)
        dgn, dw = _norm_bwd_math(yv * sz, w_ref[...], d_ref[...].astype(F32), SSM_NORM_EPS)
        dy_ref[...] = dgn * sz
        dz_ref[...] = (dgn * yv * _silu_grad(z)).astype(BF16)

        @pl.when(pl.program_id(0) == 0)
        def _():
            dw_ref[...] = jnp.zeros_like(dw_ref)

        dw_ref[...] += dw

    row = pl.BlockSpec((tm, DI), lambda i: (i, 0))
    vec = pl.BlockSpec((1, DI), lambda i: (0, 0))
    return _call(body, name="gate_norm_bwd", out_shape=(_sds((S, DI)), _sds((S, DI), BF16), _sds((1, DI))), grid=(S // tm,),
                 in_specs=[row, row, vec, row], out_specs=(row, row, vec))(y, proj, w, dout)


def _group_major(v):
    return jnp.pad(v.reshape(SSM_G, 1, SSM_R), ((0, 0), (0, 0), (0, LANES - SSM_R)))


def _ungroup(t):
    return t[:, :SSM_R].reshape(1, SSM_G * SSM_R)


def _ffn_fwd(x, P, l):
    h = _rmsnorm(x, P["norm_ffn"][l:l + 1], name=f"ffn{l}_norm")
    a = _mm(h, P["ffn_w_up"], b_layer=l, name=f"ffn{l}_up")
    act = _ffn_mid_fwd(a, P["ffn_conv_w"], P["ffn_conv_b"], l)
    out = _mm(act, P["ffn_w_down"], b_layer=l, res=x, name=f"ffn{l}_down")
    return out, (x, h, a, act)


def _ffn_bwd(saved, P, l, dx, prev_up, prev_down):
    x, h, a, act = saved
    dact = _mm(dx, P["ffn_w_down"], b_layer=l, tb=True, out_dtype=BF16, name=f"ffn{l}_down_dx")
    dw_down = _mm(act, dx, ta=True, out_dtype=PAYLOAD, out_layer=(2, l, prev_down), name=f"ffn{l}_down_dw")
    da, dcw, dcb = _ffn_mid_bwd(a, P["ffn_conv_w"], P["ffn_conv_b"], l, dact)
    dw_up = _mm(h, da, ta=True, out_dtype=PAYLOAD, out_layer=(2, l, prev_up), name=f"ffn{l}_up_dw")
    dh = _mm(da, P["ffn_w_up"], b_layer=l, tb=True, name=f"ffn{l}_up_dx")
    dx_in, dnw = _rmsnorm_bwd(x, P["norm_ffn"][l:l + 1], dh, dx, name=f"ffn{l}_norm_bwd")
    return dx_in, dnw, dw_up, dcw, dcb, dw_down


def _local_step(x, positions, target, P):
    S, D = x.shape
    inv_freq = ROPE_THETA ** (-jnp.arange(0, HEAD_DIM, 2, dtype=F32) / HEAD_DIM)
    inv_freq = jnp.tile(inv_freq, LANES // (HEAD_DIM // 2)).reshape(1, LANES)
    cos, sin = _rope_tables(positions, inv_freq)

    nm0 = P["norm_mix"][0:1]
    h0 = _rmsnorm(x, nm0, name="mix_norm")
    proj0 = _mm(h0, P["mix_w_in"], name="mix_in")
    cat0 = _attn_fwd(proj0, cos, sin, P["attn_sinks"], _pool_fwd(proj0, P["pool_w"][0], P["pool_scale"]))
    x1 = _mm(cat0, P["mix_w_out"], res=x, name="mix_out")
    x2, ffn0 = _ffn_fwd(x1, P, 0)

    nm1 = P["norm_mix"][1:2]
    h1 = _rmsnorm(x2, nm1, name="ssm_norm_in")
    z = _mm(h1, P["ssm_wz"], name="ssm_in_z")
    xbcp = _mm(h1, P["ssm_wxbc"], name="ssm_in_xbc")
    dtraw = _mm(h1, P["ssm_wdt"], name="ssm_in_dt")
    xbc = _conv_silu_fwd(xbcp, P["ssm_conv_w"], P["ssm_conv_b"])
    bias_g, alog_g, d_g = _group_major(P["ssm_dt_bias"]), _group_major(P["ssm_A_log"]), _group_major(P["ssm_D"])
    pre_g, dt_g, acs_g, acst_g = _ssd_prep_fwd(dtraw, 0, bias_g, alog_g)
    y, states = _ssd_fwd(xbc, dt_g, acs_g, acst_g, d_g)
    yn = _gate_norm_fwd(y, z, P["ssm_norm"])
    x3 = _mm(yn, P["ssm_w_out"], res=x2, name="ssm_out")
    x4, ffn1 = _ffn_fwd(x3, P, 1)

    loss, dx, d_norm_final = _final_loss(x4, P["norm_final"].reshape(1, D), target, name="final_loss")
    dx, dnf1, dwu, dcw1, dcb1, dwd = _ffn_bwd(ffn1, P, 1, dx, None, None)
    dyn = _mm(dx, P["ssm_w_out"], tb=True, out_dtype=BF16, name="ssm_out_dx")
    d_w_out1 = _mm(yn, dx, ta=True, out_dtype=PAYLOAD, name="ssm_out_dw")
    dy, dz, d_ssm_norm = _gate_norm_bwd(y, z, P["ssm_norm"], dyn)
    dxs, dB, dC, ddt_g, dacs_g, dacst_g, dd_g = _ssd_bwd(xbc, dt_g, acs_g, acst_g, d_g, states, dy)
    draw, dbias_g, dalog_g = _ssd_prep_bwd(pre_g, dt_g, alog_g, ddt_g, dacs_g, dacst_g)
    dxbc, d_conv_w1, d_conv_b1 = _conv_silu_bwd(xbcp, P["ssm_conv_w"], P["ssm_conv_b"], [dxs, dB, dC])
    d_wz = _mm(h1, dz, ta=True, out_dtype=PAYLOAD, name="ssm_in_z_dw")
    d_wxbc = _mm(h1, dxbc, ta=True, out_dtype=PAYLOAD, name="ssm_in_xbc_dw")
    d_wdt = _mm(h1, draw, ta=True, out_dtype=PAYLOAD, name="ssm_in_dt_dw")
    dh1 = _mm(dz, P["ssm_wz"], tb=True, name="ssm_in_z_dx")
    dh1 = _mm(dxbc, P["ssm_wxbc"], tb=True, res=dh1, name="ssm_in_xbc_dx")
    dh1 = _mm(draw, P["ssm_wdt"], tb=True, res=dh1, name="ssm_in_dt_dx")
    dx, dnm1 = _rmsnorm_bwd(x2, nm1, dh1, dx, name="ssm_norm_in_bwd")
    dx, dnf0, dwu, dcw0, dcb0, dwd = _ffn_bwd(ffn0, P, 0, dx, dwu, dwd)
    dcat = _mm(dx, P["mix_w_out"], tb=True, name="mix_out_dx")
    d_w_out0 = _mm(cat0, dx, ta=True, out_dtype=PAYLOAD, name="mix_out_dw")
    dproj0, dsk = _attn_bwd(proj0, cos, sin, P["attn_sinks"], dcat)
    dproj0, d_pool_w, d_pool_scale = _pool_bwd(proj0, P["pool_w"][0], P["pool_scale"], dcat, dproj0)
    d_w_in0 = _mm(h0, dproj0, ta=True, out_dtype=PAYLOAD, name="mix_in_dw")
    dh0 = _mm(dproj0, P["mix_w_in"], tb=True, name="mix_in_dx")
    grad_x, dnm0 = _rmsnorm_bwd(x, nm0, dh0, dx, name="mix_norm_bwd")

    grads = {
        "norm_mix": jnp.concatenate([dnm0, dnm1], axis=0),
        "norm_ffn": jnp.concatenate([dnf0, dnf1], axis=0),
        "norm_final": d_norm_final,
        "mix_w_in": d_w_in0,
        "pool_w": d_pool_w,
        "pool_scale": d_pool_scale,
        "attn_sinks_rows": dsk,
        "mix_w_out": d_w_out0,
        "ssm_wz": d_wz, "ssm_wxbc": d_wxbc, "ssm_wdt": d_wdt,
        "ssm_conv_w": d_conv_w1,
        "ssm_conv_b": d_conv_b1,
        "ssm_dt_bias_g": dbias_g, "ssm_A_log_g": dalog_g, "ssm_D_g": dd_g,
        "ssm_norm": d_ssm_norm,
        "ssm_w_out": d_w_out1,
        "ffn_w_up": dwu,
        "ffn_conv_w": jnp.stack([dcw0, dcw1]),
        "ffn_conv_b": jnp.concatenate([dcb0, dcb1], axis=0),
        "ffn_w_down": dwd,
    }
    return loss, grad_x, grads


def _peer(k):
    x, y, c = lax.axis_index("x"), lax.axis_index("y"), lax.axis_index("c")
    px = 1 - x if k & 4 else x
    py = 1 - y if k & 2 else y
    pc = 1 - c if k & 1 else c
    return (px, py, pc), 4 * px + 2 * py + pc


def _my_index():
    return 4 * lax.axis_index("x") + 2 * lax.axis_index("y") + lax.axis_index("c")


def _exchange(ops, *, gather, name):
    n = len(ops)

    def out_sds(a, mode):
        if mode == "slab":
            return _sds(((N_DEV,) + a.shape) if gather else a.shape, a.dtype)
        if mode == "rows":
            return _sds((N_DEV * a.shape[0],) + a.shape[1:] if gather else (N_DEV, a.shape[0] // N_DEV) + a.shape[1:], a.dtype)
        assert mode == "rows3", mode
        if gather:
            return _sds((a.shape[0], N_DEV * a.shape[1]) + a.shape[2:], a.dtype)
        return _sds((N_DEV, a.shape[0], a.shape[1] // N_DEV) + a.shape[2:], a.dtype)

    def part(ref, mode, shape, idx):
        if mode == "slab":
            return ref.at[idx]
        if mode == "rows":
            r = shape[0] // N_DEV
            return ref.at[pl.ds(idx * r, r)]
        r = shape[1] // N_DEV
        return ref.at[:, pl.ds(idx * r, r)]

    outs_sds = [out_sds(a, m) for a, m in ops]

    def body(*refs):
        ins, outs = refs[:n], refs[n:2 * n]
        send_sems, recv_sems, local_sems = refs[2 * n:]
        me = _my_index()

        def src(i, idx):
            return ins[i] if gather else part(ins[i], ops[i][1], ops[i][0].shape, idx)

        def dst(i):
            return part(outs[i], ops[i][1], outs_sds[i].shape, me) if gather else outs[i].at[me]

        copies = []
        for i in range(n):
            cp = pltpu.make_async_copy(src(i, me), dst(i), local_sems.at[i])
            cp.start()
            copies.append(cp)
        for k in range(1, N_DEV):
            dev, idx = _peer(k)
            for i in range(n):
                cp = pltpu.make_async_remote_copy(src_ref=src(i, idx), dst_ref=dst(i), send_sem=send_sems.at[i, k - 1],
                                                  recv_sem=recv_sems.at[i, k - 1], device_id=dev,
                                                  device_id_type=pl.DeviceIdType.MESH)
                cp.start()
                copies.append(cp)
        for cp in copies:
            cp.wait()

    return pl.pallas_call(
        body, name=name, out_shape=outs_sds, in_specs=[ANY] * n, out_specs=[ANY] * n,
        scratch_shapes=[pltpu.SemaphoreType.DMA((n, N_DEV - 1)), pltpu.SemaphoreType.DMA((n, N_DEV - 1)),
                        pltpu.SemaphoreType.DMA((n,))],
    )(*[a for a, _ in ops])


ADAM_ROWS = 256


def _row_tile(R, cap=ADAM_ROWS):
    best = R
    if R > cap:
        for d in range(16, cap + 1, 16):
            if R % d == 0:
                best = d
    return best


def _adamw(g_slabs, w, m, v, *, name):
    J, R, Wd = g_slabs.shape
    assert w.shape == (R, Wd), (g_slabs.shape, w.shape)
    tr = _row_tile(R)
    c1 = 1.0 / (1.0 - ADAM_B1 ** ADAM_STEP)
    c2 = 1.0 / (1.0 - ADAM_B2 ** ADAM_STEP)

    def body(g_ref, w_ref, m_ref, v_ref, go_ref, d_ref, mo_ref, vo_ref):
        g = g_ref[0].astype(F32)
        for j in range(1, J):
            g = g + g_ref[j].astype(F32)
        mn = ADAM_B1 * m_ref[...] + (1.0 - ADAM_B1) * g
        vn = ADAM_B2 * v_ref[...] + (1.0 - ADAM_B2) * (g * g)
        go_ref[...] = g
        mo_ref[...] = mn
        vo_ref[...] = vn
        d_ref[...] = -ADAM_LR * ((mn * c1) / (jnp.sqrt(vn * c2) + ADAM_EPS) + ADAM_WD * w_ref[...])

    row = pl.BlockSpec((tr, Wd), lambda i: (i, 0))
    out = _sds((R, Wd))
    return _call(body, name=name, out_shape=(out, out, out, out), grid=(R // tr,),
                 in_specs=[pl.BlockSpec((J, tr, Wd), lambda i: (0, i, 0)), row, row, row], out_specs=(row, row, row, row))(g_slabs, w, m, v)


def _sum_slabs(slabs, *, name):
    n = len(slabs)

    def body(*refs):
        for g_ref, o_ref in zip(refs[:n], refs[n:]):
            g = g_ref[0]
            for j in range(1, g_ref.shape[0]):
                g = g + g_ref[j]
            o_ref[...] = g

    return _call(body, name=name, out_shape=[_sds(s.shape[1:]) for s in slabs])(*slabs)


RELAYOUT_ROWS = 256


def _col_plan(n, segments):
    plan = []
    for j in range(N_DEV):
        lo, hi = j * n, (j + 1) * n
        for t0, t1, oi, o0 in segments:
            a, b = max(lo, t0), min(hi, t1)
            if a < b:
                plan.append((j, a - lo, oi, o0 + a - t0, b - a))
    return plan


def _interleave_segments(F):
    seg = []
    for b in range(F // LANES):
        seg.append((b * LANES, (b + 1) * LANES, 0, 2 * b * LANES))
        seg.append((F + b * LANES, F + (b + 1) * LANES, 0, (2 * b + 1) * LANES))
    return seg


def _interleave_perm(F):
    perm = []
    for b in range(F // LANES):
        perm += list(range(b * LANES, (b + 1) * LANES)) + list(range(F + b * LANES, F + (b + 1) * LANES))
    inv = [0] * (2 * F)
    for d, s in enumerate(perm):
        inv[s] = d
    return jnp.asarray(perm, jnp.int32), jnp.asarray(inv, jnp.int32)


def _cols_from_slabs(slabs, plan, widths, *, name):
    _, R, n = slabs.shape
    tr = _row_tile(R, RELAYOUT_ROWS)
    covered = [sum(e[4] for e in plan if e[2] == i) for i in range(len(widths))]

    def body(s_ref, *o_refs):
        for i, o_ref in enumerate(o_refs):
            if covered[i] < widths[i]:
                o_ref[...] = jnp.zeros_like(o_ref)
        for j, sc, oi, oc, w in plan:
            o_refs[oi][:, oc:oc + w] = s_ref[j, :, sc:sc + w]

    return _call(body, name=name, out_shape=[_sds((R, w), slabs.dtype) for w in widths], grid=(R // tr,),
                 in_specs=[pl.BlockSpec((N_DEV, tr, n), lambda i: (0, i, 0))],
                 out_specs=[pl.BlockSpec((tr, w), lambda i: (i, 0)) for w in widths])(slabs)


def _slabs_from_cols(mats, plan, n, *, name):
    R = mats[0].shape[0]
    tr = _row_tile(R, RELAYOUT_ROWS)

    def body(*refs):
        m_refs, o_ref = refs[:-1], refs[-1]
        for j, sc, oi, oc, w in plan:
            o_ref[j, :, sc:sc + w] = m_refs[oi][:, oc:oc + w]

    return _call(body, name=name, out_shape=_sds((N_DEV, R, n), mats[0].dtype), grid=(R // tr,),
                 in_specs=[pl.BlockSpec((tr, m.shape[1]), lambda i: (i, 0)) for m in mats],
                 out_specs=pl.BlockSpec((N_DEV, tr, n), lambda i: (0, i, 0)))(*mats)


def kernel(x, positions, norm_mix, norm_ffn, norm_final, mix_w_in, pool_w, pool_scale, attn_sinks, mix_w_out, ssm_w_in, ssm_conv_w, ssm_conv_b, ssm_dt_bias, ssm_A_log, ssm_D, ssm_norm, ssm_w_out, ffn_w_up, ffn_conv_w, ffn_conv_b, ffn_w_down, loss_target, m_norm_mix, m_norm_ffn, m_norm_final, m_mix_w_in, m_pool_w, m_pool_scale, m_attn_sinks, m_mix_w_out, m_ssm_w_in, m_ssm_conv_w, m_ssm_conv_b, m_ssm_dt_bias, m_ssm_A_log, m_ssm_D, m_ssm_norm, m_ssm_w_out, m_ffn_w_up, m_ffn_conv_w, m_ffn_conv_b, m_ffn_w_down, v_norm_mix, v_norm_ffn, v_norm_final, v_mix_w_in, v_pool_w, v_pool_scale, v_attn_sinks, v_mix_w_out, v_ssm_w_in, v_ssm_conv_w, v_ssm_conv_b, v_ssm_dt_bias, v_ssm_A_log, v_ssm_D, v_ssm_norm, v_ssm_w_out, v_ffn_w_up, v_ffn_conv_w, v_ffn_conv_b, v_ffn_w_down):
    args = dict(locals())
    wl = {n: args[n] for n in WEIGHTS}
    ml = {n: args["m_" + n] for n in WEIGHTS}
    vl = {n: args["v_" + n] for n in WEIGHTS}
    D = x.shape[2]
    F = ffn_w_down.shape[1] * N_DEV
    DI, CD, NH = ssm_norm.shape[1] * N_DEV, ssm_conv_b.shape[1] * N_DEV, ssm_dt_bias.shape[1]
    Kc, Kf = ssm_conv_w.shape[1], ffn_conv_w.shape[1]
    n_mix, n_ssm, n_up = mix_w_in.shape[2], ssm_w_in.shape[2], ffn_w_up.shape[2]
    plan_mix = _col_plan(n_mix, [(0, N_DEV * n_mix, 0, 0)])
    plan_ssm = _col_plan(n_ssm, [(0, DI, 0, 0), (DI, DI + CD, 1, 0), (DI + CD, DI + CD + NH, 2, 0)])
    plan_up = _col_plan(n_up, _interleave_segments(F))
    perm, inv = _interleave_perm(F)

    def two(a):
        return a.reshape(-1, a.shape[-1])

    sharded = ("mix_w_in", "ssm_w_in", "ffn_w_up", "mix_w_out", "ssm_w_out", "ffn_w_down",
               "ssm_conv_w", "ssm_conv_b", "ssm_norm", "ffn_conv_w")
    got = _exchange(
        [(two(mix_w_in).astype(PAYLOAD), "slab"), (two(ssm_w_in).astype(PAYLOAD), "slab"), (two(ffn_w_up).astype(PAYLOAD), "slab"),
         (mix_w_out[0].astype(PAYLOAD), "rows"), (ssm_w_out[0].astype(PAYLOAD), "rows"), (ffn_w_down.astype(PAYLOAD), "rows3"),
         (two(ssm_conv_w), "slab"), (ssm_conv_b, "slab"), (ssm_norm, "slab"), (two(ffn_conv_w), "slab")],
        gather=True, name="gather_weights")
    (w_in0,) = _cols_from_slabs(got[0], plan_mix, (N_DEV * n_mix,), name="unpack_mix_w_in")
    wz, wxbc, wdt = _cols_from_slabs(got[1], plan_ssm, (DI, CD, LANES), name="unpack_ssm_w_in")
    (w_up,) = _cols_from_slabs(got[2], plan_up, (2 * F,), name="unpack_ffn_w_up")
    P = {n: wl[n] for n in REPLICATED}
    P.update(mix_w_in=w_in0, ssm_wz=wz, ssm_wxbc=wxbc, ssm_wdt=wdt, ffn_w_up=w_up.reshape(2, D, 2 * F),
             mix_w_out=got[3], ssm_w_out=got[4], ffn_w_down=got[5],
             ssm_conv_w=got[6].transpose(1, 0, 2).reshape(Kc, CD), ssm_conv_b=got[7].reshape(1, CD),
             ssm_norm=got[8].reshape(1, DI),
             ffn_conv_w=jnp.take(got[9].transpose(1, 0, 2).reshape(2 * Kf, 2 * F), perm, axis=1).reshape(2, Kf, 2 * F),
             ffn_conv_b=jnp.take(ffn_conv_b, perm, axis=1))

    loss_lanes, grad_x, G = _local_step(x[0], positions.reshape(-1, 1), loss_target[0], P)
    loss = lax.psum(loss_lanes[0, 0], ("x", "y", "c"))

    res = {}

    def update(n, g_slabs):
        outs = _adamw(g_slabs, two(wl[n]), two(ml[n]), two(vl[n]), name="adamw_" + n)
        for kind, a in zip(("grad", "delta", "new_m", "new_v"), outs):
            res[kind, n] = a.reshape(wl[n].shape)

    recv = _exchange(
        [(_slabs_from_cols([G["mix_w_in"]], plan_mix, n_mix, name="pack_mix_w_in"), "slab"),
         (_slabs_from_cols([G["ssm_wz"], G["ssm_wxbc"], G["ssm_wdt"]], plan_ssm, n_ssm, name="pack_ssm_w_in"), "slab"),
         (_slabs_from_cols([G["ffn_w_up"].reshape(2 * D, 2 * F)], plan_up, n_up, name="pack_ffn_w_up"), "slab"),
         (G["mix_w_out"], "rows"), (G["ssm_w_out"], "rows"), (G["ffn_w_down"], "rows3"),
         (G["ssm_conv_w"].reshape(Kc, N_DEV, -1).transpose(1, 0, 2), "slab"),
         (G["ssm_conv_b"].reshape(N_DEV, 1, -1), "slab"), (G["ssm_norm"].reshape(N_DEV, 1, -1), "slab"),
         (jnp.take(G["ffn_conv_w"].reshape(2 * Kf, 2 * F), inv, axis=1).reshape(2 * Kf, N_DEV, n_up).transpose(1, 0, 2), "slab")],
        gather=False, name="exchange_grads")
    for n, r in zip(sharded, recv):
        update(n, r.reshape(N_DEV, -1, r.shape[-1]))

    rep = _exchange(
        [(a, "slab") for a in (G["norm_mix"], G["norm_ffn"], G["norm_final"], G["pool_w"].reshape(-1, LANES), G["pool_scale"],
                               jnp.take(G["ffn_conv_b"], inv, axis=1), G["attn_sinks_rows"],
                               G["ssm_dt_bias_g"].reshape(SSM_G, LANES), G["ssm_A_log_g"].reshape(SSM_G, LANES),
                               G["ssm_D_g"].reshape(SSM_G, LANES))],
        gather=True, name="gather_small_grads")
    for n, r in zip(("norm_mix", "norm_ffn", "norm_final", "pool_w", "pool_scale", "ffn_conv_b"), rep):
        update(n, r)
    sinks_rows, bias_g, alog_g, d_g = _sum_slabs(rep[6:], name="sum_head_grads")
    update("attn_sinks", sinks_rows[:, 0].reshape(1, 1, N_HEADS))
    update("ssm_dt_bias", _ungroup(bias_g)[None])
    update("ssm_A_log", _ungroup(alog_g)[None])
    update("ssm_D", _ungroup(d_g)[None])

    return (loss, grad_x[None], *[res[k, n] for k in ("grad", "delta", "new_m", "new_v") for n in WEIGHTS])
```

```python
import functools
import math

import jax
import jax.numpy as jnp
from jax import lax
from jax.experimental import pallas as pl
from jax.experimental.pallas import tpu as pltpu

F32 = jnp.float32
BF16 = jnp.bfloat16

N_DEV = 8
LANES = 128
HEAD_DIM = 64
N_KV_HEADS = 2
GQ = 4
N_HEADS = N_KV_HEADS * GQ
BLOCK = 128
POOL_GROUPS = 4
ROPE_THETA = 10000.0
SSM_P = 64
SSM_G = 8
SSM_R = 4
SSM_N = 128
SSM_L = 128
NORM_EPS = 1e-6
SSM_NORM_EPS = 1e-5
ADAM_LR, ADAM_B1, ADAM_B2, ADAM_EPS, ADAM_WD, ADAM_STEP = 0.001, 0.9, 0.999, 1e-08, 0.01, 10
VMEM_LIMIT = 56 * 2 ** 20
PAYLOAD = jnp.bfloat16

REPLICATED = ("norm_mix", "norm_ffn", "norm_final", "pool_w", "pool_scale", "attn_sinks",
              "ssm_dt_bias", "ssm_A_log", "ssm_D", "ffn_conv_b")
WEIGHTS = ("norm_mix", "norm_ffn", "norm_final", "mix_w_in", "pool_w", "pool_scale", "attn_sinks", "mix_w_out",
           "ssm_w_in", "ssm_conv_w", "ssm_conv_b", "ssm_dt_bias", "ssm_A_log", "ssm_D", "ssm_norm", "ssm_w_out",
           "ffn_w_up", "ffn_conv_w", "ffn_conv_b", "ffn_w_down")


def _tile(n, cap):
    if n <= cap:
        return n
    best = None
    for d in range(LANES, cap + 1, LANES):
        if n % d == 0:
            best = d
    assert best is not None, (n, cap)
    return best


def _call(body, *, name, out_shape, grid=None, in_specs=None, out_specs=None, scratch=(), aliases=None):
    kw = {}
    if grid is not None:
        kw = dict(grid=grid, in_specs=in_specs, out_specs=out_specs)
    if aliases:
        kw["input_output_aliases"] = aliases
    return pl.pallas_call(
        body, name=name, out_shape=out_shape, scratch_shapes=list(scratch),
        compiler_params=pltpu.CompilerParams(vmem_limit_bytes=VMEM_LIMIT), **kw)


ANY = pl.BlockSpec(memory_space=pl.ANY)


def _sds(shape, dtype=F32):
    return jax.ShapeDtypeStruct(tuple(shape), dtype)


def _sigmoid(x):
    return 1.0 / (1.0 + jnp.exp(-x))


def _shift_dn(x, d, t):
    if d == 0:
        return x
    return jnp.where(t >= d, pltpu.roll(x, d, axis=0), 0.0)


def _shift_up(x, d, t):
    if d == 0:
        return x
    n = x.shape[0]
    return jnp.where(t < n - d, pltpu.roll(x, n - d, axis=0), 0.0)


def _mm(a, b, *, name, ta=False, tb=False, res=None, out_dtype=F32):
    M, K = (a.shape[1], a.shape[0]) if ta else a.shape
    N = b.shape[0] if tb else b.shape[1]
    assert (b.shape[1] if tb else b.shape[0]) == K, (a.shape, b.shape, ta, tb)
    tm, tn, tk = _tile(M, 1024), _tile(N, 1408), _tile(K, 1408)
    nk = K // tk
    dims = (((0 if ta else 1,), (1 if tb else 0,)), ((), ()))

    def body(*refs):
        a_ref, b_ref = refs[:2]
        r_ref = refs[2] if res is not None else None
        o_ref, acc = refs[-2:]
        k = pl.program_id(2)

        @pl.when(k == 0)
        def _():
            acc[...] = jnp.zeros_like(acc)

        acc[...] += lax.dot_general(a_ref[...].astype(BF16), b_ref[...].astype(BF16), dims,
                                    preferred_element_type=F32)

        @pl.when(k == nk - 1)
        def _():
            out = acc[...]
            if res is not None:
                out = out + r_ref[...]
            o_ref[...] = out.astype(out_dtype)

    a_spec = pl.BlockSpec((tk, tm), lambda i, j, k: (k, i)) if ta else pl.BlockSpec((tm, tk), lambda i, j, k: (i, k))
    b_spec = pl.BlockSpec((tn, tk), lambda i, j, k: (j, k)) if tb else pl.BlockSpec((tk, tn), lambda i, j, k: (k, j))
    o_spec = pl.BlockSpec((tm, tn), lambda i, j, k: (i, j))
    ins, specs = [a, b], [a_spec, b_spec]
    if res is not None:
        ins.append(res)
        specs.append(o_spec)
    return _call(body, name=name, out_shape=_sds((M, N), out_dtype), grid=(M // tm, N // tn, nk), in_specs=specs,
                 out_specs=o_spec, scratch=[pltpu.VMEM((tm, tn), F32)])(*ins)


def _rmsnorm(x, w, *, name, eps=NORM_EPS, after=None):
    S, D = x.shape
    tm = _tile(S, 512)
    tie = [] if after is None else [after]

    def body(x_ref, w_ref, *rest):
        o_ref = rest[-1]
        xf = x_ref[...]
        r = lax.rsqrt(jnp.mean(xf * xf, axis=-1, keepdims=True) + eps)
        o_ref[...] = (xf * r * w_ref[...]).astype(BF16)

    return _call(body, name=name, out_shape=_sds((S, D), BF16), grid=(S // tm,),
                 in_specs=[pl.BlockSpec((tm, D), lambda i: (i, 0)), pl.BlockSpec((1, D), lambda i: (0, 0))] + [ANY] * len(tie),
                 out_specs=pl.BlockSpec((tm, D), lambda i: (i, 0)))(x, w, *tie)


def _norm_bwd_math(xf, w, dh, eps):
    r = lax.rsqrt(jnp.mean(xf * xf, axis=-1, keepdims=True) + eps)
    xhat = xf * r
    dxh = dh * w
    dx = r * (dxh - xhat * jnp.mean(dxh * xhat, axis=-1, keepdims=True))
    dw = jnp.sum(dh * xhat, axis=0, keepdims=True)
    return dx, dw


def _rmsnorm_bwd(x, w, dh, dres, *, name, eps=NORM_EPS, after=None):
    S, D = x.shape
    tm = _tile(S, 512)
    tie = [] if after is None else [after]

    def body(x_ref, w_ref, dh_ref, dr_ref, *rest):
        dx_ref, dw_ref = rest[-2:]
        dx, dw = _norm_bwd_math(x_ref[...], w_ref[...], dh_ref[...].astype(F32), eps)
        dx_ref[...] = dr_ref[...] + dx

        @pl.when(pl.program_id(0) == 0)
        def _():
            dw_ref[...] = jnp.zeros_like(dw_ref)

        dw_ref[...] += dw

    row = pl.BlockSpec((tm, D), lambda i: (i, 0))
    vec = pl.BlockSpec((1, D), lambda i: (0, 0))
    return _call(body, name=name, out_shape=(_sds((S, D)), _sds((1, D))), grid=(S // tm,),
                 in_specs=[row, vec, row, row] + [ANY] * len(tie), out_specs=(row, vec))(x, w, dh, dres, *tie)


def _final_loss(x, w, target, *, name):
    S, D = x.shape
    tm = _tile(S, 512)

    def body(x_ref, w_ref, t_ref, loss_ref, dx_ref, dw_ref):
        xf, wv = x_ref[...], w_ref[...]
        r = lax.rsqrt(jnp.mean(xf * xf, axis=-1, keepdims=True) + NORM_EPS)
        err = xf * r * wv - t_ref[...]
        part = 0.5 * jnp.sum(jnp.mean(err * err, axis=-1, keepdims=True), axis=0, keepdims=True)
        dx, dw = _norm_bwd_math(xf, wv, err * (1.0 / D), NORM_EPS)
        dx_ref[...] = dx

        @pl.when(pl.program_id(0) == 0)
        def _():
            dw_ref[...] = jnp.zeros_like(dw_ref)
            loss_ref[...] = jnp.zeros_like(loss_ref)

        dw_ref[...] += dw
        loss_ref[...] += jnp.broadcast_to(part, loss_ref.shape)

    row = pl.BlockSpec((tm, D), lambda i: (i, 0))
    vec = pl.BlockSpec((1, D), lambda i: (0, 0))
    return _call(body, name=name, out_shape=(_sds((1, LANES)), _sds((S, D)), _sds((1, D))), grid=(S // tm,),
                 in_specs=[row, vec, row], out_specs=(pl.BlockSpec((1, LANES), lambda i: (0, 0)), row, vec))(x, w, target)


def _rope_tables(pos, inv_freq):
    S = pos.shape[0]
    tm = _tile(S, 512)

    def body(p_ref, f_ref, c_ref, s_ref):
        ang = p_ref[...].astype(F32) * f_ref[...]
        c_ref[...] = jnp.cos(ang)
        s_ref[...] = jnp.sin(ang)

    blk = pl.BlockSpec((tm, LANES), lambda i: (i, 0))
    return _call(body, name="rope_tables", out_shape=(_sds((S, LANES)), _sds((S, LANES))), grid=(S // tm,),
                 in_specs=[pl.BlockSpec((tm, 1), lambda i: (i, 0)), pl.BlockSpec((1, LANES), lambda i: (0, 0))],
                 out_specs=(blk, blk))(pos, inv_freq)


def _rot_half(t):
    lane = lax.broadcasted_iota(jnp.int32, t.shape, 1)
    lo = (lane % HEAD_DIM) < (HEAD_DIM // 2)
    return jnp.where(lo, -pltpu.roll(t, LANES - HEAD_DIM // 2, axis=1), pltpu.roll(t, HEAD_DIM // 2, axis=1))


def _rope(t, c, s):
    return t * c + _rot_half(t) * s


def _unrope(dy, c, s):
    return dy * c - _rot_half(dy * s)


PD = POOL_GROUPS * LANES
QD = N_HEADS * HEAD_DIM
KD = N_KV_HEADS * HEAD_DIM
assert PD % QD == 0 and (PD + QD) % (2 * KD) == 0 and KD == LANES
def _attn_probs(q, kcat, sink, mask):
    s = lax.dot_general(q.astype(BF16), kcat, (((1,), (1,)), ((), ())), preferred_element_type=F32) * (HEAD_DIM ** -0.5)
    s = jnp.where(mask, s, -jnp.inf)
    m = jnp.maximum(jnp.max(s, axis=1, keepdims=True), sink)
    p = jnp.exp(s - m)
    ps = jnp.exp(sink - m)
    inv = 1.0 / (jnp.sum(p, axis=1, keepdims=True) + ps)
    return p * inv, ps * inv


def _attn_mask(n):
    qi = lax.broadcasted_iota(jnp.int32, (BLOCK, 2 * BLOCK), 0)
    kj = lax.broadcasted_iota(jnp.int32, (BLOCK, 2 * BLOCK), 1)
    rel = qi + BLOCK - kj
    return (rel >= 0) & (rel < BLOCK) & ((n > 0) | (kj >= BLOCK))


def _attn_in_specs(nb):
    def cur(n):
        return jnp.minimum(n, nb - 1)

    def prev(n):
        return jnp.clip(n - 1, 0, nb - 1)

    kvb = (PD + QD) // (2 * KD)
    return [pl.BlockSpec(memory_space=pltpu.SMEM),
            pl.BlockSpec((BLOCK, QD), lambda n: (cur(n), PD // QD)),
            pl.BlockSpec((BLOCK, 2 * KD), lambda n: (cur(n), kvb)),
            pl.BlockSpec((BLOCK, 2 * KD), lambda n: (prev(n), kvb)),
            pl.BlockSpec((BLOCK, LANES), lambda n: (cur(n), 0)), pl.BlockSpec((BLOCK, LANES), lambda n: (cur(n), 0)),
            pl.BlockSpec((BLOCK, LANES), lambda n: (prev(n), 0)), pl.BlockSpec((BLOCK, LANES), lambda n: (prev(n), 0))]


def _attn_keys(kvc_ref, kvp_ref, cc, sc, cp, sp):
    kc = _rope(kvc_ref[:, :KD], cc, sc)
    kp = _rope(kvp_ref[:, :KD], cp, sp)
    vc, vp = kvc_ref[:, KD:], kvp_ref[:, KD:]
    kcat, vcat = [], []
    for kk in range(N_KV_HEADS):
        sl = slice(kk * HEAD_DIM, (kk + 1) * HEAD_DIM)
        kcat.append(jnp.concatenate([kp[:, sl], kc[:, sl]], axis=0).astype(BF16))
        vcat.append(jnp.concatenate([vp[:, sl], vc[:, sl]], axis=0).astype(BF16))
    return kcat, vcat


def _attn_fwd(proj, cos, sin, sinks, cat):
    S = proj.shape[0]
    nb = S // BLOCK

    def body(sink_ref, q_ref, kvc_ref, kvp_ref, cc_ref, sc_ref, cp_ref, sp_ref, cat_ref, o_ref):
        n = pl.program_id(0)
        cc, sc = cc_ref[...], sc_ref[...]
        kcat, vcat = _attn_keys(kvc_ref, kvp_ref, cc, sc, cp_ref[...], sp_ref[...])
        mask = _attn_mask(n)
        for j in range(QD // LANES):
            qr = _rope(q_ref[:, j * LANES:(j + 1) * LANES], cc, sc)
            for e in range(LANES // HEAD_DIM):
                h = j * (LANES // HEAD_DIM) + e
                pn, _ = _attn_probs(qr[:, e * HEAD_DIM:(e + 1) * HEAD_DIM], kcat[h // GQ], sink_ref[0, h], mask)
                o_ref[:, h * HEAD_DIM:(h + 1) * HEAD_DIM] = jnp.dot(pn.astype(BF16), vcat[h // GQ], preferred_element_type=F32)

    return _call(body, name="attn_fwd", out_shape=_sds(cat.shape), grid=(nb,),
                 in_specs=_attn_in_specs(nb) + [ANY], out_specs=pl.BlockSpec((BLOCK, QD), lambda n: (n, PD // QD)),
                 aliases={8: 0})(sinks, proj, proj, proj, cos, sin, cos, sin, cat)


def _attn_bwd(proj, cos, sin, sinks, dcat):
    S = proj.shape[0]
    nb = S // BLOCK
    scale = HEAD_DIM ** -0.5
    per = LANES // HEAD_DIM

    def body(sink_ref, q_ref, kvc_ref, kvp_ref, cc_ref, sc_ref, cp_ref, sp_ref, do_ref, o_ref, ds_ref, hold, carry, part, pair):
        n = pl.program_id(0)

        @pl.when(n == 0)
        def _():
            hold[...] = jnp.zeros_like(hold)
            carry[...] = jnp.zeros_like(carry)
            ds_ref[...] = jnp.zeros_like(ds_ref)

        live = jnp.where(n < nb, 1.0, 0.0)
        cc, sc, cp, sp = cc_ref[...], sc_ref[...], cp_ref[...], sp_ref[...]
        kcat, vcat = _attn_keys(kvc_ref, kvp_ref, cc, sc, cp, sp)
        mask = _attn_mask(n)
        o_ref[:, :PD] = jnp.zeros((BLOCK, PD), F32)
        o_ref[:, PD:PD + QD] = hold[...]
        dk = [jnp.zeros((2 * BLOCK, HEAD_DIM), F32) for _ in range(N_KV_HEADS)]
        dv = [jnp.zeros((2 * BLOCK, HEAD_DIM), F32) for _ in range(N_KV_HEADS)]
        row = lax.broadcasted_iota(jnp.int32, (8, LANES), 0)
        dsk = jnp.zeros((8, LANES), F32)
        for j in range(QD // LANES):
            qr = _rope(q_ref[:, j * LANES:(j + 1) * LANES], cc, sc)
            for e in range(per):
                h = j * per + e
                kk = h // GQ
                qh = qr[:, e * HEAD_DIM:(e + 1) * HEAD_DIM]
                pn, psn = _attn_probs(qh, kcat[kk], sink_ref[0, h], mask)
                doh = (do_ref[:, h * HEAD_DIM:(h + 1) * HEAD_DIM] * live).astype(BF16)
                dp = lax.dot_general(doh, vcat[kk], NT, preferred_element_type=F32)
                delta = jnp.sum(pn * dp, axis=1, keepdims=True)
                ds = (pn * (dp - delta) * scale).astype(BF16)
                pair[:, e * HEAD_DIM:(e + 1) * HEAD_DIM] = jnp.dot(ds, kcat[kk], preferred_element_type=F32)
                dk[kk] = dk[kk] + lax.dot_general(ds, qh.astype(BF16), TN, preferred_element_type=F32)
                dv[kk] = dv[kk] + lax.dot_general(pn.astype(BF16), doh, TN, preferred_element_type=F32)
                dsk = dsk + jnp.where(row == h, -jnp.sum(psn * delta), 0.0)
            hold[:, j * LANES:(j + 1) * LANES] = _unrope(pair[...], cc, sc)
        for kk in range(N_KV_HEADS):
            sl = slice(kk * HEAD_DIM, (kk + 1) * HEAD_DIM)
            sv = slice(KD + kk * HEAD_DIM, KD + (kk + 1) * HEAD_DIM)
            part[0, :, sl] = dk[kk][:BLOCK]
            part[0, :, sv] = dv[kk][:BLOCK]
            part[1, :, sl] = dk[kk][BLOCK:]
            part[1, :, sv] = dv[kk][BLOCK:]
        done = carry[...] + part[0]
        o_ref[:, PD + QD:PD + QD + KD] = _unrope(done[:, :KD], cp, sp)
        o_ref[:, PD + QD + KD:] = done[:, KD:]
        carry[...] = part[1]
        ds_ref[...] += dsk

    return _call(body, name="attn_bwd", out_shape=(_sds((S, PD + QD + 2 * KD)), _sds((8, LANES))), grid=(nb + 1,),
                 in_specs=_attn_in_specs(nb) + [pl.BlockSpec((BLOCK, QD), lambda n: (jnp.minimum(n, nb - 1), PD // QD))],
                 out_specs=(pl.BlockSpec((BLOCK, PD + QD + 2 * KD), lambda n: (jnp.maximum(n - 1, 0), 0)),
                            pl.BlockSpec((8, LANES), lambda n: (0, 0))),
                 scratch=[pltpu.VMEM((BLOCK, QD), F32), pltpu.VMEM((BLOCK, 2 * KD), F32),
                          pltpu.VMEM((2, BLOCK, 2 * KD), F32), pltpu.VMEM((BLOCK, LANES), F32)])(
                     sinks, proj, proj, proj, cos, sin, cos, sin, dcat)


def _pool_sums(u, g, t, shift):
    s2 = u + shift(u, 1, t)
    s4 = s2 + shift(s2, 2, t)
    s8 = s4 + shift(s4, 4, t)
    s16 = s8 + shift(s8, 8, t)
    return jnp.where(g == 0, s2, jnp.where(g == 1, s4, jnp.where(g == 2, s8, s16)))


def _pool_specs(S):
    col = pl.BlockSpec((S, LANES), lambda g: (0, g))
    wsp = pl.BlockSpec((1, LANES, LANES), lambda g: (g, 0, 0))
    vec = pl.BlockSpec((1, LANES), lambda g: (0, g))
    return col, wsp, vec


def _pool_fwd(proj, pool_w, scale):
    S = proj.shape[0]
    col, wsp, vec = _pool_specs(S)

    def body(u_ref, w_ref, s_ref, o_ref):
        g = pl.program_id(0)
        u = u_ref[...]
        t = lax.broadcasted_iota(jnp.int32, u.shape, 0)
        cnt = jnp.minimum(t + 1, 2 << g).astype(F32)
        pm = _pool_sums(u, g, t, _shift_dn) / cnt - u
        o_ref[...] = jnp.dot(pm.astype(BF16), w_ref[0].astype(BF16), preferred_element_type=F32) * s_ref[...]

    return _call(body, name="pool_fwd", out_shape=_sds((S, PD + QD)), grid=(POOL_GROUPS,),
                 in_specs=[col, wsp, vec], out_specs=col)(proj, pool_w, scale)


def _pool_bwd(proj, pool_w, scale, dcat, dproj):
    S = proj.shape[0]
    col, wsp, vec = _pool_specs(S)

    def body(u_ref, w_ref, s_ref, d_ref, dproj_ref, du_ref, dw_ref, dsc_ref):
        g = pl.program_id(0)
        u = u_ref[...]
        t = lax.broadcasted_iota(jnp.int32, u.shape, 0)
        cnt = jnp.minimum(t + 1, 2 << g).astype(F32)
        pm = (_pool_sums(u, g, t, _shift_dn) / cnt - u).astype(BF16)
        wv = w_ref[0].astype(BF16)
        d = d_ref[...]
        pw = jnp.dot(pm, wv, preferred_element_type=F32)
        dsc_ref[...] = jnp.sum(pw * d, axis=0, keepdims=True)
        dpw = (d * s_ref[...]).astype(BF16)
        dw_ref[0] = lax.dot_general(pm, dpw, (((0,), (0,)), ((), ())), preferred_element_type=F32)
        dpm = lax.dot_general(dpw, wv, (((1,), (1,)), ((), ())), preferred_element_type=F32)
        du_ref[...] = _pool_sums(dpm / cnt, g, t, _shift_up) - dpm

    return _call(body, name="pool_bwd",
                 out_shape=(_sds(dproj.shape), _sds((POOL_GROUPS, LANES, LANES)), _sds((1, POOL_GROUPS * LANES))),
                 grid=(POOL_GROUPS,), in_specs=[col, wsp, vec, col, ANY], out_specs=(col, wsp, vec),
                 aliases={4: 0})(proj, pool_w, scale, dcat, dproj)


def _conv(x, w_ref, b_ref, t):
    K = w_ref.shape[0]
    y = b_ref[...] + jnp.zeros_like(x)
    for k in range(K):
        y = y + w_ref[k:k + 1, :] * _shift_dn(x, K - 1 - k, t)
    return y


def _conv_bwd(x, dy, w_ref, t):
    K = w_ref.shape[0]
    dx = jnp.zeros_like(x)
    dws = []
    for k in range(K):
        dx = dx + w_ref[k:k + 1, :] * _shift_up(dy, K - 1 - k, t)
        dws.append(jnp.sum(dy * _shift_dn(x, K - 1 - k, t), axis=0, keepdims=True))
    return dx, dws, jnp.sum(dy, axis=0, keepdims=True)


def _silu_grad(y):
    sg = _sigmoid(y)
    return sg * (1.0 + y * (1.0 - sg))


def _ffn_mid_specs(S, K, layer):
    return [pl.BlockSpec((S, LANES), lambda j: (0, 2 * j)), pl.BlockSpec((S, LANES), lambda j: (0, 2 * j + 1)),
            pl.BlockSpec((None, K, LANES), lambda j: (layer, 0, 2 * j)), pl.BlockSpec((None, K, LANES), lambda j: (layer, 0, 2 * j + 1)),
            pl.BlockSpec((None, 1, LANES), lambda j: (layer, 0, 2 * j)), pl.BlockSpec((None, 1, LANES), lambda j: (layer, 0, 2 * j + 1))]


def _ffn_mid_fwd(a, cw, cb, layer):
    S, F2 = a.shape
    nf = F2 // (2 * LANES)
    K = cw.shape[1]

    def body(au_ref, ag_ref, wu_ref, wg_ref, bu_ref, bg_ref, o_ref):
        t = lax.broadcasted_iota(jnp.int32, (S, LANES), 0)
        hu = _conv(au_ref[...], wu_ref, bu_ref, t)
        hg = _conv(ag_ref[...], wg_ref, bg_ref, t)
        o_ref[...] = (hg * _sigmoid(hg) * hu).astype(BF16)

    return _call(body, name="ffn_mid_fwd", out_shape=_sds((S, F2 // 2), BF16), grid=(nf,),
                 in_specs=_ffn_mid_specs(S, K, layer), out_specs=pl.BlockSpec((S, LANES), lambda j: (0, j)))(
                     a, a, cw, cw, cb[:, None], cb[:, None])


def _ffn_mid_bwd(a, cw, cb, layer, dact):
    S, F2 = a.shape
    nf = F2 // (2 * LANES)
    K = cw.shape[1]

    def body(au_ref, ag_ref, wu_ref, wg_ref, bu_ref, bg_ref, d_ref, da_ref, dw_ref, db_ref):
        t = lax.broadcasted_iota(jnp.int32, (S, LANES), 0)
        au, ag = au_ref[...], ag_ref[...]
        hu = _conv(au, wu_ref, bu_ref, t)
        hg = _conv(ag, wg_ref, bg_ref, t)
        d = d_ref[...].astype(F32)
        dhu = d * hg * _sigmoid(hg)
        dhg = d * hu * _silu_grad(hg)
        dau, dwu, dbu = _conv_bwd(au, dhu, wu_ref, t)
        dag, dwg, dbg = _conv_bwd(ag, dhg, wg_ref, t)
        da_ref[:, :LANES] = dau.astype(BF16)
        da_ref[:, LANES:] = dag.astype(BF16)
        for k in range(K):
            dw_ref[k:k + 1, :LANES] = dwu[k]
            dw_ref[k:k + 1, LANES:] = dwg[k]
        db_ref[:, :LANES] = dbu
        db_ref[:, LANES:] = dbg

    return _call(body, name="ffn_mid_bwd", out_shape=(_sds((S, F2), BF16), _sds((K, F2)), _sds((1, F2))), grid=(nf,),
                 in_specs=_ffn_mid_specs(S, K, layer) + [pl.BlockSpec((S, LANES), lambda j: (0, j))],
                 out_specs=(pl.BlockSpec((S, 2 * LANES), lambda j: (0, j)), pl.BlockSpec((K, 2 * LANES), lambda j: (0, j)),
                            pl.BlockSpec((1, 2 * LANES), lambda j: (0, j))))(a, a, cw, cw, cb[:, None], cb[:, None], dact)


def _conv_silu_fwd(x, cw, cb):
    S = x.shape[0]
    K, C = cw.shape

    def body(x_ref, w_ref, b_ref, o_ref):
        t = lax.broadcasted_iota(jnp.int32, (S, LANES), 0)
        y = _conv(x_ref[...], w_ref, b_ref, t)
        o_ref[...] = y * _sigmoid(y)

    col = pl.BlockSpec((S, LANES), lambda j: (0, j))
    return _call(body, name="conv_silu_fwd", out_shape=_sds((S, C)), grid=(C // LANES,),
                 in_specs=[col, pl.BlockSpec((K, LANES), lambda j: (0, j)), pl.BlockSpec((1, LANES), lambda j: (0, j))],
                 out_specs=col)(x, cw, cb)


def _conv_silu_bwd(x, cw, cb, douts):
    S = x.shape[0]
    K, C = cw.shape
    starts, off = [], 0
    for d in douts:
        starts.append(off)
        off += d.shape[1] // LANES
    assert off == C // LANES

    def body(x_ref, w_ref, b_ref, *rest):
        d_refs, (dx_ref, dw_ref, db_ref) = rest[:len(douts)], rest[len(douts):]
        j = pl.program_id(0)
        t = lax.broadcasted_iota(jnp.int32, (S, LANES), 0)
        x = x_ref[...]
        y = _conv(x, w_ref, b_ref, t)
        d = d_refs[0][...]
        for i in range(1, len(douts)):
            d = jnp.where(j >= starts[i], d_refs[i][...], d)
        dy = d * _silu_grad(y)
        dx, dw, db = _conv_bwd(x, dy, w_ref, t)
        dx_ref[...] = dx.astype(BF16)
        for k in range(K):
            dw_ref[k:k + 1, :] = dw[k]
        db_ref[...] = db

    col = pl.BlockSpec((S, LANES), lambda j: (0, j))
    wsp = pl.BlockSpec((K, LANES), lambda j: (0, j))
    bsp = pl.BlockSpec((1, LANES), lambda j: (0, j))

    def dspec(i):
        nblk = douts[i].shape[1] // LANES
        return pl.BlockSpec((S, LANES), lambda j: (0, jnp.clip(j - starts[i], 0, nblk - 1)))

    return _call(body, name="conv_silu_bwd", out_shape=(_sds((S, C), BF16), _sds((K, C)), _sds((1, C))), grid=(C // LANES,),
                 in_specs=[col, wsp, bsp] + [dspec(i) for i in range(len(douts))],
                 out_specs=(col, wsp, bsp))(x, cw, cb, *douts)


HI = lax.Precision.HIGHEST


def _ssd_prep_fwd(proj, col0, bias_g, alog_g):
    S = proj.shape[0]
    nc = S // SSM_L
    b0 = col0 // LANES

    def body(raw_ref, b_ref, al_ref, pre_ref, dt_ref, acs_ref, acst_ref):
        r_i = lax.broadcasted_iota(jnp.int32, (LANES, LANES), 0)
        c_i = lax.broadcasted_iota(jnp.int32, (LANES, LANES), 1)
        live = c_i < SSM_R
        tril = jnp.where(r_i >= c_i, 1.0, 0.0)
        raw = raw_ref[...]
        for g in range(SSM_G):
            sel = jnp.where((r_i == SSM_R * g + c_i) & live, 1.0, 0.0)
            pre = jnp.dot(raw, sel, preferred_element_type=F32, precision=HI) + b_ref[g]
            dt = jnp.where(live, jnp.logaddexp(pre, 0.0), 0.0)
            a = dt * (-jnp.exp(al_ref[g]))
            acs = jnp.dot(tril, a, preferred_element_type=F32, precision=HI)
            pre_ref[g] = pre
            dt_ref[g] = dt
            acs_ref[g] = acs
            acst_ref[g] = acs.T

    gsp = pl.BlockSpec((SSM_G, 1, LANES), lambda c: (0, 0, 0))
    blk = pl.BlockSpec((SSM_G, SSM_L, LANES), lambda c: (0, c, 0))
    big = _sds((SSM_G, S, LANES))
    return _call(body, name="ssd_prep_fwd", out_shape=(big, big, big, _sds((SSM_G, LANES, S))), grid=(nc,),
                 in_specs=[pl.BlockSpec((SSM_L, LANES), lambda c: (c, b0)), gsp, gsp],
                 out_specs=(blk, blk, blk, pl.BlockSpec((SSM_G, LANES, SSM_L), lambda c: (0, 0, c))))(proj, bias_g, alog_g)


def _ssd_prep_bwd(pre_g, dt_g, alog_g, ddt_g, dacs_g, dacst_g):
    S = pre_g.shape[1]
    nc = S // SSM_L

    def body(pre_ref, dt_ref, al_ref, ddt_ref, dacs_ref, dacst_ref, draw_ref, db_ref, dal_ref):
        c = pl.program_id(0)
        r_i = lax.broadcasted_iota(jnp.int32, (LANES, LANES), 0)
        c_i = lax.broadcasted_iota(jnp.int32, (LANES, LANES), 1)
        live = c_i < SSM_R
        triu = jnp.where(r_i <= c_i, 1.0, 0.0)

        @pl.when(c == 0)
        def _():
            db_ref[...] = jnp.zeros_like(db_ref)
            dal_ref[...] = jnp.zeros_like(dal_ref)

        draw = jnp.zeros((SSM_L, LANES), F32)
        for g in range(SSM_G):
            dacs = dacs_ref[g] + dacst_ref[g].T
            da = jnp.dot(triu, dacs, preferred_element_type=F32, precision=HI)
            A = -jnp.exp(al_ref[g])
            ddt = ddt_ref[g] + da * A
            dpre = jnp.where(live, ddt * _sigmoid(pre_ref[g]), 0.0)
            unsel = jnp.where((c_i == SSM_R * g + r_i) & (r_i < SSM_R), 1.0, 0.0)
            draw = draw + jnp.dot(dpre, unsel, preferred_element_type=F32, precision=HI)
            db_ref[g] += jnp.sum(dpre, axis=0, keepdims=True)
            dal_ref[g] += jnp.where(live[:1], jnp.sum(da * dt_ref[g], axis=0, keepdims=True) * A, 0.0)
        draw_ref[...] = draw

    gsp = pl.BlockSpec((SSM_G, 1, LANES), lambda c: (0, 0, 0))
    blk = pl.BlockSpec((SSM_G, SSM_L, LANES), lambda c: (0, c, 0))
    gout = _sds((SSM_G, 1, LANES))
    return _call(body, name="ssd_prep_bwd", out_shape=(_sds((S, LANES)), gout, gout), grid=(nc,),
                 in_specs=[blk, blk, gsp, blk, blk, pl.BlockSpec((SSM_G, LANES, SSM_L), lambda c: (0, 0, c))],
                 out_specs=(pl.BlockSpec((SSM_L, LANES), lambda c: (c, 0)), gsp, gsp))(pre_g, dt_g, alog_g, ddt_g, dacs_g, dacst_g)


def _ssd_chunk_terms(xs_h, dt_col, acs_col, acs_row, acs_last, Gm, tril):
    X = xs_h * dt_col
    Lm = jnp.exp(jnp.where(tril, acs_col - acs_row, -jnp.inf))
    M = Gm * Lm
    e_col = jnp.exp(acs_col)
    decay = jnp.exp(acs_last - acs_col)
    cd = jnp.exp(acs_last)
    return X, Lm, M, e_col, decay, cd


NT = (((1,), (1,)), ((), ()))
TN = (((0,), (0,)), ((), ()))


def _ssd_specs(rev, nc):
    def cc(c):
        return nc - 1 - c if rev else c
    xs = pl.BlockSpec((SSM_L, SSM_R * SSM_P), lambda g, c: (cc(c), g))
    bsp = pl.BlockSpec((SSM_L, SSM_N), lambda g, c: (cc(c), (SSM_G * SSM_R * SSM_P) // SSM_N + g))
    csp = pl.BlockSpec((SSM_L, SSM_N), lambda g, c: (cc(c), (SSM_G * SSM_R * SSM_P) // SSM_N + SSM_G + g))
    sc = pl.BlockSpec((1, SSM_L, LANES), lambda g, c: (g, cc(c), 0))
    sct = pl.BlockSpec((1, LANES, SSM_L), lambda g, c: (g, 0, cc(c)))
    gsp = pl.BlockSpec((1, 1, LANES), lambda g, c: (g, 0, 0))
    st = pl.BlockSpec((1, SSM_R, SSM_P, SSM_N), lambda g, c: (cc(c), g, 0, 0))
    return xs, bsp, csp, sc, sct, gsp, st


def _ssd_fwd(xbc, dt_g, acs_g, acst_g, d_g):
    S = xbc.shape[0]
    nc = S // SSM_L
    xs_s, b_s, c_s, sc, sct, gsp, st = _ssd_specs(False, nc)

    def body(xs_ref, b_ref, c_ref, dt_ref, acs_ref, acst_ref, d_ref, y_ref, st_ref, state):
        c = pl.program_id(1)

        @pl.when(c == 0)
        def _():
            state[...] = jnp.zeros_like(state)

        Bm, Cm = b_ref[...], c_ref[...]
        Bb, Cb = Bm.astype(BF16), Cm.astype(BF16)
        Gm = lax.dot_general(Cb, Bb, NT, preferred_element_type=F32)
        r_i = lax.broadcasted_iota(jnp.int32, (SSM_L, SSM_L), 0)
        c_i = lax.broadcasted_iota(jnp.int32, (SSM_L, SSM_L), 1)
        tril = r_i >= c_i
        for r in range(SSM_R):
            xs_h = xs_ref[:, r * SSM_P:(r + 1) * SSM_P]
            X, Lm, M, e_col, decay, cd = _ssd_chunk_terms(
                xs_h, dt_ref[0, :, r:r + 1], acs_ref[0, :, r:r + 1], acst_ref[0, r:r + 1, :],
                acs_ref[0, SSM_L - 1:SSM_L, r:r + 1], Gm, tril)
            Xb = X.astype(BF16)
            Sh = state[r]
            st_ref[0, r] = Sh
            yd = jnp.dot(M.astype(BF16), Xb, preferred_element_type=F32)
            yo = e_col * lax.dot_general(Cb, Sh.astype(BF16), NT, preferred_element_type=F32)
            y_ref[:, r * SSM_P:(r + 1) * SSM_P] = yd + yo + d_ref[0, :, r:r + 1] * xs_h
            state[r] = Sh * cd + lax.dot_general(Xb, (Bm * decay).astype(BF16), TN, preferred_element_type=F32)

    return _call(body, name="ssd_fwd",
                 out_shape=(_sds((S, SSM_G * SSM_R * SSM_P)), _sds((nc, SSM_G * SSM_R, SSM_P, SSM_N))),
                 grid=(SSM_G, nc), in_specs=[xs_s, b_s, c_s, sc, sc, sct, gsp],
                 out_specs=(xs_s, pl.BlockSpec((1, SSM_R, SSM_P, SSM_N), lambda g, c: (c, g, 0, 0))),
                 scratch=[pltpu.VMEM((SSM_R, SSM_P, SSM_N), F32)])(xbc, xbc, xbc, dt_g, acs_g, acst_g, d_g)


def _ssd_bwd(xbc, dt_g, acs_g, acst_g, d_g, states, dy):
    S = xbc.shape[0]
    nc = S // SSM_L
    xs_s, b_s, c_s, sc, sct, gsp, st = _ssd_specs(True, nc)
    bc_out = pl.BlockSpec((SSM_L, SSM_N), lambda g, c: (nc - 1 - c, g))

    def body(xs_ref, b_ref, c_ref, dt_ref, acs_ref, acst_ref, d_ref, st_ref, dy_ref,
             dxs_ref, db_ref, dc_ref, ddt_ref, dacs_ref, dacst_ref, dd_ref, dstate):
        c = pl.program_id(1)

        @pl.when(c == 0)
        def _():
            dstate[...] = jnp.zeros_like(dstate)
            dd_ref[...] = jnp.zeros_like(dd_ref)

        Bm, Cm = b_ref[...], c_ref[...]
        Bb, Cb = Bm.astype(BF16), Cm.astype(BF16)
        Gm = lax.dot_general(Cb, Bb, NT, preferred_element_type=F32)
        r_i = lax.broadcasted_iota(jnp.int32, (SSM_L, SSM_L), 0)
        c_i = lax.broadcasted_iota(jnp.int32, (SSM_L, SSM_L), 1)
        tril = r_i >= c_i
        lane = lax.broadcasted_iota(jnp.int32, (1, LANES), 1)
        subl = lax.broadcasted_iota(jnp.int32, (LANES, 1), 0)
        last_row = lax.broadcasted_iota(jnp.int32, (SSM_L, 1), 0) == SSM_L - 1
        dG = jnp.zeros((SSM_L, SSM_L), F32)
        dB = jnp.zeros((SSM_L, SSM_N), F32)
        dC = jnp.zeros((SSM_L, SSM_N), F32)
        ddt_blk = jnp.zeros((SSM_L, LANES), F32)
        dacs_blk = jnp.zeros((SSM_L, LANES), F32)
        dacst_blk = jnp.zeros((LANES, SSM_L), F32)
        dd_row = jnp.zeros((1, LANES), F32)
        for r in range(SSM_R):
            xs_h = xs_ref[:, r * SSM_P:(r + 1) * SSM_P]
            dt_col = dt_ref[0, :, r:r + 1]
            X, Lm, M, e_col, decay, cd = _ssd_chunk_terms(
                xs_h, dt_col, acs_ref[0, :, r:r + 1], acst_ref[0, r:r + 1, :],
                acs_ref[0, SSM_L - 1:SSM_L, r:r + 1], Gm, tril)
            Xb, Mb = X.astype(BF16), M.astype(BF16)
            Sh = st_ref[0, r]
            Shb = Sh.astype(BF16)
            dY = dy_ref[:, r * SSM_P:(r + 1) * SSM_P]
            dYb = dY.astype(BF16)
            dSn = dstate[r]
            dSnb = dSn.astype(BF16)
            dM = lax.dot_general(dYb, Xb, NT, preferred_element_type=F32)
            dX = lax.dot_general(Mb, dYb, TN, preferred_element_type=F32)
            dG = dG + dM * Lm
            dseg = dM * M
            dacs_col = jnp.sum(dseg, axis=1, keepdims=True)
            dacs_row = -jnp.sum(dseg, axis=0, keepdims=True)
            T = lax.dot_general(Cb, Shb, NT, preferred_element_type=F32)
            dT = (dY * e_col).astype(BF16)
            dC = dC + jnp.dot(dT, Shb, preferred_element_type=F32)
            dS_prev = lax.dot_general(dT, Cb, TN, preferred_element_type=F32)
            dacs_col = dacs_col + jnp.sum(dY * (e_col * T), axis=1, keepdims=True)
            Bd = (Bm * decay).astype(BF16)
            dS_prev = dS_prev + dSn * cd
            dcd = jnp.sum(dSn * Sh)
            dX = dX + lax.dot_general(Bd, dSnb, NT, preferred_element_type=F32)
            dBd = jnp.dot(Xb, dSnb, preferred_element_type=F32)
            dB = dB + dBd * decay
            dd = jnp.sum(dBd * Bm, axis=1, keepdims=True) * decay
            dacs_col = dacs_col - dd + jnp.where(last_row, dcd * cd + jnp.sum(dd), 0.0)
            dstate[r] = dS_prev
            dxs_ref[:, r * SSM_P:(r + 1) * SSM_P] = dX * dt_col + d_ref[0, :, r:r + 1] * dY
            ddt_blk = ddt_blk + jnp.where(lane == r, jnp.sum(dX * xs_h, axis=1, keepdims=True), 0.0)
            dacs_blk = dacs_blk + jnp.where(lane == r, dacs_col, 0.0)
            dacst_blk = dacst_blk + jnp.where(subl == r, dacs_row, 0.0)
            dd_row = dd_row + jnp.where(lane == r, jnp.sum(dY * xs_h), 0.0)
        dGb = dG.astype(BF16)
        dc_ref[...] = dC + jnp.dot(dGb, Bb, preferred_element_type=F32)
        db_ref[...] = dB + lax.dot_general(dGb, Cb, TN, preferred_element_type=F32)
        ddt_ref[0] = ddt_blk
        dacs_ref[0] = dacs_blk
        dacst_ref[0] = dacst_blk
        dd_ref[0] += dd_row

    big = _sds((SSM_G, S, LANES))
    return _call(body, name="ssd_bwd",
                 out_shape=(_sds((S, SSM_G * SSM_R * SSM_P)), _sds((S, SSM_G * SSM_N)), _sds((S, SSM_G * SSM_N)),
                            big, big, _sds((SSM_G, LANES, S)), _sds((SSM_G, 1, LANES))),
                 grid=(SSM_G, nc), in_specs=[xs_s, b_s, c_s, sc, sc, sct, gsp, st, xs_s],
                 out_specs=(xs_s, bc_out, bc_out, sc, sc, sct, gsp),
                 scratch=[pltpu.VMEM((SSM_R, SSM_P, SSM_N), F32)])(xbc, xbc, xbc, dt_g, acs_g, acst_g, d_g, states, dy)


def _gate_norm_fwd(y, proj, w):
    S, DI = y.shape
    tm = _tile(S, 256)

    def body(y_ref, z_ref, w_ref, o_ref):
        z = z_ref[...]
        gn = y_ref[...] * (z * _sigmoid(z))
        r = lax.rsqrt(jnp.mean(gn * gn, axis=-1, keepdims=True) + SSM_NORM_EPS)
        o_ref[...] = (gn * r * w_ref[...]).astype(BF16)

    row = pl.BlockSpec((tm, DI), lambda i: (i, 0))
    return _call(body, name="gate_norm_fwd", out_shape=_sds((S, DI), BF16), grid=(S // tm,),
                 in_specs=[row, row, pl.BlockSpec((1, DI), lambda i: (0, 0))], out_specs=row)(y, proj, w)


def _gate_norm_bwd(y, proj, w, dout):
    S, DI = y.shape
    tm = _tile(S, 256)

    def body(y_ref, z_ref, w_ref, d_ref, dy_ref, dz_ref, dw_ref):
        z, yv = z_ref[...], y_ref[...]
        sz = z * _sigmoid(z)
        dgn, dw = _norm_bwd_math(yv * sz, w_ref[...], d_ref[...].astype(F32), SSM_NORM_EPS)
        dy_ref[...] = dgn * sz
        dz_ref[...] = (dgn * yv * _silu_grad(z)).astype(BF16)

        @pl.when(pl.program_id(0) == 0)
        def _():
            dw_ref[...] = jnp.zeros_like(dw_ref)

        dw_ref[...] += dw

    row = pl.BlockSpec((tm, DI), lambda i: (i, 0))
    vec = pl.BlockSpec((1, DI), lambda i: (0, 0))
    return _call(body, name="gate_norm_bwd", out_shape=(_sds((S, DI)), _sds((S, DI), BF16), _sds((1, DI))), grid=(S // tm,),
                 in_specs=[row, row, vec, row], out_specs=(row, row, vec))(y, proj, w, dout)


def _group_major(v):
    return jnp.pad(v.reshape(SSM_G, 1, SSM_R), ((0, 0), (0, 0), (0, LANES - SSM_R)))


def _ungroup(t):
    return t[:, :SSM_R].reshape(1, SSM_G * SSM_R)


def _ffn_fwd(x, P, l):
    h = _rmsnorm(x, P["norm_ffn"][l:l + 1], name=f"ffn{l}_norm")
    a = _mm(h, P[f"ffn_w_up{l}"], name=f"ffn{l}_up")
    act = _ffn_mid_fwd(a, P["ffn_conv_w"], P["ffn_conv_b"], l)
    out = _mm(act, P[f"ffn_w_down{l}"], res=x, name=f"ffn{l}_down")
    return out, (x, h, a, act)


def _ffn_bwd(saved, P, l, dx, emit):
    x, h, a, act = saved
    dact = _mm(dx, P[f"ffn_w_down{l}"], tb=True, out_dtype=BF16, name=f"ffn{l}_down_dx")
    dw_down = _mm(act, dx, ta=True, out_dtype=PAYLOAD, name=f"ffn{l}_down_dw")
    da, dcw, dcb = _ffn_mid_bwd(a, P["ffn_conv_w"], P["ffn_conv_b"], l, dact)
    dw_up = _mm(h, da, ta=True, out_dtype=PAYLOAD, name=f"ffn{l}_up_dw")
    tie = emit(f"ffn{l}", {"ffn_w_up": dw_up, "ffn_w_down": dw_down})
    dh = _mm(da, P[f"ffn_w_up{l}"], tb=True, name=f"ffn{l}_up_dx")
    dx_in, dnw = _rmsnorm_bwd(x, P["norm_ffn"][l:l + 1], dh, dx, name=f"ffn{l}_norm_bwd", after=tie)
    return dx_in, dnw, dcw, dcb


def _local_step(x, positions, target, P, need, emit, after=None):
    S, D = x.shape
    inv_freq = ROPE_THETA ** (-jnp.arange(0, HEAD_DIM, 2, dtype=F32) / HEAD_DIM)
    inv_freq = jnp.tile(inv_freq, LANES // (HEAD_DIM // 2)).reshape(1, LANES)
    cos, sin = _rope_tables(positions, inv_freq)

    nm0 = P["norm_mix"][0:1]
    h0 = _rmsnorm(x, nm0, name="mix_norm", after=after)
    proj0 = _mm(h0, P["mix_w_in"], name="mix_in")
    cat0 = _attn_fwd(proj0, cos, sin, P["attn_sinks"], _pool_fwd(proj0, P["pool_w"][0], P["pool_scale"]))
    x1 = _mm(cat0, P["mix_w_out"], res=x, name="mix_out")
    need("ffn0", x1)
    x2, ffn0 = _ffn_fwd(x1, P, 0)

    nm1 = P["norm_mix"][1:2]
    need("ssm", x2)
    h1 = _rmsnorm(x2, nm1, name="ssm_norm_in")
    z = _mm(h1, P["ssm_wz"], name="ssm_in_z")
    xbcp = _mm(h1, P["ssm_wxbc"], name="ssm_in_xbc")
    dtraw = _mm(h1, P["ssm_wdt"], name="ssm_in_dt")
    xbc = _conv_silu_fwd(xbcp, P["ssm_conv_w"], P["ssm_conv_b"])
    bias_g, alog_g, d_g = _group_major(P["ssm_dt_bias"]), _group_major(P["ssm_A_log"]), _group_major(P["ssm_D"])
    pre_g, dt_g, acs_g, acst_g = _ssd_prep_fwd(dtraw, 0, bias_g, alog_g)
    y, states = _ssd_fwd(xbc, dt_g, acs_g, acst_g, d_g)
    yn = _gate_norm_fwd(y, z, P["ssm_norm"])
    x3 = _mm(yn, P["ssm_w_out"], res=x2, name="ssm_out")
    need("ffn1", x3)
    x4, ffn1 = _ffn_fwd(x3, P, 1)

    loss, dx, d_norm_final = _final_loss(x4, P["norm_final"].reshape(1, D), target, name="final_loss")
    dx, dnf1, dcw1, dcb1 = _ffn_bwd(ffn1, P, 1, dx, emit)
    dyn = _mm(dx, P["ssm_w_out"], tb=True, out_dtype=BF16, name="ssm_out_dx")
    d_w_out1 = _mm(yn, dx, ta=True, out_dtype=PAYLOAD, name="ssm_out_dw")
    dy, dz, d_ssm_norm = _gate_norm_bwd(y, z, P["ssm_norm"], dyn)
    dxs, dB, dC, ddt_g, dacs_g, dacst_g, dd_g = _ssd_bwd(xbc, dt_g, acs_g, acst_g, d_g, states, dy)
    draw, dbias_g, dalog_g = _ssd_prep_bwd(pre_g, dt_g, alog_g, ddt_g, dacs_g, dacst_g)
    dxbc, d_conv_w1, d_conv_b1 = _conv_silu_bwd(xbcp, P["ssm_conv_w"], P["ssm_conv_b"], [dxs, dB, dC])
    d_wz = _mm(h1, dz, ta=True, out_dtype=PAYLOAD, name="ssm_in_z_dw")
    d_wxbc = _mm(h1, dxbc, ta=True, out_dtype=PAYLOAD, name="ssm_in_xbc_dw")
    d_wdt = _mm(h1, draw, ta=True, out_dtype=PAYLOAD, name="ssm_in_dt_dw")
    tie = emit("ssm", {"ssm_wz": d_wz, "ssm_wxbc": d_wxbc, "ssm_wdt": d_wdt, "ssm_w_out": d_w_out1,
                       "ssm_conv_w": d_conv_w1, "ssm_conv_b": d_conv_b1, "ssm_norm": d_ssm_norm})
    dh1 = _mm(dz, P["ssm_wz"], tb=True, name="ssm_in_z_dx")
    dh1 = _mm(dxbc, P["ssm_wxbc"], tb=True, res=dh1, name="ssm_in_xbc_dx")
    dh1 = _mm(draw, P["ssm_wdt"], tb=True, res=dh1, name="ssm_in_dt_dx")
    dx, dnm1 = _rmsnorm_bwd(x2, nm1, dh1, dx, name="ssm_norm_in_bwd", after=tie)
    dx, dnf0, dcw0, dcb0 = _ffn_bwd(ffn0, P, 0, dx, emit)
    dcat = _mm(dx, P["mix_w_out"], tb=True, name="mix_out_dx")
    d_w_out0 = _mm(cat0, dx, ta=True, out_dtype=PAYLOAD, name="mix_out_dw")
    dproj0, dsk = _attn_bwd(proj0, cos, sin, P["attn_sinks"], dcat)
    dproj0, d_pool_w, d_pool_scale = _pool_bwd(proj0, P["pool_w"][0], P["pool_scale"], dcat, dproj0)
    d_w_in0 = _mm(h0, dproj0, ta=True, out_dtype=PAYLOAD, name="mix_in_dw")
    tie = emit("mix", {"mix_w_in": d_w_in0, "mix_w_out": d_w_out0, "ffn_conv_w": jnp.stack([dcw0, dcw1])})
    dh0 = _mm(dproj0, P["mix_w_in"], tb=True, name="mix_in_dx")
    grad_x, dnm0 = _rmsnorm_bwd(x, nm0, dh0, dx, name="mix_norm_bwd", after=tie)

    small = {
        "norm_mix": jnp.concatenate([dnm0, dnm1], axis=0),
        "norm_ffn": jnp.concatenate([dnf0, dnf1], axis=0),
        "norm_final": d_norm_final,
        "pool_w": d_pool_w,
        "pool_scale": d_pool_scale,
        "attn_sinks_rows": dsk,
        "ssm_dt_bias_g": dbias_g, "ssm_A_log_g": dalog_g, "ssm_D_g": dd_g,
        "ffn_conv_b": jnp.concatenate([dcb0, dcb1], axis=0),
    }
    return loss, grad_x, small


def _peer(k):
    x, y, c = lax.axis_index("x"), lax.axis_index("y"), lax.axis_index("c")
    px = 1 - x if k & 4 else x
    py = 1 - y if k & 2 else y
    pc = 1 - c if k & 1 else c
    return (px, py, pc), 4 * px + 2 * py + pc


def _my_index():
    return 4 * lax.axis_index("x") + 2 * lax.axis_index("y") + lax.axis_index("c")


def _land_sds(a, mode, gather):
    if mode == "slab":
        return _sds(((N_DEV,) + a.shape) if gather else a.shape, a.dtype)
    assert mode == "rows", mode
    return _sds((N_DEV * a.shape[0],) + a.shape[1:] if gather else (N_DEV, a.shape[0] // N_DEV) + a.shape[1:], a.dtype)


def _part(ref, mode, shape, idx):
    if mode == "slab":
        return ref.at[idx]
    r = shape[0] // N_DEV
    return ref.at[pl.ds(idx * r, r)]


def _own_copies(ops, gather, srcs, lands, sems):
    me = _my_index()
    out = []
    for i, (a, mode) in enumerate(ops):
        s = srcs[i] if gather else _part(srcs[i], mode, a.shape, me)
        d = _part(lands[i], mode, _land_sds(a, mode, gather).shape, me) if gather else lands[i].at[me]
        out.append(pltpu.make_async_copy(s, d, sems.at[i]))
    return out


def _remote_copies(ops, gather, srcs, lands, send_sems, recv_sems):
    me = _my_index()
    n = len(ops)
    out = []
    for k in range(1, N_DEV):
        dev, idx = _peer(k)
        for i, (a, mode) in enumerate(ops):
            s = srcs[i] if gather else _part(srcs[i], mode, a.shape, idx)
            d = _part(lands[i], mode, _land_sds(a, mode, gather).shape, me) if gather else lands[i].at[me]
            out.append(pltpu.make_async_remote_copy(src_ref=s, dst_ref=d, send_sem=send_sems.at[(k - 1) * n + i],
                                                    recv_sem=recv_sems.at[(k - 1) * n + i], device_id=dev,
                                                    device_id_type=pl.DeviceIdType.MESH))
    return out


def _exchange(ops, *, gather, name):
    n = len(ops)

    def body(*refs):
        ins, outs = refs[:n], refs[n:2 * n]
        send_sems, recv_sems, local_sems = refs[2 * n:]
        copies = _own_copies(ops, gather, ins, outs, local_sems) + _remote_copies(ops, gather, ins, outs, send_sems, recv_sems)
        for cp in copies:
            cp.start()
        for cp in copies:
            cp.wait()

    return pl.pallas_call(
        body, name=name, out_shape=[_land_sds(a, m, gather) for a, m in ops], in_specs=[ANY] * n, out_specs=[ANY] * n,
        scratch_shapes=[pltpu.SemaphoreType.DMA((n * (N_DEV - 1),)), pltpu.SemaphoreType.DMA((n * (N_DEV - 1),)),
                        pltpu.SemaphoreType.DMA((n,))],
    )(*[a for a, _ in ops])


HBM = pl.BlockSpec(memory_space=pltpu.HBM)
SEM = pl.BlockSpec(memory_space=pltpu.SEMAPHORE)
SIDE_EFFECT = pltpu.SideEffectType.DATAFLOW_SIDE_EFFECTING


def _in_hbm(a):
    return pltpu.with_memory_space_constraint(a, pltpu.HBM)


def _place_own(ops, *, gather, name):
    n = len(ops)

    def body(*refs):
        copies = _own_copies(ops, gather, refs[:n], refs[n:2 * n], refs[2 * n])
        for cp in copies:
            cp.start()
        for cp in copies:
            cp.wait()

    return pl.pallas_call(
        body, name=name, out_shape=[_land_sds(a, m, gather) for a, m in ops], in_specs=[ANY] * n, out_specs=[ANY] * n,
        scratch_shapes=[pltpu.SemaphoreType.DMA((n,))])(*[a for a, _ in ops])


def _exchange_start(groups, *, gather, name):
    sizes = [len(ops) for ops, _ in groups]
    n = sum(sizes)
    G = len(groups)

    def body(*refs):
        srcs, lands = refs[:n], refs[n:2 * n]
        sems = refs[2 * n:2 * n + 2 * G]
        token = refs[-1]
        off = 0
        for g, (ops, _) in enumerate(groups):
            for cp in _remote_copies(ops, gather, srcs[off:off + sizes[g]], lands[off:off + sizes[g]], sems[2 * g], sems[2 * g + 1]):
                cp.start()
            off += sizes[g]
        token[...] = jnp.zeros_like(token)

    srcs = [a for ops, _ in groups for a, _ in ops]
    lands = [l for _, ls in groups for l in ls]
    sem_shapes = [pltpu.SemaphoreType.DMA((s * (N_DEV - 1),)) for s in sizes for _ in range(2)]
    outs = pl.pallas_call(
        body, name=name,
        out_shape=sem_shapes + [pltpu.HBM(a.shape, a.dtype) for a in srcs + lands] + [_sds((8, LANES))],
        in_specs=[HBM] * (2 * n), out_specs=[SEM] * (2 * G) + [HBM] * (2 * n) + [pl.BlockSpec(memory_space=pltpu.VMEM)],
        input_output_aliases={i: 2 * G + i for i in range(2 * n)},
        compiler_params=pltpu.CompilerParams(has_side_effects=SIDE_EFFECT))(*[_in_hbm(a) for a in srcs + lands])
    sems, thru, token = outs[:2 * G], outs[2 * G:2 * G + 2 * n], outs[-1]
    states, off = [], 0
    for g, s in enumerate(sizes):
        states.append((sems[2 * g], sems[2 * g + 1], thru[off:off + s], thru[n + off:n + off + s]))
        off += s
    return states, token


def _exchange_wait(ops, state, after, *, gather, name):
    send_sems, recv_sems, srcs, lands = state
    n = len(ops)

    def body(*refs):
        for cp in _remote_copies(ops, gather, refs[:n], refs[n:2 * n], refs[2 * n], refs[2 * n + 1]):
            cp.wait_send()
            cp.wait_recv()

    outs = pl.pallas_call(
        body, name=name, out_shape=[pltpu.HBM(a.shape, a.dtype) for a in list(srcs) + list(lands)],
        in_specs=[HBM] * (2 * n) + [SEM, SEM, ANY], out_specs=[HBM] * (2 * n),
        input_output_aliases={i: i for i in range(2 * n)},
        compiler_params=pltpu.CompilerParams(has_side_effects=SIDE_EFFECT))(*srcs, *lands, send_sems, recv_sems, after)
    return outs[n:]


ADAM_ROWS = 256


def _row_tile(R, cap=ADAM_ROWS):
    best = R
    if R > cap:
        for d in range(16, cap + 1, 16):
            if R % d == 0:
                best = d
    return best


def _adamw(g_layers, w, m, v, *, name):
    L = len(g_layers)
    J, R, Wd = g_layers[0].shape
    assert w.shape == (L, R, Wd), (g_layers[0].shape, w.shape)
    tr = _row_tile(R)
    nrt = R // tr
    c1 = 1.0 / (1.0 - ADAM_B1 ** ADAM_STEP)
    c2 = 1.0 / (1.0 - ADAM_B2 ** ADAM_STEP)

    def body(*refs):
        g_refs = refs[:L]
        w_ref, m_ref, v_ref, go_ref, d_ref, mo_ref, vo_ref = refs[L:]
        layer = pl.program_id(0)
        g = None
        for l, g_ref in enumerate(g_refs):
            gl = g_ref[0].astype(F32)
            for j in range(1, J):
                gl = gl + g_ref[j].astype(F32)
            g = gl if g is None else jnp.where(layer == l, gl, g)
        mn = ADAM_B1 * m_ref[...] + (1.0 - ADAM_B1) * g
        vn = ADAM_B2 * v_ref[...] + (1.0 - ADAM_B2) * (g * g)
        go_ref[...] = g
        mo_ref[...] = mn
        vo_ref[...] = vn
        d_ref[...] = -ADAM_LR * ((mn * c1) / (jnp.sqrt(vn * c2) + ADAM_EPS) + ADAM_WD * w_ref[...])

    def g_spec(l):
        return pl.BlockSpec((J, tr, Wd), lambda ll, i: (0, jnp.where(ll == l, i, jnp.where(ll < l, 0, nrt - 1)), 0))

    row = pl.BlockSpec((None, tr, Wd), lambda ll, i: (ll, i, 0))
    out = _sds((L, R, Wd))
    return _call(body, name=name, out_shape=(out, out, out, out), grid=(L, nrt),
                 in_specs=[g_spec(l) for l in range(L)] + [row, row, row], out_specs=(row, row, row, row))(*g_layers, w, m, v)


def _sum_slabs(slabs, *, name):
    n = len(slabs)

    def body(*refs):
        for g_ref, o_ref in zip(refs[:n], refs[n:]):
            g = g_ref[0]
            for j in range(1, g_ref.shape[0]):
                g = g + g_ref[j]
            o_ref[...] = g

    return _call(body, name=name, out_shape=[_sds(s.shape[1:]) for s in slabs])(*slabs)


RELAYOUT_ROWS = 256


def _col_plan(n, segments):
    plan = []
    for j in range(N_DEV):
        lo, hi = j * n, (j + 1) * n
        for t0, t1, oi, o0 in segments:
            a, b = max(lo, t0), min(hi, t1)
            if a < b:
                plan.append((j, a - lo, oi, o0 + a - t0, b - a))
    return plan


def _interleave_segments(F):
    seg = []
    for b in range(F // LANES):
        seg.append((b * LANES, (b + 1) * LANES, 0, 2 * b * LANES))
        seg.append((F + b * LANES, F + (b + 1) * LANES, 0, (2 * b + 1) * LANES))
    return seg


def _interleave_perm(F):
    perm = []
    for b in range(F // LANES):
        perm += list(range(b * LANES, (b + 1) * LANES)) + list(range(F + b * LANES, F + (b + 1) * LANES))
    inv = [0] * (2 * F)
    for d, s in enumerate(perm):
        inv[s] = d
    return jnp.asarray(perm, jnp.int32), jnp.asarray(inv, jnp.int32)


def _cols_from_slabs(slabs, plan, widths, *, name):
    _, R, n = slabs.shape
    tr = _row_tile(R, RELAYOUT_ROWS)
    covered = [sum(e[4] for e in plan if e[2] == i) for i in range(len(widths))]

    def body(s_ref, *o_refs):
        for i, o_ref in enumerate(o_refs):
            if covered[i] < widths[i]:
                o_ref[...] = jnp.zeros_like(o_ref)
        for j, sc, oi, oc, w in plan:
            o_refs[oi][:, oc:oc + w] = s_ref[j, :, sc:sc + w]

    return _call(body, name=name, out_shape=[_sds((R, w), slabs.dtype) for w in widths], grid=(R // tr,),
                 in_specs=[pl.BlockSpec((N_DEV, tr, n), lambda i: (0, i, 0))],
                 out_specs=[pl.BlockSpec((tr, w), lambda i: (i, 0)) for w in widths])(slabs)


def _slabs_from_cols(mats, plan, n, *, name):
    R = mats[0].shape[0]
    tr = _row_tile(R, RELAYOUT_ROWS)

    def body(*refs):
        m_refs, o_ref = refs[:-1], refs[-1]
        for j, sc, oi, oc, w in plan:
            o_ref[j, :, sc:sc + w] = m_refs[oi][:, oc:oc + w]

    return _call(body, name=name, out_shape=_sds((N_DEV, R, n), mats[0].dtype), grid=(R // tr,),
                 in_specs=[pl.BlockSpec((tr, m.shape[1]), lambda i: (i, 0)) for m in mats],
                 out_specs=pl.BlockSpec((N_DEV, tr, n), lambda i: (0, i, 0)))(*mats)


def kernel(x, positions, norm_mix, norm_ffn, norm_final, mix_w_in, pool_w, pool_scale, attn_sinks, mix_w_out, ssm_w_in, ssm_conv_w, ssm_conv_b, ssm_dt_bias, ssm_A_log, ssm_D, ssm_norm, ssm_w_out, ffn_w_up, ffn_conv_w, ffn_conv_b, ffn_w_down, loss_target, m_norm_mix, m_norm_ffn, m_norm_final, m_mix_w_in, m_pool_w, m_pool_scale, m_attn_sinks, m_mix_w_out, m_ssm_w_in, m_ssm_conv_w, m_ssm_conv_b, m_ssm_dt_bias, m_ssm_A_log, m_ssm_D, m_ssm_norm, m_ssm_w_out, m_ffn_w_up, m_ffn_conv_w, m_ffn_conv_b, m_ffn_w_down, v_norm_mix, v_norm_ffn, v_norm_final, v_mix_w_in, v_pool_w, v_pool_scale, v_attn_sinks, v_mix_w_out, v_ssm_w_in, v_ssm_conv_w, v_ssm_conv_b, v_ssm_dt_bias, v_ssm_A_log, v_ssm_D, v_ssm_norm, v_ssm_w_out, v_ffn_w_up, v_ffn_conv_w, v_ffn_conv_b, v_ffn_w_down):
    args = dict(locals())
    wl = {n: args[n] for n in WEIGHTS}
    ml = {n: args["m_" + n] for n in WEIGHTS}
    vl = {n: args["v_" + n] for n in WEIGHTS}
    D = x.shape[2]
    F = ffn_w_down.shape[1] * N_DEV
    DI, CD, NH = ssm_norm.shape[1] * N_DEV, ssm_conv_b.shape[1] * N_DEV, ssm_dt_bias.shape[1]
    Kc, Kf = ssm_conv_w.shape[1], ffn_conv_w.shape[1]
    n_mix, n_ssm, n_up = mix_w_in.shape[2], ssm_w_in.shape[2], ffn_w_up.shape[2]
    plan_mix = _col_plan(n_mix, [(0, N_DEV * n_mix, 0, 0)])
    plan_ssm = _col_plan(n_ssm, [(0, DI, 0, 0), (DI, DI + CD, 1, 0), (DI + CD, DI + CD + NH, 2, 0)])
    plan_up = _col_plan(n_up, _interleave_segments(F))
    perm, inv = _interleave_perm(F)

    def two(a):
        return a.reshape(-1, a.shape[-1])

    def pay(a):
        return a.astype(PAYLOAD)

    order = ("mix", "ffn0", "ssm", "ffn1")
    gops = {
        "mix": [(pay(two(mix_w_in)), "slab"), (pay(mix_w_out[0]), "rows"), (two(ssm_conv_w), "slab"), (ssm_conv_b, "slab"),
                (ssm_norm, "slab"), (two(ffn_conv_w), "slab")],
        "ffn0": [(pay(ffn_w_up[0]), "slab"), (pay(ffn_w_down[0]), "rows")],
        "ssm": [(pay(two(ssm_w_in)), "slab"), (pay(ssm_w_out[0]), "rows")],
        "ffn1": [(pay(ffn_w_up[1]), "slab"), (pay(ffn_w_down[1]), "rows")],
    }
    lands = _place_own([op for g in order for op in gops[g]], gather=True, name="gather_own")
    groups, off = [], 0
    for g in order:
        groups.append((gops[g], lands[off:off + len(gops[g])]))
        off += len(gops[g])
    gstates, token = _exchange_start(groups, gather=True, name="gather_start")
    gstate = dict(zip(order, gstates))
    P = {n: wl[n] for n in REPLICATED}
    P["ffn_conv_b"] = jnp.take(ffn_conv_b, perm, axis=1)

    def need(g, after):
        got = _exchange_wait(gops[g], gstate[g], after, gather=True, name="gather_wait_" + g)
        if g == "mix":
            (P["mix_w_in"],) = _cols_from_slabs(got[0], plan_mix, (N_DEV * n_mix,), name="unpack_mix_w_in")
            P.update(mix_w_out=got[1], ssm_conv_w=got[2].transpose(1, 0, 2).reshape(Kc, CD), ssm_conv_b=got[3].reshape(1, CD),
                     ssm_norm=got[4].reshape(1, DI),
                     ffn_conv_w=jnp.take(got[5].transpose(1, 0, 2).reshape(2 * Kf, 2 * F), perm, axis=1).reshape(2, Kf, 2 * F))
        elif g == "ssm":
            P["ssm_wz"], P["ssm_wxbc"], P["ssm_wdt"] = _cols_from_slabs(got[0], plan_ssm, (DI, CD, LANES), name="unpack_ssm_w_in")
            P["ssm_w_out"] = got[1]
        else:
            (P["ffn_w_up" + g[3]],) = _cols_from_slabs(got[0], plan_up, (2 * F,), name="unpack_ffn_w_up" + g[3])
            P["ffn_w_down" + g[3]] = got[1]

    sent = {}

    def emit(g, d):
        if g == "mix":
            ops = [(_slabs_from_cols([d["mix_w_in"]], plan_mix, n_mix, name="pack_mix_w_in"), "slab"), (d["mix_w_out"], "rows"),
                   (jnp.take(d["ffn_conv_w"].reshape(2 * Kf, 2 * F), inv, axis=1).reshape(2 * Kf, N_DEV, n_up).transpose(1, 0, 2), "slab")]
        elif g == "ssm":
            ops = [(_slabs_from_cols([d["ssm_wz"], d["ssm_wxbc"], d["ssm_wdt"]], plan_ssm, n_ssm, name="pack_ssm_w_in"), "slab"),
                   (d["ssm_w_out"], "rows"), (d["ssm_conv_w"].reshape(Kc, N_DEV, -1).transpose(1, 0, 2), "slab"),
                   (d["ssm_conv_b"].reshape(N_DEV, 1, -1), "slab"), (d["ssm_norm"].reshape(N_DEV, 1, -1), "slab")]
        else:
            ops = [(_slabs_from_cols([d["ffn_w_up"]], plan_up, n_up, name="pack_ffn_w_up" + g[3]), "slab"), (d["ffn_w_down"], "rows")]
        own = _place_own(ops, gather=False, name="scatter_own_" + g)
        (state,), tok = _exchange_start([(ops, own)], gather=False, name="scatter_start_" + g)
        sent[g] = (ops, state)
        return tok

    need("mix", token)
    loss_lanes, grad_x, G = _local_step(x[0], positions.reshape(-1, 1), loss_target[0], P, need, emit, after=token)
    loss = lax.psum(loss_lanes[0, 0], ("x", "y", "c"))

    res = {}

    def update(n, g_layers):
        L = len(g_layers)
        g_layers = [g.reshape(g.shape[0], -1, g.shape[-1]) for g in g_layers]
        shape = (L,) + g_layers[0].shape[1:]
        outs = _adamw(g_layers, wl[n].reshape(shape), ml[n].reshape(shape), vl[n].reshape(shape), name="adamw_" + n)
        for kind, a in zip(("grad", "delta", "new_m", "new_v"), outs):
            res[kind, n] = a.reshape(wl[n].shape)

    recv = {g: _exchange_wait(sent[g][0], sent[g][1], grad_x, gather=False, name="scatter_wait_" + g)
            for g in ("ffn1", "ssm", "ffn0", "mix")}
    update("mix_w_in", [recv["mix"][0]])
    update("mix_w_out", [recv["mix"][1]])
    update("ffn_conv_w", [recv["mix"][2]])
    update("ssm_w_in", [recv["ssm"][0]])
    update("ssm_w_out", [recv["ssm"][1]])
    update("ssm_conv_w", [recv["ssm"][2]])
    update("ssm_conv_b", [recv["ssm"][3]])
    update("ssm_norm", [recv["ssm"][4]])
    update("ffn_w_up", [recv["ffn0"][0], recv["ffn1"][0]])
    update("ffn_w_down", [recv["ffn0"][1], recv["ffn1"][1]])

    rep = _exchange(
        [(a, "slab") for a in (G["norm_mix"], G["norm_ffn"], G["norm_final"], G["pool_w"].reshape(-1, LANES), G["pool_scale"],
                               jnp.take(G["ffn_conv_b"], inv, axis=1), G["attn_sinks_rows"],
                               G["ssm_dt_bias_g"].reshape(SSM_G, LANES), G["ssm_A_log_g"].reshape(SSM_G, LANES),
                               G["ssm_D_g"].reshape(SSM_G, LANES))],
        gather=True, name="gather_small_grads")
    for n, r in zip(("norm_mix", "norm_ffn", "norm_final", "pool_w", "pool_scale", "ffn_conv_b"), rep):
        update(n, [r])
    sinks_rows, bias_g, alog_g, d_g = _sum_slabs(rep[6:], name="sum_head_grads")
    update("attn_sinks", [sinks_rows[:, 0].reshape(1, 1, N_HEADS)])
    update("ssm_dt_bias", [_ungroup(bias_g)[None]])
    update("ssm_A_log", [_ungroup(alog_g)[None]])
    update("ssm_D", [_ungroup(d_g)[None]])

    return (loss, grad_x[None], *[res[k, n] for k in ("grad", "delta", "new_m", "new_v") for n in WEIGHTS])
```

```python
import functools
import math

import jax
import jax.numpy as jnp
from jax import lax
from jax.experimental import pallas as pl
from jax.experimental.pallas import tpu as pltpu

F32 = jnp.float32
BF16 = jnp.bfloat16

N_DEV = 8
LANES = 128
HEAD_DIM = 64
N_KV_HEADS = 2
GQ = 4
N_HEADS = N_KV_HEADS * GQ
BLOCK = 128
POOL_GROUPS = 4
ROPE_THETA = 10000.0
SSM_P = 64
SSM_G = 8
SSM_R = 4
SSM_N = 128
SSM_L = 128
NORM_EPS = 1e-6
SSM_NORM_EPS = 1e-5
ADAM_LR, ADAM_B1, ADAM_B2, ADAM_EPS, ADAM_WD, ADAM_STEP = 0.001, 0.9, 0.999, 1e-08, 0.01, 10
VMEM_LIMIT = 56 * 2 ** 20
PAYLOAD = jnp.bfloat16

REPLICATED = ("norm_mix", "norm_ffn", "norm_final", "pool_w", "pool_scale", "attn_sinks",
              "ssm_dt_bias", "ssm_A_log", "ssm_D", "ffn_conv_b")
WEIGHTS = ("norm_mix", "norm_ffn", "norm_final", "mix_w_in", "pool_w", "pool_scale", "attn_sinks", "mix_w_out",
           "ssm_w_in", "ssm_conv_w", "ssm_conv_b", "ssm_dt_bias", "ssm_A_log", "ssm_D", "ssm_norm", "ssm_w_out",
           "ffn_w_up", "ffn_conv_w", "ffn_conv_b", "ffn_w_down")


def _tile(n, cap):
    if n <= cap:
        return n
    best = None
    for d in range(LANES, cap + 1, LANES):
        if n % d == 0:
            best = d
    assert best is not None, (n, cap)
    return best


def _call(body, *, name, out_shape, grid=None, in_specs=None, out_specs=None, scratch=(), aliases=None):
    kw = {}
    if grid is not None:
        kw = dict(grid=grid, in_specs=in_specs, out_specs=out_specs)
    if aliases:
        kw["input_output_aliases"] = aliases
    return pl.pallas_call(
        body, name=name, out_shape=out_shape, scratch_shapes=list(scratch),
        compiler_params=pltpu.CompilerParams(vmem_limit_bytes=VMEM_LIMIT), **kw)


ANY = pl.BlockSpec(memory_space=pl.ANY)


def _sds(shape, dtype=F32):
    return jax.ShapeDtypeStruct(tuple(shape), dtype)


def _sigmoid(x):
    return 1.0 / (1.0 + jnp.exp(-x))


def _shift_dn(x, d, t):
    if d == 0:
        return x
    return jnp.where(t >= d, pltpu.roll(x, d, axis=0), 0.0)


def _shift_up(x, d, t):
    if d == 0:
        return x
    n = x.shape[0]
    return jnp.where(t < n - d, pltpu.roll(x, n - d, axis=0), 0.0)


def _mm(a, b, *, name, ta=False, tb=False, res=None, out_dtype=F32):
    M, K = (a.shape[1], a.shape[0]) if ta else a.shape
    N = b.shape[0] if tb else b.shape[1]
    assert (b.shape[1] if tb else b.shape[0]) == K, (a.shape, b.shape, ta, tb)
    tm, tn, tk = _tile(M, 1024), _tile(N, 1408), _tile(K, 1408)
    nk = K // tk
    dims = (((0 if ta else 1,), (1 if tb else 0,)), ((), ()))

    def body(*refs):
        a_ref, b_ref = refs[:2]
        r_ref = refs[2] if res is not None else None
        o_ref, acc = refs[-2:]
        k = pl.program_id(2)

        @pl.when(k == 0)
        def _():
            acc[...] = jnp.zeros_like(acc)

        acc[...] += lax.dot_general(a_ref[...].astype(BF16), b_ref[...].astype(BF16), dims,
                                    preferred_element_type=F32)

        @pl.when(k == nk - 1)
        def _():
            out = acc[...]
            if res is not None:
                out = out + r_ref[...]
            o_ref[...] = out.astype(out_dtype)

    a_spec = pl.BlockSpec((tk, tm), lambda i, j, k: (k, i)) if ta else pl.BlockSpec((tm, tk), lambda i, j, k: (i, k))
    b_spec = pl.BlockSpec((tn, tk), lambda i, j, k: (j, k)) if tb else pl.BlockSpec((tk, tn), lambda i, j, k: (k, j))
    o_spec = pl.BlockSpec((tm, tn), lambda i, j, k: (i, j))
    ins, specs = [a, b], [a_spec, b_spec]
    if res is not None:
        ins.append(res)
        specs.append(o_spec)
    return _call(body, name=name, out_shape=_sds((M, N), out_dtype), grid=(M // tm, N // tn, nk), in_specs=specs,
                 out_specs=o_spec, scratch=[pltpu.VMEM((tm, tn), F32)])(*ins)


def _rmsnorm(x, w, *, name, eps=NORM_EPS, after=None):
    S, D = x.shape
    tm = _tile(S, 512)
    tie = [] if after is None else [after]

    def body(x_ref, w_ref, *rest):
        o_ref = rest[-1]
        xf = x_ref[...]
        r = lax.rsqrt(jnp.mean(xf * xf, axis=-1, keepdims=True) + eps)
        o_ref[...] = (xf * r * w_ref[...]).astype(BF16)

    return _call(body, name=name, out_shape=_sds((S, D), BF16), grid=(S // tm,),
                 in_specs=[pl.BlockSpec((tm, D), lambda i: (i, 0)), pl.BlockSpec((1, D), lambda i: (0, 0))] + [ANY] * len(tie),
                 out_specs=pl.BlockSpec((tm, D), lambda i: (i, 0)))(x, w, *tie)


def _norm_bwd_math(xf, w, dh, eps):
    r = lax.rsqrt(jnp.mean(xf * xf, axis=-1, keepdims=True) + eps)
    xhat = xf * r
    dxh = dh * w
    dx = r * (dxh - xhat * jnp.mean(dxh * xhat, axis=-1, keepdims=True))
    dw = jnp.sum(dh * xhat, axis=0, keepdims=True)
    return dx, dw


def _rmsnorm_bwd(x, w, dh, dres, *, name, eps=NORM_EPS, after=None):
    S, D = x.shape
    tm = _tile(S, 512)
    tie = [] if after is None else [after]

    def body(x_ref, w_ref, dh_ref, dr_ref, *rest):
        dx_ref, dw_ref = rest[-2:]
        dx, dw = _norm_bwd_math(x_ref[...], w_ref[...], dh_ref[...].astype(F32), eps)
        dx_ref[...] = dr_ref[...] + dx

        @pl.when(pl.program_id(0) == 0)
        def _():
            dw_ref[...] = jnp.zeros_like(dw_ref)

        dw_ref[...] += dw

    row = pl.BlockSpec((tm, D), lambda i: (i, 0))
    vec = pl.BlockSpec((1, D), lambda i: (0, 0))
    return _call(body, name=name, out_shape=(_sds((S, D)), _sds((1, D))), grid=(S // tm,),
                 in_specs=[row, vec, row, row] + [ANY] * len(tie), out_specs=(row, vec))(x, w, dh, dres, *tie)


def _final_loss(x, w, target, *, name):
    S, D = x.shape
    tm = _tile(S, 512)

    def body(x_ref, w_ref, t_ref, loss_ref, dx_ref, dw_ref):
        xf, wv = x_ref[...], w_ref[...]
        r = lax.rsqrt(jnp.mean(xf * xf, axis=-1, keepdims=True) + NORM_EPS)
        err = xf * r * wv - t_ref[...]
        part = 0.5 * jnp.sum(jnp.mean(err * err, axis=-1, keepdims=True), axis=0, keepdims=True)
        dx, dw = _norm_bwd_math(xf, wv, err * (1.0 / D), NORM_EPS)
        dx_ref[...] = dx

        @pl.when(pl.program_id(0) == 0)
        def _():
            dw_ref[...] = jnp.zeros_like(dw_ref)
            loss_ref[...] = jnp.zeros_like(loss_ref)

        dw_ref[...] += dw
        loss_ref[...] += jnp.broadcast_to(part, loss_ref.shape)

    row = pl.BlockSpec((tm, D), lambda i: (i, 0))
    vec = pl.BlockSpec((1, D), lambda i: (0, 0))
    return _call(body, name=name, out_shape=(_sds((1, LANES)), _sds((S, D)), _sds((1, D))), grid=(S // tm,),
                 in_specs=[row, vec, row], out_specs=(pl.BlockSpec((1, LANES), lambda i: (0, 0)), row, vec))(x, w, target)


def _rope_tables(pos, inv_freq):
    S = pos.shape[0]
    tm = _tile(S, 512)

    def body(p_ref, f_ref, c_ref, s_ref):
        ang = p_ref[...].astype(F32) * f_ref[...]
        c_ref[...] = jnp.cos(ang)
        s_ref[...] = jnp.sin(ang)

    blk = pl.BlockSpec((tm, LANES), lambda i: (i, 0))
    return _call(body, name="rope_tables", out_shape=(_sds((S, LANES)), _sds((S, LANES))), grid=(S // tm,),
                 in_specs=[pl.BlockSpec((tm, 1), lambda i: (i, 0)), pl.BlockSpec((1, LANES), lambda i: (0, 0))],
                 out_specs=(blk, blk))(pos, inv_freq)


def _rot_half(t):
    lane = lax.broadcasted_iota(jnp.int32, t.shape, 1)
    lo = (lane % HEAD_DIM) < (HEAD_DIM // 2)
    return jnp.where(lo, -pltpu.roll(t, LANES - HEAD_DIM // 2, axis=1), pltpu.roll(t, HEAD_DIM // 2, axis=1))


def _rope(t, c, s):
    return t * c + _rot_half(t) * s


def _unrope(dy, c, s):
    return dy * c - _rot_half(dy * s)


PD = POOL_GROUPS * LANES
QD = N_HEADS * HEAD_DIM
KD = N_KV_HEADS * HEAD_DIM
assert PD % QD == 0 and (PD + QD) % (2 * KD) == 0 and KD == LANES
def _attn_probs(q, kcat, sink, mask):
    s = lax.dot_general(q.astype(BF16), kcat, (((1,), (1,)), ((), ())), preferred_element_type=F32) * (HEAD_DIM ** -0.5)
    s = jnp.where(mask, s, -jnp.inf)
    m = jnp.maximum(jnp.max(s, axis=1, keepdims=True), sink)
    p = jnp.exp(s - m)
    ps = jnp.exp(sink - m)
    inv = 1.0 / (jnp.sum(p, axis=1, keepdims=True) + ps)
    return p * inv, ps * inv


def _attn_mask(n):
    qi = lax.broadcasted_iota(jnp.int32, (BLOCK, 2 * BLOCK), 0)
    kj = lax.broadcasted_iota(jnp.int32, (BLOCK, 2 * BLOCK), 1)
    rel = qi + BLOCK - kj
    return (rel >= 0) & (rel < BLOCK) & ((n > 0) | (kj >= BLOCK))


def _attn_in_specs(nb):
    def cur(n):
        return jnp.minimum(n, nb - 1)

    def prev(n):
        return jnp.clip(n - 1, 0, nb - 1)

    kvb = (PD + QD) // (2 * KD)
    return [pl.BlockSpec(memory_space=pltpu.SMEM),
            pl.BlockSpec((BLOCK, QD), lambda n: (cur(n), PD // QD)),
            pl.BlockSpec((BLOCK, 2 * KD), lambda n: (cur(n), kvb)),
            pl.BlockSpec((BLOCK, 2 * KD), lambda n: (prev(n), kvb)),
            pl.BlockSpec((BLOCK, LANES), lambda n: (cur(n), 0)), pl.BlockSpec((BLOCK, LANES), lambda n: (cur(n), 0)),
            pl.BlockSpec((BLOCK, LANES), lambda n: (prev(n), 0)), pl.BlockSpec((BLOCK, LANES), lambda n: (prev(n), 0))]


def _attn_keys(kvc_ref, kvp_ref, cc, sc, cp, sp):
    kc = _rope(kvc_ref[:, :KD], cc, sc)
    kp = _rope(kvp_ref[:, :KD], cp, sp)
    vc, vp = kvc_ref[:, KD:], kvp_ref[:, KD:]
    kcat, vcat = [], []
    for kk in range(N_KV_HEADS):
        sl = slice(kk * HEAD_DIM, (kk + 1) * HEAD_DIM)
        kcat.append(jnp.concatenate([kp[:, sl], kc[:, sl]], axis=0).astype(BF16))
        vcat.append(jnp.concatenate([vp[:, sl], vc[:, sl]], axis=0).astype(BF16))
    return kcat, vcat


def _attn_fwd(proj, cos, sin, sinks, cat):
    S = proj.shape[0]
    nb = S // BLOCK

    def body(sink_ref, q_ref, kvc_ref, kvp_ref, cc_ref, sc_ref, cp_ref, sp_ref, cat_ref, o_ref):
        n = pl.program_id(0)
        cc, sc = cc_ref[...], sc_ref[...]
        kcat, vcat = _attn_keys(kvc_ref, kvp_ref, cc, sc, cp_ref[...], sp_ref[...])
        mask = _attn_mask(n)
        for j in range(QD // LANES):
            qr = _rope(q_ref[:, j * LANES:(j + 1) * LANES], cc, sc)
            for e in range(LANES // HEAD_DIM):
                h = j * (LANES // HEAD_DIM) + e
                pn, _ = _attn_probs(qr[:, e * HEAD_DIM:(e + 1) * HEAD_DIM], kcat[h // GQ], sink_ref[0, h], mask)
                o_ref[:, h * HEAD_DIM:(h + 1) * HEAD_DIM] = jnp.dot(pn.astype(BF16), vcat[h // GQ], preferred_element_type=F32)

    return _call(body, name="attn_fwd", out_shape=_sds(cat.shape), grid=(nb,),
                 in_specs=_attn_in_specs(nb) + [ANY], out_specs=pl.BlockSpec((BLOCK, QD), lambda n: (n, PD // QD)),
                 aliases={8: 0})(sinks, proj, proj, proj, cos, sin, cos, sin, cat)


def _attn_bwd(proj, cos, sin, sinks, dcat):
    S = proj.shape[0]
    nb = S // BLOCK
    scale = HEAD_DIM ** -0.5
    per = LANES // HEAD_DIM

    def body(sink_ref, q_ref, kvc_ref, kvp_ref, cc_ref, sc_ref, cp_ref, sp_ref, do_ref, o_ref, ds_ref, hold, carry, part, pair):
        n = pl.program_id(0)

        @pl.when(n == 0)
        def _():
            hold[...] = jnp.zeros_like(hold)
            carry[...] = jnp.zeros_like(carry)
            ds_ref[...] = jnp.zeros_like(ds_ref)

        live = jnp.where(n < nb, 1.0, 0.0)
        cc, sc, cp, sp = cc_ref[...], sc_ref[...], cp_ref[...], sp_ref[...]
        kcat, vcat = _attn_keys(kvc_ref, kvp_ref, cc, sc, cp, sp)
        mask = _attn_mask(n)
        o_ref[:, :PD] = jnp.zeros((BLOCK, PD), F32)
        o_ref[:, PD:PD + QD] = hold[...]
        dk = [jnp.zeros((2 * BLOCK, HEAD_DIM), F32) for _ in range(N_KV_HEADS)]
        dv = [jnp.zeros((2 * BLOCK, HEAD_DIM), F32) for _ in range(N_KV_HEADS)]
        row = lax.broadcasted_iota(jnp.int32, (8, LANES), 0)
        dsk = jnp.zeros((8, LANES), F32)
        for j in range(QD // LANES):
            qr = _rope(q_ref[:, j * LANES:(j + 1) * LANES], cc, sc)
            for e in range(per):
                h = j * per + e
                kk = h // GQ
                qh = qr[:, e * HEAD_DIM:(e + 1) * HEAD_DIM]
                pn, psn = _attn_probs(qh, kcat[kk], sink_ref[0, h], mask)
                doh = (do_ref[:, h * HEAD_DIM:(h + 1) * HEAD_DIM] * live).astype(BF16)
                dp = lax.dot_general(doh, vcat[kk], NT, preferred_element_type=F32)
                delta = jnp.sum(pn * dp, axis=1, keepdims=True)
                ds = (pn * (dp - delta) * scale).astype(BF16)
                pair[:, e * HEAD_DIM:(e + 1) * HEAD_DIM] = jnp.dot(ds, kcat[kk], preferred_element_type=F32)
                dk[kk] = dk[kk] + lax.dot_general(ds, qh.astype(BF16), TN, preferred_element_type=F32)
                dv[kk] = dv[kk] + lax.dot_general(pn.astype(BF16), doh, TN, preferred_element_type=F32)
                dsk = dsk + jnp.where(row == h, -jnp.sum(psn * delta), 0.0)
            hold[:, j * LANES:(j + 1) * LANES] = _unrope(pair[...], cc, sc)
        for kk in range(N_KV_HEADS):
            sl = slice(kk * HEAD_DIM, (kk + 1) * HEAD_DIM)
            sv = slice(KD + kk * HEAD_DIM, KD + (kk + 1) * HEAD_DIM)
            part[0, :, sl] = dk[kk][:BLOCK]
            part[0, :, sv] = dv[kk][:BLOCK]
            part[1, :, sl] = dk[kk][BLOCK:]
            part[1, :, sv] = dv[kk][BLOCK:]
        done = carry[...] + part[0]
        o_ref[:, PD + QD:PD + QD + KD] = _unrope(done[:, :KD], cp, sp)
        o_ref[:, PD + QD + KD:] = done[:, KD:]
        carry[...] = part[1]
        ds_ref[...] += dsk

    return _call(body, name="attn_bwd", out_shape=(_sds((S, PD + QD + 2 * KD)), _sds((8, LANES))), grid=(nb + 1,),
                 in_specs=_attn_in_specs(nb) + [pl.BlockSpec((BLOCK, QD), lambda n: (jnp.minimum(n, nb - 1), PD // QD))],
                 out_specs=(pl.BlockSpec((BLOCK, PD + QD + 2 * KD), lambda n: (jnp.maximum(n - 1, 0), 0)),
                            pl.BlockSpec((8, LANES), lambda n: (0, 0))),
                 scratch=[pltpu.VMEM((BLOCK, QD), F32), pltpu.VMEM((BLOCK, 2 * KD), F32),
                          pltpu.VMEM((2, BLOCK, 2 * KD), F32), pltpu.VMEM((BLOCK, LANES), F32)])(
                     sinks, proj, proj, proj, cos, sin, cos, sin, dcat)


def _pool_sums(u, g, t, shift):
    s2 = u + shift(u, 1, t)
    s4 = s2 + shift(s2, 2, t)
    s8 = s4 + shift(s4, 4, t)
    s16 = s8 + shift(s8, 8, t)
    return jnp.where(g == 0, s2, jnp.where(g == 1, s4, jnp.where(g == 2, s8, s16)))


def _pool_specs(S):
    col = pl.BlockSpec((S, LANES), lambda g: (0, g))
    wsp = pl.BlockSpec((1, LANES, LANES), lambda g: (g, 0, 0))
    vec = pl.BlockSpec((1, LANES), lambda g: (0, g))
    return col, wsp, vec


def _pool_fwd(proj, pool_w, scale):
    S = proj.shape[0]
    col, wsp, vec = _pool_specs(S)

    def body(u_ref, w_ref, s_ref, o_ref):
        g = pl.program_id(0)
        u = u_ref[...]
        t = lax.broadcasted_iota(jnp.int32, u.shape, 0)
        cnt = jnp.minimum(t + 1, 2 << g).astype(F32)
        pm = _pool_sums(u, g, t, _shift_dn) / cnt - u
        o_ref[...] = jnp.dot(pm.astype(BF16), w_ref[0].astype(BF16), preferred_element_type=F32) * s_ref[...]

    return _call(body, name="pool_fwd", out_shape=_sds((S, PD + QD)), grid=(POOL_GROUPS,),
                 in_specs=[col, wsp, vec], out_specs=col)(proj, pool_w, scale)


def _pool_bwd(proj, pool_w, scale, dcat, dproj):
    S = proj.shape[0]
    col, wsp, vec = _pool_specs(S)

    def body(u_ref, w_ref, s_ref, d_ref, dproj_ref, du_ref, dw_ref, dsc_ref):
        g = pl.program_id(0)
        u = u_ref[...]
        t = lax.broadcasted_iota(jnp.int32, u.shape, 0)
        cnt = jnp.minimum(t + 1, 2 << g).astype(F32)
        pm = (_pool_sums(u, g, t, _shift_dn) / cnt - u).astype(BF16)
        wv = w_ref[0].astype(BF16)
        d = d_ref[...]
        pw = jnp.dot(pm, wv, preferred_element_type=F32)
        dsc_ref[...] = jnp.sum(pw * d, axis=0, keepdims=True)
        dpw = (d * s_ref[...]).astype(BF16)
        dw_ref[0] = lax.dot_general(pm, dpw, (((0,), (0,)), ((), ())), preferred_element_type=F32)
        dpm = lax.dot_general(dpw, wv, (((1,), (1,)), ((), ())), preferred_element_type=F32)
        du_ref[...] = _pool_sums(dpm / cnt, g, t, _shift_up) - dpm

    return _call(body, name="pool_bwd",
                 out_shape=(_sds(dproj.shape), _sds((POOL_GROUPS, LANES, LANES)), _sds((1, POOL_GROUPS * LANES))),
                 grid=(POOL_GROUPS,), in_specs=[col, wsp, vec, col, ANY], out_specs=(col, wsp, vec),
                 aliases={4: 0})(proj, pool_w, scale, dcat, dproj)


def _conv(x, w_ref, b_ref, t):
    K = w_ref.shape[0]
    y = b_ref[...] + jnp.zeros_like(x)
    for k in range(K):
        y = y + w_ref[k:k + 1, :] * _shift_dn(x, K - 1 - k, t)
    return y


def _conv_bwd(x, dy, w_ref, t):
    K = w_ref.shape[0]
    dx = jnp.zeros_like(x)
    dws = []
    for k in range(K):
        dx = dx + w_ref[k:k + 1, :] * _shift_up(dy, K - 1 - k, t)
        dws.append(jnp.sum(dy * _shift_dn(x, K - 1 - k, t), axis=0, keepdims=True))
    return dx, dws, jnp.sum(dy, axis=0, keepdims=True)


def _silu_grad(y):
    sg = _sigmoid(y)
    return sg * (1.0 + y * (1.0 - sg))


def _ffn_mid_specs(S, K, layer):
    return [pl.BlockSpec((S, LANES), lambda j: (0, 2 * j)), pl.BlockSpec((S, LANES), lambda j: (0, 2 * j + 1)),
            pl.BlockSpec((None, K, LANES), lambda j: (layer, 0, 2 * j)), pl.BlockSpec((None, K, LANES), lambda j: (layer, 0, 2 * j + 1)),
            pl.BlockSpec((None, 1, LANES), lambda j: (layer, 0, 2 * j)), pl.BlockSpec((None, 1, LANES), lambda j: (layer, 0, 2 * j + 1))]


def _ffn_mid_fwd(a, cw, cb, layer):
    S, F2 = a.shape
    nf = F2 // (2 * LANES)
    K = cw.shape[1]

    def body(au_ref, ag_ref, wu_ref, wg_ref, bu_ref, bg_ref, o_ref):
        t = lax.broadcasted_iota(jnp.int32, (S, LANES), 0)
        hu = _conv(au_ref[...], wu_ref, bu_ref, t)
        hg = _conv(ag_ref[...], wg_ref, bg_ref, t)
        o_ref[...] = (hg * _sigmoid(hg) * hu).astype(BF16)

    return _call(body, name="ffn_mid_fwd", out_shape=_sds((S, F2 // 2), BF16), grid=(nf,),
                 in_specs=_ffn_mid_specs(S, K, layer), out_specs=pl.BlockSpec((S, LANES), lambda j: (0, j)))(
                     a, a, cw, cw, cb[:, None], cb[:, None])


def _ffn_mid_bwd(a, cw, cb, layer, dact):
    S, F2 = a.shape
    nf = F2 // (2 * LANES)
    K = cw.shape[1]

    def body(au_ref, ag_ref, wu_ref, wg_ref, bu_ref, bg_ref, d_ref, da_ref, dw_ref, db_ref):
        t = lax.broadcasted_iota(jnp.int32, (S, LANES), 0)
        au, ag = au_ref[...], ag_ref[...]
        hu = _conv(au, wu_ref, bu_ref, t)
        hg = _conv(ag, wg_ref, bg_ref, t)
        d = d_ref[...].astype(F32)
        dhu = d * hg * _sigmoid(hg)
        dhg = d * hu * _silu_grad(hg)
        dau, dwu, dbu = _conv_bwd(au, dhu, wu_ref, t)
        dag, dwg, dbg = _conv_bwd(ag, dhg, wg_ref, t)
        da_ref[:, :LANES] = dau.astype(BF16)
        da_ref[:, LANES:] = dag.astype(BF16)
        for k in range(K):
            dw_ref[k:k + 1, :LANES] = dwu[k]
            dw_ref[k:k + 1, LANES:] = dwg[k]
        db_ref[:, :LANES] = dbu
        db_ref[:, LANES:] = dbg

    return _call(body, name="ffn_mid_bwd", out_shape=(_sds((S, F2), BF16), _sds((K, F2)), _sds((1, F2))), grid=(nf,),
                 in_specs=_ffn_mid_specs(S, K, layer) + [pl.BlockSpec((S, LANES), lambda j: (0, j))],
                 out_specs=(pl.BlockSpec((S, 2 * LANES), lambda j: (0, j)), pl.BlockSpec((K, 2 * LANES), lambda j: (0, j)),
                            pl.BlockSpec((1, 2 * LANES), lambda j: (0, j))))(a, a, cw, cw, cb[:, None], cb[:, None], dact)


def _conv_silu_fwd(x, cw, cb):
    S = x.shape[0]
    K, C = cw.shape

    def body(x_ref, w_ref, b_ref, o_ref):
        t = lax.broadcasted_iota(jnp.int32, (S, LANES), 0)
        y = _conv(x_ref[...], w_ref, b_ref, t)
        o_ref[...] = y * _sigmoid(y)

    col = pl.BlockSpec((S, LANES), lambda j: (0, j))
    return _call(body, name="conv_silu_fwd", out_shape=_sds((S, C)), grid=(C // LANES,),
                 in_specs=[col, pl.BlockSpec((K, LANES), lambda j: (0, j)), pl.BlockSpec((1, LANES), lambda j: (0, j))],
                 out_specs=col)(x, cw, cb)


def _conv_silu_bwd(x, cw, cb, douts):
    S = x.shape[0]
    K, C = cw.shape
    starts, off = [], 0
    for d in douts:
        starts.append(off)
        off += d.shape[1] // LANES
    assert off == C // LANES

    def body(x_ref, w_ref, b_ref, *rest):
        d_refs, (dx_ref, dw_ref, db_ref) = rest[:len(douts)], rest[len(douts):]
        j = pl.program_id(0)
        t = lax.broadcasted_iota(jnp.int32, (S, LANES), 0)
        x = x_ref[...]
        y = _conv(x, w_ref, b_ref, t)
        d = d_refs[0][...]
        for i in range(1, len(douts)):
            d = jnp.where(j >= starts[i], d_refs[i][...], d)
        dy = d * _silu_grad(y)
        dx, dw, db = _conv_bwd(x, dy, w_ref, t)
        dx_ref[...] = dx.astype(BF16)
        for k in range(K):
            dw_ref[k:k + 1, :] = dw[k]
        db_ref[...] = db

    col = pl.BlockSpec((S, LANES), lambda j: (0, j))
    wsp = pl.BlockSpec((K, LANES), lambda j: (0, j))
    bsp = pl.BlockSpec((1, LANES), lambda j: (0, j))

    def dspec(i):
        nblk = douts[i].shape[1] // LANES
        return pl.BlockSpec((S, LANES), lambda j: (0, jnp.clip(j - starts[i], 0, nblk - 1)))

    return _call(body, name="conv_silu_bwd", out_shape=(_sds((S, C), BF16), _sds((K, C)), _sds((1, C))), grid=(C // LANES,),
                 in_specs=[col, wsp, bsp] + [dspec(i) for i in range(len(douts))],
                 out_specs=(col, wsp, bsp))(x, cw, cb, *douts)


HI = lax.Precision.HIGHEST


def _ssd_prep_fwd(proj, col0, bias_g, alog_g):
    S = proj.shape[0]
    nc = S // SSM_L
    b0 = col0 // LANES

    def body(raw_ref, b_ref, al_ref, pre_ref, dt_ref, acs_ref, acst_ref):
        r_i = lax.broadcasted_iota(jnp.int32, (LANES, LANES), 0)
        c_i = lax.broadcasted_iota(jnp.int32, (LANES, LANES), 1)
        live = c_i < SSM_R
        tril = jnp.where(r_i >= c_i, 1.0, 0.0)
        raw = raw_ref[...]
        for g in range(SSM_G):
            sel = jnp.where((r_i == SSM_R * g + c_i) & live, 1.0, 0.0)
            pre = jnp.dot(raw, sel, preferred_element_type=F32, precision=HI) + b_ref[g]
            dt = jnp.where(live, jnp.logaddexp(pre, 0.0), 0.0)
            a = dt * (-jnp.exp(al_ref[g]))
            acs = jnp.dot(tril, a, preferred_element_type=F32, precision=HI)
            pre_ref[g] = pre
            dt_ref[g] = dt
            acs_ref[g] = acs
            acst_ref[g] = acs.T

    gsp = pl.BlockSpec((SSM_G, 1, LANES), lambda c: (0, 0, 0))
    blk = pl.BlockSpec((SSM_G, SSM_L, LANES), lambda c: (0, c, 0))
    big = _sds((SSM_G, S, LANES))
    return _call(body, name="ssd_prep_fwd", out_shape=(big, big, big, _sds((SSM_G, LANES, S))), grid=(nc,),
                 in_specs=[pl.BlockSpec((SSM_L, LANES), lambda c: (c, b0)), gsp, gsp],
                 out_specs=(blk, blk, blk, pl.BlockSpec((SSM_G, LANES, SSM_L), lambda c: (0, 0, c))))(proj, bias_g, alog_g)


def _ssd_prep_bwd(pre_g, dt_g, alog_g, ddt_g, dacs_g, dacst_g):
    S = pre_g.shape[1]
    nc = S // SSM_L

    def body(pre_ref, dt_ref, al_ref, ddt_ref, dacs_ref, dacst_ref, draw_ref, db_ref, dal_ref):
        c = pl.program_id(0)
        r_i = lax.broadcasted_iota(jnp.int32, (LANES, LANES), 0)
        c_i = lax.broadcasted_iota(jnp.int32, (LANES, LANES), 1)
        live = c_i < SSM_R
        triu = jnp.where(r_i <= c_i, 1.0, 0.0)

        @pl.when(c == 0)
        def _():
            db_ref[...] = jnp.zeros_like(db_ref)
            dal_ref[...] = jnp.zeros_like(dal_ref)

        draw = jnp.zeros((SSM_L, LANES), F32)
        for g in range(SSM_G):
            dacs = dacs_ref[g] + dacst_ref[g].T
            da = jnp.dot(triu, dacs, preferred_element_type=F32, precision=HI)
            A = -jnp.exp(al_ref[g])
            ddt = ddt_ref[g] + da * A
            dpre = jnp.where(live, ddt * _sigmoid(pre_ref[g]), 0.0)
            unsel = jnp.where((c_i == SSM_R * g + r_i) & (r_i < SSM_R), 1.0, 0.0)
            draw = draw + jnp.dot(dpre, unsel, preferred_element_type=F32, precision=HI)
            db_ref[g] += jnp.sum(dpre, axis=0, keepdims=True)
            dal_ref[g] += jnp.where(live[:1], jnp.sum(da * dt_ref[g], axis=0, keepdims=True) * A, 0.0)
        draw_ref[...] = draw

    gsp = pl.BlockSpec((SSM_G, 1, LANES), lambda c: (0, 0, 0))
    blk = pl.BlockSpec((SSM_G, SSM_L, LANES), lambda c: (0, c, 0))
    gout = _sds((SSM_G, 1, LANES))
    return _call(body, name="ssd_prep_bwd", out_shape=(_sds((S, LANES)), gout, gout), grid=(nc,),
                 in_specs=[blk, blk, gsp, blk, blk, pl.BlockSpec((SSM_G, LANES, SSM_L), lambda c: (0, 0, c))],
                 out_specs=(pl.BlockSpec((SSM_L, LANES), lambda c: (c, 0)), gsp, gsp))(pre_g, dt_g, alog_g, ddt_g, dacs_g, dacst_g)


def _ssd_chunk_terms(xs_h, dt_col, acs_col, acs_row, acs_last, Gm, tril):
    X = xs_h * dt_col
    Lm = jnp.exp(jnp.where(tril, acs_col - acs_row, -jnp.inf))
    M = Gm * Lm
    e_col = jnp.exp(acs_col)
    decay = jnp.exp(acs_last - acs_col)
    cd = jnp.exp(acs_last)
    return X, Lm, M, e_col, decay, cd


NT = (((1,), (1,)), ((), ()))
TN = (((0,), (0,)), ((), ()))


def _ssd_specs(rev, nc):
    def cc(c):
        return nc - 1 - c if rev else c
    xs = pl.BlockSpec((SSM_L, SSM_R * SSM_P), lambda g, c: (cc(c), g))
    bsp = pl.BlockSpec((SSM_L, SSM_N), lambda g, c: (cc(c), (SSM_G * SSM_R * SSM_P) // SSM_N + g))
    csp = pl.BlockSpec((SSM_L, SSM_N), lambda g, c: (cc(c), (SSM_G * SSM_R * SSM_P) // SSM_N + SSM_G + g))
    sc = pl.BlockSpec((1, SSM_L, LANES), lambda g, c: (g, cc(c), 0))
    sct = pl.BlockSpec((1, LANES, SSM_L), lambda g, c: (g, 0, cc(c)))
    gsp = pl.BlockSpec((1, 1, LANES), lambda g, c: (g, 0, 0))
    st = pl.BlockSpec((1, SSM_R, SSM_P, SSM_N), lambda g, c: (cc(c), g, 0, 0))
    return xs, bsp, csp, sc, sct, gsp, st


def _ssd_fwd(xbc, dt_g, acs_g, acst_g, d_g):
    S = xbc.shape[0]
    nc = S // SSM_L
    xs_s, b_s, c_s, sc, sct, gsp, st = _ssd_specs(False, nc)

    def body(xs_ref, b_ref, c_ref, dt_ref, acs_ref, acst_ref, d_ref, y_ref, st_ref, state):
        c = pl.program_id(1)

        @pl.when(c == 0)
        def _():
            state[...] = jnp.zeros_like(state)

        Bm, Cm = b_ref[...], c_ref[...]
        Bb, Cb = Bm.astype(BF16), Cm.astype(BF16)
        Gm = lax.dot_general(Cb, Bb, NT, preferred_element_type=F32)
        r_i = lax.broadcasted_iota(jnp.int32, (SSM_L, SSM_L), 0)
        c_i = lax.broadcasted_iota(jnp.int32, (SSM_L, SSM_L), 1)
        tril = r_i >= c_i
        for r in range(SSM_R):
            xs_h = xs_ref[:, r * SSM_P:(r + 1) * SSM_P]
            X, Lm, M, e_col, decay, cd = _ssd_chunk_terms(
                xs_h, dt_ref[0, :, r:r + 1], acs_ref[0, :, r:r + 1], acst_ref[0, r:r + 1, :],
                acs_ref[0, SSM_L - 1:SSM_L, r:r + 1], Gm, tril)
            Xb = X.astype(BF16)
            Sh = state[r]
            st_ref[0, r] = Sh
            yd = jnp.dot(M.astype(BF16), Xb, preferred_element_type=F32)
            yo = e_col * lax.dot_general(Cb, Sh.astype(BF16), NT, preferred_element_type=F32)
            y_ref[:, r * SSM_P:(r + 1) * SSM_P] = yd + yo + d_ref[0, :, r:r + 1] * xs_h
            state[r] = Sh * cd + lax.dot_general(Xb, (Bm * decay).astype(BF16), TN, preferred_element_type=F32)

    return _call(body, name="ssd_fwd",
                 out_shape=(_sds((S, SSM_G * SSM_R * SSM_P)), _sds((nc, SSM_G * SSM_R, SSM_P, SSM_N))),
                 grid=(SSM_G, nc), in_specs=[xs_s, b_s, c_s, sc, sc, sct, gsp],
                 out_specs=(xs_s, pl.BlockSpec((1, SSM_R, SSM_P, SSM_N), lambda g, c: (c, g, 0, 0))),
                 scratch=[pltpu.VMEM((SSM_R, SSM_P, SSM_N), F32)])(xbc, xbc, xbc, dt_g, acs_g, acst_g, d_g)


def _ssd_bwd(xbc, dt_g, acs_g, acst_g, d_g, states, dy):
    S = xbc.shape[0]
    nc = S // SSM_L
    xs_s, b_s, c_s, sc, sct, gsp, st = _ssd_specs(True, nc)
    bc_out = pl.BlockSpec((SSM_L, SSM_N), lambda g, c: (nc - 1 - c, g))

    def body(xs_ref, b_ref, c_ref, dt_ref, acs_ref, acst_ref, d_ref, st_ref, dy_ref,
             dxs_ref, db_ref, dc_ref, ddt_ref, dacs_ref, dacst_ref, dd_ref, dstate):
        c = pl.program_id(1)

        @pl.when(c == 0)
        def _():
            dstate[...] = jnp.zeros_like(dstate)
            dd_ref[...] = jnp.zeros_like(dd_ref)

        Bm, Cm = b_ref[...], c_ref[...]
        Bb, Cb = Bm.astype(BF16), Cm.astype(BF16)
        Gm = lax.dot_general(Cb, Bb, NT, preferred_element_type=F32)
        r_i = lax.broadcasted_iota(jnp.int32, (SSM_L, SSM_L), 0)
        c_i = lax.broadcasted_iota(jnp.int32, (SSM_L, SSM_L), 1)
        tril = r_i >= c_i
        lane = lax.broadcasted_iota(jnp.int32, (1, LANES), 1)
        subl = lax.broadcasted_iota(jnp.int32, (LANES, 1), 0)
        last_row = lax.broadcasted_iota(jnp.int32, (SSM_L, 1), 0) == SSM_L - 1
        dG = jnp.zeros((SSM_L, SSM_L), F32)
        dB = jnp.zeros((SSM_L, SSM_N), F32)
        dC = jnp.zeros((SSM_L, SSM_N), F32)
        ddt_blk = jnp.zeros((SSM_L, LANES), F32)
        dacs_blk = jnp.zeros((SSM_L, LANES), F32)
        dacst_blk = jnp.zeros((LANES, SSM_L), F32)
        dd_row = jnp.zeros((1, LANES), F32)
        for r in range(SSM_R):
            xs_h = xs_ref[:, r * SSM_P:(r + 1) * SSM_P]
            dt_col = dt_ref[0, :, r:r + 1]
            X, Lm, M, e_col, decay, cd = _ssd_chunk_terms(
                xs_h, dt_col, acs_ref[0, :, r:r + 1], acst_ref[0, r:r + 1, :],
                acs_ref[0, SSM_L - 1:SSM_L, r:r + 1], Gm, tril)
            Xb, Mb = X.astype(BF16), M.astype(BF16)
            Sh = st_ref[0, r]
            Shb = Sh.astype(BF16)
            dY = dy_ref[:, r * SSM_P:(r + 1) * SSM_P]
            dYb = dY.astype(BF16)
            dSn = dstate[r]
            dSnb = dSn.astype(BF16)
            dM = lax.dot_general(dYb, Xb, NT, preferred_element_type=F32)
            dX = lax.dot_general(Mb, dYb, TN, preferred_element_type=F32)
            dG = dG + dM * Lm
            dseg = dM * M
            dacs_col = jnp.sum(dseg, axis=1, keepdims=True)
            dacs_row = -jnp.sum(dseg, axis=0, keepdims=True)
            T = lax.dot_general(Cb, Shb, NT, preferred_element_type=F32)
            dT = (dY * e_col).astype(BF16)
            dC = dC + jnp.dot(dT, Shb, preferred_element_type=F32)
            dS_prev = lax.dot_general(dT, Cb, TN, preferred_element_type=F32)
            dacs_col = dacs_col + jnp.sum(dY * (e_col * T), axis=1, keepdims=True)
            Bd = (Bm * decay).astype(BF16)
            dS_prev = dS_prev + dSn * cd
            dcd = jnp.sum(dSn * Sh)
            dX = dX + lax.dot_general(Bd, dSnb, NT, preferred_element_type=F32)
            dBd = jnp.dot(Xb, dSnb, preferred_element_type=F32)
            dB = dB + dBd * decay
            dd = jnp.sum(dBd * Bm, axis=1, keepdims=True) * decay
            dacs_col = dacs_col - dd + jnp.where(last_row, dcd * cd + jnp.sum(dd), 0.0)
            dstate[r] = dS_prev
            dxs_ref[:, r * SSM_P:(r + 1) * SSM_P] = dX * dt_col + d_ref[0, :, r:r + 1] * dY
            ddt_blk = ddt_blk + jnp.where(lane == r, jnp.sum(dX * xs_h, axis=1, keepdims=True), 0.0)
            dacs_blk = dacs_blk + jnp.where(lane == r, dacs_col, 0.0)
            dacst_blk = dacst_blk + jnp.where(subl == r, dacs_row, 0.0)
            dd_row = dd_row + jnp.where(lane == r, jnp.sum(dY * xs_h), 0.0)
        dGb = dG.astype(BF16)
        dc_ref[...] = dC + jnp.dot(dGb, Bb, preferred_element_type=F32)
        db_ref[...] = dB + lax.dot_general(dGb, Cb, TN, preferred_element_type=F32)
        ddt_ref[0] = ddt_blk
        dacs_ref[0] = dacs_blk
        dacst_ref[0] = dacst_blk
        dd_ref[0] += dd_row

    big = _sds((SSM_G, S, LANES))
    return _call(body, name="ssd_bwd",
                 out_shape=(_sds((S, SSM_G * SSM_R * SSM_P)), _sds((S, SSM_G * SSM_N)), _sds((S, SSM_G * SSM_N)),
                            big, big, _sds((SSM_G, LANES, S)), _sds((SSM_G, 1, LANES))),
                 grid=(SSM_G, nc), in_specs=[xs_s, b_s, c_s, sc, sc, sct, gsp, st, xs_s],
                 out_specs=(xs_s, bc_out, bc_out, sc, sc, sct, gsp),
                 scratch=[pltpu.VMEM((SSM_R, SSM_P, SSM_N), F32)])(xbc, xbc, xbc, dt_g, acs_g, acst_g, d_g, states, dy)


def _gate_norm_fwd(y, proj, w):
    S, DI = y.shape
    tm = _tile(S, 256)

    def body(y_ref, z_ref, w_ref, o_ref):
        z = z_ref[...]
        gn = y_ref[...] * (z * _sigmoid(z))
        r = lax.rsqrt(jnp.mean(gn * gn, axis=-1, keepdims=True) + SSM_NORM_EPS)
        o_ref[...] = (gn * r * w_ref[...]).astype(BF16)

    row = pl.BlockSpec((tm, DI), lambda i: (i, 0))
    return _call(body, name="gate_norm_fwd", out_shape=_sds((S, DI), BF16), grid=(S // tm,),
                 in_specs=[row, row, pl.BlockSpec((1, DI), lambda i: (0, 0))], out_specs=row)(y, proj, w)


def _gate_norm_bwd(y, proj, w, dout):
    S, DI = y.shape
    tm = _tile(S, 256)

    def body(y_ref, z_ref, w_ref, d_ref, dy_ref, dz_ref, dw_ref):
        z, yv = z_ref[...], y_ref[...]
        sz = z * _sigmoid(z)
        dgn, dw = _norm_bwd_math(yv * sz, w_ref[...], d_ref[...].astype(F32), SSM_NORM_EPS)
        dy_ref[...] = dgn * sz
        dz_ref[...] = (dgn * yv * _silu_grad(z)).astype(BF16)

        @pl.when(pl.program_id(0) == 0)
        def _():
            dw_ref[...] = jnp.zeros_like(dw_ref)

        dw_ref[...] += dw

    row = pl.BlockSpec((tm, DI), lambda i: (i, 0))
    vec = pl.BlockSpec((1, DI), lambda i: (0, 0))
    return _call(body, name="gate_norm_bwd", out_shape=(_sds((S, DI)), _sds((S, DI), BF16), _sds((1, DI))), grid=(S // tm,),
                 in_specs=[row, row, vec, row], out_specs=(row, row, vec))(y, proj, w, dout)


def _group_major(v):
    return jnp.pad(v.reshape(SSM_G, 1, SSM_R), ((0, 0), (0, 0), (0, LANES - SSM_R)))


def _ungroup(t):
    return t[:, :SSM_R].reshape(1, SSM_G * SSM_R)


def _ffn_fwd(x, P, l, need):
    need(f"ffn{l}_up", x)
    h = _rmsnorm(x, P["norm_ffn"][l:l + 1], name=f"ffn{l}_norm")
    a = _mm(h, P[f"ffn_w_up{l}"], name=f"ffn{l}_up")
    need(f"ffn{l}_down", a)
    act = _ffn_mid_fwd(a, P["ffn_conv_w"], P["ffn_conv_b"], l)
    out = _mm(act, P[f"ffn_w_down{l}"], res=x, name=f"ffn{l}_down")
    return out, (x, h, a, act)


def _ffn_bwd(saved, P, l, dx, emit):
    x, h, a, act = saved
    dact = _mm(dx, P[f"ffn_w_down{l}"], tb=True, out_dtype=BF16, name=f"ffn{l}_down_dx")
    dw_down = _mm(act, dx, ta=True, out_dtype=PAYLOAD, name=f"ffn{l}_down_dw")
    da, dcw, dcb = _ffn_mid_bwd(a, P["ffn_conv_w"], P["ffn_conv_b"], l, dact)
    dw_up = _mm(h, da, ta=True, out_dtype=PAYLOAD, name=f"ffn{l}_up_dw")
    tie = emit(f"ffn{l}", {"ffn_w_up": dw_up, "ffn_w_down": dw_down})
    dh = _mm(da, P[f"ffn_w_up{l}"], tb=True, name=f"ffn{l}_up_dx")
    dx_in, dnw = _rmsnorm_bwd(x, P["norm_ffn"][l:l + 1], dh, dx, name=f"ffn{l}_norm_bwd", after=tie)
    return dx_in, dnw, dcw, dcb


def _local_step(x, positions, target, P, need, emit, after=None):
    S, D = x.shape
    inv_freq = ROPE_THETA ** (-jnp.arange(0, HEAD_DIM, 2, dtype=F32) / HEAD_DIM)
    inv_freq = jnp.tile(inv_freq, LANES // (HEAD_DIM // 2)).reshape(1, LANES)
    cos, sin = _rope_tables(positions, inv_freq)

    nm0 = P["norm_mix"][0:1]
    h0 = _rmsnorm(x, nm0, name="mix_norm", after=after)
    proj0 = _mm(h0, P["mix_w_in"], name="mix_in")
    cat0 = _attn_fwd(proj0, cos, sin, P["attn_sinks"], _pool_fwd(proj0, P["pool_w"][0], P["pool_scale"]))
    need("mix_out", cat0)
    x1 = _mm(cat0, P["mix_w_out"], res=x, name="mix_out")
    x2, ffn0 = _ffn_fwd(x1, P, 0, need)

    nm1 = P["norm_mix"][1:2]
    need("ssm", x2)
    h1 = _rmsnorm(x2, nm1, name="ssm_norm_in")
    z = _mm(h1, P["ssm_wz"], name="ssm_in_z")
    xbcp = _mm(h1, P["ssm_wxbc"], name="ssm_in_xbc")
    dtraw = _mm(h1, P["ssm_wdt"], name="ssm_in_dt")
    xbc = _conv_silu_fwd(xbcp, P["ssm_conv_w"], P["ssm_conv_b"])
    bias_g, alog_g, d_g = _group_major(P["ssm_dt_bias"]), _group_major(P["ssm_A_log"]), _group_major(P["ssm_D"])
    pre_g, dt_g, acs_g, acst_g = _ssd_prep_fwd(dtraw, 0, bias_g, alog_g)
    y, states = _ssd_fwd(xbc, dt_g, acs_g, acst_g, d_g)
    yn = _gate_norm_fwd(y, z, P["ssm_norm"])
    x3 = _mm(yn, P["ssm_w_out"], res=x2, name="ssm_out")
    x4, ffn1 = _ffn_fwd(x3, P, 1, need)

    loss, dx, d_norm_final = _final_loss(x4, P["norm_final"].reshape(1, D), target, name="final_loss")
    dx, dnf1, dcw1, dcb1 = _ffn_bwd(ffn1, P, 1, dx, emit)
    dyn = _mm(dx, P["ssm_w_out"], tb=True, out_dtype=BF16, name="ssm_out_dx")
    d_w_out1 = _mm(yn, dx, ta=True, out_dtype=PAYLOAD, name="ssm_out_dw")
    dy, dz, d_ssm_norm = _gate_norm_bwd(y, z, P["ssm_norm"], dyn)
    dxs, dB, dC, ddt_g, dacs_g, dacst_g, dd_g = _ssd_bwd(xbc, dt_g, acs_g, acst_g, d_g, states, dy)
    draw, dbias_g, dalog_g = _ssd_prep_bwd(pre_g, dt_g, alog_g, ddt_g, dacs_g, dacst_g)
    dxbc, d_conv_w1, d_conv_b1 = _conv_silu_bwd(xbcp, P["ssm_conv_w"], P["ssm_conv_b"], [dxs, dB, dC])
    d_wz = _mm(h1, dz, ta=True, out_dtype=PAYLOAD, name="ssm_in_z_dw")
    d_wxbc = _mm(h1, dxbc, ta=True, out_dtype=PAYLOAD, name="ssm_in_xbc_dw")
    d_wdt = _mm(h1, draw, ta=True, out_dtype=PAYLOAD, name="ssm_in_dt_dw")
    tie = emit("ssm", {"ssm_wz": d_wz, "ssm_wxbc": d_wxbc, "ssm_wdt": d_wdt, "ssm_w_out": d_w_out1,
                       "ssm_conv_w": d_conv_w1, "ssm_conv_b": d_conv_b1, "ssm_norm": d_ssm_norm})
    dh1 = _mm(dz, P["ssm_wz"], tb=True, name="ssm_in_z_dx")
    dh1 = _mm(dxbc, P["ssm_wxbc"], tb=True, res=dh1, name="ssm_in_xbc_dx")
    dh1 = _mm(draw, P["ssm_wdt"], tb=True, res=dh1, name="ssm_in_dt_dx")
    dx, dnm1 = _rmsnorm_bwd(x2, nm1, dh1, dx, name="ssm_norm_in_bwd", after=tie)
    dx, dnf0, dcw0, dcb0 = _ffn_bwd(ffn0, P, 0, dx, emit)
    dcat = _mm(dx, P["mix_w_out"], tb=True, name="mix_out_dx")
    d_w_out0 = _mm(cat0, dx, ta=True, out_dtype=PAYLOAD, name="mix_out_dw")
    dproj0, dsk = _attn_bwd(proj0, cos, sin, P["attn_sinks"], dcat)
    dproj0, d_pool_w, d_pool_scale = _pool_bwd(proj0, P["pool_w"][0], P["pool_scale"], dcat, dproj0)
    d_w_in0 = _mm(h0, dproj0, ta=True, out_dtype=PAYLOAD, name="mix_in_dw")
    tie = emit("mix", {"mix_w_in": d_w_in0, "mix_w_out": d_w_out0, "ffn_conv_w": jnp.stack([dcw0, dcw1])})
    dh0 = _mm(dproj0, P["mix_w_in"], tb=True, name="mix_in_dx")
    grad_x, dnm0 = _rmsnorm_bwd(x, nm0, dh0, dx, name="mix_norm_bwd", after=tie)

    small = {
        "norm_mix": jnp.concatenate([dnm0, dnm1], axis=0),
        "norm_ffn": jnp.concatenate([dnf0, dnf1], axis=0),
        "norm_final": d_norm_final,
        "pool_w": d_pool_w,
        "pool_scale": d_pool_scale,
        "attn_sinks_rows": dsk,
        "ssm_dt_bias_g": dbias_g, "ssm_A_log_g": dalog_g, "ssm_D_g": dd_g,
        "ffn_conv_b": jnp.concatenate([dcb0, dcb1], axis=0),
    }
    return loss, grad_x, small


def _peer(k):
    x, y, c = lax.axis_index("x"), lax.axis_index("y"), lax.axis_index("c")
    px = 1 - x if k & 4 else x
    py = 1 - y if k & 2 else y
    pc = 1 - c if k & 1 else c
    return (px, py, pc), 4 * px + 2 * py + pc


def _my_index():
    return 4 * lax.axis_index("x") + 2 * lax.axis_index("y") + lax.axis_index("c")


def _land_sds(a, mode, gather):
    if mode == "slab":
        return _sds(((N_DEV,) + a.shape) if gather else a.shape, a.dtype)
    assert mode == "rows", mode
    return _sds((N_DEV * a.shape[0],) + a.shape[1:] if gather else (N_DEV, a.shape[0] // N_DEV) + a.shape[1:], a.dtype)


def _part(ref, mode, shape, idx):
    if mode == "slab":
        return ref.at[idx]
    r = shape[0] // N_DEV
    return ref.at[pl.ds(idx * r, r)]


def _own_copies(ops, gather, srcs, lands, sems):
    me = _my_index()
    out = []
    for i, (a, mode) in enumerate(ops):
        s = srcs[i] if gather else _part(srcs[i], mode, a.shape, me)
        d = _part(lands[i], mode, _land_sds(a, mode, gather).shape, me) if gather else lands[i].at[me]
        out.append(pltpu.make_async_copy(s, d, sems.at[i]))
    return out


def _remote_copies(ops, gather, srcs, lands, send_sems, recv_sems):
    me = _my_index()
    n = len(ops)
    out = []
    for k in range(1, N_DEV):
        dev, idx = _peer(k)
        for i, (a, mode) in enumerate(ops):
            s = srcs[i] if gather else _part(srcs[i], mode, a.shape, idx)
            d = _part(lands[i], mode, _land_sds(a, mode, gather).shape, me) if gather else lands[i].at[me]
            out.append(pltpu.make_async_remote_copy(src_ref=s, dst_ref=d, send_sem=send_sems.at[(k - 1) * n + i],
                                                    recv_sem=recv_sems.at[(k - 1) * n + i], device_id=dev,
                                                    device_id_type=pl.DeviceIdType.MESH))
    return out


def _exchange(ops, *, gather, name):
    n = len(ops)

    def body(*refs):
        ins, outs = refs[:n], refs[n:2 * n]
        send_sems, recv_sems, local_sems = refs[2 * n:]
        copies = _own_copies(ops, gather, ins, outs, local_sems) + _remote_copies(ops, gather, ins, outs, send_sems, recv_sems)
        for cp in copies:
            cp.start()
        for cp in copies:
            cp.wait()

    return pl.pallas_call(
        body, name=name, out_shape=[_land_sds(a, m, gather) for a, m in ops], in_specs=[ANY] * n, out_specs=[ANY] * n,
        scratch_shapes=[pltpu.SemaphoreType.DMA((n * (N_DEV - 1),)), pltpu.SemaphoreType.DMA((n * (N_DEV - 1),)),
                        pltpu.SemaphoreType.DMA((n,))],
    )(*[a for a, _ in ops])


HBM = pl.BlockSpec(memory_space=pltpu.HBM)
SEM = pl.BlockSpec(memory_space=pltpu.SEMAPHORE)
SIDE_EFFECT = pltpu.SideEffectType.DATAFLOW_SIDE_EFFECTING


def _in_hbm(a):
    return pltpu.with_memory_space_constraint(a, pltpu.HBM)


def _place_own(ops, *, gather, name):
    n = len(ops)

    def zeros(k):
        return (0,) * k

    in_specs, out_specs = [], []
    for a, mode in ops:
        nd = a.ndim
        if gather and mode == "slab":
            in_specs.append(pl.BlockSpec(a.shape, lambda i, nd=nd: zeros(nd)))
            out_specs.append(pl.BlockSpec((1,) + a.shape, lambda i, nd=nd: (_my_index(),) + zeros(nd)))
        elif gather:
            in_specs.append(pl.BlockSpec(a.shape, lambda i, nd=nd: zeros(nd)))
            out_specs.append(pl.BlockSpec(a.shape, lambda i, nd=nd: (_my_index(),) + zeros(nd - 1)))
        elif mode == "slab":
            in_specs.append(pl.BlockSpec((1,) + a.shape[1:], lambda i, nd=nd: (_my_index(),) + zeros(nd - 1)))
            out_specs.append(pl.BlockSpec((1,) + a.shape[1:], lambda i, nd=nd: (_my_index(),) + zeros(nd - 1)))
        else:
            r = a.shape[0] // N_DEV
            in_specs.append(pl.BlockSpec((r,) + a.shape[1:], lambda i, nd=nd: (_my_index(),) + zeros(nd - 1)))
            out_specs.append(pl.BlockSpec((1, r) + a.shape[1:], lambda i, nd=nd: (_my_index(),) + zeros(nd)))

    def body(*refs):
        for i_ref, o_ref in zip(refs[:n], refs[n:]):
            if o_ref.shape == i_ref.shape:
                o_ref[...] = i_ref[...]
            else:
                o_ref[0] = i_ref[...]

    return _call(body, name=name, out_shape=[_land_sds(a, m, gather) for a, m in ops], grid=(1,),
                 in_specs=in_specs, out_specs=out_specs)(*[a for a, _ in ops])


def _exchange_start(groups, *, gather, name):
    sizes = [len(ops) for ops, _ in groups]
    n = sum(sizes)
    G = len(groups)

    def body(*refs):
        srcs, lands = refs[:n], refs[n:2 * n]
        sems = refs[2 * n:2 * n + 2 * G]
        token = refs[-1]
        off = 0
        for g, (ops, _) in enumerate(groups):
            for cp in _remote_copies(ops, gather, srcs[off:off + sizes[g]], lands[off:off + sizes[g]], sems[2 * g], sems[2 * g + 1]):
                cp.start()
            off += sizes[g]
        token[...] = jnp.zeros_like(token)

    srcs = [a for ops, _ in groups for a, _ in ops]
    lands = [l for _, ls in groups for l in ls]
    sem_shapes = [pltpu.SemaphoreType.DMA((s * (N_DEV - 1),)) for s in sizes for _ in range(2)]
    outs = pl.pallas_call(
        body, name=name,
        out_shape=sem_shapes + [pltpu.HBM(a.shape, a.dtype) for a in srcs + lands] + [_sds((8, LANES))],
        in_specs=[HBM] * (2 * n), out_specs=[SEM] * (2 * G) + [HBM] * (2 * n) + [pl.BlockSpec(memory_space=pltpu.VMEM)],
        input_output_aliases={i: 2 * G + i for i in range(2 * n)},
        compiler_params=pltpu.CompilerParams(has_side_effects=SIDE_EFFECT))(*[_in_hbm(a) for a in srcs + lands])
    sems, thru, token = outs[:2 * G], outs[2 * G:2 * G + 2 * n], outs[-1]
    states, off = [], 0
    for g, s in enumerate(sizes):
        states.append((sems[2 * g], sems[2 * g + 1], thru[off:off + s], thru[n + off:n + off + s]))
        off += s
    return states, token


def _exchange_wait(ops, state, after, *, gather, name):
    send_sems, recv_sems, srcs, lands = state
    n = len(ops)

    def body(*refs):
        for cp in _remote_copies(ops, gather, refs[:n], refs[n:2 * n], refs[2 * n], refs[2 * n + 1]):
            cp.wait_send()
            cp.wait_recv()

    outs = pl.pallas_call(
        body, name=name, out_shape=[pltpu.HBM(a.shape, a.dtype) for a in list(srcs) + list(lands)],
        in_specs=[HBM] * (2 * n) + [SEM, SEM, ANY], out_specs=[HBM] * (2 * n),
        input_output_aliases={i: i for i in range(2 * n)},
        compiler_params=pltpu.CompilerParams(has_side_effects=SIDE_EFFECT))(*srcs, *lands, send_sems, recv_sems, after)
    return outs[n:]


ADAM_ROWS = 256


def _row_tile(R, cap=ADAM_ROWS):
    best = R
    if R > cap:
        for d in range(16, cap + 1, 16):
            if R % d == 0:
                best = d
    return best


def _adamw(g_layers, w, m, v, *, name):
    L = len(g_layers)
    J, R, Wd = g_layers[0].shape
    assert w.shape == (L, R, Wd), (g_layers[0].shape, w.shape)
    tr = _row_tile(R)
    nrt = R // tr
    c1 = 1.0 / (1.0 - ADAM_B1 ** ADAM_STEP)
    c2 = 1.0 / (1.0 - ADAM_B2 ** ADAM_STEP)

    def body(*refs):
        g_refs = refs[:L]
        w_ref, m_ref, v_ref, go_ref, d_ref, mo_ref, vo_ref = refs[L:]
        layer = pl.program_id(0)
        g = None
        for l, g_ref in enumerate(g_refs):
            gl = g_ref[0].astype(F32)
            for j in range(1, J):
                gl = gl + g_ref[j].astype(F32)
            g = gl if g is None else jnp.where(layer == l, gl, g)
        mn = ADAM_B1 * m_ref[...] + (1.0 - ADAM_B1) * g
        vn = ADAM_B2 * v_ref[...] + (1.0 - ADAM_B2) * (g * g)
        go_ref[...] = g
        mo_ref[...] = mn
        vo_ref[...] = vn
        d_ref[...] = -ADAM_LR * ((mn * c1) / (jnp.sqrt(vn * c2) + ADAM_EPS) + ADAM_WD * w_ref[...])

    def g_spec(l):
        return pl.BlockSpec((J, tr, Wd), lambda ll, i: (0, jnp.where(ll == l, i, jnp.where(ll < l, 0, nrt - 1)), 0))

    row = pl.BlockSpec((None, tr, Wd), lambda ll, i: (ll, i, 0))
    out = _sds((L, R, Wd))
    return _call(body, name=name, out_shape=(out, out, out, out), grid=(L, nrt),
                 in_specs=[g_spec(l) for l in range(L)] + [row, row, row], out_specs=(row, row, row, row))(*g_layers, w, m, v)


def _sum_slabs(slabs, *, name):
    n = len(slabs)

    def body(*refs):
        for g_ref, o_ref in zip(refs[:n], refs[n:]):
            g = g_ref[0]
            for j in range(1, g_ref.shape[0]):
                g = g + g_ref[j]
            o_ref[...] = g

    return _call(body, name=name, out_shape=[_sds(s.shape[1:]) for s in slabs])(*slabs)


RELAYOUT_ROWS = 256


def _col_plan(n, segments):
    plan = []
    for j in range(N_DEV):
        lo, hi = j * n, (j + 1) * n
        for t0, t1, oi, o0 in segments:
            a, b = max(lo, t0), min(hi, t1)
            if a < b:
                plan.append((j, a - lo, oi, o0 + a - t0, b - a))
    return plan


def _interleave_segments(F):
    seg = []
    for b in range(F // LANES):
        seg.append((b * LANES, (b + 1) * LANES, 0, 2 * b * LANES))
        seg.append((F + b * LANES, F + (b + 1) * LANES, 0, (2 * b + 1) * LANES))
    return seg


def _interleave_perm(F):
    perm = []
    for b in range(F // LANES):
        perm += list(range(b * LANES, (b + 1) * LANES)) + list(range(F + b * LANES, F + (b + 1) * LANES))
    inv = [0] * (2 * F)
    for d, s in enumerate(perm):
        inv[s] = d
    return jnp.asarray(perm, jnp.int32), jnp.asarray(inv, jnp.int32)


def _cols_from_slabs(slabs, plan, widths, *, name):
    _, R, n = slabs.shape
    tr = _row_tile(R, RELAYOUT_ROWS)
    covered = [sum(e[4] for e in plan if e[2] == i) for i in range(len(widths))]

    def body(s_ref, *o_refs):
        for i, o_ref in enumerate(o_refs):
            if covered[i] < widths[i]:
                o_ref[...] = jnp.zeros_like(o_ref)
        for j, sc, oi, oc, w in plan:
            o_refs[oi][:, oc:oc + w] = s_ref[j, :, sc:sc + w]

    return _call(body, name=name, out_shape=[_sds((R, w), slabs.dtype) for w in widths], grid=(R // tr,),
                 in_specs=[pl.BlockSpec((N_DEV, tr, n), lambda i: (0, i, 0))],
                 out_specs=[pl.BlockSpec((tr, w), lambda i: (i, 0)) for w in widths])(slabs)


def _slabs_from_cols(mats, plan, n, *, name):
    R = mats[0].shape[0]
    tr = _row_tile(R, RELAYOUT_ROWS)

    def body(*refs):
        m_refs, o_ref = refs[:-1], refs[-1]
        for j, sc, oi, oc, w in plan:
            o_ref[j, :, sc:sc + w] = m_refs[oi][:, oc:oc + w]

    return _call(body, name=name, out_shape=_sds((N_DEV, R, n), mats[0].dtype), grid=(R // tr,),
                 in_specs=[pl.BlockSpec((tr, m.shape[1]), lambda i: (i, 0)) for m in mats],
                 out_specs=pl.BlockSpec((N_DEV, tr, n), lambda i: (0, i, 0)))(*mats)


def kernel(x, positions, norm_mix, norm_ffn, norm_final, mix_w_in, pool_w, pool_scale, attn_sinks, mix_w_out, ssm_w_in, ssm_conv_w, ssm_conv_b, ssm_dt_bias, ssm_A_log, ssm_D, ssm_norm, ssm_w_out, ffn_w_up, ffn_conv_w, ffn_conv_b, ffn_w_down, loss_target, m_norm_mix, m_norm_ffn, m_norm_final, m_mix_w_in, m_pool_w, m_pool_scale, m_attn_sinks, m_mix_w_out, m_ssm_w_in, m_ssm_conv_w, m_ssm_conv_b, m_ssm_dt_bias, m_ssm_A_log, m_ssm_D, m_ssm_norm, m_ssm_w_out, m_ffn_w_up, m_ffn_conv_w, m_ffn_conv_b, m_ffn_w_down, v_norm_mix, v_norm_ffn, v_norm_final, v_mix_w_in, v_pool_w, v_pool_scale, v_attn_sinks, v_mix_w_out, v_ssm_w_in, v_ssm_conv_w, v_ssm_conv_b, v_ssm_dt_bias, v_ssm_A_log, v_ssm_D, v_ssm_norm, v_ssm_w_out, v_ffn_w_up, v_ffn_conv_w, v_ffn_conv_b, v_ffn_w_down):
    args = dict(locals())
    wl = {n: args[n] for n in WEIGHTS}
    ml = {n: args["m_" + n] for n in WEIGHTS}
    vl = {n: args["v_" + n] for n in WEIGHTS}
    D = x.shape[2]
    F = ffn_w_down.shape[1] * N_DEV
    DI, CD, NH = ssm_norm.shape[1] * N_DEV, ssm_conv_b.shape[1] * N_DEV, ssm_dt_bias.shape[1]
    Kc, Kf = ssm_conv_w.shape[1], ffn_conv_w.shape[1]
    n_mix, n_ssm, n_up = mix_w_in.shape[2], ssm_w_in.shape[2], ffn_w_up.shape[2]
    plan_mix = _col_plan(n_mix, [(0, N_DEV * n_mix, 0, 0)])
    plan_ssm = _col_plan(n_ssm, [(0, DI, 0, 0), (DI, DI + CD, 1, 0), (DI + CD, DI + CD + NH, 2, 0)])
    plan_up = _col_plan(n_up, _interleave_segments(F))
    perm, inv = _interleave_perm(F)

    def two(a):
        return a.reshape(-1, a.shape[-1])

    def pay(a):
        return a.astype(PAYLOAD)

    order = ("mix_in", "mix_out", "ffn0_up", "ffn0_down", "ssm", "ffn1_up", "ffn1_down")
    gops = {
        "mix_in": [(pay(two(mix_w_in)), "slab")],
        "mix_out": [(pay(mix_w_out[0]), "rows"), (two(ssm_conv_w), "slab"), (ssm_conv_b, "slab"), (ssm_norm, "slab"),
                    (two(ffn_conv_w), "slab")],
        "ffn0_up": [(pay(ffn_w_up[0]), "slab")], "ffn0_down": [(pay(ffn_w_down[0]), "rows")],
        "ssm": [(pay(two(ssm_w_in)), "slab"), (pay(ssm_w_out[0]), "rows")],
        "ffn1_up": [(pay(ffn_w_up[1]), "slab")], "ffn1_down": [(pay(ffn_w_down[1]), "rows")],
    }
    lands = _place_own([op for g in order for op in gops[g]], gather=True, name="gather_own")
    groups, off = [], 0
    for g in order:
        groups.append((gops[g], lands[off:off + len(gops[g])]))
        off += len(gops[g])
    gstates, token = _exchange_start(groups, gather=True, name="gather_start")
    gstate = dict(zip(order, gstates))
    P = {n: wl[n] for n in REPLICATED}
    P["ffn_conv_b"] = jnp.take(ffn_conv_b, perm, axis=1)

    def need(g, after):
        got = _exchange_wait(gops[g], gstate[g], after, gather=True, name="gather_wait_" + g)
        if g == "mix_in":
            (P["mix_w_in"],) = _cols_from_slabs(got[0], plan_mix, (N_DEV * n_mix,), name="unpack_mix_w_in")
        elif g == "mix_out":
            P.update(mix_w_out=got[0], ssm_conv_w=got[1].transpose(1, 0, 2).reshape(Kc, CD), ssm_conv_b=got[2].reshape(1, CD),
                     ssm_norm=got[3].reshape(1, DI),
                     ffn_conv_w=jnp.take(got[4].transpose(1, 0, 2).reshape(2 * Kf, 2 * F), perm, axis=1).reshape(2, Kf, 2 * F))
        elif g == "ssm":
            P["ssm_wz"], P["ssm_wxbc"], P["ssm_wdt"] = _cols_from_slabs(got[0], plan_ssm, (DI, CD, LANES), name="unpack_ssm_w_in")
            P["ssm_w_out"] = got[1]
        elif g.endswith("_up"):
            (P["ffn_w_up" + g[3]],) = _cols_from_slabs(got[0], plan_up, (2 * F,), name="unpack_ffn_w_up" + g[3])
        else:
            P["ffn_w_down" + g[3]] = got[0]

    sent = {}

    def emit(g, d):
        if g == "mix":
            ops = [(_slabs_from_cols([d["mix_w_in"]], plan_mix, n_mix, name="pack_mix_w_in"), "slab"), (d["mix_w_out"], "rows"),
                   (jnp.take(d["ffn_conv_w"].reshape(2 * Kf, 2 * F), inv, axis=1).reshape(2 * Kf, N_DEV, n_up).transpose(1, 0, 2), "slab")]
        elif g == "ssm":
            ops = [(_slabs_from_cols([d["ssm_wz"], d["ssm_wxbc"], d["ssm_wdt"]], plan_ssm, n_ssm, name="pack_ssm_w_in"), "slab"),
                   (d["ssm_w_out"], "rows"), (d["ssm_conv_w"].reshape(Kc, N_DEV, -1).transpose(1, 0, 2), "slab"),
                   (d["ssm_conv_b"].reshape(N_DEV, 1, -1), "slab"), (d["ssm_norm"].reshape(N_DEV, 1, -1), "slab")]
        else:
            ops = [(_slabs_from_cols([d["ffn_w_up"]], plan_up, n_up, name="pack_ffn_w_up" + g[3]), "slab"), (d["ffn_w_down"], "rows")]
        own = _place_own(ops, gather=False, name="scatter_own_" + g)
        (state,), tok = _exchange_start([(ops, own)], gather=False, name="scatter_start_" + g)
        sent[g] = (ops, state)
        return tok

    need("mix_in", token)
    loss_lanes, grad_x, G = _local_step(x[0], positions.reshape(-1, 1), loss_target[0], P, need, emit, after=token)
    loss = lax.psum(loss_lanes[0, 0], ("x", "y", "c"))

    res = {}

    def update(n, g_layers):
        L = len(g_layers)
        g_layers = [g.reshape(g.shape[0], -1, g.shape[-1]) for g in g_layers]
        shape = (L,) + g_layers[0].shape[1:]
        outs = _adamw(g_layers, wl[n].reshape(shape), ml[n].reshape(shape), vl[n].reshape(shape), name="adamw_" + n)
        for kind, a in zip(("grad", "delta", "new_m", "new_v"), outs):
            res[kind, n] = a.reshape(wl[n].shape)

    recv = {g: _exchange_wait(sent[g][0], sent[g][1], grad_x, gather=False, name="scatter_wait_" + g)
            for g in ("ffn1", "ssm", "ffn0", "mix")}
    update("mix_w_in", [recv["mix"][0]])
    update("mix_w_out", [recv["mix"][1]])
    update("ffn_conv_w", [recv["mix"][2]])
    update("ssm_w_in", [recv["ssm"][0]])
    update("ssm_w_out", [recv["ssm"][1]])
    update("ssm_conv_w", [recv["ssm"][2]])
    update("ssm_conv_b", [recv["ssm"][3]])
    update("ssm_norm", [recv["ssm"][4]])
    update("ffn_w_up", [recv["ffn0"][0], recv["ffn1"][0]])
    update("ffn_w_down", [recv["ffn0"][1], recv["ffn1"][1]])

    rep = _exchange(
        [(a, "slab") for a in (G["norm_mix"], G["norm_ffn"], G["norm_final"], G["pool_w"].reshape(-1, LANES), G["pool_scale"],
                               jnp.take(G["ffn_conv_b"], inv, axis=1), G["attn_sinks_rows"],
                               G["ssm_dt_bias_g"].reshape(SSM_G, LANES), G["ssm_A_log_g"].reshape(SSM_G, LANES),
                               G["ssm_D_g"].reshape(SSM_G, LANES))],
        gather=True, name="gather_small_grads")
    for n, r in zip(("norm_mix", "norm_ffn", "norm_final", "pool_w", "pool_scale", "ffn_conv_b"), rep):
        update(n, [r])
    sinks_rows, bias_g, alog_g, d_g = _sum_slabs(rep[6:], name="sum_head_grads")
    update("attn_sinks", [sinks_rows[:, 0].reshape(1, 1, N_HEADS)])
    update("ssm_dt_bias", [_ungroup(bias_g)[None]])
    update("ssm_A_log", [_ungroup(alog_g)[None]])
    update("ssm_D", [_ungroup(d_g)[None]])

    return (loss, grad_x[None], *[res[k, n] for k in ("grad", "delta", "new_m", "new_v") for n in WEIGHTS])
```

```python
import functools
import math

import jax
import jax.numpy as jnp
from jax import lax
from jax.experimental import pallas as pl
from jax.experimental.pallas import tpu as pltpu

F32 = jnp.float32
BF16 = jnp.bfloat16

N_DEV = 8
LANES = 128
HEAD_DIM = 64
N_KV_HEADS = 2
GQ = 4
N_HEADS = N_KV_HEADS * GQ
BLOCK = 128
POOL_GROUPS = 4
ROPE_THETA = 10000.0
SSM_P = 64
SSM_G = 8
SSM_R = 4
SSM_N = 128
SSM_L = 128
NORM_EPS = 1e-6
SSM_NORM_EPS = 1e-5
ADAM_LR, ADAM_B1, ADAM_B2, ADAM_EPS, ADAM_WD, ADAM_STEP = 0.001, 0.9, 0.999, 1e-08, 0.01, 10
VMEM_LIMIT = 56 * 2 ** 20
PAYLOAD = jnp.bfloat16

REPLICATED = ("norm_mix", "norm_ffn", "norm_final", "pool_w", "pool_scale", "attn_sinks",
              "ssm_dt_bias", "ssm_A_log", "ssm_D", "ffn_conv_b")
WEIGHTS = ("norm_mix", "norm_ffn", "norm_final", "mix_w_in", "pool_w", "pool_scale", "attn_sinks", "mix_w_out",
           "ssm_w_in", "ssm_conv_w", "ssm_conv_b", "ssm_dt_bias", "ssm_A_log", "ssm_D", "ssm_norm", "ssm_w_out",
           "ffn_w_up", "ffn_conv_w", "ffn_conv_b", "ffn_w_down")


def _tile(n, cap):
    if n <= cap:
        return n
    best = None
    for d in range(LANES, cap + 1, LANES):
        if n % d == 0:
            best = d
    assert best is not None, (n, cap)
    return best


def _call(body, *, name, out_shape, grid=None, in_specs=None, out_specs=None, scratch=(), aliases=None):
    kw = {}
    if grid is not None:
        kw = dict(grid=grid, in_specs=in_specs, out_specs=out_specs)
    if aliases:
        kw["input_output_aliases"] = aliases
    return pl.pallas_call(
        body, name=name, out_shape=out_shape, scratch_shapes=list(scratch),
        compiler_params=pltpu.CompilerParams(vmem_limit_bytes=VMEM_LIMIT), **kw)


ANY = pl.BlockSpec(memory_space=pl.ANY)


def _sds(shape, dtype=F32):
    return jax.ShapeDtypeStruct(tuple(shape), dtype)


def _sigmoid(x):
    return 1.0 / (1.0 + jnp.exp(-x))


def _shift_dn(x, d, t):
    if d == 0:
        return x
    return jnp.where(t >= d, pltpu.roll(x, d, axis=0), 0.0)


def _shift_up(x, d, t):
    if d == 0:
        return x
    n = x.shape[0]
    return jnp.where(t < n - d, pltpu.roll(x, n - d, axis=0), 0.0)


def _mm(a, b, *, name, ta=False, tb=False, res=None, out_dtype=F32):
    M, K = (a.shape[1], a.shape[0]) if ta else a.shape
    N = b.shape[0] if tb else b.shape[1]
    assert (b.shape[1] if tb else b.shape[0]) == K, (a.shape, b.shape, ta, tb)
    tm, tn, tk = _tile(M, 1408), _tile(N, 1408), _tile(K, 1408)
    nk = K // tk
    dims = (((0 if ta else 1,), (1 if tb else 0,)), ((), ()))

    def body(*refs):
        a_ref, b_ref = refs[:2]
        r_ref = refs[2] if res is not None else None
        o_ref, acc = refs[-2:]
        k = pl.program_id(2)

        @pl.when(k == 0)
        def _():
            acc[...] = jnp.zeros_like(acc)

        acc[...] += lax.dot_general(a_ref[...].astype(BF16), b_ref[...].astype(BF16), dims,
                                    preferred_element_type=F32)

        @pl.when(k == nk - 1)
        def _():
            out = acc[...]
            if res is not None:
                out = out + r_ref[...]
            o_ref[...] = out.astype(out_dtype)

    a_spec = pl.BlockSpec((tk, tm), lambda i, j, k: (k, i)) if ta else pl.BlockSpec((tm, tk), lambda i, j, k: (i, k))
    b_spec = pl.BlockSpec((tn, tk), lambda i, j, k: (j, k)) if tb else pl.BlockSpec((tk, tn), lambda i, j, k: (k, j))
    o_spec = pl.BlockSpec((tm, tn), lambda i, j, k: (i, j))
    ins, specs = [a, b], [a_spec, b_spec]
    if res is not None:
        ins.append(res)
        specs.append(o_spec)
    return _call(body, name=name, out_shape=_sds((M, N), out_dtype), grid=(M // tm, N // tn, nk), in_specs=specs,
                 out_specs=o_spec, scratch=[pltpu.VMEM((tm, tn), F32)])(*ins)


def _rmsnorm(x, w, *, name, eps=NORM_EPS, after=None):
    S, D = x.shape
    tm = _tile(S, 512)
    tie = [] if after is None else [after]

    def body(x_ref, w_ref, *rest):
        o_ref = rest[-1]
        xf = x_ref[...]
        r = lax.rsqrt(jnp.mean(xf * xf, axis=-1, keepdims=True) + eps)
        o_ref[...] = (xf * r * w_ref[...]).astype(BF16)

    return _call(body, name=name, out_shape=_sds((S, D), BF16), grid=(S // tm,),
                 in_specs=[pl.BlockSpec((tm, D), lambda i: (i, 0)), pl.BlockSpec((1, D), lambda i: (0, 0))] + [ANY] * len(tie),
                 out_specs=pl.BlockSpec((tm, D), lambda i: (i, 0)))(x, w, *tie)


def _norm_bwd_math(xf, w, dh, eps):
    r = lax.rsqrt(jnp.mean(xf * xf, axis=-1, keepdims=True) + eps)
    xhat = xf * r
    dxh = dh * w
    dx = r * (dxh - xhat * jnp.mean(dxh * xhat, axis=-1, keepdims=True))
    dw = jnp.sum(dh * xhat, axis=0, keepdims=True)
    return dx, dw


def _rmsnorm_bwd(x, w, dh, dres, *, name, eps=NORM_EPS, after=None):
    S, D = x.shape
    tm = _tile(S, 512)
    tie = [] if after is None else [after]

    def body(x_ref, w_ref, dh_ref, dr_ref, *rest):
        dx_ref, dw_ref = rest[-2:]
        dx, dw = _norm_bwd_math(x_ref[...], w_ref[...], dh_ref[...].astype(F32), eps)
        dx_ref[...] = dr_ref[...] + dx

        @pl.when(pl.program_id(0) == 0)
        def _():
            dw_ref[...] = jnp.zeros_like(dw_ref)

        dw_ref[...] += dw

    row = pl.BlockSpec((tm, D), lambda i: (i, 0))
    vec = pl.BlockSpec((1, D), lambda i: (0, 0))
    return _call(body, name=name, out_shape=(_sds((S, D)), _sds((1, D))), grid=(S // tm,),
                 in_specs=[row, vec, row, row] + [ANY] * len(tie), out_specs=(row, vec))(x, w, dh, dres, *tie)


def _final_loss(x, w, target, *, name):
    S, D = x.shape
    tm = _tile(S, 512)

    def body(x_ref, w_ref, t_ref, loss_ref, dx_ref, dw_ref):
        xf, wv = x_ref[...], w_ref[...]
        r = lax.rsqrt(jnp.mean(xf * xf, axis=-1, keepdims=True) + NORM_EPS)
        err = xf * r * wv - t_ref[...]
        part = 0.5 * jnp.sum(jnp.mean(err * err, axis=-1, keepdims=True), axis=0, keepdims=True)
        dx, dw = _norm_bwd_math(xf, wv, err * (1.0 / D), NORM_EPS)
        dx_ref[...] = dx

        @pl.when(pl.program_id(0) == 0)
        def _():
            dw_ref[...] = jnp.zeros_like(dw_ref)
            loss_ref[...] = jnp.zeros_like(loss_ref)

        dw_ref[...] += dw
        loss_ref[...] += jnp.broadcast_to(part, loss_ref.shape)

    row = pl.BlockSpec((tm, D), lambda i: (i, 0))
    vec = pl.BlockSpec((1, D), lambda i: (0, 0))
    return _call(body, name=name, out_shape=(_sds((1, LANES)), _sds((S, D)), _sds((1, D))), grid=(S // tm,),
                 in_specs=[row, vec, row], out_specs=(pl.BlockSpec((1, LANES), lambda i: (0, 0)), row, vec))(x, w, target)


def _rope_tables(pos, inv_freq):
    S = pos.shape[0]
    tm = _tile(S, 512)

    def body(p_ref, f_ref, c_ref, s_ref):
        ang = p_ref[...].astype(F32) * f_ref[...]
        c_ref[...] = jnp.cos(ang)
        s_ref[...] = jnp.sin(ang)

    blk = pl.BlockSpec((tm, LANES), lambda i: (i, 0))
    return _call(body, name="rope_tables", out_shape=(_sds((S, LANES)), _sds((S, LANES))), grid=(S // tm,),
                 in_specs=[pl.BlockSpec((tm, 1), lambda i: (i, 0)), pl.BlockSpec((1, LANES), lambda i: (0, 0))],
                 out_specs=(blk, blk))(pos, inv_freq)


def _rot_half(t):
    lane = lax.broadcasted_iota(jnp.int32, t.shape, 1)
    lo = (lane % HEAD_DIM) < (HEAD_DIM // 2)
    return jnp.where(lo, -pltpu.roll(t, LANES - HEAD_DIM // 2, axis=1), pltpu.roll(t, HEAD_DIM // 2, axis=1))


def _rope(t, c, s):
    return t * c + _rot_half(t) * s


def _unrope(dy, c, s):
    return dy * c - _rot_half(dy * s)


PD = POOL_GROUPS * LANES
QD = N_HEADS * HEAD_DIM
KD = N_KV_HEADS * HEAD_DIM
assert PD % QD == 0 and (PD + QD) % (2 * KD) == 0 and KD == LANES
def _attn_probs(q, kcat, sink, mask):
    s = lax.dot_general(q.astype(BF16), kcat, (((1,), (1,)), ((), ())), preferred_element_type=F32) * (HEAD_DIM ** -0.5)
    s = jnp.where(mask, s, -jnp.inf)
    m = jnp.maximum(jnp.max(s, axis=1, keepdims=True), sink)
    p = jnp.exp(s - m)
    ps = jnp.exp(sink - m)
    inv = 1.0 / (jnp.sum(p, axis=1, keepdims=True) + ps)
    return p * inv, ps * inv


def _attn_mask(n):
    qi = lax.broadcasted_iota(jnp.int32, (BLOCK, 2 * BLOCK), 0)
    kj = lax.broadcasted_iota(jnp.int32, (BLOCK, 2 * BLOCK), 1)
    rel = qi + BLOCK - kj
    return (rel >= 0) & (rel < BLOCK) & ((n > 0) | (kj >= BLOCK))


def _attn_in_specs(nb):
    def cur(n):
        return jnp.minimum(n, nb - 1)

    def prev(n):
        return jnp.clip(n - 1, 0, nb - 1)

    kvb = (PD + QD) // (2 * KD)
    return [pl.BlockSpec(memory_space=pltpu.SMEM),
            pl.BlockSpec((BLOCK, QD), lambda n: (cur(n), PD // QD)),
            pl.BlockSpec((BLOCK, 2 * KD), lambda n: (cur(n), kvb)),
            pl.BlockSpec((BLOCK, 2 * KD), lambda n: (prev(n), kvb)),
            pl.BlockSpec((BLOCK, LANES), lambda n: (cur(n), 0)), pl.BlockSpec((BLOCK, LANES), lambda n: (cur(n), 0)),
            pl.BlockSpec((BLOCK, LANES), lambda n: (prev(n), 0)), pl.BlockSpec((BLOCK, LANES), lambda n: (prev(n), 0))]


def _attn_keys(kvc_ref, kvp_ref, cc, sc, cp, sp):
    kc = _rope(kvc_ref[:, :KD], cc, sc)
    kp = _rope(kvp_ref[:, :KD], cp, sp)
    vc, vp = kvc_ref[:, KD:], kvp_ref[:, KD:]
    kcat, vcat = [], []
    for kk in range(N_KV_HEADS):
        sl = slice(kk * HEAD_DIM, (kk + 1) * HEAD_DIM)
        kcat.append(jnp.concatenate([kp[:, sl], kc[:, sl]], axis=0).astype(BF16))
        vcat.append(jnp.concatenate([vp[:, sl], vc[:, sl]], axis=0).astype(BF16))
    return kcat, vcat


def _attn_fwd(proj, cos, sin, sinks, cat):
    S = proj.shape[0]
    nb = S // BLOCK

    def body(sink_ref, q_ref, kvc_ref, kvp_ref, cc_ref, sc_ref, cp_ref, sp_ref, cat_ref, o_ref):
        n = pl.program_id(0)
        cc, sc = cc_ref[...], sc_ref[...]
        kcat, vcat = _attn_keys(kvc_ref, kvp_ref, cc, sc, cp_ref[...], sp_ref[...])
        mask = _attn_mask(n)
        for j in range(QD // LANES):
            qr = _rope(q_ref[:, j * LANES:(j + 1) * LANES], cc, sc)
            for e in range(LANES // HEAD_DIM):
                h = j * (LANES // HEAD_DIM) + e
                pn, _ = _attn_probs(qr[:, e * HEAD_DIM:(e + 1) * HEAD_DIM], kcat[h // GQ], sink_ref[0, h], mask)
                o_ref[:, h * HEAD_DIM:(h + 1) * HEAD_DIM] = jnp.dot(pn.astype(BF16), vcat[h // GQ], preferred_element_type=F32)

    return _call(body, name="attn_fwd", out_shape=_sds(cat.shape), grid=(nb,),
                 in_specs=_attn_in_specs(nb) + [ANY], out_specs=pl.BlockSpec((BLOCK, QD), lambda n: (n, PD // QD)),
                 aliases={8: 0})(sinks, proj, proj, proj, cos, sin, cos, sin, cat)


def _attn_bwd(proj, cos, sin, sinks, dcat):
    S = proj.shape[0]
    nb = S // BLOCK
    scale = HEAD_DIM ** -0.5
    per = LANES // HEAD_DIM

    def body(sink_ref, q_ref, kvc_ref, kvp_ref, cc_ref, sc_ref, cp_ref, sp_ref, do_ref, o_ref, ds_ref, hold, carry, part, pair):
        n = pl.program_id(0)

        @pl.when(n == 0)
        def _():
            hold[...] = jnp.zeros_like(hold)
            carry[...] = jnp.zeros_like(carry)
            ds_ref[...] = jnp.zeros_like(ds_ref)

        live = jnp.where(n < nb, 1.0, 0.0)
        cc, sc, cp, sp = cc_ref[...], sc_ref[...], cp_ref[...], sp_ref[...]
        kcat, vcat = _attn_keys(kvc_ref, kvp_ref, cc, sc, cp, sp)
        mask = _attn_mask(n)
        o_ref[:, :PD] = jnp.zeros((BLOCK, PD), F32)
        o_ref[:, PD:PD + QD] = hold[...]
        dk = [jnp.zeros((2 * BLOCK, HEAD_DIM), F32) for _ in range(N_KV_HEADS)]
        dv = [jnp.zeros((2 * BLOCK, HEAD_DIM), F32) for _ in range(N_KV_HEADS)]
        row = lax.broadcasted_iota(jnp.int32, (8, LANES), 0)
        dsk = jnp.zeros((8, LANES), F32)
        for j in range(QD // LANES):
            qr = _rope(q_ref[:, j * LANES:(j + 1) * LANES], cc, sc)
            for e in range(per):
                h = j * per + e
                kk = h // GQ
                qh = qr[:, e * HEAD_DIM:(e + 1) * HEAD_DIM]
                pn, psn = _attn_probs(qh, kcat[kk], sink_ref[0, h], mask)
                doh = (do_ref[:, h * HEAD_DIM:(h + 1) * HEAD_DIM] * live).astype(BF16)
                dp = lax.dot_general(doh, vcat[kk], NT, preferred_element_type=F32)
                delta = jnp.sum(pn * dp, axis=1, keepdims=True)
                ds = (pn * (dp - delta) * scale).astype(BF16)
                pair[:, e * HEAD_DIM:(e + 1) * HEAD_DIM] = jnp.dot(ds, kcat[kk], preferred_element_type=F32)
                dk[kk] = dk[kk] + lax.dot_general(ds, qh.astype(BF16), TN, preferred_element_type=F32)
                dv[kk] = dv[kk] + lax.dot_general(pn.astype(BF16), doh, TN, preferred_element_type=F32)
                dsk = dsk + jnp.where(row == h, -jnp.sum(psn * delta), 0.0)
            hold[:, j * LANES:(j + 1) * LANES] = _unrope(pair[...], cc, sc)
        for kk in range(N_KV_HEADS):
            sl = slice(kk * HEAD_DIM, (kk + 1) * HEAD_DIM)
            sv = slice(KD + kk * HEAD_DIM, KD + (kk + 1) * HEAD_DIM)
            part[0, :, sl] = dk[kk][:BLOCK]
            part[0, :, sv] = dv[kk][:BLOCK]
            part[1, :, sl] = dk[kk][BLOCK:]
            part[1, :, sv] = dv[kk][BLOCK:]
        done = carry[...] + part[0]
        o_ref[:, PD + QD:PD + QD + KD] = _unrope(done[:, :KD], cp, sp)
        o_ref[:, PD + QD + KD:] = done[:, KD:]
        carry[...] = part[1]
        ds_ref[...] += dsk

    return _call(body, name="attn_bwd", out_shape=(_sds((S, PD + QD + 2 * KD)), _sds((8, LANES))), grid=(nb + 1,),
                 in_specs=_attn_in_specs(nb) + [pl.BlockSpec((BLOCK, QD), lambda n: (jnp.minimum(n, nb - 1), PD // QD))],
                 out_specs=(pl.BlockSpec((BLOCK, PD + QD + 2 * KD), lambda n: (jnp.maximum(n - 1, 0), 0)),
                            pl.BlockSpec((8, LANES), lambda n: (0, 0))),
                 scratch=[pltpu.VMEM((BLOCK, QD), F32), pltpu.VMEM((BLOCK, 2 * KD), F32),
                          pltpu.VMEM((2, BLOCK, 2 * KD), F32), pltpu.VMEM((BLOCK, LANES), F32)])(
                     sinks, proj, proj, proj, cos, sin, cos, sin, dcat)


def _pool_sums(u, g, t, shift):
    s2 = u + shift(u, 1, t)
    s4 = s2 + shift(s2, 2, t)
    s8 = s4 + shift(s4, 4, t)
    s16 = s8 + shift(s8, 8, t)
    return jnp.where(g == 0, s2, jnp.where(g == 1, s4, jnp.where(g == 2, s8, s16)))


def _pool_specs(S):
    col = pl.BlockSpec((S, LANES), lambda g: (0, g))
    wsp = pl.BlockSpec((1, LANES, LANES), lambda g: (g, 0, 0))
    vec = pl.BlockSpec((1, LANES), lambda g: (0, g))
    return col, wsp, vec


def _pool_fwd(proj, pool_w, scale):
    S = proj.shape[0]
    col, wsp, vec = _pool_specs(S)

    def body(u_ref, w_ref, s_ref, o_ref):
        g = pl.program_id(0)
        u = u_ref[...]
        t = lax.broadcasted_iota(jnp.int32, u.shape, 0)
        cnt = jnp.minimum(t + 1, 2 << g).astype(F32)
        pm = _pool_sums(u, g, t, _shift_dn) / cnt - u
        o_ref[...] = jnp.dot(pm.astype(BF16), w_ref[0].astype(BF16), preferred_element_type=F32) * s_ref[...]

    return _call(body, name="pool_fwd", out_shape=_sds((S, PD + QD)), grid=(POOL_GROUPS,),
                 in_specs=[col, wsp, vec], out_specs=col)(proj, pool_w, scale)


def _pool_bwd(proj, pool_w, scale, dcat, dproj):
    S = proj.shape[0]
    col, wsp, vec = _pool_specs(S)

    def body(u_ref, w_ref, s_ref, d_ref, dproj_ref, du_ref, dw_ref, dsc_ref):
        g = pl.program_id(0)
        u = u_ref[...]
        t = lax.broadcasted_iota(jnp.int32, u.shape, 0)
        cnt = jnp.minimum(t + 1, 2 << g).astype(F32)
        pm = (_pool_sums(u, g, t, _shift_dn) / cnt - u).astype(BF16)
        wv = w_ref[0].astype(BF16)
        d = d_ref[...]
        pw = jnp.dot(pm, wv, preferred_element_type=F32)
        dsc_ref[...] = jnp.sum(pw * d, axis=0, keepdims=True)
        dpw = (d * s_ref[...]).astype(BF16)
        dw_ref[0] = lax.dot_general(pm, dpw, (((0,), (0,)), ((), ())), preferred_element_type=F32)
        dpm = lax.dot_general(dpw, wv, (((1,), (1,)), ((), ())), preferred_element_type=F32)
        du_ref[...] = _pool_sums(dpm / cnt, g, t, _shift_up) - dpm

    return _call(body, name="pool_bwd",
                 out_shape=(_sds(dproj.shape), _sds((POOL_GROUPS, LANES, LANES)), _sds((1, POOL_GROUPS * LANES))),
                 grid=(POOL_GROUPS,), in_specs=[col, wsp, vec, col, ANY], out_specs=(col, wsp, vec),
                 aliases={4: 0})(proj, pool_w, scale, dcat, dproj)


def _conv(x, w_ref, b_ref, t):
    K = w_ref.shape[0]
    y = b_ref[...] + jnp.zeros_like(x)
    for k in range(K):
        y = y + w_ref[k:k + 1, :] * _shift_dn(x, K - 1 - k, t)
    return y


def _conv_bwd(x, dy, w_ref, t):
    K = w_ref.shape[0]
    dx = jnp.zeros_like(x)
    dws = []
    for k in range(K):
        dx = dx + w_ref[k:k + 1, :] * _shift_up(dy, K - 1 - k, t)
        dws.append(jnp.sum(dy * _shift_dn(x, K - 1 - k, t), axis=0, keepdims=True))
    return dx, dws, jnp.sum(dy, axis=0, keepdims=True)


def _silu_grad(y):
    sg = _sigmoid(y)
    return sg * (1.0 + y * (1.0 - sg))


def _ffn_mid_specs(S, K, layer):
    return [pl.BlockSpec((S, LANES), lambda j: (0, 2 * j)), pl.BlockSpec((S, LANES), lambda j: (0, 2 * j + 1)),
            pl.BlockSpec((None, K, LANES), lambda j: (layer, 0, 2 * j)), pl.BlockSpec((None, K, LANES), lambda j: (layer, 0, 2 * j + 1)),
            pl.BlockSpec((None, 1, LANES), lambda j: (layer, 0, 2 * j)), pl.BlockSpec((None, 1, LANES), lambda j: (layer, 0, 2 * j + 1))]


def _ffn_mid_fwd(a, cw, cb, layer):
    S, F2 = a.shape
    nf = F2 // (2 * LANES)
    K = cw.shape[1]

    def body(au_ref, ag_ref, wu_ref, wg_ref, bu_ref, bg_ref, o_ref):
        t = lax.broadcasted_iota(jnp.int32, (S, LANES), 0)
        hu = _conv(au_ref[...], wu_ref, bu_ref, t)
        hg = _conv(ag_ref[...], wg_ref, bg_ref, t)
        o_ref[...] = (hg * _sigmoid(hg) * hu).astype(BF16)

    return _call(body, name="ffn_mid_fwd", out_shape=_sds((S, F2 // 2), BF16), grid=(nf,),
                 in_specs=_ffn_mid_specs(S, K, layer), out_specs=pl.BlockSpec((S, LANES), lambda j: (0, j)))(
                     a, a, cw, cw, cb[:, None], cb[:, None])


def _ffn_mid_bwd(a, cw, cb, layer, dact):
    S, F2 = a.shape
    nf = F2 // (2 * LANES)
    K = cw.shape[1]

    def body(au_ref, ag_ref, wu_ref, wg_ref, bu_ref, bg_ref, d_ref, da_ref, dw_ref, db_ref):
        t = lax.broadcasted_iota(jnp.int32, (S, LANES), 0)
        au, ag = au_ref[...], ag_ref[...]
        hu = _conv(au, wu_ref, bu_ref, t)
        hg = _conv(ag, wg_ref, bg_ref, t)
        d = d_ref[...].astype(F32)
        dhu = d * hg * _sigmoid(hg)
        dhg = d * hu * _silu_grad(hg)
        dau, dwu, dbu = _conv_bwd(au, dhu, wu_ref, t)
        dag, dwg, dbg = _conv_bwd(ag, dhg, wg_ref, t)
        da_ref[:, :LANES] = dau.astype(BF16)
        da_ref[:, LANES:] = dag.astype(BF16)
        for k in range(K):
            dw_ref[k:k + 1, :LANES] = dwu[k]
            dw_ref[k:k + 1, LANES:] = dwg[k]
        db_ref[:, :LANES] = dbu
        db_ref[:, LANES:] = dbg

    return _call(body, name="ffn_mid_bwd", out_shape=(_sds((S, F2), BF16), _sds((K, F2)), _sds((1, F2))), grid=(nf,),
                 in_specs=_ffn_mid_specs(S, K, layer) + [pl.BlockSpec((S, LANES), lambda j: (0, j))],
                 out_specs=(pl.BlockSpec((S, 2 * LANES), lambda j: (0, j)), pl.BlockSpec((K, 2 * LANES), lambda j: (0, j)),
                            pl.BlockSpec((1, 2 * LANES), lambda j: (0, j))))(a, a, cw, cw, cb[:, None], cb[:, None], dact)


def _conv_silu_fwd(x, cw, cb):
    S = x.shape[0]
    K, C = cw.shape

    def body(x_ref, w_ref, b_ref, o_ref):
        t = lax.broadcasted_iota(jnp.int32, (S, LANES), 0)
        y = _conv(x_ref[...], w_ref, b_ref, t)
        o_ref[...] = y * _sigmoid(y)

    col = pl.BlockSpec((S, LANES), lambda j: (0, j))
    return _call(body, name="conv_silu_fwd", out_shape=_sds((S, C)), grid=(C // LANES,),
                 in_specs=[col, pl.BlockSpec((K, LANES), lambda j: (0, j)), pl.BlockSpec((1, LANES), lambda j: (0, j))],
                 out_specs=col)(x, cw, cb)


def _conv_silu_bwd(x, cw, cb, douts):
    S = x.shape[0]
    K, C = cw.shape
    starts, off = [], 0
    for d in douts:
        starts.append(off)
        off += d.shape[1] // LANES
    assert off == C // LANES

    def body(x_ref, w_ref, b_ref, *rest):
        d_refs, (dx_ref, dw_ref, db_ref) = rest[:len(douts)], rest[len(douts):]
        j = pl.program_id(0)
        t = lax.broadcasted_iota(jnp.int32, (S, LANES), 0)
        x = x_ref[...]
        y = _conv(x, w_ref, b_ref, t)
        d = d_refs[0][...]
        for i in range(1, len(douts)):
            d = jnp.where(j >= starts[i], d_refs[i][...], d)
        dy = d * _silu_grad(y)
        dx, dw, db = _conv_bwd(x, dy, w_ref, t)
        dx_ref[...] = dx.astype(BF16)
        for k in range(K):
            dw_ref[k:k + 1, :] = dw[k]
        db_ref[...] = db

    col = pl.BlockSpec((S, LANES), lambda j: (0, j))
    wsp = pl.BlockSpec((K, LANES), lambda j: (0, j))
    bsp = pl.BlockSpec((1, LANES), lambda j: (0, j))

    def dspec(i):
        nblk = douts[i].shape[1] // LANES
        return pl.BlockSpec((S, LANES), lambda j: (0, jnp.clip(j - starts[i], 0, nblk - 1)))

    return _call(body, name="conv_silu_bwd", out_shape=(_sds((S, C), BF16), _sds((K, C)), _sds((1, C))), grid=(C // LANES,),
                 in_specs=[col, wsp, bsp] + [dspec(i) for i in range(len(douts))],
                 out_specs=(col, wsp, bsp))(x, cw, cb, *douts)


HI = lax.Precision.HIGHEST


def _ssd_prep_fwd(proj, col0, bias_g, alog_g):
    S = proj.shape[0]
    nc = S // SSM_L
    b0 = col0 // LANES

    def body(raw_ref, b_ref, al_ref, pre_ref, dt_ref, acs_ref, acst_ref):
        r_i = lax.broadcasted_iota(jnp.int32, (LANES, LANES), 0)
        c_i = lax.broadcasted_iota(jnp.int32, (LANES, LANES), 1)
        live = c_i < SSM_R
        tril = jnp.where(r_i >= c_i, 1.0, 0.0)
        raw = raw_ref[...]
        for g in range(SSM_G):
            sel = jnp.where((r_i == SSM_R * g + c_i) & live, 1.0, 0.0)
            pre = jnp.dot(raw, sel, preferred_element_type=F32, precision=HI) + b_ref[g]
            dt = jnp.where(live, jnp.logaddexp(pre, 0.0), 0.0)
            a = dt * (-jnp.exp(al_ref[g]))
            acs = jnp.dot(tril, a, preferred_element_type=F32, precision=HI)
            pre_ref[g] = pre
            dt_ref[g] = dt
            acs_ref[g] = acs
            acst_ref[g] = acs.T

    gsp = pl.BlockSpec((SSM_G, 1, LANES), lambda c: (0, 0, 0))
    blk = pl.BlockSpec((SSM_G, SSM_L, LANES), lambda c: (0, c, 0))
    big = _sds((SSM_G, S, LANES))
    return _call(body, name="ssd_prep_fwd", out_shape=(big, big, big, _sds((SSM_G, LANES, S))), grid=(nc,),
                 in_specs=[pl.BlockSpec((SSM_L, LANES), lambda c: (c, b0)), gsp, gsp],
                 out_specs=(blk, blk, blk, pl.BlockSpec((SSM_G, LANES, SSM_L), lambda c: (0, 0, c))))(proj, bias_g, alog_g)


def _ssd_prep_bwd(pre_g, dt_g, alog_g, ddt_g, dacs_g, dacst_g):
    S = pre_g.shape[1]
    nc = S // SSM_L

    def body(pre_ref, dt_ref, al_ref, ddt_ref, dacs_ref, dacst_ref, draw_ref, db_ref, dal_ref):
        c = pl.program_id(0)
        r_i = lax.broadcasted_iota(jnp.int32, (LANES, LANES), 0)
        c_i = lax.broadcasted_iota(jnp.int32, (LANES, LANES), 1)
        live = c_i < SSM_R
        triu = jnp.where(r_i <= c_i, 1.0, 0.0)

        @pl.when(c == 0)
        def _():
            db_ref[...] = jnp.zeros_like(db_ref)
            dal_ref[...] = jnp.zeros_like(dal_ref)

        draw = jnp.zeros((SSM_L, LANES), F32)
        for g in range(SSM_G):
            dacs = dacs_ref[g] + dacst_ref[g].T
            da = jnp.dot(triu, dacs, preferred_element_type=F32, precision=HI)
            A = -jnp.exp(al_ref[g])
            ddt = ddt_ref[g] + da * A
            dpre = jnp.where(live, ddt * _sigmoid(pre_ref[g]), 0.0)
            unsel = jnp.where((c_i == SSM_R * g + r_i) & (r_i < SSM_R), 1.0, 0.0)
            draw = draw + jnp.dot(dpre, unsel, preferred_element_type=F32, precision=HI)
            db_ref[g] += jnp.sum(dpre, axis=0, keepdims=True)
            dal_ref[g] += jnp.where(live[:1], jnp.sum(da * dt_ref[g], axis=0, keepdims=True) * A, 0.0)
        draw_ref[...] = draw

    gsp = pl.BlockSpec((SSM_G, 1, LANES), lambda c: (0, 0, 0))
    blk = pl.BlockSpec((SSM_G, SSM_L, LANES), lambda c: (0, c, 0))
    gout = _sds((SSM_G, 1, LANES))
    return _call(body, name="ssd_prep_bwd", out_shape=(_sds((S, LANES)), gout, gout), grid=(nc,),
                 in_specs=[blk, blk, gsp, blk, blk, pl.BlockSpec((SSM_G, LANES, SSM_L), lambda c: (0, 0, c))],
                 out_specs=(pl.BlockSpec((SSM_L, LANES), lambda c: (c, 0)), gsp, gsp))(pre_g, dt_g, alog_g, ddt_g, dacs_g, dacst_g)


NT = (((1,), (1,)), ((), ()))
TN = (((0,), (0,)), ((), ()))
SSM_HP = SSM_R * SSM_P


def _ssd_group_terms(xs_ref, dt_ref, acs_ref, d_ref):
    hid = lax.broadcasted_iota(jnp.int32, (1, SSM_HP), 1) // SSM_P
    rid = lax.broadcasted_iota(jnp.int32, (SSM_HP, 1), 0) // SSM_P

    def widen(cols):
        out = cols[0]
        for r in range(1, SSM_R):
            out = jnp.where(hid == r, cols[r], out)
        return out

    dt_c = [dt_ref[0, :, r:r + 1] for r in range(SSM_R)]
    acs_c = [acs_ref[0, :, r:r + 1] for r in range(SSM_R)]
    last = [acs_ref[0, SSM_L - 1:SSM_L, r:r + 1] for r in range(SSM_R)]
    decay_c = [jnp.exp(last[r] - acs_c[r]) for r in range(SSM_R)]
    cd = [jnp.exp(last[r]) for r in range(SSM_R)]
    cd_rows = cd[0]
    for r in range(1, SSM_R):
        cd_rows = jnp.where(rid == r, cd[r], cd_rows)
    xs = xs_ref[...]
    return (xs, xs * widen(dt_c), widen([jnp.exp(a) for a in acs_c]), widen(decay_c),
            widen([d_ref[0, :, r:r + 1] for r in range(SSM_R)]), cd_rows, dt_c, decay_c, cd)


def _ssd_lmat(acs_ref, acst_ref, r, tril):
    return jnp.exp(jnp.where(tril, acs_ref[0, :, r:r + 1] - acst_ref[0, r:r + 1, :], -jnp.inf))


def _ssd_specs(rev, nc):
    def cc(c):
        return nc - 1 - c if rev else c
    xs = pl.BlockSpec((SSM_L, SSM_HP), lambda g, c: (cc(c), g))
    bsp = pl.BlockSpec((SSM_L, SSM_N), lambda g, c: (cc(c), (SSM_G * SSM_HP) // SSM_N + g))
    csp = pl.BlockSpec((SSM_L, SSM_N), lambda g, c: (cc(c), (SSM_G * SSM_HP) // SSM_N + SSM_G + g))
    sc = pl.BlockSpec((1, SSM_L, LANES), lambda g, c: (g, cc(c), 0))
    sct = pl.BlockSpec((1, LANES, SSM_L), lambda g, c: (g, 0, cc(c)))
    gsp = pl.BlockSpec((1, 1, LANES), lambda g, c: (g, 0, 0))
    st = pl.BlockSpec((None, None, SSM_HP, SSM_N), lambda g, c: (cc(c), g, 0, 0))
    return xs, bsp, csp, sc, sct, gsp, st


def _ssd_fwd(xbc, dt_g, acs_g, acst_g, d_g):
    S = xbc.shape[0]
    nc = S // SSM_L
    xs_s, b_s, c_s, sc, sct, gsp, st = _ssd_specs(False, nc)

    def body(xs_ref, b_ref, c_ref, dt_ref, acs_ref, acst_ref, d_ref, y_ref, st_ref, state):
        c = pl.program_id(1)

        @pl.when(c == 0)
        def _():
            state[...] = jnp.zeros_like(state)

        Bb, Cb = b_ref[...].astype(BF16), c_ref[...].astype(BF16)
        Gm = lax.dot_general(Cb, Bb, NT, preferred_element_type=F32)
        tril = lax.broadcasted_iota(jnp.int32, (SSM_L, SSM_L), 0) >= lax.broadcasted_iota(jnp.int32, (SSM_L, SSM_L), 1)
        xs, X, e_all, decay_all, d_all, cd_rows, _, _, _ = _ssd_group_terms(xs_ref, dt_ref, acs_ref, d_ref)
        S_all = state[...]
        st_ref[...] = S_all
        y_ref[...] = e_all * lax.dot_general(Cb, S_all.astype(BF16), NT, preferred_element_type=F32) + d_all * xs
        for r in range(SSM_R):
            sl = slice(r * SSM_P, (r + 1) * SSM_P)
            M = Gm * _ssd_lmat(acs_ref, acst_ref, r, tril)
            y_ref[:, sl] += jnp.dot(M.astype(BF16), X[:, sl].astype(BF16), preferred_element_type=F32)
        state[...] = S_all * cd_rows + lax.dot_general((X * decay_all).astype(BF16), Bb, TN, preferred_element_type=F32)

    return _call(body, name="ssd_fwd",
                 out_shape=(_sds((S, SSM_G * SSM_HP)), _sds((nc, SSM_G, SSM_HP, SSM_N))),
                 grid=(SSM_G, nc), in_specs=[xs_s, b_s, c_s, sc, sc, sct, gsp],
                 out_specs=(xs_s, pl.BlockSpec((None, None, SSM_HP, SSM_N), lambda g, c: (c, g, 0, 0))),
                 scratch=[pltpu.VMEM((SSM_HP, SSM_N), F32)])(xbc, xbc, xbc, dt_g, acs_g, acst_g, d_g)


def _ssd_bwd(xbc, dt_g, acs_g, acst_g, d_g, states, dy):
    S = xbc.shape[0]
    nc = S // SSM_L
    xs_s, b_s, c_s, sc, sct, gsp, st = _ssd_specs(True, nc)
    bc_out = pl.BlockSpec((SSM_L, SSM_N), lambda g, c: (nc - 1 - c, g))

    def body(xs_ref, b_ref, c_ref, dt_ref, acs_ref, acst_ref, d_ref, st_ref, dy_ref,
             dxs_ref, db_ref, dc_ref, ddt_ref, dacs_ref, dacst_ref, dd_ref, dstate):
        c = pl.program_id(1)

        @pl.when(c == 0)
        def _():
            dstate[...] = jnp.zeros_like(dstate)
            dd_ref[...] = jnp.zeros_like(dd_ref)

        Bb, Cb = b_ref[...].astype(BF16), c_ref[...].astype(BF16)
        Gm = lax.dot_general(Cb, Bb, NT, preferred_element_type=F32)
        tril = lax.broadcasted_iota(jnp.int32, (SSM_L, SSM_L), 0) >= lax.broadcasted_iota(jnp.int32, (SSM_L, SSM_L), 1)
        lane = lax.broadcasted_iota(jnp.int32, (1, LANES), 1)
        subl = lax.broadcasted_iota(jnp.int32, (LANES, 1), 0)
        last_row = lax.broadcasted_iota(jnp.int32, (SSM_L, 1), 0) == SSM_L - 1
        xs, X, e_all, decay_all, d_all, cd_rows, dt_c, decay_c, cd = _ssd_group_terms(xs_ref, dt_ref, acs_ref, d_ref)
        S_all, dSn_all, dY = st_ref[...], dstate[...], dy_ref[...]
        Sb, dSnb = S_all.astype(BF16), dSn_all.astype(BF16)
        T = lax.dot_general(Cb, Sb, NT, preferred_element_type=F32)
        dT = (dY * e_all).astype(BF16)
        dC = jnp.dot(dT, Sb, preferred_element_type=F32)
        dS_prev = lax.dot_general(dT, Cb, TN, preferred_element_type=F32)
        yo_dy = dY * (e_all * T)
        W = lax.dot_general(Bb, dSnb, NT, preferred_element_type=F32)
        dB = jnp.dot((X * decay_all).astype(BF16), dSnb, preferred_element_type=F32)
        xw = X * W
        dcd_rows = jnp.sum(dSn_all * S_all, axis=1, keepdims=True)
        dstate[...] = dS_prev + dSn_all * cd_rows
        dX_state = W * decay_all
        dG = jnp.zeros((SSM_L, SSM_L), F32)
        ddt_blk = jnp.zeros((SSM_L, LANES), F32)
        dacs_blk = jnp.zeros((SSM_L, LANES), F32)
        dacst_blk = jnp.zeros((LANES, SSM_L), F32)
        dd_row = jnp.zeros((1, LANES), F32)
        for r in range(SSM_R):
            sl = slice(r * SSM_P, (r + 1) * SSM_P)
            Lm = _ssd_lmat(acs_ref, acst_ref, r, tril)
            M = Gm * Lm
            dYh, xs_h = dY[:, sl], xs[:, sl]
            dYb = dYh.astype(BF16)
            dM = lax.dot_general(dYb, X[:, sl].astype(BF16), NT, preferred_element_type=F32)
            dX = lax.dot_general(M.astype(BF16), dYb, TN, preferred_element_type=F32) + dX_state[:, sl]
            dG = dG + dM * Lm
            dseg = dM * M
            dd = jnp.sum(xw[:, sl], axis=1, keepdims=True) * decay_c[r]
            dcd = jnp.sum(dcd_rows[sl])
            dacs_col = (jnp.sum(dseg, axis=1, keepdims=True) + jnp.sum(yo_dy[:, sl], axis=1, keepdims=True) - dd
                        + jnp.where(last_row, dcd * cd[r] + jnp.sum(dd), 0.0))
            dacs_row = -jnp.sum(dseg, axis=0, keepdims=True)
            dxs_ref[:, sl] = dX * dt_c[r] + d_all[:, sl] * dYh
            ddt_blk = ddt_blk + jnp.where(lane == r, jnp.sum(dX * xs_h, axis=1, keepdims=True), 0.0)
            dacs_blk = dacs_blk + jnp.where(lane == r, dacs_col, 0.0)
            dacst_blk = dacst_blk + jnp.where(subl == r, dacs_row, 0.0)
            dd_row = dd_row + jnp.where(lane == r, jnp.sum(dYh * xs_h), 0.0)
        dGb = dG.astype(BF16)
        dc_ref[...] = dC + jnp.dot(dGb, Bb, preferred_element_type=F32)
        db_ref[...] = dB + lax.dot_general(dGb, Cb, TN, preferred_element_type=F32)
        ddt_ref[0] = ddt_blk
        dacs_ref[0] = dacs_blk
        dacst_ref[0] = dacst_blk
        dd_ref[0] += dd_row

    big = _sds((SSM_G, S, LANES))
    return _call(body, name="ssd_bwd",
                 out_shape=(_sds((S, SSM_G * SSM_HP)), _sds((S, SSM_G * SSM_N)), _sds((S, SSM_G * SSM_N)),
                            big, big, _sds((SSM_G, LANES, S)), _sds((SSM_G, 1, LANES))),
                 grid=(SSM_G, nc), in_specs=[xs_s, b_s, c_s, sc, sc, sct, gsp, st, xs_s],
                 out_specs=(xs_s, bc_out, bc_out, sc, sc, sct, gsp),
                 scratch=[pltpu.VMEM((SSM_HP, SSM_N), F32)])(xbc, xbc, xbc, dt_g, acs_g, acst_g, d_g, states, dy)


def _gate_norm_fwd(y, proj, w):
    S, DI = y.shape
    tm = _tile(S, 256)

    def body(y_ref, z_ref, w_ref, o_ref):
        z = z_ref[...]
        gn = y_ref[...] * (z * _sigmoid(z))
        r = lax.rsqrt(jnp.mean(gn * gn, axis=-1, keepdims=True) + SSM_NORM_EPS)
        o_ref[...] = (gn * r * w_ref[...]).astype(BF16)

    row = pl.BlockSpec((tm, DI), lambda i: (i, 0))
    return _call(body, name="gate_norm_fwd", out_shape=_sds((S, DI), BF16), grid=(S // tm,),
                 in_specs=[row, row, pl.BlockSpec((1, DI), lambda i: (0, 0))], out_specs=row)(y, proj, w)


def _gate_norm_bwd(y, proj, w, dout):
    S, DI = y.shape
    tm = _tile(S, 256)

    def body(y_ref, z_ref, w_ref, d_ref, dy_ref, dz_ref, dw_ref):
        z, yv = z_ref[...], y_ref[...]
        sz = z * _sigmoid(z)
        dgn, dw = _norm_bwd_math(yv * sz, w_ref[...], d_ref[...].astype(F32), SSM_NORM_EPS)
        dy_ref[...] = dgn * sz
        dz_ref[...] = (dgn * yv * _silu_grad(z)).astype(BF16)

        @pl.when(pl.program_id(0) == 0)
        def _():
            dw_ref[...] = jnp.zeros_like(dw_ref)

        dw_ref[...] += dw

    row = pl.BlockSpec((tm, DI), lambda i: (i, 0))
    vec = pl.BlockSpec((1, DI), lambda i: (0, 0))
    return _call(body, name="gate_norm_bwd", out_shape=(_sds((S, DI)), _sds((S, DI), BF16), _sds((1, DI))), grid=(S // tm,),
                 in_specs=[row, row, vec, row], out_specs=(row, row, vec))(y, proj, w, dout)


def _group_major(v):
    return jnp.pad(v.reshape(SSM_G, 1, SSM_R), ((0, 0), (0, 0), (0, LANES - SSM_R)))


def _ungroup(t):
    return t[:, :SSM_R].reshape(1, SSM_G * SSM_R)


def _ffn_fwd(x, P, l, need):
    need(f"ffn{l}_up", x)
    h = _rmsnorm(x, P["norm_ffn"][l:l + 1], name=f"ffn{l}_norm")
    a = _mm(h, P[f"ffn_w_up{l}"], name=f"ffn{l}_up")
    need(f"ffn{l}_down", a)
    act = _ffn_mid_fwd(a, P["ffn_conv_w"], P["ffn_conv_b"], l)
    out = _mm(act, P[f"ffn_w_down{l}"], res=x, name=f"ffn{l}_down")
    return out, (x, h, a, act)


def _ffn_bwd(saved, P, l, dx, emit):
    x, h, a, act = saved
    dact = _mm(dx, P[f"ffn_w_down{l}"], tb=True, out_dtype=BF16, name=f"ffn{l}_down_dx")
    dw_down = _mm(act, dx, ta=True, out_dtype=PAYLOAD, name=f"ffn{l}_down_dw")
    da, dcw, dcb = _ffn_mid_bwd(a, P["ffn_conv_w"], P["ffn_conv_b"], l, dact)
    dw_up = _mm(h, da, ta=True, out_dtype=PAYLOAD, name=f"ffn{l}_up_dw")
    tie = emit(f"ffn{l}", {"ffn_w_up": dw_up, "ffn_w_down": dw_down})
    dh = _mm(da, P[f"ffn_w_up{l}"], tb=True, name=f"ffn{l}_up_dx")
    dx_in, dnw = _rmsnorm_bwd(x, P["norm_ffn"][l:l + 1], dh, dx, name=f"ffn{l}_norm_bwd", after=tie)
    return dx_in, dnw, dcw, dcb


def _local_step(x, positions, target, P, need, emit, after=None):
    S, D = x.shape
    inv_freq = ROPE_THETA ** (-jnp.arange(0, HEAD_DIM, 2, dtype=F32) / HEAD_DIM)
    inv_freq = jnp.tile(inv_freq, LANES // (HEAD_DIM // 2)).reshape(1, LANES)
    cos, sin = _rope_tables(positions, inv_freq)

    nm0 = P["norm_mix"][0:1]
    h0 = _rmsnorm(x, nm0, name="mix_norm", after=after)
    proj0 = _mm(h0, P["mix_w_in"], name="mix_in")
    cat0 = _attn_fwd(proj0, cos, sin, P["attn_sinks"], _pool_fwd(proj0, P["pool_w"][0], P["pool_scale"]))
    need("mix_out", cat0)
    x1 = _mm(cat0, P["mix_w_out"], res=x, name="mix_out")
    x2, ffn0 = _ffn_fwd(x1, P, 0, need)

    nm1 = P["norm_mix"][1:2]
    need("ssm", x2)
    h1 = _rmsnorm(x2, nm1, name="ssm_norm_in")
    z = _mm(h1, P["ssm_wz"], name="ssm_in_z")
    xbcp = _mm(h1, P["ssm_wxbc"], name="ssm_in_xbc")
    dtraw = _mm(h1, P["ssm_wdt"], name="ssm_in_dt")
    xbc = _conv_silu_fwd(xbcp, P["ssm_conv_w"], P["ssm_conv_b"])
    bias_g, alog_g, d_g = _group_major(P["ssm_dt_bias"]), _group_major(P["ssm_A_log"]), _group_major(P["ssm_D"])
    pre_g, dt_g, acs_g, acst_g = _ssd_prep_fwd(dtraw, 0, bias_g, alog_g)
    y, states = _ssd_fwd(xbc, dt_g, acs_g, acst_g, d_g)
    yn = _gate_norm_fwd(y, z, P["ssm_norm"])
    x3 = _mm(yn, P["ssm_w_out"], res=x2, name="ssm_out")
    x4, ffn1 = _ffn_fwd(x3, P, 1, need)

    loss, dx, d_norm_final = _final_loss(x4, P["norm_final"].reshape(1, D), target, name="final_loss")
    dx, dnf1, dcw1, dcb1 = _ffn_bwd(ffn1, P, 1, dx, emit)
    dyn = _mm(dx, P["ssm_w_out"], tb=True, out_dtype=BF16, name="ssm_out_dx")
    d_w_out1 = _mm(yn, dx, ta=True, out_dtype=PAYLOAD, name="ssm_out_dw")
    dy, dz, d_ssm_norm = _gate_norm_bwd(y, z, P["ssm_norm"], dyn)
    dxs, dB, dC, ddt_g, dacs_g, dacst_g, dd_g = _ssd_bwd(xbc, dt_g, acs_g, acst_g, d_g, states, dy)
    draw, dbias_g, dalog_g = _ssd_prep_bwd(pre_g, dt_g, alog_g, ddt_g, dacs_g, dacst_g)
    dxbc, d_conv_w1, d_conv_b1 = _conv_silu_bwd(xbcp, P["ssm_conv_w"], P["ssm_conv_b"], [dxs, dB, dC])
    d_wz = _mm(h1, dz, ta=True, out_dtype=PAYLOAD, name="ssm_in_z_dw")
    d_wxbc = _mm(h1, dxbc, ta=True, out_dtype=PAYLOAD, name="ssm_in_xbc_dw")
    d_wdt = _mm(h1, draw, ta=True, out_dtype=PAYLOAD, name="ssm_in_dt_dw")
    tie = emit("ssm", {"ssm_wz": d_wz, "ssm_wxbc": d_wxbc, "ssm_wdt": d_wdt, "ssm_w_out": d_w_out1,
                       "ssm_conv_w": d_conv_w1, "ssm_conv_b": d_conv_b1, "ssm_norm": d_ssm_norm})
    dh1 = _mm(dz, P["ssm_wz"], tb=True, name="ssm_in_z_dx")
    dh1 = _mm(dxbc, P["ssm_wxbc"], tb=True, res=dh1, name="ssm_in_xbc_dx")
    dh1 = _mm(draw, P["ssm_wdt"], tb=True, res=dh1, name="ssm_in_dt_dx")
    dx, dnm1 = _rmsnorm_bwd(x2, nm1, dh1, dx, name="ssm_norm_in_bwd", after=tie)
    dx, dnf0, dcw0, dcb0 = _ffn_bwd(ffn0, P, 0, dx, emit)
    dcat = _mm(dx, P["mix_w_out"], tb=True, name="mix_out_dx")
    d_w_out0 = _mm(cat0, dx, ta=True, out_dtype=PAYLOAD, name="mix_out_dw")
    dproj0, dsk = _attn_bwd(proj0, cos, sin, P["attn_sinks"], dcat)
    dproj0, d_pool_w, d_pool_scale = _pool_bwd(proj0, P["pool_w"][0], P["pool_scale"], dcat, dproj0)
    d_w_in0 = _mm(h0, dproj0, ta=True, out_dtype=PAYLOAD, name="mix_in_dw")
    tie = emit("mix", {"mix_w_in": d_w_in0, "mix_w_out": d_w_out0, "ffn_conv_w": jnp.stack([dcw0, dcw1])})
    dh0 = _mm(dproj0, P["mix_w_in"], tb=True, name="mix_in_dx")
    grad_x, dnm0 = _rmsnorm_bwd(x, nm0, dh0, dx, name="mix_norm_bwd", after=tie)

    small = {
        "norm_mix": jnp.concatenate([dnm0, dnm1], axis=0),
        "norm_ffn": jnp.concatenate([dnf0, dnf1], axis=0),
        "norm_final": d_norm_final,
        "pool_w": d_pool_w,
        "pool_scale": d_pool_scale,
        "attn_sinks_rows": dsk,
        "ssm_dt_bias_g": dbias_g, "ssm_A_log_g": dalog_g, "ssm_D_g": dd_g,
        "ffn_conv_b": jnp.concatenate([dcb0, dcb1], axis=0),
    }
    return loss, grad_x, small


def _peer(k):
    x, y, c = lax.axis_index("x"), lax.axis_index("y"), lax.axis_index("c")
    px = 1 - x if k & 4 else x
    py = 1 - y if k & 2 else y
    pc = 1 - c if k & 1 else c
    return (px, py, pc), 4 * px + 2 * py + pc


def _my_index():
    return 4 * lax.axis_index("x") + 2 * lax.axis_index("y") + lax.axis_index("c")


def _land_sds(a, mode, gather):
    if mode == "slab":
        return _sds(((N_DEV,) + a.shape) if gather else a.shape, a.dtype)
    assert mode == "rows", mode
    return _sds((N_DEV * a.shape[0],) + a.shape[1:] if gather else (N_DEV, a.shape[0] // N_DEV) + a.shape[1:], a.dtype)


def _part(ref, mode, shape, idx):
    if mode == "slab":
        return ref.at[idx]
    r = shape[0] // N_DEV
    return ref.at[pl.ds(idx * r, r)]


def _own_copies(ops, gather, srcs, lands, sems):
    me = _my_index()
    out = []
    for i, (a, mode) in enumerate(ops):
        s = srcs[i] if gather else _part(srcs[i], mode, a.shape, me)
        d = _part(lands[i], mode, _land_sds(a, mode, gather).shape, me) if gather else lands[i].at[me]
        out.append(pltpu.make_async_copy(s, d, sems.at[i]))
    return out


def _remote_copies(ops, gather, srcs, lands, send_sems, recv_sems):
    me = _my_index()
    n = len(ops)
    out = []
    for k in range(1, N_DEV):
        dev, idx = _peer(k)
        for i, (a, mode) in enumerate(ops):
            s = srcs[i] if gather else _part(srcs[i], mode, a.shape, idx)
            d = _part(lands[i], mode, _land_sds(a, mode, gather).shape, me) if gather else lands[i].at[me]
            out.append(pltpu.make_async_remote_copy(src_ref=s, dst_ref=d, send_sem=send_sems.at[(k - 1) * n + i],
                                                    recv_sem=recv_sems.at[(k - 1) * n + i], device_id=dev,
                                                    device_id_type=pl.DeviceIdType.MESH))
    return out


def _exchange(ops, *, gather, name):
    n = len(ops)

    def body(*refs):
        ins, outs = refs[:n], refs[n:2 * n]
        send_sems, recv_sems, local_sems = refs[2 * n:]
        copies = _own_copies(ops, gather, ins, outs, local_sems) + _remote_copies(ops, gather, ins, outs, send_sems, recv_sems)
        for cp in copies:
            cp.start()
        for cp in copies:
            cp.wait()

    return pl.pallas_call(
        body, name=name, out_shape=[_land_sds(a, m, gather) for a, m in ops], in_specs=[ANY] * n, out_specs=[ANY] * n,
        scratch_shapes=[pltpu.SemaphoreType.DMA((n * (N_DEV - 1),)), pltpu.SemaphoreType.DMA((n * (N_DEV - 1),)),
                        pltpu.SemaphoreType.DMA((n,))],
    )(*[a for a, _ in ops])


HBM = pl.BlockSpec(memory_space=pltpu.HBM)
SEM = pl.BlockSpec(memory_space=pltpu.SEMAPHORE)
SIDE_EFFECT = pltpu.SideEffectType.DATAFLOW_SIDE_EFFECTING


def _in_hbm(a):
    return pltpu.with_memory_space_constraint(a, pltpu.HBM)


def _place_own(ops, *, gather, name):
    n = len(ops)

    def zeros(k):
        return (0,) * k

    in_specs, out_specs = [], []
    for a, mode in ops:
        nd = a.ndim
        if gather and mode == "slab":
            in_specs.append(pl.BlockSpec(a.shape, lambda i, nd=nd: zeros(nd)))
            out_specs.append(pl.BlockSpec((1,) + a.shape, lambda i, nd=nd: (_my_index(),) + zeros(nd)))
        elif gather:
            in_specs.append(pl.BlockSpec(a.shape, lambda i, nd=nd: zeros(nd)))
            out_specs.append(pl.BlockSpec(a.shape, lambda i, nd=nd: (_my_index(),) + zeros(nd - 1)))
        elif mode == "slab":
            in_specs.append(pl.BlockSpec((1,) + a.shape[1:], lambda i, nd=nd: (_my_index(),) + zeros(nd - 1)))
            out_specs.append(pl.BlockSpec((1,) + a.shape[1:], lambda i, nd=nd: (_my_index(),) + zeros(nd - 1)))
        else:
            r = a.shape[0] // N_DEV
            in_specs.append(pl.BlockSpec((r,) + a.shape[1:], lambda i, nd=nd: (_my_index(),) + zeros(nd - 1)))
            out_specs.append(pl.BlockSpec((1, r) + a.shape[1:], lambda i, nd=nd: (_my_index(),) + zeros(nd)))

    def body(*refs):
        for i_ref, o_ref in zip(refs[:n], refs[n:]):
            if o_ref.shape == i_ref.shape:
                o_ref[...] = i_ref[...]
            else:
                o_ref[0] = i_ref[...]

    return _call(body, name=name, out_shape=[_land_sds(a, m, gather) for a, m in ops], grid=(1,),
                 in_specs=in_specs, out_specs=out_specs)(*[a for a, _ in ops])


def _exchange_start(groups, *, gather, name):
    sizes = [len(ops) for ops, _ in groups]
    n = sum(sizes)
    G = len(groups)

    def body(*refs):
        srcs, lands = refs[:n], refs[n:2 * n]
        sems = refs[2 * n:2 * n + 2 * G]
        token = refs[-1]
        off = 0
        for g, (ops, _) in enumerate(groups):
            for cp in _remote_copies(ops, gather, srcs[off:off + sizes[g]], lands[off:off + sizes[g]], sems[2 * g], sems[2 * g + 1]):
                cp.start()
            off += sizes[g]
        token[...] = jnp.zeros_like(token)

    srcs = [a for ops, _ in groups for a, _ in ops]
    lands = [l for _, ls in groups for l in ls]
    sem_shapes = [pltpu.SemaphoreType.DMA((s * (N_DEV - 1),)) for s in sizes for _ in range(2)]
    outs = pl.pallas_call(
        body, name=name,
        out_shape=sem_shapes + [pltpu.HBM(a.shape, a.dtype) for a in srcs + lands] + [_sds((8, LANES))],
        in_specs=[HBM] * (2 * n), out_specs=[SEM] * (2 * G) + [HBM] * (2 * n) + [pl.BlockSpec(memory_space=pltpu.VMEM)],
        input_output_aliases={i: 2 * G + i for i in range(2 * n)},
        compiler_params=pltpu.CompilerParams(has_side_effects=SIDE_EFFECT))(*[_in_hbm(a) for a in srcs + lands])
    sems, thru, token = outs[:2 * G], outs[2 * G:2 * G + 2 * n], outs[-1]
    states, off = [], 0
    for g, s in enumerate(sizes):
        states.append((sems[2 * g], sems[2 * g + 1], thru[off:off + s], thru[n + off:n + off + s]))
        off += s
    return states, token


def _exchange_wait(ops, state, after, *, gather, name):
    send_sems, recv_sems, srcs, lands = state
    n = len(ops)

    def body(*refs):
        for cp in _remote_copies(ops, gather, refs[:n], refs[n:2 * n], refs[2 * n], refs[2 * n + 1]):
            cp.wait_send()
            cp.wait_recv()

    outs = pl.pallas_call(
        body, name=name, out_shape=[pltpu.HBM(a.shape, a.dtype) for a in list(srcs) + list(lands)],
        in_specs=[HBM] * (2 * n) + [SEM, SEM, ANY], out_specs=[HBM] * (2 * n),
        input_output_aliases={i: i for i in range(2 * n)},
        compiler_params=pltpu.CompilerParams(has_side_effects=SIDE_EFFECT))(*srcs, *lands, send_sems, recv_sems, after)
    return outs[n:]


ADAM_ROWS = 256


def _row_tile(R, cap=ADAM_ROWS):
    best = R
    if R > cap:
        for d in range(16, cap + 1, 16):
            if R % d == 0:
                best = d
    return best


def _adamw(g_layers, w, m, v, *, name):
    L = len(g_layers)
    J, R, Wd = g_layers[0].shape
    assert w.shape == (L, R, Wd), (g_layers[0].shape, w.shape)
    tr = _row_tile(R)
    nrt = R // tr
    c1 = 1.0 / (1.0 - ADAM_B1 ** ADAM_STEP)
    c2 = 1.0 / (1.0 - ADAM_B2 ** ADAM_STEP)

    def body(*refs):
        g_refs = refs[:L]
        w_ref, m_ref, v_ref, go_ref, d_ref, mo_ref, vo_ref = refs[L:]
        layer = pl.program_id(0)
        g = None
        for l, g_ref in enumerate(g_refs):
            gl = g_ref[0].astype(F32)
            for j in range(1, J):
                gl = gl + g_ref[j].astype(F32)
            g = gl if g is None else jnp.where(layer == l, gl, g)
        mn = ADAM_B1 * m_ref[...] + (1.0 - ADAM_B1) * g
        vn = ADAM_B2 * v_ref[...] + (1.0 - ADAM_B2) * (g * g)
        go_ref[...] = g
        mo_ref[...] = mn
        vo_ref[...] = vn
        d_ref[...] = -ADAM_LR * ((mn * c1) / (jnp.sqrt(vn * c2) + ADAM_EPS) + ADAM_WD * w_ref[...])

    def g_spec(l):
        return pl.BlockSpec((J, tr, Wd), lambda ll, i: (0, jnp.where(ll == l, i, jnp.where(ll < l, 0, nrt - 1)), 0))

    row = pl.BlockSpec((None, tr, Wd), lambda ll, i: (ll, i, 0))
    out = _sds((L, R, Wd))
    return _call(body, name=name, out_shape=(out, out, out, out), grid=(L, nrt),
                 in_specs=[g_spec(l) for l in range(L)] + [row, row, row], out_specs=(row, row, row, row))(*g_layers, w, m, v)


def _sum_slabs(slabs, *, name):
    n = len(slabs)

    def body(*refs):
        for g_ref, o_ref in zip(refs[:n], refs[n:]):
            g = g_ref[0]
            for j in range(1, g_ref.shape[0]):
                g = g + g_ref[j]
            o_ref[...] = g

    return _call(body, name=name, out_shape=[_sds(s.shape[1:]) for s in slabs])(*slabs)


RELAYOUT_ROWS = 256


def _col_plan(n, segments):
    plan = []
    for j in range(N_DEV):
        lo, hi = j * n, (j + 1) * n
        for t0, t1, oi, o0 in segments:
            a, b = max(lo, t0), min(hi, t1)
            if a < b:
                plan.append((j, a - lo, oi, o0 + a - t0, b - a))
    return plan


def _interleave_segments(F):
    seg = []
    for b in range(F // LANES):
        seg.append((b * LANES, (b + 1) * LANES, 0, 2 * b * LANES))
        seg.append((F + b * LANES, F + (b + 1) * LANES, 0, (2 * b + 1) * LANES))
    return seg


def _interleave_perm(F):
    perm = []
    for b in range(F // LANES):
        perm += list(range(b * LANES, (b + 1) * LANES)) + list(range(F + b * LANES, F + (b + 1) * LANES))
    inv = [0] * (2 * F)
    for d, s in enumerate(perm):
        inv[s] = d
    return jnp.asarray(perm, jnp.int32), jnp.asarray(inv, jnp.int32)


def _cols_from_slabs(slabs, plan, widths, *, name):
    _, R, n = slabs.shape
    tr = _row_tile(R, RELAYOUT_ROWS)
    covered = [sum(e[4] for e in plan if e[2] == i) for i in range(len(widths))]

    def body(s_ref, *o_refs):
        for i, o_ref in enumerate(o_refs):
            if covered[i] < widths[i]:
                o_ref[...] = jnp.zeros_like(o_ref)
        for j, sc, oi, oc, w in plan:
            o_refs[oi][:, oc:oc + w] = s_ref[j, :, sc:sc + w]

    return _call(body, name=name, out_shape=[_sds((R, w), slabs.dtype) for w in widths], grid=(R // tr,),
                 in_specs=[pl.BlockSpec((N_DEV, tr, n), lambda i: (0, i, 0))],
                 out_specs=[pl.BlockSpec((tr, w), lambda i: (i, 0)) for w in widths])(slabs)


def _slabs_from_cols(mats, plan, n, *, name):
    R = mats[0].shape[0]
    tr = _row_tile(R, RELAYOUT_ROWS)

    def body(*refs):
        m_refs, o_ref = refs[:-1], refs[-1]
        for j, sc, oi, oc, w in plan:
            o_ref[j, :, sc:sc + w] = m_refs[oi][:, oc:oc + w]

    return _call(body, name=name, out_shape=_sds((N_DEV, R, n), mats[0].dtype), grid=(R // tr,),
                 in_specs=[pl.BlockSpec((tr, m.shape[1]), lambda i: (i, 0)) for m in mats],
                 out_specs=pl.BlockSpec((N_DEV, tr, n), lambda i: (0, i, 0)))(*mats)


def kernel(x, positions, norm_mix, norm_ffn, norm_final, mix_w_in, pool_w, pool_scale, attn_sinks, mix_w_out, ssm_w_in, ssm_conv_w, ssm_conv_b, ssm_dt_bias, ssm_A_log, ssm_D, ssm_norm, ssm_w_out, ffn_w_up, ffn_conv_w, ffn_conv_b, ffn_w_down, loss_target, m_norm_mix, m_norm_ffn, m_norm_final, m_mix_w_in, m_pool_w, m_pool_scale, m_attn_sinks, m_mix_w_out, m_ssm_w_in, m_ssm_conv_w, m_ssm_conv_b, m_ssm_dt_bias, m_ssm_A_log, m_ssm_D, m_ssm_norm, m_ssm_w_out, m_ffn_w_up, m_ffn_conv_w, m_ffn_conv_b, m_ffn_w_down, v_norm_mix, v_norm_ffn, v_norm_final, v_mix_w_in, v_pool_w, v_pool_scale, v_attn_sinks, v_mix_w_out, v_ssm_w_in, v_ssm_conv_w, v_ssm_conv_b, v_ssm_dt_bias, v_ssm_A_log, v_ssm_D, v_ssm_norm, v_ssm_w_out, v_ffn_w_up, v_ffn_conv_w, v_ffn_conv_b, v_ffn_w_down):
    args = dict(locals())
    wl = {n: args[n] for n in WEIGHTS}
    ml = {n: args["m_" + n] for n in WEIGHTS}
    vl = {n: args["v_" + n] for n in WEIGHTS}
    D = x.shape[2]
    F = ffn_w_down.shape[1] * N_DEV
    DI, CD, NH = ssm_norm.shape[1] * N_DEV, ssm_conv_b.shape[1] * N_DEV, ssm_dt_bias.shape[1]
    Kc, Kf = ssm_conv_w.shape[1], ffn_conv_w.shape[1]
    n_mix, n_ssm, n_up = mix_w_in.shape[2], ssm_w_in.shape[2], ffn_w_up.shape[2]
    plan_mix = _col_plan(n_mix, [(0, N_DEV * n_mix, 0, 0)])
    plan_ssm = _col_plan(n_ssm, [(0, DI, 0, 0), (DI, DI + CD, 1, 0), (DI + CD, DI + CD + NH, 2, 0)])
    plan_up = _col_plan(n_up, _interleave_segments(F))
    perm, inv = _interleave_perm(F)

    def two(a):
        return a.reshape(-1, a.shape[-1])

    def pay(a):
        return a.astype(PAYLOAD)

    order = ("mix_in", "mix_out", "ffn0_up", "ffn0_down", "ssm", "ffn1_up", "ffn1_down")
    gops = {
        "mix_in": [(pay(two(mix_w_in)), "slab")],
        "mix_out": [(pay(mix_w_out[0]), "rows"), (two(ssm_conv_w), "slab"), (ssm_conv_b, "slab"), (ssm_norm, "slab"),
                    (two(ffn_conv_w), "slab")],
        "ffn0_up": [(pay(ffn_w_up[0]), "slab")], "ffn0_down": [(pay(ffn_w_down[0]), "rows")],
        "ssm": [(pay(two(ssm_w_in)), "slab"), (pay(ssm_w_out[0]), "rows")],
        "ffn1_up": [(pay(ffn_w_up[1]), "slab")], "ffn1_down": [(pay(ffn_w_down[1]), "rows")],
    }
    lands = _place_own([op for g in order for op in gops[g]], gather=True, name="gather_own")
    groups, off = [], 0
    for g in order:
        groups.append((gops[g], lands[off:off + len(gops[g])]))
        off += len(gops[g])
    gstates, token = _exchange_start(groups, gather=True, name="gather_start")
    gstate = dict(zip(order, gstates))
    P = {n: wl[n] for n in REPLICATED}
    P["ffn_conv_b"] = jnp.take(ffn_conv_b, perm, axis=1)

    def need(g, after):
        got = _exchange_wait(gops[g], gstate[g], after, gather=True, name="gather_wait_" + g)
        if g == "mix_in":
            (P["mix_w_in"],) = _cols_from_slabs(got[0], plan_mix, (N_DEV * n_mix,), name="unpack_mix_w_in")
        elif g == "mix_out":
            P.update(mix_w_out=got[0], ssm_conv_w=got[1].transpose(1, 0, 2).reshape(Kc, CD), ssm_conv_b=got[2].reshape(1, CD),
                     ssm_norm=got[3].reshape(1, DI),
                     ffn_conv_w=jnp.take(got[4].transpose(1, 0, 2).reshape(2 * Kf, 2 * F), perm, axis=1).reshape(2, Kf, 2 * F))
        elif g == "ssm":
            P["ssm_wz"], P["ssm_wxbc"], P["ssm_wdt"] = _cols_from_slabs(got[0], plan_ssm, (DI, CD, LANES), name="unpack_ssm_w_in")
            P["ssm_w_out"] = got[1]
        elif g.endswith("_up"):
            (P["ffn_w_up" + g[3]],) = _cols_from_slabs(got[0], plan_up, (2 * F,), name="unpack_ffn_w_up" + g[3])
        else:
            P["ffn_w_down" + g[3]] = got[0]

    sent = {}

    def emit(g, d):
        if g == "mix":
            ops = [(_slabs_from_cols([d["mix_w_in"]], plan_mix, n_mix, name="pack_mix_w_in"), "slab"), (d["mix_w_out"], "rows"),
                   (jnp.take(d["ffn_conv_w"].reshape(2 * Kf, 2 * F), inv, axis=1).reshape(2 * Kf, N_DEV, n_up).transpose(1, 0, 2), "slab")]
        elif g == "ssm":
            ops = [(_slabs_from_cols([d["ssm_wz"], d["ssm_wxbc"], d["ssm_wdt"]], plan_ssm, n_ssm, name="pack_ssm_w_in"), "slab"),
                   (d["ssm_w_out"], "rows"), (d["ssm_conv_w"].reshape(Kc, N_DEV, -1).transpose(1, 0, 2), "slab"),
                   (d["ssm_conv_b"].reshape(N_DEV, 1, -1), "slab"), (d["ssm_norm"].reshape(N_DEV, 1, -1), "slab")]
        else:
            ops = [(_slabs_from_cols([d["ffn_w_up"]], plan_up, n_up, name="pack_ffn_w_up" + g[3]), "slab"), (d["ffn_w_down"], "rows")]
        own = _place_own(ops, gather=False, name="scatter_own_" + g)
        (state,), tok = _exchange_start([(ops, own)], gather=False, name="scatter_start_" + g)
        sent[g] = (ops, state)
        return tok

    need("mix_in", token)
    loss_lanes, grad_x, G = _local_step(x[0], positions.reshape(-1, 1), loss_target[0], P, need, emit, after=token)
    loss = lax.psum(loss_lanes[0, 0], ("x", "y", "c"))

    res = {}

    def update(n, g_layers):
        L = len(g_layers)
        g_layers = [g.reshape(g.shape[0], -1, g.shape[-1]) for g in g_layers]
        shape = (L,) + g_layers[0].shape[1:]
        outs = _adamw(g_layers, wl[n].reshape(shape), ml[n].reshape(shape), vl[n].reshape(shape), name="adamw_" + n)
        for kind, a in zip(("grad", "delta", "new_m", "new_v"), outs):
            res[kind, n] = a.reshape(wl[n].shape)

    recv = {g: _exchange_wait(sent[g][0], sent[g][1], grad_x, gather=False, name="scatter_wait_" + g)
            for g in ("ffn1", "ssm", "ffn0", "mix")}
    update("mix_w_in", [recv["mix"][0]])
    update("mix_w_out", [recv["mix"][1]])
    update("ffn_conv_w", [recv["mix"][2]])
    update("ssm_w_in", [recv["ssm"][0]])
    update("ssm_w_out", [recv["ssm"][1]])
    update("ssm_conv_w", [recv["ssm"][2]])
    update("ssm_conv_b", [recv["ssm"][3]])
    update("ssm_norm", [recv["ssm"][4]])
    update("ffn_w_up", [recv["ffn0"][0], recv["ffn1"][0]])
    update("ffn_w_down", [recv["ffn0"][1], recv["ffn1"][1]])

    rep = _exchange(
        [(a, "slab") for a in (G["norm_mix"], G["norm_ffn"], G["norm_final"], G["pool_w"].reshape(-1, LANES), G["pool_scale"],
                               jnp.take(G["ffn_conv_b"], inv, axis=1), G["attn_sinks_rows"],
                               G["ssm_dt_bias_g"].reshape(SSM_G, LANES), G["ssm_A_log_g"].reshape(SSM_G, LANES),
                               G["ssm_D_g"].reshape(SSM_G, LANES))],
        gather=True, name="gather_small_grads")
    for n, r in zip(("norm_mix", "norm_ffn", "norm_final", "pool_w", "pool_scale", "ffn_conv_b"), rep):
        update(n, [r])
    sinks_rows, bias_g, alog_g, d_g = _sum_slabs(rep[6:], name="sum_head_grads")
    update("attn_sinks", [sinks_rows[:, 0].reshape(1, 1, N_HEADS)])
    update("ssm_dt_bias", [_ungroup(bias_g)[None]])
    update("ssm_A_log", [_ungroup(alog_g)[None]])
    update("ssm_D", [_ungroup(d_g)[None]])

    return (loss, grad_x[None], *[res[k, n] for k in ("grad", "delta", "new_m", "new_v") for n in WEIGHTS])
```

```python
import functools
import math

import jax
import jax.numpy as jnp
from jax import lax
from jax.experimental import pallas as pl
from jax.experimental.pallas import tpu as pltpu

F32 = jnp.float32
BF16 = jnp.bfloat16

N_DEV = 8
LANES = 128
HEAD_DIM = 64
N_KV_HEADS = 2
GQ = 4
N_HEADS = N_KV_HEADS * GQ
BLOCK = 128
POOL_GROUPS = 4
ROPE_THETA = 10000.0
SSM_P = 64
SSM_G = 8
SSM_R = 4
SSM_N = 128
SSM_L = 128
NORM_EPS = 1e-6
SSM_NORM_EPS = 1e-5
ADAM_LR, ADAM_B1, ADAM_B2, ADAM_EPS, ADAM_WD, ADAM_STEP = 0.001, 0.9, 0.999, 1e-08, 0.01, 10
VMEM_LIMIT = 56 * 2 ** 20
PAYLOAD = jnp.bfloat16

REPLICATED = ("norm_mix", "norm_ffn", "norm_final", "pool_w", "pool_scale", "attn_sinks",
              "ssm_dt_bias", "ssm_A_log", "ssm_D", "ffn_conv_b")
WEIGHTS = ("norm_mix", "norm_ffn", "norm_final", "mix_w_in", "pool_w", "pool_scale", "attn_sinks", "mix_w_out",
           "ssm_w_in", "ssm_conv_w", "ssm_conv_b", "ssm_dt_bias", "ssm_A_log", "ssm_D", "ssm_norm", "ssm_w_out",
           "ffn_w_up", "ffn_conv_w", "ffn_conv_b", "ffn_w_down")


def _tile(n, cap):
    if n <= cap:
        return n
    best = None
    for d in range(LANES, cap + 1, LANES):
        if n % d == 0:
            best = d
    assert best is not None, (n, cap)
    return best


def _call(body, *, name, out_shape, grid=None, in_specs=None, out_specs=None, scratch=(), aliases=None):
    kw = {}
    if grid is not None:
        kw = dict(grid=grid, in_specs=in_specs, out_specs=out_specs)
    if aliases:
        kw["input_output_aliases"] = aliases
    return pl.pallas_call(
        body, name=name, out_shape=out_shape, scratch_shapes=list(scratch),
        compiler_params=pltpu.CompilerParams(vmem_limit_bytes=VMEM_LIMIT), **kw)


ANY = pl.BlockSpec(memory_space=pl.ANY)


def _sds(shape, dtype=F32):
    return jax.ShapeDtypeStruct(tuple(shape), dtype)


def _sigmoid(x):
    return 1.0 / (1.0 + jnp.exp(-x))


def _shift_dn(x, d, t):
    if d == 0:
        return x
    return jnp.where(t >= d, pltpu.roll(x, d, axis=0), 0.0)


def _shift_up(x, d, t):
    if d == 0:
        return x
    n = x.shape[0]
    return jnp.where(t < n - d, pltpu.roll(x, n - d, axis=0), 0.0)


def _mm(a, b, *, name, ta=False, tb=False, res=None, out_dtype=F32):
    M, K = (a.shape[1], a.shape[0]) if ta else a.shape
    N = b.shape[0] if tb else b.shape[1]
    assert (b.shape[1] if tb else b.shape[0]) == K, (a.shape, b.shape, ta, tb)
    tm, tn, tk = _tile(M, 1408), _tile(N, 1408), _tile(K, 1408)
    nk = K // tk
    dims = (((0 if ta else 1,), (1 if tb else 0,)), ((), ()))

    def body(*refs):
        a_ref, b_ref = refs[:2]
        r_ref = refs[2] if res is not None else None
        o_ref, acc = refs[-2:]
        k = pl.program_id(2)

        @pl.when(k == 0)
        def _():
            acc[...] = jnp.zeros_like(acc)

        acc[...] += lax.dot_general(a_ref[...].astype(BF16), b_ref[...].astype(BF16), dims,
                                    preferred_element_type=F32)

        @pl.when(k == nk - 1)
        def _():
            out = acc[...]
            if res is not None:
                out = out + r_ref[...]
            o_ref[...] = out.astype(out_dtype)

    a_spec = pl.BlockSpec((tk, tm), lambda i, j, k: (k, i)) if ta else pl.BlockSpec((tm, tk), lambda i, j, k: (i, k))
    b_spec = pl.BlockSpec((tn, tk), lambda i, j, k: (j, k)) if tb else pl.BlockSpec((tk, tn), lambda i, j, k: (k, j))
    o_spec = pl.BlockSpec((tm, tn), lambda i, j, k: (i, j))
    ins, specs = [a, b], [a_spec, b_spec]
    if res is not None:
        ins.append(res)
        specs.append(o_spec)
    return _call(body, name=name, out_shape=_sds((M, N), out_dtype), grid=(M // tm, N // tn, nk), in_specs=specs,
                 out_specs=o_spec, scratch=[pltpu.VMEM((tm, tn), F32)])(*ins)


def _rmsnorm(x, w, *, name, eps=NORM_EPS, after=None):
    S, D = x.shape
    tm = _tile(S, 512)
    tie = [] if after is None else [after]

    def body(x_ref, w_ref, *rest):
        o_ref = rest[-1]
        xf = x_ref[...]
        r = lax.rsqrt(jnp.mean(xf * xf, axis=-1, keepdims=True) + eps)
        o_ref[...] = (xf * r * w_ref[...]).astype(BF16)

    return _call(body, name=name, out_shape=_sds((S, D), BF16), grid=(S // tm,),
                 in_specs=[pl.BlockSpec((tm, D), lambda i: (i, 0)), pl.BlockSpec((1, D), lambda i: (0, 0))] + [ANY] * len(tie),
                 out_specs=pl.BlockSpec((tm, D), lambda i: (i, 0)))(x, w, *tie)


def _norm_bwd_math(xf, w, dh, eps):
    r = lax.rsqrt(jnp.mean(xf * xf, axis=-1, keepdims=True) + eps)
    xhat = xf * r
    dxh = dh * w
    dx = r * (dxh - xhat * jnp.mean(dxh * xhat, axis=-1, keepdims=True))
    dw = jnp.sum(dh * xhat, axis=0, keepdims=True)
    return dx, dw


def _rmsnorm_bwd(x, w, dh, dres, *, name, eps=NORM_EPS, after=None):
    S, D = x.shape
    tm = _tile(S, 512)
    tie = [] if after is None else [after]

    def body(x_ref, w_ref, dh_ref, dr_ref, *rest):
        dx_ref, dw_ref = rest[-2:]
        dx, dw = _norm_bwd_math(x_ref[...], w_ref[...], dh_ref[...].astype(F32), eps)
        dx_ref[...] = dr_ref[...] + dx

        @pl.when(pl.program_id(0) == 0)
        def _():
            dw_ref[...] = jnp.zeros_like(dw_ref)

        dw_ref[...] += dw

    row = pl.BlockSpec((tm, D), lambda i: (i, 0))
    vec = pl.BlockSpec((1, D), lambda i: (0, 0))
    return _call(body, name=name, out_shape=(_sds((S, D)), _sds((1, D))), grid=(S // tm,),
                 in_specs=[row, vec, row, row] + [ANY] * len(tie), out_specs=(row, vec))(x, w, dh, dres, *tie)


def _final_loss(x, w, target, *, name):
    S, D = x.shape
    tm = _tile(S, 512)

    def body(x_ref, w_ref, t_ref, loss_ref, dx_ref, dw_ref):
        xf, wv = x_ref[...], w_ref[...]
        r = lax.rsqrt(jnp.mean(xf * xf, axis=-1, keepdims=True) + NORM_EPS)
        err = xf * r * wv - t_ref[...]
        part = 0.5 * jnp.sum(jnp.mean(err * err, axis=-1, keepdims=True), axis=0, keepdims=True)
        dx, dw = _norm_bwd_math(xf, wv, err * (1.0 / D), NORM_EPS)
        dx_ref[...] = dx

        @pl.when(pl.program_id(0) == 0)
        def _():
            dw_ref[...] = jnp.zeros_like(dw_ref)
            loss_ref[...] = jnp.zeros_like(loss_ref)

        dw_ref[...] += dw
        loss_ref[...] += jnp.broadcast_to(part, loss_ref.shape)

    row = pl.BlockSpec((tm, D), lambda i: (i, 0))
    vec = pl.BlockSpec((1, D), lambda i: (0, 0))
    return _call(body, name=name, out_shape=(_sds((1, LANES)), _sds((S, D)), _sds((1, D))), grid=(S // tm,),
                 in_specs=[row, vec, row], out_specs=(pl.BlockSpec((1, LANES), lambda i: (0, 0)), row, vec))(x, w, target)


def _rope_tables(pos, inv_freq):
    S = pos.shape[0]
    tm = _tile(S, 512)

    def body(p_ref, f_ref, c_ref, s_ref):
        ang = p_ref[...].astype(F32) * f_ref[...]
        c_ref[...] = jnp.cos(ang)
        s_ref[...] = jnp.sin(ang)

    blk = pl.BlockSpec((tm, LANES), lambda i: (i, 0))
    return _call(body, name="rope_tables", out_shape=(_sds((S, LANES)), _sds((S, LANES))), grid=(S // tm,),
                 in_specs=[pl.BlockSpec((tm, 1), lambda i: (i, 0)), pl.BlockSpec((1, LANES), lambda i: (0, 0))],
                 out_specs=(blk, blk))(pos, inv_freq)


def _rot_half(t):
    lane = lax.broadcasted_iota(jnp.int32, t.shape, 1)
    lo = (lane % HEAD_DIM) < (HEAD_DIM // 2)
    return jnp.where(lo, -pltpu.roll(t, LANES - HEAD_DIM // 2, axis=1), pltpu.roll(t, HEAD_DIM // 2, axis=1))


def _rope(t, c, s):
    return t * c + _rot_half(t) * s


def _unrope(dy, c, s):
    return dy * c - _rot_half(dy * s)


PD = POOL_GROUPS * LANES
QD = N_HEADS * HEAD_DIM
KD = N_KV_HEADS * HEAD_DIM
assert PD % QD == 0 and (PD + QD) % (2 * KD) == 0 and KD == LANES
def _attn_probs(q, kcat, sink, mask):
    s = lax.dot_general(q.astype(BF16), kcat, (((1,), (1,)), ((), ())), preferred_element_type=F32) * (HEAD_DIM ** -0.5)
    s = jnp.where(mask, s, -jnp.inf)
    m = jnp.maximum(jnp.max(s, axis=1, keepdims=True), sink)
    p = jnp.exp(s - m)
    ps = jnp.exp(sink - m)
    inv = 1.0 / (jnp.sum(p, axis=1, keepdims=True) + ps)
    return p * inv, ps * inv


def _attn_mask(n):
    qi = lax.broadcasted_iota(jnp.int32, (BLOCK, 2 * BLOCK), 0)
    kj = lax.broadcasted_iota(jnp.int32, (BLOCK, 2 * BLOCK), 1)
    rel = qi + BLOCK - kj
    return (rel >= 0) & (rel < BLOCK) & ((n > 0) | (kj >= BLOCK))


def _attn_in_specs(nb):
    def cur(n):
        return jnp.minimum(n, nb - 1)

    def prev(n):
        return jnp.clip(n - 1, 0, nb - 1)

    kvb = (PD + QD) // (2 * KD)
    return [pl.BlockSpec(memory_space=pltpu.SMEM),
            pl.BlockSpec((BLOCK, QD), lambda n: (cur(n), PD // QD)),
            pl.BlockSpec((BLOCK, 2 * KD), lambda n: (cur(n), kvb)),
            pl.BlockSpec((BLOCK, 2 * KD), lambda n: (prev(n), kvb)),
            pl.BlockSpec((BLOCK, LANES), lambda n: (cur(n), 0)), pl.BlockSpec((BLOCK, LANES), lambda n: (cur(n), 0)),
            pl.BlockSpec((BLOCK, LANES), lambda n: (prev(n), 0)), pl.BlockSpec((BLOCK, LANES), lambda n: (prev(n), 0))]


def _attn_keys(kvc_ref, kvp_ref, cc, sc, cp, sp):
    kc = _rope(kvc_ref[:, :KD], cc, sc)
    kp = _rope(kvp_ref[:, :KD], cp, sp)
    vc, vp = kvc_ref[:, KD:], kvp_ref[:, KD:]
    kcat, vcat = [], []
    for kk in range(N_KV_HEADS):
        sl = slice(kk * HEAD_DIM, (kk + 1) * HEAD_DIM)
        kcat.append(jnp.concatenate([kp[:, sl], kc[:, sl]], axis=0).astype(BF16))
        vcat.append(jnp.concatenate([vp[:, sl], vc[:, sl]], axis=0).astype(BF16))
    return kcat, vcat


def _attn_fwd(proj, cos, sin, sinks, cat):
    S = proj.shape[0]
    nb = S // BLOCK

    def body(sink_ref, q_ref, kvc_ref, kvp_ref, cc_ref, sc_ref, cp_ref, sp_ref, cat_ref, o_ref):
        n = pl.program_id(0)
        cc, sc = cc_ref[...], sc_ref[...]
        kcat, vcat = _attn_keys(kvc_ref, kvp_ref, cc, sc, cp_ref[...], sp_ref[...])
        mask = _attn_mask(n)
        for j in range(QD // LANES):
            qr = _rope(q_ref[:, j * LANES:(j + 1) * LANES], cc, sc)
            for e in range(LANES // HEAD_DIM):
                h = j * (LANES // HEAD_DIM) + e
                pn, _ = _attn_probs(qr[:, e * HEAD_DIM:(e + 1) * HEAD_DIM], kcat[h // GQ], sink_ref[0, h], mask)
                o_ref[:, h * HEAD_DIM:(h + 1) * HEAD_DIM] = jnp.dot(pn.astype(BF16), vcat[h // GQ], preferred_element_type=F32)

    return _call(body, name="attn_fwd", out_shape=_sds(cat.shape), grid=(nb,),
                 in_specs=_attn_in_specs(nb) + [ANY], out_specs=pl.BlockSpec((BLOCK, QD), lambda n: (n, PD // QD)),
                 aliases={8: 0})(sinks, proj, proj, proj, cos, sin, cos, sin, cat)


def _attn_bwd(proj, cos, sin, sinks, dcat):
    S = proj.shape[0]
    nb = S // BLOCK
    scale = HEAD_DIM ** -0.5
    per = LANES // HEAD_DIM

    def body(sink_ref, q_ref, kvc_ref, kvp_ref, cc_ref, sc_ref, cp_ref, sp_ref, do_ref, o_ref, ds_ref, hold, carry, part, pair):
        n = pl.program_id(0)

        @pl.when(n == 0)
        def _():
            hold[...] = jnp.zeros_like(hold)
            carry[...] = jnp.zeros_like(carry)
            ds_ref[...] = jnp.zeros_like(ds_ref)

        live = jnp.where(n < nb, 1.0, 0.0)
        cc, sc, cp, sp = cc_ref[...], sc_ref[...], cp_ref[...], sp_ref[...]
        kcat, vcat = _attn_keys(kvc_ref, kvp_ref, cc, sc, cp, sp)
        mask = _attn_mask(n)
        o_ref[:, :PD] = jnp.zeros((BLOCK, PD), F32)
        o_ref[:, PD:PD + QD] = hold[...]
        dk = [jnp.zeros((2 * BLOCK, HEAD_DIM), F32) for _ in range(N_KV_HEADS)]
        dv = [jnp.zeros((2 * BLOCK, HEAD_DIM), F32) for _ in range(N_KV_HEADS)]
        row = lax.broadcasted_iota(jnp.int32, (8, LANES), 0)
        dsk = jnp.zeros((8, LANES), F32)
        for j in range(QD // LANES):
            qr = _rope(q_ref[:, j * LANES:(j + 1) * LANES], cc, sc)
            for e in range(per):
                h = j * per + e
                kk = h // GQ
                qh = qr[:, e * HEAD_DIM:(e + 1) * HEAD_DIM]
                pn, psn = _attn_probs(qh, kcat[kk], sink_ref[0, h], mask)
                doh = (do_ref[:, h * HEAD_DIM:(h + 1) * HEAD_DIM] * live).astype(BF16)
                dp = lax.dot_general(doh, vcat[kk], NT, preferred_element_type=F32)
                delta = jnp.sum(pn * dp, axis=1, keepdims=True)
                ds = (pn * (dp - delta) * scale).astype(BF16)
                pair[:, e * HEAD_DIM:(e + 1) * HEAD_DIM] = jnp.dot(ds, kcat[kk], preferred_element_type=F32)
                dk[kk] = dk[kk] + lax.dot_general(ds, qh.astype(BF16), TN, preferred_element_type=F32)
                dv[kk] = dv[kk] + lax.dot_general(pn.astype(BF16), doh, TN, preferred_element_type=F32)
                dsk = dsk + jnp.where(row == h, -jnp.sum(psn * delta), 0.0)
            hold[:, j * LANES:(j + 1) * LANES] = _unrope(pair[...], cc, sc)
        for kk in range(N_KV_HEADS):
            sl = slice(kk * HEAD_DIM, (kk + 1) * HEAD_DIM)
            sv = slice(KD + kk * HEAD_DIM, KD + (kk + 1) * HEAD_DIM)
            part[0, :, sl] = dk[kk][:BLOCK]
            part[0, :, sv] = dv[kk][:BLOCK]
            part[1, :, sl] = dk[kk][BLOCK:]
            part[1, :, sv] = dv[kk][BLOCK:]
        done = carry[...] + part[0]
        o_ref[:, PD + QD:PD + QD + KD] = _unrope(done[:, :KD], cp, sp)
        o_ref[:, PD + QD + KD:] = done[:, KD:]
        carry[...] = part[1]
        ds_ref[...] += dsk

    return _call(body, name="attn_bwd", out_shape=(_sds((S, PD + QD + 2 * KD)), _sds((8, LANES))), grid=(nb + 1,),
                 in_specs=_attn_in_specs(nb) + [pl.BlockSpec((BLOCK, QD), lambda n: (jnp.minimum(n, nb - 1), PD // QD))],
                 out_specs=(pl.BlockSpec((BLOCK, PD + QD + 2 * KD), lambda n: (jnp.maximum(n - 1, 0), 0)),
                            pl.BlockSpec((8, LANES), lambda n: (0, 0))),
                 scratch=[pltpu.VMEM((BLOCK, QD), F32), pltpu.VMEM((BLOCK, 2 * KD), F32),
                          pltpu.VMEM((2, BLOCK, 2 * KD), F32), pltpu.VMEM((BLOCK, LANES), F32)])(
                     sinks, proj, proj, proj, cos, sin, cos, sin, dcat)


def _pool_sums(u, g, t, shift):
    s2 = u + shift(u, 1, t)
    s4 = s2 + shift(s2, 2, t)
    s8 = s4 + shift(s4, 4, t)
    s16 = s8 + shift(s8, 8, t)
    return jnp.where(g == 0, s2, jnp.where(g == 1, s4, jnp.where(g == 2, s8, s16)))


def _pool_specs(S):
    col = pl.BlockSpec((S, LANES), lambda g: (0, g))
    wsp = pl.BlockSpec((1, LANES, LANES), lambda g: (g, 0, 0))
    vec = pl.BlockSpec((1, LANES), lambda g: (0, g))
    return col, wsp, vec


def _pool_fwd(proj, pool_w, scale):
    S = proj.shape[0]
    col, wsp, vec = _pool_specs(S)

    def body(u_ref, w_ref, s_ref, o_ref):
        g = pl.program_id(0)
        u = u_ref[...]
        t = lax.broadcasted_iota(jnp.int32, u.shape, 0)
        cnt = jnp.minimum(t + 1, 2 << g).astype(F32)
        pm = _pool_sums(u, g, t, _shift_dn) / cnt - u
        o_ref[...] = jnp.dot(pm.astype(BF16), w_ref[0].astype(BF16), preferred_element_type=F32) * s_ref[...]

    return _call(body, name="pool_fwd", out_shape=_sds((S, PD + QD)), grid=(POOL_GROUPS,),
                 in_specs=[col, wsp, vec], out_specs=col)(proj, pool_w, scale)


def _pool_bwd(proj, pool_w, scale, dcat, dproj):
    S = proj.shape[0]
    col, wsp, vec = _pool_specs(S)

    def body(u_ref, w_ref, s_ref, d_ref, dproj_ref, du_ref, dw_ref, dsc_ref):
        g = pl.program_id(0)
        u = u_ref[...]
        t = lax.broadcasted_iota(jnp.int32, u.shape, 0)
        cnt = jnp.minimum(t + 1, 2 << g).astype(F32)
        pm = (_pool_sums(u, g, t, _shift_dn) / cnt - u).astype(BF16)
        wv = w_ref[0].astype(BF16)
        d = d_ref[...]
        pw = jnp.dot(pm, wv, preferred_element_type=F32)
        dsc_ref[...] = jnp.sum(pw * d, axis=0, keepdims=True)
        dpw = (d * s_ref[...]).astype(BF16)
        dw_ref[0] = lax.dot_general(pm, dpw, (((0,), (0,)), ((), ())), preferred_element_type=F32)
        dpm = lax.dot_general(dpw, wv, (((1,), (1,)), ((), ())), preferred_element_type=F32)
        du_ref[...] = _pool_sums(dpm / cnt, g, t, _shift_up) - dpm

    return _call(body, name="pool_bwd",
                 out_shape=(_sds(dproj.shape), _sds((POOL_GROUPS, LANES, LANES)), _sds((1, POOL_GROUPS * LANES))),
                 grid=(POOL_GROUPS,), in_specs=[col, wsp, vec, col, ANY], out_specs=(col, wsp, vec),
                 aliases={4: 0})(proj, pool_w, scale, dcat, dproj)


def _conv(x, w_ref, b_ref, t):
    K = w_ref.shape[0]
    y = b_ref[...] + jnp.zeros_like(x)
    for k in range(K):
        y = y + w_ref[k:k + 1, :] * _shift_dn(x, K - 1 - k, t)
    return y


def _conv_bwd(x, dy, w_ref, t):
    K = w_ref.shape[0]
    dx = jnp.zeros_like(x)
    dws = []
    for k in range(K):
        dx = dx + w_ref[k:k + 1, :] * _shift_up(dy, K - 1 - k, t)
        dws.append(jnp.sum(dy * _shift_dn(x, K - 1 - k, t), axis=0, keepdims=True))
    return dx, dws, jnp.sum(dy, axis=0, keepdims=True)


def _silu_grad(y):
    sg = _sigmoid(y)
    return sg * (1.0 + y * (1.0 - sg))


def _ffn_mid_specs(S, K, layer):
    return [pl.BlockSpec((S, LANES), lambda j: (0, 2 * j)), pl.BlockSpec((S, LANES), lambda j: (0, 2 * j + 1)),
            pl.BlockSpec((None, K, LANES), lambda j: (layer, 0, 2 * j)), pl.BlockSpec((None, K, LANES), lambda j: (layer, 0, 2 * j + 1)),
            pl.BlockSpec((None, 1, LANES), lambda j: (layer, 0, 2 * j)), pl.BlockSpec((None, 1, LANES), lambda j: (layer, 0, 2 * j + 1))]


def _ffn_mid_fwd(a, cw, cb, layer):
    S, F2 = a.shape
    nf = F2 // (2 * LANES)
    K = cw.shape[1]

    def body(au_ref, ag_ref, wu_ref, wg_ref, bu_ref, bg_ref, o_ref):
        t = lax.broadcasted_iota(jnp.int32, (S, LANES), 0)
        hu = _conv(au_ref[...], wu_ref, bu_ref, t)
        hg = _conv(ag_ref[...], wg_ref, bg_ref, t)
        o_ref[...] = (hg * _sigmoid(hg) * hu).astype(BF16)

    return _call(body, name="ffn_mid_fwd", out_shape=_sds((S, F2 // 2), BF16), grid=(nf,),
                 in_specs=_ffn_mid_specs(S, K, layer), out_specs=pl.BlockSpec((S, LANES), lambda j: (0, j)))(
                     a, a, cw, cw, cb[:, None], cb[:, None])


def _ffn_mid_bwd(a, cw, cb, layer, dact):
    S, F2 = a.shape
    nf = F2 // (2 * LANES)
    K = cw.shape[1]

    def body(au_ref, ag_ref, wu_ref, wg_ref, bu_ref, bg_ref, d_ref, da_ref, dw_ref, db_ref):
        t = lax.broadcasted_iota(jnp.int32, (S, LANES), 0)
        au, ag = au_ref[...], ag_ref[...]
        hu = _conv(au, wu_ref, bu_ref, t)
        hg = _conv(ag, wg_ref, bg_ref, t)
        d = d_ref[...].astype(F32)
        dhu = d * hg * _sigmoid(hg)
        dhg = d * hu * _silu_grad(hg)
        dau, dwu, dbu = _conv_bwd(au, dhu, wu_ref, t)
        dag, dwg, dbg = _conv_bwd(ag, dhg, wg_ref, t)
        da_ref[:, :LANES] = dau.astype(BF16)
        da_ref[:, LANES:] = dag.astype(BF16)
        for k in range(K):
            dw_ref[k:k + 1, :LANES] = dwu[k]
            dw_ref[k:k + 1, LANES:] = dwg[k]
        db_ref[:, :LANES] = dbu
        db_ref[:, LANES:] = dbg

    return _call(body, name="ffn_mid_bwd", out_shape=(_sds((S, F2), BF16), _sds((K, F2)), _sds((1, F2))), grid=(nf,),
                 in_specs=_ffn_mid_specs(S, K, layer) + [pl.BlockSpec((S, LANES), lambda j: (0, j))],
                 out_specs=(pl.BlockSpec((S, 2 * LANES), lambda j: (0, j)), pl.BlockSpec((K, 2 * LANES), lambda j: (0, j)),
                            pl.BlockSpec((1, 2 * LANES), lambda j: (0, j))))(a, a, cw, cw, cb[:, None], cb[:, None], dact)


def _conv_silu_fwd(x, cw, cb):
    S = x.shape[0]
    K, C = cw.shape

    def body(x_ref, w_ref, b_ref, o_ref):
        t = lax.broadcasted_iota(jnp.int32, (S, LANES), 0)
        y = _conv(x_ref[...], w_ref, b_ref, t)
        o_ref[...] = y * _sigmoid(y)

    col = pl.BlockSpec((S, LANES), lambda j: (0, j))
    return _call(body, name="conv_silu_fwd", out_shape=_sds((S, C)), grid=(C // LANES,),
                 in_specs=[col, pl.BlockSpec((K, LANES), lambda j: (0, j)), pl.BlockSpec((1, LANES), lambda j: (0, j))],
                 out_specs=col)(x, cw, cb)


def _conv_silu_bwd(x, cw, cb, douts):
    S = x.shape[0]
    K, C = cw.shape
    starts, off = [], 0
    for d in douts:
        starts.append(off)
        off += d.shape[1] // LANES
    assert off == C // LANES

    def body(x_ref, w_ref, b_ref, *rest):
        d_refs, (dx_ref, dw_ref, db_ref) = rest[:len(douts)], rest[len(douts):]
        j = pl.program_id(0)
        t = lax.broadcasted_iota(jnp.int32, (S, LANES), 0)
        x = x_ref[...]
        y = _conv(x, w_ref, b_ref, t)
        d = d_refs[0][...]
        for i in range(1, len(douts)):
            d = jnp.where(j >= starts[i], d_refs[i][...], d)
        dy = d * _silu_grad(y)
        dx, dw, db = _conv_bwd(x, dy, w_ref, t)
        dx_ref[...] = dx.astype(BF16)
        for k in range(K):
            dw_ref[k:k + 1, :] = dw[k]
        db_ref[...] = db

    col = pl.BlockSpec((S, LANES), lambda j: (0, j))
    wsp = pl.BlockSpec((K, LANES), lambda j: (0, j))
    bsp = pl.BlockSpec((1, LANES), lambda j: (0, j))

    def dspec(i):
        nblk = douts[i].shape[1] // LANES
        return pl.BlockSpec((S, LANES), lambda j: (0, jnp.clip(j - starts[i], 0, nblk - 1)))

    return _call(body, name="conv_silu_bwd", out_shape=(_sds((S, C), BF16), _sds((K, C)), _sds((1, C))), grid=(C // LANES,),
                 in_specs=[col, wsp, bsp] + [dspec(i) for i in range(len(douts))],
                 out_specs=(col, wsp, bsp))(x, cw, cb, *douts)


HI = lax.Precision.HIGHEST


def _ssd_prep_fwd(proj, col0, bias_g, alog_g):
    S = proj.shape[0]
    nc = S // SSM_L
    b0 = col0 // LANES

    def body(raw_ref, b_ref, al_ref, pre_ref, dt_ref, acs_ref, acst_ref):
        r_i = lax.broadcasted_iota(jnp.int32, (LANES, LANES), 0)
        c_i = lax.broadcasted_iota(jnp.int32, (LANES, LANES), 1)
        live = c_i < SSM_R
        tril = jnp.where(r_i >= c_i, 1.0, 0.0)
        raw = raw_ref[...]
        for g in range(SSM_G):
            sel = jnp.where((r_i == SSM_R * g + c_i) & live, 1.0, 0.0)
            pre = jnp.dot(raw, sel, preferred_element_type=F32, precision=HI) + b_ref[g]
            dt = jnp.where(live, jnp.logaddexp(pre, 0.0), 0.0)
            a = dt * (-jnp.exp(al_ref[g]))
            acs = jnp.dot(tril, a, preferred_element_type=F32, precision=HI)
            pre_ref[g] = pre
            dt_ref[g] = dt
            acs_ref[g] = acs
            acst_ref[g] = acs.T

    gsp = pl.BlockSpec((SSM_G, 1, LANES), lambda c: (0, 0, 0))
    blk = pl.BlockSpec((SSM_G, SSM_L, LANES), lambda c: (0, c, 0))
    big = _sds((SSM_G, S, LANES))
    return _call(body, name="ssd_prep_fwd", out_shape=(big, big, big, _sds((SSM_G, LANES, S))), grid=(nc,),
                 in_specs=[pl.BlockSpec((SSM_L, LANES), lambda c: (c, b0)), gsp, gsp],
                 out_specs=(blk, blk, blk, pl.BlockSpec((SSM_G, LANES, SSM_L), lambda c: (0, 0, c))))(proj, bias_g, alog_g)


def _ssd_prep_bwd(pre_g, dt_g, alog_g, ddt_g, dacs_g, dacst_g):
    S = pre_g.shape[1]
    nc = S // SSM_L

    def body(pre_ref, dt_ref, al_ref, ddt_ref, dacs_ref, dacst_ref, draw_ref, db_ref, dal_ref):
        c = pl.program_id(0)
        r_i = lax.broadcasted_iota(jnp.int32, (LANES, LANES), 0)
        c_i = lax.broadcasted_iota(jnp.int32, (LANES, LANES), 1)
        live = c_i < SSM_R
        triu = jnp.where(r_i <= c_i, 1.0, 0.0)

        @pl.when(c == 0)
        def _():
            db_ref[...] = jnp.zeros_like(db_ref)
            dal_ref[...] = jnp.zeros_like(dal_ref)

        draw = jnp.zeros((SSM_L, LANES), F32)
        for g in range(SSM_G):
            dacs = dacs_ref[g] + dacst_ref[g].T
            da = jnp.dot(triu, dacs, preferred_element_type=F32, precision=HI)
            A = -jnp.exp(al_ref[g])
            ddt = ddt_ref[g] + da * A
            dpre = jnp.where(live, ddt * _sigmoid(pre_ref[g]), 0.0)
            unsel = jnp.where((c_i == SSM_R * g + r_i) & (r_i < SSM_R), 1.0, 0.0)
            draw = draw + jnp.dot(dpre, unsel, preferred_element_type=F32, precision=HI)
            db_ref[g] += jnp.sum(dpre, axis=0, keepdims=True)
            dal_ref[g] += jnp.where(live[:1], jnp.sum(da * dt_ref[g], axis=0, keepdims=True) * A, 0.0)
        draw_ref[...] = draw

    gsp = pl.BlockSpec((SSM_G, 1, LANES), lambda c: (0, 0, 0))
    blk = pl.BlockSpec((SSM_G, SSM_L, LANES), lambda c: (0, c, 0))
    gout = _sds((SSM_G, 1, LANES))
    return _call(body, name="ssd_prep_bwd", out_shape=(_sds((S, LANES)), gout, gout), grid=(nc,),
                 in_specs=[blk, blk, gsp, blk, blk, pl.BlockSpec((SSM_G, LANES, SSM_L), lambda c: (0, 0, c))],
                 out_specs=(pl.BlockSpec((SSM_L, LANES), lambda c: (c, 0)), gsp, gsp))(pre_g, dt_g, alog_g, ddt_g, dacs_g, dacst_g)


NT = (((1,), (1,)), ((), ()))
TN = (((0,), (0,)), ((), ()))
SSM_HP = SSM_R * SSM_P


def _ssd_group_terms(xs_ref, dt_ref, acs_ref, d_ref):
    hid = lax.broadcasted_iota(jnp.int32, (1, SSM_HP), 1) // SSM_P
    rid = lax.broadcasted_iota(jnp.int32, (SSM_HP, 1), 0) // SSM_P

    def widen(cols):
        out = cols[0]
        for r in range(1, SSM_R):
            out = jnp.where(hid == r, cols[r], out)
        return out

    dt_c = [dt_ref[:, r:r + 1] for r in range(SSM_R)]
    acs_c = [acs_ref[:, r:r + 1] for r in range(SSM_R)]
    last = [acs_ref[SSM_L - 1:SSM_L, r:r + 1] for r in range(SSM_R)]
    decay_c = [jnp.exp(last[r] - acs_c[r]) for r in range(SSM_R)]
    cd = [jnp.exp(last[r]) for r in range(SSM_R)]
    cd_rows = cd[0]
    for r in range(1, SSM_R):
        cd_rows = jnp.where(rid == r, cd[r], cd_rows)
    xs = xs_ref[...]
    return (xs, xs * widen(dt_c), widen([jnp.exp(a) for a in acs_c]), widen(decay_c),
            widen([d_ref[:, r:r + 1] for r in range(SSM_R)]), cd_rows, dt_c, decay_c, cd)


def _ssd_lmat(acs_ref, acst_ref, r, tril):
    return jnp.exp(jnp.where(tril, acs_ref[:, r:r + 1] - acst_ref[r:r + 1, :], -jnp.inf))


SSM_GPS = 2


def _ssd_specs(rev, nc):
    def cc(c):
        return nc - 1 - c if rev else c
    xs_blocks = (SSM_G * SSM_HP) // (SSM_GPS * SSM_N)
    xs = pl.BlockSpec((SSM_L, SSM_GPS * SSM_HP), lambda g, c: (cc(c), g))
    bsp = pl.BlockSpec((SSM_L, SSM_GPS * SSM_N), lambda g, c: (cc(c), xs_blocks + g))
    csp = pl.BlockSpec((SSM_L, SSM_GPS * SSM_N), lambda g, c: (cc(c), xs_blocks + SSM_G // SSM_GPS + g))
    sc = pl.BlockSpec((SSM_GPS, SSM_L, LANES), lambda g, c: (g, cc(c), 0))
    sct = pl.BlockSpec((SSM_GPS, LANES, SSM_L), lambda g, c: (g, 0, cc(c)))
    gsp = pl.BlockSpec((SSM_GPS, 1, LANES), lambda g, c: (g, 0, 0))
    st = pl.BlockSpec((None, SSM_GPS, SSM_HP, SSM_N), lambda g, c: (cc(c), g, 0, 0))
    return xs, bsp, csp, sc, sct, gsp, st


def _ssd_group_views(gg, xs_ref, b_ref, c_ref, *per_group):
    return (xs_ref.at[:, gg * SSM_HP:(gg + 1) * SSM_HP], b_ref.at[:, gg * SSM_N:(gg + 1) * SSM_N],
            c_ref.at[:, gg * SSM_N:(gg + 1) * SSM_N]) + tuple(r.at[gg] for r in per_group)


def _ssd_fwd(xbc, dt_g, acs_g, acst_g, d_g):
    S = xbc.shape[0]
    nc = S // SSM_L
    xs_s, b_s, c_s, sc, sct, gsp, st = _ssd_specs(False, nc)

    def body(xs_ref, b_ref, c_ref, dt_ref, acs_ref, acst_ref, d_ref, y_ref, st_ref, state):
        c = pl.program_id(1)

        @pl.when(c == 0)
        def _():
            state[...] = jnp.zeros_like(state)

        tril = lax.broadcasted_iota(jnp.int32, (SSM_L, SSM_L), 0) >= lax.broadcasted_iota(jnp.int32, (SSM_L, SSM_L), 1)
        for gg in range(SSM_GPS):
            xs_v, b_v, c_v, dt_v, acs_v, acst_v, d_v, st_v, state_v = _ssd_group_views(
                gg, xs_ref, b_ref, c_ref, dt_ref, acs_ref, acst_ref, d_ref, st_ref, state)
            y_v = y_ref.at[:, gg * SSM_HP:(gg + 1) * SSM_HP]
            Bb, Cb = b_v[...].astype(BF16), c_v[...].astype(BF16)
            Gm = lax.dot_general(Cb, Bb, NT, preferred_element_type=F32)
            xs, X, e_all, decay_all, d_all, cd_rows, _, _, _ = _ssd_group_terms(xs_v, dt_v, acs_v, d_v)
            S_all = state_v[...]
            st_v[...] = S_all
            y_v[...] = e_all * lax.dot_general(Cb, S_all.astype(BF16), NT, preferred_element_type=F32) + d_all * xs
            for r in range(SSM_R):
                sl = slice(r * SSM_P, (r + 1) * SSM_P)
                M = Gm * _ssd_lmat(acs_v, acst_v, r, tril)
                y_v[:, sl] += jnp.dot(M.astype(BF16), X[:, sl].astype(BF16), preferred_element_type=F32)
            state_v[...] = S_all * cd_rows + lax.dot_general((X * decay_all).astype(BF16), Bb, TN, preferred_element_type=F32)

    return _call(body, name="ssd_fwd",
                 out_shape=(_sds((S, SSM_G * SSM_HP)), _sds((nc, SSM_G, SSM_HP, SSM_N))),
                 grid=(SSM_G // SSM_GPS, nc), in_specs=[xs_s, b_s, c_s, sc, sc, sct, gsp],
                 out_specs=(xs_s, pl.BlockSpec((None, SSM_GPS, SSM_HP, SSM_N), lambda g, c: (c, g, 0, 0))),
                 scratch=[pltpu.VMEM((SSM_GPS, SSM_HP, SSM_N), F32)])(xbc, xbc, xbc, dt_g, acs_g, acst_g, d_g)


def _ssd_bwd(xbc, dt_g, acs_g, acst_g, d_g, states, dy):
    S = xbc.shape[0]
    nc = S // SSM_L
    xs_s, b_s, c_s, sc, sct, gsp, st = _ssd_specs(True, nc)
    bc_out = pl.BlockSpec((SSM_L, SSM_GPS * SSM_N), lambda g, c: (nc - 1 - c, g))

    def body(xs_ref, b_ref, c_ref, dt_ref, acs_ref, acst_ref, d_ref, st_ref, dy_ref,
             dxs_ref, db_ref, dc_ref, ddt_ref, dacs_ref, dacst_ref, dd_ref, dstate):
        c = pl.program_id(1)

        @pl.when(c == 0)
        def _():
            dstate[...] = jnp.zeros_like(dstate)
            dd_ref[...] = jnp.zeros_like(dd_ref)

        tril = lax.broadcasted_iota(jnp.int32, (SSM_L, SSM_L), 0) >= lax.broadcasted_iota(jnp.int32, (SSM_L, SSM_L), 1)
        lane = lax.broadcasted_iota(jnp.int32, (1, LANES), 1)
        subl = lax.broadcasted_iota(jnp.int32, (LANES, 1), 0)
        last_row = lax.broadcasted_iota(jnp.int32, (SSM_L, 1), 0) == SSM_L - 1
        triu = lax.broadcasted_iota(jnp.int32, (SSM_L, SSM_L), 0) <= lax.broadcasted_iota(jnp.int32, (SSM_L, SSM_L), 1)
        for gg in range(SSM_GPS):
            xs_v, b_v, c_v, dt_v, acs_v, acst_v, d_v, st_v, ddt_v, dacs_v, dacst_v, dd_v, dstate_v = _ssd_group_views(
                gg, xs_ref, b_ref, c_ref, dt_ref, acs_ref, acst_ref, d_ref, st_ref, ddt_ref, dacs_ref, dacst_ref, dd_ref, dstate)
            dy_v, dxs_v = (r.at[:, gg * SSM_HP:(gg + 1) * SSM_HP] for r in (dy_ref, dxs_ref))
            db_v, dc_v = (r.at[:, gg * SSM_N:(gg + 1) * SSM_N] for r in (db_ref, dc_ref))
            Bb, Cb = b_v[...].astype(BF16), c_v[...].astype(BF16)
            Gm = lax.dot_general(Cb, Bb, NT, preferred_element_type=F32)
            GmT = lax.dot_general(Bb, Cb, NT, preferred_element_type=F32)
            xs, X, e_all, decay_all, d_all, cd_rows, dt_c, decay_c, cd = _ssd_group_terms(xs_v, dt_v, acs_v, d_v)
            S_all, dSn_all, dY = st_v[...], dstate_v[...], dy_v[...]
            Sb, dSnb = S_all.astype(BF16), dSn_all.astype(BF16)
            T = lax.dot_general(Cb, Sb, NT, preferred_element_type=F32)
            dT = (dY * e_all).astype(BF16)
            dC = jnp.dot(dT, Sb, preferred_element_type=F32)
            dS_prev = lax.dot_general(dT, Cb, TN, preferred_element_type=F32)
            yo_dy = dY * (e_all * T)
            W = lax.dot_general(Bb, dSnb, NT, preferred_element_type=F32)
            dB = jnp.dot((X * decay_all).astype(BF16), dSnb, preferred_element_type=F32)
            xw = X * W
            dcd_rows = jnp.sum(dSn_all * S_all, axis=1, keepdims=True)
            dstate_v[...] = dS_prev + dSn_all * cd_rows
            dX_state = W * decay_all
            dG = jnp.zeros((SSM_L, SSM_L), F32)
            dGT = jnp.zeros((SSM_L, SSM_L), F32)
            ddt_blk = jnp.zeros((SSM_L, LANES), F32)
            dacs_blk = jnp.zeros((SSM_L, LANES), F32)
            dacst_blk = jnp.zeros((LANES, SSM_L), F32)
            dd_row = jnp.zeros((1, LANES), F32)
            for r in range(SSM_R):
                sl = slice(r * SSM_P, (r + 1) * SSM_P)
                Lm = _ssd_lmat(acs_v, acst_v, r, tril)
                LmT = jnp.exp(jnp.where(triu, acst_v[r:r + 1, :] - acs_v[:, r:r + 1], -jnp.inf))
                M = Gm * Lm
                dYh, xs_h = dY[:, sl], xs[:, sl]
                dYb, Xb = dYh.astype(BF16), X[:, sl].astype(BF16)
                dM = lax.dot_general(dYb, Xb, NT, preferred_element_type=F32)
                dX = jnp.dot((GmT * LmT).astype(BF16), dYb, preferred_element_type=F32) + dX_state[:, sl]
                dG = dG + dM * Lm
                dGT = dGT + lax.dot_general(Xb, dYb, NT, preferred_element_type=F32) * LmT
                dseg = dM * M
                dd = jnp.sum(xw[:, sl], axis=1, keepdims=True) * decay_c[r]
                dcd = jnp.sum(dcd_rows[sl])
                dacs_col = (jnp.sum(dseg, axis=1, keepdims=True) + jnp.sum(yo_dy[:, sl], axis=1, keepdims=True) - dd
                            + jnp.where(last_row, dcd * cd[r] + jnp.sum(dd), 0.0))
                dacs_row = -jnp.sum(dseg, axis=0, keepdims=True)
                dxs_v[:, sl] = dX * dt_c[r] + d_all[:, sl] * dYh
                ddt_blk = ddt_blk + jnp.where(lane == r, jnp.sum(dX * xs_h, axis=1, keepdims=True), 0.0)
                dacs_blk = dacs_blk + jnp.where(lane == r, dacs_col, 0.0)
                dacst_blk = dacst_blk + jnp.where(subl == r, dacs_row, 0.0)
                dd_row = dd_row + jnp.where(lane == r, jnp.sum(dYh * xs_h), 0.0)
            dGb = dG.astype(BF16)
            dc_v[...] = dC + jnp.dot(dGb, Bb, preferred_element_type=F32)
            db_v[...] = dB + jnp.dot(dGT.astype(BF16), Cb, preferred_element_type=F32)
            ddt_v[...] = ddt_blk
            dacs_v[...] = dacs_blk
            dacst_v[...] = dacst_blk
            dd_v[...] += dd_row

    big = _sds((SSM_G, S, LANES))
    return _call(body, name="ssd_bwd",
                 out_shape=(_sds((S, SSM_G * SSM_HP)), _sds((S, SSM_G * SSM_N)), _sds((S, SSM_G * SSM_N)),
                            big, big, _sds((SSM_G, LANES, S)), _sds((SSM_G, 1, LANES))),
                 grid=(SSM_G // SSM_GPS, nc), in_specs=[xs_s, b_s, c_s, sc, sc, sct, gsp, st, xs_s],
                 out_specs=(xs_s, bc_out, bc_out, sc, sc, sct, gsp),
                 scratch=[pltpu.VMEM((SSM_GPS, SSM_HP, SSM_N), F32)])(xbc, xbc, xbc, dt_g, acs_g, acst_g, d_g, states, dy)


def _gate_norm_fwd(y, proj, w):
    S, DI = y.shape
    tm = _tile(S, 256)

    def body(y_ref, z_ref, w_ref, o_ref):
        z = z_ref[...]
        gn = y_ref[...] * (z * _sigmoid(z))
        r = lax.rsqrt(jnp.mean(gn * gn, axis=-1, keepdims=True) + SSM_NORM_EPS)
        o_ref[...] = (gn * r * w_ref[...]).astype(BF16)

    row = pl.BlockSpec((tm, DI), lambda i: (i, 0))
    return _call(body, name="gate_norm_fwd", out_shape=_sds((S, DI), BF16), grid=(S // tm,),
                 in_specs=[row, row, pl.BlockSpec((1, DI), lambda i: (0, 0))], out_specs=row)(y, proj, w)


def _gate_norm_bwd(y, proj, w, dout):
    S, DI = y.shape
    tm = _tile(S, 256)

    def body(y_ref, z_ref, w_ref, d_ref, dy_ref, dz_ref, dw_ref):
        z, yv = z_ref[...], y_ref[...]
        sz = z * _sigmoid(z)
        dgn, dw = _norm_bwd_math(yv * sz, w_ref[...], d_ref[...].astype(F32), SSM_NORM_EPS)
        dy_ref[...] = dgn * sz
        dz_ref[...] = (dgn * yv * _silu_grad(z)).astype(BF16)

        @pl.when(pl.program_id(0) == 0)
        def _():
            dw_ref[...] = jnp.zeros_like(dw_ref)

        dw_ref[...] += dw

    row = pl.BlockSpec((tm, DI), lambda i: (i, 0))
    vec = pl.BlockSpec((1, DI), lambda i: (0, 0))
    return _call(body, name="gate_norm_bwd", out_shape=(_sds((S, DI)), _sds((S, DI), BF16), _sds((1, DI))), grid=(S // tm,),
                 in_specs=[row, row, vec, row], out_specs=(row, row, vec))(y, proj, w, dout)


def _group_major(v):
    return jnp.pad(v.reshape(SSM_G, 1, SSM_R), ((0, 0), (0, 0), (0, LANES - SSM_R)))


def _ungroup(t):
    return t[:, :SSM_R].reshape(1, SSM_G * SSM_R)


def _ffn_fwd(x, P, l, need):
    need(f"ffn{l}_up", x)
    h = _rmsnorm(x, P["norm_ffn"][l:l + 1], name=f"ffn{l}_norm")
    a = _mm(h, P[f"ffn_w_up{l}"], name=f"ffn{l}_up")
    need(f"ffn{l}_down", a)
    act = _ffn_mid_fwd(a, P["ffn_conv_w"], P["ffn_conv_b"], l)
    out = _mm(act, P[f"ffn_w_down{l}"], res=x, name=f"ffn{l}_down")
    return out, (x, h, a, act)


def _ffn_bwd(saved, P, l, dx, emit):
    x, h, a, act = saved
    dact = _mm(dx, P[f"ffn_w_down{l}"], tb=True, out_dtype=BF16, name=f"ffn{l}_down_dx")
    dw_down = _mm(act, dx, ta=True, out_dtype=PAYLOAD, name=f"ffn{l}_down_dw")
    da, dcw, dcb = _ffn_mid_bwd(a, P["ffn_conv_w"], P["ffn_conv_b"], l, dact)
    dw_up = _mm(h, da, ta=True, out_dtype=PAYLOAD, name=f"ffn{l}_up_dw")
    tie = emit(f"ffn{l}", {"ffn_w_up": dw_up, "ffn_w_down": dw_down})
    dh = _mm(da, P[f"ffn_w_up{l}"], tb=True, name=f"ffn{l}_up_dx")
    dx_in, dnw = _rmsnorm_bwd(x, P["norm_ffn"][l:l + 1], dh, dx, name=f"ffn{l}_norm_bwd", after=tie)
    return dx_in, dnw, dcw, dcb


def _local_step(x, positions, target, P, need, emit, after=None):
    S, D = x.shape
    inv_freq = ROPE_THETA ** (-jnp.arange(0, HEAD_DIM, 2, dtype=F32) / HEAD_DIM)
    inv_freq = jnp.tile(inv_freq, LANES // (HEAD_DIM // 2)).reshape(1, LANES)
    cos, sin = _rope_tables(positions, inv_freq)

    nm0 = P["norm_mix"][0:1]
    h0 = _rmsnorm(x, nm0, name="mix_norm", after=after)
    proj0 = _mm(h0, P["mix_w_in"], name="mix_in")
    cat0 = _attn_fwd(proj0, cos, sin, P["attn_sinks"], _pool_fwd(proj0, P["pool_w"][0], P["pool_scale"]))
    need("mix_out", cat0)
    x1 = _mm(cat0, P["mix_w_out"], res=x, name="mix_out")
    x2, ffn0 = _ffn_fwd(x1, P, 0, need)

    nm1 = P["norm_mix"][1:2]
    need("ssm", x2)
    h1 = _rmsnorm(x2, nm1, name="ssm_norm_in")
    z = _mm(h1, P["ssm_wz"], name="ssm_in_z")
    xbcp = _mm(h1, P["ssm_wxbc"], name="ssm_in_xbc")
    dtraw = _mm(h1, P["ssm_wdt"], name="ssm_in_dt")
    xbc = _conv_silu_fwd(xbcp, P["ssm_conv_w"], P["ssm_conv_b"])
    bias_g, alog_g, d_g = _group_major(P["ssm_dt_bias"]), _group_major(P["ssm_A_log"]), _group_major(P["ssm_D"])
    pre_g, dt_g, acs_g, acst_g = _ssd_prep_fwd(dtraw, 0, bias_g, alog_g)
    y, states = _ssd_fwd(xbc, dt_g, acs_g, acst_g, d_g)
    yn = _gate_norm_fwd(y, z, P["ssm_norm"])
    x3 = _mm(yn, P["ssm_w_out"], res=x2, name="ssm_out")
    x4, ffn1 = _ffn_fwd(x3, P, 1, need)

    loss, dx, d_norm_final = _final_loss(x4, P["norm_final"].reshape(1, D), target, name="final_loss")
    dx, dnf1, dcw1, dcb1 = _ffn_bwd(ffn1, P, 1, dx, emit)
    dyn = _mm(dx, P["ssm_w_out"], tb=True, out_dtype=BF16, name="ssm_out_dx")
    d_w_out1 = _mm(yn, dx, ta=True, out_dtype=PAYLOAD, name="ssm_out_dw")
    dy, dz, d_ssm_norm = _gate_norm_bwd(y, z, P["ssm_norm"], dyn)
    dxs, dB, dC, ddt_g, dacs_g, dacst_g, dd_g = _ssd_bwd(xbc, dt_g, acs_g, acst_g, d_g, states, dy)
    draw, dbias_g, dalog_g = _ssd_prep_bwd(pre_g, dt_g, alog_g, ddt_g, dacs_g, dacst_g)
    dxbc, d_conv_w1, d_conv_b1 = _conv_silu_bwd(xbcp, P["ssm_conv_w"], P["ssm_conv_b"], [dxs, dB, dC])
    d_wz = _mm(h1, dz, ta=True, out_dtype=PAYLOAD, name="ssm_in_z_dw")
    d_wxbc = _mm(h1, dxbc, ta=True, out_dtype=PAYLOAD, name="ssm_in_xbc_dw")
    d_wdt = _mm(h1, draw, ta=True, out_dtype=PAYLOAD, name="ssm_in_dt_dw")
    tie = emit("ssm", {"ssm_wz": d_wz, "ssm_wxbc": d_wxbc, "ssm_wdt": d_wdt, "ssm_w_out": d_w_out1,
                       "ssm_conv_w": d_conv_w1, "ssm_conv_b": d_conv_b1, "ssm_norm": d_ssm_norm})
    dh1 = _mm(dz, P["ssm_wz"], tb=True, name="ssm_in_z_dx")
    dh1 = _mm(dxbc, P["ssm_wxbc"], tb=True, res=dh1, name="ssm_in_xbc_dx")
    dh1 = _mm(draw, P["ssm_wdt"], tb=True, res=dh1, name="ssm_in_dt_dx")
    dx, dnm1 = _rmsnorm_bwd(x2, nm1, dh1, dx, name="ssm_norm_in_bwd", after=tie)
    dx, dnf0, dcw0, dcb0 = _ffn_bwd(ffn0, P, 0, dx, emit)
    dcat = _mm(dx, P["mix_w_out"], tb=True, name="mix_out_dx")
    d_w_out0 = _mm(cat0, dx, ta=True, out_dtype=PAYLOAD, name="mix_out_dw")
    dproj0, dsk = _attn_bwd(proj0, cos, sin, P["attn_sinks"], dcat)
    dproj0, d_pool_w, d_pool_scale = _pool_bwd(proj0, P["pool_w"][0], P["pool_scale"], dcat, dproj0)
    d_w_in0 = _mm(h0, dproj0, ta=True, out_dtype=PAYLOAD, name="mix_in_dw")
    tie = emit("mix", {"mix_w_in": d_w_in0, "mix_w_out": d_w_out0, "ffn_conv_w": jnp.stack([dcw0, dcw1])})
    dh0 = _mm(dproj0, P["mix_w_in"], tb=True, name="mix_in_dx")
    grad_x, dnm0 = _rmsnorm_bwd(x, nm0, dh0, dx, name="mix_norm_bwd", after=tie)

    small = {
        "norm_mix": jnp.concatenate([dnm0, dnm1], axis=0),
        "norm_ffn": jnp.concatenate([dnf0, dnf1], axis=0),
        "norm_final": d_norm_final,
        "pool_w": d_pool_w,
        "pool_scale": d_pool_scale,
        "attn_sinks_rows": dsk,
        "ssm_dt_bias_g": dbias_g, "ssm_A_log_g": dalog_g, "ssm_D_g": dd_g,
        "ffn_conv_b": jnp.concatenate([dcb0, dcb1], axis=0),
    }
    return loss, grad_x, small


def _peer(k):
    x, y, c = lax.axis_index("x"), lax.axis_index("y"), lax.axis_index("c")
    px = 1 - x if k & 4 else x
    py = 1 - y if k & 2 else y
    pc = 1 - c if k & 1 else c
    return (px, py, pc), 4 * px + 2 * py + pc


def _my_index():
    return 4 * lax.axis_index("x") + 2 * lax.axis_index("y") + lax.axis_index("c")


def _land_sds(a, mode, gather):
    if mode == "slab":
        return _sds(((N_DEV,) + a.shape) if gather else a.shape, a.dtype)
    assert mode == "rows", mode
    return _sds((N_DEV * a.shape[0],) + a.shape[1:] if gather else (N_DEV, a.shape[0] // N_DEV) + a.shape[1:], a.dtype)


def _part(ref, mode, shape, idx):
    if mode == "slab":
        return ref.at[idx]
    r = shape[0] // N_DEV
    return ref.at[pl.ds(idx * r, r)]


def _own_copies(ops, gather, srcs, lands, sems):
    me = _my_index()
    out = []
    for i, (a, mode) in enumerate(ops):
        s = srcs[i] if gather else _part(srcs[i], mode, a.shape, me)
        d = _part(lands[i], mode, _land_sds(a, mode, gather).shape, me) if gather else lands[i].at[me]
        out.append(pltpu.make_async_copy(s, d, sems.at[i]))
    return out


def _remote_copies(ops, gather, srcs, lands, send_sems, recv_sems):
    me = _my_index()
    n = len(ops)
    out = []
    for k in range(1, N_DEV):
        dev, idx = _peer(k)
        for i, (a, mode) in enumerate(ops):
            s = srcs[i] if gather else _part(srcs[i], mode, a.shape, idx)
            d = _part(lands[i], mode, _land_sds(a, mode, gather).shape, me) if gather else lands[i].at[me]
            out.append(pltpu.make_async_remote_copy(src_ref=s, dst_ref=d, send_sem=send_sems.at[(k - 1) * n + i],
                                                    recv_sem=recv_sems.at[(k - 1) * n + i], device_id=dev,
                                                    device_id_type=pl.DeviceIdType.MESH))
    return out


def _exchange(ops, *, gather, name):
    n = len(ops)

    def body(*refs):
        ins, outs = refs[:n], refs[n:2 * n]
        send_sems, recv_sems, local_sems = refs[2 * n:]
        copies = _own_copies(ops, gather, ins, outs, local_sems) + _remote_copies(ops, gather, ins, outs, send_sems, recv_sems)
        for cp in copies:
            cp.start()
        for cp in copies:
            cp.wait()

    return pl.pallas_call(
        body, name=name, out_shape=[_land_sds(a, m, gather) for a, m in ops], in_specs=[ANY] * n, out_specs=[ANY] * n,
        scratch_shapes=[pltpu.SemaphoreType.DMA((n * (N_DEV - 1),)), pltpu.SemaphoreType.DMA((n * (N_DEV - 1),)),
                        pltpu.SemaphoreType.DMA((n,))],
    )(*[a for a, _ in ops])


HBM = pl.BlockSpec(memory_space=pltpu.HBM)
SEM = pl.BlockSpec(memory_space=pltpu.SEMAPHORE)
SIDE_EFFECT = pltpu.SideEffectType.DATAFLOW_SIDE_EFFECTING


def _in_hbm(a):
    return pltpu.with_memory_space_constraint(a, pltpu.HBM)


def _place_own(ops, *, gather, name):
    n = len(ops)

    def zeros(k):
        return (0,) * k

    in_specs, out_specs = [], []
    for a, mode in ops:
        nd = a.ndim
        if gather and mode == "slab":
            in_specs.append(pl.BlockSpec(a.shape, lambda i, nd=nd: zeros(nd)))
            out_specs.append(pl.BlockSpec((1,) + a.shape, lambda i, nd=nd: (_my_index(),) + zeros(nd)))
        elif gather:
            in_specs.append(pl.BlockSpec(a.shape, lambda i, nd=nd: zeros(nd)))
            out_specs.append(pl.BlockSpec(a.shape, lambda i, nd=nd: (_my_index(),) + zeros(nd - 1)))
        elif mode == "slab":
            in_specs.append(pl.BlockSpec((1,) + a.shape[1:], lambda i, nd=nd: (_my_index(),) + zeros(nd - 1)))
            out_specs.append(pl.BlockSpec((1,) + a.shape[1:], lambda i, nd=nd: (_my_index(),) + zeros(nd - 1)))
        else:
            r = a.shape[0] // N_DEV
            in_specs.append(pl.BlockSpec((r,) + a.shape[1:], lambda i, nd=nd: (_my_index(),) + zeros(nd - 1)))
            out_specs.append(pl.BlockSpec((1, r) + a.shape[1:], lambda i, nd=nd: (_my_index(),) + zeros(nd)))

    def body(*refs):
        for i_ref, o_ref in zip(refs[:n], refs[n:]):
            if o_ref.shape == i_ref.shape:
                o_ref[...] = i_ref[...]
            else:
                o_ref[0] = i_ref[...]

    return _call(body, name=name, out_shape=[_land_sds(a, m, gather) for a, m in ops], grid=(1,),
                 in_specs=in_specs, out_specs=out_specs)(*[a for a, _ in ops])


def _exchange_start(groups, *, gather, name):
    sizes = [len(ops) for ops, _ in groups]
    n = sum(sizes)
    G = len(groups)

    def body(*refs):
        srcs, lands = refs[:n], refs[n:2 * n]
        sems = refs[2 * n:2 * n + 2 * G]
        token = refs[-1]
        off = 0
        for g, (ops, _) in enumerate(groups):
            for cp in _remote_copies(ops, gather, srcs[off:off + sizes[g]], lands[off:off + sizes[g]], sems[2 * g], sems[2 * g + 1]):
                cp.start()
            off += sizes[g]
        token[...] = jnp.zeros_like(token)

    srcs = [a for ops, _ in groups for a, _ in ops]
    lands = [l for _, ls in groups for l in ls]
    sem_shapes = [pltpu.SemaphoreType.DMA((s * (N_DEV - 1),)) for s in sizes for _ in range(2)]
    outs = pl.pallas_call(
        body, name=name,
        out_shape=sem_shapes + [pltpu.HBM(a.shape, a.dtype) for a in srcs + lands] + [_sds((8, LANES))],
        in_specs=[HBM] * (2 * n), out_specs=[SEM] * (2 * G) + [HBM] * (2 * n) + [pl.BlockSpec(memory_space=pltpu.VMEM)],
        input_output_aliases={i: 2 * G + i for i in range(2 * n)},
        compiler_params=pltpu.CompilerParams(has_side_effects=SIDE_EFFECT))(*[_in_hbm(a) for a in srcs + lands])
    sems, thru, token = outs[:2 * G], outs[2 * G:2 * G + 2 * n], outs[-1]
    states, off = [], 0
    for g, s in enumerate(sizes):
        states.append((sems[2 * g], sems[2 * g + 1], thru[off:off + s], thru[n + off:n + off + s]))
        off += s
    return states, token


def _exchange_wait(ops, state, after, *, gather, name):
    send_sems, recv_sems, srcs, lands = state
    n = len(ops)

    def body(*refs):
        for cp in _remote_copies(ops, gather, refs[:n], refs[n:2 * n], refs[2 * n], refs[2 * n + 1]):
            cp.wait_send()
            cp.wait_recv()

    outs = pl.pallas_call(
        body, name=name, out_shape=[pltpu.HBM(a.shape, a.dtype) for a in list(srcs) + list(lands)],
        in_specs=[HBM] * (2 * n) + [SEM, SEM, ANY], out_specs=[HBM] * (2 * n),
        input_output_aliases={i: i for i in range(2 * n)},
        compiler_params=pltpu.CompilerParams(has_side_effects=SIDE_EFFECT))(*srcs, *lands, send_sems, recv_sems, after)
    return outs[n:]


ADAM_ROWS = 256


def _row_tile(R, cap=ADAM_ROWS):
    best = R
    if R > cap:
        for d in range(16, cap + 1, 16):
            if R % d == 0:
                best = d
    return best


def _adamw(g_layers, w, m, v, *, name):
    L = len(g_layers)
    J, R, Wd = g_layers[0].shape
    assert w.shape == (L, R, Wd), (g_layers[0].shape, w.shape)
    tr = _row_tile(R)
    nrt = R // tr
    c1 = 1.0 / (1.0 - ADAM_B1 ** ADAM_STEP)
    c2 = 1.0 / (1.0 - ADAM_B2 ** ADAM_STEP)

    def body(*refs):
        g_refs = refs[:L]
        w_ref, m_ref, v_ref, go_ref, d_ref, mo_ref, vo_ref = refs[L:]
        layer = pl.program_id(0)
        g = None
        for l, g_ref in enumerate(g_refs):
            gl = g_ref[0].astype(F32)
            for j in range(1, J):
                gl = gl + g_ref[j].astype(F32)
            g = gl if g is None else jnp.where(layer == l, gl, g)
        mn = ADAM_B1 * m_ref[...] + (1.0 - ADAM_B1) * g
        vn = ADAM_B2 * v_ref[...] + (1.0 - ADAM_B2) * (g * g)
        go_ref[...] = g
        mo_ref[...] = mn
        vo_ref[...] = vn
        d_ref[...] = -ADAM_LR * ((mn * c1) / (jnp.sqrt(vn * c2) + ADAM_EPS) + ADAM_WD * w_ref[...])

    def g_spec(l):
        return pl.BlockSpec((J, tr, Wd), lambda ll, i: (0, jnp.where(ll == l, i, jnp.where(ll < l, 0, nrt - 1)), 0))

    row = pl.BlockSpec((None, tr, Wd), lambda ll, i: (ll, i, 0))
    out = _sds((L, R, Wd))
    return _call(body, name=name, out_shape=(out, out, out, out), grid=(L, nrt),
                 in_specs=[g_spec(l) for l in range(L)] + [row, row, row], out_specs=(row, row, row, row))(*g_layers, w, m, v)


def _sum_slabs(slabs, *, name):
    n = len(slabs)

    def body(*refs):
        for g_ref, o_ref in zip(refs[:n], refs[n:]):
            g = g_ref[0]
            for j in range(1, g_ref.shape[0]):
                g = g + g_ref[j]
            o_ref[...] = g

    return _call(body, name=name, out_shape=[_sds(s.shape[1:]) for s in slabs])(*slabs)


RELAYOUT_ROWS = 256


def _col_plan(n, segments):
    plan = []
    for j in range(N_DEV):
        lo, hi = j * n, (j + 1) * n
        for t0, t1, oi, o0 in segments:
            a, b = max(lo, t0), min(hi, t1)
            if a < b:
                plan.append((j, a - lo, oi, o0 + a - t0, b - a))
    return plan


def _interleave_segments(F):
    seg = []
    for b in range(F // LANES):
        seg.append((b * LANES, (b + 1) * LANES, 0, 2 * b * LANES))
        seg.append((F + b * LANES, F + (b + 1) * LANES, 0, (2 * b + 1) * LANES))
    return seg


def _interleave_perm(F):
    perm = []
    for b in range(F // LANES):
        perm += list(range(b * LANES, (b + 1) * LANES)) + list(range(F + b * LANES, F + (b + 1) * LANES))
    inv = [0] * (2 * F)
    for d, s in enumerate(perm):
        inv[s] = d
    return jnp.asarray(perm, jnp.int32), jnp.asarray(inv, jnp.int32)


def _cols_from_slabs(slabs, plan, widths, *, name):
    _, R, n = slabs.shape
    tr = _row_tile(R, RELAYOUT_ROWS)
    covered = [sum(e[4] for e in plan if e[2] == i) for i in range(len(widths))]

    def body(s_ref, *o_refs):
        for i, o_ref in enumerate(o_refs):
            if covered[i] < widths[i]:
                o_ref[...] = jnp.zeros_like(o_ref)
        for j, sc, oi, oc, w in plan:
            o_refs[oi][:, oc:oc + w] = s_ref[j, :, sc:sc + w]

    return _call(body, name=name, out_shape=[_sds((R, w), slabs.dtype) for w in widths], grid=(R // tr,),
                 in_specs=[pl.BlockSpec((N_DEV, tr, n), lambda i: (0, i, 0))],
                 out_specs=[pl.BlockSpec((tr, w), lambda i: (i, 0)) for w in widths])(slabs)


def _slabs_from_cols(mats, plan, n, *, name):
    R = mats[0].shape[0]
    tr = _row_tile(R, RELAYOUT_ROWS)

    def body(*refs):
        m_refs, o_ref = refs[:-1], refs[-1]
        for j, sc, oi, oc, w in plan:
            o_ref[j, :, sc:sc + w] = m_refs[oi][:, oc:oc + w]

    return _call(body, name=name, out_shape=_sds((N_DEV, R, n), mats[0].dtype), grid=(R // tr,),
                 in_specs=[pl.BlockSpec((tr, m.shape[1]), lambda i: (i, 0)) for m in mats],
                 out_specs=pl.BlockSpec((N_DEV, tr, n), lambda i: (0, i, 0)))(*mats)


def kernel(x, positions, norm_mix, norm_ffn, norm_final, mix_w_in, pool_w, pool_scale, attn_sinks, mix_w_out, ssm_w_in, ssm_conv_w, ssm_conv_b, ssm_dt_bias, ssm_A_log, ssm_D, ssm_norm, ssm_w_out, ffn_w_up, ffn_conv_w, ffn_conv_b, ffn_w_down, loss_target, m_norm_mix, m_norm_ffn, m_norm_final, m_mix_w_in, m_pool_w, m_pool_scale, m_attn_sinks, m_mix_w_out, m_ssm_w_in, m_ssm_conv_w, m_ssm_conv_b, m_ssm_dt_bias, m_ssm_A_log, m_ssm_D, m_ssm_norm, m_ssm_w_out, m_ffn_w_up, m_ffn_conv_w, m_ffn_conv_b, m_ffn_w_down, v_norm_mix, v_norm_ffn, v_norm_final, v_mix_w_in, v_pool_w, v_pool_scale, v_attn_sinks, v_mix_w_out, v_ssm_w_in, v_ssm_conv_w, v_ssm_conv_b, v_ssm_dt_bias, v_ssm_A_log, v_ssm_D, v_ssm_norm, v_ssm_w_out, v_ffn_w_up, v_ffn_conv_w, v_ffn_conv_b, v_ffn_w_down):
    args = dict(locals())
    wl = {n: args[n] for n in WEIGHTS}
    ml = {n: args["m_" + n] for n in WEIGHTS}
    vl = {n: args["v_" + n] for n in WEIGHTS}
    D = x.shape[2]
    F = ffn_w_down.shape[1] * N_DEV
    DI, CD, NH = ssm_norm.shape[1] * N_DEV, ssm_conv_b.shape[1] * N_DEV, ssm_dt_bias.shape[1]
    Kc, Kf = ssm_conv_w.shape[1], ffn_conv_w.shape[1]
    n_mix, n_ssm, n_up = mix_w_in.shape[2], ssm_w_in.shape[2], ffn_w_up.shape[2]
    plan_mix = _col_plan(n_mix, [(0, N_DEV * n_mix, 0, 0)])
    plan_ssm = _col_plan(n_ssm, [(0, DI, 0, 0), (DI, DI + CD, 1, 0), (DI + CD, DI + CD + NH, 2, 0)])
    plan_up = _col_plan(n_up, _interleave_segments(F))
    perm, inv = _interleave_perm(F)

    def two(a):
        return a.reshape(-1, a.shape[-1])

    def pay(a):
        return a.astype(PAYLOAD)

    order = ("mix_in", "mix_out", "ffn0_up", "ffn0_down", "ssm", "ffn1_up", "ffn1_down")
    gops = {
        "mix_in": [(pay(two(mix_w_in)), "slab")],
        "mix_out": [(pay(mix_w_out[0]), "rows"), (two(ssm_conv_w), "slab"), (ssm_conv_b, "slab"), (ssm_norm, "slab"),
                    (two(ffn_conv_w), "slab")],
        "ffn0_up": [(pay(ffn_w_up[0]), "slab")], "ffn0_down": [(pay(ffn_w_down[0]), "rows")],
        "ssm": [(pay(two(ssm_w_in)), "slab"), (pay(ssm_w_out[0]), "rows")],
        "ffn1_up": [(pay(ffn_w_up[1]), "slab")], "ffn1_down": [(pay(ffn_w_down[1]), "rows")],
    }
    lands = _place_own([op for g in order for op in gops[g]], gather=True, name="gather_own")
    groups, off = [], 0
    for g in order:
        groups.append((gops[g], lands[off:off + len(gops[g])]))
        off += len(gops[g])
    gstates, token = _exchange_start(groups, gather=True, name="gather_start")
    gstate = dict(zip(order, gstates))
    P = {n: wl[n] for n in REPLICATED}
    P["ffn_conv_b"] = jnp.take(ffn_conv_b, perm, axis=1)

    def need(g, after):
        got = _exchange_wait(gops[g], gstate[g], after, gather=True, name="gather_wait_" + g)
        if g == "mix_in":
            (P["mix_w_in"],) = _cols_from_slabs(got[0], plan_mix, (N_DEV * n_mix,), name="unpack_mix_w_in")
        elif g == "mix_out":
            P.update(mix_w_out=got[0], ssm_conv_w=got[1].transpose(1, 0, 2).reshape(Kc, CD), ssm_conv_b=got[2].reshape(1, CD),
                     ssm_norm=got[3].reshape(1, DI),
                     ffn_conv_w=jnp.take(got[4].transpose(1, 0, 2).reshape(2 * Kf, 2 * F), perm, axis=1).reshape(2, Kf, 2 * F))
        elif g == "ssm":
            P["ssm_wz"], P["ssm_wxbc"], P["ssm_wdt"] = _cols_from_slabs(got[0], plan_ssm, (DI, CD, LANES), name="unpack_ssm_w_in")
            P["ssm_w_out"] = got[1]
        elif g.endswith("_up"):
            (P["ffn_w_up" + g[3]],) = _cols_from_slabs(got[0], plan_up, (2 * F,), name="unpack_ffn_w_up" + g[3])
        else:
            P["ffn_w_down" + g[3]] = got[0]

    sent = {}

    def emit(g, d):
        if g == "mix":
            ops = [(_slabs_from_cols([d["mix_w_in"]], plan_mix, n_mix, name="pack_mix_w_in"), "slab"), (d["mix_w_out"], "rows"),
                   (jnp.take(d["ffn_conv_w"].reshape(2 * Kf, 2 * F), inv, axis=1).reshape(2 * Kf, N_DEV, n_up).transpose(1, 0, 2), "slab")]
        elif g == "ssm":
            ops = [(_slabs_from_cols([d["ssm_wz"], d["ssm_wxbc"], d["ssm_wdt"]], plan_ssm, n_ssm, name="pack_ssm_w_in"), "slab"),
                   (d["ssm_w_out"], "rows"), (d["ssm_conv_w"].reshape(Kc, N_DEV, -1).transpose(1, 0, 2), "slab"),
                   (d["ssm_conv_b"].reshape(N_DEV, 1, -1), "slab"), (d["ssm_norm"].reshape(N_DEV, 1, -1), "slab")]
        else:
            ops = [(_slabs_from_cols([d["ffn_w_up"]], plan_up, n_up, name="pack_ffn_w_up" + g[3]), "slab"), (d["ffn_w_down"], "rows")]
        own = _place_own(ops, gather=False, name="scatter_own_" + g)
        (state,), tok = _exchange_start([(ops, own)], gather=False, name="scatter_start_" + g)
        sent[g] = (ops, state)
        return tok

    need("mix_in", token)
    loss_lanes, grad_x, G = _local_step(x[0], positions.reshape(-1, 1), loss_target[0], P, need, emit, after=token)
    loss = lax.psum(loss_lanes[0, 0], ("x", "y", "c"))

    res = {}

    def update(n, g_layers):
        L = len(g_layers)
        g_layers = [g.reshape(g.shape[0], -1, g.shape[-1]) for g in g_layers]
        shape = (L,) + g_layers[0].shape[1:]
        outs = _adamw(g_layers, wl[n].reshape(shape), ml[n].reshape(shape), vl[n].reshape(shape), name="adamw_" + n)
        for kind, a in zip(("grad", "delta", "new_m", "new_v"), outs):
            res[kind, n] = a.reshape(wl[n].shape)

    recv = {g: _exchange_wait(sent[g][0], sent[g][1], grad_x, gather=False, name="scatter_wait_" + g)
            for g in ("ffn1", "ssm", "ffn0", "mix")}
    update("mix_w_in", [recv["mix"][0]])
    update("mix_w_out", [recv["mix"][1]])
    update("ffn_conv_w", [recv["mix"][2]])
    update("ssm_w_in", [recv["ssm"][0]])
    update("ssm_w_out", [recv["ssm"][1]])
    update("ssm_conv_w", [recv["ssm"][2]])
    update("ssm_conv_b", [recv["ssm"][3]])
    update("ssm_norm", [recv["ssm"][4]])
    update("ffn_w_up", [recv["ffn0"][0], recv["ffn1"][0]])
    update("ffn_w_down", [recv["ffn0"][1], recv["ffn1"][1]])

    rep = _exchange(
        [(a, "slab") for a in (G["norm_mix"], G["norm_ffn"], G["norm_final"], G["pool_w"].reshape(-1, LANES), G["pool_scale"],
                               jnp.take(G["ffn_conv_b"], inv, axis=1), G["attn_sinks_rows"],
                               G["ssm_dt_bias_g"].reshape(SSM_G, LANES), G["ssm_A_log_g"].reshape(SSM_G, LANES),
                               G["ssm_D_g"].reshape(SSM_G, LANES))],
        gather=True, name="gather_small_grads")
    for n, r in zip(("norm_mix", "norm_ffn", "norm_final", "pool_w", "pool_scale", "ffn_conv_b"), rep):
        update(n, [r])
    sinks_rows, bias_g, alog_g, d_g = _sum_slabs(rep[6:], name="sum_head_grads")
    update("attn_sinks", [sinks_rows[:, 0].reshape(1, 1, N_HEADS)])
    update("ssm_dt_bias", [_ungroup(bias_g)[None]])
    update("ssm_A_log", [_ungroup(alog_g)[None]])
    update("ssm_D", [_ungroup(d_g)[None]])

    return (loss, grad_x[None], *[res[k, n] for k in ("grad", "delta", "new_m", "new_v") for n in WEIGHTS])
```

```python
import functools
import math

import jax
import jax.numpy as jnp
from jax import lax
from jax.experimental import pallas as pl
from jax.experimental.pallas import tpu as pltpu

F32 = jnp.float32
BF16 = jnp.bfloat16

N_DEV = 8
LANES = 128
HEAD_DIM = 64
N_KV_HEADS = 2
GQ = 4
N_HEADS = N_KV_HEADS * GQ
BLOCK = 128
POOL_GROUPS = 4
ROPE_THETA = 10000.0
SSM_P = 64
SSM_G = 8
SSM_R = 4
SSM_N = 128
SSM_L = 128
NORM_EPS = 1e-6
SSM_NORM_EPS = 1e-5
ADAM_LR, ADAM_B1, ADAM_B2, ADAM_EPS, ADAM_WD, ADAM_STEP = 0.001, 0.9, 0.999, 1e-08, 0.01, 10
VMEM_LIMIT = 56 * 2 ** 20
PAYLOAD = jnp.bfloat16

REPLICATED = ("norm_mix", "norm_ffn", "norm_final", "pool_w", "pool_scale", "attn_sinks",
              "ssm_dt_bias", "ssm_A_log", "ssm_D", "ffn_conv_b")
WEIGHTS = ("norm_mix", "norm_ffn", "norm_final", "mix_w_in", "pool_w", "pool_scale", "attn_sinks", "mix_w_out",
           "ssm_w_in", "ssm_conv_w", "ssm_conv_b", "ssm_dt_bias", "ssm_A_log", "ssm_D", "ssm_norm", "ssm_w_out",
           "ffn_w_up", "ffn_conv_w", "ffn_conv_b", "ffn_w_down")


def _tile(n, cap):
    if n <= cap:
        return n
    best = None
    for d in range(LANES, cap + 1, LANES):
        if n % d == 0:
            best = d
    assert best is not None, (n, cap)
    return best


def _call(body, *, name, out_shape, grid=None, in_specs=None, out_specs=None, scratch=(), aliases=None):
    kw = {}
    if grid is not None:
        kw = dict(grid=grid, in_specs=in_specs, out_specs=out_specs)
    if aliases:
        kw["input_output_aliases"] = aliases
    return pl.pallas_call(
        body, name=name, out_shape=out_shape, scratch_shapes=list(scratch),
        compiler_params=pltpu.CompilerParams(vmem_limit_bytes=VMEM_LIMIT), **kw)


ANY = pl.BlockSpec(memory_space=pl.ANY)


def _sds(shape, dtype=F32):
    return jax.ShapeDtypeStruct(tuple(shape), dtype)


def _sigmoid(x):
    return 1.0 / (1.0 + jnp.exp(-x))


def _shift_dn(x, d, t):
    if d == 0:
        return x
    return jnp.where(t >= d, pltpu.roll(x, d, axis=0), 0.0)


def _shift_up(x, d, t):
    if d == 0:
        return x
    n = x.shape[0]
    return jnp.where(t < n - d, pltpu.roll(x, n - d, axis=0), 0.0)


def _mm(a, b, *, name, ta=False, tb=False, res=None, out_dtype=F32):
    M, K = (a.shape[1], a.shape[0]) if ta else a.shape
    N = b.shape[0] if tb else b.shape[1]
    assert (b.shape[1] if tb else b.shape[0]) == K, (a.shape, b.shape, ta, tb)
    tm, tn, tk = _tile(M, 1408), _tile(N, 1408), _tile(K, 1408)
    nk = K // tk
    dims = (((0 if ta else 1,), (1 if tb else 0,)), ((), ()))

    def body(*refs):
        a_ref, b_ref = refs[:2]
        r_ref = refs[2] if res is not None else None
        o_ref, acc = refs[-2:]
        k = pl.program_id(2)

        @pl.when(k == 0)
        def _():
            acc[...] = jnp.zeros_like(acc)

        acc[...] += lax.dot_general(a_ref[...].astype(BF16), b_ref[...].astype(BF16), dims,
                                    preferred_element_type=F32)

        @pl.when(k == nk - 1)
        def _():
            out = acc[...]
            if res is not None:
                out = out + r_ref[...]
            o_ref[...] = out.astype(out_dtype)

    a_spec = pl.BlockSpec((tk, tm), lambda i, j, k: (k, i)) if ta else pl.BlockSpec((tm, tk), lambda i, j, k: (i, k))
    b_spec = pl.BlockSpec((tn, tk), lambda i, j, k: (j, k)) if tb else pl.BlockSpec((tk, tn), lambda i, j, k: (k, j))
    o_spec = pl.BlockSpec((tm, tn), lambda i, j, k: (i, j))
    ins, specs = [a, b], [a_spec, b_spec]
    if res is not None:
        ins.append(res)
        specs.append(o_spec)
    return _call(body, name=name, out_shape=_sds((M, N), out_dtype), grid=(M // tm, N // tn, nk), in_specs=specs,
                 out_specs=o_spec, scratch=[pltpu.VMEM((tm, tn), F32)])(*ins)


def _rmsnorm(x, w, *, name, eps=NORM_EPS, after=None):
    S, D = x.shape
    tm = _tile(S, 512)
    tie = [] if after is None else [after]

    def body(x_ref, w_ref, *rest):
        o_ref = rest[-1]
        xf = x_ref[...]
        r = lax.rsqrt(jnp.mean(xf * xf, axis=-1, keepdims=True) + eps)
        o_ref[...] = (xf * r * w_ref[...]).astype(BF16)

    return _call(body, name=name, out_shape=_sds((S, D), BF16), grid=(S // tm,),
                 in_specs=[pl.BlockSpec((tm, D), lambda i: (i, 0)), pl.BlockSpec((1, D), lambda i: (0, 0))] + [ANY] * len(tie),
                 out_specs=pl.BlockSpec((tm, D), lambda i: (i, 0)))(x, w, *tie)


def _norm_bwd_math(xf, w, dh, eps):
    r = lax.rsqrt(jnp.mean(xf * xf, axis=-1, keepdims=True) + eps)
    xhat = xf * r
    dxh = dh * w
    dx = r * (dxh - xhat * jnp.mean(dxh * xhat, axis=-1, keepdims=True))
    dw = jnp.sum(dh * xhat, axis=0, keepdims=True)
    return dx, dw


def _rmsnorm_bwd(x, w, dh, dres, *, name, eps=NORM_EPS, after=None):
    S, D = x.shape
    tm = _tile(S, 512)
    tie = [] if after is None else [after]

    def body(x_ref, w_ref, dh_ref, dr_ref, *rest):
        dx_ref, dw_ref = rest[-2:]
        dx, dw = _norm_bwd_math(x_ref[...], w_ref[...], dh_ref[...].astype(F32), eps)
        dx_ref[...] = dr_ref[...] + dx

        @pl.when(pl.program_id(0) == 0)
        def _():
            dw_ref[...] = jnp.zeros_like(dw_ref)

        dw_ref[...] += dw

    row = pl.BlockSpec((tm, D), lambda i: (i, 0))
    vec = pl.BlockSpec((1, D), lambda i: (0, 0))
    return _call(body, name=name, out_shape=(_sds((S, D)), _sds((1, D))), grid=(S // tm,),
                 in_specs=[row, vec, row, row] + [ANY] * len(tie), out_specs=(row, vec))(x, w, dh, dres, *tie)


def _final_loss(x, w, target, *, name):
    S, D = x.shape
    tm = _tile(S, 512)

    def body(x_ref, w_ref, t_ref, loss_ref, dx_ref, dw_ref):
        xf, wv = x_ref[...], w_ref[...]
        r = lax.rsqrt(jnp.mean(xf * xf, axis=-1, keepdims=True) + NORM_EPS)
        err = xf * r * wv - t_ref[...]
        part = 0.5 * jnp.sum(jnp.mean(err * err, axis=-1, keepdims=True), axis=0, keepdims=True)
        dx, dw = _norm_bwd_math(xf, wv, err * (1.0 / D), NORM_EPS)
        dx_ref[...] = dx

        @pl.when(pl.program_id(0) == 0)
        def _():
            dw_ref[...] = jnp.zeros_like(dw_ref)
            loss_ref[...] = jnp.zeros_like(loss_ref)

        dw_ref[...] += dw
        loss_ref[...] += jnp.broadcast_to(part, loss_ref.shape)

    row = pl.BlockSpec((tm, D), lambda i: (i, 0))
    vec = pl.BlockSpec((1, D), lambda i: (0, 0))
    return _call(body, name=name, out_shape=(_sds((1, LANES)), _sds((S, D)), _sds((1, D))), grid=(S // tm,),
                 in_specs=[row, vec, row], out_specs=(pl.BlockSpec((1, LANES), lambda i: (0, 0)), row, vec))(x, w, target)


def _rope_tables(pos, inv_freq):
    S = pos.shape[0]
    tm = _tile(S, 512)

    def body(p_ref, f_ref, c_ref, s_ref):
        ang = p_ref[...].astype(F32) * f_ref[...]
        c_ref[...] = jnp.cos(ang)
        s_ref[...] = jnp.sin(ang)

    blk = pl.BlockSpec((tm, LANES), lambda i: (i, 0))
    return _call(body, name="rope_tables", out_shape=(_sds((S, LANES)), _sds((S, LANES))), grid=(S // tm,),
                 in_specs=[pl.BlockSpec((tm, 1), lambda i: (i, 0)), pl.BlockSpec((1, LANES), lambda i: (0, 0))],
                 out_specs=(blk, blk))(pos, inv_freq)


def _rot_half(t):
    lane = lax.broadcasted_iota(jnp.int32, t.shape, 1)
    lo = (lane % HEAD_DIM) < (HEAD_DIM // 2)
    return jnp.where(lo, -pltpu.roll(t, LANES - HEAD_DIM // 2, axis=1), pltpu.roll(t, HEAD_DIM // 2, axis=1))


def _rope(t, c, s):
    return t * c + _rot_half(t) * s


def _unrope(dy, c, s):
    return dy * c - _rot_half(dy * s)


PD = POOL_GROUPS * LANES
QD = N_HEADS * HEAD_DIM
KD = N_KV_HEADS * HEAD_DIM
assert PD % QD == 0 and (PD + QD) % (2 * KD) == 0 and KD == LANES
def _attn_probs(q, kcat, sink, mask):
    s = lax.dot_general(q.astype(BF16), kcat, (((1,), (1,)), ((), ())), preferred_element_type=F32) * (HEAD_DIM ** -0.5)
    s = jnp.where(mask, s, -jnp.inf)
    m = jnp.maximum(jnp.max(s, axis=1, keepdims=True), sink)
    p = jnp.exp(s - m)
    ps = jnp.exp(sink - m)
    inv = 1.0 / (jnp.sum(p, axis=1, keepdims=True) + ps)
    return p * inv, ps * inv


def _attn_mask(n):
    qi = lax.broadcasted_iota(jnp.int32, (BLOCK, 2 * BLOCK), 0)
    kj = lax.broadcasted_iota(jnp.int32, (BLOCK, 2 * BLOCK), 1)
    rel = qi + BLOCK - kj
    return (rel >= 0) & (rel < BLOCK) & ((n > 0) | (kj >= BLOCK))


def _attn_in_specs(nb):
    def cur(n):
        return jnp.minimum(n, nb - 1)

    def prev(n):
        return jnp.clip(n - 1, 0, nb - 1)

    kvb = (PD + QD) // (2 * KD)
    return [pl.BlockSpec(memory_space=pltpu.SMEM),
            pl.BlockSpec((BLOCK, QD), lambda n: (cur(n), PD // QD)),
            pl.BlockSpec((BLOCK, 2 * KD), lambda n: (cur(n), kvb)),
            pl.BlockSpec((BLOCK, 2 * KD), lambda n: (prev(n), kvb)),
            pl.BlockSpec((BLOCK, LANES), lambda n: (cur(n), 0)), pl.BlockSpec((BLOCK, LANES), lambda n: (cur(n), 0)),
            pl.BlockSpec((BLOCK, LANES), lambda n: (prev(n), 0)), pl.BlockSpec((BLOCK, LANES), lambda n: (prev(n), 0))]


def _attn_keys(kvc_ref, kvp_ref, cc, sc, cp, sp):
    kc = _rope(kvc_ref[:, :KD], cc, sc)
    kp = _rope(kvp_ref[:, :KD], cp, sp)
    vc, vp = kvc_ref[:, KD:], kvp_ref[:, KD:]
    kcat, vcat = [], []
    for kk in range(N_KV_HEADS):
        sl = slice(kk * HEAD_DIM, (kk + 1) * HEAD_DIM)
        kcat.append(jnp.concatenate([kp[:, sl], kc[:, sl]], axis=0).astype(BF16))
        vcat.append(jnp.concatenate([vp[:, sl], vc[:, sl]], axis=0).astype(BF16))
    return kcat, vcat


def _attn_fwd(proj, cos, sin, sinks, cat):
    S = proj.shape[0]
    nb = S // BLOCK

    def body(sink_ref, q_ref, kvc_ref, kvp_ref, cc_ref, sc_ref, cp_ref, sp_ref, cat_ref, o_ref):
        n = pl.program_id(0)
        cc, sc = cc_ref[...], sc_ref[...]
        kcat, vcat = _attn_keys(kvc_ref, kvp_ref, cc, sc, cp_ref[...], sp_ref[...])
        mask = _attn_mask(n)
        for j in range(QD // LANES):
            qr = _rope(q_ref[:, j * LANES:(j + 1) * LANES], cc, sc)
            for e in range(LANES // HEAD_DIM):
                h = j * (LANES // HEAD_DIM) + e
                pn, _ = _attn_probs(qr[:, e * HEAD_DIM:(e + 1) * HEAD_DIM], kcat[h // GQ], sink_ref[0, h], mask)
                o_ref[:, h * HEAD_DIM:(h + 1) * HEAD_DIM] = jnp.dot(pn.astype(BF16), vcat[h // GQ], preferred_element_type=F32)

    return _call(body, name="attn_fwd", out_shape=_sds(cat.shape), grid=(nb,),
                 in_specs=_attn_in_specs(nb) + [ANY], out_specs=pl.BlockSpec((BLOCK, QD), lambda n: (n, PD // QD)),
                 aliases={8: 0})(sinks, proj, proj, proj, cos, sin, cos, sin, cat)


def _attn_bwd(proj, cos, sin, sinks, dcat):
    S = proj.shape[0]
    nb = S // BLOCK
    scale = HEAD_DIM ** -0.5
    per = LANES // HEAD_DIM

    def body(sink_ref, q_ref, kvc_ref, kvp_ref, cc_ref, sc_ref, cp_ref, sp_ref, do_ref, o_ref, ds_ref, hold, carry, part, pair):
        n = pl.program_id(0)

        @pl.when(n == 0)
        def _():
            hold[...] = jnp.zeros_like(hold)
            carry[...] = jnp.zeros_like(carry)
            ds_ref[...] = jnp.zeros_like(ds_ref)

        live = jnp.where(n < nb, 1.0, 0.0)
        cc, sc, cp, sp = cc_ref[...], sc_ref[...], cp_ref[...], sp_ref[...]
        kcat, vcat = _attn_keys(kvc_ref, kvp_ref, cc, sc, cp, sp)
        mask = _attn_mask(n)
        o_ref[:, :PD] = jnp.zeros((BLOCK, PD), F32)
        o_ref[:, PD:PD + QD] = hold[...]
        dk = [jnp.zeros((2 * BLOCK, HEAD_DIM), F32) for _ in range(N_KV_HEADS)]
        dv = [jnp.zeros((2 * BLOCK, HEAD_DIM), F32) for _ in range(N_KV_HEADS)]
        row = lax.broadcasted_iota(jnp.int32, (8, LANES), 0)
        dsk = jnp.zeros((8, LANES), F32)
        for j in range(QD // LANES):
            qr = _rope(q_ref[:, j * LANES:(j + 1) * LANES], cc, sc)
            for e in range(per):
                h = j * per + e
                kk = h // GQ
                qh = qr[:, e * HEAD_DIM:(e + 1) * HEAD_DIM]
                pn, psn = _attn_probs(qh, kcat[kk], sink_ref[0, h], mask)
                doh = (do_ref[:, h * HEAD_DIM:(h + 1) * HEAD_DIM] * live).astype(BF16)
                dp = lax.dot_general(doh, vcat[kk], NT, preferred_element_type=F32)
                delta = jnp.sum(pn * dp, axis=1, keepdims=True)
                ds = (pn * (dp - delta) * scale).astype(BF16)
                pair[:, e * HEAD_DIM:(e + 1) * HEAD_DIM] = jnp.dot(ds, kcat[kk], preferred_element_type=F32)
                dk[kk] = dk[kk] + lax.dot_general(ds, qh.astype(BF16), TN, preferred_element_type=F32)
                dv[kk] = dv[kk] + lax.dot_general(pn.astype(BF16), doh, TN, preferred_element_type=F32)
                dsk = dsk + jnp.where(row == h, -jnp.sum(psn * delta), 0.0)
            hold[:, j * LANES:(j + 1) * LANES] = _unrope(pair[...], cc, sc)
        for kk in range(N_KV_HEADS):
            sl = slice(kk * HEAD_DIM, (kk + 1) * HEAD_DIM)
            sv = slice(KD + kk * HEAD_DIM, KD + (kk + 1) * HEAD_DIM)
            part[0, :, sl] = dk[kk][:BLOCK]
            part[0, :, sv] = dv[kk][:BLOCK]
            part[1, :, sl] = dk[kk][BLOCK:]
            part[1, :, sv] = dv[kk][BLOCK:]
        done = carry[...] + part[0]
        o_ref[:, PD + QD:PD + QD + KD] = _unrope(done[:, :KD], cp, sp)
        o_ref[:, PD + QD + KD:] = done[:, KD:]
        carry[...] = part[1]
        ds_ref[...] += dsk

    return _call(body, name="attn_bwd", out_shape=(_sds((S, PD + QD + 2 * KD)), _sds((8, LANES))), grid=(nb + 1,),
                 in_specs=_attn_in_specs(nb) + [pl.BlockSpec((BLOCK, QD), lambda n: (jnp.minimum(n, nb - 1), PD // QD))],
                 out_specs=(pl.BlockSpec((BLOCK, PD + QD + 2 * KD), lambda n: (jnp.maximum(n - 1, 0), 0)),
                            pl.BlockSpec((8, LANES), lambda n: (0, 0))),
                 scratch=[pltpu.VMEM((BLOCK, QD), F32), pltpu.VMEM((BLOCK, 2 * KD), F32),
                          pltpu.VMEM((2, BLOCK, 2 * KD), F32), pltpu.VMEM((BLOCK, LANES), F32)])(
                     sinks, proj, proj, proj, cos, sin, cos, sin, dcat)


def _pool_sums(u, g, t, shift):
    s2 = u + shift(u, 1, t)
    s4 = s2 + shift(s2, 2, t)
    s8 = s4 + shift(s4, 4, t)
    s16 = s8 + shift(s8, 8, t)
    return jnp.where(g == 0, s2, jnp.where(g == 1, s4, jnp.where(g == 2, s8, s16)))


def _pool_specs(S):
    col = pl.BlockSpec((S, LANES), lambda g: (0, g))
    wsp = pl.BlockSpec((1, LANES, LANES), lambda g: (g, 0, 0))
    vec = pl.BlockSpec((1, LANES), lambda g: (0, g))
    return col, wsp, vec


def _pool_fwd(proj, pool_w, scale):
    S = proj.shape[0]
    col, wsp, vec = _pool_specs(S)

    def body(u_ref, w_ref, s_ref, o_ref):
        g = pl.program_id(0)
        u = u_ref[...]
        t = lax.broadcasted_iota(jnp.int32, u.shape, 0)
        cnt = jnp.minimum(t + 1, 2 << g).astype(F32)
        pm = _pool_sums(u, g, t, _shift_dn) / cnt - u
        o_ref[...] = jnp.dot(pm.astype(BF16), w_ref[0].astype(BF16), preferred_element_type=F32) * s_ref[...]

    return _call(body, name="pool_fwd", out_shape=_sds((S, PD + QD)), grid=(POOL_GROUPS,),
                 in_specs=[col, wsp, vec], out_specs=col)(proj, pool_w, scale)


def _pool_bwd(proj, pool_w, scale, dcat, dproj):
    S = proj.shape[0]
    col, wsp, vec = _pool_specs(S)

    def body(u_ref, w_ref, s_ref, d_ref, dproj_ref, du_ref, dw_ref, dsc_ref):
        g = pl.program_id(0)
        u = u_ref[...]
        t = lax.broadcasted_iota(jnp.int32, u.shape, 0)
        cnt = jnp.minimum(t + 1, 2 << g).astype(F32)
        pm = (_pool_sums(u, g, t, _shift_dn) / cnt - u).astype(BF16)
        wv = w_ref[0].astype(BF16)
        d = d_ref[...]
        pw = jnp.dot(pm, wv, preferred_element_type=F32)
        dsc_ref[...] = jnp.sum(pw * d, axis=0, keepdims=True)
        dpw = (d * s_ref[...]).astype(BF16)
        dw_ref[0] = lax.dot_general(pm, dpw, (((0,), (0,)), ((), ())), preferred_element_type=F32)
        dpm = lax.dot_general(dpw, wv, (((1,), (1,)), ((), ())), preferred_element_type=F32)
        du_ref[...] = _pool_sums(dpm / cnt, g, t, _shift_up) - dpm

    return _call(body, name="pool_bwd",
                 out_shape=(_sds(dproj.shape), _sds((POOL_GROUPS, LANES, LANES)), _sds((1, POOL_GROUPS * LANES))),
                 grid=(POOL_GROUPS,), in_specs=[col, wsp, vec, col, ANY], out_specs=(col, wsp, vec),
                 aliases={4: 0})(proj, pool_w, scale, dcat, dproj)


def _conv(x, w_ref, b_ref, t):
    K = w_ref.shape[0]
    y = b_ref[...] + jnp.zeros_like(x)
    for k in range(K):
        y = y + w_ref[k:k + 1, :] * _shift_dn(x, K - 1 - k, t)
    return y


def _conv_bwd(x, dy, w_ref, t):
    K = w_ref.shape[0]
    dx = jnp.zeros_like(x)
    dws = []
    for k in range(K):
        dx = dx + w_ref[k:k + 1, :] * _shift_up(dy, K - 1 - k, t)
        dws.append(jnp.sum(dy * _shift_dn(x, K - 1 - k, t), axis=0, keepdims=True))
    return dx, dws, jnp.sum(dy, axis=0, keepdims=True)


def _silu_grad(y):
    sg = _sigmoid(y)
    return sg * (1.0 + y * (1.0 - sg))


def _ffn_mid_specs(S, K, layer):
    return [pl.BlockSpec((S, LANES), lambda j: (0, 2 * j)), pl.BlockSpec((S, LANES), lambda j: (0, 2 * j + 1)),
            pl.BlockSpec((None, K, LANES), lambda j: (layer, 0, 2 * j)), pl.BlockSpec((None, K, LANES), lambda j: (layer, 0, 2 * j + 1)),
            pl.BlockSpec((None, 1, LANES), lambda j: (layer, 0, 2 * j)), pl.BlockSpec((None, 1, LANES), lambda j: (layer, 0, 2 * j + 1))]


def _ffn_mid_fwd(a, cw, cb, layer):
    S, F2 = a.shape
    nf = F2 // (2 * LANES)
    K = cw.shape[1]

    def body(au_ref, ag_ref, wu_ref, wg_ref, bu_ref, bg_ref, o_ref):
        t = lax.broadcasted_iota(jnp.int32, (S, LANES), 0)
        hu = _conv(au_ref[...], wu_ref, bu_ref, t)
        hg = _conv(ag_ref[...], wg_ref, bg_ref, t)
        o_ref[...] = (hg * _sigmoid(hg) * hu).astype(BF16)

    return _call(body, name="ffn_mid_fwd", out_shape=_sds((S, F2 // 2), BF16), grid=(nf,),
                 in_specs=_ffn_mid_specs(S, K, layer), out_specs=pl.BlockSpec((S, LANES), lambda j: (0, j)))(
                     a, a, cw, cw, cb[:, None], cb[:, None])


def _ffn_mid_bwd(a, cw, cb, layer, dact):
    S, F2 = a.shape
    nf = F2 // (2 * LANES)
    K = cw.shape[1]

    def body(au_ref, ag_ref, wu_ref, wg_ref, bu_ref, bg_ref, d_ref, da_ref, dw_ref, db_ref):
        t = lax.broadcasted_iota(jnp.int32, (S, LANES), 0)
        au, ag = au_ref[...], ag_ref[...]
        hu = _conv(au, wu_ref, bu_ref, t)
        hg = _conv(ag, wg_ref, bg_ref, t)
        d = d_ref[...].astype(F32)
        dhu = d * hg * _sigmoid(hg)
        dhg = d * hu * _silu_grad(hg)
        dau, dwu, dbu = _conv_bwd(au, dhu, wu_ref, t)
        dag, dwg, dbg = _conv_bwd(ag, dhg, wg_ref, t)
        da_ref[:, :LANES] = dau.astype(BF16)
        da_ref[:, LANES:] = dag.astype(BF16)
        for k in range(K):
            dw_ref[k:k + 1, :LANES] = dwu[k]
            dw_ref[k:k + 1, LANES:] = dwg[k]
        db_ref[:, :LANES] = dbu
        db_ref[:, LANES:] = dbg

    return _call(body, name="ffn_mid_bwd", out_shape=(_sds((S, F2), BF16), _sds((K, F2)), _sds((1, F2))), grid=(nf,),
                 in_specs=_ffn_mid_specs(S, K, layer) + [pl.BlockSpec((S, LANES), lambda j: (0, j))],
                 out_specs=(pl.BlockSpec((S, 2 * LANES), lambda j: (0, j)), pl.BlockSpec((K, 2 * LANES), lambda j: (0, j)),
                            pl.BlockSpec((1, 2 * LANES), lambda j: (0, j))))(a, a, cw, cw, cb[:, None], cb[:, None], dact)


def _conv_silu_fwd(x, cw, cb):
    S = x.shape[0]
    K, C = cw.shape

    def body(x_ref, w_ref, b_ref, o_ref):
        t = lax.broadcasted_iota(jnp.int32, (S, LANES), 0)
        y = _conv(x_ref[...], w_ref, b_ref, t)
        o_ref[...] = y * _sigmoid(y)

    col = pl.BlockSpec((S, LANES), lambda j: (0, j))
    return _call(body, name="conv_silu_fwd", out_shape=_sds((S, C)), grid=(C // LANES,),
                 in_specs=[col, pl.BlockSpec((K, LANES), lambda j: (0, j)), pl.BlockSpec((1, LANES), lambda j: (0, j))],
                 out_specs=col)(x, cw, cb)


def _conv_silu_bwd(x, cw, cb, douts):
    S = x.shape[0]
    K, C = cw.shape
    starts, off = [], 0
    for d in douts:
        starts.append(off)
        off += d.shape[1] // LANES
    assert off == C // LANES

    def body(x_ref, w_ref, b_ref, *rest):
        d_refs, (dx_ref, dw_ref, db_ref) = rest[:len(douts)], rest[len(douts):]
        j = pl.program_id(0)
        t = lax.broadcasted_iota(jnp.int32, (S, LANES), 0)
        x = x_ref[...]
        y = _conv(x, w_ref, b_ref, t)
        d = d_refs[0][...]
        for i in range(1, len(douts)):
            d = jnp.where(j >= starts[i], d_refs[i][...], d)
        dy = d * _silu_grad(y)
        dx, dw, db = _conv_bwd(x, dy, w_ref, t)
        dx_ref[...] = dx.astype(BF16)
        for k in range(K):
            dw_ref[k:k + 1, :] = dw[k]
        db_ref[...] = db

    col = pl.BlockSpec((S, LANES), lambda j: (0, j))
    wsp = pl.BlockSpec((K, LANES), lambda j: (0, j))
    bsp = pl.BlockSpec((1, LANES), lambda j: (0, j))

    def dspec(i):
        nblk = douts[i].shape[1] // LANES
        return pl.BlockSpec((S, LANES), lambda j: (0, jnp.clip(j - starts[i], 0, nblk - 1)))

    return _call(body, name="conv_silu_bwd", out_shape=(_sds((S, C), BF16), _sds((K, C)), _sds((1, C))), grid=(C // LANES,),
                 in_specs=[col, wsp, bsp] + [dspec(i) for i in range(len(douts))],
                 out_specs=(col, wsp, bsp))(x, cw, cb, *douts)


HI = lax.Precision.HIGHEST


def _ssd_prep_fwd(proj, col0, bias_g, alog_g):
    S = proj.shape[0]
    nc = S // SSM_L
    b0 = col0 // LANES

    def body(raw_ref, b_ref, al_ref, pre_ref, dt_ref, acs_ref, acst_ref):
        r_i = lax.broadcasted_iota(jnp.int32, (LANES, LANES), 0)
        c_i = lax.broadcasted_iota(jnp.int32, (LANES, LANES), 1)
        live = c_i < SSM_R
        tril = jnp.where(r_i >= c_i, 1.0, 0.0)
        raw = raw_ref[...]
        for g in range(SSM_G):
            sel = jnp.where((r_i == SSM_R * g + c_i) & live, 1.0, 0.0)
            pre = jnp.dot(raw, sel, preferred_element_type=F32, precision=HI) + b_ref[g]
            dt = jnp.where(live, jnp.logaddexp(pre, 0.0), 0.0)
            a = dt * (-jnp.exp(al_ref[g]))
            acs = jnp.dot(tril, a, preferred_element_type=F32, precision=HI)
            pre_ref[g] = pre
            dt_ref[g] = dt
            acs_ref[g] = acs
            acst_ref[g] = acs.T

    gsp = pl.BlockSpec((SSM_G, 1, LANES), lambda c: (0, 0, 0))
    blk = pl.BlockSpec((SSM_G, SSM_L, LANES), lambda c: (0, c, 0))
    big = _sds((SSM_G, S, LANES))
    return _call(body, name="ssd_prep_fwd", out_shape=(big, big, big, _sds((SSM_G, LANES, S))), grid=(nc,),
                 in_specs=[pl.BlockSpec((SSM_L, LANES), lambda c: (c, b0)), gsp, gsp],
                 out_specs=(blk, blk, blk, pl.BlockSpec((SSM_G, LANES, SSM_L), lambda c: (0, 0, c))))(proj, bias_g, alog_g)


def _ssd_prep_bwd(pre_g, dt_g, alog_g, ddt_g, dacs_g, dacst_g):
    S = pre_g.shape[1]
    nc = S // SSM_L

    def body(pre_ref, dt_ref, al_ref, ddt_ref, dacs_ref, dacst_ref, draw_ref, db_ref, dal_ref):
        c = pl.program_id(0)
        r_i = lax.broadcasted_iota(jnp.int32, (LANES, LANES), 0)
        c_i = lax.broadcasted_iota(jnp.int32, (LANES, LANES), 1)
        live = c_i < SSM_R
        triu = jnp.where(r_i <= c_i, 1.0, 0.0)

        @pl.when(c == 0)
        def _():
            db_ref[...] = jnp.zeros_like(db_ref)
            dal_ref[...] = jnp.zeros_like(dal_ref)

        draw = jnp.zeros((SSM_L, LANES), F32)
        for g in range(SSM_G):
            dacs = dacs_ref[g] + dacst_ref[g].T
            da = jnp.dot(triu, dacs, preferred_element_type=F32, precision=HI)
            A = -jnp.exp(al_ref[g])
            ddt = ddt_ref[g] + da * A
            dpre = jnp.where(live, ddt * _sigmoid(pre_ref[g]), 0.0)
            unsel = jnp.where((c_i == SSM_R * g + r_i) & (r_i < SSM_R), 1.0, 0.0)
            draw = draw + jnp.dot(dpre, unsel, preferred_element_type=F32, precision=HI)
            db_ref[g] += jnp.sum(dpre, axis=0, keepdims=True)
            dal_ref[g] += jnp.where(live[:1], jnp.sum(da * dt_ref[g], axis=0, keepdims=True) * A, 0.0)
        draw_ref[...] = draw

    gsp = pl.BlockSpec((SSM_G, 1, LANES), lambda c: (0, 0, 0))
    blk = pl.BlockSpec((SSM_G, SSM_L, LANES), lambda c: (0, c, 0))
    gout = _sds((SSM_G, 1, LANES))
    return _call(body, name="ssd_prep_bwd", out_shape=(_sds((S, LANES)), gout, gout), grid=(nc,),
                 in_specs=[blk, blk, gsp, blk, blk, pl.BlockSpec((SSM_G, LANES, SSM_L), lambda c: (0, 0, c))],
                 out_specs=(pl.BlockSpec((SSM_L, LANES), lambda c: (c, 0)), gsp, gsp))(pre_g, dt_g, alog_g, ddt_g, dacs_g, dacst_g)


NT = (((1,), (1,)), ((), ()))
TN = (((0,), (0,)), ((), ()))
SSM_HP = SSM_R * SSM_P


def _ssd_group_terms(xs_ref, dt_ref, acs_ref, d_ref):
    hid = lax.broadcasted_iota(jnp.int32, (1, SSM_HP), 1) // SSM_P
    rid = lax.broadcasted_iota(jnp.int32, (SSM_HP, 1), 0) // SSM_P

    def widen(cols):
        out = cols[0]
        for r in range(1, SSM_R):
            out = jnp.where(hid == r, cols[r], out)
        return out

    dt_c = [dt_ref[:, r:r + 1] for r in range(SSM_R)]
    acs_c = [acs_ref[:, r:r + 1] for r in range(SSM_R)]
    last = [acs_ref[SSM_L - 1:SSM_L, r:r + 1] for r in range(SSM_R)]
    decay_c = [jnp.exp(last[r] - acs_c[r]) for r in range(SSM_R)]
    cd = [jnp.exp(last[r]) for r in range(SSM_R)]
    cd_rows = cd[0]
    for r in range(1, SSM_R):
        cd_rows = jnp.where(rid == r, cd[r], cd_rows)
    xs = xs_ref[...]
    return (xs, xs * widen(dt_c), widen([jnp.exp(a) for a in acs_c]), widen(decay_c),
            widen([d_ref[:, r:r + 1] for r in range(SSM_R)]), cd_rows, dt_c, decay_c, cd)


def _ssd_lmat(acs_ref, acst_ref, r, tril):
    return jnp.exp(jnp.where(tril, acs_ref[:, r:r + 1] - acst_ref[r:r + 1, :], -jnp.inf))


SSM_GPS = 2


def _ssd_specs(rev, nc):
    def cc(c):
        return nc - 1 - c if rev else c
    xs_blocks = (SSM_G * SSM_HP) // (SSM_GPS * SSM_N)
    xs = pl.BlockSpec((SSM_L, SSM_GPS * SSM_HP), lambda g, c: (cc(c), g))
    bsp = pl.BlockSpec((SSM_L, SSM_GPS * SSM_N), lambda g, c: (cc(c), xs_blocks + g))
    csp = pl.BlockSpec((SSM_L, SSM_GPS * SSM_N), lambda g, c: (cc(c), xs_blocks + SSM_G // SSM_GPS + g))
    sc = pl.BlockSpec((SSM_GPS, SSM_L, LANES), lambda g, c: (g, cc(c), 0))
    sct = pl.BlockSpec((SSM_GPS, LANES, SSM_L), lambda g, c: (g, 0, cc(c)))
    gsp = pl.BlockSpec((SSM_GPS, 1, LANES), lambda g, c: (g, 0, 0))
    st = pl.BlockSpec((None, SSM_GPS, SSM_HP, SSM_N), lambda g, c: (cc(c), g, 0, 0))
    return xs, bsp, csp, sc, sct, gsp, st


def _interleave(gens):
    live = list(gens)
    while live:
        for g in list(live):
            try:
                next(g)
            except StopIteration:
                live.remove(g)


def _rounds(gens):
    live = list(gens)
    while live:
        for g in list(live):
            try:
                next(g)
            except StopIteration:
                live.remove(g)
        yield


def _ssd_group_views(gg, xs_ref, b_ref, c_ref, *per_group):
    return (xs_ref.at[:, gg * SSM_HP:(gg + 1) * SSM_HP], b_ref.at[:, gg * SSM_N:(gg + 1) * SSM_N],
            c_ref.at[:, gg * SSM_N:(gg + 1) * SSM_N]) + tuple(r.at[gg] for r in per_group)


def _ssd_fwd(xbc, dt_g, acs_g, acst_g, d_g):
    S = xbc.shape[0]
    nc = S // SSM_L
    xs_s, b_s, c_s, sc, sct, gsp, st = _ssd_specs(False, nc)

    def body(xs_ref, b_ref, c_ref, dt_ref, acs_ref, acst_ref, d_ref, y_ref, st_ref, state):
        c = pl.program_id(1)

        @pl.when(c == 0)
        def _():
            state[...] = jnp.zeros_like(state)

        tril = lax.broadcasted_iota(jnp.int32, (SSM_L, SSM_L), 0) >= lax.broadcasted_iota(jnp.int32, (SSM_L, SSM_L), 1)
        def group(gg):
            xs_v, b_v, c_v, dt_v, acs_v, acst_v, d_v, st_v, state_v = _ssd_group_views(
                gg, xs_ref, b_ref, c_ref, dt_ref, acs_ref, acst_ref, d_ref, st_ref, state)
            y_v = y_ref.at[:, gg * SSM_HP:(gg + 1) * SSM_HP]
            Bb, Cb = b_v[...].astype(BF16), c_v[...].astype(BF16)
            Gm = lax.dot_general(Cb, Bb, NT, preferred_element_type=F32)
            yield
            xs, X, e_all, decay_all, d_all, cd_rows, _, _, _ = _ssd_group_terms(xs_v, dt_v, acs_v, d_v)
            S_all = state_v[...]
            st_v[...] = S_all
            yield
            yo = lax.dot_general(Cb, S_all.astype(BF16), NT, preferred_element_type=F32)
            new_state = lax.dot_general((X * decay_all).astype(BF16), Bb, TN, preferred_element_type=F32)
            yield
            y_v[...] = e_all * yo + d_all * xs
            state_v[...] = S_all * cd_rows + new_state
            for r in range(SSM_R):
                yield
                sl = slice(r * SSM_P, (r + 1) * SSM_P)
                M = Gm * _ssd_lmat(acs_v, acst_v, r, tril)
                yield
                y_v[:, sl] += jnp.dot(M.astype(BF16), X[:, sl].astype(BF16), preferred_element_type=F32)

        _interleave([group(gg) for gg in range(SSM_GPS)])

    return _call(body, name="ssd_fwd",
                 out_shape=(_sds((S, SSM_G * SSM_HP)), _sds((nc, SSM_G, SSM_HP, SSM_N))),
                 grid=(SSM_G // SSM_GPS, nc), in_specs=[xs_s, b_s, c_s, sc, sc, sct, gsp],
                 out_specs=(xs_s, pl.BlockSpec((None, SSM_GPS, SSM_HP, SSM_N), lambda g, c: (c, g, 0, 0))),
                 scratch=[pltpu.VMEM((SSM_GPS, SSM_HP, SSM_N), F32)])(xbc, xbc, xbc, dt_g, acs_g, acst_g, d_g)


def _ssd_bwd(xbc, dt_g, acs_g, acst_g, d_g, states, dy):
    S = xbc.shape[0]
    nc = S // SSM_L
    xs_s, b_s, c_s, sc, sct, gsp, st = _ssd_specs(True, nc)
    bc_out = pl.BlockSpec((SSM_L, SSM_GPS * SSM_N), lambda g, c: (nc - 1 - c, g))

    def body(xs_ref, b_ref, c_ref, dt_ref, acs_ref, acst_ref, d_ref, st_ref, dy_ref,
             dxs_ref, db_ref, dc_ref, ddt_ref, dacs_ref, dacst_ref, dd_ref, dstate):
        c = pl.program_id(1)

        @pl.when(c == 0)
        def _():
            dstate[...] = jnp.zeros_like(dstate)
            dd_ref[...] = jnp.zeros_like(dd_ref)

        tril = lax.broadcasted_iota(jnp.int32, (SSM_L, SSM_L), 0) >= lax.broadcasted_iota(jnp.int32, (SSM_L, SSM_L), 1)
        lane = lax.broadcasted_iota(jnp.int32, (1, LANES), 1)
        subl = lax.broadcasted_iota(jnp.int32, (LANES, 1), 0)
        last_row = lax.broadcasted_iota(jnp.int32, (SSM_L, 1), 0) == SSM_L - 1
        triu = lax.broadcasted_iota(jnp.int32, (SSM_L, SSM_L), 0) <= lax.broadcasted_iota(jnp.int32, (SSM_L, SSM_L), 1)
        def group(gg):
            xs_v, b_v, c_v, dt_v, acs_v, acst_v, d_v, st_v, ddt_v, dacs_v, dacst_v, dd_v, dstate_v = _ssd_group_views(
                gg, xs_ref, b_ref, c_ref, dt_ref, acs_ref, acst_ref, d_ref, st_ref, ddt_ref, dacs_ref, dacst_ref, dd_ref, dstate)
            dy_v, dxs_v = (r.at[:, gg * SSM_HP:(gg + 1) * SSM_HP] for r in (dy_ref, dxs_ref))
            db_v, dc_v = (r.at[:, gg * SSM_N:(gg + 1) * SSM_N] for r in (db_ref, dc_ref))
            Bb, Cb = b_v[...].astype(BF16), c_v[...].astype(BF16)
            Gm = lax.dot_general(Cb, Bb, NT, preferred_element_type=F32)
            GmT = lax.dot_general(Bb, Cb, NT, preferred_element_type=F32)
            yield
            xs, X, e_all, decay_all, d_all, cd_rows, dt_c, decay_c, cd = _ssd_group_terms(xs_v, dt_v, acs_v, d_v)
            S_all, dSn_all, dY = st_v[...], dstate_v[...], dy_v[...]
            Sb, dSnb = S_all.astype(BF16), dSn_all.astype(BF16)
            yield
            T = lax.dot_general(Cb, Sb, NT, preferred_element_type=F32)
            dT = (dY * e_all).astype(BF16)
            dC = jnp.dot(dT, Sb, preferred_element_type=F32)
            dS_prev = lax.dot_general(dT, Cb, TN, preferred_element_type=F32)
            yield
            yo_dy = dY * (e_all * T)
            W = lax.dot_general(Bb, dSnb, NT, preferred_element_type=F32)
            dB = jnp.dot((X * decay_all).astype(BF16), dSnb, preferred_element_type=F32)
            yield
            xw = X * W
            dcd_rows = jnp.sum(dSn_all * S_all, axis=1, keepdims=True)
            dstate_v[...] = dS_prev + dSn_all * cd_rows
            dX_state = W * decay_all
            yield
            acc = dict(dG=jnp.zeros((SSM_L, SSM_L), F32), dGT=jnp.zeros((SSM_L, SSM_L), F32),
                       ddt=jnp.zeros((SSM_L, LANES), F32), dacs=jnp.zeros((SSM_L, LANES), F32),
                       dacst=jnp.zeros((LANES, SSM_L), F32), dd=jnp.zeros((1, LANES), F32))

            def head(r):
                sl = slice(r * SSM_P, (r + 1) * SSM_P)
                Lm = _ssd_lmat(acs_v, acst_v, r, tril)
                LmT = jnp.exp(jnp.where(triu, acst_v[r:r + 1, :] - acs_v[:, r:r + 1], -jnp.inf))
                M = Gm * Lm
                yield
                dYh, xs_h = dY[:, sl], xs[:, sl]
                dYb, Xb = dYh.astype(BF16), X[:, sl].astype(BF16)
                dM = lax.dot_general(dYb, Xb, NT, preferred_element_type=F32)
                yield
                dX = jnp.dot((GmT * LmT).astype(BF16), dYb, preferred_element_type=F32) + dX_state[:, sl]
                acc["dG"] = acc["dG"] + dM * Lm
                acc["dGT"] = acc["dGT"] + lax.dot_general(Xb, dYb, NT, preferred_element_type=F32) * LmT
                yield
                dseg = dM * M
                dd = jnp.sum(xw[:, sl], axis=1, keepdims=True) * decay_c[r]
                dcd = jnp.sum(dcd_rows[sl])
                dacs_col = (jnp.sum(dseg, axis=1, keepdims=True) + jnp.sum(yo_dy[:, sl], axis=1, keepdims=True) - dd
                            + jnp.where(last_row, dcd * cd[r] + jnp.sum(dd), 0.0))
                dacs_row = -jnp.sum(dseg, axis=0, keepdims=True)
                yield
                dxs_v[:, sl] = dX * dt_c[r] + d_all[:, sl] * dYh
                acc["ddt"] = acc["ddt"] + jnp.where(lane == r, jnp.sum(dX * xs_h, axis=1, keepdims=True), 0.0)
                acc["dacs"] = acc["dacs"] + jnp.where(lane == r, dacs_col, 0.0)
                acc["dacst"] = acc["dacst"] + jnp.where(subl == r, dacs_row, 0.0)
                acc["dd"] = acc["dd"] + jnp.where(lane == r, jnp.sum(dYh * xs_h), 0.0)

            yield from _rounds([head(r) for r in range(SSM_R)])
            dc_v[...] = dC + jnp.dot(acc["dG"].astype(BF16), Bb, preferred_element_type=F32)
            db_v[...] = dB + jnp.dot(acc["dGT"].astype(BF16), Cb, preferred_element_type=F32)
            ddt_v[...] = acc["ddt"]
            dacs_v[...] = acc["dacs"]
            dacst_v[...] = acc["dacst"]
            dd_v[...] += acc["dd"]

        _interleave([group(gg) for gg in range(SSM_GPS)])

    big = _sds((SSM_G, S, LANES))
    return _call(body, name="ssd_bwd",
                 out_shape=(_sds((S, SSM_G * SSM_HP)), _sds((S, SSM_G * SSM_N)), _sds((S, SSM_G * SSM_N)),
                            big, big, _sds((SSM_G, LANES, S)), _sds((SSM_G, 1, LANES))),
                 grid=(SSM_G // SSM_GPS, nc), in_specs=[xs_s, b_s, c_s, sc, sc, sct, gsp, st, xs_s],
                 out_specs=(xs_s, bc_out, bc_out, sc, sc, sct, gsp),
                 scratch=[pltpu.VMEM((SSM_GPS, SSM_HP, SSM_N), F32)])(xbc, xbc, xbc, dt_g, acs_g, acst_g, d_g, states, dy)


def _gate_norm_fwd(y, proj, w):
    S, DI = y.shape
    tm = _tile(S, 256)

    def body(y_ref, z_ref, w_ref, o_ref):
        z = z_ref[...]
        gn = y_ref[...] * (z * _sigmoid(z))
        r = lax.rsqrt(jnp.mean(gn * gn, axis=-1, keepdims=True) + SSM_NORM_EPS)
        o_ref[...] = (gn * r * w_ref[...]).astype(BF16)

    row = pl.BlockSpec((tm, DI), lambda i: (i, 0))
    return _call(body, name="gate_norm_fwd", out_shape=_sds((S, DI), BF16), grid=(S // tm,),
                 in_specs=[row, row, pl.BlockSpec((1, DI), lambda i: (0, 0))], out_specs=row)(y, proj, w)


def _gate_norm_bwd(y, proj, w, dout):
    S, DI = y.shape
    tm = _tile(S, 256)

    def body(y_ref, z_ref, w_ref, d_ref, dy_ref, dz_ref, dw_ref):
        z, yv = z_ref[...], y_ref[...]
        sz = z * _sigmoid(z)
        dgn, dw = _norm_bwd_math(yv * sz, w_ref[...], d_ref[...].astype(F32), SSM_NORM_EPS)
        dy_ref[...] = dgn * sz
        dz_ref[...] = (dgn * yv * _silu_grad(z)).astype(BF16)

        @pl.when(pl.program_id(0) == 0)
        def _():
            dw_ref[...] = jnp.zeros_like(dw_ref)

        dw_ref[...] += dw

    row = pl.BlockSpec((tm, DI), lambda i: (i, 0))
    vec = pl.BlockSpec((1, DI), lambda i: (0, 0))
    return _call(body, name="gate_norm_bwd", out_shape=(_sds((S, DI)), _sds((S, DI), BF16), _sds((1, DI))), grid=(S // tm,),
                 in_specs=[row, row, vec, row], out_specs=(row, row, vec))(y, proj, w, dout)


def _group_major(v):
    return jnp.pad(v.reshape(SSM_G, 1, SSM_R), ((0, 0), (0, 0), (0, LANES - SSM_R)))


def _ungroup(t):
    return t[:, :SSM_R].reshape(1, SSM_G * SSM_R)


def _ffn_fwd(x, P, l, need):
    need(f"ffn{l}_up", x)
    h = _rmsnorm(x, P["norm_ffn"][l:l + 1], name=f"ffn{l}_norm")
    a = _mm(h, P[f"ffn_w_up{l}"], name=f"ffn{l}_up")
    need(f"ffn{l}_down", a)
    act = _ffn_mid_fwd(a, P["ffn_conv_w"], P["ffn_conv_b"], l)
    out = _mm(act, P[f"ffn_w_down{l}"], res=x, name=f"ffn{l}_down")
    return out, (x, h, a, act)


def _ffn_bwd(saved, P, l, dx, emit):
    x, h, a, act = saved
    dact = _mm(dx, P[f"ffn_w_down{l}"], tb=True, out_dtype=BF16, name=f"ffn{l}_down_dx")
    dw_down = _mm(act, dx, ta=True, out_dtype=PAYLOAD, name=f"ffn{l}_down_dw")
    da, dcw, dcb = _ffn_mid_bwd(a, P["ffn_conv_w"], P["ffn_conv_b"], l, dact)
    dw_up = _mm(h, da, ta=True, out_dtype=PAYLOAD, name=f"ffn{l}_up_dw")
    tie = emit(f"ffn{l}", {"ffn_w_up": dw_up, "ffn_w_down": dw_down})
    dh = _mm(da, P[f"ffn_w_up{l}"], tb=True, name=f"ffn{l}_up_dx")
    dx_in, dnw = _rmsnorm_bwd(x, P["norm_ffn"][l:l + 1], dh, dx, name=f"ffn{l}_norm_bwd", after=tie)
    return dx_in, dnw, dcw, dcb


def _local_step(x, positions, target, P, need, emit, after=None):
    S, D = x.shape
    inv_freq = ROPE_THETA ** (-jnp.arange(0, HEAD_DIM, 2, dtype=F32) / HEAD_DIM)
    inv_freq = jnp.tile(inv_freq, LANES // (HEAD_DIM // 2)).reshape(1, LANES)
    cos, sin = _rope_tables(positions, inv_freq)

    nm0 = P["norm_mix"][0:1]
    h0 = _rmsnorm(x, nm0, name="mix_norm", after=after)
    proj0 = _mm(h0, P["mix_w_in"], name="mix_in")
    cat0 = _attn_fwd(proj0, cos, sin, P["attn_sinks"], _pool_fwd(proj0, P["pool_w"][0], P["pool_scale"]))
    need("mix_out", cat0)
    x1 = _mm(cat0, P["mix_w_out"], res=x, name="mix_out")
    x2, ffn0 = _ffn_fwd(x1, P, 0, need)

    nm1 = P["norm_mix"][1:2]
    need("ssm", x2)
    h1 = _rmsnorm(x2, nm1, name="ssm_norm_in")
    z = _mm(h1, P["ssm_wz"], name="ssm_in_z")
    xbcp = _mm(h1, P["ssm_wxbc"], name="ssm_in_xbc")
    dtraw = _mm(h1, P["ssm_wdt"], name="ssm_in_dt")
    xbc = _conv_silu_fwd(xbcp, P["ssm_conv_w"], P["ssm_conv_b"])
    bias_g, alog_g, d_g = _group_major(P["ssm_dt_bias"]), _group_major(P["ssm_A_log"]), _group_major(P["ssm_D"])
    pre_g, dt_g, acs_g, acst_g = _ssd_prep_fwd(dtraw, 0, bias_g, alog_g)
    y, states = _ssd_fwd(xbc, dt_g, acs_g, acst_g, d_g)
    yn = _gate_norm_fwd(y, z, P["ssm_norm"])
    x3 = _mm(yn, P["ssm_w_out"], res=x2, name="ssm_out")
    x4, ffn1 = _ffn_fwd(x3, P, 1, need)

    loss, dx, d_norm_final = _final_loss(x4, P["norm_final"].reshape(1, D), target, name="final_loss")
    dx, dnf1, dcw1, dcb1 = _ffn_bwd(ffn1, P, 1, dx, emit)
    dyn = _mm(dx, P["ssm_w_out"], tb=True, out_dtype=BF16, name="ssm_out_dx")
    d_w_out1 = _mm(yn, dx, ta=True, out_dtype=PAYLOAD, name="ssm_out_dw")
    dy, dz, d_ssm_norm = _gate_norm_bwd(y, z, P["ssm_norm"], dyn)
    dxs, dB, dC, ddt_g, dacs_g, dacst_g, dd_g = _ssd_bwd(xbc, dt_g, acs_g, acst_g, d_g, states, dy)
    draw, dbias_g, dalog_g = _ssd_prep_bwd(pre_g, dt_g, alog_g, ddt_g, dacs_g, dacst_g)
    dxbc, d_conv_w1, d_conv_b1 = _conv_silu_bwd(xbcp, P["ssm_conv_w"], P["ssm_conv_b"], [dxs, dB, dC])
    d_wz = _mm(h1, dz, ta=True, out_dtype=PAYLOAD, name="ssm_in_z_dw")
    d_wxbc = _mm(h1, dxbc, ta=True, out_dtype=PAYLOAD, name="ssm_in_xbc_dw")
    d_wdt = _mm(h1, draw, ta=True, out_dtype=PAYLOAD, name="ssm_in_dt_dw")
    tie = emit("ssm", {"ssm_wz": d_wz, "ssm_wxbc": d_wxbc, "ssm_wdt": d_wdt, "ssm_w_out": d_w_out1,
                       "ssm_conv_w": d_conv_w1, "ssm_conv_b": d_conv_b1, "ssm_norm": d_ssm_norm})
    dh1 = _mm(dz, P["ssm_wz"], tb=True, name="ssm_in_z_dx")
    dh1 = _mm(dxbc, P["ssm_wxbc"], tb=True, res=dh1, name="ssm_in_xbc_dx")
    dh1 = _mm(draw, P["ssm_wdt"], tb=True, res=dh1, name="ssm_in_dt_dx")
    dx, dnm1 = _rmsnorm_bwd(x2, nm1, dh1, dx, name="ssm_norm_in_bwd", after=tie)
    dx, dnf0, dcw0, dcb0 = _ffn_bwd(ffn0, P, 0, dx, emit)
    dcat = _mm(dx, P["mix_w_out"], tb=True, name="mix_out_dx")
    d_w_out0 = _mm(cat0, dx, ta=True, out_dtype=PAYLOAD, name="mix_out_dw")
    dproj0, dsk = _attn_bwd(proj0, cos, sin, P["attn_sinks"], dcat)
    dproj0, d_pool_w, d_pool_scale = _pool_bwd(proj0, P["pool_w"][0], P["pool_scale"], dcat, dproj0)
    d_w_in0 = _mm(h0, dproj0, ta=True, out_dtype=PAYLOAD, name="mix_in_dw")
    tie = emit("mix", {"mix_w_in": d_w_in0, "mix_w_out": d_w_out0, "ffn_conv_w": jnp.stack([dcw0, dcw1])})
    dh0 = _mm(dproj0, P["mix_w_in"], tb=True, name="mix_in_dx")
    grad_x, dnm0 = _rmsnorm_bwd(x, nm0, dh0, dx, name="mix_norm_bwd", after=tie)

    small = {
        "norm_mix": jnp.concatenate([dnm0, dnm1], axis=0),
        "norm_ffn": jnp.concatenate([dnf0, dnf1], axis=0),
        "norm_final": d_norm_final,
        "pool_w": d_pool_w,
        "pool_scale": d_pool_scale,
        "attn_sinks_rows": dsk,
        "ssm_dt_bias_g": dbias_g, "ssm_A_log_g": dalog_g, "ssm_D_g": dd_g,
        "ffn_conv_b": jnp.concatenate([dcb0, dcb1], axis=0),
    }
    return loss, grad_x, small


def _peer(k):
    x, y, c = lax.axis_index("x"), lax.axis_index("y"), lax.axis_index("c")
    px = 1 - x if k & 4 else x
    py = 1 - y if k & 2 else y
    pc = 1 - c if k & 1 else c
    return (px, py, pc), 4 * px + 2 * py + pc


def _my_index():
    return 4 * lax.axis_index("x") + 2 * lax.axis_index("y") + lax.axis_index("c")


def _land_sds(a, mode, gather):
    if mode == "slab":
        return _sds(((N_DEV,) + a.shape) if gather else a.shape, a.dtype)
    assert mode == "rows", mode
    return _sds((N_DEV * a.shape[0],) + a.shape[1:] if gather else (N_DEV, a.shape[0] // N_DEV) + a.shape[1:], a.dtype)


def _part(ref, mode, shape, idx):
    if mode == "slab":
        return ref.at[idx]
    r = shape[0] // N_DEV
    return ref.at[pl.ds(idx * r, r)]


def _own_copies(ops, gather, srcs, lands, sems):
    me = _my_index()
    out = []
    for i, (a, mode) in enumerate(ops):
        s = srcs[i] if gather else _part(srcs[i], mode, a.shape, me)
        d = _part(lands[i], mode, _land_sds(a, mode, gather).shape, me) if gather else lands[i].at[me]
        out.append(pltpu.make_async_copy(s, d, sems.at[i]))
    return out


def _remote_copies(ops, gather, srcs, lands, send_sems, recv_sems):
    me = _my_index()
    n = len(ops)
    out = []
    for k in range(1, N_DEV):
        dev, idx = _peer(k)
        for i, (a, mode) in enumerate(ops):
            s = srcs[i] if gather else _part(srcs[i], mode, a.shape, idx)
            d = _part(lands[i], mode, _land_sds(a, mode, gather).shape, me) if gather else lands[i].at[me]
            out.append(pltpu.make_async_remote_copy(src_ref=s, dst_ref=d, send_sem=send_sems.at[(k - 1) * n + i],
                                                    recv_sem=recv_sems.at[(k - 1) * n + i], device_id=dev,
                                                    device_id_type=pl.DeviceIdType.MESH))
    return out


def _exchange(ops, *, gather, name):
    n = len(ops)

    def body(*refs):
        ins, outs = refs[:n], refs[n:2 * n]
        send_sems, recv_sems, local_sems = refs[2 * n:]
        copies = _own_copies(ops, gather, ins, outs, local_sems) + _remote_copies(ops, gather, ins, outs, send_sems, recv_sems)
        for cp in copies:
            cp.start()
        for cp in copies:
            cp.wait()

    return pl.pallas_call(
        body, name=name, out_shape=[_land_sds(a, m, gather) for a, m in ops], in_specs=[ANY] * n, out_specs=[ANY] * n,
        scratch_shapes=[pltpu.SemaphoreType.DMA((n * (N_DEV - 1),)), pltpu.SemaphoreType.DMA((n * (N_DEV - 1),)),
                        pltpu.SemaphoreType.DMA((n,))],
    )(*[a for a, _ in ops])


HBM = pl.BlockSpec(memory_space=pltpu.HBM)
SEM = pl.BlockSpec(memory_space=pltpu.SEMAPHORE)
SIDE_EFFECT = pltpu.SideEffectType.DATAFLOW_SIDE_EFFECTING


def _in_hbm(a):
    return pltpu.with_memory_space_constraint(a, pltpu.HBM)


def _place_own(ops, *, gather, name):
    n = len(ops)

    def zeros(k):
        return (0,) * k

    in_specs, out_specs = [], []
    for a, mode in ops:
        nd = a.ndim
        if gather and mode == "slab":
            in_specs.append(pl.BlockSpec(a.shape, lambda i, nd=nd: zeros(nd)))
            out_specs.append(pl.BlockSpec((1,) + a.shape, lambda i, nd=nd: (_my_index(),) + zeros(nd)))
        elif gather:
            in_specs.append(pl.BlockSpec(a.shape, lambda i, nd=nd: zeros(nd)))
            out_specs.append(pl.BlockSpec(a.shape, lambda i, nd=nd: (_my_index(),) + zeros(nd - 1)))
        elif mode == "slab":
            in_specs.append(pl.BlockSpec((1,) + a.shape[1:], lambda i, nd=nd: (_my_index(),) + zeros(nd - 1)))
            out_specs.append(pl.BlockSpec((1,) + a.shape[1:], lambda i, nd=nd: (_my_index(),) + zeros(nd - 1)))
        else:
            r = a.shape[0] // N_DEV
            in_specs.append(pl.BlockSpec((r,) + a.shape[1:], lambda i, nd=nd: (_my_index(),) + zeros(nd - 1)))
            out_specs.append(pl.BlockSpec((1, r) + a.shape[1:], lambda i, nd=nd: (_my_index(),) + zeros(nd)))

    def body(*refs):
        for i_ref, o_ref in zip(refs[:n], refs[n:]):
            if o_ref.shape == i_ref.shape:
                o_ref[...] = i_ref[...]
            else:
                o_ref[0] = i_ref[...]

    return _call(body, name=name, out_shape=[_land_sds(a, m, gather) for a, m in ops], grid=(1,),
                 in_specs=in_specs, out_specs=out_specs)(*[a for a, _ in ops])


def _exchange_start(groups, *, gather, name):
    sizes = [len(ops) for ops, _ in groups]
    n = sum(sizes)
    G = len(groups)

    def body(*refs):
        srcs, lands = refs[:n], refs[n:2 * n]
        sems = refs[2 * n:2 * n + 2 * G]
        token = refs[-1]
        off = 0
        for g, (ops, _) in enumerate(groups):
            for cp in _remote_copies(ops, gather, srcs[off:off + sizes[g]], lands[off:off + sizes[g]], sems[2 * g], sems[2 * g + 1]):
                cp.start()
            off += sizes[g]
        token[...] = jnp.zeros_like(token)

    srcs = [a for ops, _ in groups for a, _ in ops]
    lands = [l for _, ls in groups for l in ls]
    sem_shapes = [pltpu.SemaphoreType.DMA((s * (N_DEV - 1),)) for s in sizes for _ in range(2)]
    outs = pl.pallas_call(
        body, name=name,
        out_shape=sem_shapes + [pltpu.HBM(a.shape, a.dtype) for a in srcs + lands] + [_sds((8, LANES))],
        in_specs=[HBM] * (2 * n), out_specs=[SEM] * (2 * G) + [HBM] * (2 * n) + [pl.BlockSpec(memory_space=pltpu.VMEM)],
        input_output_aliases={i: 2 * G + i for i in range(2 * n)},
        compiler_params=pltpu.CompilerParams(has_side_effects=SIDE_EFFECT))(*[_in_hbm(a) for a in srcs + lands])
    sems, thru, token = outs[:2 * G], outs[2 * G:2 * G + 2 * n], outs[-1]
    states, off = [], 0
    for g, s in enumerate(sizes):
        states.append((sems[2 * g], sems[2 * g + 1], thru[off:off + s], thru[n + off:n + off + s]))
        off += s
    return states, token


def _exchange_wait(ops, state, after, *, gather, name):
    send_sems, recv_sems, srcs, lands = state
    n = len(ops)

    def body(*refs):
        for cp in _remote_copies(ops, gather, refs[:n], refs[n:2 * n], refs[2 * n], refs[2 * n + 1]):
            cp.wait_send()
            cp.wait_recv()

    outs = pl.pallas_call(
        body, name=name, out_shape=[pltpu.HBM(a.shape, a.dtype) for a in list(srcs) + list(lands)],
        in_specs=[HBM] * (2 * n) + [SEM, SEM, ANY], out_specs=[HBM] * (2 * n),
        input_output_aliases={i: i for i in range(2 * n)},
        compiler_params=pltpu.CompilerParams(has_side_effects=SIDE_EFFECT))(*srcs, *lands, send_sems, recv_sems, after)
    return outs[n:]


ADAM_ROWS = 256


def _row_tile(R, cap=ADAM_ROWS):
    best = R
    if R > cap:
        for d in range(16, cap + 1, 16):
            if R % d == 0:
                best = d
    return best


def _adamw(g_layers, w, m, v, *, name):
    L = len(g_layers)
    J, R, Wd = g_layers[0].shape
    assert w.shape == (L, R, Wd), (g_layers[0].shape, w.shape)
    tr = _row_tile(R)
    nrt = R // tr
    c1 = 1.0 / (1.0 - ADAM_B1 ** ADAM_STEP)
    c2 = 1.0 / (1.0 - ADAM_B2 ** ADAM_STEP)

    def body(*refs):
        g_refs = refs[:L]
        w_ref, m_ref, v_ref, go_ref, d_ref, mo_ref, vo_ref = refs[L:]
        layer = pl.program_id(0)
        g = None
        for l, g_ref in enumerate(g_refs):
            gl = g_ref[0].astype(F32)
            for j in range(1, J):
                gl = gl + g_ref[j].astype(F32)
            g = gl if g is None else jnp.where(layer == l, gl, g)
        mn = ADAM_B1 * m_ref[...] + (1.0 - ADAM_B1) * g
        vn = ADAM_B2 * v_ref[...] + (1.0 - ADAM_B2) * (g * g)
        go_ref[...] = g
        mo_ref[...] = mn
        vo_ref[...] = vn
        d_ref[...] = -ADAM_LR * ((mn * c1) / (jnp.sqrt(vn * c2) + ADAM_EPS) + ADAM_WD * w_ref[...])

    def g_spec(l):
        return pl.BlockSpec((J, tr, Wd), lambda ll, i: (0, jnp.where(ll == l, i, jnp.where(ll < l, 0, nrt - 1)), 0))

    row = pl.BlockSpec((None, tr, Wd), lambda ll, i: (ll, i, 0))
    out = _sds((L, R, Wd))
    return _call(body, name=name, out_shape=(out, out, out, out), grid=(L, nrt),
                 in_specs=[g_spec(l) for l in range(L)] + [row, row, row], out_specs=(row, row, row, row))(*g_layers, w, m, v)


def _sum_slabs(slabs, *, name):
    n = len(slabs)

    def body(*refs):
        for g_ref, o_ref in zip(refs[:n], refs[n:]):
            g = g_ref[0]
            for j in range(1, g_ref.shape[0]):
                g = g + g_ref[j]
            o_ref[...] = g

    return _call(body, name=name, out_shape=[_sds(s.shape[1:]) for s in slabs])(*slabs)


RELAYOUT_ROWS = 256


def _col_plan(n, segments):
    plan = []
    for j in range(N_DEV):
        lo, hi = j * n, (j + 1) * n
        for t0, t1, oi, o0 in segments:
            a, b = max(lo, t0), min(hi, t1)
            if a < b:
                plan.append((j, a - lo, oi, o0 + a - t0, b - a))
    return plan


def _interleave_segments(F):
    seg = []
    for b in range(F // LANES):
        seg.append((b * LANES, (b + 1) * LANES, 0, 2 * b * LANES))
        seg.append((F + b * LANES, F + (b + 1) * LANES, 0, (2 * b + 1) * LANES))
    return seg


def _interleave_perm(F):
    perm = []
    for b in range(F // LANES):
        perm += list(range(b * LANES, (b + 1) * LANES)) + list(range(F + b * LANES, F + (b + 1) * LANES))
    inv = [0] * (2 * F)
    for d, s in enumerate(perm):
        inv[s] = d
    return jnp.asarray(perm, jnp.int32), jnp.asarray(inv, jnp.int32)


def _cols_from_slabs(slabs, plan, widths, *, name):
    _, R, n = slabs.shape
    tr = _row_tile(R, RELAYOUT_ROWS)
    covered = [sum(e[4] for e in plan if e[2] == i) for i in range(len(widths))]

    def body(s_ref, *o_refs):
        for i, o_ref in enumerate(o_refs):
            if covered[i] < widths[i]:
                o_ref[...] = jnp.zeros_like(o_ref)
        for j, sc, oi, oc, w in plan:
            o_refs[oi][:, oc:oc + w] = s_ref[j, :, sc:sc + w]

    return _call(body, name=name, out_shape=[_sds((R, w), slabs.dtype) for w in widths], grid=(R // tr,),
                 in_specs=[pl.BlockSpec((N_DEV, tr, n), lambda i: (0, i, 0))],
                 out_specs=[pl.BlockSpec((tr, w), lambda i: (i, 0)) for w in widths])(slabs)


def _slabs_from_cols(mats, plan, n, *, name):
    R = mats[0].shape[0]
    tr = _row_tile(R, RELAYOUT_ROWS)

    def body(*refs):
        m_refs, o_ref = refs[:-1], refs[-1]
        for j, sc, oi, oc, w in plan:
            o_ref[j, :, sc:sc + w] = m_refs[oi][:, oc:oc + w]

    return _call(body, name=name, out_shape=_sds((N_DEV, R, n), mats[0].dtype), grid=(R // tr,),
                 in_specs=[pl.BlockSpec((tr, m.shape[1]), lambda i: (i, 0)) for m in mats],
                 out_specs=pl.BlockSpec((N_DEV, tr, n), lambda i: (0, i, 0)))(*mats)


def kernel(x, positions, norm_mix, norm_ffn, norm_final, mix_w_in, pool_w, pool_scale, attn_sinks, mix_w_out, ssm_w_in, ssm_conv_w, ssm_conv_b, ssm_dt_bias, ssm_A_log, ssm_D, ssm_norm, ssm_w_out, ffn_w_up, ffn_conv_w, ffn_conv_b, ffn_w_down, loss_target, m_norm_mix, m_norm_ffn, m_norm_final, m_mix_w_in, m_pool_w, m_pool_scale, m_attn_sinks, m_mix_w_out, m_ssm_w_in, m_ssm_conv_w, m_ssm_conv_b, m_ssm_dt_bias, m_ssm_A_log, m_ssm_D, m_ssm_norm, m_ssm_w_out, m_ffn_w_up, m_ffn_conv_w, m_ffn_conv_b, m_ffn_w_down, v_norm_mix, v_norm_ffn, v_norm_final, v_mix_w_in, v_pool_w, v_pool_scale, v_attn_sinks, v_mix_w_out, v_ssm_w_in, v_ssm_conv_w, v_ssm_conv_b, v_ssm_dt_bias, v_ssm_A_log, v_ssm_D, v_ssm_norm, v_ssm_w_out, v_ffn_w_up, v_ffn_conv_w, v_ffn_conv_b, v_ffn_w_down):
    args = dict(locals())
    wl = {n: args[n] for n in WEIGHTS}
    ml = {n: args["m_" + n] for n in WEIGHTS}
    vl = {n: args["v_" + n] for n in WEIGHTS}
    D = x.shape[2]
    F = ffn_w_down.shape[1] * N_DEV
    DI, CD, NH = ssm_norm.shape[1] * N_DEV, ssm_conv_b.shape[1] * N_DEV, ssm_dt_bias.shape[1]
    Kc, Kf = ssm_conv_w.shape[1], ffn_conv_w.shape[1]
    n_mix, n_ssm, n_up = mix_w_in.shape[2], ssm_w_in.shape[2], ffn_w_up.shape[2]
    plan_mix = _col_plan(n_mix, [(0, N_DEV * n_mix, 0, 0)])
    plan_ssm = _col_plan(n_ssm, [(0, DI, 0, 0), (DI, DI + CD, 1, 0), (DI + CD, DI + CD + NH, 2, 0)])
    plan_up = _col_plan(n_up, _interleave_segments(F))
    perm, inv = _interleave_perm(F)

    def two(a):
        return a.reshape(-1, a.shape[-1])

    def pay(a):
        return a.astype(PAYLOAD)

    order = ("mix_in", "mix_out", "ffn0_up", "ffn0_down", "ssm", "ffn1_up", "ffn1_down")
    gops = {
        "mix_in": [(pay(two(mix_w_in)), "slab")],
        "mix_out": [(pay(mix_w_out[0]), "rows"), (two(ssm_conv_w), "slab"), (ssm_conv_b, "slab"), (ssm_norm, "slab"),
                    (two(ffn_conv_w), "slab")],
        "ffn0_up": [(pay(ffn_w_up[0]), "slab")], "ffn0_down": [(pay(ffn_w_down[0]), "rows")],
        "ssm": [(pay(two(ssm_w_in)), "slab"), (pay(ssm_w_out[0]), "rows")],
        "ffn1_up": [(pay(ffn_w_up[1]), "slab")], "ffn1_down": [(pay(ffn_w_down[1]), "rows")],
    }
    lands = _place_own([op for g in order for op in gops[g]], gather=True, name="gather_own")
    groups, off = [], 0
    for g in order:
        groups.append((gops[g], lands[off:off + len(gops[g])]))
        off += len(gops[g])
    gstates, token = _exchange_start(groups, gather=True, name="gather_start")
    gstate = dict(zip(order, gstates))
    P = {n: wl[n] for n in REPLICATED}
    P["ffn_conv_b"] = jnp.take(ffn_conv_b, perm, axis=1)

    def need(g, after):
        got = _exchange_wait(gops[g], gstate[g], after, gather=True, name="gather_wait_" + g)
        if g == "mix_in":
            (P["mix_w_in"],) = _cols_from_slabs(got[0], plan_mix, (N_DEV * n_mix,), name="unpack_mix_w_in")
        elif g == "mix_out":
            P.update(mix_w_out=got[0], ssm_conv_w=got[1].transpose(1, 0, 2).reshape(Kc, CD), ssm_conv_b=got[2].reshape(1, CD),
                     ssm_norm=got[3].reshape(1, DI),
                     ffn_conv_w=jnp.take(got[4].transpose(1, 0, 2).reshape(2 * Kf, 2 * F), perm, axis=1).reshape(2, Kf, 2 * F))
        elif g == "ssm":
            P["ssm_wz"], P["ssm_wxbc"], P["ssm_wdt"] = _cols_from_slabs(got[0], plan_ssm, (DI, CD, LANES), name="unpack_ssm_w_in")
            P["ssm_w_out"] = got[1]
        elif g.endswith("_up"):
            (P["ffn_w_up" + g[3]],) = _cols_from_slabs(got[0], plan_up, (2 * F,), name="unpack_ffn_w_up" + g[3])
        else:
            P["ffn_w_down" + g[3]] = got[0]

    sent = {}

    def emit(g, d):
        if g == "mix":
            ops = [(_slabs_from_cols([d["mix_w_in"]], plan_mix, n_mix, name="pack_mix_w_in"), "slab"), (d["mix_w_out"], "rows"),
                   (jnp.take(d["ffn_conv_w"].reshape(2 * Kf, 2 * F), inv, axis=1).reshape(2 * Kf, N_DEV, n_up).transpose(1, 0, 2), "slab")]
        elif g == "ssm":
            ops = [(_slabs_from_cols([d["ssm_wz"], d["ssm_wxbc"], d["ssm_wdt"]], plan_ssm, n_ssm, name="pack_ssm_w_in"), "slab"),
                   (d["ssm_w_out"], "rows"), (d["ssm_conv_w"].reshape(Kc, N_DEV, -1).transpose(1, 0, 2), "slab"),
                   (d["ssm_conv_b"].reshape(N_DEV, 1, -1), "slab"), (d["ssm_norm"].reshape(N_DEV, 1, -1), "slab")]
        else:
            ops = [(_slabs_from_cols([d["ffn_w_up"]], plan_up, n_up, name="pack_ffn_w_up" + g[3]), "slab"), (d["ffn_w_down"], "rows")]
        own = _place_own(ops, gather=False, name="scatter_own_" + g)
        (state,), tok = _exchange_start([(ops, own)], gather=False, name="scatter_start_" + g)
        sent[g] = (ops, state)
        return tok

    need("mix_in", token)
    loss_lanes, grad_x, G = _local_step(x[0], positions.reshape(-1, 1), loss_target[0], P, need, emit, after=token)
    loss = lax.psum(loss_lanes[0, 0], ("x", "y", "c"))

    res = {}

    def update(n, g_layers):
        L = len(g_layers)
        g_layers = [g.reshape(g.shape[0], -1, g.shape[-1]) for g in g_layers]
        shape = (L,) + g_layers[0].shape[1:]
        outs = _adamw(g_layers, wl[n].reshape(shape), ml[n].reshape(shape), vl[n].reshape(shape), name="adamw_" + n)
        for kind, a in zip(("grad", "delta", "new_m", "new_v"), outs):
            res[kind, n] = a.reshape(wl[n].shape)

    recv = {g: _exchange_wait(sent[g][0], sent[g][1], grad_x, gather=False, name="scatter_wait_" + g)
            for g in ("ffn1", "ssm", "ffn0", "mix")}
    update("mix_w_in", [recv["mix"][0]])
    update("mix_w_out", [recv["mix"][1]])
    update("ffn_conv_w", [recv["mix"][2]])
    update("ssm_w_in", [recv["ssm"][0]])
    update("ssm_w_out", [recv["ssm"][1]])
    update("ssm_conv_w", [recv["ssm"][2]])
    update("ssm_conv_b", [recv["ssm"][3]])
    update("ssm_norm", [recv["ssm"][4]])
    update("ffn_w_up", [recv["ffn0"][0], recv["ffn1"][0]])
    update("ffn_w_down", [recv["ffn0"][1], recv["ffn1"][1]])

    rep = _exchange(
        [(a, "slab") for a in (G["norm_mix"], G["norm_ffn"], G["norm_final"], G["pool_w"].reshape(-1, LANES), G["pool_scale"],
                               jnp.take(G["ffn_conv_b"], inv, axis=1), G["attn_sinks_rows"],
                               G["ssm_dt_bias_g"].reshape(SSM_G, LANES), G["ssm_A_log_g"].reshape(SSM_G, LANES),
                               G["ssm_D_g"].reshape(SSM_G, LANES))],
        gather=True, name="gather_small_grads")
    for n, r in zip(("norm_mix", "norm_ffn", "norm_final", "pool_w", "pool_scale", "ffn_conv_b"), rep):
        update(n, [r])
    sinks_rows, bias_g, alog_g, d_g = _sum_slabs(rep[6:], name="sum_head_grads")
    update("attn_sinks", [sinks_rows[:, 0].reshape(1, 1, N_HEADS)])
    update("ssm_dt_bias", [_ungroup(bias_g)[None]])
    update("ssm_A_log", [_ungroup(alog_g)[None]])
    update("ssm_D", [_ungroup(d_g)[None]])

    return (loss, grad_x[None], *[res[k, n] for k in ("grad", "delta", "new_m", "new_v") for n in WEIGHTS])
```

```python
import functools
import math

import jax
import jax.numpy as jnp
from jax import lax
from jax.experimental import pallas as pl
from jax.experimental.pallas import tpu as pltpu

F32 = jnp.float32
BF16 = jnp.bfloat16

N_DEV = 8
LANES = 128
HEAD_DIM = 64
N_KV_HEADS = 2
GQ = 4
N_HEADS = N_KV_HEADS * GQ
BLOCK = 128
POOL_GROUPS = 4
ROPE_THETA = 10000.0
SSM_P = 64
SSM_G = 8
SSM_R = 4
SSM_N = 128
SSM_L = 128
NORM_EPS = 1e-6
SSM_NORM_EPS = 1e-5
ADAM_LR, ADAM_B1, ADAM_B2, ADAM_EPS, ADAM_WD, ADAM_STEP = 0.001, 0.9, 0.999, 1e-08, 0.01, 10
VMEM_LIMIT = 56 * 2 ** 20
PAYLOAD = jnp.bfloat16

REPLICATED = ("norm_mix", "norm_ffn", "norm_final", "pool_w", "pool_scale", "attn_sinks",
              "ssm_dt_bias", "ssm_A_log", "ssm_D", "ffn_conv_b")
WEIGHTS = ("norm_mix", "norm_ffn", "norm_final", "mix_w_in", "pool_w", "pool_scale", "attn_sinks", "mix_w_out",
           "ssm_w_in", "ssm_conv_w", "ssm_conv_b", "ssm_dt_bias", "ssm_A_log", "ssm_D", "ssm_norm", "ssm_w_out",
           "ffn_w_up", "ffn_conv_w", "ffn_conv_b", "ffn_w_down")


def _tile(n, cap):
    if n <= cap:
        return n
    best = None
    for d in range(LANES, cap + 1, LANES):
        if n % d == 0:
            best = d
    assert best is not None, (n, cap)
    return best


def _call(body, *, name, out_shape, grid=None, in_specs=None, out_specs=None, scratch=(), aliases=None):
    kw = {}
    if grid is not None:
        kw = dict(grid=grid, in_specs=in_specs, out_specs=out_specs)
    if aliases:
        kw["input_output_aliases"] = aliases
    return pl.pallas_call(
        body, name=name, out_shape=out_shape, scratch_shapes=list(scratch),
        compiler_params=pltpu.CompilerParams(vmem_limit_bytes=VMEM_LIMIT), **kw)


ANY = pl.BlockSpec(memory_space=pl.ANY)


def _sds(shape, dtype=F32):
    return jax.ShapeDtypeStruct(tuple(shape), dtype)


def _sigmoid(x):
    return 1.0 / (1.0 + jnp.exp(-x))


def _shift_dn(x, d, t):
    if d == 0:
        return x
    return jnp.where(t >= d, pltpu.roll(x, d, axis=0), 0.0)


def _shift_up(x, d, t):
    if d == 0:
        return x
    n = x.shape[0]
    return jnp.where(t < n - d, pltpu.roll(x, n - d, axis=0), 0.0)


def _mm(a, b, *, name, ta=False, tb=False, res=None, out_dtype=F32):
    M, K = (a.shape[1], a.shape[0]) if ta else a.shape
    N = b.shape[0] if tb else b.shape[1]
    assert (b.shape[1] if tb else b.shape[0]) == K, (a.shape, b.shape, ta, tb)
    tm, tn, tk = _tile(M, 1408), _tile(N, 1408), _tile(K, 1408)
    nk = K // tk
    dims = (((0 if ta else 1,), (1 if tb else 0,)), ((), ()))

    def body(*refs):
        a_ref, b_ref = refs[:2]
        r_ref = refs[2] if res is not None else None
        o_ref, acc = refs[-2:]
        k = pl.program_id(2)

        @pl.when(k == 0)
        def _():
            acc[...] = jnp.zeros_like(acc)

        acc[...] += lax.dot_general(a_ref[...].astype(BF16), b_ref[...].astype(BF16), dims,
                                    preferred_element_type=F32)

        @pl.when(k == nk - 1)
        def _():
            out = acc[...]
            if res is not None:
                out = out + r_ref[...]
            o_ref[...] = out.astype(out_dtype)

    a_spec = pl.BlockSpec((tk, tm), lambda i, j, k: (k, i)) if ta else pl.BlockSpec((tm, tk), lambda i, j, k: (i, k))
    b_spec = pl.BlockSpec((tn, tk), lambda i, j, k: (j, k)) if tb else pl.BlockSpec((tk, tn), lambda i, j, k: (k, j))
    o_spec = pl.BlockSpec((tm, tn), lambda i, j, k: (i, j))
    ins, specs = [a, b], [a_spec, b_spec]
    if res is not None:
        ins.append(res)
        specs.append(o_spec)
    return _call(body, name=name, out_shape=_sds((M, N), out_dtype), grid=(M // tm, N // tn, nk), in_specs=specs,
                 out_specs=o_spec, scratch=[pltpu.VMEM((tm, tn), F32)])(*ins)


def _rmsnorm(x, w, *, name, eps=NORM_EPS, after=None):
    S, D = x.shape
    tm = _tile(S, 512)
    tie = [] if after is None else [after]

    def body(x_ref, w_ref, *rest):
        o_ref = rest[-1]
        xf = x_ref[...]
        r = lax.rsqrt(jnp.mean(xf * xf, axis=-1, keepdims=True) + eps)
        o_ref[...] = (xf * r * w_ref[...]).astype(BF16)

    return _call(body, name=name, out_shape=_sds((S, D), BF16), grid=(S // tm,),
                 in_specs=[pl.BlockSpec((tm, D), lambda i: (i, 0)), pl.BlockSpec((1, D), lambda i: (0, 0))] + [ANY] * len(tie),
                 out_specs=pl.BlockSpec((tm, D), lambda i: (i, 0)))(x, w, *tie)


def _norm_bwd_math(xf, w, dh, eps):
    r = lax.rsqrt(jnp.mean(xf * xf, axis=-1, keepdims=True) + eps)
    xhat = xf * r
    dxh = dh * w
    dx = r * (dxh - xhat * jnp.mean(dxh * xhat, axis=-1, keepdims=True))
    dw = jnp.sum(dh * xhat, axis=0, keepdims=True)
    return dx, dw


def _rmsnorm_bwd(x, w, dh, dres, *, name, eps=NORM_EPS, after=None):
    S, D = x.shape
    tm = _tile(S, 512)
    tie = [] if after is None else [after]

    def body(x_ref, w_ref, dh_ref, dr_ref, *rest):
        dx_ref, dw_ref = rest[-2:]
        dx, dw = _norm_bwd_math(x_ref[...], w_ref[...], dh_ref[...].astype(F32), eps)
        dx_ref[...] = dr_ref[...] + dx

        @pl.when(pl.program_id(0) == 0)
        def _():
            dw_ref[...] = jnp.zeros_like(dw_ref)

        dw_ref[...] += dw

    row = pl.BlockSpec((tm, D), lambda i: (i, 0))
    vec = pl.BlockSpec((1, D), lambda i: (0, 0))
    return _call(body, name=name, out_shape=(_sds((S, D)), _sds((1, D))), grid=(S // tm,),
                 in_specs=[row, vec, row, row] + [ANY] * len(tie), out_specs=(row, vec))(x, w, dh, dres, *tie)


def _final_loss(x, w, target, *, name):
    S, D = x.shape
    tm = _tile(S, 512)

    def body(x_ref, w_ref, t_ref, loss_ref, dx_ref, dw_ref):
        xf, wv = x_ref[...], w_ref[...]
        r = lax.rsqrt(jnp.mean(xf * xf, axis=-1, keepdims=True) + NORM_EPS)
        err = xf * r * wv - t_ref[...]
        part = 0.5 * jnp.sum(jnp.mean(err * err, axis=-1, keepdims=True), axis=0, keepdims=True)
        dx, dw = _norm_bwd_math(xf, wv, err * (1.0 / D), NORM_EPS)
        dx_ref[...] = dx

        @pl.when(pl.program_id(0) == 0)
        def _():
            dw_ref[...] = jnp.zeros_like(dw_ref)
            loss_ref[...] = jnp.zeros_like(loss_ref)

        dw_ref[...] += dw
        loss_ref[...] += jnp.broadcast_to(part, loss_ref.shape)

    row = pl.BlockSpec((tm, D), lambda i: (i, 0))
    vec = pl.BlockSpec((1, D), lambda i: (0, 0))
    return _call(body, name=name, out_shape=(_sds((1, LANES)), _sds((S, D)), _sds((1, D))), grid=(S // tm,),
                 in_specs=[row, vec, row], out_specs=(pl.BlockSpec((1, LANES), lambda i: (0, 0)), row, vec))(x, w, target)


def _rope_tables(pos, inv_freq):
    S = pos.shape[0]
    tm = _tile(S, 512)

    def body(p_ref, f_ref, c_ref, s_ref):
        ang = p_ref[...].astype(F32) * f_ref[...]
        c_ref[...] = jnp.cos(ang)
        s_ref[...] = jnp.sin(ang)

    blk = pl.BlockSpec((tm, LANES), lambda i: (i, 0))
    return _call(body, name="rope_tables", out_shape=(_sds((S, LANES)), _sds((S, LANES))), grid=(S // tm,),
                 in_specs=[pl.BlockSpec((tm, 1), lambda i: (i, 0)), pl.BlockSpec((1, LANES), lambda i: (0, 0))],
                 out_specs=(blk, blk))(pos, inv_freq)


def _rot_half(t):
    lane = lax.broadcasted_iota(jnp.int32, t.shape, 1)
    lo = (lane % HEAD_DIM) < (HEAD_DIM // 2)
    return jnp.where(lo, -pltpu.roll(t, LANES - HEAD_DIM // 2, axis=1), pltpu.roll(t, HEAD_DIM // 2, axis=1))


def _rope(t, c, s):
    return t * c + _rot_half(t) * s


def _unrope(dy, c, s):
    return dy * c - _rot_half(dy * s)


PD = POOL_GROUPS * LANES
QD = N_HEADS * HEAD_DIM
KD = N_KV_HEADS * HEAD_DIM
assert PD % QD == 0 and (PD + QD) % (2 * KD) == 0 and KD == LANES
def _attn_probs(q, kcat, sink, mask):
    s = lax.dot_general(q.astype(BF16), kcat, (((1,), (1,)), ((), ())), preferred_element_type=F32) * (HEAD_DIM ** -0.5)
    s = jnp.where(mask, s, -jnp.inf)
    m = jnp.maximum(jnp.max(s, axis=1, keepdims=True), sink)
    p = jnp.exp(s - m)
    ps = jnp.exp(sink - m)
    inv = 1.0 / (jnp.sum(p, axis=1, keepdims=True) + ps)
    return p * inv, ps * inv


def _attn_mask(n):
    qi = lax.broadcasted_iota(jnp.int32, (BLOCK, 2 * BLOCK), 0)
    kj = lax.broadcasted_iota(jnp.int32, (BLOCK, 2 * BLOCK), 1)
    rel = qi + BLOCK - kj
    return (rel >= 0) & (rel < BLOCK) & ((n > 0) | (kj >= BLOCK))


def _attn_in_specs(nb):
    def cur(n):
        return jnp.minimum(n, nb - 1)

    def prev(n):
        return jnp.clip(n - 1, 0, nb - 1)

    kvb = (PD + QD) // (2 * KD)
    return [pl.BlockSpec(memory_space=pltpu.SMEM),
            pl.BlockSpec((BLOCK, QD), lambda n: (cur(n), PD // QD)),
            pl.BlockSpec((BLOCK, 2 * KD), lambda n: (cur(n), kvb)),
            pl.BlockSpec((BLOCK, 2 * KD), lambda n: (prev(n), kvb)),
            pl.BlockSpec((BLOCK, LANES), lambda n: (cur(n), 0)), pl.BlockSpec((BLOCK, LANES), lambda n: (cur(n), 0)),
            pl.BlockSpec((BLOCK, LANES), lambda n: (prev(n), 0)), pl.BlockSpec((BLOCK, LANES), lambda n: (prev(n), 0))]


def _attn_keys(kvc_ref, kvp_ref, cc, sc, cp, sp):
    kc = _rope(kvc_ref[:, :KD], cc, sc)
    kp = _rope(kvp_ref[:, :KD], cp, sp)
    vc, vp = kvc_ref[:, KD:], kvp_ref[:, KD:]
    kcat, vcat = [], []
    for kk in range(N_KV_HEADS):
        sl = slice(kk * HEAD_DIM, (kk + 1) * HEAD_DIM)
        kcat.append(jnp.concatenate([kp[:, sl], kc[:, sl]], axis=0).astype(BF16))
        vcat.append(jnp.concatenate([vp[:, sl], vc[:, sl]], axis=0).astype(BF16))
    return kcat, vcat


def _attn_fwd(proj, cos, sin, sinks, cat):
    S = proj.shape[0]
    nb = S // BLOCK

    def body(sink_ref, q_ref, kvc_ref, kvp_ref, cc_ref, sc_ref, cp_ref, sp_ref, cat_ref, o_ref):
        n = pl.program_id(0)
        cc, sc = cc_ref[...], sc_ref[...]
        kcat, vcat = _attn_keys(kvc_ref, kvp_ref, cc, sc, cp_ref[...], sp_ref[...])
        mask = _attn_mask(n)
        for j in range(QD // LANES):
            qr = _rope(q_ref[:, j * LANES:(j + 1) * LANES], cc, sc)
            for e in range(LANES // HEAD_DIM):
                h = j * (LANES // HEAD_DIM) + e
                pn, _ = _attn_probs(qr[:, e * HEAD_DIM:(e + 1) * HEAD_DIM], kcat[h // GQ], sink_ref[0, h], mask)
                o_ref[:, h * HEAD_DIM:(h + 1) * HEAD_DIM] = jnp.dot(pn.astype(BF16), vcat[h // GQ], preferred_element_type=F32)

    return _call(body, name="attn_fwd", out_shape=_sds(cat.shape), grid=(nb,),
                 in_specs=_attn_in_specs(nb) + [ANY], out_specs=pl.BlockSpec((BLOCK, QD), lambda n: (n, PD // QD)),
                 aliases={8: 0})(sinks, proj, proj, proj, cos, sin, cos, sin, cat)


def _attn_bwd(proj, cos, sin, sinks, dcat):
    S = proj.shape[0]
    nb = S // BLOCK
    scale = HEAD_DIM ** -0.5
    per = LANES // HEAD_DIM

    def body(sink_ref, q_ref, kvc_ref, kvp_ref, cc_ref, sc_ref, cp_ref, sp_ref, do_ref, o_ref, ds_ref, hold, carry, part, pair):
        n = pl.program_id(0)

        @pl.when(n == 0)
        def _():
            hold[...] = jnp.zeros_like(hold)
            carry[...] = jnp.zeros_like(carry)
            ds_ref[...] = jnp.zeros_like(ds_ref)

        live = jnp.where(n < nb, 1.0, 0.0)
        cc, sc, cp, sp = cc_ref[...], sc_ref[...], cp_ref[...], sp_ref[...]
        kcat, vcat = _attn_keys(kvc_ref, kvp_ref, cc, sc, cp, sp)
        mask = _attn_mask(n)
        o_ref[:, :PD] = jnp.zeros((BLOCK, PD), F32)
        o_ref[:, PD:PD + QD] = hold[...]
        dk = [jnp.zeros((2 * BLOCK, HEAD_DIM), F32) for _ in range(N_KV_HEADS)]
        dv = [jnp.zeros((2 * BLOCK, HEAD_DIM), F32) for _ in range(N_KV_HEADS)]
        row = lax.broadcasted_iota(jnp.int32, (8, LANES), 0)
        dsk = jnp.zeros((8, LANES), F32)
        for j in range(QD // LANES):
            qr = _rope(q_ref[:, j * LANES:(j + 1) * LANES], cc, sc)
            for e in range(per):
                h = j * per + e
                kk = h // GQ
                qh = qr[:, e * HEAD_DIM:(e + 1) * HEAD_DIM]
                pn, psn = _attn_probs(qh, kcat[kk], sink_ref[0, h], mask)
                doh = (do_ref[:, h * HEAD_DIM:(h + 1) * HEAD_DIM] * live).astype(BF16)
                dp = lax.dot_general(doh, vcat[kk], NT, preferred_element_type=F32)
                delta = jnp.sum(pn * dp, axis=1, keepdims=True)
                ds = (pn * (dp - delta) * scale).astype(BF16)
                pair[:, e * HEAD_DIM:(e + 1) * HEAD_DIM] = jnp.dot(ds, kcat[kk], preferred_element_type=F32)
                dk[kk] = dk[kk] + lax.dot_general(ds, qh.astype(BF16), TN, preferred_element_type=F32)
                dv[kk] = dv[kk] + lax.dot_general(pn.astype(BF16), doh, TN, preferred_element_type=F32)
                dsk = dsk + jnp.where(row == h, -jnp.sum(psn * delta), 0.0)
            hold[:, j * LANES:(j + 1) * LANES] = _unrope(pair[...], cc, sc)
        for kk in range(N_KV_HEADS):
            sl = slice(kk * HEAD_DIM, (kk + 1) * HEAD_DIM)
            sv = slice(KD + kk * HEAD_DIM, KD + (kk + 1) * HEAD_DIM)
            part[0, :, sl] = dk[kk][:BLOCK]
            part[0, :, sv] = dv[kk][:BLOCK]
            part[1, :, sl] = dk[kk][BLOCK:]
            part[1, :, sv] = dv[kk][BLOCK:]
        done = carry[...] + part[0]
        o_ref[:, PD + QD:PD + QD + KD] = _unrope(done[:, :KD], cp, sp)
        o_ref[:, PD + QD + KD:] = done[:, KD:]
        carry[...] = part[1]
        ds_ref[...] += dsk

    return _call(body, name="attn_bwd", out_shape=(_sds((S, PD + QD + 2 * KD)), _sds((8, LANES))), grid=(nb + 1,),
                 in_specs=_attn_in_specs(nb) + [pl.BlockSpec((BLOCK, QD), lambda n: (jnp.minimum(n, nb - 1), PD // QD))],
                 out_specs=(pl.BlockSpec((BLOCK, PD + QD + 2 * KD), lambda n: (jnp.maximum(n - 1, 0), 0)),
                            pl.BlockSpec((8, LANES), lambda n: (0, 0))),
                 scratch=[pltpu.VMEM((BLOCK, QD), F32), pltpu.VMEM((BLOCK, 2 * KD), F32),
                          pltpu.VMEM((2, BLOCK, 2 * KD), F32), pltpu.VMEM((BLOCK, LANES), F32)])(
                     sinks, proj, proj, proj, cos, sin, cos, sin, dcat)


def _pool_sums(u, g, t, shift):
    s2 = u + shift(u, 1, t)
    s4 = s2 + shift(s2, 2, t)
    s8 = s4 + shift(s4, 4, t)
    s16 = s8 + shift(s8, 8, t)
    return jnp.where(g == 0, s2, jnp.where(g == 1, s4, jnp.where(g == 2, s8, s16)))


def _pool_specs(S):
    col = pl.BlockSpec((S, LANES), lambda g: (0, g))
    wsp = pl.BlockSpec((1, LANES, LANES), lambda g: (g, 0, 0))
    vec = pl.BlockSpec((1, LANES), lambda g: (0, g))
    return col, wsp, vec


def _pool_fwd(proj, pool_w, scale):
    S = proj.shape[0]
    col, wsp, vec = _pool_specs(S)

    def body(u_ref, w_ref, s_ref, o_ref):
        g = pl.program_id(0)
        u = u_ref[...]
        t = lax.broadcasted_iota(jnp.int32, u.shape, 0)
        cnt = jnp.minimum(t + 1, 2 << g).astype(F32)
        pm = _pool_sums(u, g, t, _shift_dn) / cnt - u
        o_ref[...] = jnp.dot(pm.astype(BF16), w_ref[0].astype(BF16), preferred_element_type=F32) * s_ref[...]

    return _call(body, name="pool_fwd", out_shape=_sds((S, PD + QD)), grid=(POOL_GROUPS,),
                 in_specs=[col, wsp, vec], out_specs=col)(proj, pool_w, scale)


def _pool_bwd(proj, pool_w, scale, dcat, dproj):
    S = proj.shape[0]
    col, wsp, vec = _pool_specs(S)

    def body(u_ref, w_ref, s_ref, d_ref, dproj_ref, du_ref, dw_ref, dsc_ref):
        g = pl.program_id(0)
        u = u_ref[...]
        t = lax.broadcasted_iota(jnp.int32, u.shape, 0)
        cnt = jnp.minimum(t + 1, 2 << g).astype(F32)
        pm = (_pool_sums(u, g, t, _shift_dn) / cnt - u).astype(BF16)
        wv = w_ref[0].astype(BF16)
        d = d_ref[...]
        pw = jnp.dot(pm, wv, preferred_element_type=F32)
        dsc_ref[...] = jnp.sum(pw * d, axis=0, keepdims=True)
        dpw = (d * s_ref[...]).astype(BF16)
        dw_ref[0] = lax.dot_general(pm, dpw, (((0,), (0,)), ((), ())), preferred_element_type=F32)
        dpm = lax.dot_general(dpw, wv, (((1,), (1,)), ((), ())), preferred_element_type=F32)
        du_ref[...] = _pool_sums(dpm / cnt, g, t, _shift_up) - dpm

    return _call(body, name="pool_bwd",
                 out_shape=(_sds(dproj.shape), _sds((POOL_GROUPS, LANES, LANES)), _sds((1, POOL_GROUPS * LANES))),
                 grid=(POOL_GROUPS,), in_specs=[col, wsp, vec, col, ANY], out_specs=(col, wsp, vec),
                 aliases={4: 0})(proj, pool_w, scale, dcat, dproj)


def _silu_grad(y):
    sg = _sigmoid(y)
    return sg * (1.0 + y * (1.0 - sg))


CONV_ROWS = 64
HALO = 8


def _win_above(ref, r0):
    if isinstance(r0, int):
        assert r0 == 0
        return jnp.concatenate([jnp.zeros((HALO, ref.shape[1]), F32), ref[0:CONV_ROWS, :]], axis=0)
    return ref[pl.ds(pl.multiple_of(r0 - HALO, HALO), CONV_ROWS + HALO), :]


def _rows_at(win, start):
    if start % 8 == 0:
        return win[start:start + CONV_ROWS]
    base = start // 8 * 8
    return pltpu.roll(win, win.shape[0] - (start - base), axis=0)[base:base + CONV_ROWS]


def _taps_above(win, K):
    return [_rows_at(win, HALO - (K - 1 - k)) for k in range(K)]


def _conv_taps(taps, w, b):
    y = b
    for k in range(len(w)):
        y = y + w[k] * taps[k]
    return y


def _conv_t_win(win, w):
    K = len(w)
    out = None
    for k in range(K):
        d = K - 1 - k
        term = w[k] * _rows_at(win, d)
        out = term if out is None else out + term
    return out


def _fold8(x):
    return jnp.sum(x.reshape(CONV_ROWS // 8, 8, x.shape[-1]), axis=0)


def _chunk_loop(S, step, init):
    carry = step(0, init)
    return lax.fori_loop(1, S // CONV_ROWS, lambda i, c: step(pl.multiple_of(i * CONV_ROWS, CONV_ROWS), c), carry)


def _ffn_mid_specs(S, K, layer):
    return [pl.BlockSpec((S, LANES), lambda j: (0, 2 * j)), pl.BlockSpec((S, LANES), lambda j: (0, 2 * j + 1)),
            pl.BlockSpec((None, K, LANES), lambda j: (layer, 0, 2 * j)), pl.BlockSpec((None, K, LANES), lambda j: (layer, 0, 2 * j + 1)),
            pl.BlockSpec((None, 1, LANES), lambda j: (layer, 0, 2 * j)), pl.BlockSpec((None, 1, LANES), lambda j: (layer, 0, 2 * j + 1))]


def _ffn_mid_fwd(a, cw, cb, layer):
    S, F2 = a.shape
    nf = F2 // (2 * LANES)
    K = cw.shape[1]

    def body(au_ref, ag_ref, wu_ref, wg_ref, bu_ref, bg_ref, o_ref):
        wu = [wu_ref[k:k + 1, :] for k in range(K)]
        wg = [wg_ref[k:k + 1, :] for k in range(K)]
        bu, bg = bu_ref[...], bg_ref[...]

        def step(r0, carry):
            hu = _conv_taps(_taps_above(_win_above(au_ref, r0), K), wu, bu)
            hg = _conv_taps(_taps_above(_win_above(ag_ref, r0), K), wg, bg)
            o_ref[pl.ds(r0, CONV_ROWS), :] = (hg * _sigmoid(hg) * hu).astype(BF16)
            return carry

        _chunk_loop(S, step, 0)

    return _call(body, name="ffn_mid_fwd", out_shape=_sds((S, F2 // 2), BF16), grid=(nf,),
                 in_specs=_ffn_mid_specs(S, K, layer), out_specs=pl.BlockSpec((S, LANES), lambda j: (0, j)))(
                     a, a, cw, cw, cb[:, None], cb[:, None])


def _ffn_mid_bwd(a, cw, cb, layer, dact):
    S, F2 = a.shape
    nf = F2 // (2 * LANES)
    K = cw.shape[1]

    def body(au_ref, ag_ref, wu_ref, wg_ref, bu_ref, bg_ref, d_ref, da_ref, dw_ref, db_ref, dhu_s, dhg_s):
        T = CONV_ROWS
        wu = [wu_ref[k:k + 1, :] for k in range(K)]
        wg = [wg_ref[k:k + 1, :] for k in range(K)]
        bu, bg = bu_ref[...], bg_ref[...]
        zero8 = jnp.zeros((HALO, LANES), F32)
        dhu_s[S:S + HALO, :] = zero8
        dhg_s[S:S + HALO, :] = zero8

        def first_pass(r0, acc):
            tu, tg = _taps_above(_win_above(au_ref, r0), K), _taps_above(_win_above(ag_ref, r0), K)
            hu, hg = _conv_taps(tu, wu, bu), _conv_taps(tg, wg, bg)
            d = d_ref[pl.ds(r0, T), :].astype(F32)
            sg = _sigmoid(hg)
            dhu = d * hg * sg
            dhg = d * hu * (sg * (1.0 + hg * (1.0 - sg)))
            dhu_s[pl.ds(r0, T), :] = dhu
            dhg_s[pl.ds(r0, T), :] = dhg
            new = []
            for dh, taps in ((dhu, tu), (dhg, tg)):
                for k in range(K):
                    new.append(acc[len(new)] + _fold8(dh * taps[k]))
            new.append(acc[2 * K] + _fold8(dhu))
            new.append(acc[2 * K + 1] + _fold8(dhg))
            return tuple(new)

        acc = _chunk_loop(S, first_pass, tuple(zero8 for _ in range(2 * K + 2)))
        for k in range(K):
            dw_ref[k:k + 1, :LANES] = jnp.sum(acc[k], axis=0, keepdims=True)
            dw_ref[k:k + 1, LANES:] = jnp.sum(acc[K + k], axis=0, keepdims=True)
        db_ref[:, :LANES] = jnp.sum(acc[2 * K], axis=0, keepdims=True)
        db_ref[:, LANES:] = jnp.sum(acc[2 * K + 1], axis=0, keepdims=True)

        def second_pass(i, carry):
            r0 = pl.multiple_of(i * T, T)
            da_ref[pl.ds(r0, T), :LANES] = _conv_t_win(dhu_s[pl.ds(r0, T + HALO), :], wu).astype(BF16)
            da_ref[pl.ds(r0, T), LANES:] = _conv_t_win(dhg_s[pl.ds(r0, T + HALO), :], wg).astype(BF16)
            return carry

        lax.fori_loop(0, S // T, second_pass, 0)

    return _call(body, name="ffn_mid_bwd", out_shape=(_sds((S, F2), BF16), _sds((K, F2)), _sds((1, F2))), grid=(nf,),
                 in_specs=_ffn_mid_specs(S, K, layer) + [pl.BlockSpec((S, LANES), lambda j: (0, j))],
                 out_specs=(pl.BlockSpec((S, 2 * LANES), lambda j: (0, j)), pl.BlockSpec((K, 2 * LANES), lambda j: (0, j)),
                            pl.BlockSpec((1, 2 * LANES), lambda j: (0, j))),
                 scratch=[pltpu.VMEM((S + HALO, LANES), F32), pltpu.VMEM((S + HALO, LANES), F32)])(
                     a, a, cw, cw, cb[:, None], cb[:, None], dact)


def _conv_silu_fwd(x, cw, cb):
    S = x.shape[0]
    K, C = cw.shape

    def body(x_ref, w_ref, b_ref, o_ref):
        w = [w_ref[k:k + 1, :] for k in range(K)]
        b = b_ref[...]

        def step(r0, carry):
            y = _conv_taps(_taps_above(_win_above(x_ref, r0), K), w, b)
            o_ref[pl.ds(r0, CONV_ROWS), :] = y * _sigmoid(y)
            return carry

        _chunk_loop(S, step, 0)

    col = pl.BlockSpec((S, LANES), lambda j: (0, j))
    return _call(body, name="conv_silu_fwd", out_shape=_sds((S, C)), grid=(C // LANES,),
                 in_specs=[col, pl.BlockSpec((K, LANES), lambda j: (0, j)), pl.BlockSpec((1, LANES), lambda j: (0, j))],
                 out_specs=col)(x, cw, cb)


def _conv_silu_bwd(x, cw, cb, douts):
    S = x.shape[0]
    K, C = cw.shape
    starts, off = [], 0
    for d in douts:
        starts.append(off)
        off += d.shape[1] // LANES
    assert off == C // LANES

    def body(x_ref, w_ref, b_ref, *rest):
        dy_s = rest[-1]
        d_refs, (dx_ref, dw_ref, db_ref) = rest[:len(douts)], rest[len(douts):-1]
        j = pl.program_id(0)
        T = CONV_ROWS
        w = [w_ref[k:k + 1, :] for k in range(K)]
        b = b_ref[...]
        zero8 = jnp.zeros((HALO, LANES), F32)
        dy_s[S:S + HALO, :] = zero8

        def first_pass(r0, acc):
            taps = _taps_above(_win_above(x_ref, r0), K)
            y = _conv_taps(taps, w, b)
            d = d_refs[0][pl.ds(r0, T), :]
            for i in range(1, len(douts)):
                d = jnp.where(j >= starts[i], d_refs[i][pl.ds(r0, T), :], d)
            dy = d * _silu_grad(y)
            dy_s[pl.ds(r0, T), :] = dy
            return tuple(acc[k] + _fold8(dy * taps[k]) for k in range(K)) + (acc[K] + _fold8(dy),)

        acc = _chunk_loop(S, first_pass, tuple(zero8 for _ in range(K + 1)))
        for k in range(K):
            dw_ref[k:k + 1, :] = jnp.sum(acc[k], axis=0, keepdims=True)
        db_ref[...] = jnp.sum(acc[K], axis=0, keepdims=True)

        def second_pass(i, carry):
            r0 = pl.multiple_of(i * T, T)
            dx_ref[pl.ds(r0, T), :] = _conv_t_win(dy_s[pl.ds(r0, T + HALO), :], w).astype(BF16)
            return carry

        lax.fori_loop(0, S // T, second_pass, 0)

    col = pl.BlockSpec((S, LANES), lambda j: (0, j))
    wsp = pl.BlockSpec((K, LANES), lambda j: (0, j))
    bsp = pl.BlockSpec((1, LANES), lambda j: (0, j))

    def dspec(i):
        nblk = douts[i].shape[1] // LANES
        return pl.BlockSpec((S, LANES), lambda j: (0, jnp.clip(j - starts[i], 0, nblk - 1)))

    return _call(body, name="conv_silu_bwd", out_shape=(_sds((S, C), BF16), _sds((K, C)), _sds((1, C))), grid=(C // LANES,),
                 in_specs=[col, wsp, bsp] + [dspec(i) for i in range(len(douts))],
                 out_specs=(col, wsp, bsp), scratch=[pltpu.VMEM((S + HALO, LANES), F32)])(x, cw, cb, *douts)


HI = lax.Precision.HIGHEST


def _ssd_prep_fwd(proj, col0, bias_g, alog_g):
    S = proj.shape[0]
    nc = S // SSM_L
    b0 = col0 // LANES

    def body(raw_ref, b_ref, al_ref, pre_ref, dt_ref, acs_ref, acst_ref):
        r_i = lax.broadcasted_iota(jnp.int32, (LANES, LANES), 0)
        c_i = lax.broadcasted_iota(jnp.int32, (LANES, LANES), 1)
        live = c_i < SSM_R
        tril = jnp.where(r_i >= c_i, 1.0, 0.0)
        raw = raw_ref[...]
        for g in range(SSM_G):
            sel = jnp.where((r_i == SSM_R * g + c_i) & live, 1.0, 0.0)
            pre = jnp.dot(raw, sel, preferred_element_type=F32, precision=HI) + b_ref[g]
            dt = jnp.where(live, jnp.logaddexp(pre, 0.0), 0.0)
            a = dt * (-jnp.exp(al_ref[g]))
            acs = jnp.dot(tril, a, preferred_element_type=F32, precision=HI)
            pre_ref[g] = pre
            dt_ref[g] = dt
            acs_ref[g] = acs
            acst_ref[g] = acs.T

    gsp = pl.BlockSpec((SSM_G, 1, LANES), lambda c: (0, 0, 0))
    blk = pl.BlockSpec((SSM_G, SSM_L, LANES), lambda c: (0, c, 0))
    big = _sds((SSM_G, S, LANES))
    return _call(body, name="ssd_prep_fwd", out_shape=(big, big, big, _sds((SSM_G, LANES, S))), grid=(nc,),
                 in_specs=[pl.BlockSpec((SSM_L, LANES), lambda c: (c, b0)), gsp, gsp],
                 out_specs=(blk, blk, blk, pl.BlockSpec((SSM_G, LANES, SSM_L), lambda c: (0, 0, c))))(proj, bias_g, alog_g)


def _ssd_prep_bwd(pre_g, dt_g, alog_g, ddt_g, dacs_g, dacst_g):
    S = pre_g.shape[1]
    nc = S // SSM_L

    def body(pre_ref, dt_ref, al_ref, ddt_ref, dacs_ref, dacst_ref, draw_ref, db_ref, dal_ref):
        c = pl.program_id(0)
        r_i = lax.broadcasted_iota(jnp.int32, (LANES, LANES), 0)
        c_i = lax.broadcasted_iota(jnp.int32, (LANES, LANES), 1)
        live = c_i < SSM_R
        triu = jnp.where(r_i <= c_i, 1.0, 0.0)

        @pl.when(c == 0)
        def _():
            db_ref[...] = jnp.zeros_like(db_ref)
            dal_ref[...] = jnp.zeros_like(dal_ref)

        draw = jnp.zeros((SSM_L, LANES), F32)
        for g in range(SSM_G):
            dacs = dacs_ref[g] + dacst_ref[g].T
            da = jnp.dot(triu, dacs, preferred_element_type=F32, precision=HI)
            A = -jnp.exp(al_ref[g])
            ddt = ddt_ref[g] + da * A
            dpre = jnp.where(live, ddt * _sigmoid(pre_ref[g]), 0.0)
            unsel = jnp.where((c_i == SSM_R * g + r_i) & (r_i < SSM_R), 1.0, 0.0)
            draw = draw + jnp.dot(dpre, unsel, preferred_element_type=F32, precision=HI)
            db_ref[g] += jnp.sum(dpre, axis=0, keepdims=True)
            dal_ref[g] += jnp.where(live[:1], jnp.sum(da * dt_ref[g], axis=0, keepdims=True) * A, 0.0)
        draw_ref[...] = draw

    gsp = pl.BlockSpec((SSM_G, 1, LANES), lambda c: (0, 0, 0))
    blk = pl.BlockSpec((SSM_G, SSM_L, LANES), lambda c: (0, c, 0))
    gout = _sds((SSM_G, 1, LANES))
    return _call(body, name="ssd_prep_bwd", out_shape=(_sds((S, LANES)), gout, gout), grid=(nc,),
                 in_specs=[blk, blk, gsp, blk, blk, pl.BlockSpec((SSM_G, LANES, SSM_L), lambda c: (0, 0, c))],
                 out_specs=(pl.BlockSpec((SSM_L, LANES), lambda c: (c, 0)), gsp, gsp))(pre_g, dt_g, alog_g, ddt_g, dacs_g, dacst_g)


NT = (((1,), (1,)), ((), ()))
TN = (((0,), (0,)), ((), ()))
SSM_HP = SSM_R * SSM_P


def _ssd_group_terms(xs_ref, dt_ref, acs_ref, d_ref):
    hid = lax.broadcasted_iota(jnp.int32, (1, SSM_HP), 1) // SSM_P
    rid = lax.broadcasted_iota(jnp.int32, (SSM_HP, 1), 0) // SSM_P

    def widen(cols):
        out = cols[0]
        for r in range(1, SSM_R):
            out = jnp.where(hid == r, cols[r], out)
        return out

    dt_c = [dt_ref[:, r:r + 1] for r in range(SSM_R)]
    acs_c = [acs_ref[:, r:r + 1] for r in range(SSM_R)]
    last = [acs_ref[SSM_L - 1:SSM_L, r:r + 1] for r in range(SSM_R)]
    decay_c = [jnp.exp(last[r] - acs_c[r]) for r in range(SSM_R)]
    cd = [jnp.exp(last[r]) for r in range(SSM_R)]
    cd_rows = cd[0]
    for r in range(1, SSM_R):
        cd_rows = jnp.where(rid == r, cd[r], cd_rows)
    xs = xs_ref[...]
    return (xs, xs * widen(dt_c), widen([jnp.exp(a) for a in acs_c]), widen(decay_c),
            widen([d_ref[:, r:r + 1] for r in range(SSM_R)]), cd_rows, dt_c, decay_c, cd)


def _ssd_lmat(acs_ref, acst_ref, r, tril):
    return jnp.exp(jnp.where(tril, acs_ref[:, r:r + 1] - acst_ref[r:r + 1, :], -jnp.inf))


SSM_GPS = 2


def _ssd_specs(rev, nc):
    def cc(c):
        return nc - 1 - c if rev else c
    xs_blocks = (SSM_G * SSM_HP) // (SSM_GPS * SSM_N)
    xs = pl.BlockSpec((SSM_L, SSM_GPS * SSM_HP), lambda g, c: (cc(c), g))
    bsp = pl.BlockSpec((SSM_L, SSM_GPS * SSM_N), lambda g, c: (cc(c), xs_blocks + g))
    csp = pl.BlockSpec((SSM_L, SSM_GPS * SSM_N), lambda g, c: (cc(c), xs_blocks + SSM_G // SSM_GPS + g))
    sc = pl.BlockSpec((SSM_GPS, SSM_L, LANES), lambda g, c: (g, cc(c), 0))
    sct = pl.BlockSpec((SSM_GPS, LANES, SSM_L), lambda g, c: (g, 0, cc(c)))
    gsp = pl.BlockSpec((SSM_GPS, 1, LANES), lambda g, c: (g, 0, 0))
    st = pl.BlockSpec((None, SSM_GPS, SSM_HP, SSM_N), lambda g, c: (cc(c), g, 0, 0))
    return xs, bsp, csp, sc, sct, gsp, st


def _interleave(gens):
    live = list(gens)
    while live:
        for g in list(live):
            try:
                next(g)
            except StopIteration:
                live.remove(g)


def _rounds(gens):
    live = list(gens)
    while live:
        for g in list(live):
            try:
                next(g)
            except StopIteration:
                live.remove(g)
        yield


def _ssd_group_views(gg, xs_ref, b_ref, c_ref, *per_group):
    return (xs_ref.at[:, gg * SSM_HP:(gg + 1) * SSM_HP], b_ref.at[:, gg * SSM_N:(gg + 1) * SSM_N],
            c_ref.at[:, gg * SSM_N:(gg + 1) * SSM_N]) + tuple(r.at[gg] for r in per_group)


def _ssd_fwd(xbc, dt_g, acs_g, acst_g, d_g):
    S = xbc.shape[0]
    nc = S // SSM_L
    xs_s, b_s, c_s, sc, sct, gsp, st = _ssd_specs(False, nc)

    def body(xs_ref, b_ref, c_ref, dt_ref, acs_ref, acst_ref, d_ref, y_ref, st_ref, state):
        c = pl.program_id(1)

        @pl.when(c == 0)
        def _():
            state[...] = jnp.zeros_like(state)

        tril = lax.broadcasted_iota(jnp.int32, (SSM_L, SSM_L), 0) >= lax.broadcasted_iota(jnp.int32, (SSM_L, SSM_L), 1)
        def group(gg):
            xs_v, b_v, c_v, dt_v, acs_v, acst_v, d_v, st_v, state_v = _ssd_group_views(
                gg, xs_ref, b_ref, c_ref, dt_ref, acs_ref, acst_ref, d_ref, st_ref, state)
            y_v = y_ref.at[:, gg * SSM_HP:(gg + 1) * SSM_HP]
            Bb, Cb = b_v[...].astype(BF16), c_v[...].astype(BF16)
            Gm = lax.dot_general(Cb, Bb, NT, preferred_element_type=F32)
            yield
            xs, X, e_all, decay_all, d_all, cd_rows, _, _, _ = _ssd_group_terms(xs_v, dt_v, acs_v, d_v)
            S_all = state_v[...]
            st_v[...] = S_all
            yield
            yo = lax.dot_general(Cb, S_all.astype(BF16), NT, preferred_element_type=F32)
            new_state = lax.dot_general((X * decay_all).astype(BF16), Bb, TN, preferred_element_type=F32)
            yield
            y_v[...] = e_all * yo + d_all * xs
            state_v[...] = S_all * cd_rows + new_state
            for r in range(SSM_R):
                yield
                sl = slice(r * SSM_P, (r + 1) * SSM_P)
                M = Gm * _ssd_lmat(acs_v, acst_v, r, tril)
                yield
                y_v[:, sl] += jnp.dot(M.astype(BF16), X[:, sl].astype(BF16), preferred_element_type=F32)

        _interleave([group(gg) for gg in range(SSM_GPS)])

    return _call(body, name="ssd_fwd",
                 out_shape=(_sds((S, SSM_G * SSM_HP)), _sds((nc, SSM_G, SSM_HP, SSM_N))),
                 grid=(SSM_G // SSM_GPS, nc), in_specs=[xs_s, b_s, c_s, sc, sc, sct, gsp],
                 out_specs=(xs_s, pl.BlockSpec((None, SSM_GPS, SSM_HP, SSM_N), lambda g, c: (c, g, 0, 0))),
                 scratch=[pltpu.VMEM((SSM_GPS, SSM_HP, SSM_N), F32)])(xbc, xbc, xbc, dt_g, acs_g, acst_g, d_g)


def _ssd_bwd(xbc, dt_g, acs_g, acst_g, d_g, states, dy):
    S = xbc.shape[0]
    nc = S // SSM_L
    xs_s, b_s, c_s, sc, sct, gsp, st = _ssd_specs(True, nc)
    bc_out = pl.BlockSpec((SSM_L, SSM_GPS * SSM_N), lambda g, c: (nc - 1 - c, g))

    def body(xs_ref, b_ref, c_ref, dt_ref, acs_ref, acst_ref, d_ref, st_ref, dy_ref,
             dxs_ref, db_ref, dc_ref, ddt_ref, dacs_ref, dacst_ref, dd_ref, dstate):
        c = pl.program_id(1)

        @pl.when(c == 0)
        def _():
            dstate[...] = jnp.zeros_like(dstate)
            dd_ref[...] = jnp.zeros_like(dd_ref)

        tril = lax.broadcasted_iota(jnp.int32, (SSM_L, SSM_L), 0) >= lax.broadcasted_iota(jnp.int32, (SSM_L, SSM_L), 1)
        lane = lax.broadcasted_iota(jnp.int32, (1, LANES), 1)
        subl = lax.broadcasted_iota(jnp.int32, (LANES, 1), 0)
        last_row = lax.broadcasted_iota(jnp.int32, (SSM_L, 1), 0) == SSM_L - 1
        triu = lax.broadcasted_iota(jnp.int32, (SSM_L, SSM_L), 0) <= lax.broadcasted_iota(jnp.int32, (SSM_L, SSM_L), 1)
        def group(gg):
            xs_v, b_v, c_v, dt_v, acs_v, acst_v, d_v, st_v, ddt_v, dacs_v, dacst_v, dd_v, dstate_v = _ssd_group_views(
                gg, xs_ref, b_ref, c_ref, dt_ref, acs_ref, acst_ref, d_ref, st_ref, ddt_ref, dacs_ref, dacst_ref, dd_ref, dstate)
            dy_v, dxs_v = (r.at[:, gg * SSM_HP:(gg + 1) * SSM_HP] for r in (dy_ref, dxs_ref))
            db_v, dc_v = (r.at[:, gg * SSM_N:(gg + 1) * SSM_N] for r in (db_ref, dc_ref))
            Bb, Cb = b_v[...].astype(BF16), c_v[...].astype(BF16)
            Gm = lax.dot_general(Cb, Bb, NT, preferred_element_type=F32)
            GmT = lax.dot_general(Bb, Cb, NT, preferred_element_type=F32)
            yield
            xs, X, e_all, decay_all, d_all, cd_rows, dt_c, decay_c, cd = _ssd_group_terms(xs_v, dt_v, acs_v, d_v)
            S_all, dSn_all, dY = st_v[...], dstate_v[...], dy_v[...]
            Sb, dSnb = S_all.astype(BF16), dSn_all.astype(BF16)
            yield
            T = lax.dot_general(Cb, Sb, NT, preferred_element_type=F32)
            dT = (dY * e_all).astype(BF16)
            dC = jnp.dot(dT, Sb, preferred_element_type=F32)
            dS_prev = lax.dot_general(dT, Cb, TN, preferred_element_type=F32)
            yield
            yo_dy = dY * (e_all * T)
            W = lax.dot_general(Bb, dSnb, NT, preferred_element_type=F32)
            dB = jnp.dot((X * decay_all).astype(BF16), dSnb, preferred_element_type=F32)
            yield
            xw = X * W
            dcd_rows = jnp.sum(dSn_all * S_all, axis=1, keepdims=True)
            dstate_v[...] = dS_prev + dSn_all * cd_rows
            dX_state = W * decay_all
            yield
            acc = dict(dG=jnp.zeros((SSM_L, SSM_L), F32), dGT=jnp.zeros((SSM_L, SSM_L), F32),
                       ddt=jnp.zeros((SSM_L, LANES), F32), dacs=jnp.zeros((SSM_L, LANES), F32),
                       dacst=jnp.zeros((LANES, SSM_L), F32), dd=jnp.zeros((1, LANES), F32))

            def head(r):
                sl = slice(r * SSM_P, (r + 1) * SSM_P)
                Lm = _ssd_lmat(acs_v, acst_v, r, tril)
                LmT = jnp.exp(jnp.where(triu, acst_v[r:r + 1, :] - acs_v[:, r:r + 1], -jnp.inf))
                M = Gm * Lm
                yield
                dYh, xs_h = dY[:, sl], xs[:, sl]
                dYb, Xb = dYh.astype(BF16), X[:, sl].astype(BF16)
                dM = lax.dot_general(dYb, Xb, NT, preferred_element_type=F32)
                yield
                dX = jnp.dot((GmT * LmT).astype(BF16), dYb, preferred_element_type=F32) + dX_state[:, sl]
                acc["dG"] = acc["dG"] + dM * Lm
                acc["dGT"] = acc["dGT"] + lax.dot_general(Xb, dYb, NT, preferred_element_type=F32) * LmT
                yield
                dseg = dM * M
                dd = jnp.sum(xw[:, sl], axis=1, keepdims=True) * decay_c[r]
                dcd = jnp.sum(dcd_rows[sl])
                dacs_col = (jnp.sum(dseg, axis=1, keepdims=True) + jnp.sum(yo_dy[:, sl], axis=1, keepdims=True) - dd
                            + jnp.where(last_row, dcd * cd[r] + jnp.sum(dd), 0.0))
                dacs_row = -jnp.sum(dseg, axis=0, keepdims=True)
                yield
                dxs_v[:, sl] = dX * dt_c[r] + d_all[:, sl] * dYh
                acc["ddt"] = acc["ddt"] + jnp.where(lane == r, jnp.sum(dX * xs_h, axis=1, keepdims=True), 0.0)
                acc["dacs"] = acc["dacs"] + jnp.where(lane == r, dacs_col, 0.0)
                acc["dacst"] = acc["dacst"] + jnp.where(subl == r, dacs_row, 0.0)
                acc["dd"] = acc["dd"] + jnp.where(lane == r, jnp.sum(dYh * xs_h), 0.0)

            yield from _rounds([head(r) for r in range(SSM_R)])
            dc_v[...] = dC + jnp.dot(acc["dG"].astype(BF16), Bb, preferred_element_type=F32)
            db_v[...] = dB + jnp.dot(acc["dGT"].astype(BF16), Cb, preferred_element_type=F32)
            ddt_v[...] = acc["ddt"]
            dacs_v[...] = acc["dacs"]
            dacst_v[...] = acc["dacst"]
            dd_v[...] += acc["dd"]

        _interleave([group(gg) for gg in range(SSM_GPS)])

    big = _sds((SSM_G, S, LANES))
    return _call(body, name="ssd_bwd",
                 out_shape=(_sds((S, SSM_G * SSM_HP)), _sds((S, SSM_G * SSM_N)), _sds((S, SSM_G * SSM_N)),
                            big, big, _sds((SSM_G, LANES, S)), _sds((SSM_G, 1, LANES))),
                 grid=(SSM_G // SSM_GPS, nc), in_specs=[xs_s, b_s, c_s, sc, sc, sct, gsp, st, xs_s],
                 out_specs=(xs_s, bc_out, bc_out, sc, sc, sct, gsp),
                 scratch=[pltpu.VMEM((SSM_GPS, SSM_HP, SSM_N), F32)])(xbc, xbc, xbc, dt_g, acs_g, acst_g, d_g, states, dy)


def _gate_norm_fwd(y, proj, w):
    S, DI = y.shape
    tm = _tile(S, 256)

    def body(y_ref, z_ref, w_ref, o_ref):
        z = z_ref[...]
        gn = y_ref[...] * (z * _sigmoid(z))
        r = lax.rsqrt(jnp.mean(gn * gn, axis=-1, keepdims=True) + SSM_NORM_EPS)
        o_ref[...] = (gn * r * w_ref[...]).astype(BF16)

    row = pl.BlockSpec((tm, DI), lambda i: (i, 0))
    return _call(body, name="gate_norm_fwd", out_shape=_sds((S, DI), BF16), grid=(S // tm,),
                 in_specs=[row, row, pl.BlockSpec((1, DI), lambda i: (0, 0))], out_specs=row)(y, proj, w)


def _gate_norm_bwd(y, proj, w, dout):
    S, DI = y.shape
    tm = _tile(S, 256)

    def body(y_ref, z_ref, w_ref, d_ref, dy_ref, dz_ref, dw_ref):
        z, yv = z_ref[...], y_ref[...]
        sz = z * _sigmoid(z)
        dgn, dw = _norm_bwd_math(yv * sz, w_ref[...], d_ref[...].astype(F32), SSM_NORM_EPS)
        dy_ref[...] = dgn * sz
        dz_ref[...] = (dgn * yv * _silu_grad(z)).astype(BF16)

        @pl.when(pl.program_id(0) == 0)
        def _():
            dw_ref[...] = jnp.zeros_like(dw_ref)

        dw_ref[...] += dw

    row = pl.BlockSpec((tm, DI), lambda i: (i, 0))
    vec = pl.BlockSpec((1, DI), lambda i: (0, 0))
    return _call(body, name="gate_norm_bwd", out_shape=(_sds((S, DI)), _sds((S, DI), BF16), _sds((1, DI))), grid=(S // tm,),
                 in_specs=[row, row, vec, row], out_specs=(row, row, vec))(y, proj, w, dout)


def _group_major(v):
    return jnp.pad(v.reshape(SSM_G, 1, SSM_R), ((0, 0), (0, 0), (0, LANES - SSM_R)))


def _ungroup(t):
    return t[:, :SSM_R].reshape(1, SSM_G * SSM_R)


def _ffn_fwd(x, P, l, need):
    need(f"ffn{l}_up", x)
    h = _rmsnorm(x, P["norm_ffn"][l:l + 1], name=f"ffn{l}_norm")
    a = _mm(h, P[f"ffn_w_up{l}"], name=f"ffn{l}_up")
    need(f"ffn{l}_down", a)
    act = _ffn_mid_fwd(a, P["ffn_conv_w"], P["ffn_conv_b"], l)
    out = _mm(act, P[f"ffn_w_down{l}"], res=x, name=f"ffn{l}_down")
    return out, (x, h, a, act)


def _ffn_bwd(saved, P, l, dx, emit):
    x, h, a, act = saved
    dact = _mm(dx, P[f"ffn_w_down{l}"], tb=True, out_dtype=BF16, name=f"ffn{l}_down_dx")
    dw_down = _mm(act, dx, ta=True, out_dtype=PAYLOAD, name=f"ffn{l}_down_dw")
    da, dcw, dcb = _ffn_mid_bwd(a, P["ffn_conv_w"], P["ffn_conv_b"], l, dact)
    dw_up = _mm(h, da, ta=True, out_dtype=PAYLOAD, name=f"ffn{l}_up_dw")
    tie = emit(f"ffn{l}", {"ffn_w_up": dw_up, "ffn_w_down": dw_down})
    dh = _mm(da, P[f"ffn_w_up{l}"], tb=True, name=f"ffn{l}_up_dx")
    dx_in, dnw = _rmsnorm_bwd(x, P["norm_ffn"][l:l + 1], dh, dx, name=f"ffn{l}_norm_bwd", after=tie)
    return dx_in, dnw, dcw, dcb


def _local_step(x, positions, target, P, need, emit, after=None):
    S, D = x.shape
    inv_freq = ROPE_THETA ** (-jnp.arange(0, HEAD_DIM, 2, dtype=F32) / HEAD_DIM)
    inv_freq = jnp.tile(inv_freq, LANES // (HEAD_DIM // 2)).reshape(1, LANES)
    cos, sin = _rope_tables(positions, inv_freq)

    nm0 = P["norm_mix"][0:1]
    h0 = _rmsnorm(x, nm0, name="mix_norm", after=after)
    proj0 = _mm(h0, P["mix_w_in"], name="mix_in")
    cat0 = _attn_fwd(proj0, cos, sin, P["attn_sinks"], _pool_fwd(proj0, P["pool_w"][0], P["pool_scale"]))
    need("mix_out", cat0)
    x1 = _mm(cat0, P["mix_w_out"], res=x, name="mix_out")
    x2, ffn0 = _ffn_fwd(x1, P, 0, need)

    nm1 = P["norm_mix"][1:2]
    need("ssm", x2)
    h1 = _rmsnorm(x2, nm1, name="ssm_norm_in")
    z = _mm(h1, P["ssm_wz"], name="ssm_in_z")
    xbcp = _mm(h1, P["ssm_wxbc"], name="ssm_in_xbc")
    dtraw = _mm(h1, P["ssm_wdt"], name="ssm_in_dt")
    xbc = _conv_silu_fwd(xbcp, P["ssm_conv_w"], P["ssm_conv_b"])
    bias_g, alog_g, d_g = _group_major(P["ssm_dt_bias"]), _group_major(P["ssm_A_log"]), _group_major(P["ssm_D"])
    pre_g, dt_g, acs_g, acst_g = _ssd_prep_fwd(dtraw, 0, bias_g, alog_g)
    y, states = _ssd_fwd(xbc, dt_g, acs_g, acst_g, d_g)
    yn = _gate_norm_fwd(y, z, P["ssm_norm"])
    x3 = _mm(yn, P["ssm_w_out"], res=x2, name="ssm_out")
    x4, ffn1 = _ffn_fwd(x3, P, 1, need)

    loss, dx, d_norm_final = _final_loss(x4, P["norm_final"].reshape(1, D), target, name="final_loss")
    dx, dnf1, dcw1, dcb1 = _ffn_bwd(ffn1, P, 1, dx, emit)
    dyn = _mm(dx, P["ssm_w_out"], tb=True, out_dtype=BF16, name="ssm_out_dx")
    d_w_out1 = _mm(yn, dx, ta=True, out_dtype=PAYLOAD, name="ssm_out_dw")
    dy, dz, d_ssm_norm = _gate_norm_bwd(y, z, P["ssm_norm"], dyn)
    dxs, dB, dC, ddt_g, dacs_g, dacst_g, dd_g = _ssd_bwd(xbc, dt_g, acs_g, acst_g, d_g, states, dy)
    draw, dbias_g, dalog_g = _ssd_prep_bwd(pre_g, dt_g, alog_g, ddt_g, dacs_g, dacst_g)
    dxbc, d_conv_w1, d_conv_b1 = _conv_silu_bwd(xbcp, P["ssm_conv_w"], P["ssm_conv_b"], [dxs, dB, dC])
    d_wz = _mm(h1, dz, ta=True, out_dtype=PAYLOAD, name="ssm_in_z_dw")
    d_wxbc = _mm(h1, dxbc, ta=True, out_dtype=PAYLOAD, name="ssm_in_xbc_dw")
    d_wdt = _mm(h1, draw, ta=True, out_dtype=PAYLOAD, name="ssm_in_dt_dw")
    tie = emit("ssm", {"ssm_wz": d_wz, "ssm_wxbc": d_wxbc, "ssm_wdt": d_wdt, "ssm_w_out": d_w_out1,
                       "ssm_conv_w": d_conv_w1, "ssm_conv_b": d_conv_b1, "ssm_norm": d_ssm_norm})
    dh1 = _mm(dz, P["ssm_wz"], tb=True, name="ssm_in_z_dx")
    dh1 = _mm(dxbc, P["ssm_wxbc"], tb=True, res=dh1, name="ssm_in_xbc_dx")
    dh1 = _mm(draw, P["ssm_wdt"], tb=True, res=dh1, name="ssm_in_dt_dx")
    dx, dnm1 = _rmsnorm_bwd(x2, nm1, dh1, dx, name="ssm_norm_in_bwd", after=tie)
    dx, dnf0, dcw0, dcb0 = _ffn_bwd(ffn0, P, 0, dx, emit)
    dcat = _mm(dx, P["mix_w_out"], tb=True, name="mix_out_dx")
    d_w_out0 = _mm(cat0, dx, ta=True, out_dtype=PAYLOAD, name="mix_out_dw")
    dproj0, dsk = _attn_bwd(proj0, cos, sin, P["attn_sinks"], dcat)
    dproj0, d_pool_w, d_pool_scale = _pool_bwd(proj0, P["pool_w"][0], P["pool_scale"], dcat, dproj0)
    d_w_in0 = _mm(h0, dproj0, ta=True, out_dtype=PAYLOAD, name="mix_in_dw")
    tie = emit("mix", {"mix_w_in": d_w_in0, "mix_w_out": d_w_out0, "ffn_conv_w": jnp.stack([dcw0, dcw1])})
    dh0 = _mm(dproj0, P["mix_w_in"], tb=True, name="mix_in_dx")
    grad_x, dnm0 = _rmsnorm_bwd(x, nm0, dh0, dx, name="mix_norm_bwd", after=tie)

    small = {
        "norm_mix": jnp.concatenate([dnm0, dnm1], axis=0),
        "norm_ffn": jnp.concatenate([dnf0, dnf1], axis=0),
        "norm_final": d_norm_final,
        "pool_w": d_pool_w,
        "pool_scale": d_pool_scale,
        "attn_sinks_rows": dsk,
        "ssm_dt_bias_g": dbias_g, "ssm_A_log_g": dalog_g, "ssm_D_g": dd_g,
        "ffn_conv_b": jnp.concatenate([dcb0, dcb1], axis=0),
    }
    return loss, grad_x, small


def _peer(k):
    x, y, c = lax.axis_index("x"), lax.axis_index("y"), lax.axis_index("c")
    px = 1 - x if k & 4 else x
    py = 1 - y if k & 2 else y
    pc = 1 - c if k & 1 else c
    return (px, py, pc), 4 * px + 2 * py + pc


def _my_index():
    return 4 * lax.axis_index("x") + 2 * lax.axis_index("y") + lax.axis_index("c")


def _land_sds(a, mode, gather):
    if mode == "slab":
        return _sds(((N_DEV,) + a.shape) if gather else a.shape, a.dtype)
    assert mode == "rows", mode
    return _sds((N_DEV * a.shape[0],) + a.shape[1:] if gather else (N_DEV, a.shape[0] // N_DEV) + a.shape[1:], a.dtype)


def _part(ref, mode, shape, idx):
    if mode == "slab":
        return ref.at[idx]
    r = shape[0] // N_DEV
    return ref.at[pl.ds(idx * r, r)]


def _own_copies(ops, gather, srcs, lands, sems):
    me = _my_index()
    out = []
    for i, (a, mode) in enumerate(ops):
        s = srcs[i] if gather else _part(srcs[i], mode, a.shape, me)
        d = _part(lands[i], mode, _land_sds(a, mode, gather).shape, me) if gather else lands[i].at[me]
        out.append(pltpu.make_async_copy(s, d, sems.at[i]))
    return out


def _remote_copies(ops, gather, srcs, lands, send_sems, recv_sems):
    me = _my_index()
    n = len(ops)
    out = []
    for k in range(1, N_DEV):
        dev, idx = _peer(k)
        for i, (a, mode) in enumerate(ops):
            s = srcs[i] if gather else _part(srcs[i], mode, a.shape, idx)
            d = _part(lands[i], mode, _land_sds(a, mode, gather).shape, me) if gather else lands[i].at[me]
            out.append(pltpu.make_async_remote_copy(src_ref=s, dst_ref=d, send_sem=send_sems.at[(k - 1) * n + i],
                                                    recv_sem=recv_sems.at[(k - 1) * n + i], device_id=dev,
                                                    device_id_type=pl.DeviceIdType.MESH))
    return out


def _exchange(ops, *, gather, name):
    n = len(ops)

    def body(*refs):
        ins, outs = refs[:n], refs[n:2 * n]
        send_sems, recv_sems, local_sems = refs[2 * n:]
        copies = _own_copies(ops, gather, ins, outs, local_sems) + _remote_copies(ops, gather, ins, outs, send_sems, recv_sems)
        for cp in copies:
            cp.start()
        for cp in copies:
            cp.wait()

    return pl.pallas_call(
        body, name=name, out_shape=[_land_sds(a, m, gather) for a, m in ops], in_specs=[ANY] * n, out_specs=[ANY] * n,
        scratch_shapes=[pltpu.SemaphoreType.DMA((n * (N_DEV - 1),)), pltpu.SemaphoreType.DMA((n * (N_DEV - 1),)),
                        pltpu.SemaphoreType.DMA((n,))],
    )(*[a for a, _ in ops])


HBM = pl.BlockSpec(memory_space=pltpu.HBM)
SEM = pl.BlockSpec(memory_space=pltpu.SEMAPHORE)
SIDE_EFFECT = pltpu.SideEffectType.DATAFLOW_SIDE_EFFECTING


def _in_hbm(a):
    return pltpu.with_memory_space_constraint(a, pltpu.HBM)


def _place_own(ops, *, gather, name):
    n = len(ops)

    def zeros(k):
        return (0,) * k

    in_specs, out_specs = [], []
    for a, mode in ops:
        nd = a.ndim
        if gather and mode == "slab":
            in_specs.append(pl.BlockSpec(a.shape, lambda i, nd=nd: zeros(nd)))
            out_specs.append(pl.BlockSpec((1,) + a.shape, lambda i, nd=nd: (_my_index(),) + zeros(nd)))
        elif gather:
            in_specs.append(pl.BlockSpec(a.shape, lambda i, nd=nd: zeros(nd)))
            out_specs.append(pl.BlockSpec(a.shape, lambda i, nd=nd: (_my_index(),) + zeros(nd - 1)))
        elif mode == "slab":
            in_specs.append(pl.BlockSpec((1,) + a.shape[1:], lambda i, nd=nd: (_my_index(),) + zeros(nd - 1)))
            out_specs.append(pl.BlockSpec((1,) + a.shape[1:], lambda i, nd=nd: (_my_index(),) + zeros(nd - 1)))
        else:
            r = a.shape[0] // N_DEV
            in_specs.append(pl.BlockSpec((r,) + a.shape[1:], lambda i, nd=nd: (_my_index(),) + zeros(nd - 1)))
            out_specs.append(pl.BlockSpec((1, r) + a.shape[1:], lambda i, nd=nd: (_my_index(),) + zeros(nd)))

    def body(*refs):
        for i_ref, o_ref in zip(refs[:n], refs[n:]):
            if o_ref.shape == i_ref.shape:
                o_ref[...] = i_ref[...]
            else:
                o_ref[0] = i_ref[...]

    return _call(body, name=name, out_shape=[_land_sds(a, m, gather) for a, m in ops], grid=(1,),
                 in_specs=in_specs, out_specs=out_specs)(*[a for a, _ in ops])


def _exchange_start(groups, *, gather, name):
    sizes = [len(ops) for ops, _ in groups]
    n = sum(sizes)
    G = len(groups)

    def body(*refs):
        srcs, lands = refs[:n], refs[n:2 * n]
        sems = refs[2 * n:2 * n + 2 * G]
        token = refs[-1]
        off = 0
        for g, (ops, _) in enumerate(groups):
            for cp in _remote_copies(ops, gather, srcs[off:off + sizes[g]], lands[off:off + sizes[g]], sems[2 * g], sems[2 * g + 1]):
                cp.start()
            off += sizes[g]
        token[...] = jnp.zeros_like(token)

    srcs = [a for ops, _ in groups for a, _ in ops]
    lands = [l for _, ls in groups for l in ls]
    sem_shapes = [pltpu.SemaphoreType.DMA((s * (N_DEV - 1),)) for s in sizes for _ in range(2)]
    outs = pl.pallas_call(
        body, name=name,
        out_shape=sem_shapes + [pltpu.HBM(a.shape, a.dtype) for a in srcs + lands] + [_sds((8, LANES))],
        in_specs=[HBM] * (2 * n), out_specs=[SEM] * (2 * G) + [HBM] * (2 * n) + [pl.BlockSpec(memory_space=pltpu.VMEM)],
        input_output_aliases={i: 2 * G + i for i in range(2 * n)},
        compiler_params=pltpu.CompilerParams(has_side_effects=SIDE_EFFECT))(*[_in_hbm(a) for a in srcs + lands])
    sems, thru, token = outs[:2 * G], outs[2 * G:2 * G + 2 * n], outs[-1]
    states, off = [], 0
    for g, s in enumerate(sizes):
        states.append((sems[2 * g], sems[2 * g + 1], thru[off:off + s], thru[n + off:n + off + s]))
        off += s
    return states, token


def _exchange_wait(ops, state, after, *, gather, name):
    send_sems, recv_sems, srcs, lands = state
    n = len(ops)

    def body(*refs):
        for cp in _remote_copies(ops, gather, refs[:n], refs[n:2 * n], refs[2 * n], refs[2 * n + 1]):
            cp.wait_send()
            cp.wait_recv()

    outs = pl.pallas_call(
        body, name=name, out_shape=[pltpu.HBM(a.shape, a.dtype) for a in list(srcs) + list(lands)],
        in_specs=[HBM] * (2 * n) + [SEM, SEM, ANY], out_specs=[HBM] * (2 * n),
        input_output_aliases={i: i for i in range(2 * n)},
        compiler_params=pltpu.CompilerParams(has_side_effects=SIDE_EFFECT))(*srcs, *lands, send_sems, recv_sems, after)
    return outs[n:]


ADAM_ROWS = 256


def _row_tile(R, cap=ADAM_ROWS):
    best = R
    if R > cap:
        for d in range(16, cap + 1, 16):
            if R % d == 0:
                best = d
    return best


def _adamw(g_layers, w, m, v, *, name):
    L = len(g_layers)
    J, R, Wd = g_layers[0].shape
    assert w.shape == (L, R, Wd), (g_layers[0].shape, w.shape)
    tr = _row_tile(R)
    nrt = R // tr
    c1 = 1.0 / (1.0 - ADAM_B1 ** ADAM_STEP)
    c2 = 1.0 / (1.0 - ADAM_B2 ** ADAM_STEP)

    def body(*refs):
        g_refs = refs[:L]
        w_ref, m_ref, v_ref, go_ref, d_ref, mo_ref, vo_ref = refs[L:]
        layer = pl.program_id(0)
        g = None
        for l, g_ref in enumerate(g_refs):
            gl = g_ref[0].astype(F32)
            for j in range(1, J):
                gl = gl + g_ref[j].astype(F32)
            g = gl if g is None else jnp.where(layer == l, gl, g)
        mn = ADAM_B1 * m_ref[...] + (1.0 - ADAM_B1) * g
        vn = ADAM_B2 * v_ref[...] + (1.0 - ADAM_B2) * (g * g)
        go_ref[...] = g
        mo_ref[...] = mn
        vo_ref[...] = vn
        d_ref[...] = -ADAM_LR * ((mn * c1) / (jnp.sqrt(vn * c2) + ADAM_EPS) + ADAM_WD * w_ref[...])

    def g_spec(l):
        return pl.BlockSpec((J, tr, Wd), lambda ll, i: (0, jnp.where(ll == l, i, jnp.where(ll < l, 0, nrt - 1)), 0))

    row = pl.BlockSpec((None, tr, Wd), lambda ll, i: (ll, i, 0))
    out = _sds((L, R, Wd))
    return _call(body, name=name, out_shape=(out, out, out, out), grid=(L, nrt),
                 in_specs=[g_spec(l) for l in range(L)] + [row, row, row], out_specs=(row, row, row, row))(*g_layers, w, m, v)


def _sum_slabs(slabs, *, name):
    n = len(slabs)

    def body(*refs):
        for g_ref, o_ref in zip(refs[:n], refs[n:]):
            g = g_ref[0]
            for j in range(1, g_ref.shape[0]):
                g = g + g_ref[j]
            o_ref[...] = g

    return _call(body, name=name, out_shape=[_sds(s.shape[1:]) for s in slabs])(*slabs)


RELAYOUT_ROWS = 256


def _col_plan(n, segments):
    plan = []
    for j in range(N_DEV):
        lo, hi = j * n, (j + 1) * n
        for t0, t1, oi, o0 in segments:
            a, b = max(lo, t0), min(hi, t1)
            if a < b:
                plan.append((j, a - lo, oi, o0 + a - t0, b - a))
    return plan


def _interleave_segments(F):
    seg = []
    for b in range(F // LANES):
        seg.append((b * LANES, (b + 1) * LANES, 0, 2 * b * LANES))
        seg.append((F + b * LANES, F + (b + 1) * LANES, 0, (2 * b + 1) * LANES))
    return seg


def _interleave_perm(F):
    perm = []
    for b in range(F // LANES):
        perm += list(range(b * LANES, (b + 1) * LANES)) + list(range(F + b * LANES, F + (b + 1) * LANES))
    inv = [0] * (2 * F)
    for d, s in enumerate(perm):
        inv[s] = d
    return jnp.asarray(perm, jnp.int32), jnp.asarray(inv, jnp.int32)


def _cols_from_slabs(slabs, plan, widths, *, name):
    _, R, n = slabs.shape
    tr = _row_tile(R, RELAYOUT_ROWS)
    covered = [sum(e[4] for e in plan if e[2] == i) for i in range(len(widths))]

    def body(s_ref, *o_refs):
        for i, o_ref in enumerate(o_refs):
            if covered[i] < widths[i]:
                o_ref[...] = jnp.zeros_like(o_ref)
        for j, sc, oi, oc, w in plan:
            o_refs[oi][:, oc:oc + w] = s_ref[j, :, sc:sc + w]

    return _call(body, name=name, out_shape=[_sds((R, w), slabs.dtype) for w in widths], grid=(R // tr,),
                 in_specs=[pl.BlockSpec((N_DEV, tr, n), lambda i: (0, i, 0))],
                 out_specs=[pl.BlockSpec((tr, w), lambda i: (i, 0)) for w in widths])(slabs)


def _slabs_from_cols(mats, plan, n, *, name):
    R = mats[0].shape[0]
    tr = _row_tile(R, RELAYOUT_ROWS)

    def body(*refs):
        m_refs, o_ref = refs[:-1], refs[-1]
        for j, sc, oi, oc, w in plan:
            o_ref[j, :, sc:sc + w] = m_refs[oi][:, oc:oc + w]

    return _call(body, name=name, out_shape=_sds((N_DEV, R, n), mats[0].dtype), grid=(R // tr,),
                 in_specs=[pl.BlockSpec((tr, m.shape[1]), lambda i: (i, 0)) for m in mats],
                 out_specs=pl.BlockSpec((N_DEV, tr, n), lambda i: (0, i, 0)))(*mats)


def kernel(x, positions, norm_mix, norm_ffn, norm_final, mix_w_in, pool_w, pool_scale, attn_sinks, mix_w_out, ssm_w_in, ssm_conv_w, ssm_conv_b, ssm_dt_bias, ssm_A_log, ssm_D, ssm_norm, ssm_w_out, ffn_w_up, ffn_conv_w, ffn_conv_b, ffn_w_down, loss_target, m_norm_mix, m_norm_ffn, m_norm_final, m_mix_w_in, m_pool_w, m_pool_scale, m_attn_sinks, m_mix_w_out, m_ssm_w_in, m_ssm_conv_w, m_ssm_conv_b, m_ssm_dt_bias, m_ssm_A_log, m_ssm_D, m_ssm_norm, m_ssm_w_out, m_ffn_w_up, m_ffn_conv_w, m_ffn_conv_b, m_ffn_w_down, v_norm_mix, v_norm_ffn, v_norm_final, v_mix_w_in, v_pool_w, v_pool_scale, v_attn_sinks, v_mix_w_out, v_ssm_w_in, v_ssm_conv_w, v_ssm_conv_b, v_ssm_dt_bias, v_ssm_A_log, v_ssm_D, v_ssm_norm, v_ssm_w_out, v_ffn_w_up, v_ffn_conv_w, v_ffn_conv_b, v_ffn_w_down):
    args = dict(locals())
    wl = {n: args[n] for n in WEIGHTS}
    ml = {n: args["m_" + n] for n in WEIGHTS}
    vl = {n: args["v_" + n] for n in WEIGHTS}
    D = x.shape[2]
    F = ffn_w_down.shape[1] * N_DEV
    DI, CD, NH = ssm_norm.shape[1] * N_DEV, ssm_conv_b.shape[1] * N_DEV, ssm_dt_bias.shape[1]
    Kc, Kf = ssm_conv_w.shape[1], ffn_conv_w.shape[1]
    n_mix, n_ssm, n_up = mix_w_in.shape[2], ssm_w_in.shape[2], ffn_w_up.shape[2]
    plan_mix = _col_plan(n_mix, [(0, N_DEV * n_mix, 0, 0)])
    plan_ssm = _col_plan(n_ssm, [(0, DI, 0, 0), (DI, DI + CD, 1, 0), (DI + CD, DI + CD + NH, 2, 0)])
    plan_up = _col_plan(n_up, _interleave_segments(F))
    perm, inv = _interleave_perm(F)

    def two(a):
        return a.reshape(-1, a.shape[-1])

    def pay(a):
        return a.astype(PAYLOAD)

    order = ("mix_in", "mix_out", "ffn0_up", "ffn0_down", "ssm", "ffn1_up", "ffn1_down")
    gops = {
        "mix_in": [(pay(two(mix_w_in)), "slab")],
        "mix_out": [(pay(mix_w_out[0]), "rows"), (two(ssm_conv_w), "slab"), (ssm_conv_b, "slab"), (ssm_norm, "slab"),
                    (two(ffn_conv_w), "slab")],
        "ffn0_up": [(pay(ffn_w_up[0]), "slab")], "ffn0_down": [(pay(ffn_w_down[0]), "rows")],
        "ssm": [(pay(two(ssm_w_in)), "slab"), (pay(ssm_w_out[0]), "rows")],
        "ffn1_up": [(pay(ffn_w_up[1]), "slab")], "ffn1_down": [(pay(ffn_w_down[1]), "rows")],
    }
    lands = _place_own([op for g in order for op in gops[g]], gather=True, name="gather_own")
    groups, off = [], 0
    for g in order:
        groups.append((gops[g], lands[off:off + len(gops[g])]))
        off += len(gops[g])
    gstates, token = _exchange_start(groups, gather=True, name="gather_start")
    gstate = dict(zip(order, gstates))
    P = {n: wl[n] for n in REPLICATED}
    P["ffn_conv_b"] = jnp.take(ffn_conv_b, perm, axis=1)

    def need(g, after):
        got = _exchange_wait(gops[g], gstate[g], after, gather=True, name="gather_wait_" + g)
        if g == "mix_in":
            (P["mix_w_in"],) = _cols_from_slabs(got[0], plan_mix, (N_DEV * n_mix,), name="unpack_mix_w_in")
        elif g == "mix_out":
            P.update(mix_w_out=got[0], ssm_conv_w=got[1].transpose(1, 0, 2).reshape(Kc, CD), ssm_conv_b=got[2].reshape(1, CD),
                     ssm_norm=got[3].reshape(1, DI),
                     ffn_conv_w=jnp.take(got[4].transpose(1, 0, 2).reshape(2 * Kf, 2 * F), perm, axis=1).reshape(2, Kf, 2 * F))
        elif g == "ssm":
            P["ssm_wz"], P["ssm_wxbc"], P["ssm_wdt"] = _cols_from_slabs(got[0], plan_ssm, (DI, CD, LANES), name="unpack_ssm_w_in")
            P["ssm_w_out"] = got[1]
        elif g.endswith("_up"):
            (P["ffn_w_up" + g[3]],) = _cols_from_slabs(got[0], plan_up, (2 * F,), name="unpack_ffn_w_up" + g[3])
        else:
            P["ffn_w_down" + g[3]] = got[0]

    sent = {}

    def emit(g, d):
        if g == "mix":
            ops = [(_slabs_from_cols([d["mix_w_in"]], plan_mix, n_mix, name="pack_mix_w_in"), "slab"), (d["mix_w_out"], "rows"),
                   (jnp.take(d["ffn_conv_w"].reshape(2 * Kf, 2 * F), inv, axis=1).reshape(2 * Kf, N_DEV, n_up).transpose(1, 0, 2), "slab")]
        elif g == "ssm":
            ops = [(_slabs_from_cols([d["ssm_wz"], d["ssm_wxbc"], d["ssm_wdt"]], plan_ssm, n_ssm, name="pack_ssm_w_in"), "slab"),
                   (d["ssm_w_out"], "rows"), (d["ssm_conv_w"].reshape(Kc, N_DEV, -1).transpose(1, 0, 2), "slab"),
                   (d["ssm_conv_b"].reshape(N_DEV, 1, -1), "slab"), (d["ssm_norm"].reshape(N_DEV, 1, -1), "slab")]
        else:
            ops = [(_slabs_from_cols([d["ffn_w_up"]], plan_up, n_up, name="pack_ffn_w_up" + g[3]), "slab"), (d["ffn_w_down"], "rows")]
        own = _place_own(ops, gather=False, name="scatter_own_" + g)
        (state,), tok = _exchange_start([(ops, own)], gather=False, name="scatter_start_" + g)
        sent[g] = (ops, state)
        return tok

    need("mix_in", token)
    loss_lanes, grad_x, G = _local_step(x[0], positions.reshape(-1, 1), loss_target[0], P, need, emit, after=token)
    loss = lax.psum(loss_lanes[0, 0], ("x", "y", "c"))

    res = {}

    def update(n, g_layers):
        L = len(g_layers)
        g_layers = [g.reshape(g.shape[0], -1, g.shape[-1]) for g in g_layers]
        shape = (L,) + g_layers[0].shape[1:]
        outs = _adamw(g_layers, wl[n].reshape(shape), ml[n].reshape(shape), vl[n].reshape(shape), name="adamw_" + n)
        for kind, a in zip(("grad", "delta", "new_m", "new_v"), outs):
            res[kind, n] = a.reshape(wl[n].shape)

    recv = {g: _exchange_wait(sent[g][0], sent[g][1], grad_x, gather=False, name="scatter_wait_" + g)
            for g in ("ffn1", "ssm", "ffn0", "mix")}
    update("mix_w_in", [recv["mix"][0]])
    update("mix_w_out", [recv["mix"][1]])
    update("ffn_conv_w", [recv["mix"][2]])
    update("ssm_w_in", [recv["ssm"][0]])
    update("ssm_w_out", [recv["ssm"][1]])
    update("ssm_conv_w", [recv["ssm"][2]])
    update("ssm_conv_b", [recv["ssm"][3]])
    update("ssm_norm", [recv["ssm"][4]])
    update("ffn_w_up", [recv["ffn0"][0], recv["ffn1"][0]])
    update("ffn_w_down", [recv["ffn0"][1], recv["ffn1"][1]])

    rep = _exchange(
        [(a, "slab") for a in (G["norm_mix"], G["norm_ffn"], G["norm_final"], G["pool_w"].reshape(-1, LANES), G["pool_scale"],
                               jnp.take(G["ffn_conv_b"], inv, axis=1), G["attn_sinks_rows"],
                               G["ssm_dt_bias_g"].reshape(SSM_G, LANES), G["ssm_A_log_g"].reshape(SSM_G, LANES),
                               G["ssm_D_g"].reshape(SSM_G, LANES))],
        gather=True, name="gather_small_grads")
    for n, r in zip(("norm_mix", "norm_ffn", "norm_final", "pool_w", "pool_scale", "ffn_conv_b"), rep):
        update(n, [r])
    sinks_rows, bias_g, alog_g, d_g = _sum_slabs(rep[6:], name="sum_head_grads")
    update("attn_sinks", [sinks_rows[:, 0].reshape(1, 1, N_HEADS)])
    update("ssm_dt_bias", [_ungroup(bias_g)[None]])
    update("ssm_A_log", [_ungroup(alog_g)[None]])
    update("ssm_D", [_ungroup(d_g)[None]])

    return (loss, grad_x[None], *[res[k, n] for k in ("grad", "delta", "new_m", "new_v") for n in WEIGHTS])
```

```python
import functools
import math

import jax
import jax.numpy as jnp
from jax import lax
from jax.experimental import pallas as pl
from jax.experimental.pallas import tpu as pltpu

F32 = jnp.float32
BF16 = jnp.bfloat16

N_DEV = 8
LANES = 128
HEAD_DIM = 64
N_KV_HEADS = 2
GQ = 4
N_HEADS = N_KV_HEADS * GQ
BLOCK = 128
POOL_GROUPS = 4
ROPE_THETA = 10000.0
SSM_P = 64
SSM_G = 8
SSM_R = 4
SSM_N = 128
SSM_L = 128
NORM_EPS = 1e-6
SSM_NORM_EPS = 1e-5
ADAM_LR, ADAM_B1, ADAM_B2, ADAM_EPS, ADAM_WD, ADAM_STEP = 0.001, 0.9, 0.999, 1e-08, 0.01, 10
VMEM_LIMIT = 56 * 2 ** 20
PAYLOAD = jnp.bfloat16

REPLICATED = ("norm_mix", "norm_ffn", "norm_final", "pool_w", "pool_scale", "attn_sinks",
              "ssm_dt_bias", "ssm_A_log", "ssm_D", "ffn_conv_b")
WEIGHTS = ("norm_mix", "norm_ffn", "norm_final", "mix_w_in", "pool_w", "pool_scale", "attn_sinks", "mix_w_out",
           "ssm_w_in", "ssm_conv_w", "ssm_conv_b", "ssm_dt_bias", "ssm_A_log", "ssm_D", "ssm_norm", "ssm_w_out",
           "ffn_w_up", "ffn_conv_w", "ffn_conv_b", "ffn_w_down")


def _tile(n, cap):
    if n <= cap:
        return n
    best = None
    for d in range(LANES, cap + 1, LANES):
        if n % d == 0:
            best = d
    assert best is not None, (n, cap)
    return best


def _call(body, *, name, out_shape, grid=None, in_specs=None, out_specs=None, scratch=(), aliases=None):
    kw = {}
    if grid is not None:
        kw = dict(grid=grid, in_specs=in_specs, out_specs=out_specs)
    if aliases:
        kw["input_output_aliases"] = aliases
    return pl.pallas_call(
        body, name=name, out_shape=out_shape, scratch_shapes=list(scratch),
        compiler_params=pltpu.CompilerParams(vmem_limit_bytes=VMEM_LIMIT), **kw)


ANY = pl.BlockSpec(memory_space=pl.ANY)


def _sds(shape, dtype=F32):
    return jax.ShapeDtypeStruct(tuple(shape), dtype)


def _sigmoid(x):
    return 1.0 / (1.0 + jnp.exp(-x))


def _shift_dn(x, d, t):
    if d == 0:
        return x
    return jnp.where(t >= d, pltpu.roll(x, d, axis=0), 0.0)


def _shift_up(x, d, t):
    if d == 0:
        return x
    n = x.shape[0]
    return jnp.where(t < n - d, pltpu.roll(x, n - d, axis=0), 0.0)


def _mm(a, b, *, name, ta=False, tb=False, res=None, out_dtype=F32):
    M, K = (a.shape[1], a.shape[0]) if ta else a.shape
    N = b.shape[0] if tb else b.shape[1]
    assert (b.shape[1] if tb else b.shape[0]) == K, (a.shape, b.shape, ta, tb)
    tm, tn, tk = _tile(M, 1408), _tile(N, 1408), _tile(K, 1408)
    nk = K // tk
    dims = (((0 if ta else 1,), (1 if tb else 0,)), ((), ()))

    def body(*refs):
        a_ref, b_ref = refs[:2]
        r_ref = refs[2] if res is not None else None
        o_ref, acc = refs[-2:]
        k = pl.program_id(2)

        @pl.when(k == 0)
        def _():
            acc[...] = jnp.zeros_like(acc)

        acc[...] += lax.dot_general(a_ref[...].astype(BF16), b_ref[...].astype(BF16), dims,
                                    preferred_element_type=F32)

        @pl.when(k == nk - 1)
        def _():
            out = acc[...]
            if res is not None:
                out = out + r_ref[...]
            o_ref[...] = out.astype(out_dtype)

    a_spec = pl.BlockSpec((tk, tm), lambda i, j, k: (k, i)) if ta else pl.BlockSpec((tm, tk), lambda i, j, k: (i, k))
    b_spec = pl.BlockSpec((tn, tk), lambda i, j, k: (j, k)) if tb else pl.BlockSpec((tk, tn), lambda i, j, k: (k, j))
    o_spec = pl.BlockSpec((tm, tn), lambda i, j, k: (i, j))
    ins, specs = [a, b], [a_spec, b_spec]
    if res is not None:
        ins.append(res)
        specs.append(o_spec)
    return _call(body, name=name, out_shape=_sds((M, N), out_dtype), grid=(M // tm, N // tn, nk), in_specs=specs,
                 out_specs=o_spec, scratch=[pltpu.VMEM((tm, tn), F32)])(*ins)


def _rmsnorm(x, w, *, name, eps=NORM_EPS, after=None):
    S, D = x.shape
    tm = _tile(S, 512)
    tie = [] if after is None else [after]

    def body(x_ref, w_ref, *rest):
        o_ref = rest[-1]
        xf = x_ref[...]
        r = lax.rsqrt(jnp.mean(xf * xf, axis=-1, keepdims=True) + eps)
        o_ref[...] = (xf * r * w_ref[...]).astype(BF16)

    return _call(body, name=name, out_shape=_sds((S, D), BF16), grid=(S // tm,),
                 in_specs=[pl.BlockSpec((tm, D), lambda i: (i, 0)), pl.BlockSpec((1, D), lambda i: (0, 0))] + [ANY] * len(tie),
                 out_specs=pl.BlockSpec((tm, D), lambda i: (i, 0)))(x, w, *tie)


def _norm_bwd_math(xf, w, dh, eps):
    r = lax.rsqrt(jnp.mean(xf * xf, axis=-1, keepdims=True) + eps)
    xhat = xf * r
    dxh = dh * w
    dx = r * (dxh - xhat * jnp.mean(dxh * xhat, axis=-1, keepdims=True))
    dw = jnp.sum(dh * xhat, axis=0, keepdims=True)
    return dx, dw


def _rmsnorm_bwd(x, w, dh, dres, *, name, eps=NORM_EPS, after=None):
    S, D = x.shape
    tm = _tile(S, 512)
    tie = [] if after is None else [after]

    def body(x_ref, w_ref, dh_ref, dr_ref, *rest):
        dx_ref, dw_ref = rest[-2:]
        dx, dw = _norm_bwd_math(x_ref[...], w_ref[...], dh_ref[...].astype(F32), eps)
        dx_ref[...] = dr_ref[...] + dx

        @pl.when(pl.program_id(0) == 0)
        def _():
            dw_ref[...] = jnp.zeros_like(dw_ref)

        dw_ref[...] += dw

    row = pl.BlockSpec((tm, D), lambda i: (i, 0))
    vec = pl.BlockSpec((1, D), lambda i: (0, 0))
    return _call(body, name=name, out_shape=(_sds((S, D)), _sds((1, D))), grid=(S // tm,),
                 in_specs=[row, vec, row, row] + [ANY] * len(tie), out_specs=(row, vec))(x, w, dh, dres, *tie)


def _final_loss(x, w, target, *, name):
    S, D = x.shape
    tm = _tile(S, 512)

    def body(x_ref, w_ref, t_ref, loss_ref, dx_ref, dw_ref):
        xf, wv = x_ref[...], w_ref[...]
        r = lax.rsqrt(jnp.mean(xf * xf, axis=-1, keepdims=True) + NORM_EPS)
        err = xf * r * wv - t_ref[...]
        part = 0.5 * jnp.sum(jnp.mean(err * err, axis=-1, keepdims=True), axis=0, keepdims=True)
        dx, dw = _norm_bwd_math(xf, wv, err * (1.0 / D), NORM_EPS)
        dx_ref[...] = dx

        @pl.when(pl.program_id(0) == 0)
        def _():
            dw_ref[...] = jnp.zeros_like(dw_ref)
            loss_ref[...] = jnp.zeros_like(loss_ref)

        dw_ref[...] += dw
        loss_ref[...] += jnp.broadcast_to(part, loss_ref.shape)

    row = pl.BlockSpec((tm, D), lambda i: (i, 0))
    vec = pl.BlockSpec((1, D), lambda i: (0, 0))
    return _call(body, name=name, out_shape=(_sds((1, LANES)), _sds((S, D)), _sds((1, D))), grid=(S // tm,),
                 in_specs=[row, vec, row], out_specs=(pl.BlockSpec((1, LANES), lambda i: (0, 0)), row, vec))(x, w, target)


def _rope_tables(pos, inv_freq):
    S = pos.shape[0]
    tm = _tile(S, 512)

    def body(p_ref, f_ref, c_ref, s_ref):
        ang = p_ref[...].astype(F32) * f_ref[...]
        c_ref[...] = jnp.cos(ang)
        s_ref[...] = jnp.sin(ang)

    blk = pl.BlockSpec((tm, LANES), lambda i: (i, 0))
    return _call(body, name="rope_tables", out_shape=(_sds((S, LANES)), _sds((S, LANES))), grid=(S // tm,),
                 in_specs=[pl.BlockSpec((tm, 1), lambda i: (i, 0)), pl.BlockSpec((1, LANES), lambda i: (0, 0))],
                 out_specs=(blk, blk))(pos, inv_freq)


def _rot_half(t):
    lane = lax.broadcasted_iota(jnp.int32, t.shape, 1)
    lo = (lane % HEAD_DIM) < (HEAD_DIM // 2)
    return jnp.where(lo, -pltpu.roll(t, LANES - HEAD_DIM // 2, axis=1), pltpu.roll(t, HEAD_DIM // 2, axis=1))


def _rope(t, c, s):
    return t * c + _rot_half(t) * s


def _unrope(dy, c, s):
    return dy * c - _rot_half(dy * s)


PD = POOL_GROUPS * LANES
QD = N_HEADS * HEAD_DIM
KD = N_KV_HEADS * HEAD_DIM
assert PD % QD == 0 and (PD + QD) % (2 * KD) == 0 and KD == LANES
def _attn_probs(q, kcat, sink, mask):
    s = lax.dot_general(q.astype(BF16), kcat, (((1,), (1,)), ((), ())), preferred_element_type=F32) * (HEAD_DIM ** -0.5)
    s = jnp.where(mask, s, -jnp.inf)
    m = jnp.maximum(jnp.max(s, axis=1, keepdims=True), sink)
    p = jnp.exp(s - m)
    ps = jnp.exp(sink - m)
    inv = 1.0 / (jnp.sum(p, axis=1, keepdims=True) + ps)
    return p * inv, ps * inv


def _attn_mask(n):
    qi = lax.broadcasted_iota(jnp.int32, (BLOCK, 2 * BLOCK), 0)
    kj = lax.broadcasted_iota(jnp.int32, (BLOCK, 2 * BLOCK), 1)
    rel = qi + BLOCK - kj
    return (rel >= 0) & (rel < BLOCK) & ((n > 0) | (kj >= BLOCK))


def _attn_in_specs(nb):
    def cur(n):
        return jnp.minimum(n, nb - 1)

    def prev(n):
        return jnp.clip(n - 1, 0, nb - 1)

    kvb = (PD + QD) // (2 * KD)
    return [pl.BlockSpec(memory_space=pltpu.SMEM),
            pl.BlockSpec((BLOCK, QD), lambda n: (cur(n), PD // QD)),
            pl.BlockSpec((BLOCK, 2 * KD), lambda n: (cur(n), kvb)),
            pl.BlockSpec((BLOCK, 2 * KD), lambda n: (prev(n), kvb)),
            pl.BlockSpec((BLOCK, LANES), lambda n: (cur(n), 0)), pl.BlockSpec((BLOCK, LANES), lambda n: (cur(n), 0)),
            pl.BlockSpec((BLOCK, LANES), lambda n: (prev(n), 0)), pl.BlockSpec((BLOCK, LANES), lambda n: (prev(n), 0))]


def _attn_keys(kvc_ref, kvp_ref, cc, sc, cp, sp):
    kc = _rope(kvc_ref[:, :KD], cc, sc)
    kp = _rope(kvp_ref[:, :KD], cp, sp)
    vc, vp = kvc_ref[:, KD:], kvp_ref[:, KD:]
    kcat, vcat = [], []
    for kk in range(N_KV_HEADS):
        sl = slice(kk * HEAD_DIM, (kk + 1) * HEAD_DIM)
        kcat.append(jnp.concatenate([kp[:, sl], kc[:, sl]], axis=0).astype(BF16))
        vcat.append(jnp.concatenate([vp[:, sl], vc[:, sl]], axis=0).astype(BF16))
    return kcat, vcat


def _attn_fwd(proj, cos, sin, sinks, cat):
    S = proj.shape[0]
    nb = S // BLOCK

    def body(sink_ref, q_ref, kvc_ref, kvp_ref, cc_ref, sc_ref, cp_ref, sp_ref, cat_ref, o_ref):
        n = pl.program_id(0)
        cc, sc = cc_ref[...], sc_ref[...]
        kcat, vcat = _attn_keys(kvc_ref, kvp_ref, cc, sc, cp_ref[...], sp_ref[...])
        mask = _attn_mask(n)
        for j in range(QD // LANES):
            qr = _rope(q_ref[:, j * LANES:(j + 1) * LANES], cc, sc)
            for e in range(LANES // HEAD_DIM):
                h = j * (LANES // HEAD_DIM) + e
                pn, _ = _attn_probs(qr[:, e * HEAD_DIM:(e + 1) * HEAD_DIM], kcat[h // GQ], sink_ref[0, h], mask)
                o_ref[:, h * HEAD_DIM:(h + 1) * HEAD_DIM] = jnp.dot(pn.astype(BF16), vcat[h // GQ], preferred_element_type=F32)

    return _call(body, name="attn_fwd", out_shape=_sds(cat.shape), grid=(nb,),
                 in_specs=_attn_in_specs(nb) + [ANY], out_specs=pl.BlockSpec((BLOCK, QD), lambda n: (n, PD // QD)),
                 aliases={8: 0})(sinks, proj, proj, proj, cos, sin, cos, sin, cat)


def _attn_bwd(proj, cos, sin, sinks, dcat):
    S = proj.shape[0]
    nb = S // BLOCK
    scale = HEAD_DIM ** -0.5
    per = LANES // HEAD_DIM

    def body(sink_ref, q_ref, kvc_ref, kvp_ref, cc_ref, sc_ref, cp_ref, sp_ref, do_ref, o_ref, ds_ref, hold, carry, part, pair):
        n = pl.program_id(0)

        @pl.when(n == 0)
        def _():
            hold[...] = jnp.zeros_like(hold)
            carry[...] = jnp.zeros_like(carry)
            ds_ref[...] = jnp.zeros_like(ds_ref)

        live = jnp.where(n < nb, 1.0, 0.0)
        cc, sc, cp, sp = cc_ref[...], sc_ref[...], cp_ref[...], sp_ref[...]
        kcat, vcat = _attn_keys(kvc_ref, kvp_ref, cc, sc, cp, sp)
        mask = _attn_mask(n)
        o_ref[:, :PD] = jnp.zeros((BLOCK, PD), F32)
        o_ref[:, PD:PD + QD] = hold[...]
        dk = [jnp.zeros((2 * BLOCK, HEAD_DIM), F32) for _ in range(N_KV_HEADS)]
        dv = [jnp.zeros((2 * BLOCK, HEAD_DIM), F32) for _ in range(N_KV_HEADS)]
        row = lax.broadcasted_iota(jnp.int32, (8, LANES), 0)
        acc = {"dsk": jnp.zeros((8, LANES), F32)}

        def head_pair(j):
            qr = _rope(q_ref[:, j * LANES:(j + 1) * LANES], cc, sc)
            for e in range(per):
                yield
                h = j * per + e
                kk = h // GQ
                qh = qr[:, e * HEAD_DIM:(e + 1) * HEAD_DIM]
                pn, psn = _attn_probs(qh, kcat[kk], sink_ref[0, h], mask)
                yield
                doh = (do_ref[:, h * HEAD_DIM:(h + 1) * HEAD_DIM] * live).astype(BF16)
                dp = lax.dot_general(doh, vcat[kk], NT, preferred_element_type=F32)
                yield
                delta = jnp.sum(pn * dp, axis=1, keepdims=True)
                ds = (pn * (dp - delta) * scale).astype(BF16)
                pair[j, :, e * HEAD_DIM:(e + 1) * HEAD_DIM] = jnp.dot(ds, kcat[kk], preferred_element_type=F32)
                yield
                dk[kk] = dk[kk] + lax.dot_general(ds, qh.astype(BF16), TN, preferred_element_type=F32)
                dv[kk] = dv[kk] + lax.dot_general(pn.astype(BF16), doh, TN, preferred_element_type=F32)
                acc["dsk"] = acc["dsk"] + jnp.where(row == h, -jnp.sum(psn * delta), 0.0)
            yield
            hold[:, j * LANES:(j + 1) * LANES] = _unrope(pair[j], cc, sc)

        _interleave([head_pair(j) for j in range(QD // LANES)])
        dsk = acc["dsk"]
        for kk in range(N_KV_HEADS):
            sl = slice(kk * HEAD_DIM, (kk + 1) * HEAD_DIM)
            sv = slice(KD + kk * HEAD_DIM, KD + (kk + 1) * HEAD_DIM)
            part[0, :, sl] = dk[kk][:BLOCK]
            part[0, :, sv] = dv[kk][:BLOCK]
            part[1, :, sl] = dk[kk][BLOCK:]
            part[1, :, sv] = dv[kk][BLOCK:]
        done = carry[...] + part[0]
        o_ref[:, PD + QD:PD + QD + KD] = _unrope(done[:, :KD], cp, sp)
        o_ref[:, PD + QD + KD:] = done[:, KD:]
        carry[...] = part[1]
        ds_ref[...] += dsk

    return _call(body, name="attn_bwd", out_shape=(_sds((S, PD + QD + 2 * KD)), _sds((8, LANES))), grid=(nb + 1,),
                 in_specs=_attn_in_specs(nb) + [pl.BlockSpec((BLOCK, QD), lambda n: (jnp.minimum(n, nb - 1), PD // QD))],
                 out_specs=(pl.BlockSpec((BLOCK, PD + QD + 2 * KD), lambda n: (jnp.maximum(n - 1, 0), 0)),
                            pl.BlockSpec((8, LANES), lambda n: (0, 0))),
                 scratch=[pltpu.VMEM((BLOCK, QD), F32), pltpu.VMEM((BLOCK, 2 * KD), F32),
                          pltpu.VMEM((2, BLOCK, 2 * KD), F32), pltpu.VMEM((QD // LANES, BLOCK, LANES), F32)])(
                     sinks, proj, proj, proj, cos, sin, cos, sin, dcat)


def _pool_sums(u, g, t, shift):
    s2 = u + shift(u, 1, t)
    s4 = s2 + shift(s2, 2, t)
    s8 = s4 + shift(s4, 4, t)
    s16 = s8 + shift(s8, 8, t)
    return jnp.where(g == 0, s2, jnp.where(g == 1, s4, jnp.where(g == 2, s8, s16)))


def _pool_specs(S):
    col = pl.BlockSpec((S, LANES), lambda g: (0, g))
    wsp = pl.BlockSpec((1, LANES, LANES), lambda g: (g, 0, 0))
    vec = pl.BlockSpec((1, LANES), lambda g: (0, g))
    return col, wsp, vec


def _pool_fwd(proj, pool_w, scale):
    S = proj.shape[0]
    col, wsp, vec = _pool_specs(S)

    def body(u_ref, w_ref, s_ref, o_ref):
        g = pl.program_id(0)
        u = u_ref[...]
        t = lax.broadcasted_iota(jnp.int32, u.shape, 0)
        cnt = jnp.minimum(t + 1, 2 << g).astype(F32)
        pm = _pool_sums(u, g, t, _shift_dn) / cnt - u
        o_ref[...] = jnp.dot(pm.astype(BF16), w_ref[0].astype(BF16), preferred_element_type=F32) * s_ref[...]

    return _call(body, name="pool_fwd", out_shape=_sds((S, PD + QD)), grid=(POOL_GROUPS,),
                 in_specs=[col, wsp, vec], out_specs=col)(proj, pool_w, scale)


def _pool_bwd(proj, pool_w, scale, dcat, dproj):
    S = proj.shape[0]
    col, wsp, vec = _pool_specs(S)

    def body(u_ref, w_ref, s_ref, d_ref, dproj_ref, du_ref, dw_ref, dsc_ref):
        g = pl.program_id(0)
        u = u_ref[...]
        t = lax.broadcasted_iota(jnp.int32, u.shape, 0)
        cnt = jnp.minimum(t + 1, 2 << g).astype(F32)
        pm = (_pool_sums(u, g, t, _shift_dn) / cnt - u).astype(BF16)
        wv = w_ref[0].astype(BF16)
        d = d_ref[...]
        pw = jnp.dot(pm, wv, preferred_element_type=F32)
        dsc_ref[...] = jnp.sum(pw * d, axis=0, keepdims=True)
        dpw = (d * s_ref[...]).astype(BF16)
        dw_ref[0] = lax.dot_general(pm, dpw, (((0,), (0,)), ((), ())), preferred_element_type=F32)
        dpm = lax.dot_general(dpw, wv, (((1,), (1,)), ((), ())), preferred_element_type=F32)
        du_ref[...] = _pool_sums(dpm / cnt, g, t, _shift_up) - dpm

    return _call(body, name="pool_bwd",
                 out_shape=(_sds(dproj.shape), _sds((POOL_GROUPS, LANES, LANES)), _sds((1, POOL_GROUPS * LANES))),
                 grid=(POOL_GROUPS,), in_specs=[col, wsp, vec, col, ANY], out_specs=(col, wsp, vec),
                 aliases={4: 0})(proj, pool_w, scale, dcat, dproj)


def _conv(x, w_ref, b_ref, t):
    K = w_ref.shape[0]
    y = b_ref[...] + jnp.zeros_like(x)
    for k in range(K):
        y = y + w_ref[k:k + 1, :] * _shift_dn(x, K - 1 - k, t)
    return y


def _silu_grad(y):
    sg = _sigmoid(y)
    return sg * (1.0 + y * (1.0 - sg))


CONV_ROWS = 64
HALO = 8


def _win_above(ref, r0):
    if isinstance(r0, int):
        assert r0 == 0
        return jnp.concatenate([jnp.zeros((HALO, ref.shape[1]), F32), ref[0:CONV_ROWS, :]], axis=0)
    return ref[pl.ds(pl.multiple_of(r0 - HALO, HALO), CONV_ROWS + HALO), :]


def _rows_at(win, start):
    if start % 8 == 0:
        return win[start:start + CONV_ROWS]
    base = start // 8 * 8
    return pltpu.roll(win, win.shape[0] - (start - base), axis=0)[base:base + CONV_ROWS]


def _taps_above(win, K):
    return [_rows_at(win, HALO - (K - 1 - k)) for k in range(K)]


def _conv_taps(taps, w, b):
    y = b
    for k in range(len(w)):
        y = y + w[k] * taps[k]
    return y


def _conv_t_win(win, w):
    K = len(w)
    out = None
    for k in range(K):
        d = K - 1 - k
        term = w[k] * _rows_at(win, d)
        out = term if out is None else out + term
    return out


def _fold8(x):
    return jnp.sum(x.reshape(CONV_ROWS // 8, 8, x.shape[-1]), axis=0)


def _chunk_loop(S, step, init):
    carry = step(0, init)
    return lax.fori_loop(1, S // CONV_ROWS, lambda i, c: step(pl.multiple_of(i * CONV_ROWS, CONV_ROWS), c), carry)


def _ffn_mid_specs(S, K, layer):
    return [pl.BlockSpec((S, LANES), lambda j: (0, 2 * j)), pl.BlockSpec((S, LANES), lambda j: (0, 2 * j + 1)),
            pl.BlockSpec((None, K, LANES), lambda j: (layer, 0, 2 * j)), pl.BlockSpec((None, K, LANES), lambda j: (layer, 0, 2 * j + 1)),
            pl.BlockSpec((None, 1, LANES), lambda j: (layer, 0, 2 * j)), pl.BlockSpec((None, 1, LANES), lambda j: (layer, 0, 2 * j + 1))]


def _ffn_mid_fwd(a, cw, cb, layer):
    S, F2 = a.shape
    nf = F2 // (2 * LANES)
    K = cw.shape[1]

    def body(au_ref, ag_ref, wu_ref, wg_ref, bu_ref, bg_ref, o_ref):
        t = lax.broadcasted_iota(jnp.int32, (S, LANES), 0)
        hu = _conv(au_ref[...], wu_ref, bu_ref, t)
        hg = _conv(ag_ref[...], wg_ref, bg_ref, t)
        o_ref[...] = (hg * _sigmoid(hg) * hu).astype(BF16)

    return _call(body, name="ffn_mid_fwd", out_shape=_sds((S, F2 // 2), BF16), grid=(nf,),
                 in_specs=_ffn_mid_specs(S, K, layer), out_specs=pl.BlockSpec((S, LANES), lambda j: (0, j)))(
                     a, a, cw, cw, cb[:, None], cb[:, None])


def _ffn_mid_bwd(a, cw, cb, layer, dact):
    S, F2 = a.shape
    nf = F2 // (2 * LANES)
    K = cw.shape[1]

    def body(au_ref, ag_ref, wu_ref, wg_ref, bu_ref, bg_ref, d_ref, da_ref, dw_ref, db_ref, dhu_s, dhg_s):
        T = CONV_ROWS
        wu = [wu_ref[k:k + 1, :] for k in range(K)]
        wg = [wg_ref[k:k + 1, :] for k in range(K)]
        bu, bg = bu_ref[...], bg_ref[...]
        zero8 = jnp.zeros((HALO, LANES), F32)
        dhu_s[S:S + HALO, :] = zero8
        dhg_s[S:S + HALO, :] = zero8

        def first_pass(r0, acc):
            tu, tg = _taps_above(_win_above(au_ref, r0), K), _taps_above(_win_above(ag_ref, r0), K)
            hu, hg = _conv_taps(tu, wu, bu), _conv_taps(tg, wg, bg)
            d = d_ref[pl.ds(r0, T), :].astype(F32)
            sg = _sigmoid(hg)
            dhu = d * hg * sg
            dhg = d * hu * (sg * (1.0 + hg * (1.0 - sg)))
            dhu_s[pl.ds(r0, T), :] = dhu
            dhg_s[pl.ds(r0, T), :] = dhg
            new = []
            for dh, taps in ((dhu, tu), (dhg, tg)):
                for k in range(K):
                    new.append(acc[len(new)] + _fold8(dh * taps[k]))
            new.append(acc[2 * K] + _fold8(dhu))
            new.append(acc[2 * K + 1] + _fold8(dhg))
            return tuple(new)

        acc = _chunk_loop(S, first_pass, tuple(zero8 for _ in range(2 * K + 2)))
        for k in range(K):
            dw_ref[k:k + 1, :LANES] = jnp.sum(acc[k], axis=0, keepdims=True)
            dw_ref[k:k + 1, LANES:] = jnp.sum(acc[K + k], axis=0, keepdims=True)
        db_ref[:, :LANES] = jnp.sum(acc[2 * K], axis=0, keepdims=True)
        db_ref[:, LANES:] = jnp.sum(acc[2 * K + 1], axis=0, keepdims=True)

        def second_pass(i, carry):
            r0 = pl.multiple_of(i * T, T)
            da_ref[pl.ds(r0, T), :LANES] = _conv_t_win(dhu_s[pl.ds(r0, T + HALO), :], wu).astype(BF16)
            da_ref[pl.ds(r0, T), LANES:] = _conv_t_win(dhg_s[pl.ds(r0, T + HALO), :], wg).astype(BF16)
            return carry

        lax.fori_loop(0, S // T, second_pass, 0)

    return _call(body, name="ffn_mid_bwd", out_shape=(_sds((S, F2), BF16), _sds((K, F2)), _sds((1, F2))), grid=(nf,),
                 in_specs=_ffn_mid_specs(S, K, layer) + [pl.BlockSpec((S, LANES), lambda j: (0, j))],
                 out_specs=(pl.BlockSpec((S, 2 * LANES), lambda j: (0, j)), pl.BlockSpec((K, 2 * LANES), lambda j: (0, j)),
                            pl.BlockSpec((1, 2 * LANES), lambda j: (0, j))),
                 scratch=[pltpu.VMEM((S + HALO, LANES), F32), pltpu.VMEM((S + HALO, LANES), F32)])(
                     a, a, cw, cw, cb[:, None], cb[:, None], dact)


def _conv_silu_fwd(x, cw, cb):
    S = x.shape[0]
    K, C = cw.shape

    def body(x_ref, w_ref, b_ref, o_ref):
        t = lax.broadcasted_iota(jnp.int32, (S, LANES), 0)
        y = _conv(x_ref[...], w_ref, b_ref, t)
        o_ref[...] = y * _sigmoid(y)

    col = pl.BlockSpec((S, LANES), lambda j: (0, j))
    return _call(body, name="conv_silu_fwd", out_shape=_sds((S, C)), grid=(C // LANES,),
                 in_specs=[col, pl.BlockSpec((K, LANES), lambda j: (0, j)), pl.BlockSpec((1, LANES), lambda j: (0, j))],
                 out_specs=col)(x, cw, cb)


def _conv_silu_bwd(x, cw, cb, douts):
    S = x.shape[0]
    K, C = cw.shape
    starts, off = [], 0
    for d in douts:
        starts.append(off)
        off += d.shape[1] // LANES
    assert off == C // LANES

    def body(x_ref, w_ref, b_ref, *rest):
        dy_s = rest[-1]
        d_refs, (dx_ref, dw_ref, db_ref) = rest[:len(douts)], rest[len(douts):-1]
        j = pl.program_id(0)
        T = CONV_ROWS
        w = [w_ref[k:k + 1, :] for k in range(K)]
        b = b_ref[...]
        zero8 = jnp.zeros((HALO, LANES), F32)
        dy_s[S:S + HALO, :] = zero8

        def first_pass(r0, acc):
            taps = _taps_above(_win_above(x_ref, r0), K)
            y = _conv_taps(taps, w, b)
            d = d_refs[0][pl.ds(r0, T), :]
            for i in range(1, len(douts)):
                d = jnp.where(j >= starts[i], d_refs[i][pl.ds(r0, T), :], d)
            dy = d * _silu_grad(y)
            dy_s[pl.ds(r0, T), :] = dy
            return tuple(acc[k] + _fold8(dy * taps[k]) for k in range(K)) + (acc[K] + _fold8(dy),)

        acc = _chunk_loop(S, first_pass, tuple(zero8 for _ in range(K + 1)))
        for k in range(K):
            dw_ref[k:k + 1, :] = jnp.sum(acc[k], axis=0, keepdims=True)
        db_ref[...] = jnp.sum(acc[K], axis=0, keepdims=True)

        def second_pass(i, carry):
            r0 = pl.multiple_of(i * T, T)
            dx_ref[pl.ds(r0, T), :] = _conv_t_win(dy_s[pl.ds(r0, T + HALO), :], w).astype(BF16)
            return carry

        lax.fori_loop(0, S // T, second_pass, 0)

    col = pl.BlockSpec((S, LANES), lambda j: (0, j))
    wsp = pl.BlockSpec((K, LANES), lambda j: (0, j))
    bsp = pl.BlockSpec((1, LANES), lambda j: (0, j))

    def dspec(i):
        nblk = douts[i].shape[1] // LANES
        return pl.BlockSpec((S, LANES), lambda j: (0, jnp.clip(j - starts[i], 0, nblk - 1)))

    return _call(body, name="conv_silu_bwd", out_shape=(_sds((S, C), BF16), _sds((K, C)), _sds((1, C))), grid=(C // LANES,),
                 in_specs=[col, wsp, bsp] + [dspec(i) for i in range(len(douts))],
                 out_specs=(col, wsp, bsp), scratch=[pltpu.VMEM((S + HALO, LANES), F32)])(x, cw, cb, *douts)


HI = lax.Precision.HIGHEST


def _ssd_prep_fwd(proj, col0, bias_g, alog_g):
    S = proj.shape[0]
    nc = S // SSM_L
    b0 = col0 // LANES

    def body(raw_ref, b_ref, al_ref, pre_ref, dt_ref, acs_ref, acst_ref):
        r_i = lax.broadcasted_iota(jnp.int32, (LANES, LANES), 0)
        c_i = lax.broadcasted_iota(jnp.int32, (LANES, LANES), 1)
        live = c_i < SSM_R
        tril = jnp.where(r_i >= c_i, 1.0, 0.0)
        raw = raw_ref[...]
        for g in range(SSM_G):
            sel = jnp.where((r_i == SSM_R * g + c_i) & live, 1.0, 0.0)
            pre = jnp.dot(raw, sel, preferred_element_type=F32, precision=HI) + b_ref[g]
            dt = jnp.where(live, jnp.logaddexp(pre, 0.0), 0.0)
            a = dt * (-jnp.exp(al_ref[g]))
            acs = jnp.dot(tril, a, preferred_element_type=F32, precision=HI)
            pre_ref[g] = pre
            dt_ref[g] = dt
            acs_ref[g] = acs
            acst_ref[g] = acs.T

    gsp = pl.BlockSpec((SSM_G, 1, LANES), lambda c: (0, 0, 0))
    blk = pl.BlockSpec((SSM_G, SSM_L, LANES), lambda c: (0, c, 0))
    big = _sds((SSM_G, S, LANES))
    return _call(body, name="ssd_prep_fwd", out_shape=(big, big, big, _sds((SSM_G, LANES, S))), grid=(nc,),
                 in_specs=[pl.BlockSpec((SSM_L, LANES), lambda c: (c, b0)), gsp, gsp],
                 out_specs=(blk, blk, blk, pl.BlockSpec((SSM_G, LANES, SSM_L), lambda c: (0, 0, c))))(proj, bias_g, alog_g)


def _ssd_prep_bwd(pre_g, dt_g, alog_g, ddt_g, dacs_g, dacst_g):
    S = pre_g.shape[1]
    nc = S // SSM_L

    def body(pre_ref, dt_ref, al_ref, ddt_ref, dacs_ref, dacst_ref, draw_ref, db_ref, dal_ref):
        c = pl.program_id(0)
        r_i = lax.broadcasted_iota(jnp.int32, (LANES, LANES), 0)
        c_i = lax.broadcasted_iota(jnp.int32, (LANES, LANES), 1)
        live = c_i < SSM_R
        triu = jnp.where(r_i <= c_i, 1.0, 0.0)

        @pl.when(c == 0)
        def _():
            db_ref[...] = jnp.zeros_like(db_ref)
            dal_ref[...] = jnp.zeros_like(dal_ref)

        draw = jnp.zeros((SSM_L, LANES), F32)
        for g in range(SSM_G):
            dacs = dacs_ref[g] + dacst_ref[g].T
            da = jnp.dot(triu, dacs, preferred_element_type=F32, precision=HI)
            A = -jnp.exp(al_ref[g])
            ddt = ddt_ref[g] + da * A
            dpre = jnp.where(live, ddt * _sigmoid(pre_ref[g]), 0.0)
            unsel = jnp.where((c_i == SSM_R * g + r_i) & (r_i < SSM_R), 1.0, 0.0)
            draw = draw + jnp.dot(dpre, unsel, preferred_element_type=F32, precision=HI)
            db_ref[g] += jnp.sum(dpre, axis=0, keepdims=True)
            dal_ref[g] += jnp.where(live[:1], jnp.sum(da * dt_ref[g], axis=0, keepdims=True) * A, 0.0)
        draw_ref[...] = draw

    gsp = pl.BlockSpec((SSM_G, 1, LANES), lambda c: (0, 0, 0))
    blk = pl.BlockSpec((SSM_G, SSM_L, LANES), lambda c: (0, c, 0))
    gout = _sds((SSM_G, 1, LANES))
    return _call(body, name="ssd_prep_bwd", out_shape=(_sds((S, LANES)), gout, gout), grid=(nc,),
                 in_specs=[blk, blk, gsp, blk, blk, pl.BlockSpec((SSM_G, LANES, SSM_L), lambda c: (0, 0, c))],
                 out_specs=(pl.BlockSpec((SSM_L, LANES), lambda c: (c, 0)), gsp, gsp))(pre_g, dt_g, alog_g, ddt_g, dacs_g, dacst_g)


NT = (((1,), (1,)), ((), ()))
TN = (((0,), (0,)), ((), ()))
SSM_HP = SSM_R * SSM_P


def _ssd_group_terms(xs_ref, dt_ref, acs_ref, d_ref):
    hid = lax.broadcasted_iota(jnp.int32, (1, SSM_HP), 1) // SSM_P
    rid = lax.broadcasted_iota(jnp.int32, (SSM_HP, 1), 0) // SSM_P

    def widen(cols):
        out = cols[0]
        for r in range(1, SSM_R):
            out = jnp.where(hid == r, cols[r], out)
        return out

    dt_c = [dt_ref[:, r:r + 1] for r in range(SSM_R)]
    acs_c = [acs_ref[:, r:r + 1] for r in range(SSM_R)]
    last = [acs_ref[SSM_L - 1:SSM_L, r:r + 1] for r in range(SSM_R)]
    decay_c = [jnp.exp(last[r] - acs_c[r]) for r in range(SSM_R)]
    cd = [jnp.exp(last[r]) for r in range(SSM_R)]
    cd_rows = cd[0]
    for r in range(1, SSM_R):
        cd_rows = jnp.where(rid == r, cd[r], cd_rows)
    xs = xs_ref[...]
    return (xs, xs * widen(dt_c), widen([jnp.exp(a) for a in acs_c]), widen(decay_c),
            widen([d_ref[:, r:r + 1] for r in range(SSM_R)]), cd_rows, dt_c, decay_c, cd)


def _ssd_lmat(acs_ref, acst_ref, r, tril):
    return jnp.exp(jnp.where(tril, acs_ref[:, r:r + 1] - acst_ref[r:r + 1, :], -jnp.inf))


SSM_GPS = 2


def _ssd_specs(rev, nc):
    def cc(c):
        return nc - 1 - c if rev else c
    xs_blocks = (SSM_G * SSM_HP) // (SSM_GPS * SSM_N)
    xs = pl.BlockSpec((SSM_L, SSM_GPS * SSM_HP), lambda g, c: (cc(c), g))
    bsp = pl.BlockSpec((SSM_L, SSM_GPS * SSM_N), lambda g, c: (cc(c), xs_blocks + g))
    csp = pl.BlockSpec((SSM_L, SSM_GPS * SSM_N), lambda g, c: (cc(c), xs_blocks + SSM_G // SSM_GPS + g))
    sc = pl.BlockSpec((SSM_GPS, SSM_L, LANES), lambda g, c: (g, cc(c), 0))
    sct = pl.BlockSpec((SSM_GPS, LANES, SSM_L), lambda g, c: (g, 0, cc(c)))
    gsp = pl.BlockSpec((SSM_GPS, 1, LANES), lambda g, c: (g, 0, 0))
    st = pl.BlockSpec((None, SSM_GPS, SSM_HP, SSM_N), lambda g, c: (cc(c), g, 0, 0))
    return xs, bsp, csp, sc, sct, gsp, st


def _interleave(gens):
    live = list(gens)
    while live:
        for g in list(live):
            try:
                next(g)
            except StopIteration:
                live.remove(g)


def _rounds(gens):
    live = list(gens)
    while live:
        for g in list(live):
            try:
                next(g)
            except StopIteration:
                live.remove(g)
        yield


def _ssd_group_views(gg, xs_ref, b_ref, c_ref, *per_group):
    return (xs_ref.at[:, gg * SSM_HP:(gg + 1) * SSM_HP], b_ref.at[:, gg * SSM_N:(gg + 1) * SSM_N],
            c_ref.at[:, gg * SSM_N:(gg + 1) * SSM_N]) + tuple(r.at[gg] for r in per_group)


def _ssd_fwd(xbc, dt_g, acs_g, acst_g, d_g):
    S = xbc.shape[0]
    nc = S // SSM_L
    xs_s, b_s, c_s, sc, sct, gsp, st = _ssd_specs(False, nc)

    def body(xs_ref, b_ref, c_ref, dt_ref, acs_ref, acst_ref, d_ref, y_ref, st_ref, state):
        c = pl.program_id(1)

        @pl.when(c == 0)
        def _():
            state[...] = jnp.zeros_like(state)

        tril = lax.broadcasted_iota(jnp.int32, (SSM_L, SSM_L), 0) >= lax.broadcasted_iota(jnp.int32, (SSM_L, SSM_L), 1)
        def group(gg):
            xs_v, b_v, c_v, dt_v, acs_v, acst_v, d_v, st_v, state_v = _ssd_group_views(
                gg, xs_ref, b_ref, c_ref, dt_ref, acs_ref, acst_ref, d_ref, st_ref, state)
            y_v = y_ref.at[:, gg * SSM_HP:(gg + 1) * SSM_HP]
            Bb, Cb = b_v[...].astype(BF16), c_v[...].astype(BF16)
            Gm = lax.dot_general(Cb, Bb, NT, preferred_element_type=F32)
            yield
            xs, X, e_all, decay_all, d_all, cd_rows, _, _, _ = _ssd_group_terms(xs_v, dt_v, acs_v, d_v)
            S_all = state_v[...]
            st_v[...] = S_all
            yield
            yo = lax.dot_general(Cb, S_all.astype(BF16), NT, preferred_element_type=F32)
            new_state = lax.dot_general((X * decay_all).astype(BF16), Bb, TN, preferred_element_type=F32)
            yield
            y_v[...] = e_all * yo + d_all * xs
            state_v[...] = S_all * cd_rows + new_state
            for r in range(SSM_R):
                yield
                sl = slice(r * SSM_P, (r + 1) * SSM_P)
                M = Gm * _ssd_lmat(acs_v, acst_v, r, tril)
                yield
                y_v[:, sl] += jnp.dot(M.astype(BF16), X[:, sl].astype(BF16), preferred_element_type=F32)

        _interleave([group(gg) for gg in range(SSM_GPS)])

    return _call(body, name="ssd_fwd",
                 out_shape=(_sds((S, SSM_G * SSM_HP)), _sds((nc, SSM_G, SSM_HP, SSM_N))),
                 grid=(SSM_G // SSM_GPS, nc), in_specs=[xs_s, b_s, c_s, sc, sc, sct, gsp],
                 out_specs=(xs_s, pl.BlockSpec((None, SSM_GPS, SSM_HP, SSM_N), lambda g, c: (c, g, 0, 0))),
                 scratch=[pltpu.VMEM((SSM_GPS, SSM_HP, SSM_N), F32)])(xbc, xbc, xbc, dt_g, acs_g, acst_g, d_g)


def _ssd_bwd(xbc, dt_g, acs_g, acst_g, d_g, states, dy):
    S = xbc.shape[0]
    nc = S // SSM_L
    xs_s, b_s, c_s, sc, sct, gsp, st = _ssd_specs(True, nc)
    bc_out = pl.BlockSpec((SSM_L, SSM_GPS * SSM_N), lambda g, c: (nc - 1 - c, g))

    def body(xs_ref, b_ref, c_ref, dt_ref, acs_ref, acst_ref, d_ref, st_ref, dy_ref,
             dxs_ref, db_ref, dc_ref, ddt_ref, dacs_ref, dacst_ref, dd_ref, dstate):
        c = pl.program_id(1)

        @pl.when(c == 0)
        def _():
            dstate[...] = jnp.zeros_like(dstate)
            dd_ref[...] = jnp.zeros_like(dd_ref)

        tril = lax.broadcasted_iota(jnp.int32, (SSM_L, SSM_L), 0) >= lax.broadcasted_iota(jnp.int32, (SSM_L, SSM_L), 1)
        lane = lax.broadcasted_iota(jnp.int32, (1, LANES), 1)
        subl = lax.broadcasted_iota(jnp.int32, (LANES, 1), 0)
        last_row = lax.broadcasted_iota(jnp.int32, (SSM_L, 1), 0) == SSM_L - 1
        triu = lax.broadcasted_iota(jnp.int32, (SSM_L, SSM_L), 0) <= lax.broadcasted_iota(jnp.int32, (SSM_L, SSM_L), 1)
        def group(gg):
            xs_v, b_v, c_v, dt_v, acs_v, acst_v, d_v, st_v, ddt_v, dacs_v, dacst_v, dd_v, dstate_v = _ssd_group_views(
                gg, xs_ref, b_ref, c_ref, dt_ref, acs_ref, acst_ref, d_ref, st_ref, ddt_ref, dacs_ref, dacst_ref, dd_ref, dstate)
            dy_v, dxs_v = (r.at[:, gg * SSM_HP:(gg + 1) * SSM_HP] for r in (dy_ref, dxs_ref))
            db_v, dc_v = (r.at[:, gg * SSM_N:(gg + 1) * SSM_N] for r in (db_ref, dc_ref))
            Bb, Cb = b_v[...].astype(BF16), c_v[...].astype(BF16)
            Gm = lax.dot_general(Cb, Bb, NT, preferred_element_type=F32)
            GmT = lax.dot_general(Bb, Cb, NT, preferred_element_type=F32)
            yield
            xs, X, e_all, decay_all, d_all, cd_rows, dt_c, decay_c, cd = _ssd_group_terms(xs_v, dt_v, acs_v, d_v)
            S_all, dSn_all, dY = st_v[...], dstate_v[...], dy_v[...]
            Sb, dSnb = S_all.astype(BF16), dSn_all.astype(BF16)
            yield
            T = lax.dot_general(Cb, Sb, NT, preferred_element_type=F32)
            dT = (dY * e_all).astype(BF16)
            dC = jnp.dot(dT, Sb, preferred_element_type=F32)
            dS_prev = lax.dot_general(dT, Cb, TN, preferred_element_type=F32)
            yield
            yo_dy = dY * (e_all * T)
            W = lax.dot_general(Bb, dSnb, NT, preferred_element_type=F32)
            dB = jnp.dot((X * decay_all).astype(BF16), dSnb, preferred_element_type=F32)
            yield
            xw = X * W
            dcd_rows = jnp.sum(dSn_all * S_all, axis=1, keepdims=True)
            dstate_v[...] = dS_prev + dSn_all * cd_rows
            dX_state = W * decay_all
            yield
            acc = dict(dG=jnp.zeros((SSM_L, SSM_L), F32), dGT=jnp.zeros((SSM_L, SSM_L), F32),
                       ddt=jnp.zeros((SSM_L, LANES), F32), dacs=jnp.zeros((SSM_L, LANES), F32),
                       dacst=jnp.zeros((LANES, SSM_L), F32), dd=jnp.zeros((1, LANES), F32))

            def head(r):
                sl = slice(r * SSM_P, (r + 1) * SSM_P)
                Lm = _ssd_lmat(acs_v, acst_v, r, tril)
                LmT = jnp.exp(jnp.where(triu, acst_v[r:r + 1, :] - acs_v[:, r:r + 1], -jnp.inf))
                M = Gm * Lm
                yield
                dYh, xs_h = dY[:, sl], xs[:, sl]
                dYb, Xb = dYh.astype(BF16), X[:, sl].astype(BF16)
                dM = lax.dot_general(dYb, Xb, NT, preferred_element_type=F32)
                yield
                dX = jnp.dot((GmT * LmT).astype(BF16), dYb, preferred_element_type=F32) + dX_state[:, sl]
                acc["dG"] = acc["dG"] + dM * Lm
                acc["dGT"] = acc["dGT"] + lax.dot_general(Xb, dYb, NT, preferred_element_type=F32) * LmT
                yield
                dseg = dM * M
                dd = jnp.sum(xw[:, sl], axis=1, keepdims=True) * decay_c[r]
                dcd = jnp.sum(dcd_rows[sl])
                dacs_col = (jnp.sum(dseg, axis=1, keepdims=True) + jnp.sum(yo_dy[:, sl], axis=1, keepdims=True) - dd
                            + jnp.where(last_row, dcd * cd[r] + jnp.sum(dd), 0.0))
                dacs_row = -jnp.sum(dseg, axis=0, keepdims=True)
                yield
                dxs_v[:, sl] = dX * dt_c[r] + d_all[:, sl] * dYh
                acc["ddt"] = acc["ddt"] + jnp.where(lane == r, jnp.sum(dX * xs_h, axis=1, keepdims=True), 0.0)
                acc["dacs"] = acc["dacs"] + jnp.where(lane == r, dacs_col, 0.0)
                acc["dacst"] = acc["dacst"] + jnp.where(subl == r, dacs_row, 0.0)
                acc["dd"] = acc["dd"] + jnp.where(lane == r, jnp.sum(dYh * xs_h), 0.0)

            yield from _rounds([head(r) for r in range(SSM_R)])
            dc_v[...] = dC + jnp.dot(acc["dG"].astype(BF16), Bb, preferred_element_type=F32)
            db_v[...] = dB + jnp.dot(acc["dGT"].astype(BF16), Cb, preferred_element_type=F32)
            ddt_v[...] = acc["ddt"]
            dacs_v[...] = acc["dacs"]
            dacst_v[...] = acc["dacst"]
            dd_v[...] += acc["dd"]

        _interleave([group(gg) for gg in range(SSM_GPS)])

    big = _sds((SSM_G, S, LANES))
    return _call(body, name="ssd_bwd",
                 out_shape=(_sds((S, SSM_G * SSM_HP)), _sds((S, SSM_G * SSM_N)), _sds((S, SSM_G * SSM_N)),
                            big, big, _sds((SSM_G, LANES, S)), _sds((SSM_G, 1, LANES))),
                 grid=(SSM_G // SSM_GPS, nc), in_specs=[xs_s, b_s, c_s, sc, sc, sct, gsp, st, xs_s],
                 out_specs=(xs_s, bc_out, bc_out, sc, sc, sct, gsp),
                 scratch=[pltpu.VMEM((SSM_GPS, SSM_HP, SSM_N), F32)])(xbc, xbc, xbc, dt_g, acs_g, acst_g, d_g, states, dy)


def _gate_norm_fwd(y, proj, w):
    S, DI = y.shape
    tm = _tile(S, 256)

    def body(y_ref, z_ref, w_ref, o_ref):
        z = z_ref[...]
        gn = y_ref[...] * (z * _sigmoid(z))
        r = lax.rsqrt(jnp.mean(gn * gn, axis=-1, keepdims=True) + SSM_NORM_EPS)
        o_ref[...] = (gn * r * w_ref[...]).astype(BF16)

    row = pl.BlockSpec((tm, DI), lambda i: (i, 0))
    return _call(body, name="gate_norm_fwd", out_shape=_sds((S, DI), BF16), grid=(S // tm,),
                 in_specs=[row, row, pl.BlockSpec((1, DI), lambda i: (0, 0))], out_specs=row)(y, proj, w)


def _gate_norm_bwd(y, proj, w, dout):
    S, DI = y.shape
    tm = _tile(S, 256)

    def body(y_ref, z_ref, w_ref, d_ref, dy_ref, dz_ref, dw_ref):
        z, yv = z_ref[...], y_ref[...]
        sz = z * _sigmoid(z)
        dgn, dw = _norm_bwd_math(yv * sz, w_ref[...], d_ref[...].astype(F32), SSM_NORM_EPS)
        dy_ref[...] = dgn * sz
        dz_ref[...] = (dgn * yv * _silu_grad(z)).astype(BF16)

        @pl.when(pl.program_id(0) == 0)
        def _():
            dw_ref[...] = jnp.zeros_like(dw_ref)

        dw_ref[...] += dw

    row = pl.BlockSpec((tm, DI), lambda i: (i, 0))
    vec = pl.BlockSpec((1, DI), lambda i: (0, 0))
    return _call(body, name="gate_norm_bwd", out_shape=(_sds((S, DI)), _sds((S, DI), BF16), _sds((1, DI))), grid=(S // tm,),
                 in_specs=[row, row, vec, row], out_specs=(row, row, vec))(y, proj, w, dout)


def _group_major(v):
    return jnp.pad(v.reshape(SSM_G, 1, SSM_R), ((0, 0), (0, 0), (0, LANES - SSM_R)))


def _ungroup(t):
    return t[:, :SSM_R].reshape(1, SSM_G * SSM_R)


def _ffn_fwd(x, P, l, need):
    need(f"ffn{l}_up", x)
    h = _rmsnorm(x, P["norm_ffn"][l:l + 1], name=f"ffn{l}_norm")
    a = _mm(h, P[f"ffn_w_up{l}"], name=f"ffn{l}_up")
    need(f"ffn{l}_down", a)
    act = _ffn_mid_fwd(a, P["ffn_conv_w"], P["ffn_conv_b"], l)
    out = _mm(act, P[f"ffn_w_down{l}"], res=x, name=f"ffn{l}_down")
    return out, (x, h, a, act)


def _ffn_bwd(saved, P, l, dx, emit):
    x, h, a, act = saved
    dact = _mm(dx, P[f"ffn_w_down{l}"], tb=True, out_dtype=BF16, name=f"ffn{l}_down_dx")
    dw_down = _mm(act, dx, ta=True, out_dtype=PAYLOAD, name=f"ffn{l}_down_dw")
    da, dcw, dcb = _ffn_mid_bwd(a, P["ffn_conv_w"], P["ffn_conv_b"], l, dact)
    dw_up = _mm(h, da, ta=True, out_dtype=PAYLOAD, name=f"ffn{l}_up_dw")
    tie = emit(f"ffn{l}", {"ffn_w_up": dw_up, "ffn_w_down": dw_down})
    dh = _mm(da, P[f"ffn_w_up{l}"], tb=True, name=f"ffn{l}_up_dx")
    dx_in, dnw = _rmsnorm_bwd(x, P["norm_ffn"][l:l + 1], dh, dx, name=f"ffn{l}_norm_bwd", after=tie)
    return dx_in, dnw, dcw, dcb


def _local_step(x, positions, target, P, need, emit, after=None):
    S, D = x.shape
    inv_freq = ROPE_THETA ** (-jnp.arange(0, HEAD_DIM, 2, dtype=F32) / HEAD_DIM)
    inv_freq = jnp.tile(inv_freq, LANES // (HEAD_DIM // 2)).reshape(1, LANES)
    cos, sin = _rope_tables(positions, inv_freq)

    nm0 = P["norm_mix"][0:1]
    h0 = _rmsnorm(x, nm0, name="mix_norm", after=after)
    need("mix_in", h0)
    proj0 = _mm(h0, P["mix_w_in"], name="mix_in")
    cat0 = _attn_fwd(proj0, cos, sin, P["attn_sinks"], _pool_fwd(proj0, P["pool_w"][0], P["pool_scale"]))
    need("mix_out", cat0)
    x1 = _mm(cat0, P["mix_w_out"], res=x, name="mix_out")
    x2, ffn0 = _ffn_fwd(x1, P, 0, need)

    nm1 = P["norm_mix"][1:2]
    need("ssm", x2)
    h1 = _rmsnorm(x2, nm1, name="ssm_norm_in")
    z = _mm(h1, P["ssm_wz"], name="ssm_in_z")
    xbcp = _mm(h1, P["ssm_wxbc"], name="ssm_in_xbc")
    dtraw = _mm(h1, P["ssm_wdt"], name="ssm_in_dt")
    xbc = _conv_silu_fwd(xbcp, P["ssm_conv_w"], P["ssm_conv_b"])
    bias_g, alog_g, d_g = _group_major(P["ssm_dt_bias"]), _group_major(P["ssm_A_log"]), _group_major(P["ssm_D"])
    pre_g, dt_g, acs_g, acst_g = _ssd_prep_fwd(dtraw, 0, bias_g, alog_g)
    y, states = _ssd_fwd(xbc, dt_g, acs_g, acst_g, d_g)
    yn = _gate_norm_fwd(y, z, P["ssm_norm"])
    x3 = _mm(yn, P["ssm_w_out"], res=x2, name="ssm_out")
    x4, ffn1 = _ffn_fwd(x3, P, 1, need)

    loss, dx, d_norm_final = _final_loss(x4, P["norm_final"].reshape(1, D), target, name="final_loss")
    dx, dnf1, dcw1, dcb1 = _ffn_bwd(ffn1, P, 1, dx, emit)
    dyn = _mm(dx, P["ssm_w_out"], tb=True, out_dtype=BF16, name="ssm_out_dx")
    d_w_out1 = _mm(yn, dx, ta=True, out_dtype=PAYLOAD, name="ssm_out_dw")
    dy, dz, d_ssm_norm = _gate_norm_bwd(y, z, P["ssm_norm"], dyn)
    dxs, dB, dC, ddt_g, dacs_g, dacst_g, dd_g = _ssd_bwd(xbc, dt_g, acs_g, acst_g, d_g, states, dy)
    draw, dbias_g, dalog_g = _ssd_prep_bwd(pre_g, dt_g, alog_g, ddt_g, dacs_g, dacst_g)
    dxbc, d_conv_w1, d_conv_b1 = _conv_silu_bwd(xbcp, P["ssm_conv_w"], P["ssm_conv_b"], [dxs, dB, dC])
    d_wz = _mm(h1, dz, ta=True, out_dtype=PAYLOAD, name="ssm_in_z_dw")
    d_wxbc = _mm(h1, dxbc, ta=True, out_dtype=PAYLOAD, name="ssm_in_xbc_dw")
    d_wdt = _mm(h1, draw, ta=True, out_dtype=PAYLOAD, name="ssm_in_dt_dw")
    tie = emit("ssm", {"ssm_wz": d_wz, "ssm_wxbc": d_wxbc, "ssm_wdt": d_wdt, "ssm_w_out": d_w_out1,
                       "ssm_conv_w": d_conv_w1, "ssm_conv_b": d_conv_b1, "ssm_norm": d_ssm_norm})
    dh1 = _mm(dz, P["ssm_wz"], tb=True, name="ssm_in_z_dx")
    dh1 = _mm(dxbc, P["ssm_wxbc"], tb=True, res=dh1, name="ssm_in_xbc_dx")
    dh1 = _mm(draw, P["ssm_wdt"], tb=True, res=dh1, name="ssm_in_dt_dx")
    dx, dnm1 = _rmsnorm_bwd(x2, nm1, dh1, dx, name="ssm_norm_in_bwd", after=tie)
    dx, dnf0, dcw0, dcb0 = _ffn_bwd(ffn0, P, 0, dx, emit)
    dcat = _mm(dx, P["mix_w_out"], tb=True, name="mix_out_dx")
    d_w_out0 = _mm(cat0, dx, ta=True, out_dtype=PAYLOAD, name="mix_out_dw")
    dproj0, dsk = _attn_bwd(proj0, cos, sin, P["attn_sinks"], dcat)
    dproj0, d_pool_w, d_pool_scale = _pool_bwd(proj0, P["pool_w"][0], P["pool_scale"], dcat, dproj0)
    d_w_in0 = _mm(h0, dproj0, ta=True, out_dtype=PAYLOAD, name="mix_in_dw")
    tie = emit("mix", {"mix_w_in": d_w_in0, "mix_w_out": d_w_out0, "ffn_conv_w": jnp.stack([dcw0, dcw1])})
    dh0 = _mm(dproj0, P["mix_w_in"], tb=True, name="mix_in_dx")
    grad_x, dnm0 = _rmsnorm_bwd(x, nm0, dh0, dx, name="mix_norm_bwd", after=tie)

    small = {
        "norm_mix": jnp.concatenate([dnm0, dnm1], axis=0),
        "norm_ffn": jnp.concatenate([dnf0, dnf1], axis=0),
        "norm_final": d_norm_final,
        "pool_w": d_pool_w,
        "pool_scale": d_pool_scale,
        "attn_sinks_rows": dsk,
        "ssm_dt_bias_g": dbias_g, "ssm_A_log_g": dalog_g, "ssm_D_g": dd_g,
        "ffn_conv_b": jnp.concatenate([dcb0, dcb1], axis=0),
    }
    return loss, grad_x, small


def _peer(k):
    x, y, c = lax.axis_index("x"), lax.axis_index("y"), lax.axis_index("c")
    px = 1 - x if k & 4 else x
    py = 1 - y if k & 2 else y
    pc = 1 - c if k & 1 else c
    return (px, py, pc), 4 * px + 2 * py + pc


def _my_index():
    return 4 * lax.axis_index("x") + 2 * lax.axis_index("y") + lax.axis_index("c")


def _land_sds(a, mode, gather):
    if mode == "slab":
        return _sds(((N_DEV,) + a.shape) if gather else a.shape, a.dtype)
    assert mode == "rows", mode
    return _sds((N_DEV * a.shape[0],) + a.shape[1:] if gather else (N_DEV, a.shape[0] // N_DEV) + a.shape[1:], a.dtype)


def _part(ref, mode, shape, idx):
    if mode == "slab":
        return ref.at[idx]
    r = shape[0] // N_DEV
    return ref.at[pl.ds(idx * r, r)]


def _own_copies(ops, gather, srcs, lands, sems):
    me = _my_index()
    out = []
    for i, (a, mode) in enumerate(ops):
        s = srcs[i] if gather else _part(srcs[i], mode, a.shape, me)
        d = _part(lands[i], mode, _land_sds(a, mode, gather).shape, me) if gather else lands[i].at[me]
        out.append(pltpu.make_async_copy(s, d, sems.at[i]))
    return out


def _remote_copies(ops, gather, srcs, lands, send_sems, recv_sems):
    me = _my_index()
    n = len(ops)
    out = []
    for k in range(1, N_DEV):
        dev, idx = _peer(k)
        for i, (a, mode) in enumerate(ops):
            s = srcs[i] if gather else _part(srcs[i], mode, a.shape, idx)
            d = _part(lands[i], mode, _land_sds(a, mode, gather).shape, me) if gather else lands[i].at[me]
            out.append(pltpu.make_async_remote_copy(src_ref=s, dst_ref=d, send_sem=send_sems.at[(k - 1) * n + i],
                                                    recv_sem=recv_sems.at[(k - 1) * n + i], device_id=dev,
                                                    device_id_type=pl.DeviceIdType.MESH))
    return out


def _exchange(ops, *, gather, name):
    n = len(ops)

    def body(*refs):
        ins, outs = refs[:n], refs[n:2 * n]
        send_sems, recv_sems, local_sems = refs[2 * n:]
        copies = _own_copies(ops, gather, ins, outs, local_sems) + _remote_copies(ops, gather, ins, outs, send_sems, recv_sems)
        for cp in copies:
            cp.start()
        for cp in copies:
            cp.wait()

    return pl.pallas_call(
        body, name=name, out_shape=[_land_sds(a, m, gather) for a, m in ops], in_specs=[ANY] * n, out_specs=[ANY] * n,
        scratch_shapes=[pltpu.SemaphoreType.DMA((n * (N_DEV - 1),)), pltpu.SemaphoreType.DMA((n * (N_DEV - 1),)),
                        pltpu.SemaphoreType.DMA((n,))],
    )(*[a for a, _ in ops])


HBM = pl.BlockSpec(memory_space=pltpu.HBM)
SEM = pl.BlockSpec(memory_space=pltpu.SEMAPHORE)
SIDE_EFFECT = pltpu.SideEffectType.DATAFLOW_SIDE_EFFECTING


def _in_hbm(a):
    return pltpu.with_memory_space_constraint(a, pltpu.HBM)


def _place_own(ops, *, gather, name):
    n = len(ops)

    def zeros(k):
        return (0,) * k

    in_specs, out_specs = [], []
    for a, mode in ops:
        nd = a.ndim
        if gather and mode == "slab":
            in_specs.append(pl.BlockSpec(a.shape, lambda i, nd=nd: zeros(nd)))
            out_specs.append(pl.BlockSpec((1,) + a.shape, lambda i, nd=nd: (_my_index(),) + zeros(nd)))
        elif gather:
            in_specs.append(pl.BlockSpec(a.shape, lambda i, nd=nd: zeros(nd)))
            out_specs.append(pl.BlockSpec(a.shape, lambda i, nd=nd: (_my_index(),) + zeros(nd - 1)))
        elif mode == "slab":
            in_specs.append(pl.BlockSpec((1,) + a.shape[1:], lambda i, nd=nd: (_my_index(),) + zeros(nd - 1)))
            out_specs.append(pl.BlockSpec((1,) + a.shape[1:], lambda i, nd=nd: (_my_index(),) + zeros(nd - 1)))
        else:
            r = a.shape[0] // N_DEV
            in_specs.append(pl.BlockSpec((r,) + a.shape[1:], lambda i, nd=nd: (_my_index(),) + zeros(nd - 1)))
            out_specs.append(pl.BlockSpec((1, r) + a.shape[1:], lambda i, nd=nd: (_my_index(),) + zeros(nd)))

    def body(*refs):
        for i_ref, o_ref in zip(refs[:n], refs[n:2 * n]):
            if o_ref.shape == i_ref.shape:
                o_ref[...] = i_ref[...]
            else:
                o_ref[0] = i_ref[...]

    outs = _call(body, name=name, grid=(1,), in_specs=in_specs, out_specs=out_specs + [ANY] * n,
                 out_shape=[_land_sds(a, m, gather) for a, m in ops] + [_sds(a.shape, a.dtype) for a, _ in ops],
                 aliases={i: n + i for i in range(n)})(*[a for a, _ in ops])
    return outs[:n], [(src, m) for src, (_, m) in zip(outs[n:], ops)]


def _exchange_start(groups, *, gather, name):
    sizes = [len(ops) for ops, _ in groups]
    n = sum(sizes)
    G = len(groups)

    def body(*refs):
        srcs, lands = refs[:n], refs[n:2 * n]
        sems = refs[2 * n:2 * n + 2 * G]
        token = refs[-1]
        off = 0
        for g, (ops, _) in enumerate(groups):
            for cp in _remote_copies(ops, gather, srcs[off:off + sizes[g]], lands[off:off + sizes[g]], sems[2 * g], sems[2 * g + 1]):
                cp.start()
            off += sizes[g]
        token[...] = jnp.zeros_like(token)

    srcs = [a for ops, _ in groups for a, _ in ops]
    lands = [l for _, ls in groups for l in ls]
    sem_shapes = [pltpu.SemaphoreType.DMA((s * (N_DEV - 1),)) for s in sizes for _ in range(2)]
    outs = pl.pallas_call(
        body, name=name,
        out_shape=sem_shapes + [pltpu.HBM(a.shape, a.dtype) for a in srcs + lands] + [_sds((8, LANES))],
        in_specs=[HBM] * (2 * n), out_specs=[SEM] * (2 * G) + [HBM] * (2 * n) + [pl.BlockSpec(memory_space=pltpu.VMEM)],
        input_output_aliases={i: 2 * G + i for i in range(2 * n)},
        compiler_params=pltpu.CompilerParams(has_side_effects=SIDE_EFFECT))(*[_in_hbm(a) for a in srcs + lands])
    sems, thru, token = outs[:2 * G], outs[2 * G:2 * G + 2 * n], outs[-1]
    states, off = [], 0
    for g, s in enumerate(sizes):
        states.append((sems[2 * g], sems[2 * g + 1], thru[off:off + s], thru[n + off:n + off + s]))
        off += s
    return states, token


def _exchange_wait(ops, state, after, *, gather, name):
    send_sems, recv_sems, srcs, lands = state
    n = len(ops)

    def body(*refs):
        for cp in _remote_copies(ops, gather, refs[:n], refs[n:2 * n], refs[2 * n], refs[2 * n + 1]):
            cp.wait_send()
            cp.wait_recv()

    outs = pl.pallas_call(
        body, name=name, out_shape=[pltpu.HBM(a.shape, a.dtype) for a in list(srcs) + list(lands)],
        in_specs=[HBM] * (2 * n) + [SEM, SEM, ANY], out_specs=[HBM] * (2 * n),
        input_output_aliases={i: i for i in range(2 * n)},
        compiler_params=pltpu.CompilerParams(has_side_effects=SIDE_EFFECT))(*srcs, *lands, send_sems, recv_sems, after)
    return outs[n:]


ADAM_ROWS = 256


def _row_tile(R, cap=ADAM_ROWS):
    best = R
    if R > cap:
        for d in range(16, cap + 1, 16):
            if R % d == 0:
                best = d
    return best


def _adamw(g_layers, w, m, v, *, name):
    L = len(g_layers)
    J, R, Wd = g_layers[0].shape
    assert w.shape == (L, R, Wd), (g_layers[0].shape, w.shape)
    tr = _row_tile(R)
    nrt = R // tr
    c1 = 1.0 / (1.0 - ADAM_B1 ** ADAM_STEP)
    c2 = 1.0 / (1.0 - ADAM_B2 ** ADAM_STEP)

    def body(*refs):
        g_refs = refs[:L]
        w_ref, m_ref, v_ref, go_ref, d_ref, mo_ref, vo_ref = refs[L:]
        layer = pl.program_id(0)
        g = None
        for l, g_ref in enumerate(g_refs):
            gl = g_ref[0].astype(F32)
            for j in range(1, J):
                gl = gl + g_ref[j].astype(F32)
            g = gl if g is None else jnp.where(layer == l, gl, g)
        mn = ADAM_B1 * m_ref[...] + (1.0 - ADAM_B1) * g
        vn = ADAM_B2 * v_ref[...] + (1.0 - ADAM_B2) * (g * g)
        go_ref[...] = g
        mo_ref[...] = mn
        vo_ref[...] = vn
        d_ref[...] = -ADAM_LR * ((mn * c1) / (jnp.sqrt(vn * c2) + ADAM_EPS) + ADAM_WD * w_ref[...])

    def g_spec(l):
        return pl.BlockSpec((J, tr, Wd), lambda ll, i: (0, jnp.where(ll == l, i, jnp.where(ll < l, 0, nrt - 1)), 0))

    row = pl.BlockSpec((None, tr, Wd), lambda ll, i: (ll, i, 0))
    out = _sds((L, R, Wd))
    return _call(body, name=name, out_shape=(out, out, out, out), grid=(L, nrt),
                 in_specs=[g_spec(l) for l in range(L)] + [row, row, row], out_specs=(row, row, row, row))(*g_layers, w, m, v)


def _sum_slabs(slabs, *, name):
    n = len(slabs)

    def body(*refs):
        for g_ref, o_ref in zip(refs[:n], refs[n:]):
            g = g_ref[0]
            for j in range(1, g_ref.shape[0]):
                g = g + g_ref[j]
            o_ref[...] = g

    return _call(body, name=name, out_shape=[_sds(s.shape[1:]) for s in slabs])(*slabs)


RELAYOUT_ROWS = 256


def _col_plan(n, segments):
    plan = []
    for j in range(N_DEV):
        lo, hi = j * n, (j + 1) * n
        for t0, t1, oi, o0 in segments:
            a, b = max(lo, t0), min(hi, t1)
            if a < b:
                plan.append((j, a - lo, oi, o0 + a - t0, b - a))
    return plan


def _interleave_segments(F):
    seg = []
    for b in range(F // LANES):
        seg.append((b * LANES, (b + 1) * LANES, 0, 2 * b * LANES))
        seg.append((F + b * LANES, F + (b + 1) * LANES, 0, (2 * b + 1) * LANES))
    return seg


def _to_interleaved(a):
    lead, F2 = a.shape[:-1], a.shape[-1]
    return a.reshape(lead + (2, F2 // (2 * LANES), LANES)).swapaxes(-3, -2).reshape(lead + (F2,))


def _from_interleaved(a):
    lead, F2 = a.shape[:-1], a.shape[-1]
    return a.reshape(lead + (F2 // (2 * LANES), 2, LANES)).swapaxes(-3, -2).reshape(lead + (F2,))


def _cols_from_slabs(slabs, plan, widths, *, name):
    _, R, n = slabs.shape
    tr = _row_tile(R, RELAYOUT_ROWS)
    covered = [sum(e[4] for e in plan if e[2] == i) for i in range(len(widths))]

    def body(s_ref, *o_refs):
        for i, o_ref in enumerate(o_refs):
            if covered[i] < widths[i]:
                o_ref[...] = jnp.zeros_like(o_ref)
        for j, sc, oi, oc, w in plan:
            o_refs[oi][:, oc:oc + w] = s_ref[j, :, sc:sc + w]

    return _call(body, name=name, out_shape=[_sds((R, w), slabs.dtype) for w in widths], grid=(R // tr,),
                 in_specs=[pl.BlockSpec((N_DEV, tr, n), lambda i: (0, i, 0))],
                 out_specs=[pl.BlockSpec((tr, w), lambda i: (i, 0)) for w in widths])(slabs)


def _slabs_from_cols(mats, plan, n, *, name):
    R = mats[0].shape[0]
    tr = _row_tile(R, RELAYOUT_ROWS)

    def body(*refs):
        m_refs, o_ref = refs[:-1], refs[-1]
        for j, sc, oi, oc, w in plan:
            o_ref[j, :, sc:sc + w] = m_refs[oi][:, oc:oc + w]

    return _call(body, name=name, out_shape=_sds((N_DEV, R, n), mats[0].dtype), grid=(R // tr,),
                 in_specs=[pl.BlockSpec((tr, m.shape[1]), lambda i: (i, 0)) for m in mats],
                 out_specs=pl.BlockSpec((N_DEV, tr, n), lambda i: (0, i, 0)))(*mats)


def kernel(x, positions, norm_mix, norm_ffn, norm_final, mix_w_in, pool_w, pool_scale, attn_sinks, mix_w_out, ssm_w_in, ssm_conv_w, ssm_conv_b, ssm_dt_bias, ssm_A_log, ssm_D, ssm_norm, ssm_w_out, ffn_w_up, ffn_conv_w, ffn_conv_b, ffn_w_down, loss_target, m_norm_mix, m_norm_ffn, m_norm_final, m_mix_w_in, m_pool_w, m_pool_scale, m_attn_sinks, m_mix_w_out, m_ssm_w_in, m_ssm_conv_w, m_ssm_conv_b, m_ssm_dt_bias, m_ssm_A_log, m_ssm_D, m_ssm_norm, m_ssm_w_out, m_ffn_w_up, m_ffn_conv_w, m_ffn_conv_b, m_ffn_w_down, v_norm_mix, v_norm_ffn, v_norm_final, v_mix_w_in, v_pool_w, v_pool_scale, v_attn_sinks, v_mix_w_out, v_ssm_w_in, v_ssm_conv_w, v_ssm_conv_b, v_ssm_dt_bias, v_ssm_A_log, v_ssm_D, v_ssm_norm, v_ssm_w_out, v_ffn_w_up, v_ffn_conv_w, v_ffn_conv_b, v_ffn_w_down):
    args = dict(locals())
    wl = {n: args[n] for n in WEIGHTS}
    ml = {n: args["m_" + n] for n in WEIGHTS}
    vl = {n: args["v_" + n] for n in WEIGHTS}
    D = x.shape[2]
    F = ffn_w_down.shape[1] * N_DEV
    DI, CD, NH = ssm_norm.shape[1] * N_DEV, ssm_conv_b.shape[1] * N_DEV, ssm_dt_bias.shape[1]
    Kc, Kf = ssm_conv_w.shape[1], ffn_conv_w.shape[1]
    n_mix, n_ssm, n_up = mix_w_in.shape[2], ssm_w_in.shape[2], ffn_w_up.shape[2]
    plan_mix = _col_plan(n_mix, [(0, N_DEV * n_mix, 0, 0)])
    plan_ssm = _col_plan(n_ssm, [(0, DI, 0, 0), (DI, DI + CD, 1, 0), (DI + CD, DI + CD + NH, 2, 0)])
    plan_up = _col_plan(n_up, _interleave_segments(F))

    def two(a):
        return a.reshape(-1, a.shape[-1])

    def pay(a):
        return a.astype(PAYLOAD)

    order = ("mix_in", "mix_out", "ffn0_up", "ffn0_down", "ssm", "ffn1_up", "ffn1_down")
    gops = {
        "mix_in": [(pay(two(mix_w_in)), "slab")],
        "mix_out": [(pay(mix_w_out[0]), "rows"), (two(ssm_conv_w), "slab"), (ssm_conv_b, "slab"), (ssm_norm, "slab"),
                    (two(ffn_conv_w), "slab")],
        "ffn0_up": [(pay(ffn_w_up[0]), "slab")], "ffn0_down": [(pay(ffn_w_down[0]), "rows")],
        "ssm": [(pay(two(ssm_w_in)), "slab"), (pay(ssm_w_out[0]), "rows")],
        "ffn1_up": [(pay(ffn_w_up[1]), "slab")], "ffn1_down": [(pay(ffn_w_down[1]), "rows")],
    }
    lands, handed = _place_own([op for g in order for op in gops[g]], gather=True, name="gather_own")
    groups, off = [], 0
    for g in order:
        gops[g] = handed[off:off + len(gops[g])]
        groups.append((gops[g], lands[off:off + len(gops[g])]))
        off += len(gops[g])
    gstates, token = _exchange_start(groups, gather=True, name="gather_start")
    gstate = dict(zip(order, gstates))
    P = {n: wl[n] for n in REPLICATED}
    P["ffn_conv_b"] = _to_interleaved(ffn_conv_b)

    def need(g, after):
        got = _exchange_wait(gops[g], gstate[g], after, gather=True, name="gather_wait_" + g)
        if g == "mix_in":
            (P["mix_w_in"],) = _cols_from_slabs(got[0], plan_mix, (N_DEV * n_mix,), name="unpack_mix_w_in")
        elif g == "mix_out":
            P.update(mix_w_out=got[0], ssm_conv_w=got[1].transpose(1, 0, 2).reshape(Kc, CD), ssm_conv_b=got[2].reshape(1, CD),
                     ssm_norm=got[3].reshape(1, DI),
                     ffn_conv_w=_to_interleaved(got[4].transpose(1, 0, 2).reshape(2, Kf, 2 * F)))
        elif g == "ssm":
            P["ssm_wz"], P["ssm_wxbc"], P["ssm_wdt"] = _cols_from_slabs(got[0], plan_ssm, (DI, CD, LANES), name="unpack_ssm_w_in")
            P["ssm_w_out"] = got[1]
        elif g.endswith("_up"):
            (P["ffn_w_up" + g[3]],) = _cols_from_slabs(got[0], plan_up, (2 * F,), name="unpack_ffn_w_up" + g[3])
        else:
            P["ffn_w_down" + g[3]] = got[0]

    sent = {}

    def emit(g, d):
        if g == "mix":
            ops = [(_slabs_from_cols([d["mix_w_in"]], plan_mix, n_mix, name="pack_mix_w_in"), "slab"), (d["mix_w_out"], "rows"),
                   (_from_interleaved(d["ffn_conv_w"]).reshape(2 * Kf, N_DEV, n_up).transpose(1, 0, 2), "slab")]
        elif g == "ssm":
            ops = [(_slabs_from_cols([d["ssm_wz"], d["ssm_wxbc"], d["ssm_wdt"]], plan_ssm, n_ssm, name="pack_ssm_w_in"), "slab"),
                   (d["ssm_w_out"], "rows"), (d["ssm_conv_w"].reshape(Kc, N_DEV, -1).transpose(1, 0, 2), "slab"),
                   (d["ssm_conv_b"].reshape(N_DEV, 1, -1), "slab"), (d["ssm_norm"].reshape(N_DEV, 1, -1), "slab")]
        else:
            ops = [(_slabs_from_cols([d["ffn_w_up"]], plan_up, n_up, name="pack_ffn_w_up" + g[3]), "slab"), (d["ffn_w_down"], "rows")]
        own, ops = _place_own(ops, gather=False, name="scatter_own_" + g)
        (state,), tok = _exchange_start([(ops, own)], gather=False, name="scatter_start_" + g)
        sent[g] = (ops, state)
        return tok

    loss_lanes, grad_x, G = _local_step(x[0], positions.reshape(-1, 1), loss_target[0], P, need, emit, after=token)
    loss = lax.psum(loss_lanes[0, 0], ("x", "y", "c"))

    res = {}

    def update(n, g_layers):
        L = len(g_layers)
        g_layers = [g.reshape(g.shape[0], -1, g.shape[-1]) for g in g_layers]
        shape = (L,) + g_layers[0].shape[1:]
        outs = _adamw(g_layers, wl[n].reshape(shape), ml[n].reshape(shape), vl[n].reshape(shape), name="adamw_" + n)
        for kind, a in zip(("grad", "delta", "new_m", "new_v"), outs):
            res[kind, n] = a.reshape(wl[n].shape)

    recv = {g: _exchange_wait(sent[g][0], sent[g][1], grad_x, gather=False, name="scatter_wait_" + g)
            for g in ("ffn1", "ssm", "ffn0", "mix")}
    update("mix_w_in", [recv["mix"][0]])
    update("mix_w_out", [recv["mix"][1]])
    update("ffn_conv_w", [recv["mix"][2]])
    update("ssm_w_in", [recv["ssm"][0]])
    update("ssm_w_out", [recv["ssm"][1]])
    update("ssm_conv_w", [recv["ssm"][2]])
    update("ssm_conv_b", [recv["ssm"][3]])
    update("ssm_norm", [recv["ssm"][4]])
    update("ffn_w_up", [recv["ffn0"][0], recv["ffn1"][0]])
    update("ffn_w_down", [recv["ffn0"][1], recv["ffn1"][1]])

    rep = _exchange(
        [(a, "slab") for a in (G["norm_mix"], G["norm_ffn"], G["norm_final"], G["pool_w"].reshape(-1, LANES), G["pool_scale"],
                               _from_interleaved(G["ffn_conv_b"]), G["attn_sinks_rows"],
                               G["ssm_dt_bias_g"].reshape(SSM_G, LANES), G["ssm_A_log_g"].reshape(SSM_G, LANES),
                               G["ssm_D_g"].reshape(SSM_G, LANES))],
        gather=True, name="gather_small_grads")
    for n, r in zip(("norm_mix", "norm_ffn", "norm_final", "pool_w", "pool_scale", "ffn_conv_b"), rep):
        update(n, [r])
    sinks_rows, bias_g, alog_g, d_g = _sum_slabs(rep[6:], name="sum_head_grads")
    update("attn_sinks", [sinks_rows[:, 0].reshape(1, 1, N_HEADS)])
    update("ssm_dt_bias", [_ungroup(bias_g)[None]])
    update("ssm_A_log", [_ungroup(alog_g)[None]])
    update("ssm_D", [_ungroup(d_g)[None]])

    return (loss, grad_x[None], *[res[k, n] for k in ("grad", "delta", "new_m", "new_v") for n in WEIGHTS])
```

```python
import functools
import math

import jax
import jax.numpy as jnp
from jax import lax
from jax.experimental import pallas as pl
from jax.experimental.pallas import tpu as pltpu

F32 = jnp.float32
BF16 = jnp.bfloat16

N_DEV = 8
LANES = 128
HEAD_DIM = 64
N_KV_HEADS = 2
GQ = 4
N_HEADS = N_KV_HEADS * GQ
BLOCK = 128
POOL_GROUPS = 4
ROPE_THETA = 10000.0
SSM_P = 64
SSM_G = 8
SSM_R = 4
SSM_N = 128
SSM_L = 128
NORM_EPS = 1e-6
SSM_NORM_EPS = 1e-5
ADAM_LR, ADAM_B1, ADAM_B2, ADAM_EPS, ADAM_WD, ADAM_STEP = 0.001, 0.9, 0.999, 1e-08, 0.01, 10
VMEM_LIMIT = 56 * 2 ** 20
PAYLOAD = jnp.bfloat16

REPLICATED = ("norm_mix", "norm_ffn", "norm_final", "pool_w", "pool_scale", "attn_sinks",
              "ssm_dt_bias", "ssm_A_log", "ssm_D", "ffn_conv_b")
WEIGHTS = ("norm_mix", "norm_ffn", "norm_final", "mix_w_in", "pool_w", "pool_scale", "attn_sinks", "mix_w_out",
           "ssm_w_in", "ssm_conv_w", "ssm_conv_b", "ssm_dt_bias", "ssm_A_log", "ssm_D", "ssm_norm", "ssm_w_out",
           "ffn_w_up", "ffn_conv_w", "ffn_conv_b", "ffn_w_down")


def _tile(n, cap):
    if n <= cap:
        return n
    best = None
    for d in range(LANES, cap + 1, LANES):
        if n % d == 0:
            best = d
    assert best is not None, (n, cap)
    return best


def _call(body, *, name, out_shape, grid=None, in_specs=None, out_specs=None, scratch=(), aliases=None):
    kw = {}
    if grid is not None:
        kw = dict(grid=grid, in_specs=in_specs, out_specs=out_specs)
    if aliases:
        kw["input_output_aliases"] = aliases
    return pl.pallas_call(
        body, name=name, out_shape=out_shape, scratch_shapes=list(scratch),
        compiler_params=pltpu.CompilerParams(vmem_limit_bytes=VMEM_LIMIT), **kw)


ANY = pl.BlockSpec(memory_space=pl.ANY)


def _sds(shape, dtype=F32):
    return jax.ShapeDtypeStruct(tuple(shape), dtype)


def _sigmoid(x):
    return 1.0 / (1.0 + jnp.exp(-x))


def _shift_dn(x, d, t):
    if d == 0:
        return x
    return jnp.where(t >= d, pltpu.roll(x, d, axis=0), 0.0)


def _shift_up(x, d, t):
    if d == 0:
        return x
    n = x.shape[0]
    return jnp.where(t < n - d, pltpu.roll(x, n - d, axis=0), 0.0)


def _mm(a, b, *, name, ta=False, tb=False, res=None, out_dtype=F32, b_rows=None, out_rows=None):
    M, K = (a.shape[1], a.shape[0]) if ta else a.shape
    b0, bn = b_rows if b_rows is not None else (0, b.shape[0])
    N = bn if tb else b.shape[1]
    assert (b.shape[1] if tb else bn) == K, (a.shape, b.shape, ta, tb, b_rows)
    tm, tn, tk = _tile(M, 1408), _tile(N, 1408), _tile(K, 1408)
    nk = K // tk
    dims = (((0 if ta else 1,), (1 if tb else 0,)), ((), ()))
    aliased = out_rows is not None and out_rows[2] is not None

    def body(*refs):
        a_ref, b_ref = refs[:2]
        r_ref = refs[2] if res is not None else None
        o_ref, acc = refs[-2:]
        k = pl.program_id(2)

        @pl.when(k == 0)
        def _():
            acc[...] = jnp.zeros_like(acc)

        acc[...] += lax.dot_general(a_ref[...].astype(BF16), b_ref[...].astype(BF16), dims,
                                    preferred_element_type=F32)

        @pl.when(k == nk - 1)
        def _():
            out = acc[...]
            if res is not None:
                out = out + r_ref[...]
            o_ref[...] = out.astype(out_dtype)

    a_spec = pl.BlockSpec((tk, tm), lambda i, j, k: (k, i)) if ta else pl.BlockSpec((tm, tk), lambda i, j, k: (i, k))
    if tb:
        assert b0 % tn == 0, (b_rows, tn)
        b_spec = pl.BlockSpec((tn, tk), lambda i, j, k: (b0 // tn + j, k))
    else:
        assert b0 % tk == 0, (b_rows, tk)
        b_spec = pl.BlockSpec((tk, tn), lambda i, j, k: (b0 // tk + k, j))
    ins, specs = [a, b], [a_spec, b_spec]
    if res is not None:
        ins.append(res)
        specs.append(pl.BlockSpec((tm, tn), lambda i, j, k: (i, j)))
    aliases = None
    if out_rows is None:
        o_spec = pl.BlockSpec((tm, tn), lambda i, j, k: (i, j))
        out_shape = _sds((M, N), out_dtype)
    else:
        total, o0, prev = out_rows
        assert o0 % tm == 0, (out_rows, tm)
        o_spec = pl.BlockSpec((tm, tn), lambda i, j, k: (o0 // tm + i, j))
        out_shape = _sds((total, N), out_dtype)
        if aliased:
            aliases = {len(ins): 0}
            ins.append(prev)
            specs.append(ANY)
    return _call(body, name=name, out_shape=out_shape, grid=(M // tm, N // tn, nk), in_specs=specs,
                 out_specs=o_spec, scratch=[pltpu.VMEM((tm, tn), F32)], aliases=aliases)(*ins)


def _rmsnorm(x, w, *, name, eps=NORM_EPS, after=None):
    S, D = x.shape
    tm = _tile(S, 512)
    tie = [] if after is None else [after]

    def body(x_ref, w_ref, *rest):
        o_ref = rest[-1]
        xf = x_ref[...]
        r = lax.rsqrt(jnp.mean(xf * xf, axis=-1, keepdims=True) + eps)
        o_ref[...] = (xf * r * w_ref[...]).astype(BF16)

    return _call(body, name=name, out_shape=_sds((S, D), BF16), grid=(S // tm,),
                 in_specs=[pl.BlockSpec((tm, D), lambda i: (i, 0)), pl.BlockSpec((1, D), lambda i: (0, 0))] + [ANY] * len(tie),
                 out_specs=pl.BlockSpec((tm, D), lambda i: (i, 0)))(x, w, *tie)


def _norm_bwd_math(xf, w, dh, eps):
    r = lax.rsqrt(jnp.mean(xf * xf, axis=-1, keepdims=True) + eps)
    xhat = xf * r
    dxh = dh * w
    dx = r * (dxh - xhat * jnp.mean(dxh * xhat, axis=-1, keepdims=True))
    dw = jnp.sum(dh * xhat, axis=0, keepdims=True)
    return dx, dw


def _rmsnorm_bwd(x, w, dh, dres, *, name, eps=NORM_EPS, after=None):
    S, D = x.shape
    tm = _tile(S, 512)
    tie = [] if after is None else [after]

    def body(x_ref, w_ref, dh_ref, dr_ref, *rest):
        dx_ref, dw_ref = rest[-2:]
        dx, dw = _norm_bwd_math(x_ref[...], w_ref[...], dh_ref[...].astype(F32), eps)
        dx_ref[...] = dr_ref[...] + dx

        @pl.when(pl.program_id(0) == 0)
        def _():
            dw_ref[...] = jnp.zeros_like(dw_ref)

        dw_ref[...] += dw

    row = pl.BlockSpec((tm, D), lambda i: (i, 0))
    vec = pl.BlockSpec((1, D), lambda i: (0, 0))
    return _call(body, name=name, out_shape=(_sds((S, D)), _sds((1, D))), grid=(S // tm,),
                 in_specs=[row, vec, row, row] + [ANY] * len(tie), out_specs=(row, vec))(x, w, dh, dres, *tie)


def _final_loss(x, w, target, *, name):
    S, D = x.shape
    tm = _tile(S, 512)

    def body(x_ref, w_ref, t_ref, loss_ref, dx_ref, dw_ref):
        xf, wv = x_ref[...], w_ref[...]
        r = lax.rsqrt(jnp.mean(xf * xf, axis=-1, keepdims=True) + NORM_EPS)
        err = xf * r * wv - t_ref[...]
        part = 0.5 * jnp.sum(jnp.mean(err * err, axis=-1, keepdims=True), axis=0, keepdims=True)
        dx, dw = _norm_bwd_math(xf, wv, err * (1.0 / D), NORM_EPS)
        dx_ref[...] = dx

        @pl.when(pl.program_id(0) == 0)
        def _():
            dw_ref[...] = jnp.zeros_like(dw_ref)
            loss_ref[...] = jnp.zeros_like(loss_ref)

        dw_ref[...] += dw
        loss_ref[...] += jnp.broadcast_to(part, loss_ref.shape)

    row = pl.BlockSpec((tm, D), lambda i: (i, 0))
    vec = pl.BlockSpec((1, D), lambda i: (0, 0))
    return _call(body, name=name, out_shape=(_sds((1, LANES)), _sds((S, D)), _sds((1, D))), grid=(S // tm,),
                 in_specs=[row, vec, row], out_specs=(pl.BlockSpec((1, LANES), lambda i: (0, 0)), row, vec))(x, w, target)


def _rope_tables(pos, inv_freq):
    S = pos.shape[0]
    tm = _tile(S, 512)

    def body(p_ref, f_ref, c_ref, s_ref):
        ang = p_ref[...].astype(F32) * f_ref[...]
        c_ref[...] = jnp.cos(ang)
        s_ref[...] = jnp.sin(ang)

    blk = pl.BlockSpec((tm, LANES), lambda i: (i, 0))
    return _call(body, name="rope_tables", out_shape=(_sds((S, LANES)), _sds((S, LANES))), grid=(S // tm,),
                 in_specs=[pl.BlockSpec((tm, 1), lambda i: (i, 0)), pl.BlockSpec((1, LANES), lambda i: (0, 0))],
                 out_specs=(blk, blk))(pos, inv_freq)


def _rot_half(t):
    lane = lax.broadcasted_iota(jnp.int32, t.shape, 1)
    lo = (lane % HEAD_DIM) < (HEAD_DIM // 2)
    return jnp.where(lo, -pltpu.roll(t, LANES - HEAD_DIM // 2, axis=1), pltpu.roll(t, HEAD_DIM // 2, axis=1))


def _rope(t, c, s):
    return t * c + _rot_half(t) * s


def _unrope(dy, c, s):
    return dy * c - _rot_half(dy * s)


PD = POOL_GROUPS * LANES
QD = N_HEADS * HEAD_DIM
KD = N_KV_HEADS * HEAD_DIM
assert PD % QD == 0 and (PD + QD) % (2 * KD) == 0 and KD == LANES
def _attn_probs(q, kcat, sink, mask):
    s = lax.dot_general(q.astype(BF16), kcat, (((1,), (1,)), ((), ())), preferred_element_type=F32) * (HEAD_DIM ** -0.5)
    s = jnp.where(mask, s, -jnp.inf)
    m = jnp.maximum(jnp.max(s, axis=1, keepdims=True), sink)
    p = jnp.exp(s - m)
    ps = jnp.exp(sink - m)
    inv = 1.0 / (jnp.sum(p, axis=1, keepdims=True) + ps)
    return p * inv, ps * inv


def _attn_mask(n):
    qi = lax.broadcasted_iota(jnp.int32, (BLOCK, 2 * BLOCK), 0)
    kj = lax.broadcasted_iota(jnp.int32, (BLOCK, 2 * BLOCK), 1)
    rel = qi + BLOCK - kj
    return (rel >= 0) & (rel < BLOCK) & ((n > 0) | (kj >= BLOCK))


def _attn_in_specs(nb):
    def cur(n):
        return jnp.minimum(n, nb - 1)

    def prev(n):
        return jnp.clip(n - 1, 0, nb - 1)

    kvb = (PD + QD) // (2 * KD)
    return [pl.BlockSpec(memory_space=pltpu.SMEM),
            pl.BlockSpec((BLOCK, QD), lambda n: (cur(n), PD // QD)),
            pl.BlockSpec((BLOCK, 2 * KD), lambda n: (cur(n), kvb)),
            pl.BlockSpec((BLOCK, 2 * KD), lambda n: (prev(n), kvb)),
            pl.BlockSpec((BLOCK, LANES), lambda n: (cur(n), 0)), pl.BlockSpec((BLOCK, LANES), lambda n: (cur(n), 0)),
            pl.BlockSpec((BLOCK, LANES), lambda n: (prev(n), 0)), pl.BlockSpec((BLOCK, LANES), lambda n: (prev(n), 0))]


def _attn_keys(kvc_ref, kvp_ref, cc, sc, cp, sp):
    kc = _rope(kvc_ref[:, :KD], cc, sc)
    kp = _rope(kvp_ref[:, :KD], cp, sp)
    vc, vp = kvc_ref[:, KD:], kvp_ref[:, KD:]
    kcat, vcat = [], []
    for kk in range(N_KV_HEADS):
        sl = slice(kk * HEAD_DIM, (kk + 1) * HEAD_DIM)
        kcat.append(jnp.concatenate([kp[:, sl], kc[:, sl]], axis=0).astype(BF16))
        vcat.append(jnp.concatenate([vp[:, sl], vc[:, sl]], axis=0).astype(BF16))
    return kcat, vcat


def _attn_fwd(proj, cos, sin, sinks, cat):
    S = proj.shape[0]
    nb = S // BLOCK

    def body(sink_ref, q_ref, kvc_ref, kvp_ref, cc_ref, sc_ref, cp_ref, sp_ref, cat_ref, o_ref):
        n = pl.program_id(0)
        cc, sc = cc_ref[...], sc_ref[...]
        kcat, vcat = _attn_keys(kvc_ref, kvp_ref, cc, sc, cp_ref[...], sp_ref[...])
        mask = _attn_mask(n)
        for j in range(QD // LANES):
            qr = _rope(q_ref[:, j * LANES:(j + 1) * LANES], cc, sc)
            for e in range(LANES // HEAD_DIM):
                h = j * (LANES // HEAD_DIM) + e
                pn, _ = _attn_probs(qr[:, e * HEAD_DIM:(e + 1) * HEAD_DIM], kcat[h // GQ], sink_ref[0, h], mask)
                o_ref[:, h * HEAD_DIM:(h + 1) * HEAD_DIM] = jnp.dot(pn.astype(BF16), vcat[h // GQ], preferred_element_type=F32)

    return _call(body, name="attn_fwd", out_shape=_sds(cat.shape), grid=(nb,),
                 in_specs=_attn_in_specs(nb) + [ANY], out_specs=pl.BlockSpec((BLOCK, QD), lambda n: (n, PD // QD)),
                 aliases={8: 0})(sinks, proj, proj, proj, cos, sin, cos, sin, cat)


def _attn_bwd(proj, cos, sin, sinks, dcat):
    S = proj.shape[0]
    nb = S // BLOCK
    scale = HEAD_DIM ** -0.5
    per = LANES // HEAD_DIM

    def body(sink_ref, q_ref, kvc_ref, kvp_ref, cc_ref, sc_ref, cp_ref, sp_ref, do_ref, o_ref, ds_ref, hold, carry, part, pair):
        n = pl.program_id(0)

        @pl.when(n == 0)
        def _():
            hold[...] = jnp.zeros_like(hold)
            carry[...] = jnp.zeros_like(carry)
            ds_ref[...] = jnp.zeros_like(ds_ref)

        live = jnp.where(n < nb, 1.0, 0.0)
        cc, sc, cp, sp = cc_ref[...], sc_ref[...], cp_ref[...], sp_ref[...]
        kcat, vcat = _attn_keys(kvc_ref, kvp_ref, cc, sc, cp, sp)
        mask = _attn_mask(n)
        o_ref[:, :PD] = jnp.zeros((BLOCK, PD), F32)
        o_ref[:, PD:PD + QD] = hold[...]
        dk = [jnp.zeros((2 * BLOCK, HEAD_DIM), F32) for _ in range(N_KV_HEADS)]
        dv = [jnp.zeros((2 * BLOCK, HEAD_DIM), F32) for _ in range(N_KV_HEADS)]
        row = lax.broadcasted_iota(jnp.int32, (8, LANES), 0)
        acc = {"dsk": jnp.zeros((8, LANES), F32)}

        def head_pair(j):
            qr = _rope(q_ref[:, j * LANES:(j + 1) * LANES], cc, sc)
            for e in range(per):
                yield
                h = j * per + e
                kk = h // GQ
                qh = qr[:, e * HEAD_DIM:(e + 1) * HEAD_DIM]
                pn, psn = _attn_probs(qh, kcat[kk], sink_ref[0, h], mask)
                yield
                doh = (do_ref[:, h * HEAD_DIM:(h + 1) * HEAD_DIM] * live).astype(BF16)
                dp = lax.dot_general(doh, vcat[kk], NT, preferred_element_type=F32)
                yield
                delta = jnp.sum(pn * dp, axis=1, keepdims=True)
                ds = (pn * (dp - delta) * scale).astype(BF16)
                pair[j, :, e * HEAD_DIM:(e + 1) * HEAD_DIM] = jnp.dot(ds, kcat[kk], preferred_element_type=F32)
                yield
                dk[kk] = dk[kk] + lax.dot_general(ds, qh.astype(BF16), TN, preferred_element_type=F32)
                dv[kk] = dv[kk] + lax.dot_general(pn.astype(BF16), doh, TN, preferred_element_type=F32)
                acc["dsk"] = acc["dsk"] + jnp.where(row == h, -jnp.sum(psn * delta), 0.0)
            yield
            hold[:, j * LANES:(j + 1) * LANES] = _unrope(pair[j], cc, sc)

        _interleave([head_pair(j) for j in range(QD // LANES)])
        dsk = acc["dsk"]
        for kk in range(N_KV_HEADS):
            sl = slice(kk * HEAD_DIM, (kk + 1) * HEAD_DIM)
            sv = slice(KD + kk * HEAD_DIM, KD + (kk + 1) * HEAD_DIM)
            part[0, :, sl] = dk[kk][:BLOCK]
            part[0, :, sv] = dv[kk][:BLOCK]
            part[1, :, sl] = dk[kk][BLOCK:]
            part[1, :, sv] = dv[kk][BLOCK:]
        done = carry[...] + part[0]
        o_ref[:, PD + QD:PD + QD + KD] = _unrope(done[:, :KD], cp, sp)
        o_ref[:, PD + QD + KD:] = done[:, KD:]
        carry[...] = part[1]
        ds_ref[...] += dsk

    return _call(body, name="attn_bwd", out_shape=(_sds((S, PD + QD + 2 * KD)), _sds((8, LANES))), grid=(nb + 1,),
                 in_specs=_attn_in_specs(nb) + [pl.BlockSpec((BLOCK, QD), lambda n: (jnp.minimum(n, nb - 1), PD // QD))],
                 out_specs=(pl.BlockSpec((BLOCK, PD + QD + 2 * KD), lambda n: (jnp.maximum(n - 1, 0), 0)),
                            pl.BlockSpec((8, LANES), lambda n: (0, 0))),
                 scratch=[pltpu.VMEM((BLOCK, QD), F32), pltpu.VMEM((BLOCK, 2 * KD), F32),
                          pltpu.VMEM((2, BLOCK, 2 * KD), F32), pltpu.VMEM((QD // LANES, BLOCK, LANES), F32)])(
                     sinks, proj, proj, proj, cos, sin, cos, sin, dcat)


def _pool_sums(u, g, t, shift):
    s2 = u + shift(u, 1, t)
    s4 = s2 + shift(s2, 2, t)
    s8 = s4 + shift(s4, 4, t)
    s16 = s8 + shift(s8, 8, t)
    return jnp.where(g == 0, s2, jnp.where(g == 1, s4, jnp.where(g == 2, s8, s16)))


def _pool_specs(S):
    col = pl.BlockSpec((S, LANES), lambda g: (0, g))
    wsp = pl.BlockSpec((1, LANES, LANES), lambda g: (g, 0, 0))
    vec = pl.BlockSpec((1, LANES), lambda g: (0, g))
    return col, wsp, vec


def _pool_fwd(proj, pool_w, scale):
    S = proj.shape[0]
    col, wsp, vec = _pool_specs(S)

    def body(u_ref, w_ref, s_ref, o_ref):
        g = pl.program_id(0)
        u = u_ref[...]
        t = lax.broadcasted_iota(jnp.int32, u.shape, 0)
        cnt = jnp.minimum(t + 1, 2 << g).astype(F32)
        pm = _pool_sums(u, g, t, _shift_dn) / cnt - u
        o_ref[...] = jnp.dot(pm.astype(BF16), w_ref[0].astype(BF16), preferred_element_type=F32) * s_ref[...]

    return _call(body, name="pool_fwd", out_shape=_sds((S, PD + QD)), grid=(POOL_GROUPS,),
                 in_specs=[col, wsp, vec], out_specs=col)(proj, pool_w, scale)


def _pool_bwd(proj, pool_w, scale, dcat, dproj):
    S = proj.shape[0]
    col, wsp, vec = _pool_specs(S)

    def body(u_ref, w_ref, s_ref, d_ref, dproj_ref, du_ref, dw_ref, dsc_ref):
        g = pl.program_id(0)
        u = u_ref[...]
        t = lax.broadcasted_iota(jnp.int32, u.shape, 0)
        cnt = jnp.minimum(t + 1, 2 << g).astype(F32)
        pm = (_pool_sums(u, g, t, _shift_dn) / cnt - u).astype(BF16)
        wv = w_ref[0].astype(BF16)
        d = d_ref[...]
        pw = jnp.dot(pm, wv, preferred_element_type=F32)
        dsc_ref[...] = jnp.sum(pw * d, axis=0, keepdims=True)
        dpw = (d * s_ref[...]).astype(BF16)
        dw_ref[0] = lax.dot_general(pm, dpw, (((0,), (0,)), ((), ())), preferred_element_type=F32)
        dpm = lax.dot_general(dpw, wv, (((1,), (1,)), ((), ())), preferred_element_type=F32)
        du_ref[...] = _pool_sums(dpm / cnt, g, t, _shift_up) - dpm

    return _call(body, name="pool_bwd",
                 out_shape=(_sds(dproj.shape), _sds((POOL_GROUPS, LANES, LANES)), _sds((1, POOL_GROUPS * LANES))),
                 grid=(POOL_GROUPS,), in_specs=[col, wsp, vec, col, ANY], out_specs=(col, wsp, vec),
                 aliases={4: 0})(proj, pool_w, scale, dcat, dproj)


def _conv(x, w_ref, b_ref, t):
    K = w_ref.shape[0]
    y = b_ref[...] + jnp.zeros_like(x)
    for k in range(K):
        y = y + w_ref[k:k + 1, :] * _shift_dn(x, K - 1 - k, t)
    return y


def _silu_grad(y):
    sg = _sigmoid(y)
    return sg * (1.0 + y * (1.0 - sg))


CONV_ROWS = 64
HALO = 8


def _win_above(ref, r0):
    if isinstance(r0, int):
        assert r0 == 0
        return jnp.concatenate([jnp.zeros((HALO, ref.shape[1]), F32), ref[0:CONV_ROWS, :]], axis=0)
    return ref[pl.ds(pl.multiple_of(r0 - HALO, HALO), CONV_ROWS + HALO), :]


def _rows_at(win, start):
    if start % 8 == 0:
        return win[start:start + CONV_ROWS]
    base = start // 8 * 8
    return pltpu.roll(win, win.shape[0] - (start - base), axis=0)[base:base + CONV_ROWS]


def _taps_above(win, K):
    return [_rows_at(win, HALO - (K - 1 - k)) for k in range(K)]


def _conv_taps(taps, w, b):
    y = b
    for k in range(len(w)):
        y = y + w[k] * taps[k]
    return y


def _conv_t_win(win, w):
    K = len(w)
    out = None
    for k in range(K):
        d = K - 1 - k
        term = w[k] * _rows_at(win, d)
        out = term if out is None else out + term
    return out


def _fold8(x):
    return jnp.sum(x.reshape(CONV_ROWS // 8, 8, x.shape[-1]), axis=0)


def _chunk_loop(S, step, init):
    carry = step(0, init)
    return lax.fori_loop(1, S // CONV_ROWS, lambda i, c: step(pl.multiple_of(i * CONV_ROWS, CONV_ROWS), c), carry)


def _ffn_mid_specs(S, K, layer, nf):
    col = pl.BlockSpec((S, LANES), lambda j: (0, j))
    return [col, col,
            pl.BlockSpec((None, K, LANES), lambda j: (layer, 0, j)), pl.BlockSpec((None, K, LANES), lambda j: (layer, 0, nf + j)),
            pl.BlockSpec((None, 1, LANES), lambda j: (layer, 0, j)), pl.BlockSpec((None, 1, LANES), lambda j: (layer, 0, nf + j))]


def _ffn_mid_fwd(au, ag, cw, cb, layer):
    S, F = au.shape
    nf = F // LANES
    K = cw.shape[1]

    def body(au_ref, ag_ref, wu_ref, wg_ref, bu_ref, bg_ref, o_ref):
        t = lax.broadcasted_iota(jnp.int32, (S, LANES), 0)
        hu = _conv(au_ref[...], wu_ref, bu_ref, t)
        hg = _conv(ag_ref[...], wg_ref, bg_ref, t)
        o_ref[...] = (hg * _sigmoid(hg) * hu).astype(BF16)

    return _call(body, name="ffn_mid_fwd", out_shape=_sds((S, F), BF16), grid=(nf,),
                 in_specs=_ffn_mid_specs(S, K, layer, nf), out_specs=pl.BlockSpec((S, LANES), lambda j: (0, j)))(
                     au, ag, cw, cw, cb[:, None], cb[:, None])


def _ffn_mid_bwd(au, ag, cw, cb, layer, dact):
    S, F = au.shape
    nf = F // LANES
    K = cw.shape[1]

    def body(au_ref, ag_ref, wu_ref, wg_ref, bu_ref, bg_ref, d_ref, dau_ref, dag_ref, dwu_ref, dwg_ref, dbu_ref, dbg_ref,
             dhu_s, dhg_s):
        T = CONV_ROWS
        wu = [wu_ref[k:k + 1, :] for k in range(K)]
        wg = [wg_ref[k:k + 1, :] for k in range(K)]
        bu, bg = bu_ref[...], bg_ref[...]
        zero8 = jnp.zeros((HALO, LANES), F32)
        dhu_s[S:S + HALO, :] = zero8
        dhg_s[S:S + HALO, :] = zero8

        def first_pass(r0, acc):
            tu, tg = _taps_above(_win_above(au_ref, r0), K), _taps_above(_win_above(ag_ref, r0), K)
            hu, hg = _conv_taps(tu, wu, bu), _conv_taps(tg, wg, bg)
            d = d_ref[pl.ds(r0, T), :].astype(F32)
            sg = _sigmoid(hg)
            dhu = d * hg * sg
            dhg = d * hu * (sg * (1.0 + hg * (1.0 - sg)))
            dhu_s[pl.ds(r0, T), :] = dhu
            dhg_s[pl.ds(r0, T), :] = dhg
            new = []
            for dh, taps in ((dhu, tu), (dhg, tg)):
                for k in range(K):
                    new.append(acc[len(new)] + _fold8(dh * taps[k]))
            new.append(acc[2 * K] + _fold8(dhu))
            new.append(acc[2 * K + 1] + _fold8(dhg))
            return tuple(new)

        acc = _chunk_loop(S, first_pass, tuple(zero8 for _ in range(2 * K + 2)))
        for k in range(K):
            dwu_ref[k:k + 1, :] = jnp.sum(acc[k], axis=0, keepdims=True)
            dwg_ref[k:k + 1, :] = jnp.sum(acc[K + k], axis=0, keepdims=True)
        dbu_ref[...] = jnp.sum(acc[2 * K], axis=0, keepdims=True)
        dbg_ref[...] = jnp.sum(acc[2 * K + 1], axis=0, keepdims=True)

        def second_pass(i, carry):
            r0 = pl.multiple_of(i * T, T)
            dau_ref[pl.ds(r0, T), :] = _conv_t_win(dhu_s[pl.ds(r0, T + HALO), :], wu).astype(BF16)
            dag_ref[pl.ds(r0, T), :] = _conv_t_win(dhg_s[pl.ds(r0, T + HALO), :], wg).astype(BF16)
            return carry

        lax.fori_loop(0, S // T, second_pass, 0)

    col = pl.BlockSpec((S, LANES), lambda j: (0, j))
    wsp = pl.BlockSpec((K, LANES), lambda j: (0, j))
    bsp = pl.BlockSpec((1, LANES), lambda j: (0, j))
    dau, dag, dwu, dwg, dbu, dbg = _call(
        body, name="ffn_mid_bwd",
        out_shape=(_sds((S, F), BF16), _sds((S, F), BF16), _sds((K, F)), _sds((K, F)), _sds((1, F)), _sds((1, F))), grid=(nf,),
        in_specs=_ffn_mid_specs(S, K, layer, nf) + [col], out_specs=(col, col, wsp, wsp, bsp, bsp),
        scratch=[pltpu.VMEM((S + HALO, LANES), F32), pltpu.VMEM((S + HALO, LANES), F32)])(
            au, ag, cw, cw, cb[:, None], cb[:, None], dact)
    return dau, dag, jnp.concatenate([dwu, dwg], axis=1), jnp.concatenate([dbu, dbg], axis=1)


def _conv_silu_fwd(x, cw, cb):
    S = x.shape[0]
    K, C = cw.shape

    def body(x_ref, w_ref, b_ref, o_ref):
        t = lax.broadcasted_iota(jnp.int32, (S, LANES), 0)
        y = _conv(x_ref[...], w_ref, b_ref, t)
        o_ref[...] = y * _sigmoid(y)

    col = pl.BlockSpec((S, LANES), lambda j: (0, j))
    return _call(body, name="conv_silu_fwd", out_shape=_sds((S, C)), grid=(C // LANES,),
                 in_specs=[col, pl.BlockSpec((K, LANES), lambda j: (0, j)), pl.BlockSpec((1, LANES), lambda j: (0, j))],
                 out_specs=col)(x, cw, cb)


def _conv_silu_bwd(x, cw, cb, douts):
    S = x.shape[0]
    K, C = cw.shape
    starts, off = [], 0
    for d in douts:
        starts.append(off)
        off += d.shape[1] // LANES
    assert off == C // LANES

    def body(x_ref, w_ref, b_ref, *rest):
        dy_s = rest[-1]
        d_refs, (dx_ref, dw_ref, db_ref) = rest[:len(douts)], rest[len(douts):-1]
        j = pl.program_id(0)
        T = CONV_ROWS
        w = [w_ref[k:k + 1, :] for k in range(K)]
        b = b_ref[...]
        zero8 = jnp.zeros((HALO, LANES), F32)
        dy_s[S:S + HALO, :] = zero8

        def first_pass(r0, acc):
            taps = _taps_above(_win_above(x_ref, r0), K)
            y = _conv_taps(taps, w, b)
            d = d_refs[0][pl.ds(r0, T), :]
            for i in range(1, len(douts)):
                d = jnp.where(j >= starts[i], d_refs[i][pl.ds(r0, T), :], d)
            dy = d * _silu_grad(y)
            dy_s[pl.ds(r0, T), :] = dy
            return tuple(acc[k] + _fold8(dy * taps[k]) for k in range(K)) + (acc[K] + _fold8(dy),)

        acc = _chunk_loop(S, first_pass, tuple(zero8 for _ in range(K + 1)))
        for k in range(K):
            dw_ref[k:k + 1, :] = jnp.sum(acc[k], axis=0, keepdims=True)
        db_ref[...] = jnp.sum(acc[K], axis=0, keepdims=True)

        def second_pass(i, carry):
            r0 = pl.multiple_of(i * T, T)
            dx_ref[pl.ds(r0, T), :] = _conv_t_win(dy_s[pl.ds(r0, T + HALO), :], w).astype(BF16)
            return carry

        lax.fori_loop(0, S // T, second_pass, 0)

    col = pl.BlockSpec((S, LANES), lambda j: (0, j))
    wsp = pl.BlockSpec((K, LANES), lambda j: (0, j))
    bsp = pl.BlockSpec((1, LANES), lambda j: (0, j))

    def dspec(i):
        nblk = douts[i].shape[1] // LANES
        return pl.BlockSpec((S, LANES), lambda j: (0, jnp.clip(j - starts[i], 0, nblk - 1)))

    return _call(body, name="conv_silu_bwd", out_shape=(_sds((S, C), BF16), _sds((K, C)), _sds((1, C))), grid=(C // LANES,),
                 in_specs=[col, wsp, bsp] + [dspec(i) for i in range(len(douts))],
                 out_specs=(col, wsp, bsp), scratch=[pltpu.VMEM((S + HALO, LANES), F32)])(x, cw, cb, *douts)


HI = lax.Precision.HIGHEST


def _ssd_prep_fwd(proj, col0, bias_g, alog_g):
    S = proj.shape[0]
    nc = S // SSM_L
    b0 = col0 // LANES

    def body(raw_ref, b_ref, al_ref, pre_ref, dt_ref, acs_ref, acst_ref):
        r_i = lax.broadcasted_iota(jnp.int32, (LANES, LANES), 0)
        c_i = lax.broadcasted_iota(jnp.int32, (LANES, LANES), 1)
        live = c_i < SSM_R
        tril = jnp.where(r_i >= c_i, 1.0, 0.0)
        raw = raw_ref[...]
        for g in range(SSM_G):
            sel = jnp.where((r_i == SSM_R * g + c_i) & live, 1.0, 0.0)
            pre = jnp.dot(raw, sel, preferred_element_type=F32, precision=HI) + b_ref[g]
            dt = jnp.where(live, jnp.logaddexp(pre, 0.0), 0.0)
            a = dt * (-jnp.exp(al_ref[g]))
            acs = jnp.dot(tril, a, preferred_element_type=F32, precision=HI)
            pre_ref[g] = pre
            dt_ref[g] = dt
            acs_ref[g] = acs
            acst_ref[g] = acs.T

    gsp = pl.BlockSpec((SSM_G, 1, LANES), lambda c: (0, 0, 0))
    blk = pl.BlockSpec((SSM_G, SSM_L, LANES), lambda c: (0, c, 0))
    big = _sds((SSM_G, S, LANES))
    return _call(body, name="ssd_prep_fwd", out_shape=(big, big, big, _sds((SSM_G, LANES, S))), grid=(nc,),
                 in_specs=[pl.BlockSpec((SSM_L, LANES), lambda c: (c, b0)), gsp, gsp],
                 out_specs=(blk, blk, blk, pl.BlockSpec((SSM_G, LANES, SSM_L), lambda c: (0, 0, c))))(proj, bias_g, alog_g)


def _ssd_prep_bwd(pre_g, dt_g, alog_g, ddt_g, dacs_g, dacst_g):
    S = pre_g.shape[1]
    nc = S // SSM_L

    def body(pre_ref, dt_ref, al_ref, ddt_ref, dacs_ref, dacst_ref, draw_ref, db_ref, dal_ref):
        c = pl.program_id(0)
        r_i = lax.broadcasted_iota(jnp.int32, (LANES, LANES), 0)
        c_i = lax.broadcasted_iota(jnp.int32, (LANES, LANES), 1)
        live = c_i < SSM_R
        triu = jnp.where(r_i <= c_i, 1.0, 0.0)

        @pl.when(c == 0)
        def _():
            db_ref[...] = jnp.zeros_like(db_ref)
            dal_ref[...] = jnp.zeros_like(dal_ref)

        draw = jnp.zeros((SSM_L, LANES), F32)
        for g in range(SSM_G):
            dacs = dacs_ref[g] + dacst_ref[g].T
            da = jnp.dot(triu, dacs, preferred_element_type=F32, precision=HI)
            A = -jnp.exp(al_ref[g])
            ddt = ddt_ref[g] + da * A
            dpre = jnp.where(live, ddt * _sigmoid(pre_ref[g]), 0.0)
            unsel = jnp.where((c_i == SSM_R * g + r_i) & (r_i < SSM_R), 1.0, 0.0)
            draw = draw + jnp.dot(dpre, unsel, preferred_element_type=F32, precision=HI)
            db_ref[g] += jnp.sum(dpre, axis=0, keepdims=True)
            dal_ref[g] += jnp.where(live[:1], jnp.sum(da * dt_ref[g], axis=0, keepdims=True) * A, 0.0)
        draw_ref[...] = draw

    gsp = pl.BlockSpec((SSM_G, 1, LANES), lambda c: (0, 0, 0))
    blk = pl.BlockSpec((SSM_G, SSM_L, LANES), lambda c: (0, c, 0))
    gout = _sds((SSM_G, 1, LANES))
    return _call(body, name="ssd_prep_bwd", out_shape=(_sds((S, LANES)), gout, gout), grid=(nc,),
                 in_specs=[blk, blk, gsp, blk, blk, pl.BlockSpec((SSM_G, LANES, SSM_L), lambda c: (0, 0, c))],
                 out_specs=(pl.BlockSpec((SSM_L, LANES), lambda c: (c, 0)), gsp, gsp))(pre_g, dt_g, alog_g, ddt_g, dacs_g, dacst_g)


NT = (((1,), (1,)), ((), ()))
TN = (((0,), (0,)), ((), ()))
SSM_HP = SSM_R * SSM_P


def _ssd_group_terms(xs_ref, dt_ref, acs_ref, d_ref):
    hid = lax.broadcasted_iota(jnp.int32, (1, SSM_HP), 1) // SSM_P
    rid = lax.broadcasted_iota(jnp.int32, (SSM_HP, 1), 0) // SSM_P

    def widen(cols):
        out = cols[0]
        for r in range(1, SSM_R):
            out = jnp.where(hid == r, cols[r], out)
        return out

    dt_c = [dt_ref[:, r:r + 1] for r in range(SSM_R)]
    acs_c = [acs_ref[:, r:r + 1] for r in range(SSM_R)]
    last = [acs_ref[SSM_L - 1:SSM_L, r:r + 1] for r in range(SSM_R)]
    decay_c = [jnp.exp(last[r] - acs_c[r]) for r in range(SSM_R)]
    cd = [jnp.exp(last[r]) for r in range(SSM_R)]
    cd_rows = cd[0]
    for r in range(1, SSM_R):
        cd_rows = jnp.where(rid == r, cd[r], cd_rows)
    xs = xs_ref[...]
    return (xs, xs * widen(dt_c), widen([jnp.exp(a) for a in acs_c]), widen(decay_c),
            widen([d_ref[:, r:r + 1] for r in range(SSM_R)]), cd_rows, dt_c, decay_c, cd)


def _ssd_lmat(acs_ref, acst_ref, r, tril):
    return jnp.exp(jnp.where(tril, acs_ref[:, r:r + 1] - acst_ref[r:r + 1, :], -jnp.inf))


SSM_GPS = 2


def _ssd_specs(rev, nc):
    def cc(c):
        return nc - 1 - c if rev else c
    xs_blocks = (SSM_G * SSM_HP) // (SSM_GPS * SSM_N)
    xs = pl.BlockSpec((SSM_L, SSM_GPS * SSM_HP), lambda g, c: (cc(c), g))
    bsp = pl.BlockSpec((SSM_L, SSM_GPS * SSM_N), lambda g, c: (cc(c), xs_blocks + g))
    csp = pl.BlockSpec((SSM_L, SSM_GPS * SSM_N), lambda g, c: (cc(c), xs_blocks + SSM_G // SSM_GPS + g))
    sc = pl.BlockSpec((SSM_GPS, SSM_L, LANES), lambda g, c: (g, cc(c), 0))
    sct = pl.BlockSpec((SSM_GPS, LANES, SSM_L), lambda g, c: (g, 0, cc(c)))
    gsp = pl.BlockSpec((SSM_GPS, 1, LANES), lambda g, c: (g, 0, 0))
    st = pl.BlockSpec((None, SSM_GPS, SSM_HP, SSM_N), lambda g, c: (cc(c), g, 0, 0))
    return xs, bsp, csp, sc, sct, gsp, st


def _interleave(gens):
    live = list(gens)
    while live:
        for g in list(live):
            try:
                next(g)
            except StopIteration:
                live.remove(g)


def _rounds(gens):
    live = list(gens)
    while live:
        for g in list(live):
            try:
                next(g)
            except StopIteration:
                live.remove(g)
        yield


def _ssd_group_views(gg, xs_ref, b_ref, c_ref, *per_group):
    return (xs_ref.at[:, gg * SSM_HP:(gg + 1) * SSM_HP], b_ref.at[:, gg * SSM_N:(gg + 1) * SSM_N],
            c_ref.at[:, gg * SSM_N:(gg + 1) * SSM_N]) + tuple(r.at[gg] for r in per_group)


def _ssd_fwd(xbc, dt_g, acs_g, acst_g, d_g):
    S = xbc.shape[0]
    nc = S // SSM_L
    xs_s, b_s, c_s, sc, sct, gsp, st = _ssd_specs(False, nc)

    def body(xs_ref, b_ref, c_ref, dt_ref, acs_ref, acst_ref, d_ref, y_ref, st_ref, state):
        c = pl.program_id(1)

        @pl.when(c == 0)
        def _():
            state[...] = jnp.zeros_like(state)

        tril = lax.broadcasted_iota(jnp.int32, (SSM_L, SSM_L), 0) >= lax.broadcasted_iota(jnp.int32, (SSM_L, SSM_L), 1)
        def group(gg):
            xs_v, b_v, c_v, dt_v, acs_v, acst_v, d_v, st_v, state_v = _ssd_group_views(
                gg, xs_ref, b_ref, c_ref, dt_ref, acs_ref, acst_ref, d_ref, st_ref, state)
            y_v = y_ref.at[:, gg * SSM_HP:(gg + 1) * SSM_HP]
            Bb, Cb = b_v[...].astype(BF16), c_v[...].astype(BF16)
            Gm = lax.dot_general(Cb, Bb, NT, preferred_element_type=F32)
            yield
            xs, X, e_all, decay_all, d_all, cd_rows, _, _, _ = _ssd_group_terms(xs_v, dt_v, acs_v, d_v)
            S_all = state_v[...]
            st_v[...] = S_all
            yield
            yo = lax.dot_general(Cb, S_all.astype(BF16), NT, preferred_element_type=F32)
            new_state = lax.dot_general((X * decay_all).astype(BF16), Bb, TN, preferred_element_type=F32)
            yield
            y_v[...] = e_all * yo + d_all * xs
            state_v[...] = S_all * cd_rows + new_state
            for r in range(SSM_R):
                yield
                sl = slice(r * SSM_P, (r + 1) * SSM_P)
                M = Gm * _ssd_lmat(acs_v, acst_v, r, tril)
                yield
                y_v[:, sl] += jnp.dot(M.astype(BF16), X[:, sl].astype(BF16), preferred_element_type=F32)

        _interleave([group(gg) for gg in range(SSM_GPS)])

    return _call(body, name="ssd_fwd",
                 out_shape=(_sds((S, SSM_G * SSM_HP)), _sds((nc, SSM_G, SSM_HP, SSM_N))),
                 grid=(SSM_G // SSM_GPS, nc), in_specs=[xs_s, b_s, c_s, sc, sc, sct, gsp],
                 out_specs=(xs_s, pl.BlockSpec((None, SSM_GPS, SSM_HP, SSM_N), lambda g, c: (c, g, 0, 0))),
                 scratch=[pltpu.VMEM((SSM_GPS, SSM_HP, SSM_N), F32)])(xbc, xbc, xbc, dt_g, acs_g, acst_g, d_g)


def _ssd_bwd(xbc, dt_g, acs_g, acst_g, d_g, states, dy):
    S = xbc.shape[0]
    nc = S // SSM_L
    xs_s, b_s, c_s, sc, sct, gsp, st = _ssd_specs(True, nc)
    bc_out = pl.BlockSpec((SSM_L, SSM_GPS * SSM_N), lambda g, c: (nc - 1 - c, g))

    def body(xs_ref, b_ref, c_ref, dt_ref, acs_ref, acst_ref, d_ref, st_ref, dy_ref,
             dxs_ref, db_ref, dc_ref, ddt_ref, dacs_ref, dacst_ref, dd_ref, dstate):
        c = pl.program_id(1)

        @pl.when(c == 0)
        def _():
            dstate[...] = jnp.zeros_like(dstate)
            dd_ref[...] = jnp.zeros_like(dd_ref)

        tril = lax.broadcasted_iota(jnp.int32, (SSM_L, SSM_L), 0) >= lax.broadcasted_iota(jnp.int32, (SSM_L, SSM_L), 1)
        lane = lax.broadcasted_iota(jnp.int32, (1, LANES), 1)
        subl = lax.broadcasted_iota(jnp.int32, (LANES, 1), 0)
        last_row = lax.broadcasted_iota(jnp.int32, (SSM_L, 1), 0) == SSM_L - 1
        triu = lax.broadcasted_iota(jnp.int32, (SSM_L, SSM_L), 0) <= lax.broadcasted_iota(jnp.int32, (SSM_L, SSM_L), 1)
        def group(gg):
            xs_v, b_v, c_v, dt_v, acs_v, acst_v, d_v, st_v, ddt_v, dacs_v, dacst_v, dd_v, dstate_v = _ssd_group_views(
                gg, xs_ref, b_ref, c_ref, dt_ref, acs_ref, acst_ref, d_ref, st_ref, ddt_ref, dacs_ref, dacst_ref, dd_ref, dstate)
            dy_v, dxs_v = (r.at[:, gg * SSM_HP:(gg + 1) * SSM_HP] for r in (dy_ref, dxs_ref))
            db_v, dc_v = (r.at[:, gg * SSM_N:(gg + 1) * SSM_N] for r in (db_ref, dc_ref))
            Bb, Cb = b_v[...].astype(BF16), c_v[...].astype(BF16)
            Gm = lax.dot_general(Cb, Bb, NT, preferred_element_type=F32)
            GmT = lax.dot_general(Bb, Cb, NT, preferred_element_type=F32)
            yield
            xs, X, e_all, decay_all, d_all, cd_rows, dt_c, decay_c, cd = _ssd_group_terms(xs_v, dt_v, acs_v, d_v)
            S_all, dSn_all, dY = st_v[...], dstate_v[...], dy_v[...]
            Sb, dSnb = S_all.astype(BF16), dSn_all.astype(BF16)
            yield
            T = lax.dot_general(Cb, Sb, NT, preferred_element_type=F32)
            dT = (dY * e_all).astype(BF16)
            dC = jnp.dot(dT, Sb, preferred_element_type=F32)
            dS_prev = lax.dot_general(dT, Cb, TN, preferred_element_type=F32)
            yield
            yo_dy = dY * (e_all * T)
            W = lax.dot_general(Bb, dSnb, NT, preferred_element_type=F32)
            dB = jnp.dot((X * decay_all).astype(BF16), dSnb, preferred_element_type=F32)
            yield
            xw = X * W
            dcd_rows = jnp.sum(dSn_all * S_all, axis=1, keepdims=True)
            dstate_v[...] = dS_prev + dSn_all * cd_rows
            dX_state = W * decay_all
            yield
            acc = dict(dG=jnp.zeros((SSM_L, SSM_L), F32), dGT=jnp.zeros((SSM_L, SSM_L), F32),
                       ddt=jnp.zeros((SSM_L, LANES), F32), dacs=jnp.zeros((SSM_L, LANES), F32),
                       dacst=jnp.zeros((LANES, SSM_L), F32), dd=jnp.zeros((1, LANES), F32))

            def head(r):
                sl = slice(r * SSM_P, (r + 1) * SSM_P)
                Lm = _ssd_lmat(acs_v, acst_v, r, tril)
                LmT = jnp.exp(jnp.where(triu, acst_v[r:r + 1, :] - acs_v[:, r:r + 1], -jnp.inf))
                M = Gm * Lm
                yield
                dYh, xs_h = dY[:, sl], xs[:, sl]
                dYb, Xb = dYh.astype(BF16), X[:, sl].astype(BF16)
                dM = lax.dot_general(dYb, Xb, NT, preferred_element_type=F32)
                yield
                dX = jnp.dot((GmT * LmT).astype(BF16), dYb, preferred_element_type=F32) + dX_state[:, sl]
                acc["dG"] = acc["dG"] + dM * Lm
                acc["dGT"] = acc["dGT"] + lax.dot_general(Xb, dYb, NT, preferred_element_type=F32) * LmT
                yield
                dseg = dM * M
                dd = jnp.sum(xw[:, sl], axis=1, keepdims=True) * decay_c[r]
                dcd = jnp.sum(dcd_rows[sl])
                dacs_col = (jnp.sum(dseg, axis=1, keepdims=True) + jnp.sum(yo_dy[:, sl], axis=1, keepdims=True) - dd
                            + jnp.where(last_row, dcd * cd[r] + jnp.sum(dd), 0.0))
                dacs_row = -jnp.sum(dseg, axis=0, keepdims=True)
                yield
                dxs_v[:, sl] = dX * dt_c[r] + d_all[:, sl] * dYh
                acc["ddt"] = acc["ddt"] + jnp.where(lane == r, jnp.sum(dX * xs_h, axis=1, keepdims=True), 0.0)
                acc["dacs"] = acc["dacs"] + jnp.where(lane == r, dacs_col, 0.0)
                acc["dacst"] = acc["dacst"] + jnp.where(subl == r, dacs_row, 0.0)
                acc["dd"] = acc["dd"] + jnp.where(lane == r, jnp.sum(dYh * xs_h), 0.0)

            yield from _rounds([head(r) for r in range(SSM_R)])
            dc_v[...] = dC + jnp.dot(acc["dG"].astype(BF16), Bb, preferred_element_type=F32)
            db_v[...] = dB + jnp.dot(acc["dGT"].astype(BF16), Cb, preferred_element_type=F32)
            ddt_v[...] = acc["ddt"]
            dacs_v[...] = acc["dacs"]
            dacst_v[...] = acc["dacst"]
            dd_v[...] += acc["dd"]

        _interleave([group(gg) for gg in range(SSM_GPS)])

    big = _sds((SSM_G, S, LANES))
    return _call(body, name="ssd_bwd",
                 out_shape=(_sds((S, SSM_G * SSM_HP)), _sds((S, SSM_G * SSM_N)), _sds((S, SSM_G * SSM_N)),
                            big, big, _sds((SSM_G, LANES, S)), _sds((SSM_G, 1, LANES))),
                 grid=(SSM_G // SSM_GPS, nc), in_specs=[xs_s, b_s, c_s, sc, sc, sct, gsp, st, xs_s],
                 out_specs=(xs_s, bc_out, bc_out, sc, sc, sct, gsp),
                 scratch=[pltpu.VMEM((SSM_GPS, SSM_HP, SSM_N), F32)])(xbc, xbc, xbc, dt_g, acs_g, acst_g, d_g, states, dy)


def _gate_norm_fwd(y, proj, w):
    S, DI = y.shape
    tm = _tile(S, 256)

    def body(y_ref, z_ref, w_ref, o_ref):
        z = z_ref[...]
        gn = y_ref[...] * (z * _sigmoid(z))
        r = lax.rsqrt(jnp.mean(gn * gn, axis=-1, keepdims=True) + SSM_NORM_EPS)
        o_ref[...] = (gn * r * w_ref[...]).astype(BF16)

    row = pl.BlockSpec((tm, DI), lambda i: (i, 0))
    return _call(body, name="gate_norm_fwd", out_shape=_sds((S, DI), BF16), grid=(S // tm,),
                 in_specs=[row, row, pl.BlockSpec((1, DI), lambda i: (0, 0))], out_specs=row)(y, proj, w)


def _gate_norm_bwd(y, proj, w, dout):
    S, DI = y.shape
    tm = _tile(S, 256)

    def body(y_ref, z_ref, w_ref, d_ref, dy_ref, dz_ref, dw_ref):
        z, yv = z_ref[...], y_ref[...]
        sz = z * _sigmoid(z)
        dgn, dw = _norm_bwd_math(yv * sz, w_ref[...], d_ref[...].astype(F32), SSM_NORM_EPS)
        dy_ref[...] = dgn * sz
        dz_ref[...] = (dgn * yv * _silu_grad(z)).astype(BF16)

        @pl.when(pl.program_id(0) == 0)
        def _():
            dw_ref[...] = jnp.zeros_like(dw_ref)

        dw_ref[...] += dw

    row = pl.BlockSpec((tm, DI), lambda i: (i, 0))
    vec = pl.BlockSpec((1, DI), lambda i: (0, 0))
    return _call(body, name="gate_norm_bwd", out_shape=(_sds((S, DI)), _sds((S, DI), BF16), _sds((1, DI))), grid=(S // tm,),
                 in_specs=[row, row, vec, row], out_specs=(row, row, vec))(y, proj, w, dout)


def _group_major(v):
    return jnp.pad(v.reshape(SSM_G, 1, SSM_R), ((0, 0), (0, 0), (0, LANES - SSM_R)))


def _ungroup(t):
    return t[:, :SSM_R].reshape(1, SSM_G * SSM_R)


def _ffn_fwd(x, P, l, need):
    need(f"ffn{l}_up", x)
    h = _rmsnorm(x, P["norm_ffn"][l:l + 1], name=f"ffn{l}_norm")
    wT = P[f"ffn_w_upT{l}"]
    F = wT.shape[0] // 2
    au = _mm(h, wT, tb=True, b_rows=(0, F), name=f"ffn{l}_up_u")
    ag = _mm(h, wT, tb=True, b_rows=(F, F), name=f"ffn{l}_up_g")
    need(f"ffn{l}_down", ag)
    act = _ffn_mid_fwd(au, ag, P["ffn_conv_w"], P["ffn_conv_b"], l)
    out = _mm(act, P[f"ffn_w_down{l}"], res=x, name=f"ffn{l}_down")
    return out, (x, h, au, ag, act)


def _ffn_bwd(saved, P, l, dx, emit):
    x, h, au, ag, act = saved
    wT = P[f"ffn_w_upT{l}"]
    F = wT.shape[0] // 2
    dact = _mm(dx, P[f"ffn_w_down{l}"], tb=True, out_dtype=BF16, name=f"ffn{l}_down_dx")
    dw_down = _mm(act, dx, ta=True, out_dtype=PAYLOAD, name=f"ffn{l}_down_dw")
    dau, dag, dcw, dcb = _ffn_mid_bwd(au, ag, P["ffn_conv_w"], P["ffn_conv_b"], l, dact)
    dw_upT = _mm(dau, h, ta=True, out_dtype=PAYLOAD, out_rows=(2 * F, 0, None), name=f"ffn{l}_up_u_dw")
    dw_upT = _mm(dag, h, ta=True, out_dtype=PAYLOAD, out_rows=(2 * F, F, dw_upT), name=f"ffn{l}_up_g_dw")
    tie = emit(f"ffn{l}", {"ffn_w_upT": dw_upT, "ffn_w_down": dw_down})
    dh = _mm(dau, wT, b_rows=(0, F), name=f"ffn{l}_up_u_dx")
    dh = _mm(dag, wT, b_rows=(F, F), res=dh, name=f"ffn{l}_up_g_dx")
    dx_in, dnw = _rmsnorm_bwd(x, P["norm_ffn"][l:l + 1], dh, dx, name=f"ffn{l}_norm_bwd", after=tie)
    return dx_in, dnw, dcw, dcb


def _local_step(x, positions, target, P, need, emit, after=None):
    S, D = x.shape
    inv_freq = ROPE_THETA ** (-jnp.arange(0, HEAD_DIM, 2, dtype=F32) / HEAD_DIM)
    inv_freq = jnp.tile(inv_freq, LANES // (HEAD_DIM // 2)).reshape(1, LANES)
    cos, sin = _rope_tables(positions, inv_freq)

    nm0 = P["norm_mix"][0:1]
    h0 = _rmsnorm(x, nm0, name="mix_norm", after=after)
    need("mix_in", h0)
    proj0 = _mm(h0, P["mix_w_inT"], tb=True, name="mix_in")
    cat0 = _attn_fwd(proj0, cos, sin, P["attn_sinks"], _pool_fwd(proj0, P["pool_w"][0], P["pool_scale"]))
    need("mix_out", cat0)
    x1 = _mm(cat0, P["mix_w_out"], res=x, name="mix_out")
    x2, ffn0 = _ffn_fwd(x1, P, 0, need)

    nm1 = P["norm_mix"][1:2]
    need("ssm", x2)
    h1 = _rmsnorm(x2, nm1, name="ssm_norm_in")
    w1T, wdtT = P["ssm_w_inT"], P["ssm_wdtT"]
    DI, CD, NH = P["ssm_norm"].shape[1], P["ssm_conv_w"].shape[1], P["ssm_dt_bias"].shape[1]
    z = _mm(h1, w1T, tb=True, b_rows=(0, DI), name="ssm_in_z")
    xbcp = _mm(h1, w1T, tb=True, b_rows=(DI, CD), name="ssm_in_xbc")
    dtraw = _mm(h1, wdtT, tb=True, name="ssm_in_dt")
    xbc = _conv_silu_fwd(xbcp, P["ssm_conv_w"], P["ssm_conv_b"])
    bias_g, alog_g, d_g = _group_major(P["ssm_dt_bias"]), _group_major(P["ssm_A_log"]), _group_major(P["ssm_D"])
    pre_g, dt_g, acs_g, acst_g = _ssd_prep_fwd(dtraw, 0, bias_g, alog_g)
    y, states = _ssd_fwd(xbc, dt_g, acs_g, acst_g, d_g)
    yn = _gate_norm_fwd(y, z, P["ssm_norm"])
    x3 = _mm(yn, P["ssm_w_out"], res=x2, name="ssm_out")
    x4, ffn1 = _ffn_fwd(x3, P, 1, need)

    loss, dx, d_norm_final = _final_loss(x4, P["norm_final"].reshape(1, D), target, name="final_loss")
    dx, dnf1, dcw1, dcb1 = _ffn_bwd(ffn1, P, 1, dx, emit)
    dyn = _mm(dx, P["ssm_w_out"], tb=True, out_dtype=BF16, name="ssm_out_dx")
    d_w_out1 = _mm(yn, dx, ta=True, out_dtype=PAYLOAD, name="ssm_out_dw")
    dy, dz, d_ssm_norm = _gate_norm_bwd(y, z, P["ssm_norm"], dyn)
    dxs, dB, dC, ddt_g, dacs_g, dacst_g, dd_g = _ssd_bwd(xbc, dt_g, acs_g, acst_g, d_g, states, dy)
    draw, dbias_g, dalog_g = _ssd_prep_bwd(pre_g, dt_g, alog_g, ddt_g, dacs_g, dacst_g)
    dxbc, d_conv_w1, d_conv_b1 = _conv_silu_bwd(xbcp, P["ssm_conv_w"], P["ssm_conv_b"], [dxs, dB, dC])
    rows = DI + CD + NH
    d_w1T = _mm(dz, h1, ta=True, out_dtype=PAYLOAD, out_rows=(rows, 0, None), name="ssm_in_z_dw")
    d_w1T = _mm(dxbc, h1, ta=True, out_dtype=PAYLOAD, out_rows=(rows, DI, d_w1T), name="ssm_in_xbc_dw")
    d_w1T = _mm(draw[:, :NH], h1, ta=True, out_dtype=PAYLOAD, out_rows=(rows, DI + CD, d_w1T), name="ssm_in_dt_dw")
    tie = emit("ssm", {"ssm_w_inT": d_w1T, "ssm_w_out": d_w_out1,
                       "ssm_conv_w": d_conv_w1, "ssm_conv_b": d_conv_b1, "ssm_norm": d_ssm_norm})
    dh1 = _mm(dz, w1T, b_rows=(0, DI), name="ssm_in_z_dx")
    dh1 = _mm(dxbc, w1T, b_rows=(DI, CD), res=dh1, name="ssm_in_xbc_dx")
    dh1 = _mm(draw, wdtT, res=dh1, name="ssm_in_dt_dx")
    dx, dnm1 = _rmsnorm_bwd(x2, nm1, dh1, dx, name="ssm_norm_in_bwd", after=tie)
    dx, dnf0, dcw0, dcb0 = _ffn_bwd(ffn0, P, 0, dx, emit)
    dcat = _mm(dx, P["mix_w_out"], tb=True, name="mix_out_dx")
    d_w_out0 = _mm(cat0, dx, ta=True, out_dtype=PAYLOAD, name="mix_out_dw")
    dproj0, dsk = _attn_bwd(proj0, cos, sin, P["attn_sinks"], dcat)
    dproj0, d_pool_w, d_pool_scale = _pool_bwd(proj0, P["pool_w"][0], P["pool_scale"], dcat, dproj0)
    d_w_in0 = _mm(dproj0, h0, ta=True, out_dtype=PAYLOAD, name="mix_in_dw")
    tie = emit("mix", {"mix_w_inT": d_w_in0, "mix_w_out": d_w_out0, "ffn_conv_w": jnp.stack([dcw0, dcw1])})
    dh0 = _mm(dproj0, P["mix_w_inT"], name="mix_in_dx")
    grad_x, dnm0 = _rmsnorm_bwd(x, nm0, dh0, dx, name="mix_norm_bwd", after=tie)

    small = {
        "norm_mix": jnp.concatenate([dnm0, dnm1], axis=0),
        "norm_ffn": jnp.concatenate([dnf0, dnf1], axis=0),
        "norm_final": d_norm_final,
        "pool_w": d_pool_w,
        "pool_scale": d_pool_scale,
        "attn_sinks_rows": dsk,
        "ssm_dt_bias_g": dbias_g, "ssm_A_log_g": dalog_g, "ssm_D_g": dd_g,
        "ffn_conv_b": jnp.concatenate([dcb0, dcb1], axis=0),
    }
    return loss, grad_x, small


def _peer(k):
    x, y, c = lax.axis_index("x"), lax.axis_index("y"), lax.axis_index("c")
    px = 1 - x if k & 4 else x
    py = 1 - y if k & 2 else y
    pc = 1 - c if k & 1 else c
    return (px, py, pc), 4 * px + 2 * py + pc


def _my_index():
    return 4 * lax.axis_index("x") + 2 * lax.axis_index("y") + lax.axis_index("c")


def _land_sds(a, mode, gather):
    if mode == "slab":
        return _sds(((N_DEV,) + a.shape) if gather else a.shape, a.dtype)
    assert mode == "rows", mode
    return _sds((N_DEV * a.shape[0],) + a.shape[1:] if gather else (N_DEV, a.shape[0] // N_DEV) + a.shape[1:], a.dtype)


def _part(ref, mode, shape, idx):
    if mode == "slab":
        return ref.at[idx]
    r = shape[0] // N_DEV
    return ref.at[pl.ds(idx * r, r)]


def _own_copies(ops, gather, srcs, lands, sems):
    me = _my_index()
    out = []
    for i, (a, mode) in enumerate(ops):
        s = srcs[i] if gather else _part(srcs[i], mode, a.shape, me)
        d = _part(lands[i], mode, _land_sds(a, mode, gather).shape, me) if gather else lands[i].at[me]
        out.append(pltpu.make_async_copy(s, d, sems.at[i]))
    return out


def _remote_copies(ops, gather, srcs, lands, send_sems, recv_sems):
    me = _my_index()
    n = len(ops)
    out = []
    for k in range(1, N_DEV):
        dev, idx = _peer(k)
        for i, (a, mode) in enumerate(ops):
            s = srcs[i] if gather else _part(srcs[i], mode, a.shape, idx)
            d = _part(lands[i], mode, _land_sds(a, mode, gather).shape, me) if gather else lands[i].at[me]
            out.append(pltpu.make_async_remote_copy(src_ref=s, dst_ref=d, send_sem=send_sems.at[(k - 1) * n + i],
                                                    recv_sem=recv_sems.at[(k - 1) * n + i], device_id=dev,
                                                    device_id_type=pl.DeviceIdType.MESH))
    return out


def _exchange(ops, *, gather, name):
    n = len(ops)

    def body(*refs):
        ins, outs = refs[:n], refs[n:2 * n]
        send_sems, recv_sems, local_sems = refs[2 * n:]
        copies = _own_copies(ops, gather, ins, outs, local_sems) + _remote_copies(ops, gather, ins, outs, send_sems, recv_sems)
        for cp in copies:
            cp.start()
        for cp in copies:
            cp.wait()

    return pl.pallas_call(
        body, name=name, out_shape=[_land_sds(a, m, gather) for a, m in ops], in_specs=[ANY] * n, out_specs=[ANY] * n,
        scratch_shapes=[pltpu.SemaphoreType.DMA((n * (N_DEV - 1),)), pltpu.SemaphoreType.DMA((n * (N_DEV - 1),)),
                        pltpu.SemaphoreType.DMA((n,))],
    )(*[a for a, _ in ops])


HBM = pl.BlockSpec(memory_space=pltpu.HBM)
SEM = pl.BlockSpec(memory_space=pltpu.SEMAPHORE)
SIDE_EFFECT = pltpu.SideEffectType.DATAFLOW_SIDE_EFFECTING


def _in_hbm(a):
    return pltpu.with_memory_space_constraint(a, pltpu.HBM)


def _place_own(ops, *, gather, name):
    n = len(ops)

    def zeros(k):
        return (0,) * k

    in_specs, out_specs = [], []
    for a, mode in ops:
        nd = a.ndim
        if gather and mode == "slab":
            in_specs.append(pl.BlockSpec(a.shape, lambda i, nd=nd: zeros(nd)))
            out_specs.append(pl.BlockSpec((1,) + a.shape, lambda i, nd=nd: (_my_index(),) + zeros(nd)))
        elif gather:
            in_specs.append(pl.BlockSpec(a.shape, lambda i, nd=nd: zeros(nd)))
            out_specs.append(pl.BlockSpec(a.shape, lambda i, nd=nd: (_my_index(),) + zeros(nd - 1)))
        elif mode == "slab":
            in_specs.append(pl.BlockSpec((1,) + a.shape[1:], lambda i, nd=nd: (_my_index(),) + zeros(nd - 1)))
            out_specs.append(pl.BlockSpec((1,) + a.shape[1:], lambda i, nd=nd: (_my_index(),) + zeros(nd - 1)))
        else:
            r = a.shape[0] // N_DEV
            in_specs.append(pl.BlockSpec((r,) + a.shape[1:], lambda i, nd=nd: (_my_index(),) + zeros(nd - 1)))
            out_specs.append(pl.BlockSpec((1, r) + a.shape[1:], lambda i, nd=nd: (_my_index(),) + zeros(nd)))

    def body(*refs):
        for i_ref, o_ref in zip(refs[:n], refs[n:2 * n]):
            if o_ref.shape == i_ref.shape:
                o_ref[...] = i_ref[...]
            else:
                o_ref[0] = i_ref[...]

    outs = _call(body, name=name, grid=(1,), in_specs=in_specs, out_specs=out_specs + [ANY] * n,
                 out_shape=[_land_sds(a, m, gather) for a, m in ops] + [_sds(a.shape, a.dtype) for a, _ in ops],
                 aliases={i: n + i for i in range(n)})(*[a for a, _ in ops])
    return outs[:n], [(src, m) for src, (_, m) in zip(outs[n:], ops)]


def _exchange_start(groups, *, gather, name):
    sizes = [len(ops) for ops, _ in groups]
    n = sum(sizes)
    G = len(groups)

    def body(*refs):
        srcs, lands = refs[:n], refs[n:2 * n]
        sems = refs[2 * n:2 * n + 2 * G]
        token = refs[-1]
        off = 0
        for g, (ops, _) in enumerate(groups):
            for cp in _remote_copies(ops, gather, srcs[off:off + sizes[g]], lands[off:off + sizes[g]], sems[2 * g], sems[2 * g + 1]):
                cp.start()
            off += sizes[g]
        token[...] = jnp.zeros_like(token)

    srcs = [a for ops, _ in groups for a, _ in ops]
    lands = [l for _, ls in groups for l in ls]
    sem_shapes = [pltpu.SemaphoreType.DMA((s * (N_DEV - 1),)) for s in sizes for _ in range(2)]
    outs = pl.pallas_call(
        body, name=name,
        out_shape=sem_shapes + [pltpu.HBM(a.shape, a.dtype) for a in srcs + lands] + [_sds((8, LANES))],
        in_specs=[HBM] * (2 * n), out_specs=[SEM] * (2 * G) + [HBM] * (2 * n) + [pl.BlockSpec(memory_space=pltpu.VMEM)],
        input_output_aliases={i: 2 * G + i for i in range(2 * n)},
        compiler_params=pltpu.CompilerParams(has_side_effects=SIDE_EFFECT))(*[_in_hbm(a) for a in srcs + lands])
    sems, thru, token = outs[:2 * G], outs[2 * G:2 * G + 2 * n], outs[-1]
    states, off = [], 0
    for g, s in enumerate(sizes):
        states.append((sems[2 * g], sems[2 * g + 1], thru[off:off + s], thru[n + off:n + off + s]))
        off += s
    return states, token


def _exchange_wait(ops, state, after, *, gather, name):
    send_sems, recv_sems, srcs, lands = state
    n = len(ops)

    def body(*refs):
        for cp in _remote_copies(ops, gather, refs[:n], refs[n:2 * n], refs[2 * n], refs[2 * n + 1]):
            cp.wait_send()
            cp.wait_recv()

    outs = pl.pallas_call(
        body, name=name, out_shape=[pltpu.HBM(a.shape, a.dtype) for a in list(srcs) + list(lands)],
        in_specs=[HBM] * (2 * n) + [SEM, SEM, ANY], out_specs=[HBM] * (2 * n),
        input_output_aliases={i: i for i in range(2 * n)},
        compiler_params=pltpu.CompilerParams(has_side_effects=SIDE_EFFECT))(*srcs, *lands, send_sems, recv_sems, after)
    return outs[n:]


ADAM_ROWS = 256


def _row_tile(R, cap=ADAM_ROWS):
    best = R
    if R > cap:
        for d in range(16, cap + 1, 16):
            if R % d == 0:
                best = d
    return best


def _adamw(g_layers, w, m, v, *, name):
    L = len(g_layers)
    J, R, Wd = g_layers[0].shape
    assert w.shape == (L, R, Wd), (g_layers[0].shape, w.shape)
    tr = _row_tile(R)
    nrt = R // tr
    c1 = 1.0 / (1.0 - ADAM_B1 ** ADAM_STEP)
    c2 = 1.0 / (1.0 - ADAM_B2 ** ADAM_STEP)

    def body(*refs):
        g_refs = refs[:L]
        w_ref, m_ref, v_ref, go_ref, d_ref, mo_ref, vo_ref = refs[L:]
        layer = pl.program_id(0)
        g = None
        for l, g_ref in enumerate(g_refs):
            gl = g_ref[0].astype(F32)
            for j in range(1, J):
                gl = gl + g_ref[j].astype(F32)
            g = gl if g is None else jnp.where(layer == l, gl, g)
        mn = ADAM_B1 * m_ref[...] + (1.0 - ADAM_B1) * g
        vn = ADAM_B2 * v_ref[...] + (1.0 - ADAM_B2) * (g * g)
        go_ref[...] = g
        mo_ref[...] = mn
        vo_ref[...] = vn
        d_ref[...] = -ADAM_LR * ((mn * c1) / (jnp.sqrt(vn * c2) + ADAM_EPS) + ADAM_WD * w_ref[...])

    def g_spec(l):
        return pl.BlockSpec((J, tr, Wd), lambda ll, i: (0, jnp.where(ll == l, i, jnp.where(ll < l, 0, nrt - 1)), 0))

    row = pl.BlockSpec((None, tr, Wd), lambda ll, i: (ll, i, 0))
    out = _sds((L, R, Wd))
    return _call(body, name=name, out_shape=(out, out, out, out), grid=(L, nrt),
                 in_specs=[g_spec(l) for l in range(L)] + [row, row, row], out_specs=(row, row, row, row))(*g_layers, w, m, v)


def _sum_slabs(slabs, *, name):
    n = len(slabs)

    def body(*refs):
        for g_ref, o_ref in zip(refs[:n], refs[n:]):
            g = g_ref[0]
            for j in range(1, g_ref.shape[0]):
                g = g + g_ref[j]
            o_ref[...] = g

    return _call(body, name=name, out_shape=[_sds(s.shape[1:]) for s in slabs])(*slabs)


def kernel(x, positions, norm_mix, norm_ffn, norm_final, mix_w_in, pool_w, pool_scale, attn_sinks, mix_w_out, ssm_w_in, ssm_conv_w, ssm_conv_b, ssm_dt_bias, ssm_A_log, ssm_D, ssm_norm, ssm_w_out, ffn_w_up, ffn_conv_w, ffn_conv_b, ffn_w_down, loss_target, m_norm_mix, m_norm_ffn, m_norm_final, m_mix_w_in, m_pool_w, m_pool_scale, m_attn_sinks, m_mix_w_out, m_ssm_w_in, m_ssm_conv_w, m_ssm_conv_b, m_ssm_dt_bias, m_ssm_A_log, m_ssm_D, m_ssm_norm, m_ssm_w_out, m_ffn_w_up, m_ffn_conv_w, m_ffn_conv_b, m_ffn_w_down, v_norm_mix, v_norm_ffn, v_norm_final, v_mix_w_in, v_pool_w, v_pool_scale, v_attn_sinks, v_mix_w_out, v_ssm_w_in, v_ssm_conv_w, v_ssm_conv_b, v_ssm_dt_bias, v_ssm_A_log, v_ssm_D, v_ssm_norm, v_ssm_w_out, v_ffn_w_up, v_ffn_conv_w, v_ffn_conv_b, v_ffn_w_down):
    args = dict(locals())
    wl = {n: args[n] for n in WEIGHTS}
    ml = {n: args["m_" + n] for n in WEIGHTS}
    vl = {n: args["v_" + n] for n in WEIGHTS}
    F = ffn_w_down.shape[1] * N_DEV
    DI, CD, NH = ssm_norm.shape[1] * N_DEV, ssm_conv_b.shape[1] * N_DEV, ssm_dt_bias.shape[1]
    Kc, Kf = ssm_conv_w.shape[1], ffn_conv_w.shape[1]
    n_up = ffn_w_up.shape[2]
    col_sharded = ("mix_w_in", "ssm_w_in", "ffn_w_up")

    def tr(a):
        return jnp.swapaxes(a, -1, -2)

    def two(a):
        return a.reshape(-1, a.shape[-1])

    def pay(a):
        return a.astype(PAYLOAD)

    order = ("mix_in", "mix_out", "ffn0_up", "ffn0_down", "ssm", "ffn1_up", "ffn1_down")
    gops = {
        "mix_in": [(pay(tr(mix_w_in)[0]), "rows")],
        "mix_out": [(pay(mix_w_out[0]), "rows"), (two(ssm_conv_w), "slab"), (ssm_conv_b, "slab"), (ssm_norm, "slab"),
                    (two(ffn_conv_w), "slab")],
        "ffn0_up": [(pay(tr(ffn_w_up)[0]), "rows")], "ffn0_down": [(pay(ffn_w_down[0]), "rows")],
        "ssm": [(pay(tr(ssm_w_in)[0]), "slab"), (pay(ssm_w_out[0]), "rows")],
        "ffn1_up": [(pay(tr(ffn_w_up)[1]), "rows")], "ffn1_down": [(pay(ffn_w_down[1]), "rows")],
    }
    lands, handed = _place_own([op for g in order for op in gops[g]], gather=True, name="gather_own")
    groups, off = [], 0
    for g in order:
        gops[g] = handed[off:off + len(gops[g])]
        groups.append((gops[g], lands[off:off + len(gops[g])]))
        off += len(gops[g])
    gstates, token = _exchange_start(groups, gather=True, name="gather_start")
    gstate = dict(zip(order, gstates))
    P = {n: wl[n] for n in REPLICATED}

    def need(g, after):
        got = _exchange_wait(gops[g], gstate[g], after, gather=True, name="gather_wait_" + g)
        if g == "mix_in":
            P["mix_w_inT"] = got[0]
        elif g == "mix_out":
            P.update(mix_w_out=got[0], ssm_conv_w=got[1].transpose(1, 0, 2).reshape(Kc, CD), ssm_conv_b=got[2].reshape(1, CD),
                     ssm_norm=got[3].reshape(1, DI), ffn_conv_w=got[4].transpose(1, 0, 2).reshape(2, Kf, 2 * F))
        elif g == "ssm":
            w1T = got[0].reshape(-1, got[0].shape[-1])
            P.update(ssm_w_inT=w1T, ssm_w_out=got[1], ssm_wdtT=jnp.pad(w1T[DI + CD:], ((0, LANES - NH), (0, 0))))
        elif g.endswith("_up"):
            P["ffn_w_upT" + g[3]] = got[0]
        else:
            P["ffn_w_down" + g[3]] = got[0]

    sent = {}

    def emit(g, d):
        if g == "mix":
            ops = [(d["mix_w_inT"], "rows"), (d["mix_w_out"], "rows"),
                   (d["ffn_conv_w"].reshape(2 * Kf, N_DEV, n_up).transpose(1, 0, 2), "slab")]
        elif g == "ssm":
            ops = [(d["ssm_w_inT"].reshape(N_DEV, -1, d["ssm_w_inT"].shape[-1]), "slab"), (d["ssm_w_out"], "rows"),
                   (d["ssm_conv_w"].reshape(Kc, N_DEV, -1).transpose(1, 0, 2), "slab"),
                   (d["ssm_conv_b"].reshape(N_DEV, 1, -1), "slab"), (d["ssm_norm"].reshape(N_DEV, 1, -1), "slab")]
        else:
            ops = [(d["ffn_w_upT"], "rows"), (d["ffn_w_down"], "rows")]
        own, ops = _place_own(ops, gather=False, name="scatter_own_" + g)
        (state,), tok = _exchange_start([(ops, own)], gather=False, name="scatter_start_" + g)
        sent[g] = (ops, state)
        return tok

    loss_lanes, grad_x, G = _local_step(x[0], positions.reshape(-1, 1), loss_target[0], P, need, emit, after=token)
    loss = lax.psum(loss_lanes[0, 0], ("x", "y", "c"))

    res = {}

    def update(n, g_layers):
        L = len(g_layers)
        g_layers = [g.reshape(g.shape[0], -1, g.shape[-1]) for g in g_layers]
        shape = (L,) + g_layers[0].shape[1:]
        view = tr if n in col_sharded else (lambda a: a)
        outs = _adamw(g_layers, view(wl[n]).reshape(shape), view(ml[n]).reshape(shape), view(vl[n]).reshape(shape),
                      name="adamw_" + n)
        for kind, a in zip(("grad", "delta", "new_m", "new_v"), outs):
            res[kind, n] = view(a.reshape(view(wl[n]).shape))

    recv = {g: _exchange_wait(sent[g][0], sent[g][1], grad_x, gather=False, name="scatter_wait_" + g)
            for g in ("ffn1", "ssm", "ffn0", "mix")}
    update("mix_w_in", [recv["mix"][0]])
    update("mix_w_out", [recv["mix"][1]])
    update("ffn_conv_w", [recv["mix"][2]])
    update("ssm_w_in", [recv["ssm"][0]])
    update("ssm_w_out", [recv["ssm"][1]])
    update("ssm_conv_w", [recv["ssm"][2]])
    update("ssm_conv_b", [recv["ssm"][3]])
    update("ssm_norm", [recv["ssm"][4]])
    update("ffn_w_up", [recv["ffn0"][0], recv["ffn1"][0]])
    update("ffn_w_down", [recv["ffn0"][1], recv["ffn1"][1]])

    rep = _exchange(
        [(a, "slab") for a in (G["norm_mix"], G["norm_ffn"], G["norm_final"], G["pool_w"].reshape(-1, LANES), G["pool_scale"],
                               G["ffn_conv_b"], G["attn_sinks_rows"],
                               G["ssm_dt_bias_g"].reshape(SSM_G, LANES), G["ssm_A_log_g"].reshape(SSM_G, LANES),
                               G["ssm_D_g"].reshape(SSM_G, LANES))],
        gather=True, name="gather_small_grads")
    for n, r in zip(("norm_mix", "norm_ffn", "norm_final", "pool_w", "pool_scale", "ffn_conv_b"), rep):
        update(n, [r])
    sinks_rows, bias_g, alog_g, d_g = _sum_slabs(rep[6:], name="sum_head_grads")
    update("attn_sinks", [sinks_rows[:, 0].reshape(1, 1, N_HEADS)])
    update("ssm_dt_bias", [_ungroup(bias_g)[None]])
    update("ssm_A_log", [_ungroup(alog_g)[None]])
    update("ssm_D", [_ungroup(d_g)[None]])

    return (loss, grad_x[None], *[res[k, n] for k in ("grad", "delta", "new_m", "new_v") for n in WEIGHTS])
```

```python
import functools
import math

import jax
import jax.numpy as jnp
from jax import lax
from jax.experimental import pallas as pl
from jax.experimental.pallas import tpu as pltpu

F32 = jnp.float32
BF16 = jnp.bfloat16

N_DEV = 8
LANES = 128
HEAD_DIM = 64
N_KV_HEADS = 2
GQ = 4
N_HEADS = N_KV_HEADS * GQ
BLOCK = 128
POOL_GROUPS = 4
ROPE_THETA = 10000.0
SSM_P = 64
SSM_G = 8
SSM_R = 4
SSM_N = 128
SSM_L = 128
NORM_EPS = 1e-6
SSM_NORM_EPS = 1e-5
ADAM_LR, ADAM_B1, ADAM_B2, ADAM_EPS, ADAM_WD, ADAM_STEP = 0.001, 0.9, 0.999, 1e-08, 0.01, 10
VMEM_LIMIT = 56 * 2 ** 20
PAYLOAD = jnp.bfloat16

REPLICATED = ("norm_mix", "norm_ffn", "norm_final", "pool_w", "pool_scale", "attn_sinks",
              "ssm_dt_bias", "ssm_A_log", "ssm_D", "ffn_conv_b")
WEIGHTS = ("norm_mix", "norm_ffn", "norm_final", "mix_w_in", "pool_w", "pool_scale", "attn_sinks", "mix_w_out",
           "ssm_w_in", "ssm_conv_w", "ssm_conv_b", "ssm_dt_bias", "ssm_A_log", "ssm_D", "ssm_norm", "ssm_w_out",
           "ffn_w_up", "ffn_conv_w", "ffn_conv_b", "ffn_w_down")


def _tile(n, cap):
    if n <= cap:
        return n
    best = None
    for d in range(LANES, cap + 1, LANES):
        if n % d == 0:
            best = d
    assert best is not None, (n, cap)
    return best


def _call(body, *, name, out_shape, grid=None, in_specs=None, out_specs=None, scratch=(), aliases=None):
    kw = {}
    if grid is not None:
        kw = dict(grid=grid, in_specs=in_specs, out_specs=out_specs)
    if aliases:
        kw["input_output_aliases"] = aliases
    return pl.pallas_call(
        body, name=name, out_shape=out_shape, scratch_shapes=list(scratch),
        compiler_params=pltpu.CompilerParams(vmem_limit_bytes=VMEM_LIMIT), **kw)


ANY = pl.BlockSpec(memory_space=pl.ANY)


def _sds(shape, dtype=F32):
    return jax.ShapeDtypeStruct(tuple(shape), dtype)


def _sigmoid(x):
    return 1.0 / (1.0 + jnp.exp(-x))


def _shift_dn(x, d, t):
    if d == 0:
        return x
    return jnp.where(t >= d, pltpu.roll(x, d, axis=0), 0.0)


def _shift_up(x, d, t):
    if d == 0:
        return x
    n = x.shape[0]
    return jnp.where(t < n - d, pltpu.roll(x, n - d, axis=0), 0.0)


def _mm(a, b, *, name, ta=False, tb=False, res=None, out_dtype=F32, b_rows=None, out_rows=None):
    M, K = (a.shape[1], a.shape[0]) if ta else a.shape
    b0, bn = b_rows if b_rows is not None else (0, b.shape[0])
    N = bn if tb else b.shape[1]
    assert (b.shape[1] if tb else bn) == K, (a.shape, b.shape, ta, tb, b_rows)
    tm, tn, tk = _tile(M, 1408), _tile(N, 1408), _tile(K, 1408)
    nk = K // tk
    dims = (((0 if ta else 1,), (1 if tb else 0,)), ((), ()))
    aliased = out_rows is not None and out_rows[2] is not None

    def body(*refs):
        a_ref, b_ref = refs[:2]
        r_ref = refs[2] if res is not None else None
        o_ref, acc = refs[-2:]
        k = pl.program_id(2)

        @pl.when(k == 0)
        def _():
            acc[...] = jnp.zeros_like(acc)

        acc[...] += lax.dot_general(a_ref[...].astype(BF16), b_ref[...].astype(BF16), dims,
                                    preferred_element_type=F32)

        @pl.when(k == nk - 1)
        def _():
            out = acc[...]
            if res is not None:
                out = out + r_ref[...]
            o_ref[...] = out.astype(out_dtype)

    a_spec = pl.BlockSpec((tk, tm), lambda i, j, k: (k, i)) if ta else pl.BlockSpec((tm, tk), lambda i, j, k: (i, k))
    if tb:
        assert b0 % tn == 0, (b_rows, tn)
        b_spec = pl.BlockSpec((tn, tk), lambda i, j, k: (b0 // tn + j, k))
    else:
        assert b0 % tk == 0, (b_rows, tk)
        b_spec = pl.BlockSpec((tk, tn), lambda i, j, k: (b0 // tk + k, j))
    ins, specs = [a, b], [a_spec, b_spec]
    if res is not None:
        ins.append(res)
        specs.append(pl.BlockSpec((tm, tn), lambda i, j, k: (i, j)))
    aliases = None
    if out_rows is None:
        o_spec = pl.BlockSpec((tm, tn), lambda i, j, k: (i, j))
        out_shape = _sds((M, N), out_dtype)
    else:
        total, o0, prev = out_rows
        assert o0 % tm == 0, (out_rows, tm)
        o_spec = pl.BlockSpec((tm, tn), lambda i, j, k: (o0 // tm + i, j))
        out_shape = _sds((total, N), out_dtype)
        if aliased:
            aliases = {len(ins): 0}
            ins.append(prev)
            specs.append(ANY)
    return _call(body, name=name, out_shape=out_shape, grid=(M // tm, N // tn, nk), in_specs=specs,
                 out_specs=o_spec, scratch=[pltpu.VMEM((tm, tn), F32)], aliases=aliases)(*ins)


def _rmsnorm(x, w, *, name, eps=NORM_EPS, after=None):
    S, D = x.shape
    tm = _tile(S, 512)
    tie = [] if after is None else [after]

    def body(x_ref, w_ref, *rest):
        o_ref = rest[-1]
        xf = x_ref[...]
        r = lax.rsqrt(jnp.mean(xf * xf, axis=-1, keepdims=True) + eps)
        o_ref[...] = (xf * r * w_ref[...]).astype(BF16)

    return _call(body, name=name, out_shape=_sds((S, D), BF16), grid=(S // tm,),
                 in_specs=[pl.BlockSpec((tm, D), lambda i: (i, 0)), pl.BlockSpec((1, D), lambda i: (0, 0))] + [ANY] * len(tie),
                 out_specs=pl.BlockSpec((tm, D), lambda i: (i, 0)))(x, w, *tie)


def _norm_bwd_math(xf, w, dh, eps):
    r = lax.rsqrt(jnp.mean(xf * xf, axis=-1, keepdims=True) + eps)
    xhat = xf * r
    dxh = dh * w
    dx = r * (dxh - xhat * jnp.mean(dxh * xhat, axis=-1, keepdims=True))
    dw = jnp.sum(dh * xhat, axis=0, keepdims=True)
    return dx, dw


def _rmsnorm_bwd(x, w, dh, dres, *, name, eps=NORM_EPS, after=None):
    S, D = x.shape
    tm = _tile(S, 512)
    tie = [] if after is None else [after]

    def body(x_ref, w_ref, dh_ref, dr_ref, *rest):
        dx_ref, dw_ref = rest[-2:]
        dx, dw = _norm_bwd_math(x_ref[...], w_ref[...], dh_ref[...].astype(F32), eps)
        dx_ref[...] = dr_ref[...] + dx

        @pl.when(pl.program_id(0) == 0)
        def _():
            dw_ref[...] = jnp.zeros_like(dw_ref)

        dw_ref[...] += dw

    row = pl.BlockSpec((tm, D), lambda i: (i, 0))
    vec = pl.BlockSpec((1, D), lambda i: (0, 0))
    return _call(body, name=name, out_shape=(_sds((S, D)), _sds((1, D))), grid=(S // tm,),
                 in_specs=[row, vec, row, row] + [ANY] * len(tie), out_specs=(row, vec))(x, w, dh, dres, *tie)


def _final_loss(x, w, target, *, name):
    S, D = x.shape
    tm = _tile(S, 512)

    def body(x_ref, w_ref, t_ref, loss_ref, dx_ref, dw_ref):
        xf, wv = x_ref[...], w_ref[...]
        r = lax.rsqrt(jnp.mean(xf * xf, axis=-1, keepdims=True) + NORM_EPS)
        err = xf * r * wv - t_ref[...]
        part = 0.5 * jnp.sum(jnp.mean(err * err, axis=-1, keepdims=True), axis=0, keepdims=True)
        dx, dw = _norm_bwd_math(xf, wv, err * (1.0 / D), NORM_EPS)
        dx_ref[...] = dx

        @pl.when(pl.program_id(0) == 0)
        def _():
            dw_ref[...] = jnp.zeros_like(dw_ref)
            loss_ref[...] = jnp.zeros_like(loss_ref)

        dw_ref[...] += dw
        loss_ref[...] += jnp.broadcast_to(part, loss_ref.shape)

    row = pl.BlockSpec((tm, D), lambda i: (i, 0))
    vec = pl.BlockSpec((1, D), lambda i: (0, 0))
    return _call(body, name=name, out_shape=(_sds((1, LANES)), _sds((S, D)), _sds((1, D))), grid=(S // tm,),
                 in_specs=[row, vec, row], out_specs=(pl.BlockSpec((1, LANES), lambda i: (0, 0)), row, vec))(x, w, target)


def _rope_tables(pos, inv_freq):
    S = pos.shape[0]
    tm = _tile(S, 512)

    def body(p_ref, f_ref, c_ref, s_ref):
        ang = p_ref[...].astype(F32) * f_ref[...]
        c_ref[...] = jnp.cos(ang)
        s_ref[...] = jnp.sin(ang)

    blk = pl.BlockSpec((tm, LANES), lambda i: (i, 0))
    return _call(body, name="rope_tables", out_shape=(_sds((S, LANES)), _sds((S, LANES))), grid=(S // tm,),
                 in_specs=[pl.BlockSpec((tm, 1), lambda i: (i, 0)), pl.BlockSpec((1, LANES), lambda i: (0, 0))],
                 out_specs=(blk, blk))(pos, inv_freq)


def _rot_half(t):
    lane = lax.broadcasted_iota(jnp.int32, t.shape, 1)
    lo = (lane % HEAD_DIM) < (HEAD_DIM // 2)
    return jnp.where(lo, -pltpu.roll(t, LANES - HEAD_DIM // 2, axis=1), pltpu.roll(t, HEAD_DIM // 2, axis=1))


def _rope(t, c, s):
    return t * c + _rot_half(t) * s


def _unrope(dy, c, s):
    return dy * c - _rot_half(dy * s)


PD = POOL_GROUPS * LANES
QD = N_HEADS * HEAD_DIM
KD = N_KV_HEADS * HEAD_DIM
assert PD % QD == 0 and (PD + QD) % (2 * KD) == 0 and KD == LANES
def _attn_probs(q, kcat, sink, mask):
    s = lax.dot_general(q.astype(BF16), kcat, (((1,), (1,)), ((), ())), preferred_element_type=F32) * (HEAD_DIM ** -0.5)
    s = jnp.where(mask, s, -jnp.inf)
    m = jnp.maximum(jnp.max(s, axis=1, keepdims=True), sink)
    p = jnp.exp(s - m)
    ps = jnp.exp(sink - m)
    inv = 1.0 / (jnp.sum(p, axis=1, keepdims=True) + ps)
    return p * inv, ps * inv


def _attn_mask(n):
    qi = lax.broadcasted_iota(jnp.int32, (BLOCK, 2 * BLOCK), 0)
    kj = lax.broadcasted_iota(jnp.int32, (BLOCK, 2 * BLOCK), 1)
    rel = qi + BLOCK - kj
    return (rel >= 0) & (rel < BLOCK) & ((n > 0) | (kj >= BLOCK))


def _attn_in_specs(nb):
    def cur(n):
        return jnp.minimum(n, nb - 1)

    def prev(n):
        return jnp.clip(n - 1, 0, nb - 1)

    kvb = (PD + QD) // (2 * KD)
    return [pl.BlockSpec(memory_space=pltpu.SMEM),
            pl.BlockSpec((BLOCK, QD), lambda n: (cur(n), PD // QD)),
            pl.BlockSpec((BLOCK, 2 * KD), lambda n: (cur(n), kvb)),
            pl.BlockSpec((BLOCK, 2 * KD), lambda n: (prev(n), kvb)),
            pl.BlockSpec((BLOCK, LANES), lambda n: (cur(n), 0)), pl.BlockSpec((BLOCK, LANES), lambda n: (cur(n), 0)),
            pl.BlockSpec((BLOCK, LANES), lambda n: (prev(n), 0)), pl.BlockSpec((BLOCK, LANES), lambda n: (prev(n), 0))]


def _attn_keys(kvc_ref, kvp_ref, cc, sc, cp, sp):
    kc = _rope(kvc_ref[:, :KD], cc, sc)
    kp = _rope(kvp_ref[:, :KD], cp, sp)
    vc, vp = kvc_ref[:, KD:], kvp_ref[:, KD:]
    kcat, vcat = [], []
    for kk in range(N_KV_HEADS):
        sl = slice(kk * HEAD_DIM, (kk + 1) * HEAD_DIM)
        kcat.append(jnp.concatenate([kp[:, sl], kc[:, sl]], axis=0).astype(BF16))
        vcat.append(jnp.concatenate([vp[:, sl], vc[:, sl]], axis=0).astype(BF16))
    return kcat, vcat


def _attn_fwd(proj, cos, sin, sinks, cat):
    S = proj.shape[0]
    nb = S // BLOCK

    def body(sink_ref, q_ref, kvc_ref, kvp_ref, cc_ref, sc_ref, cp_ref, sp_ref, cat_ref, o_ref):
        n = pl.program_id(0)
        cc, sc = cc_ref[...], sc_ref[...]
        kcat, vcat = _attn_keys(kvc_ref, kvp_ref, cc, sc, cp_ref[...], sp_ref[...])
        mask = _attn_mask(n)
        def head_pair(j):
            qr = _rope(q_ref[:, j * LANES:(j + 1) * LANES], cc, sc)
            for e in range(LANES // HEAD_DIM):
                yield
                h = j * (LANES // HEAD_DIM) + e
                pn, _ = _attn_probs(qr[:, e * HEAD_DIM:(e + 1) * HEAD_DIM], kcat[h // GQ], sink_ref[0, h], mask)
                yield
                o_ref[:, h * HEAD_DIM:(h + 1) * HEAD_DIM] = jnp.dot(pn.astype(BF16), vcat[h // GQ], preferred_element_type=F32)

        _interleave([head_pair(j) for j in range(QD // LANES)])

    return _call(body, name="attn_fwd", out_shape=_sds(cat.shape), grid=(nb,),
                 in_specs=_attn_in_specs(nb) + [ANY], out_specs=pl.BlockSpec((BLOCK, QD), lambda n: (n, PD // QD)),
                 aliases={8: 0})(sinks, proj, proj, proj, cos, sin, cos, sin, cat)


def _attn_bwd(proj, cos, sin, sinks, dcat):
    S = proj.shape[0]
    nb = S // BLOCK
    scale = HEAD_DIM ** -0.5
    per = LANES // HEAD_DIM

    def body(sink_ref, q_ref, kvc_ref, kvp_ref, cc_ref, sc_ref, cp_ref, sp_ref, do_ref, o_ref, ds_ref, hold, carry, part, pair):
        n = pl.program_id(0)

        @pl.when(n == 0)
        def _():
            hold[...] = jnp.zeros_like(hold)
            carry[...] = jnp.zeros_like(carry)
            ds_ref[...] = jnp.zeros_like(ds_ref)

        live = jnp.where(n < nb, 1.0, 0.0)
        cc, sc, cp, sp = cc_ref[...], sc_ref[...], cp_ref[...], sp_ref[...]
        kcat, vcat = _attn_keys(kvc_ref, kvp_ref, cc, sc, cp, sp)
        mask = _attn_mask(n)
        o_ref[:, :PD] = jnp.zeros((BLOCK, PD), F32)
        o_ref[:, PD:PD + QD] = hold[...]
        dk = [jnp.zeros((2 * BLOCK, HEAD_DIM), F32) for _ in range(N_KV_HEADS)]
        dv = [jnp.zeros((2 * BLOCK, HEAD_DIM), F32) for _ in range(N_KV_HEADS)]
        row = lax.broadcasted_iota(jnp.int32, (8, LANES), 0)
        acc = {"dsk": jnp.zeros((8, LANES), F32)}

        def head_pair(j):
            qr = _rope(q_ref[:, j * LANES:(j + 1) * LANES], cc, sc)
            for e in range(per):
                yield
                h = j * per + e
                kk = h // GQ
                qh = qr[:, e * HEAD_DIM:(e + 1) * HEAD_DIM]
                pn, psn = _attn_probs(qh, kcat[kk], sink_ref[0, h], mask)
                yield
                doh = (do_ref[:, h * HEAD_DIM:(h + 1) * HEAD_DIM] * live).astype(BF16)
                dp = lax.dot_general(doh, vcat[kk], NT, preferred_element_type=F32)
                yield
                delta = jnp.sum(pn * dp, axis=1, keepdims=True)
                ds = (pn * (dp - delta) * scale).astype(BF16)
                pair[j, :, e * HEAD_DIM:(e + 1) * HEAD_DIM] = jnp.dot(ds, kcat[kk], preferred_element_type=F32)
                yield
                dk[kk] = dk[kk] + lax.dot_general(ds, qh.astype(BF16), TN, preferred_element_type=F32)
                dv[kk] = dv[kk] + lax.dot_general(pn.astype(BF16), doh, TN, preferred_element_type=F32)
                acc["dsk"] = acc["dsk"] + jnp.where(row == h, -jnp.sum(psn * delta), 0.0)
            yield
            hold[:, j * LANES:(j + 1) * LANES] = _unrope(pair[j], cc, sc)

        _interleave([head_pair(j) for j in range(QD // LANES)])
        dsk = acc["dsk"]
        for kk in range(N_KV_HEADS):
            sl = slice(kk * HEAD_DIM, (kk + 1) * HEAD_DIM)
            sv = slice(KD + kk * HEAD_DIM, KD + (kk + 1) * HEAD_DIM)
            part[0, :, sl] = dk[kk][:BLOCK]
            part[0, :, sv] = dv[kk][:BLOCK]
            part[1, :, sl] = dk[kk][BLOCK:]
            part[1, :, sv] = dv[kk][BLOCK:]
        done = carry[...] + part[0]
        o_ref[:, PD + QD:PD + QD + KD] = _unrope(done[:, :KD], cp, sp)
        o_ref[:, PD + QD + KD:] = done[:, KD:]
        carry[...] = part[1]
        ds_ref[...] += dsk

    return _call(body, name="attn_bwd", out_shape=(_sds((S, PD + QD + 2 * KD)), _sds((8, LANES))), grid=(nb + 1,),
                 in_specs=_attn_in_specs(nb) + [pl.BlockSpec((BLOCK, QD), lambda n: (jnp.minimum(n, nb - 1), PD // QD))],
                 out_specs=(pl.BlockSpec((BLOCK, PD + QD + 2 * KD), lambda n: (jnp.maximum(n - 1, 0), 0)),
                            pl.BlockSpec((8, LANES), lambda n: (0, 0))),
                 scratch=[pltpu.VMEM((BLOCK, QD), F32), pltpu.VMEM((BLOCK, 2 * KD), F32),
                          pltpu.VMEM((2, BLOCK, 2 * KD), F32), pltpu.VMEM((QD // LANES, BLOCK, LANES), F32)])(
                     sinks, proj, proj, proj, cos, sin, cos, sin, dcat)


def _pool_sums(u, g, t, shift):
    s2 = u + shift(u, 1, t)
    s4 = s2 + shift(s2, 2, t)
    s8 = s4 + shift(s4, 4, t)
    s16 = s8 + shift(s8, 8, t)
    return jnp.where(g == 0, s2, jnp.where(g == 1, s4, jnp.where(g == 2, s8, s16)))


def _pool_specs(S):
    col = pl.BlockSpec((S, LANES), lambda g: (0, g))
    wsp = pl.BlockSpec((1, LANES, LANES), lambda g: (g, 0, 0))
    vec = pl.BlockSpec((1, LANES), lambda g: (0, g))
    return col, wsp, vec


def _pool_fwd(proj, pool_w, scale):
    S = proj.shape[0]
    col, wsp, vec = _pool_specs(S)

    def body(u_ref, w_ref, s_ref, o_ref):
        g = pl.program_id(0)
        u = u_ref[...]
        t = lax.broadcasted_iota(jnp.int32, u.shape, 0)
        cnt = jnp.minimum(t + 1, 2 << g).astype(F32)
        pm = _pool_sums(u, g, t, _shift_dn) / cnt - u
        o_ref[...] = jnp.dot(pm.astype(BF16), w_ref[0].astype(BF16), preferred_element_type=F32) * s_ref[...]

    return _call(body, name="pool_fwd", out_shape=_sds((S, PD + QD)), grid=(POOL_GROUPS,),
                 in_specs=[col, wsp, vec], out_specs=col)(proj, pool_w, scale)


def _pool_bwd(proj, pool_w, scale, dcat, dproj):
    S = proj.shape[0]
    col, wsp, vec = _pool_specs(S)

    def body(u_ref, w_ref, s_ref, d_ref, dproj_ref, du_ref, dw_ref, dsc_ref):
        g = pl.program_id(0)
        u = u_ref[...]
        t = lax.broadcasted_iota(jnp.int32, u.shape, 0)
        cnt = jnp.minimum(t + 1, 2 << g).astype(F32)
        pm = (_pool_sums(u, g, t, _shift_dn) / cnt - u).astype(BF16)
        wv = w_ref[0].astype(BF16)
        d = d_ref[...]
        pw = jnp.dot(pm, wv, preferred_element_type=F32)
        dsc_ref[...] = jnp.sum(pw * d, axis=0, keepdims=True)
        dpw = (d * s_ref[...]).astype(BF16)
        dw_ref[0] = lax.dot_general(pm, dpw, (((0,), (0,)), ((), ())), preferred_element_type=F32)
        dpm = lax.dot_general(dpw, wv, (((1,), (1,)), ((), ())), preferred_element_type=F32)
        du_ref[...] = _pool_sums(dpm / cnt, g, t, _shift_up) - dpm

    return _call(body, name="pool_bwd",
                 out_shape=(_sds(dproj.shape), _sds((POOL_GROUPS, LANES, LANES)), _sds((1, POOL_GROUPS * LANES))),
                 grid=(POOL_GROUPS,), in_specs=[col, wsp, vec, col, ANY], out_specs=(col, wsp, vec),
                 aliases={4: 0})(proj, pool_w, scale, dcat, dproj)


def _conv(x, w_ref, b_ref, t):
    K = w_ref.shape[0]
    y = b_ref[...] + jnp.zeros_like(x)
    for k in range(K):
        y = y + w_ref[k:k + 1, :] * _shift_dn(x, K - 1 - k, t)
    return y


def _silu_grad(y):
    sg = _sigmoid(y)
    return sg * (1.0 + y * (1.0 - sg))


CONV_ROWS = 64
HALO = 8


def _win_above(ref, r0):
    if isinstance(r0, int):
        assert r0 == 0
        return jnp.concatenate([jnp.zeros((HALO, ref.shape[1]), F32), ref[0:CONV_ROWS, :]], axis=0)
    return ref[pl.ds(pl.multiple_of(r0 - HALO, HALO), CONV_ROWS + HALO), :]


def _rows_at(win, start):
    if start % 8 == 0:
        return win[start:start + CONV_ROWS]
    base = start // 8 * 8
    return pltpu.roll(win, win.shape[0] - (start - base), axis=0)[base:base + CONV_ROWS]


def _taps_above(win, K):
    return [_rows_at(win, HALO - (K - 1 - k)) for k in range(K)]


def _conv_taps(taps, w, b):
    y = b
    for k in range(len(w)):
        y = y + w[k] * taps[k]
    return y


def _conv_t_win(win, w):
    K = len(w)
    out = None
    for k in range(K):
        d = K - 1 - k
        term = w[k] * _rows_at(win, d)
        out = term if out is None else out + term
    return out


def _fold8(x):
    return jnp.sum(x.reshape(CONV_ROWS // 8, 8, x.shape[-1]), axis=0)


def _chunk_loop(S, step, init):
    carry = step(0, init)
    return lax.fori_loop(1, S // CONV_ROWS, lambda i, c: step(pl.multiple_of(i * CONV_ROWS, CONV_ROWS), c), carry)


def _ffn_mid_specs(S, K, layer, nf):
    return [pl.BlockSpec((S, LANES), lambda j: (0, j)), pl.BlockSpec((S, LANES), lambda j: (0, nf + j)),
            pl.BlockSpec((None, K, LANES), lambda j: (layer, 0, j)), pl.BlockSpec((None, K, LANES), lambda j: (layer, 0, nf + j)),
            pl.BlockSpec((None, 1, LANES), lambda j: (layer, 0, j)), pl.BlockSpec((None, 1, LANES), lambda j: (layer, 0, nf + j))]


def _ffn_mid_fwd(a, cw, cb, layer):
    S, F = a.shape[0], a.shape[1] // 2
    nf = F // LANES
    K = cw.shape[1]

    def body(au_ref, ag_ref, wu_ref, wg_ref, bu_ref, bg_ref, o_ref):
        t = lax.broadcasted_iota(jnp.int32, (S, LANES), 0)
        hu = _conv(au_ref[...], wu_ref, bu_ref, t)
        hg = _conv(ag_ref[...], wg_ref, bg_ref, t)
        o_ref[...] = (hg * _sigmoid(hg) * hu).astype(BF16)

    return _call(body, name="ffn_mid_fwd", out_shape=_sds((S, F), BF16), grid=(nf,),
                 in_specs=_ffn_mid_specs(S, K, layer, nf), out_specs=pl.BlockSpec((S, LANES), lambda j: (0, j)))(
                     a, a, cw, cw, cb[:, None], cb[:, None])


def _ffn_mid_bwd(a, cw, cb, layer, dact):
    S, F = a.shape[0], a.shape[1] // 2
    nf = F // LANES
    K = cw.shape[1]

    def body(au_ref, ag_ref, wu_ref, wg_ref, bu_ref, bg_ref, d_ref, dau_ref, dag_ref, dwu_ref, dwg_ref, dbu_ref, dbg_ref,
             dhu_s, dhg_s):
        T = CONV_ROWS
        wu = [wu_ref[k:k + 1, :] for k in range(K)]
        wg = [wg_ref[k:k + 1, :] for k in range(K)]
        bu, bg = bu_ref[...], bg_ref[...]
        zero8 = jnp.zeros((HALO, LANES), F32)
        dhu_s[S:S + HALO, :] = zero8
        dhg_s[S:S + HALO, :] = zero8

        def first_pass(r0, acc):
            tu, tg = _taps_above(_win_above(au_ref, r0), K), _taps_above(_win_above(ag_ref, r0), K)
            hu, hg = _conv_taps(tu, wu, bu), _conv_taps(tg, wg, bg)
            d = d_ref[pl.ds(r0, T), :].astype(F32)
            sg = _sigmoid(hg)
            dhu = d * hg * sg
            dhg = d * hu * (sg * (1.0 + hg * (1.0 - sg)))
            dhu_s[pl.ds(r0, T), :] = dhu
            dhg_s[pl.ds(r0, T), :] = dhg
            new = []
            for dh, taps in ((dhu, tu), (dhg, tg)):
                for k in range(K):
                    new.append(acc[len(new)] + _fold8(dh * taps[k]))
            new.append(acc[2 * K] + _fold8(dhu))
            new.append(acc[2 * K + 1] + _fold8(dhg))
            return tuple(new)

        acc = _chunk_loop(S, first_pass, tuple(zero8 for _ in range(2 * K + 2)))
        for k in range(K):
            dwu_ref[k:k + 1, :] = jnp.sum(acc[k], axis=0, keepdims=True)
            dwg_ref[k:k + 1, :] = jnp.sum(acc[K + k], axis=0, keepdims=True)
        dbu_ref[...] = jnp.sum(acc[2 * K], axis=0, keepdims=True)
        dbg_ref[...] = jnp.sum(acc[2 * K + 1], axis=0, keepdims=True)

        def second_pass(i, carry):
            r0 = pl.multiple_of(i * T, T)
            dau_ref[pl.ds(r0, T), :] = _conv_t_win(dhu_s[pl.ds(r0, T + HALO), :], wu).astype(BF16)
            dag_ref[pl.ds(r0, T), :] = _conv_t_win(dhg_s[pl.ds(r0, T + HALO), :], wg).astype(BF16)
            return carry

        lax.fori_loop(0, S // T, second_pass, 0)

    col = pl.BlockSpec((S, LANES), lambda j: (0, j))
    wsp = pl.BlockSpec((K, LANES), lambda j: (0, j))
    bsp = pl.BlockSpec((1, LANES), lambda j: (0, j))
    dau, dag, dwu, dwg, dbu, dbg = _call(
        body, name="ffn_mid_bwd",
        out_shape=(_sds((S, F), BF16), _sds((S, F), BF16), _sds((K, F)), _sds((K, F)), _sds((1, F)), _sds((1, F))), grid=(nf,),
        in_specs=_ffn_mid_specs(S, K, layer, nf) + [col], out_specs=(col, col, wsp, wsp, bsp, bsp),
        scratch=[pltpu.VMEM((S + HALO, LANES), F32), pltpu.VMEM((S + HALO, LANES), F32)])(
            a, a, cw, cw, cb[:, None], cb[:, None], dact)
    return dau, dag, jnp.concatenate([dwu, dwg], axis=1), jnp.concatenate([dbu, dbg], axis=1)


def _conv_silu_fwd(x, cw, cb):
    S = x.shape[0]
    K, C = cw.shape

    def body(x_ref, w_ref, b_ref, o_ref):
        t = lax.broadcasted_iota(jnp.int32, (S, LANES), 0)
        y = _conv(x_ref[...], w_ref, b_ref, t)
        o_ref[...] = y * _sigmoid(y)

    col = pl.BlockSpec((S, LANES), lambda j: (0, j))
    return _call(body, name="conv_silu_fwd", out_shape=_sds((S, C)), grid=(C // LANES,),
                 in_specs=[col, pl.BlockSpec((K, LANES), lambda j: (0, j)), pl.BlockSpec((1, LANES), lambda j: (0, j))],
                 out_specs=col)(x, cw, cb)


def _conv_silu_bwd(x, cw, cb, douts):
    S = x.shape[0]
    K, C = cw.shape
    starts, off = [], 0
    for d in douts:
        starts.append(off)
        off += d.shape[1] // LANES
    assert off == C // LANES

    def body(x_ref, w_ref, b_ref, *rest):
        dy_s = rest[-1]
        d_refs, (dx_ref, dw_ref, db_ref) = rest[:len(douts)], rest[len(douts):-1]
        j = pl.program_id(0)
        T = CONV_ROWS
        w = [w_ref[k:k + 1, :] for k in range(K)]
        b = b_ref[...]
        zero8 = jnp.zeros((HALO, LANES), F32)
        dy_s[S:S + HALO, :] = zero8

        def first_pass(r0, acc):
            taps = _taps_above(_win_above(x_ref, r0), K)
            y = _conv_taps(taps, w, b)
            d = d_refs[0][pl.ds(r0, T), :]
            for i in range(1, len(douts)):
                d = jnp.where(j >= starts[i], d_refs[i][pl.ds(r0, T), :], d)
            dy = d * _silu_grad(y)
            dy_s[pl.ds(r0, T), :] = dy
            return tuple(acc[k] + _fold8(dy * taps[k]) for k in range(K)) + (acc[K] + _fold8(dy),)

        acc = _chunk_loop(S, first_pass, tuple(zero8 for _ in range(K + 1)))
        for k in range(K):
            dw_ref[k:k + 1, :] = jnp.sum(acc[k], axis=0, keepdims=True)
        db_ref[...] = jnp.sum(acc[K], axis=0, keepdims=True)

        def second_pass(i, carry):
            r0 = pl.multiple_of(i * T, T)
            dx_ref[pl.ds(r0, T), :] = _conv_t_win(dy_s[pl.ds(r0, T + HALO), :], w).astype(BF16)
            return carry

        lax.fori_loop(0, S // T, second_pass, 0)

    col = pl.BlockSpec((S, LANES), lambda j: (0, j))
    wsp = pl.BlockSpec((K, LANES), lambda j: (0, j))
    bsp = pl.BlockSpec((1, LANES), lambda j: (0, j))

    def dspec(i):
        nblk = douts[i].shape[1] // LANES
        return pl.BlockSpec((S, LANES), lambda j: (0, jnp.clip(j - starts[i], 0, nblk - 1)))

    return _call(body, name="conv_silu_bwd", out_shape=(_sds((S, C), BF16), _sds((K, C)), _sds((1, C))), grid=(C // LANES,),
                 in_specs=[col, wsp, bsp] + [dspec(i) for i in range(len(douts))],
                 out_specs=(col, wsp, bsp), scratch=[pltpu.VMEM((S + HALO, LANES), F32)])(x, cw, cb, *douts)


HI = lax.Precision.HIGHEST


def _ssd_prep_fwd(proj, col0, bias_g, alog_g):
    S = proj.shape[0]
    nc = S // SSM_L
    b0 = col0 // LANES

    def body(raw_ref, b_ref, al_ref, pre_ref, dt_ref, acs_ref, acst_ref):
        r_i = lax.broadcasted_iota(jnp.int32, (LANES, LANES), 0)
        c_i = lax.broadcasted_iota(jnp.int32, (LANES, LANES), 1)
        live = c_i < SSM_R
        tril = jnp.where(r_i >= c_i, 1.0, 0.0)
        raw = raw_ref[...]
        for g in range(SSM_G):
            sel = jnp.where((r_i == SSM_R * g + c_i) & live, 1.0, 0.0)
            pre = jnp.dot(raw, sel, preferred_element_type=F32, precision=HI) + b_ref[g]
            dt = jnp.where(live, jnp.logaddexp(pre, 0.0), 0.0)
            a = dt * (-jnp.exp(al_ref[g]))
            acs = jnp.dot(tril, a, preferred_element_type=F32, precision=HI)
            pre_ref[g] = pre
            dt_ref[g] = dt
            acs_ref[g] = acs
            acst_ref[g] = acs.T

    gsp = pl.BlockSpec((SSM_G, 1, LANES), lambda c: (0, 0, 0))
    blk = pl.BlockSpec((SSM_G, SSM_L, LANES), lambda c: (0, c, 0))
    big = _sds((SSM_G, S, LANES))
    return _call(body, name="ssd_prep_fwd", out_shape=(big, big, big, _sds((SSM_G, LANES, S))), grid=(nc,),
                 in_specs=[pl.BlockSpec((SSM_L, LANES), lambda c: (c, b0)), gsp, gsp],
                 out_specs=(blk, blk, blk, pl.BlockSpec((SSM_G, LANES, SSM_L), lambda c: (0, 0, c))))(proj, bias_g, alog_g)


def _ssd_prep_bwd(pre_g, dt_g, alog_g, ddt_g, dacs_g, dacst_g):
    S = pre_g.shape[1]
    nc = S // SSM_L

    def body(pre_ref, dt_ref, al_ref, ddt_ref, dacs_ref, dacst_ref, draw_ref, db_ref, dal_ref):
        c = pl.program_id(0)
        r_i = lax.broadcasted_iota(jnp.int32, (LANES, LANES), 0)
        c_i = lax.broadcasted_iota(jnp.int32, (LANES, LANES), 1)
        live = c_i < SSM_R
        triu = jnp.where(r_i <= c_i, 1.0, 0.0)

        @pl.when(c == 0)
        def _():
            db_ref[...] = jnp.zeros_like(db_ref)
            dal_ref[...] = jnp.zeros_like(dal_ref)

        draw = jnp.zeros((SSM_L, LANES), F32)
        for g in range(SSM_G):
            dacs = dacs_ref[g] + dacst_ref[g].T
            da = jnp.dot(triu, dacs, preferred_element_type=F32, precision=HI)
            A = -jnp.exp(al_ref[g])
            ddt = ddt_ref[g] + da * A
            dpre = jnp.where(live, ddt * _sigmoid(pre_ref[g]), 0.0)
            unsel = jnp.where((c_i == SSM_R * g + r_i) & (r_i < SSM_R), 1.0, 0.0)
            draw = draw + jnp.dot(dpre, unsel, preferred_element_type=F32, precision=HI)
            db_ref[g] += jnp.sum(dpre, axis=0, keepdims=True)
            dal_ref[g] += jnp.where(live[:1], jnp.sum(da * dt_ref[g], axis=0, keepdims=True) * A, 0.0)
        draw_ref[...] = draw

    gsp = pl.BlockSpec((SSM_G, 1, LANES), lambda c: (0, 0, 0))
    blk = pl.BlockSpec((SSM_G, SSM_L, LANES), lambda c: (0, c, 0))
    gout = _sds((SSM_G, 1, LANES))
    return _call(body, name="ssd_prep_bwd", out_shape=(_sds((S, LANES)), gout, gout), grid=(nc,),
                 in_specs=[blk, blk, gsp, blk, blk, pl.BlockSpec((SSM_G, LANES, SSM_L), lambda c: (0, 0, c))],
                 out_specs=(pl.BlockSpec((SSM_L, LANES), lambda c: (c, 0)), gsp, gsp))(pre_g, dt_g, alog_g, ddt_g, dacs_g, dacst_g)


NT = (((1,), (1,)), ((), ()))
TN = (((0,), (0,)), ((), ()))
SSM_HP = SSM_R * SSM_P


def _ssd_group_terms(xs_ref, dt_ref, acs_ref, d_ref):
    hid = lax.broadcasted_iota(jnp.int32, (1, SSM_HP), 1) // SSM_P
    rid = lax.broadcasted_iota(jnp.int32, (SSM_HP, 1), 0) // SSM_P

    def widen(cols):
        out = cols[0]
        for r in range(1, SSM_R):
            out = jnp.where(hid == r, cols[r], out)
        return out

    dt_c = [dt_ref[:, r:r + 1] for r in range(SSM_R)]
    acs_c = [acs_ref[:, r:r + 1] for r in range(SSM_R)]
    last = [acs_ref[SSM_L - 1:SSM_L, r:r + 1] for r in range(SSM_R)]
    decay_c = [jnp.exp(last[r] - acs_c[r]) for r in range(SSM_R)]
    cd = [jnp.exp(last[r]) for r in range(SSM_R)]
    cd_rows = cd[0]
    for r in range(1, SSM_R):
        cd_rows = jnp.where(rid == r, cd[r], cd_rows)
    xs = xs_ref[...]
    return (xs, xs * widen(dt_c), widen([jnp.exp(a) for a in acs_c]), widen(decay_c),
            widen([d_ref[:, r:r + 1] for r in range(SSM_R)]), cd_rows, dt_c, decay_c, cd)


def _ssd_lmat(acs_ref, acst_ref, r, tril):
    return jnp.exp(jnp.where(tril, acs_ref[:, r:r + 1] - acst_ref[r:r + 1, :], -jnp.inf))


SSM_GPS = 2


def _ssd_specs(rev, nc):
    def cc(c):
        return nc - 1 - c if rev else c
    xs_blocks = (SSM_G * SSM_HP) // (SSM_GPS * SSM_N)
    xs = pl.BlockSpec((SSM_L, SSM_GPS * SSM_HP), lambda g, c: (cc(c), g))
    bsp = pl.BlockSpec((SSM_L, SSM_GPS * SSM_N), lambda g, c: (cc(c), xs_blocks + g))
    csp = pl.BlockSpec((SSM_L, SSM_GPS * SSM_N), lambda g, c: (cc(c), xs_blocks + SSM_G // SSM_GPS + g))
    sc = pl.BlockSpec((SSM_GPS, SSM_L, LANES), lambda g, c: (g, cc(c), 0))
    sct = pl.BlockSpec((SSM_GPS, LANES, SSM_L), lambda g, c: (g, 0, cc(c)))
    gsp = pl.BlockSpec((SSM_GPS, 1, LANES), lambda g, c: (g, 0, 0))
    st = pl.BlockSpec((None, SSM_GPS, SSM_HP, SSM_N), lambda g, c: (cc(c), g, 0, 0))
    return xs, bsp, csp, sc, sct, gsp, st


def _interleave(gens):
    live = list(gens)
    while live:
        for g in list(live):
            try:
                next(g)
            except StopIteration:
                live.remove(g)


def _rounds(gens):
    live = list(gens)
    while live:
        for g in list(live):
            try:
                next(g)
            except StopIteration:
                live.remove(g)
        yield


def _ssd_group_views(gg, xs_ref, b_ref, c_ref, *per_group):
    return (xs_ref.at[:, gg * SSM_HP:(gg + 1) * SSM_HP], b_ref.at[:, gg * SSM_N:(gg + 1) * SSM_N],
            c_ref.at[:, gg * SSM_N:(gg + 1) * SSM_N]) + tuple(r.at[gg] for r in per_group)


def _ssd_fwd(xbc, dt_g, acs_g, acst_g, d_g):
    S = xbc.shape[0]
    nc = S // SSM_L
    xs_s, b_s, c_s, sc, sct, gsp, st = _ssd_specs(False, nc)

    def body(xs_ref, b_ref, c_ref, dt_ref, acs_ref, acst_ref, d_ref, y_ref, st_ref, state):
        c = pl.program_id(1)

        @pl.when(c == 0)
        def _():
            state[...] = jnp.zeros_like(state)

        tril = lax.broadcasted_iota(jnp.int32, (SSM_L, SSM_L), 0) >= lax.broadcasted_iota(jnp.int32, (SSM_L, SSM_L), 1)
        def group(gg):
            xs_v, b_v, c_v, dt_v, acs_v, acst_v, d_v, st_v, state_v = _ssd_group_views(
                gg, xs_ref, b_ref, c_ref, dt_ref, acs_ref, acst_ref, d_ref, st_ref, state)
            y_v = y_ref.at[:, gg * SSM_HP:(gg + 1) * SSM_HP]
            Bb, Cb = b_v[...].astype(BF16), c_v[...].astype(BF16)
            Gm = lax.dot_general(Cb, Bb, NT, preferred_element_type=F32)
            yield
            xs, X, e_all, decay_all, d_all, cd_rows, _, _, _ = _ssd_group_terms(xs_v, dt_v, acs_v, d_v)
            S_all = state_v[...]
            st_v[...] = S_all
            yield
            yo = lax.dot_general(Cb, S_all.astype(BF16), NT, preferred_element_type=F32)
            new_state = lax.dot_general((X * decay_all).astype(BF16), Bb, TN, preferred_element_type=F32)
            yield
            y_v[...] = e_all * yo + d_all * xs
            state_v[...] = S_all * cd_rows + new_state
            for r in range(SSM_R):
                yield
                sl = slice(r * SSM_P, (r + 1) * SSM_P)
                M = Gm * _ssd_lmat(acs_v, acst_v, r, tril)
                yield
                y_v[:, sl] += jnp.dot(M.astype(BF16), X[:, sl].astype(BF16), preferred_element_type=F32)

        _interleave([group(gg) for gg in range(SSM_GPS)])

    return _call(body, name="ssd_fwd",
                 out_shape=(_sds((S, SSM_G * SSM_HP)), _sds((nc, SSM_G, SSM_HP, SSM_N))),
                 grid=(SSM_G // SSM_GPS, nc), in_specs=[xs_s, b_s, c_s, sc, sc, sct, gsp],
                 out_specs=(xs_s, pl.BlockSpec((None, SSM_GPS, SSM_HP, SSM_N), lambda g, c: (c, g, 0, 0))),
                 scratch=[pltpu.VMEM((SSM_GPS, SSM_HP, SSM_N), F32)])(xbc, xbc, xbc, dt_g, acs_g, acst_g, d_g)


def _ssd_bwd(xbc, dt_g, acs_g, acst_g, d_g, states, dy):
    S = xbc.shape[0]
    nc = S // SSM_L
    xs_s, b_s, c_s, sc, sct, gsp, st = _ssd_specs(True, nc)
    bc_out = pl.BlockSpec((SSM_L, SSM_GPS * SSM_N), lambda g, c: (nc - 1 - c, g))

    def body(xs_ref, b_ref, c_ref, dt_ref, acs_ref, acst_ref, d_ref, st_ref, dy_ref,
             dxs_ref, db_ref, dc_ref, ddt_ref, dacs_ref, dacst_ref, dd_ref, dstate):
        c = pl.program_id(1)

        @pl.when(c == 0)
        def _():
            dstate[...] = jnp.zeros_like(dstate)
            dd_ref[...] = jnp.zeros_like(dd_ref)

        tril = lax.broadcasted_iota(jnp.int32, (SSM_L, SSM_L), 0) >= lax.broadcasted_iota(jnp.int32, (SSM_L, SSM_L), 1)
        lane = lax.broadcasted_iota(jnp.int32, (1, LANES), 1)
        subl = lax.broadcasted_iota(jnp.int32, (LANES, 1), 0)
        last_row = lax.broadcasted_iota(jnp.int32, (SSM_L, 1), 0) == SSM_L - 1
        triu = lax.broadcasted_iota(jnp.int32, (SSM_L, SSM_L), 0) <= lax.broadcasted_iota(jnp.int32, (SSM_L, SSM_L), 1)
        def group(gg):
            xs_v, b_v, c_v, dt_v, acs_v, acst_v, d_v, st_v, ddt_v, dacs_v, dacst_v, dd_v, dstate_v = _ssd_group_views(
                gg, xs_ref, b_ref, c_ref, dt_ref, acs_ref, acst_ref, d_ref, st_ref, ddt_ref, dacs_ref, dacst_ref, dd_ref, dstate)
            dy_v, dxs_v = (r.at[:, gg * SSM_HP:(gg + 1) * SSM_HP] for r in (dy_ref, dxs_ref))
            db_v, dc_v = (r.at[:, gg * SSM_N:(gg + 1) * SSM_N] for r in (db_ref, dc_ref))
            Bb, Cb = b_v[...].astype(BF16), c_v[...].astype(BF16)
            Gm = lax.dot_general(Cb, Bb, NT, preferred_element_type=F32)
            GmT = lax.dot_general(Bb, Cb, NT, preferred_element_type=F32)
            yield
            xs, X, e_all, decay_all, d_all, cd_rows, dt_c, decay_c, cd = _ssd_group_terms(xs_v, dt_v, acs_v, d_v)
            S_all, dSn_all, dY = st_v[...], dstate_v[...], dy_v[...]
            Sb, dSnb = S_all.astype(BF16), dSn_all.astype(BF16)
            yield
            T = lax.dot_general(Cb, Sb, NT, preferred_element_type=F32)
            dT = (dY * e_all).astype(BF16)
            dC = jnp.dot(dT, Sb, preferred_element_type=F32)
            dS_prev = lax.dot_general(dT, Cb, TN, preferred_element_type=F32)
            yield
            yo_dy = dY * (e_all * T)
            W = lax.dot_general(Bb, dSnb, NT, preferred_element_type=F32)
            dB = jnp.dot((X * decay_all).astype(BF16), dSnb, preferred_element_type=F32)
            yield
            xw = X * W
            dcd_rows = jnp.sum(dSn_all * S_all, axis=1, keepdims=True)
            dstate_v[...] = dS_prev + dSn_all * cd_rows
            dX_state = W * decay_all
            yield
            acc = dict(dG=jnp.zeros((SSM_L, SSM_L), F32), dGT=jnp.zeros((SSM_L, SSM_L), F32),
                       ddt=jnp.zeros((SSM_L, LANES), F32), dacs=jnp.zeros((SSM_L, LANES), F32),
                       dacst=jnp.zeros((LANES, SSM_L), F32), dd=jnp.zeros((1, LANES), F32))

            def head(r):
                sl = slice(r * SSM_P, (r + 1) * SSM_P)
                Lm = _ssd_lmat(acs_v, acst_v, r, tril)
                LmT = jnp.exp(jnp.where(triu, acst_v[r:r + 1, :] - acs_v[:, r:r + 1], -jnp.inf))
                M = Gm * Lm
                yield
                dYh, xs_h = dY[:, sl], xs[:, sl]
                dYb, Xb = dYh.astype(BF16), X[:, sl].astype(BF16)
                dM = lax.dot_general(dYb, Xb, NT, preferred_element_type=F32)
                yield
                dX = jnp.dot((GmT * LmT).astype(BF16), dYb, preferred_element_type=F32) + dX_state[:, sl]
                acc["dG"] = acc["dG"] + dM * Lm
                acc["dGT"] = acc["dGT"] + lax.dot_general(Xb, dYb, NT, preferred_element_type=F32) * LmT
                yield
                dseg = dM * M
                dd = jnp.sum(xw[:, sl], axis=1, keepdims=True) * decay_c[r]
                dcd = jnp.sum(dcd_rows[sl])
                dacs_col = (jnp.sum(dseg, axis=1, keepdims=True) + jnp.sum(yo_dy[:, sl], axis=1, keepdims=True) - dd
                            + jnp.where(last_row, dcd * cd[r] + jnp.sum(dd), 0.0))
                dacs_row = -jnp.sum(dseg, axis=0, keepdims=True)
                yield
                dxs_v[:, sl] = dX * dt_c[r] + d_all[:, sl] * dYh
                acc["ddt"] = acc["ddt"] + jnp.where(lane == r, jnp.sum(dX * xs_h, axis=1, keepdims=True), 0.0)
                acc["dacs"] = acc["dacs"] + jnp.where(lane == r, dacs_col, 0.0)
                acc["dacst"] = acc["dacst"] + jnp.where(subl == r, dacs_row, 0.0)
                acc["dd"] = acc["dd"] + jnp.where(lane == r, jnp.sum(dYh * xs_h), 0.0)

            yield from _rounds([head(r) for r in range(SSM_R)])
            dc_v[...] = dC + jnp.dot(acc["dG"].astype(BF16), Bb, preferred_element_type=F32)
            db_v[...] = dB + jnp.dot(acc["dGT"].astype(BF16), Cb, preferred_element_type=F32)
            ddt_v[...] = acc["ddt"]
            dacs_v[...] = acc["dacs"]
            dacst_v[...] = acc["dacst"]
            dd_v[...] += acc["dd"]

        _interleave([group(gg) for gg in range(SSM_GPS)])

    big = _sds((SSM_G, S, LANES))
    return _call(body, name="ssd_bwd",
                 out_shape=(_sds((S, SSM_G * SSM_HP)), _sds((S, SSM_G * SSM_N)), _sds((S, SSM_G * SSM_N)),
                            big, big, _sds((SSM_G, LANES, S)), _sds((SSM_G, 1, LANES))),
                 grid=(SSM_G // SSM_GPS, nc), in_specs=[xs_s, b_s, c_s, sc, sc, sct, gsp, st, xs_s],
                 out_specs=(xs_s, bc_out, bc_out, sc, sc, sct, gsp),
                 scratch=[pltpu.VMEM((SSM_GPS, SSM_HP, SSM_N), F32)])(xbc, xbc, xbc, dt_g, acs_g, acst_g, d_g, states, dy)


def _gate_norm_fwd(y, proj, w):
    S, DI = y.shape
    tm = _tile(S, 256)

    def body(y_ref, z_ref, w_ref, o_ref):
        z = z_ref[...]
        gn = y_ref[...] * (z * _sigmoid(z))
        r = lax.rsqrt(jnp.mean(gn * gn, axis=-1, keepdims=True) + SSM_NORM_EPS)
        o_ref[...] = (gn * r * w_ref[...]).astype(BF16)

    row = pl.BlockSpec((tm, DI), lambda i: (i, 0))
    return _call(body, name="gate_norm_fwd", out_shape=_sds((S, DI), BF16), grid=(S // tm,),
                 in_specs=[row, row, pl.BlockSpec((1, DI), lambda i: (0, 0))], out_specs=row)(y, proj, w)


def _gate_norm_bwd(y, proj, w, dout):
    S, DI = y.shape
    tm = _tile(S, 256)

    def body(y_ref, z_ref, w_ref, d_ref, dy_ref, dz_ref, dw_ref):
        z, yv = z_ref[...], y_ref[...]
        sz = z * _sigmoid(z)
        dgn, dw = _norm_bwd_math(yv * sz, w_ref[...], d_ref[...].astype(F32), SSM_NORM_EPS)
        dy_ref[...] = dgn * sz
        dz_ref[...] = (dgn * yv * _silu_grad(z)).astype(BF16)

        @pl.when(pl.program_id(0) == 0)
        def _():
            dw_ref[...] = jnp.zeros_like(dw_ref)

        dw_ref[...] += dw

    row = pl.BlockSpec((tm, DI), lambda i: (i, 0))
    vec = pl.BlockSpec((1, DI), lambda i: (0, 0))
    return _call(body, name="gate_norm_bwd", out_shape=(_sds((S, DI)), _sds((S, DI), BF16), _sds((1, DI))), grid=(S // tm,),
                 in_specs=[row, row, vec, row], out_specs=(row, row, vec))(y, proj, w, dout)


def _group_major(v):
    return jnp.pad(v.reshape(SSM_G, 1, SSM_R), ((0, 0), (0, 0), (0, LANES - SSM_R)))


def _ungroup(t):
    return t[:, :SSM_R].reshape(1, SSM_G * SSM_R)


def _ffn_fwd(x, P, l, need):
    need(f"ffn{l}_up", x)
    h = _rmsnorm(x, P["norm_ffn"][l:l + 1], name=f"ffn{l}_norm")
    wT = P[f"ffn_w_upT{l}"]
    F = wT.shape[0] // 2
    a = _mm(h, wT, tb=True, name=f"ffn{l}_up")
    need(f"ffn{l}_down", a)
    act = _ffn_mid_fwd(a, P["ffn_conv_w"], P["ffn_conv_b"], l)
    out = _mm(act, P[f"ffn_w_down{l}"], res=x, name=f"ffn{l}_down")
    return out, (x, h, a, act)


def _ffn_bwd(saved, P, l, dx, emit):
    x, h, a, act = saved
    wT = P[f"ffn_w_upT{l}"]
    F = wT.shape[0] // 2
    dact = _mm(dx, P[f"ffn_w_down{l}"], tb=True, out_dtype=BF16, name=f"ffn{l}_down_dx")
    dw_down = _mm(act, dx, ta=True, out_dtype=PAYLOAD, name=f"ffn{l}_down_dw")
    dau, dag, dcw, dcb = _ffn_mid_bwd(a, P["ffn_conv_w"], P["ffn_conv_b"], l, dact)
    dw_upT = _mm(dau, h, ta=True, out_dtype=PAYLOAD, out_rows=(2 * F, 0, None), name=f"ffn{l}_up_u_dw")
    dw_upT = _mm(dag, h, ta=True, out_dtype=PAYLOAD, out_rows=(2 * F, F, dw_upT), name=f"ffn{l}_up_g_dw")
    tie = emit(f"ffn{l}", {"ffn_w_upT": dw_upT, "ffn_w_down": dw_down})
    dh = _mm(dau, wT, b_rows=(0, F), name=f"ffn{l}_up_u_dx")
    dh = _mm(dag, wT, b_rows=(F, F), res=dh, name=f"ffn{l}_up_g_dx")
    dx_in, dnw = _rmsnorm_bwd(x, P["norm_ffn"][l:l + 1], dh, dx, name=f"ffn{l}_norm_bwd", after=tie)
    return dx_in, dnw, dcw, dcb


def _local_step(x, positions, target, P, need, emit, after=None):
    S, D = x.shape
    inv_freq = ROPE_THETA ** (-jnp.arange(0, HEAD_DIM, 2, dtype=F32) / HEAD_DIM)
    inv_freq = jnp.tile(inv_freq, LANES // (HEAD_DIM // 2)).reshape(1, LANES)
    cos, sin = _rope_tables(positions, inv_freq)

    nm0 = P["norm_mix"][0:1]
    h0 = _rmsnorm(x, nm0, name="mix_norm", after=after)
    need("mix_in", h0)
    proj0 = _mm(h0, P["mix_w_inT"], tb=True, name="mix_in")
    cat0 = _attn_fwd(proj0, cos, sin, P["attn_sinks"], _pool_fwd(proj0, P["pool_w"][0], P["pool_scale"]))
    need("mix_out", cat0)
    x1 = _mm(cat0, P["mix_w_out"], res=x, name="mix_out")
    x2, ffn0 = _ffn_fwd(x1, P, 0, need)

    nm1 = P["norm_mix"][1:2]
    need("ssm", x2)
    h1 = _rmsnorm(x2, nm1, name="ssm_norm_in")
    w1T, wdtT = P["ssm_w_inT"], P["ssm_wdtT"]
    DI, CD, NH = P["ssm_norm"].shape[1], P["ssm_conv_w"].shape[1], P["ssm_dt_bias"].shape[1]
    z = _mm(h1, w1T, tb=True, b_rows=(0, DI), name="ssm_in_z")
    xbcp = _mm(h1, w1T, tb=True, b_rows=(DI, CD), name="ssm_in_xbc")
    dtraw = _mm(h1, wdtT, tb=True, name="ssm_in_dt")
    xbc = _conv_silu_fwd(xbcp, P["ssm_conv_w"], P["ssm_conv_b"])
    bias_g, alog_g, d_g = _group_major(P["ssm_dt_bias"]), _group_major(P["ssm_A_log"]), _group_major(P["ssm_D"])
    pre_g, dt_g, acs_g, acst_g = _ssd_prep_fwd(dtraw, 0, bias_g, alog_g)
    y, states = _ssd_fwd(xbc, dt_g, acs_g, acst_g, d_g)
    yn = _gate_norm_fwd(y, z, P["ssm_norm"])
    x3 = _mm(yn, P["ssm_w_out"], res=x2, name="ssm_out")
    x4, ffn1 = _ffn_fwd(x3, P, 1, need)

    loss, dx, d_norm_final = _final_loss(x4, P["norm_final"].reshape(1, D), target, name="final_loss")
    dx, dnf1, dcw1, dcb1 = _ffn_bwd(ffn1, P, 1, dx, emit)
    dyn = _mm(dx, P["ssm_w_out"], tb=True, out_dtype=BF16, name="ssm_out_dx")
    d_w_out1 = _mm(yn, dx, ta=True, out_dtype=PAYLOAD, name="ssm_out_dw")
    dy, dz, d_ssm_norm = _gate_norm_bwd(y, z, P["ssm_norm"], dyn)
    dxs, dB, dC, ddt_g, dacs_g, dacst_g, dd_g = _ssd_bwd(xbc, dt_g, acs_g, acst_g, d_g, states, dy)
    draw, dbias_g, dalog_g = _ssd_prep_bwd(pre_g, dt_g, alog_g, ddt_g, dacs_g, dacst_g)
    dxbc, d_conv_w1, d_conv_b1 = _conv_silu_bwd(xbcp, P["ssm_conv_w"], P["ssm_conv_b"], [dxs, dB, dC])
    rows = DI + CD + NH
    d_w1T = _mm(dz, h1, ta=True, out_dtype=PAYLOAD, out_rows=(rows, 0, None), name="ssm_in_z_dw")
    d_w1T = _mm(dxbc, h1, ta=True, out_dtype=PAYLOAD, out_rows=(rows, DI, d_w1T), name="ssm_in_xbc_dw")
    d_w1T = _mm(draw[:, :NH], h1, ta=True, out_dtype=PAYLOAD, out_rows=(rows, DI + CD, d_w1T), name="ssm_in_dt_dw")
    tie = emit("ssm", {"ssm_w_inT": d_w1T, "ssm_w_out": d_w_out1,
                       "ssm_conv_w": d_conv_w1, "ssm_conv_b": d_conv_b1, "ssm_norm": d_ssm_norm})
    dh1 = _mm(dz, w1T, b_rows=(0, DI), name="ssm_in_z_dx")
    dh1 = _mm(dxbc, w1T, b_rows=(DI, CD), res=dh1, name="ssm_in_xbc_dx")
    dh1 = _mm(draw, wdtT, res=dh1, name="ssm_in_dt_dx")
    dx, dnm1 = _rmsnorm_bwd(x2, nm1, dh1, dx, name="ssm_norm_in_bwd", after=tie)
    dx, dnf0, dcw0, dcb0 = _ffn_bwd(ffn0, P, 0, dx, emit)
    dcat = _mm(dx, P["mix_w_out"], tb=True, name="mix_out_dx")
    d_w_out0 = _mm(cat0, dx, ta=True, out_dtype=PAYLOAD, name="mix_out_dw")
    dproj0, dsk = _attn_bwd(proj0, cos, sin, P["attn_sinks"], dcat)
    dproj0, d_pool_w, d_pool_scale = _pool_bwd(proj0, P["pool_w"][0], P["pool_scale"], dcat, dproj0)
    d_w_in0 = _mm(dproj0, h0, ta=True, out_dtype=PAYLOAD, name="mix_in_dw")
    tie = emit("mix", {"mix_w_inT": d_w_in0, "mix_w_out": d_w_out0, "ffn_conv_w": jnp.stack([dcw0, dcw1])})
    dh0 = _mm(dproj0, P["mix_w_inT"], name="mix_in_dx")
    grad_x, dnm0 = _rmsnorm_bwd(x, nm0, dh0, dx, name="mix_norm_bwd", after=tie)

    small = {
        "norm_mix": jnp.concatenate([dnm0, dnm1], axis=0),
        "norm_ffn": jnp.concatenate([dnf0, dnf1], axis=0),
        "norm_final": d_norm_final,
        "pool_w": d_pool_w,
        "pool_scale": d_pool_scale,
        "attn_sinks_rows": dsk,
        "ssm_dt_bias_g": dbias_g, "ssm_A_log_g": dalog_g, "ssm_D_g": dd_g,
        "ffn_conv_b": jnp.concatenate([dcb0, dcb1], axis=0),
    }
    return loss, grad_x, small


def _peer(k):
    x, y, c = lax.axis_index("x"), lax.axis_index("y"), lax.axis_index("c")
    px = 1 - x if k & 4 else x
    py = 1 - y if k & 2 else y
    pc = 1 - c if k & 1 else c
    return (px, py, pc), 4 * px + 2 * py + pc


def _my_index():
    return 4 * lax.axis_index("x") + 2 * lax.axis_index("y") + lax.axis_index("c")


def _land_sds(a, mode, gather):
    if mode == "slab":
        return _sds(((N_DEV,) + a.shape) if gather else a.shape, a.dtype)
    assert mode == "rows", mode
    return _sds((N_DEV * a.shape[0],) + a.shape[1:] if gather else (N_DEV, a.shape[0] // N_DEV) + a.shape[1:], a.dtype)


def _part(ref, mode, shape, idx):
    if mode == "slab":
        return ref.at[idx]
    r = shape[0] // N_DEV
    return ref.at[pl.ds(idx * r, r)]


def _own_copies(ops, gather, srcs, lands, sems):
    me = _my_index()
    out = []
    for i, (a, mode) in enumerate(ops):
        s = srcs[i] if gather else _part(srcs[i], mode, a.shape, me)
        d = _part(lands[i], mode, _land_sds(a, mode, gather).shape, me) if gather else lands[i].at[me]
        out.append(pltpu.make_async_copy(s, d, sems.at[i]))
    return out


def _remote_copies(ops, gather, srcs, lands, send_sems, recv_sems):
    me = _my_index()
    n = len(ops)
    out = []
    for k in range(1, N_DEV):
        dev, idx = _peer(k)
        for i, (a, mode) in enumerate(ops):
            s = srcs[i] if gather else _part(srcs[i], mode, a.shape, idx)
            d = _part(lands[i], mode, _land_sds(a, mode, gather).shape, me) if gather else lands[i].at[me]
            out.append(pltpu.make_async_remote_copy(src_ref=s, dst_ref=d, send_sem=send_sems.at[(k - 1) * n + i],
                                                    recv_sem=recv_sems.at[(k - 1) * n + i], device_id=dev,
                                                    device_id_type=pl.DeviceIdType.MESH))
    return out


def _exchange(ops, *, gather, name):
    n = len(ops)

    def body(*refs):
        ins, outs = refs[:n], refs[n:2 * n]
        send_sems, recv_sems, local_sems = refs[2 * n:]
        copies = _own_copies(ops, gather, ins, outs, local_sems) + _remote_copies(ops, gather, ins, outs, send_sems, recv_sems)
        for cp in copies:
            cp.start()
        for cp in copies:
            cp.wait()

    return pl.pallas_call(
        body, name=name, out_shape=[_land_sds(a, m, gather) for a, m in ops], in_specs=[ANY] * n, out_specs=[ANY] * n,
        scratch_shapes=[pltpu.SemaphoreType.DMA((n * (N_DEV - 1),)), pltpu.SemaphoreType.DMA((n * (N_DEV - 1),)),
                        pltpu.SemaphoreType.DMA((n,))],
    )(*[a for a, _ in ops])


HBM = pl.BlockSpec(memory_space=pltpu.HBM)
SEM = pl.BlockSpec(memory_space=pltpu.SEMAPHORE)
SIDE_EFFECT = pltpu.SideEffectType.DATAFLOW_SIDE_EFFECTING


def _in_hbm(a):
    return pltpu.with_memory_space_constraint(a, pltpu.HBM)


def _place_own(ops, *, gather, name):
    n = len(ops)

    def zeros(k):
        return (0,) * k

    in_specs, out_specs = [], []
    for a, mode in ops:
        nd = a.ndim
        if gather and mode == "slab":
            in_specs.append(pl.BlockSpec(a.shape, lambda i, nd=nd: zeros(nd)))
            out_specs.append(pl.BlockSpec((1,) + a.shape, lambda i, nd=nd: (_my_index(),) + zeros(nd)))
        elif gather:
            in_specs.append(pl.BlockSpec(a.shape, lambda i, nd=nd: zeros(nd)))
            out_specs.append(pl.BlockSpec(a.shape, lambda i, nd=nd: (_my_index(),) + zeros(nd - 1)))
        elif mode == "slab":
            in_specs.append(pl.BlockSpec((1,) + a.shape[1:], lambda i, nd=nd: (_my_index(),) + zeros(nd - 1)))
            out_specs.append(pl.BlockSpec((1,) + a.shape[1:], lambda i, nd=nd: (_my_index(),) + zeros(nd - 1)))
        else:
            r = a.shape[0] // N_DEV
            in_specs.append(pl.BlockSpec((r,) + a.shape[1:], lambda i, nd=nd: (_my_index(),) + zeros(nd - 1)))
            out_specs.append(pl.BlockSpec((1, r) + a.shape[1:], lambda i, nd=nd: (_my_index(),) + zeros(nd)))

    def body(*refs):
        for i_ref, o_ref in zip(refs[:n], refs[n:2 * n]):
            if o_ref.shape == i_ref.shape:
                o_ref[...] = i_ref[...]
            else:
                o_ref[0] = i_ref[...]

    outs = _call(body, name=name, grid=(1,), in_specs=in_specs, out_specs=out_specs + [ANY] * n,
                 out_shape=[_land_sds(a, m, gather) for a, m in ops] + [_sds(a.shape, a.dtype) for a, _ in ops],
                 aliases={i: n + i for i in range(n)})(*[a for a, _ in ops])
    return outs[:n], [(src, m) for src, (_, m) in zip(outs[n:], ops)]


def _exchange_start(groups, *, gather, name):
    sizes = [len(ops) for ops, _ in groups]
    n = sum(sizes)
    G = len(groups)

    def body(*refs):
        srcs, lands = refs[:n], refs[n:2 * n]
        sems = refs[2 * n:2 * n + 2 * G]
        token = refs[-1]
        off = 0
        for g, (ops, _) in enumerate(groups):
            for cp in _remote_copies(ops, gather, srcs[off:off + sizes[g]], lands[off:off + sizes[g]], sems[2 * g], sems[2 * g + 1]):
                cp.start()
            off += sizes[g]
        token[...] = jnp.zeros_like(token)

    srcs = [a for ops, _ in groups for a, _ in ops]
    lands = [l for _, ls in groups for l in ls]
    sem_shapes = [pltpu.SemaphoreType.DMA((s * (N_DEV - 1),)) for s in sizes for _ in range(2)]
    outs = pl.pallas_call(
        body, name=name,
        out_shape=sem_shapes + [pltpu.HBM(a.shape, a.dtype) for a in srcs + lands] + [_sds((8, LANES))],
        in_specs=[HBM] * (2 * n), out_specs=[SEM] * (2 * G) + [HBM] * (2 * n) + [pl.BlockSpec(memory_space=pltpu.VMEM)],
        input_output_aliases={i: 2 * G + i for i in range(2 * n)},
        compiler_params=pltpu.CompilerParams(has_side_effects=SIDE_EFFECT))(*[_in_hbm(a) for a in srcs + lands])
    sems, thru, token = outs[:2 * G], outs[2 * G:2 * G + 2 * n], outs[-1]
    states, off = [], 0
    for g, s in enumerate(sizes):
        states.append((sems[2 * g], sems[2 * g + 1], thru[off:off + s], thru[n + off:n + off + s]))
        off += s
    return states, token


def _exchange_wait(ops, state, after, *, gather, name):
    send_sems, recv_sems, srcs, lands = state
    n = len(ops)

    def body(*refs):
        for cp in _remote_copies(ops, gather, refs[:n], refs[n:2 * n], refs[2 * n], refs[2 * n + 1]):
            cp.wait_send()
            cp.wait_recv()

    outs = pl.pallas_call(
        body, name=name, out_shape=[pltpu.HBM(a.shape, a.dtype) for a in list(srcs) + list(lands)],
        in_specs=[HBM] * (2 * n) + [SEM, SEM, ANY], out_specs=[HBM] * (2 * n),
        input_output_aliases={i: i for i in range(2 * n)},
        compiler_params=pltpu.CompilerParams(has_side_effects=SIDE_EFFECT))(*srcs, *lands, send_sems, recv_sems, after)
    return outs[n:]


ADAM_ROWS = 256


def _row_tile(R, cap=ADAM_ROWS):
    best = R
    if R > cap:
        for d in range(16, cap + 1, 16):
            if R % d == 0:
                best = d
    return best


def _adamw(g_layers, w, m, v, *, name):
    L = len(g_layers)
    J, R, Wd = g_layers[0].shape
    assert w.shape == (L, R, Wd), (g_layers[0].shape, w.shape)
    tr = _row_tile(R)
    nrt = R // tr
    c1 = 1.0 / (1.0 - ADAM_B1 ** ADAM_STEP)
    c2 = 1.0 / (1.0 - ADAM_B2 ** ADAM_STEP)

    def body(*refs):
        g_refs = refs[:L]
        w_ref, m_ref, v_ref, go_ref, d_ref, mo_ref, vo_ref = refs[L:]
        layer = pl.program_id(0)
        g = None
        for l, g_ref in enumerate(g_refs):
            gl = g_ref[0].astype(F32)
            for j in range(1, J):
                gl = gl + g_ref[j].astype(F32)
            g = gl if g is None else jnp.where(layer == l, gl, g)
        mn = ADAM_B1 * m_ref[...] + (1.0 - ADAM_B1) * g
        vn = ADAM_B2 * v_ref[...] + (1.0 - ADAM_B2) * (g * g)
        go_ref[...] = g
        mo_ref[...] = mn
        vo_ref[...] = vn
        d_ref[...] = -ADAM_LR * ((mn * c1) / (jnp.sqrt(vn * c2) + ADAM_EPS) + ADAM_WD * w_ref[...])

    def g_spec(l):
        return pl.BlockSpec((J, tr, Wd), lambda ll, i: (0, jnp.where(ll == l, i, jnp.where(ll < l, 0, nrt - 1)), 0))

    row = pl.BlockSpec((None, tr, Wd), lambda ll, i: (ll, i, 0))
    out = _sds((L, R, Wd))
    return _call(body, name=name, out_shape=(out, out, out, out), grid=(L, nrt),
                 in_specs=[g_spec(l) for l in range(L)] + [row, row, row], out_specs=(row, row, row, row))(*g_layers, w, m, v)


def _sum_slabs(slabs, *, name):
    n = len(slabs)

    def body(*refs):
        for g_ref, o_ref in zip(refs[:n], refs[n:]):
            g = g_ref[0]
            for j in range(1, g_ref.shape[0]):
                g = g + g_ref[j]
            o_ref[...] = g

    return _call(body, name=name, out_shape=[_sds(s.shape[1:]) for s in slabs])(*slabs)


def kernel(x, positions, norm_mix, norm_ffn, norm_final, mix_w_in, pool_w, pool_scale, attn_sinks, mix_w_out, ssm_w_in, ssm_conv_w, ssm_conv_b, ssm_dt_bias, ssm_A_log, ssm_D, ssm_norm, ssm_w_out, ffn_w_up, ffn_conv_w, ffn_conv_b, ffn_w_down, loss_target, m_norm_mix, m_norm_ffn, m_norm_final, m_mix_w_in, m_pool_w, m_pool_scale, m_attn_sinks, m_mix_w_out, m_ssm_w_in, m_ssm_conv_w, m_ssm_conv_b, m_ssm_dt_bias, m_ssm_A_log, m_ssm_D, m_ssm_norm, m_ssm_w_out, m_ffn_w_up, m_ffn_conv_w, m_ffn_conv_b, m_ffn_w_down, v_norm_mix, v_norm_ffn, v_norm_final, v_mix_w_in, v_pool_w, v_pool_scale, v_attn_sinks, v_mix_w_out, v_ssm_w_in, v_ssm_conv_w, v_ssm_conv_b, v_ssm_dt_bias, v_ssm_A_log, v_ssm_D, v_ssm_norm, v_ssm_w_out, v_ffn_w_up, v_ffn_conv_w, v_ffn_conv_b, v_ffn_w_down):
    args = dict(locals())
    wl = {n: args[n] for n in WEIGHTS}
    ml = {n: args["m_" + n] for n in WEIGHTS}
    vl = {n: args["v_" + n] for n in WEIGHTS}
    F = ffn_w_down.shape[1] * N_DEV
    DI, CD, NH = ssm_norm.shape[1] * N_DEV, ssm_conv_b.shape[1] * N_DEV, ssm_dt_bias.shape[1]
    Kc, Kf = ssm_conv_w.shape[1], ffn_conv_w.shape[1]
    n_up = ffn_w_up.shape[2]
    col_sharded = ("mix_w_in", "ssm_w_in", "ffn_w_up")

    def tr(a):
        return jnp.swapaxes(a, -1, -2)

    def two(a):
        return a.reshape(-1, a.shape[-1])

    def pay(a):
        return a.astype(PAYLOAD)

    order = ("mix_in", "mix_out", "ffn0_up", "ffn0_down", "ssm", "ffn1_up", "ffn1_down")
    gops = {
        "mix_in": [(pay(tr(mix_w_in)[0]), "rows")],
        "mix_out": [(pay(mix_w_out[0]), "rows"), (two(ssm_conv_w), "slab"), (ssm_conv_b, "slab"), (ssm_norm, "slab"),
                    (two(ffn_conv_w), "slab")],
        "ffn0_up": [(pay(tr(ffn_w_up)[0]), "rows")], "ffn0_down": [(pay(ffn_w_down[0]), "rows")],
        "ssm": [(pay(tr(ssm_w_in)[0]), "slab"), (pay(ssm_w_out[0]), "rows")],
        "ffn1_up": [(pay(tr(ffn_w_up)[1]), "rows")], "ffn1_down": [(pay(ffn_w_down[1]), "rows")],
    }
    lands, handed = _place_own([op for g in order for op in gops[g]], gather=True, name="gather_own")
    groups, off = [], 0
    for g in order:
        gops[g] = handed[off:off + len(gops[g])]
        groups.append((gops[g], lands[off:off + len(gops[g])]))
        off += len(gops[g])
    gstates, token = _exchange_start(groups, gather=True, name="gather_start")
    gstate = dict(zip(order, gstates))
    P = {n: wl[n] for n in REPLICATED}

    def need(g, after):
        got = _exchange_wait(gops[g], gstate[g], after, gather=True, name="gather_wait_" + g)
        if g == "mix_in":
            P["mix_w_inT"] = got[0]
        elif g == "mix_out":
            P.update(mix_w_out=got[0], ssm_conv_w=got[1].transpose(1, 0, 2).reshape(Kc, CD), ssm_conv_b=got[2].reshape(1, CD),
                     ssm_norm=got[3].reshape(1, DI), ffn_conv_w=got[4].transpose(1, 0, 2).reshape(2, Kf, 2 * F))
        elif g == "ssm":
            w1T = got[0].reshape(-1, got[0].shape[-1])
            P.update(ssm_w_inT=w1T, ssm_w_out=got[1], ssm_wdtT=jnp.pad(w1T[DI + CD:], ((0, LANES - NH), (0, 0))))
        elif g.endswith("_up"):
            P["ffn_w_upT" + g[3]] = got[0]
        else:
            P["ffn_w_down" + g[3]] = got[0]

    sent = {}

    def emit(g, d):
        if g == "mix":
            ops = [(d["mix_w_inT"], "rows"), (d["mix_w_out"], "rows"),
                   (d["ffn_conv_w"].reshape(2 * Kf, N_DEV, n_up).transpose(1, 0, 2), "slab")]
        elif g == "ssm":
            ops = [(d["ssm_w_inT"].reshape(N_DEV, -1, d["ssm_w_inT"].shape[-1]), "slab"), (d["ssm_w_out"], "rows"),
                   (d["ssm_conv_w"].reshape(Kc, N_DEV, -1).transpose(1, 0, 2), "slab"),
                   (d["ssm_conv_b"].reshape(N_DEV, 1, -1), "slab"), (d["ssm_norm"].reshape(N_DEV, 1, -1), "slab")]
        else:
            ops = [(d["ffn_w_upT"], "rows"), (d["ffn_w_down"], "rows")]
        own, ops = _place_own(ops, gather=False, name="scatter_own_" + g)
        (state,), tok = _exchange_start([(ops, own)], gather=False, name="scatter_start_" + g)
        sent[g] = (ops, state)
        return tok

    loss_lanes, grad_x, G = _local_step(x[0], positions.reshape(-1, 1), loss_target[0], P, need, emit, after=token)
    loss = lax.psum(loss_lanes[0, 0], ("x", "y", "c"))

    res = {}

    def update(n, g_layers):
        L = len(g_layers)
        g_layers = [g.reshape(g.shape[0], -1, g.shape[-1]) for g in g_layers]
        shape = (L,) + g_layers[0].shape[1:]
        view = tr if n in col_sharded else (lambda a: a)
        outs = _adamw(g_layers, view(wl[n]).reshape(shape), view(ml[n]).reshape(shape), view(vl[n]).reshape(shape),
                      name="adamw_" + n)
        for kind, a in zip(("grad", "delta", "new_m", "new_v"), outs):
            res[kind, n] = view(a.reshape(view(wl[n]).shape))

    rep_ops = [(a, "slab") for a in (G["norm_mix"], G["norm_ffn"], G["norm_final"], G["pool_w"].reshape(-1, LANES),
                                     G["pool_scale"], G["ffn_conv_b"], G["attn_sinks_rows"],
                                     G["ssm_dt_bias_g"].reshape(SSM_G, LANES), G["ssm_A_log_g"].reshape(SSM_G, LANES),
                                     G["ssm_D_g"].reshape(SSM_G, LANES))]
    rep_own, rep_ops = _place_own(rep_ops, gather=True, name="small_own")
    (rep_state,), rep_token = _exchange_start([(rep_ops, rep_own)], gather=True, name="small_start")

    recv = {g: _exchange_wait(sent[g][0], sent[g][1], rep_token, gather=False, name="scatter_wait_" + g)
            for g in ("ffn1", "ssm", "ffn0", "mix")}
    update("mix_w_in", [recv["mix"][0]])
    update("mix_w_out", [recv["mix"][1]])
    update("ffn_conv_w", [recv["mix"][2]])
    update("ssm_w_in", [recv["ssm"][0]])
    update("ssm_w_out", [recv["ssm"][1]])
    update("ssm_conv_w", [recv["ssm"][2]])
    update("ssm_conv_b", [recv["ssm"][3]])
    update("ssm_norm", [recv["ssm"][4]])
    update("ffn_w_up", [recv["ffn0"][0], recv["ffn1"][0]])
    update("ffn_w_down", [recv["ffn0"][1], recv["ffn1"][1]])

    rep = _exchange_wait(rep_ops, rep_state, res["new_v", "ffn_w_down"], gather=True, name="small_wait")
    for n, r in zip(("norm_mix", "norm_ffn", "norm_final", "pool_w", "pool_scale", "ffn_conv_b"), rep):
        update(n, [r])
    sinks_rows, bias_g, alog_g, d_g = _sum_slabs(rep[6:], name="sum_head_grads")
    update("attn_sinks", [sinks_rows[:, 0].reshape(1, 1, N_HEADS)])
    update("ssm_dt_bias", [_ungroup(bias_g)[None]])
    update("ssm_A_log", [_ungroup(alog_g)[None]])
    update("ssm_D", [_ungroup(d_g)[None]])

    return (loss, grad_x[None], *[res[k, n] for k in ("grad", "delta", "new_m", "new_v") for n in WEIGHTS])
```

```python
import functools
import math

import jax
import jax.numpy as jnp
from jax import lax
from jax.experimental import pallas as pl
from jax.experimental.pallas import tpu as pltpu

F32 = jnp.float32
BF16 = jnp.bfloat16

N_DEV = 8
LANES = 128
HEAD_DIM = 64
N_KV_HEADS = 2
GQ = 4
N_HEADS = N_KV_HEADS * GQ
BLOCK = 128
POOL_GROUPS = 4
ROPE_THETA = 10000.0
SSM_P = 64
SSM_G = 8
SSM_R = 4
SSM_N = 128
SSM_L = 128
NORM_EPS = 1e-6
SSM_NORM_EPS = 1e-5
ADAM_LR, ADAM_B1, ADAM_B2, ADAM_EPS, ADAM_WD, ADAM_STEP = 0.001, 0.9, 0.999, 1e-08, 0.01, 10
VMEM_LIMIT = 56 * 2 ** 20
PAYLOAD = jnp.bfloat16

REPLICATED = ("norm_mix", "norm_ffn", "norm_final", "pool_w", "pool_scale", "attn_sinks",
              "ssm_dt_bias", "ssm_A_log", "ssm_D", "ffn_conv_b")
WEIGHTS = ("norm_mix", "norm_ffn", "norm_final", "mix_w_in", "pool_w", "pool_scale", "attn_sinks", "mix_w_out",
           "ssm_w_in", "ssm_conv_w", "ssm_conv_b", "ssm_dt_bias", "ssm_A_log", "ssm_D", "ssm_norm", "ssm_w_out",
           "ffn_w_up", "ffn_conv_w", "ffn_conv_b", "ffn_w_down")


def _tile(n, cap):
    if n <= cap:
        return n
    best = None
    for d in range(LANES, cap + 1, LANES):
        if n % d == 0:
            best = d
    assert best is not None, (n, cap)
    return best


def _call(body, *, name, out_shape, grid=None, in_specs=None, out_specs=None, scratch=(), aliases=None):
    kw = {}
    if grid is not None:
        kw = dict(grid=grid, in_specs=in_specs, out_specs=out_specs)
    if aliases:
        kw["input_output_aliases"] = aliases
    return pl.pallas_call(
        body, name=name, out_shape=out_shape, scratch_shapes=list(scratch),
        compiler_params=pltpu.CompilerParams(vmem_limit_bytes=VMEM_LIMIT), **kw)


ANY = pl.BlockSpec(memory_space=pl.ANY)


def _sds(shape, dtype=F32):
    return jax.ShapeDtypeStruct(tuple(shape), dtype)


def _sigmoid(x):
    return 1.0 / (1.0 + jnp.exp(-x))


def _shift_dn(x, d, t):
    if d == 0:
        return x
    return jnp.where(t >= d, pltpu.roll(x, d, axis=0), 0.0)


def _shift_up(x, d, t):
    if d == 0:
        return x
    n = x.shape[0]
    return jnp.where(t < n - d, pltpu.roll(x, n - d, axis=0), 0.0)


def _mm(a, b, *, name, ta=False, tb=False, res=None, out_dtype=F32, b_rows=None, out_rows=None):
    M, K = (a.shape[1], a.shape[0]) if ta else a.shape
    b0, bn = b_rows if b_rows is not None else (0, b.shape[0])
    N = bn if tb else b.shape[1]
    assert (b.shape[1] if tb else bn) == K, (a.shape, b.shape, ta, tb, b_rows)
    tm, tn, tk = _tile(M, 1408), _tile(N, 1408), _tile(K, 1408)
    nk = K // tk
    dims = (((0 if ta else 1,), (1 if tb else 0,)), ((), ()))
    aliased = out_rows is not None and out_rows[2] is not None

    def body(*refs):
        a_ref, b_ref = refs[:2]
        r_ref = refs[2] if res is not None else None
        o_ref, acc = refs[-2:]
        k = pl.program_id(2)

        @pl.when(k == 0)
        def _():
            acc[...] = jnp.zeros_like(acc)

        acc[...] += lax.dot_general(a_ref[...].astype(BF16), b_ref[...].astype(BF16), dims,
                                    preferred_element_type=F32)

        @pl.when(k == nk - 1)
        def _():
            out = acc[...]
            if res is not None:
                out = out + r_ref[...]
            o_ref[...] = out.astype(out_dtype)

    a_spec = pl.BlockSpec((tk, tm), lambda i, j, k: (k, i)) if ta else pl.BlockSpec((tm, tk), lambda i, j, k: (i, k))
    if tb:
        assert b0 % tn == 0, (b_rows, tn)
        b_spec = pl.BlockSpec((tn, tk), lambda i, j, k: (b0 // tn + j, k))
    else:
        assert b0 % tk == 0, (b_rows, tk)
        b_spec = pl.BlockSpec((tk, tn), lambda i, j, k: (b0 // tk + k, j))
    ins, specs = [a, b], [a_spec, b_spec]
    if res is not None:
        ins.append(res)
        specs.append(pl.BlockSpec((tm, tn), lambda i, j, k: (i, j)))
    aliases = None
    if out_rows is None:
        o_spec = pl.BlockSpec((tm, tn), lambda i, j, k: (i, j))
        out_shape = _sds((M, N), out_dtype)
    else:
        total, o0, prev = out_rows
        assert o0 % tm == 0, (out_rows, tm)
        o_spec = pl.BlockSpec((tm, tn), lambda i, j, k: (o0 // tm + i, j))
        out_shape = _sds((total, N), out_dtype)
        if aliased:
            aliases = {len(ins): 0}
            ins.append(prev)
            specs.append(ANY)
    return _call(body, name=name, out_shape=out_shape, grid=(M // tm, N // tn, nk), in_specs=specs,
                 out_specs=o_spec, scratch=[pltpu.VMEM((tm, tn), F32)], aliases=aliases)(*ins)


def _rmsnorm(x, w, *, name, eps=NORM_EPS, after=None):
    S, D = x.shape
    tm = _tile(S, 512)
    tie = [] if after is None else [after]

    def body(x_ref, w_ref, *rest):
        o_ref = rest[-1]
        xf = x_ref[...]
        r = lax.rsqrt(jnp.mean(xf * xf, axis=-1, keepdims=True) + eps)
        o_ref[...] = (xf * r * w_ref[...]).astype(BF16)

    return _call(body, name=name, out_shape=_sds((S, D), BF16), grid=(S // tm,),
                 in_specs=[pl.BlockSpec((tm, D), lambda i: (i, 0)), pl.BlockSpec((1, D), lambda i: (0, 0))] + [ANY] * len(tie),
                 out_specs=pl.BlockSpec((tm, D), lambda i: (i, 0)))(x, w, *tie)


def _norm_bwd_math(xf, w, dh, eps):
    r = lax.rsqrt(jnp.mean(xf * xf, axis=-1, keepdims=True) + eps)
    xhat = xf * r
    dxh = dh * w
    dx = r * (dxh - xhat * jnp.mean(dxh * xhat, axis=-1, keepdims=True))
    dw = jnp.sum(dh * xhat, axis=0, keepdims=True)
    return dx, dw


def _mm_norm_bwd(a, b, x, w, dres, *, name, res=None, b_rows=None, after=None, eps=NORM_EPS):
    M, K = a.shape
    b0, bn = b_rows if b_rows is not None else (0, b.shape[0])
    D = b.shape[1]
    assert bn == K and x.shape == (M, D), (a.shape, b.shape, b_rows, x.shape)
    tm, tk = _tile(M, 512), _tile(K, 1408)
    assert b0 % tk == 0, (b_rows, tk)
    nk = K // tk

    def body(*refs):
        a_ref, b_ref, x_ref, w_ref, dr_ref = refs[:5]
        r_ref = refs[5] if res is not None else None
        dx_ref, dw_ref, acc = refs[-3:]
        i, k = pl.program_id(0), pl.program_id(1)

        @pl.when(k == 0)
        def _():
            acc[...] = jnp.zeros_like(acc)

        @pl.when((i == 0) & (k == 0))
        def _():
            dw_ref[...] = jnp.zeros_like(dw_ref)

        acc[...] += jnp.dot(a_ref[...].astype(BF16), b_ref[...].astype(BF16), preferred_element_type=F32)

        @pl.when(k == nk - 1)
        def _():
            dh = acc[...] if res is None else acc[...] + r_ref[...]
            dx, dw = _norm_bwd_math(x_ref[...], w_ref[...], dh, eps)
            dx_ref[...] = dr_ref[...] + dx
            dw_ref[...] += dw

    row = pl.BlockSpec((tm, D), lambda i, k: (i, 0))
    vec = pl.BlockSpec((1, D), lambda i, k: (0, 0))
    ins = [a, b, x, w, dres]
    specs = [pl.BlockSpec((tm, tk), lambda i, k: (i, k)), pl.BlockSpec((tk, D), lambda i, k: (b0 // tk + k, 0)), row, vec, row]
    if res is not None:
        ins.append(res)
        specs.append(row)
    if after is not None:
        ins.append(after)
        specs.append(ANY)
    return _call(body, name=name, out_shape=(_sds((M, D)), _sds((1, D))), grid=(M // tm, nk), in_specs=specs,
                 out_specs=(row, vec), scratch=[pltpu.VMEM((tm, D), F32)])(*ins)


def _final_loss(x, w, target, *, name):
    S, D = x.shape
    tm = _tile(S, 512)

    def body(x_ref, w_ref, t_ref, loss_ref, dx_ref, dw_ref):
        xf, wv = x_ref[...], w_ref[...]
        r = lax.rsqrt(jnp.mean(xf * xf, axis=-1, keepdims=True) + NORM_EPS)
        err = xf * r * wv - t_ref[...]
        part = 0.5 * jnp.sum(jnp.mean(err * err, axis=-1, keepdims=True), axis=0, keepdims=True)
        dx, dw = _norm_bwd_math(xf, wv, err * (1.0 / D), NORM_EPS)
        dx_ref[...] = dx

        @pl.when(pl.program_id(0) == 0)
        def _():
            dw_ref[...] = jnp.zeros_like(dw_ref)
            loss_ref[...] = jnp.zeros_like(loss_ref)

        dw_ref[...] += dw
        loss_ref[...] += jnp.broadcast_to(part, loss_ref.shape)

    row = pl.BlockSpec((tm, D), lambda i: (i, 0))
    vec = pl.BlockSpec((1, D), lambda i: (0, 0))
    return _call(body, name=name, out_shape=(_sds((1, LANES)), _sds((S, D)), _sds((1, D))), grid=(S // tm,),
                 in_specs=[row, vec, row], out_specs=(pl.BlockSpec((1, LANES), lambda i: (0, 0)), row, vec))(x, w, target)


def _rope_tables(pos, inv_freq):
    S = pos.shape[0]
    tm = _tile(S, 512)

    def body(p_ref, f_ref, c_ref, s_ref):
        ang = p_ref[...].astype(F32) * f_ref[...]
        c_ref[...] = jnp.cos(ang)
        s_ref[...] = jnp.sin(ang)

    blk = pl.BlockSpec((tm, LANES), lambda i: (i, 0))
    return _call(body, name="rope_tables", out_shape=(_sds((S, LANES)), _sds((S, LANES))), grid=(S // tm,),
                 in_specs=[pl.BlockSpec((tm, 1), lambda i: (i, 0)), pl.BlockSpec((1, LANES), lambda i: (0, 0))],
                 out_specs=(blk, blk))(pos, inv_freq)


def _rot_half(t):
    lane = lax.broadcasted_iota(jnp.int32, t.shape, 1)
    lo = (lane % HEAD_DIM) < (HEAD_DIM // 2)
    return jnp.where(lo, -pltpu.roll(t, LANES - HEAD_DIM // 2, axis=1), pltpu.roll(t, HEAD_DIM // 2, axis=1))


def _rope(t, c, s):
    return t * c + _rot_half(t) * s


def _unrope(dy, c, s):
    return dy * c - _rot_half(dy * s)


PD = POOL_GROUPS * LANES
QD = N_HEADS * HEAD_DIM
KD = N_KV_HEADS * HEAD_DIM
assert PD % QD == 0 and (PD + QD) % (2 * KD) == 0 and KD == LANES
def _attn_probs(q, kcat, sink, mask):
    s = lax.dot_general(q.astype(BF16), kcat, (((1,), (1,)), ((), ())), preferred_element_type=F32) * (HEAD_DIM ** -0.5)
    s = jnp.where(mask, s, -jnp.inf)
    m = jnp.maximum(jnp.max(s, axis=1, keepdims=True), sink)
    p = jnp.exp(s - m)
    ps = jnp.exp(sink - m)
    inv = 1.0 / (jnp.sum(p, axis=1, keepdims=True) + ps)
    return p * inv, ps * inv


def _attn_mask(n):
    qi = lax.broadcasted_iota(jnp.int32, (BLOCK, 2 * BLOCK), 0)
    kj = lax.broadcasted_iota(jnp.int32, (BLOCK, 2 * BLOCK), 1)
    rel = qi + BLOCK - kj
    return (rel >= 0) & (rel < BLOCK) & ((n > 0) | (kj >= BLOCK))


def _attn_in_specs(nb):
    def cur(n):
        return jnp.minimum(n, nb - 1)

    def prev(n):
        return jnp.clip(n - 1, 0, nb - 1)

    kvb = (PD + QD) // (2 * KD)
    return [pl.BlockSpec(memory_space=pltpu.SMEM),
            pl.BlockSpec((BLOCK, QD), lambda n: (cur(n), PD // QD)),
            pl.BlockSpec((BLOCK, 2 * KD), lambda n: (cur(n), kvb)),
            pl.BlockSpec((BLOCK, 2 * KD), lambda n: (prev(n), kvb)),
            pl.BlockSpec((BLOCK, LANES), lambda n: (cur(n), 0)), pl.BlockSpec((BLOCK, LANES), lambda n: (cur(n), 0)),
            pl.BlockSpec((BLOCK, LANES), lambda n: (prev(n), 0)), pl.BlockSpec((BLOCK, LANES), lambda n: (prev(n), 0))]


def _attn_keys(kvc_ref, kvp_ref, cc, sc, cp, sp):
    kc = _rope(kvc_ref[:, :KD], cc, sc)
    kp = _rope(kvp_ref[:, :KD], cp, sp)
    vc, vp = kvc_ref[:, KD:], kvp_ref[:, KD:]
    kcat, vcat = [], []
    for kk in range(N_KV_HEADS):
        sl = slice(kk * HEAD_DIM, (kk + 1) * HEAD_DIM)
        kcat.append(jnp.concatenate([kp[:, sl], kc[:, sl]], axis=0).astype(BF16))
        vcat.append(jnp.concatenate([vp[:, sl], vc[:, sl]], axis=0).astype(BF16))
    return kcat, vcat


def _attn_fwd(proj, cos, sin, sinks, cat):
    S = proj.shape[0]
    nb = S // BLOCK

    def body(sink_ref, q_ref, kvc_ref, kvp_ref, cc_ref, sc_ref, cp_ref, sp_ref, cat_ref, o_ref):
        n = pl.program_id(0)
        cc, sc = cc_ref[...], sc_ref[...]
        kcat, vcat = _attn_keys(kvc_ref, kvp_ref, cc, sc, cp_ref[...], sp_ref[...])
        mask = _attn_mask(n)
        def head_pair(j):
            qr = _rope(q_ref[:, j * LANES:(j + 1) * LANES], cc, sc)
            for e in range(LANES // HEAD_DIM):
                yield
                h = j * (LANES // HEAD_DIM) + e
                pn, _ = _attn_probs(qr[:, e * HEAD_DIM:(e + 1) * HEAD_DIM], kcat[h // GQ], sink_ref[0, h], mask)
                yield
                o_ref[:, h * HEAD_DIM:(h + 1) * HEAD_DIM] = jnp.dot(pn.astype(BF16), vcat[h // GQ], preferred_element_type=F32)

        _interleave([head_pair(j) for j in range(QD // LANES)])

    return _call(body, name="attn_fwd", out_shape=_sds(cat.shape), grid=(nb,),
                 in_specs=_attn_in_specs(nb) + [ANY], out_specs=pl.BlockSpec((BLOCK, QD), lambda n: (n, PD // QD)),
                 aliases={8: 0})(sinks, proj, proj, proj, cos, sin, cos, sin, cat)


def _attn_bwd(proj, cos, sin, sinks, dcat):
    S = proj.shape[0]
    nb = S // BLOCK
    scale = HEAD_DIM ** -0.5
    per = LANES // HEAD_DIM

    def body(sink_ref, q_ref, kvc_ref, kvp_ref, cc_ref, sc_ref, cp_ref, sp_ref, do_ref, o_ref, ds_ref, hold, carry, part, pair):
        n = pl.program_id(0)

        @pl.when(n == 0)
        def _():
            hold[...] = jnp.zeros_like(hold)
            carry[...] = jnp.zeros_like(carry)
            ds_ref[...] = jnp.zeros_like(ds_ref)

        live = jnp.where(n < nb, 1.0, 0.0)
        cc, sc, cp, sp = cc_ref[...], sc_ref[...], cp_ref[...], sp_ref[...]
        kcat, vcat = _attn_keys(kvc_ref, kvp_ref, cc, sc, cp, sp)
        mask = _attn_mask(n)
        o_ref[:, :PD] = jnp.zeros((BLOCK, PD), F32)
        o_ref[:, PD:PD + QD] = hold[...]
        dk = [jnp.zeros((2 * BLOCK, HEAD_DIM), F32) for _ in range(N_KV_HEADS)]
        dv = [jnp.zeros((2 * BLOCK, HEAD_DIM), F32) for _ in range(N_KV_HEADS)]
        row = lax.broadcasted_iota(jnp.int32, (8, LANES), 0)
        acc = {"dsk": jnp.zeros((8, LANES), F32)}

        def head_pair(j):
            qr = _rope(q_ref[:, j * LANES:(j + 1) * LANES], cc, sc)
            for e in range(per):
                yield
                h = j * per + e
                kk = h // GQ
                qh = qr[:, e * HEAD_DIM:(e + 1) * HEAD_DIM]
                pn, psn = _attn_probs(qh, kcat[kk], sink_ref[0, h], mask)
                yield
                doh = (do_ref[:, h * HEAD_DIM:(h + 1) * HEAD_DIM] * live).astype(BF16)
                dp = lax.dot_general(doh, vcat[kk], NT, preferred_element_type=F32)
                yield
                delta = jnp.sum(pn * dp, axis=1, keepdims=True)
                ds = (pn * (dp - delta) * scale).astype(BF16)
                pair[j, :, e * HEAD_DIM:(e + 1) * HEAD_DIM] = jnp.dot(ds, kcat[kk], preferred_element_type=F32)
                yield
                dk[kk] = dk[kk] + lax.dot_general(ds, qh.astype(BF16), TN, preferred_element_type=F32)
                dv[kk] = dv[kk] + lax.dot_general(pn.astype(BF16), doh, TN, preferred_element_type=F32)
                acc["dsk"] = acc["dsk"] + jnp.where(row == h, -jnp.sum(psn * delta), 0.0)
            yield
            hold[:, j * LANES:(j + 1) * LANES] = _unrope(pair[j], cc, sc)

        _interleave([head_pair(j) for j in range(QD // LANES)])
        dsk = acc["dsk"]
        for kk in range(N_KV_HEADS):
            sl = slice(kk * HEAD_DIM, (kk + 1) * HEAD_DIM)
            sv = slice(KD + kk * HEAD_DIM, KD + (kk + 1) * HEAD_DIM)
            part[0, :, sl] = dk[kk][:BLOCK]
            part[0, :, sv] = dv[kk][:BLOCK]
            part[1, :, sl] = dk[kk][BLOCK:]
            part[1, :, sv] = dv[kk][BLOCK:]
        done = carry[...] + part[0]
        o_ref[:, PD + QD:PD + QD + KD] = _unrope(done[:, :KD], cp, sp)
        o_ref[:, PD + QD + KD:] = done[:, KD:]
        carry[...] = part[1]
        ds_ref[...] += dsk

    return _call(body, name="attn_bwd", out_shape=(_sds((S, PD + QD + 2 * KD)), _sds((8, LANES))), grid=(nb + 1,),
                 in_specs=_attn_in_specs(nb) + [pl.BlockSpec((BLOCK, QD), lambda n: (jnp.minimum(n, nb - 1), PD // QD))],
                 out_specs=(pl.BlockSpec((BLOCK, PD + QD + 2 * KD), lambda n: (jnp.maximum(n - 1, 0), 0)),
                            pl.BlockSpec((8, LANES), lambda n: (0, 0))),
                 scratch=[pltpu.VMEM((BLOCK, QD), F32), pltpu.VMEM((BLOCK, 2 * KD), F32),
                          pltpu.VMEM((2, BLOCK, 2 * KD), F32), pltpu.VMEM((QD // LANES, BLOCK, LANES), F32)])(
                     sinks, proj, proj, proj, cos, sin, cos, sin, dcat)


def _pool_sums(u, g, t, shift):
    s2 = u + shift(u, 1, t)
    s4 = s2 + shift(s2, 2, t)
    s8 = s4 + shift(s4, 4, t)
    s16 = s8 + shift(s8, 8, t)
    return jnp.where(g == 0, s2, jnp.where(g == 1, s4, jnp.where(g == 2, s8, s16)))


def _pool_specs(S):
    col = pl.BlockSpec((S, LANES), lambda g: (0, g))
    wsp = pl.BlockSpec((1, LANES, LANES), lambda g: (g, 0, 0))
    vec = pl.BlockSpec((1, LANES), lambda g: (0, g))
    return col, wsp, vec


def _pool_fwd(proj, pool_w, scale):
    S = proj.shape[0]
    col, wsp, vec = _pool_specs(S)

    def body(u_ref, w_ref, s_ref, o_ref):
        g = pl.program_id(0)
        u = u_ref[...]
        t = lax.broadcasted_iota(jnp.int32, u.shape, 0)
        cnt = jnp.minimum(t + 1, 2 << g).astype(F32)
        pm = _pool_sums(u, g, t, _shift_dn) / cnt - u
        o_ref[...] = jnp.dot(pm.astype(BF16), w_ref[0].astype(BF16), preferred_element_type=F32) * s_ref[...]

    return _call(body, name="pool_fwd", out_shape=_sds((S, PD + QD)), grid=(POOL_GROUPS,),
                 in_specs=[col, wsp, vec], out_specs=col)(proj, pool_w, scale)


def _pool_bwd(proj, pool_w, scale, dcat, dproj):
    S = proj.shape[0]
    col, wsp, vec = _pool_specs(S)

    def body(u_ref, w_ref, s_ref, d_ref, dproj_ref, du_ref, dw_ref, dsc_ref):
        g = pl.program_id(0)
        u = u_ref[...]
        t = lax.broadcasted_iota(jnp.int32, u.shape, 0)
        cnt = jnp.minimum(t + 1, 2 << g).astype(F32)
        pm = (_pool_sums(u, g, t, _shift_dn) / cnt - u).astype(BF16)
        wv = w_ref[0].astype(BF16)
        d = d_ref[...]
        pw = jnp.dot(pm, wv, preferred_element_type=F32)
        dsc_ref[...] = jnp.sum(pw * d, axis=0, keepdims=True)
        dpw = (d * s_ref[...]).astype(BF16)
        dw_ref[0] = lax.dot_general(pm, dpw, (((0,), (0,)), ((), ())), preferred_element_type=F32)
        dpm = lax.dot_general(dpw, wv, (((1,), (1,)), ((), ())), preferred_element_type=F32)
        du_ref[...] = _pool_sums(dpm / cnt, g, t, _shift_up) - dpm

    return _call(body, name="pool_bwd",
                 out_shape=(_sds(dproj.shape), _sds((POOL_GROUPS, LANES, LANES)), _sds((1, POOL_GROUPS * LANES))),
                 grid=(POOL_GROUPS,), in_specs=[col, wsp, vec, col, ANY], out_specs=(col, wsp, vec),
                 aliases={4: 0})(proj, pool_w, scale, dcat, dproj)


def _conv(x, w_ref, b_ref, t):
    K = w_ref.shape[0]
    y = b_ref[...] + jnp.zeros_like(x)
    for k in range(K):
        y = y + w_ref[k:k + 1, :] * _shift_dn(x, K - 1 - k, t)
    return y


def _silu_grad(y):
    sg = _sigmoid(y)
    return sg * (1.0 + y * (1.0 - sg))


CONV_ROWS = 64
HALO = 8


def _win_above(ref, r0):
    if isinstance(r0, int):
        assert r0 == 0
        return jnp.concatenate([jnp.zeros((HALO, ref.shape[1]), F32), ref[0:CONV_ROWS, :]], axis=0)
    return ref[pl.ds(pl.multiple_of(r0 - HALO, HALO), CONV_ROWS + HALO), :]


def _rows_at(win, start):
    if start % 8 == 0:
        return win[start:start + CONV_ROWS]
    base = start // 8 * 8
    return pltpu.roll(win, win.shape[0] - (start - base), axis=0)[base:base + CONV_ROWS]


def _taps_above(win, K):
    return [_rows_at(win, HALO - (K - 1 - k)) for k in range(K)]


def _conv_taps(taps, w, b):
    y = b
    for k in range(len(w)):
        y = y + w[k] * taps[k]
    return y


def _conv_t_win(win, w):
    K = len(w)
    out = None
    for k in range(K):
        d = K - 1 - k
        term = w[k] * _rows_at(win, d)
        out = term if out is None else out + term
    return out


def _fold8(x):
    return jnp.sum(x.reshape(CONV_ROWS // 8, 8, x.shape[-1]), axis=0)


def _chunk_loop(S, step, init):
    carry = step(0, init)
    return lax.fori_loop(1, S // CONV_ROWS, lambda i, c: step(pl.multiple_of(i * CONV_ROWS, CONV_ROWS), c), carry)


def _ffn_mid_specs(S, K, layer, nf):
    return [pl.BlockSpec((S, LANES), lambda j: (0, j)), pl.BlockSpec((S, LANES), lambda j: (0, nf + j)),
            pl.BlockSpec((None, K, LANES), lambda j: (layer, 0, j)), pl.BlockSpec((None, K, LANES), lambda j: (layer, 0, nf + j)),
            pl.BlockSpec((None, 1, LANES), lambda j: (layer, 0, j)), pl.BlockSpec((None, 1, LANES), lambda j: (layer, 0, nf + j))]


def _ffn_mid_fwd(a, cw, cb, layer):
    S, F = a.shape[0], a.shape[1] // 2
    nf = F // LANES
    K = cw.shape[1]

    def body(au_ref, ag_ref, wu_ref, wg_ref, bu_ref, bg_ref, o_ref):
        t = lax.broadcasted_iota(jnp.int32, (S, LANES), 0)
        hu = _conv(au_ref[...], wu_ref, bu_ref, t)
        hg = _conv(ag_ref[...], wg_ref, bg_ref, t)
        o_ref[...] = (hg * _sigmoid(hg) * hu).astype(BF16)

    return _call(body, name="ffn_mid_fwd", out_shape=_sds((S, F), BF16), grid=(nf,),
                 in_specs=_ffn_mid_specs(S, K, layer, nf), out_specs=pl.BlockSpec((S, LANES), lambda j: (0, j)))(
                     a, a, cw, cw, cb[:, None], cb[:, None])


def _ffn_mid_bwd(a, cw, cb, layer, dact):
    S, F = a.shape[0], a.shape[1] // 2
    nf = F // LANES
    K = cw.shape[1]

    def body(au_ref, ag_ref, wu_ref, wg_ref, bu_ref, bg_ref, d_ref, dau_ref, dag_ref, dwu_ref, dwg_ref, dbu_ref, dbg_ref,
             dhu_s, dhg_s):
        T = CONV_ROWS
        wu = [wu_ref[k:k + 1, :] for k in range(K)]
        wg = [wg_ref[k:k + 1, :] for k in range(K)]
        bu, bg = bu_ref[...], bg_ref[...]
        zero8 = jnp.zeros((HALO, LANES), F32)
        dhu_s[S:S + HALO, :] = zero8
        dhg_s[S:S + HALO, :] = zero8

        def first_pass(r0, acc):
            tu, tg = _taps_above(_win_above(au_ref, r0), K), _taps_above(_win_above(ag_ref, r0), K)
            hu, hg = _conv_taps(tu, wu, bu), _conv_taps(tg, wg, bg)
            d = d_ref[pl.ds(r0, T), :].astype(F32)
            sg = _sigmoid(hg)
            dhu = d * hg * sg
            dhg = d * hu * (sg * (1.0 + hg * (1.0 - sg)))
            dhu_s[pl.ds(r0, T), :] = dhu
            dhg_s[pl.ds(r0, T), :] = dhg
            new = []
            for dh, taps in ((dhu, tu), (dhg, tg)):
                for k in range(K):
                    new.append(acc[len(new)] + _fold8(dh * taps[k]))
            new.append(acc[2 * K] + _fold8(dhu))
            new.append(acc[2 * K + 1] + _fold8(dhg))
            return tuple(new)

        acc = _chunk_loop(S, first_pass, tuple(zero8 for _ in range(2 * K + 2)))
        for k in range(K):
            dwu_ref[k:k + 1, :] = jnp.sum(acc[k], axis=0, keepdims=True)
            dwg_ref[k:k + 1, :] = jnp.sum(acc[K + k], axis=0, keepdims=True)
        dbu_ref[...] = jnp.sum(acc[2 * K], axis=0, keepdims=True)
        dbg_ref[...] = jnp.sum(acc[2 * K + 1], axis=0, keepdims=True)

        def second_pass(i, carry):
            r0 = pl.multiple_of(i * T, T)
            dau_ref[pl.ds(r0, T), :] = _conv_t_win(dhu_s[pl.ds(r0, T + HALO), :], wu).astype(BF16)
            dag_ref[pl.ds(r0, T), :] = _conv_t_win(dhg_s[pl.ds(r0, T + HALO), :], wg).astype(BF16)
            return carry

        lax.fori_loop(0, S // T, second_pass, 0)

    col = pl.BlockSpec((S, LANES), lambda j: (0, j))
    wsp = pl.BlockSpec((K, LANES), lambda j: (0, j))
    bsp = pl.BlockSpec((1, LANES), lambda j: (0, j))
    dau, dag, dwu, dwg, dbu, dbg = _call(
        body, name="ffn_mid_bwd",
        out_shape=(_sds((S, F), BF16), _sds((S, F), BF16), _sds((K, F)), _sds((K, F)), _sds((1, F)), _sds((1, F))), grid=(nf,),
        in_specs=_ffn_mid_specs(S, K, layer, nf) + [col], out_specs=(col, col, wsp, wsp, bsp, bsp),
        scratch=[pltpu.VMEM((S + HALO, LANES), F32), pltpu.VMEM((S + HALO, LANES), F32)])(
            a, a, cw, cw, cb[:, None], cb[:, None], dact)
    return dau, dag, jnp.concatenate([dwu, dwg], axis=1), jnp.concatenate([dbu, dbg], axis=1)


def _conv_silu_fwd(x, cw, cb):
    S = x.shape[0]
    K, C = cw.shape

    def body(x_ref, w_ref, b_ref, o_ref):
        t = lax.broadcasted_iota(jnp.int32, (S, LANES), 0)
        y = _conv(x_ref[...], w_ref, b_ref, t)
        o_ref[...] = y * _sigmoid(y)

    col = pl.BlockSpec((S, LANES), lambda j: (0, j))
    return _call(body, name="conv_silu_fwd", out_shape=_sds((S, C)), grid=(C // LANES,),
                 in_specs=[col, pl.BlockSpec((K, LANES), lambda j: (0, j)), pl.BlockSpec((1, LANES), lambda j: (0, j))],
                 out_specs=col)(x, cw, cb)


def _conv_silu_bwd(x, cw, cb, douts):
    S = x.shape[0]
    K, C = cw.shape
    starts, off = [], 0
    for d in douts:
        starts.append(off)
        off += d.shape[1] // LANES
    assert off == C // LANES

    def body(x_ref, w_ref, b_ref, *rest):
        dy_s = rest[-1]
        d_refs, (dx_ref, dw_ref, db_ref) = rest[:len(douts)], rest[len(douts):-1]
        j = pl.program_id(0)
        T = CONV_ROWS
        w = [w_ref[k:k + 1, :] for k in range(K)]
        b = b_ref[...]
        zero8 = jnp.zeros((HALO, LANES), F32)
        dy_s[S:S + HALO, :] = zero8

        def first_pass(r0, acc):
            taps = _taps_above(_win_above(x_ref, r0), K)
            y = _conv_taps(taps, w, b)
            d = d_refs[0][pl.ds(r0, T), :]
            for i in range(1, len(douts)):
                d = jnp.where(j >= starts[i], d_refs[i][pl.ds(r0, T), :], d)
            dy = d * _silu_grad(y)
            dy_s[pl.ds(r0, T), :] = dy
            return tuple(acc[k] + _fold8(dy * taps[k]) for k in range(K)) + (acc[K] + _fold8(dy),)

        acc = _chunk_loop(S, first_pass, tuple(zero8 for _ in range(K + 1)))
        for k in range(K):
            dw_ref[k:k + 1, :] = jnp.sum(acc[k], axis=0, keepdims=True)
        db_ref[...] = jnp.sum(acc[K], axis=0, keepdims=True)

        def second_pass(i, carry):
            r0 = pl.multiple_of(i * T, T)
            dx_ref[pl.ds(r0, T), :] = _conv_t_win(dy_s[pl.ds(r0, T + HALO), :], w).astype(BF16)
            return carry

        lax.fori_loop(0, S // T, second_pass, 0)

    col = pl.BlockSpec((S, LANES), lambda j: (0, j))
    wsp = pl.BlockSpec((K, LANES), lambda j: (0, j))
    bsp = pl.BlockSpec((1, LANES), lambda j: (0, j))

    def dspec(i):
        nblk = douts[i].shape[1] // LANES
        return pl.BlockSpec((S, LANES), lambda j: (0, jnp.clip(j - starts[i], 0, nblk - 1)))

    return _call(body, name="conv_silu_bwd", out_shape=(_sds((S, C), BF16), _sds((K, C)), _sds((1, C))), grid=(C // LANES,),
                 in_specs=[col, wsp, bsp] + [dspec(i) for i in range(len(douts))],
                 out_specs=(col, wsp, bsp), scratch=[pltpu.VMEM((S + HALO, LANES), F32)])(x, cw, cb, *douts)


HI = lax.Precision.HIGHEST


def _ssd_prep_fwd(proj, col0, bias_g, alog_g):
    S = proj.shape[0]
    nc = S // SSM_L
    b0 = col0 // LANES

    def body(raw_ref, b_ref, al_ref, pre_ref, dt_ref, acs_ref, acst_ref):
        r_i = lax.broadcasted_iota(jnp.int32, (LANES, LANES), 0)
        c_i = lax.broadcasted_iota(jnp.int32, (LANES, LANES), 1)
        live = c_i < SSM_R
        tril = jnp.where(r_i >= c_i, 1.0, 0.0)
        raw = raw_ref[...]
        for g in range(SSM_G):
            mine = raw if g == 0 else pltpu.roll(raw, LANES - SSM_R * g, axis=1)
            pre = jnp.where(live, mine, 0.0) + b_ref[g]
            dt = jnp.where(live, jnp.logaddexp(pre, 0.0), 0.0)
            a = dt * (-jnp.exp(al_ref[g]))
            acs = jnp.dot(tril, a, preferred_element_type=F32, precision=HI)
            pre_ref[g] = pre
            dt_ref[g] = dt
            acs_ref[g] = acs
            acst_ref[g] = acs.T

    gsp = pl.BlockSpec((SSM_G, 1, LANES), lambda c: (0, 0, 0))
    blk = pl.BlockSpec((SSM_G, SSM_L, LANES), lambda c: (0, c, 0))
    big = _sds((SSM_G, S, LANES))
    return _call(body, name="ssd_prep_fwd", out_shape=(big, big, big, _sds((SSM_G, LANES, S))), grid=(nc,),
                 in_specs=[pl.BlockSpec((SSM_L, LANES), lambda c: (c, b0)), gsp, gsp],
                 out_specs=(blk, blk, blk, pl.BlockSpec((SSM_G, LANES, SSM_L), lambda c: (0, 0, c))))(proj, bias_g, alog_g)


def _ssd_prep_bwd(pre_g, dt_g, alog_g, ddt_g, dacs_g, dacst_g):
    S = pre_g.shape[1]
    nc = S // SSM_L

    def body(pre_ref, dt_ref, al_ref, ddt_ref, dacs_ref, dacst_ref, draw_ref, db_ref, dal_ref):
        c = pl.program_id(0)
        r_i = lax.broadcasted_iota(jnp.int32, (LANES, LANES), 0)
        c_i = lax.broadcasted_iota(jnp.int32, (LANES, LANES), 1)
        live = c_i < SSM_R
        triu = jnp.where(r_i <= c_i, 1.0, 0.0)

        @pl.when(c == 0)
        def _():
            db_ref[...] = jnp.zeros_like(db_ref)
            dal_ref[...] = jnp.zeros_like(dal_ref)

        draw = jnp.zeros((SSM_L, LANES), F32)
        for g in range(SSM_G):
            dacs = dacs_ref[g] + dacst_ref[g].T
            da = jnp.dot(triu, dacs, preferred_element_type=F32, precision=HI)
            A = -jnp.exp(al_ref[g])
            ddt = ddt_ref[g] + da * A
            dpre = jnp.where(live, ddt * _sigmoid(pre_ref[g]), 0.0)
            draw = draw + (dpre if g == 0 else pltpu.roll(dpre, SSM_R * g, axis=1))
            db_ref[g] += jnp.sum(dpre, axis=0, keepdims=True)
            dal_ref[g] += jnp.where(live[:1], jnp.sum(da * dt_ref[g], axis=0, keepdims=True) * A, 0.0)
        draw_ref[...] = draw

    gsp = pl.BlockSpec((SSM_G, 1, LANES), lambda c: (0, 0, 0))
    blk = pl.BlockSpec((SSM_G, SSM_L, LANES), lambda c: (0, c, 0))
    gout = _sds((SSM_G, 1, LANES))
    return _call(body, name="ssd_prep_bwd", out_shape=(_sds((S, LANES)), gout, gout), grid=(nc,),
                 in_specs=[blk, blk, gsp, blk, blk, pl.BlockSpec((SSM_G, LANES, SSM_L), lambda c: (0, 0, c))],
                 out_specs=(pl.BlockSpec((SSM_L, LANES), lambda c: (c, 0)), gsp, gsp))(pre_g, dt_g, alog_g, ddt_g, dacs_g, dacst_g)


NT = (((1,), (1,)), ((), ()))
TN = (((0,), (0,)), ((), ()))
SSM_HP = SSM_R * SSM_P


def _ssd_group_terms(xs_ref, dt_ref, acs_ref, d_ref):
    hid = lax.broadcasted_iota(jnp.int32, (1, SSM_HP), 1) // SSM_P
    rid = lax.broadcasted_iota(jnp.int32, (SSM_HP, 1), 0) // SSM_P

    def widen(cols):
        out = cols[0]
        for r in range(1, SSM_R):
            out = jnp.where(hid == r, cols[r], out)
        return out

    dt_c = [dt_ref[:, r:r + 1] for r in range(SSM_R)]
    acs_c = [acs_ref[:, r:r + 1] for r in range(SSM_R)]
    last = [acs_ref[SSM_L - 1:SSM_L, r:r + 1] for r in range(SSM_R)]
    decay_c = [jnp.exp(last[r] - acs_c[r]) for r in range(SSM_R)]
    cd = [jnp.exp(last[r]) for r in range(SSM_R)]
    cd_rows = cd[0]
    for r in range(1, SSM_R):
        cd_rows = jnp.where(rid == r, cd[r], cd_rows)
    xs = xs_ref[...]
    return (xs, xs * widen(dt_c), widen([jnp.exp(a) for a in acs_c]), widen(decay_c),
            widen([d_ref[:, r:r + 1] for r in range(SSM_R)]), cd_rows, dt_c, decay_c, cd)


def _ssd_lmat(acs_ref, acst_ref, r, tril):
    return jnp.exp(jnp.where(tril, acs_ref[:, r:r + 1] - acst_ref[r:r + 1, :], -jnp.inf))


SSM_GPS = 2


def _ssd_specs(rev, nc):
    def cc(c):
        return nc - 1 - c if rev else c
    xs_blocks = (SSM_G * SSM_HP) // (SSM_GPS * SSM_N)
    xs = pl.BlockSpec((SSM_L, SSM_GPS * SSM_HP), lambda g, c: (cc(c), g))
    bsp = pl.BlockSpec((SSM_L, SSM_GPS * SSM_N), lambda g, c: (cc(c), xs_blocks + g))
    csp = pl.BlockSpec((SSM_L, SSM_GPS * SSM_N), lambda g, c: (cc(c), xs_blocks + SSM_G // SSM_GPS + g))
    sc = pl.BlockSpec((SSM_GPS, SSM_L, LANES), lambda g, c: (g, cc(c), 0))
    sct = pl.BlockSpec((SSM_GPS, LANES, SSM_L), lambda g, c: (g, 0, cc(c)))
    gsp = pl.BlockSpec((SSM_GPS, 1, LANES), lambda g, c: (g, 0, 0))
    st = pl.BlockSpec((None, SSM_GPS, SSM_HP, SSM_N), lambda g, c: (cc(c), g, 0, 0))
    return xs, bsp, csp, sc, sct, gsp, st


def _interleave(gens):
    live = list(gens)
    while live:
        for g in list(live):
            try:
                next(g)
            except StopIteration:
                live.remove(g)


def _rounds(gens):
    live = list(gens)
    while live:
        for g in list(live):
            try:
                next(g)
            except StopIteration:
                live.remove(g)
        yield


def _ssd_group_views(gg, xs_ref, b_ref, c_ref, *per_group):
    return (xs_ref.at[:, gg * SSM_HP:(gg + 1) * SSM_HP], b_ref.at[:, gg * SSM_N:(gg + 1) * SSM_N],
            c_ref.at[:, gg * SSM_N:(gg + 1) * SSM_N]) + tuple(r.at[gg] for r in per_group)


def _ssd_fwd(xbc, dt_g, acs_g, acst_g, d_g):
    S = xbc.shape[0]
    nc = S // SSM_L
    xs_s, b_s, c_s, sc, sct, gsp, st = _ssd_specs(False, nc)

    def body(xs_ref, b_ref, c_ref, dt_ref, acs_ref, acst_ref, d_ref, y_ref, st_ref, state):
        c = pl.program_id(1)

        @pl.when(c == 0)
        def _():
            state[...] = jnp.zeros_like(state)

        tril = lax.broadcasted_iota(jnp.int32, (SSM_L, SSM_L), 0) >= lax.broadcasted_iota(jnp.int32, (SSM_L, SSM_L), 1)
        def group(gg):
            xs_v, b_v, c_v, dt_v, acs_v, acst_v, d_v, st_v, state_v = _ssd_group_views(
                gg, xs_ref, b_ref, c_ref, dt_ref, acs_ref, acst_ref, d_ref, st_ref, state)
            y_v = y_ref.at[:, gg * SSM_HP:(gg + 1) * SSM_HP]
            Bb, Cb = b_v[...].astype(BF16), c_v[...].astype(BF16)
            Gm = lax.dot_general(Cb, Bb, NT, preferred_element_type=F32)
            yield
            xs, X, e_all, decay_all, d_all, cd_rows, _, _, _ = _ssd_group_terms(xs_v, dt_v, acs_v, d_v)
            S_all = state_v[...]
            st_v[...] = S_all
            yield
            yo = lax.dot_general(Cb, S_all.astype(BF16), NT, preferred_element_type=F32)
            new_state = lax.dot_general((X * decay_all).astype(BF16), Bb, TN, preferred_element_type=F32)
            yield
            y_v[...] = e_all * yo + d_all * xs
            state_v[...] = S_all * cd_rows + new_state
            for r in range(SSM_R):
                yield
                sl = slice(r * SSM_P, (r + 1) * SSM_P)
                M = Gm * _ssd_lmat(acs_v, acst_v, r, tril)
                yield
                y_v[:, sl] += jnp.dot(M.astype(BF16), X[:, sl].astype(BF16), preferred_element_type=F32)

        _interleave([group(gg) for gg in range(SSM_GPS)])

    return _call(body, name="ssd_fwd",
                 out_shape=(_sds((S, SSM_G * SSM_HP)), _sds((nc, SSM_G, SSM_HP, SSM_N))),
                 grid=(SSM_G // SSM_GPS, nc), in_specs=[xs_s, b_s, c_s, sc, sc, sct, gsp],
                 out_specs=(xs_s, pl.BlockSpec((None, SSM_GPS, SSM_HP, SSM_N), lambda g, c: (c, g, 0, 0))),
                 scratch=[pltpu.VMEM((SSM_GPS, SSM_HP, SSM_N), F32)])(xbc, xbc, xbc, dt_g, acs_g, acst_g, d_g)


def _ssd_bwd(xbc, dt_g, acs_g, acst_g, d_g, states, dy):
    S = xbc.shape[0]
    nc = S // SSM_L
    xs_s, b_s, c_s, sc, sct, gsp, st = _ssd_specs(True, nc)
    bc_out = pl.BlockSpec((SSM_L, SSM_GPS * SSM_N), lambda g, c: (nc - 1 - c, g))

    def body(xs_ref, b_ref, c_ref, dt_ref, acs_ref, acst_ref, d_ref, st_ref, dy_ref,
             dxs_ref, db_ref, dc_ref, ddt_ref, dacs_ref, dacst_ref, dd_ref, dstate):
        c = pl.program_id(1)

        @pl.when(c == 0)
        def _():
            dstate[...] = jnp.zeros_like(dstate)
            dd_ref[...] = jnp.zeros_like(dd_ref)

        tril = lax.broadcasted_iota(jnp.int32, (SSM_L, SSM_L), 0) >= lax.broadcasted_iota(jnp.int32, (SSM_L, SSM_L), 1)
        lane = lax.broadcasted_iota(jnp.int32, (1, LANES), 1)
        subl = lax.broadcasted_iota(jnp.int32, (LANES, 1), 0)
        last_row = lax.broadcasted_iota(jnp.int32, (SSM_L, 1), 0) == SSM_L - 1
        triu = lax.broadcasted_iota(jnp.int32, (SSM_L, SSM_L), 0) <= lax.broadcasted_iota(jnp.int32, (SSM_L, SSM_L), 1)
        def group(gg):
            xs_v, b_v, c_v, dt_v, acs_v, acst_v, d_v, st_v, ddt_v, dacs_v, dacst_v, dd_v, dstate_v = _ssd_group_views(
                gg, xs_ref, b_ref, c_ref, dt_ref, acs_ref, acst_ref, d_ref, st_ref, ddt_ref, dacs_ref, dacst_ref, dd_ref, dstate)
            dy_v, dxs_v = (r.at[:, gg * SSM_HP:(gg + 1) * SSM_HP] for r in (dy_ref, dxs_ref))
            db_v, dc_v = (r.at[:, gg * SSM_N:(gg + 1) * SSM_N] for r in (db_ref, dc_ref))
            Bb, Cb = b_v[...].astype(BF16), c_v[...].astype(BF16)
            Gm = lax.dot_general(Cb, Bb, NT, preferred_element_type=F32)
            GmT = lax.dot_general(Bb, Cb, NT, preferred_element_type=F32)
            yield
            xs, X, e_all, decay_all, d_all, cd_rows, dt_c, decay_c, cd = _ssd_group_terms(xs_v, dt_v, acs_v, d_v)
            S_all, dSn_all, dY = st_v[...], dstate_v[...], dy_v[...]
            Sb, dSnb = S_all.astype(BF16), dSn_all.astype(BF16)
            yield
            T = lax.dot_general(Cb, Sb, NT, preferred_element_type=F32)
            dT = (dY * e_all).astype(BF16)
            dC = jnp.dot(dT, Sb, preferred_element_type=F32)
            dS_prev = lax.dot_general(dT, Cb, TN, preferred_element_type=F32)
            yield
            yo_dy = dY * (e_all * T)
            W = lax.dot_general(Bb, dSnb, NT, preferred_element_type=F32)
            dB = jnp.dot((X * decay_all).astype(BF16), dSnb, preferred_element_type=F32)
            yield
            xw = X * W
            dcd_rows = jnp.sum(dSn_all * S_all, axis=1, keepdims=True)
            dstate_v[...] = dS_prev + dSn_all * cd_rows
            dX_state = W * decay_all
            yield
            acc = dict(dG=jnp.zeros((SSM_L, SSM_L), F32), dGT=jnp.zeros((SSM_L, SSM_L), F32),
                       ddt=jnp.zeros((SSM_L, LANES), F32), dacs=jnp.zeros((SSM_L, LANES), F32),
                       dacst=jnp.zeros((LANES, SSM_L), F32), dd=jnp.zeros((1, LANES), F32))

            def head(r):
                sl = slice(r * SSM_P, (r + 1) * SSM_P)
                Lm = _ssd_lmat(acs_v, acst_v, r, tril)
                LmT = jnp.exp(jnp.where(triu, acst_v[r:r + 1, :] - acs_v[:, r:r + 1], -jnp.inf))
                M = Gm * Lm
                yield
                dYh, xs_h = dY[:, sl], xs[:, sl]
                dYb, Xb = dYh.astype(BF16), X[:, sl].astype(BF16)
                dM = lax.dot_general(dYb, Xb, NT, preferred_element_type=F32)
                yield
                dX = jnp.dot((GmT * LmT).astype(BF16), dYb, preferred_element_type=F32) + dX_state[:, sl]
                acc["dG"] = acc["dG"] + dM * Lm
                acc["dGT"] = acc["dGT"] + lax.dot_general(Xb, dYb, NT, preferred_element_type=F32) * LmT
                yield
                dseg = dM * M
                dd = jnp.sum(xw[:, sl], axis=1, keepdims=True) * decay_c[r]
                dcd = jnp.sum(dcd_rows[sl])
                dacs_col = (jnp.sum(dseg, axis=1, keepdims=True) + jnp.sum(yo_dy[:, sl], axis=1, keepdims=True) - dd
                            + jnp.where(last_row, dcd * cd[r] + jnp.sum(dd), 0.0))
                dacs_row = -jnp.sum(dseg, axis=0, keepdims=True)
                yield
                dxs_v[:, sl] = dX * dt_c[r] + d_all[:, sl] * dYh
                acc["ddt"] = acc["ddt"] + jnp.where(lane == r, jnp.sum(dX * xs_h, axis=1, keepdims=True), 0.0)
                acc["dacs"] = acc["dacs"] + jnp.where(lane == r, dacs_col, 0.0)
                acc["dacst"] = acc["dacst"] + jnp.where(subl == r, dacs_row, 0.0)
                acc["dd"] = acc["dd"] + jnp.where(lane == r, jnp.sum(dYh * xs_h), 0.0)

            yield from _rounds([head(r) for r in range(SSM_R)])
            dc_v[...] = dC + jnp.dot(acc["dG"].astype(BF16), Bb, preferred_element_type=F32)
            db_v[...] = dB + jnp.dot(acc["dGT"].astype(BF16), Cb, preferred_element_type=F32)
            ddt_v[...] = acc["ddt"]
            dacs_v[...] = acc["dacs"]
            dacst_v[...] = acc["dacst"]
            dd_v[...] += acc["dd"]

        _interleave([group(gg) for gg in range(SSM_GPS)])

    big = _sds((SSM_G, S, LANES))
    return _call(body, name="ssd_bwd",
                 out_shape=(_sds((S, SSM_G * SSM_HP)), _sds((S, SSM_G * SSM_N)), _sds((S, SSM_G * SSM_N)),
                            big, big, _sds((SSM_G, LANES, S)), _sds((SSM_G, 1, LANES))),
                 grid=(SSM_G // SSM_GPS, nc), in_specs=[xs_s, b_s, c_s, sc, sc, sct, gsp, st, xs_s],
                 out_specs=(xs_s, bc_out, bc_out, sc, sc, sct, gsp),
                 scratch=[pltpu.VMEM((SSM_GPS, SSM_HP, SSM_N), F32)])(xbc, xbc, xbc, dt_g, acs_g, acst_g, d_g, states, dy)


def _gate_norm_fwd(y, proj, w):
    S, DI = y.shape
    tm = _tile(S, 256)

    def body(y_ref, z_ref, w_ref, o_ref):
        z = z_ref[...]
        gn = y_ref[...] * (z * _sigmoid(z))
        r = lax.rsqrt(jnp.mean(gn * gn, axis=-1, keepdims=True) + SSM_NORM_EPS)
        o_ref[...] = (gn * r * w_ref[...]).astype(BF16)

    row = pl.BlockSpec((tm, DI), lambda i: (i, 0))
    return _call(body, name="gate_norm_fwd", out_shape=_sds((S, DI), BF16), grid=(S // tm,),
                 in_specs=[row, row, pl.BlockSpec((1, DI), lambda i: (0, 0))], out_specs=row)(y, proj, w)


def _gate_norm_bwd(y, proj, w, dout):
    S, DI = y.shape
    tm = _tile(S, 256)

    def body(y_ref, z_ref, w_ref, d_ref, dy_ref, dz_ref, dw_ref):
        z, yv = z_ref[...], y_ref[...]
        sz = z * _sigmoid(z)
        dgn, dw = _norm_bwd_math(yv * sz, w_ref[...], d_ref[...].astype(F32), SSM_NORM_EPS)
        dy_ref[...] = dgn * sz
        dz_ref[...] = (dgn * yv * _silu_grad(z)).astype(BF16)

        @pl.when(pl.program_id(0) == 0)
        def _():
            dw_ref[...] = jnp.zeros_like(dw_ref)

        dw_ref[...] += dw

    row = pl.BlockSpec((tm, DI), lambda i: (i, 0))
    vec = pl.BlockSpec((1, DI), lambda i: (0, 0))
    return _call(body, name="gate_norm_bwd", out_shape=(_sds((S, DI)), _sds((S, DI), BF16), _sds((1, DI))), grid=(S // tm,),
                 in_specs=[row, row, vec, row], out_specs=(row, row, vec))(y, proj, w, dout)


def _group_major(v):
    return jnp.pad(v.reshape(SSM_G, 1, SSM_R), ((0, 0), (0, 0), (0, LANES - SSM_R)))


def _ungroup(t):
    return t[:, :SSM_R].reshape(1, SSM_G * SSM_R)


def _ffn_fwd(x, P, l, need):
    need(f"ffn{l}_up", x)
    h = _rmsnorm(x, P["norm_ffn"][l:l + 1], name=f"ffn{l}_norm")
    wT = P[f"ffn_w_upT{l}"]
    F = wT.shape[0] // 2
    a = _mm(h, wT, tb=True, name=f"ffn{l}_up")
    need(f"ffn{l}_down", a)
    act = _ffn_mid_fwd(a, P["ffn_conv_w"], P["ffn_conv_b"], l)
    out = _mm(act, P[f"ffn_w_down{l}"], res=x, name=f"ffn{l}_down")
    return out, (x, h, a, act)


def _ffn_bwd(saved, P, l, dx, emit):
    x, h, a, act = saved
    wT = P[f"ffn_w_upT{l}"]
    F = wT.shape[0] // 2
    dact = _mm(dx, P[f"ffn_w_down{l}"], tb=True, out_dtype=BF16, name=f"ffn{l}_down_dx")
    dw_down = _mm(act, dx, ta=True, out_dtype=PAYLOAD, name=f"ffn{l}_down_dw")
    dau, dag, dcw, dcb = _ffn_mid_bwd(a, P["ffn_conv_w"], P["ffn_conv_b"], l, dact)
    dw_upT = _mm(dau, h, ta=True, out_dtype=PAYLOAD, out_rows=(2 * F, 0, None), name=f"ffn{l}_up_u_dw")
    dw_upT = _mm(dag, h, ta=True, out_dtype=PAYLOAD, out_rows=(2 * F, F, dw_upT), name=f"ffn{l}_up_g_dw")
    tie = emit(f"ffn{l}", {"ffn_w_upT": dw_upT, "ffn_w_down": dw_down})
    dh = _mm(dau, wT, b_rows=(0, F), name=f"ffn{l}_up_u_dx")
    dx_in, dnw = _mm_norm_bwd(dag, wT, x, P["norm_ffn"][l:l + 1], dx, res=dh, b_rows=(F, F), after=tie,
                              name=f"ffn{l}_up_g_dx_norm_bwd")
    return dx_in, dnw, dcw, dcb


def _local_step(x, positions, target, P, need, emit, after=None):
    S, D = x.shape
    inv_freq = ROPE_THETA ** (-jnp.arange(0, HEAD_DIM, 2, dtype=F32) / HEAD_DIM)
    inv_freq = jnp.tile(inv_freq, LANES // (HEAD_DIM // 2)).reshape(1, LANES)
    cos, sin = _rope_tables(positions, inv_freq)

    nm0 = P["norm_mix"][0:1]
    h0 = _rmsnorm(x, nm0, name="mix_norm", after=after)
    need("mix_in", h0)
    proj0 = _mm(h0, P["mix_w_inT"], tb=True, name="mix_in")
    cat0 = _attn_fwd(proj0, cos, sin, P["attn_sinks"], _pool_fwd(proj0, P["pool_w"][0], P["pool_scale"]))
    need("mix_out", cat0)
    x1 = _mm(cat0, P["mix_w_out"], res=x, name="mix_out")
    x2, ffn0 = _ffn_fwd(x1, P, 0, need)

    nm1 = P["norm_mix"][1:2]
    need("ssm", x2)
    h1 = _rmsnorm(x2, nm1, name="ssm_norm_in")
    w1T, wdtT = P["ssm_w_inT"], P["ssm_wdtT"]
    DI, CD, NH = P["ssm_norm"].shape[1], P["ssm_conv_w"].shape[1], P["ssm_dt_bias"].shape[1]
    z = _mm(h1, w1T, tb=True, b_rows=(0, DI), name="ssm_in_z")
    xbcp = _mm(h1, w1T, tb=True, b_rows=(DI, CD), name="ssm_in_xbc")
    dtraw = _mm(h1, wdtT, tb=True, name="ssm_in_dt")
    xbc = _conv_silu_fwd(xbcp, P["ssm_conv_w"], P["ssm_conv_b"])
    bias_g, alog_g, d_g = _group_major(P["ssm_dt_bias"]), _group_major(P["ssm_A_log"]), _group_major(P["ssm_D"])
    pre_g, dt_g, acs_g, acst_g = _ssd_prep_fwd(dtraw, 0, bias_g, alog_g)
    y, states = _ssd_fwd(xbc, dt_g, acs_g, acst_g, d_g)
    yn = _gate_norm_fwd(y, z, P["ssm_norm"])
    x3 = _mm(yn, P["ssm_w_out"], res=x2, name="ssm_out")
    x4, ffn1 = _ffn_fwd(x3, P, 1, need)

    loss, dx, d_norm_final = _final_loss(x4, P["norm_final"].reshape(1, D), target, name="final_loss")
    dx, dnf1, dcw1, dcb1 = _ffn_bwd(ffn1, P, 1, dx, emit)
    dyn = _mm(dx, P["ssm_w_out"], tb=True, out_dtype=BF16, name="ssm_out_dx")
    d_w_out1 = _mm(yn, dx, ta=True, out_dtype=PAYLOAD, name="ssm_out_dw")
    dy, dz, d_ssm_norm = _gate_norm_bwd(y, z, P["ssm_norm"], dyn)
    dxs, dB, dC, ddt_g, dacs_g, dacst_g, dd_g = _ssd_bwd(xbc, dt_g, acs_g, acst_g, d_g, states, dy)
    draw, dbias_g, dalog_g = _ssd_prep_bwd(pre_g, dt_g, alog_g, ddt_g, dacs_g, dacst_g)
    dxbc, d_conv_w1, d_conv_b1 = _conv_silu_bwd(xbcp, P["ssm_conv_w"], P["ssm_conv_b"], [dxs, dB, dC])
    rows = DI + CD + NH
    d_w1T = _mm(dz, h1, ta=True, out_dtype=PAYLOAD, out_rows=(rows, 0, None), name="ssm_in_z_dw")
    d_w1T = _mm(dxbc, h1, ta=True, out_dtype=PAYLOAD, out_rows=(rows, DI, d_w1T), name="ssm_in_xbc_dw")
    d_w1T = _mm(draw[:, :NH], h1, ta=True, out_dtype=PAYLOAD, out_rows=(rows, DI + CD, d_w1T), name="ssm_in_dt_dw")
    tie = emit("ssm", {"ssm_w_inT": d_w1T, "ssm_w_out": d_w_out1,
                       "ssm_conv_w": d_conv_w1, "ssm_conv_b": d_conv_b1, "ssm_norm": d_ssm_norm})
    dh1 = _mm(dz, w1T, b_rows=(0, DI), name="ssm_in_z_dx")
    dh1 = _mm(dxbc, w1T, b_rows=(DI, CD), res=dh1, name="ssm_in_xbc_dx")
    dx, dnm1 = _mm_norm_bwd(draw, wdtT, x2, nm1, dx, res=dh1, after=tie, name="ssm_in_dt_dx_norm_bwd")
    dx, dnf0, dcw0, dcb0 = _ffn_bwd(ffn0, P, 0, dx, emit)
    dcat = _mm(dx, P["mix_w_out"], tb=True, name="mix_out_dx")
    d_w_out0 = _mm(cat0, dx, ta=True, out_dtype=PAYLOAD, name="mix_out_dw")
    dproj0, dsk = _attn_bwd(proj0, cos, sin, P["attn_sinks"], dcat)
    dproj0, d_pool_w, d_pool_scale = _pool_bwd(proj0, P["pool_w"][0], P["pool_scale"], dcat, dproj0)
    d_w_in0 = _mm(dproj0, h0, ta=True, out_dtype=PAYLOAD, name="mix_in_dw")
    tie = emit("mix", {"mix_w_inT": d_w_in0, "mix_w_out": d_w_out0, "ffn_conv_w": jnp.stack([dcw0, dcw1])})
    grad_x, dnm0 = _mm_norm_bwd(dproj0, P["mix_w_inT"], x, nm0, dx, after=tie, name="mix_in_dx_norm_bwd")

    small = {
        "norm_mix": jnp.concatenate([dnm0, dnm1], axis=0),
        "norm_ffn": jnp.concatenate([dnf0, dnf1], axis=0),
        "norm_final": d_norm_final,
        "pool_w": d_pool_w,
        "pool_scale": d_pool_scale,
        "attn_sinks_rows": dsk,
        "ssm_dt_bias_g": dbias_g, "ssm_A_log_g": dalog_g, "ssm_D_g": dd_g,
        "ffn_conv_b": jnp.concatenate([dcb0, dcb1], axis=0),
    }
    return loss, grad_x, small


def _peer(k):
    x, y, c = lax.axis_index("x"), lax.axis_index("y"), lax.axis_index("c")
    px = 1 - x if k & 4 else x
    py = 1 - y if k & 2 else y
    pc = 1 - c if k & 1 else c
    return (px, py, pc), 4 * px + 2 * py + pc


def _my_index():
    return 4 * lax.axis_index("x") + 2 * lax.axis_index("y") + lax.axis_index("c")


def _land_sds(a, mode, gather):
    if mode == "slab":
        return _sds(((N_DEV,) + a.shape) if gather else a.shape, a.dtype)
    assert mode == "rows", mode
    return _sds((N_DEV * a.shape[0],) + a.shape[1:] if gather else (N_DEV, a.shape[0] // N_DEV) + a.shape[1:], a.dtype)


def _part(ref, mode, shape, idx):
    if mode == "slab":
        return ref.at[idx]
    r = shape[0] // N_DEV
    return ref.at[pl.ds(idx * r, r)]


def _own_copies(ops, gather, srcs, lands, sems):
    me = _my_index()
    out = []
    for i, (a, mode) in enumerate(ops):
        s = srcs[i] if gather else _part(srcs[i], mode, a.shape, me)
        d = _part(lands[i], mode, _land_sds(a, mode, gather).shape, me) if gather else lands[i].at[me]
        out.append(pltpu.make_async_copy(s, d, sems.at[i]))
    return out


def _remote_copies(ops, gather, srcs, lands, send_sems, recv_sems):
    me = _my_index()
    n = len(ops)
    out = []
    for k in range(1, N_DEV):
        dev, idx = _peer(k)
        for i, (a, mode) in enumerate(ops):
            s = srcs[i] if gather else _part(srcs[i], mode, a.shape, idx)
            d = _part(lands[i], mode, _land_sds(a, mode, gather).shape, me) if gather else lands[i].at[me]
            out.append(pltpu.make_async_remote_copy(src_ref=s, dst_ref=d, send_sem=send_sems.at[(k - 1) * n + i],
                                                    recv_sem=recv_sems.at[(k - 1) * n + i], device_id=dev,
                                                    device_id_type=pl.DeviceIdType.MESH))
    return out


def _exchange(ops, *, gather, name):
    n = len(ops)

    def body(*refs):
        ins, outs = refs[:n], refs[n:2 * n]
        send_sems, recv_sems, local_sems = refs[2 * n:]
        copies = _own_copies(ops, gather, ins, outs, local_sems) + _remote_copies(ops, gather, ins, outs, send_sems, recv_sems)
        for cp in copies:
            cp.start()
        for cp in copies:
            cp.wait()

    return pl.pallas_call(
        body, name=name, out_shape=[_land_sds(a, m, gather) for a, m in ops], in_specs=[ANY] * n, out_specs=[ANY] * n,
        scratch_shapes=[pltpu.SemaphoreType.DMA((n * (N_DEV - 1),)), pltpu.SemaphoreType.DMA((n * (N_DEV - 1),)),
                        pltpu.SemaphoreType.DMA((n,))],
    )(*[a for a, _ in ops])


HBM = pl.BlockSpec(memory_space=pltpu.HBM)
SEM = pl.BlockSpec(memory_space=pltpu.SEMAPHORE)
SIDE_EFFECT = pltpu.SideEffectType.DATAFLOW_SIDE_EFFECTING


def _in_hbm(a):
    return pltpu.with_memory_space_constraint(a, pltpu.HBM)


def _place_own(ops, *, gather, name):
    n = len(ops)

    def zeros(k):
        return (0,) * k

    in_specs, out_specs = [], []
    for a, mode in ops:
        nd = a.ndim
        if gather and mode == "slab":
            in_specs.append(pl.BlockSpec(a.shape, lambda i, nd=nd: zeros(nd)))
            out_specs.append(pl.BlockSpec((1,) + a.shape, lambda i, nd=nd: (_my_index(),) + zeros(nd)))
        elif gather:
            in_specs.append(pl.BlockSpec(a.shape, lambda i, nd=nd: zeros(nd)))
            out_specs.append(pl.BlockSpec(a.shape, lambda i, nd=nd: (_my_index(),) + zeros(nd - 1)))
        elif mode == "slab":
            in_specs.append(pl.BlockSpec((1,) + a.shape[1:], lambda i, nd=nd: (_my_index(),) + zeros(nd - 1)))
            out_specs.append(pl.BlockSpec((1,) + a.shape[1:], lambda i, nd=nd: (_my_index(),) + zeros(nd - 1)))
        else:
            r = a.shape[0] // N_DEV
            in_specs.append(pl.BlockSpec((r,) + a.shape[1:], lambda i, nd=nd: (_my_index(),) + zeros(nd - 1)))
            out_specs.append(pl.BlockSpec((1, r) + a.shape[1:], lambda i, nd=nd: (_my_index(),) + zeros(nd)))

    def body(*refs):
        for i_ref, o_ref in zip(refs[:n], refs[n:2 * n]):
            if o_ref.shape == i_ref.shape:
                o_ref[...] = i_ref[...]
            else:
                o_ref[0] = i_ref[...]

    outs = _call(body, name=name, grid=(1,), in_specs=in_specs, out_specs=out_specs + [ANY] * n,
                 out_shape=[_land_sds(a, m, gather) for a, m in ops] + [_sds(a.shape, a.dtype) for a, _ in ops],
                 aliases={i: n + i for i in range(n)})(*[a for a, _ in ops])
    return outs[:n], [(src, m) for src, (_, m) in zip(outs[n:], ops)]


def _exchange_start(groups, *, gather, name):
    sizes = [len(ops) for ops, _ in groups]
    n = sum(sizes)
    G = len(groups)

    def body(*refs):
        srcs, lands = refs[:n], refs[n:2 * n]
        sems = refs[2 * n:2 * n + 2 * G]
        token = refs[-1]
        off = 0
        for g, (ops, _) in enumerate(groups):
            for cp in _remote_copies(ops, gather, srcs[off:off + sizes[g]], lands[off:off + sizes[g]], sems[2 * g], sems[2 * g + 1]):
                cp.start()
            off += sizes[g]
        token[...] = jnp.zeros_like(token)

    srcs = [a for ops, _ in groups for a, _ in ops]
    lands = [l for _, ls in groups for l in ls]
    sem_shapes = [pltpu.SemaphoreType.DMA((s * (N_DEV - 1),)) for s in sizes for _ in range(2)]
    outs = pl.pallas_call(
        body, name=name,
        out_shape=sem_shapes + [pltpu.HBM(a.shape, a.dtype) for a in srcs + lands] + [_sds((8, LANES))],
        in_specs=[HBM] * (2 * n), out_specs=[SEM] * (2 * G) + [HBM] * (2 * n) + [pl.BlockSpec(memory_space=pltpu.VMEM)],
        input_output_aliases={i: 2 * G + i for i in range(2 * n)},
        compiler_params=pltpu.CompilerParams(has_side_effects=SIDE_EFFECT))(*[_in_hbm(a) for a in srcs + lands])
    sems, thru, token = outs[:2 * G], outs[2 * G:2 * G + 2 * n], outs[-1]
    states, off = [], 0
    for g, s in enumerate(sizes):
        states.append((sems[2 * g], sems[2 * g + 1], thru[off:off + s], thru[n + off:n + off + s]))
        off += s
    return states, token


def _exchange_wait(ops, state, after, *, gather, name):
    send_sems, recv_sems, srcs, lands = state
    n = len(ops)

    def body(*refs):
        for cp in _remote_copies(ops, gather, refs[:n], refs[n:2 * n], refs[2 * n], refs[2 * n + 1]):
            cp.wait_send()
            cp.wait_recv()

    outs = pl.pallas_call(
        body, name=name, out_shape=[pltpu.HBM(a.shape, a.dtype) for a in list(srcs) + list(lands)],
        in_specs=[HBM] * (2 * n) + [SEM, SEM, ANY], out_specs=[HBM] * (2 * n),
        input_output_aliases={i: i for i in range(2 * n)},
        compiler_params=pltpu.CompilerParams(has_side_effects=SIDE_EFFECT))(*srcs, *lands, send_sems, recv_sems, after)
    return outs[n:]


ADAM_ROWS = 256


def _row_tile(R, cap=ADAM_ROWS):
    best = R
    if R > cap:
        for d in range(16, cap + 1, 16):
            if R % d == 0:
                best = d
    return best


def _adamw(g_layers, w, m, v, *, name):
    L = len(g_layers)
    J, R, Wd = g_layers[0].shape
    assert w.shape == (L, R, Wd), (g_layers[0].shape, w.shape)
    tr = _row_tile(R)
    nrt = R // tr
    c1 = 1.0 / (1.0 - ADAM_B1 ** ADAM_STEP)
    c2 = 1.0 / (1.0 - ADAM_B2 ** ADAM_STEP)

    def body(*refs):
        g_refs = refs[:L]
        w_ref, m_ref, v_ref, go_ref, d_ref, mo_ref, vo_ref = refs[L:]
        layer = pl.program_id(0)
        g = None
        for l, g_ref in enumerate(g_refs):
            gl = g_ref[0].astype(F32)
            for j in range(1, J):
                gl = gl + g_ref[j].astype(F32)
            g = gl if g is None else jnp.where(layer == l, gl, g)
        mn = ADAM_B1 * m_ref[...] + (1.0 - ADAM_B1) * g
        vn = ADAM_B2 * v_ref[...] + (1.0 - ADAM_B2) * (g * g)
        go_ref[...] = g
        mo_ref[...] = mn
        vo_ref[...] = vn
        d_ref[...] = -ADAM_LR * ((mn * c1) / (jnp.sqrt(vn * c2) + ADAM_EPS) + ADAM_WD * w_ref[...])

    def g_spec(l):
        return pl.BlockSpec((J, tr, Wd), lambda ll, i: (0, jnp.where(ll == l, i, jnp.where(ll < l, 0, nrt - 1)), 0))

    row = pl.BlockSpec((None, tr, Wd), lambda ll, i: (ll, i, 0))
    out = _sds((L, R, Wd))
    return _call(body, name=name, out_shape=(out, out, out, out), grid=(L, nrt),
                 in_specs=[g_spec(l) for l in range(L)] + [row, row, row], out_specs=(row, row, row, row))(*g_layers, w, m, v)


def _sum_slabs(slabs, *, name):
    n = len(slabs)

    def body(*refs):
        for g_ref, o_ref in zip(refs[:n], refs[n:]):
            g = g_ref[0]
            for j in range(1, g_ref.shape[0]):
                g = g + g_ref[j]
            o_ref[...] = g

    return _call(body, name=name, out_shape=[_sds(s.shape[1:]) for s in slabs])(*slabs)


def kernel(x, positions, norm_mix, norm_ffn, norm_final, mix_w_in, pool_w, pool_scale, attn_sinks, mix_w_out, ssm_w_in, ssm_conv_w, ssm_conv_b, ssm_dt_bias, ssm_A_log, ssm_D, ssm_norm, ssm_w_out, ffn_w_up, ffn_conv_w, ffn_conv_b, ffn_w_down, loss_target, m_norm_mix, m_norm_ffn, m_norm_final, m_mix_w_in, m_pool_w, m_pool_scale, m_attn_sinks, m_mix_w_out, m_ssm_w_in, m_ssm_conv_w, m_ssm_conv_b, m_ssm_dt_bias, m_ssm_A_log, m_ssm_D, m_ssm_norm, m_ssm_w_out, m_ffn_w_up, m_ffn_conv_w, m_ffn_conv_b, m_ffn_w_down, v_norm_mix, v_norm_ffn, v_norm_final, v_mix_w_in, v_pool_w, v_pool_scale, v_attn_sinks, v_mix_w_out, v_ssm_w_in, v_ssm_conv_w, v_ssm_conv_b, v_ssm_dt_bias, v_ssm_A_log, v_ssm_D, v_ssm_norm, v_ssm_w_out, v_ffn_w_up, v_ffn_conv_w, v_ffn_conv_b, v_ffn_w_down):
    args = dict(locals())
    wl = {n: args[n] for n in WEIGHTS}
    ml = {n: args["m_" + n] for n in WEIGHTS}
    vl = {n: args["v_" + n] for n in WEIGHTS}
    F = ffn_w_down.shape[1] * N_DEV
    DI, CD, NH = ssm_norm.shape[1] * N_DEV, ssm_conv_b.shape[1] * N_DEV, ssm_dt_bias.shape[1]
    Kc, Kf = ssm_conv_w.shape[1], ffn_conv_w.shape[1]
    n_up = ffn_w_up.shape[2]
    col_sharded = ("mix_w_in", "ssm_w_in", "ffn_w_up")

    def tr(a):
        return jnp.swapaxes(a, -1, -2)

    def two(a):
        return a.reshape(-1, a.shape[-1])

    def pay(a):
        return a.astype(PAYLOAD)

    order = ("mix_in", "mix_out", "ffn0_up", "ffn0_down", "ssm", "ffn1_up", "ffn1_down")
    gops = {
        "mix_in": [(pay(tr(mix_w_in)[0]), "rows")],
        "mix_out": [(pay(mix_w_out[0]), "rows"), (two(ssm_conv_w), "slab"), (ssm_conv_b, "slab"), (ssm_norm, "slab"),
                    (two(ffn_conv_w), "slab")],
        "ffn0_up": [(pay(tr(ffn_w_up)[0]), "rows")], "ffn0_down": [(pay(ffn_w_down[0]), "rows")],
        "ssm": [(pay(tr(ssm_w_in)[0]), "slab"), (pay(ssm_w_out[0]), "rows")],
        "ffn1_up": [(pay(tr(ffn_w_up)[1]), "rows")], "ffn1_down": [(pay(ffn_w_down[1]), "rows")],
    }
    lands, handed = _place_own([op for g in order for op in gops[g]], gather=True, name="gather_own")
    groups, off = [], 0
    for g in order:
        gops[g] = handed[off:off + len(gops[g])]
        groups.append((gops[g], lands[off:off + len(gops[g])]))
        off += len(gops[g])
    gstates, token = _exchange_start(groups, gather=True, name="gather_start")
    gstate = dict(zip(order, gstates))
    P = {n: wl[n] for n in REPLICATED}

    def need(g, after):
        got = _exchange_wait(gops[g], gstate[g], after, gather=True, name="gather_wait_" + g)
        if g == "mix_in":
            P["mix_w_inT"] = got[0]
        elif g == "mix_out":
            P.update(mix_w_out=got[0], ssm_conv_w=got[1].transpose(1, 0, 2).reshape(Kc, CD), ssm_conv_b=got[2].reshape(1, CD),
                     ssm_norm=got[3].reshape(1, DI), ffn_conv_w=got[4].transpose(1, 0, 2).reshape(2, Kf, 2 * F))
        elif g == "ssm":
            w1T = got[0].reshape(-1, got[0].shape[-1])
            P.update(ssm_w_inT=w1T, ssm_w_out=got[1], ssm_wdtT=jnp.pad(w1T[DI + CD:], ((0, LANES - NH), (0, 0))))
        elif g.endswith("_up"):
            P["ffn_w_upT" + g[3]] = got[0]
        else:
            P["ffn_w_down" + g[3]] = got[0]

    sent = {}

    def emit(g, d):
        if g == "mix":
            ops = [(d["mix_w_inT"], "rows"), (d["mix_w_out"], "rows"),
                   (d["ffn_conv_w"].reshape(2 * Kf, N_DEV, n_up).transpose(1, 0, 2), "slab")]
        elif g == "ssm":
            ops = [(d["ssm_w_inT"].reshape(N_DEV, -1, d["ssm_w_inT"].shape[-1]), "slab"), (d["ssm_w_out"], "rows"),
                   (d["ssm_conv_w"].reshape(Kc, N_DEV, -1).transpose(1, 0, 2), "slab"),
                   (d["ssm_conv_b"].reshape(N_DEV, 1, -1), "slab"), (d["ssm_norm"].reshape(N_DEV, 1, -1), "slab")]
        else:
            ops = [(d["ffn_w_upT"], "rows"), (d["ffn_w_down"], "rows")]
        own, ops = _place_own(ops, gather=False, name="scatter_own_" + g)
        (state,), tok = _exchange_start([(ops, own)], gather=False, name="scatter_start_" + g)
        sent[g] = (ops, state)
        return tok

    loss_lanes, grad_x, G = _local_step(x[0], positions.reshape(-1, 1), loss_target[0], P, need, emit, after=token)
    loss = lax.psum(loss_lanes[0, 0], ("x", "y", "c"))

    res = {}

    def update(n, g_layers):
        L = len(g_layers)
        g_layers = [g.reshape(g.shape[0], -1, g.shape[-1]) for g in g_layers]
        shape = (L,) + g_layers[0].shape[1:]
        view = tr if n in col_sharded else (lambda a: a)
        outs = _adamw(g_layers, view(wl[n]).reshape(shape), view(ml[n]).reshape(shape), view(vl[n]).reshape(shape),
                      name="adamw_" + n)
        for kind, a in zip(("grad", "delta", "new_m", "new_v"), outs):
            res[kind, n] = view(a.reshape(view(wl[n]).shape))

    rep_ops = [(a, "slab") for a in (G["norm_mix"], G["norm_ffn"], G["norm_final"], G["pool_w"].reshape(-1, LANES),
                                     G["pool_scale"], G["ffn_conv_b"], G["attn_sinks_rows"],
                                     G["ssm_dt_bias_g"].reshape(SSM_G, LANES), G["ssm_A_log_g"].reshape(SSM_G, LANES),
                                     G["ssm_D_g"].reshape(SSM_G, LANES))]
    rep_own, rep_ops = _place_own(rep_ops, gather=True, name="small_own")
    (rep_state,), rep_token = _exchange_start([(rep_ops, rep_own)], gather=True, name="small_start")

    recv = {g: _exchange_wait(sent[g][0], sent[g][1], rep_token, gather=False, name="scatter_wait_" + g)
            for g in ("ffn1", "ssm", "ffn0", "mix")}
    update("mix_w_in", [recv["mix"][0]])
    update("mix_w_out", [recv["mix"][1]])
    update("ffn_conv_w", [recv["mix"][2]])
    update("ssm_w_in", [recv["ssm"][0]])
    update("ssm_w_out", [recv["ssm"][1]])
    update("ssm_conv_w", [recv["ssm"][2]])
    update("ssm_conv_b", [recv["ssm"][3]])
    update("ssm_norm", [recv["ssm"][4]])
    update("ffn_w_up", [recv["ffn0"][0], recv["ffn1"][0]])
    update("ffn_w_down", [recv["ffn0"][1], recv["ffn1"][1]])

    rep = _exchange_wait(rep_ops, rep_state, res["new_v", "ffn_w_down"], gather=True, name="small_wait")
    for n, r in zip(("norm_mix", "norm_ffn", "norm_final", "pool_w", "pool_scale", "ffn_conv_b"), rep):
        update(n, [r])
    sinks_rows, bias_g, alog_g, d_g = _sum_slabs(rep[6:], name="sum_head_grads")
    update("attn_sinks", [sinks_rows[:, 0].reshape(1, 1, N_HEADS)])
    update("ssm_dt_bias", [_ungroup(bias_g)[None]])
    update("ssm_A_log", [_ungroup(alog_g)[None]])
    update("ssm_D", [_ungroup(d_g)[None]])

    return (loss, grad_x[None], *[res[k, n] for k in ("grad", "delta", "new_m", "new_v") for n in WEIGHTS])
```

```python
import functools
import math

import jax
import jax.numpy as jnp
from jax import lax
from jax.experimental import pallas as pl
from jax.experimental.pallas import tpu as pltpu

F32 = jnp.float32
BF16 = jnp.bfloat16

N_DEV = 8
LANES = 128
HEAD_DIM = 64
N_KV_HEADS = 2
GQ = 4
N_HEADS = N_KV_HEADS * GQ
BLOCK = 128
POOL_GROUPS = 4
ROPE_THETA = 10000.0
SSM_P = 64
SSM_G = 8
SSM_R = 4
SSM_N = 128
SSM_L = 128
NORM_EPS = 1e-6
SSM_NORM_EPS = 1e-5
ADAM_LR, ADAM_B1, ADAM_B2, ADAM_EPS, ADAM_WD, ADAM_STEP = 0.001, 0.9, 0.999, 1e-08, 0.01, 10
VMEM_LIMIT = 56 * 2 ** 20
PAYLOAD = jnp.bfloat16

REPLICATED = ("norm_mix", "norm_ffn", "norm_final", "pool_w", "pool_scale", "attn_sinks",
              "ssm_dt_bias", "ssm_A_log", "ssm_D", "ffn_conv_b")
WEIGHTS = ("norm_mix", "norm_ffn", "norm_final", "mix_w_in", "pool_w", "pool_scale", "attn_sinks", "mix_w_out",
           "ssm_w_in", "ssm_conv_w", "ssm_conv_b", "ssm_dt_bias", "ssm_A_log", "ssm_D", "ssm_norm", "ssm_w_out",
           "ffn_w_up", "ffn_conv_w", "ffn_conv_b", "ffn_w_down")


def _tile(n, cap):
    if n <= cap:
        return n
    best = None
    for d in range(LANES, cap + 1, LANES):
        if n % d == 0:
            best = d
    assert best is not None, (n, cap)
    return best


def _call(body, *, name, out_shape, grid=None, in_specs=None, out_specs=None, scratch=(), aliases=None):
    kw = {}
    if grid is not None:
        kw = dict(grid=grid, in_specs=in_specs, out_specs=out_specs)
    if aliases:
        kw["input_output_aliases"] = aliases
    return pl.pallas_call(
        body, name=name, out_shape=out_shape, scratch_shapes=list(scratch),
        compiler_params=pltpu.CompilerParams(vmem_limit_bytes=VMEM_LIMIT), **kw)


ANY = pl.BlockSpec(memory_space=pl.ANY)


def _sds(shape, dtype=F32):
    return jax.ShapeDtypeStruct(tuple(shape), dtype)


def _sigmoid(x):
    return 1.0 / (1.0 + jnp.exp(-x))


def _shift_dn(x, d, t):
    if d == 0:
        return x
    return jnp.where(t >= d, pltpu.roll(x, d, axis=0), 0.0)


def _shift_up(x, d, t):
    if d == 0:
        return x
    n = x.shape[0]
    return jnp.where(t < n - d, pltpu.roll(x, n - d, axis=0), 0.0)


def _mm(a, b, *, name, ta=False, tb=False, res=None, out_dtype=F32, b_rows=None, out_rows=None):
    M, K = (a.shape[1], a.shape[0]) if ta else a.shape
    b0, bn = b_rows if b_rows is not None else (0, b.shape[0])
    N = bn if tb else b.shape[1]
    assert (b.shape[1] if tb else bn) == K, (a.shape, b.shape, ta, tb, b_rows)
    tm, tn, tk = _tile(M, 1408), _tile(N, 1408), _tile(K, 1408)
    nk = K // tk
    dims = (((0 if ta else 1,), (1 if tb else 0,)), ((), ()))
    aliased = out_rows is not None and out_rows[2] is not None

    def body(*refs):
        a_ref, b_ref = refs[:2]
        r_ref = refs[2] if res is not None else None
        o_ref, acc = refs[-2:]
        k = pl.program_id(2)

        @pl.when(k == 0)
        def _():
            acc[...] = jnp.zeros_like(acc)

        acc[...] += lax.dot_general(a_ref[...].astype(BF16), b_ref[...].astype(BF16), dims,
                                    preferred_element_type=F32)

        @pl.when(k == nk - 1)
        def _():
            out = acc[...]
            if res is not None:
                out = out + r_ref[...]
            o_ref[...] = out.astype(out_dtype)

    a_spec = pl.BlockSpec((tk, tm), lambda i, j, k: (k, i)) if ta else pl.BlockSpec((tm, tk), lambda i, j, k: (i, k))
    if tb:
        assert b0 % tn == 0, (b_rows, tn)
        b_spec = pl.BlockSpec((tn, tk), lambda i, j, k: (b0 // tn + j, k))
    else:
        assert b0 % tk == 0, (b_rows, tk)
        b_spec = pl.BlockSpec((tk, tn), lambda i, j, k: (b0 // tk + k, j))
    ins, specs = [a, b], [a_spec, b_spec]
    if res is not None:
        ins.append(res)
        specs.append(pl.BlockSpec((tm, tn), lambda i, j, k: (i, j)))
    aliases = None
    if out_rows is None:
        o_spec = pl.BlockSpec((tm, tn), lambda i, j, k: (i, j))
        out_shape = _sds((M, N), out_dtype)
    else:
        total, o0, prev = out_rows
        assert o0 % tm == 0, (out_rows, tm)
        o_spec = pl.BlockSpec((tm, tn), lambda i, j, k: (o0 // tm + i, j))
        out_shape = _sds((total, N), out_dtype)
        if aliased:
            aliases = {len(ins): 0}
            ins.append(prev)
            specs.append(ANY)
    return _call(body, name=name, out_shape=out_shape, grid=(M // tm, N // tn, nk), in_specs=specs,
                 out_specs=o_spec, scratch=[pltpu.VMEM((tm, tn), F32)], aliases=aliases)(*ins)


def _rmsnorm(x, w, *, name, eps=NORM_EPS, after=None):
    S, D = x.shape
    tm = _tile(S, 512)
    tie = [] if after is None else [after]

    def body(x_ref, w_ref, *rest):
        o_ref = rest[-1]
        xf = x_ref[...]
        r = lax.rsqrt(jnp.mean(xf * xf, axis=-1, keepdims=True) + eps)
        o_ref[...] = (xf * r * w_ref[...]).astype(BF16)

    return _call(body, name=name, out_shape=_sds((S, D), BF16), grid=(S // tm,),
                 in_specs=[pl.BlockSpec((tm, D), lambda i: (i, 0)), pl.BlockSpec((1, D), lambda i: (0, 0))] + [ANY] * len(tie),
                 out_specs=pl.BlockSpec((tm, D), lambda i: (i, 0)))(x, w, *tie)


def _norm_bwd_math(xf, w, dh, eps):
    r = lax.rsqrt(jnp.mean(xf * xf, axis=-1, keepdims=True) + eps)
    xhat = xf * r
    dxh = dh * w
    dx = r * (dxh - xhat * jnp.mean(dxh * xhat, axis=-1, keepdims=True))
    dw = jnp.sum(dh * xhat, axis=0, keepdims=True)
    return dx, dw


def _mm_norm_bwd(a, b, x, w, dres, *, name, res=None, b_rows=None, after=None, eps=NORM_EPS):
    M, K = a.shape
    b0, bn = b_rows if b_rows is not None else (0, b.shape[0])
    D = b.shape[1]
    assert bn == K and x.shape == (M, D), (a.shape, b.shape, b_rows, x.shape)
    tm, tk = _tile(M, 512), _tile(K, 1408)
    assert b0 % tk == 0, (b_rows, tk)
    nk = K // tk

    def body(*refs):
        a_ref, b_ref, x_ref, w_ref, dr_ref = refs[:5]
        r_ref = refs[5] if res is not None else None
        dx_ref, dw_ref, acc = refs[-3:]
        i, k = pl.program_id(0), pl.program_id(1)

        @pl.when(k == 0)
        def _():
            acc[...] = jnp.zeros_like(acc)

        @pl.when((i == 0) & (k == 0))
        def _():
            dw_ref[...] = jnp.zeros_like(dw_ref)

        acc[...] += jnp.dot(a_ref[...].astype(BF16), b_ref[...].astype(BF16), preferred_element_type=F32)

        @pl.when(k == nk - 1)
        def _():
            dh = acc[...] if res is None else acc[...] + r_ref[...]
            dx, dw = _norm_bwd_math(x_ref[...], w_ref[...], dh, eps)
            dx_ref[...] = dr_ref[...] + dx
            dw_ref[...] += dw

    row = pl.BlockSpec((tm, D), lambda i, k: (i, 0))
    vec = pl.BlockSpec((1, D), lambda i, k: (0, 0))
    ins = [a, b, x, w, dres]
    specs = [pl.BlockSpec((tm, tk), lambda i, k: (i, k)), pl.BlockSpec((tk, D), lambda i, k: (b0 // tk + k, 0)), row, vec, row]
    if res is not None:
        ins.append(res)
        specs.append(row)
    if after is not None:
        ins.append(after)
        specs.append(ANY)
    return _call(body, name=name, out_shape=(_sds((M, D)), _sds((1, D))), grid=(M // tm, nk), in_specs=specs,
                 out_specs=(row, vec), scratch=[pltpu.VMEM((tm, D), F32)])(*ins)


def _final_loss(x, w, target, *, name):
    S, D = x.shape
    tm = _tile(S, 512)

    def body(x_ref, w_ref, t_ref, loss_ref, dx_ref, dw_ref):
        xf, wv = x_ref[...], w_ref[...]
        r = lax.rsqrt(jnp.mean(xf * xf, axis=-1, keepdims=True) + NORM_EPS)
        err = xf * r * wv - t_ref[...]
        part = 0.5 * jnp.sum(jnp.mean(err * err, axis=-1, keepdims=True), axis=0, keepdims=True)
        dx, dw = _norm_bwd_math(xf, wv, err * (1.0 / D), NORM_EPS)
        dx_ref[...] = dx

        @pl.when(pl.program_id(0) == 0)
        def _():
            dw_ref[...] = jnp.zeros_like(dw_ref)
            loss_ref[...] = jnp.zeros_like(loss_ref)

        dw_ref[...] += dw
        loss_ref[...] += jnp.broadcast_to(part, loss_ref.shape)

    row = pl.BlockSpec((tm, D), lambda i: (i, 0))
    vec = pl.BlockSpec((1, D), lambda i: (0, 0))
    return _call(body, name=name, out_shape=(_sds((1, LANES)), _sds((S, D)), _sds((1, D))), grid=(S // tm,),
                 in_specs=[row, vec, row], out_specs=(pl.BlockSpec((1, LANES), lambda i: (0, 0)), row, vec))(x, w, target)


def _rope_tables(pos, inv_freq):
    S = pos.shape[0]
    tm = _tile(S, 512)

    def body(p_ref, f_ref, c_ref, s_ref):
        ang = p_ref[...].astype(F32) * f_ref[...]
        c_ref[...] = jnp.cos(ang)
        s_ref[...] = jnp.sin(ang)

    blk = pl.BlockSpec((tm, LANES), lambda i: (i, 0))
    return _call(body, name="rope_tables", out_shape=(_sds((S, LANES)), _sds((S, LANES))), grid=(S // tm,),
                 in_specs=[pl.BlockSpec((tm, 1), lambda i: (i, 0)), pl.BlockSpec((1, LANES), lambda i: (0, 0))],
                 out_specs=(blk, blk))(pos, inv_freq)


def _rot_half(t):
    lane = lax.broadcasted_iota(jnp.int32, t.shape, 1)
    lo = (lane % HEAD_DIM) < (HEAD_DIM // 2)
    return jnp.where(lo, -pltpu.roll(t, LANES - HEAD_DIM // 2, axis=1), pltpu.roll(t, HEAD_DIM // 2, axis=1))


def _rope(t, c, s):
    return t * c + _rot_half(t) * s


def _unrope(dy, c, s):
    return dy * c - _rot_half(dy * s)


PD = POOL_GROUPS * LANES
QD = N_HEADS * HEAD_DIM
KD = N_KV_HEADS * HEAD_DIM
assert PD % QD == 0 and (PD + QD) % (2 * KD) == 0 and KD == LANES
def _attn_probs(q, kcat, sink, mask):
    s = lax.dot_general(q.astype(BF16), kcat, (((1,), (1,)), ((), ())), preferred_element_type=F32) * (HEAD_DIM ** -0.5)
    s = jnp.where(mask, s, -jnp.inf)
    m = jnp.maximum(jnp.max(s, axis=1, keepdims=True), sink)
    p = jnp.exp(s - m)
    ps = jnp.exp(sink - m)
    inv = 1.0 / (jnp.sum(p, axis=1, keepdims=True) + ps)
    return p * inv, ps * inv


def _attn_mask(n):
    qi = lax.broadcasted_iota(jnp.int32, (BLOCK, 2 * BLOCK), 0)
    kj = lax.broadcasted_iota(jnp.int32, (BLOCK, 2 * BLOCK), 1)
    rel = qi + BLOCK - kj
    return (rel >= 0) & (rel < BLOCK) & ((n > 0) | (kj >= BLOCK))


def _attn_in_specs(nb):
    def cur(n):
        return jnp.minimum(n, nb - 1)

    def prev(n):
        return jnp.clip(n - 1, 0, nb - 1)

    kvb = (PD + QD) // (2 * KD)
    return [pl.BlockSpec(memory_space=pltpu.SMEM),
            pl.BlockSpec((BLOCK, QD), lambda n: (cur(n), PD // QD)),
            pl.BlockSpec((BLOCK, 2 * KD), lambda n: (cur(n), kvb)),
            pl.BlockSpec((BLOCK, 2 * KD), lambda n: (prev(n), kvb)),
            pl.BlockSpec((BLOCK, LANES), lambda n: (cur(n), 0)), pl.BlockSpec((BLOCK, LANES), lambda n: (cur(n), 0)),
            pl.BlockSpec((BLOCK, LANES), lambda n: (prev(n), 0)), pl.BlockSpec((BLOCK, LANES), lambda n: (prev(n), 0))]


def _attn_keys(kvc_ref, kvp_ref, cc, sc, cp, sp):
    kc = _rope(kvc_ref[:, :KD], cc, sc)
    kp = _rope(kvp_ref[:, :KD], cp, sp)
    vc, vp = kvc_ref[:, KD:], kvp_ref[:, KD:]
    kcat, vcat = [], []
    for kk in range(N_KV_HEADS):
        sl = slice(kk * HEAD_DIM, (kk + 1) * HEAD_DIM)
        kcat.append(jnp.concatenate([kp[:, sl], kc[:, sl]], axis=0).astype(BF16))
        vcat.append(jnp.concatenate([vp[:, sl], vc[:, sl]], axis=0).astype(BF16))
    return kcat, vcat


def _attn_fwd(proj, cos, sin, sinks, cat):
    S = proj.shape[0]
    nb = S // BLOCK

    def body(sink_ref, q_ref, kvc_ref, kvp_ref, cc_ref, sc_ref, cp_ref, sp_ref, cat_ref, o_ref):
        n = pl.program_id(0)
        cc, sc = cc_ref[...], sc_ref[...]
        kcat, vcat = _attn_keys(kvc_ref, kvp_ref, cc, sc, cp_ref[...], sp_ref[...])
        mask = _attn_mask(n)
        def head_pair(j):
            qr = _rope(q_ref[:, j * LANES:(j + 1) * LANES], cc, sc)
            for e in range(LANES // HEAD_DIM):
                yield
                h = j * (LANES // HEAD_DIM) + e
                pn, _ = _attn_probs(qr[:, e * HEAD_DIM:(e + 1) * HEAD_DIM], kcat[h // GQ], sink_ref[0, h], mask)
                yield
                o_ref[:, h * HEAD_DIM:(h + 1) * HEAD_DIM] = jnp.dot(pn.astype(BF16), vcat[h // GQ], preferred_element_type=F32)

        _interleave([head_pair(j) for j in range(QD // LANES)])

    return _call(body, name="attn_fwd", out_shape=_sds(cat.shape), grid=(nb,),
                 in_specs=_attn_in_specs(nb) + [ANY], out_specs=pl.BlockSpec((BLOCK, QD), lambda n: (n, PD // QD)),
                 aliases={8: 0})(sinks, proj, proj, proj, cos, sin, cos, sin, cat)


def _attn_bwd(proj, cos, sin, sinks, dcat):
    S = proj.shape[0]
    nb = S // BLOCK
    scale = HEAD_DIM ** -0.5
    per = LANES // HEAD_DIM

    def body(sink_ref, q_ref, kvc_ref, kvp_ref, cc_ref, sc_ref, cp_ref, sp_ref, do_ref, o_ref, ds_ref, hold, carry, part, pair):
        n = pl.program_id(0)

        @pl.when(n == 0)
        def _():
            hold[...] = jnp.zeros_like(hold)
            carry[...] = jnp.zeros_like(carry)
            ds_ref[...] = jnp.zeros_like(ds_ref)

        live = jnp.where(n < nb, 1.0, 0.0)
        cc, sc, cp, sp = cc_ref[...], sc_ref[...], cp_ref[...], sp_ref[...]
        kcat, vcat = _attn_keys(kvc_ref, kvp_ref, cc, sc, cp, sp)
        mask = _attn_mask(n)
        o_ref[:, :PD] = jnp.zeros((BLOCK, PD), F32)
        o_ref[:, PD:PD + QD] = hold[...]
        dk = [jnp.zeros((2 * BLOCK, HEAD_DIM), F32) for _ in range(N_KV_HEADS)]
        dv = [jnp.zeros((2 * BLOCK, HEAD_DIM), F32) for _ in range(N_KV_HEADS)]
        row = lax.broadcasted_iota(jnp.int32, (8, LANES), 0)
        acc = {"dsk": jnp.zeros((8, LANES), F32)}

        def head_pair(j):
            qr = _rope(q_ref[:, j * LANES:(j + 1) * LANES], cc, sc)
            for e in range(per):
                yield
                h = j * per + e
                kk = h // GQ
                qh = qr[:, e * HEAD_DIM:(e + 1) * HEAD_DIM]
                pn, psn = _attn_probs(qh, kcat[kk], sink_ref[0, h], mask)
                yield
                doh = (do_ref[:, h * HEAD_DIM:(h + 1) * HEAD_DIM] * live).astype(BF16)
                dp = lax.dot_general(doh, vcat[kk], NT, preferred_element_type=F32)
                yield
                delta = jnp.sum(pn * dp, axis=1, keepdims=True)
                ds = (pn * (dp - delta) * scale).astype(BF16)
                pair[j, :, e * HEAD_DIM:(e + 1) * HEAD_DIM] = jnp.dot(ds, kcat[kk], preferred_element_type=F32)
                yield
                dk[kk] = dk[kk] + lax.dot_general(ds, qh.astype(BF16), TN, preferred_element_type=F32)
                dv[kk] = dv[kk] + lax.dot_general(pn.astype(BF16), doh, TN, preferred_element_type=F32)
                acc["dsk"] = acc["dsk"] + jnp.where(row == h, -jnp.sum(psn * delta), 0.0)
            yield
            hold[:, j * LANES:(j + 1) * LANES] = _unrope(pair[j], cc, sc)

        _interleave([head_pair(j) for j in range(QD // LANES)])
        dsk = acc["dsk"]
        for kk in range(N_KV_HEADS):
            sl = slice(kk * HEAD_DIM, (kk + 1) * HEAD_DIM)
            sv = slice(KD + kk * HEAD_DIM, KD + (kk + 1) * HEAD_DIM)
            part[0, :, sl] = dk[kk][:BLOCK]
            part[0, :, sv] = dv[kk][:BLOCK]
            part[1, :, sl] = dk[kk][BLOCK:]
            part[1, :, sv] = dv[kk][BLOCK:]
        done = carry[...] + part[0]
        o_ref[:, PD + QD:PD + QD + KD] = _unrope(done[:, :KD], cp, sp)
        o_ref[:, PD + QD + KD:] = done[:, KD:]
        carry[...] = part[1]
        ds_ref[...] += dsk

    return _call(body, name="attn_bwd", out_shape=(_sds((S, PD + QD + 2 * KD)), _sds((8, LANES))), grid=(nb + 1,),
                 in_specs=_attn_in_specs(nb) + [pl.BlockSpec((BLOCK, QD), lambda n: (jnp.minimum(n, nb - 1), PD // QD))],
                 out_specs=(pl.BlockSpec((BLOCK, PD + QD + 2 * KD), lambda n: (jnp.maximum(n - 1, 0), 0)),
                            pl.BlockSpec((8, LANES), lambda n: (0, 0))),
                 scratch=[pltpu.VMEM((BLOCK, QD), F32), pltpu.VMEM((BLOCK, 2 * KD), F32),
                          pltpu.VMEM((2, BLOCK, 2 * KD), F32), pltpu.VMEM((QD // LANES, BLOCK, LANES), F32)])(
                     sinks, proj, proj, proj, cos, sin, cos, sin, dcat)


def _pool_sums(u, g, t, shift):
    s2 = u + shift(u, 1, t)
    s4 = s2 + shift(s2, 2, t)
    s8 = s4 + shift(s4, 4, t)
    s16 = s8 + shift(s8, 8, t)
    return jnp.where(g == 0, s2, jnp.where(g == 1, s4, jnp.where(g == 2, s8, s16)))


def _pool_specs(S):
    col = pl.BlockSpec((S, LANES), lambda g: (0, g))
    wsp = pl.BlockSpec((1, LANES, LANES), lambda g: (g, 0, 0))
    vec = pl.BlockSpec((1, LANES), lambda g: (0, g))
    return col, wsp, vec


def _pool_fwd(proj, pool_w, scale):
    S = proj.shape[0]
    col, wsp, vec = _pool_specs(S)

    def body(u_ref, w_ref, s_ref, o_ref):
        g = pl.program_id(0)
        u = u_ref[...]
        t = lax.broadcasted_iota(jnp.int32, u.shape, 0)
        cnt = jnp.minimum(t + 1, 2 << g).astype(F32)
        pm = _pool_sums(u, g, t, _shift_dn) / cnt - u
        o_ref[...] = jnp.dot(pm.astype(BF16), w_ref[0].astype(BF16), preferred_element_type=F32) * s_ref[...]

    return _call(body, name="pool_fwd", out_shape=_sds((S, PD + QD)), grid=(POOL_GROUPS,),
                 in_specs=[col, wsp, vec], out_specs=col)(proj, pool_w, scale)


def _pool_bwd(proj, pool_w, scale, dcat, dproj):
    S = proj.shape[0]
    col, wsp, vec = _pool_specs(S)

    def body(u_ref, w_ref, s_ref, d_ref, dproj_ref, du_ref, dw_ref, dsc_ref):
        g = pl.program_id(0)
        u = u_ref[...]
        t = lax.broadcasted_iota(jnp.int32, u.shape, 0)
        cnt = jnp.minimum(t + 1, 2 << g).astype(F32)
        pm = (_pool_sums(u, g, t, _shift_dn) / cnt - u).astype(BF16)
        wv = w_ref[0].astype(BF16)
        d = d_ref[...]
        pw = jnp.dot(pm, wv, preferred_element_type=F32)
        dsc_ref[...] = jnp.sum(pw * d, axis=0, keepdims=True)
        dpw = (d * s_ref[...]).astype(BF16)
        dw_ref[0] = lax.dot_general(pm, dpw, (((0,), (0,)), ((), ())), preferred_element_type=F32)
        dpm = lax.dot_general(dpw, wv, (((1,), (1,)), ((), ())), preferred_element_type=F32)
        du_ref[...] = _pool_sums(dpm / cnt, g, t, _shift_up) - dpm

    return _call(body, name="pool_bwd",
                 out_shape=(_sds(dproj.shape), _sds((POOL_GROUPS, LANES, LANES)), _sds((1, POOL_GROUPS * LANES))),
                 grid=(POOL_GROUPS,), in_specs=[col, wsp, vec, col, ANY], out_specs=(col, wsp, vec),
                 aliases={4: 0})(proj, pool_w, scale, dcat, dproj)


def _conv(x, w_ref, b_ref, t):
    K = w_ref.shape[0]
    y = b_ref[...] + jnp.zeros_like(x)
    for k in range(K):
        y = y + w_ref[k:k + 1, :] * _shift_dn(x, K - 1 - k, t)
    return y


def _silu_grad(y):
    sg = _sigmoid(y)
    return sg * (1.0 + y * (1.0 - sg))


CONV_ROWS = 64
HALO = 8


def _win_above(ref, r0):
    if isinstance(r0, int):
        assert r0 == 0
        return jnp.concatenate([jnp.zeros((HALO, ref.shape[1]), F32), ref[0:CONV_ROWS, :]], axis=0)
    return ref[pl.ds(pl.multiple_of(r0 - HALO, HALO), CONV_ROWS + HALO), :]


def _rows_at(win, start):
    if start % 8 == 0:
        return win[start:start + CONV_ROWS]
    base = start // 8 * 8
    return pltpu.roll(win, win.shape[0] - (start - base), axis=0)[base:base + CONV_ROWS]


def _taps_above(win, K):
    return [_rows_at(win, HALO - (K - 1 - k)) for k in range(K)]


def _conv_taps(taps, w, b):
    y = b
    for k in range(len(w)):
        y = y + w[k] * taps[k]
    return y


def _conv_t_win(win, w):
    K = len(w)
    out = None
    for k in range(K):
        d = K - 1 - k
        term = w[k] * _rows_at(win, d)
        out = term if out is None else out + term
    return out


def _fold8(x):
    return jnp.sum(x.reshape(CONV_ROWS // 8, 8, x.shape[-1]), axis=0)


def _chunk_loop(S, step, init):
    carry = step(0, init)
    return lax.fori_loop(1, S // CONV_ROWS, lambda i, c: step(pl.multiple_of(i * CONV_ROWS, CONV_ROWS), c), carry)


def _ffn_mid_specs(S, K, layer, nf):
    return [pl.BlockSpec((S, LANES), lambda j: (0, j)), pl.BlockSpec((S, LANES), lambda j: (0, nf + j)),
            pl.BlockSpec((None, K, LANES), lambda j: (layer, 0, j)), pl.BlockSpec((None, K, LANES), lambda j: (layer, 0, nf + j)),
            pl.BlockSpec((None, 1, LANES), lambda j: (layer, 0, j)), pl.BlockSpec((None, 1, LANES), lambda j: (layer, 0, nf + j))]


def _ffn_mid_fwd(a, cw, cb, layer):
    S, F = a.shape[0], a.shape[1] // 2
    nf = F // LANES
    K = cw.shape[1]

    def body(au_ref, ag_ref, wu_ref, wg_ref, bu_ref, bg_ref, o_ref):
        t = lax.broadcasted_iota(jnp.int32, (S, LANES), 0)
        hu = _conv(au_ref[...], wu_ref, bu_ref, t)
        hg = _conv(ag_ref[...], wg_ref, bg_ref, t)
        o_ref[...] = (hg * _sigmoid(hg) * hu).astype(BF16)

    return _call(body, name="ffn_mid_fwd", out_shape=_sds((S, F), BF16), grid=(nf,),
                 in_specs=_ffn_mid_specs(S, K, layer, nf), out_specs=pl.BlockSpec((S, LANES), lambda j: (0, j)))(
                     a, a, cw, cw, cb[:, None], cb[:, None])


def _ffn_mid_bwd(a, cw, cb, layer, dact):
    S, F = a.shape[0], a.shape[1] // 2
    nf = F // LANES
    K = cw.shape[1]

    def body(au_ref, ag_ref, wu_ref, wg_ref, bu_ref, bg_ref, d_ref, dau_ref, dag_ref, dwu_ref, dwg_ref, dbu_ref, dbg_ref,
             dhu_s, dhg_s):
        T = CONV_ROWS
        wu = [wu_ref[k:k + 1, :] for k in range(K)]
        wg = [wg_ref[k:k + 1, :] for k in range(K)]
        bu, bg = bu_ref[...], bg_ref[...]
        zero8 = jnp.zeros((HALO, LANES), F32)
        dhu_s[S:S + HALO, :] = zero8
        dhg_s[S:S + HALO, :] = zero8

        def first_pass(r0, acc):
            tu, tg = _taps_above(_win_above(au_ref, r0), K), _taps_above(_win_above(ag_ref, r0), K)
            hu, hg = _conv_taps(tu, wu, bu), _conv_taps(tg, wg, bg)
            d = d_ref[pl.ds(r0, T), :].astype(F32)
            sg = _sigmoid(hg)
            dhu = d * hg * sg
            dhg = d * hu * (sg * (1.0 + hg * (1.0 - sg)))
            dhu_s[pl.ds(r0, T), :] = dhu
            dhg_s[pl.ds(r0, T), :] = dhg
            new = []
            for dh, taps in ((dhu, tu), (dhg, tg)):
                for k in range(K):
                    new.append(acc[len(new)] + _fold8(dh * taps[k]))
            new.append(acc[2 * K] + _fold8(dhu))
            new.append(acc[2 * K + 1] + _fold8(dhg))
            return tuple(new)

        acc = _chunk_loop(S, first_pass, tuple(zero8 for _ in range(2 * K + 2)))
        for k in range(K):
            dwu_ref[k:k + 1, :] = jnp.sum(acc[k], axis=0, keepdims=True)
            dwg_ref[k:k + 1, :] = jnp.sum(acc[K + k], axis=0, keepdims=True)
        dbu_ref[...] = jnp.sum(acc[2 * K], axis=0, keepdims=True)
        dbg_ref[...] = jnp.sum(acc[2 * K + 1], axis=0, keepdims=True)

        def second_pass(i, carry):
            r0 = pl.multiple_of(i * T, T)
            dau_ref[pl.ds(r0, T), :] = _conv_t_win(dhu_s[pl.ds(r0, T + HALO), :], wu).astype(BF16)
            dag_ref[pl.ds(r0, T), :] = _conv_t_win(dhg_s[pl.ds(r0, T + HALO), :], wg).astype(BF16)
            return carry

        lax.fori_loop(0, S // T, second_pass, 0)

    col = pl.BlockSpec((S, LANES), lambda j: (0, j))
    wsp = pl.BlockSpec((K, LANES), lambda j: (0, j))
    bsp = pl.BlockSpec((1, LANES), lambda j: (0, j))
    dau, dag, dwu, dwg, dbu, dbg = _call(
        body, name="ffn_mid_bwd",
        out_shape=(_sds((S, F), BF16), _sds((S, F), BF16), _sds((K, F)), _sds((K, F)), _sds((1, F)), _sds((1, F))), grid=(nf,),
        in_specs=_ffn_mid_specs(S, K, layer, nf) + [col], out_specs=(col, col, wsp, wsp, bsp, bsp),
        scratch=[pltpu.VMEM((S + HALO, LANES), F32), pltpu.VMEM((S + HALO, LANES), F32)])(
            a, a, cw, cw, cb[:, None], cb[:, None], dact)
    return dau, dag, jnp.concatenate([dwu, dwg], axis=1), jnp.concatenate([dbu, dbg], axis=1)


def _conv_silu_fwd(x, cw, cb):
    S = x.shape[0]
    K, C = cw.shape

    def body(x_ref, w_ref, b_ref, o_ref):
        t = lax.broadcasted_iota(jnp.int32, (S, LANES), 0)
        y = _conv(x_ref[...], w_ref, b_ref, t)
        o_ref[...] = y * _sigmoid(y)

    col = pl.BlockSpec((S, LANES), lambda j: (0, j))
    return _call(body, name="conv_silu_fwd", out_shape=_sds((S, C)), grid=(C // LANES,),
                 in_specs=[col, pl.BlockSpec((K, LANES), lambda j: (0, j)), pl.BlockSpec((1, LANES), lambda j: (0, j))],
                 out_specs=col)(x, cw, cb)


def _conv_silu_bwd(x, cw, cb, douts):
    S = x.shape[0]
    K, C = cw.shape
    starts, off = [], 0
    for d in douts:
        starts.append(off)
        off += d.shape[1] // LANES
    assert off == C // LANES

    def body(x_ref, w_ref, b_ref, *rest):
        dy_s = rest[-1]
        d_refs, (dx_ref, dw_ref, db_ref) = rest[:len(douts)], rest[len(douts):-1]
        j = pl.program_id(0)
        T = CONV_ROWS
        w = [w_ref[k:k + 1, :] for k in range(K)]
        b = b_ref[...]
        zero8 = jnp.zeros((HALO, LANES), F32)
        dy_s[S:S + HALO, :] = zero8

        def first_pass(r0, acc):
            taps = _taps_above(_win_above(x_ref, r0), K)
            y = _conv_taps(taps, w, b)
            d = d_refs[0][pl.ds(r0, T), :]
            for i in range(1, len(douts)):
                d = jnp.where(j >= starts[i], d_refs[i][pl.ds(r0, T), :], d)
            dy = d * _silu_grad(y)
            dy_s[pl.ds(r0, T), :] = dy
            return tuple(acc[k] + _fold8(dy * taps[k]) for k in range(K)) + (acc[K] + _fold8(dy),)

        acc = _chunk_loop(S, first_pass, tuple(zero8 for _ in range(K + 1)))
        for k in range(K):
            dw_ref[k:k + 1, :] = jnp.sum(acc[k], axis=0, keepdims=True)
        db_ref[...] = jnp.sum(acc[K], axis=0, keepdims=True)

        def second_pass(i, carry):
            r0 = pl.multiple_of(i * T, T)
            dx_ref[pl.ds(r0, T), :] = _conv_t_win(dy_s[pl.ds(r0, T + HALO), :], w).astype(BF16)
            return carry

        lax.fori_loop(0, S // T, second_pass, 0)

    col = pl.BlockSpec((S, LANES), lambda j: (0, j))
    wsp = pl.BlockSpec((K, LANES), lambda j: (0, j))
    bsp = pl.BlockSpec((1, LANES), lambda j: (0, j))

    def dspec(i):
        nblk = douts[i].shape[1] // LANES
        return pl.BlockSpec((S, LANES), lambda j: (0, jnp.clip(j - starts[i], 0, nblk - 1)))

    return _call(body, name="conv_silu_bwd", out_shape=(_sds((S, C), BF16), _sds((K, C)), _sds((1, C))), grid=(C // LANES,),
                 in_specs=[col, wsp, bsp] + [dspec(i) for i in range(len(douts))],
                 out_specs=(col, wsp, bsp), scratch=[pltpu.VMEM((S + HALO, LANES), F32)])(x, cw, cb, *douts)


HI = lax.Precision.HIGHEST


def _ssd_prep_fwd(proj, col0, bias_g, alog_g):
    S = proj.shape[0]
    nc = S // SSM_L
    b0 = col0 // LANES

    def body(raw_ref, b_ref, al_ref, pre_ref, dt_ref, acs_ref, acst_ref):
        r_i = lax.broadcasted_iota(jnp.int32, (LANES, LANES), 0)
        c_i = lax.broadcasted_iota(jnp.int32, (LANES, LANES), 1)
        live = c_i < SSM_R
        tril = jnp.where(r_i >= c_i, 1.0, 0.0)
        raw = raw_ref[...]
        for g in range(SSM_G):
            mine = raw if g == 0 else pltpu.roll(raw, LANES - SSM_R * g, axis=1)
            pre = jnp.where(live, mine, 0.0) + b_ref[g]
            dt = jnp.where(live, jnp.logaddexp(pre, 0.0), 0.0)
            a = dt * (-jnp.exp(al_ref[g]))
            acs = jnp.dot(tril, a, preferred_element_type=F32, precision=HI)
            pre_ref[g] = pre
            dt_ref[g] = dt
            acs_ref[g] = acs
            acst_ref[g] = acs.T

    gsp = pl.BlockSpec((SSM_G, 1, LANES), lambda c: (0, 0, 0))
    blk = pl.BlockSpec((SSM_G, SSM_L, LANES), lambda c: (0, c, 0))
    big = _sds((SSM_G, S, LANES))
    return _call(body, name="ssd_prep_fwd", out_shape=(big, big, big, _sds((SSM_G, LANES, S))), grid=(nc,),
                 in_specs=[pl.BlockSpec((SSM_L, LANES), lambda c: (c, b0)), gsp, gsp],
                 out_specs=(blk, blk, blk, pl.BlockSpec((SSM_G, LANES, SSM_L), lambda c: (0, 0, c))))(proj, bias_g, alog_g)


def _ssd_prep_bwd(pre_g, dt_g, alog_g, ddt_g, dacs_g, dacst_g):
    S = pre_g.shape[1]
    nc = S // SSM_L

    def body(pre_ref, dt_ref, al_ref, ddt_ref, dacs_ref, dacst_ref, draw_ref, db_ref, dal_ref):
        c = pl.program_id(0)
        r_i = lax.broadcasted_iota(jnp.int32, (LANES, LANES), 0)
        c_i = lax.broadcasted_iota(jnp.int32, (LANES, LANES), 1)
        live = c_i < SSM_R
        triu = jnp.where(r_i <= c_i, 1.0, 0.0)

        @pl.when(c == 0)
        def _():
            db_ref[...] = jnp.zeros_like(db_ref)
            dal_ref[...] = jnp.zeros_like(dal_ref)

        draw = jnp.zeros((SSM_L, LANES), F32)
        for g in range(SSM_G):
            dacs = dacs_ref[g] + dacst_ref[g].T
            da = jnp.dot(triu, dacs, preferred_element_type=F32, precision=HI)
            A = -jnp.exp(al_ref[g])
            ddt = ddt_ref[g] + da * A
            dpre = jnp.where(live, ddt * _sigmoid(pre_ref[g]), 0.0)
            draw = draw + (dpre if g == 0 else pltpu.roll(dpre, SSM_R * g, axis=1))
            db_ref[g] += jnp.sum(dpre, axis=0, keepdims=True)
            dal_ref[g] += jnp.where(live[:1], jnp.sum(da * dt_ref[g], axis=0, keepdims=True) * A, 0.0)
        draw_ref[...] = draw

    gsp = pl.BlockSpec((SSM_G, 1, LANES), lambda c: (0, 0, 0))
    blk = pl.BlockSpec((SSM_G, SSM_L, LANES), lambda c: (0, c, 0))
    gout = _sds((SSM_G, 1, LANES))
    return _call(body, name="ssd_prep_bwd", out_shape=(_sds((S, LANES)), gout, gout), grid=(nc,),
                 in_specs=[blk, blk, gsp, blk, blk, pl.BlockSpec((SSM_G, LANES, SSM_L), lambda c: (0, 0, c))],
                 out_specs=(pl.BlockSpec((SSM_L, LANES), lambda c: (c, 0)), gsp, gsp))(pre_g, dt_g, alog_g, ddt_g, dacs_g, dacst_g)


NT = (((1,), (1,)), ((), ()))
TN = (((0,), (0,)), ((), ()))
SSM_HP = SSM_R * SSM_P


def _ssd_group_terms(xs_ref, dt_ref, acs_ref, d_ref):
    hid = lax.broadcasted_iota(jnp.int32, (1, SSM_HP), 1) // SSM_P
    rid = lax.broadcasted_iota(jnp.int32, (SSM_HP, 1), 0) // SSM_P

    def widen(cols):
        out = cols[0]
        for r in range(1, SSM_R):
            out = jnp.where(hid == r, cols[r], out)
        return out

    dt_c = [dt_ref[:, r:r + 1] for r in range(SSM_R)]
    acs_c = [acs_ref[:, r:r + 1] for r in range(SSM_R)]
    last = [acs_ref[SSM_L - 1:SSM_L, r:r + 1] for r in range(SSM_R)]
    decay_c = [jnp.exp(last[r] - acs_c[r]) for r in range(SSM_R)]
    cd = [jnp.exp(last[r]) for r in range(SSM_R)]
    cd_rows = cd[0]
    for r in range(1, SSM_R):
        cd_rows = jnp.where(rid == r, cd[r], cd_rows)
    xs = xs_ref[...]
    return (xs, xs * widen(dt_c), widen([jnp.exp(a) for a in acs_c]), widen(decay_c),
            widen([d_ref[:, r:r + 1] for r in range(SSM_R)]), cd_rows, dt_c, decay_c, cd)


def _ssd_lmat(acs_ref, acst_ref, r, tril):
    return jnp.exp(jnp.where(tril, acs_ref[:, r:r + 1] - acst_ref[r:r + 1, :], -jnp.inf))


SSM_GPS = 2


def _ssd_specs(rev, nc):
    def cc(c):
        return nc - 1 - c if rev else c
    xs_blocks = (SSM_G * SSM_HP) // (SSM_GPS * SSM_N)
    xs = pl.BlockSpec((SSM_L, SSM_GPS * SSM_HP), lambda g, c: (cc(c), g))
    bsp = pl.BlockSpec((SSM_L, SSM_GPS * SSM_N), lambda g, c: (cc(c), xs_blocks + g))
    csp = pl.BlockSpec((SSM_L, SSM_GPS * SSM_N), lambda g, c: (cc(c), xs_blocks + SSM_G // SSM_GPS + g))
    sc = pl.BlockSpec((SSM_GPS, SSM_L, LANES), lambda g, c: (g, cc(c), 0))
    sct = pl.BlockSpec((SSM_GPS, LANES, SSM_L), lambda g, c: (g, 0, cc(c)))
    gsp = pl.BlockSpec((SSM_GPS, 1, LANES), lambda g, c: (g, 0, 0))
    st = pl.BlockSpec((None, SSM_GPS, SSM_HP, SSM_N), lambda g, c: (cc(c), g, 0, 0))
    return xs, bsp, csp, sc, sct, gsp, st


def _interleave(gens):
    live = list(gens)
    while live:
        for g in list(live):
            try:
                next(g)
            except StopIteration:
                live.remove(g)


def _rounds(gens):
    live = list(gens)
    while live:
        for g in list(live):
            try:
                next(g)
            except StopIteration:
                live.remove(g)
        yield


def _ssd_group_views(gg, xs_ref, b_ref, c_ref, *per_group):
    return (xs_ref.at[:, gg * SSM_HP:(gg + 1) * SSM_HP], b_ref.at[:, gg * SSM_N:(gg + 1) * SSM_N],
            c_ref.at[:, gg * SSM_N:(gg + 1) * SSM_N]) + tuple(r.at[gg] for r in per_group)


def _ssd_fwd(xbc, dt_g, acs_g, acst_g, d_g):
    S = xbc.shape[0]
    nc = S // SSM_L
    xs_s, b_s, c_s, sc, sct, gsp, st = _ssd_specs(False, nc)

    def body(xs_ref, b_ref, c_ref, dt_ref, acs_ref, acst_ref, d_ref, y_ref, st_ref, state):
        c = pl.program_id(1)

        @pl.when(c == 0)
        def _():
            state[...] = jnp.zeros_like(state)

        tril = lax.broadcasted_iota(jnp.int32, (SSM_L, SSM_L), 0) >= lax.broadcasted_iota(jnp.int32, (SSM_L, SSM_L), 1)
        def group(gg):
            xs_v, b_v, c_v, dt_v, acs_v, acst_v, d_v, st_v, state_v = _ssd_group_views(
                gg, xs_ref, b_ref, c_ref, dt_ref, acs_ref, acst_ref, d_ref, st_ref, state)
            y_v = y_ref.at[:, gg * SSM_HP:(gg + 1) * SSM_HP]
            Bb, Cb = b_v[...].astype(BF16), c_v[...].astype(BF16)
            Gm = lax.dot_general(Cb, Bb, NT, preferred_element_type=F32)
            yield
            xs, X, e_all, decay_all, d_all, cd_rows, _, _, _ = _ssd_group_terms(xs_v, dt_v, acs_v, d_v)
            S_all = state_v[...]
            st_v[...] = S_all
            yield
            yo = lax.dot_general(Cb, S_all.astype(BF16), NT, preferred_element_type=F32)
            new_state = lax.dot_general((X * decay_all).astype(BF16), Bb, TN, preferred_element_type=F32)
            yield
            y_v[...] = e_all * yo + d_all * xs
            state_v[...] = S_all * cd_rows + new_state

            def head(r):
                sl = slice(r * SSM_P, (r + 1) * SSM_P)
                M = Gm * _ssd_lmat(acs_v, acst_v, r, tril)
                yield
                y_v[:, sl] += jnp.dot(M.astype(BF16), X[:, sl].astype(BF16), preferred_element_type=F32)

            yield from _rounds([head(r) for r in range(SSM_R)])

        _interleave([group(gg) for gg in range(SSM_GPS)])

    return _call(body, name="ssd_fwd",
                 out_shape=(_sds((S, SSM_G * SSM_HP)), _sds((nc, SSM_G, SSM_HP, SSM_N))),
                 grid=(SSM_G // SSM_GPS, nc), in_specs=[xs_s, b_s, c_s, sc, sc, sct, gsp],
                 out_specs=(xs_s, pl.BlockSpec((None, SSM_GPS, SSM_HP, SSM_N), lambda g, c: (c, g, 0, 0))),
                 scratch=[pltpu.VMEM((SSM_GPS, SSM_HP, SSM_N), F32)])(xbc, xbc, xbc, dt_g, acs_g, acst_g, d_g)


def _ssd_bwd(xbc, dt_g, acs_g, acst_g, d_g, states, dy):
    S = xbc.shape[0]
    nc = S // SSM_L
    xs_s, b_s, c_s, sc, sct, gsp, st = _ssd_specs(True, nc)
    bc_out = pl.BlockSpec((SSM_L, SSM_GPS * SSM_N), lambda g, c: (nc - 1 - c, g))

    def body(xs_ref, b_ref, c_ref, dt_ref, acs_ref, acst_ref, d_ref, st_ref, dy_ref,
             dxs_ref, db_ref, dc_ref, ddt_ref, dacs_ref, dacst_ref, dd_ref, dstate):
        c = pl.program_id(1)

        @pl.when(c == 0)
        def _():
            dstate[...] = jnp.zeros_like(dstate)
            dd_ref[...] = jnp.zeros_like(dd_ref)

        tril = lax.broadcasted_iota(jnp.int32, (SSM_L, SSM_L), 0) >= lax.broadcasted_iota(jnp.int32, (SSM_L, SSM_L), 1)
        lane = lax.broadcasted_iota(jnp.int32, (1, LANES), 1)
        subl = lax.broadcasted_iota(jnp.int32, (LANES, 1), 0)
        last_row = lax.broadcasted_iota(jnp.int32, (SSM_L, 1), 0) == SSM_L - 1
        triu = lax.broadcasted_iota(jnp.int32, (SSM_L, SSM_L), 0) <= lax.broadcasted_iota(jnp.int32, (SSM_L, SSM_L), 1)
        def group(gg):
            xs_v, b_v, c_v, dt_v, acs_v, acst_v, d_v, st_v, ddt_v, dacs_v, dacst_v, dd_v, dstate_v = _ssd_group_views(
                gg, xs_ref, b_ref, c_ref, dt_ref, acs_ref, acst_ref, d_ref, st_ref, ddt_ref, dacs_ref, dacst_ref, dd_ref, dstate)
            dy_v, dxs_v = (r.at[:, gg * SSM_HP:(gg + 1) * SSM_HP] for r in (dy_ref, dxs_ref))
            db_v, dc_v = (r.at[:, gg * SSM_N:(gg + 1) * SSM_N] for r in (db_ref, dc_ref))
            Bb, Cb = b_v[...].astype(BF16), c_v[...].astype(BF16)
            Gm = lax.dot_general(Cb, Bb, NT, preferred_element_type=F32)
            GmT = lax.dot_general(Bb, Cb, NT, preferred_element_type=F32)
            yield
            xs, X, e_all, decay_all, d_all, cd_rows, dt_c, decay_c, cd = _ssd_group_terms(xs_v, dt_v, acs_v, d_v)
            S_all, dSn_all, dY = st_v[...], dstate_v[...], dy_v[...]
            Sb, dSnb = S_all.astype(BF16), dSn_all.astype(BF16)
            yield
            T = lax.dot_general(Cb, Sb, NT, preferred_element_type=F32)
            dT = (dY * e_all).astype(BF16)
            dC = jnp.dot(dT, Sb, preferred_element_type=F32)
            dS_prev = lax.dot_general(dT, Cb, TN, preferred_element_type=F32)
            yield
            yo_dy = dY * (e_all * T)
            W = lax.dot_general(Bb, dSnb, NT, preferred_element_type=F32)
            dB = jnp.dot((X * decay_all).astype(BF16), dSnb, preferred_element_type=F32)
            yield
            xw = X * W
            dcd_rows = jnp.sum(dSn_all * S_all, axis=1, keepdims=True)
            dstate_v[...] = dS_prev + dSn_all * cd_rows
            dX_state = W * decay_all
            yield
            acc = dict(dG=jnp.zeros((SSM_L, SSM_L), F32), dGT=jnp.zeros((SSM_L, SSM_L), F32),
                       ddt=jnp.zeros((SSM_L, LANES), F32), dacs=jnp.zeros((SSM_L, LANES), F32),
                       dacst=jnp.zeros((LANES, SSM_L), F32), dd=jnp.zeros((1, LANES), F32))

            def head(r):
                sl = slice(r * SSM_P, (r + 1) * SSM_P)
                Lm = _ssd_lmat(acs_v, acst_v, r, tril)
                LmT = jnp.exp(jnp.where(triu, acst_v[r:r + 1, :] - acs_v[:, r:r + 1], -jnp.inf))
                M = Gm * Lm
                yield
                dYh, xs_h = dY[:, sl], xs[:, sl]
                dYb, Xb = dYh.astype(BF16), X[:, sl].astype(BF16)
                dM = lax.dot_general(dYb, Xb, NT, preferred_element_type=F32)
                yield
                dX = jnp.dot((GmT * LmT).astype(BF16), dYb, preferred_element_type=F32) + dX_state[:, sl]
                acc["dG"] = acc["dG"] + dM * Lm
                acc["dGT"] = acc["dGT"] + lax.dot_general(Xb, dYb, NT, preferred_element_type=F32) * LmT
                yield
                dseg = dM * M
                dd = jnp.sum(xw[:, sl], axis=1, keepdims=True) * decay_c[r]
                dcd = jnp.sum(dcd_rows[sl])
                dacs_col = (jnp.sum(dseg, axis=1, keepdims=True) + jnp.sum(yo_dy[:, sl], axis=1, keepdims=True) - dd
                            + jnp.where(last_row, dcd * cd[r] + jnp.sum(dd), 0.0))
                dacs_row = -jnp.sum(dseg, axis=0, keepdims=True)
                yield
                dxs_v[:, sl] = dX * dt_c[r] + d_all[:, sl] * dYh
                acc["ddt"] = acc["ddt"] + jnp.where(lane == r, jnp.sum(dX * xs_h, axis=1, keepdims=True), 0.0)
                acc["dacs"] = acc["dacs"] + jnp.where(lane == r, dacs_col, 0.0)
                acc["dacst"] = acc["dacst"] + jnp.where(subl == r, dacs_row, 0.0)
                acc["dd"] = acc["dd"] + jnp.where(lane == r, jnp.sum(dYh * xs_h), 0.0)

            yield from _rounds([head(r) for r in range(SSM_R)])
            dc_v[...] = dC + jnp.dot(acc["dG"].astype(BF16), Bb, preferred_element_type=F32)
            db_v[...] = dB + jnp.dot(acc["dGT"].astype(BF16), Cb, preferred_element_type=F32)
            ddt_v[...] = acc["ddt"]
            dacs_v[...] = acc["dacs"]
            dacst_v[...] = acc["dacst"]
            dd_v[...] += acc["dd"]

        _interleave([group(gg) for gg in range(SSM_GPS)])

    big = _sds((SSM_G, S, LANES))
    return _call(body, name="ssd_bwd",
                 out_shape=(_sds((S, SSM_G * SSM_HP)), _sds((S, SSM_G * SSM_N)), _sds((S, SSM_G * SSM_N)),
                            big, big, _sds((SSM_G, LANES, S)), _sds((SSM_G, 1, LANES))),
                 grid=(SSM_G // SSM_GPS, nc), in_specs=[xs_s, b_s, c_s, sc, sc, sct, gsp, st, xs_s],
                 out_specs=(xs_s, bc_out, bc_out, sc, sc, sct, gsp),
                 scratch=[pltpu.VMEM((SSM_GPS, SSM_HP, SSM_N), F32)])(xbc, xbc, xbc, dt_g, acs_g, acst_g, d_g, states, dy)


def _gate_norm_fwd(y, proj, w):
    S, DI = y.shape
    tm = _tile(S, 256)

    def body(y_ref, z_ref, w_ref, o_ref):
        z = z_ref[...]
        gn = y_ref[...] * (z * _sigmoid(z))
        r = lax.rsqrt(jnp.mean(gn * gn, axis=-1, keepdims=True) + SSM_NORM_EPS)
        o_ref[...] = (gn * r * w_ref[...]).astype(BF16)

    row = pl.BlockSpec((tm, DI), lambda i: (i, 0))
    return _call(body, name="gate_norm_fwd", out_shape=_sds((S, DI), BF16), grid=(S // tm,),
                 in_specs=[row, row, pl.BlockSpec((1, DI), lambda i: (0, 0))], out_specs=row)(y, proj, w)


def _gate_norm_bwd(y, proj, w, dout):
    S, DI = y.shape
    tm = _tile(S, 256)

    def body(y_ref, z_ref, w_ref, d_ref, dy_ref, dz_ref, dw_ref):
        z, yv = z_ref[...], y_ref[...]
        sz = z * _sigmoid(z)
        dgn, dw = _norm_bwd_math(yv * sz, w_ref[...], d_ref[...].astype(F32), SSM_NORM_EPS)
        dy_ref[...] = dgn * sz
        dz_ref[...] = (dgn * yv * _silu_grad(z)).astype(BF16)

        @pl.when(pl.program_id(0) == 0)
        def _():
            dw_ref[...] = jnp.zeros_like(dw_ref)

        dw_ref[...] += dw

    row = pl.BlockSpec((tm, DI), lambda i: (i, 0))
    vec = pl.BlockSpec((1, DI), lambda i: (0, 0))
    return _call(body, name="gate_norm_bwd", out_shape=(_sds((S, DI)), _sds((S, DI), BF16), _sds((1, DI))), grid=(S // tm,),
                 in_specs=[row, row, vec, row], out_specs=(row, row, vec))(y, proj, w, dout)


def _group_major(v):
    return jnp.pad(v.reshape(SSM_G, 1, SSM_R), ((0, 0), (0, 0), (0, LANES - SSM_R)))


def _ungroup(t):
    return t[:, :SSM_R].reshape(1, SSM_G * SSM_R)


def _ffn_fwd(x, P, l, need):
    need(f"ffn{l}_up", x)
    h = _rmsnorm(x, P["norm_ffn"][l:l + 1], name=f"ffn{l}_norm")
    wT = P[f"ffn_w_upT{l}"]
    F = wT.shape[0] // 2
    a = _mm(h, wT, tb=True, name=f"ffn{l}_up")
    need(f"ffn{l}_down", a)
    act = _ffn_mid_fwd(a, P["ffn_conv_w"], P["ffn_conv_b"], l)
    out = _mm(act, P[f"ffn_w_down{l}"], res=x, name=f"ffn{l}_down")
    return out, (x, h, a, act)


def _ffn_bwd(saved, P, l, dx, emit):
    x, h, a, act = saved
    wT = P[f"ffn_w_upT{l}"]
    F = wT.shape[0] // 2
    dact = _mm(dx, P[f"ffn_w_down{l}"], tb=True, out_dtype=BF16, name=f"ffn{l}_down_dx")
    dw_down = _mm(act, dx, ta=True, out_dtype=PAYLOAD, name=f"ffn{l}_down_dw")
    dau, dag, dcw, dcb = _ffn_mid_bwd(a, P["ffn_conv_w"], P["ffn_conv_b"], l, dact)
    dw_upT = _mm(dau, h, ta=True, out_dtype=PAYLOAD, out_rows=(2 * F, 0, None), name=f"ffn{l}_up_u_dw")
    dw_upT = _mm(dag, h, ta=True, out_dtype=PAYLOAD, out_rows=(2 * F, F, dw_upT), name=f"ffn{l}_up_g_dw")
    tie = emit(f"ffn{l}", {"ffn_w_upT": dw_upT, "ffn_w_down": dw_down})
    dh = _mm(dau, wT, b_rows=(0, F), name=f"ffn{l}_up_u_dx")
    dx_in, dnw = _mm_norm_bwd(dag, wT, x, P["norm_ffn"][l:l + 1], dx, res=dh, b_rows=(F, F), after=tie,
                              name=f"ffn{l}_up_g_dx_norm_bwd")
    return dx_in, dnw, dcw, dcb


def _local_step(x, positions, target, P, need, emit, after=None):
    S, D = x.shape
    inv_freq = ROPE_THETA ** (-jnp.arange(0, HEAD_DIM, 2, dtype=F32) / HEAD_DIM)
    inv_freq = jnp.tile(inv_freq, LANES // (HEAD_DIM // 2)).reshape(1, LANES)
    cos, sin = _rope_tables(positions, inv_freq)

    nm0 = P["norm_mix"][0:1]
    h0 = _rmsnorm(x, nm0, name="mix_norm", after=after)
    need("mix_in", h0)
    proj0 = _mm(h0, P["mix_w_inT"], tb=True, name="mix_in")
    cat0 = _attn_fwd(proj0, cos, sin, P["attn_sinks"], _pool_fwd(proj0, P["pool_w"][0], P["pool_scale"]))
    need("mix_out", cat0)
    x1 = _mm(cat0, P["mix_w_out"], res=x, name="mix_out")
    x2, ffn0 = _ffn_fwd(x1, P, 0, need)

    nm1 = P["norm_mix"][1:2]
    need("ssm", x2)
    h1 = _rmsnorm(x2, nm1, name="ssm_norm_in")
    w1T, wdtT = P["ssm_w_inT"], P["ssm_wdtT"]
    DI, CD, NH = P["ssm_norm"].shape[1], P["ssm_conv_w"].shape[1], P["ssm_dt_bias"].shape[1]
    z = _mm(h1, w1T, tb=True, b_rows=(0, DI), name="ssm_in_z")
    xbcp = _mm(h1, w1T, tb=True, b_rows=(DI, CD), name="ssm_in_xbc")
    dtraw = _mm(h1, wdtT, tb=True, name="ssm_in_dt")
    xbc = _conv_silu_fwd(xbcp, P["ssm_conv_w"], P["ssm_conv_b"])
    bias_g, alog_g, d_g = _group_major(P["ssm_dt_bias"]), _group_major(P["ssm_A_log"]), _group_major(P["ssm_D"])
    pre_g, dt_g, acs_g, acst_g = _ssd_prep_fwd(dtraw, 0, bias_g, alog_g)
    y, states = _ssd_fwd(xbc, dt_g, acs_g, acst_g, d_g)
    yn = _gate_norm_fwd(y, z, P["ssm_norm"])
    need("ssm_out", yn)
    x3 = _mm(yn, P["ssm_w_out"], res=x2, name="ssm_out")
    x4, ffn1 = _ffn_fwd(x3, P, 1, need)

    loss, dx, d_norm_final = _final_loss(x4, P["norm_final"].reshape(1, D), target, name="final_loss")
    dx, dnf1, dcw1, dcb1 = _ffn_bwd(ffn1, P, 1, dx, emit)
    dyn = _mm(dx, P["ssm_w_out"], tb=True, out_dtype=BF16, name="ssm_out_dx")
    d_w_out1 = _mm(yn, dx, ta=True, out_dtype=PAYLOAD, name="ssm_out_dw")
    dy, dz, d_ssm_norm = _gate_norm_bwd(y, z, P["ssm_norm"], dyn)
    dxs, dB, dC, ddt_g, dacs_g, dacst_g, dd_g = _ssd_bwd(xbc, dt_g, acs_g, acst_g, d_g, states, dy)
    draw, dbias_g, dalog_g = _ssd_prep_bwd(pre_g, dt_g, alog_g, ddt_g, dacs_g, dacst_g)
    dxbc, d_conv_w1, d_conv_b1 = _conv_silu_bwd(xbcp, P["ssm_conv_w"], P["ssm_conv_b"], [dxs, dB, dC])
    rows = DI + CD + NH
    d_w1T = _mm(dz, h1, ta=True, out_dtype=PAYLOAD, out_rows=(rows, 0, None), name="ssm_in_z_dw")
    d_w1T = _mm(dxbc, h1, ta=True, out_dtype=PAYLOAD, out_rows=(rows, DI, d_w1T), name="ssm_in_xbc_dw")
    d_w1T = _mm(draw[:, :NH], h1, ta=True, out_dtype=PAYLOAD, out_rows=(rows, DI + CD, d_w1T), name="ssm_in_dt_dw")
    tie = emit("ssm", {"ssm_w_inT": d_w1T, "ssm_w_out": d_w_out1,
                       "ssm_conv_w": d_conv_w1, "ssm_conv_b": d_conv_b1, "ssm_norm": d_ssm_norm})
    dh1 = _mm(dz, w1T, b_rows=(0, DI), name="ssm_in_z_dx")
    dh1 = _mm(dxbc, w1T, b_rows=(DI, CD), res=dh1, name="ssm_in_xbc_dx")
    dx, dnm1 = _mm_norm_bwd(draw, wdtT, x2, nm1, dx, res=dh1, after=tie, name="ssm_in_dt_dx_norm_bwd")
    dx, dnf0, dcw0, dcb0 = _ffn_bwd(ffn0, P, 0, dx, emit)
    dcat = _mm(dx, P["mix_w_out"], tb=True, name="mix_out_dx")
    d_w_out0 = _mm(cat0, dx, ta=True, out_dtype=PAYLOAD, name="mix_out_dw")
    dproj0, dsk = _attn_bwd(proj0, cos, sin, P["attn_sinks"], dcat)
    dproj0, d_pool_w, d_pool_scale = _pool_bwd(proj0, P["pool_w"][0], P["pool_scale"], dcat, dproj0)
    d_w_in0 = _mm(dproj0, h0, ta=True, out_dtype=PAYLOAD, name="mix_in_dw")
    tie = emit("mix", {"mix_w_inT": d_w_in0, "mix_w_out": d_w_out0, "ffn_conv_w": jnp.stack([dcw0, dcw1])})
    grad_x, dnm0 = _mm_norm_bwd(dproj0, P["mix_w_inT"], x, nm0, dx, after=tie, name="mix_in_dx_norm_bwd")

    small = {
        "norm_mix": jnp.concatenate([dnm0, dnm1], axis=0),
        "norm_ffn": jnp.concatenate([dnf0, dnf1], axis=0),
        "norm_final": d_norm_final,
        "pool_w": d_pool_w,
        "pool_scale": d_pool_scale,
        "attn_sinks_rows": dsk,
        "ssm_dt_bias_g": dbias_g, "ssm_A_log_g": dalog_g, "ssm_D_g": dd_g,
        "ffn_conv_b": jnp.concatenate([dcb0, dcb1], axis=0),
    }
    return loss, grad_x, small


def _peer(k):
    x, y, c = lax.axis_index("x"), lax.axis_index("y"), lax.axis_index("c")
    px = 1 - x if k & 4 else x
    py = 1 - y if k & 2 else y
    pc = 1 - c if k & 1 else c
    return (px, py, pc), 4 * px + 2 * py + pc


def _my_index():
    return 4 * lax.axis_index("x") + 2 * lax.axis_index("y") + lax.axis_index("c")


def _land_sds(a, mode, gather):
    if mode == "slab":
        return _sds(((N_DEV,) + a.shape) if gather else a.shape, a.dtype)
    assert mode == "rows", mode
    return _sds((N_DEV * a.shape[0],) + a.shape[1:] if gather else (N_DEV, a.shape[0] // N_DEV) + a.shape[1:], a.dtype)


def _part(ref, mode, shape, idx):
    if mode == "slab":
        return ref.at[idx]
    r = shape[0] // N_DEV
    return ref.at[pl.ds(idx * r, r)]


def _own_copies(ops, gather, srcs, lands, sems):
    me = _my_index()
    out = []
    for i, (a, mode) in enumerate(ops):
        s = srcs[i] if gather else _part(srcs[i], mode, a.shape, me)
        d = _part(lands[i], mode, _land_sds(a, mode, gather).shape, me) if gather else lands[i].at[me]
        out.append(pltpu.make_async_copy(s, d, sems.at[i]))
    return out


def _remote_copies(ops, gather, srcs, lands, send_sems, recv_sems):
    me = _my_index()
    n = len(ops)
    out = []
    for k in range(1, N_DEV):
        dev, idx = _peer(k)
        for i, (a, mode) in enumerate(ops):
            s = srcs[i] if gather else _part(srcs[i], mode, a.shape, idx)
            d = _part(lands[i], mode, _land_sds(a, mode, gather).shape, me) if gather else lands[i].at[me]
            out.append(pltpu.make_async_remote_copy(src_ref=s, dst_ref=d, send_sem=send_sems.at[(k - 1) * n + i],
                                                    recv_sem=recv_sems.at[(k - 1) * n + i], device_id=dev,
                                                    device_id_type=pl.DeviceIdType.MESH))
    return out


def _exchange(ops, *, gather, name):
    n = len(ops)

    def body(*refs):
        ins, outs = refs[:n], refs[n:2 * n]
        send_sems, recv_sems, local_sems = refs[2 * n:]
        copies = _own_copies(ops, gather, ins, outs, local_sems) + _remote_copies(ops, gather, ins, outs, send_sems, recv_sems)
        for cp in copies:
            cp.start()
        for cp in copies:
            cp.wait()

    return pl.pallas_call(
        body, name=name, out_shape=[_land_sds(a, m, gather) for a, m in ops], in_specs=[ANY] * n, out_specs=[ANY] * n,
        scratch_shapes=[pltpu.SemaphoreType.DMA((n * (N_DEV - 1),)), pltpu.SemaphoreType.DMA((n * (N_DEV - 1),)),
                        pltpu.SemaphoreType.DMA((n,))],
    )(*[a for a, _ in ops])


HBM = pl.BlockSpec(memory_space=pltpu.HBM)
SEM = pl.BlockSpec(memory_space=pltpu.SEMAPHORE)
SIDE_EFFECT = pltpu.SideEffectType.DATAFLOW_SIDE_EFFECTING


def _in_hbm(a):
    return pltpu.with_memory_space_constraint(a, pltpu.HBM)


def _place_own(ops, *, gather, name):
    n = len(ops)

    def zeros(k):
        return (0,) * k

    in_specs, out_specs = [], []
    for a, mode in ops:
        nd = a.ndim
        if gather and mode == "slab":
            in_specs.append(pl.BlockSpec(a.shape, lambda i, nd=nd: zeros(nd)))
            out_specs.append(pl.BlockSpec((1,) + a.shape, lambda i, nd=nd: (_my_index(),) + zeros(nd)))
        elif gather:
            in_specs.append(pl.BlockSpec(a.shape, lambda i, nd=nd: zeros(nd)))
            out_specs.append(pl.BlockSpec(a.shape, lambda i, nd=nd: (_my_index(),) + zeros(nd - 1)))
        elif mode == "slab":
            in_specs.append(pl.BlockSpec((1,) + a.shape[1:], lambda i, nd=nd: (_my_index(),) + zeros(nd - 1)))
            out_specs.append(pl.BlockSpec((1,) + a.shape[1:], lambda i, nd=nd: (_my_index(),) + zeros(nd - 1)))
        else:
            r = a.shape[0] // N_DEV
            in_specs.append(pl.BlockSpec((r,) + a.shape[1:], lambda i, nd=nd: (_my_index(),) + zeros(nd - 1)))
            out_specs.append(pl.BlockSpec((1, r) + a.shape[1:], lambda i, nd=nd: (_my_index(),) + zeros(nd)))

    def body(*refs):
        for i_ref, o_ref in zip(refs[:n], refs[n:2 * n]):
            if o_ref.shape == i_ref.shape:
                o_ref[...] = i_ref[...]
            else:
                o_ref[0] = i_ref[...]

    outs = _call(body, name=name, grid=(1,), in_specs=in_specs, out_specs=out_specs + [ANY] * n,
                 out_shape=[_land_sds(a, m, gather) for a, m in ops] + [_sds(a.shape, a.dtype) for a, _ in ops],
                 aliases={i: n + i for i in range(n)})(*[a for a, _ in ops])
    return outs[:n], [(src, m) for src, (_, m) in zip(outs[n:], ops)]


def _exchange_start(groups, *, gather, name):
    sizes = [len(ops) for ops, _ in groups]
    n = sum(sizes)
    G = len(groups)

    def body(*refs):
        srcs, lands = refs[:n], refs[n:2 * n]
        sems = refs[2 * n:2 * n + 2 * G]
        token = refs[-1]
        off = 0
        for g, (ops, _) in enumerate(groups):
            for cp in _remote_copies(ops, gather, srcs[off:off + sizes[g]], lands[off:off + sizes[g]], sems[2 * g], sems[2 * g + 1]):
                cp.start()
            off += sizes[g]
        token[...] = jnp.zeros_like(token)

    srcs = [a for ops, _ in groups for a, _ in ops]
    lands = [l for _, ls in groups for l in ls]
    sem_shapes = [pltpu.SemaphoreType.DMA((s * (N_DEV - 1),)) for s in sizes for _ in range(2)]
    outs = pl.pallas_call(
        body, name=name,
        out_shape=sem_shapes + [pltpu.HBM(a.shape, a.dtype) for a in srcs + lands] + [_sds((8, LANES))],
        in_specs=[HBM] * (2 * n), out_specs=[SEM] * (2 * G) + [HBM] * (2 * n) + [pl.BlockSpec(memory_space=pltpu.VMEM)],
        input_output_aliases={i: 2 * G + i for i in range(2 * n)},
        compiler_params=pltpu.CompilerParams(has_side_effects=SIDE_EFFECT))(*[_in_hbm(a) for a in srcs + lands])
    sems, thru, token = outs[:2 * G], outs[2 * G:2 * G + 2 * n], outs[-1]
    states, off = [], 0
    for g, s in enumerate(sizes):
        states.append((sems[2 * g], sems[2 * g + 1], thru[off:off + s], thru[n + off:n + off + s]))
        off += s
    return states, token


def _exchange_wait(ops, state, after, *, gather, name):
    send_sems, recv_sems, srcs, lands = state
    n = len(ops)

    def body(*refs):
        for cp in _remote_copies(ops, gather, refs[:n], refs[n:2 * n], refs[2 * n], refs[2 * n + 1]):
            cp.wait_send()
            cp.wait_recv()

    outs = pl.pallas_call(
        body, name=name, out_shape=[pltpu.HBM(a.shape, a.dtype) for a in list(srcs) + list(lands)],
        in_specs=[HBM] * (2 * n) + [SEM, SEM, ANY], out_specs=[HBM] * (2 * n),
        input_output_aliases={i: i for i in range(2 * n)},
        compiler_params=pltpu.CompilerParams(has_side_effects=SIDE_EFFECT))(*srcs, *lands, send_sems, recv_sems, after)
    return outs[n:]


ADAM_ROWS = 256


def _row_tile(R, cap=ADAM_ROWS):
    best = R
    if R > cap:
        for d in range(16, cap + 1, 16):
            if R % d == 0:
                best = d
    return best


def _adamw(g_layers, w, m, v, *, name):
    L = len(g_layers)
    J, R, Wd = g_layers[0].shape
    assert w.shape == (L, R, Wd), (g_layers[0].shape, w.shape)
    tr = _row_tile(R)
    nrt = R // tr
    c1 = 1.0 / (1.0 - ADAM_B1 ** ADAM_STEP)
    c2 = 1.0 / (1.0 - ADAM_B2 ** ADAM_STEP)

    def body(*refs):
        g_refs = refs[:L]
        w_ref, m_ref, v_ref, go_ref, d_ref, mo_ref, vo_ref = refs[L:]
        layer = pl.program_id(0)
        g = None
        for l, g_ref in enumerate(g_refs):
            gl = g_ref[0].astype(F32)
            for j in range(1, J):
                gl = gl + g_ref[j].astype(F32)
            g = gl if g is None else jnp.where(layer == l, gl, g)
        mn = ADAM_B1 * m_ref[...] + (1.0 - ADAM_B1) * g
        vn = ADAM_B2 * v_ref[...] + (1.0 - ADAM_B2) * (g * g)
        go_ref[...] = g
        mo_ref[...] = mn
        vo_ref[...] = vn
        d_ref[...] = -ADAM_LR * ((mn * c1) / (jnp.sqrt(vn * c2) + ADAM_EPS) + ADAM_WD * w_ref[...])

    def g_spec(l):
        return pl.BlockSpec((J, tr, Wd), lambda ll, i: (0, jnp.where(ll == l, i, jnp.where(ll < l, 0, nrt - 1)), 0))

    row = pl.BlockSpec((None, tr, Wd), lambda ll, i: (ll, i, 0))
    out = _sds((L, R, Wd))
    return _call(body, name=name, out_shape=(out, out, out, out), grid=(L, nrt),
                 in_specs=[g_spec(l) for l in range(L)] + [row, row, row], out_specs=(row, row, row, row))(*g_layers, w, m, v)


def _sum_slabs(slabs, *, name):
    n = len(slabs)

    def body(*refs):
        for g_ref, o_ref in zip(refs[:n], refs[n:]):
            g = g_ref[0]
            for j in range(1, g_ref.shape[0]):
                g = g + g_ref[j]
            o_ref[...] = g

    return _call(body, name=name, out_shape=[_sds(s.shape[1:]) for s in slabs])(*slabs)


def kernel(x, positions, norm_mix, norm_ffn, norm_final, mix_w_in, pool_w, pool_scale, attn_sinks, mix_w_out, ssm_w_in, ssm_conv_w, ssm_conv_b, ssm_dt_bias, ssm_A_log, ssm_D, ssm_norm, ssm_w_out, ffn_w_up, ffn_conv_w, ffn_conv_b, ffn_w_down, loss_target, m_norm_mix, m_norm_ffn, m_norm_final, m_mix_w_in, m_pool_w, m_pool_scale, m_attn_sinks, m_mix_w_out, m_ssm_w_in, m_ssm_conv_w, m_ssm_conv_b, m_ssm_dt_bias, m_ssm_A_log, m_ssm_D, m_ssm_norm, m_ssm_w_out, m_ffn_w_up, m_ffn_conv_w, m_ffn_conv_b, m_ffn_w_down, v_norm_mix, v_norm_ffn, v_norm_final, v_mix_w_in, v_pool_w, v_pool_scale, v_attn_sinks, v_mix_w_out, v_ssm_w_in, v_ssm_conv_w, v_ssm_conv_b, v_ssm_dt_bias, v_ssm_A_log, v_ssm_D, v_ssm_norm, v_ssm_w_out, v_ffn_w_up, v_ffn_conv_w, v_ffn_conv_b, v_ffn_w_down):
    args = dict(locals())
    wl = {n: args[n] for n in WEIGHTS}
    ml = {n: args["m_" + n] for n in WEIGHTS}
    vl = {n: args["v_" + n] for n in WEIGHTS}
    F = ffn_w_down.shape[1] * N_DEV
    DI, CD, NH = ssm_norm.shape[1] * N_DEV, ssm_conv_b.shape[1] * N_DEV, ssm_dt_bias.shape[1]
    Kc, Kf = ssm_conv_w.shape[1], ffn_conv_w.shape[1]
    n_up = ffn_w_up.shape[2]
    col_sharded = ("mix_w_in", "ssm_w_in", "ffn_w_up")

    def tr(a):
        return jnp.swapaxes(a, -1, -2)

    def two(a):
        return a.reshape(-1, a.shape[-1])

    def pay(a):
        return a.astype(PAYLOAD)

    order = ("mix_in", "mix_out", "ffn0_up", "ffn0_down", "ssm", "ssm_out", "ffn1_up", "ffn1_down")
    gops = {
        "mix_in": [(pay(tr(mix_w_in)[0]), "rows")],
        "mix_out": [(pay(mix_w_out[0]), "rows"), (two(ssm_conv_w), "slab"), (ssm_conv_b, "slab"), (ssm_norm, "slab"),
                    (two(ffn_conv_w), "slab")],
        "ffn0_up": [(pay(tr(ffn_w_up)[0]), "rows")], "ffn0_down": [(pay(ffn_w_down[0]), "rows")],
        "ssm": [(pay(tr(ssm_w_in)[0]), "slab")], "ssm_out": [(pay(ssm_w_out[0]), "rows")],
        "ffn1_up": [(pay(tr(ffn_w_up)[1]), "rows")], "ffn1_down": [(pay(ffn_w_down[1]), "rows")],
    }
    lands, handed = _place_own([op for g in order for op in gops[g]], gather=True, name="gather_own")
    groups, off = [], 0
    for g in order:
        gops[g] = handed[off:off + len(gops[g])]
        groups.append((gops[g], lands[off:off + len(gops[g])]))
        off += len(gops[g])
    gstates, token = _exchange_start(groups, gather=True, name="gather_start")
    gstate = dict(zip(order, gstates))
    P = {n: wl[n] for n in REPLICATED}

    def need(g, after):
        got = _exchange_wait(gops[g], gstate[g], after, gather=True, name="gather_wait_" + g)
        if g == "mix_in":
            P["mix_w_inT"] = got[0]
        elif g == "mix_out":
            P.update(mix_w_out=got[0], ssm_conv_w=got[1].transpose(1, 0, 2).reshape(Kc, CD), ssm_conv_b=got[2].reshape(1, CD),
                     ssm_norm=got[3].reshape(1, DI), ffn_conv_w=got[4].transpose(1, 0, 2).reshape(2, Kf, 2 * F))
        elif g == "ssm":
            w1T = got[0].reshape(-1, got[0].shape[-1])
            P.update(ssm_w_inT=w1T, ssm_wdtT=jnp.pad(w1T[DI + CD:], ((0, LANES - NH), (0, 0))))
        elif g == "ssm_out":
            P["ssm_w_out"] = got[0]
        elif g.endswith("_up"):
            P["ffn_w_upT" + g[3]] = got[0]
        else:
            P["ffn_w_down" + g[3]] = got[0]

    sent = {}

    def emit(g, d):
        if g == "mix":
            ops = [(d["mix_w_inT"], "rows"), (d["mix_w_out"], "rows"),
                   (d["ffn_conv_w"].reshape(2 * Kf, N_DEV, n_up).transpose(1, 0, 2), "slab")]
        elif g == "ssm":
            ops = [(d["ssm_w_inT"].reshape(N_DEV, -1, d["ssm_w_inT"].shape[-1]), "slab"), (d["ssm_w_out"], "rows"),
                   (d["ssm_conv_w"].reshape(Kc, N_DEV, -1).transpose(1, 0, 2), "slab"),
                   (d["ssm_conv_b"].reshape(N_DEV, 1, -1), "slab"), (d["ssm_norm"].reshape(N_DEV, 1, -1), "slab")]
        else:
            ops = [(d["ffn_w_upT"], "rows"), (d["ffn_w_down"], "rows")]
        own, ops = _place_own(ops, gather=False, name="scatter_own_" + g)
        (state,), tok = _exchange_start([(ops, own)], gather=False, name="scatter_start_" + g)
        sent[g] = (ops, state)
        return tok

    loss_lanes, grad_x, G = _local_step(x[0], positions.reshape(-1, 1), loss_target[0], P, need, emit, after=token)
    loss = lax.psum(loss_lanes[0, 0], ("x", "y", "c"))

    res = {}

    def update(n, g_layers):
        L = len(g_layers)
        g_layers = [g.reshape(g.shape[0], -1, g.shape[-1]) for g in g_layers]
        shape = (L,) + g_layers[0].shape[1:]
        view = tr if n in col_sharded else (lambda a: a)
        outs = _adamw(g_layers, view(wl[n]).reshape(shape), view(ml[n]).reshape(shape), view(vl[n]).reshape(shape),
                      name="adamw_" + n)
        for kind, a in zip(("grad", "delta", "new_m", "new_v"), outs):
            res[kind, n] = view(a.reshape(view(wl[n]).shape))

    rep_ops = [(a, "slab") for a in (G["norm_mix"], G["norm_ffn"], G["norm_final"], G["pool_w"].reshape(-1, LANES),
                                     G["pool_scale"], G["ffn_conv_b"], G["attn_sinks_rows"],
                                     G["ssm_dt_bias_g"].reshape(SSM_G, LANES), G["ssm_A_log_g"].reshape(SSM_G, LANES),
                                     G["ssm_D_g"].reshape(SSM_G, LANES))]
    rep_own, rep_ops = _place_own(rep_ops, gather=True, name="small_own")
    (rep_state,), rep_token = _exchange_start([(rep_ops, rep_own)], gather=True, name="small_start")

    recv = {g: _exchange_wait(sent[g][0], sent[g][1], rep_token, gather=False, name="scatter_wait_" + g)
            for g in ("ffn1", "ssm", "ffn0", "mix")}
    update("mix_w_in", [recv["mix"][0]])
    update("mix_w_out", [recv["mix"][1]])
    update("ffn_conv_w", [recv["mix"][2]])
    update("ssm_w_in", [recv["ssm"][0]])
    update("ssm_w_out", [recv["ssm"][1]])
    update("ssm_conv_w", [recv["ssm"][2]])
    update("ssm_conv_b", [recv["ssm"][3]])
    update("ssm_norm", [recv["ssm"][4]])
    update("ffn_w_up", [recv["ffn0"][0], recv["ffn1"][0]])
    update("ffn_w_down", [recv["ffn0"][1], recv["ffn1"][1]])

    rep = _exchange_wait(rep_ops, rep_state, res["new_v", "ffn_w_down"], gather=True, name="small_wait")
    for n, r in zip(("norm_mix", "norm_ffn", "norm_final", "pool_w", "pool_scale", "ffn_conv_b"), rep):
        update(n, [r])
    sinks_rows, bias_g, alog_g, d_g = _sum_slabs(rep[6:], name="sum_head_grads")
    update("attn_sinks", [sinks_rows[:, 0].reshape(1, 1, N_HEADS)])
    update("ssm_dt_bias", [_ungroup(bias_g)[None]])
    update("ssm_A_log", [_ungroup(alog_g)[None]])
    update("ssm_D", [_ungroup(d_g)[None]])

    return (loss, grad_x[None], *[res[k, n] for k in ("grad", "delta", "new_m", "new_v") for n in WEIGHTS])
```

```python
import functools
import math

import jax
import jax.numpy as jnp
from jax import lax
from jax.experimental import pallas as pl
from jax.experimental.pallas import tpu as pltpu

F32 = jnp.float32
BF16 = jnp.bfloat16

N_DEV = 8
LANES = 128
HEAD_DIM = 64
N_KV_HEADS = 2
GQ = 4
N_HEADS = N_KV_HEADS * GQ
BLOCK = 128
POOL_GROUPS = 4
ROPE_THETA = 10000.0
SSM_P = 64
SSM_G = 8
SSM_R = 4
SSM_N = 128
SSM_L = 128
NORM_EPS = 1e-6
SSM_NORM_EPS = 1e-5
ADAM_LR, ADAM_B1, ADAM_B2, ADAM_EPS, ADAM_WD, ADAM_STEP = 0.001, 0.9, 0.999, 1e-08, 0.01, 10
VMEM_LIMIT = 56 * 2 ** 20
PAYLOAD = jnp.bfloat16

REPLICATED = ("norm_mix", "norm_ffn", "norm_final", "pool_w", "pool_scale", "attn_sinks",
              "ssm_dt_bias", "ssm_A_log", "ssm_D", "ffn_conv_b")
WEIGHTS = ("norm_mix", "norm_ffn", "norm_final", "mix_w_in", "pool_w", "pool_scale", "attn_sinks", "mix_w_out",
           "ssm_w_in", "ssm_conv_w", "ssm_conv_b", "ssm_dt_bias", "ssm_A_log", "ssm_D", "ssm_norm", "ssm_w_out",
           "ffn_w_up", "ffn_conv_w", "ffn_conv_b", "ffn_w_down")


def _tile(n, cap):
    if n <= cap:
        return n
    best = None
    for d in range(LANES, cap + 1, LANES):
        if n % d == 0:
            best = d
    assert best is not None, (n, cap)
    return best


def _call(body, *, name, out_shape, grid=None, in_specs=None, out_specs=None, scratch=(), aliases=None):
    kw = {}
    if grid is not None:
        kw = dict(grid=grid, in_specs=in_specs, out_specs=out_specs)
    if aliases:
        kw["input_output_aliases"] = aliases
    return pl.pallas_call(
        body, name=name, out_shape=out_shape, scratch_shapes=list(scratch),
        compiler_params=pltpu.CompilerParams(vmem_limit_bytes=VMEM_LIMIT), **kw)


ANY = pl.BlockSpec(memory_space=pl.ANY)


def _sds(shape, dtype=F32):
    return jax.ShapeDtypeStruct(tuple(shape), dtype)


def _sigmoid(x):
    return 1.0 / (1.0 + jnp.exp(-x))


def _shift_dn(x, d, t):
    if d == 0:
        return x
    return jnp.where(t >= d, pltpu.roll(x, d, axis=0), 0.0)


def _shift_up(x, d, t):
    if d == 0:
        return x
    n = x.shape[0]
    return jnp.where(t < n - d, pltpu.roll(x, n - d, axis=0), 0.0)


def _mm(a, b, *, name, ta=False, tb=False, res=None, out_dtype=F32, b_rows=None, out_rows=None):
    M, K = (a.shape[1], a.shape[0]) if ta else a.shape
    b0, bn = b_rows if b_rows is not None else (0, b.shape[0])
    N = bn if tb else b.shape[1]
    assert (b.shape[1] if tb else bn) == K, (a.shape, b.shape, ta, tb, b_rows)
    tm, tn, tk = _tile(M, 1408), _tile(N, 1408), _tile(K, 1408)
    nk = K // tk
    dims = (((0 if ta else 1,), (1 if tb else 0,)), ((), ()))
    aliased = out_rows is not None and out_rows[2] is not None

    def body(*refs):
        a_ref, b_ref = refs[:2]
        r_ref = refs[2] if res is not None else None
        o_ref, acc = refs[-2:]
        k = pl.program_id(2)

        @pl.when(k == 0)
        def _():
            acc[...] = jnp.zeros_like(acc)

        acc[...] += lax.dot_general(a_ref[...].astype(BF16), b_ref[...].astype(BF16), dims,
                                    preferred_element_type=F32)

        @pl.when(k == nk - 1)
        def _():
            out = acc[...]
            if res is not None:
                out = out + r_ref[...]
            o_ref[...] = out.astype(out_dtype)

    a_spec = pl.BlockSpec((tk, tm), lambda i, j, k: (k, i)) if ta else pl.BlockSpec((tm, tk), lambda i, j, k: (i, k))
    if tb:
        assert b0 % tn == 0, (b_rows, tn)
        b_spec = pl.BlockSpec((tn, tk), lambda i, j, k: (b0 // tn + j, k))
    else:
        assert b0 % tk == 0, (b_rows, tk)
        b_spec = pl.BlockSpec((tk, tn), lambda i, j, k: (b0 // tk + k, j))
    ins, specs = [a, b], [a_spec, b_spec]
    if res is not None:
        ins.append(res)
        specs.append(pl.BlockSpec((tm, tn), lambda i, j, k: (i, j)))
    aliases = None
    if out_rows is None:
        o_spec = pl.BlockSpec((tm, tn), lambda i, j, k: (i, j))
        out_shape = _sds((M, N), out_dtype)
    else:
        total, o0, prev = out_rows
        assert o0 % tm == 0, (out_rows, tm)
        o_spec = pl.BlockSpec((tm, tn), lambda i, j, k: (o0 // tm + i, j))
        out_shape = _sds((total, N), out_dtype)
        if aliased:
            aliases = {len(ins): 0}
            ins.append(prev)
            specs.append(ANY)
    return _call(body, name=name, out_shape=out_shape, grid=(M // tm, N // tn, nk), in_specs=specs,
                 out_specs=o_spec, scratch=[pltpu.VMEM((tm, tn), F32)], aliases=aliases)(*ins)


def _rmsnorm(x, w, *, name, eps=NORM_EPS, after=None):
    S, D = x.shape
    tm = _tile(S, 512)
    tie = [] if after is None else [after]

    def body(x_ref, w_ref, *rest):
        o_ref = rest[-1]
        xf = x_ref[...]
        r = lax.rsqrt(jnp.mean(xf * xf, axis=-1, keepdims=True) + eps)
        o_ref[...] = (xf * r * w_ref[...]).astype(BF16)

    return _call(body, name=name, out_shape=_sds((S, D), BF16), grid=(S // tm,),
                 in_specs=[pl.BlockSpec((tm, D), lambda i: (i, 0)), pl.BlockSpec((1, D), lambda i: (0, 0))] + [ANY] * len(tie),
                 out_specs=pl.BlockSpec((tm, D), lambda i: (i, 0)))(x, w, *tie)


def _norm_bwd_math(xf, w, dh, eps):
    r = lax.rsqrt(jnp.mean(xf * xf, axis=-1, keepdims=True) + eps)
    xhat = xf * r
    dxh = dh * w
    dx = r * (dxh - xhat * jnp.mean(dxh * xhat, axis=-1, keepdims=True))
    dw = jnp.sum(dh * xhat, axis=0, keepdims=True)
    return dx, dw


def _mm_norm_bwd(a, b, x, w, dres, *, name, res=None, b_rows=None, after=None, eps=NORM_EPS):
    M, K = a.shape
    b0, bn = b_rows if b_rows is not None else (0, b.shape[0])
    D = b.shape[1]
    assert bn == K and x.shape == (M, D), (a.shape, b.shape, b_rows, x.shape)
    tm, tk = _tile(M, 512), _tile(K, 1408)
    assert b0 % tk == 0, (b_rows, tk)
    nk = K // tk

    def body(*refs):
        a_ref, b_ref, x_ref, w_ref, dr_ref = refs[:5]
        r_ref = refs[5] if res is not None else None
        dx_ref, dw_ref, acc = refs[-3:]
        i, k = pl.program_id(0), pl.program_id(1)

        @pl.when(k == 0)
        def _():
            acc[...] = jnp.zeros_like(acc)

        @pl.when((i == 0) & (k == 0))
        def _():
            dw_ref[...] = jnp.zeros_like(dw_ref)

        acc[...] += jnp.dot(a_ref[...].astype(BF16), b_ref[...].astype(BF16), preferred_element_type=F32)

        @pl.when(k == nk - 1)
        def _():
            dh = acc[...] if res is None else acc[...] + r_ref[...]
            dx, dw = _norm_bwd_math(x_ref[...], w_ref[...], dh, eps)
            dx_ref[...] = dr_ref[...] + dx
            dw_ref[...] += dw

    row = pl.BlockSpec((tm, D), lambda i, k: (i, 0))
    vec = pl.BlockSpec((1, D), lambda i, k: (0, 0))
    ins = [a, b, x, w, dres]
    specs = [pl.BlockSpec((tm, tk), lambda i, k: (i, k)), pl.BlockSpec((tk, D), lambda i, k: (b0 // tk + k, 0)), row, vec, row]
    if res is not None:
        ins.append(res)
        specs.append(row)
    if after is not None:
        ins.append(after)
        specs.append(ANY)
    return _call(body, name=name, out_shape=(_sds((M, D)), _sds((1, D))), grid=(M // tm, nk), in_specs=specs,
                 out_specs=(row, vec), scratch=[pltpu.VMEM((tm, D), F32)])(*ins)


def _final_loss(x, w, target, *, name):
    S, D = x.shape
    tm = _tile(S, 512)

    def body(x_ref, w_ref, t_ref, loss_ref, dx_ref, dw_ref):
        xf, wv = x_ref[...], w_ref[...]
        r = lax.rsqrt(jnp.mean(xf * xf, axis=-1, keepdims=True) + NORM_EPS)
        err = xf * r * wv - t_ref[...]
        part = 0.5 * jnp.sum(jnp.mean(err * err, axis=-1, keepdims=True), axis=0, keepdims=True)
        dx, dw = _norm_bwd_math(xf, wv, err * (1.0 / D), NORM_EPS)
        dx_ref[...] = dx

        @pl.when(pl.program_id(0) == 0)
        def _():
            dw_ref[...] = jnp.zeros_like(dw_ref)
            loss_ref[...] = jnp.zeros_like(loss_ref)

        dw_ref[...] += dw
        loss_ref[...] += jnp.broadcast_to(part, loss_ref.shape)

    row = pl.BlockSpec((tm, D), lambda i: (i, 0))
    vec = pl.BlockSpec((1, D), lambda i: (0, 0))
    return _call(body, name=name, out_shape=(_sds((1, LANES)), _sds((S, D)), _sds((1, D))), grid=(S // tm,),
                 in_specs=[row, vec, row], out_specs=(pl.BlockSpec((1, LANES), lambda i: (0, 0)), row, vec))(x, w, target)


def _rope_tables(pos, inv_freq):
    S = pos.shape[0]
    tm = _tile(S, 512)

    def body(p_ref, f_ref, c_ref, s_ref):
        ang = p_ref[...].astype(F32) * f_ref[...]
        c_ref[...] = jnp.cos(ang)
        s_ref[...] = jnp.sin(ang)

    blk = pl.BlockSpec((tm, LANES), lambda i: (i, 0))
    return _call(body, name="rope_tables", out_shape=(_sds((S, LANES)), _sds((S, LANES))), grid=(S // tm,),
                 in_specs=[pl.BlockSpec((tm, 1), lambda i: (i, 0)), pl.BlockSpec((1, LANES), lambda i: (0, 0))],
                 out_specs=(blk, blk))(pos, inv_freq)


def _rot_half(t):
    lane = lax.broadcasted_iota(jnp.int32, t.shape, 1)
    lo = (lane % HEAD_DIM) < (HEAD_DIM // 2)
    return jnp.where(lo, -pltpu.roll(t, LANES - HEAD_DIM // 2, axis=1), pltpu.roll(t, HEAD_DIM // 2, axis=1))


def _rope(t, c, s):
    return t * c + _rot_half(t) * s


def _unrope(dy, c, s):
    return dy * c - _rot_half(dy * s)


PD = POOL_GROUPS * LANES
QD = N_HEADS * HEAD_DIM
KD = N_KV_HEADS * HEAD_DIM
assert PD % QD == 0 and (PD + QD) % (2 * KD) == 0 and KD == LANES
def _attn_probs(q, kcat, sink, mask):
    s = lax.dot_general(q.astype(BF16), kcat, (((1,), (1,)), ((), ())), preferred_element_type=F32) * (HEAD_DIM ** -0.5)
    s = jnp.where(mask, s, -jnp.inf)
    m = jnp.maximum(jnp.max(s, axis=1, keepdims=True), sink)
    p = jnp.exp(s - m)
    ps = jnp.exp(sink - m)
    inv = 1.0 / (jnp.sum(p, axis=1, keepdims=True) + ps)
    return p * inv, ps * inv


def _attn_mask(n):
    qi = lax.broadcasted_iota(jnp.int32, (BLOCK, 2 * BLOCK), 0)
    kj = lax.broadcasted_iota(jnp.int32, (BLOCK, 2 * BLOCK), 1)
    rel = qi + BLOCK - kj
    return (rel >= 0) & (rel < BLOCK) & ((n > 0) | (kj >= BLOCK))


def _attn_in_specs(nb):
    def cur(n):
        return jnp.minimum(n, nb - 1)

    def prev(n):
        return jnp.clip(n - 1, 0, nb - 1)

    kvb = (PD + QD) // (2 * KD)
    return [pl.BlockSpec(memory_space=pltpu.SMEM),
            pl.BlockSpec((BLOCK, QD), lambda n: (cur(n), PD // QD)),
            pl.BlockSpec((BLOCK, 2 * KD), lambda n: (cur(n), kvb)),
            pl.BlockSpec((BLOCK, 2 * KD), lambda n: (prev(n), kvb)),
            pl.BlockSpec((BLOCK, LANES), lambda n: (cur(n), 0)), pl.BlockSpec((BLOCK, LANES), lambda n: (cur(n), 0)),
            pl.BlockSpec((BLOCK, LANES), lambda n: (prev(n), 0)), pl.BlockSpec((BLOCK, LANES), lambda n: (prev(n), 0))]


def _attn_keys(kvc_ref, kvp_ref, cc, sc, cp, sp):
    kc = _rope(kvc_ref[:, :KD], cc, sc)
    kp = _rope(kvp_ref[:, :KD], cp, sp)
    vc, vp = kvc_ref[:, KD:], kvp_ref[:, KD:]
    kcat, vcat = [], []
    for kk in range(N_KV_HEADS):
        sl = slice(kk * HEAD_DIM, (kk + 1) * HEAD_DIM)
        kcat.append(jnp.concatenate([kp[:, sl], kc[:, sl]], axis=0).astype(BF16))
        vcat.append(jnp.concatenate([vp[:, sl], vc[:, sl]], axis=0).astype(BF16))
    return kcat, vcat


def _attn_fwd(proj, cos, sin, sinks, cat):
    S = proj.shape[0]
    nb = S // BLOCK

    def body(sink_ref, q_ref, kvc_ref, kvp_ref, cc_ref, sc_ref, cp_ref, sp_ref, cat_ref, o_ref):
        n = pl.program_id(0)
        cc, sc = cc_ref[...], sc_ref[...]
        kcat, vcat = _attn_keys(kvc_ref, kvp_ref, cc, sc, cp_ref[...], sp_ref[...])
        mask = _attn_mask(n)
        def head_pair(j):
            qr = _rope(q_ref[:, j * LANES:(j + 1) * LANES], cc, sc)
            for e in range(LANES // HEAD_DIM):
                yield
                h = j * (LANES // HEAD_DIM) + e
                pn, _ = _attn_probs(qr[:, e * HEAD_DIM:(e + 1) * HEAD_DIM], kcat[h // GQ], sink_ref[0, h], mask)
                yield
                o_ref[:, h * HEAD_DIM:(h + 1) * HEAD_DIM] = jnp.dot(
                    pn.astype(BF16), vcat[h // GQ], preferred_element_type=F32).astype(o_ref.dtype)

        _interleave([head_pair(j) for j in range(QD // LANES)])

    return _call(body, name="attn_fwd", out_shape=_sds(cat.shape, cat.dtype), grid=(nb,),
                 in_specs=_attn_in_specs(nb) + [ANY], out_specs=pl.BlockSpec((BLOCK, QD), lambda n: (n, PD // QD)),
                 aliases={8: 0})(sinks, proj, proj, proj, cos, sin, cos, sin, cat)


def _attn_bwd(proj, cos, sin, sinks, dcat):
    S = proj.shape[0]
    nb = S // BLOCK
    scale = HEAD_DIM ** -0.5
    per = LANES // HEAD_DIM

    def body(sink_ref, q_ref, kvc_ref, kvp_ref, cc_ref, sc_ref, cp_ref, sp_ref, do_ref, o_ref, ds_ref, hold, carry, part, pair):
        n = pl.program_id(0)

        @pl.when(n == 0)
        def _():
            hold[...] = jnp.zeros_like(hold)
            carry[...] = jnp.zeros_like(carry)
            ds_ref[...] = jnp.zeros_like(ds_ref)

        live = jnp.where(n < nb, 1.0, 0.0)
        cc, sc, cp, sp = cc_ref[...], sc_ref[...], cp_ref[...], sp_ref[...]
        kcat, vcat = _attn_keys(kvc_ref, kvp_ref, cc, sc, cp, sp)
        mask = _attn_mask(n)
        o_ref[:, :PD] = jnp.zeros((BLOCK, PD), F32)
        o_ref[:, PD:PD + QD] = hold[...]
        dk = [jnp.zeros((2 * BLOCK, HEAD_DIM), F32) for _ in range(N_KV_HEADS)]
        dv = [jnp.zeros((2 * BLOCK, HEAD_DIM), F32) for _ in range(N_KV_HEADS)]
        row = lax.broadcasted_iota(jnp.int32, (8, LANES), 0)
        acc = {"dsk": jnp.zeros((8, LANES), F32)}

        def head_pair(j):
            qr = _rope(q_ref[:, j * LANES:(j + 1) * LANES], cc, sc)
            for e in range(per):
                yield
                h = j * per + e
                kk = h // GQ
                qh = qr[:, e * HEAD_DIM:(e + 1) * HEAD_DIM]
                pn, psn = _attn_probs(qh, kcat[kk], sink_ref[0, h], mask)
                yield
                doh = (do_ref[:, h * HEAD_DIM:(h + 1) * HEAD_DIM] * live).astype(BF16)
                dp = lax.dot_general(doh, vcat[kk], NT, preferred_element_type=F32)
                yield
                delta = jnp.sum(pn * dp, axis=1, keepdims=True)
                ds = (pn * (dp - delta) * scale).astype(BF16)
                pair[j, :, e * HEAD_DIM:(e + 1) * HEAD_DIM] = jnp.dot(ds, kcat[kk], preferred_element_type=F32)
                yield
                dk[kk] = dk[kk] + lax.dot_general(ds, qh.astype(BF16), TN, preferred_element_type=F32)
                dv[kk] = dv[kk] + lax.dot_general(pn.astype(BF16), doh, TN, preferred_element_type=F32)
                acc["dsk"] = acc["dsk"] + jnp.where(row == h, -jnp.sum(psn * delta), 0.0)
            yield
            hold[:, j * LANES:(j + 1) * LANES] = _unrope(pair[j], cc, sc)

        _interleave([head_pair(j) for j in range(QD // LANES)])
        dsk = acc["dsk"]
        for kk in range(N_KV_HEADS):
            sl = slice(kk * HEAD_DIM, (kk + 1) * HEAD_DIM)
            sv = slice(KD + kk * HEAD_DIM, KD + (kk + 1) * HEAD_DIM)
            part[0, :, sl] = dk[kk][:BLOCK]
            part[0, :, sv] = dv[kk][:BLOCK]
            part[1, :, sl] = dk[kk][BLOCK:]
            part[1, :, sv] = dv[kk][BLOCK:]
        done = carry[...] + part[0]
        o_ref[:, PD + QD:PD + QD + KD] = _unrope(done[:, :KD], cp, sp)
        o_ref[:, PD + QD + KD:] = done[:, KD:]
        carry[...] = part[1]
        ds_ref[...] += dsk

    return _call(body, name="attn_bwd", out_shape=(_sds((S, PD + QD + 2 * KD)), _sds((8, LANES))), grid=(nb + 1,),
                 in_specs=_attn_in_specs(nb) + [pl.BlockSpec((BLOCK, QD), lambda n: (jnp.minimum(n, nb - 1), PD // QD))],
                 out_specs=(pl.BlockSpec((BLOCK, PD + QD + 2 * KD), lambda n: (jnp.maximum(n - 1, 0), 0)),
                            pl.BlockSpec((8, LANES), lambda n: (0, 0))),
                 scratch=[pltpu.VMEM((BLOCK, QD), F32), pltpu.VMEM((BLOCK, 2 * KD), F32),
                          pltpu.VMEM((2, BLOCK, 2 * KD), F32), pltpu.VMEM((QD // LANES, BLOCK, LANES), F32)])(
                     sinks, proj, proj, proj, cos, sin, cos, sin, dcat)


def _pool_sums(u, g, t, shift):
    s2 = u + shift(u, 1, t)
    s4 = s2 + shift(s2, 2, t)
    s8 = s4 + shift(s4, 4, t)
    s16 = s8 + shift(s8, 8, t)
    return jnp.where(g == 0, s2, jnp.where(g == 1, s4, jnp.where(g == 2, s8, s16)))


def _pool_specs(S):
    col = pl.BlockSpec((S, LANES), lambda g: (0, g))
    wsp = pl.BlockSpec((1, LANES, LANES), lambda g: (g, 0, 0))
    vec = pl.BlockSpec((1, LANES), lambda g: (0, g))
    return col, wsp, vec


def _pool_fwd(proj, pool_w, scale):
    S = proj.shape[0]
    col, wsp, vec = _pool_specs(S)

    def body(u_ref, w_ref, s_ref, o_ref):
        g = pl.program_id(0)
        u = u_ref[...]
        t = lax.broadcasted_iota(jnp.int32, u.shape, 0)
        cnt = jnp.minimum(t + 1, 2 << g).astype(F32)
        pm = _pool_sums(u, g, t, _shift_dn) / cnt - u
        o_ref[...] = (jnp.dot(pm.astype(BF16), w_ref[0].astype(BF16), preferred_element_type=F32) * s_ref[...]).astype(BF16)

    return _call(body, name="pool_fwd", out_shape=_sds((S, PD + QD), BF16), grid=(POOL_GROUPS,),
                 in_specs=[col, wsp, vec], out_specs=col)(proj, pool_w, scale)


def _pool_bwd(proj, pool_w, scale, dcat, dproj):
    S = proj.shape[0]
    col, wsp, vec = _pool_specs(S)

    def body(u_ref, w_ref, s_ref, d_ref, dproj_ref, du_ref, dw_ref, dsc_ref):
        g = pl.program_id(0)
        u = u_ref[...]
        t = lax.broadcasted_iota(jnp.int32, u.shape, 0)
        cnt = jnp.minimum(t + 1, 2 << g).astype(F32)
        pm = (_pool_sums(u, g, t, _shift_dn) / cnt - u).astype(BF16)
        wv = w_ref[0].astype(BF16)
        d = d_ref[...]
        pw = jnp.dot(pm, wv, preferred_element_type=F32)
        dsc_ref[...] = jnp.sum(pw * d, axis=0, keepdims=True)
        dpw = (d * s_ref[...]).astype(BF16)
        dw_ref[0] = lax.dot_general(pm, dpw, (((0,), (0,)), ((), ())), preferred_element_type=F32)
        dpm = lax.dot_general(dpw, wv, (((1,), (1,)), ((), ())), preferred_element_type=F32)
        du_ref[...] = _pool_sums(dpm / cnt, g, t, _shift_up) - dpm

    return _call(body, name="pool_bwd",
                 out_shape=(_sds(dproj.shape), _sds((POOL_GROUPS, LANES, LANES)), _sds((1, POOL_GROUPS * LANES))),
                 grid=(POOL_GROUPS,), in_specs=[col, wsp, vec, col, ANY], out_specs=(col, wsp, vec),
                 aliases={4: 0})(proj, pool_w, scale, dcat, dproj)


def _conv(x, w_ref, b_ref, t):
    K = w_ref.shape[0]
    y = b_ref[...] + jnp.zeros_like(x)
    for k in range(K):
        y = y + w_ref[k:k + 1, :] * _shift_dn(x, K - 1 - k, t)
    return y


def _silu_grad(y):
    sg = _sigmoid(y)
    return sg * (1.0 + y * (1.0 - sg))


CONV_ROWS = 64
HALO = 8


def _win_above(ref, r0):
    if isinstance(r0, int):
        assert r0 == 0
        return jnp.concatenate([jnp.zeros((HALO, ref.shape[1]), F32), ref[0:CONV_ROWS, :]], axis=0)
    return ref[pl.ds(pl.multiple_of(r0 - HALO, HALO), CONV_ROWS + HALO), :]


def _rows_at(win, start):
    if start % 8 == 0:
        return win[start:start + CONV_ROWS]
    base = start // 8 * 8
    return pltpu.roll(win, win.shape[0] - (start - base), axis=0)[base:base + CONV_ROWS]


def _taps_above(win, K):
    return [_rows_at(win, HALO - (K - 1 - k)) for k in range(K)]


def _conv_taps(taps, w, b):
    y = b
    for k in range(len(w)):
        y = y + w[k] * taps[k]
    return y


def _conv_t_win(win, w):
    K = len(w)
    out = None
    for k in range(K):
        d = K - 1 - k
        term = w[k] * _rows_at(win, d)
        out = term if out is None else out + term
    return out


def _fold8(x):
    return jnp.sum(x.reshape(CONV_ROWS // 8, 8, x.shape[-1]), axis=0)


def _chunk_loop(S, step, init):
    carry = step(0, init)
    return lax.fori_loop(1, S // CONV_ROWS, lambda i, c: step(pl.multiple_of(i * CONV_ROWS, CONV_ROWS), c), carry)


def _ffn_mid_specs(S, K, layer, nf):
    return [pl.BlockSpec((S, LANES), lambda j: (0, j)), pl.BlockSpec((S, LANES), lambda j: (0, nf + j)),
            pl.BlockSpec((None, K, LANES), lambda j: (layer, 0, j)), pl.BlockSpec((None, K, LANES), lambda j: (layer, 0, nf + j)),
            pl.BlockSpec((None, 1, LANES), lambda j: (layer, 0, j)), pl.BlockSpec((None, 1, LANES), lambda j: (layer, 0, nf + j))]


def _ffn_mid_fwd(a, cw, cb, layer):
    S, F = a.shape[0], a.shape[1] // 2
    nf = F // LANES
    K = cw.shape[1]

    def body(au_ref, ag_ref, wu_ref, wg_ref, bu_ref, bg_ref, o_ref):
        t = lax.broadcasted_iota(jnp.int32, (S, LANES), 0)
        hu = _conv(au_ref[...], wu_ref, bu_ref, t)
        hg = _conv(ag_ref[...], wg_ref, bg_ref, t)
        o_ref[...] = (hg * _sigmoid(hg) * hu).astype(BF16)

    return _call(body, name="ffn_mid_fwd", out_shape=_sds((S, F), BF16), grid=(nf,),
                 in_specs=_ffn_mid_specs(S, K, layer, nf), out_specs=pl.BlockSpec((S, LANES), lambda j: (0, j)))(
                     a, a, cw, cw, cb[:, None], cb[:, None])


def _ffn_mid_bwd(a, cw, cb, layer, dact):
    S, F = a.shape[0], a.shape[1] // 2
    nf = F // LANES
    K = cw.shape[1]

    def body(au_ref, ag_ref, wu_ref, wg_ref, bu_ref, bg_ref, d_ref, dau_ref, dag_ref, dwu_ref, dwg_ref, dbu_ref, dbg_ref,
             dhu_s, dhg_s):
        T = CONV_ROWS
        wu = [wu_ref[k:k + 1, :] for k in range(K)]
        wg = [wg_ref[k:k + 1, :] for k in range(K)]
        bu, bg = bu_ref[...], bg_ref[...]
        zero8 = jnp.zeros((HALO, LANES), F32)
        dhu_s[S:S + HALO, :] = zero8
        dhg_s[S:S + HALO, :] = zero8

        def first_pass(r0, acc):
            tu, tg = _taps_above(_win_above(au_ref, r0), K), _taps_above(_win_above(ag_ref, r0), K)
            hu, hg = _conv_taps(tu, wu, bu), _conv_taps(tg, wg, bg)
            d = d_ref[pl.ds(r0, T), :].astype(F32)
            sg = _sigmoid(hg)
            dhu = d * hg * sg
            dhg = d * hu * (sg * (1.0 + hg * (1.0 - sg)))
            dhu_s[pl.ds(r0, T), :] = dhu
            dhg_s[pl.ds(r0, T), :] = dhg
            new = []
            for dh, taps in ((dhu, tu), (dhg, tg)):
                for k in range(K):
                    new.append(acc[len(new)] + _fold8(dh * taps[k]))
            new.append(acc[2 * K] + _fold8(dhu))
            new.append(acc[2 * K + 1] + _fold8(dhg))
            return tuple(new)

        acc = _chunk_loop(S, first_pass, tuple(zero8 for _ in range(2 * K + 2)))
        for k in range(K):
            dwu_ref[k:k + 1, :] = jnp.sum(acc[k], axis=0, keepdims=True)
            dwg_ref[k:k + 1, :] = jnp.sum(acc[K + k], axis=0, keepdims=True)
        dbu_ref[...] = jnp.sum(acc[2 * K], axis=0, keepdims=True)
        dbg_ref[...] = jnp.sum(acc[2 * K + 1], axis=0, keepdims=True)

        def second_pass(i, carry):
            r0 = pl.multiple_of(i * T, T)
            dau_ref[pl.ds(r0, T), :] = _conv_t_win(dhu_s[pl.ds(r0, T + HALO), :], wu).astype(BF16)
            dag_ref[pl.ds(r0, T), :] = _conv_t_win(dhg_s[pl.ds(r0, T + HALO), :], wg).astype(BF16)
            return carry

        lax.fori_loop(0, S // T, second_pass, 0)

    col = pl.BlockSpec((S, LANES), lambda j: (0, j))
    wsp = pl.BlockSpec((K, LANES), lambda j: (0, j))
    bsp = pl.BlockSpec((1, LANES), lambda j: (0, j))
    dau, dag, dwu, dwg, dbu, dbg = _call(
        body, name="ffn_mid_bwd",
        out_shape=(_sds((S, F), BF16), _sds((S, F), BF16), _sds((K, F)), _sds((K, F)), _sds((1, F)), _sds((1, F))), grid=(nf,),
        in_specs=_ffn_mid_specs(S, K, layer, nf) + [col], out_specs=(col, col, wsp, wsp, bsp, bsp),
        scratch=[pltpu.VMEM((S + HALO, LANES), F32), pltpu.VMEM((S + HALO, LANES), F32)])(
            a, a, cw, cw, cb[:, None], cb[:, None], dact)
    return dau, dag, jnp.concatenate([dwu, dwg], axis=1), jnp.concatenate([dbu, dbg], axis=1)


def _conv_silu_fwd(x, cw, cb):
    S = x.shape[0]
    K, C = cw.shape

    def body(x_ref, w_ref, b_ref, o_ref):
        t = lax.broadcasted_iota(jnp.int32, (S, LANES), 0)
        y = _conv(x_ref[...], w_ref, b_ref, t)
        o_ref[...] = y * _sigmoid(y)

    col = pl.BlockSpec((S, LANES), lambda j: (0, j))
    return _call(body, name="conv_silu_fwd", out_shape=_sds((S, C)), grid=(C // LANES,),
                 in_specs=[col, pl.BlockSpec((K, LANES), lambda j: (0, j)), pl.BlockSpec((1, LANES), lambda j: (0, j))],
                 out_specs=col)(x, cw, cb)


def _conv_silu_bwd(x, cw, cb, douts):
    S = x.shape[0]
    K, C = cw.shape
    starts, off = [], 0
    for d in douts:
        starts.append(off)
        off += d.shape[1] // LANES
    assert off == C // LANES

    def body(x_ref, w_ref, b_ref, *rest):
        dy_s = rest[-1]
        d_refs, (dx_ref, dw_ref, db_ref) = rest[:len(douts)], rest[len(douts):-1]
        j = pl.program_id(0)
        T = CONV_ROWS
        w = [w_ref[k:k + 1, :] for k in range(K)]
        b = b_ref[...]
        zero8 = jnp.zeros((HALO, LANES), F32)
        dy_s[S:S + HALO, :] = zero8

        def first_pass(r0, acc):
            taps = _taps_above(_win_above(x_ref, r0), K)
            y = _conv_taps(taps, w, b)
            d = d_refs[0][pl.ds(r0, T), :]
            for i in range(1, len(douts)):
                d = jnp.where(j >= starts[i], d_refs[i][pl.ds(r0, T), :], d)
            dy = d * _silu_grad(y)
            dy_s[pl.ds(r0, T), :] = dy
            return tuple(acc[k] + _fold8(dy * taps[k]) for k in range(K)) + (acc[K] + _fold8(dy),)

        acc = _chunk_loop(S, first_pass, tuple(zero8 for _ in range(K + 1)))
        for k in range(K):
            dw_ref[k:k + 1, :] = jnp.sum(acc[k], axis=0, keepdims=True)
        db_ref[...] = jnp.sum(acc[K], axis=0, keepdims=True)

        def second_pass(i, carry):
            r0 = pl.multiple_of(i * T, T)
            dx_ref[pl.ds(r0, T), :] = _conv_t_win(dy_s[pl.ds(r0, T + HALO), :], w).astype(BF16)
            return carry

        lax.fori_loop(0, S // T, second_pass, 0)

    col = pl.BlockSpec((S, LANES), lambda j: (0, j))
    wsp = pl.BlockSpec((K, LANES), lambda j: (0, j))
    bsp = pl.BlockSpec((1, LANES), lambda j: (0, j))

    def dspec(i):
        nblk = douts[i].shape[1] // LANES
        return pl.BlockSpec((S, LANES), lambda j: (0, jnp.clip(j - starts[i], 0, nblk - 1)))

    return _call(body, name="conv_silu_bwd", out_shape=(_sds((S, C), BF16), _sds((K, C)), _sds((1, C))), grid=(C // LANES,),
                 in_specs=[col, wsp, bsp] + [dspec(i) for i in range(len(douts))],
                 out_specs=(col, wsp, bsp), scratch=[pltpu.VMEM((S + HALO, LANES), F32)])(x, cw, cb, *douts)


HI = lax.Precision.HIGHEST


def _ssd_prep_fwd(proj, col0, bias_g, alog_g):
    S = proj.shape[0]
    nc = S // SSM_L
    b0 = col0 // LANES

    def body(raw_ref, b_ref, al_ref, pre_ref, dt_ref, acs_ref, acst_ref):
        r_i = lax.broadcasted_iota(jnp.int32, (LANES, LANES), 0)
        c_i = lax.broadcasted_iota(jnp.int32, (LANES, LANES), 1)
        live = c_i < SSM_R
        tril = jnp.where(r_i >= c_i, 1.0, 0.0)
        raw = raw_ref[...]
        for g in range(SSM_G):
            mine = raw if g == 0 else pltpu.roll(raw, LANES - SSM_R * g, axis=1)
            pre = jnp.where(live, mine, 0.0) + b_ref[g]
            dt = jnp.where(live, jnp.logaddexp(pre, 0.0), 0.0)
            a = dt * (-jnp.exp(al_ref[g]))
            acs = jnp.dot(tril, a, preferred_element_type=F32, precision=HI)
            pre_ref[g] = pre
            dt_ref[g] = dt
            acs_ref[g] = acs
            acst_ref[g] = acs.T

    gsp = pl.BlockSpec((SSM_G, 1, LANES), lambda c: (0, 0, 0))
    blk = pl.BlockSpec((SSM_G, SSM_L, LANES), lambda c: (0, c, 0))
    big = _sds((SSM_G, S, LANES))
    return _call(body, name="ssd_prep_fwd", out_shape=(big, big, big, _sds((SSM_G, LANES, S))), grid=(nc,),
                 in_specs=[pl.BlockSpec((SSM_L, LANES), lambda c: (c, b0)), gsp, gsp],
                 out_specs=(blk, blk, blk, pl.BlockSpec((SSM_G, LANES, SSM_L), lambda c: (0, 0, c))))(proj, bias_g, alog_g)


def _ssd_prep_bwd(pre_g, dt_g, alog_g, ddt_g, dacs_g, dacst_g):
    S = pre_g.shape[1]
    nc = S // SSM_L

    def body(pre_ref, dt_ref, al_ref, ddt_ref, dacs_ref, dacst_ref, draw_ref, db_ref, dal_ref):
        c = pl.program_id(0)
        r_i = lax.broadcasted_iota(jnp.int32, (LANES, LANES), 0)
        c_i = lax.broadcasted_iota(jnp.int32, (LANES, LANES), 1)
        live = c_i < SSM_R
        triu = jnp.where(r_i <= c_i, 1.0, 0.0)

        @pl.when(c == 0)
        def _():
            db_ref[...] = jnp.zeros_like(db_ref)
            dal_ref[...] = jnp.zeros_like(dal_ref)

        draw = jnp.zeros((SSM_L, LANES), F32)
        for g in range(SSM_G):
            dacs = dacs_ref[g] + dacst_ref[g].T
            da = jnp.dot(triu, dacs, preferred_element_type=F32, precision=HI)
            A = -jnp.exp(al_ref[g])
            ddt = ddt_ref[g] + da * A
            dpre = jnp.where(live, ddt * _sigmoid(pre_ref[g]), 0.0)
            draw = draw + (dpre if g == 0 else pltpu.roll(dpre, SSM_R * g, axis=1))
            db_ref[g] += jnp.sum(dpre, axis=0, keepdims=True)
            dal_ref[g] += jnp.where(live[:1], jnp.sum(da * dt_ref[g], axis=0, keepdims=True) * A, 0.0)
        draw_ref[...] = draw

    gsp = pl.BlockSpec((SSM_G, 1, LANES), lambda c: (0, 0, 0))
    blk = pl.BlockSpec((SSM_G, SSM_L, LANES), lambda c: (0, c, 0))
    gout = _sds((SSM_G, 1, LANES))
    return _call(body, name="ssd_prep_bwd", out_shape=(_sds((S, LANES)), gout, gout), grid=(nc,),
                 in_specs=[blk, blk, gsp, blk, blk, pl.BlockSpec((SSM_G, LANES, SSM_L), lambda c: (0, 0, c))],
                 out_specs=(pl.BlockSpec((SSM_L, LANES), lambda c: (c, 0)), gsp, gsp))(pre_g, dt_g, alog_g, ddt_g, dacs_g, dacst_g)


NT = (((1,), (1,)), ((), ()))
TN = (((0,), (0,)), ((), ()))
SSM_HP = SSM_R * SSM_P


def _ssd_group_terms(xs_ref, dt_ref, acs_ref, d_ref):
    hid = lax.broadcasted_iota(jnp.int32, (1, SSM_HP), 1) // SSM_P
    rid = lax.broadcasted_iota(jnp.int32, (SSM_HP, 1), 0) // SSM_P

    def widen(cols):
        out = cols[0]
        for r in range(1, SSM_R):
            out = jnp.where(hid == r, cols[r], out)
        return out

    dt_c = [dt_ref[:, r:r + 1] for r in range(SSM_R)]
    acs_c = [acs_ref[:, r:r + 1] for r in range(SSM_R)]
    last = [acs_ref[SSM_L - 1:SSM_L, r:r + 1] for r in range(SSM_R)]
    decay_c = [jnp.exp(last[r] - acs_c[r]) for r in range(SSM_R)]
    cd = [jnp.exp(last[r]) for r in range(SSM_R)]
    cd_rows = cd[0]
    for r in range(1, SSM_R):
        cd_rows = jnp.where(rid == r, cd[r], cd_rows)
    xs = xs_ref[...]
    return (xs, xs * widen(dt_c), widen([jnp.exp(a) for a in acs_c]), widen(decay_c),
            widen([d_ref[:, r:r + 1] for r in range(SSM_R)]), cd_rows, dt_c, decay_c, cd)


def _ssd_lmat(acs_ref, acst_ref, r, tril):
    return jnp.exp(jnp.where(tril, acs_ref[:, r:r + 1] - acst_ref[r:r + 1, :], -jnp.inf))


SSM_GPS = 4


def _ssd_specs(rev, nc):
    def cc(c):
        return nc - 1 - c if rev else c
    xs_blocks = (SSM_G * SSM_HP) // (SSM_GPS * SSM_N)
    xs = pl.BlockSpec((SSM_L, SSM_GPS * SSM_HP), lambda g, c: (cc(c), g))
    bsp = pl.BlockSpec((SSM_L, SSM_GPS * SSM_N), lambda g, c: (cc(c), xs_blocks + g))
    csp = pl.BlockSpec((SSM_L, SSM_GPS * SSM_N), lambda g, c: (cc(c), xs_blocks + SSM_G // SSM_GPS + g))
    sc = pl.BlockSpec((SSM_GPS, SSM_L, LANES), lambda g, c: (g, cc(c), 0))
    sct = pl.BlockSpec((SSM_GPS, LANES, SSM_L), lambda g, c: (g, 0, cc(c)))
    gsp = pl.BlockSpec((SSM_GPS, 1, LANES), lambda g, c: (g, 0, 0))
    st = pl.BlockSpec((None, SSM_GPS, SSM_HP, SSM_N), lambda g, c: (cc(c), g, 0, 0))
    return xs, bsp, csp, sc, sct, gsp, st


def _interleave(gens):
    live = list(gens)
    while live:
        for g in list(live):
            try:
                next(g)
            except StopIteration:
                live.remove(g)


def _rounds(gens):
    live = list(gens)
    while live:
        for g in list(live):
            try:
                next(g)
            except StopIteration:
                live.remove(g)
        yield


def _ssd_group_views(gg, xs_ref, b_ref, c_ref, *per_group):
    return (xs_ref.at[:, gg * SSM_HP:(gg + 1) * SSM_HP], b_ref.at[:, gg * SSM_N:(gg + 1) * SSM_N],
            c_ref.at[:, gg * SSM_N:(gg + 1) * SSM_N]) + tuple(r.at[gg] for r in per_group)


def _ssd_fwd(xbc, dt_g, acs_g, acst_g, d_g):
    S = xbc.shape[0]
    nc = S // SSM_L
    xs_s, b_s, c_s, sc, sct, gsp, st = _ssd_specs(False, nc)

    def body(xs_ref, b_ref, c_ref, dt_ref, acs_ref, acst_ref, d_ref, y_ref, st_ref, state):
        c = pl.program_id(1)

        @pl.when(c == 0)
        def _():
            state[...] = jnp.zeros_like(state)

        tril = lax.broadcasted_iota(jnp.int32, (SSM_L, SSM_L), 0) >= lax.broadcasted_iota(jnp.int32, (SSM_L, SSM_L), 1)
        def group(gg):
            xs_v, b_v, c_v, dt_v, acs_v, acst_v, d_v, st_v, state_v = _ssd_group_views(
                gg, xs_ref, b_ref, c_ref, dt_ref, acs_ref, acst_ref, d_ref, st_ref, state)
            y_v = y_ref.at[:, gg * SSM_HP:(gg + 1) * SSM_HP]
            Bb, Cb = b_v[...].astype(BF16), c_v[...].astype(BF16)
            Gm = lax.dot_general(Cb, Bb, NT, preferred_element_type=F32)
            yield
            xs, X, e_all, decay_all, d_all, cd_rows, _, _, _ = _ssd_group_terms(xs_v, dt_v, acs_v, d_v)
            S_all = state_v[...]
            st_v[...] = S_all
            yield
            yo = lax.dot_general(Cb, S_all.astype(BF16), NT, preferred_element_type=F32)
            new_state = lax.dot_general((X * decay_all).astype(BF16), Bb, TN, preferred_element_type=F32)
            yield
            y_v[...] = e_all * yo + d_all * xs
            state_v[...] = S_all * cd_rows + new_state

            def head(r):
                sl = slice(r * SSM_P, (r + 1) * SSM_P)
                M = Gm * _ssd_lmat(acs_v, acst_v, r, tril)
                yield
                y_v[:, sl] += jnp.dot(M.astype(BF16), X[:, sl].astype(BF16), preferred_element_type=F32)

            yield from _rounds([head(r) for r in range(SSM_R)])

        _interleave([group(gg) for gg in range(SSM_GPS)])

    return _call(body, name="ssd_fwd",
                 out_shape=(_sds((S, SSM_G * SSM_HP)), _sds((nc, SSM_G, SSM_HP, SSM_N))),
                 grid=(SSM_G // SSM_GPS, nc), in_specs=[xs_s, b_s, c_s, sc, sc, sct, gsp],
                 out_specs=(xs_s, pl.BlockSpec((None, SSM_GPS, SSM_HP, SSM_N), lambda g, c: (c, g, 0, 0))),
                 scratch=[pltpu.VMEM((SSM_GPS, SSM_HP, SSM_N), F32)])(xbc, xbc, xbc, dt_g, acs_g, acst_g, d_g)


def _ssd_bwd(xbc, dt_g, acs_g, acst_g, d_g, states, dy):
    S = xbc.shape[0]
    nc = S // SSM_L
    xs_s, b_s, c_s, sc, sct, gsp, st = _ssd_specs(True, nc)
    bc_out = pl.BlockSpec((SSM_L, SSM_GPS * SSM_N), lambda g, c: (nc - 1 - c, g))

    def body(xs_ref, b_ref, c_ref, dt_ref, acs_ref, acst_ref, d_ref, st_ref, dy_ref,
             dxs_ref, db_ref, dc_ref, ddt_ref, dacs_ref, dacst_ref, dd_ref, dstate):
        c = pl.program_id(1)

        @pl.when(c == 0)
        def _():
            dstate[...] = jnp.zeros_like(dstate)
            dd_ref[...] = jnp.zeros_like(dd_ref)

        tril = lax.broadcasted_iota(jnp.int32, (SSM_L, SSM_L), 0) >= lax.broadcasted_iota(jnp.int32, (SSM_L, SSM_L), 1)
        lane = lax.broadcasted_iota(jnp.int32, (1, LANES), 1)
        subl = lax.broadcasted_iota(jnp.int32, (LANES, 1), 0)
        last_row = lax.broadcasted_iota(jnp.int32, (SSM_L, 1), 0) == SSM_L - 1
        triu = lax.broadcasted_iota(jnp.int32, (SSM_L, SSM_L), 0) <= lax.broadcasted_iota(jnp.int32, (SSM_L, SSM_L), 1)
        def group(gg):
            xs_v, b_v, c_v, dt_v, acs_v, acst_v, d_v, st_v, ddt_v, dacs_v, dacst_v, dd_v, dstate_v = _ssd_group_views(
                gg, xs_ref, b_ref, c_ref, dt_ref, acs_ref, acst_ref, d_ref, st_ref, ddt_ref, dacs_ref, dacst_ref, dd_ref, dstate)
            dy_v, dxs_v = (r.at[:, gg * SSM_HP:(gg + 1) * SSM_HP] for r in (dy_ref, dxs_ref))
            db_v, dc_v = (r.at[:, gg * SSM_N:(gg + 1) * SSM_N] for r in (db_ref, dc_ref))
            Bb, Cb = b_v[...].astype(BF16), c_v[...].astype(BF16)
            Gm = lax.dot_general(Cb, Bb, NT, preferred_element_type=F32)
            GmT = lax.dot_general(Bb, Cb, NT, preferred_element_type=F32)
            yield
            xs, X, e_all, decay_all, d_all, cd_rows, dt_c, decay_c, cd = _ssd_group_terms(xs_v, dt_v, acs_v, d_v)
            S_all, dSn_all, dY = st_v[...], dstate_v[...], dy_v[...]
            Sb, dSnb = S_all.astype(BF16), dSn_all.astype(BF16)
            yield
            T = lax.dot_general(Cb, Sb, NT, preferred_element_type=F32)
            dT = (dY * e_all).astype(BF16)
            dC = jnp.dot(dT, Sb, preferred_element_type=F32)
            dS_prev = lax.dot_general(dT, Cb, TN, preferred_element_type=F32)
            yield
            yo_dy = dY * (e_all * T)
            W = lax.dot_general(Bb, dSnb, NT, preferred_element_type=F32)
            dB = jnp.dot((X * decay_all).astype(BF16), dSnb, preferred_element_type=F32)
            yield
            xw = X * W
            dcd_rows = jnp.sum(dSn_all * S_all, axis=1, keepdims=True)
            dstate_v[...] = dS_prev + dSn_all * cd_rows
            dX_state = W * decay_all
            yield
            acc = dict(dG=jnp.zeros((SSM_L, SSM_L), F32), dGT=jnp.zeros((SSM_L, SSM_L), F32),
                       ddt=jnp.zeros((SSM_L, LANES), F32), dacs=jnp.zeros((SSM_L, LANES), F32),
                       dacst=jnp.zeros((LANES, SSM_L), F32), dd=jnp.zeros((1, LANES), F32))

            def head(r):
                sl = slice(r * SSM_P, (r + 1) * SSM_P)
                Lm = _ssd_lmat(acs_v, acst_v, r, tril)
                LmT = jnp.exp(jnp.where(triu, acst_v[r:r + 1, :] - acs_v[:, r:r + 1], -jnp.inf))
                M = Gm * Lm
                yield
                dYh, xs_h = dY[:, sl], xs[:, sl]
                dYb, Xb = dYh.astype(BF16), X[:, sl].astype(BF16)
                dM = lax.dot_general(dYb, Xb, NT, preferred_element_type=F32)
                yield
                dX = jnp.dot((GmT * LmT).astype(BF16), dYb, preferred_element_type=F32) + dX_state[:, sl]
                acc["dG"] = acc["dG"] + dM * Lm
                acc["dGT"] = acc["dGT"] + lax.dot_general(Xb, dYb, NT, preferred_element_type=F32) * LmT
                yield
                dseg = dM * M
                dd = jnp.sum(xw[:, sl], axis=1, keepdims=True) * decay_c[r]
                dcd = jnp.sum(dcd_rows[sl])
                dacs_col = (jnp.sum(dseg, axis=1, keepdims=True) + jnp.sum(yo_dy[:, sl], axis=1, keepdims=True) - dd
                            + jnp.where(last_row, dcd * cd[r] + jnp.sum(dd), 0.0))
                dacs_row = -jnp.sum(dseg, axis=0, keepdims=True)
                yield
                dxs_v[:, sl] = dX * dt_c[r] + d_all[:, sl] * dYh
                acc["ddt"] = acc["ddt"] + jnp.where(lane == r, jnp.sum(dX * xs_h, axis=1, keepdims=True), 0.0)
                acc["dacs"] = acc["dacs"] + jnp.where(lane == r, dacs_col, 0.0)
                acc["dacst"] = acc["dacst"] + jnp.where(subl == r, dacs_row, 0.0)
                acc["dd"] = acc["dd"] + jnp.where(lane == r, jnp.sum(dYh * xs_h), 0.0)

            yield from _rounds([head(r) for r in range(SSM_R)])
            dc_v[...] = dC + jnp.dot(acc["dG"].astype(BF16), Bb, preferred_element_type=F32)
            db_v[...] = dB + jnp.dot(acc["dGT"].astype(BF16), Cb, preferred_element_type=F32)
            ddt_v[...] = acc["ddt"]
            dacs_v[...] = acc["dacs"]
            dacst_v[...] = acc["dacst"]
            dd_v[...] += acc["dd"]

        _interleave([group(gg) for gg in range(SSM_GPS)])

    big = _sds((SSM_G, S, LANES))
    return _call(body, name="ssd_bwd",
                 out_shape=(_sds((S, SSM_G * SSM_HP)), _sds((S, SSM_G * SSM_N)), _sds((S, SSM_G * SSM_N)),
                            big, big, _sds((SSM_G, LANES, S)), _sds((SSM_G, 1, LANES))),
                 grid=(SSM_G // SSM_GPS, nc), in_specs=[xs_s, b_s, c_s, sc, sc, sct, gsp, st, xs_s],
                 out_specs=(xs_s, bc_out, bc_out, sc, sc, sct, gsp),
                 scratch=[pltpu.VMEM((SSM_GPS, SSM_HP, SSM_N), F32)])(xbc, xbc, xbc, dt_g, acs_g, acst_g, d_g, states, dy)


def _gate_norm_fwd(y, proj, w):
    S, DI = y.shape
    tm = _tile(S, 256)

    def body(y_ref, z_ref, w_ref, o_ref):
        z = z_ref[...]
        gn = y_ref[...] * (z * _sigmoid(z))
        r = lax.rsqrt(jnp.mean(gn * gn, axis=-1, keepdims=True) + SSM_NORM_EPS)
        o_ref[...] = (gn * r * w_ref[...]).astype(BF16)

    row = pl.BlockSpec((tm, DI), lambda i: (i, 0))
    return _call(body, name="gate_norm_fwd", out_shape=_sds((S, DI), BF16), grid=(S // tm,),
                 in_specs=[row, row, pl.BlockSpec((1, DI), lambda i: (0, 0))], out_specs=row)(y, proj, w)


def _gate_norm_bwd(y, proj, w, dout):
    S, DI = y.shape
    tm = _tile(S, 256)

    def body(y_ref, z_ref, w_ref, d_ref, dy_ref, dz_ref, dw_ref):
        z, yv = z_ref[...], y_ref[...]
        sz = z * _sigmoid(z)
        dgn, dw = _norm_bwd_math(yv * sz, w_ref[...], d_ref[...].astype(F32), SSM_NORM_EPS)
        dy_ref[...] = dgn * sz
        dz_ref[...] = (dgn * yv * _silu_grad(z)).astype(BF16)

        @pl.when(pl.program_id(0) == 0)
        def _():
            dw_ref[...] = jnp.zeros_like(dw_ref)

        dw_ref[...] += dw

    row = pl.BlockSpec((tm, DI), lambda i: (i, 0))
    vec = pl.BlockSpec((1, DI), lambda i: (0, 0))
    return _call(body, name="gate_norm_bwd", out_shape=(_sds((S, DI)), _sds((S, DI), BF16), _sds((1, DI))), grid=(S // tm,),
                 in_specs=[row, row, vec, row], out_specs=(row, row, vec))(y, proj, w, dout)


def _group_major(v):
    return jnp.pad(v.reshape(SSM_G, 1, SSM_R), ((0, 0), (0, 0), (0, LANES - SSM_R)))


def _ungroup(t):
    return t[:, :SSM_R].reshape(1, SSM_G * SSM_R)


def _ffn_fwd(x, P, l, need):
    need(f"ffn{l}_up", x)
    h = _rmsnorm(x, P["norm_ffn"][l:l + 1], name=f"ffn{l}_norm")
    wT = P[f"ffn_w_upT{l}"]
    F = wT.shape[0] // 2
    a = _mm(h, wT, tb=True, name=f"ffn{l}_up")
    need(f"ffn{l}_down", a)
    act = _ffn_mid_fwd(a, P["ffn_conv_w"], P["ffn_conv_b"], l)
    out = _mm(act, P[f"ffn_w_down{l}"], res=x, name=f"ffn{l}_down")
    return out, (x, h, a, act)


def _ffn_bwd(saved, P, l, dx, emit):
    x, h, a, act = saved
    wT = P[f"ffn_w_upT{l}"]
    F = wT.shape[0] // 2
    dact = _mm(dx, P[f"ffn_w_down{l}"], tb=True, out_dtype=BF16, name=f"ffn{l}_down_dx")
    dw_down = _mm(act, dx, ta=True, out_dtype=PAYLOAD, name=f"ffn{l}_down_dw")
    dau, dag, dcw, dcb = _ffn_mid_bwd(a, P["ffn_conv_w"], P["ffn_conv_b"], l, dact)
    dw_upT = _mm(dau, h, ta=True, out_dtype=PAYLOAD, out_rows=(2 * F, 0, None), name=f"ffn{l}_up_u_dw")
    dw_upT = _mm(dag, h, ta=True, out_dtype=PAYLOAD, out_rows=(2 * F, F, dw_upT), name=f"ffn{l}_up_g_dw")
    tie = emit(f"ffn{l}", {"ffn_w_upT": dw_upT, "ffn_w_down": dw_down})
    dh = _mm(dau, wT, b_rows=(0, F), name=f"ffn{l}_up_u_dx")
    dx_in, dnw = _mm_norm_bwd(dag, wT, x, P["norm_ffn"][l:l + 1], dx, res=dh, b_rows=(F, F), after=tie,
                              name=f"ffn{l}_up_g_dx_norm_bwd")
    return dx_in, dnw, dcw, dcb


def _local_step(x, positions, target, P, need, emit, after=None):
    S, D = x.shape
    inv_freq = ROPE_THETA ** (-jnp.arange(0, HEAD_DIM, 2, dtype=F32) / HEAD_DIM)
    inv_freq = jnp.tile(inv_freq, LANES // (HEAD_DIM // 2)).reshape(1, LANES)
    cos, sin = _rope_tables(positions, inv_freq)

    nm0 = P["norm_mix"][0:1]
    h0 = _rmsnorm(x, nm0, name="mix_norm", after=after)
    need("mix_in", h0)
    proj0 = _mm(h0, P["mix_w_inT"], tb=True, name="mix_in")
    cat0 = _attn_fwd(proj0, cos, sin, P["attn_sinks"], _pool_fwd(proj0, P["pool_w"][0], P["pool_scale"]))
    need("mix_out", cat0)
    x1 = _mm(cat0, P["mix_w_out"], res=x, name="mix_out")
    x2, ffn0 = _ffn_fwd(x1, P, 0, need)

    nm1 = P["norm_mix"][1:2]
    need("ssm", x2)
    h1 = _rmsnorm(x2, nm1, name="ssm_norm_in")
    w1T, wdtT = P["ssm_w_inT"], P["ssm_wdtT"]
    DI, CD, NH = P["ssm_norm"].shape[1], P["ssm_conv_w"].shape[1], P["ssm_dt_bias"].shape[1]
    z = _mm(h1, w1T, tb=True, b_rows=(0, DI), name="ssm_in_z")
    xbcp = _mm(h1, w1T, tb=True, b_rows=(DI, CD), name="ssm_in_xbc")
    dtraw = _mm(h1, wdtT, tb=True, name="ssm_in_dt")
    xbc = _conv_silu_fwd(xbcp, P["ssm_conv_w"], P["ssm_conv_b"])
    bias_g, alog_g, d_g = _group_major(P["ssm_dt_bias"]), _group_major(P["ssm_A_log"]), _group_major(P["ssm_D"])
    pre_g, dt_g, acs_g, acst_g = _ssd_prep_fwd(dtraw, 0, bias_g, alog_g)
    y, states = _ssd_fwd(xbc, dt_g, acs_g, acst_g, d_g)
    yn = _gate_norm_fwd(y, z, P["ssm_norm"])
    need("ssm_out", yn)
    x3 = _mm(yn, P["ssm_w_out"], res=x2, name="ssm_out")
    x4, ffn1 = _ffn_fwd(x3, P, 1, need)

    loss, dx, d_norm_final = _final_loss(x4, P["norm_final"].reshape(1, D), target, name="final_loss")
    dx, dnf1, dcw1, dcb1 = _ffn_bwd(ffn1, P, 1, dx, emit)
    dyn = _mm(dx, P["ssm_w_out"], tb=True, out_dtype=BF16, name="ssm_out_dx")
    d_w_out1 = _mm(yn, dx, ta=True, out_dtype=PAYLOAD, name="ssm_out_dw")
    dy, dz, d_ssm_norm = _gate_norm_bwd(y, z, P["ssm_norm"], dyn)
    dxs, dB, dC, ddt_g, dacs_g, dacst_g, dd_g = _ssd_bwd(xbc, dt_g, acs_g, acst_g, d_g, states, dy)
    draw, dbias_g, dalog_g = _ssd_prep_bwd(pre_g, dt_g, alog_g, ddt_g, dacs_g, dacst_g)
    dxbc, d_conv_w1, d_conv_b1 = _conv_silu_bwd(xbcp, P["ssm_conv_w"], P["ssm_conv_b"], [dxs, dB, dC])
    rows = DI + CD + NH
    d_w1T = _mm(dz, h1, ta=True, out_dtype=PAYLOAD, out_rows=(rows, 0, None), name="ssm_in_z_dw")
    d_w1T = _mm(dxbc, h1, ta=True, out_dtype=PAYLOAD, out_rows=(rows, DI, d_w1T), name="ssm_in_xbc_dw")
    d_w1T = _mm(draw[:, :NH], h1, ta=True, out_dtype=PAYLOAD, out_rows=(rows, DI + CD, d_w1T), name="ssm_in_dt_dw")
    tie = emit("ssm", {"ssm_w_inT": d_w1T, "ssm_w_out": d_w_out1,
                       "ssm_conv_w": d_conv_w1, "ssm_conv_b": d_conv_b1, "ssm_norm": d_ssm_norm})
    dh1 = _mm(dz, w1T, b_rows=(0, DI), name="ssm_in_z_dx")
    dh1 = _mm(dxbc, w1T, b_rows=(DI, CD), res=dh1, name="ssm_in_xbc_dx")
    dx, dnm1 = _mm_norm_bwd(draw, wdtT, x2, nm1, dx, res=dh1, after=tie, name="ssm_in_dt_dx_norm_bwd")
    dx, dnf0, dcw0, dcb0 = _ffn_bwd(ffn0, P, 0, dx, emit)
    dcat = _mm(dx, P["mix_w_out"], tb=True, name="mix_out_dx")
    d_w_out0 = _mm(cat0, dx, ta=True, out_dtype=PAYLOAD, name="mix_out_dw")
    dproj0, dsk = _attn_bwd(proj0, cos, sin, P["attn_sinks"], dcat)
    dproj0, d_pool_w, d_pool_scale = _pool_bwd(proj0, P["pool_w"][0], P["pool_scale"], dcat, dproj0)
    d_w_in0 = _mm(dproj0, h0, ta=True, out_dtype=PAYLOAD, name="mix_in_dw")
    tie = emit("mix", {"mix_w_inT": d_w_in0, "mix_w_out": d_w_out0, "ffn_conv_w": jnp.stack([dcw0, dcw1])})
    grad_x, dnm0 = _mm_norm_bwd(dproj0, P["mix_w_inT"], x, nm0, dx, after=tie, name="mix_in_dx_norm_bwd")

    small = {
        "norm_mix": jnp.concatenate([dnm0, dnm1], axis=0),
        "norm_ffn": jnp.concatenate([dnf0, dnf1], axis=0),
        "norm_final": d_norm_final,
        "pool_w": d_pool_w,
        "pool_scale": d_pool_scale,
        "attn_sinks_rows": dsk,
        "ssm_dt_bias_g": dbias_g, "ssm_A_log_g": dalog_g, "ssm_D_g": dd_g,
        "ffn_conv_b": jnp.concatenate([dcb0, dcb1], axis=0),
    }
    return loss, grad_x, small


def _peer(k):
    x, y, c = lax.axis_index("x"), lax.axis_index("y"), lax.axis_index("c")
    px = 1 - x if k & 4 else x
    py = 1 - y if k & 2 else y
    pc = 1 - c if k & 1 else c
    return (px, py, pc), 4 * px + 2 * py + pc


def _my_index():
    return 4 * lax.axis_index("x") + 2 * lax.axis_index("y") + lax.axis_index("c")


def _land_sds(a, mode, gather):
    if mode == "slab":
        return _sds(((N_DEV,) + a.shape) if gather else a.shape, a.dtype)
    assert mode == "rows", mode
    return _sds((N_DEV * a.shape[0],) + a.shape[1:] if gather else (N_DEV, a.shape[0] // N_DEV) + a.shape[1:], a.dtype)


def _part(ref, mode, shape, idx):
    if mode == "slab":
        return ref.at[idx]
    r = shape[0] // N_DEV
    return ref.at[pl.ds(idx * r, r)]


def _remote_copies(ops, gather, srcs, lands, send_sems, recv_sems):
    me = _my_index()
    n = len(ops)
    out = []
    for k in range(1, N_DEV):
        dev, idx = _peer(k)
        for i, (a, mode) in enumerate(ops):
            s = srcs[i] if gather else _part(srcs[i], mode, a.shape, idx)
            d = _part(lands[i], mode, _land_sds(a, mode, gather).shape, me) if gather else lands[i].at[me]
            out.append(pltpu.make_async_remote_copy(src_ref=s, dst_ref=d, send_sem=send_sems.at[(k - 1) * n + i],
                                                    recv_sem=recv_sems.at[(k - 1) * n + i], device_id=dev,
                                                    device_id_type=pl.DeviceIdType.MESH))
    return out


HBM = pl.BlockSpec(memory_space=pltpu.HBM)
SEM = pl.BlockSpec(memory_space=pltpu.SEMAPHORE)
SIDE_EFFECT = pltpu.SideEffectType.DATAFLOW_SIDE_EFFECTING


def _in_hbm(a):
    return pltpu.with_memory_space_constraint(a, pltpu.HBM)


def _place_own(ops, *, gather, name):
    n = len(ops)

    def zeros(k):
        return (0,) * k

    in_specs, out_specs = [], []
    for a, mode in ops:
        nd = a.ndim
        if gather and mode == "slab":
            in_specs.append(pl.BlockSpec(a.shape, lambda i, nd=nd: zeros(nd)))
            out_specs.append(pl.BlockSpec((1,) + a.shape, lambda i, nd=nd: (_my_index(),) + zeros(nd)))
        elif gather:
            in_specs.append(pl.BlockSpec(a.shape, lambda i, nd=nd: zeros(nd)))
            out_specs.append(pl.BlockSpec(a.shape, lambda i, nd=nd: (_my_index(),) + zeros(nd - 1)))
        elif mode == "slab":
            in_specs.append(pl.BlockSpec((1,) + a.shape[1:], lambda i, nd=nd: (_my_index(),) + zeros(nd - 1)))
            out_specs.append(pl.BlockSpec((1,) + a.shape[1:], lambda i, nd=nd: (_my_index(),) + zeros(nd - 1)))
        else:
            r = a.shape[0] // N_DEV
            in_specs.append(pl.BlockSpec((r,) + a.shape[1:], lambda i, nd=nd: (_my_index(),) + zeros(nd - 1)))
            out_specs.append(pl.BlockSpec((1, r) + a.shape[1:], lambda i, nd=nd: (_my_index(),) + zeros(nd)))

    def body(*refs):
        for i_ref, o_ref in zip(refs[:n], refs[n:2 * n]):
            if o_ref.shape == i_ref.shape:
                o_ref[...] = i_ref[...]
            else:
                o_ref[0] = i_ref[...]

    outs = _call(body, name=name, grid=(1,), in_specs=in_specs, out_specs=out_specs + [ANY] * n,
                 out_shape=[_land_sds(a, m, gather) for a, m in ops] + [_sds(a.shape, a.dtype) for a, _ in ops],
                 aliases={i: n + i for i in range(n)})(*[a for a, _ in ops])
    return outs[:n], [(src, m) for src, (_, m) in zip(outs[n:], ops)]


def _exchange_start(groups, *, gather, name):
    sizes = [len(ops) for ops, _ in groups]
    n = sum(sizes)
    G = len(groups)

    def body(*refs):
        srcs, lands = refs[:n], refs[n:2 * n]
        sems = refs[2 * n:2 * n + 2 * G]
        token = refs[-1]
        off = 0
        for g, (ops, _) in enumerate(groups):
            for cp in _remote_copies(ops, gather, srcs[off:off + sizes[g]], lands[off:off + sizes[g]], sems[2 * g], sems[2 * g + 1]):
                cp.start()
            off += sizes[g]
        token[...] = jnp.zeros_like(token)

    srcs = [a for ops, _ in groups for a, _ in ops]
    lands = [l for _, ls in groups for l in ls]
    sem_shapes = [pltpu.SemaphoreType.DMA((s * (N_DEV - 1),)) for s in sizes for _ in range(2)]
    outs = pl.pallas_call(
        body, name=name,
        out_shape=sem_shapes + [pltpu.HBM(a.shape, a.dtype) for a in srcs + lands] + [_sds((8, LANES))],
        in_specs=[HBM] * (2 * n), out_specs=[SEM] * (2 * G) + [HBM] * (2 * n) + [pl.BlockSpec(memory_space=pltpu.VMEM)],
        input_output_aliases={i: 2 * G + i for i in range(2 * n)},
        compiler_params=pltpu.CompilerParams(has_side_effects=SIDE_EFFECT))(*[_in_hbm(a) for a in srcs + lands])
    sems, thru, token = outs[:2 * G], outs[2 * G:2 * G + 2 * n], outs[-1]
    states, off = [], 0
    for g, s in enumerate(sizes):
        states.append((sems[2 * g], sems[2 * g + 1], thru[off:off + s], thru[n + off:n + off + s]))
        off += s
    return states, token


def _exchange_wait(ops, state, after, *, gather, name):
    send_sems, recv_sems, srcs, lands = state
    n = len(ops)

    def body(*refs):
        for cp in _remote_copies(ops, gather, refs[:n], refs[n:2 * n], refs[2 * n], refs[2 * n + 1]):
            cp.wait_send()
            cp.wait_recv()

    outs = pl.pallas_call(
        body, name=name, out_shape=[pltpu.HBM(a.shape, a.dtype) for a in list(srcs) + list(lands)],
        in_specs=[HBM] * (2 * n) + [SEM, SEM, ANY], out_specs=[HBM] * (2 * n),
        input_output_aliases={i: i for i in range(2 * n)},
        compiler_params=pltpu.CompilerParams(has_side_effects=SIDE_EFFECT))(*srcs, *lands, send_sems, recv_sems, after)
    return outs[n:]


ADAM_ROWS = 256


def _row_tile(R, cap=ADAM_ROWS):
    best = R
    if R > cap:
        for d in range(16, cap + 1, 16):
            if R % d == 0:
                best = d
    return best


def _adamw(g_layers, w, m, v, *, name):
    L = len(g_layers)
    J, R, Wd = g_layers[0].shape
    assert w.shape == (L, R, Wd), (g_layers[0].shape, w.shape)
    tr = _row_tile(R)
    nrt = R // tr
    c1 = 1.0 / (1.0 - ADAM_B1 ** ADAM_STEP)
    c2 = 1.0 / (1.0 - ADAM_B2 ** ADAM_STEP)

    def body(*refs):
        g_refs = refs[:L]
        w_ref, m_ref, v_ref, go_ref, d_ref, mo_ref, vo_ref = refs[L:]
        layer = pl.program_id(0)
        g = None
        for l, g_ref in enumerate(g_refs):
            gl = g_ref[0].astype(F32)
            for j in range(1, J):
                gl = gl + g_ref[j].astype(F32)
            g = gl if g is None else jnp.where(layer == l, gl, g)
        mn = ADAM_B1 * m_ref[...] + (1.0 - ADAM_B1) * g
        vn = ADAM_B2 * v_ref[...] + (1.0 - ADAM_B2) * (g * g)
        go_ref[...] = g
        mo_ref[...] = mn
        vo_ref[...] = vn
        d_ref[...] = -ADAM_LR * ((mn * c1) / (jnp.sqrt(vn * c2) + ADAM_EPS) + ADAM_WD * w_ref[...])

    def g_spec(l):
        return pl.BlockSpec((J, tr, Wd), lambda ll, i: (0, jnp.where(ll == l, i, jnp.where(ll < l, 0, nrt - 1)), 0))

    row = pl.BlockSpec((None, tr, Wd), lambda ll, i: (ll, i, 0))
    out = _sds((L, R, Wd))
    return _call(body, name=name, out_shape=(out, out, out, out), grid=(L, nrt),
                 in_specs=[g_spec(l) for l in range(L)] + [row, row, row], out_specs=(row, row, row, row))(*g_layers, w, m, v)


def _sum_slabs(slabs, *, name):
    n = len(slabs)

    def body(*refs):
        for g_ref, o_ref in zip(refs[:n], refs[n:]):
            g = g_ref[0]
            for j in range(1, g_ref.shape[0]):
                g = g + g_ref[j]
            o_ref[...] = g

    return _call(body, name=name, out_shape=[_sds(s.shape[1:]) for s in slabs])(*slabs)


def kernel(x, positions, norm_mix, norm_ffn, norm_final, mix_w_in, pool_w, pool_scale, attn_sinks, mix_w_out, ssm_w_in, ssm_conv_w, ssm_conv_b, ssm_dt_bias, ssm_A_log, ssm_D, ssm_norm, ssm_w_out, ffn_w_up, ffn_conv_w, ffn_conv_b, ffn_w_down, loss_target, m_norm_mix, m_norm_ffn, m_norm_final, m_mix_w_in, m_pool_w, m_pool_scale, m_attn_sinks, m_mix_w_out, m_ssm_w_in, m_ssm_conv_w, m_ssm_conv_b, m_ssm_dt_bias, m_ssm_A_log, m_ssm_D, m_ssm_norm, m_ssm_w_out, m_ffn_w_up, m_ffn_conv_w, m_ffn_conv_b, m_ffn_w_down, v_norm_mix, v_norm_ffn, v_norm_final, v_mix_w_in, v_pool_w, v_pool_scale, v_attn_sinks, v_mix_w_out, v_ssm_w_in, v_ssm_conv_w, v_ssm_conv_b, v_ssm_dt_bias, v_ssm_A_log, v_ssm_D, v_ssm_norm, v_ssm_w_out, v_ffn_w_up, v_ffn_conv_w, v_ffn_conv_b, v_ffn_w_down):
    args = dict(locals())
    wl = {n: args[n] for n in WEIGHTS}
    ml = {n: args["m_" + n] for n in WEIGHTS}
    vl = {n: args["v_" + n] for n in WEIGHTS}
    F = ffn_w_down.shape[1] * N_DEV
    DI, CD, NH = ssm_norm.shape[1] * N_DEV, ssm_conv_b.shape[1] * N_DEV, ssm_dt_bias.shape[1]
    Kc, Kf = ssm_conv_w.shape[1], ffn_conv_w.shape[1]
    n_up = ffn_w_up.shape[2]
    col_sharded = ("mix_w_in", "ssm_w_in", "ffn_w_up")

    def tr(a):
        return jnp.swapaxes(a, -1, -2)

    def two(a):
        return a.reshape(-1, a.shape[-1])

    def pay(a):
        return a.astype(PAYLOAD)

    order = ("mix_in", "mix_out", "ffn0_up", "ffn0_down", "ssm", "ssm_out", "ffn1_up", "ffn1_down")
    gops = {
        "mix_in": [(pay(tr(mix_w_in)[0]), "rows")],
        "mix_out": [(pay(mix_w_out[0]), "rows"), (two(ssm_conv_w), "slab"), (ssm_conv_b, "slab"), (ssm_norm, "slab"),
                    (two(ffn_conv_w), "slab")],
        "ffn0_up": [(pay(tr(ffn_w_up)[0]), "rows")], "ffn0_down": [(pay(ffn_w_down[0]), "rows")],
        "ssm": [(pay(tr(ssm_w_in)[0]), "slab")], "ssm_out": [(pay(ssm_w_out[0]), "rows")],
        "ffn1_up": [(pay(tr(ffn_w_up)[1]), "rows")], "ffn1_down": [(pay(ffn_w_down[1]), "rows")],
    }
    lands, handed = _place_own([op for g in order for op in gops[g]], gather=True, name="gather_own")
    groups, off = [], 0
    for g in order:
        gops[g] = handed[off:off + len(gops[g])]
        groups.append((gops[g], lands[off:off + len(gops[g])]))
        off += len(gops[g])
    gstates, token = _exchange_start(groups, gather=True, name="gather_start")
    gstate = dict(zip(order, gstates))
    P = {n: wl[n] for n in REPLICATED}

    def need(g, after):
        got = _exchange_wait(gops[g], gstate[g], after, gather=True, name="gather_wait_" + g)
        if g == "mix_in":
            P["mix_w_inT"] = got[0]
        elif g == "mix_out":
            P.update(mix_w_out=got[0], ssm_conv_w=got[1].transpose(1, 0, 2).reshape(Kc, CD), ssm_conv_b=got[2].reshape(1, CD),
                     ssm_norm=got[3].reshape(1, DI), ffn_conv_w=got[4].transpose(1, 0, 2).reshape(2, Kf, 2 * F))
        elif g == "ssm":
            w1T = got[0].reshape(-1, got[0].shape[-1])
            P.update(ssm_w_inT=w1T, ssm_wdtT=jnp.pad(w1T[DI + CD:], ((0, LANES - NH), (0, 0))))
        elif g == "ssm_out":
            P["ssm_w_out"] = got[0]
        elif g.endswith("_up"):
            P["ffn_w_upT" + g[3]] = got[0]
        else:
            P["ffn_w_down" + g[3]] = got[0]

    sent = {}

    def emit(g, d):
        if g == "mix":
            ops = [(d["mix_w_inT"], "rows"), (d["mix_w_out"], "rows"),
                   (d["ffn_conv_w"].reshape(2 * Kf, N_DEV, n_up).transpose(1, 0, 2), "slab")]
        elif g == "ssm":
            ops = [(d["ssm_w_inT"].reshape(N_DEV, -1, d["ssm_w_inT"].shape[-1]), "slab"), (d["ssm_w_out"], "rows"),
                   (d["ssm_conv_w"].reshape(Kc, N_DEV, -1).transpose(1, 0, 2), "slab"),
                   (d["ssm_conv_b"].reshape(N_DEV, 1, -1), "slab"), (d["ssm_norm"].reshape(N_DEV, 1, -1), "slab")]
        else:
            ops = [(d["ffn_w_upT"], "rows"), (d["ffn_w_down"], "rows")]
        own, ops = _place_own(ops, gather=False, name="scatter_own_" + g)
        (state,), tok = _exchange_start([(ops, own)], gather=False, name="scatter_start_" + g)
        sent[g] = (ops, state)
        return tok

    loss_lanes, grad_x, G = _local_step(x[0], positions.reshape(-1, 1), loss_target[0], P, need, emit, after=token)
    loss = lax.psum(loss_lanes[0, 0], ("x", "y", "c"))

    res = {}

    def update(n, g_layers):
        L = len(g_layers)
        g_layers = [g.reshape(g.shape[0], -1, g.shape[-1]) for g in g_layers]
        shape = (L,) + g_layers[0].shape[1:]
        view = tr if n in col_sharded else (lambda a: a)
        outs = _adamw(g_layers, view(wl[n]).reshape(shape), view(ml[n]).reshape(shape), view(vl[n]).reshape(shape),
                      name="adamw_" + n)
        for kind, a in zip(("grad", "delta", "new_m", "new_v"), outs):
            res[kind, n] = view(a.reshape(view(wl[n]).shape))

    rep_ops = [(a, "slab") for a in (G["norm_mix"], G["norm_ffn"], G["norm_final"], G["pool_w"].reshape(-1, LANES),
                                     G["pool_scale"], G["ffn_conv_b"], G["attn_sinks_rows"],
                                     G["ssm_dt_bias_g"].reshape(SSM_G, LANES), G["ssm_A_log_g"].reshape(SSM_G, LANES),
                                     G["ssm_D_g"].reshape(SSM_G, LANES))]
    rep_own, rep_ops = _place_own(rep_ops, gather=True, name="small_own")
    (rep_state,), rep_token = _exchange_start([(rep_ops, rep_own)], gather=True, name="small_start")

    recv = {g: _exchange_wait(sent[g][0], sent[g][1], rep_token, gather=False, name="scatter_wait_" + g)
            for g in ("ffn1", "ssm", "ffn0", "mix")}
    update("mix_w_in", [recv["mix"][0]])
    update("mix_w_out", [recv["mix"][1]])
    update("ffn_conv_w", [recv["mix"][2]])
    update("ssm_w_in", [recv["ssm"][0]])
    update("ssm_w_out", [recv["ssm"][1]])
    update("ssm_conv_w", [recv["ssm"][2]])
    update("ssm_conv_b", [recv["ssm"][3]])
    update("ssm_norm", [recv["ssm"][4]])
    update("ffn_w_up", [recv["ffn0"][0], recv["ffn1"][0]])
    update("ffn_w_down", [recv["ffn0"][1], recv["ffn1"][1]])

    rep = _exchange_wait(rep_ops, rep_state, res["new_v", "ffn_w_down"], gather=True, name="small_wait")
    for n, r in zip(("norm_mix", "norm_ffn", "norm_final", "pool_w", "pool_scale", "ffn_conv_b"), rep):
        update(n, [r])
    sinks_rows, bias_g, alog_g, d_g = _sum_slabs(rep[6:], name="sum_head_grads")
    update("attn_sinks", [sinks_rows[:, 0].reshape(1, 1, N_HEADS)])
    update("ssm_dt_bias", [_ungroup(bias_g)[None]])
    update("ssm_A_log", [_ungroup(alog_g)[None]])
    update("ssm_D", [_ungroup(d_g)[None]])

    return (loss, grad_x[None], *[res[k, n] for k in ("grad", "delta", "new_m", "new_v") for n in WEIGHTS])
```

```python
import functools
import math

import jax
import jax.numpy as jnp
from jax import lax
from jax.experimental import pallas as pl
from jax.experimental.pallas import tpu as pltpu

F32 = jnp.float32
BF16 = jnp.bfloat16

N_DEV = 8
LANES = 128
HEAD_DIM = 64
N_KV_HEADS = 2
GQ = 4
N_HEADS = N_KV_HEADS * GQ
BLOCK = 128
POOL_GROUPS = 4
ROPE_THETA = 10000.0
SSM_P = 64
SSM_G = 8
SSM_R = 4
SSM_N = 128
SSM_L = 128
NORM_EPS = 1e-6
SSM_NORM_EPS = 1e-5
ADAM_LR, ADAM_B1, ADAM_B2, ADAM_EPS, ADAM_WD, ADAM_STEP = 0.001, 0.9, 0.999, 1e-08, 0.01, 10
VMEM_LIMIT = 56 * 2 ** 20
PAYLOAD = jnp.bfloat16

REPLICATED = ("norm_mix", "norm_ffn", "norm_final", "pool_w", "pool_scale", "attn_sinks",
              "ssm_dt_bias", "ssm_A_log", "ssm_D", "ffn_conv_b")
WEIGHTS = ("norm_mix", "norm_ffn", "norm_final", "mix_w_in", "pool_w", "pool_scale", "attn_sinks", "mix_w_out",
           "ssm_w_in", "ssm_conv_w", "ssm_conv_b", "ssm_dt_bias", "ssm_A_log", "ssm_D", "ssm_norm", "ssm_w_out",
           "ffn_w_up", "ffn_conv_w", "ffn_conv_b", "ffn_w_down")


def _tile(n, cap):
    if n <= cap:
        return n
    best = None
    for d in range(LANES, cap + 1, LANES):
        if n % d == 0:
            best = d
    assert best is not None, (n, cap)
    return best


def _call(body, *, name, out_shape, grid=None, in_specs=None, out_specs=None, scratch=(), aliases=None):
    kw = {}
    if grid is not None:
        kw = dict(grid=grid, in_specs=in_specs, out_specs=out_specs)
    if aliases:
        kw["input_output_aliases"] = aliases
    return pl.pallas_call(
        body, name=name, out_shape=out_shape, scratch_shapes=list(scratch),
        compiler_params=pltpu.CompilerParams(vmem_limit_bytes=VMEM_LIMIT), **kw)


ANY = pl.BlockSpec(memory_space=pl.ANY)


def _sds(shape, dtype=F32):
    return jax.ShapeDtypeStruct(tuple(shape), dtype)


def _sigmoid(x):
    return 1.0 / (1.0 + jnp.exp(-x))


def _shift_dn(x, d, t):
    if d == 0:
        return x
    return jnp.where(t >= d, pltpu.roll(x, d, axis=0), 0.0)


def _shift_up(x, d, t):
    if d == 0:
        return x
    n = x.shape[0]
    return jnp.where(t < n - d, pltpu.roll(x, n - d, axis=0), 0.0)


def _mm(a, b, *, name, ta=False, tb=False, res=None, out_dtype=F32, b_rows=None, out_rows=None):
    M, K = (a.shape[1], a.shape[0]) if ta else a.shape
    b0, bn = b_rows if b_rows is not None else (0, b.shape[0])
    N = bn if tb else b.shape[1]
    assert (b.shape[1] if tb else bn) == K, (a.shape, b.shape, ta, tb, b_rows)
    tm, tn, tk = _tile(M, 1408), _tile(N, 1408), _tile(K, 1408)
    nk = K // tk
    dims = (((0 if ta else 1,), (1 if tb else 0,)), ((), ()))
    aliased = out_rows is not None and out_rows[2] is not None

    def body(*refs):
        a_ref, b_ref = refs[:2]
        r_ref = refs[2] if res is not None else None
        o_ref, acc = refs[-2:]
        k = pl.program_id(2)

        @pl.when(k == 0)
        def _():
            acc[...] = jnp.zeros_like(acc)

        acc[...] += lax.dot_general(a_ref[...].astype(BF16), b_ref[...].astype(BF16), dims,
                                    preferred_element_type=F32)

        @pl.when(k == nk - 1)
        def _():
            out = acc[...]
            if res is not None:
                out = out + r_ref[...]
            o_ref[...] = out.astype(out_dtype)

    a_spec = pl.BlockSpec((tk, tm), lambda i, j, k: (k, i)) if ta else pl.BlockSpec((tm, tk), lambda i, j, k: (i, k))
    if tb:
        assert b0 % tn == 0, (b_rows, tn)
        b_spec = pl.BlockSpec((tn, tk), lambda i, j, k: (b0 // tn + j, k))
    else:
        assert b0 % tk == 0, (b_rows, tk)
        b_spec = pl.BlockSpec((tk, tn), lambda i, j, k: (b0 // tk + k, j))
    ins, specs = [a, b], [a_spec, b_spec]
    if res is not None:
        ins.append(res)
        specs.append(pl.BlockSpec((tm, tn), lambda i, j, k: (i, j)))
    aliases = None
    if out_rows is None:
        o_spec = pl.BlockSpec((tm, tn), lambda i, j, k: (i, j))
        out_shape = _sds((M, N), out_dtype)
    else:
        total, o0, prev = out_rows
        assert o0 % tm == 0, (out_rows, tm)
        o_spec = pl.BlockSpec((tm, tn), lambda i, j, k: (o0 // tm + i, j))
        out_shape = _sds((total, N), out_dtype)
        if aliased:
            aliases = {len(ins): 0}
            ins.append(prev)
            specs.append(ANY)
    return _call(body, name=name, out_shape=out_shape, grid=(M // tm, N // tn, nk), in_specs=specs,
                 out_specs=o_spec, scratch=[pltpu.VMEM((tm, tn), F32)], aliases=aliases)(*ins)


def _rmsnorm(x, w, *, name, eps=NORM_EPS, after=None):
    S, D = x.shape
    tm = _tile(S, 512)
    tie = [] if after is None else [after]

    def body(x_ref, w_ref, *rest):
        o_ref = rest[-1]
        xf = x_ref[...]
        r = lax.rsqrt(jnp.mean(xf * xf, axis=-1, keepdims=True) + eps)
        o_ref[...] = (xf * r * w_ref[...]).astype(BF16)

    return _call(body, name=name, out_shape=_sds((S, D), BF16), grid=(S // tm,),
                 in_specs=[pl.BlockSpec((tm, D), lambda i: (i, 0)), pl.BlockSpec((1, D), lambda i: (0, 0))] + [ANY] * len(tie),
                 out_specs=pl.BlockSpec((tm, D), lambda i: (i, 0)))(x, w, *tie)


def _norm_bwd_math(xf, w, dh, eps):
    r = lax.rsqrt(jnp.mean(xf * xf, axis=-1, keepdims=True) + eps)
    xhat = xf * r
    dxh = dh * w
    dx = r * (dxh - xhat * jnp.mean(dxh * xhat, axis=-1, keepdims=True))
    dw = jnp.sum(dh * xhat, axis=0, keepdims=True)
    return dx, dw


def _mm_norm_bwd(a, b, x, w, dres, *, name, res=None, b_rows=None, after=None, eps=NORM_EPS):
    M, K = a.shape
    b0, bn = b_rows if b_rows is not None else (0, b.shape[0])
    D = b.shape[1]
    assert bn == K and x.shape == (M, D), (a.shape, b.shape, b_rows, x.shape)
    tm, tk = _tile(M, 512), _tile(K, 1408)
    assert b0 % tk == 0, (b_rows, tk)
    nk = K // tk

    def body(*refs):
        a_ref, b_ref, x_ref, w_ref, dr_ref = refs[:5]
        r_ref = refs[5] if res is not None else None
        dx_ref, dw_ref, acc = refs[-3:]
        i, k = pl.program_id(0), pl.program_id(1)

        @pl.when(k == 0)
        def _():
            acc[...] = jnp.zeros_like(acc)

        @pl.when((i == 0) & (k == 0))
        def _():
            dw_ref[...] = jnp.zeros_like(dw_ref)

        acc[...] += jnp.dot(a_ref[...].astype(BF16), b_ref[...].astype(BF16), preferred_element_type=F32)

        @pl.when(k == nk - 1)
        def _():
            dh = acc[...] if res is None else acc[...] + r_ref[...]
            dx, dw = _norm_bwd_math(x_ref[...], w_ref[...], dh, eps)
            dx_ref[...] = dr_ref[...] + dx
            dw_ref[...] += dw

    row = pl.BlockSpec((tm, D), lambda i, k: (i, 0))
    vec = pl.BlockSpec((1, D), lambda i, k: (0, 0))
    ins = [a, b, x, w, dres]
    specs = [pl.BlockSpec((tm, tk), lambda i, k: (i, k)), pl.BlockSpec((tk, D), lambda i, k: (b0 // tk + k, 0)), row, vec, row]
    if res is not None:
        ins.append(res)
        specs.append(row)
    if after is not None:
        ins.append(after)
        specs.append(ANY)
    return _call(body, name=name, out_shape=(_sds((M, D)), _sds((1, D))), grid=(M // tm, nk), in_specs=specs,
                 out_specs=(row, vec), scratch=[pltpu.VMEM((tm, D), F32)])(*ins)


def _final_loss(x, w, target, *, name):
    S, D = x.shape
    tm = _tile(S, 512)

    def body(x_ref, w_ref, t_ref, loss_ref, dx_ref, dw_ref):
        xf, wv = x_ref[...], w_ref[...]
        r = lax.rsqrt(jnp.mean(xf * xf, axis=-1, keepdims=True) + NORM_EPS)
        err = xf * r * wv - t_ref[...]
        part = 0.5 * jnp.sum(jnp.mean(err * err, axis=-1, keepdims=True), axis=0, keepdims=True)
        dx, dw = _norm_bwd_math(xf, wv, err * (1.0 / D), NORM_EPS)
        dx_ref[...] = dx

        @pl.when(pl.program_id(0) == 0)
        def _():
            dw_ref[...] = jnp.zeros_like(dw_ref)
            loss_ref[...] = jnp.zeros_like(loss_ref)

        dw_ref[...] += dw
        loss_ref[...] += jnp.broadcast_to(part, loss_ref.shape)

    row = pl.BlockSpec((tm, D), lambda i: (i, 0))
    vec = pl.BlockSpec((1, D), lambda i: (0, 0))
    return _call(body, name=name, out_shape=(_sds((1, LANES)), _sds((S, D)), _sds((1, D))), grid=(S // tm,),
                 in_specs=[row, vec, row], out_specs=(pl.BlockSpec((1, LANES), lambda i: (0, 0)), row, vec))(x, w, target)


def _rope_tables(pos, inv_freq):
    S = pos.shape[0]
    tm = _tile(S, 512)

    def body(p_ref, f_ref, c_ref, s_ref):
        ang = p_ref[...].astype(F32) * f_ref[...]
        c_ref[...] = jnp.cos(ang)
        s_ref[...] = jnp.sin(ang)

    blk = pl.BlockSpec((tm, LANES), lambda i: (i, 0))
    return _call(body, name="rope_tables", out_shape=(_sds((S, LANES)), _sds((S, LANES))), grid=(S // tm,),
                 in_specs=[pl.BlockSpec((tm, 1), lambda i: (i, 0)), pl.BlockSpec((1, LANES), lambda i: (0, 0))],
                 out_specs=(blk, blk))(pos, inv_freq)


def _rot_half(t):
    lane = lax.broadcasted_iota(jnp.int32, t.shape, 1)
    lo = (lane % HEAD_DIM) < (HEAD_DIM // 2)
    return jnp.where(lo, -pltpu.roll(t, LANES - HEAD_DIM // 2, axis=1), pltpu.roll(t, HEAD_DIM // 2, axis=1))


def _rope(t, c, s):
    return t * c + _rot_half(t) * s


def _unrope(dy, c, s):
    return dy * c - _rot_half(dy * s)


PD = POOL_GROUPS * LANES
QD = N_HEADS * HEAD_DIM
KD = N_KV_HEADS * HEAD_DIM
assert PD % QD == 0 and (PD + QD) % (2 * KD) == 0 and KD == LANES
def _attn_probs(q, kcat, sink, mask):
    s = lax.dot_general(q.astype(BF16), kcat, (((1,), (1,)), ((), ())), preferred_element_type=F32) * (HEAD_DIM ** -0.5)
    s = jnp.where(mask, s, -jnp.inf)
    m = jnp.maximum(jnp.max(s, axis=1, keepdims=True), sink)
    p = jnp.exp(s - m)
    ps = jnp.exp(sink - m)
    inv = 1.0 / (jnp.sum(p, axis=1, keepdims=True) + ps)
    return p * inv, ps * inv


def _attn_mask(n):
    qi = lax.broadcasted_iota(jnp.int32, (BLOCK, 2 * BLOCK), 0)
    kj = lax.broadcasted_iota(jnp.int32, (BLOCK, 2 * BLOCK), 1)
    rel = qi + BLOCK - kj
    return (rel >= 0) & (rel < BLOCK) & ((n > 0) | (kj >= BLOCK))


def _attn_in_specs(nb):
    def cur(n):
        return jnp.minimum(n, nb - 1)

    def prev(n):
        return jnp.clip(n - 1, 0, nb - 1)

    kvb = (PD + QD) // (2 * KD)
    return [pl.BlockSpec(memory_space=pltpu.SMEM),
            pl.BlockSpec((BLOCK, QD), lambda n: (cur(n), PD // QD)),
            pl.BlockSpec((BLOCK, 2 * KD), lambda n: (cur(n), kvb)),
            pl.BlockSpec((BLOCK, 2 * KD), lambda n: (prev(n), kvb)),
            pl.BlockSpec((BLOCK, LANES), lambda n: (cur(n), 0)), pl.BlockSpec((BLOCK, LANES), lambda n: (cur(n), 0)),
            pl.BlockSpec((BLOCK, LANES), lambda n: (prev(n), 0)), pl.BlockSpec((BLOCK, LANES), lambda n: (prev(n), 0))]


def _attn_keys(kvc_ref, kvp_ref, cc, sc, cp, sp):
    kc = _rope(kvc_ref[:, :KD], cc, sc)
    kp = _rope(kvp_ref[:, :KD], cp, sp)
    vc, vp = kvc_ref[:, KD:], kvp_ref[:, KD:]
    kcat, vcat = [], []
    for kk in range(N_KV_HEADS):
        sl = slice(kk * HEAD_DIM, (kk + 1) * HEAD_DIM)
        kcat.append(jnp.concatenate([kp[:, sl], kc[:, sl]], axis=0).astype(BF16))
        vcat.append(jnp.concatenate([vp[:, sl], vc[:, sl]], axis=0).astype(BF16))
    return kcat, vcat


def _attn_fwd(proj, cos, sin, sinks, cat):
    S = proj.shape[0]
    nb = S // BLOCK

    def body(sink_ref, q_ref, kvc_ref, kvp_ref, cc_ref, sc_ref, cp_ref, sp_ref, cat_ref, o_ref):
        n = pl.program_id(0)
        cc, sc = cc_ref[...], sc_ref[...]
        kcat, vcat = _attn_keys(kvc_ref, kvp_ref, cc, sc, cp_ref[...], sp_ref[...])
        mask = _attn_mask(n)
        def head_pair(j):
            qr = _rope(q_ref[:, j * LANES:(j + 1) * LANES], cc, sc)
            for e in range(LANES // HEAD_DIM):
                yield
                h = j * (LANES // HEAD_DIM) + e
                pn, _ = _attn_probs(qr[:, e * HEAD_DIM:(e + 1) * HEAD_DIM], kcat[h // GQ], sink_ref[0, h], mask)
                yield
                o_ref[:, h * HEAD_DIM:(h + 1) * HEAD_DIM] = jnp.dot(
                    pn.astype(BF16), vcat[h // GQ], preferred_element_type=F32).astype(o_ref.dtype)

        _interleave([head_pair(j) for j in range(QD // LANES)])

    return _call(body, name="attn_fwd", out_shape=_sds(cat.shape, cat.dtype), grid=(nb,),
                 in_specs=_attn_in_specs(nb) + [ANY], out_specs=pl.BlockSpec((BLOCK, QD), lambda n: (n, PD // QD)),
                 aliases={8: 0})(sinks, proj, proj, proj, cos, sin, cos, sin, cat)


def _attn_bwd(proj, cos, sin, sinks, dcat):
    S = proj.shape[0]
    nb = S // BLOCK
    scale = HEAD_DIM ** -0.5
    per = LANES // HEAD_DIM

    def body(sink_ref, q_ref, kvc_ref, kvp_ref, cc_ref, sc_ref, cp_ref, sp_ref, do_ref, o_ref, ds_ref, hold, carry, part, pair):
        n = pl.program_id(0)

        @pl.when(n == 0)
        def _():
            hold[...] = jnp.zeros_like(hold)
            carry[...] = jnp.zeros_like(carry)
            ds_ref[...] = jnp.zeros_like(ds_ref)

        live = jnp.where(n < nb, 1.0, 0.0)
        cc, sc, cp, sp = cc_ref[...], sc_ref[...], cp_ref[...], sp_ref[...]
        kcat, vcat = _attn_keys(kvc_ref, kvp_ref, cc, sc, cp, sp)
        mask = _attn_mask(n)
        o_ref[:, :PD] = jnp.zeros((BLOCK, PD), F32)
        o_ref[:, PD:PD + QD] = hold[...]
        dk = [jnp.zeros((2 * BLOCK, HEAD_DIM), F32) for _ in range(N_KV_HEADS)]
        dv = [jnp.zeros((2 * BLOCK, HEAD_DIM), F32) for _ in range(N_KV_HEADS)]
        row = lax.broadcasted_iota(jnp.int32, (8, LANES), 0)
        acc = {"dsk": jnp.zeros((8, LANES), F32)}

        def head_pair(j):
            qr = _rope(q_ref[:, j * LANES:(j + 1) * LANES], cc, sc)
            for e in range(per):
                yield
                h = j * per + e
                kk = h // GQ
                qh = qr[:, e * HEAD_DIM:(e + 1) * HEAD_DIM]
                pn, psn = _attn_probs(qh, kcat[kk], sink_ref[0, h], mask)
                yield
                doh = (do_ref[:, h * HEAD_DIM:(h + 1) * HEAD_DIM] * live).astype(BF16)
                dp = lax.dot_general(doh, vcat[kk], NT, preferred_element_type=F32)
                yield
                delta = jnp.sum(pn * dp, axis=1, keepdims=True)
                ds = (pn * (dp - delta) * scale).astype(BF16)
                pair[j, :, e * HEAD_DIM:(e + 1) * HEAD_DIM] = jnp.dot(ds, kcat[kk], preferred_element_type=F32)
                yield
                dk[kk] = dk[kk] + lax.dot_general(ds, qh.astype(BF16), TN, preferred_element_type=F32)
                dv[kk] = dv[kk] + lax.dot_general(pn.astype(BF16), doh, TN, preferred_element_type=F32)
                acc["dsk"] = acc["dsk"] + jnp.where(row == h, -jnp.sum(psn * delta), 0.0)
            yield
            hold[:, j * LANES:(j + 1) * LANES] = _unrope(pair[j], cc, sc)

        _interleave([head_pair(j) for j in range(QD // LANES)])
        dsk = acc["dsk"]
        for kk in range(N_KV_HEADS):
            sl = slice(kk * HEAD_DIM, (kk + 1) * HEAD_DIM)
            sv = slice(KD + kk * HEAD_DIM, KD + (kk + 1) * HEAD_DIM)
            part[0, :, sl] = dk[kk][:BLOCK]
            part[0, :, sv] = dv[kk][:BLOCK]
            part[1, :, sl] = dk[kk][BLOCK:]
            part[1, :, sv] = dv[kk][BLOCK:]
        done = carry[...] + part[0]
        o_ref[:, PD + QD:PD + QD + KD] = _unrope(done[:, :KD], cp, sp)
        o_ref[:, PD + QD + KD:] = done[:, KD:]
        carry[...] = part[1]
        ds_ref[...] += dsk

    return _call(body, name="attn_bwd", out_shape=(_sds((S, PD + QD + 2 * KD)), _sds((8, LANES))), grid=(nb + 1,),
                 in_specs=_attn_in_specs(nb) + [pl.BlockSpec((BLOCK, QD), lambda n: (jnp.minimum(n, nb - 1), PD // QD))],
                 out_specs=(pl.BlockSpec((BLOCK, PD + QD + 2 * KD), lambda n: (jnp.maximum(n - 1, 0), 0)),
                            pl.BlockSpec((8, LANES), lambda n: (0, 0))),
                 scratch=[pltpu.VMEM((BLOCK, QD), F32), pltpu.VMEM((BLOCK, 2 * KD), F32),
                          pltpu.VMEM((2, BLOCK, 2 * KD), F32), pltpu.VMEM((QD // LANES, BLOCK, LANES), F32)])(
                     sinks, proj, proj, proj, cos, sin, cos, sin, dcat)


def _pool_sums(u, g, t, shift):
    s2 = u + shift(u, 1, t)
    s4 = s2 + shift(s2, 2, t)
    s8 = s4 + shift(s4, 4, t)
    s16 = s8 + shift(s8, 8, t)
    return jnp.where(g == 0, s2, jnp.where(g == 1, s4, jnp.where(g == 2, s8, s16)))


def _pool_specs(S):
    col = pl.BlockSpec((S, LANES), lambda g: (0, g))
    wsp = pl.BlockSpec((1, LANES, LANES), lambda g: (g, 0, 0))
    vec = pl.BlockSpec((1, LANES), lambda g: (0, g))
    return col, wsp, vec


def _pool_fwd(proj, pool_w, scale):
    S = proj.shape[0]
    col, wsp, vec = _pool_specs(S)

    def body(u_ref, w_ref, s_ref, o_ref):
        g = pl.program_id(0)
        u = u_ref[...]
        t = lax.broadcasted_iota(jnp.int32, u.shape, 0)
        cnt = jnp.minimum(t + 1, 2 << g).astype(F32)
        pm = _pool_sums(u, g, t, _shift_dn) / cnt - u
        o_ref[...] = (jnp.dot(pm.astype(BF16), w_ref[0].astype(BF16), preferred_element_type=F32) * s_ref[...]).astype(BF16)

    return _call(body, name="pool_fwd", out_shape=_sds((S, PD + QD), BF16), grid=(POOL_GROUPS,),
                 in_specs=[col, wsp, vec], out_specs=col)(proj, pool_w, scale)


def _pool_bwd(proj, pool_w, scale, dcat, dproj):
    S = proj.shape[0]
    col, wsp, vec = _pool_specs(S)

    def body(u_ref, w_ref, s_ref, d_ref, dproj_ref, du_ref, dw_ref, dsc_ref):
        g = pl.program_id(0)
        u = u_ref[...]
        t = lax.broadcasted_iota(jnp.int32, u.shape, 0)
        cnt = jnp.minimum(t + 1, 2 << g).astype(F32)
        pm = (_pool_sums(u, g, t, _shift_dn) / cnt - u).astype(BF16)
        wv = w_ref[0].astype(BF16)
        d = d_ref[...]
        pw = jnp.dot(pm, wv, preferred_element_type=F32)
        dsc_ref[...] = jnp.sum(pw * d, axis=0, keepdims=True)
        dpw = (d * s_ref[...]).astype(BF16)
        dw_ref[0] = lax.dot_general(pm, dpw, (((0,), (0,)), ((), ())), preferred_element_type=F32)
        dpm = lax.dot_general(dpw, wv, (((1,), (1,)), ((), ())), preferred_element_type=F32)
        du_ref[...] = _pool_sums(dpm / cnt, g, t, _shift_up) - dpm

    return _call(body, name="pool_bwd",
                 out_shape=(_sds(dproj.shape), _sds((POOL_GROUPS, LANES, LANES)), _sds((1, POOL_GROUPS * LANES))),
                 grid=(POOL_GROUPS,), in_specs=[col, wsp, vec, col, ANY], out_specs=(col, wsp, vec),
                 aliases={4: 0})(proj, pool_w, scale, dcat, dproj)


def _conv(x, w_ref, b_ref, t):
    K = w_ref.shape[0]
    y = b_ref[...] + jnp.zeros_like(x)
    for k in range(K):
        y = y + w_ref[k:k + 1, :] * _shift_dn(x, K - 1 - k, t)
    return y


def _silu_grad(y):
    sg = _sigmoid(y)
    return sg * (1.0 + y * (1.0 - sg))


CONV_ROWS = 256
HALO = 8


def _win_above(ref, r0):
    if isinstance(r0, int):
        assert r0 == 0
        return jnp.concatenate([jnp.zeros((HALO, ref.shape[1]), F32), ref[0:CONV_ROWS, :]], axis=0)
    return ref[pl.ds(pl.multiple_of(r0 - HALO, HALO), CONV_ROWS + HALO), :]


def _rows_at(win, start):
    if start % 8 == 0:
        return win[start:start + CONV_ROWS]
    base = start // 8 * 8
    return pltpu.roll(win, win.shape[0] - (start - base), axis=0)[base:base + CONV_ROWS]


def _taps_above(win, K):
    return [_rows_at(win, HALO - (K - 1 - k)) for k in range(K)]


def _conv_taps(taps, w, b):
    y = b
    for k in range(len(w)):
        y = y + w[k] * taps[k]
    return y


def _conv_t_win(win, w):
    K = len(w)
    out = None
    for k in range(K):
        d = K - 1 - k
        term = w[k] * _rows_at(win, d)
        out = term if out is None else out + term
    return out


def _fold8(x):
    return jnp.sum(x.reshape(CONV_ROWS // 8, 8, x.shape[-1]), axis=0)


def _chunk_loop(S, step, init):
    carry = step(0, init)
    return lax.fori_loop(1, S // CONV_ROWS, lambda i, c: step(pl.multiple_of(i * CONV_ROWS, CONV_ROWS), c), carry)


def _ffn_mid_specs(S, K, layer, nf):
    return [pl.BlockSpec((S, LANES), lambda j: (0, j)), pl.BlockSpec((S, LANES), lambda j: (0, nf + j)),
            pl.BlockSpec((None, K, LANES), lambda j: (layer, 0, j)), pl.BlockSpec((None, K, LANES), lambda j: (layer, 0, nf + j)),
            pl.BlockSpec((None, 1, LANES), lambda j: (layer, 0, j)), pl.BlockSpec((None, 1, LANES), lambda j: (layer, 0, nf + j))]


def _ffn_mid_fwd(a, cw, cb, layer):
    S, F = a.shape[0], a.shape[1] // 2
    nf = F // LANES
    K = cw.shape[1]

    def body(au_ref, ag_ref, wu_ref, wg_ref, bu_ref, bg_ref, o_ref):
        t = lax.broadcasted_iota(jnp.int32, (S, LANES), 0)
        hu = _conv(au_ref[...], wu_ref, bu_ref, t)
        hg = _conv(ag_ref[...], wg_ref, bg_ref, t)
        o_ref[...] = (hg * _sigmoid(hg) * hu).astype(BF16)

    return _call(body, name="ffn_mid_fwd", out_shape=_sds((S, F), BF16), grid=(nf,),
                 in_specs=_ffn_mid_specs(S, K, layer, nf), out_specs=pl.BlockSpec((S, LANES), lambda j: (0, j)))(
                     a, a, cw, cw, cb[:, None], cb[:, None])


def _ffn_mid_bwd(a, cw, cb, layer, dact):
    S, F = a.shape[0], a.shape[1] // 2
    nf = F // LANES
    K = cw.shape[1]

    def body(au_ref, ag_ref, wu_ref, wg_ref, bu_ref, bg_ref, d_ref, dau_ref, dag_ref, dwu_ref, dwg_ref, dbu_ref, dbg_ref,
             dhu_s, dhg_s):
        T = CONV_ROWS
        wu = [wu_ref[k:k + 1, :] for k in range(K)]
        wg = [wg_ref[k:k + 1, :] for k in range(K)]
        bu, bg = bu_ref[...], bg_ref[...]
        zero8 = jnp.zeros((HALO, LANES), F32)
        dhu_s[S:S + HALO, :] = zero8
        dhg_s[S:S + HALO, :] = zero8

        def first_pass(r0, acc):
            tu, tg = _taps_above(_win_above(au_ref, r0), K), _taps_above(_win_above(ag_ref, r0), K)
            hu, hg = _conv_taps(tu, wu, bu), _conv_taps(tg, wg, bg)
            d = d_ref[pl.ds(r0, T), :].astype(F32)
            sg = _sigmoid(hg)
            dhu = d * hg * sg
            dhg = d * hu * (sg * (1.0 + hg * (1.0 - sg)))
            dhu_s[pl.ds(r0, T), :] = dhu
            dhg_s[pl.ds(r0, T), :] = dhg
            new = []
            for dh, taps in ((dhu, tu), (dhg, tg)):
                for k in range(K):
                    new.append(acc[len(new)] + _fold8(dh * taps[k]))
            new.append(acc[2 * K] + _fold8(dhu))
            new.append(acc[2 * K + 1] + _fold8(dhg))
            return tuple(new)

        acc = _chunk_loop(S, first_pass, tuple(zero8 for _ in range(2 * K + 2)))
        for k in range(K):
            dwu_ref[k:k + 1, :] = jnp.sum(acc[k], axis=0, keepdims=True)
            dwg_ref[k:k + 1, :] = jnp.sum(acc[K + k], axis=0, keepdims=True)
        dbu_ref[...] = jnp.sum(acc[2 * K], axis=0, keepdims=True)
        dbg_ref[...] = jnp.sum(acc[2 * K + 1], axis=0, keepdims=True)

        def second_pass(i, carry):
            r0 = pl.multiple_of(i * T, T)
            dau_ref[pl.ds(r0, T), :] = _conv_t_win(dhu_s[pl.ds(r0, T + HALO), :], wu).astype(BF16)
            dag_ref[pl.ds(r0, T), :] = _conv_t_win(dhg_s[pl.ds(r0, T + HALO), :], wg).astype(BF16)
            return carry

        lax.fori_loop(0, S // T, second_pass, 0)

    col = pl.BlockSpec((S, LANES), lambda j: (0, j))
    wsp = pl.BlockSpec((K, LANES), lambda j: (0, j))
    bsp = pl.BlockSpec((1, LANES), lambda j: (0, j))
    dau, dag, dwu, dwg, dbu, dbg = _call(
        body, name="ffn_mid_bwd",
        out_shape=(_sds((S, F), BF16), _sds((S, F), BF16), _sds((K, F)), _sds((K, F)), _sds((1, F)), _sds((1, F))), grid=(nf,),
        in_specs=_ffn_mid_specs(S, K, layer, nf) + [col], out_specs=(col, col, wsp, wsp, bsp, bsp),
        scratch=[pltpu.VMEM((S + HALO, LANES), F32), pltpu.VMEM((S + HALO, LANES), F32)])(
            a, a, cw, cw, cb[:, None], cb[:, None], dact)
    return dau, dag, jnp.concatenate([dwu, dwg], axis=1), jnp.concatenate([dbu, dbg], axis=1)


def _conv_silu_fwd(x, cw, cb):
    S = x.shape[0]
    K, C = cw.shape

    def body(x_ref, w_ref, b_ref, o_ref):
        t = lax.broadcasted_iota(jnp.int32, (S, LANES), 0)
        y = _conv(x_ref[...], w_ref, b_ref, t)
        o_ref[...] = y * _sigmoid(y)

    col = pl.BlockSpec((S, LANES), lambda j: (0, j))
    return _call(body, name="conv_silu_fwd", out_shape=_sds((S, C)), grid=(C // LANES,),
                 in_specs=[col, pl.BlockSpec((K, LANES), lambda j: (0, j)), pl.BlockSpec((1, LANES), lambda j: (0, j))],
                 out_specs=col)(x, cw, cb)


def _conv_silu_bwd(x, cw, cb, douts):
    S = x.shape[0]
    K, C = cw.shape
    starts, off = [], 0
    for d in douts:
        starts.append(off)
        off += d.shape[1] // LANES
    assert off == C // LANES

    def body(x_ref, w_ref, b_ref, *rest):
        dy_s = rest[-1]
        d_refs, (dx_ref, dw_ref, db_ref) = rest[:len(douts)], rest[len(douts):-1]
        j = pl.program_id(0)
        T = CONV_ROWS
        w = [w_ref[k:k + 1, :] for k in range(K)]
        b = b_ref[...]
        zero8 = jnp.zeros((HALO, LANES), F32)
        dy_s[S:S + HALO, :] = zero8

        def first_pass(r0, acc):
            taps = _taps_above(_win_above(x_ref, r0), K)
            y = _conv_taps(taps, w, b)
            d = d_refs[0][pl.ds(r0, T), :]
            for i in range(1, len(douts)):
                d = jnp.where(j >= starts[i], d_refs[i][pl.ds(r0, T), :], d)
            dy = d * _silu_grad(y)
            dy_s[pl.ds(r0, T), :] = dy
            return tuple(acc[k] + _fold8(dy * taps[k]) for k in range(K)) + (acc[K] + _fold8(dy),)

        acc = _chunk_loop(S, first_pass, tuple(zero8 for _ in range(K + 1)))
        for k in range(K):
            dw_ref[k:k + 1, :] = jnp.sum(acc[k], axis=0, keepdims=True)
        db_ref[...] = jnp.sum(acc[K], axis=0, keepdims=True)

        def second_pass(i, carry):
            r0 = pl.multiple_of(i * T, T)
            dx_ref[pl.ds(r0, T), :] = _conv_t_win(dy_s[pl.ds(r0, T + HALO), :], w).astype(BF16)
            return carry

        lax.fori_loop(0, S // T, second_pass, 0)

    col = pl.BlockSpec((S, LANES), lambda j: (0, j))
    wsp = pl.BlockSpec((K, LANES), lambda j: (0, j))
    bsp = pl.BlockSpec((1, LANES), lambda j: (0, j))

    def dspec(i):
        nblk = douts[i].shape[1] // LANES
        return pl.BlockSpec((S, LANES), lambda j: (0, jnp.clip(j - starts[i], 0, nblk - 1)))

    return _call(body, name="conv_silu_bwd", out_shape=(_sds((S, C), BF16), _sds((K, C)), _sds((1, C))), grid=(C // LANES,),
                 in_specs=[col, wsp, bsp] + [dspec(i) for i in range(len(douts))],
                 out_specs=(col, wsp, bsp), scratch=[pltpu.VMEM((S + HALO, LANES), F32)])(x, cw, cb, *douts)


HI = lax.Precision.HIGHEST


def _ssd_prep_fwd(proj, col0, bias_g, alog_g):
    S = proj.shape[0]
    nc = S // SSM_L
    b0 = col0 // LANES

    def body(raw_ref, b_ref, al_ref, pre_ref, dt_ref, acs_ref, acst_ref):
        r_i = lax.broadcasted_iota(jnp.int32, (LANES, LANES), 0)
        c_i = lax.broadcasted_iota(jnp.int32, (LANES, LANES), 1)
        live = c_i < SSM_R
        tril = jnp.where(r_i >= c_i, 1.0, 0.0)
        raw = raw_ref[...]
        for g in range(SSM_G):
            mine = raw if g == 0 else pltpu.roll(raw, LANES - SSM_R * g, axis=1)
            pre = jnp.where(live, mine, 0.0) + b_ref[g]
            dt = jnp.where(live, jnp.logaddexp(pre, 0.0), 0.0)
            a = dt * (-jnp.exp(al_ref[g]))
            acs = jnp.dot(tril, a, preferred_element_type=F32, precision=HI)
            pre_ref[g] = pre
            dt_ref[g] = dt
            acs_ref[g] = acs
            acst_ref[g] = acs.T

    gsp = pl.BlockSpec((SSM_G, 1, LANES), lambda c: (0, 0, 0))
    blk = pl.BlockSpec((SSM_G, SSM_L, LANES), lambda c: (0, c, 0))
    big = _sds((SSM_G, S, LANES))
    return _call(body, name="ssd_prep_fwd", out_shape=(big, big, big, _sds((SSM_G, LANES, S))), grid=(nc,),
                 in_specs=[pl.BlockSpec((SSM_L, LANES), lambda c: (c, b0)), gsp, gsp],
                 out_specs=(blk, blk, blk, pl.BlockSpec((SSM_G, LANES, SSM_L), lambda c: (0, 0, c))))(proj, bias_g, alog_g)


def _ssd_prep_bwd(pre_g, dt_g, alog_g, ddt_g, dacs_g, dacst_g):
    S = pre_g.shape[1]
    nc = S // SSM_L

    def body(pre_ref, dt_ref, al_ref, ddt_ref, dacs_ref, dacst_ref, draw_ref, db_ref, dal_ref):
        c = pl.program_id(0)
        r_i = lax.broadcasted_iota(jnp.int32, (LANES, LANES), 0)
        c_i = lax.broadcasted_iota(jnp.int32, (LANES, LANES), 1)
        live = c_i < SSM_R
        triu = jnp.where(r_i <= c_i, 1.0, 0.0)

        @pl.when(c == 0)
        def _():
            db_ref[...] = jnp.zeros_like(db_ref)
            dal_ref[...] = jnp.zeros_like(dal_ref)

        draw = jnp.zeros((SSM_L, LANES), F32)
        for g in range(SSM_G):
            dacs = dacs_ref[g] + dacst_ref[g].T
            da = jnp.dot(triu, dacs, preferred_element_type=F32, precision=HI)
            A = -jnp.exp(al_ref[g])
            ddt = ddt_ref[g] + da * A
            dpre = jnp.where(live, ddt * _sigmoid(pre_ref[g]), 0.0)
            draw = draw + (dpre if g == 0 else pltpu.roll(dpre, SSM_R * g, axis=1))
            db_ref[g] += jnp.sum(dpre, axis=0, keepdims=True)
            dal_ref[g] += jnp.where(live[:1], jnp.sum(da * dt_ref[g], axis=0, keepdims=True) * A, 0.0)
        draw_ref[...] = draw

    gsp = pl.BlockSpec((SSM_G, 1, LANES), lambda c: (0, 0, 0))
    blk = pl.BlockSpec((SSM_G, SSM_L, LANES), lambda c: (0, c, 0))
    gout = _sds((SSM_G, 1, LANES))
    return _call(body, name="ssd_prep_bwd", out_shape=(_sds((S, LANES)), gout, gout), grid=(nc,),
                 in_specs=[blk, blk, gsp, blk, blk, pl.BlockSpec((SSM_G, LANES, SSM_L), lambda c: (0, 0, c))],
                 out_specs=(pl.BlockSpec((SSM_L, LANES), lambda c: (c, 0)), gsp, gsp))(pre_g, dt_g, alog_g, ddt_g, dacs_g, dacst_g)


NT = (((1,), (1,)), ((), ()))
TN = (((0,), (0,)), ((), ()))
SSM_HP = SSM_R * SSM_P


def _ssd_group_terms(xs_ref, dt_ref, acs_ref, d_ref):
    hid = lax.broadcasted_iota(jnp.int32, (1, SSM_HP), 1) // SSM_P
    rid = lax.broadcasted_iota(jnp.int32, (SSM_HP, 1), 0) // SSM_P

    def widen(cols):
        out = cols[0]
        for r in range(1, SSM_R):
            out = jnp.where(hid == r, cols[r], out)
        return out

    dt_c = [dt_ref[:, r:r + 1] for r in range(SSM_R)]
    acs_c = [acs_ref[:, r:r + 1] for r in range(SSM_R)]
    last = [acs_ref[SSM_L - 1:SSM_L, r:r + 1] for r in range(SSM_R)]
    decay_c = [jnp.exp(last[r] - acs_c[r]) for r in range(SSM_R)]
    cd = [jnp.exp(last[r]) for r in range(SSM_R)]
    cd_rows = cd[0]
    for r in range(1, SSM_R):
        cd_rows = jnp.where(rid == r, cd[r], cd_rows)
    xs = xs_ref[...]
    return (xs, xs * widen(dt_c), widen([jnp.exp(a) for a in acs_c]), widen(decay_c),
            widen([d_ref[:, r:r + 1] for r in range(SSM_R)]), cd_rows, dt_c, decay_c, cd)


def _ssd_lmat(acs_ref, acst_ref, r, tril):
    return jnp.exp(jnp.where(tril, acs_ref[:, r:r + 1] - acst_ref[r:r + 1, :], -jnp.inf))


SSM_GPS = 4


def _ssd_specs(rev, nc):
    def cc(c):
        return nc - 1 - c if rev else c
    xs_blocks = (SSM_G * SSM_HP) // (SSM_GPS * SSM_N)
    xs = pl.BlockSpec((SSM_L, SSM_GPS * SSM_HP), lambda g, c: (cc(c), g))
    bsp = pl.BlockSpec((SSM_L, SSM_GPS * SSM_N), lambda g, c: (cc(c), xs_blocks + g))
    csp = pl.BlockSpec((SSM_L, SSM_GPS * SSM_N), lambda g, c: (cc(c), xs_blocks + SSM_G // SSM_GPS + g))
    sc = pl.BlockSpec((SSM_GPS, SSM_L, LANES), lambda g, c: (g, cc(c), 0))
    sct = pl.BlockSpec((SSM_GPS, LANES, SSM_L), lambda g, c: (g, 0, cc(c)))
    gsp = pl.BlockSpec((SSM_GPS, 1, LANES), lambda g, c: (g, 0, 0))
    st = pl.BlockSpec((None, SSM_GPS, SSM_HP, SSM_N), lambda g, c: (cc(c), g, 0, 0))
    return xs, bsp, csp, sc, sct, gsp, st


def _interleave(gens):
    live = list(gens)
    while live:
        for g in list(live):
            try:
                next(g)
            except StopIteration:
                live.remove(g)


def _rounds(gens):
    live = list(gens)
    while live:
        for g in list(live):
            try:
                next(g)
            except StopIteration:
                live.remove(g)
        yield


def _ssd_group_views(gg, xs_ref, b_ref, c_ref, *per_group):
    return (xs_ref.at[:, gg * SSM_HP:(gg + 1) * SSM_HP], b_ref.at[:, gg * SSM_N:(gg + 1) * SSM_N],
            c_ref.at[:, gg * SSM_N:(gg + 1) * SSM_N]) + tuple(r.at[gg] for r in per_group)


def _ssd_fwd(xbc, dt_g, acs_g, acst_g, d_g):
    S = xbc.shape[0]
    nc = S // SSM_L
    xs_s, b_s, c_s, sc, sct, gsp, st = _ssd_specs(False, nc)

    def body(xs_ref, b_ref, c_ref, dt_ref, acs_ref, acst_ref, d_ref, y_ref, st_ref, state):
        c = pl.program_id(1)

        @pl.when(c == 0)
        def _():
            state[...] = jnp.zeros_like(state)

        tril = lax.broadcasted_iota(jnp.int32, (SSM_L, SSM_L), 0) >= lax.broadcasted_iota(jnp.int32, (SSM_L, SSM_L), 1)
        def group(gg):
            xs_v, b_v, c_v, dt_v, acs_v, acst_v, d_v, st_v, state_v = _ssd_group_views(
                gg, xs_ref, b_ref, c_ref, dt_ref, acs_ref, acst_ref, d_ref, st_ref, state)
            y_v = y_ref.at[:, gg * SSM_HP:(gg + 1) * SSM_HP]
            Bb, Cb = b_v[...].astype(BF16), c_v[...].astype(BF16)
            Gm = lax.dot_general(Cb, Bb, NT, preferred_element_type=F32)
            yield
            xs, X, e_all, decay_all, d_all, cd_rows, _, _, _ = _ssd_group_terms(xs_v, dt_v, acs_v, d_v)
            S_all = state_v[...]
            st_v[...] = S_all
            yield
            yo = lax.dot_general(Cb, S_all.astype(BF16), NT, preferred_element_type=F32)
            new_state = lax.dot_general((X * decay_all).astype(BF16), Bb, TN, preferred_element_type=F32)
            yield
            y_v[...] = e_all * yo + d_all * xs
            state_v[...] = S_all * cd_rows + new_state

            def head(r):
                sl = slice(r * SSM_P, (r + 1) * SSM_P)
                M = Gm * _ssd_lmat(acs_v, acst_v, r, tril)
                yield
                y_v[:, sl] += jnp.dot(M.astype(BF16), X[:, sl].astype(BF16), preferred_element_type=F32)

            yield from _rounds([head(r) for r in range(SSM_R)])

        _interleave([group(gg) for gg in range(SSM_GPS)])

    return _call(body, name="ssd_fwd",
                 out_shape=(_sds((S, SSM_G * SSM_HP)), _sds((nc, SSM_G, SSM_HP, SSM_N))),
                 grid=(SSM_G // SSM_GPS, nc), in_specs=[xs_s, b_s, c_s, sc, sc, sct, gsp],
                 out_specs=(xs_s, pl.BlockSpec((None, SSM_GPS, SSM_HP, SSM_N), lambda g, c: (c, g, 0, 0))),
                 scratch=[pltpu.VMEM((SSM_GPS, SSM_HP, SSM_N), F32)])(xbc, xbc, xbc, dt_g, acs_g, acst_g, d_g)


def _ssd_bwd(xbc, dt_g, acs_g, acst_g, d_g, states, dy):
    S = xbc.shape[0]
    nc = S // SSM_L
    xs_s, b_s, c_s, sc, sct, gsp, st = _ssd_specs(True, nc)
    bc_out = pl.BlockSpec((SSM_L, SSM_GPS * SSM_N), lambda g, c: (nc - 1 - c, g))

    def body(xs_ref, b_ref, c_ref, dt_ref, acs_ref, acst_ref, d_ref, st_ref, dy_ref,
             dxs_ref, db_ref, dc_ref, ddt_ref, dacs_ref, dacst_ref, dd_ref, dstate):
        c = pl.program_id(1)

        @pl.when(c == 0)
        def _():
            dstate[...] = jnp.zeros_like(dstate)
            dd_ref[...] = jnp.zeros_like(dd_ref)

        tril = lax.broadcasted_iota(jnp.int32, (SSM_L, SSM_L), 0) >= lax.broadcasted_iota(jnp.int32, (SSM_L, SSM_L), 1)
        lane = lax.broadcasted_iota(jnp.int32, (1, LANES), 1)
        subl = lax.broadcasted_iota(jnp.int32, (LANES, 1), 0)
        last_row = lax.broadcasted_iota(jnp.int32, (SSM_L, 1), 0) == SSM_L - 1
        triu = lax.broadcasted_iota(jnp.int32, (SSM_L, SSM_L), 0) <= lax.broadcasted_iota(jnp.int32, (SSM_L, SSM_L), 1)
        def group(gg):
            xs_v, b_v, c_v, dt_v, acs_v, acst_v, d_v, st_v, ddt_v, dacs_v, dacst_v, dd_v, dstate_v = _ssd_group_views(
                gg, xs_ref, b_ref, c_ref, dt_ref, acs_ref, acst_ref, d_ref, st_ref, ddt_ref, dacs_ref, dacst_ref, dd_ref, dstate)
            dy_v, dxs_v = (r.at[:, gg * SSM_HP:(gg + 1) * SSM_HP] for r in (dy_ref, dxs_ref))
            db_v, dc_v = (r.at[:, gg * SSM_N:(gg + 1) * SSM_N] for r in (db_ref, dc_ref))
            Bb, Cb = b_v[...].astype(BF16), c_v[...].astype(BF16)
            Gm = lax.dot_general(Cb, Bb, NT, preferred_element_type=F32)
            GmT = lax.dot_general(Bb, Cb, NT, preferred_element_type=F32)
            yield
            xs, X, e_all, decay_all, d_all, cd_rows, dt_c, decay_c, cd = _ssd_group_terms(xs_v, dt_v, acs_v, d_v)
            S_all, dSn_all, dY = st_v[...], dstate_v[...], dy_v[...]
            Sb, dSnb = S_all.astype(BF16), dSn_all.astype(BF16)
            yield
            T = lax.dot_general(Cb, Sb, NT, preferred_element_type=F32)
            dT = (dY * e_all).astype(BF16)
            dC = jnp.dot(dT, Sb, preferred_element_type=F32)
            dS_prev = lax.dot_general(dT, Cb, TN, preferred_element_type=F32)
            yield
            yo_dy = dY * (e_all * T)
            W = lax.dot_general(Bb, dSnb, NT, preferred_element_type=F32)
            dB = jnp.dot((X * decay_all).astype(BF16), dSnb, preferred_element_type=F32)
            yield
            xw = X * W
            dcd_rows = jnp.sum(dSn_all * S_all, axis=1, keepdims=True)
            dstate_v[...] = dS_prev + dSn_all * cd_rows
            dX_state = W * decay_all
            yield
            acc = dict(dG=jnp.zeros((SSM_L, SSM_L), F32), dGT=jnp.zeros((SSM_L, SSM_L), F32),
                       ddt=jnp.zeros((SSM_L, LANES), F32), dacs=jnp.zeros((SSM_L, LANES), F32),
                       dacst=jnp.zeros((LANES, SSM_L), F32), dd=jnp.zeros((1, LANES), F32))

            def head(r):
                sl = slice(r * SSM_P, (r + 1) * SSM_P)
                Lm = _ssd_lmat(acs_v, acst_v, r, tril)
                LmT = jnp.exp(jnp.where(triu, acst_v[r:r + 1, :] - acs_v[:, r:r + 1], -jnp.inf))
                M = Gm * Lm
                yield
                dYh, xs_h = dY[:, sl], xs[:, sl]
                dYb, Xb = dYh.astype(BF16), X[:, sl].astype(BF16)
                dM = lax.dot_general(dYb, Xb, NT, preferred_element_type=F32)
                yield
                dX = jnp.dot((GmT * LmT).astype(BF16), dYb, preferred_element_type=F32) + dX_state[:, sl]
                acc["dG"] = acc["dG"] + dM * Lm
                acc["dGT"] = acc["dGT"] + lax.dot_general(Xb, dYb, NT, preferred_element_type=F32) * LmT
                yield
                dseg = dM * M
                dd = jnp.sum(xw[:, sl], axis=1, keepdims=True) * decay_c[r]
                dcd = jnp.sum(dcd_rows[sl])
                dacs_col = (jnp.sum(dseg, axis=1, keepdims=True) + jnp.sum(yo_dy[:, sl], axis=1, keepdims=True) - dd
                            + jnp.where(last_row, dcd * cd[r] + jnp.sum(dd), 0.0))
                dacs_row = -jnp.sum(dseg, axis=0, keepdims=True)
                yield
                dxs_v[:, sl] = dX * dt_c[r] + d_all[:, sl] * dYh
                acc["ddt"] = acc["ddt"] + jnp.where(lane == r, jnp.sum(dX * xs_h, axis=1, keepdims=True), 0.0)
                acc["dacs"] = acc["dacs"] + jnp.where(lane == r, dacs_col, 0.0)
                acc["dacst"] = acc["dacst"] + jnp.where(subl == r, dacs_row, 0.0)
                acc["dd"] = acc["dd"] + jnp.where(lane == r, jnp.sum(dYh * xs_h), 0.0)

            yield from _rounds([head(r) for r in range(SSM_R)])
            dc_v[...] = dC + jnp.dot(acc["dG"].astype(BF16), Bb, preferred_element_type=F32)
            db_v[...] = dB + jnp.dot(acc["dGT"].astype(BF16), Cb, preferred_element_type=F32)
            ddt_v[...] = acc["ddt"]
            dacs_v[...] = acc["dacs"]
            dacst_v[...] = acc["dacst"]
            dd_v[...] += acc["dd"]

        _interleave([group(gg) for gg in range(SSM_GPS)])

    big = _sds((SSM_G, S, LANES))
    return _call(body, name="ssd_bwd",
                 out_shape=(_sds((S, SSM_G * SSM_HP)), _sds((S, SSM_G * SSM_N)), _sds((S, SSM_G * SSM_N)),
                            big, big, _sds((SSM_G, LANES, S)), _sds((SSM_G, 1, LANES))),
                 grid=(SSM_G // SSM_GPS, nc), in_specs=[xs_s, b_s, c_s, sc, sc, sct, gsp, st, xs_s],
                 out_specs=(xs_s, bc_out, bc_out, sc, sc, sct, gsp),
                 scratch=[pltpu.VMEM((SSM_GPS, SSM_HP, SSM_N), F32)])(xbc, xbc, xbc, dt_g, acs_g, acst_g, d_g, states, dy)


def _gate_norm_fwd(y, proj, w):
    S, DI = y.shape
    tm = _tile(S, 256)

    def body(y_ref, z_ref, w_ref, o_ref):
        z = z_ref[...]
        gn = y_ref[...] * (z * _sigmoid(z))
        r = lax.rsqrt(jnp.mean(gn * gn, axis=-1, keepdims=True) + SSM_NORM_EPS)
        o_ref[...] = (gn * r * w_ref[...]).astype(BF16)

    row = pl.BlockSpec((tm, DI), lambda i: (i, 0))
    return _call(body, name="gate_norm_fwd", out_shape=_sds((S, DI), BF16), grid=(S // tm,),
                 in_specs=[row, row, pl.BlockSpec((1, DI), lambda i: (0, 0))], out_specs=row)(y, proj, w)


def _gate_norm_bwd(y, proj, w, dout):
    S, DI = y.shape
    tm = _tile(S, 256)

    def body(y_ref, z_ref, w_ref, d_ref, dy_ref, dz_ref, dw_ref):
        z, yv = z_ref[...], y_ref[...]
        sz = z * _sigmoid(z)
        dgn, dw = _norm_bwd_math(yv * sz, w_ref[...], d_ref[...].astype(F32), SSM_NORM_EPS)
        dy_ref[...] = dgn * sz
        dz_ref[...] = (dgn * yv * _silu_grad(z)).astype(BF16)

        @pl.when(pl.program_id(0) == 0)
        def _():
            dw_ref[...] = jnp.zeros_like(dw_ref)

        dw_ref[...] += dw

    row = pl.BlockSpec((tm, DI), lambda i: (i, 0))
    vec = pl.BlockSpec((1, DI), lambda i: (0, 0))
    return _call(body, name="gate_norm_bwd", out_shape=(_sds((S, DI)), _sds((S, DI), BF16), _sds((1, DI))), grid=(S // tm,),
                 in_specs=[row, row, vec, row], out_specs=(row, row, vec))(y, proj, w, dout)


def _group_major(v):
    return jnp.pad(v.reshape(SSM_G, 1, SSM_R), ((0, 0), (0, 0), (0, LANES - SSM_R)))


def _ungroup(t):
    return t[:, :SSM_R].reshape(1, SSM_G * SSM_R)


def _ffn_fwd(x, P, l, need):
    need(f"ffn{l}_up", x)
    h = _rmsnorm(x, P["norm_ffn"][l:l + 1], name=f"ffn{l}_norm")
    wT = P[f"ffn_w_upT{l}"]
    F = wT.shape[0] // 2
    a = _mm(h, wT, tb=True, name=f"ffn{l}_up")
    need(f"ffn{l}_down", a)
    act = _ffn_mid_fwd(a, P["ffn_conv_w"], P["ffn_conv_b"], l)
    out = _mm(act, P[f"ffn_w_down{l}"], res=x, name=f"ffn{l}_down")
    return out, (x, h, a, act)


def _ffn_bwd(saved, P, l, dx, emit):
    x, h, a, act = saved
    wT = P[f"ffn_w_upT{l}"]
    F = wT.shape[0] // 2
    dact = _mm(dx, P[f"ffn_w_down{l}"], tb=True, out_dtype=BF16, name=f"ffn{l}_down_dx")
    dw_down = _mm(act, dx, ta=True, out_dtype=PAYLOAD, name=f"ffn{l}_down_dw")
    dau, dag, dcw, dcb = _ffn_mid_bwd(a, P["ffn_conv_w"], P["ffn_conv_b"], l, dact)
    dw_upT = _mm(dau, h, ta=True, out_dtype=PAYLOAD, out_rows=(2 * F, 0, None), name=f"ffn{l}_up_u_dw")
    dw_upT = _mm(dag, h, ta=True, out_dtype=PAYLOAD, out_rows=(2 * F, F, dw_upT), name=f"ffn{l}_up_g_dw")
    tie = emit(f"ffn{l}", {"ffn_w_upT": dw_upT, "ffn_w_down": dw_down})
    dh = _mm(dau, wT, b_rows=(0, F), name=f"ffn{l}_up_u_dx")
    dx_in, dnw = _mm_norm_bwd(dag, wT, x, P["norm_ffn"][l:l + 1], dx, res=dh, b_rows=(F, F), after=tie,
                              name=f"ffn{l}_up_g_dx_norm_bwd")
    return dx_in, dnw, dcw, dcb


def _local_step(x, positions, target, P, need, emit, after=None):
    S, D = x.shape
    inv_freq = ROPE_THETA ** (-jnp.arange(0, HEAD_DIM, 2, dtype=F32) / HEAD_DIM)
    inv_freq = jnp.tile(inv_freq, LANES // (HEAD_DIM // 2)).reshape(1, LANES)
    cos, sin = _rope_tables(positions, inv_freq)

    nm0 = P["norm_mix"][0:1]
    h0 = _rmsnorm(x, nm0, name="mix_norm", after=after)
    need("mix_in", h0)
    proj0 = _mm(h0, P["mix_w_inT"], tb=True, name="mix_in")
    cat0 = _attn_fwd(proj0, cos, sin, P["attn_sinks"], _pool_fwd(proj0, P["pool_w"][0], P["pool_scale"]))
    need("mix_out", cat0)
    x1 = _mm(cat0, P["mix_w_out"], res=x, name="mix_out")
    x2, ffn0 = _ffn_fwd(x1, P, 0, need)

    nm1 = P["norm_mix"][1:2]
    need("ssm", x2)
    h1 = _rmsnorm(x2, nm1, name="ssm_norm_in")
    w1T, wdtT = P["ssm_w_inT"], P["ssm_wdtT"]
    DI, CD, NH = P["ssm_norm"].shape[1], P["ssm_conv_w"].shape[1], P["ssm_dt_bias"].shape[1]
    z = _mm(h1, w1T, tb=True, b_rows=(0, DI), name="ssm_in_z")
    xbcp = _mm(h1, w1T, tb=True, b_rows=(DI, CD), name="ssm_in_xbc")
    dtraw = _mm(h1, wdtT, tb=True, name="ssm_in_dt")
    xbc = _conv_silu_fwd(xbcp, P["ssm_conv_w"], P["ssm_conv_b"])
    bias_g, alog_g, d_g = _group_major(P["ssm_dt_bias"]), _group_major(P["ssm_A_log"]), _group_major(P["ssm_D"])
    pre_g, dt_g, acs_g, acst_g = _ssd_prep_fwd(dtraw, 0, bias_g, alog_g)
    y, states = _ssd_fwd(xbc, dt_g, acs_g, acst_g, d_g)
    yn = _gate_norm_fwd(y, z, P["ssm_norm"])
    need("ssm_out", yn)
    x3 = _mm(yn, P["ssm_w_out"], res=x2, name="ssm_out")
    x4, ffn1 = _ffn_fwd(x3, P, 1, need)

    loss, dx, d_norm_final = _final_loss(x4, P["norm_final"].reshape(1, D), target, name="final_loss")
    dx, dnf1, dcw1, dcb1 = _ffn_bwd(ffn1, P, 1, dx, emit)
    dyn = _mm(dx, P["ssm_w_out"], tb=True, out_dtype=BF16, name="ssm_out_dx")
    d_w_out1 = _mm(yn, dx, ta=True, out_dtype=PAYLOAD, name="ssm_out_dw")
    dy, dz, d_ssm_norm = _gate_norm_bwd(y, z, P["ssm_norm"], dyn)
    dxs, dB, dC, ddt_g, dacs_g, dacst_g, dd_g = _ssd_bwd(xbc, dt_g, acs_g, acst_g, d_g, states, dy)
    draw, dbias_g, dalog_g = _ssd_prep_bwd(pre_g, dt_g, alog_g, ddt_g, dacs_g, dacst_g)
    dxbc, d_conv_w1, d_conv_b1 = _conv_silu_bwd(xbcp, P["ssm_conv_w"], P["ssm_conv_b"], [dxs, dB, dC])
    rows = DI + CD + NH
    d_w1T = _mm(dz, h1, ta=True, out_dtype=PAYLOAD, out_rows=(rows, 0, None), name="ssm_in_z_dw")
    d_w1T = _mm(dxbc, h1, ta=True, out_dtype=PAYLOAD, out_rows=(rows, DI, d_w1T), name="ssm_in_xbc_dw")
    d_w1T = _mm(draw[:, :NH], h1, ta=True, out_dtype=PAYLOAD, out_rows=(rows, DI + CD, d_w1T), name="ssm_in_dt_dw")
    tie = emit("ssm", {"ssm_w_inT": d_w1T, "ssm_w_out": d_w_out1,
                       "ssm_conv_w": d_conv_w1, "ssm_conv_b": d_conv_b1, "ssm_norm": d_ssm_norm})
    dh1 = _mm(dz, w1T, b_rows=(0, DI), name="ssm_in_z_dx")
    dh1 = _mm(dxbc, w1T, b_rows=(DI, CD), res=dh1, name="ssm_in_xbc_dx")
    dx, dnm1 = _mm_norm_bwd(draw, wdtT, x2, nm1, dx, res=dh1, after=tie, name="ssm_in_dt_dx_norm_bwd")
    dx, dnf0, dcw0, dcb0 = _ffn_bwd(ffn0, P, 0, dx, emit)
    dcat = _mm(dx, P["mix_w_out"], tb=True, name="mix_out_dx")
    d_w_out0 = _mm(cat0, dx, ta=True, out_dtype=PAYLOAD, name="mix_out_dw")
    dproj0, dsk = _attn_bwd(proj0, cos, sin, P["attn_sinks"], dcat)
    dproj0, d_pool_w, d_pool_scale = _pool_bwd(proj0, P["pool_w"][0], P["pool_scale"], dcat, dproj0)
    d_w_in0 = _mm(dproj0, h0, ta=True, out_dtype=PAYLOAD, name="mix_in_dw")
    tie = emit("mix", {"mix_w_inT": d_w_in0, "mix_w_out": d_w_out0, "ffn_conv_w": jnp.stack([dcw0, dcw1])})
    grad_x, dnm0 = _mm_norm_bwd(dproj0, P["mix_w_inT"], x, nm0, dx, after=tie, name="mix_in_dx_norm_bwd")

    small = {
        "norm_mix": jnp.concatenate([dnm0, dnm1], axis=0),
        "norm_ffn": jnp.concatenate([dnf0, dnf1], axis=0),
        "norm_final": d_norm_final,
        "pool_w": d_pool_w,
        "pool_scale": d_pool_scale,
        "attn_sinks_rows": dsk,
        "ssm_dt_bias_g": dbias_g, "ssm_A_log_g": dalog_g, "ssm_D_g": dd_g,
        "ffn_conv_b": jnp.concatenate([dcb0, dcb1], axis=0),
    }
    return loss, grad_x, small


def _peer(k):
    x, y, c = lax.axis_index("x"), lax.axis_index("y"), lax.axis_index("c")
    px = 1 - x if k & 4 else x
    py = 1 - y if k & 2 else y
    pc = 1 - c if k & 1 else c
    return (px, py, pc), 4 * px + 2 * py + pc


def _my_index():
    return 4 * lax.axis_index("x") + 2 * lax.axis_index("y") + lax.axis_index("c")


def _land_sds(a, mode, gather):
    if mode == "slab":
        return _sds(((N_DEV,) + a.shape) if gather else a.shape, a.dtype)
    assert mode == "rows", mode
    return _sds((N_DEV * a.shape[0],) + a.shape[1:] if gather else (N_DEV, a.shape[0] // N_DEV) + a.shape[1:], a.dtype)


def _part(ref, mode, shape, idx):
    if mode == "slab":
        return ref.at[idx]
    r = shape[0] // N_DEV
    return ref.at[pl.ds(idx * r, r)]


def _remote_copies(ops, gather, srcs, lands, send_sems, recv_sems):
    me = _my_index()
    n = len(ops)
    out = []
    for k in range(1, N_DEV):
        dev, idx = _peer(k)
        for i, (a, mode) in enumerate(ops):
            s = srcs[i] if gather else _part(srcs[i], mode, a.shape, idx)
            d = _part(lands[i], mode, _land_sds(a, mode, gather).shape, me) if gather else lands[i].at[me]
            out.append(pltpu.make_async_remote_copy(src_ref=s, dst_ref=d, send_sem=send_sems.at[(k - 1) * n + i],
                                                    recv_sem=recv_sems.at[(k - 1) * n + i], device_id=dev,
                                                    device_id_type=pl.DeviceIdType.MESH))
    return out


HBM = pl.BlockSpec(memory_space=pltpu.HBM)
SEM = pl.BlockSpec(memory_space=pltpu.SEMAPHORE)
SIDE_EFFECT = pltpu.SideEffectType.DATAFLOW_SIDE_EFFECTING


def _in_hbm(a):
    return pltpu.with_memory_space_constraint(a, pltpu.HBM)


def _place_own(ops, *, gather, name):
    n = len(ops)

    def zeros(k):
        return (0,) * k

    in_specs, out_specs = [], []
    for a, mode in ops:
        nd = a.ndim
        if gather and mode == "slab":
            in_specs.append(pl.BlockSpec(a.shape, lambda i, nd=nd: zeros(nd)))
            out_specs.append(pl.BlockSpec((1,) + a.shape, lambda i, nd=nd: (_my_index(),) + zeros(nd)))
        elif gather:
            in_specs.append(pl.BlockSpec(a.shape, lambda i, nd=nd: zeros(nd)))
            out_specs.append(pl.BlockSpec(a.shape, lambda i, nd=nd: (_my_index(),) + zeros(nd - 1)))
        elif mode == "slab":
            in_specs.append(pl.BlockSpec((1,) + a.shape[1:], lambda i, nd=nd: (_my_index(),) + zeros(nd - 1)))
            out_specs.append(pl.BlockSpec((1,) + a.shape[1:], lambda i, nd=nd: (_my_index(),) + zeros(nd - 1)))
        else:
            r = a.shape[0] // N_DEV
            in_specs.append(pl.BlockSpec((r,) + a.shape[1:], lambda i, nd=nd: (_my_index(),) + zeros(nd - 1)))
            out_specs.append(pl.BlockSpec((1, r) + a.shape[1:], lambda i, nd=nd: (_my_index(),) + zeros(nd)))

    def body(*refs):
        for i_ref, o_ref in zip(refs[:n], refs[n:2 * n]):
            if o_ref.shape == i_ref.shape:
                o_ref[...] = i_ref[...]
            else:
                o_ref[0] = i_ref[...]

    outs = _call(body, name=name, grid=(1,), in_specs=in_specs, out_specs=out_specs + [ANY] * n,
                 out_shape=[_land_sds(a, m, gather) for a, m in ops] + [_sds(a.shape, a.dtype) for a, _ in ops],
                 aliases={i: n + i for i in range(n)})(*[a for a, _ in ops])
    return outs[:n], [(src, m) for src, (_, m) in zip(outs[n:], ops)]


def _exchange_start(groups, *, gather, name):
    sizes = [len(ops) for ops, _ in groups]
    n = sum(sizes)
    G = len(groups)

    def body(*refs):
        srcs, lands = refs[:n], refs[n:2 * n]
        sems = refs[2 * n:2 * n + 2 * G]
        token = refs[-1]
        off = 0
        for g, (ops, _) in enumerate(groups):
            for cp in _remote_copies(ops, gather, srcs[off:off + sizes[g]], lands[off:off + sizes[g]], sems[2 * g], sems[2 * g + 1]):
                cp.start()
            off += sizes[g]
        token[...] = jnp.zeros_like(token)

    srcs = [a for ops, _ in groups for a, _ in ops]
    lands = [l for _, ls in groups for l in ls]
    sem_shapes = [pltpu.SemaphoreType.DMA((s * (N_DEV - 1),)) for s in sizes for _ in range(2)]
    outs = pl.pallas_call(
        body, name=name,
        out_shape=sem_shapes + [pltpu.HBM(a.shape, a.dtype) for a in srcs + lands] + [_sds((8, LANES))],
        in_specs=[HBM] * (2 * n), out_specs=[SEM] * (2 * G) + [HBM] * (2 * n) + [pl.BlockSpec(memory_space=pltpu.VMEM)],
        input_output_aliases={i: 2 * G + i for i in range(2 * n)},
        compiler_params=pltpu.CompilerParams(has_side_effects=SIDE_EFFECT))(*[_in_hbm(a) for a in srcs + lands])
    sems, thru, token = outs[:2 * G], outs[2 * G:2 * G + 2 * n], outs[-1]
    states, off = [], 0
    for g, s in enumerate(sizes):
        states.append((sems[2 * g], sems[2 * g + 1], thru[off:off + s], thru[n + off:n + off + s]))
        off += s
    return states, token


def _exchange_wait(ops, state, after, *, gather, name):
    send_sems, recv_sems, srcs, lands = state
    n = len(ops)

    def body(*refs):
        for cp in _remote_copies(ops, gather, refs[:n], refs[n:2 * n], refs[2 * n], refs[2 * n + 1]):
            cp.wait_send()
            cp.wait_recv()

    outs = pl.pallas_call(
        body, name=name, out_shape=[pltpu.HBM(a.shape, a.dtype) for a in list(srcs) + list(lands)],
        in_specs=[HBM] * (2 * n) + [SEM, SEM, ANY], out_specs=[HBM] * (2 * n),
        input_output_aliases={i: i for i in range(2 * n)},
        compiler_params=pltpu.CompilerParams(has_side_effects=SIDE_EFFECT))(*srcs, *lands, send_sems, recv_sems, after)
    return outs[n:]


ADAM_ROWS = 256


def _row_tile(R, cap=ADAM_ROWS):
    best = R
    if R > cap:
        for d in range(16, cap + 1, 16):
            if R % d == 0:
                best = d
    return best


def _adamw(g_layers, w, m, v, *, name):
    L = len(g_layers)
    J, R, Wd = g_layers[0].shape
    assert w.shape == (L, R, Wd), (g_layers[0].shape, w.shape)
    tr = _row_tile(R)
    nrt = R // tr
    c1 = 1.0 / (1.0 - ADAM_B1 ** ADAM_STEP)
    c2 = 1.0 / (1.0 - ADAM_B2 ** ADAM_STEP)

    def body(*refs):
        g_refs = refs[:L]
        w_ref, m_ref, v_ref, go_ref, d_ref, mo_ref, vo_ref = refs[L:]
        layer = pl.program_id(0)
        g = None
        for l, g_ref in enumerate(g_refs):
            gl = g_ref[0].astype(F32)
            for j in range(1, J):
                gl = gl + g_ref[j].astype(F32)
            g = gl if g is None else jnp.where(layer == l, gl, g)
        mn = ADAM_B1 * m_ref[...] + (1.0 - ADAM_B1) * g
        vn = ADAM_B2 * v_ref[...] + (1.0 - ADAM_B2) * (g * g)
        go_ref[...] = g
        mo_ref[...] = mn
        vo_ref[...] = vn
        d_ref[...] = -ADAM_LR * ((mn * c1) / (jnp.sqrt(vn * c2) + ADAM_EPS) + ADAM_WD * w_ref[...])

    def g_spec(l):
        return pl.BlockSpec((J, tr, Wd), lambda ll, i: (0, jnp.where(ll == l, i, jnp.where(ll < l, 0, nrt - 1)), 0))

    row = pl.BlockSpec((None, tr, Wd), lambda ll, i: (ll, i, 0))
    out = _sds((L, R, Wd))
    return _call(body, name=name, out_shape=(out, out, out, out), grid=(L, nrt),
                 in_specs=[g_spec(l) for l in range(L)] + [row, row, row], out_specs=(row, row, row, row))(*g_layers, w, m, v)


def _sum_slabs(slabs, *, name):
    n = len(slabs)

    def body(*refs):
        for g_ref, o_ref in zip(refs[:n], refs[n:]):
            g = g_ref[0]
            for j in range(1, g_ref.shape[0]):
                g = g + g_ref[j]
            o_ref[...] = g

    return _call(body, name=name, out_shape=[_sds(s.shape[1:]) for s in slabs])(*slabs)


def kernel(x, positions, norm_mix, norm_ffn, norm_final, mix_w_in, pool_w, pool_scale, attn_sinks, mix_w_out, ssm_w_in, ssm_conv_w, ssm_conv_b, ssm_dt_bias, ssm_A_log, ssm_D, ssm_norm, ssm_w_out, ffn_w_up, ffn_conv_w, ffn_conv_b, ffn_w_down, loss_target, m_norm_mix, m_norm_ffn, m_norm_final, m_mix_w_in, m_pool_w, m_pool_scale, m_attn_sinks, m_mix_w_out, m_ssm_w_in, m_ssm_conv_w, m_ssm_conv_b, m_ssm_dt_bias, m_ssm_A_log, m_ssm_D, m_ssm_norm, m_ssm_w_out, m_ffn_w_up, m_ffn_conv_w, m_ffn_conv_b, m_ffn_w_down, v_norm_mix, v_norm_ffn, v_norm_final, v_mix_w_in, v_pool_w, v_pool_scale, v_attn_sinks, v_mix_w_out, v_ssm_w_in, v_ssm_conv_w, v_ssm_conv_b, v_ssm_dt_bias, v_ssm_A_log, v_ssm_D, v_ssm_norm, v_ssm_w_out, v_ffn_w_up, v_ffn_conv_w, v_ffn_conv_b, v_ffn_w_down):
    args = dict(locals())
    wl = {n: args[n] for n in WEIGHTS}
    ml = {n: args["m_" + n] for n in WEIGHTS}
    vl = {n: args["v_" + n] for n in WEIGHTS}
    F = ffn_w_down.shape[1] * N_DEV
    DI, CD, NH = ssm_norm.shape[1] * N_DEV, ssm_conv_b.shape[1] * N_DEV, ssm_dt_bias.shape[1]
    Kc, Kf = ssm_conv_w.shape[1], ffn_conv_w.shape[1]
    n_up = ffn_w_up.shape[2]
    col_sharded = ("mix_w_in", "ssm_w_in", "ffn_w_up")

    def tr(a):
        return jnp.swapaxes(a, -1, -2)

    def two(a):
        return a.reshape(-1, a.shape[-1])

    def pay(a):
        return a.astype(PAYLOAD)

    order = ("mix_in", "mix_out", "ffn0_up", "ffn0_down", "ssm", "ssm_out", "ffn1_up", "ffn1_down")
    gops = {
        "mix_in": [(pay(tr(mix_w_in)[0]), "rows")],
        "mix_out": [(pay(mix_w_out[0]), "rows"), (two(ssm_conv_w), "slab"), (ssm_conv_b, "slab"), (ssm_norm, "slab"),
                    (two(ffn_conv_w), "slab")],
        "ffn0_up": [(pay(tr(ffn_w_up)[0]), "rows")], "ffn0_down": [(pay(ffn_w_down[0]), "rows")],
        "ssm": [(pay(tr(ssm_w_in)[0]), "slab")], "ssm_out": [(pay(ssm_w_out[0]), "rows")],
        "ffn1_up": [(pay(tr(ffn_w_up)[1]), "rows")], "ffn1_down": [(pay(ffn_w_down[1]), "rows")],
    }
    lands, handed = _place_own([op for g in order for op in gops[g]], gather=True, name="gather_own")
    groups, off = [], 0
    for g in order:
        gops[g] = handed[off:off + len(gops[g])]
        groups.append((gops[g], lands[off:off + len(gops[g])]))
        off += len(gops[g])
    gstates, token = _exchange_start(groups, gather=True, name="gather_start")
    gstate = dict(zip(order, gstates))
    P = {n: wl[n] for n in REPLICATED}

    def need(g, after):
        got = _exchange_wait(gops[g], gstate[g], after, gather=True, name="gather_wait_" + g)
        if g == "mix_in":
            P["mix_w_inT"] = got[0]
        elif g == "mix_out":
            P.update(mix_w_out=got[0], ssm_conv_w=got[1].transpose(1, 0, 2).reshape(Kc, CD), ssm_conv_b=got[2].reshape(1, CD),
                     ssm_norm=got[3].reshape(1, DI), ffn_conv_w=got[4].transpose(1, 0, 2).reshape(2, Kf, 2 * F))
        elif g == "ssm":
            w1T = got[0].reshape(-1, got[0].shape[-1])
            P.update(ssm_w_inT=w1T, ssm_wdtT=jnp.pad(w1T[DI + CD:], ((0, LANES - NH), (0, 0))))
        elif g == "ssm_out":
            P["ssm_w_out"] = got[0]
        elif g.endswith("_up"):
            P["ffn_w_upT" + g[3]] = got[0]
        else:
            P["ffn_w_down" + g[3]] = got[0]

    sent = {}

    def emit(g, d):
        if g == "mix":
            ops = [(d["mix_w_inT"], "rows"), (d["mix_w_out"], "rows"),
                   (d["ffn_conv_w"].reshape(2 * Kf, N_DEV, n_up).transpose(1, 0, 2), "slab")]
        elif g == "ssm":
            ops = [(d["ssm_w_inT"].reshape(N_DEV, -1, d["ssm_w_inT"].shape[-1]), "slab"), (d["ssm_w_out"], "rows"),
                   (d["ssm_conv_w"].reshape(Kc, N_DEV, -1).transpose(1, 0, 2), "slab"),
                   (d["ssm_conv_b"].reshape(N_DEV, 1, -1), "slab"), (d["ssm_norm"].reshape(N_DEV, 1, -1), "slab")]
        else:
            ops = [(d["ffn_w_upT"], "rows"), (d["ffn_w_down"], "rows")]
        own, ops = _place_own(ops, gather=False, name="scatter_own_" + g)
        (state,), tok = _exchange_start([(ops, own)], gather=False, name="scatter_start_" + g)
        sent[g] = (ops, state)
        return tok

    loss_lanes, grad_x, G = _local_step(x[0], positions.reshape(-1, 1), loss_target[0], P, need, emit, after=token)
    loss = lax.psum(loss_lanes[0, 0], ("x", "y", "c"))

    res = {}

    def update(n, g_layers):
        L = len(g_layers)
        g_layers = [g.reshape(g.shape[0], -1, g.shape[-1]) for g in g_layers]
        shape = (L,) + g_layers[0].shape[1:]
        view = tr if n in col_sharded else (lambda a: a)
        outs = _adamw(g_layers, view(wl[n]).reshape(shape), view(ml[n]).reshape(shape), view(vl[n]).reshape(shape),
                      name="adamw_" + n)
        for kind, a in zip(("grad", "delta", "new_m", "new_v"), outs):
            res[kind, n] = view(a.reshape(view(wl[n]).shape))

    rep_ops = [(a, "slab") for a in (G["norm_mix"], G["norm_ffn"], G["norm_final"], G["pool_w"].reshape(-1, LANES),
                                     G["pool_scale"], G["ffn_conv_b"], G["attn_sinks_rows"],
                                     G["ssm_dt_bias_g"].reshape(SSM_G, LANES), G["ssm_A_log_g"].reshape(SSM_G, LANES),
                                     G["ssm_D_g"].reshape(SSM_G, LANES))]
    rep_own, rep_ops = _place_own(rep_ops, gather=True, name="small_own")
    (rep_state,), rep_token = _exchange_start([(rep_ops, rep_own)], gather=True, name="small_start")

    recv = {g: _exchange_wait(sent[g][0], sent[g][1], rep_token, gather=False, name="scatter_wait_" + g)
            for g in ("ffn1", "ssm", "ffn0")}
    update("ssm_w_in", [recv["ssm"][0]])
    update("ssm_w_out", [recv["ssm"][1]])
    update("ssm_conv_w", [recv["ssm"][2]])
    update("ssm_conv_b", [recv["ssm"][3]])
    update("ssm_norm", [recv["ssm"][4]])
    update("ffn_w_up", [recv["ffn0"][0], recv["ffn1"][0]])
    update("ffn_w_down", [recv["ffn0"][1], recv["ffn1"][1]])
    recv["mix"] = _exchange_wait(sent["mix"][0], sent["mix"][1], res["new_v", "ffn_w_down"], gather=False,
                                 name="scatter_wait_mix")
    update("mix_w_in", [recv["mix"][0]])
    update("mix_w_out", [recv["mix"][1]])
    update("ffn_conv_w", [recv["mix"][2]])

    rep = _exchange_wait(rep_ops, rep_state, res["new_v", "mix_w_out"], gather=True, name="small_wait")
    for n, r in zip(("norm_mix", "norm_ffn", "norm_final", "pool_w", "pool_scale", "ffn_conv_b"), rep):
        update(n, [r])
    sinks_rows, bias_g, alog_g, d_g = _sum_slabs(rep[6:], name="sum_head_grads")
    update("attn_sinks", [sinks_rows[:, 0].reshape(1, 1, N_HEADS)])
    update("ssm_dt_bias", [_ungroup(bias_g)[None]])
    update("ssm_A_log", [_ungroup(alog_g)[None]])
    update("ssm_D", [_ungroup(d_g)[None]])

    return (loss, grad_x[None], *[res[k, n] for k in ("grad", "delta", "new_m", "new_v") for n in WEIGHTS])
```

```python
import functools
import math

import jax
import jax.numpy as jnp
from jax import lax
from jax.experimental import pallas as pl
from jax.experimental.pallas import tpu as pltpu

F32 = jnp.float32
BF16 = jnp.bfloat16

N_DEV = 8
LANES = 128
HEAD_DIM = 64
N_KV_HEADS = 2
GQ = 4
N_HEADS = N_KV_HEADS * GQ
BLOCK = 128
POOL_GROUPS = 4
ROPE_THETA = 10000.0
SSM_P = 64
SSM_G = 8
SSM_R = 4
SSM_N = 128
SSM_L = 128
NORM_EPS = 1e-6
SSM_NORM_EPS = 1e-5
ADAM_LR, ADAM_B1, ADAM_B2, ADAM_EPS, ADAM_WD, ADAM_STEP = 0.001, 0.9, 0.999, 1e-08, 0.01, 10
VMEM_LIMIT = 56 * 2 ** 20
PAYLOAD = jnp.bfloat16

REPLICATED = ("norm_mix", "norm_ffn", "norm_final", "pool_w", "pool_scale", "attn_sinks",
              "ssm_dt_bias", "ssm_A_log", "ssm_D", "ffn_conv_b")
WEIGHTS = ("norm_mix", "norm_ffn", "norm_final", "mix_w_in", "pool_w", "pool_scale", "attn_sinks", "mix_w_out",
           "ssm_w_in", "ssm_conv_w", "ssm_conv_b", "ssm_dt_bias", "ssm_A_log", "ssm_D", "ssm_norm", "ssm_w_out",
           "ffn_w_up", "ffn_conv_w", "ffn_conv_b", "ffn_w_down")


def _tile(n, cap):
    if n <= cap:
        return n
    best = None
    for d in range(LANES, cap + 1, LANES):
        if n % d == 0:
            best = d
    assert best is not None, (n, cap)
    return best


def _call(body, *, name, out_shape, grid=None, in_specs=None, out_specs=None, scratch=(), aliases=None):
    kw = {}
    if grid is not None:
        kw = dict(grid=grid, in_specs=in_specs, out_specs=out_specs)
    if aliases:
        kw["input_output_aliases"] = aliases
    return pl.pallas_call(
        body, name=name, out_shape=out_shape, scratch_shapes=list(scratch),
        compiler_params=pltpu.CompilerParams(vmem_limit_bytes=VMEM_LIMIT), **kw)


ANY = pl.BlockSpec(memory_space=pl.ANY)


def _sds(shape, dtype=F32):
    return jax.ShapeDtypeStruct(tuple(shape), dtype)


def _sigmoid(x):
    return 1.0 / (1.0 + jnp.exp(-x))


def _shift_dn(x, d, t):
    if d == 0:
        return x
    return jnp.where(t >= d, pltpu.roll(x, d, axis=0), 0.0)


def _shift_up(x, d, t):
    if d == 0:
        return x
    n = x.shape[0]
    return jnp.where(t < n - d, pltpu.roll(x, n - d, axis=0), 0.0)


def _mm(a, b, *, name, ta=False, tb=False, res=None, out_dtype=F32, b_rows=None, out_rows=None):
    M, K = (a.shape[1], a.shape[0]) if ta else a.shape
    b0, bn = b_rows if b_rows is not None else (0, b.shape[0])
    N = bn if tb else b.shape[1]
    assert (b.shape[1] if tb else bn) == K, (a.shape, b.shape, ta, tb, b_rows)
    tm, tn, tk = _tile(M, 1408), _tile(N, 1408), _tile(K, 1408)
    nk = K // tk
    dims = (((0 if ta else 1,), (1 if tb else 0,)), ((), ()))
    aliased = out_rows is not None and out_rows[2] is not None

    def body(*refs):
        a_ref, b_ref = refs[:2]
        r_ref = refs[2] if res is not None else None
        o_ref, acc = refs[-2:]
        k = pl.program_id(2)

        @pl.when(k == 0)
        def _():
            acc[...] = jnp.zeros_like(acc)

        acc[...] += lax.dot_general(a_ref[...].astype(BF16), b_ref[...].astype(BF16), dims,
                                    preferred_element_type=F32)

        @pl.when(k == nk - 1)
        def _():
            out = acc[...]
            if res is not None:
                out = out + r_ref[...]
            o_ref[...] = out.astype(out_dtype)

    a_spec = pl.BlockSpec((tk, tm), lambda i, j, k: (k, i)) if ta else pl.BlockSpec((tm, tk), lambda i, j, k: (i, k))
    if tb:
        assert b0 % tn == 0, (b_rows, tn)
        b_spec = pl.BlockSpec((tn, tk), lambda i, j, k: (b0 // tn + j, k))
    else:
        assert b0 % tk == 0, (b_rows, tk)
        b_spec = pl.BlockSpec((tk, tn), lambda i, j, k: (b0 // tk + k, j))
    ins, specs = [a, b], [a_spec, b_spec]
    if res is not None:
        ins.append(res)
        specs.append(pl.BlockSpec((tm, tn), lambda i, j, k: (i, j)))
    aliases = None
    if out_rows is None:
        o_spec = pl.BlockSpec((tm, tn), lambda i, j, k: (i, j))
        out_shape = _sds((M, N), out_dtype)
    else:
        total, o0, prev = out_rows
        assert o0 % tm == 0, (out_rows, tm)
        o_spec = pl.BlockSpec((tm, tn), lambda i, j, k: (o0 // tm + i, j))
        out_shape = _sds((total, N), out_dtype)
        if aliased:
            aliases = {len(ins): 0}
            ins.append(prev)
            specs.append(ANY)
    return _call(body, name=name, out_shape=out_shape, grid=(M // tm, N // tn, nk), in_specs=specs,
                 out_specs=o_spec, scratch=[pltpu.VMEM((tm, tn), F32)], aliases=aliases)(*ins)


def _rmsnorm(x, w, *, name, eps=NORM_EPS, after=None):
    S, D = x.shape
    tm = _tile(S, 512)
    tie = [] if after is None else [after]

    def body(x_ref, w_ref, *rest):
        o_ref = rest[-1]
        xf = x_ref[...]
        r = lax.rsqrt(jnp.mean(xf * xf, axis=-1, keepdims=True) + eps)
        o_ref[...] = (xf * r * w_ref[...]).astype(BF16)

    return _call(body, name=name, out_shape=_sds((S, D), BF16), grid=(S // tm,),
                 in_specs=[pl.BlockSpec((tm, D), lambda i: (i, 0)), pl.BlockSpec((1, D), lambda i: (0, 0))] + [ANY] * len(tie),
                 out_specs=pl.BlockSpec((tm, D), lambda i: (i, 0)))(x, w, *tie)


def _norm_bwd_math(xf, w, dh, eps):
    r = lax.rsqrt(jnp.mean(xf * xf, axis=-1, keepdims=True) + eps)
    xhat = xf * r
    dxh = dh * w
    dx = r * (dxh - xhat * jnp.mean(dxh * xhat, axis=-1, keepdims=True))
    dw = jnp.sum(dh * xhat, axis=0, keepdims=True)
    return dx, dw


def _mm_norm_bwd(a, b, x, w, dres, *, name, res=None, b_rows=None, after=None, eps=NORM_EPS):
    M, K = a.shape
    b0, bn = b_rows if b_rows is not None else (0, b.shape[0])
    D = b.shape[1]
    assert bn == K and x.shape == (M, D), (a.shape, b.shape, b_rows, x.shape)
    tm, tk = _tile(M, 512), _tile(K, 1408)
    assert b0 % tk == 0, (b_rows, tk)
    nk = K // tk

    def body(*refs):
        a_ref, b_ref, x_ref, w_ref, dr_ref = refs[:5]
        r_ref = refs[5] if res is not None else None
        dx_ref, dw_ref, acc = refs[-3:]
        i, k = pl.program_id(0), pl.program_id(1)

        @pl.when(k == 0)
        def _():
            acc[...] = jnp.zeros_like(acc)

        @pl.when((i == 0) & (k == 0))
        def _():
            dw_ref[...] = jnp.zeros_like(dw_ref)

        acc[...] += jnp.dot(a_ref[...].astype(BF16), b_ref[...].astype(BF16), preferred_element_type=F32)

        @pl.when(k == nk - 1)
        def _():
            dh = acc[...] if res is None else acc[...] + r_ref[...]
            dx, dw = _norm_bwd_math(x_ref[...], w_ref[...], dh, eps)
            dx_ref[...] = dr_ref[...] + dx
            dw_ref[...] += dw

    row = pl.BlockSpec((tm, D), lambda i, k: (i, 0))
    vec = pl.BlockSpec((1, D), lambda i, k: (0, 0))
    ins = [a, b, x, w, dres]
    specs = [pl.BlockSpec((tm, tk), lambda i, k: (i, k)), pl.BlockSpec((tk, D), lambda i, k: (b0 // tk + k, 0)), row, vec, row]
    if res is not None:
        ins.append(res)
        specs.append(row)
    if after is not None:
        ins.append(after)
        specs.append(ANY)
    return _call(body, name=name, out_shape=(_sds((M, D)), _sds((1, D))), grid=(M // tm, nk), in_specs=specs,
                 out_specs=(row, vec), scratch=[pltpu.VMEM((tm, D), F32)])(*ins)


def _final_loss(x, w, target, *, name):
    S, D = x.shape
    tm = _tile(S, 512)

    def body(x_ref, w_ref, t_ref, loss_ref, dx_ref, dw_ref):
        xf, wv = x_ref[...], w_ref[...]
        r = lax.rsqrt(jnp.mean(xf * xf, axis=-1, keepdims=True) + NORM_EPS)
        err = xf * r * wv - t_ref[...]
        part = 0.5 * jnp.sum(jnp.mean(err * err, axis=-1, keepdims=True), axis=0, keepdims=True)
        dx, dw = _norm_bwd_math(xf, wv, err * (1.0 / D), NORM_EPS)
        dx_ref[...] = dx

        @pl.when(pl.program_id(0) == 0)
        def _():
            dw_ref[...] = jnp.zeros_like(dw_ref)
            loss_ref[...] = jnp.zeros_like(loss_ref)

        dw_ref[...] += dw
        loss_ref[...] += jnp.broadcast_to(part, loss_ref.shape)

    row = pl.BlockSpec((tm, D), lambda i: (i, 0))
    vec = pl.BlockSpec((1, D), lambda i: (0, 0))
    return _call(body, name=name, out_shape=(_sds((1, LANES)), _sds((S, D)), _sds((1, D))), grid=(S // tm,),
                 in_specs=[row, vec, row], out_specs=(pl.BlockSpec((1, LANES), lambda i: (0, 0)), row, vec))(x, w, target)


def _rope_tables(pos, inv_freq):
    S = pos.shape[0]
    tm = _tile(S, 512)

    def body(p_ref, f_ref, c_ref, s_ref):
        ang = p_ref[...].astype(F32) * f_ref[...]
        c_ref[...] = jnp.cos(ang)
        s_ref[...] = jnp.sin(ang)

    blk = pl.BlockSpec((tm, LANES), lambda i: (i, 0))
    return _call(body, name="rope_tables", out_shape=(_sds((S, LANES)), _sds((S, LANES))), grid=(S // tm,),
                 in_specs=[pl.BlockSpec((tm, 1), lambda i: (i, 0)), pl.BlockSpec((1, LANES), lambda i: (0, 0))],
                 out_specs=(blk, blk))(pos, inv_freq)


def _rot_half(t):
    lane = lax.broadcasted_iota(jnp.int32, t.shape, 1)
    lo = (lane % HEAD_DIM) < (HEAD_DIM // 2)
    return jnp.where(lo, -pltpu.roll(t, LANES - HEAD_DIM // 2, axis=1), pltpu.roll(t, HEAD_DIM // 2, axis=1))


def _rope(t, c, s):
    return t * c + _rot_half(t) * s


def _unrope(dy, c, s):
    return dy * c - _rot_half(dy * s)


PD = POOL_GROUPS * LANES
QD = N_HEADS * HEAD_DIM
KD = N_KV_HEADS * HEAD_DIM
assert PD % QD == 0 and (PD + QD) % (2 * KD) == 0 and KD == LANES
def _attn_probs(q, kcat, sink, mask):
    s = lax.dot_general(q.astype(BF16), kcat, (((1,), (1,)), ((), ())), preferred_element_type=F32) * (HEAD_DIM ** -0.5)
    s = jnp.where(mask, s, -jnp.inf)
    m = jnp.maximum(jnp.max(s, axis=1, keepdims=True), sink)
    p = jnp.exp(s - m)
    ps = jnp.exp(sink - m)
    inv = 1.0 / (jnp.sum(p, axis=1, keepdims=True) + ps)
    return p * inv, ps * inv


def _attn_mask(n):
    qi = lax.broadcasted_iota(jnp.int32, (BLOCK, 2 * BLOCK), 0)
    kj = lax.broadcasted_iota(jnp.int32, (BLOCK, 2 * BLOCK), 1)
    rel = qi + BLOCK - kj
    return (rel >= 0) & (rel < BLOCK) & ((n > 0) | (kj >= BLOCK))


def _attn_in_specs(nb):
    def cur(n):
        return jnp.minimum(n, nb - 1)

    def prev(n):
        return jnp.clip(n - 1, 0, nb - 1)

    kvb = (PD + QD) // (2 * KD)
    return [pl.BlockSpec(memory_space=pltpu.SMEM),
            pl.BlockSpec((BLOCK, QD), lambda n: (cur(n), PD // QD)),
            pl.BlockSpec((BLOCK, 2 * KD), lambda n: (cur(n), kvb)),
            pl.BlockSpec((BLOCK, 2 * KD), lambda n: (prev(n), kvb)),
            pl.BlockSpec((BLOCK, LANES), lambda n: (cur(n), 0)), pl.BlockSpec((BLOCK, LANES), lambda n: (cur(n), 0)),
            pl.BlockSpec((BLOCK, LANES), lambda n: (prev(n), 0)), pl.BlockSpec((BLOCK, LANES), lambda n: (prev(n), 0))]


def _attn_keys(kvc_ref, kvp_ref, cc, sc, cp, sp):
    kc = _rope(kvc_ref[:, :KD], cc, sc)
    kp = _rope(kvp_ref[:, :KD], cp, sp)
    vc, vp = kvc_ref[:, KD:], kvp_ref[:, KD:]
    kcat, vcat = [], []
    for kk in range(N_KV_HEADS):
        sl = slice(kk * HEAD_DIM, (kk + 1) * HEAD_DIM)
        kcat.append(jnp.concatenate([kp[:, sl], kc[:, sl]], axis=0).astype(BF16))
        vcat.append(jnp.concatenate([vp[:, sl], vc[:, sl]], axis=0).astype(BF16))
    return kcat, vcat


def _attn_fwd(proj, cos, sin, sinks, cat):
    S = proj.shape[0]
    nb = S // BLOCK

    def body(sink_ref, q_ref, kvc_ref, kvp_ref, cc_ref, sc_ref, cp_ref, sp_ref, cat_ref, o_ref):
        n = pl.program_id(0)
        cc, sc = cc_ref[...], sc_ref[...]
        kcat, vcat = _attn_keys(kvc_ref, kvp_ref, cc, sc, cp_ref[...], sp_ref[...])
        mask = _attn_mask(n)
        def head_pair(j):
            qr = _rope(q_ref[:, j * LANES:(j + 1) * LANES], cc, sc)
            for e in range(LANES // HEAD_DIM):
                yield
                h = j * (LANES // HEAD_DIM) + e
                pn, _ = _attn_probs(qr[:, e * HEAD_DIM:(e + 1) * HEAD_DIM], kcat[h // GQ], sink_ref[0, h], mask)
                yield
                o_ref[:, h * HEAD_DIM:(h + 1) * HEAD_DIM] = jnp.dot(
                    pn.astype(BF16), vcat[h // GQ], preferred_element_type=F32).astype(o_ref.dtype)

        _interleave([head_pair(j) for j in range(QD // LANES)])

    return _call(body, name="attn_fwd", out_shape=_sds(cat.shape, cat.dtype), grid=(nb,),
                 in_specs=_attn_in_specs(nb) + [ANY], out_specs=pl.BlockSpec((BLOCK, QD), lambda n: (n, PD // QD)),
                 aliases={8: 0})(sinks, proj, proj, proj, cos, sin, cos, sin, cat)


def _attn_bwd(proj, cos, sin, sinks, dcat):
    S = proj.shape[0]
    nb = S // BLOCK
    scale = HEAD_DIM ** -0.5
    per = LANES // HEAD_DIM

    def body(sink_ref, q_ref, kvc_ref, kvp_ref, cc_ref, sc_ref, cp_ref, sp_ref, do_ref, o_ref, ds_ref, hold, carry, part, pair):
        n = pl.program_id(0)

        @pl.when(n == 0)
        def _():
            hold[...] = jnp.zeros_like(hold)
            carry[...] = jnp.zeros_like(carry)
            ds_ref[...] = jnp.zeros_like(ds_ref)

        live = jnp.where(n < nb, 1.0, 0.0)
        cc, sc, cp, sp = cc_ref[...], sc_ref[...], cp_ref[...], sp_ref[...]
        kcat, vcat = _attn_keys(kvc_ref, kvp_ref, cc, sc, cp, sp)
        mask = _attn_mask(n)
        o_ref[:, :PD] = jnp.zeros((BLOCK, PD), F32)
        o_ref[:, PD:PD + QD] = hold[...]
        dk = [jnp.zeros((2 * BLOCK, HEAD_DIM), F32) for _ in range(N_KV_HEADS)]
        dv = [jnp.zeros((2 * BLOCK, HEAD_DIM), F32) for _ in range(N_KV_HEADS)]
        row = lax.broadcasted_iota(jnp.int32, (8, LANES), 0)
        acc = {"dsk": jnp.zeros((8, LANES), F32)}

        def head_pair(j):
            qr = _rope(q_ref[:, j * LANES:(j + 1) * LANES], cc, sc)
            for e in range(per):
                yield
                h = j * per + e
                kk = h // GQ
                qh = qr[:, e * HEAD_DIM:(e + 1) * HEAD_DIM]
                pn, psn = _attn_probs(qh, kcat[kk], sink_ref[0, h], mask)
                yield
                doh = (do_ref[:, h * HEAD_DIM:(h + 1) * HEAD_DIM] * live).astype(BF16)
                dp = lax.dot_general(doh, vcat[kk], NT, preferred_element_type=F32)
                yield
                delta = jnp.sum(pn * dp, axis=1, keepdims=True)
                ds = (pn * (dp - delta) * scale).astype(BF16)
                pair[j, :, e * HEAD_DIM:(e + 1) * HEAD_DIM] = jnp.dot(ds, kcat[kk], preferred_element_type=F32)
                yield
                dk[kk] = dk[kk] + lax.dot_general(ds, qh.astype(BF16), TN, preferred_element_type=F32)
                dv[kk] = dv[kk] + lax.dot_general(pn.astype(BF16), doh, TN, preferred_element_type=F32)
                acc["dsk"] = acc["dsk"] + jnp.where(row == h, -jnp.sum(psn * delta), 0.0)
            yield
            hold[:, j * LANES:(j + 1) * LANES] = _unrope(pair[j], cc, sc)

        _interleave([head_pair(j) for j in range(QD // LANES)])
        dsk = acc["dsk"]
        for kk in range(N_KV_HEADS):
            sl = slice(kk * HEAD_DIM, (kk + 1) * HEAD_DIM)
            sv = slice(KD + kk * HEAD_DIM, KD + (kk + 1) * HEAD_DIM)
            part[0, :, sl] = dk[kk][:BLOCK]
            part[0, :, sv] = dv[kk][:BLOCK]
            part[1, :, sl] = dk[kk][BLOCK:]
            part[1, :, sv] = dv[kk][BLOCK:]
        done = carry[...] + part[0]
        o_ref[:, PD + QD:PD + QD + KD] = _unrope(done[:, :KD], cp, sp)
        o_ref[:, PD + QD + KD:] = done[:, KD:]
        carry[...] = part[1]
        ds_ref[...] += dsk

    return _call(body, name="attn_bwd", out_shape=(_sds((S, PD + QD + 2 * KD)), _sds((8, LANES))), grid=(nb + 1,),
                 in_specs=_attn_in_specs(nb) + [pl.BlockSpec((BLOCK, QD), lambda n: (jnp.minimum(n, nb - 1), PD // QD))],
                 out_specs=(pl.BlockSpec((BLOCK, PD + QD + 2 * KD), lambda n: (jnp.maximum(n - 1, 0), 0)),
                            pl.BlockSpec((8, LANES), lambda n: (0, 0))),
                 scratch=[pltpu.VMEM((BLOCK, QD), F32), pltpu.VMEM((BLOCK, 2 * KD), F32),
                          pltpu.VMEM((2, BLOCK, 2 * KD), F32), pltpu.VMEM((QD // LANES, BLOCK, LANES), F32)])(
                     sinks, proj, proj, proj, cos, sin, cos, sin, dcat)


def _pool_sums(u, g, t, shift):
    s2 = u + shift(u, 1, t)
    s4 = s2 + shift(s2, 2, t)
    s8 = s4 + shift(s4, 4, t)
    s16 = s8 + shift(s8, 8, t)
    return jnp.where(g == 0, s2, jnp.where(g == 1, s4, jnp.where(g == 2, s8, s16)))


def _pool_specs(S):
    col = pl.BlockSpec((S, LANES), lambda g: (0, g))
    wsp = pl.BlockSpec((1, LANES, LANES), lambda g: (g, 0, 0))
    vec = pl.BlockSpec((1, LANES), lambda g: (0, g))
    return col, wsp, vec


def _pool_fwd(proj, pool_w, scale):
    S = proj.shape[0]
    col, wsp, vec = _pool_specs(S)

    def body(u_ref, w_ref, s_ref, o_ref):
        g = pl.program_id(0)
        u = u_ref[...]
        t = lax.broadcasted_iota(jnp.int32, u.shape, 0)
        cnt = jnp.minimum(t + 1, 2 << g).astype(F32)
        pm = _pool_sums(u, g, t, _shift_dn) / cnt - u
        o_ref[...] = (jnp.dot(pm.astype(BF16), w_ref[0].astype(BF16), preferred_element_type=F32) * s_ref[...]).astype(BF16)

    return _call(body, name="pool_fwd", out_shape=_sds((S, PD + QD), BF16), grid=(POOL_GROUPS,),
                 in_specs=[col, wsp, vec], out_specs=col)(proj, pool_w, scale)


def _pool_bwd(proj, pool_w, scale, dcat, dproj):
    S = proj.shape[0]
    col, wsp, vec = _pool_specs(S)

    def body(u_ref, w_ref, s_ref, d_ref, dproj_ref, du_ref, dw_ref, dsc_ref):
        g = pl.program_id(0)
        u = u_ref[...]
        t = lax.broadcasted_iota(jnp.int32, u.shape, 0)
        cnt = jnp.minimum(t + 1, 2 << g).astype(F32)
        pm = (_pool_sums(u, g, t, _shift_dn) / cnt - u).astype(BF16)
        wv = w_ref[0].astype(BF16)
        d = d_ref[...]
        pw = jnp.dot(pm, wv, preferred_element_type=F32)
        dsc_ref[...] = jnp.sum(pw * d, axis=0, keepdims=True)
        dpw = (d * s_ref[...]).astype(BF16)
        dw_ref[0] = lax.dot_general(pm, dpw, (((0,), (0,)), ((), ())), preferred_element_type=F32)
        dpm = lax.dot_general(dpw, wv, (((1,), (1,)), ((), ())), preferred_element_type=F32)
        du_ref[...] = _pool_sums(dpm / cnt, g, t, _shift_up) - dpm

    return _call(body, name="pool_bwd",
                 out_shape=(_sds(dproj.shape), _sds((POOL_GROUPS, LANES, LANES)), _sds((1, POOL_GROUPS * LANES))),
                 grid=(POOL_GROUPS,), in_specs=[col, wsp, vec, col, ANY], out_specs=(col, wsp, vec),
                 aliases={4: 0})(proj, pool_w, scale, dcat, dproj)


def _conv(x, w_ref, b_ref, t):
    K = w_ref.shape[0]
    y = b_ref[...] + jnp.zeros_like(x)
    for k in range(K):
        y = y + w_ref[k:k + 1, :] * _shift_dn(x, K - 1 - k, t)
    return y


def _silu_grad(y):
    sg = _sigmoid(y)
    return sg * (1.0 + y * (1.0 - sg))


CONV_ROWS = 256
HALO = 8


def _win_above(ref, r0):
    if isinstance(r0, int):
        assert r0 == 0
        return jnp.concatenate([jnp.zeros((HALO, ref.shape[1]), F32), ref[0:CONV_ROWS, :]], axis=0)
    return ref[pl.ds(pl.multiple_of(r0 - HALO, HALO), CONV_ROWS + HALO), :]


def _rows_at(win, start):
    if start % 8 == 0:
        return win[start:start + CONV_ROWS]
    base = start // 8 * 8
    return pltpu.roll(win, win.shape[0] - (start - base), axis=0)[base:base + CONV_ROWS]


def _taps_above(win, K):
    return [_rows_at(win, HALO - (K - 1 - k)) for k in range(K)]


def _conv_taps(taps, w, b):
    y = b
    for k in range(len(w)):
        y = y + w[k] * taps[k]
    return y


def _conv_t_win(win, w):
    K = len(w)
    out = None
    for k in range(K):
        d = K - 1 - k
        term = w[k] * _rows_at(win, d)
        out = term if out is None else out + term
    return out


def _fold8(x):
    return jnp.sum(x.reshape(CONV_ROWS // 8, 8, x.shape[-1]), axis=0)


def _chunk_loop(S, step, init):
    carry = step(0, init)
    return lax.fori_loop(1, S // CONV_ROWS, lambda i, c: step(pl.multiple_of(i * CONV_ROWS, CONV_ROWS), c), carry)


def _ffn_mid_specs(S, K, layer, nf):
    return [pl.BlockSpec((S, LANES), lambda j: (0, j)), pl.BlockSpec((S, LANES), lambda j: (0, nf + j)),
            pl.BlockSpec((None, K, LANES), lambda j: (layer, 0, j)), pl.BlockSpec((None, K, LANES), lambda j: (layer, 0, nf + j)),
            pl.BlockSpec((None, 1, LANES), lambda j: (layer, 0, j)), pl.BlockSpec((None, 1, LANES), lambda j: (layer, 0, nf + j))]


def _ffn_mid_fwd(a, cw, cb, layer):
    S, F = a.shape[0], a.shape[1] // 2
    nf = F // LANES
    K = cw.shape[1]

    def body(au_ref, ag_ref, wu_ref, wg_ref, bu_ref, bg_ref, o_ref):
        t = lax.broadcasted_iota(jnp.int32, (S, LANES), 0)
        hu = _conv(au_ref[...], wu_ref, bu_ref, t)
        hg = _conv(ag_ref[...], wg_ref, bg_ref, t)
        o_ref[...] = (hg * _sigmoid(hg) * hu).astype(BF16)

    return _call(body, name="ffn_mid_fwd", out_shape=_sds((S, F), BF16), grid=(nf,),
                 in_specs=_ffn_mid_specs(S, K, layer, nf), out_specs=pl.BlockSpec((S, LANES), lambda j: (0, j)))(
                     a, a, cw, cw, cb[:, None], cb[:, None])


def _ffn_mid_bwd(a, cw, cb, layer, dact):
    S, F = a.shape[0], a.shape[1] // 2
    nf = F // LANES
    K = cw.shape[1]

    def body(au_ref, ag_ref, wu_ref, wg_ref, bu_ref, bg_ref, d_ref, dau_ref, dag_ref, dwu_ref, dwg_ref, dbu_ref, dbg_ref,
             dhu_s, dhg_s):
        T = CONV_ROWS
        wu = [wu_ref[k:k + 1, :] for k in range(K)]
        wg = [wg_ref[k:k + 1, :] for k in range(K)]
        bu, bg = bu_ref[...], bg_ref[...]
        zero8 = jnp.zeros((HALO, LANES), F32)
        dhu_s[S:S + HALO, :] = zero8
        dhg_s[S:S + HALO, :] = zero8

        def first_pass(r0, acc):
            tu, tg = _taps_above(_win_above(au_ref, r0), K), _taps_above(_win_above(ag_ref, r0), K)
            hu, hg = _conv_taps(tu, wu, bu), _conv_taps(tg, wg, bg)
            d = d_ref[pl.ds(r0, T), :].astype(F32)
            sg = _sigmoid(hg)
            dhu = d * hg * sg
            dhg = d * hu * (sg * (1.0 + hg * (1.0 - sg)))
            dhu_s[pl.ds(r0, T), :] = dhu
            dhg_s[pl.ds(r0, T), :] = dhg
            new = []
            for dh, taps in ((dhu, tu), (dhg, tg)):
                for k in range(K):
                    new.append(acc[len(new)] + _fold8(dh * taps[k]))
            new.append(acc[2 * K] + _fold8(dhu))
            new.append(acc[2 * K + 1] + _fold8(dhg))
            return tuple(new)

        acc = _chunk_loop(S, first_pass, tuple(zero8 for _ in range(2 * K + 2)))
        for k in range(K):
            dwu_ref[k:k + 1, :] = jnp.sum(acc[k], axis=0, keepdims=True)
            dwg_ref[k:k + 1, :] = jnp.sum(acc[K + k], axis=0, keepdims=True)
        dbu_ref[...] = jnp.sum(acc[2 * K], axis=0, keepdims=True)
        dbg_ref[...] = jnp.sum(acc[2 * K + 1], axis=0, keepdims=True)

        def second_pass(i, carry):
            r0 = pl.multiple_of(i * T, T)
            dau_ref[pl.ds(r0, T), :] = _conv_t_win(dhu_s[pl.ds(r0, T + HALO), :], wu).astype(BF16)
            dag_ref[pl.ds(r0, T), :] = _conv_t_win(dhg_s[pl.ds(r0, T + HALO), :], wg).astype(BF16)
            return carry

        lax.fori_loop(0, S // T, second_pass, 0)

    col = pl.BlockSpec((S, LANES), lambda j: (0, j))
    wsp = pl.BlockSpec((K, LANES), lambda j: (0, j))
    bsp = pl.BlockSpec((1, LANES), lambda j: (0, j))
    dau, dag, dwu, dwg, dbu, dbg = _call(
        body, name="ffn_mid_bwd",
        out_shape=(_sds((S, F), BF16), _sds((S, F), BF16), _sds((K, F)), _sds((K, F)), _sds((1, F)), _sds((1, F))), grid=(nf,),
        in_specs=_ffn_mid_specs(S, K, layer, nf) + [col], out_specs=(col, col, wsp, wsp, bsp, bsp),
        scratch=[pltpu.VMEM((S + HALO, LANES), F32), pltpu.VMEM((S + HALO, LANES), F32)])(
            a, a, cw, cw, cb[:, None], cb[:, None], dact)
    return dau, dag, jnp.concatenate([dwu, dwg], axis=1), jnp.concatenate([dbu, dbg], axis=1)


def _conv_silu_fwd(x, cw, cb):
    S = x.shape[0]
    K, C = cw.shape

    def body(x_ref, w_ref, b_ref, o_ref):
        t = lax.broadcasted_iota(jnp.int32, (S, LANES), 0)
        y = _conv(x_ref[...], w_ref, b_ref, t)
        o_ref[...] = y * _sigmoid(y)

    col = pl.BlockSpec((S, LANES), lambda j: (0, j))
    return _call(body, name="conv_silu_fwd", out_shape=_sds((S, C)), grid=(C // LANES,),
                 in_specs=[col, pl.BlockSpec((K, LANES), lambda j: (0, j)), pl.BlockSpec((1, LANES), lambda j: (0, j))],
                 out_specs=col)(x, cw, cb)


def _conv_silu_bwd(x, cw, cb, douts):
    S = x.shape[0]
    K, C = cw.shape
    starts, off = [], 0
    for d in douts:
        starts.append(off)
        off += d.shape[1] // LANES
    assert off == C // LANES

    def body(x_ref, w_ref, b_ref, *rest):
        dy_s = rest[-1]
        d_refs, (dx_ref, dw_ref, db_ref) = rest[:len(douts)], rest[len(douts):-1]
        j = pl.program_id(0)
        T = CONV_ROWS
        w = [w_ref[k:k + 1, :] for k in range(K)]
        b = b_ref[...]
        zero8 = jnp.zeros((HALO, LANES), F32)
        dy_s[S:S + HALO, :] = zero8

        def first_pass(r0, acc):
            taps = _taps_above(_win_above(x_ref, r0), K)
            y = _conv_taps(taps, w, b)
            d = d_refs[0][pl.ds(r0, T), :]
            for i in range(1, len(douts)):
                d = jnp.where(j >= starts[i], d_refs[i][pl.ds(r0, T), :], d)
            dy = d * _silu_grad(y)
            dy_s[pl.ds(r0, T), :] = dy
            return tuple(acc[k] + _fold8(dy * taps[k]) for k in range(K)) + (acc[K] + _fold8(dy),)

        acc = _chunk_loop(S, first_pass, tuple(zero8 for _ in range(K + 1)))
        for k in range(K):
            dw_ref[k:k + 1, :] = jnp.sum(acc[k], axis=0, keepdims=True)
        db_ref[...] = jnp.sum(acc[K], axis=0, keepdims=True)

        def second_pass(i, carry):
            r0 = pl.multiple_of(i * T, T)
            dx_ref[pl.ds(r0, T), :] = _conv_t_win(dy_s[pl.ds(r0, T + HALO), :], w).astype(BF16)
            return carry

        lax.fori_loop(0, S // T, second_pass, 0)

    col = pl.BlockSpec((S, LANES), lambda j: (0, j))
    wsp = pl.BlockSpec((K, LANES), lambda j: (0, j))
    bsp = pl.BlockSpec((1, LANES), lambda j: (0, j))

    def dspec(i):
        nblk = douts[i].shape[1] // LANES
        return pl.BlockSpec((S, LANES), lambda j: (0, jnp.clip(j - starts[i], 0, nblk - 1)))

    return _call(body, name="conv_silu_bwd", out_shape=(_sds((S, C), BF16), _sds((K, C)), _sds((1, C))), grid=(C // LANES,),
                 in_specs=[col, wsp, bsp] + [dspec(i) for i in range(len(douts))],
                 out_specs=(col, wsp, bsp), scratch=[pltpu.VMEM((S + HALO, LANES), F32)])(x, cw, cb, *douts)


HI = lax.Precision.HIGHEST


def _to_group(x, g, live):
    return jnp.where(live, x if g == 0 else pltpu.roll(x, LANES - SSM_R * g, axis=1), 0.0)


def _from_groups(refs_g):
    out = refs_g[0]
    for g in range(1, SSM_G):
        out = out + pltpu.roll(refs_g[g], SSM_R * g, axis=1)
    return out


def _ssd_prep_fwd(raw, bias_row, alog_row):
    S = raw.shape[0]
    nc = S // SSM_L
    NH = SSM_G * SSM_R

    def body(raw_ref, b_ref, al_ref, pre_ref, dth_ref, dt_ref, acs_ref, acst_ref):
        r_i = lax.broadcasted_iota(jnp.int32, (LANES, LANES), 0)
        c_i = lax.broadcasted_iota(jnp.int32, (LANES, LANES), 1)
        live = c_i < SSM_R
        tril = jnp.where(r_i >= c_i, 1.0, 0.0)
        pre = raw_ref[...] + b_ref[...]
        dt = jnp.where(c_i < NH, jnp.logaddexp(pre, 0.0), 0.0)
        acs = jnp.dot(tril, dt * (-jnp.exp(al_ref[...])), preferred_element_type=F32, precision=HI)
        pre_ref[...] = pre
        dth_ref[...] = dt
        for g in range(SSM_G):
            acs_g = _to_group(acs, g, live)
            dt_ref[g] = _to_group(dt, g, live)
            acs_ref[g] = acs_g
            acst_ref[g] = acs_g.T

    row = pl.BlockSpec((1, LANES), lambda c: (0, 0))
    flat = pl.BlockSpec((SSM_L, LANES), lambda c: (c, 0))
    blk = pl.BlockSpec((SSM_G, SSM_L, LANES), lambda c: (0, c, 0))
    big = _sds((SSM_G, S, LANES))
    return _call(body, name="ssd_prep_fwd",
                 out_shape=(_sds((S, LANES)), _sds((S, LANES)), big, big, _sds((SSM_G, LANES, S))), grid=(nc,),
                 in_specs=[flat, row, row],
                 out_specs=(flat, flat, blk, blk, pl.BlockSpec((SSM_G, LANES, SSM_L), lambda c: (0, 0, c))))(
                     raw, bias_row, alog_row)


def _ssd_prep_bwd(pre_h, dt_h, alog_row, ddt_g, dacs_g, dacst_g):
    S = pre_h.shape[0]
    nc = S // SSM_L
    NH = SSM_G * SSM_R

    def body(pre_ref, dt_ref, al_ref, ddt_ref, dacs_ref, dacst_ref, draw_ref, db_ref, dal_ref):
        c = pl.program_id(0)
        r_i = lax.broadcasted_iota(jnp.int32, (LANES, LANES), 0)
        c_i = lax.broadcasted_iota(jnp.int32, (LANES, LANES), 1)
        triu = jnp.where(r_i <= c_i, 1.0, 0.0)

        @pl.when(c == 0)
        def _():
            db_ref[...] = jnp.zeros_like(db_ref)
            dal_ref[...] = jnp.zeros_like(dal_ref)

        dacs = _from_groups([dacs_ref[g] + dacst_ref[g].T for g in range(SSM_G)])
        da = jnp.dot(triu, dacs, preferred_element_type=F32, precision=HI)
        A = -jnp.exp(al_ref[...])
        ddt = _from_groups([ddt_ref[g] for g in range(SSM_G)]) + da * A
        dpre = jnp.where(c_i < NH, ddt * _sigmoid(pre_ref[...]), 0.0)
        draw_ref[...] = dpre
        db_ref[...] += jnp.sum(dpre, axis=0, keepdims=True)
        dal_ref[...] += jnp.where(c_i[:1] < NH, jnp.sum(da * dt_ref[...], axis=0, keepdims=True) * A, 0.0)

    row = pl.BlockSpec((1, LANES), lambda c: (0, 0))
    flat = pl.BlockSpec((SSM_L, LANES), lambda c: (c, 0))
    blk = pl.BlockSpec((SSM_G, SSM_L, LANES), lambda c: (0, c, 0))
    return _call(body, name="ssd_prep_bwd", out_shape=(_sds((S, LANES)), _sds((1, LANES)), _sds((1, LANES))), grid=(nc,),
                 in_specs=[flat, flat, row, blk, blk, pl.BlockSpec((SSM_G, LANES, SSM_L), lambda c: (0, 0, c))],
                 out_specs=(flat, row, row))(pre_h, dt_h, alog_row, ddt_g, dacs_g, dacst_g)


NT = (((1,), (1,)), ((), ()))
TN = (((0,), (0,)), ((), ()))
SSM_HP = SSM_R * SSM_P


def _ssd_group_terms(xs_ref, dt_ref, acs_ref, d_ref):
    hid = lax.broadcasted_iota(jnp.int32, (1, SSM_HP), 1) // SSM_P
    rid = lax.broadcasted_iota(jnp.int32, (SSM_HP, 1), 0) // SSM_P

    def widen(cols):
        out = cols[0]
        for r in range(1, SSM_R):
            out = jnp.where(hid == r, cols[r], out)
        return out

    dt_c = [dt_ref[:, r:r + 1] for r in range(SSM_R)]
    acs_c = [acs_ref[:, r:r + 1] for r in range(SSM_R)]
    last = [acs_ref[SSM_L - 1:SSM_L, r:r + 1] for r in range(SSM_R)]
    decay_c = [jnp.exp(last[r] - acs_c[r]) for r in range(SSM_R)]
    cd = [jnp.exp(last[r]) for r in range(SSM_R)]
    cd_rows = cd[0]
    for r in range(1, SSM_R):
        cd_rows = jnp.where(rid == r, cd[r], cd_rows)
    xs = xs_ref[...]
    return (xs, xs * widen(dt_c), widen([jnp.exp(a) for a in acs_c]), widen(decay_c),
            widen([d_ref[:, r:r + 1] for r in range(SSM_R)]), cd_rows, dt_c, decay_c, cd)


def _ssd_lmat(acs_ref, acst_ref, r, tril):
    return jnp.exp(jnp.where(tril, acs_ref[:, r:r + 1] - acst_ref[r:r + 1, :], -jnp.inf))


SSM_GPS = 4


def _ssd_specs(rev, nc):
    def cc(c):
        return nc - 1 - c if rev else c
    xs_blocks = (SSM_G * SSM_HP) // (SSM_GPS * SSM_N)
    xs = pl.BlockSpec((SSM_L, SSM_GPS * SSM_HP), lambda g, c: (cc(c), g))
    bsp = pl.BlockSpec((SSM_L, SSM_GPS * SSM_N), lambda g, c: (cc(c), xs_blocks + g))
    csp = pl.BlockSpec((SSM_L, SSM_GPS * SSM_N), lambda g, c: (cc(c), xs_blocks + SSM_G // SSM_GPS + g))
    sc = pl.BlockSpec((SSM_GPS, SSM_L, LANES), lambda g, c: (g, cc(c), 0))
    sct = pl.BlockSpec((SSM_GPS, LANES, SSM_L), lambda g, c: (g, 0, cc(c)))
    gsp = pl.BlockSpec((SSM_GPS, 1, LANES), lambda g, c: (g, 0, 0))
    st = pl.BlockSpec((None, SSM_GPS, SSM_HP, SSM_N), lambda g, c: (cc(c), g, 0, 0))
    return xs, bsp, csp, sc, sct, gsp, st


def _interleave(gens):
    live = list(gens)
    while live:
        for g in list(live):
            try:
                next(g)
            except StopIteration:
                live.remove(g)


def _rounds(gens):
    live = list(gens)
    while live:
        for g in list(live):
            try:
                next(g)
            except StopIteration:
                live.remove(g)
        yield


def _ssd_group_views(gg, xs_ref, b_ref, c_ref, *per_group):
    return (xs_ref.at[:, gg * SSM_HP:(gg + 1) * SSM_HP], b_ref.at[:, gg * SSM_N:(gg + 1) * SSM_N],
            c_ref.at[:, gg * SSM_N:(gg + 1) * SSM_N]) + tuple(r.at[gg] for r in per_group)


def _ssd_fwd(xbc, dt_g, acs_g, acst_g, d_g):
    S = xbc.shape[0]
    nc = S // SSM_L
    xs_s, b_s, c_s, sc, sct, gsp, st = _ssd_specs(False, nc)

    def body(xs_ref, b_ref, c_ref, dt_ref, acs_ref, acst_ref, d_ref, y_ref, st_ref, state):
        c = pl.program_id(1)

        @pl.when(c == 0)
        def _():
            state[...] = jnp.zeros_like(state)

        tril = lax.broadcasted_iota(jnp.int32, (SSM_L, SSM_L), 0) >= lax.broadcasted_iota(jnp.int32, (SSM_L, SSM_L), 1)
        def group(gg):
            xs_v, b_v, c_v, dt_v, acs_v, acst_v, d_v, st_v, state_v = _ssd_group_views(
                gg, xs_ref, b_ref, c_ref, dt_ref, acs_ref, acst_ref, d_ref, st_ref, state)
            y_v = y_ref.at[:, gg * SSM_HP:(gg + 1) * SSM_HP]
            Bb, Cb = b_v[...].astype(BF16), c_v[...].astype(BF16)
            Gm = lax.dot_general(Cb, Bb, NT, preferred_element_type=F32)
            yield
            xs, X, e_all, decay_all, d_all, cd_rows, _, _, _ = _ssd_group_terms(xs_v, dt_v, acs_v, d_v)
            S_all = state_v[...]
            st_v[...] = S_all
            yield
            yo = lax.dot_general(Cb, S_all.astype(BF16), NT, preferred_element_type=F32)
            new_state = lax.dot_general((X * decay_all).astype(BF16), Bb, TN, preferred_element_type=F32)
            yield
            y_v[...] = e_all * yo + d_all * xs
            state_v[...] = S_all * cd_rows + new_state

            def head(r):
                sl = slice(r * SSM_P, (r + 1) * SSM_P)
                M = Gm * _ssd_lmat(acs_v, acst_v, r, tril)
                yield
                y_v[:, sl] += jnp.dot(M.astype(BF16), X[:, sl].astype(BF16), preferred_element_type=F32)

            yield from _rounds([head(r) for r in range(SSM_R)])

        _interleave([group(gg) for gg in range(SSM_GPS)])

    return _call(body, name="ssd_fwd",
                 out_shape=(_sds((S, SSM_G * SSM_HP)), _sds((nc, SSM_G, SSM_HP, SSM_N))),
                 grid=(SSM_G // SSM_GPS, nc), in_specs=[xs_s, b_s, c_s, sc, sc, sct, gsp],
                 out_specs=(xs_s, pl.BlockSpec((None, SSM_GPS, SSM_HP, SSM_N), lambda g, c: (c, g, 0, 0))),
                 scratch=[pltpu.VMEM((SSM_GPS, SSM_HP, SSM_N), F32)])(xbc, xbc, xbc, dt_g, acs_g, acst_g, d_g)


def _ssd_bwd(xbc, dt_g, acs_g, acst_g, d_g, states, dy):
    S = xbc.shape[0]
    nc = S // SSM_L
    xs_s, b_s, c_s, sc, sct, gsp, st = _ssd_specs(True, nc)
    bc_out = pl.BlockSpec((SSM_L, SSM_GPS * SSM_N), lambda g, c: (nc - 1 - c, g))

    def body(xs_ref, b_ref, c_ref, dt_ref, acs_ref, acst_ref, d_ref, st_ref, dy_ref,
             dxs_ref, db_ref, dc_ref, ddt_ref, dacs_ref, dacst_ref, dd_ref, dstate):
        c = pl.program_id(1)

        @pl.when(c == 0)
        def _():
            dstate[...] = jnp.zeros_like(dstate)
            dd_ref[...] = jnp.zeros_like(dd_ref)

        tril = lax.broadcasted_iota(jnp.int32, (SSM_L, SSM_L), 0) >= lax.broadcasted_iota(jnp.int32, (SSM_L, SSM_L), 1)
        lane = lax.broadcasted_iota(jnp.int32, (1, LANES), 1)
        subl = lax.broadcasted_iota(jnp.int32, (LANES, 1), 0)
        last_row = lax.broadcasted_iota(jnp.int32, (SSM_L, 1), 0) == SSM_L - 1
        triu = lax.broadcasted_iota(jnp.int32, (SSM_L, SSM_L), 0) <= lax.broadcasted_iota(jnp.int32, (SSM_L, SSM_L), 1)
        def group(gg):
            xs_v, b_v, c_v, dt_v, acs_v, acst_v, d_v, st_v, ddt_v, dacs_v, dacst_v, dd_v, dstate_v = _ssd_group_views(
                gg, xs_ref, b_ref, c_ref, dt_ref, acs_ref, acst_ref, d_ref, st_ref, ddt_ref, dacs_ref, dacst_ref, dd_ref, dstate)
            dy_v, dxs_v = (r.at[:, gg * SSM_HP:(gg + 1) * SSM_HP] for r in (dy_ref, dxs_ref))
            db_v, dc_v = (r.at[:, gg * SSM_N:(gg + 1) * SSM_N] for r in (db_ref, dc_ref))
            Bb, Cb = b_v[...].astype(BF16), c_v[...].astype(BF16)
            Gm = lax.dot_general(Cb, Bb, NT, preferred_element_type=F32)
            GmT = lax.dot_general(Bb, Cb, NT, preferred_element_type=F32)
            yield
            xs, X, e_all, decay_all, d_all, cd_rows, dt_c, decay_c, cd = _ssd_group_terms(xs_v, dt_v, acs_v, d_v)
            S_all, dSn_all, dY = st_v[...], dstate_v[...], dy_v[...]
            Sb, dSnb = S_all.astype(BF16), dSn_all.astype(BF16)
            yield
            T = lax.dot_general(Cb, Sb, NT, preferred_element_type=F32)
            dT = (dY * e_all).astype(BF16)
            dC = jnp.dot(dT, Sb, preferred_element_type=F32)
            dS_prev = lax.dot_general(dT, Cb, TN, preferred_element_type=F32)
            yield
            yo_dy = dY * (e_all * T)
            W = lax.dot_general(Bb, dSnb, NT, preferred_element_type=F32)
            dB = jnp.dot((X * decay_all).astype(BF16), dSnb, preferred_element_type=F32)
            yield
            xw = X * W
            dcd_rows = jnp.sum(dSn_all * S_all, axis=1, keepdims=True)
            dstate_v[...] = dS_prev + dSn_all * cd_rows
            dX_state = W * decay_all
            yield
            acc = dict(dG=jnp.zeros((SSM_L, SSM_L), F32), dGT=jnp.zeros((SSM_L, SSM_L), F32),
                       ddt=jnp.zeros((SSM_L, LANES), F32), dacs=jnp.zeros((SSM_L, LANES), F32),
                       dacst=jnp.zeros((LANES, SSM_L), F32), dd=jnp.zeros((1, LANES), F32))

            def head(r):
                sl = slice(r * SSM_P, (r + 1) * SSM_P)
                Lm = _ssd_lmat(acs_v, acst_v, r, tril)
                LmT = jnp.exp(jnp.where(triu, acst_v[r:r + 1, :] - acs_v[:, r:r + 1], -jnp.inf))
                M = Gm * Lm
                yield
                dYh, xs_h = dY[:, sl], xs[:, sl]
                dYb, Xb = dYh.astype(BF16), X[:, sl].astype(BF16)
                dM = lax.dot_general(dYb, Xb, NT, preferred_element_type=F32)
                yield
                dX = jnp.dot((GmT * LmT).astype(BF16), dYb, preferred_element_type=F32) + dX_state[:, sl]
                acc["dG"] = acc["dG"] + dM * Lm
                acc["dGT"] = acc["dGT"] + lax.dot_general(Xb, dYb, NT, preferred_element_type=F32) * LmT
                yield
                dseg = dM * M
                dd = jnp.sum(xw[:, sl], axis=1, keepdims=True) * decay_c[r]
                dcd = jnp.sum(dcd_rows[sl])
                dacs_col = (jnp.sum(dseg, axis=1, keepdims=True) + jnp.sum(yo_dy[:, sl], axis=1, keepdims=True) - dd
                            + jnp.where(last_row, dcd * cd[r] + jnp.sum(dd), 0.0))
                dacs_row = -jnp.sum(dseg, axis=0, keepdims=True)
                yield
                dxs_v[:, sl] = dX * dt_c[r] + d_all[:, sl] * dYh
                acc["ddt"] = acc["ddt"] + jnp.where(lane == r, jnp.sum(dX * xs_h, axis=1, keepdims=True), 0.0)
                acc["dacs"] = acc["dacs"] + jnp.where(lane == r, dacs_col, 0.0)
                acc["dacst"] = acc["dacst"] + jnp.where(subl == r, dacs_row, 0.0)
                acc["dd"] = acc["dd"] + jnp.where(lane == r, jnp.sum(dYh * xs_h), 0.0)

            yield from _rounds([head(r) for r in range(SSM_R)])
            dc_v[...] = dC + jnp.dot(acc["dG"].astype(BF16), Bb, preferred_element_type=F32)
            db_v[...] = dB + jnp.dot(acc["dGT"].astype(BF16), Cb, preferred_element_type=F32)
            ddt_v[...] = acc["ddt"]
            dacs_v[...] = acc["dacs"]
            dacst_v[...] = acc["dacst"]
            dd_v[...] += acc["dd"]

        _interleave([group(gg) for gg in range(SSM_GPS)])

    big = _sds((SSM_G, S, LANES))
    return _call(body, name="ssd_bwd",
                 out_shape=(_sds((S, SSM_G * SSM_HP)), _sds((S, SSM_G * SSM_N)), _sds((S, SSM_G * SSM_N)),
                            big, big, _sds((SSM_G, LANES, S)), _sds((SSM_G, 1, LANES))),
                 grid=(SSM_G // SSM_GPS, nc), in_specs=[xs_s, b_s, c_s, sc, sc, sct, gsp, st, xs_s],
                 out_specs=(xs_s, bc_out, bc_out, sc, sc, sct, gsp),
                 scratch=[pltpu.VMEM((SSM_GPS, SSM_HP, SSM_N), F32)])(xbc, xbc, xbc, dt_g, acs_g, acst_g, d_g, states, dy)


def _gate_norm_fwd(y, proj, w):
    S, DI = y.shape
    tm = _tile(S, 256)

    def body(y_ref, z_ref, w_ref, o_ref):
        z = z_ref[...]
        gn = y_ref[...] * (z * _sigmoid(z))
        r = lax.rsqrt(jnp.mean(gn * gn, axis=-1, keepdims=True) + SSM_NORM_EPS)
        o_ref[...] = (gn * r * w_ref[...]).astype(BF16)

    row = pl.BlockSpec((tm, DI), lambda i: (i, 0))
    return _call(body, name="gate_norm_fwd", out_shape=_sds((S, DI), BF16), grid=(S // tm,),
                 in_specs=[row, row, pl.BlockSpec((1, DI), lambda i: (0, 0))], out_specs=row)(y, proj, w)


def _gate_norm_bwd(y, proj, w, dout):
    S, DI = y.shape
    tm = _tile(S, 256)

    def body(y_ref, z_ref, w_ref, d_ref, dy_ref, dz_ref, dw_ref):
        z, yv = z_ref[...], y_ref[...]
        sz = z * _sigmoid(z)
        dgn, dw = _norm_bwd_math(yv * sz, w_ref[...], d_ref[...].astype(F32), SSM_NORM_EPS)
        dy_ref[...] = dgn * sz
        dz_ref[...] = (dgn * yv * _silu_grad(z)).astype(BF16)

        @pl.when(pl.program_id(0) == 0)
        def _():
            dw_ref[...] = jnp.zeros_like(dw_ref)

        dw_ref[...] += dw

    row = pl.BlockSpec((tm, DI), lambda i: (i, 0))
    vec = pl.BlockSpec((1, DI), lambda i: (0, 0))
    return _call(body, name="gate_norm_bwd", out_shape=(_sds((S, DI)), _sds((S, DI), BF16), _sds((1, DI))), grid=(S // tm,),
                 in_specs=[row, row, vec, row], out_specs=(row, row, vec))(y, proj, w, dout)


def _group_major(v):
    return jnp.pad(v.reshape(SSM_G, 1, SSM_R), ((0, 0), (0, 0), (0, LANES - SSM_R)))


def _ungroup(t):
    return t[:, :SSM_R].reshape(1, SSM_G * SSM_R)


def _ffn_fwd(x, P, l, need):
    need(f"ffn{l}_up", x)
    h = _rmsnorm(x, P["norm_ffn"][l:l + 1], name=f"ffn{l}_norm")
    wT = P[f"ffn_w_upT{l}"]
    F = wT.shape[0] // 2
    a = _mm(h, wT, tb=True, name=f"ffn{l}_up")
    need(f"ffn{l}_down", a)
    act = _ffn_mid_fwd(a, P["ffn_conv_w"], P["ffn_conv_b"], l)
    out = _mm(act, P[f"ffn_w_down{l}"], res=x, name=f"ffn{l}_down")
    return out, (x, h, a, act)


def _ffn_bwd(saved, P, l, dx, emit):
    x, h, a, act = saved
    wT = P[f"ffn_w_upT{l}"]
    F = wT.shape[0] // 2
    dact = _mm(dx, P[f"ffn_w_down{l}"], tb=True, out_dtype=BF16, name=f"ffn{l}_down_dx")
    dw_down = _mm(act, dx, ta=True, out_dtype=PAYLOAD, name=f"ffn{l}_down_dw")
    dau, dag, dcw, dcb = _ffn_mid_bwd(a, P["ffn_conv_w"], P["ffn_conv_b"], l, dact)
    dw_upT = _mm(dau, h, ta=True, out_dtype=PAYLOAD, out_rows=(2 * F, 0, None), name=f"ffn{l}_up_u_dw")
    dw_upT = _mm(dag, h, ta=True, out_dtype=PAYLOAD, out_rows=(2 * F, F, dw_upT), name=f"ffn{l}_up_g_dw")
    tie = emit(f"ffn{l}", {"ffn_w_upT": dw_upT, "ffn_w_down": dw_down})
    dh = _mm(dau, wT, b_rows=(0, F), name=f"ffn{l}_up_u_dx")
    dx_in, dnw = _mm_norm_bwd(dag, wT, x, P["norm_ffn"][l:l + 1], dx, res=dh, b_rows=(F, F), after=tie,
                              name=f"ffn{l}_up_g_dx_norm_bwd")
    return dx_in, dnw, dcw, dcb


def _local_step(x, positions, target, P, need, emit, after=None):
    S, D = x.shape
    inv_freq = ROPE_THETA ** (-jnp.arange(0, HEAD_DIM, 2, dtype=F32) / HEAD_DIM)
    inv_freq = jnp.tile(inv_freq, LANES // (HEAD_DIM // 2)).reshape(1, LANES)
    cos, sin = _rope_tables(positions, inv_freq)

    nm0 = P["norm_mix"][0:1]
    h0 = _rmsnorm(x, nm0, name="mix_norm", after=after)
    need("mix_in", h0)
    proj0 = _mm(h0, P["mix_w_inT"], tb=True, name="mix_in")
    cat0 = _attn_fwd(proj0, cos, sin, P["attn_sinks"], _pool_fwd(proj0, P["pool_w"][0], P["pool_scale"]))
    need("mix_out", cat0)
    x1 = _mm(cat0, P["mix_w_out"], res=x, name="mix_out")
    x2, ffn0 = _ffn_fwd(x1, P, 0, need)

    nm1 = P["norm_mix"][1:2]
    need("ssm", x2)
    h1 = _rmsnorm(x2, nm1, name="ssm_norm_in")
    w1T, wdtT = P["ssm_w_inT"], P["ssm_wdtT"]
    DI, CD, NH = P["ssm_norm"].shape[1], P["ssm_conv_w"].shape[1], P["ssm_dt_bias"].shape[1]
    z = _mm(h1, w1T, tb=True, b_rows=(0, DI), name="ssm_in_z")
    xbcp = _mm(h1, w1T, tb=True, b_rows=(DI, CD), name="ssm_in_xbc")
    dtraw = _mm(h1, wdtT, tb=True, name="ssm_in_dt")
    xbc = _conv_silu_fwd(xbcp, P["ssm_conv_w"], P["ssm_conv_b"])
    bias_row = jnp.pad(P["ssm_dt_bias"], ((0, 0), (0, LANES - NH)))
    alog_row = jnp.pad(P["ssm_A_log"], ((0, 0), (0, LANES - NH)))
    d_g = _group_major(P["ssm_D"])
    pre_h, dt_h, dt_g, acs_g, acst_g = _ssd_prep_fwd(dtraw, bias_row, alog_row)
    y, states = _ssd_fwd(xbc, dt_g, acs_g, acst_g, d_g)
    yn = _gate_norm_fwd(y, z, P["ssm_norm"])
    need("ssm_out", yn)
    x3 = _mm(yn, P["ssm_w_out"], res=x2, name="ssm_out")
    x4, ffn1 = _ffn_fwd(x3, P, 1, need)

    loss, dx, d_norm_final = _final_loss(x4, P["norm_final"].reshape(1, D), target, name="final_loss")
    dx, dnf1, dcw1, dcb1 = _ffn_bwd(ffn1, P, 1, dx, emit)
    dyn = _mm(dx, P["ssm_w_out"], tb=True, out_dtype=BF16, name="ssm_out_dx")
    d_w_out1 = _mm(yn, dx, ta=True, out_dtype=PAYLOAD, name="ssm_out_dw")
    dy, dz, d_ssm_norm = _gate_norm_bwd(y, z, P["ssm_norm"], dyn)
    dxs, dB, dC, ddt_g, dacs_g, dacst_g, dd_g = _ssd_bwd(xbc, dt_g, acs_g, acst_g, d_g, states, dy)
    draw, dbias_row, dalog_row = _ssd_prep_bwd(pre_h, dt_h, alog_row, ddt_g, dacs_g, dacst_g)
    dxbc, d_conv_w1, d_conv_b1 = _conv_silu_bwd(xbcp, P["ssm_conv_w"], P["ssm_conv_b"], [dxs, dB, dC])
    rows = DI + CD + NH
    d_w1T = _mm(dz, h1, ta=True, out_dtype=PAYLOAD, out_rows=(rows, 0, None), name="ssm_in_z_dw")
    d_w1T = _mm(dxbc, h1, ta=True, out_dtype=PAYLOAD, out_rows=(rows, DI, d_w1T), name="ssm_in_xbc_dw")
    d_w1T = _mm(draw[:, :NH], h1, ta=True, out_dtype=PAYLOAD, out_rows=(rows, DI + CD, d_w1T), name="ssm_in_dt_dw")
    tie = emit("ssm", {"ssm_w_inT": d_w1T, "ssm_w_out": d_w_out1,
                       "ssm_conv_w": d_conv_w1, "ssm_conv_b": d_conv_b1, "ssm_norm": d_ssm_norm})
    dh1 = _mm(dz, w1T, b_rows=(0, DI), name="ssm_in_z_dx")
    dh1 = _mm(dxbc, w1T, b_rows=(DI, CD), res=dh1, name="ssm_in_xbc_dx")
    dx, dnm1 = _mm_norm_bwd(draw, wdtT, x2, nm1, dx, res=dh1, after=tie, name="ssm_in_dt_dx_norm_bwd")
    dx, dnf0, dcw0, dcb0 = _ffn_bwd(ffn0, P, 0, dx, emit)
    dcat = _mm(dx, P["mix_w_out"], tb=True, name="mix_out_dx")
    d_w_out0 = _mm(cat0, dx, ta=True, out_dtype=PAYLOAD, name="mix_out_dw")
    dproj0, dsk = _attn_bwd(proj0, cos, sin, P["attn_sinks"], dcat)
    dproj0, d_pool_w, d_pool_scale = _pool_bwd(proj0, P["pool_w"][0], P["pool_scale"], dcat, dproj0)
    d_w_in0 = _mm(dproj0, h0, ta=True, out_dtype=PAYLOAD, name="mix_in_dw")
    tie = emit("mix", {"mix_w_inT": d_w_in0, "mix_w_out": d_w_out0, "ffn_conv_w": jnp.stack([dcw0, dcw1])})
    grad_x, dnm0 = _mm_norm_bwd(dproj0, P["mix_w_inT"], x, nm0, dx, after=tie, name="mix_in_dx_norm_bwd")

    small = {
        "norm_mix": jnp.concatenate([dnm0, dnm1], axis=0),
        "norm_ffn": jnp.concatenate([dnf0, dnf1], axis=0),
        "norm_final": d_norm_final,
        "pool_w": d_pool_w,
        "pool_scale": d_pool_scale,
        "attn_sinks_rows": dsk,
        "ssm_dt_bias_row": dbias_row, "ssm_A_log_row": dalog_row, "ssm_D_g": dd_g,
        "ffn_conv_b": jnp.concatenate([dcb0, dcb1], axis=0),
    }
    return loss, grad_x, small


def _peer(k):
    x, y, c = lax.axis_index("x"), lax.axis_index("y"), lax.axis_index("c")
    px = 1 - x if k & 4 else x
    py = 1 - y if k & 2 else y
    pc = 1 - c if k & 1 else c
    return (px, py, pc), 4 * px + 2 * py + pc


def _my_index():
    return 4 * lax.axis_index("x") + 2 * lax.axis_index("y") + lax.axis_index("c")


def _land_sds(a, mode, gather):
    if mode == "slab":
        return _sds(((N_DEV,) + a.shape) if gather else a.shape, a.dtype)
    assert mode == "rows", mode
    return _sds((N_DEV * a.shape[0],) + a.shape[1:] if gather else (N_DEV, a.shape[0] // N_DEV) + a.shape[1:], a.dtype)


def _part(ref, mode, shape, idx):
    if mode == "slab":
        return ref.at[idx]
    r = shape[0] // N_DEV
    return ref.at[pl.ds(idx * r, r)]


def _remote_copies(ops, gather, srcs, lands, send_sems, recv_sems):
    me = _my_index()
    n = len(ops)
    out = []
    for k in range(1, N_DEV):
        dev, idx = _peer(k)
        for i, (a, mode) in enumerate(ops):
            s = srcs[i] if gather else _part(srcs[i], mode, a.shape, idx)
            d = _part(lands[i], mode, _land_sds(a, mode, gather).shape, me) if gather else lands[i].at[me]
            out.append(pltpu.make_async_remote_copy(src_ref=s, dst_ref=d, send_sem=send_sems.at[(k - 1) * n + i],
                                                    recv_sem=recv_sems.at[(k - 1) * n + i], device_id=dev,
                                                    device_id_type=pl.DeviceIdType.MESH))
    return out


HBM = pl.BlockSpec(memory_space=pltpu.HBM)
SEM = pl.BlockSpec(memory_space=pltpu.SEMAPHORE)
SIDE_EFFECT = pltpu.SideEffectType.DATAFLOW_SIDE_EFFECTING


def _in_hbm(a):
    return pltpu.with_memory_space_constraint(a, pltpu.HBM)


def _place_own(ops, *, gather, name):
    n = len(ops)

    def zeros(k):
        return (0,) * k

    in_specs, out_specs = [], []
    for a, mode in ops:
        nd = a.ndim
        if gather and mode == "slab":
            in_specs.append(pl.BlockSpec(a.shape, lambda i, nd=nd: zeros(nd)))
            out_specs.append(pl.BlockSpec((1,) + a.shape, lambda i, nd=nd: (_my_index(),) + zeros(nd)))
        elif gather:
            in_specs.append(pl.BlockSpec(a.shape, lambda i, nd=nd: zeros(nd)))
            out_specs.append(pl.BlockSpec(a.shape, lambda i, nd=nd: (_my_index(),) + zeros(nd - 1)))
        elif mode == "slab":
            in_specs.append(pl.BlockSpec((1,) + a.shape[1:], lambda i, nd=nd: (_my_index(),) + zeros(nd - 1)))
            out_specs.append(pl.BlockSpec((1,) + a.shape[1:], lambda i, nd=nd: (_my_index(),) + zeros(nd - 1)))
        else:
            r = a.shape[0] // N_DEV
            in_specs.append(pl.BlockSpec((r,) + a.shape[1:], lambda i, nd=nd: (_my_index(),) + zeros(nd - 1)))
            out_specs.append(pl.BlockSpec((1, r) + a.shape[1:], lambda i, nd=nd: (_my_index(),) + zeros(nd)))

    def body(*refs):
        for i_ref, o_ref in zip(refs[:n], refs[n:2 * n]):
            if o_ref.shape == i_ref.shape:
                o_ref[...] = i_ref[...]
            else:
                o_ref[0] = i_ref[...]

    outs = _call(body, name=name, grid=(1,), in_specs=in_specs, out_specs=out_specs + [ANY] * n,
                 out_shape=[_land_sds(a, m, gather) for a, m in ops] + [_sds(a.shape, a.dtype) for a, _ in ops],
                 aliases={i: n + i for i in range(n)})(*[a for a, _ in ops])
    return outs[:n], [(src, m) for src, (_, m) in zip(outs[n:], ops)]


def _exchange_start(groups, *, gather, name):
    sizes = [len(ops) for ops, _ in groups]
    n = sum(sizes)
    G = len(groups)

    def body(*refs):
        srcs, lands = refs[:n], refs[n:2 * n]
        sems = refs[2 * n:2 * n + 2 * G]
        token = refs[-1]
        off = 0
        for g, (ops, _) in enumerate(groups):
            for cp in _remote_copies(ops, gather, srcs[off:off + sizes[g]], lands[off:off + sizes[g]], sems[2 * g], sems[2 * g + 1]):
                cp.start()
            off += sizes[g]
        token[...] = jnp.zeros_like(token)

    srcs = [a for ops, _ in groups for a, _ in ops]
    lands = [l for _, ls in groups for l in ls]
    sem_shapes = [pltpu.SemaphoreType.DMA((s * (N_DEV - 1),)) for s in sizes for _ in range(2)]
    outs = pl.pallas_call(
        body, name=name,
        out_shape=sem_shapes + [pltpu.HBM(a.shape, a.dtype) for a in srcs + lands] + [_sds((8, LANES))],
        in_specs=[HBM] * (2 * n), out_specs=[SEM] * (2 * G) + [HBM] * (2 * n) + [pl.BlockSpec(memory_space=pltpu.VMEM)],
        input_output_aliases={i: 2 * G + i for i in range(2 * n)},
        compiler_params=pltpu.CompilerParams(has_side_effects=SIDE_EFFECT))(*[_in_hbm(a) for a in srcs + lands])
    sems, thru, token = outs[:2 * G], outs[2 * G:2 * G + 2 * n], outs[-1]
    states, off = [], 0
    for g, s in enumerate(sizes):
        states.append((sems[2 * g], sems[2 * g + 1], thru[off:off + s], thru[n + off:n + off + s]))
        off += s
    return states, token


def _exchange_wait(ops, state, after, *, gather, name):
    send_sems, recv_sems, srcs, lands = state
    n = len(ops)

    def body(*refs):
        for cp in _remote_copies(ops, gather, refs[:n], refs[n:2 * n], refs[2 * n], refs[2 * n + 1]):
            cp.wait_send()
            cp.wait_recv()

    outs = pl.pallas_call(
        body, name=name, out_shape=[pltpu.HBM(a.shape, a.dtype) for a in list(srcs) + list(lands)],
        in_specs=[HBM] * (2 * n) + [SEM, SEM, ANY], out_specs=[HBM] * (2 * n),
        input_output_aliases={i: i for i in range(2 * n)},
        compiler_params=pltpu.CompilerParams(has_side_effects=SIDE_EFFECT))(*srcs, *lands, send_sems, recv_sems, after)
    return outs[n:]


ADAM_ROWS = 256


def _row_tile(R, cap=ADAM_ROWS):
    best = R
    if R > cap:
        for d in range(16, cap + 1, 16):
            if R % d == 0:
                best = d
    return best


def _adamw(g_layers, w, m, v, *, name):
    L = len(g_layers)
    J, R, Wd = g_layers[0].shape
    assert w.shape == (L, R, Wd), (g_layers[0].shape, w.shape)
    tr = _row_tile(R)
    nrt = R // tr
    c1 = 1.0 / (1.0 - ADAM_B1 ** ADAM_STEP)
    c2 = 1.0 / (1.0 - ADAM_B2 ** ADAM_STEP)

    def body(*refs):
        g_refs = refs[:L]
        w_ref, m_ref, v_ref, go_ref, d_ref, mo_ref, vo_ref = refs[L:]
        layer = pl.program_id(0)
        g = None
        for l, g_ref in enumerate(g_refs):
            gl = g_ref[0].astype(F32)
            for j in range(1, J):
                gl = gl + g_ref[j].astype(F32)
            g = gl if g is None else jnp.where(layer == l, gl, g)
        mn = ADAM_B1 * m_ref[...] + (1.0 - ADAM_B1) * g
        vn = ADAM_B2 * v_ref[...] + (1.0 - ADAM_B2) * (g * g)
        go_ref[...] = g
        mo_ref[...] = mn
        vo_ref[...] = vn
        d_ref[...] = -ADAM_LR * ((mn * c1) / (jnp.sqrt(vn * c2) + ADAM_EPS) + ADAM_WD * w_ref[...])

    def g_spec(l):
        return pl.BlockSpec((J, tr, Wd), lambda ll, i: (0, jnp.where(ll == l, i, jnp.where(ll < l, 0, nrt - 1)), 0))

    row = pl.BlockSpec((None, tr, Wd), lambda ll, i: (ll, i, 0))
    out = _sds((L, R, Wd))
    return _call(body, name=name, out_shape=(out, out, out, out), grid=(L, nrt),
                 in_specs=[g_spec(l) for l in range(L)] + [row, row, row], out_specs=(row, row, row, row))(*g_layers, w, m, v)


def _sum_slabs(slabs, *, name):
    n = len(slabs)

    def body(*refs):
        for g_ref, o_ref in zip(refs[:n], refs[n:]):
            g = g_ref[0]
            for j in range(1, g_ref.shape[0]):
                g = g + g_ref[j]
            o_ref[...] = g

    return _call(body, name=name, out_shape=[_sds(s.shape[1:]) for s in slabs])(*slabs)


def kernel(x, positions, norm_mix, norm_ffn, norm_final, mix_w_in, pool_w, pool_scale, attn_sinks, mix_w_out, ssm_w_in, ssm_conv_w, ssm_conv_b, ssm_dt_bias, ssm_A_log, ssm_D, ssm_norm, ssm_w_out, ffn_w_up, ffn_conv_w, ffn_conv_b, ffn_w_down, loss_target, m_norm_mix, m_norm_ffn, m_norm_final, m_mix_w_in, m_pool_w, m_pool_scale, m_attn_sinks, m_mix_w_out, m_ssm_w_in, m_ssm_conv_w, m_ssm_conv_b, m_ssm_dt_bias, m_ssm_A_log, m_ssm_D, m_ssm_norm, m_ssm_w_out, m_ffn_w_up, m_ffn_conv_w, m_ffn_conv_b, m_ffn_w_down, v_norm_mix, v_norm_ffn, v_norm_final, v_mix_w_in, v_pool_w, v_pool_scale, v_attn_sinks, v_mix_w_out, v_ssm_w_in, v_ssm_conv_w, v_ssm_conv_b, v_ssm_dt_bias, v_ssm_A_log, v_ssm_D, v_ssm_norm, v_ssm_w_out, v_ffn_w_up, v_ffn_conv_w, v_ffn_conv_b, v_ffn_w_down):
    args = dict(locals())
    wl = {n: args[n] for n in WEIGHTS}
    ml = {n: args["m_" + n] for n in WEIGHTS}
    vl = {n: args["v_" + n] for n in WEIGHTS}
    F = ffn_w_down.shape[1] * N_DEV
    DI, CD, NH = ssm_norm.shape[1] * N_DEV, ssm_conv_b.shape[1] * N_DEV, ssm_dt_bias.shape[1]
    Kc, Kf = ssm_conv_w.shape[1], ffn_conv_w.shape[1]
    n_up = ffn_w_up.shape[2]
    col_sharded = ("mix_w_in", "ssm_w_in", "ffn_w_up")

    def tr(a):
        return jnp.swapaxes(a, -1, -2)

    def two(a):
        return a.reshape(-1, a.shape[-1])

    def pay(a):
        return a.astype(PAYLOAD)

    order = ("mix_in", "mix_out", "ffn0_up", "ffn0_down", "ssm", "ssm_out", "ffn1_up", "ffn1_down")
    gops = {
        "mix_in": [(pay(tr(mix_w_in)[0]), "rows")],
        "mix_out": [(pay(mix_w_out[0]), "rows"), (two(ssm_conv_w), "slab"), (ssm_conv_b, "slab"), (ssm_norm, "slab"),
                    (two(ffn_conv_w), "slab")],
        "ffn0_up": [(pay(tr(ffn_w_up)[0]), "rows")], "ffn0_down": [(pay(ffn_w_down[0]), "rows")],
        "ssm": [(pay(tr(ssm_w_in)[0]), "slab")], "ssm_out": [(pay(ssm_w_out[0]), "rows")],
        "ffn1_up": [(pay(tr(ffn_w_up)[1]), "rows")], "ffn1_down": [(pay(ffn_w_down[1]), "rows")],
    }
    lands, handed = _place_own([op for g in order for op in gops[g]], gather=True, name="gather_own")
    groups, off = [], 0
    for g in order:
        gops[g] = handed[off:off + len(gops[g])]
        groups.append((gops[g], lands[off:off + len(gops[g])]))
        off += len(gops[g])
    gstates, token = _exchange_start(groups, gather=True, name="gather_start")
    gstate = dict(zip(order, gstates))
    P = {n: wl[n] for n in REPLICATED}

    def need(g, after):
        got = _exchange_wait(gops[g], gstate[g], after, gather=True, name="gather_wait_" + g)
        if g == "mix_in":
            P["mix_w_inT"] = got[0]
        elif g == "mix_out":
            P.update(mix_w_out=got[0], ssm_conv_w=got[1].transpose(1, 0, 2).reshape(Kc, CD), ssm_conv_b=got[2].reshape(1, CD),
                     ssm_norm=got[3].reshape(1, DI), ffn_conv_w=got[4].transpose(1, 0, 2).reshape(2, Kf, 2 * F))
        elif g == "ssm":
            w1T = got[0].reshape(-1, got[0].shape[-1])
            P.update(ssm_w_inT=w1T, ssm_wdtT=jnp.pad(w1T[DI + CD:], ((0, LANES - NH), (0, 0))))
        elif g == "ssm_out":
            P["ssm_w_out"] = got[0]
        elif g.endswith("_up"):
            P["ffn_w_upT" + g[3]] = got[0]
        else:
            P["ffn_w_down" + g[3]] = got[0]

    sent = {}

    def emit(g, d):
        if g == "mix":
            ops = [(d["mix_w_inT"], "rows"), (d["mix_w_out"], "rows"),
                   (d["ffn_conv_w"].reshape(2 * Kf, N_DEV, n_up).transpose(1, 0, 2), "slab")]
        elif g == "ssm":
            ops = [(d["ssm_w_inT"].reshape(N_DEV, -1, d["ssm_w_inT"].shape[-1]), "slab"), (d["ssm_w_out"], "rows"),
                   (d["ssm_conv_w"].reshape(Kc, N_DEV, -1).transpose(1, 0, 2), "slab"),
                   (d["ssm_conv_b"].reshape(N_DEV, 1, -1), "slab"), (d["ssm_norm"].reshape(N_DEV, 1, -1), "slab")]
        else:
            ops = [(d["ffn_w_upT"], "rows"), (d["ffn_w_down"], "rows")]
        own, ops = _place_own(ops, gather=False, name="scatter_own_" + g)
        (state,), tok = _exchange_start([(ops, own)], gather=False, name="scatter_start_" + g)
        sent[g] = (ops, state)
        return tok

    loss_lanes, grad_x, G = _local_step(x[0], positions.reshape(-1, 1), loss_target[0], P, need, emit, after=token)

    res = {}

    def update(n, g_layers):
        L = len(g_layers)
        g_layers = [g.reshape(g.shape[0], -1, g.shape[-1]) for g in g_layers]
        shape = (L,) + g_layers[0].shape[1:]
        view = tr if n in col_sharded else (lambda a: a)
        outs = _adamw(g_layers, view(wl[n]).reshape(shape), view(ml[n]).reshape(shape), view(vl[n]).reshape(shape),
                      name="adamw_" + n)
        for kind, a in zip(("grad", "delta", "new_m", "new_v"), outs):
            res[kind, n] = view(a.reshape(view(wl[n]).shape))

    rep_ops = [(a, "slab") for a in (G["norm_mix"], G["norm_ffn"], G["norm_final"], G["pool_w"].reshape(-1, LANES),
                                     G["pool_scale"], G["ffn_conv_b"], G["attn_sinks_rows"],
                                     G["ssm_dt_bias_row"], G["ssm_A_log_row"], G["ssm_D_g"].reshape(SSM_G, LANES),
                                     loss_lanes)]
    rep_own, rep_ops = _place_own(rep_ops, gather=True, name="small_own")
    (rep_state,), rep_token = _exchange_start([(rep_ops, rep_own)], gather=True, name="small_start")

    recv = {g: _exchange_wait(sent[g][0], sent[g][1], rep_token, gather=False, name="scatter_wait_" + g)
            for g in ("ffn1", "ssm", "ffn0")}
    update("ssm_w_in", [recv["ssm"][0]])
    update("ssm_w_out", [recv["ssm"][1]])
    update("ssm_conv_w", [recv["ssm"][2]])
    update("ssm_conv_b", [recv["ssm"][3]])
    update("ssm_norm", [recv["ssm"][4]])
    update("ffn_w_up", [recv["ffn0"][0], recv["ffn1"][0]])
    update("ffn_w_down", [recv["ffn0"][1], recv["ffn1"][1]])
    recv["mix"] = _exchange_wait(sent["mix"][0], sent["mix"][1], res["new_v", "ffn_w_down"], gather=False,
                                 name="scatter_wait_mix")
    update("mix_w_in", [recv["mix"][0]])
    update("mix_w_out", [recv["mix"][1]])
    update("ffn_conv_w", [recv["mix"][2]])

    rep = _exchange_wait(rep_ops, rep_state, res["new_v", "mix_w_out"], gather=True, name="small_wait")
    for n, r in zip(("norm_mix", "norm_ffn", "norm_final", "pool_w", "pool_scale", "ffn_conv_b"), rep):
        update(n, [r])
    sinks_rows, bias_row, alog_row, d_g, loss_sum = _sum_slabs(rep[6:], name="sum_head_grads")
    update("attn_sinks", [sinks_rows[:, 0].reshape(1, 1, N_HEADS)])
    update("ssm_dt_bias", [bias_row[:, :NH][None]])
    update("ssm_A_log", [alog_row[:, :NH][None]])
    update("ssm_D", [_ungroup(d_g)[None]])
    loss = loss_sum[0, 0]

    return (loss, grad_x[None], *[res[k, n] for k in ("grad", "delta", "new_m", "new_v") for n in WEIGHTS])
```

```python
import functools
import math

import jax
import jax.numpy as jnp
from jax import lax
from jax.experimental import pallas as pl
from jax.experimental.pallas import tpu as pltpu

F32 = jnp.float32
BF16 = jnp.bfloat16

N_DEV = 8
LANES = 128
HEAD_DIM = 64
N_KV_HEADS = 2
GQ = 4
N_HEADS = N_KV_HEADS * GQ
BLOCK = 128
POOL_GROUPS = 4
ROPE_THETA = 10000.0
SSM_P = 64
SSM_G = 8
SSM_R = 4
SSM_N = 128
SSM_L = 128
NORM_EPS = 1e-6
SSM_NORM_EPS = 1e-5
ADAM_LR, ADAM_B1, ADAM_B2, ADAM_EPS, ADAM_WD, ADAM_STEP = 0.001, 0.9, 0.999, 1e-08, 0.01, 10
VMEM_LIMIT = 56 * 2 ** 20
PAYLOAD = jnp.bfloat16

REPLICATED = ("norm_mix", "norm_ffn", "norm_final", "pool_w", "pool_scale", "attn_sinks",
              "ssm_dt_bias", "ssm_A_log", "ssm_D", "ffn_conv_b")
WEIGHTS = ("norm_mix", "norm_ffn", "norm_final", "mix_w_in", "pool_w", "pool_scale", "attn_sinks", "mix_w_out",
           "ssm_w_in", "ssm_conv_w", "ssm_conv_b", "ssm_dt_bias", "ssm_A_log", "ssm_D", "ssm_norm", "ssm_w_out",
           "ffn_w_up", "ffn_conv_w", "ffn_conv_b", "ffn_w_down")


def _tile(n, cap):
    if n <= cap:
        return n
    best = None
    for d in range(LANES, cap + 1, LANES):
        if n % d == 0:
            best = d
    assert best is not None, (n, cap)
    return best


def _call(body, *, name, out_shape, grid=None, in_specs=None, out_specs=None, scratch=(), aliases=None):
    kw = {}
    if grid is not None:
        kw = dict(grid=grid, in_specs=in_specs, out_specs=out_specs)
    if aliases:
        kw["input_output_aliases"] = aliases
    return pl.pallas_call(
        body, name=name, out_shape=out_shape, scratch_shapes=list(scratch),
        compiler_params=pltpu.CompilerParams(vmem_limit_bytes=VMEM_LIMIT), **kw)


ANY = pl.BlockSpec(memory_space=pl.ANY)


def _sds(shape, dtype=F32):
    return jax.ShapeDtypeStruct(tuple(shape), dtype)


def _sigmoid(x):
    return 1.0 / (1.0 + jnp.exp(-x))


def _shift_dn(x, d, t):
    if d == 0:
        return x
    return jnp.where(t >= d, pltpu.roll(x, d, axis=0), 0.0)


def _shift_up(x, d, t):
    if d == 0:
        return x
    n = x.shape[0]
    return jnp.where(t < n - d, pltpu.roll(x, n - d, axis=0), 0.0)


def _mm(a, b, *, name, ta=False, tb=False, res=None, out_dtype=F32, b_rows=None, out_rows=None):
    M, K = (a.shape[1], a.shape[0]) if ta else a.shape
    b0, bn = b_rows if b_rows is not None else (0, b.shape[0])
    N = bn if tb else b.shape[1]
    assert (b.shape[1] if tb else bn) == K, (a.shape, b.shape, ta, tb, b_rows)
    tm, tn, tk = _tile(M, 1408), _tile(N, 1408), _tile(K, 1408)
    nk = K // tk
    dims = (((0 if ta else 1,), (1 if tb else 0,)), ((), ()))
    aliased = out_rows is not None and out_rows[2] is not None

    def body(*refs):
        a_ref, b_ref = refs[:2]
        r_ref = refs[2] if res is not None else None
        o_ref, acc = refs[-2:]
        k = pl.program_id(2)

        @pl.when(k == 0)
        def _():
            acc[...] = jnp.zeros_like(acc)

        acc[...] += lax.dot_general(a_ref[...].astype(BF16), b_ref[...].astype(BF16), dims,
                                    preferred_element_type=F32)

        @pl.when(k == nk - 1)
        def _():
            out = acc[...]
            if res is not None:
                out = out + r_ref[...]
            o_ref[...] = out.astype(out_dtype)

    a_spec = pl.BlockSpec((tk, tm), lambda i, j, k: (k, i)) if ta else pl.BlockSpec((tm, tk), lambda i, j, k: (i, k))
    if tb:
        assert b0 % tn == 0, (b_rows, tn)
        b_spec = pl.BlockSpec((tn, tk), lambda i, j, k: (b0 // tn + j, k))
    else:
        assert b0 % tk == 0, (b_rows, tk)
        b_spec = pl.BlockSpec((tk, tn), lambda i, j, k: (b0 // tk + k, j))
    ins, specs = [a, b], [a_spec, b_spec]
    if res is not None:
        ins.append(res)
        specs.append(pl.BlockSpec((tm, tn), lambda i, j, k: (i, j)))
    aliases = None
    if out_rows is None:
        o_spec = pl.BlockSpec((tm, tn), lambda i, j, k: (i, j))
        out_shape = _sds((M, N), out_dtype)
    else:
        total, o0, prev = out_rows
        assert o0 % tm == 0, (out_rows, tm)
        o_spec = pl.BlockSpec((tm, tn), lambda i, j, k: (o0 // tm + i, j))
        out_shape = _sds((total, N), out_dtype)
        if aliased:
            aliases = {len(ins): 0}
            ins.append(prev)
            specs.append(ANY)
    return _call(body, name=name, out_shape=out_shape, grid=(M // tm, N // tn, nk), in_specs=specs,
                 out_specs=o_spec, scratch=[pltpu.VMEM((tm, tn), F32)], aliases=aliases)(*ins)


def _rmsnorm(x, w, *, name, eps=NORM_EPS, after=None):
    S, D = x.shape
    tm = _tile(S, 512)
    tie = [] if after is None else [after]

    def body(x_ref, w_ref, *rest):
        o_ref = rest[-1]
        xf = x_ref[...]
        r = lax.rsqrt(jnp.mean(xf * xf, axis=-1, keepdims=True) + eps)
        o_ref[...] = (xf * r * w_ref[...]).astype(BF16)

    return _call(body, name=name, out_shape=_sds((S, D), BF16), grid=(S // tm,),
                 in_specs=[pl.BlockSpec((tm, D), lambda i: (i, 0)), pl.BlockSpec((1, D), lambda i: (0, 0))] + [ANY] * len(tie),
                 out_specs=pl.BlockSpec((tm, D), lambda i: (i, 0)))(x, w, *tie)


def _norm_bwd_math(xf, w, dh, eps):
    r = lax.rsqrt(jnp.mean(xf * xf, axis=-1, keepdims=True) + eps)
    xhat = xf * r
    dxh = dh * w
    dx = r * (dxh - xhat * jnp.mean(dxh * xhat, axis=-1, keepdims=True))
    dw = jnp.sum(dh * xhat, axis=0, keepdims=True)
    return dx, dw


def _mm_norm_bwd(a, b, x, w, dres, *, name, res=None, b_rows=None, after=None, eps=NORM_EPS):
    M, K = a.shape
    b0, bn = b_rows if b_rows is not None else (0, b.shape[0])
    D = b.shape[1]
    assert bn == K and x.shape == (M, D), (a.shape, b.shape, b_rows, x.shape)
    tm, tk = _tile(M, 512), _tile(K, 1408)
    assert b0 % tk == 0, (b_rows, tk)
    nk = K // tk

    def body(*refs):
        a_ref, b_ref, x_ref, w_ref, dr_ref = refs[:5]
        r_ref = refs[5] if res is not None else None
        dx_ref, dw_ref, acc = refs[-3:]
        i, k = pl.program_id(0), pl.program_id(1)

        @pl.when(k == 0)
        def _():
            acc[...] = jnp.zeros_like(acc)

        @pl.when((i == 0) & (k == 0))
        def _():
            dw_ref[...] = jnp.zeros_like(dw_ref)

        acc[...] += jnp.dot(a_ref[...].astype(BF16), b_ref[...].astype(BF16), preferred_element_type=F32)

        @pl.when(k == nk - 1)
        def _():
            dh = acc[...] if res is None else acc[...] + r_ref[...]
            dx, dw = _norm_bwd_math(x_ref[...], w_ref[...], dh, eps)
            dx_ref[...] = dr_ref[...] + dx
            dw_ref[...] += dw

    row = pl.BlockSpec((tm, D), lambda i, k: (i, 0))
    vec = pl.BlockSpec((1, D), lambda i, k: (0, 0))
    ins = [a, b, x, w, dres]
    specs = [pl.BlockSpec((tm, tk), lambda i, k: (i, k)), pl.BlockSpec((tk, D), lambda i, k: (b0 // tk + k, 0)), row, vec, row]
    if res is not None:
        ins.append(res)
        specs.append(row)
    if after is not None:
        ins.append(after)
        specs.append(ANY)
    return _call(body, name=name, out_shape=(_sds((M, D)), _sds((1, D))), grid=(M // tm, nk), in_specs=specs,
                 out_specs=(row, vec), scratch=[pltpu.VMEM((tm, D), F32)])(*ins)


def _final_loss(x, w, target, *, name):
    S, D = x.shape
    tm = _tile(S, 512)

    def body(x_ref, w_ref, t_ref, loss_ref, dx_ref, dw_ref):
        xf, wv = x_ref[...], w_ref[...]
        r = lax.rsqrt(jnp.mean(xf * xf, axis=-1, keepdims=True) + NORM_EPS)
        err = xf * r * wv - t_ref[...]
        part = 0.5 * jnp.sum(jnp.mean(err * err, axis=-1, keepdims=True), axis=0, keepdims=True)
        dx, dw = _norm_bwd_math(xf, wv, err * (1.0 / D), NORM_EPS)
        dx_ref[...] = dx

        @pl.when(pl.program_id(0) == 0)
        def _():
            dw_ref[...] = jnp.zeros_like(dw_ref)
            loss_ref[...] = jnp.zeros_like(loss_ref)

        dw_ref[...] += dw
        loss_ref[...] += jnp.broadcast_to(part, loss_ref.shape)

    row = pl.BlockSpec((tm, D), lambda i: (i, 0))
    vec = pl.BlockSpec((1, D), lambda i: (0, 0))
    return _call(body, name=name, out_shape=(_sds((1, LANES)), _sds((S, D)), _sds((1, D))), grid=(S // tm,),
                 in_specs=[row, vec, row], out_specs=(pl.BlockSpec((1, LANES), lambda i: (0, 0)), row, vec))(x, w, target)


def _rope_tables(pos, inv_freq):
    S = pos.shape[0]
    tm = _tile(S, 512)

    def body(p_ref, f_ref, c_ref, s_ref):
        ang = p_ref[...].astype(F32) * f_ref[...]
        c_ref[...] = jnp.cos(ang)
        s_ref[...] = jnp.sin(ang)

    blk = pl.BlockSpec((tm, LANES), lambda i: (i, 0))
    return _call(body, name="rope_tables", out_shape=(_sds((S, LANES)), _sds((S, LANES))), grid=(S // tm,),
                 in_specs=[pl.BlockSpec((tm, 1), lambda i: (i, 0)), pl.BlockSpec((1, LANES), lambda i: (0, 0))],
                 out_specs=(blk, blk))(pos, inv_freq)


def _rot_half(t):
    lane = lax.broadcasted_iota(jnp.int32, t.shape, 1)
    lo = (lane % HEAD_DIM) < (HEAD_DIM // 2)
    return jnp.where(lo, -pltpu.roll(t, LANES - HEAD_DIM // 2, axis=1), pltpu.roll(t, HEAD_DIM // 2, axis=1))


def _rope(t, c, s):
    return t * c + _rot_half(t) * s


def _unrope(dy, c, s):
    return dy * c - _rot_half(dy * s)


PD = POOL_GROUPS * LANES
QD = N_HEADS * HEAD_DIM
KD = N_KV_HEADS * HEAD_DIM
assert PD % QD == 0 and (PD + QD) % (2 * KD) == 0 and KD == LANES
def _attn_probs(q, kcat, sink, mask):
    s = lax.dot_general(q.astype(BF16), kcat, (((1,), (1,)), ((), ())), preferred_element_type=F32) * (HEAD_DIM ** -0.5)
    s = jnp.where(mask, s, -jnp.inf)
    m = jnp.maximum(jnp.max(s, axis=1, keepdims=True), sink)
    p = jnp.exp(s - m)
    ps = jnp.exp(sink - m)
    inv = 1.0 / (jnp.sum(p, axis=1, keepdims=True) + ps)
    return p * inv, ps * inv


def _attn_mask(n):
    qi = lax.broadcasted_iota(jnp.int32, (BLOCK, 2 * BLOCK), 0)
    kj = lax.broadcasted_iota(jnp.int32, (BLOCK, 2 * BLOCK), 1)
    rel = qi + BLOCK - kj
    return (rel >= 0) & (rel < BLOCK) & ((n > 0) | (kj >= BLOCK))


def _attn_in_specs(nb):
    def cur(n):
        return jnp.minimum(n, nb - 1)

    def prev(n):
        return jnp.clip(n - 1, 0, nb - 1)

    kvb = (PD + QD) // (2 * KD)
    return [pl.BlockSpec(memory_space=pltpu.SMEM),
            pl.BlockSpec((BLOCK, QD), lambda n: (cur(n), PD // QD)),
            pl.BlockSpec((BLOCK, 2 * KD), lambda n: (cur(n), kvb)),
            pl.BlockSpec((BLOCK, 2 * KD), lambda n: (prev(n), kvb)),
            pl.BlockSpec((BLOCK, LANES), lambda n: (cur(n), 0)), pl.BlockSpec((BLOCK, LANES), lambda n: (cur(n), 0)),
            pl.BlockSpec((BLOCK, LANES), lambda n: (prev(n), 0)), pl.BlockSpec((BLOCK, LANES), lambda n: (prev(n), 0))]


def _attn_keys(kvc_ref, kvp_ref, cc, sc, cp, sp):
    kc = _rope(kvc_ref[:, :KD], cc, sc)
    kp = _rope(kvp_ref[:, :KD], cp, sp)
    vc, vp = kvc_ref[:, KD:], kvp_ref[:, KD:]
    kcat, vcat = [], []
    for kk in range(N_KV_HEADS):
        sl = slice(kk * HEAD_DIM, (kk + 1) * HEAD_DIM)
        kcat.append(jnp.concatenate([kp[:, sl], kc[:, sl]], axis=0).astype(BF16))
        vcat.append(jnp.concatenate([vp[:, sl], vc[:, sl]], axis=0).astype(BF16))
    return kcat, vcat


def _attn_fwd(proj, cos, sin, sinks, cat):
    S = proj.shape[0]
    nb = S // BLOCK

    def body(sink_ref, q_ref, kvc_ref, kvp_ref, cc_ref, sc_ref, cp_ref, sp_ref, cat_ref, o_ref):
        n = pl.program_id(0)
        cc, sc = cc_ref[...], sc_ref[...]
        kcat, vcat = _attn_keys(kvc_ref, kvp_ref, cc, sc, cp_ref[...], sp_ref[...])
        mask = _attn_mask(n)
        def head_pair(j):
            qr = _rope(q_ref[:, j * LANES:(j + 1) * LANES], cc, sc)
            for e in range(LANES // HEAD_DIM):
                yield
                h = j * (LANES // HEAD_DIM) + e
                pn, _ = _attn_probs(qr[:, e * HEAD_DIM:(e + 1) * HEAD_DIM], kcat[h // GQ], sink_ref[0, h], mask)
                yield
                o_ref[:, h * HEAD_DIM:(h + 1) * HEAD_DIM] = jnp.dot(
                    pn.astype(BF16), vcat[h // GQ], preferred_element_type=F32).astype(o_ref.dtype)

        _interleave([head_pair(j) for j in range(QD // LANES)])

    return _call(body, name="attn_fwd", out_shape=_sds(cat.shape, cat.dtype), grid=(nb,),
                 in_specs=_attn_in_specs(nb) + [ANY], out_specs=pl.BlockSpec((BLOCK, QD), lambda n: (n, PD // QD)),
                 aliases={8: 0})(sinks, proj, proj, proj, cos, sin, cos, sin, cat)


def _attn_bwd(proj, cos, sin, sinks, dcat):
    S = proj.shape[0]
    nb = S // BLOCK
    scale = HEAD_DIM ** -0.5
    per = LANES // HEAD_DIM

    def body(sink_ref, q_ref, kvc_ref, kvp_ref, cc_ref, sc_ref, cp_ref, sp_ref, do_ref, o_ref, ds_ref, hold, carry, part, pair):
        n = pl.program_id(0)

        @pl.when(n == 0)
        def _():
            hold[...] = jnp.zeros_like(hold)
            carry[...] = jnp.zeros_like(carry)
            ds_ref[...] = jnp.zeros_like(ds_ref)

        live = jnp.where(n < nb, 1.0, 0.0)
        cc, sc, cp, sp = cc_ref[...], sc_ref[...], cp_ref[...], sp_ref[...]
        kcat, vcat = _attn_keys(kvc_ref, kvp_ref, cc, sc, cp, sp)
        mask = _attn_mask(n)
        o_ref[:, :PD] = jnp.zeros((BLOCK, PD), F32)
        o_ref[:, PD:PD + QD] = hold[...]
        dk = [jnp.zeros((2 * BLOCK, HEAD_DIM), F32) for _ in range(N_KV_HEADS)]
        dv = [jnp.zeros((2 * BLOCK, HEAD_DIM), F32) for _ in range(N_KV_HEADS)]
        row = lax.broadcasted_iota(jnp.int32, (8, LANES), 0)
        acc = {"dsk": jnp.zeros((8, LANES), F32)}

        def head_pair(j):
            qr = _rope(q_ref[:, j * LANES:(j + 1) * LANES], cc, sc)
            for e in range(per):
                yield
                h = j * per + e
                kk = h // GQ
                qh = qr[:, e * HEAD_DIM:(e + 1) * HEAD_DIM]
                pn, psn = _attn_probs(qh, kcat[kk], sink_ref[0, h], mask)
                yield
                doh = (do_ref[:, h * HEAD_DIM:(h + 1) * HEAD_DIM] * live).astype(BF16)
                dp = lax.dot_general(doh, vcat[kk], NT, preferred_element_type=F32)
                yield
                delta = jnp.sum(pn * dp, axis=1, keepdims=True)
                ds = (pn * (dp - delta) * scale).astype(BF16)
                pair[j, :, e * HEAD_DIM:(e + 1) * HEAD_DIM] = jnp.dot(ds, kcat[kk], preferred_element_type=F32)
                yield
                dk[kk] = dk[kk] + lax.dot_general(ds, qh.astype(BF16), TN, preferred_element_type=F32)
                dv[kk] = dv[kk] + lax.dot_general(pn.astype(BF16), doh, TN, preferred_element_type=F32)
                acc["dsk"] = acc["dsk"] + jnp.where(row == h, -jnp.sum(psn * delta), 0.0)
            yield
            hold[:, j * LANES:(j + 1) * LANES] = _unrope(pair[j], cc, sc)

        _interleave([head_pair(j) for j in range(QD // LANES)])
        dsk = acc["dsk"]
        for kk in range(N_KV_HEADS):
            sl = slice(kk * HEAD_DIM, (kk + 1) * HEAD_DIM)
            sv = slice(KD + kk * HEAD_DIM, KD + (kk + 1) * HEAD_DIM)
            part[0, :, sl] = dk[kk][:BLOCK]
            part[0, :, sv] = dv[kk][:BLOCK]
            part[1, :, sl] = dk[kk][BLOCK:]
            part[1, :, sv] = dv[kk][BLOCK:]
        done = carry[...] + part[0]
        o_ref[:, PD + QD:PD + QD + KD] = _unrope(done[:, :KD], cp, sp)
        o_ref[:, PD + QD + KD:] = done[:, KD:]
        carry[...] = part[1]
        ds_ref[...] += dsk

    return _call(body, name="attn_bwd", out_shape=(_sds((S, PD + QD + 2 * KD)), _sds((8, LANES))), grid=(nb + 1,),
                 in_specs=_attn_in_specs(nb) + [pl.BlockSpec((BLOCK, QD), lambda n: (jnp.minimum(n, nb - 1), PD // QD))],
                 out_specs=(pl.BlockSpec((BLOCK, PD + QD + 2 * KD), lambda n: (jnp.maximum(n - 1, 0), 0)),
                            pl.BlockSpec((8, LANES), lambda n: (0, 0))),
                 scratch=[pltpu.VMEM((BLOCK, QD), F32), pltpu.VMEM((BLOCK, 2 * KD), F32),
                          pltpu.VMEM((2, BLOCK, 2 * KD), F32), pltpu.VMEM((QD // LANES, BLOCK, LANES), F32)])(
                     sinks, proj, proj, proj, cos, sin, cos, sin, dcat)


def _pool_sums(u, g, t, shift):
    s2 = u + shift(u, 1, t)
    s4 = s2 + shift(s2, 2, t)
    s8 = s4 + shift(s4, 4, t)
    s16 = s8 + shift(s8, 8, t)
    return jnp.where(g == 0, s2, jnp.where(g == 1, s4, jnp.where(g == 2, s8, s16)))


def _pool_specs(S):
    col = pl.BlockSpec((S, LANES), lambda g: (0, g))
    wsp = pl.BlockSpec((1, LANES, LANES), lambda g: (g, 0, 0))
    vec = pl.BlockSpec((1, LANES), lambda g: (0, g))
    return col, wsp, vec


def _pool_fwd(proj, pool_w, scale):
    S = proj.shape[0]
    col, wsp, vec = _pool_specs(S)

    def body(u_ref, w_ref, s_ref, o_ref):
        g = pl.program_id(0)
        u = u_ref[...]
        t = lax.broadcasted_iota(jnp.int32, u.shape, 0)
        cnt = jnp.minimum(t + 1, 2 << g).astype(F32)
        pm = _pool_sums(u, g, t, _shift_dn) / cnt - u
        o_ref[...] = (jnp.dot(pm.astype(BF16), w_ref[0].astype(BF16), preferred_element_type=F32) * s_ref[...]).astype(BF16)

    return _call(body, name="pool_fwd", out_shape=_sds((S, PD + QD), BF16), grid=(POOL_GROUPS,),
                 in_specs=[col, wsp, vec], out_specs=col)(proj, pool_w, scale)


def _pool_bwd(proj, pool_w, scale, dcat, dproj):
    S = proj.shape[0]
    col, wsp, vec = _pool_specs(S)

    def body(u_ref, w_ref, s_ref, d_ref, dproj_ref, du_ref, dw_ref, dsc_ref):
        g = pl.program_id(0)
        u = u_ref[...]
        t = lax.broadcasted_iota(jnp.int32, u.shape, 0)
        cnt = jnp.minimum(t + 1, 2 << g).astype(F32)
        pm = (_pool_sums(u, g, t, _shift_dn) / cnt - u).astype(BF16)
        wv = w_ref[0].astype(BF16)
        d = d_ref[...]
        pw = jnp.dot(pm, wv, preferred_element_type=F32)
        dsc_ref[...] = jnp.sum(pw * d, axis=0, keepdims=True)
        dpw = (d * s_ref[...]).astype(BF16)
        dw_ref[0] = lax.dot_general(pm, dpw, (((0,), (0,)), ((), ())), preferred_element_type=F32)
        dpm = lax.dot_general(dpw, wv, (((1,), (1,)), ((), ())), preferred_element_type=F32)
        du_ref[...] = _pool_sums(dpm / cnt, g, t, _shift_up) - dpm

    return _call(body, name="pool_bwd",
                 out_shape=(_sds(dproj.shape), _sds((POOL_GROUPS, LANES, LANES)), _sds((1, POOL_GROUPS * LANES))),
                 grid=(POOL_GROUPS,), in_specs=[col, wsp, vec, col, ANY], out_specs=(col, wsp, vec),
                 aliases={4: 0})(proj, pool_w, scale, dcat, dproj)


def _conv(x, w_ref, b_ref, t):
    K = w_ref.shape[0]
    y = b_ref[...] + jnp.zeros_like(x)
    for k in range(K):
        y = y + w_ref[k:k + 1, :] * _shift_dn(x, K - 1 - k, t)
    return y


def _silu_grad(y):
    sg = _sigmoid(y)
    return sg * (1.0 + y * (1.0 - sg))


CONV_ROWS = 256
HALO = 8


def _win_above(ref, r0):
    if isinstance(r0, int):
        assert r0 == 0
        return jnp.concatenate([jnp.zeros((HALO, ref.shape[1]), F32), ref[0:CONV_ROWS, :]], axis=0)
    return ref[pl.ds(pl.multiple_of(r0 - HALO, HALO), CONV_ROWS + HALO), :]


def _rows_at(win, start):
    if start % 8 == 0:
        return win[start:start + CONV_ROWS]
    base = start // 8 * 8
    return pltpu.roll(win, win.shape[0] - (start - base), axis=0)[base:base + CONV_ROWS]


def _taps_above(win, K):
    return [_rows_at(win, HALO - (K - 1 - k)) for k in range(K)]


def _conv_taps(taps, w, b):
    y = b
    for k in range(len(w)):
        y = y + w[k] * taps[k]
    return y


def _conv_t_win(win, w):
    K = len(w)
    out = None
    for k in range(K):
        d = K - 1 - k
        term = w[k] * _rows_at(win, d)
        out = term if out is None else out + term
    return out


def _fold8(x):
    return jnp.sum(x.reshape(CONV_ROWS // 8, 8, x.shape[-1]), axis=0)


def _chunk_loop(S, step, init):
    carry = step(0, init)
    return lax.fori_loop(1, S // CONV_ROWS, lambda i, c: step(pl.multiple_of(i * CONV_ROWS, CONV_ROWS), c), carry)


def _ffn_mid_specs(S, K, layer, nf):
    return [pl.BlockSpec((S, LANES), lambda j: (0, j)), pl.BlockSpec((S, LANES), lambda j: (0, nf + j)),
            pl.BlockSpec((None, K, LANES), lambda j: (layer, 0, j)), pl.BlockSpec((None, K, LANES), lambda j: (layer, 0, nf + j)),
            pl.BlockSpec((None, 1, LANES), lambda j: (layer, 0, j)), pl.BlockSpec((None, 1, LANES), lambda j: (layer, 0, nf + j))]


def _ffn_mid_fwd(a, cw, cb, layer):
    S, F = a.shape[0], a.shape[1] // 2
    nf = F // LANES
    K = cw.shape[1]

    def body(au_ref, ag_ref, wu_ref, wg_ref, bu_ref, bg_ref, o_ref):
        t = lax.broadcasted_iota(jnp.int32, (S, LANES), 0)
        hu = _conv(au_ref[...], wu_ref, bu_ref, t)
        hg = _conv(ag_ref[...], wg_ref, bg_ref, t)
        o_ref[...] = (hg * _sigmoid(hg) * hu).astype(BF16)

    return _call(body, name="ffn_mid_fwd", out_shape=_sds((S, F), BF16), grid=(nf,),
                 in_specs=_ffn_mid_specs(S, K, layer, nf), out_specs=pl.BlockSpec((S, LANES), lambda j: (0, j)))(
                     a, a, cw, cw, cb[:, None], cb[:, None])


def _ffn_mid_bwd(a, cw, cb, layer, dact):
    S, F = a.shape[0], a.shape[1] // 2
    nf = F // LANES
    K = cw.shape[1]

    def body(au_ref, ag_ref, wu_ref, wg_ref, bu_ref, bg_ref, d_ref, dau_ref, dag_ref, dwu_ref, dwg_ref, dbu_ref, dbg_ref,
             dhu_s, dhg_s):
        T = CONV_ROWS
        wu = [wu_ref[k:k + 1, :] for k in range(K)]
        wg = [wg_ref[k:k + 1, :] for k in range(K)]
        bu, bg = bu_ref[...], bg_ref[...]
        zero8 = jnp.zeros((HALO, LANES), F32)
        dhu_s[S:S + HALO, :] = zero8
        dhg_s[S:S + HALO, :] = zero8

        def first_pass(r0, acc):
            tu, tg = _taps_above(_win_above(au_ref, r0), K), _taps_above(_win_above(ag_ref, r0), K)
            hu, hg = _conv_taps(tu, wu, bu), _conv_taps(tg, wg, bg)
            d = d_ref[pl.ds(r0, T), :].astype(F32)
            sg = _sigmoid(hg)
            dhu = d * hg * sg
            dhg = d * hu * (sg * (1.0 + hg * (1.0 - sg)))
            dhu_s[pl.ds(r0, T), :] = dhu
            dhg_s[pl.ds(r0, T), :] = dhg
            new = []
            for dh, taps in ((dhu, tu), (dhg, tg)):
                for k in range(K):
                    new.append(acc[len(new)] + _fold8(dh * taps[k]))
            new.append(acc[2 * K] + _fold8(dhu))
            new.append(acc[2 * K + 1] + _fold8(dhg))
            return tuple(new)

        acc = _chunk_loop(S, first_pass, tuple(zero8 for _ in range(2 * K + 2)))
        for k in range(K):
            dwu_ref[k:k + 1, :] = jnp.sum(acc[k], axis=0, keepdims=True)
            dwg_ref[k:k + 1, :] = jnp.sum(acc[K + k], axis=0, keepdims=True)
        dbu_ref[...] = jnp.sum(acc[2 * K], axis=0, keepdims=True)
        dbg_ref[...] = jnp.sum(acc[2 * K + 1], axis=0, keepdims=True)

        def second_pass(i, carry):
            r0 = pl.multiple_of(i * T, T)
            dau_ref[pl.ds(r0, T), :] = _conv_t_win(dhu_s[pl.ds(r0, T + HALO), :], wu).astype(BF16)
            dag_ref[pl.ds(r0, T), :] = _conv_t_win(dhg_s[pl.ds(r0, T + HALO), :], wg).astype(BF16)
            return carry

        lax.fori_loop(0, S // T, second_pass, 0)

    col = pl.BlockSpec((S, LANES), lambda j: (0, j))
    wsp = pl.BlockSpec((K, LANES), lambda j: (0, j))
    bsp = pl.BlockSpec((1, LANES), lambda j: (0, j))
    dau, dag, dwu, dwg, dbu, dbg = _call(
        body, name="ffn_mid_bwd",
        out_shape=(_sds((S, F), BF16), _sds((S, F), BF16), _sds((K, F)), _sds((K, F)), _sds((1, F)), _sds((1, F))), grid=(nf,),
        in_specs=_ffn_mid_specs(S, K, layer, nf) + [col], out_specs=(col, col, wsp, wsp, bsp, bsp),
        scratch=[pltpu.VMEM((S + HALO, LANES), F32), pltpu.VMEM((S + HALO, LANES), F32)])(
            a, a, cw, cw, cb[:, None], cb[:, None], dact)
    return dau, dag, jnp.concatenate([dwu, dwg], axis=1), jnp.concatenate([dbu, dbg], axis=1)


def _conv_silu_fwd(x, cw, cb):
    S = x.shape[0]
    K, C = cw.shape

    def body(x_ref, w_ref, b_ref, o_ref):
        t = lax.broadcasted_iota(jnp.int32, (S, LANES), 0)
        y = _conv(x_ref[...], w_ref, b_ref, t)
        o_ref[...] = y * _sigmoid(y)

    col = pl.BlockSpec((S, LANES), lambda j: (0, j))
    return _call(body, name="conv_silu_fwd", out_shape=_sds((S, C)), grid=(C // LANES,),
                 in_specs=[col, pl.BlockSpec((K, LANES), lambda j: (0, j)), pl.BlockSpec((1, LANES), lambda j: (0, j))],
                 out_specs=col)(x, cw, cb)


def _conv_silu_bwd(x, cw, cb, douts):
    S = x.shape[0]
    K, C = cw.shape
    starts, off = [], 0
    for d in douts:
        starts.append(off)
        off += d.shape[1] // LANES
    assert off == C // LANES

    def body(x_ref, w_ref, b_ref, *rest):
        dy_s = rest[-1]
        d_refs, (dx_ref, dw_ref, db_ref) = rest[:len(douts)], rest[len(douts):-1]
        j = pl.program_id(0)
        T = CONV_ROWS
        w = [w_ref[k:k + 1, :] for k in range(K)]
        b = b_ref[...]
        zero8 = jnp.zeros((HALO, LANES), F32)
        dy_s[S:S + HALO, :] = zero8

        def first_pass(r0, acc):
            taps = _taps_above(_win_above(x_ref, r0), K)
            y = _conv_taps(taps, w, b)
            d = d_refs[0][pl.ds(r0, T), :]
            for i in range(1, len(douts)):
                d = jnp.where(j >= starts[i], d_refs[i][pl.ds(r0, T), :], d)
            dy = d * _silu_grad(y)
            dy_s[pl.ds(r0, T), :] = dy
            return tuple(acc[k] + _fold8(dy * taps[k]) for k in range(K)) + (acc[K] + _fold8(dy),)

        acc = _chunk_loop(S, first_pass, tuple(zero8 for _ in range(K + 1)))
        for k in range(K):
            dw_ref[k:k + 1, :] = jnp.sum(acc[k], axis=0, keepdims=True)
        db_ref[...] = jnp.sum(acc[K], axis=0, keepdims=True)

        def second_pass(i, carry):
            r0 = pl.multiple_of(i * T, T)
            dx_ref[pl.ds(r0, T), :] = _conv_t_win(dy_s[pl.ds(r0, T + HALO), :], w).astype(BF16)
            return carry

        lax.fori_loop(0, S // T, second_pass, 0)

    col = pl.BlockSpec((S, LANES), lambda j: (0, j))
    wsp = pl.BlockSpec((K, LANES), lambda j: (0, j))
    bsp = pl.BlockSpec((1, LANES), lambda j: (0, j))

    def dspec(i):
        nblk = douts[i].shape[1] // LANES
        return pl.BlockSpec((S, LANES), lambda j: (0, jnp.clip(j - starts[i], 0, nblk - 1)))

    return _call(body, name="conv_silu_bwd", out_shape=(_sds((S, C), BF16), _sds((K, C)), _sds((1, C))), grid=(C // LANES,),
                 in_specs=[col, wsp, bsp] + [dspec(i) for i in range(len(douts))],
                 out_specs=(col, wsp, bsp), scratch=[pltpu.VMEM((S + HALO, LANES), F32)])(x, cw, cb, *douts)


HI = lax.Precision.HIGHEST


def _to_group(x, g, live):
    return jnp.where(live, x if g == 0 else pltpu.roll(x, LANES - SSM_R * g, axis=1), 0.0)


def _from_groups(refs_g):
    out = refs_g[0]
    for g in range(1, SSM_G):
        out = out + pltpu.roll(refs_g[g], SSM_R * g, axis=1)
    return out


def _ssd_prep_fwd(raw, bias_row, alog_row):
    S = raw.shape[0]
    nc = S // SSM_L
    NH = SSM_G * SSM_R

    def body(raw_ref, b_ref, al_ref, pre_ref, dth_ref, dt_ref, acs_ref, acst_ref):
        r_i = lax.broadcasted_iota(jnp.int32, (LANES, LANES), 0)
        c_i = lax.broadcasted_iota(jnp.int32, (LANES, LANES), 1)
        live = c_i < SSM_R
        tril = jnp.where(r_i >= c_i, 1.0, 0.0)
        pre = raw_ref[...] + b_ref[...]
        dt = jnp.where(c_i < NH, jnp.logaddexp(pre, 0.0), 0.0)
        acs = jnp.dot(tril, dt * (-jnp.exp(al_ref[...])), preferred_element_type=F32, precision=HI)
        pre_ref[...] = pre
        dth_ref[...] = dt
        for g in range(SSM_G):
            acs_g = _to_group(acs, g, live)
            dt_ref[g] = _to_group(dt, g, live)
            acs_ref[g] = acs_g
            acst_ref[g] = acs_g.T

    row = pl.BlockSpec((1, LANES), lambda c: (0, 0))
    flat = pl.BlockSpec((SSM_L, LANES), lambda c: (c, 0))
    blk = pl.BlockSpec((SSM_G, SSM_L, LANES), lambda c: (0, c, 0))
    big = _sds((SSM_G, S, LANES))
    return _call(body, name="ssd_prep_fwd",
                 out_shape=(_sds((S, LANES)), _sds((S, LANES)), big, big, _sds((SSM_G, LANES, S))), grid=(nc,),
                 in_specs=[flat, row, row],
                 out_specs=(flat, flat, blk, blk, pl.BlockSpec((SSM_G, LANES, SSM_L), lambda c: (0, 0, c))))(
                     raw, bias_row, alog_row)


def _ssd_prep_bwd(pre_h, dt_h, alog_row, ddt_g, dacs_g, dacst_g):
    S = pre_h.shape[0]
    nc = S // SSM_L
    NH = SSM_G * SSM_R

    def body(pre_ref, dt_ref, al_ref, ddt_ref, dacs_ref, dacst_ref, draw_ref, db_ref, dal_ref):
        c = pl.program_id(0)
        r_i = lax.broadcasted_iota(jnp.int32, (LANES, LANES), 0)
        c_i = lax.broadcasted_iota(jnp.int32, (LANES, LANES), 1)
        triu = jnp.where(r_i <= c_i, 1.0, 0.0)

        @pl.when(c == 0)
        def _():
            db_ref[...] = jnp.zeros_like(db_ref)
            dal_ref[...] = jnp.zeros_like(dal_ref)

        dacs = _from_groups([dacs_ref[g] + dacst_ref[g].T for g in range(SSM_G)])
        da = jnp.dot(triu, dacs, preferred_element_type=F32, precision=HI)
        A = -jnp.exp(al_ref[...])
        ddt = _from_groups([ddt_ref[g] for g in range(SSM_G)]) + da * A
        dpre = jnp.where(c_i < NH, ddt * _sigmoid(pre_ref[...]), 0.0)
        draw_ref[...] = dpre
        db_ref[...] += jnp.sum(dpre, axis=0, keepdims=True)
        dal_ref[...] += jnp.where(c_i[:1] < NH, jnp.sum(da * dt_ref[...], axis=0, keepdims=True) * A, 0.0)

    row = pl.BlockSpec((1, LANES), lambda c: (0, 0))
    flat = pl.BlockSpec((SSM_L, LANES), lambda c: (c, 0))
    blk = pl.BlockSpec((SSM_G, SSM_L, LANES), lambda c: (0, c, 0))
    return _call(body, name="ssd_prep_bwd", out_shape=(_sds((S, LANES)), _sds((1, LANES)), _sds((1, LANES))), grid=(nc,),
                 in_specs=[flat, flat, row, blk, blk, pl.BlockSpec((SSM_G, LANES, SSM_L), lambda c: (0, 0, c))],
                 out_specs=(flat, row, row))(pre_h, dt_h, alog_row, ddt_g, dacs_g, dacst_g)


NT = (((1,), (1,)), ((), ()))
TN = (((0,), (0,)), ((), ()))
SSM_HP = SSM_R * SSM_P


def _ssd_group_terms(xs_ref, dt_ref, acs_ref, d_ref):
    hid = lax.broadcasted_iota(jnp.int32, (1, SSM_HP), 1) // SSM_P
    rid = lax.broadcasted_iota(jnp.int32, (SSM_HP, 1), 0) // SSM_P

    def widen(cols):
        out = cols[0]
        for r in range(1, SSM_R):
            out = jnp.where(hid == r, cols[r], out)
        return out

    dt_c = [dt_ref[:, r:r + 1] for r in range(SSM_R)]
    acs_c = [acs_ref[:, r:r + 1] for r in range(SSM_R)]
    last = [acs_ref[SSM_L - 1:SSM_L, r:r + 1] for r in range(SSM_R)]
    decay_c = [jnp.exp(last[r] - acs_c[r]) for r in range(SSM_R)]
    cd = [jnp.exp(last[r]) for r in range(SSM_R)]
    cd_rows = cd[0]
    for r in range(1, SSM_R):
        cd_rows = jnp.where(rid == r, cd[r], cd_rows)
    xs = xs_ref[...]
    return (xs, xs * widen(dt_c), widen([jnp.exp(a) for a in acs_c]), widen(decay_c),
            widen([d_ref[:, r:r + 1] for r in range(SSM_R)]), cd_rows, dt_c, decay_c, cd)


def _ssd_lmat(acs_ref, acst_ref, r, tril):
    return jnp.exp(jnp.where(tril, acs_ref[:, r:r + 1] - acst_ref[r:r + 1, :], -jnp.inf))


SSM_GPS = 8


def _ssd_specs(rev, nc):
    def cc(c):
        return nc - 1 - c if rev else c
    xs_blocks = (SSM_G * SSM_HP) // (SSM_GPS * SSM_N)
    xs = pl.BlockSpec((SSM_L, SSM_GPS * SSM_HP), lambda g, c: (cc(c), g))
    bsp = pl.BlockSpec((SSM_L, SSM_GPS * SSM_N), lambda g, c: (cc(c), xs_blocks + g))
    csp = pl.BlockSpec((SSM_L, SSM_GPS * SSM_N), lambda g, c: (cc(c), xs_blocks + SSM_G // SSM_GPS + g))
    sc = pl.BlockSpec((SSM_GPS, SSM_L, LANES), lambda g, c: (g, cc(c), 0))
    sct = pl.BlockSpec((SSM_GPS, LANES, SSM_L), lambda g, c: (g, 0, cc(c)))
    gsp = pl.BlockSpec((SSM_GPS, 1, LANES), lambda g, c: (g, 0, 0))
    st = pl.BlockSpec((None, SSM_GPS, SSM_HP, SSM_N), lambda g, c: (cc(c), g, 0, 0))
    return xs, bsp, csp, sc, sct, gsp, st


def _interleave(gens):
    live = list(gens)
    while live:
        for g in list(live):
            try:
                next(g)
            except StopIteration:
                live.remove(g)


def _rounds(gens):
    live = list(gens)
    while live:
        for g in list(live):
            try:
                next(g)
            except StopIteration:
                live.remove(g)
        yield


def _ssd_group_views(gg, xs_ref, b_ref, c_ref, *per_group):
    return (xs_ref.at[:, gg * SSM_HP:(gg + 1) * SSM_HP], b_ref.at[:, gg * SSM_N:(gg + 1) * SSM_N],
            c_ref.at[:, gg * SSM_N:(gg + 1) * SSM_N]) + tuple(r.at[gg] for r in per_group)


def _ssd_fwd(xbc, dt_g, acs_g, acst_g, d_g):
    S = xbc.shape[0]
    nc = S // SSM_L
    xs_s, b_s, c_s, sc, sct, gsp, st = _ssd_specs(False, nc)

    def body(xs_ref, b_ref, c_ref, dt_ref, acs_ref, acst_ref, d_ref, y_ref, st_ref, state):
        c = pl.program_id(1)

        @pl.when(c == 0)
        def _():
            state[...] = jnp.zeros_like(state)

        tril = lax.broadcasted_iota(jnp.int32, (SSM_L, SSM_L), 0) >= lax.broadcasted_iota(jnp.int32, (SSM_L, SSM_L), 1)
        def group(gg):
            xs_v, b_v, c_v, dt_v, acs_v, acst_v, d_v, st_v, state_v = _ssd_group_views(
                gg, xs_ref, b_ref, c_ref, dt_ref, acs_ref, acst_ref, d_ref, st_ref, state)
            y_v = y_ref.at[:, gg * SSM_HP:(gg + 1) * SSM_HP]
            Bb, Cb = b_v[...].astype(BF16), c_v[...].astype(BF16)
            Gm = lax.dot_general(Cb, Bb, NT, preferred_element_type=F32)
            yield
            xs, X, e_all, decay_all, d_all, cd_rows, _, _, _ = _ssd_group_terms(xs_v, dt_v, acs_v, d_v)
            S_all = state_v[...]
            st_v[...] = S_all
            yield
            yo = lax.dot_general(Cb, S_all.astype(BF16), NT, preferred_element_type=F32)
            new_state = lax.dot_general((X * decay_all).astype(BF16), Bb, TN, preferred_element_type=F32)
            yield
            y_v[...] = e_all * yo + d_all * xs
            state_v[...] = S_all * cd_rows + new_state

            def head(r):
                sl = slice(r * SSM_P, (r + 1) * SSM_P)
                M = Gm * _ssd_lmat(acs_v, acst_v, r, tril)
                yield
                y_v[:, sl] += jnp.dot(M.astype(BF16), X[:, sl].astype(BF16), preferred_element_type=F32)

            yield from _rounds([head(r) for r in range(SSM_R)])

        _interleave([group(gg) for gg in range(SSM_GPS)])

    return _call(body, name="ssd_fwd",
                 out_shape=(_sds((S, SSM_G * SSM_HP)), _sds((nc, SSM_G, SSM_HP, SSM_N))),
                 grid=(SSM_G // SSM_GPS, nc), in_specs=[xs_s, b_s, c_s, sc, sc, sct, gsp],
                 out_specs=(xs_s, pl.BlockSpec((None, SSM_GPS, SSM_HP, SSM_N), lambda g, c: (c, g, 0, 0))),
                 scratch=[pltpu.VMEM((SSM_GPS, SSM_HP, SSM_N), F32)])(xbc, xbc, xbc, dt_g, acs_g, acst_g, d_g)


def _ssd_bwd(xbc, dt_g, acs_g, acst_g, d_g, states, dy):
    S = xbc.shape[0]
    nc = S // SSM_L
    xs_s, b_s, c_s, sc, sct, gsp, st = _ssd_specs(True, nc)
    bc_out = pl.BlockSpec((SSM_L, SSM_GPS * SSM_N), lambda g, c: (nc - 1 - c, g))

    def body(xs_ref, b_ref, c_ref, dt_ref, acs_ref, acst_ref, d_ref, st_ref, dy_ref,
             dxs_ref, db_ref, dc_ref, ddt_ref, dacs_ref, dacst_ref, dd_ref, dstate):
        c = pl.program_id(1)

        @pl.when(c == 0)
        def _():
            dstate[...] = jnp.zeros_like(dstate)
            dd_ref[...] = jnp.zeros_like(dd_ref)

        tril = lax.broadcasted_iota(jnp.int32, (SSM_L, SSM_L), 0) >= lax.broadcasted_iota(jnp.int32, (SSM_L, SSM_L), 1)
        lane = lax.broadcasted_iota(jnp.int32, (1, LANES), 1)
        subl = lax.broadcasted_iota(jnp.int32, (LANES, 1), 0)
        last_row = lax.broadcasted_iota(jnp.int32, (SSM_L, 1), 0) == SSM_L - 1
        triu = lax.broadcasted_iota(jnp.int32, (SSM_L, SSM_L), 0) <= lax.broadcasted_iota(jnp.int32, (SSM_L, SSM_L), 1)
        def group(gg):
            xs_v, b_v, c_v, dt_v, acs_v, acst_v, d_v, st_v, ddt_v, dacs_v, dacst_v, dd_v, dstate_v = _ssd_group_views(
                gg, xs_ref, b_ref, c_ref, dt_ref, acs_ref, acst_ref, d_ref, st_ref, ddt_ref, dacs_ref, dacst_ref, dd_ref, dstate)
            dy_v, dxs_v = (r.at[:, gg * SSM_HP:(gg + 1) * SSM_HP] for r in (dy_ref, dxs_ref))
            db_v, dc_v = (r.at[:, gg * SSM_N:(gg + 1) * SSM_N] for r in (db_ref, dc_ref))
            Bb, Cb = b_v[...].astype(BF16), c_v[...].astype(BF16)
            Gm = lax.dot_general(Cb, Bb, NT, preferred_element_type=F32)
            GmT = lax.dot_general(Bb, Cb, NT, preferred_element_type=F32)
            yield
            xs, X, e_all, decay_all, d_all, cd_rows, dt_c, decay_c, cd = _ssd_group_terms(xs_v, dt_v, acs_v, d_v)
            S_all, dSn_all, dY = st_v[...], dstate_v[...], dy_v[...]
            Sb, dSnb = S_all.astype(BF16), dSn_all.astype(BF16)
            yield
            T = lax.dot_general(Cb, Sb, NT, preferred_element_type=F32)
            dT = (dY * e_all).astype(BF16)
            dC = jnp.dot(dT, Sb, preferred_element_type=F32)
            dS_prev = lax.dot_general(dT, Cb, TN, preferred_element_type=F32)
            yield
            yo_dy = dY * (e_all * T)
            W = lax.dot_general(Bb, dSnb, NT, preferred_element_type=F32)
            dB = jnp.dot((X * decay_all).astype(BF16), dSnb, preferred_element_type=F32)
            yield
            xw = X * W
            dcd_rows = jnp.sum(dSn_all * S_all, axis=1, keepdims=True)
            dstate_v[...] = dS_prev + dSn_all * cd_rows
            dX_state = W * decay_all
            yield
            acc = dict(dG=jnp.zeros((SSM_L, SSM_L), F32), dGT=jnp.zeros((SSM_L, SSM_L), F32),
                       ddt=jnp.zeros((SSM_L, LANES), F32), dacs=jnp.zeros((SSM_L, LANES), F32),
                       dacst=jnp.zeros((LANES, SSM_L), F32), dd=jnp.zeros((1, LANES), F32))

            def head(r):
                sl = slice(r * SSM_P, (r + 1) * SSM_P)
                Lm = _ssd_lmat(acs_v, acst_v, r, tril)
                LmT = jnp.exp(jnp.where(triu, acst_v[r:r + 1, :] - acs_v[:, r:r + 1], -jnp.inf))
                M = Gm * Lm
                yield
                dYh, xs_h = dY[:, sl], xs[:, sl]
                dYb, Xb = dYh.astype(BF16), X[:, sl].astype(BF16)
                dM = lax.dot_general(dYb, Xb, NT, preferred_element_type=F32)
                yield
                dX = jnp.dot((GmT * LmT).astype(BF16), dYb, preferred_element_type=F32) + dX_state[:, sl]
                acc["dG"] = acc["dG"] + dM * Lm
                acc["dGT"] = acc["dGT"] + lax.dot_general(Xb, dYb, NT, preferred_element_type=F32) * LmT
                yield
                dseg = dM * M
                dd = jnp.sum(xw[:, sl], axis=1, keepdims=True) * decay_c[r]
                dcd = jnp.sum(dcd_rows[sl])
                dacs_col = (jnp.sum(dseg, axis=1, keepdims=True) + jnp.sum(yo_dy[:, sl], axis=1, keepdims=True) - dd
                            + jnp.where(last_row, dcd * cd[r] + jnp.sum(dd), 0.0))
                dacs_row = -jnp.sum(dseg, axis=0, keepdims=True)
                yield
                dxs_v[:, sl] = dX * dt_c[r] + d_all[:, sl] * dYh
                acc["ddt"] = acc["ddt"] + jnp.where(lane == r, jnp.sum(dX * xs_h, axis=1, keepdims=True), 0.0)
                acc["dacs"] = acc["dacs"] + jnp.where(lane == r, dacs_col, 0.0)
                acc["dacst"] = acc["dacst"] + jnp.where(subl == r, dacs_row, 0.0)
                acc["dd"] = acc["dd"] + jnp.where(lane == r, jnp.sum(dYh * xs_h), 0.0)

            yield from _rounds([head(r) for r in range(SSM_R)])
            dc_v[...] = dC + jnp.dot(acc["dG"].astype(BF16), Bb, preferred_element_type=F32)
            db_v[...] = dB + jnp.dot(acc["dGT"].astype(BF16), Cb, preferred_element_type=F32)
            ddt_v[...] = acc["ddt"]
            dacs_v[...] = acc["dacs"]
            dacst_v[...] = acc["dacst"]
            dd_v[...] += acc["dd"]

        _interleave([group(gg) for gg in range(SSM_GPS)])

    big = _sds((SSM_G, S, LANES))
    return _call(body, name="ssd_bwd",
                 out_shape=(_sds((S, SSM_G * SSM_HP)), _sds((S, SSM_G * SSM_N)), _sds((S, SSM_G * SSM_N)),
                            big, big, _sds((SSM_G, LANES, S)), _sds((SSM_G, 1, LANES))),
                 grid=(SSM_G // SSM_GPS, nc), in_specs=[xs_s, b_s, c_s, sc, sc, sct, gsp, st, xs_s],
                 out_specs=(xs_s, bc_out, bc_out, sc, sc, sct, gsp),
                 scratch=[pltpu.VMEM((SSM_GPS, SSM_HP, SSM_N), F32)])(xbc, xbc, xbc, dt_g, acs_g, acst_g, d_g, states, dy)


def _gate_norm_fwd(y, proj, w):
    S, DI = y.shape
    tm = _tile(S, 256)

    def body(y_ref, z_ref, w_ref, o_ref):
        z = z_ref[...]
        gn = y_ref[...] * (z * _sigmoid(z))
        r = lax.rsqrt(jnp.mean(gn * gn, axis=-1, keepdims=True) + SSM_NORM_EPS)
        o_ref[...] = (gn * r * w_ref[...]).astype(BF16)

    row = pl.BlockSpec((tm, DI), lambda i: (i, 0))
    return _call(body, name="gate_norm_fwd", out_shape=_sds((S, DI), BF16), grid=(S // tm,),
                 in_specs=[row, row, pl.BlockSpec((1, DI), lambda i: (0, 0))], out_specs=row)(y, proj, w)


def _gate_norm_bwd(y, proj, w, dout):
    S, DI = y.shape
    tm = _tile(S, 256)

    def body(y_ref, z_ref, w_ref, d_ref, dy_ref, dz_ref, dw_ref):
        z, yv = z_ref[...], y_ref[...]
        sz = z * _sigmoid(z)
        dgn, dw = _norm_bwd_math(yv * sz, w_ref[...], d_ref[...].astype(F32), SSM_NORM_EPS)
        dy_ref[...] = dgn * sz
        dz_ref[...] = (dgn * yv * _silu_grad(z)).astype(BF16)

        @pl.when(pl.program_id(0) == 0)
        def _():
            dw_ref[...] = jnp.zeros_like(dw_ref)

        dw_ref[...] += dw

    row = pl.BlockSpec((tm, DI), lambda i: (i, 0))
    vec = pl.BlockSpec((1, DI), lambda i: (0, 0))
    return _call(body, name="gate_norm_bwd", out_shape=(_sds((S, DI)), _sds((S, DI), BF16), _sds((1, DI))), grid=(S // tm,),
                 in_specs=[row, row, vec, row], out_specs=(row, row, vec))(y, proj, w, dout)


def _group_major(v):
    return jnp.pad(v.reshape(SSM_G, 1, SSM_R), ((0, 0), (0, 0), (0, LANES - SSM_R)))


def _ungroup(t):
    return t[:, :SSM_R].reshape(1, SSM_G * SSM_R)


def _ffn_fwd(x, P, l, need):
    need(f"ffn{l}_up", x)
    h = _rmsnorm(x, P["norm_ffn"][l:l + 1], name=f"ffn{l}_norm")
    wT = P[f"ffn_w_upT{l}"]
    F = wT.shape[0] // 2
    a = _mm(h, wT, tb=True, name=f"ffn{l}_up")
    need(f"ffn{l}_down", a)
    act = _ffn_mid_fwd(a, P["ffn_conv_w"], P["ffn_conv_b"], l)
    out = _mm(act, P[f"ffn_w_down{l}"], res=x, name=f"ffn{l}_down")
    return out, (x, h, a, act)


def _ffn_bwd(saved, P, l, dx, emit):
    x, h, a, act = saved
    wT = P[f"ffn_w_upT{l}"]
    F = wT.shape[0] // 2
    dact = _mm(dx, P[f"ffn_w_down{l}"], tb=True, out_dtype=BF16, name=f"ffn{l}_down_dx")
    dw_down = _mm(act, dx, ta=True, out_dtype=PAYLOAD, name=f"ffn{l}_down_dw")
    dau, dag, dcw, dcb = _ffn_mid_bwd(a, P["ffn_conv_w"], P["ffn_conv_b"], l, dact)
    dw_upT = _mm(dau, h, ta=True, out_dtype=PAYLOAD, out_rows=(2 * F, 0, None), name=f"ffn{l}_up_u_dw")
    dw_upT = _mm(dag, h, ta=True, out_dtype=PAYLOAD, out_rows=(2 * F, F, dw_upT), name=f"ffn{l}_up_g_dw")
    tie = emit(f"ffn{l}", {"ffn_w_upT": dw_upT, "ffn_w_down": dw_down})
    dh = _mm(dau, wT, b_rows=(0, F), name=f"ffn{l}_up_u_dx")
    dx_in, dnw = _mm_norm_bwd(dag, wT, x, P["norm_ffn"][l:l + 1], dx, res=dh, b_rows=(F, F), after=tie,
                              name=f"ffn{l}_up_g_dx_norm_bwd")
    return dx_in, dnw, dcw, dcb


def _local_step(x, positions, target, P, need, emit, after=None):
    S, D = x.shape
    inv_freq = ROPE_THETA ** (-jnp.arange(0, HEAD_DIM, 2, dtype=F32) / HEAD_DIM)
    inv_freq = jnp.tile(inv_freq, LANES // (HEAD_DIM // 2)).reshape(1, LANES)
    cos, sin = _rope_tables(positions, inv_freq)

    nm0 = P["norm_mix"][0:1]
    h0 = _rmsnorm(x, nm0, name="mix_norm", after=after)
    need("mix_in", h0)
    proj0 = _mm(h0, P["mix_w_inT"], tb=True, name="mix_in")
    cat0 = _attn_fwd(proj0, cos, sin, P["attn_sinks"], _pool_fwd(proj0, P["pool_w"][0], P["pool_scale"]))
    need("mix_out", cat0)
    x1 = _mm(cat0, P["mix_w_out"], res=x, name="mix_out")
    x2, ffn0 = _ffn_fwd(x1, P, 0, need)

    nm1 = P["norm_mix"][1:2]
    need("ssm", x2)
    h1 = _rmsnorm(x2, nm1, name="ssm_norm_in")
    w1T, wdtT = P["ssm_w_inT"], P["ssm_wdtT"]
    DI, CD, NH = P["ssm_norm"].shape[1], P["ssm_conv_w"].shape[1], P["ssm_dt_bias"].shape[1]
    z = _mm(h1, w1T, tb=True, b_rows=(0, DI), name="ssm_in_z")
    xbcp = _mm(h1, w1T, tb=True, b_rows=(DI, CD), name="ssm_in_xbc")
    dtraw = _mm(h1, wdtT, tb=True, name="ssm_in_dt")
    xbc = _conv_silu_fwd(xbcp, P["ssm_conv_w"], P["ssm_conv_b"])
    bias_row = jnp.pad(P["ssm_dt_bias"], ((0, 0), (0, LANES - NH)))
    alog_row = jnp.pad(P["ssm_A_log"], ((0, 0), (0, LANES - NH)))
    d_g = _group_major(P["ssm_D"])
    pre_h, dt_h, dt_g, acs_g, acst_g = _ssd_prep_fwd(dtraw, bias_row, alog_row)
    y, states = _ssd_fwd(xbc, dt_g, acs_g, acst_g, d_g)
    yn = _gate_norm_fwd(y, z, P["ssm_norm"])
    need("ssm_out", yn)
    x3 = _mm(yn, P["ssm_w_out"], res=x2, name="ssm_out")
    x4, ffn1 = _ffn_fwd(x3, P, 1, need)

    loss, dx, d_norm_final = _final_loss(x4, P["norm_final"].reshape(1, D), target, name="final_loss")
    dx, dnf1, dcw1, dcb1 = _ffn_bwd(ffn1, P, 1, dx, emit)
    dyn = _mm(dx, P["ssm_w_out"], tb=True, out_dtype=BF16, name="ssm_out_dx")
    d_w_out1 = _mm(yn, dx, ta=True, out_dtype=PAYLOAD, name="ssm_out_dw")
    dy, dz, d_ssm_norm = _gate_norm_bwd(y, z, P["ssm_norm"], dyn)
    dxs, dB, dC, ddt_g, dacs_g, dacst_g, dd_g = _ssd_bwd(xbc, dt_g, acs_g, acst_g, d_g, states, dy)
    draw, dbias_row, dalog_row = _ssd_prep_bwd(pre_h, dt_h, alog_row, ddt_g, dacs_g, dacst_g)
    dxbc, d_conv_w1, d_conv_b1 = _conv_silu_bwd(xbcp, P["ssm_conv_w"], P["ssm_conv_b"], [dxs, dB, dC])
    rows = DI + CD + NH
    d_w1T = _mm(dz, h1, ta=True, out_dtype=PAYLOAD, out_rows=(rows, 0, None), name="ssm_in_z_dw")
    d_w1T = _mm(dxbc, h1, ta=True, out_dtype=PAYLOAD, out_rows=(rows, DI, d_w1T), name="ssm_in_xbc_dw")
    d_w1T = _mm(draw[:, :NH], h1, ta=True, out_dtype=PAYLOAD, out_rows=(rows, DI + CD, d_w1T), name="ssm_in_dt_dw")
    tie = emit("ssm", {"ssm_w_inT": d_w1T, "ssm_w_out": d_w_out1,
                       "ssm_conv_w": d_conv_w1, "ssm_conv_b": d_conv_b1, "ssm_norm": d_ssm_norm})
    dh1 = _mm(dz, w1T, b_rows=(0, DI), name="ssm_in_z_dx")
    dh1 = _mm(dxbc, w1T, b_rows=(DI, CD), res=dh1, name="ssm_in_xbc_dx")
    dx, dnm1 = _mm_norm_bwd(draw, wdtT, x2, nm1, dx, res=dh1, after=tie, name="ssm_in_dt_dx_norm_bwd")
    dx, dnf0, dcw0, dcb0 = _ffn_bwd(ffn0, P, 0, dx, emit)
    dcat = _mm(dx, P["mix_w_out"], tb=True, name="mix_out_dx")
    d_w_out0 = _mm(cat0, dx, ta=True, out_dtype=PAYLOAD, name="mix_out_dw")
    dproj0, dsk = _attn_bwd(proj0, cos, sin, P["attn_sinks"], dcat)
    dproj0, d_pool_w, d_pool_scale = _pool_bwd(proj0, P["pool_w"][0], P["pool_scale"], dcat, dproj0)
    d_w_in0 = _mm(dproj0, h0, ta=True, out_dtype=PAYLOAD, name="mix_in_dw")
    tie = emit("mix", {"mix_w_inT": d_w_in0, "mix_w_out": d_w_out0, "ffn_conv_w": jnp.stack([dcw0, dcw1])})
    grad_x, dnm0 = _mm_norm_bwd(dproj0, P["mix_w_inT"], x, nm0, dx, after=tie, name="mix_in_dx_norm_bwd")

    small = {
        "norm_mix": jnp.concatenate([dnm0, dnm1], axis=0),
        "norm_ffn": jnp.concatenate([dnf0, dnf1], axis=0),
        "norm_final": d_norm_final,
        "pool_w": d_pool_w,
        "pool_scale": d_pool_scale,
        "attn_sinks_rows": dsk,
        "ssm_dt_bias_row": dbias_row, "ssm_A_log_row": dalog_row, "ssm_D_g": dd_g,
        "ffn_conv_b": jnp.concatenate([dcb0, dcb1], axis=0),
    }
    return loss, grad_x, small


def _peer(k):
    x, y, c = lax.axis_index("x"), lax.axis_index("y"), lax.axis_index("c")
    px = 1 - x if k & 4 else x
    py = 1 - y if k & 2 else y
    pc = 1 - c if k & 1 else c
    return (px, py, pc), 4 * px + 2 * py + pc


def _my_index():
    return 4 * lax.axis_index("x") + 2 * lax.axis_index("y") + lax.axis_index("c")


def _land_sds(a, mode, gather):
    if mode == "slab":
        return _sds(((N_DEV,) + a.shape) if gather else a.shape, a.dtype)
    assert mode == "rows", mode
    return _sds((N_DEV * a.shape[0],) + a.shape[1:] if gather else (N_DEV, a.shape[0] // N_DEV) + a.shape[1:], a.dtype)


def _part(ref, mode, shape, idx):
    if mode == "slab":
        return ref.at[idx]
    r = shape[0] // N_DEV
    return ref.at[pl.ds(idx * r, r)]


def _remote_copies(ops, gather, srcs, lands, send_sems, recv_sems):
    me = _my_index()
    n = len(ops)
    out = []
    for k in range(1, N_DEV):
        dev, idx = _peer(k)
        for i, (a, mode) in enumerate(ops):
            s = srcs[i] if gather else _part(srcs[i], mode, a.shape, idx)
            d = _part(lands[i], mode, _land_sds(a, mode, gather).shape, me) if gather else lands[i].at[me]
            out.append(pltpu.make_async_remote_copy(src_ref=s, dst_ref=d, send_sem=send_sems.at[(k - 1) * n + i],
                                                    recv_sem=recv_sems.at[(k - 1) * n + i], device_id=dev,
                                                    device_id_type=pl.DeviceIdType.MESH))
    return out


HBM = pl.BlockSpec(memory_space=pltpu.HBM)
SEM = pl.BlockSpec(memory_space=pltpu.SEMAPHORE)
SIDE_EFFECT = pltpu.SideEffectType.DATAFLOW_SIDE_EFFECTING


def _in_hbm(a):
    return pltpu.with_memory_space_constraint(a, pltpu.HBM)


def _place_own(ops, *, gather, name):
    n = len(ops)

    def zeros(k):
        return (0,) * k

    in_specs, out_specs = [], []
    for a, mode in ops:
        nd = a.ndim
        if gather and mode == "slab":
            in_specs.append(pl.BlockSpec(a.shape, lambda i, nd=nd: zeros(nd)))
            out_specs.append(pl.BlockSpec((1,) + a.shape, lambda i, nd=nd: (_my_index(),) + zeros(nd)))
        elif gather:
            in_specs.append(pl.BlockSpec(a.shape, lambda i, nd=nd: zeros(nd)))
            out_specs.append(pl.BlockSpec(a.shape, lambda i, nd=nd: (_my_index(),) + zeros(nd - 1)))
        elif mode == "slab":
            in_specs.append(pl.BlockSpec((1,) + a.shape[1:], lambda i, nd=nd: (_my_index(),) + zeros(nd - 1)))
            out_specs.append(pl.BlockSpec((1,) + a.shape[1:], lambda i, nd=nd: (_my_index(),) + zeros(nd - 1)))
        else:
            r = a.shape[0] // N_DEV
            in_specs.append(pl.BlockSpec((r,) + a.shape[1:], lambda i, nd=nd: (_my_index(),) + zeros(nd - 1)))
            out_specs.append(pl.BlockSpec((1, r) + a.shape[1:], lambda i, nd=nd: (_my_index(),) + zeros(nd)))

    def body(*refs):
        for i_ref, o_ref in zip(refs[:n], refs[n:2 * n]):
            if o_ref.shape == i_ref.shape:
                o_ref[...] = i_ref[...]
            else:
                o_ref[0] = i_ref[...]

    outs = _call(body, name=name, grid=(1,), in_specs=in_specs, out_specs=out_specs + [ANY] * n,
                 out_shape=[_land_sds(a, m, gather) for a, m in ops] + [_sds(a.shape, a.dtype) for a, _ in ops],
                 aliases={i: n + i for i in range(n)})(*[a for a, _ in ops])
    return outs[:n], [(src, m) for src, (_, m) in zip(outs[n:], ops)]


def _exchange_start(groups, *, gather, name):
    sizes = [len(ops) for ops, _ in groups]
    n = sum(sizes)
    G = len(groups)

    def body(*refs):
        srcs, lands = refs[:n], refs[n:2 * n]
        sems = refs[2 * n:2 * n + 2 * G]
        token = refs[-1]
        off = 0
        for g, (ops, _) in enumerate(groups):
            for cp in _remote_copies(ops, gather, srcs[off:off + sizes[g]], lands[off:off + sizes[g]], sems[2 * g], sems[2 * g + 1]):
                cp.start()
            off += sizes[g]
        token[...] = jnp.zeros_like(token)

    srcs = [a for ops, _ in groups for a, _ in ops]
    lands = [l for _, ls in groups for l in ls]
    sem_shapes = [pltpu.SemaphoreType.DMA((s * (N_DEV - 1),)) for s in sizes for _ in range(2)]
    outs = pl.pallas_call(
        body, name=name,
        out_shape=sem_shapes + [pltpu.HBM(a.shape, a.dtype) for a in srcs + lands] + [_sds((8, LANES))],
        in_specs=[HBM] * (2 * n), out_specs=[SEM] * (2 * G) + [HBM] * (2 * n) + [pl.BlockSpec(memory_space=pltpu.VMEM)],
        input_output_aliases={i: 2 * G + i for i in range(2 * n)},
        compiler_params=pltpu.CompilerParams(has_side_effects=SIDE_EFFECT))(*[_in_hbm(a) for a in srcs + lands])
    sems, thru, token = outs[:2 * G], outs[2 * G:2 * G + 2 * n], outs[-1]
    states, off = [], 0
    for g, s in enumerate(sizes):
        states.append((sems[2 * g], sems[2 * g + 1], thru[off:off + s], thru[n + off:n + off + s]))
        off += s
    return states, token


def _exchange_wait(ops, state, after, *, gather, name):
    send_sems, recv_sems, srcs, lands = state
    n = len(ops)

    def body(*refs):
        for cp in _remote_copies(ops, gather, refs[:n], refs[n:2 * n], refs[2 * n], refs[2 * n + 1]):
            cp.wait_send()
            cp.wait_recv()

    outs = pl.pallas_call(
        body, name=name, out_shape=[pltpu.HBM(a.shape, a.dtype) for a in list(srcs) + list(lands)],
        in_specs=[HBM] * (2 * n) + [SEM, SEM, ANY], out_specs=[HBM] * (2 * n),
        input_output_aliases={i: i for i in range(2 * n)},
        compiler_params=pltpu.CompilerParams(has_side_effects=SIDE_EFFECT))(*srcs, *lands, send_sems, recv_sems, after)
    return outs[n:]


ADAM_ROWS = 256


def _row_tile(R, cap=ADAM_ROWS):
    best = R
    if R > cap:
        for d in range(16, cap + 1, 16):
            if R % d == 0:
                best = d
    return best


def _adamw(g_layers, w, m, v, *, name):
    L = len(g_layers)
    J, R, Wd = g_layers[0].shape
    assert w.shape == (L, R, Wd), (g_layers[0].shape, w.shape)
    tr = _row_tile(R)
    nrt = R // tr
    c1 = 1.0 / (1.0 - ADAM_B1 ** ADAM_STEP)
    c2 = 1.0 / (1.0 - ADAM_B2 ** ADAM_STEP)

    def body(*refs):
        g_refs = refs[:L]
        w_ref, m_ref, v_ref, go_ref, d_ref, mo_ref, vo_ref = refs[L:]
        layer = pl.program_id(0)
        g = None
        for l, g_ref in enumerate(g_refs):
            gl = g_ref[0].astype(F32)
            for j in range(1, J):
                gl = gl + g_ref[j].astype(F32)
            g = gl if g is None else jnp.where(layer == l, gl, g)
        mn = ADAM_B1 * m_ref[...] + (1.0 - ADAM_B1) * g
        vn = ADAM_B2 * v_ref[...] + (1.0 - ADAM_B2) * (g * g)
        go_ref[...] = g
        mo_ref[...] = mn
        vo_ref[...] = vn
        d_ref[...] = -ADAM_LR * ((mn * c1) / (jnp.sqrt(vn * c2) + ADAM_EPS) + ADAM_WD * w_ref[...])

    def g_spec(l):
        return pl.BlockSpec((J, tr, Wd), lambda ll, i: (0, jnp.where(ll == l, i, jnp.where(ll < l, 0, nrt - 1)), 0))

    row = pl.BlockSpec((None, tr, Wd), lambda ll, i: (ll, i, 0))
    out = _sds((L, R, Wd))
    return _call(body, name=name, out_shape=(out, out, out, out), grid=(L, nrt),
                 in_specs=[g_spec(l) for l in range(L)] + [row, row, row], out_specs=(row, row, row, row))(*g_layers, w, m, v)


def _sum_slabs(slabs, *, name):
    n = len(slabs)

    def body(*refs):
        for g_ref, o_ref in zip(refs[:n], refs[n:]):
            g = g_ref[0]
            for j in range(1, g_ref.shape[0]):
                g = g + g_ref[j]
            o_ref[...] = g

    return _call(body, name=name, out_shape=[_sds(s.shape[1:]) for s in slabs])(*slabs)


def kernel(x, positions, norm_mix, norm_ffn, norm_final, mix_w_in, pool_w, pool_scale, attn_sinks, mix_w_out, ssm_w_in, ssm_conv_w, ssm_conv_b, ssm_dt_bias, ssm_A_log, ssm_D, ssm_norm, ssm_w_out, ffn_w_up, ffn_conv_w, ffn_conv_b, ffn_w_down, loss_target, m_norm_mix, m_norm_ffn, m_norm_final, m_mix_w_in, m_pool_w, m_pool_scale, m_attn_sinks, m_mix_w_out, m_ssm_w_in, m_ssm_conv_w, m_ssm_conv_b, m_ssm_dt_bias, m_ssm_A_log, m_ssm_D, m_ssm_norm, m_ssm_w_out, m_ffn_w_up, m_ffn_conv_w, m_ffn_conv_b, m_ffn_w_down, v_norm_mix, v_norm_ffn, v_norm_final, v_mix_w_in, v_pool_w, v_pool_scale, v_attn_sinks, v_mix_w_out, v_ssm_w_in, v_ssm_conv_w, v_ssm_conv_b, v_ssm_dt_bias, v_ssm_A_log, v_ssm_D, v_ssm_norm, v_ssm_w_out, v_ffn_w_up, v_ffn_conv_w, v_ffn_conv_b, v_ffn_w_down):
    args = dict(locals())
    wl = {n: args[n] for n in WEIGHTS}
    ml = {n: args["m_" + n] for n in WEIGHTS}
    vl = {n: args["v_" + n] for n in WEIGHTS}
    F = ffn_w_down.shape[1] * N_DEV
    DI, CD, NH = ssm_norm.shape[1] * N_DEV, ssm_conv_b.shape[1] * N_DEV, ssm_dt_bias.shape[1]
    Kc, Kf = ssm_conv_w.shape[1], ffn_conv_w.shape[1]
    n_up = ffn_w_up.shape[2]
    col_sharded = ("mix_w_in", "ssm_w_in", "ffn_w_up")

    def tr(a):
        return jnp.swapaxes(a, -1, -2)

    def two(a):
        return a.reshape(-1, a.shape[-1])

    def pay(a):
        return a.astype(PAYLOAD)

    order = ("mix_in", "mix_out", "ffn0_up", "ffn0_down", "ssm", "ssm_out", "ffn1_up", "ffn1_down")
    gops = {
        "mix_in": [(pay(tr(mix_w_in)[0]), "rows")],
        "mix_out": [(pay(mix_w_out[0]), "rows"), (two(ssm_conv_w), "slab"), (ssm_conv_b, "slab"), (ssm_norm, "slab"),
                    (two(ffn_conv_w), "slab")],
        "ffn0_up": [(pay(tr(ffn_w_up)[0]), "rows")], "ffn0_down": [(pay(ffn_w_down[0]), "rows")],
        "ssm": [(pay(tr(ssm_w_in)[0]), "slab")], "ssm_out": [(pay(ssm_w_out[0]), "rows")],
        "ffn1_up": [(pay(tr(ffn_w_up)[1]), "rows")], "ffn1_down": [(pay(ffn_w_down[1]), "rows")],
    }
    lands, handed = _place_own([op for g in order for op in gops[g]], gather=True, name="gather_own")
    groups, off = [], 0
    for g in order:
        gops[g] = handed[off:off + len(gops[g])]
        groups.append((gops[g], lands[off:off + len(gops[g])]))
        off += len(gops[g])
    gstates, token = _exchange_start(groups, gather=True, name="gather_start")
    gstate = dict(zip(order, gstates))
    P = {n: wl[n] for n in REPLICATED}

    def need(g, after):
        got = _exchange_wait(gops[g], gstate[g], after, gather=True, name="gather_wait_" + g)
        if g == "mix_in":
            P["mix_w_inT"] = got[0]
        elif g == "mix_out":
            P.update(mix_w_out=got[0], ssm_conv_w=got[1].transpose(1, 0, 2).reshape(Kc, CD), ssm_conv_b=got[2].reshape(1, CD),
                     ssm_norm=got[3].reshape(1, DI), ffn_conv_w=got[4].transpose(1, 0, 2).reshape(2, Kf, 2 * F))
        elif g == "ssm":
            w1T = got[0].reshape(-1, got[0].shape[-1])
            P.update(ssm_w_inT=w1T, ssm_wdtT=jnp.pad(w1T[DI + CD:], ((0, LANES - NH), (0, 0))))
        elif g == "ssm_out":
            P["ssm_w_out"] = got[0]
        elif g.endswith("_up"):
            P["ffn_w_upT" + g[3]] = got[0]
        else:
            P["ffn_w_down" + g[3]] = got[0]

    sent = {}

    def emit(g, d):
        if g == "mix":
            ops = [(d["mix_w_inT"], "rows"), (d["mix_w_out"], "rows"),
                   (d["ffn_conv_w"].reshape(2 * Kf, N_DEV, n_up).transpose(1, 0, 2), "slab")]
        elif g == "ssm":
            ops = [(d["ssm_w_inT"].reshape(N_DEV, -1, d["ssm_w_inT"].shape[-1]), "slab"), (d["ssm_w_out"], "rows"),
                   (d["ssm_conv_w"].reshape(Kc, N_DEV, -1).transpose(1, 0, 2), "slab"),
                   (d["ssm_conv_b"].reshape(N_DEV, 1, -1), "slab"), (d["ssm_norm"].reshape(N_DEV, 1, -1), "slab")]
        else:
            ops = [(d["ffn_w_upT"], "rows"), (d["ffn_w_down"], "rows")]
        own, ops = _place_own(ops, gather=False, name="scatter_own_" + g)
        (state,), tok = _exchange_start([(ops, own)], gather=False, name="scatter_start_" + g)
        sent[g] = (ops, state)
        return tok

    loss_lanes, grad_x, G = _local_step(x[0], positions.reshape(-1, 1), loss_target[0], P, need, emit, after=token)

    res = {}

    def update(n, g_layers):
        L = len(g_layers)
        g_layers = [g.reshape(g.shape[0], -1, g.shape[-1]) for g in g_layers]
        shape = (L,) + g_layers[0].shape[1:]
        view = tr if n in col_sharded else (lambda a: a)
        outs = _adamw(g_layers, view(wl[n]).reshape(shape), view(ml[n]).reshape(shape), view(vl[n]).reshape(shape),
                      name="adamw_" + n)
        for kind, a in zip(("grad", "delta", "new_m", "new_v"), outs):
            res[kind, n] = view(a.reshape(view(wl[n]).shape))

    rep_ops = [(a, "slab") for a in (G["norm_mix"], G["norm_ffn"], G["norm_final"], G["pool_w"].reshape(-1, LANES),
                                     G["pool_scale"], G["ffn_conv_b"], G["attn_sinks_rows"],
                                     G["ssm_dt_bias_row"], G["ssm_A_log_row"], G["ssm_D_g"].reshape(SSM_G, LANES),
                                     loss_lanes)]
    rep_own, rep_ops = _place_own(rep_ops, gather=True, name="small_own")
    (rep_state,), rep_token = _exchange_start([(rep_ops, rep_own)], gather=True, name="small_start")

    recv = {g: _exchange_wait(sent[g][0], sent[g][1], rep_token, gather=False, name="scatter_wait_" + g)
            for g in ("ffn1", "ssm", "ffn0")}
    update("ssm_w_in", [recv["ssm"][0]])
    update("ssm_w_out", [recv["ssm"][1]])
    update("ssm_conv_w", [recv["ssm"][2]])
    update("ssm_conv_b", [recv["ssm"][3]])
    update("ssm_norm", [recv["ssm"][4]])
    update("ffn_w_up", [recv["ffn0"][0], recv["ffn1"][0]])
    update("ffn_w_down", [recv["ffn0"][1], recv["ffn1"][1]])
    recv["mix"] = _exchange_wait(sent["mix"][0], sent["mix"][1], res["new_v", "ffn_w_down"], gather=False,
                                 name="scatter_wait_mix")
    update("mix_w_in", [recv["mix"][0]])
    update("mix_w_out", [recv["mix"][1]])
    update("ffn_conv_w", [recv["mix"][2]])

    rep = _exchange_wait(rep_ops, rep_state, res["new_v", "mix_w_out"], gather=True, name="small_wait")
    for n, r in zip(("norm_mix", "norm_ffn", "norm_final", "pool_w", "pool_scale", "ffn_conv_b"), rep):
        update(n, [r])
    sinks_rows, bias_row, alog_row, d_g, loss_sum = _sum_slabs(rep[6:], name="sum_head_grads")
    update("attn_sinks", [sinks_rows[:, 0].reshape(1, 1, N_HEADS)])
    update("ssm_dt_bias", [bias_row[:, :NH][None]])
    update("ssm_A_log", [alog_row[:, :NH][None]])
    update("ssm_D", [_ungroup(d_g)[None]])
    loss = loss_sum[0, 0]

    return (loss, grad_x[None], *[res[k, n] for k in ("grad", "delta", "new_m", "new_v") for n in WEIGHTS])
```

```python
import functools
import math

import jax
import jax.numpy as jnp
from jax import lax
from jax.experimental import pallas as pl
from jax.experimental.pallas import tpu as pltpu

F32 = jnp.float32
BF16 = jnp.bfloat16

N_DEV = 8
LANES = 128
HEAD_DIM = 64
N_KV_HEADS = 2
GQ = 4
N_HEADS = N_KV_HEADS * GQ
BLOCK = 128
POOL_GROUPS = 4
ROPE_THETA = 10000.0
SSM_P = 64
SSM_G = 8
SSM_R = 4
SSM_N = 128
SSM_L = 128
NORM_EPS = 1e-6
SSM_NORM_EPS = 1e-5
ADAM_LR, ADAM_B1, ADAM_B2, ADAM_EPS, ADAM_WD, ADAM_STEP = 0.001, 0.9, 0.999, 1e-08, 0.01, 10
VMEM_LIMIT = 56 * 2 ** 20
PAYLOAD = jnp.bfloat16

REPLICATED = ("norm_mix", "norm_ffn", "norm_final", "pool_w", "pool_scale", "attn_sinks",
              "ssm_dt_bias", "ssm_A_log", "ssm_D", "ffn_conv_b")
WEIGHTS = ("norm_mix", "norm_ffn", "norm_final", "mix_w_in", "pool_w", "pool_scale", "attn_sinks", "mix_w_out",
           "ssm_w_in", "ssm_conv_w", "ssm_conv_b", "ssm_dt_bias", "ssm_A_log", "ssm_D", "ssm_norm", "ssm_w_out",
           "ffn_w_up", "ffn_conv_w", "ffn_conv_b", "ffn_w_down")


def _tile(n, cap):
    if n <= cap:
        return n
    best = None
    for d in range(LANES, cap + 1, LANES):
        if n % d == 0:
            best = d
    assert best is not None, (n, cap)
    return best


def _call(body, *, name, out_shape, grid=None, in_specs=None, out_specs=None, scratch=(), aliases=None):
    kw = {}
    if grid is not None:
        kw = dict(grid=grid, in_specs=in_specs, out_specs=out_specs)
    if aliases:
        kw["input_output_aliases"] = aliases
    return pl.pallas_call(
        body, name=name, out_shape=out_shape, scratch_shapes=list(scratch),
        compiler_params=pltpu.CompilerParams(vmem_limit_bytes=VMEM_LIMIT), **kw)


ANY = pl.BlockSpec(memory_space=pl.ANY)


def _sds(shape, dtype=F32):
    return jax.ShapeDtypeStruct(tuple(shape), dtype)


def _sigmoid(x):
    return 1.0 / (1.0 + jnp.exp(-x))


def _shift_dn(x, d, t):
    if d == 0:
        return x
    return jnp.where(t >= d, pltpu.roll(x, d, axis=0), 0.0)


def _shift_up(x, d, t):
    if d == 0:
        return x
    n = x.shape[0]
    return jnp.where(t < n - d, pltpu.roll(x, n - d, axis=0), 0.0)


def _mm(a, b, *, name, ta=False, tb=False, res=None, out_dtype=F32, b_rows=None, out_rows=None):
    M, K = (a.shape[1], a.shape[0]) if ta else a.shape
    b0, bn = b_rows if b_rows is not None else (0, b.shape[0])
    N = bn if tb else b.shape[1]
    assert (b.shape[1] if tb else bn) == K, (a.shape, b.shape, ta, tb, b_rows)
    tm, tn, tk = _tile(M, 1408), _tile(N, 1408), _tile(K, 1408)
    nk = K // tk
    dims = (((0 if ta else 1,), (1 if tb else 0,)), ((), ()))
    aliased = out_rows is not None and out_rows[2] is not None

    def body(*refs):
        a_ref, b_ref = refs[:2]
        r_ref = refs[2] if res is not None else None
        o_ref, acc = refs[-2:]
        k = pl.program_id(2)

        @pl.when(k == 0)
        def _():
            acc[...] = jnp.zeros_like(acc)

        acc[...] += lax.dot_general(a_ref[...].astype(BF16), b_ref[...].astype(BF16), dims,
                                    preferred_element_type=F32)

        @pl.when(k == nk - 1)
        def _():
            out = acc[...]
            if res is not None:
                out = out + r_ref[...]
            o_ref[...] = out.astype(out_dtype)

    a_spec = pl.BlockSpec((tk, tm), lambda i, j, k: (k, i)) if ta else pl.BlockSpec((tm, tk), lambda i, j, k: (i, k))
    if tb:
        assert b0 % tn == 0, (b_rows, tn)
        b_spec = pl.BlockSpec((tn, tk), lambda i, j, k: (b0 // tn + j, k))
    else:
        assert b0 % tk == 0, (b_rows, tk)
        b_spec = pl.BlockSpec((tk, tn), lambda i, j, k: (b0 // tk + k, j))
    ins, specs = [a, b], [a_spec, b_spec]
    if res is not None:
        ins.append(res)
        specs.append(pl.BlockSpec((tm, tn), lambda i, j, k: (i, j)))
    aliases = None
    if out_rows is None:
        o_spec = pl.BlockSpec((tm, tn), lambda i, j, k: (i, j))
        out_shape = _sds((M, N), out_dtype)
    else:
        total, o0, prev = out_rows
        assert o0 % tm == 0, (out_rows, tm)
        o_spec = pl.BlockSpec((tm, tn), lambda i, j, k: (o0 // tm + i, j))
        out_shape = _sds((total, N), out_dtype)
        if aliased:
            aliases = {len(ins): 0}
            ins.append(prev)
            specs.append(ANY)
    return _call(body, name=name, out_shape=out_shape, grid=(M // tm, N // tn, nk), in_specs=specs,
                 out_specs=o_spec, scratch=[pltpu.VMEM((tm, tn), F32)], aliases=aliases)(*ins)


def _rmsnorm(x, w, *, name, eps=NORM_EPS, after=None):
    S, D = x.shape
    tm = _tile(S, 512)
    tie = [] if after is None else [after]

    def body(x_ref, w_ref, *rest):
        o_ref = rest[-1]
        xf = x_ref[...]
        r = lax.rsqrt(jnp.mean(xf * xf, axis=-1, keepdims=True) + eps)
        o_ref[...] = (xf * r * w_ref[...]).astype(BF16)

    return _call(body, name=name, out_shape=_sds((S, D), BF16), grid=(S // tm,),
                 in_specs=[pl.BlockSpec((tm, D), lambda i: (i, 0)), pl.BlockSpec((1, D), lambda i: (0, 0))] + [ANY] * len(tie),
                 out_specs=pl.BlockSpec((tm, D), lambda i: (i, 0)))(x, w, *tie)


def _norm_bwd_math(xf, w, dh, eps):
    r = lax.rsqrt(jnp.mean(xf * xf, axis=-1, keepdims=True) + eps)
    xhat = xf * r
    dxh = dh * w
    dx = r * (dxh - xhat * jnp.mean(dxh * xhat, axis=-1, keepdims=True))
    dw = jnp.sum(dh * xhat, axis=0, keepdims=True)
    return dx, dw


def _mm_norm_bwd(a, b, x, w, dres, *, name, res=None, b_rows=None, after=None, eps=NORM_EPS):
    M, K = a.shape
    b0, bn = b_rows if b_rows is not None else (0, b.shape[0])
    D = b.shape[1]
    assert bn == K and x.shape == (M, D), (a.shape, b.shape, b_rows, x.shape)
    tm, tk = _tile(M, 512), _tile(K, 1408)
    assert b0 % tk == 0, (b_rows, tk)
    nk = K // tk

    def body(*refs):
        a_ref, b_ref, x_ref, w_ref, dr_ref = refs[:5]
        r_ref = refs[5] if res is not None else None
        dx_ref, dw_ref, acc = refs[-3:]
        i, k = pl.program_id(0), pl.program_id(1)

        @pl.when(k == 0)
        def _():
            acc[...] = jnp.zeros_like(acc)

        @pl.when((i == 0) & (k == 0))
        def _():
            dw_ref[...] = jnp.zeros_like(dw_ref)

        acc[...] += jnp.dot(a_ref[...].astype(BF16), b_ref[...].astype(BF16), preferred_element_type=F32)

        @pl.when(k == nk - 1)
        def _():
            dh = acc[...] if res is None else acc[...] + r_ref[...]
            dx, dw = _norm_bwd_math(x_ref[...], w_ref[...], dh, eps)
            dx_ref[...] = dr_ref[...] + dx
            dw_ref[...] += dw

    row = pl.BlockSpec((tm, D), lambda i, k: (i, 0))
    vec = pl.BlockSpec((1, D), lambda i, k: (0, 0))
    ins = [a, b, x, w, dres]
    specs = [pl.BlockSpec((tm, tk), lambda i, k: (i, k)), pl.BlockSpec((tk, D), lambda i, k: (b0 // tk + k, 0)), row, vec, row]
    if res is not None:
        ins.append(res)
        specs.append(row)
    if after is not None:
        ins.append(after)
        specs.append(ANY)
    return _call(body, name=name, out_shape=(_sds((M, D)), _sds((1, D))), grid=(M // tm, nk), in_specs=specs,
                 out_specs=(row, vec), scratch=[pltpu.VMEM((tm, D), F32)])(*ins)


def _final_loss(x, w, target, *, name):
    S, D = x.shape
    tm = _tile(S, 512)

    def body(x_ref, w_ref, t_ref, loss_ref, dx_ref, dw_ref):
        xf, wv = x_ref[...], w_ref[...]
        r = lax.rsqrt(jnp.mean(xf * xf, axis=-1, keepdims=True) + NORM_EPS)
        err = xf * r * wv - t_ref[...]
        part = 0.5 * jnp.sum(jnp.mean(err * err, axis=-1, keepdims=True), axis=0, keepdims=True)
        dx, dw = _norm_bwd_math(xf, wv, err * (1.0 / D), NORM_EPS)
        dx_ref[...] = dx

        @pl.when(pl.program_id(0) == 0)
        def _():
            dw_ref[...] = jnp.zeros_like(dw_ref)
            loss_ref[...] = jnp.zeros_like(loss_ref)

        dw_ref[...] += dw
        loss_ref[...] += jnp.broadcast_to(part, loss_ref.shape)

    row = pl.BlockSpec((tm, D), lambda i: (i, 0))
    vec = pl.BlockSpec((1, D), lambda i: (0, 0))
    return _call(body, name=name, out_shape=(_sds((1, LANES)), _sds((S, D)), _sds((1, D))), grid=(S // tm,),
                 in_specs=[row, vec, row], out_specs=(pl.BlockSpec((1, LANES), lambda i: (0, 0)), row, vec))(x, w, target)


def _rope_tables(pos, inv_freq):
    S = pos.shape[0]
    tm = _tile(S, 512)

    def body(p_ref, f_ref, c_ref, s_ref):
        ang = p_ref[...].astype(F32) * f_ref[...]
        c_ref[...] = jnp.cos(ang)
        s_ref[...] = jnp.sin(ang)

    blk = pl.BlockSpec((tm, LANES), lambda i: (i, 0))
    return _call(body, name="rope_tables", out_shape=(_sds((S, LANES)), _sds((S, LANES))), grid=(S // tm,),
                 in_specs=[pl.BlockSpec((tm, 1), lambda i: (i, 0)), pl.BlockSpec((1, LANES), lambda i: (0, 0))],
                 out_specs=(blk, blk))(pos, inv_freq)


def _rot_half(t):
    lane = lax.broadcasted_iota(jnp.int32, t.shape, 1)
    lo = (lane % HEAD_DIM) < (HEAD_DIM // 2)
    return jnp.where(lo, -pltpu.roll(t, LANES - HEAD_DIM // 2, axis=1), pltpu.roll(t, HEAD_DIM // 2, axis=1))


def _rope(t, c, s):
    return t * c + _rot_half(t) * s


def _unrope(dy, c, s):
    return dy * c - _rot_half(dy * s)


PD = POOL_GROUPS * LANES
QD = N_HEADS * HEAD_DIM
KD = N_KV_HEADS * HEAD_DIM
assert PD % QD == 0 and (PD + QD) % (2 * KD) == 0 and KD == LANES
def _attn_probs(q, kcat, sink, mask):
    s = lax.dot_general(q.astype(BF16), kcat, (((1,), (1,)), ((), ())), preferred_element_type=F32) * (HEAD_DIM ** -0.5)
    s = jnp.where(mask, s, -jnp.inf)
    m = jnp.maximum(jnp.max(s, axis=1, keepdims=True), sink)
    p = jnp.exp(s - m)
    ps = jnp.exp(sink - m)
    inv = 1.0 / (jnp.sum(p, axis=1, keepdims=True) + ps)
    return p * inv, ps * inv


def _attn_mask(n):
    qi = lax.broadcasted_iota(jnp.int32, (BLOCK, 2 * BLOCK), 0)
    kj = lax.broadcasted_iota(jnp.int32, (BLOCK, 2 * BLOCK), 1)
    rel = qi + BLOCK - kj
    return (rel >= 0) & (rel < BLOCK) & ((n > 0) | (kj >= BLOCK))


def _attn_in_specs(nb):
    def cur(n):
        return jnp.minimum(n, nb - 1)

    def prev(n):
        return jnp.clip(n - 1, 0, nb - 1)

    kvb = (PD + QD) // (2 * KD)
    return [pl.BlockSpec(memory_space=pltpu.SMEM),
            pl.BlockSpec((BLOCK, QD), lambda n: (cur(n), PD // QD)),
            pl.BlockSpec((BLOCK, 2 * KD), lambda n: (cur(n), kvb)),
            pl.BlockSpec((BLOCK, 2 * KD), lambda n: (prev(n), kvb)),
            pl.BlockSpec((BLOCK, LANES), lambda n: (cur(n), 0)), pl.BlockSpec((BLOCK, LANES), lambda n: (cur(n), 0)),
            pl.BlockSpec((BLOCK, LANES), lambda n: (prev(n), 0)), pl.BlockSpec((BLOCK, LANES), lambda n: (prev(n), 0))]


def _attn_keys(kvc_ref, kvp_ref, cc, sc, cp, sp):
    kc = _rope(kvc_ref[:, :KD], cc, sc)
    kp = _rope(kvp_ref[:, :KD], cp, sp)
    vc, vp = kvc_ref[:, KD:], kvp_ref[:, KD:]
    kcat, vcat = [], []
    for kk in range(N_KV_HEADS):
        sl = slice(kk * HEAD_DIM, (kk + 1) * HEAD_DIM)
        kcat.append(jnp.concatenate([kp[:, sl], kc[:, sl]], axis=0).astype(BF16))
        vcat.append(jnp.concatenate([vp[:, sl], vc[:, sl]], axis=0).astype(BF16))
    return kcat, vcat


def _attn_fwd(proj, cos, sin, sinks, cat):
    S = proj.shape[0]
    nb = S // BLOCK

    def body(sink_ref, q_ref, kvc_ref, kvp_ref, cc_ref, sc_ref, cp_ref, sp_ref, cat_ref, o_ref):
        n = pl.program_id(0)
        cc, sc = cc_ref[...], sc_ref[...]
        kcat, vcat = _attn_keys(kvc_ref, kvp_ref, cc, sc, cp_ref[...], sp_ref[...])
        mask = _attn_mask(n)
        def head_pair(j):
            qr = _rope(q_ref[:, j * LANES:(j + 1) * LANES], cc, sc)
            for e in range(LANES // HEAD_DIM):
                yield
                h = j * (LANES // HEAD_DIM) + e
                pn, _ = _attn_probs(qr[:, e * HEAD_DIM:(e + 1) * HEAD_DIM], kcat[h // GQ], sink_ref[0, h], mask)
                yield
                o_ref[:, h * HEAD_DIM:(h + 1) * HEAD_DIM] = jnp.dot(
                    pn.astype(BF16), vcat[h // GQ], preferred_element_type=F32).astype(o_ref.dtype)

        _interleave([head_pair(j) for j in range(QD // LANES)])

    return _call(body, name="attn_fwd", out_shape=_sds(cat.shape, cat.dtype), grid=(nb,),
                 in_specs=_attn_in_specs(nb) + [ANY], out_specs=pl.BlockSpec((BLOCK, QD), lambda n: (n, PD // QD)),
                 aliases={8: 0})(sinks, proj, proj, proj, cos, sin, cos, sin, cat)


def _attn_bwd(proj, cos, sin, sinks, dcat):
    S = proj.shape[0]
    nb = S // BLOCK
    scale = HEAD_DIM ** -0.5
    per = LANES // HEAD_DIM

    def body(sink_ref, q_ref, kvc_ref, kvp_ref, cc_ref, sc_ref, cp_ref, sp_ref, do_ref, o_ref, ds_ref, hold, carry, part, pair):
        n = pl.program_id(0)

        @pl.when(n == 0)
        def _():
            hold[...] = jnp.zeros_like(hold)
            carry[...] = jnp.zeros_like(carry)
            ds_ref[...] = jnp.zeros_like(ds_ref)

        live = jnp.where(n < nb, 1.0, 0.0)
        cc, sc, cp, sp = cc_ref[...], sc_ref[...], cp_ref[...], sp_ref[...]
        kcat, vcat = _attn_keys(kvc_ref, kvp_ref, cc, sc, cp, sp)
        mask = _attn_mask(n)
        o_ref[:, :PD] = jnp.zeros((BLOCK, PD), F32)
        o_ref[:, PD:PD + QD] = hold[...]
        dk = [jnp.zeros((2 * BLOCK, HEAD_DIM), F32) for _ in range(N_KV_HEADS)]
        dv = [jnp.zeros((2 * BLOCK, HEAD_DIM), F32) for _ in range(N_KV_HEADS)]
        row = lax.broadcasted_iota(jnp.int32, (8, LANES), 0)
        acc = {"dsk": jnp.zeros((8, LANES), F32)}

        def head_pair(j):
            qr = _rope(q_ref[:, j * LANES:(j + 1) * LANES], cc, sc)
            for e in range(per):
                yield
                h = j * per + e
                kk = h // GQ
                qh = qr[:, e * HEAD_DIM:(e + 1) * HEAD_DIM]
                pn, psn = _attn_probs(qh, kcat[kk], sink_ref[0, h], mask)
                yield
                doh = (do_ref[:, h * HEAD_DIM:(h + 1) * HEAD_DIM] * live).astype(BF16)
                dp = lax.dot_general(doh, vcat[kk], NT, preferred_element_type=F32)
                yield
                delta = jnp.sum(pn * dp, axis=1, keepdims=True)
                ds = (pn * (dp - delta) * scale).astype(BF16)
                pair[j, :, e * HEAD_DIM:(e + 1) * HEAD_DIM] = jnp.dot(ds, kcat[kk], preferred_element_type=F32)
                yield
                dk[kk] = dk[kk] + lax.dot_general(ds, qh.astype(BF16), TN, preferred_element_type=F32)
                dv[kk] = dv[kk] + lax.dot_general(pn.astype(BF16), doh, TN, preferred_element_type=F32)
                acc["dsk"] = acc["dsk"] + jnp.where(row == h, -jnp.sum(psn * delta), 0.0)
            yield
            hold[:, j * LANES:(j + 1) * LANES] = _unrope(pair[j], cc, sc)

        _interleave([head_pair(j) for j in range(QD // LANES)])
        dsk = acc["dsk"]
        for kk in range(N_KV_HEADS):
            sl = slice(kk * HEAD_DIM, (kk + 1) * HEAD_DIM)
            sv = slice(KD + kk * HEAD_DIM, KD + (kk + 1) * HEAD_DIM)
            part[0, :, sl] = dk[kk][:BLOCK]
            part[0, :, sv] = dv[kk][:BLOCK]
            part[1, :, sl] = dk[kk][BLOCK:]
            part[1, :, sv] = dv[kk][BLOCK:]
        done = carry[...] + part[0]
        o_ref[:, PD + QD:PD + QD + KD] = _unrope(done[:, :KD], cp, sp)
        o_ref[:, PD + QD + KD:] = done[:, KD:]
        carry[...] = part[1]
        ds_ref[...] += dsk

    return _call(body, name="attn_bwd", out_shape=(_sds((S, PD + QD + 2 * KD)), _sds((8, LANES))), grid=(nb + 1,),
                 in_specs=_attn_in_specs(nb) + [pl.BlockSpec((BLOCK, QD), lambda n: (jnp.minimum(n, nb - 1), PD // QD))],
                 out_specs=(pl.BlockSpec((BLOCK, PD + QD + 2 * KD), lambda n: (jnp.maximum(n - 1, 0), 0)),
                            pl.BlockSpec((8, LANES), lambda n: (0, 0))),
                 scratch=[pltpu.VMEM((BLOCK, QD), F32), pltpu.VMEM((BLOCK, 2 * KD), F32),
                          pltpu.VMEM((2, BLOCK, 2 * KD), F32), pltpu.VMEM((QD // LANES, BLOCK, LANES), F32)])(
                     sinks, proj, proj, proj, cos, sin, cos, sin, dcat)


def _pool_sums(u, g, t, shift):
    s2 = u + shift(u, 1, t)
    s4 = s2 + shift(s2, 2, t)
    s8 = s4 + shift(s4, 4, t)
    s16 = s8 + shift(s8, 8, t)
    return jnp.where(g == 0, s2, jnp.where(g == 1, s4, jnp.where(g == 2, s8, s16)))


def _pool_specs(S):
    col = pl.BlockSpec((S, LANES), lambda g: (0, g))
    wsp = pl.BlockSpec((1, LANES, LANES), lambda g: (g, 0, 0))
    vec = pl.BlockSpec((1, LANES), lambda g: (0, g))
    return col, wsp, vec


def _pool_fwd(proj, pool_w, scale):
    S = proj.shape[0]
    col, wsp, vec = _pool_specs(S)

    def body(u_ref, w_ref, s_ref, o_ref):
        g = pl.program_id(0)
        u = u_ref[...]
        t = lax.broadcasted_iota(jnp.int32, u.shape, 0)
        cnt = jnp.minimum(t + 1, 2 << g).astype(F32)
        pm = _pool_sums(u, g, t, _shift_dn) / cnt - u
        o_ref[...] = (jnp.dot(pm.astype(BF16), w_ref[0].astype(BF16), preferred_element_type=F32) * s_ref[...]).astype(BF16)

    return _call(body, name="pool_fwd", out_shape=_sds((S, PD + QD), BF16), grid=(POOL_GROUPS,),
                 in_specs=[col, wsp, vec], out_specs=col)(proj, pool_w, scale)


def _pool_bwd(proj, pool_w, scale, dcat, dproj):
    S = proj.shape[0]
    col, wsp, vec = _pool_specs(S)

    def body(u_ref, w_ref, s_ref, d_ref, dproj_ref, du_ref, dw_ref, dsc_ref):
        g = pl.program_id(0)
        u = u_ref[...]
        t = lax.broadcasted_iota(jnp.int32, u.shape, 0)
        cnt = jnp.minimum(t + 1, 2 << g).astype(F32)
        pm = (_pool_sums(u, g, t, _shift_dn) / cnt - u).astype(BF16)
        wv = w_ref[0].astype(BF16)
        d = d_ref[...]
        pw = jnp.dot(pm, wv, preferred_element_type=F32)
        dsc_ref[...] = jnp.sum(pw * d, axis=0, keepdims=True)
        dpw = (d * s_ref[...]).astype(BF16)
        dw_ref[0] = lax.dot_general(pm, dpw, (((0,), (0,)), ((), ())), preferred_element_type=F32)
        dpm = lax.dot_general(dpw, wv, (((1,), (1,)), ((), ())), preferred_element_type=F32)
        du_ref[...] = _pool_sums(dpm / cnt, g, t, _shift_up) - dpm

    return _call(body, name="pool_bwd",
                 out_shape=(_sds(dproj.shape), _sds((POOL_GROUPS, LANES, LANES)), _sds((1, POOL_GROUPS * LANES))),
                 grid=(POOL_GROUPS,), in_specs=[col, wsp, vec, col, ANY], out_specs=(col, wsp, vec),
                 aliases={4: 0})(proj, pool_w, scale, dcat, dproj)


def _conv(x, w_ref, b_ref, t):
    K = w_ref.shape[0]
    y = b_ref[...] + jnp.zeros_like(x)
    for k in range(K):
        y = y + w_ref[k:k + 1, :] * _shift_dn(x, K - 1 - k, t)
    return y


def _silu_grad(y):
    sg = _sigmoid(y)
    return sg * (1.0 + y * (1.0 - sg))


CONV_ROWS = 256
HALO = 8


def _win_above(ref, r0):
    if isinstance(r0, int):
        assert r0 == 0
        return jnp.concatenate([jnp.zeros((HALO, ref.shape[1]), F32), ref[0:CONV_ROWS, :]], axis=0)
    return ref[pl.ds(pl.multiple_of(r0 - HALO, HALO), CONV_ROWS + HALO), :]


def _rows_at(win, start):
    if start % 8 == 0:
        return win[start:start + CONV_ROWS]
    base = start // 8 * 8
    return pltpu.roll(win, win.shape[0] - (start - base), axis=0)[base:base + CONV_ROWS]


def _taps_above(win, K):
    return [_rows_at(win, HALO - (K - 1 - k)) for k in range(K)]


def _conv_taps(taps, w, b):
    y = b
    for k in range(len(w)):
        y = y + w[k] * taps[k]
    return y


def _conv_t_win(win, w):
    K = len(w)
    out = None
    for k in range(K):
        d = K - 1 - k
        term = w[k] * _rows_at(win, d)
        out = term if out is None else out + term
    return out


def _fold8(x):
    return jnp.sum(x.reshape(CONV_ROWS // 8, 8, x.shape[-1]), axis=0)


def _chunk_loop(S, step, init):
    carry = step(0, init)
    return lax.fori_loop(1, S // CONV_ROWS, lambda i, c: step(pl.multiple_of(i * CONV_ROWS, CONV_ROWS), c), carry)


def _ffn_mid_specs(S, K, layer, nf):
    return [pl.BlockSpec((S, LANES), lambda j: (0, j)), pl.BlockSpec((S, LANES), lambda j: (0, nf + j)),
            pl.BlockSpec((None, K, LANES), lambda j: (layer, 0, j)), pl.BlockSpec((None, K, LANES), lambda j: (layer, 0, nf + j)),
            pl.BlockSpec((None, 1, LANES), lambda j: (layer, 0, j)), pl.BlockSpec((None, 1, LANES), lambda j: (layer, 0, nf + j))]


def _ffn_mid_fwd(a, cw, cb, layer):
    S, F = a.shape[0], a.shape[1] // 2
    nf = F // LANES
    K = cw.shape[1]

    def body(au_ref, ag_ref, wu_ref, wg_ref, bu_ref, bg_ref, o_ref):
        t = lax.broadcasted_iota(jnp.int32, (S, LANES), 0)
        hu = _conv(au_ref[...], wu_ref, bu_ref, t)
        hg = _conv(ag_ref[...], wg_ref, bg_ref, t)
        o_ref[...] = (hg * _sigmoid(hg) * hu).astype(BF16)

    return _call(body, name="ffn_mid_fwd", out_shape=_sds((S, F), BF16), grid=(nf,),
                 in_specs=_ffn_mid_specs(S, K, layer, nf), out_specs=pl.BlockSpec((S, LANES), lambda j: (0, j)))(
                     a, a, cw, cw, cb[:, None], cb[:, None])


def _ffn_mid_bwd(a, cw, cb, layer, dact):
    S, F = a.shape[0], a.shape[1] // 2
    nf = F // LANES
    K = cw.shape[1]

    def body(au_ref, ag_ref, wu_ref, wg_ref, bu_ref, bg_ref, d_ref, dau_ref, dag_ref, dwu_ref, dwg_ref, dbu_ref, dbg_ref,
             dhu_s, dhg_s):
        T = CONV_ROWS
        wu = [wu_ref[k:k + 1, :] for k in range(K)]
        wg = [wg_ref[k:k + 1, :] for k in range(K)]
        bu, bg = bu_ref[...], bg_ref[...]
        zero8 = jnp.zeros((HALO, LANES), F32)
        dhu_s[S:S + HALO, :] = zero8
        dhg_s[S:S + HALO, :] = zero8

        def first_pass(r0, acc):
            tu, tg = _taps_above(_win_above(au_ref, r0), K), _taps_above(_win_above(ag_ref, r0), K)
            hu, hg = _conv_taps(tu, wu, bu), _conv_taps(tg, wg, bg)
            d = d_ref[pl.ds(r0, T), :].astype(F32)
            sg = _sigmoid(hg)
            dhu = d * hg * sg
            dhg = d * hu * (sg * (1.0 + hg * (1.0 - sg)))
            dhu_s[pl.ds(r0, T), :] = dhu
            dhg_s[pl.ds(r0, T), :] = dhg
            new = []
            for dh, taps in ((dhu, tu), (dhg, tg)):
                for k in range(K):
                    new.append(acc[len(new)] + _fold8(dh * taps[k]))
            new.append(acc[2 * K] + _fold8(dhu))
            new.append(acc[2 * K + 1] + _fold8(dhg))
            return tuple(new)

        acc = _chunk_loop(S, first_pass, tuple(zero8 for _ in range(2 * K + 2)))
        for k in range(K):
            dwu_ref[k:k + 1, :] = jnp.sum(acc[k], axis=0, keepdims=True)
            dwg_ref[k:k + 1, :] = jnp.sum(acc[K + k], axis=0, keepdims=True)
        dbu_ref[...] = jnp.sum(acc[2 * K], axis=0, keepdims=True)
        dbg_ref[...] = jnp.sum(acc[2 * K + 1], axis=0, keepdims=True)

        def second_pass(i, carry):
            r0 = pl.multiple_of(i * T, T)
            dau_ref[pl.ds(r0, T), :] = _conv_t_win(dhu_s[pl.ds(r0, T + HALO), :], wu).astype(BF16)
            dag_ref[pl.ds(r0, T), :] = _conv_t_win(dhg_s[pl.ds(r0, T + HALO), :], wg).astype(BF16)
            return carry

        lax.fori_loop(0, S // T, second_pass, 0)

    col = pl.BlockSpec((S, LANES), lambda j: (0, j))
    wsp = pl.BlockSpec((K, LANES), lambda j: (0, j))
    bsp = pl.BlockSpec((1, LANES), lambda j: (0, j))
    dau, dag, dwu, dwg, dbu, dbg = _call(
        body, name="ffn_mid_bwd",
        out_shape=(_sds((S, F), BF16), _sds((S, F), BF16), _sds((K, F)), _sds((K, F)), _sds((1, F)), _sds((1, F))), grid=(nf,),
        in_specs=_ffn_mid_specs(S, K, layer, nf) + [col], out_specs=(col, col, wsp, wsp, bsp, bsp),
        scratch=[pltpu.VMEM((S + HALO, LANES), F32), pltpu.VMEM((S + HALO, LANES), F32)])(
            a, a, cw, cw, cb[:, None], cb[:, None], dact)
    return dau, dag, jnp.concatenate([dwu, dwg], axis=1), jnp.concatenate([dbu, dbg], axis=1)


def _conv_silu_fwd(x, cw, cb):
    S = x.shape[0]
    K, C = cw.shape

    def body(x_ref, w_ref, b_ref, o_ref):
        t = lax.broadcasted_iota(jnp.int32, (S, LANES), 0)
        y = _conv(x_ref[...], w_ref, b_ref, t)
        o_ref[...] = y * _sigmoid(y)

    col = pl.BlockSpec((S, LANES), lambda j: (0, j))
    return _call(body, name="conv_silu_fwd", out_shape=_sds((S, C)), grid=(C // LANES,),
                 in_specs=[col, pl.BlockSpec((K, LANES), lambda j: (0, j)), pl.BlockSpec((1, LANES), lambda j: (0, j))],
                 out_specs=col)(x, cw, cb)


def _conv_silu_bwd(x, cw, cb, douts):
    S = x.shape[0]
    K, C = cw.shape
    starts, off = [], 0
    for d in douts:
        starts.append(off)
        off += d.shape[1] // LANES
    assert off == C // LANES

    def body(x_ref, w_ref, b_ref, *rest):
        dy_s = rest[-1]
        d_refs, (dx_ref, dw_ref, db_ref) = rest[:len(douts)], rest[len(douts):-1]
        j = pl.program_id(0)
        T = CONV_ROWS
        w = [w_ref[k:k + 1, :] for k in range(K)]
        b = b_ref[...]
        zero8 = jnp.zeros((HALO, LANES), F32)
        dy_s[S:S + HALO, :] = zero8

        def first_pass(r0, acc):
            taps = _taps_above(_win_above(x_ref, r0), K)
            y = _conv_taps(taps, w, b)
            d = d_refs[0][pl.ds(r0, T), :]
            for i in range(1, len(douts)):
                d = jnp.where(j >= starts[i], d_refs[i][pl.ds(r0, T), :], d)
            dy = d * _silu_grad(y)
            dy_s[pl.ds(r0, T), :] = dy
            return tuple(acc[k] + _fold8(dy * taps[k]) for k in range(K)) + (acc[K] + _fold8(dy),)

        acc = _chunk_loop(S, first_pass, tuple(zero8 for _ in range(K + 1)))
        for k in range(K):
            dw_ref[k:k + 1, :] = jnp.sum(acc[k], axis=0, keepdims=True)
        db_ref[...] = jnp.sum(acc[K], axis=0, keepdims=True)

        def second_pass(i, carry):
            r0 = pl.multiple_of(i * T, T)
            dx_ref[pl.ds(r0, T), :] = _conv_t_win(dy_s[pl.ds(r0, T + HALO), :], w).astype(BF16)
            return carry

        lax.fori_loop(0, S // T, second_pass, 0)

    col = pl.BlockSpec((S, LANES), lambda j: (0, j))
    wsp = pl.BlockSpec((K, LANES), lambda j: (0, j))
    bsp = pl.BlockSpec((1, LANES), lambda j: (0, j))

    def dspec(i):
        nblk = douts[i].shape[1] // LANES
        return pl.BlockSpec((S, LANES), lambda j: (0, jnp.clip(j - starts[i], 0, nblk - 1)))

    return _call(body, name="conv_silu_bwd", out_shape=(_sds((S, C), BF16), _sds((K, C)), _sds((1, C))), grid=(C // LANES,),
                 in_specs=[col, wsp, bsp] + [dspec(i) for i in range(len(douts))],
                 out_specs=(col, wsp, bsp), scratch=[pltpu.VMEM((S + HALO, LANES), F32)])(x, cw, cb, *douts)


HI = lax.Precision.HIGHEST


def _to_group(x, g, live):
    return jnp.where(live, x if g == 0 else pltpu.roll(x, LANES - SSM_R * g, axis=1), 0.0)


def _from_groups(refs_g):
    out = refs_g[0]
    for g in range(1, SSM_G):
        out = out + pltpu.roll(refs_g[g], SSM_R * g, axis=1)
    return out


def _ssd_prep_fwd(raw, bias_row, alog_row):
    S = raw.shape[0]
    nc = S // SSM_L
    NH = SSM_G * SSM_R

    def body(raw_ref, b_ref, al_ref, pre_ref, dth_ref, dt_ref, acs_ref, acst_ref):
        r_i = lax.broadcasted_iota(jnp.int32, (LANES, LANES), 0)
        c_i = lax.broadcasted_iota(jnp.int32, (LANES, LANES), 1)
        live = c_i < SSM_R
        tril = jnp.where(r_i >= c_i, 1.0, 0.0)
        pre = raw_ref[...] + b_ref[...]
        dt = jnp.where(c_i < NH, jnp.logaddexp(pre, 0.0), 0.0)
        acs = jnp.dot(tril, dt * (-jnp.exp(al_ref[...])), preferred_element_type=F32, precision=HI)
        pre_ref[...] = pre
        dth_ref[...] = dt
        for g in range(SSM_G):
            acs_g = _to_group(acs, g, live)
            dt_ref[g] = _to_group(dt, g, live)
            acs_ref[g] = acs_g
            acst_ref[g] = acs_g.T

    row = pl.BlockSpec((1, LANES), lambda c: (0, 0))
    flat = pl.BlockSpec((SSM_L, LANES), lambda c: (c, 0))
    blk = pl.BlockSpec((SSM_G, SSM_L, LANES), lambda c: (0, c, 0))
    big = _sds((SSM_G, S, LANES))
    return _call(body, name="ssd_prep_fwd",
                 out_shape=(_sds((S, LANES)), _sds((S, LANES)), big, big, _sds((SSM_G, LANES, S))), grid=(nc,),
                 in_specs=[flat, row, row],
                 out_specs=(flat, flat, blk, blk, pl.BlockSpec((SSM_G, LANES, SSM_L), lambda c: (0, 0, c))))(
                     raw, bias_row, alog_row)


def _ssd_prep_bwd(pre_h, dt_h, alog_row, ddt_g, dacs_g, dacst_g):
    S = pre_h.shape[0]
    nc = S // SSM_L
    NH = SSM_G * SSM_R

    def body(pre_ref, dt_ref, al_ref, ddt_ref, dacs_ref, dacst_ref, draw_ref, db_ref, dal_ref):
        c = pl.program_id(0)
        r_i = lax.broadcasted_iota(jnp.int32, (LANES, LANES), 0)
        c_i = lax.broadcasted_iota(jnp.int32, (LANES, LANES), 1)
        triu = jnp.where(r_i <= c_i, 1.0, 0.0)

        @pl.when(c == 0)
        def _():
            db_ref[...] = jnp.zeros_like(db_ref)
            dal_ref[...] = jnp.zeros_like(dal_ref)

        dacs = _from_groups([dacs_ref[g] + dacst_ref[g].T for g in range(SSM_G)])
        da = jnp.dot(triu, dacs, preferred_element_type=F32, precision=HI)
        A = -jnp.exp(al_ref[...])
        ddt = _from_groups([ddt_ref[g] for g in range(SSM_G)]) + da * A
        dpre = jnp.where(c_i < NH, ddt * _sigmoid(pre_ref[...]), 0.0)
        draw_ref[...] = dpre
        db_ref[...] += jnp.sum(dpre, axis=0, keepdims=True)
        dal_ref[...] += jnp.where(c_i[:1] < NH, jnp.sum(da * dt_ref[...], axis=0, keepdims=True) * A, 0.0)

    row = pl.BlockSpec((1, LANES), lambda c: (0, 0))
    flat = pl.BlockSpec((SSM_L, LANES), lambda c: (c, 0))
    blk = pl.BlockSpec((SSM_G, SSM_L, LANES), lambda c: (0, c, 0))
    return _call(body, name="ssd_prep_bwd", out_shape=(_sds((S, LANES)), _sds((1, LANES)), _sds((1, LANES))), grid=(nc,),
                 in_specs=[flat, flat, row, blk, blk, pl.BlockSpec((SSM_G, LANES, SSM_L), lambda c: (0, 0, c))],
                 out_specs=(flat, row, row))(pre_h, dt_h, alog_row, ddt_g, dacs_g, dacst_g)


NT = (((1,), (1,)), ((), ()))
TN = (((0,), (0,)), ((), ()))
SSM_HP = SSM_R * SSM_P


def _ssd_group_terms(xs_ref, dt_ref, acs_ref, d_ref):
    hid = lax.broadcasted_iota(jnp.int32, (1, SSM_HP), 1) // SSM_P
    rid = lax.broadcasted_iota(jnp.int32, (SSM_HP, 1), 0) // SSM_P

    def widen(cols):
        out = cols[0]
        for r in range(1, SSM_R):
            out = jnp.where(hid == r, cols[r], out)
        return out

    dt_c = [dt_ref[:, r:r + 1] for r in range(SSM_R)]
    acs_c = [acs_ref[:, r:r + 1] for r in range(SSM_R)]
    last = [acs_ref[SSM_L - 1:SSM_L, r:r + 1] for r in range(SSM_R)]
    decay_c = [jnp.exp(last[r] - acs_c[r]) for r in range(SSM_R)]
    cd = [jnp.exp(last[r]) for r in range(SSM_R)]
    cd_rows = cd[0]
    for r in range(1, SSM_R):
        cd_rows = jnp.where(rid == r, cd[r], cd_rows)
    xs = xs_ref[...]
    return (xs, xs * widen(dt_c), widen([jnp.exp(a) for a in acs_c]), widen(decay_c),
            widen([d_ref[:, r:r + 1] for r in range(SSM_R)]), cd_rows, dt_c, decay_c, cd)


def _ssd_lmat(acs_ref, acst_ref, r, tril):
    return jnp.exp(jnp.where(tril, acs_ref[:, r:r + 1] - acst_ref[r:r + 1, :], -jnp.inf))


SSM_GPS = 8


def _ssd_specs(rev, nc):
    def cc(c):
        return nc - 1 - c if rev else c
    xs_blocks = (SSM_G * SSM_HP) // (SSM_GPS * SSM_N)
    xs = pl.BlockSpec((SSM_L, SSM_GPS * SSM_HP), lambda g, c: (cc(c), g))
    bsp = pl.BlockSpec((SSM_L, SSM_GPS * SSM_N), lambda g, c: (cc(c), xs_blocks + g))
    csp = pl.BlockSpec((SSM_L, SSM_GPS * SSM_N), lambda g, c: (cc(c), xs_blocks + SSM_G // SSM_GPS + g))
    sc = pl.BlockSpec((SSM_GPS, SSM_L, LANES), lambda g, c: (g, cc(c), 0))
    sct = pl.BlockSpec((SSM_GPS, LANES, SSM_L), lambda g, c: (g, 0, cc(c)))
    gsp = pl.BlockSpec((SSM_GPS, 1, LANES), lambda g, c: (g, 0, 0))
    st = pl.BlockSpec((None, SSM_GPS, SSM_HP, SSM_N), lambda g, c: (cc(c), g, 0, 0))
    return xs, bsp, csp, sc, sct, gsp, st


def _interleave(gens):
    live = list(gens)
    while live:
        for g in list(live):
            try:
                next(g)
            except StopIteration:
                live.remove(g)


def _rounds(gens):
    live = list(gens)
    while live:
        for g in list(live):
            try:
                next(g)
            except StopIteration:
                live.remove(g)
        yield


def _ssd_group_views(gg, xs_ref, b_ref, c_ref, *per_group):
    return (xs_ref.at[:, gg * SSM_HP:(gg + 1) * SSM_HP], b_ref.at[:, gg * SSM_N:(gg + 1) * SSM_N],
            c_ref.at[:, gg * SSM_N:(gg + 1) * SSM_N]) + tuple(r.at[gg] for r in per_group)


def _ssd_fwd(xbc, dt_g, acs_g, acst_g, d_g):
    S = xbc.shape[0]
    nc = S // SSM_L
    xs_s, b_s, c_s, sc, sct, gsp, st = _ssd_specs(False, nc)

    def body(xs_ref, b_ref, c_ref, dt_ref, acs_ref, acst_ref, d_ref, y_ref, st_ref, state):
        c = pl.program_id(1)

        @pl.when(c == 0)
        def _():
            state[...] = jnp.zeros_like(state)

        tril = lax.broadcasted_iota(jnp.int32, (SSM_L, SSM_L), 0) >= lax.broadcasted_iota(jnp.int32, (SSM_L, SSM_L), 1)
        def group(gg):
            xs_v, b_v, c_v, dt_v, acs_v, acst_v, d_v, st_v, state_v = _ssd_group_views(
                gg, xs_ref, b_ref, c_ref, dt_ref, acs_ref, acst_ref, d_ref, st_ref, state)
            y_v = y_ref.at[:, gg * SSM_HP:(gg + 1) * SSM_HP]
            Bb, Cb = b_v[...].astype(BF16), c_v[...].astype(BF16)
            Gm = lax.dot_general(Cb, Bb, NT, preferred_element_type=F32)
            yield
            xs, X, e_all, decay_all, d_all, cd_rows, _, _, _ = _ssd_group_terms(xs_v, dt_v, acs_v, d_v)
            S_all = state_v[...]
            st_v[...] = S_all
            yield
            yo = lax.dot_general(Cb, S_all.astype(BF16), NT, preferred_element_type=F32)
            new_state = lax.dot_general((X * decay_all).astype(BF16), Bb, TN, preferred_element_type=F32)
            yield
            y_v[...] = e_all * yo + d_all * xs
            state_v[...] = S_all * cd_rows + new_state

            def head(r):
                sl = slice(r * SSM_P, (r + 1) * SSM_P)
                M = Gm * _ssd_lmat(acs_v, acst_v, r, tril)
                yield
                y_v[:, sl] += jnp.dot(M.astype(BF16), X[:, sl].astype(BF16), preferred_element_type=F32)

            yield from _rounds([head(r) for r in range(SSM_R)])

        _interleave([group(gg) for gg in range(SSM_GPS)])

    return _call(body, name="ssd_fwd",
                 out_shape=(_sds((S, SSM_G * SSM_HP)), _sds((nc, SSM_G, SSM_HP, SSM_N))),
                 grid=(SSM_G // SSM_GPS, nc), in_specs=[xs_s, b_s, c_s, sc, sc, sct, gsp],
                 out_specs=(xs_s, pl.BlockSpec((None, SSM_GPS, SSM_HP, SSM_N), lambda g, c: (c, g, 0, 0))),
                 scratch=[pltpu.VMEM((SSM_GPS, SSM_HP, SSM_N), F32)])(xbc, xbc, xbc, dt_g, acs_g, acst_g, d_g)


def _ssd_bwd(xbc, dt_g, acs_g, acst_g, d_g, states, dy):
    S = xbc.shape[0]
    nc = S // SSM_L
    xs_s, b_s, c_s, sc, sct, gsp, st = _ssd_specs(True, nc)
    bc_out = pl.BlockSpec((SSM_L, SSM_GPS * SSM_N), lambda g, c: (nc - 1 - c, g))

    def body(xs_ref, b_ref, c_ref, dt_ref, acs_ref, acst_ref, d_ref, st_ref, dy_ref,
             dxs_ref, db_ref, dc_ref, ddt_ref, dacs_ref, dacst_ref, dd_ref, dstate):
        c = pl.program_id(1)

        @pl.when(c == 0)
        def _():
            dstate[...] = jnp.zeros_like(dstate)
            dd_ref[...] = jnp.zeros_like(dd_ref)

        tril = lax.broadcasted_iota(jnp.int32, (SSM_L, SSM_L), 0) >= lax.broadcasted_iota(jnp.int32, (SSM_L, SSM_L), 1)
        lane = lax.broadcasted_iota(jnp.int32, (1, LANES), 1)
        subl = lax.broadcasted_iota(jnp.int32, (LANES, 1), 0)
        last_row = lax.broadcasted_iota(jnp.int32, (SSM_L, 1), 0) == SSM_L - 1
        triu = lax.broadcasted_iota(jnp.int32, (SSM_L, SSM_L), 0) <= lax.broadcasted_iota(jnp.int32, (SSM_L, SSM_L), 1)
        def group(gg):
            xs_v, b_v, c_v, dt_v, acs_v, acst_v, d_v, st_v, ddt_v, dacs_v, dacst_v, dd_v, dstate_v = _ssd_group_views(
                gg, xs_ref, b_ref, c_ref, dt_ref, acs_ref, acst_ref, d_ref, st_ref, ddt_ref, dacs_ref, dacst_ref, dd_ref, dstate)
            dy_v, dxs_v = (r.at[:, gg * SSM_HP:(gg + 1) * SSM_HP] for r in (dy_ref, dxs_ref))
            db_v, dc_v = (r.at[:, gg * SSM_N:(gg + 1) * SSM_N] for r in (db_ref, dc_ref))
            Bb, Cb = b_v[...].astype(BF16), c_v[...].astype(BF16)
            Gm = lax.dot_general(Cb, Bb, NT, preferred_element_type=F32)
            GmT = lax.dot_general(Bb, Cb, NT, preferred_element_type=F32)
            yield
            xs, X, e_all, decay_all, d_all, cd_rows, dt_c, decay_c, cd = _ssd_group_terms(xs_v, dt_v, acs_v, d_v)
            S_all, dSn_all, dY = st_v[...], dstate_v[...], dy_v[...]
            Sb, dSnb = S_all.astype(BF16), dSn_all.astype(BF16)
            yield
            T = lax.dot_general(Cb, Sb, NT, preferred_element_type=F32)
            dT = (dY * e_all).astype(BF16)
            dC = jnp.dot(dT, Sb, preferred_element_type=F32)
            dS_prev = lax.dot_general(dT, Cb, TN, preferred_element_type=F32)
            yield
            yo_dy = dY * (e_all * T)
            W = lax.dot_general(Bb, dSnb, NT, preferred_element_type=F32)
            dB = jnp.dot((X * decay_all).astype(BF16), dSnb, preferred_element_type=F32)
            yield
            xw = X * W
            dcd_rows = jnp.sum(dSn_all * S_all, axis=1, keepdims=True)
            dstate_v[...] = dS_prev + dSn_all * cd_rows
            dX_state = W * decay_all
            yield
            acc = dict(dG=jnp.zeros((SSM_L, SSM_L), F32), dGT=jnp.zeros((SSM_L, SSM_L), F32),
                       ddt=jnp.zeros((SSM_L, LANES), F32), dacs=jnp.zeros((SSM_L, LANES), F32),
                       dacst=jnp.zeros((LANES, SSM_L), F32), dd=jnp.zeros((1, LANES), F32))

            def head(r):
                sl = slice(r * SSM_P, (r + 1) * SSM_P)
                Lm = _ssd_lmat(acs_v, acst_v, r, tril)
                LmT = jnp.exp(jnp.where(triu, acst_v[r:r + 1, :] - acs_v[:, r:r + 1], -jnp.inf))
                M = Gm * Lm
                yield
                dYh, xs_h = dY[:, sl], xs[:, sl]
                dYb, Xb = dYh.astype(BF16), X[:, sl].astype(BF16)
                dM = lax.dot_general(dYb, Xb, NT, preferred_element_type=F32)
                yield
                dX = jnp.dot((GmT * LmT).astype(BF16), dYb, preferred_element_type=F32) + dX_state[:, sl]
                acc["dG"] = acc["dG"] + dM * Lm
                acc["dGT"] = acc["dGT"] + lax.dot_general(Xb, dYb, NT, preferred_element_type=F32) * LmT
                yield
                dseg = dM * M
                dd = jnp.sum(xw[:, sl], axis=1, keepdims=True) * decay_c[r]
                dcd = jnp.sum(dcd_rows[sl])
                dacs_col = (jnp.sum(dseg, axis=1, keepdims=True) + jnp.sum(yo_dy[:, sl], axis=1, keepdims=True) - dd
                            + jnp.where(last_row, dcd * cd[r] + jnp.sum(dd), 0.0))
                dacs_row = -jnp.sum(dseg, axis=0, keepdims=True)
                yield
                dxs_v[:, sl] = dX * dt_c[r] + d_all[:, sl] * dYh
                acc["ddt"] = acc["ddt"] + jnp.where(lane == r, jnp.sum(dX * xs_h, axis=1, keepdims=True), 0.0)
                acc["dacs"] = acc["dacs"] + jnp.where(lane == r, dacs_col, 0.0)
                acc["dacst"] = acc["dacst"] + jnp.where(subl == r, dacs_row, 0.0)
                acc["dd"] = acc["dd"] + jnp.where(lane == r, jnp.sum(dYh * xs_h), 0.0)

            yield from _rounds([head(r) for r in range(SSM_R)])
            dc_v[...] = dC + jnp.dot(acc["dG"].astype(BF16), Bb, preferred_element_type=F32)
            db_v[...] = dB + jnp.dot(acc["dGT"].astype(BF16), Cb, preferred_element_type=F32)
            ddt_v[...] = acc["ddt"]
            dacs_v[...] = acc["dacs"]
            dacst_v[...] = acc["dacst"]
            dd_v[...] += acc["dd"]

        _interleave([group(gg) for gg in range(SSM_GPS)])

    big = _sds((SSM_G, S, LANES))
    return _call(body, name="ssd_bwd",
                 out_shape=(_sds((S, SSM_G * SSM_HP)), _sds((S, SSM_G * SSM_N)), _sds((S, SSM_G * SSM_N)),
                            big, big, _sds((SSM_G, LANES, S)), _sds((SSM_G, 1, LANES))),
                 grid=(SSM_G // SSM_GPS, nc), in_specs=[xs_s, b_s, c_s, sc, sc, sct, gsp, st, xs_s],
                 out_specs=(xs_s, bc_out, bc_out, sc, sc, sct, gsp),
                 scratch=[pltpu.VMEM((SSM_GPS, SSM_HP, SSM_N), F32)])(xbc, xbc, xbc, dt_g, acs_g, acst_g, d_g, states, dy)


def _gate_norm_fwd(y, proj, w):
    S, DI = y.shape
    tm = _tile(S, 256)

    def body(y_ref, z_ref, w_ref, o_ref):
        z = z_ref[...]
        gn = y_ref[...] * (z * _sigmoid(z))
        r = lax.rsqrt(jnp.mean(gn * gn, axis=-1, keepdims=True) + SSM_NORM_EPS)
        o_ref[...] = (gn * r * w_ref[...]).astype(BF16)

    row = pl.BlockSpec((tm, DI), lambda i: (i, 0))
    return _call(body, name="gate_norm_fwd", out_shape=_sds((S, DI), BF16), grid=(S // tm,),
                 in_specs=[row, row, pl.BlockSpec((1, DI), lambda i: (0, 0))], out_specs=row)(y, proj, w)


def _gate_norm_bwd(y, proj, w, dout):
    S, DI = y.shape
    tm = _tile(S, 256)

    def body(y_ref, z_ref, w_ref, d_ref, dy_ref, dz_ref, dw_ref):
        z, yv = z_ref[...], y_ref[...]
        sz = z * _sigmoid(z)
        dgn, dw = _norm_bwd_math(yv * sz, w_ref[...], d_ref[...].astype(F32), SSM_NORM_EPS)
        dy_ref[...] = dgn * sz
        dz_ref[...] = (dgn * yv * _silu_grad(z)).astype(BF16)

        @pl.when(pl.program_id(0) == 0)
        def _():
            dw_ref[...] = jnp.zeros_like(dw_ref)

        dw_ref[...] += dw

    row = pl.BlockSpec((tm, DI), lambda i: (i, 0))
    vec = pl.BlockSpec((1, DI), lambda i: (0, 0))
    return _call(body, name="gate_norm_bwd", out_shape=(_sds((S, DI)), _sds((S, DI), BF16), _sds((1, DI))), grid=(S // tm,),
                 in_specs=[row, row, vec, row], out_specs=(row, row, vec))(y, proj, w, dout)


def _group_major(v):
    return jnp.pad(v.reshape(SSM_G, 1, SSM_R), ((0, 0), (0, 0), (0, LANES - SSM_R)))


def _ungroup(t):
    return t[:, :SSM_R].reshape(1, SSM_G * SSM_R)


def _ffn_fwd(x, P, l, need):
    h = _rmsnorm(x, P["norm_ffn"][l:l + 1], name=f"ffn{l}_norm")
    need(f"ffn{l}_up", h)
    a = _mm(h, P[f"ffn_w_upT{l}"], tb=True, name=f"ffn{l}_up")
    need(f"ffn{l}_down", a)
    act = _ffn_mid_fwd(a, P["ffn_conv_w"], P["ffn_conv_b"], l)
    out = _mm(act, P[f"ffn_w_down{l}"], res=x, name=f"ffn{l}_down")
    return out, (x, h, a, act)


def _ffn_bwd(saved, P, l, dx, emit):
    x, h, a, act = saved
    wT = P[f"ffn_w_upT{l}"]
    F = wT.shape[0] // 2
    dact = _mm(dx, P[f"ffn_w_down{l}"], tb=True, out_dtype=BF16, name=f"ffn{l}_down_dx")
    dw_down = _mm(act, dx, ta=True, out_dtype=PAYLOAD, name=f"ffn{l}_down_dw")
    dau, dag, dcw, dcb = _ffn_mid_bwd(a, P["ffn_conv_w"], P["ffn_conv_b"], l, dact)
    dw_upT = _mm(dau, h, ta=True, out_dtype=PAYLOAD, out_rows=(2 * F, 0, None), name=f"ffn{l}_up_u_dw")
    dw_upT = _mm(dag, h, ta=True, out_dtype=PAYLOAD, out_rows=(2 * F, F, dw_upT), name=f"ffn{l}_up_g_dw")
    tie = emit(f"ffn{l}", {"ffn_w_upT": dw_upT, "ffn_w_down": dw_down})
    dh = _mm(dau, wT, b_rows=(0, F), name=f"ffn{l}_up_u_dx")
    dx_in, dnw = _mm_norm_bwd(dag, wT, x, P["norm_ffn"][l:l + 1], dx, res=dh, b_rows=(F, F), after=tie,
                              name=f"ffn{l}_up_g_dx_norm_bwd")
    return dx_in, dnw, dcw, dcb


def _local_step(x, positions, target, P, need, emit, after=None):
    S, D = x.shape
    inv_freq = ROPE_THETA ** (-jnp.arange(0, HEAD_DIM, 2, dtype=F32) / HEAD_DIM)
    inv_freq = jnp.tile(inv_freq, LANES // (HEAD_DIM // 2)).reshape(1, LANES)
    cos, sin = _rope_tables(positions, inv_freq)

    nm0 = P["norm_mix"][0:1]
    h0 = _rmsnorm(x, nm0, name="mix_norm", after=after)
    need("mix_in", h0)
    proj0 = _mm(h0, P["mix_w_inT"], tb=True, name="mix_in")
    cat0 = _attn_fwd(proj0, cos, sin, P["attn_sinks"], _pool_fwd(proj0, P["pool_w"][0], P["pool_scale"]))
    need("mix_out", cat0)
    x1 = _mm(cat0, P["mix_w_out"], res=x, name="mix_out")
    x2, ffn0 = _ffn_fwd(x1, P, 0, need)

    nm1 = P["norm_mix"][1:2]
    h1 = _rmsnorm(x2, nm1, name="ssm_norm_in")
    need("ssm", h1)
    w1T, wdtT = P["ssm_w_inT"], P["ssm_wdtT"]
    DI, CD, NH = P["ssm_norm"].shape[1], P["ssm_conv_w"].shape[1], P["ssm_dt_bias"].shape[1]
    z = _mm(h1, w1T, tb=True, b_rows=(0, DI), name="ssm_in_z")
    xbcp = _mm(h1, w1T, tb=True, b_rows=(DI, CD), name="ssm_in_xbc")
    dtraw = _mm(h1, wdtT, tb=True, name="ssm_in_dt")
    xbc = _conv_silu_fwd(xbcp, P["ssm_conv_w"], P["ssm_conv_b"])
    bias_row = jnp.pad(P["ssm_dt_bias"], ((0, 0), (0, LANES - NH)))
    alog_row = jnp.pad(P["ssm_A_log"], ((0, 0), (0, LANES - NH)))
    d_g = _group_major(P["ssm_D"])
    pre_h, dt_h, dt_g, acs_g, acst_g = _ssd_prep_fwd(dtraw, bias_row, alog_row)
    y, states = _ssd_fwd(xbc, dt_g, acs_g, acst_g, d_g)
    yn = _gate_norm_fwd(y, z, P["ssm_norm"])
    need("ssm_out", yn)
    x3 = _mm(yn, P["ssm_w_out"], res=x2, name="ssm_out")
    x4, ffn1 = _ffn_fwd(x3, P, 1, need)

    loss, dx, d_norm_final = _final_loss(x4, P["norm_final"].reshape(1, D), target, name="final_loss")
    dx, dnf1, dcw1, dcb1 = _ffn_bwd(ffn1, P, 1, dx, emit)
    dyn = _mm(dx, P["ssm_w_out"], tb=True, out_dtype=BF16, name="ssm_out_dx")
    d_w_out1 = _mm(yn, dx, ta=True, out_dtype=PAYLOAD, name="ssm_out_dw")
    dy, dz, d_ssm_norm = _gate_norm_bwd(y, z, P["ssm_norm"], dyn)
    dxs, dB, dC, ddt_g, dacs_g, dacst_g, dd_g = _ssd_bwd(xbc, dt_g, acs_g, acst_g, d_g, states, dy)
    draw, dbias_row, dalog_row = _ssd_prep_bwd(pre_h, dt_h, alog_row, ddt_g, dacs_g, dacst_g)
    dxbc, d_conv_w1, d_conv_b1 = _conv_silu_bwd(xbcp, P["ssm_conv_w"], P["ssm_conv_b"], [dxs, dB, dC])
    rows = DI + CD + NH
    d_w1T = _mm(dz, h1, ta=True, out_dtype=PAYLOAD, out_rows=(rows, 0, None), name="ssm_in_z_dw")
    d_w1T = _mm(dxbc, h1, ta=True, out_dtype=PAYLOAD, out_rows=(rows, DI, d_w1T), name="ssm_in_xbc_dw")
    d_w1T = _mm(draw[:, :NH], h1, ta=True, out_dtype=PAYLOAD, out_rows=(rows, DI + CD, d_w1T), name="ssm_in_dt_dw")
    tie = emit("ssm", {"ssm_w_inT": d_w1T, "ssm_w_out": d_w_out1,
                       "ssm_conv_w": d_conv_w1, "ssm_conv_b": d_conv_b1, "ssm_norm": d_ssm_norm})
    dh1 = _mm(dz, w1T, b_rows=(0, DI), name="ssm_in_z_dx")
    dh1 = _mm(dxbc, w1T, b_rows=(DI, CD), res=dh1, name="ssm_in_xbc_dx")
    dx, dnm1 = _mm_norm_bwd(draw, wdtT, x2, nm1, dx, res=dh1, after=tie, name="ssm_in_dt_dx_norm_bwd")
    dx, dnf0, dcw0, dcb0 = _ffn_bwd(ffn0, P, 0, dx, emit)
    dcat = _mm(dx, P["mix_w_out"], tb=True, name="mix_out_dx")
    d_w_out0 = _mm(cat0, dx, ta=True, out_dtype=PAYLOAD, name="mix_out_dw")
    dproj0, dsk = _attn_bwd(proj0, cos, sin, P["attn_sinks"], dcat)
    dproj0, d_pool_w, d_pool_scale = _pool_bwd(proj0, P["pool_w"][0], P["pool_scale"], dcat, dproj0)
    d_w_in0 = _mm(dproj0, h0, ta=True, out_dtype=PAYLOAD, name="mix_in_dw")
    tie = emit("mix", {"mix_w_inT": d_w_in0, "mix_w_out": d_w_out0, "ffn_conv_w": jnp.stack([dcw0, dcw1])})
    grad_x, dnm0 = _mm_norm_bwd(dproj0, P["mix_w_inT"], x, nm0, dx, after=tie, name="mix_in_dx_norm_bwd")

    small = {
        "norm_mix": jnp.concatenate([dnm0, dnm1], axis=0),
        "norm_ffn": jnp.concatenate([dnf0, dnf1], axis=0),
        "norm_final": d_norm_final,
        "pool_w": d_pool_w,
        "pool_scale": d_pool_scale,
        "attn_sinks_rows": dsk,
        "ssm_dt_bias_row": dbias_row, "ssm_A_log_row": dalog_row, "ssm_D_g": dd_g,
        "ffn_conv_b": jnp.concatenate([dcb0, dcb1], axis=0),
    }
    return loss, grad_x, small


def _peer(k):
    x, y, c = lax.axis_index("x"), lax.axis_index("y"), lax.axis_index("c")
    px = 1 - x if k & 4 else x
    py = 1 - y if k & 2 else y
    pc = 1 - c if k & 1 else c
    return (px, py, pc), 4 * px + 2 * py + pc


def _my_index():
    return 4 * lax.axis_index("x") + 2 * lax.axis_index("y") + lax.axis_index("c")


def _land_sds(a, mode, gather):
    if mode == "slab":
        return _sds(((N_DEV,) + a.shape) if gather else a.shape, a.dtype)
    assert mode == "rows", mode
    return _sds((N_DEV * a.shape[0],) + a.shape[1:] if gather else (N_DEV, a.shape[0] // N_DEV) + a.shape[1:], a.dtype)


def _part(ref, mode, shape, idx):
    if mode == "slab":
        return ref.at[idx]
    r = shape[0] // N_DEV
    return ref.at[pl.ds(idx * r, r)]


def _remote_copies(ops, gather, srcs, lands, send_sems, recv_sems):
    me = _my_index()
    n = len(ops)
    out = []
    for k in range(1, N_DEV):
        dev, idx = _peer(k)
        for i, (a, mode) in enumerate(ops):
            s = srcs[i] if gather else _part(srcs[i], mode, a.shape, idx)
            d = _part(lands[i], mode, _land_sds(a, mode, gather).shape, me) if gather else lands[i].at[me]
            out.append(pltpu.make_async_remote_copy(src_ref=s, dst_ref=d, send_sem=send_sems.at[(k - 1) * n + i],
                                                    recv_sem=recv_sems.at[(k - 1) * n + i], device_id=dev,
                                                    device_id_type=pl.DeviceIdType.MESH))
    return out


HBM = pl.BlockSpec(memory_space=pltpu.HBM)
SEM = pl.BlockSpec(memory_space=pltpu.SEMAPHORE)
SIDE_EFFECT = pltpu.SideEffectType.DATAFLOW_SIDE_EFFECTING


def _in_hbm(a):
    return pltpu.with_memory_space_constraint(a, pltpu.HBM)


def _place_own(ops, *, gather, name):
    n = len(ops)

    def zeros(k):
        return (0,) * k

    in_specs, out_specs = [], []
    for a, mode in ops:
        nd = a.ndim
        if gather and mode == "slab":
            in_specs.append(pl.BlockSpec(a.shape, lambda i, nd=nd: zeros(nd)))
            out_specs.append(pl.BlockSpec((1,) + a.shape, lambda i, nd=nd: (_my_index(),) + zeros(nd)))
        elif gather:
            in_specs.append(pl.BlockSpec(a.shape, lambda i, nd=nd: zeros(nd)))
            out_specs.append(pl.BlockSpec(a.shape, lambda i, nd=nd: (_my_index(),) + zeros(nd - 1)))
        elif mode == "slab":
            in_specs.append(pl.BlockSpec((1,) + a.shape[1:], lambda i, nd=nd: (_my_index(),) + zeros(nd - 1)))
            out_specs.append(pl.BlockSpec((1,) + a.shape[1:], lambda i, nd=nd: (_my_index(),) + zeros(nd - 1)))
        else:
            r = a.shape[0] // N_DEV
            in_specs.append(pl.BlockSpec((r,) + a.shape[1:], lambda i, nd=nd: (_my_index(),) + zeros(nd - 1)))
            out_specs.append(pl.BlockSpec((1, r) + a.shape[1:], lambda i, nd=nd: (_my_index(),) + zeros(nd)))

    def body(*refs):
        for i_ref, o_ref in zip(refs[:n], refs[n:2 * n]):
            if o_ref.shape == i_ref.shape:
                o_ref[...] = i_ref[...]
            else:
                o_ref[0] = i_ref[...]

    outs = _call(body, name=name, grid=(1,), in_specs=in_specs, out_specs=out_specs + [ANY] * n,
                 out_shape=[_land_sds(a, m, gather) for a, m in ops] + [_sds(a.shape, a.dtype) for a, _ in ops],
                 aliases={i: n + i for i in range(n)})(*[a for a, _ in ops])
    return outs[:n], [(src, m) for src, (_, m) in zip(outs[n:], ops)]


def _exchange_start(groups, *, gather, name):
    sizes = [len(ops) for ops, _ in groups]
    n = sum(sizes)
    G = len(groups)

    def body(*refs):
        srcs, lands = refs[:n], refs[n:2 * n]
        sems = refs[2 * n:2 * n + 2 * G]
        token = refs[-1]
        off = 0
        for g, (ops, _) in enumerate(groups):
            for cp in _remote_copies(ops, gather, srcs[off:off + sizes[g]], lands[off:off + sizes[g]], sems[2 * g], sems[2 * g + 1]):
                cp.start()
            off += sizes[g]
        token[...] = jnp.zeros_like(token)

    srcs = [a for ops, _ in groups for a, _ in ops]
    lands = [l for _, ls in groups for l in ls]
    sem_shapes = [pltpu.SemaphoreType.DMA((s * (N_DEV - 1),)) for s in sizes for _ in range(2)]
    outs = pl.pallas_call(
        body, name=name,
        out_shape=sem_shapes + [pltpu.HBM(a.shape, a.dtype) for a in srcs + lands] + [_sds((8, LANES))],
        in_specs=[HBM] * (2 * n), out_specs=[SEM] * (2 * G) + [HBM] * (2 * n) + [pl.BlockSpec(memory_space=pltpu.VMEM)],
        input_output_aliases={i: 2 * G + i for i in range(2 * n)},
        compiler_params=pltpu.CompilerParams(has_side_effects=SIDE_EFFECT))(*[_in_hbm(a) for a in srcs + lands])
    sems, thru, token = outs[:2 * G], outs[2 * G:2 * G + 2 * n], outs[-1]
    states, off = [], 0
    for g, s in enumerate(sizes):
        states.append((sems[2 * g], sems[2 * g + 1], thru[off:off + s], thru[n + off:n + off + s]))
        off += s
    return states, token


def _exchange_wait(ops, state, after, *, gather, name):
    send_sems, recv_sems, srcs, lands = state
    n = len(ops)

    def body(*refs):
        for cp in _remote_copies(ops, gather, refs[:n], refs[n:2 * n], refs[2 * n], refs[2 * n + 1]):
            cp.wait_send()
            cp.wait_recv()

    outs = pl.pallas_call(
        body, name=name, out_shape=[pltpu.HBM(a.shape, a.dtype) for a in list(srcs) + list(lands)],
        in_specs=[HBM] * (2 * n) + [SEM, SEM, ANY], out_specs=[HBM] * (2 * n),
        input_output_aliases={i: i for i in range(2 * n)},
        compiler_params=pltpu.CompilerParams(has_side_effects=SIDE_EFFECT))(*srcs, *lands, send_sems, recv_sems, after)
    return outs[n:]


ADAM_ROWS = 256


def _row_tile(R, cap=ADAM_ROWS):
    best = R
    if R > cap:
        for d in range(16, cap + 1, 16):
            if R % d == 0:
                best = d
    return best


def _adamw(g_layers, w, m, v, *, name):
    L = len(g_layers)
    J, R, Wd = g_layers[0].shape
    assert w.shape == (L, R, Wd), (g_layers[0].shape, w.shape)
    tr = _row_tile(R)
    nrt = R // tr
    c1 = 1.0 / (1.0 - ADAM_B1 ** ADAM_STEP)
    c2 = 1.0 / (1.0 - ADAM_B2 ** ADAM_STEP)

    def body(*refs):
        g_refs = refs[:L]
        w_ref, m_ref, v_ref, go_ref, d_ref, mo_ref, vo_ref = refs[L:]
        layer = pl.program_id(0)
        g = None
        for l, g_ref in enumerate(g_refs):
            gl = g_ref[0].astype(F32)
            for j in range(1, J):
                gl = gl + g_ref[j].astype(F32)
            g = gl if g is None else jnp.where(layer == l, gl, g)
        mn = ADAM_B1 * m_ref[...] + (1.0 - ADAM_B1) * g
        vn = ADAM_B2 * v_ref[...] + (1.0 - ADAM_B2) * (g * g)
        go_ref[...] = g
        mo_ref[...] = mn
        vo_ref[...] = vn
        d_ref[...] = -ADAM_LR * ((mn * c1) / (jnp.sqrt(vn * c2) + ADAM_EPS) + ADAM_WD * w_ref[...])

    def g_spec(l):
        return pl.BlockSpec((J, tr, Wd), lambda ll, i: (0, jnp.where(ll == l, i, jnp.where(ll < l, 0, nrt - 1)), 0))

    row = pl.BlockSpec((None, tr, Wd), lambda ll, i: (ll, i, 0))
    out = _sds((L, R, Wd))
    return _call(body, name=name, out_shape=(out, out, out, out), grid=(L, nrt),
                 in_specs=[g_spec(l) for l in range(L)] + [row, row, row], out_specs=(row, row, row, row))(*g_layers, w, m, v)


def _sum_slabs(slabs, *, name):
    n = len(slabs)

    def body(*refs):
        for g_ref, o_ref in zip(refs[:n], refs[n:]):
            g = g_ref[0]
            for j in range(1, g_ref.shape[0]):
                g = g + g_ref[j]
            o_ref[...] = g

    return _call(body, name=name, out_shape=[_sds(s.shape[1:]) for s in slabs])(*slabs)


def kernel(x, positions, norm_mix, norm_ffn, norm_final, mix_w_in, pool_w, pool_scale, attn_sinks, mix_w_out, ssm_w_in, ssm_conv_w, ssm_conv_b, ssm_dt_bias, ssm_A_log, ssm_D, ssm_norm, ssm_w_out, ffn_w_up, ffn_conv_w, ffn_conv_b, ffn_w_down, loss_target, m_norm_mix, m_norm_ffn, m_norm_final, m_mix_w_in, m_pool_w, m_pool_scale, m_attn_sinks, m_mix_w_out, m_ssm_w_in, m_ssm_conv_w, m_ssm_conv_b, m_ssm_dt_bias, m_ssm_A_log, m_ssm_D, m_ssm_norm, m_ssm_w_out, m_ffn_w_up, m_ffn_conv_w, m_ffn_conv_b, m_ffn_w_down, v_norm_mix, v_norm_ffn, v_norm_final, v_mix_w_in, v_pool_w, v_pool_scale, v_attn_sinks, v_mix_w_out, v_ssm_w_in, v_ssm_conv_w, v_ssm_conv_b, v_ssm_dt_bias, v_ssm_A_log, v_ssm_D, v_ssm_norm, v_ssm_w_out, v_ffn_w_up, v_ffn_conv_w, v_ffn_conv_b, v_ffn_w_down):
    args = dict(locals())
    wl = {n: args[n] for n in WEIGHTS}
    ml = {n: args["m_" + n] for n in WEIGHTS}
    vl = {n: args["v_" + n] for n in WEIGHTS}
    F = ffn_w_down.shape[1] * N_DEV
    DI, CD, NH = ssm_norm.shape[1] * N_DEV, ssm_conv_b.shape[1] * N_DEV, ssm_dt_bias.shape[1]
    Kc, Kf = ssm_conv_w.shape[1], ffn_conv_w.shape[1]
    n_up = ffn_w_up.shape[2]
    col_sharded = ("mix_w_in", "ssm_w_in", "ffn_w_up")

    def tr(a):
        return jnp.swapaxes(a, -1, -2)

    def two(a):
        return a.reshape(-1, a.shape[-1])

    def pay(a):
        return a.astype(PAYLOAD)

    order = ("mix_in", "mix_out", "ffn0_up", "ffn0_down", "ssm", "ssm_out", "ffn1_up", "ffn1_down")
    gops = {
        "mix_in": [(pay(tr(mix_w_in)[0]), "rows")],
        "mix_out": [(pay(mix_w_out[0]), "rows"), (two(ssm_conv_w), "slab"), (ssm_conv_b, "slab"), (ssm_norm, "slab"),
                    (two(ffn_conv_w), "slab")],
        "ffn0_up": [(pay(tr(ffn_w_up)[0]), "rows")], "ffn0_down": [(pay(ffn_w_down[0]), "rows")],
        "ssm": [(pay(tr(ssm_w_in)[0]), "slab")], "ssm_out": [(pay(ssm_w_out[0]), "rows")],
        "ffn1_up": [(pay(tr(ffn_w_up)[1]), "rows")], "ffn1_down": [(pay(ffn_w_down[1]), "rows")],
    }
    lands, handed = _place_own([op for g in order for op in gops[g]], gather=True, name="gather_own")
    groups, off = [], 0
    for g in order:
        gops[g] = handed[off:off + len(gops[g])]
        groups.append((gops[g], lands[off:off + len(gops[g])]))
        off += len(gops[g])
    gstates, token = _exchange_start(groups, gather=True, name="gather_start")
    gstate = dict(zip(order, gstates))
    P = {n: wl[n] for n in REPLICATED}

    def need(g, after):
        got = _exchange_wait(gops[g], gstate[g], after, gather=True, name="gather_wait_" + g)
        if g == "mix_in":
            P["mix_w_inT"] = got[0]
        elif g == "mix_out":
            P.update(mix_w_out=got[0], ssm_conv_w=got[1].transpose(1, 0, 2).reshape(Kc, CD), ssm_conv_b=got[2].reshape(1, CD),
                     ssm_norm=got[3].reshape(1, DI), ffn_conv_w=got[4].transpose(1, 0, 2).reshape(2, Kf, 2 * F))
        elif g == "ssm":
            w1T = got[0].reshape(-1, got[0].shape[-1])
            P.update(ssm_w_inT=w1T, ssm_wdtT=jnp.pad(w1T[DI + CD:], ((0, LANES - NH), (0, 0))))
        elif g == "ssm_out":
            P["ssm_w_out"] = got[0]
        elif g.endswith("_up"):
            P["ffn_w_upT" + g[3]] = got[0]
        else:
            P["ffn_w_down" + g[3]] = got[0]

    sent = {}

    def emit(g, d):
        if g == "mix":
            ops = [(d["mix_w_inT"], "rows"), (d["mix_w_out"], "rows"),
                   (d["ffn_conv_w"].reshape(2 * Kf, N_DEV, n_up).transpose(1, 0, 2), "slab")]
        elif g == "ssm":
            ops = [(d["ssm_w_inT"].reshape(N_DEV, -1, d["ssm_w_inT"].shape[-1]), "slab"), (d["ssm_w_out"], "rows"),
                   (d["ssm_conv_w"].reshape(Kc, N_DEV, -1).transpose(1, 0, 2), "slab"),
                   (d["ssm_conv_b"].reshape(N_DEV, 1, -1), "slab"), (d["ssm_norm"].reshape(N_DEV, 1, -1), "slab")]
        else:
            ops = [(d["ffn_w_upT"], "rows"), (d["ffn_w_down"], "rows")]
        own, ops = _place_own(ops, gather=False, name="scatter_own_" + g)
        (state,), tok = _exchange_start([(ops, own)], gather=False, name="scatter_start_" + g)
        sent[g] = (ops, state)
        return tok

    loss_lanes, grad_x, G = _local_step(x[0], positions.reshape(-1, 1), loss_target[0], P, need, emit, after=token)

    res = {}

    def update(n, g_layers):
        L = len(g_layers)
        g_layers = [g.reshape(g.shape[0], -1, g.shape[-1]) for g in g_layers]
        shape = (L,) + g_layers[0].shape[1:]
        view = tr if n in col_sharded else (lambda a: a)
        outs = _adamw(g_layers, view(wl[n]).reshape(shape), view(ml[n]).reshape(shape), view(vl[n]).reshape(shape),
                      name="adamw_" + n)
        for kind, a in zip(("grad", "delta", "new_m", "new_v"), outs):
            res[kind, n] = view(a.reshape(view(wl[n]).shape))

    rep_ops = [(a, "slab") for a in (G["norm_mix"], G["norm_ffn"], G["norm_final"], G["pool_w"].reshape(-1, LANES),
                                     G["pool_scale"], G["ffn_conv_b"], G["attn_sinks_rows"],
                                     G["ssm_dt_bias_row"], G["ssm_A_log_row"], G["ssm_D_g"].reshape(SSM_G, LANES),
                                     loss_lanes)]
    rep_own, rep_ops = _place_own(rep_ops, gather=True, name="small_own")
    (rep_state,), rep_token = _exchange_start([(rep_ops, rep_own)], gather=True, name="small_start")

    recv = {g: _exchange_wait(sent[g][0], sent[g][1], rep_token, gather=False, name="scatter_wait_" + g)
            for g in ("ffn1", "ssm", "ffn0")}
    update("ssm_w_in", [recv["ssm"][0]])
    update("ssm_w_out", [recv["ssm"][1]])
    update("ssm_conv_w", [recv["ssm"][2]])
    update("ssm_conv_b", [recv["ssm"][3]])
    update("ssm_norm", [recv["ssm"][4]])
    update("ffn_w_up", [recv["ffn0"][0], recv["ffn1"][0]])
    update("ffn_w_down", [recv["ffn0"][1], recv["ffn1"][1]])
    recv["mix"] = _exchange_wait(sent["mix"][0], sent["mix"][1], res["new_v", "ffn_w_down"], gather=False,
                                 name="scatter_wait_mix")
    update("mix_w_in", [recv["mix"][0]])
    update("mix_w_out", [recv["mix"][1]])
    update("ffn_conv_w", [recv["mix"][2]])

    rep = _exchange_wait(rep_ops, rep_state, res["new_v", "mix_w_out"], gather=True, name="small_wait")
    for n, r in zip(("norm_mix", "norm_ffn", "norm_final", "pool_w", "pool_scale", "ffn_conv_b"), rep):
        update(n, [r])
    sinks_rows, bias_row, alog_row, d_g, loss_sum = _sum_slabs(rep[6:], name="sum_head_grads")
    update("attn_sinks", [sinks_rows[:, 0].reshape(1, 1, N_HEADS)])
    update("ssm_dt_bias", [bias_row[:, :NH][None]])
    update("ssm_A_log", [alog_row[:, :NH][None]])
    update("ssm_D", [_ungroup(d_g)[None]])
    loss = loss_sum[0, 0]

    return (loss, grad_x[None], *[res[k, n] for k in ("grad", "delta", "new_m", "new_v") for n in WEIGHTS])
```

```python
import functools
import math

import jax
import jax.numpy as jnp
from jax import lax
from jax.experimental import pallas as pl
from jax.experimental.pallas import tpu as pltpu

F32 = jnp.float32
BF16 = jnp.bfloat16

N_DEV = 8
LANES = 128
HEAD_DIM = 64
N_KV_HEADS = 2
GQ = 4
N_HEADS = N_KV_HEADS * GQ
BLOCK = 128
POOL_GROUPS = 4
ROPE_THETA = 10000.0
SSM_P = 64
SSM_G = 8
SSM_R = 4
SSM_N = 128
SSM_L = 128
NORM_EPS = 1e-6
SSM_NORM_EPS = 1e-5
ADAM_LR, ADAM_B1, ADAM_B2, ADAM_EPS, ADAM_WD, ADAM_STEP = 0.001, 0.9, 0.999, 1e-08, 0.01, 10
VMEM_LIMIT = 56 * 2 ** 20
PAYLOAD = jnp.bfloat16

REPLICATED = ("norm_mix", "norm_ffn", "norm_final", "pool_w", "pool_scale", "attn_sinks",
              "ssm_dt_bias", "ssm_A_log", "ssm_D", "ffn_conv_b")
WEIGHTS = ("norm_mix", "norm_ffn", "norm_final", "mix_w_in", "pool_w", "pool_scale", "attn_sinks", "mix_w_out",
           "ssm_w_in", "ssm_conv_w", "ssm_conv_b", "ssm_dt_bias", "ssm_A_log", "ssm_D", "ssm_norm", "ssm_w_out",
           "ffn_w_up", "ffn_conv_w", "ffn_conv_b", "ffn_w_down")


def _tile(n, cap):
    if n <= cap:
        return n
    best = None
    for d in range(LANES, cap + 1, LANES):
        if n % d == 0:
            best = d
    assert best is not None, (n, cap)
    return best


def _call(body, *, name, out_shape, grid=None, in_specs=None, out_specs=None, scratch=(), aliases=None):
    kw = {}
    if grid is not None:
        kw = dict(grid=grid, in_specs=in_specs, out_specs=out_specs)
    if aliases:
        kw["input_output_aliases"] = aliases
    return pl.pallas_call(
        body, name=name, out_shape=out_shape, scratch_shapes=list(scratch),
        compiler_params=pltpu.CompilerParams(vmem_limit_bytes=VMEM_LIMIT), **kw)


ANY = pl.BlockSpec(memory_space=pl.ANY)


def _sds(shape, dtype=F32):
    return jax.ShapeDtypeStruct(tuple(shape), dtype)


def _sigmoid(x):
    return 1.0 / (1.0 + jnp.exp(-x))


def _shift_dn(x, d, t):
    if d == 0:
        return x
    return jnp.where(t >= d, pltpu.roll(x, d, axis=0), 0.0)


def _shift_up(x, d, t):
    if d == 0:
        return x
    n = x.shape[0]
    return jnp.where(t < n - d, pltpu.roll(x, n - d, axis=0), 0.0)


def _mm(a, b, *, name, ta=False, tb=False, res=None, out_dtype=F32, b_rows=None, out_rows=None):
    M, K = (a.shape[1], a.shape[0]) if ta else a.shape
    b0, bn = b_rows if b_rows is not None else (0, b.shape[0])
    N = bn if tb else b.shape[1]
    assert (b.shape[1] if tb else bn) == K, (a.shape, b.shape, ta, tb, b_rows)
    tm, tn, tk = _tile(M, 1408), _tile(N, 1408), _tile(K, 1408)
    nk = K // tk
    dims = (((0 if ta else 1,), (1 if tb else 0,)), ((), ()))
    aliased = out_rows is not None and out_rows[2] is not None

    def body(*refs):
        a_ref, b_ref = refs[:2]
        r_ref = refs[2] if res is not None else None
        o_ref, acc = refs[-2:]
        k = pl.program_id(2)

        @pl.when(k == 0)
        def _():
            acc[...] = jnp.zeros_like(acc)

        acc[...] += lax.dot_general(a_ref[...].astype(BF16), b_ref[...].astype(BF16), dims,
                                    preferred_element_type=F32)

        @pl.when(k == nk - 1)
        def _():
            out = acc[...]
            if res is not None:
                out = out + r_ref[...]
            o_ref[...] = out.astype(out_dtype)

    a_spec = pl.BlockSpec((tk, tm), lambda i, j, k: (k, i)) if ta else pl.BlockSpec((tm, tk), lambda i, j, k: (i, k))
    if tb:
        assert b0 % tn == 0, (b_rows, tn)
        b_spec = pl.BlockSpec((tn, tk), lambda i, j, k: (b0 // tn + j, k))
    else:
        assert b0 % tk == 0, (b_rows, tk)
        b_spec = pl.BlockSpec((tk, tn), lambda i, j, k: (b0 // tk + k, j))
    ins, specs = [a, b], [a_spec, b_spec]
    if res is not None:
        ins.append(res)
        specs.append(pl.BlockSpec((tm, tn), lambda i, j, k: (i, j)))
    aliases = None
    if out_rows is None:
        o_spec = pl.BlockSpec((tm, tn), lambda i, j, k: (i, j))
        out_shape = _sds((M, N), out_dtype)
    else:
        total, o0, prev = out_rows
        assert o0 % tm == 0, (out_rows, tm)
        o_spec = pl.BlockSpec((tm, tn), lambda i, j, k: (o0 // tm + i, j))
        out_shape = _sds((total, N), out_dtype)
        if aliased:
            aliases = {len(ins): 0}
            ins.append(prev)
            specs.append(ANY)
    return _call(body, name=name, out_shape=out_shape, grid=(M // tm, N // tn, nk), in_specs=specs,
                 out_specs=o_spec, scratch=[pltpu.VMEM((tm, tn), F32)], aliases=aliases)(*ins)


def _rmsnorm(x, w, *, name, eps=NORM_EPS, after=None):
    S, D = x.shape
    tm = _tile(S, 512)
    tie = [] if after is None else [after]

    def body(x_ref, w_ref, *rest):
        o_ref = rest[-1]
        xf = x_ref[...]
        r = lax.rsqrt(jnp.mean(xf * xf, axis=-1, keepdims=True) + eps)
        o_ref[...] = (xf * r * w_ref[...]).astype(BF16)

    return _call(body, name=name, out_shape=_sds((S, D), BF16), grid=(S // tm,),
                 in_specs=[pl.BlockSpec((tm, D), lambda i: (i, 0)), pl.BlockSpec((1, D), lambda i: (0, 0))] + [ANY] * len(tie),
                 out_specs=pl.BlockSpec((tm, D), lambda i: (i, 0)))(x, w, *tie)


def _norm_bwd_math(xf, w, dh, eps):
    r = lax.rsqrt(jnp.mean(xf * xf, axis=-1, keepdims=True) + eps)
    xhat = xf * r
    dxh = dh * w
    dx = r * (dxh - xhat * jnp.mean(dxh * xhat, axis=-1, keepdims=True))
    dw = jnp.sum(dh * xhat, axis=0, keepdims=True)
    return dx, dw


def _mm_norm_bwd(a, b, x, w, dres, *, name, res=None, b_rows=None, after=None, eps=NORM_EPS):
    M, K = a.shape
    b0, bn = b_rows if b_rows is not None else (0, b.shape[0])
    D = b.shape[1]
    assert bn == K and x.shape == (M, D), (a.shape, b.shape, b_rows, x.shape)
    tm, tk = _tile(M, 1024), _tile(K, 1408)
    assert b0 % tk == 0, (b_rows, tk)
    nk = K // tk

    def body(*refs):
        a_ref, b_ref, x_ref, w_ref, dr_ref = refs[:5]
        r_ref = refs[5] if res is not None else None
        dx_ref, dw_ref, acc = refs[-3:]
        i, k = pl.program_id(0), pl.program_id(1)

        @pl.when(k == 0)
        def _():
            acc[...] = jnp.zeros_like(acc)

        @pl.when((i == 0) & (k == 0))
        def _():
            dw_ref[...] = jnp.zeros_like(dw_ref)

        acc[...] += jnp.dot(a_ref[...].astype(BF16), b_ref[...].astype(BF16), preferred_element_type=F32)

        @pl.when(k == nk - 1)
        def _():
            dh = acc[...] if res is None else acc[...] + r_ref[...]
            dx, dw = _norm_bwd_math(x_ref[...], w_ref[...], dh, eps)
            dx_ref[...] = dr_ref[...] + dx
            dw_ref[...] += dw

    row = pl.BlockSpec((tm, D), lambda i, k: (i, 0))
    vec = pl.BlockSpec((1, D), lambda i, k: (0, 0))
    ins = [a, b, x, w, dres]
    specs = [pl.BlockSpec((tm, tk), lambda i, k: (i, k)), pl.BlockSpec((tk, D), lambda i, k: (b0 // tk + k, 0)), row, vec, row]
    if res is not None:
        ins.append(res)
        specs.append(row)
    if after is not None:
        ins.append(after)
        specs.append(ANY)
    return _call(body, name=name, out_shape=(_sds((M, D)), _sds((1, D))), grid=(M // tm, nk), in_specs=specs,
                 out_specs=(row, vec), scratch=[pltpu.VMEM((tm, D), F32)])(*ins)


def _final_loss(x, w, target, *, name):
    S, D = x.shape
    tm = _tile(S, 512)

    def body(x_ref, w_ref, t_ref, loss_ref, dx_ref, dw_ref):
        xf, wv = x_ref[...], w_ref[...]
        r = lax.rsqrt(jnp.mean(xf * xf, axis=-1, keepdims=True) + NORM_EPS)
        err = xf * r * wv - t_ref[...]
        part = 0.5 * jnp.sum(jnp.mean(err * err, axis=-1, keepdims=True), axis=0, keepdims=True)
        dx, dw = _norm_bwd_math(xf, wv, err * (1.0 / D), NORM_EPS)
        dx_ref[...] = dx

        @pl.when(pl.program_id(0) == 0)
        def _():
            dw_ref[...] = jnp.zeros_like(dw_ref)
            loss_ref[...] = jnp.zeros_like(loss_ref)

        dw_ref[...] += dw
        loss_ref[...] += jnp.broadcast_to(part, loss_ref.shape)

    row = pl.BlockSpec((tm, D), lambda i: (i, 0))
    vec = pl.BlockSpec((1, D), lambda i: (0, 0))
    return _call(body, name=name, out_shape=(_sds((1, LANES)), _sds((S, D)), _sds((1, D))), grid=(S // tm,),
                 in_specs=[row, vec, row], out_specs=(pl.BlockSpec((1, LANES), lambda i: (0, 0)), row, vec))(x, w, target)


def _rope_tables(pos, inv_freq):
    S = pos.shape[0]
    tm = _tile(S, 512)

    def body(p_ref, f_ref, c_ref, s_ref):
        ang = p_ref[...].astype(F32) * f_ref[...]
        c_ref[...] = jnp.cos(ang)
        s_ref[...] = jnp.sin(ang)

    blk = pl.BlockSpec((tm, LANES), lambda i: (i, 0))
    return _call(body, name="rope_tables", out_shape=(_sds((S, LANES)), _sds((S, LANES))), grid=(S // tm,),
                 in_specs=[pl.BlockSpec((tm, 1), lambda i: (i, 0)), pl.BlockSpec((1, LANES), lambda i: (0, 0))],
                 out_specs=(blk, blk))(pos, inv_freq)


def _rot_half(t):
    lane = lax.broadcasted_iota(jnp.int32, t.shape, 1)
    lo = (lane % HEAD_DIM) < (HEAD_DIM // 2)
    return jnp.where(lo, -pltpu.roll(t, LANES - HEAD_DIM // 2, axis=1), pltpu.roll(t, HEAD_DIM // 2, axis=1))


def _rope(t, c, s):
    return t * c + _rot_half(t) * s


def _unrope(dy, c, s):
    return dy * c - _rot_half(dy * s)


PD = POOL_GROUPS * LANES
QD = N_HEADS * HEAD_DIM
KD = N_KV_HEADS * HEAD_DIM
assert PD % QD == 0 and (PD + QD) % (2 * KD) == 0 and KD == LANES
def _attn_probs(q, kcat, sink, mask):
    s = lax.dot_general(q.astype(BF16), kcat, (((1,), (1,)), ((), ())), preferred_element_type=F32) * (HEAD_DIM ** -0.5)
    s = jnp.where(mask, s, -jnp.inf)
    m = jnp.maximum(jnp.max(s, axis=1, keepdims=True), sink)
    p = jnp.exp(s - m)
    ps = jnp.exp(sink - m)
    inv = 1.0 / (jnp.sum(p, axis=1, keepdims=True) + ps)
    return p * inv, ps * inv


def _attn_mask(n):
    qi = lax.broadcasted_iota(jnp.int32, (BLOCK, 2 * BLOCK), 0)
    kj = lax.broadcasted_iota(jnp.int32, (BLOCK, 2 * BLOCK), 1)
    rel = qi + BLOCK - kj
    return (rel >= 0) & (rel < BLOCK) & ((n > 0) | (kj >= BLOCK))


def _attn_in_specs(nb):
    def cur(n):
        return jnp.minimum(n, nb - 1)

    def prev(n):
        return jnp.clip(n - 1, 0, nb - 1)

    kvb = (PD + QD) // (2 * KD)
    return [pl.BlockSpec(memory_space=pltpu.SMEM),
            pl.BlockSpec((BLOCK, QD), lambda n: (cur(n), PD // QD)),
            pl.BlockSpec((BLOCK, 2 * KD), lambda n: (cur(n), kvb)),
            pl.BlockSpec((BLOCK, 2 * KD), lambda n: (prev(n), kvb)),
            pl.BlockSpec((BLOCK, LANES), lambda n: (cur(n), 0)), pl.BlockSpec((BLOCK, LANES), lambda n: (cur(n), 0)),
            pl.BlockSpec((BLOCK, LANES), lambda n: (prev(n), 0)), pl.BlockSpec((BLOCK, LANES), lambda n: (prev(n), 0))]


def _attn_keys(kvc_ref, kvp_ref, cc, sc, cp, sp):
    kc = _rope(kvc_ref[:, :KD], cc, sc)
    kp = _rope(kvp_ref[:, :KD], cp, sp)
    vc, vp = kvc_ref[:, KD:], kvp_ref[:, KD:]
    kcat, vcat = [], []
    for kk in range(N_KV_HEADS):
        sl = slice(kk * HEAD_DIM, (kk + 1) * HEAD_DIM)
        kcat.append(jnp.concatenate([kp[:, sl], kc[:, sl]], axis=0).astype(BF16))
        vcat.append(jnp.concatenate([vp[:, sl], vc[:, sl]], axis=0).astype(BF16))
    return kcat, vcat


def _attn_fwd(proj, cos, sin, sinks, cat):
    S = proj.shape[0]
    nb = S // BLOCK

    def body(sink_ref, q_ref, kvc_ref, kvp_ref, cc_ref, sc_ref, cp_ref, sp_ref, cat_ref, o_ref):
        n = pl.program_id(0)
        cc, sc = cc_ref[...], sc_ref[...]
        kcat, vcat = _attn_keys(kvc_ref, kvp_ref, cc, sc, cp_ref[...], sp_ref[...])
        mask = _attn_mask(n)
        def head_pair(j):
            qr = _rope(q_ref[:, j * LANES:(j + 1) * LANES], cc, sc)
            for e in range(LANES // HEAD_DIM):
                yield
                h = j * (LANES // HEAD_DIM) + e
                pn, _ = _attn_probs(qr[:, e * HEAD_DIM:(e + 1) * HEAD_DIM], kcat[h // GQ], sink_ref[0, h], mask)
                yield
                o_ref[:, h * HEAD_DIM:(h + 1) * HEAD_DIM] = jnp.dot(
                    pn.astype(BF16), vcat[h // GQ], preferred_element_type=F32).astype(o_ref.dtype)

        _interleave([head_pair(j) for j in range(QD // LANES)])

    return _call(body, name="attn_fwd", out_shape=_sds(cat.shape, cat.dtype), grid=(nb,),
                 in_specs=_attn_in_specs(nb) + [ANY], out_specs=pl.BlockSpec((BLOCK, QD), lambda n: (n, PD // QD)),
                 aliases={8: 0})(sinks, proj, proj, proj, cos, sin, cos, sin, cat)


def _attn_bwd(proj, cos, sin, sinks, dcat):
    S = proj.shape[0]
    nb = S // BLOCK
    scale = HEAD_DIM ** -0.5
    per = LANES // HEAD_DIM

    def body(sink_ref, q_ref, kvc_ref, kvp_ref, cc_ref, sc_ref, cp_ref, sp_ref, do_ref, o_ref, ds_ref, hold, carry, part, pair):
        n = pl.program_id(0)

        @pl.when(n == 0)
        def _():
            hold[...] = jnp.zeros_like(hold)
            carry[...] = jnp.zeros_like(carry)
            ds_ref[...] = jnp.zeros_like(ds_ref)

        live = jnp.where(n < nb, 1.0, 0.0)
        cc, sc, cp, sp = cc_ref[...], sc_ref[...], cp_ref[...], sp_ref[...]
        kcat, vcat = _attn_keys(kvc_ref, kvp_ref, cc, sc, cp, sp)
        mask = _attn_mask(n)
        o_ref[:, :PD] = jnp.zeros((BLOCK, PD), F32)
        o_ref[:, PD:PD + QD] = hold[...]
        dk = [jnp.zeros((2 * BLOCK, HEAD_DIM), F32) for _ in range(N_KV_HEADS)]
        dv = [jnp.zeros((2 * BLOCK, HEAD_DIM), F32) for _ in range(N_KV_HEADS)]
        row = lax.broadcasted_iota(jnp.int32, (8, LANES), 0)
        acc = {"dsk": jnp.zeros((8, LANES), F32)}

        def head_pair(j):
            qr = _rope(q_ref[:, j * LANES:(j + 1) * LANES], cc, sc)
            for e in range(per):
                yield
                h = j * per + e
                kk = h // GQ
                qh = qr[:, e * HEAD_DIM:(e + 1) * HEAD_DIM]
                pn, psn = _attn_probs(qh, kcat[kk], sink_ref[0, h], mask)
                yield
                doh = (do_ref[:, h * HEAD_DIM:(h + 1) * HEAD_DIM] * live).astype(BF16)
                dp = lax.dot_general(doh, vcat[kk], NT, preferred_element_type=F32)
                yield
                delta = jnp.sum(pn * dp, axis=1, keepdims=True)
                ds = (pn * (dp - delta) * scale).astype(BF16)
                pair[j, :, e * HEAD_DIM:(e + 1) * HEAD_DIM] = jnp.dot(ds, kcat[kk], preferred_element_type=F32)
                yield
                dk[kk] = dk[kk] + lax.dot_general(ds, qh.astype(BF16), TN, preferred_element_type=F32)
                dv[kk] = dv[kk] + lax.dot_general(pn.astype(BF16), doh, TN, preferred_element_type=F32)
                acc["dsk"] = acc["dsk"] + jnp.where(row == h, -jnp.sum(psn * delta), 0.0)
            yield
            hold[:, j * LANES:(j + 1) * LANES] = _unrope(pair[j], cc, sc)

        _interleave([head_pair(j) for j in range(QD // LANES)])
        dsk = acc["dsk"]
        for kk in range(N_KV_HEADS):
            sl = slice(kk * HEAD_DIM, (kk + 1) * HEAD_DIM)
            sv = slice(KD + kk * HEAD_DIM, KD + (kk + 1) * HEAD_DIM)
            part[0, :, sl] = dk[kk][:BLOCK]
            part[0, :, sv] = dv[kk][:BLOCK]
            part[1, :, sl] = dk[kk][BLOCK:]
            part[1, :, sv] = dv[kk][BLOCK:]
        done = carry[...] + part[0]
        o_ref[:, PD + QD:PD + QD + KD] = _unrope(done[:, :KD], cp, sp)
        o_ref[:, PD + QD + KD:] = done[:, KD:]
        carry[...] = part[1]
        ds_ref[...] += dsk

    return _call(body, name="attn_bwd", out_shape=(_sds((S, PD + QD + 2 * KD)), _sds((8, LANES))), grid=(nb + 1,),
                 in_specs=_attn_in_specs(nb) + [pl.BlockSpec((BLOCK, QD), lambda n: (jnp.minimum(n, nb - 1), PD // QD))],
                 out_specs=(pl.BlockSpec((BLOCK, PD + QD + 2 * KD), lambda n: (jnp.maximum(n - 1, 0), 0)),
                            pl.BlockSpec((8, LANES), lambda n: (0, 0))),
                 scratch=[pltpu.VMEM((BLOCK, QD), F32), pltpu.VMEM((BLOCK, 2 * KD), F32),
                          pltpu.VMEM((2, BLOCK, 2 * KD), F32), pltpu.VMEM((QD // LANES, BLOCK, LANES), F32)])(
                     sinks, proj, proj, proj, cos, sin, cos, sin, dcat)


def _pool_sums(u, g, t, shift):
    s2 = u + shift(u, 1, t)
    s4 = s2 + shift(s2, 2, t)
    s8 = s4 + shift(s4, 4, t)
    s16 = s8 + shift(s8, 8, t)
    return jnp.where(g == 0, s2, jnp.where(g == 1, s4, jnp.where(g == 2, s8, s16)))


def _pool_specs(S):
    col = pl.BlockSpec((S, LANES), lambda g: (0, g))
    wsp = pl.BlockSpec((1, LANES, LANES), lambda g: (g, 0, 0))
    vec = pl.BlockSpec((1, LANES), lambda g: (0, g))
    return col, wsp, vec


def _pool_fwd(proj, pool_w, scale):
    S = proj.shape[0]
    col, wsp, vec = _pool_specs(S)

    def body(u_ref, w_ref, s_ref, o_ref):
        g = pl.program_id(0)
        u = u_ref[...]
        t = lax.broadcasted_iota(jnp.int32, u.shape, 0)
        cnt = jnp.minimum(t + 1, 2 << g).astype(F32)
        pm = _pool_sums(u, g, t, _shift_dn) / cnt - u
        o_ref[...] = (jnp.dot(pm.astype(BF16), w_ref[0].astype(BF16), preferred_element_type=F32) * s_ref[...]).astype(BF16)

    return _call(body, name="pool_fwd", out_shape=_sds((S, PD + QD), BF16), grid=(POOL_GROUPS,),
                 in_specs=[col, wsp, vec], out_specs=col)(proj, pool_w, scale)


def _pool_bwd(proj, pool_w, scale, dcat, dproj):
    S = proj.shape[0]
    col, wsp, vec = _pool_specs(S)

    def body(u_ref, w_ref, s_ref, d_ref, dproj_ref, du_ref, dw_ref, dsc_ref):
        g = pl.program_id(0)
        u = u_ref[...]
        t = lax.broadcasted_iota(jnp.int32, u.shape, 0)
        cnt = jnp.minimum(t + 1, 2 << g).astype(F32)
        pm = (_pool_sums(u, g, t, _shift_dn) / cnt - u).astype(BF16)
        wv = w_ref[0].astype(BF16)
        d = d_ref[...]
        pw = jnp.dot(pm, wv, preferred_element_type=F32)
        dsc_ref[...] = jnp.sum(pw * d, axis=0, keepdims=True)
        dpw = (d * s_ref[...]).astype(BF16)
        dw_ref[0] = lax.dot_general(pm, dpw, (((0,), (0,)), ((), ())), preferred_element_type=F32)
        dpm = lax.dot_general(dpw, wv, (((1,), (1,)), ((), ())), preferred_element_type=F32)
        du_ref[...] = _pool_sums(dpm / cnt, g, t, _shift_up) - dpm

    return _call(body, name="pool_bwd",
                 out_shape=(_sds(dproj.shape), _sds((POOL_GROUPS, LANES, LANES)), _sds((1, POOL_GROUPS * LANES))),
                 grid=(POOL_GROUPS,), in_specs=[col, wsp, vec, col, ANY], out_specs=(col, wsp, vec),
                 aliases={4: 0})(proj, pool_w, scale, dcat, dproj)


def _conv(x, w_ref, b_ref, t):
    K = w_ref.shape[0]
    y = b_ref[...] + jnp.zeros_like(x)
    for k in range(K):
        y = y + w_ref[k:k + 1, :] * _shift_dn(x, K - 1 - k, t)
    return y


def _silu_grad(y):
    sg = _sigmoid(y)
    return sg * (1.0 + y * (1.0 - sg))


CONV_ROWS = 256
HALO = 8


def _win_above(ref, r0):
    if isinstance(r0, int):
        assert r0 == 0
        return jnp.concatenate([jnp.zeros((HALO, ref.shape[1]), F32), ref[0:CONV_ROWS, :]], axis=0)
    return ref[pl.ds(pl.multiple_of(r0 - HALO, HALO), CONV_ROWS + HALO), :]


def _rows_at(win, start):
    if start % 8 == 0:
        return win[start:start + CONV_ROWS]
    base = start // 8 * 8
    return pltpu.roll(win, win.shape[0] - (start - base), axis=0)[base:base + CONV_ROWS]


def _taps_above(win, K):
    return [_rows_at(win, HALO - (K - 1 - k)) for k in range(K)]


def _conv_taps(taps, w, b):
    y = b
    for k in range(len(w)):
        y = y + w[k] * taps[k]
    return y


def _conv_t_win(win, w):
    K = len(w)
    out = None
    for k in range(K):
        d = K - 1 - k
        term = w[k] * _rows_at(win, d)
        out = term if out is None else out + term
    return out


def _fold8(x):
    return jnp.sum(x.reshape(CONV_ROWS // 8, 8, x.shape[-1]), axis=0)


def _chunk_loop(S, step, init):
    carry = step(0, init)
    return lax.fori_loop(1, S // CONV_ROWS, lambda i, c: step(pl.multiple_of(i * CONV_ROWS, CONV_ROWS), c), carry)


def _ffn_mid_specs(S, K, layer, nf):
    return [pl.BlockSpec((S, LANES), lambda j: (0, j)), pl.BlockSpec((S, LANES), lambda j: (0, nf + j)),
            pl.BlockSpec((None, K, LANES), lambda j: (layer, 0, j)), pl.BlockSpec((None, K, LANES), lambda j: (layer, 0, nf + j)),
            pl.BlockSpec((None, 1, LANES), lambda j: (layer, 0, j)), pl.BlockSpec((None, 1, LANES), lambda j: (layer, 0, nf + j))]


def _ffn_mid_fwd(a, cw, cb, layer):
    S, F = a.shape[0], a.shape[1] // 2
    nf = F // LANES
    K = cw.shape[1]

    def body(au_ref, ag_ref, wu_ref, wg_ref, bu_ref, bg_ref, o_ref):
        t = lax.broadcasted_iota(jnp.int32, (S, LANES), 0)
        hu = _conv(au_ref[...], wu_ref, bu_ref, t)
        hg = _conv(ag_ref[...], wg_ref, bg_ref, t)
        o_ref[...] = (hg * _sigmoid(hg) * hu).astype(BF16)

    return _call(body, name="ffn_mid_fwd", out_shape=_sds((S, F), BF16), grid=(nf,),
                 in_specs=_ffn_mid_specs(S, K, layer, nf), out_specs=pl.BlockSpec((S, LANES), lambda j: (0, j)))(
                     a, a, cw, cw, cb[:, None], cb[:, None])


def _ffn_mid_bwd(a, cw, cb, layer, dact):
    S, F = a.shape[0], a.shape[1] // 2
    nf = F // LANES
    K = cw.shape[1]

    def body(au_ref, ag_ref, wu_ref, wg_ref, bu_ref, bg_ref, d_ref, dau_ref, dag_ref, dwu_ref, dwg_ref, dbu_ref, dbg_ref,
             dhu_s, dhg_s):
        T = CONV_ROWS
        wu = [wu_ref[k:k + 1, :] for k in range(K)]
        wg = [wg_ref[k:k + 1, :] for k in range(K)]
        bu, bg = bu_ref[...], bg_ref[...]
        zero8 = jnp.zeros((HALO, LANES), F32)
        dhu_s[S:S + HALO, :] = zero8
        dhg_s[S:S + HALO, :] = zero8

        def first_pass(r0, acc):
            tu, tg = _taps_above(_win_above(au_ref, r0), K), _taps_above(_win_above(ag_ref, r0), K)
            hu, hg = _conv_taps(tu, wu, bu), _conv_taps(tg, wg, bg)
            d = d_ref[pl.ds(r0, T), :].astype(F32)
            sg = _sigmoid(hg)
            dhu = d * hg * sg
            dhg = d * hu * (sg * (1.0 + hg * (1.0 - sg)))
            dhu_s[pl.ds(r0, T), :] = dhu
            dhg_s[pl.ds(r0, T), :] = dhg
            new = []
            for dh, taps in ((dhu, tu), (dhg, tg)):
                for k in range(K):
                    new.append(acc[len(new)] + _fold8(dh * taps[k]))
            new.append(acc[2 * K] + _fold8(dhu))
            new.append(acc[2 * K + 1] + _fold8(dhg))
            return tuple(new)

        acc = _chunk_loop(S, first_pass, tuple(zero8 for _ in range(2 * K + 2)))
        for k in range(K):
            dwu_ref[k:k + 1, :] = jnp.sum(acc[k], axis=0, keepdims=True)
            dwg_ref[k:k + 1, :] = jnp.sum(acc[K + k], axis=0, keepdims=True)
        dbu_ref[...] = jnp.sum(acc[2 * K], axis=0, keepdims=True)
        dbg_ref[...] = jnp.sum(acc[2 * K + 1], axis=0, keepdims=True)

        def second_pass(i, carry):
            r0 = pl.multiple_of(i * T, T)
            dau_ref[pl.ds(r0, T), :] = _conv_t_win(dhu_s[pl.ds(r0, T + HALO), :], wu).astype(BF16)
            dag_ref[pl.ds(r0, T), :] = _conv_t_win(dhg_s[pl.ds(r0, T + HALO), :], wg).astype(BF16)
            return carry

        lax.fori_loop(0, S // T, second_pass, 0)

    col = pl.BlockSpec((S, LANES), lambda j: (0, j))
    wsp = pl.BlockSpec((K, LANES), lambda j: (0, j))
    bsp = pl.BlockSpec((1, LANES), lambda j: (0, j))
    dau, dag, dwu, dwg, dbu, dbg = _call(
        body, name="ffn_mid_bwd",
        out_shape=(_sds((S, F), BF16), _sds((S, F), BF16), _sds((K, F)), _sds((K, F)), _sds((1, F)), _sds((1, F))), grid=(nf,),
        in_specs=_ffn_mid_specs(S, K, layer, nf) + [col], out_specs=(col, col, wsp, wsp, bsp, bsp),
        scratch=[pltpu.VMEM((S + HALO, LANES), F32), pltpu.VMEM((S + HALO, LANES), F32)])(
            a, a, cw, cw, cb[:, None], cb[:, None], dact)
    return dau, dag, jnp.concatenate([dwu, dwg], axis=1), jnp.concatenate([dbu, dbg], axis=1)


def _conv_silu_fwd(x, cw, cb):
    S = x.shape[0]
    K, C = cw.shape

    def body(x_ref, w_ref, b_ref, o_ref):
        t = lax.broadcasted_iota(jnp.int32, (S, LANES), 0)
        y = _conv(x_ref[...], w_ref, b_ref, t)
        o_ref[...] = y * _sigmoid(y)

    col = pl.BlockSpec((S, LANES), lambda j: (0, j))
    return _call(body, name="conv_silu_fwd", out_shape=_sds((S, C)), grid=(C // LANES,),
                 in_specs=[col, pl.BlockSpec((K, LANES), lambda j: (0, j)), pl.BlockSpec((1, LANES), lambda j: (0, j))],
                 out_specs=col)(x, cw, cb)


def _conv_silu_bwd(x, cw, cb, douts):
    S = x.shape[0]
    K, C = cw.shape
    starts, off = [], 0
    for d in douts:
        starts.append(off)
        off += d.shape[1] // LANES
    assert off == C // LANES

    def body(x_ref, w_ref, b_ref, *rest):
        dy_s = rest[-1]
        d_refs, (dx_ref, dw_ref, db_ref) = rest[:len(douts)], rest[len(douts):-1]
        j = pl.program_id(0)
        T = CONV_ROWS
        w = [w_ref[k:k + 1, :] for k in range(K)]
        b = b_ref[...]
        zero8 = jnp.zeros((HALO, LANES), F32)
        dy_s[S:S + HALO, :] = zero8

        def first_pass(r0, acc):
            taps = _taps_above(_win_above(x_ref, r0), K)
            y = _conv_taps(taps, w, b)
            d = d_refs[0][pl.ds(r0, T), :]
            for i in range(1, len(douts)):
                d = jnp.where(j >= starts[i], d_refs[i][pl.ds(r0, T), :], d)
            dy = d * _silu_grad(y)
            dy_s[pl.ds(r0, T), :] = dy
            return tuple(acc[k] + _fold8(dy * taps[k]) for k in range(K)) + (acc[K] + _fold8(dy),)

        acc = _chunk_loop(S, first_pass, tuple(zero8 for _ in range(K + 1)))
        for k in range(K):
            dw_ref[k:k + 1, :] = jnp.sum(acc[k], axis=0, keepdims=True)
        db_ref[...] = jnp.sum(acc[K], axis=0, keepdims=True)

        def second_pass(i, carry):
            r0 = pl.multiple_of(i * T, T)
            dx_ref[pl.ds(r0, T), :] = _conv_t_win(dy_s[pl.ds(r0, T + HALO), :], w).astype(BF16)
            return carry

        lax.fori_loop(0, S // T, second_pass, 0)

    col = pl.BlockSpec((S, LANES), lambda j: (0, j))
    wsp = pl.BlockSpec((K, LANES), lambda j: (0, j))
    bsp = pl.BlockSpec((1, LANES), lambda j: (0, j))

    def dspec(i):
        nblk = douts[i].shape[1] // LANES
        return pl.BlockSpec((S, LANES), lambda j: (0, jnp.clip(j - starts[i], 0, nblk - 1)))

    return _call(body, name="conv_silu_bwd", out_shape=(_sds((S, C), BF16), _sds((K, C)), _sds((1, C))), grid=(C // LANES,),
                 in_specs=[col, wsp, bsp] + [dspec(i) for i in range(len(douts))],
                 out_specs=(col, wsp, bsp), scratch=[pltpu.VMEM((S + HALO, LANES), F32)])(x, cw, cb, *douts)


HI = lax.Precision.HIGHEST


def _to_group(x, g, live):
    return jnp.where(live, x if g == 0 else pltpu.roll(x, LANES - SSM_R * g, axis=1), 0.0)


def _from_groups(refs_g):
    out = refs_g[0]
    for g in range(1, SSM_G):
        out = out + pltpu.roll(refs_g[g], SSM_R * g, axis=1)
    return out


def _ssd_prep_fwd(raw, bias_row, alog_row):
    S = raw.shape[0]
    nc = S // SSM_L
    NH = SSM_G * SSM_R

    def body(raw_ref, b_ref, al_ref, pre_ref, dth_ref, dt_ref, acs_ref, acst_ref):
        r_i = lax.broadcasted_iota(jnp.int32, (LANES, LANES), 0)
        c_i = lax.broadcasted_iota(jnp.int32, (LANES, LANES), 1)
        live = c_i < SSM_R
        tril = jnp.where(r_i >= c_i, 1.0, 0.0)
        pre = raw_ref[...] + b_ref[...]
        dt = jnp.where(c_i < NH, jnp.logaddexp(pre, 0.0), 0.0)
        acs = jnp.dot(tril, dt * (-jnp.exp(al_ref[...])), preferred_element_type=F32, precision=HI)
        pre_ref[...] = pre
        dth_ref[...] = dt
        for g in range(SSM_G):
            acs_g = _to_group(acs, g, live)
            dt_ref[g] = _to_group(dt, g, live)
            acs_ref[g] = acs_g
            acst_ref[g] = acs_g.T

    row = pl.BlockSpec((1, LANES), lambda c: (0, 0))
    flat = pl.BlockSpec((SSM_L, LANES), lambda c: (c, 0))
    blk = pl.BlockSpec((SSM_G, SSM_L, LANES), lambda c: (0, c, 0))
    big = _sds((SSM_G, S, LANES))
    return _call(body, name="ssd_prep_fwd",
                 out_shape=(_sds((S, LANES)), _sds((S, LANES)), big, big, _sds((SSM_G, LANES, S))), grid=(nc,),
                 in_specs=[flat, row, row],
                 out_specs=(flat, flat, blk, blk, pl.BlockSpec((SSM_G, LANES, SSM_L), lambda c: (0, 0, c))))(
                     raw, bias_row, alog_row)


def _ssd_prep_bwd(pre_h, dt_h, alog_row, ddt_g, dacs_g, dacst_g):
    S = pre_h.shape[0]
    nc = S // SSM_L
    NH = SSM_G * SSM_R

    def body(pre_ref, dt_ref, al_ref, ddt_ref, dacs_ref, dacst_ref, draw_ref, db_ref, dal_ref):
        c = pl.program_id(0)
        r_i = lax.broadcasted_iota(jnp.int32, (LANES, LANES), 0)
        c_i = lax.broadcasted_iota(jnp.int32, (LANES, LANES), 1)
        triu = jnp.where(r_i <= c_i, 1.0, 0.0)

        @pl.when(c == 0)
        def _():
            db_ref[...] = jnp.zeros_like(db_ref)
            dal_ref[...] = jnp.zeros_like(dal_ref)

        dacs = _from_groups([dacs_ref[g] + dacst_ref[g].T for g in range(SSM_G)])
        da = jnp.dot(triu, dacs, preferred_element_type=F32, precision=HI)
        A = -jnp.exp(al_ref[...])
        ddt = _from_groups([ddt_ref[g] for g in range(SSM_G)]) + da * A
        dpre = jnp.where(c_i < NH, ddt * _sigmoid(pre_ref[...]), 0.0)
        draw_ref[...] = dpre
        db_ref[...] += jnp.sum(dpre, axis=0, keepdims=True)
        dal_ref[...] += jnp.where(c_i[:1] < NH, jnp.sum(da * dt_ref[...], axis=0, keepdims=True) * A, 0.0)

    row = pl.BlockSpec((1, LANES), lambda c: (0, 0))
    flat = pl.BlockSpec((SSM_L, LANES), lambda c: (c, 0))
    blk = pl.BlockSpec((SSM_G, SSM_L, LANES), lambda c: (0, c, 0))
    return _call(body, name="ssd_prep_bwd", out_shape=(_sds((S, LANES)), _sds((1, LANES)), _sds((1, LANES))), grid=(nc,),
                 in_specs=[flat, flat, row, blk, blk, pl.BlockSpec((SSM_G, LANES, SSM_L), lambda c: (0, 0, c))],
                 out_specs=(flat, row, row))(pre_h, dt_h, alog_row, ddt_g, dacs_g, dacst_g)


NT = (((1,), (1,)), ((), ()))
TN = (((0,), (0,)), ((), ()))
SSM_HP = SSM_R * SSM_P


def _ssd_group_terms(xs_ref, dt_ref, acs_ref, d_ref):
    hid = lax.broadcasted_iota(jnp.int32, (1, SSM_HP), 1) // SSM_P
    rid = lax.broadcasted_iota(jnp.int32, (SSM_HP, 1), 0) // SSM_P

    def widen(cols):
        out = cols[0]
        for r in range(1, SSM_R):
            out = jnp.where(hid == r, cols[r], out)
        return out

    dt_c = [dt_ref[:, r:r + 1] for r in range(SSM_R)]
    acs_c = [acs_ref[:, r:r + 1] for r in range(SSM_R)]
    last = [acs_ref[SSM_L - 1:SSM_L, r:r + 1] for r in range(SSM_R)]
    decay_c = [jnp.exp(last[r] - acs_c[r]) for r in range(SSM_R)]
    cd = [jnp.exp(last[r]) for r in range(SSM_R)]
    cd_rows = cd[0]
    for r in range(1, SSM_R):
        cd_rows = jnp.where(rid == r, cd[r], cd_rows)
    xs = xs_ref[...]
    return (xs, xs * widen(dt_c), widen([jnp.exp(a) for a in acs_c]), widen(decay_c),
            widen([d_ref[:, r:r + 1] for r in range(SSM_R)]), cd_rows, dt_c, decay_c, cd)


def _ssd_lmat(acs_ref, acst_ref, r, tril):
    return jnp.exp(jnp.where(tril, acs_ref[:, r:r + 1] - acst_ref[r:r + 1, :], -jnp.inf))


SSM_GPS = 8


def _ssd_specs(rev, nc):
    def cc(c):
        return nc - 1 - c if rev else c
    xs_blocks = (SSM_G * SSM_HP) // (SSM_GPS * SSM_N)
    xs = pl.BlockSpec((SSM_L, SSM_GPS * SSM_HP), lambda g, c: (cc(c), g))
    bsp = pl.BlockSpec((SSM_L, SSM_GPS * SSM_N), lambda g, c: (cc(c), xs_blocks + g))
    csp = pl.BlockSpec((SSM_L, SSM_GPS * SSM_N), lambda g, c: (cc(c), xs_blocks + SSM_G // SSM_GPS + g))
    sc = pl.BlockSpec((SSM_GPS, SSM_L, LANES), lambda g, c: (g, cc(c), 0))
    sct = pl.BlockSpec((SSM_GPS, LANES, SSM_L), lambda g, c: (g, 0, cc(c)))
    gsp = pl.BlockSpec((SSM_GPS, 1, LANES), lambda g, c: (g, 0, 0))
    st = pl.BlockSpec((None, SSM_GPS, SSM_HP, SSM_N), lambda g, c: (cc(c), g, 0, 0))
    return xs, bsp, csp, sc, sct, gsp, st


def _interleave(gens):
    live = list(gens)
    while live:
        for g in list(live):
            try:
                next(g)
            except StopIteration:
                live.remove(g)


def _rounds(gens):
    live = list(gens)
    while live:
        for g in list(live):
            try:
                next(g)
            except StopIteration:
                live.remove(g)
        yield


def _ssd_group_views(gg, xs_ref, b_ref, c_ref, *per_group):
    return (xs_ref.at[:, gg * SSM_HP:(gg + 1) * SSM_HP], b_ref.at[:, gg * SSM_N:(gg + 1) * SSM_N],
            c_ref.at[:, gg * SSM_N:(gg + 1) * SSM_N]) + tuple(r.at[gg] for r in per_group)


def _ssd_fwd(xbc, dt_g, acs_g, acst_g, d_g):
    S = xbc.shape[0]
    nc = S // SSM_L
    xs_s, b_s, c_s, sc, sct, gsp, st = _ssd_specs(False, nc)

    def body(xs_ref, b_ref, c_ref, dt_ref, acs_ref, acst_ref, d_ref, y_ref, st_ref, state):
        c = pl.program_id(1)

        @pl.when(c == 0)
        def _():
            state[...] = jnp.zeros_like(state)

        tril = lax.broadcasted_iota(jnp.int32, (SSM_L, SSM_L), 0) >= lax.broadcasted_iota(jnp.int32, (SSM_L, SSM_L), 1)
        def group(gg):
            xs_v, b_v, c_v, dt_v, acs_v, acst_v, d_v, st_v, state_v = _ssd_group_views(
                gg, xs_ref, b_ref, c_ref, dt_ref, acs_ref, acst_ref, d_ref, st_ref, state)
            y_v = y_ref.at[:, gg * SSM_HP:(gg + 1) * SSM_HP]
            Bb, Cb = b_v[...].astype(BF16), c_v[...].astype(BF16)
            Gm = lax.dot_general(Cb, Bb, NT, preferred_element_type=F32)
            yield
            xs, X, e_all, decay_all, d_all, cd_rows, _, _, _ = _ssd_group_terms(xs_v, dt_v, acs_v, d_v)
            S_all = state_v[...]
            st_v[...] = S_all
            yield
            yo = lax.dot_general(Cb, S_all.astype(BF16), NT, preferred_element_type=F32)
            new_state = lax.dot_general((X * decay_all).astype(BF16), Bb, TN, preferred_element_type=F32)
            yield
            y_v[...] = e_all * yo + d_all * xs
            state_v[...] = S_all * cd_rows + new_state

            def head(r):
                sl = slice(r * SSM_P, (r + 1) * SSM_P)
                M = Gm * _ssd_lmat(acs_v, acst_v, r, tril)
                yield
                y_v[:, sl] += jnp.dot(M.astype(BF16), X[:, sl].astype(BF16), preferred_element_type=F32)

            yield from _rounds([head(r) for r in range(SSM_R)])

        _interleave([group(gg) for gg in range(SSM_GPS)])

    return _call(body, name="ssd_fwd",
                 out_shape=(_sds((S, SSM_G * SSM_HP)), _sds((nc, SSM_G, SSM_HP, SSM_N))),
                 grid=(SSM_G // SSM_GPS, nc), in_specs=[xs_s, b_s, c_s, sc, sc, sct, gsp],
                 out_specs=(xs_s, pl.BlockSpec((None, SSM_GPS, SSM_HP, SSM_N), lambda g, c: (c, g, 0, 0))),
                 scratch=[pltpu.VMEM((SSM_GPS, SSM_HP, SSM_N), F32)])(xbc, xbc, xbc, dt_g, acs_g, acst_g, d_g)


def _ssd_bwd(xbc, dt_g, acs_g, acst_g, d_g, states, dy):
    S = xbc.shape[0]
    nc = S // SSM_L
    xs_s, b_s, c_s, sc, sct, gsp, st = _ssd_specs(True, nc)
    bc_out = pl.BlockSpec((SSM_L, SSM_GPS * SSM_N), lambda g, c: (nc - 1 - c, g))

    def body(xs_ref, b_ref, c_ref, dt_ref, acs_ref, acst_ref, d_ref, st_ref, dy_ref,
             dxs_ref, db_ref, dc_ref, ddt_ref, dacs_ref, dacst_ref, dd_ref, dstate):
        c = pl.program_id(1)

        @pl.when(c == 0)
        def _():
            dstate[...] = jnp.zeros_like(dstate)
            dd_ref[...] = jnp.zeros_like(dd_ref)

        tril = lax.broadcasted_iota(jnp.int32, (SSM_L, SSM_L), 0) >= lax.broadcasted_iota(jnp.int32, (SSM_L, SSM_L), 1)
        lane = lax.broadcasted_iota(jnp.int32, (1, LANES), 1)
        subl = lax.broadcasted_iota(jnp.int32, (LANES, 1), 0)
        last_row = lax.broadcasted_iota(jnp.int32, (SSM_L, 1), 0) == SSM_L - 1
        triu = lax.broadcasted_iota(jnp.int32, (SSM_L, SSM_L), 0) <= lax.broadcasted_iota(jnp.int32, (SSM_L, SSM_L), 1)
        def group(gg):
            xs_v, b_v, c_v, dt_v, acs_v, acst_v, d_v, st_v, ddt_v, dacs_v, dacst_v, dd_v, dstate_v = _ssd_group_views(
                gg, xs_ref, b_ref, c_ref, dt_ref, acs_ref, acst_ref, d_ref, st_ref, ddt_ref, dacs_ref, dacst_ref, dd_ref, dstate)
            dy_v, dxs_v = (r.at[:, gg * SSM_HP:(gg + 1) * SSM_HP] for r in (dy_ref, dxs_ref))
            db_v, dc_v = (r.at[:, gg * SSM_N:(gg + 1) * SSM_N] for r in (db_ref, dc_ref))
            Bb, Cb = b_v[...].astype(BF16), c_v[...].astype(BF16)
            Gm = lax.dot_general(Cb, Bb, NT, preferred_element_type=F32)
            GmT = lax.dot_general(Bb, Cb, NT, preferred_element_type=F32)
            yield
            xs, X, e_all, decay_all, d_all, cd_rows, dt_c, decay_c, cd = _ssd_group_terms(xs_v, dt_v, acs_v, d_v)
            S_all, dSn_all, dY = st_v[...], dstate_v[...], dy_v[...]
            Sb, dSnb = S_all.astype(BF16), dSn_all.astype(BF16)
            yield
            T = lax.dot_general(Cb, Sb, NT, preferred_element_type=F32)
            dT = (dY * e_all).astype(BF16)
            dC = jnp.dot(dT, Sb, preferred_element_type=F32)
            dS_prev = lax.dot_general(dT, Cb, TN, preferred_element_type=F32)
            yield
            yo_dy = dY * (e_all * T)
            W = lax.dot_general(Bb, dSnb, NT, preferred_element_type=F32)
            dB = jnp.dot((X * decay_all).astype(BF16), dSnb, preferred_element_type=F32)
            yield
            xw = X * W
            dcd_rows = jnp.sum(dSn_all * S_all, axis=1, keepdims=True)
            dstate_v[...] = dS_prev + dSn_all * cd_rows
            dX_state = W * decay_all
            yield
            acc = dict(dG=jnp.zeros((SSM_L, SSM_L), F32), dGT=jnp.zeros((SSM_L, SSM_L), F32),
                       ddt=jnp.zeros((SSM_L, LANES), F32), dacs=jnp.zeros((SSM_L, LANES), F32),
                       dacst=jnp.zeros((LANES, SSM_L), F32), dd=jnp.zeros((1, LANES), F32))

            def head(r):
                sl = slice(r * SSM_P, (r + 1) * SSM_P)
                Lm = _ssd_lmat(acs_v, acst_v, r, tril)
                LmT = jnp.exp(jnp.where(triu, acst_v[r:r + 1, :] - acs_v[:, r:r + 1], -jnp.inf))
                M = Gm * Lm
                yield
                dYh, xs_h = dY[:, sl], xs[:, sl]
                dYb, Xb = dYh.astype(BF16), X[:, sl].astype(BF16)
                dM = lax.dot_general(dYb, Xb, NT, preferred_element_type=F32)
                yield
                dX = jnp.dot((GmT * LmT).astype(BF16), dYb, preferred_element_type=F32) + dX_state[:, sl]
                acc["dG"] = acc["dG"] + dM * Lm
                acc["dGT"] = acc["dGT"] + lax.dot_general(Xb, dYb, NT, preferred_element_type=F32) * LmT
                yield
                dseg = dM * M
                dd = jnp.sum(xw[:, sl], axis=1, keepdims=True) * decay_c[r]
                dcd = jnp.sum(dcd_rows[sl])
                dacs_col = (jnp.sum(dseg, axis=1, keepdims=True) + jnp.sum(yo_dy[:, sl], axis=1, keepdims=True) - dd
                            + jnp.where(last_row, dcd * cd[r] + jnp.sum(dd), 0.0))
                dacs_row = -jnp.sum(dseg, axis=0, keepdims=True)
                yield
                dxs_v[:, sl] = dX * dt_c[r] + d_all[:, sl] * dYh
                acc["ddt"] = acc["ddt"] + jnp.where(lane == r, jnp.sum(dX * xs_h, axis=1, keepdims=True), 0.0)
                acc["dacs"] = acc["dacs"] + jnp.where(lane == r, dacs_col, 0.0)
                acc["dacst"] = acc["dacst"] + jnp.where(subl == r, dacs_row, 0.0)
                acc["dd"] = acc["dd"] + jnp.where(lane == r, jnp.sum(dYh * xs_h), 0.0)

            yield from _rounds([head(r) for r in range(SSM_R)])
            dc_v[...] = dC + jnp.dot(acc["dG"].astype(BF16), Bb, preferred_element_type=F32)
            db_v[...] = dB + jnp.dot(acc["dGT"].astype(BF16), Cb, preferred_element_type=F32)
            ddt_v[...] = acc["ddt"]
            dacs_v[...] = acc["dacs"]
            dacst_v[...] = acc["dacst"]
            dd_v[...] += acc["dd"]

        _interleave([group(gg) for gg in range(SSM_GPS)])

    big = _sds((SSM_G, S, LANES))
    return _call(body, name="ssd_bwd",
                 out_shape=(_sds((S, SSM_G * SSM_HP)), _sds((S, SSM_G * SSM_N)), _sds((S, SSM_G * SSM_N)),
                            big, big, _sds((SSM_G, LANES, S)), _sds((SSM_G, 1, LANES))),
                 grid=(SSM_G // SSM_GPS, nc), in_specs=[xs_s, b_s, c_s, sc, sc, sct, gsp, st, xs_s],
                 out_specs=(xs_s, bc_out, bc_out, sc, sc, sct, gsp),
                 scratch=[pltpu.VMEM((SSM_GPS, SSM_HP, SSM_N), F32)])(xbc, xbc, xbc, dt_g, acs_g, acst_g, d_g, states, dy)


def _gate_norm_fwd(y, proj, w):
    S, DI = y.shape
    tm = _tile(S, 256)

    def body(y_ref, z_ref, w_ref, o_ref):
        z = z_ref[...]
        gn = y_ref[...] * (z * _sigmoid(z))
        r = lax.rsqrt(jnp.mean(gn * gn, axis=-1, keepdims=True) + SSM_NORM_EPS)
        o_ref[...] = (gn * r * w_ref[...]).astype(BF16)

    row = pl.BlockSpec((tm, DI), lambda i: (i, 0))
    return _call(body, name="gate_norm_fwd", out_shape=_sds((S, DI), BF16), grid=(S // tm,),
                 in_specs=[row, row, pl.BlockSpec((1, DI), lambda i: (0, 0))], out_specs=row)(y, proj, w)


def _gate_norm_bwd(y, proj, w, dout):
    S, DI = y.shape
    tm = _tile(S, 256)

    def body(y_ref, z_ref, w_ref, d_ref, dy_ref, dz_ref, dw_ref):
        z, yv = z_ref[...], y_ref[...]
        sz = z * _sigmoid(z)
        dgn, dw = _norm_bwd_math(yv * sz, w_ref[...], d_ref[...].astype(F32), SSM_NORM_EPS)
        dy_ref[...] = dgn * sz
        dz_ref[...] = (dgn * yv * _silu_grad(z)).astype(BF16)

        @pl.when(pl.program_id(0) == 0)
        def _():
            dw_ref[...] = jnp.zeros_like(dw_ref)

        dw_ref[...] += dw

    row = pl.BlockSpec((tm, DI), lambda i: (i, 0))
    vec = pl.BlockSpec((1, DI), lambda i: (0, 0))
    return _call(body, name="gate_norm_bwd", out_shape=(_sds((S, DI)), _sds((S, DI), BF16), _sds((1, DI))), grid=(S // tm,),
                 in_specs=[row, row, vec, row], out_specs=(row, row, vec))(y, proj, w, dout)


def _group_major(v):
    return jnp.pad(v.reshape(SSM_G, 1, SSM_R), ((0, 0), (0, 0), (0, LANES - SSM_R)))


def _ungroup(t):
    return t[:, :SSM_R].reshape(1, SSM_G * SSM_R)


def _ffn_fwd(x, P, l, need):
    h = _rmsnorm(x, P["norm_ffn"][l:l + 1], name=f"ffn{l}_norm")
    need(f"ffn{l}_up", h)
    a = _mm(h, P[f"ffn_w_upT{l}"], tb=True, name=f"ffn{l}_up")
    need(f"ffn{l}_down", a)
    act = _ffn_mid_fwd(a, P["ffn_conv_w"], P["ffn_conv_b"], l)
    out = _mm(act, P[f"ffn_w_down{l}"], res=x, name=f"ffn{l}_down")
    return out, (x, h, a, act)


def _ffn_bwd(saved, P, l, dx, emit):
    x, h, a, act = saved
    wT = P[f"ffn_w_upT{l}"]
    F = wT.shape[0] // 2
    dact = _mm(dx, P[f"ffn_w_down{l}"], tb=True, out_dtype=BF16, name=f"ffn{l}_down_dx")
    dw_down = _mm(act, dx, ta=True, out_dtype=PAYLOAD, name=f"ffn{l}_down_dw")
    dau, dag, dcw, dcb = _ffn_mid_bwd(a, P["ffn_conv_w"], P["ffn_conv_b"], l, dact)
    dw_upT = _mm(dau, h, ta=True, out_dtype=PAYLOAD, out_rows=(2 * F, 0, None), name=f"ffn{l}_up_u_dw")
    dw_upT = _mm(dag, h, ta=True, out_dtype=PAYLOAD, out_rows=(2 * F, F, dw_upT), name=f"ffn{l}_up_g_dw")
    tie = emit(f"ffn{l}", {"ffn_w_upT": dw_upT, "ffn_w_down": dw_down})
    dh = _mm(dau, wT, b_rows=(0, F), name=f"ffn{l}_up_u_dx")
    dx_in, dnw = _mm_norm_bwd(dag, wT, x, P["norm_ffn"][l:l + 1], dx, res=dh, b_rows=(F, F), after=tie,
                              name=f"ffn{l}_up_g_dx_norm_bwd")
    return dx_in, dnw, dcw, dcb


def _local_step(x, positions, target, P, need, emit, after=None):
    S, D = x.shape
    inv_freq = ROPE_THETA ** (-jnp.arange(0, HEAD_DIM, 2, dtype=F32) / HEAD_DIM)
    inv_freq = jnp.tile(inv_freq, LANES // (HEAD_DIM // 2)).reshape(1, LANES)
    cos, sin = _rope_tables(positions, inv_freq)

    nm0 = P["norm_mix"][0:1]
    h0 = _rmsnorm(x, nm0, name="mix_norm", after=after)
    need("mix_in", h0)
    proj0 = _mm(h0, P["mix_w_inT"], tb=True, name="mix_in")
    cat0 = _attn_fwd(proj0, cos, sin, P["attn_sinks"], _pool_fwd(proj0, P["pool_w"][0], P["pool_scale"]))
    need("mix_out", cat0)
    x1 = _mm(cat0, P["mix_w_out"], res=x, name="mix_out")
    x2, ffn0 = _ffn_fwd(x1, P, 0, need)

    nm1 = P["norm_mix"][1:2]
    h1 = _rmsnorm(x2, nm1, name="ssm_norm_in")
    need("ssm", h1)
    w1T, wdtT = P["ssm_w_inT"], P["ssm_wdtT"]
    DI, CD, NH = P["ssm_norm"].shape[1], P["ssm_conv_w"].shape[1], P["ssm_dt_bias"].shape[1]
    z = _mm(h1, w1T, tb=True, b_rows=(0, DI), name="ssm_in_z")
    xbcp = _mm(h1, w1T, tb=True, b_rows=(DI, CD), name="ssm_in_xbc")
    dtraw = _mm(h1, wdtT, tb=True, name="ssm_in_dt")
    xbc = _conv_silu_fwd(xbcp, P["ssm_conv_w"], P["ssm_conv_b"])
    bias_row = jnp.pad(P["ssm_dt_bias"], ((0, 0), (0, LANES - NH)))
    alog_row = jnp.pad(P["ssm_A_log"], ((0, 0), (0, LANES - NH)))
    d_g = _group_major(P["ssm_D"])
    pre_h, dt_h, dt_g, acs_g, acst_g = _ssd_prep_fwd(dtraw, bias_row, alog_row)
    y, states = _ssd_fwd(xbc, dt_g, acs_g, acst_g, d_g)
    yn = _gate_norm_fwd(y, z, P["ssm_norm"])
    need("ssm_out", yn)
    x3 = _mm(yn, P["ssm_w_out"], res=x2, name="ssm_out")
    x4, ffn1 = _ffn_fwd(x3, P, 1, need)

    loss, dx, d_norm_final = _final_loss(x4, P["norm_final"].reshape(1, D), target, name="final_loss")
    dx, dnf1, dcw1, dcb1 = _ffn_bwd(ffn1, P, 1, dx, emit)
    dyn = _mm(dx, P["ssm_w_out"], tb=True, out_dtype=BF16, name="ssm_out_dx")
    d_w_out1 = _mm(yn, dx, ta=True, out_dtype=PAYLOAD, name="ssm_out_dw")
    dy, dz, d_ssm_norm = _gate_norm_bwd(y, z, P["ssm_norm"], dyn)
    dxs, dB, dC, ddt_g, dacs_g, dacst_g, dd_g = _ssd_bwd(xbc, dt_g, acs_g, acst_g, d_g, states, dy)
    draw, dbias_row, dalog_row = _ssd_prep_bwd(pre_h, dt_h, alog_row, ddt_g, dacs_g, dacst_g)
    dxbc, d_conv_w1, d_conv_b1 = _conv_silu_bwd(xbcp, P["ssm_conv_w"], P["ssm_conv_b"], [dxs, dB, dC])
    rows = DI + CD + NH
    d_w1T = _mm(dz, h1, ta=True, out_dtype=PAYLOAD, out_rows=(rows, 0, None), name="ssm_in_z_dw")
    d_w1T = _mm(dxbc, h1, ta=True, out_dtype=PAYLOAD, out_rows=(rows, DI, d_w1T), name="ssm_in_xbc_dw")
    d_w1T = _mm(draw[:, :NH], h1, ta=True, out_dtype=PAYLOAD, out_rows=(rows, DI + CD, d_w1T), name="ssm_in_dt_dw")
    tie = emit("ssm", {"ssm_w_inT": d_w1T, "ssm_w_out": d_w_out1,
                       "ssm_conv_w": d_conv_w1, "ssm_conv_b": d_conv_b1, "ssm_norm": d_ssm_norm})
    dh1 = _mm(dz, w1T, b_rows=(0, DI), name="ssm_in_z_dx")
    dh1 = _mm(dxbc, w1T, b_rows=(DI, CD), res=dh1, name="ssm_in_xbc_dx")
    dx, dnm1 = _mm_norm_bwd(draw, wdtT, x2, nm1, dx, res=dh1, after=tie, name="ssm_in_dt_dx_norm_bwd")
    dx, dnf0, dcw0, dcb0 = _ffn_bwd(ffn0, P, 0, dx, emit)
    dcat = _mm(dx, P["mix_w_out"], tb=True, name="mix_out_dx")
    d_w_out0 = _mm(cat0, dx, ta=True, out_dtype=PAYLOAD, name="mix_out_dw")
    dproj0, dsk = _attn_bwd(proj0, cos, sin, P["attn_sinks"], dcat)
    dproj0, d_pool_w, d_pool_scale = _pool_bwd(proj0, P["pool_w"][0], P["pool_scale"], dcat, dproj0)
    d_w_in0 = _mm(dproj0, h0, ta=True, out_dtype=PAYLOAD, name="mix_in_dw")
    tie = emit("mix", {"mix_w_inT": d_w_in0, "mix_w_out": d_w_out0, "ffn_conv_w": jnp.stack([dcw0, dcw1])})
    grad_x, dnm0 = _mm_norm_bwd(dproj0, P["mix_w_inT"], x, nm0, dx, after=tie, name="mix_in_dx_norm_bwd")

    small = {
        "norm_mix": jnp.concatenate([dnm0, dnm1], axis=0),
        "norm_ffn": jnp.concatenate([dnf0, dnf1], axis=0),
        "norm_final": d_norm_final,
        "pool_w": d_pool_w,
        "pool_scale": d_pool_scale,
        "attn_sinks_rows": dsk,
        "ssm_dt_bias_row": dbias_row, "ssm_A_log_row": dalog_row, "ssm_D_g": dd_g,
        "ffn_conv_b": jnp.concatenate([dcb0, dcb1], axis=0),
    }
    return loss, grad_x, small


def _peer(k):
    x, y, c = lax.axis_index("x"), lax.axis_index("y"), lax.axis_index("c")
    px = 1 - x if k & 4 else x
    py = 1 - y if k & 2 else y
    pc = 1 - c if k & 1 else c
    return (px, py, pc), 4 * px + 2 * py + pc


def _my_index():
    return 4 * lax.axis_index("x") + 2 * lax.axis_index("y") + lax.axis_index("c")


def _land_sds(a, mode, gather):
    if mode == "slab":
        return _sds(((N_DEV,) + a.shape) if gather else a.shape, a.dtype)
    assert mode == "rows", mode
    return _sds((N_DEV * a.shape[0],) + a.shape[1:] if gather else (N_DEV, a.shape[0] // N_DEV) + a.shape[1:], a.dtype)


def _part(ref, mode, shape, idx):
    if mode == "slab":
        return ref.at[idx]
    r = shape[0] // N_DEV
    return ref.at[pl.ds(idx * r, r)]


def _remote_copies(ops, gather, srcs, lands, send_sems, recv_sems):
    me = _my_index()
    n = len(ops)
    out = []
    for k in range(1, N_DEV):
        dev, idx = _peer(k)
        for i, (a, mode) in enumerate(ops):
            s = srcs[i] if gather else _part(srcs[i], mode, a.shape, idx)
            d = _part(lands[i], mode, _land_sds(a, mode, gather).shape, me) if gather else lands[i].at[me]
            out.append(pltpu.make_async_remote_copy(src_ref=s, dst_ref=d, send_sem=send_sems.at[(k - 1) * n + i],
                                                    recv_sem=recv_sems.at[(k - 1) * n + i], device_id=dev,
                                                    device_id_type=pl.DeviceIdType.MESH))
    return out


HBM = pl.BlockSpec(memory_space=pltpu.HBM)
SEM = pl.BlockSpec(memory_space=pltpu.SEMAPHORE)
SIDE_EFFECT = pltpu.SideEffectType.DATAFLOW_SIDE_EFFECTING


def _in_hbm(a):
    return pltpu.with_memory_space_constraint(a, pltpu.HBM)


def _place_own(ops, *, gather, name):
    n = len(ops)

    def zeros(k):
        return (0,) * k

    in_specs, out_specs = [], []
    for a, mode in ops:
        nd = a.ndim
        if gather and mode == "slab":
            in_specs.append(pl.BlockSpec(a.shape, lambda i, nd=nd: zeros(nd)))
            out_specs.append(pl.BlockSpec((1,) + a.shape, lambda i, nd=nd: (_my_index(),) + zeros(nd)))
        elif gather:
            in_specs.append(pl.BlockSpec(a.shape, lambda i, nd=nd: zeros(nd)))
            out_specs.append(pl.BlockSpec(a.shape, lambda i, nd=nd: (_my_index(),) + zeros(nd - 1)))
        elif mode == "slab":
            in_specs.append(pl.BlockSpec((1,) + a.shape[1:], lambda i, nd=nd: (_my_index(),) + zeros(nd - 1)))
            out_specs.append(pl.BlockSpec((1,) + a.shape[1:], lambda i, nd=nd: (_my_index(),) + zeros(nd - 1)))
        else:
            r = a.shape[0] // N_DEV
            in_specs.append(pl.BlockSpec((r,) + a.shape[1:], lambda i, nd=nd: (_my_index(),) + zeros(nd - 1)))
            out_specs.append(pl.BlockSpec((1, r) + a.shape[1:], lambda i, nd=nd: (_my_index(),) + zeros(nd)))

    def body(*refs):
        for i_ref, o_ref in zip(refs[:n], refs[n:2 * n]):
            if o_ref.shape == i_ref.shape:
                o_ref[...] = i_ref[...]
            else:
                o_ref[0] = i_ref[...]

    outs = _call(body, name=name, grid=(1,), in_specs=in_specs, out_specs=out_specs + [ANY] * n,
                 out_shape=[_land_sds(a, m, gather) for a, m in ops] + [_sds(a.shape, a.dtype) for a, _ in ops],
                 aliases={i: n + i for i in range(n)})(*[a for a, _ in ops])
    return outs[:n], [(src, m) for src, (_, m) in zip(outs[n:], ops)]


def _exchange_start(groups, *, gather, name):
    sizes = [len(ops) for ops, _ in groups]
    n = sum(sizes)
    G = len(groups)

    def body(*refs):
        srcs, lands = refs[:n], refs[n:2 * n]
        sems = refs[2 * n:2 * n + 2 * G]
        token = refs[-1]
        off = 0
        for g, (ops, _) in enumerate(groups):
            for cp in _remote_copies(ops, gather, srcs[off:off + sizes[g]], lands[off:off + sizes[g]], sems[2 * g], sems[2 * g + 1]):
                cp.start()
            off += sizes[g]
        token[...] = jnp.zeros_like(token)

    srcs = [a for ops, _ in groups for a, _ in ops]
    lands = [l for _, ls in groups for l in ls]
    sem_shapes = [pltpu.SemaphoreType.DMA((s * (N_DEV - 1),)) for s in sizes for _ in range(2)]
    outs = pl.pallas_call(
        body, name=name,
        out_shape=sem_shapes + [pltpu.HBM(a.shape, a.dtype) for a in srcs + lands] + [_sds((8, LANES))],
        in_specs=[HBM] * (2 * n), out_specs=[SEM] * (2 * G) + [HBM] * (2 * n) + [pl.BlockSpec(memory_space=pltpu.VMEM)],
        input_output_aliases={i: 2 * G + i for i in range(2 * n)},
        compiler_params=pltpu.CompilerParams(has_side_effects=SIDE_EFFECT))(*[_in_hbm(a) for a in srcs + lands])
    sems, thru, token = outs[:2 * G], outs[2 * G:2 * G + 2 * n], outs[-1]
    states, off = [], 0
    for g, s in enumerate(sizes):
        states.append((sems[2 * g], sems[2 * g + 1], thru[off:off + s], thru[n + off:n + off + s]))
        off += s
    return states, token


def _exchange_wait(ops, state, after, *, gather, name):
    send_sems, recv_sems, srcs, lands = state
    n = len(ops)

    def body(*refs):
        for cp in _remote_copies(ops, gather, refs[:n], refs[n:2 * n], refs[2 * n], refs[2 * n + 1]):
            cp.wait_send()
            cp.wait_recv()

    outs = pl.pallas_call(
        body, name=name, out_shape=[pltpu.HBM(a.shape, a.dtype) for a in list(srcs) + list(lands)],
        in_specs=[HBM] * (2 * n) + [SEM, SEM, ANY], out_specs=[HBM] * (2 * n),
        input_output_aliases={i: i for i in range(2 * n)},
        compiler_params=pltpu.CompilerParams(has_side_effects=SIDE_EFFECT))(*srcs, *lands, send_sems, recv_sems, after)
    return outs[n:]


ADAM_ROWS = 256


def _row_tile(R, cap=ADAM_ROWS):
    best = R
    if R > cap:
        for d in range(16, cap + 1, 16):
            if R % d == 0:
                best = d
    return best


def _adamw(g_layers, w, m, v, *, name):
    L = len(g_layers)
    J, R, Wd = g_layers[0].shape
    assert w.shape == (L, R, Wd), (g_layers[0].shape, w.shape)
    tr = _row_tile(R)
    nrt = R // tr
    c1 = 1.0 / (1.0 - ADAM_B1 ** ADAM_STEP)
    c2 = 1.0 / (1.0 - ADAM_B2 ** ADAM_STEP)

    def body(*refs):
        g_refs = refs[:L]
        w_ref, m_ref, v_ref, go_ref, d_ref, mo_ref, vo_ref = refs[L:]
        layer = pl.program_id(0)
        g = None
        for l, g_ref in enumerate(g_refs):
            gl = g_ref[0].astype(F32)
            for j in range(1, J):
                gl = gl + g_ref[j].astype(F32)
            g = gl if g is None else jnp.where(layer == l, gl, g)
        mn = ADAM_B1 * m_ref[...] + (1.0 - ADAM_B1) * g
        vn = ADAM_B2 * v_ref[...] + (1.0 - ADAM_B2) * (g * g)
        go_ref[...] = g
        mo_ref[...] = mn
        vo_ref[...] = vn
        d_ref[...] = -ADAM_LR * ((mn * c1) / (jnp.sqrt(vn * c2) + ADAM_EPS) + ADAM_WD * w_ref[...])

    def g_spec(l):
        return pl.BlockSpec((J, tr, Wd), lambda ll, i: (0, jnp.where(ll == l, i, jnp.where(ll < l, 0, nrt - 1)), 0))

    row = pl.BlockSpec((None, tr, Wd), lambda ll, i: (ll, i, 0))
    out = _sds((L, R, Wd))
    return _call(body, name=name, out_shape=(out, out, out, out), grid=(L, nrt),
                 in_specs=[g_spec(l) for l in range(L)] + [row, row, row], out_specs=(row, row, row, row))(*g_layers, w, m, v)


def _sum_slabs(slabs, *, name):
    n = len(slabs)

    def body(*refs):
        for g_ref, o_ref in zip(refs[:n], refs[n:]):
            g = g_ref[0]
            for j in range(1, g_ref.shape[0]):
                g = g + g_ref[j]
            o_ref[...] = g

    return _call(body, name=name, out_shape=[_sds(s.shape[1:]) for s in slabs])(*slabs)


def kernel(x, positions, norm_mix, norm_ffn, norm_final, mix_w_in, pool_w, pool_scale, attn_sinks, mix_w_out, ssm_w_in, ssm_conv_w, ssm_conv_b, ssm_dt_bias, ssm_A_log, ssm_D, ssm_norm, ssm_w_out, ffn_w_up, ffn_conv_w, ffn_conv_b, ffn_w_down, loss_target, m_norm_mix, m_norm_ffn, m_norm_final, m_mix_w_in, m_pool_w, m_pool_scale, m_attn_sinks, m_mix_w_out, m_ssm_w_in, m_ssm_conv_w, m_ssm_conv_b, m_ssm_dt_bias, m_ssm_A_log, m_ssm_D, m_ssm_norm, m_ssm_w_out, m_ffn_w_up, m_ffn_conv_w, m_ffn_conv_b, m_ffn_w_down, v_norm_mix, v_norm_ffn, v_norm_final, v_mix_w_in, v_pool_w, v_pool_scale, v_attn_sinks, v_mix_w_out, v_ssm_w_in, v_ssm_conv_w, v_ssm_conv_b, v_ssm_dt_bias, v_ssm_A_log, v_ssm_D, v_ssm_norm, v_ssm_w_out, v_ffn_w_up, v_ffn_conv_w, v_ffn_conv_b, v_ffn_w_down):
    args = dict(locals())
    wl = {n: args[n] for n in WEIGHTS}
    ml = {n: args["m_" + n] for n in WEIGHTS}
    vl = {n: args["v_" + n] for n in WEIGHTS}
    F = ffn_w_down.shape[1] * N_DEV
    DI, CD, NH = ssm_norm.shape[1] * N_DEV, ssm_conv_b.shape[1] * N_DEV, ssm_dt_bias.shape[1]
    Kc, Kf = ssm_conv_w.shape[1], ffn_conv_w.shape[1]
    n_up = ffn_w_up.shape[2]
    col_sharded = ("mix_w_in", "ssm_w_in", "ffn_w_up")

    def tr(a):
        return jnp.swapaxes(a, -1, -2)

    def two(a):
        return a.reshape(-1, a.shape[-1])

    def pay(a):
        return a.astype(PAYLOAD)

    order = ("mix_in", "mix_out", "ffn0_up", "ffn0_down", "ssm", "ssm_out", "ffn1_up", "ffn1_down")
    gops = {
        "mix_in": [(pay(tr(mix_w_in)[0]), "rows")],
        "mix_out": [(pay(mix_w_out[0]), "rows"), (two(ssm_conv_w), "slab"), (ssm_conv_b, "slab"), (ssm_norm, "slab"),
                    (two(ffn_conv_w), "slab")],
        "ffn0_up": [(pay(tr(ffn_w_up)[0]), "rows")], "ffn0_down": [(pay(ffn_w_down[0]), "rows")],
        "ssm": [(pay(tr(ssm_w_in)[0]), "slab")], "ssm_out": [(pay(ssm_w_out[0]), "rows")],
        "ffn1_up": [(pay(tr(ffn_w_up)[1]), "rows")], "ffn1_down": [(pay(ffn_w_down[1]), "rows")],
    }
    lands, handed = _place_own([op for g in order for op in gops[g]], gather=True, name="gather_own")
    groups, off = [], 0
    for g in order:
        gops[g] = handed[off:off + len(gops[g])]
        groups.append((gops[g], lands[off:off + len(gops[g])]))
        off += len(gops[g])
    gstates, token = _exchange_start(groups, gather=True, name="gather_start")
    gstate = dict(zip(order, gstates))
    P = {n: wl[n] for n in REPLICATED}

    def need(g, after):
        got = _exchange_wait(gops[g], gstate[g], after, gather=True, name="gather_wait_" + g)
        if g == "mix_in":
            P["mix_w_inT"] = got[0]
        elif g == "mix_out":
            P.update(mix_w_out=got[0], ssm_conv_w=got[1].transpose(1, 0, 2).reshape(Kc, CD), ssm_conv_b=got[2].reshape(1, CD),
                     ssm_norm=got[3].reshape(1, DI), ffn_conv_w=got[4].transpose(1, 0, 2).reshape(2, Kf, 2 * F))
        elif g == "ssm":
            w1T = got[0].reshape(-1, got[0].shape[-1])
            P.update(ssm_w_inT=w1T, ssm_wdtT=jnp.pad(w1T[DI + CD:], ((0, LANES - NH), (0, 0))))
        elif g == "ssm_out":
            P["ssm_w_out"] = got[0]
        elif g.endswith("_up"):
            P["ffn_w_upT" + g[3]] = got[0]
        else:
            P["ffn_w_down" + g[3]] = got[0]

    sent = {}

    def emit(g, d):
        if g == "mix":
            ops = [(d["mix_w_inT"], "rows"), (d["mix_w_out"], "rows"),
                   (d["ffn_conv_w"].reshape(2 * Kf, N_DEV, n_up).transpose(1, 0, 2), "slab")]
        elif g == "ssm":
            ops = [(d["ssm_w_inT"].reshape(N_DEV, -1, d["ssm_w_inT"].shape[-1]), "slab"), (d["ssm_w_out"], "rows"),
                   (d["ssm_conv_w"].reshape(Kc, N_DEV, -1).transpose(1, 0, 2), "slab"),
                   (d["ssm_conv_b"].reshape(N_DEV, 1, -1), "slab"), (d["ssm_norm"].reshape(N_DEV, 1, -1), "slab")]
        else:
            ops = [(d["ffn_w_upT"], "rows"), (d["ffn_w_down"], "rows")]
        own, ops = _place_own(ops, gather=False, name="scatter_own_" + g)
        (state,), tok = _exchange_start([(ops, own)], gather=False, name="scatter_start_" + g)
        sent[g] = (ops, state)
        return tok

    loss_lanes, grad_x, G = _local_step(x[0], positions.reshape(-1, 1), loss_target[0], P, need, emit, after=token)

    res = {}

    def update(n, g_layers):
        L = len(g_layers)
        g_layers = [g.reshape(g.shape[0], -1, g.shape[-1]) for g in g_layers]
        shape = (L,) + g_layers[0].shape[1:]
        view = tr if n in col_sharded else (lambda a: a)
        outs = _adamw(g_layers, view(wl[n]).reshape(shape), view(ml[n]).reshape(shape), view(vl[n]).reshape(shape),
                      name="adamw_" + n)
        for kind, a in zip(("grad", "delta", "new_m", "new_v"), outs):
            res[kind, n] = view(a.reshape(view(wl[n]).shape))

    rep_ops = [(a, "slab") for a in (G["norm_mix"], G["norm_ffn"], G["norm_final"], G["pool_w"].reshape(-1, LANES),
                                     G["pool_scale"], G["ffn_conv_b"], G["attn_sinks_rows"],
                                     G["ssm_dt_bias_row"], G["ssm_A_log_row"], G["ssm_D_g"].reshape(SSM_G, LANES),
                                     loss_lanes)]
    rep_own, rep_ops = _place_own(rep_ops, gather=True, name="small_own")
    (rep_state,), rep_token = _exchange_start([(rep_ops, rep_own)], gather=True, name="small_start")

    recv = {g: _exchange_wait(sent[g][0], sent[g][1], rep_token, gather=False, name="scatter_wait_" + g)
            for g in ("ffn1", "ssm", "ffn0")}
    update("ssm_w_in", [recv["ssm"][0]])
    update("ssm_w_out", [recv["ssm"][1]])
    update("ssm_conv_w", [recv["ssm"][2]])
    update("ssm_conv_b", [recv["ssm"][3]])
    update("ssm_norm", [recv["ssm"][4]])
    update("ffn_w_up", [recv["ffn0"][0], recv["ffn1"][0]])
    update("ffn_w_down", [recv["ffn0"][1], recv["ffn1"][1]])
    recv["mix"] = _exchange_wait(sent["mix"][0], sent["mix"][1], res["new_v", "ffn_w_down"], gather=False,
                                 name="scatter_wait_mix")
    update("mix_w_in", [recv["mix"][0]])
    update("mix_w_out", [recv["mix"][1]])
    update("ffn_conv_w", [recv["mix"][2]])

    rep = _exchange_wait(rep_ops, rep_state, res["new_v", "mix_w_out"], gather=True, name="small_wait")
    for n, r in zip(("norm_mix", "norm_ffn", "norm_final", "pool_w", "pool_scale", "ffn_conv_b"), rep):
        update(n, [r])
    sinks_rows, bias_row, alog_row, d_g, loss_sum = _sum_slabs(rep[6:], name="sum_head_grads")
    update("attn_sinks", [sinks_rows[:, 0].reshape(1, 1, N_HEADS)])
    update("ssm_dt_bias", [bias_row[:, :NH][None]])
    update("ssm_A_log", [alog_row[:, :NH][None]])
    update("ssm_D", [_ungroup(d_g)[None]])
    loss = loss_sum[0, 0]

    return (loss, grad_x[None], *[res[k, n] for k in ("grad", "delta", "new_m", "new_v") for n in WEIGHTS])
```

```python
import functools
import math

import jax
import jax.numpy as jnp
from jax import lax
from jax.experimental import pallas as pl
from jax.experimental.pallas import tpu as pltpu

F32 = jnp.float32
BF16 = jnp.bfloat16

N_DEV = 8
LANES = 128
HEAD_DIM = 64
N_KV_HEADS = 2
GQ = 4
N_HEADS = N_KV_HEADS * GQ
BLOCK = 128
POOL_GROUPS = 4
ROPE_THETA = 10000.0
SSM_P = 64
SSM_G = 8
SSM_R = 4
SSM_N = 128
SSM_L = 128
NORM_EPS = 1e-6
SSM_NORM_EPS = 1e-5
ADAM_LR, ADAM_B1, ADAM_B2, ADAM_EPS, ADAM_WD, ADAM_STEP = 0.001, 0.9, 0.999, 1e-08, 0.01, 10
VMEM_LIMIT = 56 * 2 ** 20
PAYLOAD = jnp.bfloat16

REPLICATED = ("norm_mix", "norm_ffn", "norm_final", "pool_w", "pool_scale", "attn_sinks",
              "ssm_dt_bias", "ssm_A_log", "ssm_D", "ffn_conv_b")
WEIGHTS = ("norm_mix", "norm_ffn", "norm_final", "mix_w_in", "pool_w", "pool_scale", "attn_sinks", "mix_w_out",
           "ssm_w_in", "ssm_conv_w", "ssm_conv_b", "ssm_dt_bias", "ssm_A_log", "ssm_D", "ssm_norm", "ssm_w_out",
           "ffn_w_up", "ffn_conv_w", "ffn_conv_b", "ffn_w_down")


def _tile(n, cap):
    if n <= cap:
        return n
    best = None
    for d in range(LANES, cap + 1, LANES):
        if n % d == 0:
            best = d
    assert best is not None, (n, cap)
    return best


def _call(body, *, name, out_shape, grid=None, in_specs=None, out_specs=None, scratch=(), aliases=None):
    kw = {}
    if grid is not None:
        kw = dict(grid=grid, in_specs=in_specs, out_specs=out_specs)
    if aliases:
        kw["input_output_aliases"] = aliases
    return pl.pallas_call(
        body, name=name, out_shape=out_shape, scratch_shapes=list(scratch),
        compiler_params=pltpu.CompilerParams(vmem_limit_bytes=VMEM_LIMIT), **kw)


ANY = pl.BlockSpec(memory_space=pl.ANY)


def _sds(shape, dtype=F32):
    return jax.ShapeDtypeStruct(tuple(shape), dtype)


def _sigmoid(x):
    return 1.0 / (1.0 + jnp.exp(-x))


def _shift_dn(x, d, t):
    if d == 0:
        return x
    return jnp.where(t >= d, pltpu.roll(x, d, axis=0), 0.0)


def _shift_up(x, d, t):
    if d == 0:
        return x
    n = x.shape[0]
    return jnp.where(t < n - d, pltpu.roll(x, n - d, axis=0), 0.0)


def _mm(a, b, *, name, ta=False, tb=False, res=None, out_dtype=F32, b_rows=None, out_rows=None):
    M, K = (a.shape[1], a.shape[0]) if ta else a.shape
    b0, bn = b_rows if b_rows is not None else (0, b.shape[0])
    N = bn if tb else b.shape[1]
    assert (b.shape[1] if tb else bn) == K, (a.shape, b.shape, ta, tb, b_rows)
    tm, tn, tk = _tile(M, 1408), _tile(N, 1408), _tile(K, 1408)
    nk = K // tk
    dims = (((0 if ta else 1,), (1 if tb else 0,)), ((), ()))
    aliased = out_rows is not None and out_rows[2] is not None

    def body(*refs):
        a_ref, b_ref = refs[:2]
        r_ref = refs[2] if res is not None else None
        o_ref, acc = refs[-2:]
        k = pl.program_id(2)

        @pl.when(k == 0)
        def _():
            acc[...] = jnp.zeros_like(acc)

        acc[...] += lax.dot_general(a_ref[...].astype(BF16), b_ref[...].astype(BF16), dims,
                                    preferred_element_type=F32)

        @pl.when(k == nk - 1)
        def _():
            out = acc[...]
            if res is not None:
                out = out + r_ref[...]
            o_ref[...] = out.astype(out_dtype)

    a_spec = pl.BlockSpec((tk, tm), lambda i, j, k: (k, i)) if ta else pl.BlockSpec((tm, tk), lambda i, j, k: (i, k))
    if tb:
        assert b0 % tn == 0, (b_rows, tn)
        b_spec = pl.BlockSpec((tn, tk), lambda i, j, k: (b0 // tn + j, k))
    else:
        assert b0 % tk == 0, (b_rows, tk)
        b_spec = pl.BlockSpec((tk, tn), lambda i, j, k: (b0 // tk + k, j))
    ins, specs = [a, b], [a_spec, b_spec]
    if res is not None:
        ins.append(res)
        specs.append(pl.BlockSpec((tm, tn), lambda i, j, k: (i, j)))
    aliases = None
    if out_rows is None:
        o_spec = pl.BlockSpec((tm, tn), lambda i, j, k: (i, j))
        out_shape = _sds((M, N), out_dtype)
    else:
        total, o0, prev = out_rows
        assert o0 % tm == 0, (out_rows, tm)
        o_spec = pl.BlockSpec((tm, tn), lambda i, j, k: (o0 // tm + i, j))
        out_shape = _sds((total, N), out_dtype)
        if aliased:
            aliases = {len(ins): 0}
            ins.append(prev)
            specs.append(ANY)
    return _call(body, name=name, out_shape=out_shape, grid=(M // tm, N // tn, nk), in_specs=specs,
                 out_specs=o_spec, scratch=[pltpu.VMEM((tm, tn), F32)], aliases=aliases)(*ins)


def _rmsnorm(x, w, *, name, eps=NORM_EPS, after=None):
    S, D = x.shape
    tm = _tile(S, 1024)
    tie = [] if after is None else [after]

    def body(x_ref, w_ref, *rest):
        o_ref = rest[-1]
        xf = x_ref[...]
        r = lax.rsqrt(jnp.mean(xf * xf, axis=-1, keepdims=True) + eps)
        o_ref[...] = (xf * r * w_ref[...]).astype(BF16)

    return _call(body, name=name, out_shape=_sds((S, D), BF16), grid=(S // tm,),
                 in_specs=[pl.BlockSpec((tm, D), lambda i: (i, 0)), pl.BlockSpec((1, D), lambda i: (0, 0))] + [ANY] * len(tie),
                 out_specs=pl.BlockSpec((tm, D), lambda i: (i, 0)))(x, w, *tie)


def _norm_bwd_math(xf, w, dh, eps):
    r = lax.rsqrt(jnp.mean(xf * xf, axis=-1, keepdims=True) + eps)
    xhat = xf * r
    dxh = dh * w
    dx = r * (dxh - xhat * jnp.mean(dxh * xhat, axis=-1, keepdims=True))
    dw = jnp.sum(dh * xhat, axis=0, keepdims=True)
    return dx, dw


def _mm_norm_bwd(a, b, x, w, dres, *, name, res=None, b_rows=None, after=None, eps=NORM_EPS):
    M, K = a.shape
    b0, bn = b_rows if b_rows is not None else (0, b.shape[0])
    D = b.shape[1]
    assert bn == K and x.shape == (M, D), (a.shape, b.shape, b_rows, x.shape)
    tm, tk = _tile(M, 1024), _tile(K, 1408)
    assert b0 % tk == 0, (b_rows, tk)
    nk = K // tk

    def body(*refs):
        a_ref, b_ref, x_ref, w_ref, dr_ref = refs[:5]
        r_ref = refs[5] if res is not None else None
        dx_ref, dw_ref, acc = refs[-3:]
        i, k = pl.program_id(0), pl.program_id(1)

        @pl.when(k == 0)
        def _():
            acc[...] = jnp.zeros_like(acc)

        @pl.when((i == 0) & (k == 0))
        def _():
            dw_ref[...] = jnp.zeros_like(dw_ref)

        acc[...] += jnp.dot(a_ref[...].astype(BF16), b_ref[...].astype(BF16), preferred_element_type=F32)

        @pl.when(k == nk - 1)
        def _():
            dh = acc[...] if res is None else acc[...] + r_ref[...]
            dx, dw = _norm_bwd_math(x_ref[...], w_ref[...], dh, eps)
            dx_ref[...] = dr_ref[...] + dx
            dw_ref[...] += dw

    row = pl.BlockSpec((tm, D), lambda i, k: (i, 0))
    vec = pl.BlockSpec((1, D), lambda i, k: (0, 0))
    ins = [a, b, x, w, dres]
    specs = [pl.BlockSpec((tm, tk), lambda i, k: (i, k)), pl.BlockSpec((tk, D), lambda i, k: (b0 // tk + k, 0)), row, vec, row]
    if res is not None:
        ins.append(res)
        specs.append(row)
    if after is not None:
        ins.append(after)
        specs.append(ANY)
    return _call(body, name=name, out_shape=(_sds((M, D)), _sds((1, D))), grid=(M // tm, nk), in_specs=specs,
                 out_specs=(row, vec), scratch=[pltpu.VMEM((tm, D), F32)])(*ins)


def _final_loss(x, w, target, *, name):
    S, D = x.shape
    tm = _tile(S, 1024)

    def body(x_ref, w_ref, t_ref, loss_ref, dx_ref, dw_ref):
        xf, wv = x_ref[...], w_ref[...]
        r = lax.rsqrt(jnp.mean(xf * xf, axis=-1, keepdims=True) + NORM_EPS)
        err = xf * r * wv - t_ref[...]
        part = 0.5 * jnp.sum(jnp.mean(err * err, axis=-1, keepdims=True), axis=0, keepdims=True)
        dx, dw = _norm_bwd_math(xf, wv, err * (1.0 / D), NORM_EPS)
        dx_ref[...] = dx

        @pl.when(pl.program_id(0) == 0)
        def _():
            dw_ref[...] = jnp.zeros_like(dw_ref)
            loss_ref[...] = jnp.zeros_like(loss_ref)

        dw_ref[...] += dw
        loss_ref[...] += jnp.broadcast_to(part, loss_ref.shape)

    row = pl.BlockSpec((tm, D), lambda i: (i, 0))
    vec = pl.BlockSpec((1, D), lambda i: (0, 0))
    return _call(body, name=name, out_shape=(_sds((1, LANES)), _sds((S, D)), _sds((1, D))), grid=(S // tm,),
                 in_specs=[row, vec, row], out_specs=(pl.BlockSpec((1, LANES), lambda i: (0, 0)), row, vec))(x, w, target)


def _rope_tables(pos, inv_freq):
    S = pos.shape[0]
    tm = _tile(S, 512)

    def body(p_ref, f_ref, c_ref, s_ref):
        ang = p_ref[...].astype(F32) * f_ref[...]
        c_ref[...] = jnp.cos(ang)
        s_ref[...] = jnp.sin(ang)

    blk = pl.BlockSpec((tm, LANES), lambda i: (i, 0))
    return _call(body, name="rope_tables", out_shape=(_sds((S, LANES)), _sds((S, LANES))), grid=(S // tm,),
                 in_specs=[pl.BlockSpec((tm, 1), lambda i: (i, 0)), pl.BlockSpec((1, LANES), lambda i: (0, 0))],
                 out_specs=(blk, blk))(pos, inv_freq)


def _rot_half(t):
    lane = lax.broadcasted_iota(jnp.int32, t.shape, 1)
    lo = (lane % HEAD_DIM) < (HEAD_DIM // 2)
    return jnp.where(lo, -pltpu.roll(t, LANES - HEAD_DIM // 2, axis=1), pltpu.roll(t, HEAD_DIM // 2, axis=1))


def _rope(t, c, s):
    return t * c + _rot_half(t) * s


def _unrope(dy, c, s):
    return dy * c - _rot_half(dy * s)


PD = POOL_GROUPS * LANES
QD = N_HEADS * HEAD_DIM
KD = N_KV_HEADS * HEAD_DIM
assert PD % QD == 0 and (PD + QD) % (2 * KD) == 0 and KD == LANES
def _attn_probs(q, kcat, sink, mask):
    s = lax.dot_general(q.astype(BF16), kcat, (((1,), (1,)), ((), ())), preferred_element_type=F32) * (HEAD_DIM ** -0.5)
    s = jnp.where(mask, s, -jnp.inf)
    m = jnp.maximum(jnp.max(s, axis=1, keepdims=True), sink)
    p = jnp.exp(s - m)
    ps = jnp.exp(sink - m)
    inv = 1.0 / (jnp.sum(p, axis=1, keepdims=True) + ps)
    return p * inv, ps * inv


def _attn_mask(n):
    qi = lax.broadcasted_iota(jnp.int32, (BLOCK, 2 * BLOCK), 0)
    kj = lax.broadcasted_iota(jnp.int32, (BLOCK, 2 * BLOCK), 1)
    rel = qi + BLOCK - kj
    return (rel >= 0) & (rel < BLOCK) & ((n > 0) | (kj >= BLOCK))


def _attn_in_specs(nb):
    def cur(n):
        return jnp.minimum(n, nb - 1)

    def prev(n):
        return jnp.clip(n - 1, 0, nb - 1)

    kvb = (PD + QD) // (2 * KD)
    return [pl.BlockSpec(memory_space=pltpu.SMEM),
            pl.BlockSpec((BLOCK, QD), lambda n: (cur(n), PD // QD)),
            pl.BlockSpec((BLOCK, 2 * KD), lambda n: (cur(n), kvb)),
            pl.BlockSpec((BLOCK, 2 * KD), lambda n: (prev(n), kvb)),
            pl.BlockSpec((BLOCK, LANES), lambda n: (cur(n), 0)), pl.BlockSpec((BLOCK, LANES), lambda n: (cur(n), 0)),
            pl.BlockSpec((BLOCK, LANES), lambda n: (prev(n), 0)), pl.BlockSpec((BLOCK, LANES), lambda n: (prev(n), 0))]


def _attn_keys(kvc_ref, kvp_ref, cc, sc, cp, sp):
    kc = _rope(kvc_ref[:, :KD], cc, sc)
    kp = _rope(kvp_ref[:, :KD], cp, sp)
    vc, vp = kvc_ref[:, KD:], kvp_ref[:, KD:]
    kcat, vcat = [], []
    for kk in range(N_KV_HEADS):
        sl = slice(kk * HEAD_DIM, (kk + 1) * HEAD_DIM)
        kcat.append(jnp.concatenate([kp[:, sl], kc[:, sl]], axis=0).astype(BF16))
        vcat.append(jnp.concatenate([vp[:, sl], vc[:, sl]], axis=0).astype(BF16))
    return kcat, vcat


def _attn_fwd(proj, cos, sin, sinks, cat):
    S = proj.shape[0]
    nb = S // BLOCK

    def body(sink_ref, q_ref, kvc_ref, kvp_ref, cc_ref, sc_ref, cp_ref, sp_ref, cat_ref, o_ref):
        n = pl.program_id(0)
        cc, sc = cc_ref[...], sc_ref[...]
        kcat, vcat = _attn_keys(kvc_ref, kvp_ref, cc, sc, cp_ref[...], sp_ref[...])
        mask = _attn_mask(n)
        def head_pair(j):
            qr = _rope(q_ref[:, j * LANES:(j + 1) * LANES], cc, sc)
            for e in range(LANES // HEAD_DIM):
                yield
                h = j * (LANES // HEAD_DIM) + e
                pn, _ = _attn_probs(qr[:, e * HEAD_DIM:(e + 1) * HEAD_DIM], kcat[h // GQ], sink_ref[0, h], mask)
                yield
                o_ref[:, h * HEAD_DIM:(h + 1) * HEAD_DIM] = jnp.dot(
                    pn.astype(BF16), vcat[h // GQ], preferred_element_type=F32).astype(o_ref.dtype)

        _interleave([head_pair(j) for j in range(QD // LANES)])

    return _call(body, name="attn_fwd", out_shape=_sds(cat.shape, cat.dtype), grid=(nb,),
                 in_specs=_attn_in_specs(nb) + [ANY], out_specs=pl.BlockSpec((BLOCK, QD), lambda n: (n, PD // QD)),
                 aliases={8: 0})(sinks, proj, proj, proj, cos, sin, cos, sin, cat)


def _attn_bwd(proj, cos, sin, sinks, dcat):
    S = proj.shape[0]
    nb = S // BLOCK
    scale = HEAD_DIM ** -0.5
    per = LANES // HEAD_DIM

    def body(sink_ref, q_ref, kvc_ref, kvp_ref, cc_ref, sc_ref, cp_ref, sp_ref, do_ref, o_ref, ds_ref, hold, carry, part, pair):
        n = pl.program_id(0)

        @pl.when(n == 0)
        def _():
            hold[...] = jnp.zeros_like(hold)
            carry[...] = jnp.zeros_like(carry)
            ds_ref[...] = jnp.zeros_like(ds_ref)

        live = jnp.where(n < nb, 1.0, 0.0)
        cc, sc, cp, sp = cc_ref[...], sc_ref[...], cp_ref[...], sp_ref[...]
        kcat, vcat = _attn_keys(kvc_ref, kvp_ref, cc, sc, cp, sp)
        mask = _attn_mask(n)
        o_ref[:, :PD] = jnp.zeros((BLOCK, PD), F32)
        o_ref[:, PD:PD + QD] = hold[...]
        dk = [jnp.zeros((2 * BLOCK, HEAD_DIM), F32) for _ in range(N_KV_HEADS)]
        dv = [jnp.zeros((2 * BLOCK, HEAD_DIM), F32) for _ in range(N_KV_HEADS)]
        row = lax.broadcasted_iota(jnp.int32, (8, LANES), 0)
        acc = {"dsk": jnp.zeros((8, LANES), F32)}

        def head_pair(j):
            qr = _rope(q_ref[:, j * LANES:(j + 1) * LANES], cc, sc)
            for e in range(per):
                yield
                h = j * per + e
                kk = h // GQ
                qh = qr[:, e * HEAD_DIM:(e + 1) * HEAD_DIM]
                pn, psn = _attn_probs(qh, kcat[kk], sink_ref[0, h], mask)
                yield
                doh = (do_ref[:, h * HEAD_DIM:(h + 1) * HEAD_DIM] * live).astype(BF16)
                dp = lax.dot_general(doh, vcat[kk], NT, preferred_element_type=F32)
                yield
                delta = jnp.sum(pn * dp, axis=1, keepdims=True)
                ds = (pn * (dp - delta) * scale).astype(BF16)
                pair[j, :, e * HEAD_DIM:(e + 1) * HEAD_DIM] = jnp.dot(ds, kcat[kk], preferred_element_type=F32)
                yield
                dk[kk] = dk[kk] + lax.dot_general(ds, qh.astype(BF16), TN, preferred_element_type=F32)
                dv[kk] = dv[kk] + lax.dot_general(pn.astype(BF16), doh, TN, preferred_element_type=F32)
                acc["dsk"] = acc["dsk"] + jnp.where(row == h, -jnp.sum(psn * delta), 0.0)
            yield
            hold[:, j * LANES:(j + 1) * LANES] = _unrope(pair[j], cc, sc)

        _interleave([head_pair(j) for j in range(QD // LANES)])
        dsk = acc["dsk"]
        for kk in range(N_KV_HEADS):
            sl = slice(kk * HEAD_DIM, (kk + 1) * HEAD_DIM)
            sv = slice(KD + kk * HEAD_DIM, KD + (kk + 1) * HEAD_DIM)
            part[0, :, sl] = dk[kk][:BLOCK]
            part[0, :, sv] = dv[kk][:BLOCK]
            part[1, :, sl] = dk[kk][BLOCK:]
            part[1, :, sv] = dv[kk][BLOCK:]
        done = carry[...] + part[0]
        o_ref[:, PD + QD:PD + QD + KD] = _unrope(done[:, :KD], cp, sp)
        o_ref[:, PD + QD + KD:] = done[:, KD:]
        carry[...] = part[1]
        ds_ref[...] += dsk

    return _call(body, name="attn_bwd", out_shape=(_sds((S, PD + QD + 2 * KD)), _sds((8, LANES))), grid=(nb + 1,),
                 in_specs=_attn_in_specs(nb) + [pl.BlockSpec((BLOCK, QD), lambda n: (jnp.minimum(n, nb - 1), PD // QD))],
                 out_specs=(pl.BlockSpec((BLOCK, PD + QD + 2 * KD), lambda n: (jnp.maximum(n - 1, 0), 0)),
                            pl.BlockSpec((8, LANES), lambda n: (0, 0))),
                 scratch=[pltpu.VMEM((BLOCK, QD), F32), pltpu.VMEM((BLOCK, 2 * KD), F32),
                          pltpu.VMEM((2, BLOCK, 2 * KD), F32), pltpu.VMEM((QD // LANES, BLOCK, LANES), F32)])(
                     sinks, proj, proj, proj, cos, sin, cos, sin, dcat)


def _pool_sums(u, g, t, shift):
    s2 = u + shift(u, 1, t)
    s4 = s2 + shift(s2, 2, t)
    s8 = s4 + shift(s4, 4, t)
    s16 = s8 + shift(s8, 8, t)
    return jnp.where(g == 0, s2, jnp.where(g == 1, s4, jnp.where(g == 2, s8, s16)))


def _pool_specs(S):
    col = pl.BlockSpec((S, LANES), lambda g: (0, g))
    wsp = pl.BlockSpec((1, LANES, LANES), lambda g: (g, 0, 0))
    vec = pl.BlockSpec((1, LANES), lambda g: (0, g))
    return col, wsp, vec


def _pool_fwd(proj, pool_w, scale):
    S = proj.shape[0]
    col, wsp, vec = _pool_specs(S)

    def body(u_ref, w_ref, s_ref, o_ref):
        g = pl.program_id(0)
        u = u_ref[...]
        t = lax.broadcasted_iota(jnp.int32, u.shape, 0)
        cnt = jnp.minimum(t + 1, 2 << g).astype(F32)
        pm = _pool_sums(u, g, t, _shift_dn) / cnt - u
        o_ref[...] = (jnp.dot(pm.astype(BF16), w_ref[0].astype(BF16), preferred_element_type=F32) * s_ref[...]).astype(BF16)

    return _call(body, name="pool_fwd", out_shape=_sds((S, PD + QD), BF16), grid=(POOL_GROUPS,),
                 in_specs=[col, wsp, vec], out_specs=col)(proj, pool_w, scale)


def _pool_bwd(proj, pool_w, scale, dcat, dproj):
    S = proj.shape[0]
    col, wsp, vec = _pool_specs(S)

    def body(u_ref, w_ref, s_ref, d_ref, dproj_ref, du_ref, dw_ref, dsc_ref):
        g = pl.program_id(0)
        u = u_ref[...]
        t = lax.broadcasted_iota(jnp.int32, u.shape, 0)
        cnt = jnp.minimum(t + 1, 2 << g).astype(F32)
        pm = (_pool_sums(u, g, t, _shift_dn) / cnt - u).astype(BF16)
        wv = w_ref[0].astype(BF16)
        d = d_ref[...]
        pw = jnp.dot(pm, wv, preferred_element_type=F32)
        dsc_ref[...] = jnp.sum(pw * d, axis=0, keepdims=True)
        dpw = (d * s_ref[...]).astype(BF16)
        dw_ref[0] = lax.dot_general(pm, dpw, (((0,), (0,)), ((), ())), preferred_element_type=F32)
        dpm = lax.dot_general(dpw, wv, (((1,), (1,)), ((), ())), preferred_element_type=F32)
        du_ref[...] = _pool_sums(dpm / cnt, g, t, _shift_up) - dpm

    return _call(body, name="pool_bwd",
                 out_shape=(_sds(dproj.shape), _sds((POOL_GROUPS, LANES, LANES)), _sds((1, POOL_GROUPS * LANES))),
                 grid=(POOL_GROUPS,), in_specs=[col, wsp, vec, col, ANY], out_specs=(col, wsp, vec),
                 aliases={4: 0})(proj, pool_w, scale, dcat, dproj)


def _conv(x, w_ref, b_ref, t):
    K = w_ref.shape[0]
    y = b_ref[...] + jnp.zeros_like(x)
    for k in range(K):
        y = y + w_ref[k:k + 1, :] * _shift_dn(x, K - 1 - k, t)
    return y


def _silu_grad(y):
    sg = _sigmoid(y)
    return sg * (1.0 + y * (1.0 - sg))


CONV_ROWS = 256
HALO = 8


def _win_above(ref, r0):
    if isinstance(r0, int):
        assert r0 == 0
        return jnp.concatenate([jnp.zeros((HALO, ref.shape[1]), F32), ref[0:CONV_ROWS, :]], axis=0)
    return ref[pl.ds(pl.multiple_of(r0 - HALO, HALO), CONV_ROWS + HALO), :]


def _rows_at(win, start):
    if start % 8 == 0:
        return win[start:start + CONV_ROWS]
    base = start // 8 * 8
    return pltpu.roll(win, win.shape[0] - (start - base), axis=0)[base:base + CONV_ROWS]


def _taps_above(win, K):
    return [_rows_at(win, HALO - (K - 1 - k)) for k in range(K)]


def _conv_taps(taps, w, b):
    y = b
    for k in range(len(w)):
        y = y + w[k] * taps[k]
    return y


def _conv_t_win(win, w):
    K = len(w)
    out = None
    for k in range(K):
        d = K - 1 - k
        term = w[k] * _rows_at(win, d)
        out = term if out is None else out + term
    return out


def _fold8(x):
    return jnp.sum(x.reshape(CONV_ROWS // 8, 8, x.shape[-1]), axis=0)


def _chunk_loop(S, step, init):
    carry = step(0, init)
    return lax.fori_loop(1, S // CONV_ROWS, lambda i, c: step(pl.multiple_of(i * CONV_ROWS, CONV_ROWS), c), carry)


def _ffn_mid_specs(S, K, layer, nf):
    return [pl.BlockSpec((S, LANES), lambda j: (0, j)), pl.BlockSpec((S, LANES), lambda j: (0, nf + j)),
            pl.BlockSpec((None, K, LANES), lambda j: (layer, 0, j)), pl.BlockSpec((None, K, LANES), lambda j: (layer, 0, nf + j)),
            pl.BlockSpec((None, 1, LANES), lambda j: (layer, 0, j)), pl.BlockSpec((None, 1, LANES), lambda j: (layer, 0, nf + j))]


def _ffn_mid_fwd(a, cw, cb, layer):
    S, F = a.shape[0], a.shape[1] // 2
    nf = F // LANES
    K = cw.shape[1]

    def body(au_ref, ag_ref, wu_ref, wg_ref, bu_ref, bg_ref, o_ref):
        t = lax.broadcasted_iota(jnp.int32, (S, LANES), 0)
        hu = _conv(au_ref[...], wu_ref, bu_ref, t)
        hg = _conv(ag_ref[...], wg_ref, bg_ref, t)
        o_ref[...] = (hg * _sigmoid(hg) * hu).astype(BF16)

    return _call(body, name="ffn_mid_fwd", out_shape=_sds((S, F), BF16), grid=(nf,),
                 in_specs=_ffn_mid_specs(S, K, layer, nf), out_specs=pl.BlockSpec((S, LANES), lambda j: (0, j)))(
                     a, a, cw, cw, cb[:, None], cb[:, None])


def _ffn_mid_bwd(a, cw, cb, layer, dact):
    S, F = a.shape[0], a.shape[1] // 2
    nf = F // LANES
    K = cw.shape[1]

    def body(au_ref, ag_ref, wu_ref, wg_ref, bu_ref, bg_ref, d_ref, dau_ref, dag_ref, dwu_ref, dwg_ref, dbu_ref, dbg_ref,
             dhu_s, dhg_s):
        T = CONV_ROWS
        wu = [wu_ref[k:k + 1, :] for k in range(K)]
        wg = [wg_ref[k:k + 1, :] for k in range(K)]
        bu, bg = bu_ref[...], bg_ref[...]
        zero8 = jnp.zeros((HALO, LANES), F32)
        dhu_s[S:S + HALO, :] = zero8
        dhg_s[S:S + HALO, :] = zero8

        def first_pass(r0, acc):
            tu, tg = _taps_above(_win_above(au_ref, r0), K), _taps_above(_win_above(ag_ref, r0), K)
            hu, hg = _conv_taps(tu, wu, bu), _conv_taps(tg, wg, bg)
            d = d_ref[pl.ds(r0, T), :].astype(F32)
            sg = _sigmoid(hg)
            dhu = d * hg * sg
            dhg = d * hu * (sg * (1.0 + hg * (1.0 - sg)))
            dhu_s[pl.ds(r0, T), :] = dhu
            dhg_s[pl.ds(r0, T), :] = dhg
            new = []
            for dh, taps in ((dhu, tu), (dhg, tg)):
                for k in range(K):
                    new.append(acc[len(new)] + _fold8(dh * taps[k]))
            new.append(acc[2 * K] + _fold8(dhu))
            new.append(acc[2 * K + 1] + _fold8(dhg))
            return tuple(new)

        acc = _chunk_loop(S, first_pass, tuple(zero8 for _ in range(2 * K + 2)))
        for k in range(K):
            dwu_ref[k:k + 1, :] = jnp.sum(acc[k], axis=0, keepdims=True)
            dwg_ref[k:k + 1, :] = jnp.sum(acc[K + k], axis=0, keepdims=True)
        dbu_ref[...] = jnp.sum(acc[2 * K], axis=0, keepdims=True)
        dbg_ref[...] = jnp.sum(acc[2 * K + 1], axis=0, keepdims=True)

        def second_pass(i, carry):
            r0 = pl.multiple_of(i * T, T)
            dau_ref[pl.ds(r0, T), :] = _conv_t_win(dhu_s[pl.ds(r0, T + HALO), :], wu).astype(BF16)
            dag_ref[pl.ds(r0, T), :] = _conv_t_win(dhg_s[pl.ds(r0, T + HALO), :], wg).astype(BF16)
            return carry

        lax.fori_loop(0, S // T, second_pass, 0)

    col = pl.BlockSpec((S, LANES), lambda j: (0, j))
    wsp = pl.BlockSpec((K, LANES), lambda j: (0, j))
    bsp = pl.BlockSpec((1, LANES), lambda j: (0, j))
    dau, dag, dwu, dwg, dbu, dbg = _call(
        body, name="ffn_mid_bwd",
        out_shape=(_sds((S, F), BF16), _sds((S, F), BF16), _sds((K, F)), _sds((K, F)), _sds((1, F)), _sds((1, F))), grid=(nf,),
        in_specs=_ffn_mid_specs(S, K, layer, nf) + [col], out_specs=(col, col, wsp, wsp, bsp, bsp),
        scratch=[pltpu.VMEM((S + HALO, LANES), F32), pltpu.VMEM((S + HALO, LANES), F32)])(
            a, a, cw, cw, cb[:, None], cb[:, None], dact)
    return dau, dag, jnp.concatenate([dwu, dwg], axis=1), jnp.concatenate([dbu, dbg], axis=1)


def _conv_silu_fwd(x, cw, cb):
    S = x.shape[0]
    K, C = cw.shape

    def body(x_ref, w_ref, b_ref, o_ref):
        t = lax.broadcasted_iota(jnp.int32, (S, LANES), 0)
        y = _conv(x_ref[...], w_ref, b_ref, t)
        o_ref[...] = y * _sigmoid(y)

    col = pl.BlockSpec((S, LANES), lambda j: (0, j))
    return _call(body, name="conv_silu_fwd", out_shape=_sds((S, C)), grid=(C // LANES,),
                 in_specs=[col, pl.BlockSpec((K, LANES), lambda j: (0, j)), pl.BlockSpec((1, LANES), lambda j: (0, j))],
                 out_specs=col)(x, cw, cb)


def _conv_silu_bwd(x, cw, cb, douts):
    S = x.shape[0]
    K, C = cw.shape
    starts, off = [], 0
    for d in douts:
        starts.append(off)
        off += d.shape[1] // LANES
    assert off == C // LANES

    def body(x_ref, w_ref, b_ref, *rest):
        dy_s = rest[-1]
        d_refs, (dx_ref, dw_ref, db_ref) = rest[:len(douts)], rest[len(douts):-1]
        j = pl.program_id(0)
        T = CONV_ROWS
        w = [w_ref[k:k + 1, :] for k in range(K)]
        b = b_ref[...]
        zero8 = jnp.zeros((HALO, LANES), F32)
        dy_s[S:S + HALO, :] = zero8

        def first_pass(r0, acc):
            taps = _taps_above(_win_above(x_ref, r0), K)
            y = _conv_taps(taps, w, b)
            d = d_refs[0][pl.ds(r0, T), :]
            for i in range(1, len(douts)):
                d = jnp.where(j >= starts[i], d_refs[i][pl.ds(r0, T), :], d)
            dy = d * _silu_grad(y)
            dy_s[pl.ds(r0, T), :] = dy
            return tuple(acc[k] + _fold8(dy * taps[k]) for k in range(K)) + (acc[K] + _fold8(dy),)

        acc = _chunk_loop(S, first_pass, tuple(zero8 for _ in range(K + 1)))
        for k in range(K):
            dw_ref[k:k + 1, :] = jnp.sum(acc[k], axis=0, keepdims=True)
        db_ref[...] = jnp.sum(acc[K], axis=0, keepdims=True)

        def second_pass(i, carry):
            r0 = pl.multiple_of(i * T, T)
            dx_ref[pl.ds(r0, T), :] = _conv_t_win(dy_s[pl.ds(r0, T + HALO), :], w).astype(BF16)
            return carry

        lax.fori_loop(0, S // T, second_pass, 0)

    col = pl.BlockSpec((S, LANES), lambda j: (0, j))
    wsp = pl.BlockSpec((K, LANES), lambda j: (0, j))
    bsp = pl.BlockSpec((1, LANES), lambda j: (0, j))

    def dspec(i):
        nblk = douts[i].shape[1] // LANES
        return pl.BlockSpec((S, LANES), lambda j: (0, jnp.clip(j - starts[i], 0, nblk - 1)))

    return _call(body, name="conv_silu_bwd", out_shape=(_sds((S, C), BF16), _sds((K, C)), _sds((1, C))), grid=(C // LANES,),
                 in_specs=[col, wsp, bsp] + [dspec(i) for i in range(len(douts))],
                 out_specs=(col, wsp, bsp), scratch=[pltpu.VMEM((S + HALO, LANES), F32)])(x, cw, cb, *douts)


HI = lax.Precision.HIGHEST


def _to_group(x, g, live):
    return jnp.where(live, x if g == 0 else pltpu.roll(x, LANES - SSM_R * g, axis=1), 0.0)


def _from_groups(refs_g):
    out = refs_g[0]
    for g in range(1, SSM_G):
        out = out + pltpu.roll(refs_g[g], SSM_R * g, axis=1)
    return out


def _ssd_prep_fwd(raw, bias_row, alog_row):
    S = raw.shape[0]
    nc = S // SSM_L
    NH = SSM_G * SSM_R

    def body(raw_ref, b_ref, al_ref, pre_ref, dth_ref, dt_ref, acs_ref, acst_ref):
        r_i = lax.broadcasted_iota(jnp.int32, (LANES, LANES), 0)
        c_i = lax.broadcasted_iota(jnp.int32, (LANES, LANES), 1)
        live = c_i < SSM_R
        tril = jnp.where(r_i >= c_i, 1.0, 0.0)
        pre = raw_ref[...] + b_ref[...]
        dt = jnp.where(c_i < NH, jnp.logaddexp(pre, 0.0), 0.0)
        acs = jnp.dot(tril, dt * (-jnp.exp(al_ref[...])), preferred_element_type=F32, precision=HI)
        pre_ref[...] = pre
        dth_ref[...] = dt
        for g in range(SSM_G):
            acs_g = _to_group(acs, g, live)
            dt_ref[g] = _to_group(dt, g, live)
            acs_ref[g] = acs_g
            acst_ref[g] = acs_g.T

    row = pl.BlockSpec((1, LANES), lambda c: (0, 0))
    flat = pl.BlockSpec((SSM_L, LANES), lambda c: (c, 0))
    blk = pl.BlockSpec((SSM_G, SSM_L, LANES), lambda c: (0, c, 0))
    big = _sds((SSM_G, S, LANES))
    return _call(body, name="ssd_prep_fwd",
                 out_shape=(_sds((S, LANES)), _sds((S, LANES)), big, big, _sds((SSM_G, LANES, S))), grid=(nc,),
                 in_specs=[flat, row, row],
                 out_specs=(flat, flat, blk, blk, pl.BlockSpec((SSM_G, LANES, SSM_L), lambda c: (0, 0, c))))(
                     raw, bias_row, alog_row)


def _ssd_prep_bwd(pre_h, dt_h, alog_row, ddt_g, dacs_g, dacst_g):
    S = pre_h.shape[0]
    nc = S // SSM_L
    NH = SSM_G * SSM_R

    def body(pre_ref, dt_ref, al_ref, ddt_ref, dacs_ref, dacst_ref, draw_ref, db_ref, dal_ref):
        c = pl.program_id(0)
        r_i = lax.broadcasted_iota(jnp.int32, (LANES, LANES), 0)
        c_i = lax.broadcasted_iota(jnp.int32, (LANES, LANES), 1)
        triu = jnp.where(r_i <= c_i, 1.0, 0.0)

        @pl.when(c == 0)
        def _():
            db_ref[...] = jnp.zeros_like(db_ref)
            dal_ref[...] = jnp.zeros_like(dal_ref)

        dacs = _from_groups([dacs_ref[g] + dacst_ref[g].T for g in range(SSM_G)])
        da = jnp.dot(triu, dacs, preferred_element_type=F32, precision=HI)
        A = -jnp.exp(al_ref[...])
        ddt = _from_groups([ddt_ref[g] for g in range(SSM_G)]) + da * A
        dpre = jnp.where(c_i < NH, ddt * _sigmoid(pre_ref[...]), 0.0)
        draw_ref[...] = dpre
        db_ref[...] += jnp.sum(dpre, axis=0, keepdims=True)
        dal_ref[...] += jnp.where(c_i[:1] < NH, jnp.sum(da * dt_ref[...], axis=0, keepdims=True) * A, 0.0)

    row = pl.BlockSpec((1, LANES), lambda c: (0, 0))
    flat = pl.BlockSpec((SSM_L, LANES), lambda c: (c, 0))
    blk = pl.BlockSpec((SSM_G, SSM_L, LANES), lambda c: (0, c, 0))
    return _call(body, name="ssd_prep_bwd", out_shape=(_sds((S, LANES)), _sds((1, LANES)), _sds((1, LANES))), grid=(nc,),
                 in_specs=[flat, flat, row, blk, blk, pl.BlockSpec((SSM_G, LANES, SSM_L), lambda c: (0, 0, c))],
                 out_specs=(flat, row, row))(pre_h, dt_h, alog_row, ddt_g, dacs_g, dacst_g)


NT = (((1,), (1,)), ((), ()))
TN = (((0,), (0,)), ((), ()))
SSM_HP = SSM_R * SSM_P


def _ssd_group_terms(xs_ref, dt_ref, acs_ref, d_ref):
    hid = lax.broadcasted_iota(jnp.int32, (1, SSM_HP), 1) // SSM_P
    rid = lax.broadcasted_iota(jnp.int32, (SSM_HP, 1), 0) // SSM_P

    def widen(cols):
        out = cols[0]
        for r in range(1, SSM_R):
            out = jnp.where(hid == r, cols[r], out)
        return out

    dt_c = [dt_ref[:, r:r + 1] for r in range(SSM_R)]
    acs_c = [acs_ref[:, r:r + 1] for r in range(SSM_R)]
    last = [acs_ref[SSM_L - 1:SSM_L, r:r + 1] for r in range(SSM_R)]
    decay_c = [jnp.exp(last[r] - acs_c[r]) for r in range(SSM_R)]
    cd = [jnp.exp(last[r]) for r in range(SSM_R)]
    cd_rows = cd[0]
    for r in range(1, SSM_R):
        cd_rows = jnp.where(rid == r, cd[r], cd_rows)
    xs = xs_ref[...]
    return (xs, xs * widen(dt_c), widen([jnp.exp(a) for a in acs_c]), widen(decay_c),
            widen([d_ref[:, r:r + 1] for r in range(SSM_R)]), cd_rows, dt_c, decay_c, cd)


def _ssd_lmat(acs_ref, acst_ref, r, tril):
    return jnp.exp(jnp.where(tril, acs_ref[:, r:r + 1] - acst_ref[r:r + 1, :], -jnp.inf))


SSM_GPS = 8


def _ssd_specs(rev, nc):
    def cc(c):
        return nc - 1 - c if rev else c
    xs_blocks = (SSM_G * SSM_HP) // (SSM_GPS * SSM_N)
    xs = pl.BlockSpec((SSM_L, SSM_GPS * SSM_HP), lambda g, c: (cc(c), g))
    bsp = pl.BlockSpec((SSM_L, SSM_GPS * SSM_N), lambda g, c: (cc(c), xs_blocks + g))
    csp = pl.BlockSpec((SSM_L, SSM_GPS * SSM_N), lambda g, c: (cc(c), xs_blocks + SSM_G // SSM_GPS + g))
    sc = pl.BlockSpec((SSM_GPS, SSM_L, LANES), lambda g, c: (g, cc(c), 0))
    sct = pl.BlockSpec((SSM_GPS, LANES, SSM_L), lambda g, c: (g, 0, cc(c)))
    gsp = pl.BlockSpec((SSM_GPS, 1, LANES), lambda g, c: (g, 0, 0))
    st = pl.BlockSpec((None, SSM_GPS, SSM_HP, SSM_N), lambda g, c: (cc(c), g, 0, 0))
    return xs, bsp, csp, sc, sct, gsp, st


def _interleave(gens):
    live = list(gens)
    while live:
        for g in list(live):
            try:
                next(g)
            except StopIteration:
                live.remove(g)


def _rounds(gens):
    live = list(gens)
    while live:
        for g in list(live):
            try:
                next(g)
            except StopIteration:
                live.remove(g)
        yield


def _ssd_group_views(gg, xs_ref, b_ref, c_ref, *per_group):
    return (xs_ref.at[:, gg * SSM_HP:(gg + 1) * SSM_HP], b_ref.at[:, gg * SSM_N:(gg + 1) * SSM_N],
            c_ref.at[:, gg * SSM_N:(gg + 1) * SSM_N]) + tuple(r.at[gg] for r in per_group)


def _ssd_fwd(xbc, dt_g, acs_g, acst_g, d_g):
    S = xbc.shape[0]
    nc = S // SSM_L
    xs_s, b_s, c_s, sc, sct, gsp, st = _ssd_specs(False, nc)

    def body(xs_ref, b_ref, c_ref, dt_ref, acs_ref, acst_ref, d_ref, y_ref, st_ref, state):
        c = pl.program_id(1)

        @pl.when(c == 0)
        def _():
            state[...] = jnp.zeros_like(state)

        tril = lax.broadcasted_iota(jnp.int32, (SSM_L, SSM_L), 0) >= lax.broadcasted_iota(jnp.int32, (SSM_L, SSM_L), 1)
        def group(gg):
            xs_v, b_v, c_v, dt_v, acs_v, acst_v, d_v, st_v, state_v = _ssd_group_views(
                gg, xs_ref, b_ref, c_ref, dt_ref, acs_ref, acst_ref, d_ref, st_ref, state)
            y_v = y_ref.at[:, gg * SSM_HP:(gg + 1) * SSM_HP]
            Bb, Cb = b_v[...].astype(BF16), c_v[...].astype(BF16)
            Gm = lax.dot_general(Cb, Bb, NT, preferred_element_type=F32)
            yield
            xs, X, e_all, decay_all, d_all, cd_rows, _, _, _ = _ssd_group_terms(xs_v, dt_v, acs_v, d_v)
            S_all = state_v[...]
            st_v[...] = S_all
            yield
            yo = lax.dot_general(Cb, S_all.astype(BF16), NT, preferred_element_type=F32)
            new_state = lax.dot_general((X * decay_all).astype(BF16), Bb, TN, preferred_element_type=F32)
            yield
            y_v[...] = e_all * yo + d_all * xs
            state_v[...] = S_all * cd_rows + new_state

            def head(r):
                sl = slice(r * SSM_P, (r + 1) * SSM_P)
                M = Gm * _ssd_lmat(acs_v, acst_v, r, tril)
                yield
                y_v[:, sl] += jnp.dot(M.astype(BF16), X[:, sl].astype(BF16), preferred_element_type=F32)

            yield from _rounds([head(r) for r in range(SSM_R)])

        _interleave([group(gg) for gg in range(SSM_GPS)])

    return _call(body, name="ssd_fwd",
                 out_shape=(_sds((S, SSM_G * SSM_HP)), _sds((nc, SSM_G, SSM_HP, SSM_N))),
                 grid=(SSM_G // SSM_GPS, nc), in_specs=[xs_s, b_s, c_s, sc, sc, sct, gsp],
                 out_specs=(xs_s, pl.BlockSpec((None, SSM_GPS, SSM_HP, SSM_N), lambda g, c: (c, g, 0, 0))),
                 scratch=[pltpu.VMEM((SSM_GPS, SSM_HP, SSM_N), F32)])(xbc, xbc, xbc, dt_g, acs_g, acst_g, d_g)


def _ssd_bwd(xbc, dt_g, acs_g, acst_g, d_g, states, dy):
    S = xbc.shape[0]
    nc = S // SSM_L
    xs_s, b_s, c_s, sc, sct, gsp, st = _ssd_specs(True, nc)
    bc_out = pl.BlockSpec((SSM_L, SSM_GPS * SSM_N), lambda g, c: (nc - 1 - c, g))

    def body(xs_ref, b_ref, c_ref, dt_ref, acs_ref, acst_ref, d_ref, st_ref, dy_ref,
             dxs_ref, db_ref, dc_ref, ddt_ref, dacs_ref, dacst_ref, dd_ref, dstate):
        c = pl.program_id(1)

        @pl.when(c == 0)
        def _():
            dstate[...] = jnp.zeros_like(dstate)
            dd_ref[...] = jnp.zeros_like(dd_ref)

        tril = lax.broadcasted_iota(jnp.int32, (SSM_L, SSM_L), 0) >= lax.broadcasted_iota(jnp.int32, (SSM_L, SSM_L), 1)
        lane = lax.broadcasted_iota(jnp.int32, (1, LANES), 1)
        subl = lax.broadcasted_iota(jnp.int32, (LANES, 1), 0)
        last_row = lax.broadcasted_iota(jnp.int32, (SSM_L, 1), 0) == SSM_L - 1
        triu = lax.broadcasted_iota(jnp.int32, (SSM_L, SSM_L), 0) <= lax.broadcasted_iota(jnp.int32, (SSM_L, SSM_L), 1)
        def group(gg):
            xs_v, b_v, c_v, dt_v, acs_v, acst_v, d_v, st_v, ddt_v, dacs_v, dacst_v, dd_v, dstate_v = _ssd_group_views(
                gg, xs_ref, b_ref, c_ref, dt_ref, acs_ref, acst_ref, d_ref, st_ref, ddt_ref, dacs_ref, dacst_ref, dd_ref, dstate)
            dy_v, dxs_v = (r.at[:, gg * SSM_HP:(gg + 1) * SSM_HP] for r in (dy_ref, dxs_ref))
            db_v, dc_v = (r.at[:, gg * SSM_N:(gg + 1) * SSM_N] for r in (db_ref, dc_ref))
            Bb, Cb = b_v[...].astype(BF16), c_v[...].astype(BF16)
            Gm = lax.dot_general(Cb, Bb, NT, preferred_element_type=F32)
            GmT = lax.dot_general(Bb, Cb, NT, preferred_element_type=F32)
            yield
            xs, X, e_all, decay_all, d_all, cd_rows, dt_c, decay_c, cd = _ssd_group_terms(xs_v, dt_v, acs_v, d_v)
            S_all, dSn_all, dY = st_v[...], dstate_v[...], dy_v[...]
            Sb, dSnb = S_all.astype(BF16), dSn_all.astype(BF16)
            yield
            T = lax.dot_general(Cb, Sb, NT, preferred_element_type=F32)
            dT = (dY * e_all).astype(BF16)
            dC = jnp.dot(dT, Sb, preferred_element_type=F32)
            dS_prev = lax.dot_general(dT, Cb, TN, preferred_element_type=F32)
            yield
            yo_dy = dY * (e_all * T)
            W = lax.dot_general(Bb, dSnb, NT, preferred_element_type=F32)
            dB = jnp.dot((X * decay_all).astype(BF16), dSnb, preferred_element_type=F32)
            yield
            xw = X * W
            dcd_rows = jnp.sum(dSn_all * S_all, axis=1, keepdims=True)
            dstate_v[...] = dS_prev + dSn_all * cd_rows
            dX_state = W * decay_all
            yield
            acc = dict(dG=jnp.zeros((SSM_L, SSM_L), F32), dGT=jnp.zeros((SSM_L, SSM_L), F32),
                       ddt=jnp.zeros((SSM_L, LANES), F32), dacs=jnp.zeros((SSM_L, LANES), F32),
                       dacst=jnp.zeros((LANES, SSM_L), F32), dd=jnp.zeros((1, LANES), F32))

            def head(r):
                sl = slice(r * SSM_P, (r + 1) * SSM_P)
                Lm = _ssd_lmat(acs_v, acst_v, r, tril)
                LmT = jnp.exp(jnp.where(triu, acst_v[r:r + 1, :] - acs_v[:, r:r + 1], -jnp.inf))
                M = Gm * Lm
                yield
                dYh, xs_h = dY[:, sl], xs[:, sl]
                dYb, Xb = dYh.astype(BF16), X[:, sl].astype(BF16)
                dM = lax.dot_general(dYb, Xb, NT, preferred_element_type=F32)
                yield
                dX = jnp.dot((GmT * LmT).astype(BF16), dYb, preferred_element_type=F32) + dX_state[:, sl]
                acc["dG"] = acc["dG"] + dM * Lm
                acc["dGT"] = acc["dGT"] + lax.dot_general(Xb, dYb, NT, preferred_element_type=F32) * LmT
                yield
                dseg = dM * M
                dd = jnp.sum(xw[:, sl], axis=1, keepdims=True) * decay_c[r]
                dcd = jnp.sum(dcd_rows[sl])
                dacs_col = (jnp.sum(dseg, axis=1, keepdims=True) + jnp.sum(yo_dy[:, sl], axis=1, keepdims=True) - dd
                            + jnp.where(last_row, dcd * cd[r] + jnp.sum(dd), 0.0))
                dacs_row = -jnp.sum(dseg, axis=0, keepdims=True)
                yield
                dxs_v[:, sl] = dX * dt_c[r] + d_all[:, sl] * dYh
                acc["ddt"] = acc["ddt"] + jnp.where(lane == r, jnp.sum(dX * xs_h, axis=1, keepdims=True), 0.0)
                acc["dacs"] = acc["dacs"] + jnp.where(lane == r, dacs_col, 0.0)
                acc["dacst"] = acc["dacst"] + jnp.where(subl == r, dacs_row, 0.0)
                acc["dd"] = acc["dd"] + jnp.where(lane == r, jnp.sum(dYh * xs_h), 0.0)

            yield from _rounds([head(r) for r in range(SSM_R)])
            dc_v[...] = dC + jnp.dot(acc["dG"].astype(BF16), Bb, preferred_element_type=F32)
            db_v[...] = dB + jnp.dot(acc["dGT"].astype(BF16), Cb, preferred_element_type=F32)
            ddt_v[...] = acc["ddt"]
            dacs_v[...] = acc["dacs"]
            dacst_v[...] = acc["dacst"]
            dd_v[...] += acc["dd"]

        _interleave([group(gg) for gg in range(SSM_GPS)])

    big = _sds((SSM_G, S, LANES))
    return _call(body, name="ssd_bwd",
                 out_shape=(_sds((S, SSM_G * SSM_HP)), _sds((S, SSM_G * SSM_N)), _sds((S, SSM_G * SSM_N)),
                            big, big, _sds((SSM_G, LANES, S)), _sds((SSM_G, 1, LANES))),
                 grid=(SSM_G // SSM_GPS, nc), in_specs=[xs_s, b_s, c_s, sc, sc, sct, gsp, st, xs_s],
                 out_specs=(xs_s, bc_out, bc_out, sc, sc, sct, gsp),
                 scratch=[pltpu.VMEM((SSM_GPS, SSM_HP, SSM_N), F32)])(xbc, xbc, xbc, dt_g, acs_g, acst_g, d_g, states, dy)


def _gate_norm_fwd(y, proj, w):
    S, DI = y.shape
    tm = _tile(S, 256)

    def body(y_ref, z_ref, w_ref, o_ref):
        z = z_ref[...]
        gn = y_ref[...] * (z * _sigmoid(z))
        r = lax.rsqrt(jnp.mean(gn * gn, axis=-1, keepdims=True) + SSM_NORM_EPS)
        o_ref[...] = (gn * r * w_ref[...]).astype(BF16)

    row = pl.BlockSpec((tm, DI), lambda i: (i, 0))
    return _call(body, name="gate_norm_fwd", out_shape=_sds((S, DI), BF16), grid=(S // tm,),
                 in_specs=[row, row, pl.BlockSpec((1, DI), lambda i: (0, 0))], out_specs=row)(y, proj, w)


def _gate_norm_bwd(y, proj, w, dout):
    S, DI = y.shape
    tm = _tile(S, 256)

    def body(y_ref, z_ref, w_ref, d_ref, dy_ref, dz_ref, dw_ref):
        z, yv = z_ref[...], y_ref[...]
        sz = z * _sigmoid(z)
        dgn, dw = _norm_bwd_math(yv * sz, w_ref[...], d_ref[...].astype(F32), SSM_NORM_EPS)
        dy_ref[...] = dgn * sz
        dz_ref[...] = (dgn * yv * _silu_grad(z)).astype(BF16)

        @pl.when(pl.program_id(0) == 0)
        def _():
            dw_ref[...] = jnp.zeros_like(dw_ref)

        dw_ref[...] += dw

    row = pl.BlockSpec((tm, DI), lambda i: (i, 0))
    vec = pl.BlockSpec((1, DI), lambda i: (0, 0))
    return _call(body, name="gate_norm_bwd", out_shape=(_sds((S, DI)), _sds((S, DI), BF16), _sds((1, DI))), grid=(S // tm,),
                 in_specs=[row, row, vec, row], out_specs=(row, row, vec))(y, proj, w, dout)


def _group_major(v):
    return jnp.pad(v.reshape(SSM_G, 1, SSM_R), ((0, 0), (0, 0), (0, LANES - SSM_R)))


def _ungroup(t):
    return t[:, :SSM_R].reshape(1, SSM_G * SSM_R)


def _ffn_fwd(x, P, l, need):
    h = _rmsnorm(x, P["norm_ffn"][l:l + 1], name=f"ffn{l}_norm")
    need(f"ffn{l}_up", h)
    a = _mm(h, P[f"ffn_w_upT{l}"], tb=True, name=f"ffn{l}_up")
    need(f"ffn{l}_down", a)
    act = _ffn_mid_fwd(a, P["ffn_conv_w"], P["ffn_conv_b"], l)
    out = _mm(act, P[f"ffn_w_down{l}"], res=x, name=f"ffn{l}_down")
    return out, (x, h, a, act)


def _ffn_bwd(saved, P, l, dx, emit):
    x, h, a, act = saved
    wT = P[f"ffn_w_upT{l}"]
    F = wT.shape[0] // 2
    dact = _mm(dx, P[f"ffn_w_down{l}"], tb=True, out_dtype=BF16, name=f"ffn{l}_down_dx")
    dw_down = _mm(act, dx, ta=True, out_dtype=PAYLOAD, name=f"ffn{l}_down_dw")
    dau, dag, dcw, dcb = _ffn_mid_bwd(a, P["ffn_conv_w"], P["ffn_conv_b"], l, dact)
    dw_upT = _mm(dau, h, ta=True, out_dtype=PAYLOAD, out_rows=(2 * F, 0, None), name=f"ffn{l}_up_u_dw")
    dw_upT = _mm(dag, h, ta=True, out_dtype=PAYLOAD, out_rows=(2 * F, F, dw_upT), name=f"ffn{l}_up_g_dw")
    tie = emit(f"ffn{l}", {"ffn_w_upT": dw_upT, "ffn_w_down": dw_down})
    dh = _mm(dau, wT, b_rows=(0, F), name=f"ffn{l}_up_u_dx")
    dx_in, dnw = _mm_norm_bwd(dag, wT, x, P["norm_ffn"][l:l + 1], dx, res=dh, b_rows=(F, F), after=tie,
                              name=f"ffn{l}_up_g_dx_norm_bwd")
    return dx_in, dnw, dcw, dcb


def _local_step(x, positions, target, P, need, emit, after=None):
    S, D = x.shape
    inv_freq = ROPE_THETA ** (-jnp.arange(0, HEAD_DIM, 2, dtype=F32) / HEAD_DIM)
    inv_freq = jnp.tile(inv_freq, LANES // (HEAD_DIM // 2)).reshape(1, LANES)
    cos, sin = _rope_tables(positions, inv_freq)

    nm0 = P["norm_mix"][0:1]
    h0 = _rmsnorm(x, nm0, name="mix_norm", after=after)
    need("mix_in", h0)
    proj0 = _mm(h0, P["mix_w_inT"], tb=True, name="mix_in")
    cat0 = _attn_fwd(proj0, cos, sin, P["attn_sinks"], _pool_fwd(proj0, P["pool_w"][0], P["pool_scale"]))
    need("mix_out", cat0)
    x1 = _mm(cat0, P["mix_w_out"], res=x, name="mix_out")
    x2, ffn0 = _ffn_fwd(x1, P, 0, need)

    nm1 = P["norm_mix"][1:2]
    h1 = _rmsnorm(x2, nm1, name="ssm_norm_in")
    need("ssm", h1)
    w1T, wdtT = P["ssm_w_inT"], P["ssm_wdtT"]
    DI, CD, NH = P["ssm_norm"].shape[1], P["ssm_conv_w"].shape[1], P["ssm_dt_bias"].shape[1]
    z = _mm(h1, w1T, tb=True, b_rows=(0, DI), name="ssm_in_z")
    xbcp = _mm(h1, w1T, tb=True, b_rows=(DI, CD), name="ssm_in_xbc")
    dtraw = _mm(h1, wdtT, tb=True, name="ssm_in_dt")
    xbc = _conv_silu_fwd(xbcp, P["ssm_conv_w"], P["ssm_conv_b"])
    bias_row = jnp.pad(P["ssm_dt_bias"], ((0, 0), (0, LANES - NH)))
    alog_row = jnp.pad(P["ssm_A_log"], ((0, 0), (0, LANES - NH)))
    d_g = _group_major(P["ssm_D"])
    pre_h, dt_h, dt_g, acs_g, acst_g = _ssd_prep_fwd(dtraw, bias_row, alog_row)
    y, states = _ssd_fwd(xbc, dt_g, acs_g, acst_g, d_g)
    yn = _gate_norm_fwd(y, z, P["ssm_norm"])
    need("ssm_out", yn)
    x3 = _mm(yn, P["ssm_w_out"], res=x2, name="ssm_out")
    x4, ffn1 = _ffn_fwd(x3, P, 1, need)

    loss, dx, d_norm_final = _final_loss(x4, P["norm_final"].reshape(1, D), target, name="final_loss")
    dx, dnf1, dcw1, dcb1 = _ffn_bwd(ffn1, P, 1, dx, emit)
    dyn = _mm(dx, P["ssm_w_out"], tb=True, out_dtype=BF16, name="ssm_out_dx")
    d_w_out1 = _mm(yn, dx, ta=True, out_dtype=PAYLOAD, name="ssm_out_dw")
    dy, dz, d_ssm_norm = _gate_norm_bwd(y, z, P["ssm_norm"], dyn)
    dxs, dB, dC, ddt_g, dacs_g, dacst_g, dd_g = _ssd_bwd(xbc, dt_g, acs_g, acst_g, d_g, states, dy)
    draw, dbias_row, dalog_row = _ssd_prep_bwd(pre_h, dt_h, alog_row, ddt_g, dacs_g, dacst_g)
    dxbc, d_conv_w1, d_conv_b1 = _conv_silu_bwd(xbcp, P["ssm_conv_w"], P["ssm_conv_b"], [dxs, dB, dC])
    rows = DI + CD + NH
    d_w1T = _mm(dz, h1, ta=True, out_dtype=PAYLOAD, out_rows=(rows, 0, None), name="ssm_in_z_dw")
    d_w1T = _mm(dxbc, h1, ta=True, out_dtype=PAYLOAD, out_rows=(rows, DI, d_w1T), name="ssm_in_xbc_dw")
    d_w1T = _mm(draw[:, :NH], h1, ta=True, out_dtype=PAYLOAD, out_rows=(rows, DI + CD, d_w1T), name="ssm_in_dt_dw")
    tie = emit("ssm", {"ssm_w_inT": d_w1T, "ssm_w_out": d_w_out1,
                       "ssm_conv_w": d_conv_w1, "ssm_conv_b": d_conv_b1, "ssm_norm": d_ssm_norm})
    dh1 = _mm(dz, w1T, b_rows=(0, DI), name="ssm_in_z_dx")
    dh1 = _mm(dxbc, w1T, b_rows=(DI, CD), res=dh1, name="ssm_in_xbc_dx")
    dx, dnm1 = _mm_norm_bwd(draw, wdtT, x2, nm1, dx, res=dh1, after=tie, name="ssm_in_dt_dx_norm_bwd")
    dx, dnf0, dcw0, dcb0 = _ffn_bwd(ffn0, P, 0, dx, emit)
    dcat = _mm(dx, P["mix_w_out"], tb=True, name="mix_out_dx")
    d_w_out0 = _mm(cat0, dx, ta=True, out_dtype=PAYLOAD, name="mix_out_dw")
    dproj0, dsk = _attn_bwd(proj0, cos, sin, P["attn_sinks"], dcat)
    dproj0, d_pool_w, d_pool_scale = _pool_bwd(proj0, P["pool_w"][0], P["pool_scale"], dcat, dproj0)
    d_w_in0 = _mm(dproj0, h0, ta=True, out_dtype=PAYLOAD, name="mix_in_dw")
    tie = emit("mix", {"mix_w_inT": d_w_in0, "mix_w_out": d_w_out0, "ffn_conv_w": jnp.stack([dcw0, dcw1])})
    grad_x, dnm0 = _mm_norm_bwd(dproj0, P["mix_w_inT"], x, nm0, dx, after=tie, name="mix_in_dx_norm_bwd")

    small = {
        "norm_mix": jnp.concatenate([dnm0, dnm1], axis=0),
        "norm_ffn": jnp.concatenate([dnf0, dnf1], axis=0),
        "norm_final": d_norm_final,
        "pool_w": d_pool_w,
        "pool_scale": d_pool_scale,
        "attn_sinks_rows": dsk,
        "ssm_dt_bias_row": dbias_row, "ssm_A_log_row": dalog_row, "ssm_D_g": dd_g,
        "ffn_conv_b": jnp.concatenate([dcb0, dcb1], axis=0),
    }
    return loss, grad_x, small


def _peer(k):
    x, y, c = lax.axis_index("x"), lax.axis_index("y"), lax.axis_index("c")
    px = 1 - x if k & 4 else x
    py = 1 - y if k & 2 else y
    pc = 1 - c if k & 1 else c
    return (px, py, pc), 4 * px + 2 * py + pc


def _my_index():
    return 4 * lax.axis_index("x") + 2 * lax.axis_index("y") + lax.axis_index("c")


def _land_sds(a, mode, gather):
    if mode == "slab":
        return _sds(((N_DEV,) + a.shape) if gather else a.shape, a.dtype)
    assert mode == "rows", mode
    return _sds((N_DEV * a.shape[0],) + a.shape[1:] if gather else (N_DEV, a.shape[0] // N_DEV) + a.shape[1:], a.dtype)


def _part(ref, mode, shape, idx):
    if mode == "slab":
        return ref.at[idx]
    r = shape[0] // N_DEV
    return ref.at[pl.ds(idx * r, r)]


def _remote_copies(ops, gather, srcs, lands, send_sems, recv_sems):
    me = _my_index()
    n = len(ops)
    out = []
    for k in range(1, N_DEV):
        dev, idx = _peer(k)
        for i, (a, mode) in enumerate(ops):
            s = srcs[i] if gather else _part(srcs[i], mode, a.shape, idx)
            d = _part(lands[i], mode, _land_sds(a, mode, gather).shape, me) if gather else lands[i].at[me]
            out.append(pltpu.make_async_remote_copy(src_ref=s, dst_ref=d, send_sem=send_sems.at[(k - 1) * n + i],
                                                    recv_sem=recv_sems.at[(k - 1) * n + i], device_id=dev,
                                                    device_id_type=pl.DeviceIdType.MESH))
    return out


HBM = pl.BlockSpec(memory_space=pltpu.HBM)
SEM = pl.BlockSpec(memory_space=pltpu.SEMAPHORE)
SIDE_EFFECT = pltpu.SideEffectType.DATAFLOW_SIDE_EFFECTING


def _in_hbm(a):
    return pltpu.with_memory_space_constraint(a, pltpu.HBM)


def _place_own(ops, *, gather, name):
    n = len(ops)

    def zeros(k):
        return (0,) * k

    in_specs, out_specs = [], []
    for a, mode in ops:
        nd = a.ndim
        if gather and mode == "slab":
            in_specs.append(pl.BlockSpec(a.shape, lambda i, nd=nd: zeros(nd)))
            out_specs.append(pl.BlockSpec((1,) + a.shape, lambda i, nd=nd: (_my_index(),) + zeros(nd)))
        elif gather:
            in_specs.append(pl.BlockSpec(a.shape, lambda i, nd=nd: zeros(nd)))
            out_specs.append(pl.BlockSpec(a.shape, lambda i, nd=nd: (_my_index(),) + zeros(nd - 1)))
        elif mode == "slab":
            in_specs.append(pl.BlockSpec((1,) + a.shape[1:], lambda i, nd=nd: (_my_index(),) + zeros(nd - 1)))
            out_specs.append(pl.BlockSpec((1,) + a.shape[1:], lambda i, nd=nd: (_my_index(),) + zeros(nd - 1)))
        else:
            r = a.shape[0] // N_DEV
            in_specs.append(pl.BlockSpec((r,) + a.shape[1:], lambda i, nd=nd: (_my_index(),) + zeros(nd - 1)))
            out_specs.append(pl.BlockSpec((1, r) + a.shape[1:], lambda i, nd=nd: (_my_index(),) + zeros(nd)))

    def body(*refs):
        for i_ref, o_ref in zip(refs[:n], refs[n:2 * n]):
            if o_ref.shape == i_ref.shape:
                o_ref[...] = i_ref[...]
            else:
                o_ref[0] = i_ref[...]

    outs = _call(body, name=name, grid=(1,), in_specs=in_specs, out_specs=out_specs + [ANY] * n,
                 out_shape=[_land_sds(a, m, gather) for a, m in ops] + [_sds(a.shape, a.dtype) for a, _ in ops],
                 aliases={i: n + i for i in range(n)})(*[a for a, _ in ops])
    return outs[:n], [(src, m) for src, (_, m) in zip(outs[n:], ops)]


def _exchange_start(groups, *, gather, name):
    sizes = [len(ops) for ops, _ in groups]
    n = sum(sizes)
    G = len(groups)

    def body(*refs):
        srcs, lands = refs[:n], refs[n:2 * n]
        sems = refs[2 * n:2 * n + 2 * G]
        token = refs[-1]
        off = 0
        for g, (ops, _) in enumerate(groups):
            for cp in _remote_copies(ops, gather, srcs[off:off + sizes[g]], lands[off:off + sizes[g]], sems[2 * g], sems[2 * g + 1]):
                cp.start()
            off += sizes[g]
        token[...] = jnp.zeros_like(token)

    srcs = [a for ops, _ in groups for a, _ in ops]
    lands = [l for _, ls in groups for l in ls]
    sem_shapes = [pltpu.SemaphoreType.DMA((s * (N_DEV - 1),)) for s in sizes for _ in range(2)]
    outs = pl.pallas_call(
        body, name=name,
        out_shape=sem_shapes + [pltpu.HBM(a.shape, a.dtype) for a in srcs + lands] + [_sds((8, LANES))],
        in_specs=[HBM] * (2 * n), out_specs=[SEM] * (2 * G) + [HBM] * (2 * n) + [pl.BlockSpec(memory_space=pltpu.VMEM)],
        input_output_aliases={i: 2 * G + i for i in range(2 * n)},
        compiler_params=pltpu.CompilerParams(has_side_effects=SIDE_EFFECT))(*[_in_hbm(a) for a in srcs + lands])
    sems, thru, token = outs[:2 * G], outs[2 * G:2 * G + 2 * n], outs[-1]
    states, off = [], 0
    for g, s in enumerate(sizes):
        states.append((sems[2 * g], sems[2 * g + 1], thru[off:off + s], thru[n + off:n + off + s]))
        off += s
    return states, token


def _exchange_wait(ops, state, after, *, gather, name):
    send_sems, recv_sems, srcs, lands = state
    n = len(ops)

    def body(*refs):
        for cp in _remote_copies(ops, gather, refs[:n], refs[n:2 * n], refs[2 * n], refs[2 * n + 1]):
            cp.wait_send()
            cp.wait_recv()

    outs = pl.pallas_call(
        body, name=name, out_shape=[pltpu.HBM(a.shape, a.dtype) for a in list(srcs) + list(lands)],
        in_specs=[HBM] * (2 * n) + [SEM, SEM, ANY], out_specs=[HBM] * (2 * n),
        input_output_aliases={i: i for i in range(2 * n)},
        compiler_params=pltpu.CompilerParams(has_side_effects=SIDE_EFFECT))(*srcs, *lands, send_sems, recv_sems, after)
    return outs[n:]


ADAM_ROWS = 256


def _row_tile(R, cap=ADAM_ROWS):
    best = R
    if R > cap:
        for d in range(16, cap + 1, 16):
            if R % d == 0:
                best = d
    return best


def _adamw(g_layers, w, m, v, *, name):
    L = len(g_layers)
    J, R, Wd = g_layers[0].shape
    assert w.shape == (L, R, Wd), (g_layers[0].shape, w.shape)
    tr = _row_tile(R)
    nrt = R // tr
    c1 = 1.0 / (1.0 - ADAM_B1 ** ADAM_STEP)
    c2 = 1.0 / (1.0 - ADAM_B2 ** ADAM_STEP)

    def body(*refs):
        g_refs = refs[:L]
        w_ref, m_ref, v_ref, go_ref, d_ref, mo_ref, vo_ref = refs[L:]
        layer = pl.program_id(0)
        g = None
        for l, g_ref in enumerate(g_refs):
            gl = g_ref[0].astype(F32)
            for j in range(1, J):
                gl = gl + g_ref[j].astype(F32)
            g = gl if g is None else jnp.where(layer == l, gl, g)
        mn = ADAM_B1 * m_ref[...] + (1.0 - ADAM_B1) * g
        vn = ADAM_B2 * v_ref[...] + (1.0 - ADAM_B2) * (g * g)
        go_ref[...] = g
        mo_ref[...] = mn
        vo_ref[...] = vn
        d_ref[...] = -ADAM_LR * ((mn * c1) / (jnp.sqrt(vn * c2) + ADAM_EPS) + ADAM_WD * w_ref[...])

    def g_spec(l):
        return pl.BlockSpec((J, tr, Wd), lambda ll, i: (0, jnp.where(ll == l, i, jnp.where(ll < l, 0, nrt - 1)), 0))

    row = pl.BlockSpec((None, tr, Wd), lambda ll, i: (ll, i, 0))
    out = _sds((L, R, Wd))
    return _call(body, name=name, out_shape=(out, out, out, out), grid=(L, nrt),
                 in_specs=[g_spec(l) for l in range(L)] + [row, row, row], out_specs=(row, row, row, row))(*g_layers, w, m, v)


def _sum_slabs(slabs, *, name):
    n = len(slabs)

    def body(*refs):
        for g_ref, o_ref in zip(refs[:n], refs[n:]):
            g = g_ref[0]
            for j in range(1, g_ref.shape[0]):
                g = g + g_ref[j]
            o_ref[...] = g

    return _call(body, name=name, out_shape=[_sds(s.shape[1:]) for s in slabs])(*slabs)


def kernel(x, positions, norm_mix, norm_ffn, norm_final, mix_w_in, pool_w, pool_scale, attn_sinks, mix_w_out, ssm_w_in, ssm_conv_w, ssm_conv_b, ssm_dt_bias, ssm_A_log, ssm_D, ssm_norm, ssm_w_out, ffn_w_up, ffn_conv_w, ffn_conv_b, ffn_w_down, loss_target, m_norm_mix, m_norm_ffn, m_norm_final, m_mix_w_in, m_pool_w, m_pool_scale, m_attn_sinks, m_mix_w_out, m_ssm_w_in, m_ssm_conv_w, m_ssm_conv_b, m_ssm_dt_bias, m_ssm_A_log, m_ssm_D, m_ssm_norm, m_ssm_w_out, m_ffn_w_up, m_ffn_conv_w, m_ffn_conv_b, m_ffn_w_down, v_norm_mix, v_norm_ffn, v_norm_final, v_mix_w_in, v_pool_w, v_pool_scale, v_attn_sinks, v_mix_w_out, v_ssm_w_in, v_ssm_conv_w, v_ssm_conv_b, v_ssm_dt_bias, v_ssm_A_log, v_ssm_D, v_ssm_norm, v_ssm_w_out, v_ffn_w_up, v_ffn_conv_w, v_ffn_conv_b, v_ffn_w_down):
    args = dict(locals())
    wl = {n: args[n] for n in WEIGHTS}
    ml = {n: args["m_" + n] for n in WEIGHTS}
    vl = {n: args["v_" + n] for n in WEIGHTS}
    F = ffn_w_down.shape[1] * N_DEV
    DI, CD, NH = ssm_norm.shape[1] * N_DEV, ssm_conv_b.shape[1] * N_DEV, ssm_dt_bias.shape[1]
    Kc, Kf = ssm_conv_w.shape[1], ffn_conv_w.shape[1]
    n_up = ffn_w_up.shape[2]
    col_sharded = ("mix_w_in", "ssm_w_in", "ffn_w_up")

    def tr(a):
        return jnp.swapaxes(a, -1, -2)

    def two(a):
        return a.reshape(-1, a.shape[-1])

    def pay(a):
        return a.astype(PAYLOAD)

    order = ("mix_in", "mix_out", "ffn0_up", "ffn0_down", "ssm", "ssm_out", "ffn1_up", "ffn1_down")
    gops = {
        "mix_in": [(pay(tr(mix_w_in)[0]), "rows")],
        "mix_out": [(pay(mix_w_out[0]), "rows"), (two(ssm_conv_w), "slab"), (ssm_conv_b, "slab"), (ssm_norm, "slab"),
                    (two(ffn_conv_w), "slab")],
        "ffn0_up": [(pay(tr(ffn_w_up)[0]), "rows")], "ffn0_down": [(pay(ffn_w_down[0]), "rows")],
        "ssm": [(pay(tr(ssm_w_in)[0]), "slab")], "ssm_out": [(pay(ssm_w_out[0]), "rows")],
        "ffn1_up": [(pay(tr(ffn_w_up)[1]), "rows")], "ffn1_down": [(pay(ffn_w_down[1]), "rows")],
    }
    lands, handed = _place_own([op for g in order for op in gops[g]], gather=True, name="gather_own")
    groups, off = [], 0
    for g in order:
        gops[g] = handed[off:off + len(gops[g])]
        groups.append((gops[g], lands[off:off + len(gops[g])]))
        off += len(gops[g])
    gstates, token = _exchange_start(groups, gather=True, name="gather_start")
    gstate = dict(zip(order, gstates))
    P = {n: wl[n] for n in REPLICATED}

    def need(g, after):
        got = _exchange_wait(gops[g], gstate[g], after, gather=True, name="gather_wait_" + g)
        if g == "mix_in":
            P["mix_w_inT"] = got[0]
        elif g == "mix_out":
            P.update(mix_w_out=got[0], ssm_conv_w=got[1].transpose(1, 0, 2).reshape(Kc, CD), ssm_conv_b=got[2].reshape(1, CD),
                     ssm_norm=got[3].reshape(1, DI), ffn_conv_w=got[4].transpose(1, 0, 2).reshape(2, Kf, 2 * F))
        elif g == "ssm":
            w1T = got[0].reshape(-1, got[0].shape[-1])
            P.update(ssm_w_inT=w1T, ssm_wdtT=jnp.pad(w1T[DI + CD:], ((0, LANES - NH), (0, 0))))
        elif g == "ssm_out":
            P["ssm_w_out"] = got[0]
        elif g.endswith("_up"):
            P["ffn_w_upT" + g[3]] = got[0]
        else:
            P["ffn_w_down" + g[3]] = got[0]

    sent = {}

    def emit(g, d):
        if g == "mix":
            ops = [(d["mix_w_inT"], "rows"), (d["mix_w_out"], "rows"),
                   (d["ffn_conv_w"].reshape(2 * Kf, N_DEV, n_up).transpose(1, 0, 2), "slab")]
        elif g == "ssm":
            ops = [(d["ssm_w_inT"].reshape(N_DEV, -1, d["ssm_w_inT"].shape[-1]), "slab"), (d["ssm_w_out"], "rows"),
                   (d["ssm_conv_w"].reshape(Kc, N_DEV, -1).transpose(1, 0, 2), "slab"),
                   (d["ssm_conv_b"].reshape(N_DEV, 1, -1), "slab"), (d["ssm_norm"].reshape(N_DEV, 1, -1), "slab")]
        else:
            ops = [(d["ffn_w_upT"], "rows"), (d["ffn_w_down"], "rows")]
        own, ops = _place_own(ops, gather=False, name="scatter_own_" + g)
        (state,), tok = _exchange_start([(ops, own)], gather=False, name="scatter_start_" + g)
        sent[g] = (ops, state)
        return tok

    loss_lanes, grad_x, G = _local_step(x[0], positions.reshape(-1, 1), loss_target[0], P, need, emit, after=token)

    res = {}

    def update(n, g_layers):
        L = len(g_layers)
        g_layers = [g.reshape(g.shape[0], -1, g.shape[-1]) for g in g_layers]
        shape = (L,) + g_layers[0].shape[1:]
        view = tr if n in col_sharded else (lambda a: a)
        outs = _adamw(g_layers, view(wl[n]).reshape(shape), view(ml[n]).reshape(shape), view(vl[n]).reshape(shape),
                      name="adamw_" + n)
        for kind, a in zip(("grad", "delta", "new_m", "new_v"), outs):
            res[kind, n] = view(a.reshape(view(wl[n]).shape))

    rep_ops = [(a, "slab") for a in (G["norm_mix"], G["norm_ffn"], G["norm_final"], G["pool_w"].reshape(-1, LANES),
                                     G["pool_scale"], G["ffn_conv_b"], G["attn_sinks_rows"],
                                     G["ssm_dt_bias_row"], G["ssm_A_log_row"], G["ssm_D_g"].reshape(SSM_G, LANES),
                                     loss_lanes)]
    rep_own, rep_ops = _place_own(rep_ops, gather=True, name="small_own")
    (rep_state,), rep_token = _exchange_start([(rep_ops, rep_own)], gather=True, name="small_start")

    recv = {g: _exchange_wait(sent[g][0], sent[g][1], rep_token, gather=False, name="scatter_wait_" + g)
            for g in ("ffn1", "ssm", "ffn0")}
    update("ssm_w_in", [recv["ssm"][0]])
    update("ssm_w_out", [recv["ssm"][1]])
    update("ssm_conv_w", [recv["ssm"][2]])
    update("ssm_conv_b", [recv["ssm"][3]])
    update("ssm_norm", [recv["ssm"][4]])
    update("ffn_w_up", [recv["ffn0"][0], recv["ffn1"][0]])
    update("ffn_w_down", [recv["ffn0"][1], recv["ffn1"][1]])
    recv["mix"] = _exchange_wait(sent["mix"][0], sent["mix"][1], res["new_v", "ffn_w_down"], gather=False,
                                 name="scatter_wait_mix")
    update("mix_w_in", [recv["mix"][0]])
    update("mix_w_out", [recv["mix"][1]])
    update("ffn_conv_w", [recv["mix"][2]])

    rep = _exchange_wait(rep_ops, rep_state, res["new_v", "mix_w_out"], gather=True, name="small_wait")
    for n, r in zip(("norm_mix", "norm_ffn", "norm_final", "pool_w", "pool_scale", "ffn_conv_b"), rep):
        update(n, [r])
    sinks_rows, bias_row, alog_row, d_g, loss_sum = _sum_slabs(rep[6:], name="sum_head_grads")
    update("attn_sinks", [sinks_rows[:, 0].reshape(1, 1, N_HEADS)])
    update("ssm_dt_bias", [bias_row[:, :NH][None]])
    update("ssm_A_log", [alog_row[:, :NH][None]])
    update("ssm_D", [_ungroup(d_g)[None]])
    loss = loss_sum[0, 0]

    return (loss, grad_x[None], *[res[k, n] for k in ("grad", "delta", "new_m", "new_v") for n in WEIGHTS])
```

```python
import functools
import math

import jax
import jax.numpy as jnp
from jax import lax
from jax.experimental import pallas as pl
from jax.experimental.pallas import tpu as pltpu

F32 = jnp.float32
BF16 = jnp.bfloat16

N_DEV = 8
LANES = 128
HEAD_DIM = 64
N_KV_HEADS = 2
GQ = 4
N_HEADS = N_KV_HEADS * GQ
BLOCK = 128
POOL_GROUPS = 4
ROPE_THETA = 10000.0
SSM_P = 64
SSM_G = 8
SSM_R = 4
SSM_N = 128
SSM_L = 128
NORM_EPS = 1e-6
SSM_NORM_EPS = 1e-5
ADAM_LR, ADAM_B1, ADAM_B2, ADAM_EPS, ADAM_WD, ADAM_STEP = 0.001, 0.9, 0.999, 1e-08, 0.01, 10
VMEM_LIMIT = 56 * 2 ** 20
PAYLOAD = jnp.bfloat16

REPLICATED = ("norm_mix", "norm_ffn", "norm_final", "pool_w", "pool_scale", "attn_sinks",
              "ssm_dt_bias", "ssm_A_log", "ssm_D", "ffn_conv_b")
WEIGHTS = ("norm_mix", "norm_ffn", "norm_final", "mix_w_in", "pool_w", "pool_scale", "attn_sinks", "mix_w_out",
           "ssm_w_in", "ssm_conv_w", "ssm_conv_b", "ssm_dt_bias", "ssm_A_log", "ssm_D", "ssm_norm", "ssm_w_out",
           "ffn_w_up", "ffn_conv_w", "ffn_conv_b", "ffn_w_down")


def _tile(n, cap):
    if n <= cap:
        return n
    best = None
    for d in range(LANES, cap + 1, LANES):
        if n % d == 0:
            best = d
    assert best is not None, (n, cap)
    return best


def _call(body, *, name, out_shape, grid=None, in_specs=None, out_specs=None, scratch=(), aliases=None):
    kw = {}
    if grid is not None:
        kw = dict(grid=grid, in_specs=in_specs, out_specs=out_specs)
    if aliases:
        kw["input_output_aliases"] = aliases
    return pl.pallas_call(
        body, name=name, out_shape=out_shape, scratch_shapes=list(scratch),
        compiler_params=pltpu.CompilerParams(vmem_limit_bytes=VMEM_LIMIT), **kw)


ANY = pl.BlockSpec(memory_space=pl.ANY)


def _sds(shape, dtype=F32):
    return jax.ShapeDtypeStruct(tuple(shape), dtype)


def _sigmoid(x):
    return 1.0 / (1.0 + jnp.exp(-x))


def _shift_dn(x, d, t):
    if d == 0:
        return x
    return jnp.where(t >= d, pltpu.roll(x, d, axis=0), 0.0)


def _shift_up(x, d, t):
    if d == 0:
        return x
    n = x.shape[0]
    return jnp.where(t < n - d, pltpu.roll(x, n - d, axis=0), 0.0)


def _mm(a, b, *, name, ta=False, tb=False, res=None, out_dtype=F32, b_rows=None, out_rows=None):
    M, K = (a.shape[1], a.shape[0]) if ta else a.shape
    b0, bn = b_rows if b_rows is not None else (0, b.shape[0])
    N = bn if tb else b.shape[1]
    assert (b.shape[1] if tb else bn) == K, (a.shape, b.shape, ta, tb, b_rows)
    tm, tn, tk = _tile(M, 1408), _tile(N, 1408), _tile(K, 1408)
    nk = K // tk
    dims = (((0 if ta else 1,), (1 if tb else 0,)), ((), ()))
    aliased = out_rows is not None and out_rows[2] is not None

    def body(*refs):
        a_ref, b_ref = refs[:2]
        r_ref = refs[2] if res is not None else None
        o_ref, acc = refs[-2:]
        k = pl.program_id(2)

        @pl.when(k == 0)
        def _():
            acc[...] = jnp.zeros_like(acc)

        acc[...] += lax.dot_general(a_ref[...].astype(BF16), b_ref[...].astype(BF16), dims,
                                    preferred_element_type=F32)

        @pl.when(k == nk - 1)
        def _():
            out = acc[...]
            if res is not None:
                out = out + r_ref[...]
            o_ref[...] = out.astype(out_dtype)

    a_spec = pl.BlockSpec((tk, tm), lambda i, j, k: (k, i)) if ta else pl.BlockSpec((tm, tk), lambda i, j, k: (i, k))
    if tb:
        assert b0 % tn == 0, (b_rows, tn)
        b_spec = pl.BlockSpec((tn, tk), lambda i, j, k: (b0 // tn + j, k))
    else:
        assert b0 % tk == 0, (b_rows, tk)
        b_spec = pl.BlockSpec((tk, tn), lambda i, j, k: (b0 // tk + k, j))
    ins, specs = [a, b], [a_spec, b_spec]
    if res is not None:
        ins.append(res)
        specs.append(pl.BlockSpec((tm, tn), lambda i, j, k: (i, j)))
    aliases = None
    if out_rows is None:
        o_spec = pl.BlockSpec((tm, tn), lambda i, j, k: (i, j))
        out_shape = _sds((M, N), out_dtype)
    else:
        total, o0, prev = out_rows
        assert o0 % tm == 0, (out_rows, tm)
        o_spec = pl.BlockSpec((tm, tn), lambda i, j, k: (o0 // tm + i, j))
        out_shape = _sds((total, N), out_dtype)
        if aliased:
            aliases = {len(ins): 0}
            ins.append(prev)
            specs.append(ANY)
    return _call(body, name=name, out_shape=out_shape, grid=(M // tm, N // tn, nk), in_specs=specs,
                 out_specs=o_spec, scratch=[pltpu.VMEM((tm, tn), F32)], aliases=aliases)(*ins)


def _rmsnorm(x, w, *, name, eps=NORM_EPS, after=None):
    S, D = x.shape
    tm = _tile(S, 512)
    tie = [] if after is None else [after]

    def body(x_ref, w_ref, *rest):
        o_ref = rest[-1]
        xf = x_ref[...]
        r = lax.rsqrt(jnp.mean(xf * xf, axis=-1, keepdims=True) + eps)
        o_ref[...] = (xf * r * w_ref[...]).astype(BF16)

    return _call(body, name=name, out_shape=_sds((S, D), BF16), grid=(S // tm,),
                 in_specs=[pl.BlockSpec((tm, D), lambda i: (i, 0)), pl.BlockSpec((1, D), lambda i: (0, 0))] + [ANY] * len(tie),
                 out_specs=pl.BlockSpec((tm, D), lambda i: (i, 0)))(x, w, *tie)


def _norm_bwd_math(xf, w, dh, eps):
    r = lax.rsqrt(jnp.mean(xf * xf, axis=-1, keepdims=True) + eps)
    xhat = xf * r
    dxh = dh * w
    dx = r * (dxh - xhat * jnp.mean(dxh * xhat, axis=-1, keepdims=True))
    dw = jnp.sum(dh * xhat, axis=0, keepdims=True)
    return dx, dw


def _mm_norm_bwd(a, b, x, w, dres, *, name, res=None, b_rows=None, after=None, eps=NORM_EPS):
    M, K = a.shape
    b0, bn = b_rows if b_rows is not None else (0, b.shape[0])
    D = b.shape[1]
    assert bn == K and x.shape == (M, D), (a.shape, b.shape, b_rows, x.shape)
    tm, tk = _tile(M, 1024), _tile(K, 1408)
    assert b0 % tk == 0, (b_rows, tk)
    nk = K // tk

    def body(*refs):
        a_ref, b_ref, x_ref, w_ref, dr_ref = refs[:5]
        r_ref = refs[5] if res is not None else None
        dx_ref, dw_ref, acc = refs[-3:]
        i, k = pl.program_id(0), pl.program_id(1)

        @pl.when(k == 0)
        def _():
            acc[...] = jnp.zeros_like(acc)

        @pl.when((i == 0) & (k == 0))
        def _():
            dw_ref[...] = jnp.zeros_like(dw_ref)

        acc[...] += jnp.dot(a_ref[...].astype(BF16), b_ref[...].astype(BF16), preferred_element_type=F32)

        @pl.when(k == nk - 1)
        def _():
            dh = acc[...] if res is None else acc[...] + r_ref[...]
            dx, dw = _norm_bwd_math(x_ref[...], w_ref[...], dh, eps)
            dx_ref[...] = dr_ref[...] + dx
            dw_ref[...] += dw

    row = pl.BlockSpec((tm, D), lambda i, k: (i, 0))
    vec = pl.BlockSpec((1, D), lambda i, k: (0, 0))
    ins = [a, b, x, w, dres]
    specs = [pl.BlockSpec((tm, tk), lambda i, k: (i, k)), pl.BlockSpec((tk, D), lambda i, k: (b0 // tk + k, 0)), row, vec, row]
    if res is not None:
        ins.append(res)
        specs.append(row)
    if after is not None:
        ins.append(after)
        specs.append(ANY)
    return _call(body, name=name, out_shape=(_sds((M, D)), _sds((1, D))), grid=(M // tm, nk), in_specs=specs,
                 out_specs=(row, vec), scratch=[pltpu.VMEM((tm, D), F32)])(*ins)


def _final_loss(x, w, target, *, name):
    S, D = x.shape
    tm = _tile(S, 512)

    def body(x_ref, w_ref, t_ref, loss_ref, dx_ref, dw_ref):
        xf, wv = x_ref[...], w_ref[...]
        r = lax.rsqrt(jnp.mean(xf * xf, axis=-1, keepdims=True) + NORM_EPS)
        err = xf * r * wv - t_ref[...]
        part = 0.5 * jnp.sum(jnp.mean(err * err, axis=-1, keepdims=True), axis=0, keepdims=True)
        dx, dw = _norm_bwd_math(xf, wv, err * (1.0 / D), NORM_EPS)
        dx_ref[...] = dx

        @pl.when(pl.program_id(0) == 0)
        def _():
            dw_ref[...] = jnp.zeros_like(dw_ref)
            loss_ref[...] = jnp.zeros_like(loss_ref)

        dw_ref[...] += dw
        loss_ref[...] += jnp.broadcast_to(part, loss_ref.shape)

    row = pl.BlockSpec((tm, D), lambda i: (i, 0))
    vec = pl.BlockSpec((1, D), lambda i: (0, 0))
    return _call(body, name=name, out_shape=(_sds((1, LANES)), _sds((S, D)), _sds((1, D))), grid=(S // tm,),
                 in_specs=[row, vec, row], out_specs=(pl.BlockSpec((1, LANES), lambda i: (0, 0)), row, vec))(x, w, target)


def _rope_tables(pos, inv_freq):
    S = pos.shape[0]
    tm = _tile(S, 512)

    def body(p_ref, f_ref, c_ref, s_ref):
        ang = p_ref[...].astype(F32) * f_ref[...]
        c_ref[...] = jnp.cos(ang)
        s_ref[...] = jnp.sin(ang)

    blk = pl.BlockSpec((tm, LANES), lambda i: (i, 0))
    return _call(body, name="rope_tables", out_shape=(_sds((S, LANES)), _sds((S, LANES))), grid=(S // tm,),
                 in_specs=[pl.BlockSpec((tm, 1), lambda i: (i, 0)), pl.BlockSpec((1, LANES), lambda i: (0, 0))],
                 out_specs=(blk, blk))(pos, inv_freq)


def _rot_half(t):
    lane = lax.broadcasted_iota(jnp.int32, t.shape, 1)
    lo = (lane % HEAD_DIM) < (HEAD_DIM // 2)
    return jnp.where(lo, -pltpu.roll(t, LANES - HEAD_DIM // 2, axis=1), pltpu.roll(t, HEAD_DIM // 2, axis=1))


def _rope(t, c, s):
    return t * c + _rot_half(t) * s


def _unrope(dy, c, s):
    return dy * c - _rot_half(dy * s)


PD = POOL_GROUPS * LANES
QD = N_HEADS * HEAD_DIM
KD = N_KV_HEADS * HEAD_DIM
assert PD % QD == 0 and (PD + QD) % (2 * KD) == 0 and KD == LANES
def _attn_probs(q, kcat, sink, mask):
    s = lax.dot_general(q.astype(BF16), kcat, (((1,), (1,)), ((), ())), preferred_element_type=F32) * (HEAD_DIM ** -0.5)
    s = jnp.where(mask, s, -jnp.inf)
    m = jnp.maximum(jnp.max(s, axis=1, keepdims=True), sink)
    p = jnp.exp(s - m)
    ps = jnp.exp(sink - m)
    inv = 1.0 / (jnp.sum(p, axis=1, keepdims=True) + ps)
    return p * inv, ps * inv


def _attn_mask(n):
    qi = lax.broadcasted_iota(jnp.int32, (BLOCK, 2 * BLOCK), 0)
    kj = lax.broadcasted_iota(jnp.int32, (BLOCK, 2 * BLOCK), 1)
    rel = qi + BLOCK - kj
    return (rel >= 0) & (rel < BLOCK) & ((n > 0) | (kj >= BLOCK))


def _attn_in_specs(nb):
    def cur(n):
        return jnp.minimum(n, nb - 1)

    def prev(n):
        return jnp.clip(n - 1, 0, nb - 1)

    kvb = (PD + QD) // (2 * KD)
    return [pl.BlockSpec(memory_space=pltpu.SMEM),
            pl.BlockSpec((BLOCK, QD), lambda n: (cur(n), PD // QD)),
            pl.BlockSpec((BLOCK, 2 * KD), lambda n: (cur(n), kvb)),
            pl.BlockSpec((BLOCK, 2 * KD), lambda n: (prev(n), kvb)),
            pl.BlockSpec((BLOCK, LANES), lambda n: (cur(n), 0)), pl.BlockSpec((BLOCK, LANES), lambda n: (cur(n), 0)),
            pl.BlockSpec((BLOCK, LANES), lambda n: (prev(n), 0)), pl.BlockSpec((BLOCK, LANES), lambda n: (prev(n), 0))]


def _attn_keys(kvc_ref, kvp_ref, cc, sc, cp, sp):
    kc = _rope(kvc_ref[:, :KD], cc, sc)
    kp = _rope(kvp_ref[:, :KD], cp, sp)
    vc, vp = kvc_ref[:, KD:], kvp_ref[:, KD:]
    kcat, vcat = [], []
    for kk in range(N_KV_HEADS):
        sl = slice(kk * HEAD_DIM, (kk + 1) * HEAD_DIM)
        kcat.append(jnp.concatenate([kp[:, sl], kc[:, sl]], axis=0).astype(BF16))
        vcat.append(jnp.concatenate([vp[:, sl], vc[:, sl]], axis=0).astype(BF16))
    return kcat, vcat


def _attn_fwd(proj, cos, sin, sinks, cat):
    S = proj.shape[0]
    nb = S // BLOCK

    def body(sink_ref, q_ref, kvc_ref, kvp_ref, cc_ref, sc_ref, cp_ref, sp_ref, cat_ref, o_ref):
        n = pl.program_id(0)
        cc, sc = cc_ref[...], sc_ref[...]
        kcat, vcat = _attn_keys(kvc_ref, kvp_ref, cc, sc, cp_ref[...], sp_ref[...])
        mask = _attn_mask(n)
        def head_pair(j):
            qr = _rope(q_ref[:, j * LANES:(j + 1) * LANES], cc, sc)
            for e in range(LANES // HEAD_DIM):
                yield
                h = j * (LANES // HEAD_DIM) + e
                pn, _ = _attn_probs(qr[:, e * HEAD_DIM:(e + 1) * HEAD_DIM], kcat[h // GQ], sink_ref[0, h], mask)
                yield
                o_ref[:, h * HEAD_DIM:(h + 1) * HEAD_DIM] = jnp.dot(
                    pn.astype(BF16), vcat[h // GQ], preferred_element_type=F32).astype(o_ref.dtype)

        _interleave([head_pair(j) for j in range(QD // LANES)])

    return _call(body, name="attn_fwd", out_shape=_sds(cat.shape, cat.dtype), grid=(nb,),
                 in_specs=_attn_in_specs(nb) + [ANY], out_specs=pl.BlockSpec((BLOCK, QD), lambda n: (n, PD // QD)),
                 aliases={8: 0})(sinks, proj, proj, proj, cos, sin, cos, sin, cat)


def _attn_bwd(proj, cos, sin, sinks, dcat):
    S = proj.shape[0]
    nb = S // BLOCK
    scale = HEAD_DIM ** -0.5
    per = LANES // HEAD_DIM

    def body(sink_ref, q_ref, kvc_ref, kvp_ref, cc_ref, sc_ref, cp_ref, sp_ref, do_ref, o_ref, ds_ref, hold, carry, part, pair):
        n = pl.program_id(0)

        @pl.when(n == 0)
        def _():
            hold[...] = jnp.zeros_like(hold)
            carry[...] = jnp.zeros_like(carry)
            ds_ref[...] = jnp.zeros_like(ds_ref)

        live = jnp.where(n < nb, 1.0, 0.0)
        cc, sc, cp, sp = cc_ref[...], sc_ref[...], cp_ref[...], sp_ref[...]
        kcat, vcat = _attn_keys(kvc_ref, kvp_ref, cc, sc, cp, sp)
        mask = _attn_mask(n)
        o_ref[:, :PD] = jnp.zeros((BLOCK, PD), F32)
        o_ref[:, PD:PD + QD] = hold[...]
        dk = [jnp.zeros((2 * BLOCK, HEAD_DIM), F32) for _ in range(N_KV_HEADS)]
        dv = [jnp.zeros((2 * BLOCK, HEAD_DIM), F32) for _ in range(N_KV_HEADS)]
        row = lax.broadcasted_iota(jnp.int32, (8, LANES), 0)
        acc = {"dsk": jnp.zeros((8, LANES), F32)}

        def head_pair(j):
            qr = _rope(q_ref[:, j * LANES:(j + 1) * LANES], cc, sc)
            for e in range(per):
                yield
                h = j * per + e
                kk = h // GQ
                qh = qr[:, e * HEAD_DIM:(e + 1) * HEAD_DIM]
                pn, psn = _attn_probs(qh, kcat[kk], sink_ref[0, h], mask)
                yield
                doh = (do_ref[:, h * HEAD_DIM:(h + 1) * HEAD_DIM] * live).astype(BF16)
                dp = lax.dot_general(doh, vcat[kk], NT, preferred_element_type=F32)
                yield
                delta = jnp.sum(pn * dp, axis=1, keepdims=True)
                ds = (pn * (dp - delta) * scale).astype(BF16)
                pair[j, :, e * HEAD_DIM:(e + 1) * HEAD_DIM] = jnp.dot(ds, kcat[kk], preferred_element_type=F32)
                yield
                dk[kk] = dk[kk] + lax.dot_general(ds, qh.astype(BF16), TN, preferred_element_type=F32)
                dv[kk] = dv[kk] + lax.dot_general(pn.astype(BF16), doh, TN, preferred_element_type=F32)
                acc["dsk"] = acc["dsk"] + jnp.where(row == h, -jnp.sum(psn * delta), 0.0)
            yield
            hold[:, j * LANES:(j + 1) * LANES] = _unrope(pair[j], cc, sc)

        _interleave([head_pair(j) for j in range(QD // LANES)])
        dsk = acc["dsk"]
        for kk in range(N_KV_HEADS):
            sl = slice(kk * HEAD_DIM, (kk + 1) * HEAD_DIM)
            sv = slice(KD + kk * HEAD_DIM, KD + (kk + 1) * HEAD_DIM)
            part[0, :, sl] = dk[kk][:BLOCK]
            part[0, :, sv] = dv[kk][:BLOCK]
            part[1, :, sl] = dk[kk][BLOCK:]
            part[1, :, sv] = dv[kk][BLOCK:]
        done = carry[...] + part[0]
        o_ref[:, PD + QD:PD + QD + KD] = _unrope(done[:, :KD], cp, sp)
        o_ref[:, PD + QD + KD:] = done[:, KD:]
        carry[...] = part[1]
        ds_ref[...] += dsk

    return _call(body, name="attn_bwd", out_shape=(_sds((S, PD + QD + 2 * KD)), _sds((8, LANES))), grid=(nb + 1,),
                 in_specs=_attn_in_specs(nb) + [pl.BlockSpec((BLOCK, QD), lambda n: (jnp.minimum(n, nb - 1), PD // QD))],
                 out_specs=(pl.BlockSpec((BLOCK, PD + QD + 2 * KD), lambda n: (jnp.maximum(n - 1, 0), 0)),
                            pl.BlockSpec((8, LANES), lambda n: (0, 0))),
                 scratch=[pltpu.VMEM((BLOCK, QD), F32), pltpu.VMEM((BLOCK, 2 * KD), F32),
                          pltpu.VMEM((2, BLOCK, 2 * KD), F32), pltpu.VMEM((QD // LANES, BLOCK, LANES), F32)])(
                     sinks, proj, proj, proj, cos, sin, cos, sin, dcat)


def _pool_sums(u, g, t, shift):
    s2 = u + shift(u, 1, t)
    s4 = s2 + shift(s2, 2, t)
    s8 = s4 + shift(s4, 4, t)
    s16 = s8 + shift(s8, 8, t)
    return jnp.where(g == 0, s2, jnp.where(g == 1, s4, jnp.where(g == 2, s8, s16)))


def _pool_specs(S):
    col = pl.BlockSpec((S, LANES), lambda g: (0, g))
    wsp = pl.BlockSpec((1, LANES, LANES), lambda g: (g, 0, 0))
    vec = pl.BlockSpec((1, LANES), lambda g: (0, g))
    return col, wsp, vec


def _pool_fwd(proj, pool_w, scale):
    S = proj.shape[0]
    col, wsp, vec = _pool_specs(S)

    def body(u_ref, w_ref, s_ref, o_ref):
        g = pl.program_id(0)
        u = u_ref[...]
        t = lax.broadcasted_iota(jnp.int32, u.shape, 0)
        cnt = jnp.minimum(t + 1, 2 << g).astype(F32)
        pm = _pool_sums(u, g, t, _shift_dn) / cnt - u
        o_ref[...] = (jnp.dot(pm.astype(BF16), w_ref[0].astype(BF16), preferred_element_type=F32) * s_ref[...]).astype(BF16)

    return _call(body, name="pool_fwd", out_shape=_sds((S, PD + QD), BF16), grid=(POOL_GROUPS,),
                 in_specs=[col, wsp, vec], out_specs=col)(proj, pool_w, scale)


def _pool_bwd(proj, pool_w, scale, dcat, dproj):
    S = proj.shape[0]
    col, wsp, vec = _pool_specs(S)

    def body(u_ref, w_ref, s_ref, d_ref, dproj_ref, du_ref, dw_ref, dsc_ref):
        g = pl.program_id(0)
        u = u_ref[...]
        t = lax.broadcasted_iota(jnp.int32, u.shape, 0)
        cnt = jnp.minimum(t + 1, 2 << g).astype(F32)
        pm = (_pool_sums(u, g, t, _shift_dn) / cnt - u).astype(BF16)
        wv = w_ref[0].astype(BF16)
        d = d_ref[...]
        pw = jnp.dot(pm, wv, preferred_element_type=F32)
        dsc_ref[...] = jnp.sum(pw * d, axis=0, keepdims=True)
        dpw = (d * s_ref[...]).astype(BF16)
        dw_ref[0] = lax.dot_general(pm, dpw, (((0,), (0,)), ((), ())), preferred_element_type=F32)
        dpm = lax.dot_general(dpw, wv, (((1,), (1,)), ((), ())), preferred_element_type=F32)
        du_ref[...] = _pool_sums(dpm / cnt, g, t, _shift_up) - dpm

    return _call(body, name="pool_bwd",
                 out_shape=(_sds(dproj.shape), _sds((POOL_GROUPS, LANES, LANES)), _sds((1, POOL_GROUPS * LANES))),
                 grid=(POOL_GROUPS,), in_specs=[col, wsp, vec, col, ANY], out_specs=(col, wsp, vec),
                 aliases={4: 0})(proj, pool_w, scale, dcat, dproj)


def _conv(x, w_ref, b_ref, t):
    K = w_ref.shape[0]
    y = b_ref[...] + jnp.zeros_like(x)
    for k in range(K):
        y = y + w_ref[k:k + 1, :] * _shift_dn(x, K - 1 - k, t)
    return y


def _silu_grad(y):
    sg = _sigmoid(y)
    return sg * (1.0 + y * (1.0 - sg))


CONV_ROWS = 256
HALO = 8


def _win_above(ref, r0):
    if isinstance(r0, int):
        assert r0 == 0
        return jnp.concatenate([jnp.zeros((HALO, ref.shape[1]), F32), ref[0:CONV_ROWS, :]], axis=0)
    return ref[pl.ds(pl.multiple_of(r0 - HALO, HALO), CONV_ROWS + HALO), :]


def _rows_at(win, start):
    if start % 8 == 0:
        return win[start:start + CONV_ROWS]
    base = start // 8 * 8
    return pltpu.roll(win, win.shape[0] - (start - base), axis=0)[base:base + CONV_ROWS]


def _taps_above(win, K):
    return [_rows_at(win, HALO - (K - 1 - k)) for k in range(K)]


def _conv_taps(taps, w, b):
    y = b
    for k in range(len(w)):
        y = y + w[k] * taps[k]
    return y


def _conv_t_win(win, w):
    K = len(w)
    out = None
    for k in range(K):
        d = K - 1 - k
        term = w[k] * _rows_at(win, d)
        out = term if out is None else out + term
    return out


def _fold8(x):
    return jnp.sum(x.reshape(CONV_ROWS // 8, 8, x.shape[-1]), axis=0)


def _chunk_loop(S, step, init):
    carry = step(0, init)
    return lax.fori_loop(1, S // CONV_ROWS, lambda i, c: step(pl.multiple_of(i * CONV_ROWS, CONV_ROWS), c), carry)


def _ffn_mid_specs(S, K, layer, nf):
    return [pl.BlockSpec((S, LANES), lambda j: (0, j)), pl.BlockSpec((S, LANES), lambda j: (0, nf + j)),
            pl.BlockSpec((None, K, LANES), lambda j: (layer, 0, j)), pl.BlockSpec((None, K, LANES), lambda j: (layer, 0, nf + j)),
            pl.BlockSpec((None, 1, LANES), lambda j: (layer, 0, j)), pl.BlockSpec((None, 1, LANES), lambda j: (layer, 0, nf + j))]


def _ffn_mid_fwd(a, cw, cb, layer):
    S, F = a.shape[0], a.shape[1] // 2
    nf = F // LANES
    K = cw.shape[1]

    def body(au_ref, ag_ref, wu_ref, wg_ref, bu_ref, bg_ref, o_ref):
        t = lax.broadcasted_iota(jnp.int32, (S, LANES), 0)
        hu = _conv(au_ref[...], wu_ref, bu_ref, t)
        hg = _conv(ag_ref[...], wg_ref, bg_ref, t)
        o_ref[...] = (hg * _sigmoid(hg) * hu).astype(BF16)

    return _call(body, name="ffn_mid_fwd", out_shape=_sds((S, F), BF16), grid=(nf,),
                 in_specs=_ffn_mid_specs(S, K, layer, nf), out_specs=pl.BlockSpec((S, LANES), lambda j: (0, j)))(
                     a, a, cw, cw, cb[:, None], cb[:, None])


def _ffn_mid_bwd(a, cw, cb, layer, dact):
    S, F = a.shape[0], a.shape[1] // 2
    nf = F // LANES
    K = cw.shape[1]

    def body(au_ref, ag_ref, wu_ref, wg_ref, bu_ref, bg_ref, d_ref, dau_ref, dag_ref, dwu_ref, dwg_ref, dbu_ref, dbg_ref,
             dhu_s, dhg_s):
        T = CONV_ROWS
        wu = [wu_ref[k:k + 1, :] for k in range(K)]
        wg = [wg_ref[k:k + 1, :] for k in range(K)]
        bu, bg = bu_ref[...], bg_ref[...]
        zero8 = jnp.zeros((HALO, LANES), F32)
        dhu_s[S:S + HALO, :] = zero8
        dhg_s[S:S + HALO, :] = zero8

        def first_pass(r0, acc):
            tu, tg = _taps_above(_win_above(au_ref, r0), K), _taps_above(_win_above(ag_ref, r0), K)
            hu, hg = _conv_taps(tu, wu, bu), _conv_taps(tg, wg, bg)
            d = d_ref[pl.ds(r0, T), :].astype(F32)
            sg = _sigmoid(hg)
            dhu = d * hg * sg
            dhg = d * hu * (sg * (1.0 + hg * (1.0 - sg)))
            dhu_s[pl.ds(r0, T), :] = dhu
            dhg_s[pl.ds(r0, T), :] = dhg
            new = []
            for dh, taps in ((dhu, tu), (dhg, tg)):
                for k in range(K):
                    new.append(acc[len(new)] + _fold8(dh * taps[k]))
            new.append(acc[2 * K] + _fold8(dhu))
            new.append(acc[2 * K + 1] + _fold8(dhg))
            return tuple(new)

        acc = _chunk_loop(S, first_pass, tuple(zero8 for _ in range(2 * K + 2)))
        for k in range(K):
            dwu_ref[k:k + 1, :] = jnp.sum(acc[k], axis=0, keepdims=True)
            dwg_ref[k:k + 1, :] = jnp.sum(acc[K + k], axis=0, keepdims=True)
        dbu_ref[...] = jnp.sum(acc[2 * K], axis=0, keepdims=True)
        dbg_ref[...] = jnp.sum(acc[2 * K + 1], axis=0, keepdims=True)

        def second_pass(i, carry):
            r0 = pl.multiple_of(i * T, T)
            dau_ref[pl.ds(r0, T), :] = _conv_t_win(dhu_s[pl.ds(r0, T + HALO), :], wu).astype(BF16)
            dag_ref[pl.ds(r0, T), :] = _conv_t_win(dhg_s[pl.ds(r0, T + HALO), :], wg).astype(BF16)
            return carry

        lax.fori_loop(0, S // T, second_pass, 0)

    col = pl.BlockSpec((S, LANES), lambda j: (0, j))
    wsp = pl.BlockSpec((K, LANES), lambda j: (0, j))
    bsp = pl.BlockSpec((1, LANES), lambda j: (0, j))
    dau, dag, dwu, dwg, dbu, dbg = _call(
        body, name="ffn_mid_bwd",
        out_shape=(_sds((S, F), BF16), _sds((S, F), BF16), _sds((K, F)), _sds((K, F)), _sds((1, F)), _sds((1, F))), grid=(nf,),
        in_specs=_ffn_mid_specs(S, K, layer, nf) + [col], out_specs=(col, col, wsp, wsp, bsp, bsp),
        scratch=[pltpu.VMEM((S + HALO, LANES), F32), pltpu.VMEM((S + HALO, LANES), F32)])(
            a, a, cw, cw, cb[:, None], cb[:, None], dact)
    return dau, dag, jnp.concatenate([dwu, dwg], axis=1), jnp.concatenate([dbu, dbg], axis=1)


def _conv_silu_fwd(x, cw, cb):
    S = x.shape[0]
    K, C = cw.shape

    def body(x_ref, w_ref, b_ref, o_ref):
        t = lax.broadcasted_iota(jnp.int32, (S, LANES), 0)
        y = _conv(x_ref[...], w_ref, b_ref, t)
        o_ref[...] = y * _sigmoid(y)

    col = pl.BlockSpec((S, LANES), lambda j: (0, j))
    return _call(body, name="conv_silu_fwd", out_shape=_sds((S, C)), grid=(C // LANES,),
                 in_specs=[col, pl.BlockSpec((K, LANES), lambda j: (0, j)), pl.BlockSpec((1, LANES), lambda j: (0, j))],
                 out_specs=col)(x, cw, cb)


def _conv_silu_bwd(x, cw, cb, douts):
    S = x.shape[0]
    K, C = cw.shape
    starts, off = [], 0
    for d in douts:
        starts.append(off)
        off += d.shape[1] // LANES
    assert off == C // LANES

    def body(x_ref, w_ref, b_ref, *rest):
        dy_s = rest[-1]
        d_refs, (dx_ref, dw_ref, db_ref) = rest[:len(douts)], rest[len(douts):-1]
        j = pl.program_id(0)
        T = CONV_ROWS
        w = [w_ref[k:k + 1, :] for k in range(K)]
        b = b_ref[...]
        zero8 = jnp.zeros((HALO, LANES), F32)
        dy_s[S:S + HALO, :] = zero8

        def first_pass(r0, acc):
            taps = _taps_above(_win_above(x_ref, r0), K)
            y = _conv_taps(taps, w, b)
            d = d_refs[0][pl.ds(r0, T), :]
            for i in range(1, len(douts)):
                d = jnp.where(j >= starts[i], d_refs[i][pl.ds(r0, T), :], d)
            dy = d * _silu_grad(y)
            dy_s[pl.ds(r0, T), :] = dy
            return tuple(acc[k] + _fold8(dy * taps[k]) for k in range(K)) + (acc[K] + _fold8(dy),)

        acc = _chunk_loop(S, first_pass, tuple(zero8 for _ in range(K + 1)))
        for k in range(K):
            dw_ref[k:k + 1, :] = jnp.sum(acc[k], axis=0, keepdims=True)
        db_ref[...] = jnp.sum(acc[K], axis=0, keepdims=True)

        def second_pass(i, carry):
            r0 = pl.multiple_of(i * T, T)
            dx_ref[pl.ds(r0, T), :] = _conv_t_win(dy_s[pl.ds(r0, T + HALO), :], w).astype(BF16)
            return carry

        lax.fori_loop(0, S // T, second_pass, 0)

    col = pl.BlockSpec((S, LANES), lambda j: (0, j))
    wsp = pl.BlockSpec((K, LANES), lambda j: (0, j))
    bsp = pl.BlockSpec((1, LANES), lambda j: (0, j))

    def dspec(i):
        nblk = douts[i].shape[1] // LANES
        return pl.BlockSpec((S, LANES), lambda j: (0, jnp.clip(j - starts[i], 0, nblk - 1)))

    return _call(body, name="conv_silu_bwd", out_shape=(_sds((S, C), BF16), _sds((K, C)), _sds((1, C))), grid=(C // LANES,),
                 in_specs=[col, wsp, bsp] + [dspec(i) for i in range(len(douts))],
                 out_specs=(col, wsp, bsp), scratch=[pltpu.VMEM((S + HALO, LANES), F32)])(x, cw, cb, *douts)


HI = lax.Precision.HIGHEST


PREP_CHUNKS = 2


def _to_group(x, g, live):
    return jnp.where(live, x if g == 0 else pltpu.roll(x, LANES - SSM_R * g, axis=1), 0.0)


def _from_groups(refs_g):
    out = refs_g[0]
    for g in range(1, SSM_G):
        out = out + pltpu.roll(refs_g[g], SSM_R * g, axis=1)
    return out


def _ssd_prep_fwd(raw, bias_row, alog_row):
    S = raw.shape[0]
    nc = S // SSM_L
    NH = SSM_G * SSM_R

    def body(raw_ref, b_ref, al_ref, pre_ref, dth_ref, dt_ref, acs_ref, acst_ref):
        r_i = lax.broadcasted_iota(jnp.int32, (LANES, LANES), 0)
        c_i = lax.broadcasted_iota(jnp.int32, (LANES, LANES), 1)
        live = c_i < SSM_R
        tril = jnp.where(r_i >= c_i, 1.0, 0.0)
        for h in range(PREP_CHUNKS):
            rows = slice(h * SSM_L, (h + 1) * SSM_L)
            pre = raw_ref[rows, :] + b_ref[...]
            dt = jnp.where(c_i < NH, jnp.logaddexp(pre, 0.0), 0.0)
            acs = jnp.dot(tril, dt * (-jnp.exp(al_ref[...])), preferred_element_type=F32, precision=HI)
            pre_ref[rows, :] = pre
            dth_ref[rows, :] = dt
            for g in range(SSM_G):
                acs_g = _to_group(acs, g, live)
                dt_ref[g, rows, :] = _to_group(dt, g, live)
                acs_ref[g, rows, :] = acs_g
                acst_ref[g, :, rows] = acs_g.T

    row = pl.BlockSpec((1, LANES), lambda c: (0, 0))
    flat = pl.BlockSpec((PREP_CHUNKS * SSM_L, LANES), lambda c: (c, 0))
    blk = pl.BlockSpec((SSM_G, PREP_CHUNKS * SSM_L, LANES), lambda c: (0, c, 0))
    big = _sds((SSM_G, S, LANES))
    return _call(body, name="ssd_prep_fwd",
                 out_shape=(_sds((S, LANES)), _sds((S, LANES)), big, big, _sds((SSM_G, LANES, S))), grid=(nc // PREP_CHUNKS,),
                 in_specs=[flat, row, row],
                 out_specs=(flat, flat, blk, blk, pl.BlockSpec((SSM_G, LANES, PREP_CHUNKS * SSM_L), lambda c: (0, 0, c))))(
                     raw, bias_row, alog_row)


def _ssd_prep_bwd(pre_h, dt_h, alog_row, ddt_g, dacs_g, dacst_g):
    S = pre_h.shape[0]
    nc = S // SSM_L
    NH = SSM_G * SSM_R

    def body(pre_ref, dt_ref, al_ref, ddt_ref, dacs_ref, dacst_ref, draw_ref, db_ref, dal_ref):
        c = pl.program_id(0)
        r_i = lax.broadcasted_iota(jnp.int32, (LANES, LANES), 0)
        c_i = lax.broadcasted_iota(jnp.int32, (LANES, LANES), 1)
        triu = jnp.where(r_i <= c_i, 1.0, 0.0)

        @pl.when(c == 0)
        def _():
            db_ref[...] = jnp.zeros_like(db_ref)
            dal_ref[...] = jnp.zeros_like(dal_ref)

        A = -jnp.exp(al_ref[...])
        for h in range(PREP_CHUNKS):
            rows = slice(h * SSM_L, (h + 1) * SSM_L)
            dacs = _from_groups([dacs_ref[g, rows, :] + dacst_ref[g, :, rows].T for g in range(SSM_G)])
            da = jnp.dot(triu, dacs, preferred_element_type=F32, precision=HI)
            ddt = _from_groups([ddt_ref[g, rows, :] for g in range(SSM_G)]) + da * A
            dpre = jnp.where(c_i < NH, ddt * _sigmoid(pre_ref[rows, :]), 0.0)
            draw_ref[rows, :] = dpre
            db_ref[...] += jnp.sum(dpre, axis=0, keepdims=True)
            dal_ref[...] += jnp.where(c_i[:1] < NH, jnp.sum(da * dt_ref[rows, :], axis=0, keepdims=True) * A, 0.0)

    row = pl.BlockSpec((1, LANES), lambda c: (0, 0))
    flat = pl.BlockSpec((PREP_CHUNKS * SSM_L, LANES), lambda c: (c, 0))
    blk = pl.BlockSpec((SSM_G, PREP_CHUNKS * SSM_L, LANES), lambda c: (0, c, 0))
    return _call(body, name="ssd_prep_bwd", out_shape=(_sds((S, LANES)), _sds((1, LANES)), _sds((1, LANES))),
                 grid=(nc // PREP_CHUNKS,),
                 in_specs=[flat, flat, row, blk, blk,
                           pl.BlockSpec((SSM_G, LANES, PREP_CHUNKS * SSM_L), lambda c: (0, 0, c))],
                 out_specs=(flat, row, row))(pre_h, dt_h, alog_row, ddt_g, dacs_g, dacst_g)


NT = (((1,), (1,)), ((), ()))
TN = (((0,), (0,)), ((), ()))
SSM_HP = SSM_R * SSM_P


def _ssd_group_terms(xs_ref, dt_ref, acs_ref, d_ref):
    hid = lax.broadcasted_iota(jnp.int32, (1, SSM_HP), 1) // SSM_P
    rid = lax.broadcasted_iota(jnp.int32, (SSM_HP, 1), 0) // SSM_P

    def widen(cols):
        out = cols[0]
        for r in range(1, SSM_R):
            out = jnp.where(hid == r, cols[r], out)
        return out

    dt_c = [dt_ref[:, r:r + 1] for r in range(SSM_R)]
    acs_c = [acs_ref[:, r:r + 1] for r in range(SSM_R)]
    last = [acs_ref[SSM_L - 1:SSM_L, r:r + 1] for r in range(SSM_R)]
    decay_c = [jnp.exp(last[r] - acs_c[r]) for r in range(SSM_R)]
    cd = [jnp.exp(last[r]) for r in range(SSM_R)]
    cd_rows = cd[0]
    for r in range(1, SSM_R):
        cd_rows = jnp.where(rid == r, cd[r], cd_rows)
    xs = xs_ref[...]
    return (xs, xs * widen(dt_c), widen([jnp.exp(a) for a in acs_c]), widen(decay_c),
            widen([d_ref[:, r:r + 1] for r in range(SSM_R)]), cd_rows, dt_c, decay_c, cd)


def _ssd_lmat(acs_ref, acst_ref, r, tril):
    return jnp.exp(jnp.where(tril, acs_ref[:, r:r + 1] - acst_ref[r:r + 1, :], -jnp.inf))


SSM_GPS = 8


def _ssd_specs(rev, nc):
    def cc(c):
        return nc - 1 - c if rev else c
    xs_blocks = (SSM_G * SSM_HP) // (SSM_GPS * SSM_N)
    xs = pl.BlockSpec((SSM_L, SSM_GPS * SSM_HP), lambda g, c: (cc(c), g))
    bsp = pl.BlockSpec((SSM_L, SSM_GPS * SSM_N), lambda g, c: (cc(c), xs_blocks + g))
    csp = pl.BlockSpec((SSM_L, SSM_GPS * SSM_N), lambda g, c: (cc(c), xs_blocks + SSM_G // SSM_GPS + g))
    sc = pl.BlockSpec((SSM_GPS, SSM_L, LANES), lambda g, c: (g, cc(c), 0))
    sct = pl.BlockSpec((SSM_GPS, LANES, SSM_L), lambda g, c: (g, 0, cc(c)))
    gsp = pl.BlockSpec((SSM_GPS, 1, LANES), lambda g, c: (g, 0, 0))
    st = pl.BlockSpec((None, SSM_GPS, SSM_HP, SSM_N), lambda g, c: (cc(c), g, 0, 0))
    return xs, bsp, csp, sc, sct, gsp, st


def _interleave(gens):
    live = list(gens)
    while live:
        for g in list(live):
            try:
                next(g)
            except StopIteration:
                live.remove(g)


def _rounds(gens):
    live = list(gens)
    while live:
        for g in list(live):
            try:
                next(g)
            except StopIteration:
                live.remove(g)
        yield


def _ssd_group_views(gg, xs_ref, b_ref, c_ref, *per_group):
    return (xs_ref.at[:, gg * SSM_HP:(gg + 1) * SSM_HP], b_ref.at[:, gg * SSM_N:(gg + 1) * SSM_N],
            c_ref.at[:, gg * SSM_N:(gg + 1) * SSM_N]) + tuple(r.at[gg] for r in per_group)


def _ssd_fwd(xbc, dt_g, acs_g, acst_g, d_g):
    S = xbc.shape[0]
    nc = S // SSM_L
    xs_s, b_s, c_s, sc, sct, gsp, st = _ssd_specs(False, nc)

    def body(xs_ref, b_ref, c_ref, dt_ref, acs_ref, acst_ref, d_ref, y_ref, st_ref, state):
        c = pl.program_id(1)

        @pl.when(c == 0)
        def _():
            state[...] = jnp.zeros_like(state)

        tril = lax.broadcasted_iota(jnp.int32, (SSM_L, SSM_L), 0) >= lax.broadcasted_iota(jnp.int32, (SSM_L, SSM_L), 1)
        def group(gg):
            xs_v, b_v, c_v, dt_v, acs_v, acst_v, d_v, st_v, state_v = _ssd_group_views(
                gg, xs_ref, b_ref, c_ref, dt_ref, acs_ref, acst_ref, d_ref, st_ref, state)
            y_v = y_ref.at[:, gg * SSM_HP:(gg + 1) * SSM_HP]
            Bb, Cb = b_v[...].astype(BF16), c_v[...].astype(BF16)
            Gm = lax.dot_general(Cb, Bb, NT, preferred_element_type=F32)
            yield
            xs, X, e_all, decay_all, d_all, cd_rows, _, _, _ = _ssd_group_terms(xs_v, dt_v, acs_v, d_v)
            S_all = state_v[...]
            st_v[...] = S_all
            yield
            yo = lax.dot_general(Cb, S_all.astype(BF16), NT, preferred_element_type=F32)
            new_state = lax.dot_general((X * decay_all).astype(BF16), Bb, TN, preferred_element_type=F32)
            yield
            y_v[...] = e_all * yo + d_all * xs
            state_v[...] = S_all * cd_rows + new_state

            def head(r):
                sl = slice(r * SSM_P, (r + 1) * SSM_P)
                M = Gm * _ssd_lmat(acs_v, acst_v, r, tril)
                yield
                y_v[:, sl] += jnp.dot(M.astype(BF16), X[:, sl].astype(BF16), preferred_element_type=F32)

            yield from _rounds([head(r) for r in range(SSM_R)])

        _interleave([group(gg) for gg in range(SSM_GPS)])

    return _call(body, name="ssd_fwd",
                 out_shape=(_sds((S, SSM_G * SSM_HP)), _sds((nc, SSM_G, SSM_HP, SSM_N))),
                 grid=(SSM_G // SSM_GPS, nc), in_specs=[xs_s, b_s, c_s, sc, sc, sct, gsp],
                 out_specs=(xs_s, pl.BlockSpec((None, SSM_GPS, SSM_HP, SSM_N), lambda g, c: (c, g, 0, 0))),
                 scratch=[pltpu.VMEM((SSM_GPS, SSM_HP, SSM_N), F32)])(xbc, xbc, xbc, dt_g, acs_g, acst_g, d_g)


def _ssd_bwd(xbc, dt_g, acs_g, acst_g, d_g, states, dy):
    S = xbc.shape[0]
    nc = S // SSM_L
    xs_s, b_s, c_s, sc, sct, gsp, st = _ssd_specs(True, nc)
    bc_out = pl.BlockSpec((SSM_L, SSM_GPS * SSM_N), lambda g, c: (nc - 1 - c, g))

    def body(xs_ref, b_ref, c_ref, dt_ref, acs_ref, acst_ref, d_ref, st_ref, dy_ref,
             dxs_ref, db_ref, dc_ref, ddt_ref, dacs_ref, dacst_ref, dd_ref, dstate):
        c = pl.program_id(1)

        @pl.when(c == 0)
        def _():
            dstate[...] = jnp.zeros_like(dstate)
            dd_ref[...] = jnp.zeros_like(dd_ref)

        tril = lax.broadcasted_iota(jnp.int32, (SSM_L, SSM_L), 0) >= lax.broadcasted_iota(jnp.int32, (SSM_L, SSM_L), 1)
        lane = lax.broadcasted_iota(jnp.int32, (1, LANES), 1)
        subl = lax.broadcasted_iota(jnp.int32, (LANES, 1), 0)
        last_row = lax.broadcasted_iota(jnp.int32, (SSM_L, 1), 0) == SSM_L - 1
        triu = lax.broadcasted_iota(jnp.int32, (SSM_L, SSM_L), 0) <= lax.broadcasted_iota(jnp.int32, (SSM_L, SSM_L), 1)
        def group(gg):
            xs_v, b_v, c_v, dt_v, acs_v, acst_v, d_v, st_v, ddt_v, dacs_v, dacst_v, dd_v, dstate_v = _ssd_group_views(
                gg, xs_ref, b_ref, c_ref, dt_ref, acs_ref, acst_ref, d_ref, st_ref, ddt_ref, dacs_ref, dacst_ref, dd_ref, dstate)
            dy_v, dxs_v = (r.at[:, gg * SSM_HP:(gg + 1) * SSM_HP] for r in (dy_ref, dxs_ref))
            db_v, dc_v = (r.at[:, gg * SSM_N:(gg + 1) * SSM_N] for r in (db_ref, dc_ref))
            Bb, Cb = b_v[...].astype(BF16), c_v[...].astype(BF16)
            Gm = lax.dot_general(Cb, Bb, NT, preferred_element_type=F32)
            GmT = lax.dot_general(Bb, Cb, NT, preferred_element_type=F32)
            yield
            xs, X, e_all, decay_all, d_all, cd_rows, dt_c, decay_c, cd = _ssd_group_terms(xs_v, dt_v, acs_v, d_v)
            S_all, dSn_all, dY = st_v[...], dstate_v[...], dy_v[...]
            Sb, dSnb = S_all.astype(BF16), dSn_all.astype(BF16)
            yield
            T = lax.dot_general(Cb, Sb, NT, preferred_element_type=F32)
            dT = (dY * e_all).astype(BF16)
            dC = jnp.dot(dT, Sb, preferred_element_type=F32)
            dS_prev = lax.dot_general(dT, Cb, TN, preferred_element_type=F32)
            yield
            yo_dy = dY * (e_all * T)
            W = lax.dot_general(Bb, dSnb, NT, preferred_element_type=F32)
            dB = jnp.dot((X * decay_all).astype(BF16), dSnb, preferred_element_type=F32)
            yield
            xw = X * W
            dcd_rows = jnp.sum(dSn_all * S_all, axis=1, keepdims=True)
            dstate_v[...] = dS_prev + dSn_all * cd_rows
            dX_state = W * decay_all
            yield
            acc = dict(dG=jnp.zeros((SSM_L, SSM_L), F32), dGT=jnp.zeros((SSM_L, SSM_L), F32),
                       ddt=jnp.zeros((SSM_L, LANES), F32), dacs=jnp.zeros((SSM_L, LANES), F32),
                       dacst=jnp.zeros((LANES, SSM_L), F32), dd=jnp.zeros((1, LANES), F32))

            def head(r):
                sl = slice(r * SSM_P, (r + 1) * SSM_P)
                Lm = _ssd_lmat(acs_v, acst_v, r, tril)
                LmT = jnp.exp(jnp.where(triu, acst_v[r:r + 1, :] - acs_v[:, r:r + 1], -jnp.inf))
                M = Gm * Lm
                yield
                dYh, xs_h = dY[:, sl], xs[:, sl]
                dYb, Xb = dYh.astype(BF16), X[:, sl].astype(BF16)
                dM = lax.dot_general(dYb, Xb, NT, preferred_element_type=F32)
                yield
                dX = jnp.dot((GmT * LmT).astype(BF16), dYb, preferred_element_type=F32) + dX_state[:, sl]
                acc["dG"] = acc["dG"] + dM * Lm
                acc["dGT"] = acc["dGT"] + lax.dot_general(Xb, dYb, NT, preferred_element_type=F32) * LmT
                yield
                dseg = dM * M
                dd = jnp.sum(xw[:, sl], axis=1, keepdims=True) * decay_c[r]
                dcd = jnp.sum(dcd_rows[sl])
                dacs_col = (jnp.sum(dseg, axis=1, keepdims=True) + jnp.sum(yo_dy[:, sl], axis=1, keepdims=True) - dd
                            + jnp.where(last_row, dcd * cd[r] + jnp.sum(dd), 0.0))
                dacs_row = -jnp.sum(dseg, axis=0, keepdims=True)
                yield
                dxs_v[:, sl] = dX * dt_c[r] + d_all[:, sl] * dYh
                acc["ddt"] = acc["ddt"] + jnp.where(lane == r, jnp.sum(dX * xs_h, axis=1, keepdims=True), 0.0)
                acc["dacs"] = acc["dacs"] + jnp.where(lane == r, dacs_col, 0.0)
                acc["dacst"] = acc["dacst"] + jnp.where(subl == r, dacs_row, 0.0)
                acc["dd"] = acc["dd"] + jnp.where(lane == r, jnp.sum(dYh * xs_h), 0.0)

            yield from _rounds([head(r) for r in range(SSM_R)])
            dc_v[...] = dC + jnp.dot(acc["dG"].astype(BF16), Bb, preferred_element_type=F32)
            db_v[...] = dB + jnp.dot(acc["dGT"].astype(BF16), Cb, preferred_element_type=F32)
            ddt_v[...] = acc["ddt"]
            dacs_v[...] = acc["dacs"]
            dacst_v[...] = acc["dacst"]
            dd_v[...] += acc["dd"]

        _interleave([group(gg) for gg in range(SSM_GPS)])

    big = _sds((SSM_G, S, LANES))
    return _call(body, name="ssd_bwd",
                 out_shape=(_sds((S, SSM_G * SSM_HP)), _sds((S, SSM_G * SSM_N)), _sds((S, SSM_G * SSM_N)),
                            big, big, _sds((SSM_G, LANES, S)), _sds((SSM_G, 1, LANES))),
                 grid=(SSM_G // SSM_GPS, nc), in_specs=[xs_s, b_s, c_s, sc, sc, sct, gsp, st, xs_s],
                 out_specs=(xs_s, bc_out, bc_out, sc, sc, sct, gsp),
                 scratch=[pltpu.VMEM((SSM_GPS, SSM_HP, SSM_N), F32)])(xbc, xbc, xbc, dt_g, acs_g, acst_g, d_g, states, dy)


def _gate_norm_fwd(y, proj, w):
    S, DI = y.shape
    tm = _tile(S, 256)

    def body(y_ref, z_ref, w_ref, o_ref):
        z = z_ref[...]
        gn = y_ref[...] * (z * _sigmoid(z))
        r = lax.rsqrt(jnp.mean(gn * gn, axis=-1, keepdims=True) + SSM_NORM_EPS)
        o_ref[...] = (gn * r * w_ref[...]).astype(BF16)

    row = pl.BlockSpec((tm, DI), lambda i: (i, 0))
    return _call(body, name="gate_norm_fwd", out_shape=_sds((S, DI), BF16), grid=(S // tm,),
                 in_specs=[row, row, pl.BlockSpec((1, DI), lambda i: (0, 0))], out_specs=row)(y, proj, w)


def _gate_norm_bwd(y, proj, w, dout):
    S, DI = y.shape
    tm = _tile(S, 256)

    def body(y_ref, z_ref, w_ref, d_ref, dy_ref, dz_ref, dw_ref):
        z, yv = z_ref[...], y_ref[...]
        sz = z * _sigmoid(z)
        dgn, dw = _norm_bwd_math(yv * sz, w_ref[...], d_ref[...].astype(F32), SSM_NORM_EPS)
        dy_ref[...] = dgn * sz
        dz_ref[...] = (dgn * yv * _silu_grad(z)).astype(BF16)

        @pl.when(pl.program_id(0) == 0)
        def _():
            dw_ref[...] = jnp.zeros_like(dw_ref)

        dw_ref[...] += dw

    row = pl.BlockSpec((tm, DI), lambda i: (i, 0))
    vec = pl.BlockSpec((1, DI), lambda i: (0, 0))
    return _call(body, name="gate_norm_bwd", out_shape=(_sds((S, DI)), _sds((S, DI), BF16), _sds((1, DI))), grid=(S // tm,),
                 in_specs=[row, row, vec, row], out_specs=(row, row, vec))(y, proj, w, dout)


def _group_major(v):
    return jnp.pad(v.reshape(SSM_G, 1, SSM_R), ((0, 0), (0, 0), (0, LANES - SSM_R)))


def _ungroup(t):
    return t[:, :SSM_R].reshape(1, SSM_G * SSM_R)


def _ffn_fwd(x, P, l, need):
    h = _rmsnorm(x, P["norm_ffn"][l:l + 1], name=f"ffn{l}_norm")
    need(f"ffn{l}_up", h)
    a = _mm(h, P[f"ffn_w_upT{l}"], tb=True, name=f"ffn{l}_up")
    need(f"ffn{l}_down", a)
    act = _ffn_mid_fwd(a, P["ffn_conv_w"], P["ffn_conv_b"], l)
    out = _mm(act, P[f"ffn_w_down{l}"], res=x, name=f"ffn{l}_down")
    return out, (x, h, a, act)


def _ffn_bwd(saved, P, l, dx, emit):
    x, h, a, act = saved
    wT = P[f"ffn_w_upT{l}"]
    F = wT.shape[0] // 2
    dact = _mm(dx, P[f"ffn_w_down{l}"], tb=True, out_dtype=BF16, name=f"ffn{l}_down_dx")
    dw_down = _mm(act, dx, ta=True, out_dtype=PAYLOAD, name=f"ffn{l}_down_dw")
    dau, dag, dcw, dcb = _ffn_mid_bwd(a, P["ffn_conv_w"], P["ffn_conv_b"], l, dact)
    dw_upT = _mm(dau, h, ta=True, out_dtype=PAYLOAD, out_rows=(2 * F, 0, None), name=f"ffn{l}_up_u_dw")
    dw_upT = _mm(dag, h, ta=True, out_dtype=PAYLOAD, out_rows=(2 * F, F, dw_upT), name=f"ffn{l}_up_g_dw")
    tie = emit(f"ffn{l}", {"ffn_w_upT": dw_upT, "ffn_w_down": dw_down})
    dh = _mm(dau, wT, b_rows=(0, F), name=f"ffn{l}_up_u_dx")
    dx_in, dnw = _mm_norm_bwd(dag, wT, x, P["norm_ffn"][l:l + 1], dx, res=dh, b_rows=(F, F), after=tie,
                              name=f"ffn{l}_up_g_dx_norm_bwd")
    return dx_in, dnw, dcw, dcb


def _local_step(x, positions, target, P, need, emit, after=None):
    S, D = x.shape
    inv_freq = ROPE_THETA ** (-jnp.arange(0, HEAD_DIM, 2, dtype=F32) / HEAD_DIM)
    inv_freq = jnp.tile(inv_freq, LANES // (HEAD_DIM // 2)).reshape(1, LANES)
    cos, sin = _rope_tables(positions, inv_freq)

    nm0 = P["norm_mix"][0:1]
    h0 = _rmsnorm(x, nm0, name="mix_norm", after=after)
    need("mix_in", h0)
    proj0 = _mm(h0, P["mix_w_inT"], tb=True, name="mix_in")
    cat0 = _attn_fwd(proj0, cos, sin, P["attn_sinks"], _pool_fwd(proj0, P["pool_w"][0], P["pool_scale"]))
    need("mix_out", cat0)
    x1 = _mm(cat0, P["mix_w_out"], res=x, name="mix_out")
    x2, ffn0 = _ffn_fwd(x1, P, 0, need)

    nm1 = P["norm_mix"][1:2]
    h1 = _rmsnorm(x2, nm1, name="ssm_norm_in")
    need("ssm", h1)
    w1T, wdtT = P["ssm_w_inT"], P["ssm_wdtT"]
    DI, CD, NH = P["ssm_norm"].shape[1], P["ssm_conv_w"].shape[1], P["ssm_dt_bias"].shape[1]
    z = _mm(h1, w1T, tb=True, b_rows=(0, DI), name="ssm_in_z")
    xbcp = _mm(h1, w1T, tb=True, b_rows=(DI, CD), name="ssm_in_xbc")
    dtraw = _mm(h1, wdtT, tb=True, name="ssm_in_dt")
    xbc = _conv_silu_fwd(xbcp, P["ssm_conv_w"], P["ssm_conv_b"])
    bias_row = jnp.pad(P["ssm_dt_bias"], ((0, 0), (0, LANES - NH)))
    alog_row = jnp.pad(P["ssm_A_log"], ((0, 0), (0, LANES - NH)))
    d_g = _group_major(P["ssm_D"])
    pre_h, dt_h, dt_g, acs_g, acst_g = _ssd_prep_fwd(dtraw, bias_row, alog_row)
    y, states = _ssd_fwd(xbc, dt_g, acs_g, acst_g, d_g)
    yn = _gate_norm_fwd(y, z, P["ssm_norm"])
    need("ssm_out", yn)
    x3 = _mm(yn, P["ssm_w_out"], res=x2, name="ssm_out")
    x4, ffn1 = _ffn_fwd(x3, P, 1, need)

    loss, dx, d_norm_final = _final_loss(x4, P["norm_final"].reshape(1, D), target, name="final_loss")
    dx, dnf1, dcw1, dcb1 = _ffn_bwd(ffn1, P, 1, dx, emit)
    dyn = _mm(dx, P["ssm_w_out"], tb=True, out_dtype=BF16, name="ssm_out_dx")
    d_w_out1 = _mm(yn, dx, ta=True, out_dtype=PAYLOAD, name="ssm_out_dw")
    dy, dz, d_ssm_norm = _gate_norm_bwd(y, z, P["ssm_norm"], dyn)
    dxs, dB, dC, ddt_g, dacs_g, dacst_g, dd_g = _ssd_bwd(xbc, dt_g, acs_g, acst_g, d_g, states, dy)
    draw, dbias_row, dalog_row = _ssd_prep_bwd(pre_h, dt_h, alog_row, ddt_g, dacs_g, dacst_g)
    dxbc, d_conv_w1, d_conv_b1 = _conv_silu_bwd(xbcp, P["ssm_conv_w"], P["ssm_conv_b"], [dxs, dB, dC])
    rows = DI + CD + NH
    d_w1T = _mm(dz, h1, ta=True, out_dtype=PAYLOAD, out_rows=(rows, 0, None), name="ssm_in_z_dw")
    d_w1T = _mm(dxbc, h1, ta=True, out_dtype=PAYLOAD, out_rows=(rows, DI, d_w1T), name="ssm_in_xbc_dw")
    d_w1T = _mm(draw[:, :NH], h1, ta=True, out_dtype=PAYLOAD, out_rows=(rows, DI + CD, d_w1T), name="ssm_in_dt_dw")
    tie = emit("ssm", {"ssm_w_inT": d_w1T, "ssm_w_out": d_w_out1,
                       "ssm_conv_w": d_conv_w1, "ssm_conv_b": d_conv_b1, "ssm_norm": d_ssm_norm})
    dh1 = _mm(dz, w1T, b_rows=(0, DI), name="ssm_in_z_dx")
    dh1 = _mm(dxbc, w1T, b_rows=(DI, CD), res=dh1, name="ssm_in_xbc_dx")
    dx, dnm1 = _mm_norm_bwd(draw, wdtT, x2, nm1, dx, res=dh1, after=tie, name="ssm_in_dt_dx_norm_bwd")
    dx, dnf0, dcw0, dcb0 = _ffn_bwd(ffn0, P, 0, dx, emit)
    dcat = _mm(dx, P["mix_w_out"], tb=True, name="mix_out_dx")
    d_w_out0 = _mm(cat0, dx, ta=True, out_dtype=PAYLOAD, name="mix_out_dw")
    dproj0, dsk = _attn_bwd(proj0, cos, sin, P["attn_sinks"], dcat)
    dproj0, d_pool_w, d_pool_scale = _pool_bwd(proj0, P["pool_w"][0], P["pool_scale"], dcat, dproj0)
    d_w_in0 = _mm(dproj0, h0, ta=True, out_dtype=PAYLOAD, name="mix_in_dw")
    tie = emit("mix", {"mix_w_inT": d_w_in0, "mix_w_out": d_w_out0, "ffn_conv_w": jnp.stack([dcw0, dcw1])})
    grad_x, dnm0 = _mm_norm_bwd(dproj0, P["mix_w_inT"], x, nm0, dx, after=tie, name="mix_in_dx_norm_bwd")

    small = {
        "norm_mix": jnp.concatenate([dnm0, dnm1], axis=0),
        "norm_ffn": jnp.concatenate([dnf0, dnf1], axis=0),
        "norm_final": d_norm_final,
        "pool_w": d_pool_w,
        "pool_scale": d_pool_scale,
        "attn_sinks_rows": dsk,
        "ssm_dt_bias_row": dbias_row, "ssm_A_log_row": dalog_row, "ssm_D_g": dd_g,
        "ffn_conv_b": jnp.concatenate([dcb0, dcb1], axis=0),
    }
    return loss, grad_x, small


def _peer(k):
    x, y, c = lax.axis_index("x"), lax.axis_index("y"), lax.axis_index("c")
    px = 1 - x if k & 4 else x
    py = 1 - y if k & 2 else y
    pc = 1 - c if k & 1 else c
    return (px, py, pc), 4 * px + 2 * py + pc


def _my_index():
    return 4 * lax.axis_index("x") + 2 * lax.axis_index("y") + lax.axis_index("c")


def _land_sds(a, mode, gather):
    if mode == "slab":
        return _sds(((N_DEV,) + a.shape) if gather else a.shape, a.dtype)
    assert mode == "rows", mode
    return _sds((N_DEV * a.shape[0],) + a.shape[1:] if gather else (N_DEV, a.shape[0] // N_DEV) + a.shape[1:], a.dtype)


def _part(ref, mode, shape, idx):
    if mode == "slab":
        return ref.at[idx]
    r = shape[0] // N_DEV
    return ref.at[pl.ds(idx * r, r)]


def _remote_copies(ops, gather, srcs, lands, send_sems, recv_sems):
    me = _my_index()
    n = len(ops)
    out = []
    for k in range(1, N_DEV):
        dev, idx = _peer(k)
        for i, (a, mode) in enumerate(ops):
            s = srcs[i] if gather else _part(srcs[i], mode, a.shape, idx)
            d = _part(lands[i], mode, _land_sds(a, mode, gather).shape, me) if gather else lands[i].at[me]
            out.append(pltpu.make_async_remote_copy(src_ref=s, dst_ref=d, send_sem=send_sems.at[(k - 1) * n + i],
                                                    recv_sem=recv_sems.at[(k - 1) * n + i], device_id=dev,
                                                    device_id_type=pl.DeviceIdType.MESH))
    return out


HBM = pl.BlockSpec(memory_space=pltpu.HBM)
SEM = pl.BlockSpec(memory_space=pltpu.SEMAPHORE)
SIDE_EFFECT = pltpu.SideEffectType.DATAFLOW_SIDE_EFFECTING


def _in_hbm(a):
    return pltpu.with_memory_space_constraint(a, pltpu.HBM)


def _place_own(ops, *, gather, name):
    n = len(ops)

    def zeros(k):
        return (0,) * k

    in_specs, out_specs = [], []
    for a, mode in ops:
        nd = a.ndim
        if gather and mode == "slab":
            in_specs.append(pl.BlockSpec(a.shape, lambda i, nd=nd: zeros(nd)))
            out_specs.append(pl.BlockSpec((1,) + a.shape, lambda i, nd=nd: (_my_index(),) + zeros(nd)))
        elif gather:
            in_specs.append(pl.BlockSpec(a.shape, lambda i, nd=nd: zeros(nd)))
            out_specs.append(pl.BlockSpec(a.shape, lambda i, nd=nd: (_my_index(),) + zeros(nd - 1)))
        elif mode == "slab":
            in_specs.append(pl.BlockSpec((1,) + a.shape[1:], lambda i, nd=nd: (_my_index(),) + zeros(nd - 1)))
            out_specs.append(pl.BlockSpec((1,) + a.shape[1:], lambda i, nd=nd: (_my_index(),) + zeros(nd - 1)))
        else:
            r = a.shape[0] // N_DEV
            in_specs.append(pl.BlockSpec((r,) + a.shape[1:], lambda i, nd=nd: (_my_index(),) + zeros(nd - 1)))
            out_specs.append(pl.BlockSpec((1, r) + a.shape[1:], lambda i, nd=nd: (_my_index(),) + zeros(nd)))

    def body(*refs):
        for i_ref, o_ref in zip(refs[:n], refs[n:2 * n]):
            if o_ref.shape == i_ref.shape:
                o_ref[...] = i_ref[...]
            else:
                o_ref[0] = i_ref[...]

    outs = _call(body, name=name, grid=(1,), in_specs=in_specs, out_specs=out_specs + [ANY] * n,
                 out_shape=[_land_sds(a, m, gather) for a, m in ops] + [_sds(a.shape, a.dtype) for a, _ in ops],
                 aliases={i: n + i for i in range(n)})(*[a for a, _ in ops])
    return outs[:n], [(src, m) for src, (_, m) in zip(outs[n:], ops)]


def _exchange_start(groups, *, gather, name):
    sizes = [len(ops) for ops, _ in groups]
    n = sum(sizes)
    G = len(groups)

    def body(*refs):
        srcs, lands = refs[:n], refs[n:2 * n]
        sems = refs[2 * n:2 * n + 2 * G]
        token = refs[-1]
        off = 0
        for g, (ops, _) in enumerate(groups):
            for cp in _remote_copies(ops, gather, srcs[off:off + sizes[g]], lands[off:off + sizes[g]], sems[2 * g], sems[2 * g + 1]):
                cp.start()
            off += sizes[g]
        token[...] = jnp.zeros_like(token)

    srcs = [a for ops, _ in groups for a, _ in ops]
    lands = [l for _, ls in groups for l in ls]
    sem_shapes = [pltpu.SemaphoreType.DMA((s * (N_DEV - 1),)) for s in sizes for _ in range(2)]
    outs = pl.pallas_call(
        body, name=name,
        out_shape=sem_shapes + [pltpu.HBM(a.shape, a.dtype) for a in srcs + lands] + [_sds((8, LANES))],
        in_specs=[HBM] * (2 * n), out_specs=[SEM] * (2 * G) + [HBM] * (2 * n) + [pl.BlockSpec(memory_space=pltpu.VMEM)],
        input_output_aliases={i: 2 * G + i for i in range(2 * n)},
        compiler_params=pltpu.CompilerParams(has_side_effects=SIDE_EFFECT))(*[_in_hbm(a) for a in srcs + lands])
    sems, thru, token = outs[:2 * G], outs[2 * G:2 * G + 2 * n], outs[-1]
    states, off = [], 0
    for g, s in enumerate(sizes):
        states.append((sems[2 * g], sems[2 * g + 1], thru[off:off + s], thru[n + off:n + off + s]))
        off += s
    return states, token


def _exchange_wait(ops, state, after, *, gather, name):
    send_sems, recv_sems, srcs, lands = state
    n = len(ops)

    def body(*refs):
        for cp in _remote_copies(ops, gather, refs[:n], refs[n:2 * n], refs[2 * n], refs[2 * n + 1]):
            cp.wait_send()
            cp.wait_recv()

    outs = pl.pallas_call(
        body, name=name, out_shape=[pltpu.HBM(a.shape, a.dtype) for a in list(srcs) + list(lands)],
        in_specs=[HBM] * (2 * n) + [SEM, SEM, ANY], out_specs=[HBM] * (2 * n),
        input_output_aliases={i: i for i in range(2 * n)},
        compiler_params=pltpu.CompilerParams(has_side_effects=SIDE_EFFECT))(*srcs, *lands, send_sems, recv_sems, after)
    return outs[n:]


ADAM_ROWS = 256


def _row_tile(R, cap=ADAM_ROWS):
    best = R
    if R > cap:
        for d in range(16, cap + 1, 16):
            if R % d == 0:
                best = d
    return best


def _adamw(g_layers, w, m, v, *, name):
    L = len(g_layers)
    J, R, Wd = g_layers[0].shape
    assert w.shape == (L, R, Wd), (g_layers[0].shape, w.shape)
    tr = _row_tile(R)
    nrt = R // tr
    c1 = 1.0 / (1.0 - ADAM_B1 ** ADAM_STEP)
    c2 = 1.0 / (1.0 - ADAM_B2 ** ADAM_STEP)

    def body(*refs):
        g_refs = refs[:L]
        w_ref, m_ref, v_ref, go_ref, d_ref, mo_ref, vo_ref = refs[L:]
        layer = pl.program_id(0)
        g = None
        for l, g_ref in enumerate(g_refs):
            gl = g_ref[0].astype(F32)
            for j in range(1, J):
                gl = gl + g_ref[j].astype(F32)
            g = gl if g is None else jnp.where(layer == l, gl, g)
        mn = ADAM_B1 * m_ref[...] + (1.0 - ADAM_B1) * g
        vn = ADAM_B2 * v_ref[...] + (1.0 - ADAM_B2) * (g * g)
        go_ref[...] = g
        mo_ref[...] = mn
        vo_ref[...] = vn
        d_ref[...] = -ADAM_LR * ((mn * c1) / (jnp.sqrt(vn * c2) + ADAM_EPS) + ADAM_WD * w_ref[...])

    def g_spec(l):
        return pl.BlockSpec((J, tr, Wd), lambda ll, i: (0, jnp.where(ll == l, i, jnp.where(ll < l, 0, nrt - 1)), 0))

    row = pl.BlockSpec((None, tr, Wd), lambda ll, i: (ll, i, 0))
    out = _sds((L, R, Wd))
    return _call(body, name=name, out_shape=(out, out, out, out), grid=(L, nrt),
                 in_specs=[g_spec(l) for l in range(L)] + [row, row, row], out_specs=(row, row, row, row))(*g_layers, w, m, v)


def _sum_slabs(slabs, *, name):
    n = len(slabs)

    def body(*refs):
        for g_ref, o_ref in zip(refs[:n], refs[n:]):
            g = g_ref[0]
            for j in range(1, g_ref.shape[0]):
                g = g + g_ref[j]
            o_ref[...] = g

    return _call(body, name=name, out_shape=[_sds(s.shape[1:]) for s in slabs])(*slabs)


def kernel(x, positions, norm_mix, norm_ffn, norm_final, mix_w_in, pool_w, pool_scale, attn_sinks, mix_w_out, ssm_w_in, ssm_conv_w, ssm_conv_b, ssm_dt_bias, ssm_A_log, ssm_D, ssm_norm, ssm_w_out, ffn_w_up, ffn_conv_w, ffn_conv_b, ffn_w_down, loss_target, m_norm_mix, m_norm_ffn, m_norm_final, m_mix_w_in, m_pool_w, m_pool_scale, m_attn_sinks, m_mix_w_out, m_ssm_w_in, m_ssm_conv_w, m_ssm_conv_b, m_ssm_dt_bias, m_ssm_A_log, m_ssm_D, m_ssm_norm, m_ssm_w_out, m_ffn_w_up, m_ffn_conv_w, m_ffn_conv_b, m_ffn_w_down, v_norm_mix, v_norm_ffn, v_norm_final, v_mix_w_in, v_pool_w, v_pool_scale, v_attn_sinks, v_mix_w_out, v_ssm_w_in, v_ssm_conv_w, v_ssm_conv_b, v_ssm_dt_bias, v_ssm_A_log, v_ssm_D, v_ssm_norm, v_ssm_w_out, v_ffn_w_up, v_ffn_conv_w, v_ffn_conv_b, v_ffn_w_down):
    args = dict(locals())
    wl = {n: args[n] for n in WEIGHTS}
    ml = {n: args["m_" + n] for n in WEIGHTS}
    vl = {n: args["v_" + n] for n in WEIGHTS}
    F = ffn_w_down.shape[1] * N_DEV
    DI, CD, NH = ssm_norm.shape[1] * N_DEV, ssm_conv_b.shape[1] * N_DEV, ssm_dt_bias.shape[1]
    Kc, Kf = ssm_conv_w.shape[1], ffn_conv_w.shape[1]
    n_up = ffn_w_up.shape[2]
    col_sharded = ("mix_w_in", "ssm_w_in", "ffn_w_up")

    def tr(a):
        return jnp.swapaxes(a, -1, -2)

    def two(a):
        return a.reshape(-1, a.shape[-1])

    def pay(a):
        return a.astype(PAYLOAD)

    order = ("mix_in", "mix_out", "ffn0_up", "ffn0_down", "ssm", "ssm_out", "ffn1_up", "ffn1_down")
    gops = {
        "mix_in": [(pay(tr(mix_w_in)[0]), "rows")],
        "mix_out": [(pay(mix_w_out[0]), "rows"), (two(ssm_conv_w), "slab"), (ssm_conv_b, "slab"), (ssm_norm, "slab"),
                    (two(ffn_conv_w), "slab")],
        "ffn0_up": [(pay(tr(ffn_w_up)[0]), "rows")], "ffn0_down": [(pay(ffn_w_down[0]), "rows")],
        "ssm": [(pay(tr(ssm_w_in)[0]), "slab")], "ssm_out": [(pay(ssm_w_out[0]), "rows")],
        "ffn1_up": [(pay(tr(ffn_w_up)[1]), "rows")], "ffn1_down": [(pay(ffn_w_down[1]), "rows")],
    }
    lands, handed = _place_own([op for g in order for op in gops[g]], gather=True, name="gather_own")
    groups, off = [], 0
    for g in order:
        gops[g] = handed[off:off + len(gops[g])]
        groups.append((gops[g], lands[off:off + len(gops[g])]))
        off += len(gops[g])
    gstates, token = _exchange_start(groups, gather=True, name="gather_start")
    gstate = dict(zip(order, gstates))
    P = {n: wl[n] for n in REPLICATED}

    def need(g, after):
        got = _exchange_wait(gops[g], gstate[g], after, gather=True, name="gather_wait_" + g)
        if g == "mix_in":
            P["mix_w_inT"] = got[0]
        elif g == "mix_out":
            P.update(mix_w_out=got[0], ssm_conv_w=got[1].transpose(1, 0, 2).reshape(Kc, CD), ssm_conv_b=got[2].reshape(1, CD),
                     ssm_norm=got[3].reshape(1, DI), ffn_conv_w=got[4].transpose(1, 0, 2).reshape(2, Kf, 2 * F))
        elif g == "ssm":
            w1T = got[0].reshape(-1, got[0].shape[-1])
            P.update(ssm_w_inT=w1T, ssm_wdtT=jnp.pad(w1T[DI + CD:], ((0, LANES - NH), (0, 0))))
        elif g == "ssm_out":
            P["ssm_w_out"] = got[0]
        elif g.endswith("_up"):
            P["ffn_w_upT" + g[3]] = got[0]
        else:
            P["ffn_w_down" + g[3]] = got[0]

    sent = {}

    def emit(g, d):
        if g == "mix":
            ops = [(d["mix_w_inT"], "rows"), (d["mix_w_out"], "rows"),
                   (d["ffn_conv_w"].reshape(2 * Kf, N_DEV, n_up).transpose(1, 0, 2), "slab")]
        elif g == "ssm":
            ops = [(d["ssm_w_inT"].reshape(N_DEV, -1, d["ssm_w_inT"].shape[-1]), "slab"), (d["ssm_w_out"], "rows"),
                   (d["ssm_conv_w"].reshape(Kc, N_DEV, -1).transpose(1, 0, 2), "slab"),
                   (d["ssm_conv_b"].reshape(N_DEV, 1, -1), "slab"), (d["ssm_norm"].reshape(N_DEV, 1, -1), "slab")]
        else:
            ops = [(d["ffn_w_upT"], "rows"), (d["ffn_w_down"], "rows")]
        own, ops = _place_own(ops, gather=False, name="scatter_own_" + g)
        (state,), tok = _exchange_start([(ops, own)], gather=False, name="scatter_start_" + g)
        sent[g] = (ops, state)
        return tok

    loss_lanes, grad_x, G = _local_step(x[0], positions.reshape(-1, 1), loss_target[0], P, need, emit, after=token)

    res = {}

    def update(n, g_layers):
        L = len(g_layers)
        g_layers = [g.reshape(g.shape[0], -1, g.shape[-1]) for g in g_layers]
        shape = (L,) + g_layers[0].shape[1:]
        view = tr if n in col_sharded else (lambda a: a)
        outs = _adamw(g_layers, view(wl[n]).reshape(shape), view(ml[n]).reshape(shape), view(vl[n]).reshape(shape),
                      name="adamw_" + n)
        for kind, a in zip(("grad", "delta", "new_m", "new_v"), outs):
            res[kind, n] = view(a.reshape(view(wl[n]).shape))

    rep_ops = [(a, "slab") for a in (G["norm_mix"], G["norm_ffn"], G["norm_final"], G["pool_w"].reshape(-1, LANES),
                                     G["pool_scale"], G["ffn_conv_b"], G["attn_sinks_rows"],
                                     G["ssm_dt_bias_row"], G["ssm_A_log_row"], G["ssm_D_g"].reshape(SSM_G, LANES),
                                     loss_lanes)]
    rep_own, rep_ops = _place_own(rep_ops, gather=True, name="small_own")
    (rep_state,), rep_token = _exchange_start([(rep_ops, rep_own)], gather=True, name="small_start")

    recv = {g: _exchange_wait(sent[g][0], sent[g][1], rep_token, gather=False, name="scatter_wait_" + g)
            for g in ("ffn1", "ssm", "ffn0")}
    update("ssm_w_in", [recv["ssm"][0]])
    update("ssm_w_out", [recv["ssm"][1]])
    update("ssm_conv_w", [recv["ssm"][2]])
    update("ssm_conv_b", [recv["ssm"][3]])
    update("ssm_norm", [recv["ssm"][4]])
    update("ffn_w_up", [recv["ffn0"][0], recv["ffn1"][0]])
    update("ffn_w_down", [recv["ffn0"][1], recv["ffn1"][1]])
    recv["mix"] = _exchange_wait(sent["mix"][0], sent["mix"][1], res["new_v", "ffn_w_down"], gather=False,
                                 name="scatter_wait_mix")
    update("mix_w_in", [recv["mix"][0]])
    update("mix_w_out", [recv["mix"][1]])
    update("ffn_conv_w", [recv["mix"][2]])

    rep = _exchange_wait(rep_ops, rep_state, res["new_v", "mix_w_out"], gather=True, name="small_wait")
    for n, r in zip(("norm_mix", "norm_ffn", "norm_final", "pool_w", "pool_scale", "ffn_conv_b"), rep):
        update(n, [r])
    sinks_rows, bias_row, alog_row, d_g, loss_sum = _sum_slabs(rep[6:], name="sum_head_grads")
    update("attn_sinks", [sinks_rows[:, 0].reshape(1, 1, N_HEADS)])
    update("ssm_dt_bias", [bias_row[:, :NH][None]])
    update("ssm_A_log", [alog_row[:, :NH][None]])
    update("ssm_D", [_ungroup(d_g)[None]])
    loss = loss_sum[0, 0]

    return (loss, grad_x[None], *[res[k, n] for k in ("grad", "delta", "new_m", "new_v") for n in WEIGHTS])
```

```python
import functools
import math

import jax
import jax.numpy as jnp
from jax import lax
from jax.experimental import pallas as pl
from jax.experimental.pallas import tpu as pltpu

F32 = jnp.float32
BF16 = jnp.bfloat16

N_DEV = 8
LANES = 128
HEAD_DIM = 64
N_KV_HEADS = 2
GQ = 4
N_HEADS = N_KV_HEADS * GQ
BLOCK = 128
POOL_GROUPS = 4
ROPE_THETA = 10000.0
SSM_P = 64
SSM_G = 8
SSM_R = 4
SSM_N = 128
SSM_L = 128
NORM_EPS = 1e-6
SSM_NORM_EPS = 1e-5
ADAM_LR, ADAM_B1, ADAM_B2, ADAM_EPS, ADAM_WD, ADAM_STEP = 0.001, 0.9, 0.999, 1e-08, 0.01, 10
VMEM_LIMIT = 56 * 2 ** 20
PAYLOAD = jnp.bfloat16

REPLICATED = ("norm_mix", "norm_ffn", "norm_final", "pool_w", "pool_scale", "attn_sinks",
              "ssm_dt_bias", "ssm_A_log", "ssm_D", "ffn_conv_b")
WEIGHTS = ("norm_mix", "norm_ffn", "norm_final", "mix_w_in", "pool_w", "pool_scale", "attn_sinks", "mix_w_out",
           "ssm_w_in", "ssm_conv_w", "ssm_conv_b", "ssm_dt_bias", "ssm_A_log", "ssm_D", "ssm_norm", "ssm_w_out",
           "ffn_w_up", "ffn_conv_w", "ffn_conv_b", "ffn_w_down")


def _tile(n, cap):
    if n <= cap:
        return n
    best = None
    for d in range(LANES, cap + 1, LANES):
        if n % d == 0:
            best = d
    assert best is not None, (n, cap)
    return best


def _call(body, *, name, out_shape, grid=None, in_specs=None, out_specs=None, scratch=(), aliases=None):
    kw = {}
    if grid is not None:
        kw = dict(grid=grid, in_specs=in_specs, out_specs=out_specs)
    if aliases:
        kw["input_output_aliases"] = aliases
    return pl.pallas_call(
        body, name=name, out_shape=out_shape, scratch_shapes=list(scratch),
        compiler_params=pltpu.CompilerParams(vmem_limit_bytes=VMEM_LIMIT), **kw)


ANY = pl.BlockSpec(memory_space=pl.ANY)


def _sds(shape, dtype=F32):
    return jax.ShapeDtypeStruct(tuple(shape), dtype)


def _sigmoid(x):
    return 1.0 / (1.0 + jnp.exp(-x))


def _shift_dn(x, d, t):
    if d == 0:
        return x
    return jnp.where(t >= d, pltpu.roll(x, d, axis=0), 0.0)


def _shift_up(x, d, t):
    if d == 0:
        return x
    n = x.shape[0]
    return jnp.where(t < n - d, pltpu.roll(x, n - d, axis=0), 0.0)


def _mm(a, b, *, name, ta=False, tb=False, res=None, out_dtype=F32, b_rows=None, out_rows=None):
    M, K = (a.shape[1], a.shape[0]) if ta else a.shape
    b0, bn = b_rows if b_rows is not None else (0, b.shape[0])
    N = bn if tb else b.shape[1]
    assert (b.shape[1] if tb else bn) == K, (a.shape, b.shape, ta, tb, b_rows)
    tm, tn, tk = _tile(M, 1408), _tile(N, 1408), _tile(K, 1408)
    nk = K // tk
    dims = (((0 if ta else 1,), (1 if tb else 0,)), ((), ()))
    aliased = out_rows is not None and out_rows[2] is not None

    def body(*refs):
        a_ref, b_ref = refs[:2]
        r_ref = refs[2] if res is not None else None
        o_ref, acc = refs[-2:]
        k = pl.program_id(2)

        @pl.when(k == 0)
        def _():
            acc[...] = jnp.zeros_like(acc)

        acc[...] += lax.dot_general(a_ref[...].astype(BF16), b_ref[...].astype(BF16), dims,
                                    preferred_element_type=F32)

        @pl.when(k == nk - 1)
        def _():
            out = acc[...]
            if res is not None:
                out = out + r_ref[...]
            o_ref[...] = out.astype(out_dtype)

    a_spec = pl.BlockSpec((tk, tm), lambda i, j, k: (k, i)) if ta else pl.BlockSpec((tm, tk), lambda i, j, k: (i, k))
    if tb:
        assert b0 % tn == 0, (b_rows, tn)
        b_spec = pl.BlockSpec((tn, tk), lambda i, j, k: (b0 // tn + j, k))
    else:
        assert b0 % tk == 0, (b_rows, tk)
        b_spec = pl.BlockSpec((tk, tn), lambda i, j, k: (b0 // tk + k, j))
    ins, specs = [a, b], [a_spec, b_spec]
    if res is not None:
        ins.append(res)
        specs.append(pl.BlockSpec((tm, tn), lambda i, j, k: (i, j)))
    aliases = None
    if out_rows is None:
        o_spec = pl.BlockSpec((tm, tn), lambda i, j, k: (i, j))
        out_shape = _sds((M, N), out_dtype)
    else:
        total, o0, prev = out_rows
        assert o0 % tm == 0, (out_rows, tm)
        o_spec = pl.BlockSpec((tm, tn), lambda i, j, k: (o0 // tm + i, j))
        out_shape = _sds((total, N), out_dtype)
        if aliased:
            aliases = {len(ins): 0}
            ins.append(prev)
            specs.append(ANY)
    return _call(body, name=name, out_shape=out_shape, grid=(M // tm, N // tn, nk), in_specs=specs,
                 out_specs=o_spec, scratch=[pltpu.VMEM((tm, tn), F32)], aliases=aliases)(*ins)


def _rmsnorm(x, w, *, name, eps=NORM_EPS, after=None):
    S, D = x.shape
    tm = _tile(S, 512)
    tie = [] if after is None else [after]

    def body(x_ref, w_ref, *rest):
        o_ref = rest[-1]
        xf = x_ref[...]
        r = lax.rsqrt(jnp.mean(xf * xf, axis=-1, keepdims=True) + eps)
        o_ref[...] = (xf * r * w_ref[...]).astype(BF16)

    return _call(body, name=name, out_shape=_sds((S, D), BF16), grid=(S // tm,),
                 in_specs=[pl.BlockSpec((tm, D), lambda i: (i, 0)), pl.BlockSpec((1, D), lambda i: (0, 0))] + [ANY] * len(tie),
                 out_specs=pl.BlockSpec((tm, D), lambda i: (i, 0)))(x, w, *tie)


def _norm_bwd_math(xf, w, dh, eps):
    r = lax.rsqrt(jnp.mean(xf * xf, axis=-1, keepdims=True) + eps)
    xhat = xf * r
    dxh = dh * w
    dx = r * (dxh - xhat * jnp.mean(dxh * xhat, axis=-1, keepdims=True))
    dw = jnp.sum(dh * xhat, axis=0, keepdims=True)
    return dx, dw


def _mm_norm_bwd(a, b, x, w, dres, *, name, res=None, b_rows=None, after=None, eps=NORM_EPS):
    M, K = a.shape
    b0, bn = b_rows if b_rows is not None else (0, b.shape[0])
    D = b.shape[1]
    assert bn == K and x.shape == (M, D), (a.shape, b.shape, b_rows, x.shape)
    tm, tk = _tile(M, 1024), _tile(K, 1408)
    assert b0 % tk == 0, (b_rows, tk)
    nk = K // tk

    def body(*refs):
        a_ref, b_ref, x_ref, w_ref, dr_ref = refs[:5]
        r_ref = refs[5] if res is not None else None
        dx_ref, dw_ref, acc = refs[-3:]
        i, k = pl.program_id(0), pl.program_id(1)

        @pl.when(k == 0)
        def _():
            acc[...] = jnp.zeros_like(acc)

        @pl.when((i == 0) & (k == 0))
        def _():
            dw_ref[...] = jnp.zeros_like(dw_ref)

        acc[...] += jnp.dot(a_ref[...].astype(BF16), b_ref[...].astype(BF16), preferred_element_type=F32)

        @pl.when(k == nk - 1)
        def _():
            dh = acc[...] if res is None else acc[...] + r_ref[...]
            dx, dw = _norm_bwd_math(x_ref[...], w_ref[...], dh, eps)
            dx_ref[...] = dr_ref[...] + dx
            dw_ref[...] += dw

    row = pl.BlockSpec((tm, D), lambda i, k: (i, 0))
    vec = pl.BlockSpec((1, D), lambda i, k: (0, 0))
    ins = [a, b, x, w, dres]
    specs = [pl.BlockSpec((tm, tk), lambda i, k: (i, k)), pl.BlockSpec((tk, D), lambda i, k: (b0 // tk + k, 0)), row, vec, row]
    if res is not None:
        ins.append(res)
        specs.append(row)
    if after is not None:
        ins.append(after)
        specs.append(ANY)
    return _call(body, name=name, out_shape=(_sds((M, D)), _sds((1, D))), grid=(M // tm, nk), in_specs=specs,
                 out_specs=(row, vec), scratch=[pltpu.VMEM((tm, D), F32)])(*ins)


def _final_loss(x, w, target, *, name):
    S, D = x.shape
    tm = _tile(S, 512)

    def body(x_ref, w_ref, t_ref, loss_ref, dx_ref, dw_ref):
        xf, wv = x_ref[...], w_ref[...]
        r = lax.rsqrt(jnp.mean(xf * xf, axis=-1, keepdims=True) + NORM_EPS)
        err = xf * r * wv - t_ref[...]
        part = 0.5 * jnp.sum(jnp.mean(err * err, axis=-1, keepdims=True), axis=0, keepdims=True)
        dx, dw = _norm_bwd_math(xf, wv, err * (1.0 / D), NORM_EPS)
        dx_ref[...] = dx

        @pl.when(pl.program_id(0) == 0)
        def _():
            dw_ref[...] = jnp.zeros_like(dw_ref)
            loss_ref[...] = jnp.zeros_like(loss_ref)

        dw_ref[...] += dw
        loss_ref[...] += jnp.broadcast_to(part, loss_ref.shape)

    row = pl.BlockSpec((tm, D), lambda i: (i, 0))
    vec = pl.BlockSpec((1, D), lambda i: (0, 0))
    return _call(body, name=name, out_shape=(_sds((1, LANES)), _sds((S, D)), _sds((1, D))), grid=(S // tm,),
                 in_specs=[row, vec, row], out_specs=(pl.BlockSpec((1, LANES), lambda i: (0, 0)), row, vec))(x, w, target)


def _rope_tables(pos, inv_freq):
    S = pos.shape[0]
    tm = _tile(S, 512)

    def body(p_ref, f_ref, c_ref, s_ref):
        ang = p_ref[...].astype(F32) * f_ref[...]
        c_ref[...] = jnp.cos(ang)
        s_ref[...] = jnp.sin(ang)

    blk = pl.BlockSpec((tm, LANES), lambda i: (i, 0))
    return _call(body, name="rope_tables", out_shape=(_sds((S, LANES)), _sds((S, LANES))), grid=(S // tm,),
                 in_specs=[pl.BlockSpec((tm, 1), lambda i: (i, 0)), pl.BlockSpec((1, LANES), lambda i: (0, 0))],
                 out_specs=(blk, blk))(pos, inv_freq)


def _rot_half(t):
    lane = lax.broadcasted_iota(jnp.int32, t.shape, 1)
    lo = (lane % HEAD_DIM) < (HEAD_DIM // 2)
    return jnp.where(lo, -pltpu.roll(t, LANES - HEAD_DIM // 2, axis=1), pltpu.roll(t, HEAD_DIM // 2, axis=1))


def _rope(t, c, s):
    return t * c + _rot_half(t) * s


def _unrope(dy, c, s):
    return dy * c - _rot_half(dy * s)


PD = POOL_GROUPS * LANES
QD = N_HEADS * HEAD_DIM
KD = N_KV_HEADS * HEAD_DIM
assert PD % QD == 0 and (PD + QD) % (2 * KD) == 0 and KD == LANES
def _attn_probs(q, kcat, sink, mask):
    s = lax.dot_general(q.astype(BF16), kcat, (((1,), (1,)), ((), ())), preferred_element_type=F32) * (HEAD_DIM ** -0.5)
    s = jnp.where(mask, s, -jnp.inf)
    m = jnp.maximum(jnp.max(s, axis=1, keepdims=True), sink)
    p = jnp.exp(s - m)
    ps = jnp.exp(sink - m)
    inv = 1.0 / (jnp.sum(p, axis=1, keepdims=True) + ps)
    return p * inv, ps * inv


def _attn_mask(n):
    qi = lax.broadcasted_iota(jnp.int32, (BLOCK, 2 * BLOCK), 0)
    kj = lax.broadcasted_iota(jnp.int32, (BLOCK, 2 * BLOCK), 1)
    rel = qi + BLOCK - kj
    return (rel >= 0) & (rel < BLOCK) & ((n > 0) | (kj >= BLOCK))


def _attn_in_specs(nb):
    def cur(n):
        return jnp.minimum(n, nb - 1)

    def prev(n):
        return jnp.clip(n - 1, 0, nb - 1)

    kvb = (PD + QD) // (2 * KD)
    return [pl.BlockSpec(memory_space=pltpu.SMEM),
            pl.BlockSpec((BLOCK, QD), lambda n: (cur(n), PD // QD)),
            pl.BlockSpec((BLOCK, 2 * KD), lambda n: (cur(n), kvb)),
            pl.BlockSpec((BLOCK, 2 * KD), lambda n: (prev(n), kvb)),
            pl.BlockSpec((BLOCK, LANES), lambda n: (cur(n), 0)), pl.BlockSpec((BLOCK, LANES), lambda n: (cur(n), 0)),
            pl.BlockSpec((BLOCK, LANES), lambda n: (prev(n), 0)), pl.BlockSpec((BLOCK, LANES), lambda n: (prev(n), 0))]


def _attn_keys(kvc_ref, kvp_ref, cc, sc, cp, sp):
    kc = _rope(kvc_ref[:, :KD], cc, sc)
    kp = _rope(kvp_ref[:, :KD], cp, sp)
    vc, vp = kvc_ref[:, KD:], kvp_ref[:, KD:]
    kcat, vcat = [], []
    for kk in range(N_KV_HEADS):
        sl = slice(kk * HEAD_DIM, (kk + 1) * HEAD_DIM)
        kcat.append(jnp.concatenate([kp[:, sl], kc[:, sl]], axis=0).astype(BF16))
        vcat.append(jnp.concatenate([vp[:, sl], vc[:, sl]], axis=0).astype(BF16))
    return kcat, vcat


def _attn_fwd(proj, cos, sin, sinks, cat):
    S = proj.shape[0]
    nb = S // BLOCK

    def body(sink_ref, q_ref, kvc_ref, kvp_ref, cc_ref, sc_ref, cp_ref, sp_ref, cat_ref, o_ref):
        n = pl.program_id(0)
        cc, sc = cc_ref[...], sc_ref[...]
        kcat, vcat = _attn_keys(kvc_ref, kvp_ref, cc, sc, cp_ref[...], sp_ref[...])
        mask = _attn_mask(n)
        def head_pair(j):
            qr = _rope(q_ref[:, j * LANES:(j + 1) * LANES], cc, sc)
            for e in range(LANES // HEAD_DIM):
                yield
                h = j * (LANES // HEAD_DIM) + e
                pn, _ = _attn_probs(qr[:, e * HEAD_DIM:(e + 1) * HEAD_DIM], kcat[h // GQ], sink_ref[0, h], mask)
                yield
                o_ref[:, h * HEAD_DIM:(h + 1) * HEAD_DIM] = jnp.dot(
                    pn.astype(BF16), vcat[h // GQ], preferred_element_type=F32).astype(o_ref.dtype)

        _interleave([head_pair(j) for j in range(QD // LANES)])

    return _call(body, name="attn_fwd", out_shape=_sds(cat.shape, cat.dtype), grid=(nb,),
                 in_specs=_attn_in_specs(nb) + [ANY], out_specs=pl.BlockSpec((BLOCK, QD), lambda n: (n, PD // QD)),
                 aliases={8: 0})(sinks, proj, proj, proj, cos, sin, cos, sin, cat)


def _attn_bwd(proj, cos, sin, sinks, dcat):
    S = proj.shape[0]
    nb = S // BLOCK
    scale = HEAD_DIM ** -0.5
    per = LANES // HEAD_DIM

    def body(sink_ref, q_ref, kvc_ref, kvp_ref, cc_ref, sc_ref, cp_ref, sp_ref, do_ref, o_ref, ds_ref, hold, carry, part, pair):
        n = pl.program_id(0)

        @pl.when(n == 0)
        def _():
            hold[...] = jnp.zeros_like(hold)
            carry[...] = jnp.zeros_like(carry)
            ds_ref[...] = jnp.zeros_like(ds_ref)

        live = jnp.where(n < nb, 1.0, 0.0)
        cc, sc, cp, sp = cc_ref[...], sc_ref[...], cp_ref[...], sp_ref[...]
        kcat, vcat = _attn_keys(kvc_ref, kvp_ref, cc, sc, cp, sp)
        mask = _attn_mask(n)
        o_ref[:, :PD] = jnp.zeros((BLOCK, PD), F32)
        o_ref[:, PD:PD + QD] = hold[...]
        dk = [jnp.zeros((2 * BLOCK, HEAD_DIM), F32) for _ in range(N_KV_HEADS)]
        dv = [jnp.zeros((2 * BLOCK, HEAD_DIM), F32) for _ in range(N_KV_HEADS)]
        row = lax.broadcasted_iota(jnp.int32, (8, LANES), 0)
        acc = {"dsk": jnp.zeros((8, LANES), F32)}

        def head_pair(j):
            qr = _rope(q_ref[:, j * LANES:(j + 1) * LANES], cc, sc)
            for e in range(per):
                yield
                h = j * per + e
                kk = h // GQ
                qh = qr[:, e * HEAD_DIM:(e + 1) * HEAD_DIM]
                pn, psn = _attn_probs(qh, kcat[kk], sink_ref[0, h], mask)
                yield
                doh = (do_ref[:, h * HEAD_DIM:(h + 1) * HEAD_DIM] * live).astype(BF16)
                dp = lax.dot_general(doh, vcat[kk], NT, preferred_element_type=F32)
                yield
                delta = jnp.sum(pn * dp, axis=1, keepdims=True)
                ds = (pn * (dp - delta) * scale).astype(BF16)
                pair[j, :, e * HEAD_DIM:(e + 1) * HEAD_DIM] = jnp.dot(ds, kcat[kk], preferred_element_type=F32)
                yield
                dk[kk] = dk[kk] + lax.dot_general(ds, qh.astype(BF16), TN, preferred_element_type=F32)
                dv[kk] = dv[kk] + lax.dot_general(pn.astype(BF16), doh, TN, preferred_element_type=F32)
                acc["dsk"] = acc["dsk"] + jnp.where(row == h, -jnp.sum(psn * delta), 0.0)
            yield
            hold[:, j * LANES:(j + 1) * LANES] = _unrope(pair[j], cc, sc)

        _interleave([head_pair(j) for j in range(QD // LANES)])
        dsk = acc["dsk"]
        for kk in range(N_KV_HEADS):
            sl = slice(kk * HEAD_DIM, (kk + 1) * HEAD_DIM)
            sv = slice(KD + kk * HEAD_DIM, KD + (kk + 1) * HEAD_DIM)
            part[0, :, sl] = dk[kk][:BLOCK]
            part[0, :, sv] = dv[kk][:BLOCK]
            part[1, :, sl] = dk[kk][BLOCK:]
            part[1, :, sv] = dv[kk][BLOCK:]
        done = carry[...] + part[0]
        o_ref[:, PD + QD:PD + QD + KD] = _unrope(done[:, :KD], cp, sp)
        o_ref[:, PD + QD + KD:] = done[:, KD:]
        carry[...] = part[1]
        ds_ref[...] += dsk

    return _call(body, name="attn_bwd", out_shape=(_sds((S, PD + QD + 2 * KD)), _sds((8, LANES))), grid=(nb + 1,),
                 in_specs=_attn_in_specs(nb) + [pl.BlockSpec((BLOCK, QD), lambda n: (jnp.minimum(n, nb - 1), PD // QD))],
                 out_specs=(pl.BlockSpec((BLOCK, PD + QD + 2 * KD), lambda n: (jnp.maximum(n - 1, 0), 0)),
                            pl.BlockSpec((8, LANES), lambda n: (0, 0))),
                 scratch=[pltpu.VMEM((BLOCK, QD), F32), pltpu.VMEM((BLOCK, 2 * KD), F32),
                          pltpu.VMEM((2, BLOCK, 2 * KD), F32), pltpu.VMEM((QD // LANES, BLOCK, LANES), F32)])(
                     sinks, proj, proj, proj, cos, sin, cos, sin, dcat)


def _pool_sums(u, g, t, shift):
    s2 = u + shift(u, 1, t)
    s4 = s2 + shift(s2, 2, t)
    s8 = s4 + shift(s4, 4, t)
    s16 = s8 + shift(s8, 8, t)
    return jnp.where(g == 0, s2, jnp.where(g == 1, s4, jnp.where(g == 2, s8, s16)))


def _pool_specs(S):
    col = pl.BlockSpec((S, LANES), lambda g: (0, g))
    wsp = pl.BlockSpec((1, LANES, LANES), lambda g: (g, 0, 0))
    vec = pl.BlockSpec((1, LANES), lambda g: (0, g))
    return col, wsp, vec


def _pool_fwd(proj, pool_w, scale):
    S = proj.shape[0]
    col, wsp, vec = _pool_specs(S)

    def body(u_ref, w_ref, s_ref, o_ref):
        g = pl.program_id(0)
        u = u_ref[...]
        t = lax.broadcasted_iota(jnp.int32, u.shape, 0)
        cnt = jnp.minimum(t + 1, 2 << g).astype(F32)
        pm = _pool_sums(u, g, t, _shift_dn) / cnt - u
        o_ref[...] = (jnp.dot(pm.astype(BF16), w_ref[0].astype(BF16), preferred_element_type=F32) * s_ref[...]).astype(BF16)

    return _call(body, name="pool_fwd", out_shape=_sds((S, PD + QD), BF16), grid=(POOL_GROUPS,),
                 in_specs=[col, wsp, vec], out_specs=col)(proj, pool_w, scale)


def _pool_bwd(proj, pool_w, scale, dcat, dproj):
    S = proj.shape[0]
    col, wsp, vec = _pool_specs(S)

    def body(u_ref, w_ref, s_ref, d_ref, dproj_ref, du_ref, dw_ref, dsc_ref):
        g = pl.program_id(0)
        u = u_ref[...]
        t = lax.broadcasted_iota(jnp.int32, u.shape, 0)
        cnt = jnp.minimum(t + 1, 2 << g).astype(F32)
        pm = (_pool_sums(u, g, t, _shift_dn) / cnt - u).astype(BF16)
        wv = w_ref[0].astype(BF16)
        d = d_ref[...]
        pw = jnp.dot(pm, wv, preferred_element_type=F32)
        dsc_ref[...] = jnp.sum(pw * d, axis=0, keepdims=True)
        dpw = (d * s_ref[...]).astype(BF16)
        dw_ref[0] = lax.dot_general(pm, dpw, (((0,), (0,)), ((), ())), preferred_element_type=F32)
        dpm = lax.dot_general(dpw, wv, (((1,), (1,)), ((), ())), preferred_element_type=F32)
        du_ref[...] = _pool_sums(dpm / cnt, g, t, _shift_up) - dpm

    return _call(body, name="pool_bwd",
                 out_shape=(_sds(dproj.shape), _sds((POOL_GROUPS, LANES, LANES)), _sds((1, POOL_GROUPS * LANES))),
                 grid=(POOL_GROUPS,), in_specs=[col, wsp, vec, col, ANY], out_specs=(col, wsp, vec),
                 aliases={4: 0})(proj, pool_w, scale, dcat, dproj)


def _conv(x, w_ref, b_ref, t):
    K = w_ref.shape[0]
    y = b_ref[...] + jnp.zeros_like(x)
    for k in range(K):
        y = y + w_ref[k:k + 1, :] * _shift_dn(x, K - 1 - k, t)
    return y


def _silu_grad(y):
    sg = _sigmoid(y)
    return sg * (1.0 + y * (1.0 - sg))


CONV_ROWS = 256
HALO = 8


def _win_above(ref, r0):
    if isinstance(r0, int):
        assert r0 == 0
        return jnp.concatenate([jnp.zeros((HALO, ref.shape[1]), F32), ref[0:CONV_ROWS, :]], axis=0)
    return ref[pl.ds(pl.multiple_of(r0 - HALO, HALO), CONV_ROWS + HALO), :]


def _rows_at(win, start):
    if start % 8 == 0:
        return win[start:start + CONV_ROWS]
    base = start // 8 * 8
    return pltpu.roll(win, win.shape[0] - (start - base), axis=0)[base:base + CONV_ROWS]


def _taps_above(win, K):
    return [_rows_at(win, HALO - (K - 1 - k)) for k in range(K)]


def _conv_taps(taps, w, b):
    y = b
    for k in range(len(w)):
        y = y + w[k] * taps[k]
    return y


def _conv_t_win(win, w):
    K = len(w)
    out = None
    for k in range(K):
        d = K - 1 - k
        term = w[k] * _rows_at(win, d)
        out = term if out is None else out + term
    return out


def _fold8(x):
    return jnp.sum(x.reshape(CONV_ROWS // 8, 8, x.shape[-1]), axis=0)


def _chunk_loop(S, step, init):
    carry = step(0, init)
    return lax.fori_loop(1, S // CONV_ROWS, lambda i, c: step(pl.multiple_of(i * CONV_ROWS, CONV_ROWS), c), carry)


def _ffn_mid_specs(S, K, layer, nf):
    return [pl.BlockSpec((S, LANES), lambda j: (0, j)), pl.BlockSpec((S, LANES), lambda j: (0, nf + j)),
            pl.BlockSpec((None, K, LANES), lambda j: (layer, 0, j)), pl.BlockSpec((None, K, LANES), lambda j: (layer, 0, nf + j)),
            pl.BlockSpec((None, 1, LANES), lambda j: (layer, 0, j)), pl.BlockSpec((None, 1, LANES), lambda j: (layer, 0, nf + j))]


def _ffn_mid_fwd(a, cw, cb, layer):
    S, F = a.shape[0], a.shape[1] // 2
    nf = F // LANES
    K = cw.shape[1]

    def body(au_ref, ag_ref, wu_ref, wg_ref, bu_ref, bg_ref, o_ref):
        t = lax.broadcasted_iota(jnp.int32, (S, LANES), 0)
        hu = _conv(au_ref[...], wu_ref, bu_ref, t)
        hg = _conv(ag_ref[...], wg_ref, bg_ref, t)
        o_ref[...] = (hg * _sigmoid(hg) * hu).astype(BF16)

    return _call(body, name="ffn_mid_fwd", out_shape=_sds((S, F), BF16), grid=(nf,),
                 in_specs=_ffn_mid_specs(S, K, layer, nf), out_specs=pl.BlockSpec((S, LANES), lambda j: (0, j)))(
                     a, a, cw, cw, cb[:, None], cb[:, None])


def _ffn_mid_bwd(a, cw, cb, layer, dact):
    S, F = a.shape[0], a.shape[1] // 2
    nf = F // LANES
    K = cw.shape[1]

    def body(au_ref, ag_ref, wu_ref, wg_ref, bu_ref, bg_ref, d_ref, dau_ref, dag_ref, dwu_ref, dwg_ref, dbu_ref, dbg_ref,
             dhu_s, dhg_s):
        T = CONV_ROWS
        wu = [wu_ref[k:k + 1, :] for k in range(K)]
        wg = [wg_ref[k:k + 1, :] for k in range(K)]
        bu, bg = bu_ref[...], bg_ref[...]
        zero8 = jnp.zeros((HALO, LANES), F32)
        dhu_s[S:S + HALO, :] = zero8
        dhg_s[S:S + HALO, :] = zero8

        def first_pass(r0, acc):
            tu, tg = _taps_above(_win_above(au_ref, r0), K), _taps_above(_win_above(ag_ref, r0), K)
            hu, hg = _conv_taps(tu, wu, bu), _conv_taps(tg, wg, bg)
            d = d_ref[pl.ds(r0, T), :].astype(F32)
            sg = _sigmoid(hg)
            dhu = d * hg * sg
            dhg = d * hu * (sg * (1.0 + hg * (1.0 - sg)))
            dhu_s[pl.ds(r0, T), :] = dhu
            dhg_s[pl.ds(r0, T), :] = dhg
            new = []
            for dh, taps in ((dhu, tu), (dhg, tg)):
                for k in range(K):
                    new.append(acc[len(new)] + _fold8(dh * taps[k]))
            new.append(acc[2 * K] + _fold8(dhu))
            new.append(acc[2 * K + 1] + _fold8(dhg))
            return tuple(new)

        acc = _chunk_loop(S, first_pass, tuple(zero8 for _ in range(2 * K + 2)))
        for k in range(K):
            dwu_ref[k:k + 1, :] = jnp.sum(acc[k], axis=0, keepdims=True)
            dwg_ref[k:k + 1, :] = jnp.sum(acc[K + k], axis=0, keepdims=True)
        dbu_ref[...] = jnp.sum(acc[2 * K], axis=0, keepdims=True)
        dbg_ref[...] = jnp.sum(acc[2 * K + 1], axis=0, keepdims=True)

        def second_pass(i, carry):
            r0 = pl.multiple_of(i * T, T)
            dau_ref[pl.ds(r0, T), :] = _conv_t_win(dhu_s[pl.ds(r0, T + HALO), :], wu).astype(BF16)
            dag_ref[pl.ds(r0, T), :] = _conv_t_win(dhg_s[pl.ds(r0, T + HALO), :], wg).astype(BF16)
            return carry

        lax.fori_loop(0, S // T, second_pass, 0)

    col = pl.BlockSpec((S, LANES), lambda j: (0, j))
    wsp = pl.BlockSpec((K, LANES), lambda j: (0, j))
    bsp = pl.BlockSpec((1, LANES), lambda j: (0, j))
    dau, dag, dwu, dwg, dbu, dbg = _call(
        body, name="ffn_mid_bwd",
        out_shape=(_sds((S, F), BF16), _sds((S, F), BF16), _sds((K, F)), _sds((K, F)), _sds((1, F)), _sds((1, F))), grid=(nf,),
        in_specs=_ffn_mid_specs(S, K, layer, nf) + [col], out_specs=(col, col, wsp, wsp, bsp, bsp),
        scratch=[pltpu.VMEM((S + HALO, LANES), F32), pltpu.VMEM((S + HALO, LANES), F32)])(
            a, a, cw, cw, cb[:, None], cb[:, None], dact)
    return dau, dag, jnp.concatenate([dwu, dwg], axis=1), jnp.concatenate([dbu, dbg], axis=1)


def _conv_silu_fwd(x, cw, cb):
    S = x.shape[0]
    K, C = cw.shape

    def body(x_ref, w_ref, b_ref, o_ref):
        t = lax.broadcasted_iota(jnp.int32, (S, LANES), 0)
        y = _conv(x_ref[...], w_ref, b_ref, t)
        o_ref[...] = y * _sigmoid(y)

    col = pl.BlockSpec((S, LANES), lambda j: (0, j))
    return _call(body, name="conv_silu_fwd", out_shape=_sds((S, C)), grid=(C // LANES,),
                 in_specs=[col, pl.BlockSpec((K, LANES), lambda j: (0, j)), pl.BlockSpec((1, LANES), lambda j: (0, j))],
                 out_specs=col)(x, cw, cb)


def _conv_silu_bwd(x, cw, cb, douts):
    S = x.shape[0]
    K, C = cw.shape
    starts, off = [], 0
    for d in douts:
        starts.append(off)
        off += d.shape[1] // LANES
    assert off == C // LANES

    def body(x_ref, w_ref, b_ref, *rest):
        dy_s = rest[-1]
        d_refs, (dx_ref, dw_ref, db_ref) = rest[:len(douts)], rest[len(douts):-1]
        j = pl.program_id(0)
        T = CONV_ROWS
        w = [w_ref[k:k + 1, :] for k in range(K)]
        b = b_ref[...]
        zero8 = jnp.zeros((HALO, LANES), F32)
        dy_s[S:S + HALO, :] = zero8

        def first_pass(r0, acc):
            taps = _taps_above(_win_above(x_ref, r0), K)
            y = _conv_taps(taps, w, b)
            d = d_refs[0][pl.ds(r0, T), :]
            for i in range(1, len(douts)):
                d = jnp.where(j >= starts[i], d_refs[i][pl.ds(r0, T), :], d)
            dy = d * _silu_grad(y)
            dy_s[pl.ds(r0, T), :] = dy
            return tuple(acc[k] + _fold8(dy * taps[k]) for k in range(K)) + (acc[K] + _fold8(dy),)

        acc = _chunk_loop(S, first_pass, tuple(zero8 for _ in range(K + 1)))
        for k in range(K):
            dw_ref[k:k + 1, :] = jnp.sum(acc[k], axis=0, keepdims=True)
        db_ref[...] = jnp.sum(acc[K], axis=0, keepdims=True)

        def second_pass(i, carry):
            r0 = pl.multiple_of(i * T, T)
            dx_ref[pl.ds(r0, T), :] = _conv_t_win(dy_s[pl.ds(r0, T + HALO), :], w).astype(BF16)
            return carry

        lax.fori_loop(0, S // T, second_pass, 0)

    col = pl.BlockSpec((S, LANES), lambda j: (0, j))
    wsp = pl.BlockSpec((K, LANES), lambda j: (0, j))
    bsp = pl.BlockSpec((1, LANES), lambda j: (0, j))

    def dspec(i):
        nblk = douts[i].shape[1] // LANES
        return pl.BlockSpec((S, LANES), lambda j: (0, jnp.clip(j - starts[i], 0, nblk - 1)))

    return _call(body, name="conv_silu_bwd", out_shape=(_sds((S, C), BF16), _sds((K, C)), _sds((1, C))), grid=(C // LANES,),
                 in_specs=[col, wsp, bsp] + [dspec(i) for i in range(len(douts))],
                 out_specs=(col, wsp, bsp), scratch=[pltpu.VMEM((S + HALO, LANES), F32)])(x, cw, cb, *douts)


HI = lax.Precision.HIGHEST


PREP_CHUNKS = 4


def _to_group(x, g, live):
    return jnp.where(live, x if g == 0 else pltpu.roll(x, LANES - SSM_R * g, axis=1), 0.0)


def _from_groups(refs_g):
    out = refs_g[0]
    for g in range(1, SSM_G):
        out = out + pltpu.roll(refs_g[g], SSM_R * g, axis=1)
    return out


def _ssd_prep_fwd(raw, bias_row, alog_row):
    S = raw.shape[0]
    nc = S // SSM_L
    NH = SSM_G * SSM_R

    def body(raw_ref, b_ref, al_ref, pre_ref, dth_ref, dt_ref, acs_ref, acst_ref):
        r_i = lax.broadcasted_iota(jnp.int32, (LANES, LANES), 0)
        c_i = lax.broadcasted_iota(jnp.int32, (LANES, LANES), 1)
        live = c_i < SSM_R
        tril = jnp.where(r_i >= c_i, 1.0, 0.0)
        for h in range(PREP_CHUNKS):
            rows = slice(h * SSM_L, (h + 1) * SSM_L)
            pre = raw_ref[rows, :] + b_ref[...]
            dt = jnp.where(c_i < NH, jnp.logaddexp(pre, 0.0), 0.0)
            acs = jnp.dot(tril, dt * (-jnp.exp(al_ref[...])), preferred_element_type=F32, precision=HI)
            pre_ref[rows, :] = pre
            dth_ref[rows, :] = dt
            for g in range(SSM_G):
                acs_g = _to_group(acs, g, live)
                dt_ref[g, rows, :] = _to_group(dt, g, live)
                acs_ref[g, rows, :] = acs_g
                acst_ref[g, :, rows] = acs_g.T

    row = pl.BlockSpec((1, LANES), lambda c: (0, 0))
    flat = pl.BlockSpec((PREP_CHUNKS * SSM_L, LANES), lambda c: (c, 0))
    blk = pl.BlockSpec((SSM_G, PREP_CHUNKS * SSM_L, LANES), lambda c: (0, c, 0))
    big = _sds((SSM_G, S, LANES))
    return _call(body, name="ssd_prep_fwd",
                 out_shape=(_sds((S, LANES)), _sds((S, LANES)), big, big, _sds((SSM_G, LANES, S))), grid=(nc // PREP_CHUNKS,),
                 in_specs=[flat, row, row],
                 out_specs=(flat, flat, blk, blk, pl.BlockSpec((SSM_G, LANES, PREP_CHUNKS * SSM_L), lambda c: (0, 0, c))))(
                     raw, bias_row, alog_row)


def _ssd_prep_bwd(pre_h, dt_h, alog_row, ddt_g, dacs_g, dacst_g):
    S = pre_h.shape[0]
    nc = S // SSM_L
    NH = SSM_G * SSM_R

    def body(pre_ref, dt_ref, al_ref, ddt_ref, dacs_ref, dacst_ref, draw_ref, db_ref, dal_ref):
        c = pl.program_id(0)
        r_i = lax.broadcasted_iota(jnp.int32, (LANES, LANES), 0)
        c_i = lax.broadcasted_iota(jnp.int32, (LANES, LANES), 1)
        triu = jnp.where(r_i <= c_i, 1.0, 0.0)

        @pl.when(c == 0)
        def _():
            db_ref[...] = jnp.zeros_like(db_ref)
            dal_ref[...] = jnp.zeros_like(dal_ref)

        A = -jnp.exp(al_ref[...])
        for h in range(PREP_CHUNKS):
            rows = slice(h * SSM_L, (h + 1) * SSM_L)
            dacs = _from_groups([dacs_ref[g, rows, :] + dacst_ref[g, :, rows].T for g in range(SSM_G)])
            da = jnp.dot(triu, dacs, preferred_element_type=F32, precision=HI)
            ddt = _from_groups([ddt_ref[g, rows, :] for g in range(SSM_G)]) + da * A
            dpre = jnp.where(c_i < NH, ddt * _sigmoid(pre_ref[rows, :]), 0.0)
            draw_ref[rows, :] = dpre
            db_ref[...] += jnp.sum(dpre, axis=0, keepdims=True)
            dal_ref[...] += jnp.where(c_i[:1] < NH, jnp.sum(da * dt_ref[rows, :], axis=0, keepdims=True) * A, 0.0)

    row = pl.BlockSpec((1, LANES), lambda c: (0, 0))
    flat = pl.BlockSpec((PREP_CHUNKS * SSM_L, LANES), lambda c: (c, 0))
    blk = pl.BlockSpec((SSM_G, PREP_CHUNKS * SSM_L, LANES), lambda c: (0, c, 0))
    return _call(body, name="ssd_prep_bwd", out_shape=(_sds((S, LANES)), _sds((1, LANES)), _sds((1, LANES))),
                 grid=(nc // PREP_CHUNKS,),
                 in_specs=[flat, flat, row, blk, blk,
                           pl.BlockSpec((SSM_G, LANES, PREP_CHUNKS * SSM_L), lambda c: (0, 0, c))],
                 out_specs=(flat, row, row))(pre_h, dt_h, alog_row, ddt_g, dacs_g, dacst_g)


NT = (((1,), (1,)), ((), ()))
TN = (((0,), (0,)), ((), ()))
SSM_HP = SSM_R * SSM_P


def _ssd_group_terms(xs_ref, dt_ref, acs_ref, d_ref):
    hid = lax.broadcasted_iota(jnp.int32, (1, SSM_HP), 1) // SSM_P
    rid = lax.broadcasted_iota(jnp.int32, (SSM_HP, 1), 0) // SSM_P

    def widen(cols):
        out = cols[0]
        for r in range(1, SSM_R):
            out = jnp.where(hid == r, cols[r], out)
        return out

    dt_c = [dt_ref[:, r:r + 1] for r in range(SSM_R)]
    acs_c = [acs_ref[:, r:r + 1] for r in range(SSM_R)]
    last = [acs_ref[SSM_L - 1:SSM_L, r:r + 1] for r in range(SSM_R)]
    decay_c = [jnp.exp(last[r] - acs_c[r]) for r in range(SSM_R)]
    cd = [jnp.exp(last[r]) for r in range(SSM_R)]
    cd_rows = cd[0]
    for r in range(1, SSM_R):
        cd_rows = jnp.where(rid == r, cd[r], cd_rows)
    xs = xs_ref[...]
    return (xs, xs * widen(dt_c), widen([jnp.exp(a) for a in acs_c]), widen(decay_c),
            widen([d_ref[:, r:r + 1] for r in range(SSM_R)]), cd_rows, dt_c, decay_c, cd)


def _ssd_lmat(acs_ref, acst_ref, r, tril):
    return jnp.exp(jnp.where(tril, acs_ref[:, r:r + 1] - acst_ref[r:r + 1, :], -jnp.inf))


SSM_GPS = 8


def _ssd_specs(rev, nc):
    def cc(c):
        return nc - 1 - c if rev else c
    xs_blocks = (SSM_G * SSM_HP) // (SSM_GPS * SSM_N)
    xs = pl.BlockSpec((SSM_L, SSM_GPS * SSM_HP), lambda g, c: (cc(c), g))
    bsp = pl.BlockSpec((SSM_L, SSM_GPS * SSM_N), lambda g, c: (cc(c), xs_blocks + g))
    csp = pl.BlockSpec((SSM_L, SSM_GPS * SSM_N), lambda g, c: (cc(c), xs_blocks + SSM_G // SSM_GPS + g))
    sc = pl.BlockSpec((SSM_GPS, SSM_L, LANES), lambda g, c: (g, cc(c), 0))
    sct = pl.BlockSpec((SSM_GPS, LANES, SSM_L), lambda g, c: (g, 0, cc(c)))
    gsp = pl.BlockSpec((SSM_GPS, 1, LANES), lambda g, c: (g, 0, 0))
    st = pl.BlockSpec((None, SSM_GPS, SSM_HP, SSM_N), lambda g, c: (cc(c), g, 0, 0))
    return xs, bsp, csp, sc, sct, gsp, st


def _interleave(gens):
    live = list(gens)
    while live:
        for g in list(live):
            try:
                next(g)
            except StopIteration:
                live.remove(g)


def _rounds(gens):
    live = list(gens)
    while live:
        for g in list(live):
            try:
                next(g)
            except StopIteration:
                live.remove(g)
        yield


def _ssd_group_views(gg, xs_ref, b_ref, c_ref, *per_group):
    return (xs_ref.at[:, gg * SSM_HP:(gg + 1) * SSM_HP], b_ref.at[:, gg * SSM_N:(gg + 1) * SSM_N],
            c_ref.at[:, gg * SSM_N:(gg + 1) * SSM_N]) + tuple(r.at[gg] for r in per_group)


def _ssd_fwd(xbc, dt_g, acs_g, acst_g, d_g):
    S = xbc.shape[0]
    nc = S // SSM_L
    xs_s, b_s, c_s, sc, sct, gsp, st = _ssd_specs(False, nc)

    def body(xs_ref, b_ref, c_ref, dt_ref, acs_ref, acst_ref, d_ref, y_ref, st_ref, state):
        c = pl.program_id(1)

        @pl.when(c == 0)
        def _():
            state[...] = jnp.zeros_like(state)

        tril = lax.broadcasted_iota(jnp.int32, (SSM_L, SSM_L), 0) >= lax.broadcasted_iota(jnp.int32, (SSM_L, SSM_L), 1)
        def group(gg):
            xs_v, b_v, c_v, dt_v, acs_v, acst_v, d_v, st_v, state_v = _ssd_group_views(
                gg, xs_ref, b_ref, c_ref, dt_ref, acs_ref, acst_ref, d_ref, st_ref, state)
            y_v = y_ref.at[:, gg * SSM_HP:(gg + 1) * SSM_HP]
            Bb, Cb = b_v[...].astype(BF16), c_v[...].astype(BF16)
            Gm = lax.dot_general(Cb, Bb, NT, preferred_element_type=F32)
            yield
            xs, X, e_all, decay_all, d_all, cd_rows, _, _, _ = _ssd_group_terms(xs_v, dt_v, acs_v, d_v)
            S_all = state_v[...]
            st_v[...] = S_all
            yield
            yo = lax.dot_general(Cb, S_all.astype(BF16), NT, preferred_element_type=F32)
            new_state = lax.dot_general((X * decay_all).astype(BF16), Bb, TN, preferred_element_type=F32)
            yield
            y_v[...] = e_all * yo + d_all * xs
            state_v[...] = S_all * cd_rows + new_state

            def head(r):
                sl = slice(r * SSM_P, (r + 1) * SSM_P)
                M = Gm * _ssd_lmat(acs_v, acst_v, r, tril)
                yield
                y_v[:, sl] += jnp.dot(M.astype(BF16), X[:, sl].astype(BF16), preferred_element_type=F32)

            yield from _rounds([head(r) for r in range(SSM_R)])

        _interleave([group(gg) for gg in range(SSM_GPS)])

    return _call(body, name="ssd_fwd",
                 out_shape=(_sds((S, SSM_G * SSM_HP)), _sds((nc, SSM_G, SSM_HP, SSM_N))),
                 grid=(SSM_G // SSM_GPS, nc), in_specs=[xs_s, b_s, c_s, sc, sc, sct, gsp],
                 out_specs=(xs_s, pl.BlockSpec((None, SSM_GPS, SSM_HP, SSM_N), lambda g, c: (c, g, 0, 0))),
                 scratch=[pltpu.VMEM((SSM_GPS, SSM_HP, SSM_N), F32)])(xbc, xbc, xbc, dt_g, acs_g, acst_g, d_g)


def _ssd_bwd(xbc, dt_g, acs_g, acst_g, d_g, states, dy):
    S = xbc.shape[0]
    nc = S // SSM_L
    xs_s, b_s, c_s, sc, sct, gsp, st = _ssd_specs(True, nc)
    bc_out = pl.BlockSpec((SSM_L, SSM_GPS * SSM_N), lambda g, c: (nc - 1 - c, g))

    def body(xs_ref, b_ref, c_ref, dt_ref, acs_ref, acst_ref, d_ref, st_ref, dy_ref,
             dxs_ref, db_ref, dc_ref, ddt_ref, dacs_ref, dacst_ref, dd_ref, dstate):
        c = pl.program_id(1)

        @pl.when(c == 0)
        def _():
            dstate[...] = jnp.zeros_like(dstate)
            dd_ref[...] = jnp.zeros_like(dd_ref)

        tril = lax.broadcasted_iota(jnp.int32, (SSM_L, SSM_L), 0) >= lax.broadcasted_iota(jnp.int32, (SSM_L, SSM_L), 1)
        lane = lax.broadcasted_iota(jnp.int32, (1, LANES), 1)
        subl = lax.broadcasted_iota(jnp.int32, (LANES, 1), 0)
        last_row = lax.broadcasted_iota(jnp.int32, (SSM_L, 1), 0) == SSM_L - 1
        triu = lax.broadcasted_iota(jnp.int32, (SSM_L, SSM_L), 0) <= lax.broadcasted_iota(jnp.int32, (SSM_L, SSM_L), 1)
        def group(gg):
            xs_v, b_v, c_v, dt_v, acs_v, acst_v, d_v, st_v, ddt_v, dacs_v, dacst_v, dd_v, dstate_v = _ssd_group_views(
                gg, xs_ref, b_ref, c_ref, dt_ref, acs_ref, acst_ref, d_ref, st_ref, ddt_ref, dacs_ref, dacst_ref, dd_ref, dstate)
            dy_v, dxs_v = (r.at[:, gg * SSM_HP:(gg + 1) * SSM_HP] for r in (dy_ref, dxs_ref))
            db_v, dc_v = (r.at[:, gg * SSM_N:(gg + 1) * SSM_N] for r in (db_ref, dc_ref))
            Bb, Cb = b_v[...].astype(BF16), c_v[...].astype(BF16)
            Gm = lax.dot_general(Cb, Bb, NT, preferred_element_type=F32)
            GmT = lax.dot_general(Bb, Cb, NT, preferred_element_type=F32)
            yield
            xs, X, e_all, decay_all, d_all, cd_rows, dt_c, decay_c, cd = _ssd_group_terms(xs_v, dt_v, acs_v, d_v)
            S_all, dSn_all, dY = st_v[...], dstate_v[...], dy_v[...]
            Sb, dSnb = S_all.astype(BF16), dSn_all.astype(BF16)
            yield
            T = lax.dot_general(Cb, Sb, NT, preferred_element_type=F32)
            dT = (dY * e_all).astype(BF16)
            dC = jnp.dot(dT, Sb, preferred_element_type=F32)
            dS_prev = lax.dot_general(dT, Cb, TN, preferred_element_type=F32)
            yield
            yo_dy = dY * (e_all * T)
            W = lax.dot_general(Bb, dSnb, NT, preferred_element_type=F32)
            dB = jnp.dot((X * decay_all).astype(BF16), dSnb, preferred_element_type=F32)
            yield
            xw = X * W
            dcd_rows = jnp.sum(dSn_all * S_all, axis=1, keepdims=True)
            dstate_v[...] = dS_prev + dSn_all * cd_rows
            dX_state = W * decay_all
            yield
            acc = dict(dG=jnp.zeros((SSM_L, SSM_L), F32), dGT=jnp.zeros((SSM_L, SSM_L), F32),
                       ddt=jnp.zeros((SSM_L, LANES), F32), dacs=jnp.zeros((SSM_L, LANES), F32),
                       dacst=jnp.zeros((LANES, SSM_L), F32), dd=jnp.zeros((1, LANES), F32))

            def head(r):
                sl = slice(r * SSM_P, (r + 1) * SSM_P)
                Lm = _ssd_lmat(acs_v, acst_v, r, tril)
                LmT = jnp.exp(jnp.where(triu, acst_v[r:r + 1, :] - acs_v[:, r:r + 1], -jnp.inf))
                M = Gm * Lm
                yield
                dYh, xs_h = dY[:, sl], xs[:, sl]
                dYb, Xb = dYh.astype(BF16), X[:, sl].astype(BF16)
                dM = lax.dot_general(dYb, Xb, NT, preferred_element_type=F32)
                yield
                dX = jnp.dot((GmT * LmT).astype(BF16), dYb, preferred_element_type=F32) + dX_state[:, sl]
                acc["dG"] = acc["dG"] + dM * Lm
                acc["dGT"] = acc["dGT"] + lax.dot_general(Xb, dYb, NT, preferred_element_type=F32) * LmT
                yield
                dseg = dM * M
                dd = jnp.sum(xw[:, sl], axis=1, keepdims=True) * decay_c[r]
                dcd = jnp.sum(dcd_rows[sl])
                dacs_col = (jnp.sum(dseg, axis=1, keepdims=True) + jnp.sum(yo_dy[:, sl], axis=1, keepdims=True) - dd
                            + jnp.where(last_row, dcd * cd[r] + jnp.sum(dd), 0.0))
                dacs_row = -jnp.sum(dseg, axis=0, keepdims=True)
                yield
                dxs_v[:, sl] = dX * dt_c[r] + d_all[:, sl] * dYh
                acc["ddt"] = acc["ddt"] + jnp.where(lane == r, jnp.sum(dX * xs_h, axis=1, keepdims=True), 0.0)
                acc["dacs"] = acc["dacs"] + jnp.where(lane == r, dacs_col, 0.0)
                acc["dacst"] = acc["dacst"] + jnp.where(subl == r, dacs_row, 0.0)
                acc["dd"] = acc["dd"] + jnp.where(lane == r, jnp.sum(dYh * xs_h), 0.0)

            yield from _rounds([head(r) for r in range(SSM_R)])
            dc_v[...] = dC + jnp.dot(acc["dG"].astype(BF16), Bb, preferred_element_type=F32)
            db_v[...] = dB + jnp.dot(acc["dGT"].astype(BF16), Cb, preferred_element_type=F32)
            ddt_v[...] = acc["ddt"]
            dacs_v[...] = acc["dacs"]
            dacst_v[...] = acc["dacst"]
            dd_v[...] += acc["dd"]

        _interleave([group(gg) for gg in range(SSM_GPS)])

    big = _sds((SSM_G, S, LANES))
    return _call(body, name="ssd_bwd",
                 out_shape=(_sds((S, SSM_G * SSM_HP)), _sds((S, SSM_G * SSM_N)), _sds((S, SSM_G * SSM_N)),
                            big, big, _sds((SSM_G, LANES, S)), _sds((SSM_G, 1, LANES))),
                 grid=(SSM_G // SSM_GPS, nc), in_specs=[xs_s, b_s, c_s, sc, sc, sct, gsp, st, xs_s],
                 out_specs=(xs_s, bc_out, bc_out, sc, sc, sct, gsp),
                 scratch=[pltpu.VMEM((SSM_GPS, SSM_HP, SSM_N), F32)])(xbc, xbc, xbc, dt_g, acs_g, acst_g, d_g, states, dy)


def _gate_norm_fwd(y, proj, w):
    S, DI = y.shape
    tm = _tile(S, 256)

    def body(y_ref, z_ref, w_ref, o_ref):
        z = z_ref[...]
        gn = y_ref[...] * (z * _sigmoid(z))
        r = lax.rsqrt(jnp.mean(gn * gn, axis=-1, keepdims=True) + SSM_NORM_EPS)
        o_ref[...] = (gn * r * w_ref[...]).astype(BF16)

    row = pl.BlockSpec((tm, DI), lambda i: (i, 0))
    return _call(body, name="gate_norm_fwd", out_shape=_sds((S, DI), BF16), grid=(S // tm,),
                 in_specs=[row, row, pl.BlockSpec((1, DI), lambda i: (0, 0))], out_specs=row)(y, proj, w)


def _gate_norm_bwd(y, proj, w, dout):
    S, DI = y.shape
    tm = _tile(S, 256)

    def body(y_ref, z_ref, w_ref, d_ref, dy_ref, dz_ref, dw_ref):
        z, yv = z_ref[...], y_ref[...]
        sz = z * _sigmoid(z)
        dgn, dw = _norm_bwd_math(yv * sz, w_ref[...], d_ref[...].astype(F32), SSM_NORM_EPS)
        dy_ref[...] = dgn * sz
        dz_ref[...] = (dgn * yv * _silu_grad(z)).astype(BF16)

        @pl.when(pl.program_id(0) == 0)
        def _():
            dw_ref[...] = jnp.zeros_like(dw_ref)

        dw_ref[...] += dw

    row = pl.BlockSpec((tm, DI), lambda i: (i, 0))
    vec = pl.BlockSpec((1, DI), lambda i: (0, 0))
    return _call(body, name="gate_norm_bwd", out_shape=(_sds((S, DI)), _sds((S, DI), BF16), _sds((1, DI))), grid=(S // tm,),
                 in_specs=[row, row, vec, row], out_specs=(row, row, vec))(y, proj, w, dout)


def _group_major(v):
    return jnp.pad(v.reshape(SSM_G, 1, SSM_R), ((0, 0), (0, 0), (0, LANES - SSM_R)))


def _ungroup(t):
    return t[:, :SSM_R].reshape(1, SSM_G * SSM_R)


def _ffn_fwd(x, P, l, need):
    h = _rmsnorm(x, P["norm_ffn"][l:l + 1], name=f"ffn{l}_norm")
    need(f"ffn{l}_up", h)
    a = _mm(h, P[f"ffn_w_upT{l}"], tb=True, name=f"ffn{l}_up")
    need(f"ffn{l}_down", a)
    act = _ffn_mid_fwd(a, P["ffn_conv_w"], P["ffn_conv_b"], l)
    out = _mm(act, P[f"ffn_w_down{l}"], res=x, name=f"ffn{l}_down")
    return out, (x, h, a, act)


def _ffn_bwd(saved, P, l, dx, emit):
    x, h, a, act = saved
    wT = P[f"ffn_w_upT{l}"]
    F = wT.shape[0] // 2
    dact = _mm(dx, P[f"ffn_w_down{l}"], tb=True, out_dtype=BF16, name=f"ffn{l}_down_dx")
    dw_down = _mm(act, dx, ta=True, out_dtype=PAYLOAD, name=f"ffn{l}_down_dw")
    dau, dag, dcw, dcb = _ffn_mid_bwd(a, P["ffn_conv_w"], P["ffn_conv_b"], l, dact)
    dw_upT = _mm(dau, h, ta=True, out_dtype=PAYLOAD, out_rows=(2 * F, 0, None), name=f"ffn{l}_up_u_dw")
    dw_upT = _mm(dag, h, ta=True, out_dtype=PAYLOAD, out_rows=(2 * F, F, dw_upT), name=f"ffn{l}_up_g_dw")
    tie = emit(f"ffn{l}", {"ffn_w_upT": dw_upT, "ffn_w_down": dw_down})
    dh = _mm(dau, wT, b_rows=(0, F), name=f"ffn{l}_up_u_dx")
    dx_in, dnw = _mm_norm_bwd(dag, wT, x, P["norm_ffn"][l:l + 1], dx, res=dh, b_rows=(F, F), after=tie,
                              name=f"ffn{l}_up_g_dx_norm_bwd")
    return dx_in, dnw, dcw, dcb


def _local_step(x, positions, target, P, need, emit, after=None):
    S, D = x.shape
    inv_freq = ROPE_THETA ** (-jnp.arange(0, HEAD_DIM, 2, dtype=F32) / HEAD_DIM)
    inv_freq = jnp.tile(inv_freq, LANES // (HEAD_DIM // 2)).reshape(1, LANES)
    cos, sin = _rope_tables(positions, inv_freq)

    nm0 = P["norm_mix"][0:1]
    h0 = _rmsnorm(x, nm0, name="mix_norm", after=after)
    need("mix_in", h0)
    proj0 = _mm(h0, P["mix_w_inT"], tb=True, name="mix_in")
    cat0 = _attn_fwd(proj0, cos, sin, P["attn_sinks"], _pool_fwd(proj0, P["pool_w"][0], P["pool_scale"]))
    need("mix_out", cat0)
    x1 = _mm(cat0, P["mix_w_out"], res=x, name="mix_out")
    x2, ffn0 = _ffn_fwd(x1, P, 0, need)

    nm1 = P["norm_mix"][1:2]
    h1 = _rmsnorm(x2, nm1, name="ssm_norm_in")
    need("ssm", h1)
    w1T, wdtT = P["ssm_w_inT"], P["ssm_wdtT"]
    DI, CD, NH = P["ssm_norm"].shape[1], P["ssm_conv_w"].shape[1], P["ssm_dt_bias"].shape[1]
    z = _mm(h1, w1T, tb=True, b_rows=(0, DI), name="ssm_in_z")
    xbcp = _mm(h1, w1T, tb=True, b_rows=(DI, CD), name="ssm_in_xbc")
    dtraw = _mm(h1, wdtT, tb=True, name="ssm_in_dt")
    xbc = _conv_silu_fwd(xbcp, P["ssm_conv_w"], P["ssm_conv_b"])
    bias_row = jnp.pad(P["ssm_dt_bias"], ((0, 0), (0, LANES - NH)))
    alog_row = jnp.pad(P["ssm_A_log"], ((0, 0), (0, LANES - NH)))
    d_g = _group_major(P["ssm_D"])
    pre_h, dt_h, dt_g, acs_g, acst_g = _ssd_prep_fwd(dtraw, bias_row, alog_row)
    y, states = _ssd_fwd(xbc, dt_g, acs_g, acst_g, d_g)
    yn = _gate_norm_fwd(y, z, P["ssm_norm"])
    need("ssm_out", yn)
    x3 = _mm(yn, P["ssm_w_out"], res=x2, name="ssm_out")
    x4, ffn1 = _ffn_fwd(x3, P, 1, need)

    loss, dx, d_norm_final = _final_loss(x4, P["norm_final"].reshape(1, D), target, name="final_loss")
    dx, dnf1, dcw1, dcb1 = _ffn_bwd(ffn1, P, 1, dx, emit)
    dyn = _mm(dx, P["ssm_w_out"], tb=True, out_dtype=BF16, name="ssm_out_dx")
    d_w_out1 = _mm(yn, dx, ta=True, out_dtype=PAYLOAD, name="ssm_out_dw")
    dy, dz, d_ssm_norm = _gate_norm_bwd(y, z, P["ssm_norm"], dyn)
    dxs, dB, dC, ddt_g, dacs_g, dacst_g, dd_g = _ssd_bwd(xbc, dt_g, acs_g, acst_g, d_g, states, dy)
    draw, dbias_row, dalog_row = _ssd_prep_bwd(pre_h, dt_h, alog_row, ddt_g, dacs_g, dacst_g)
    dxbc, d_conv_w1, d_conv_b1 = _conv_silu_bwd(xbcp, P["ssm_conv_w"], P["ssm_conv_b"], [dxs, dB, dC])
    rows = DI + CD + NH
    d_w1T = _mm(dz, h1, ta=True, out_dtype=PAYLOAD, out_rows=(rows, 0, None), name="ssm_in_z_dw")
    d_w1T = _mm(dxbc, h1, ta=True, out_dtype=PAYLOAD, out_rows=(rows, DI, d_w1T), name="ssm_in_xbc_dw")
    d_w1T = _mm(draw[:, :NH], h1, ta=True, out_dtype=PAYLOAD, out_rows=(rows, DI + CD, d_w1T), name="ssm_in_dt_dw")
    tie = emit("ssm", {"ssm_w_inT": d_w1T, "ssm_w_out": d_w_out1,
                       "ssm_conv_w": d_conv_w1, "ssm_conv_b": d_conv_b1, "ssm_norm": d_ssm_norm})
    dh1 = _mm(dz, w1T, b_rows=(0, DI), name="ssm_in_z_dx")
    dh1 = _mm(dxbc, w1T, b_rows=(DI, CD), res=dh1, name="ssm_in_xbc_dx")
    dx, dnm1 = _mm_norm_bwd(draw, wdtT, x2, nm1, dx, res=dh1, after=tie, name="ssm_in_dt_dx_norm_bwd")
    dx, dnf0, dcw0, dcb0 = _ffn_bwd(ffn0, P, 0, dx, emit)
    dcat = _mm(dx, P["mix_w_out"], tb=True, name="mix_out_dx")
    d_w_out0 = _mm(cat0, dx, ta=True, out_dtype=PAYLOAD, name="mix_out_dw")
    dproj0, dsk = _attn_bwd(proj0, cos, sin, P["attn_sinks"], dcat)
    dproj0, d_pool_w, d_pool_scale = _pool_bwd(proj0, P["pool_w"][0], P["pool_scale"], dcat, dproj0)
    d_w_in0 = _mm(dproj0, h0, ta=True, out_dtype=PAYLOAD, name="mix_in_dw")
    tie = emit("mix", {"mix_w_inT": d_w_in0, "mix_w_out": d_w_out0, "ffn_conv_w": jnp.stack([dcw0, dcw1])})
    grad_x, dnm0 = _mm_norm_bwd(dproj0, P["mix_w_inT"], x, nm0, dx, after=tie, name="mix_in_dx_norm_bwd")

    small = {
        "norm_mix": jnp.concatenate([dnm0, dnm1], axis=0),
        "norm_ffn": jnp.concatenate([dnf0, dnf1], axis=0),
        "norm_final": d_norm_final,
        "pool_w": d_pool_w,
        "pool_scale": d_pool_scale,
        "attn_sinks_rows": dsk,
        "ssm_dt_bias_row": dbias_row, "ssm_A_log_row": dalog_row, "ssm_D_g": dd_g,
        "ffn_conv_b": jnp.concatenate([dcb0, dcb1], axis=0),
    }
    return loss, grad_x, small


def _peer(k):
    x, y, c = lax.axis_index("x"), lax.axis_index("y"), lax.axis_index("c")
    px = 1 - x if k & 4 else x
    py = 1 - y if k & 2 else y
    pc = 1 - c if k & 1 else c
    return (px, py, pc), 4 * px + 2 * py + pc


def _my_index():
    return 4 * lax.axis_index("x") + 2 * lax.axis_index("y") + lax.axis_index("c")


def _land_sds(a, mode, gather):
    if mode == "slab":
        return _sds(((N_DEV,) + a.shape) if gather else a.shape, a.dtype)
    assert mode == "rows", mode
    return _sds((N_DEV * a.shape[0],) + a.shape[1:] if gather else (N_DEV, a.shape[0] // N_DEV) + a.shape[1:], a.dtype)


def _part(ref, mode, shape, idx):
    if mode == "slab":
        return ref.at[idx]
    r = shape[0] // N_DEV
    return ref.at[pl.ds(idx * r, r)]


def _remote_copies(ops, gather, srcs, lands, send_sems, recv_sems):
    me = _my_index()
    n = len(ops)
    out = []
    for k in range(1, N_DEV):
        dev, idx = _peer(k)
        for i, (a, mode) in enumerate(ops):
            s = srcs[i] if gather else _part(srcs[i], mode, a.shape, idx)
            d = _part(lands[i], mode, _land_sds(a, mode, gather).shape, me) if gather else lands[i].at[me]
            out.append(pltpu.make_async_remote_copy(src_ref=s, dst_ref=d, send_sem=send_sems.at[(k - 1) * n + i],
                                                    recv_sem=recv_sems.at[(k - 1) * n + i], device_id=dev,
                                                    device_id_type=pl.DeviceIdType.MESH))
    return out


HBM = pl.BlockSpec(memory_space=pltpu.HBM)
SEM = pl.BlockSpec(memory_space=pltpu.SEMAPHORE)
SIDE_EFFECT = pltpu.SideEffectType.DATAFLOW_SIDE_EFFECTING


def _in_hbm(a):
    return pltpu.with_memory_space_constraint(a, pltpu.HBM)


def _place_own(ops, *, gather, name):
    n = len(ops)

    def zeros(k):
        return (0,) * k

    in_specs, out_specs = [], []
    for a, mode in ops:
        nd = a.ndim
        if gather and mode == "slab":
            in_specs.append(pl.BlockSpec(a.shape, lambda i, nd=nd: zeros(nd)))
            out_specs.append(pl.BlockSpec((1,) + a.shape, lambda i, nd=nd: (_my_index(),) + zeros(nd)))
        elif gather:
            in_specs.append(pl.BlockSpec(a.shape, lambda i, nd=nd: zeros(nd)))
            out_specs.append(pl.BlockSpec(a.shape, lambda i, nd=nd: (_my_index(),) + zeros(nd - 1)))
        elif mode == "slab":
            in_specs.append(pl.BlockSpec((1,) + a.shape[1:], lambda i, nd=nd: (_my_index(),) + zeros(nd - 1)))
            out_specs.append(pl.BlockSpec((1,) + a.shape[1:], lambda i, nd=nd: (_my_index(),) + zeros(nd - 1)))
        else:
            r = a.shape[0] // N_DEV
            in_specs.append(pl.BlockSpec((r,) + a.shape[1:], lambda i, nd=nd: (_my_index(),) + zeros(nd - 1)))
            out_specs.append(pl.BlockSpec((1, r) + a.shape[1:], lambda i, nd=nd: (_my_index(),) + zeros(nd)))

    def body(*refs):
        for i_ref, o_ref in zip(refs[:n], refs[n:2 * n]):
            if o_ref.shape == i_ref.shape:
                o_ref[...] = i_ref[...]
            else:
                o_ref[0] = i_ref[...]

    outs = _call(body, name=name, grid=(1,), in_specs=in_specs, out_specs=out_specs + [ANY] * n,
                 out_shape=[_land_sds(a, m, gather) for a, m in ops] + [_sds(a.shape, a.dtype) for a, _ in ops],
                 aliases={i: n + i for i in range(n)})(*[a for a, _ in ops])
    return outs[:n], [(src, m) for src, (_, m) in zip(outs[n:], ops)]


def _exchange_start(groups, *, gather, name):
    sizes = [len(ops) for ops, _ in groups]
    n = sum(sizes)
    G = len(groups)

    def body(*refs):
        srcs, lands = refs[:n], refs[n:2 * n]
        sems = refs[2 * n:2 * n + 2 * G]
        token = refs[-1]
        off = 0
        for g, (ops, _) in enumerate(groups):
            for cp in _remote_copies(ops, gather, srcs[off:off + sizes[g]], lands[off:off + sizes[g]], sems[2 * g], sems[2 * g + 1]):
                cp.start()
            off += sizes[g]
        token[...] = jnp.zeros_like(token)

    srcs = [a for ops, _ in groups for a, _ in ops]
    lands = [l for _, ls in groups for l in ls]
    sem_shapes = [pltpu.SemaphoreType.DMA((s * (N_DEV - 1),)) for s in sizes for _ in range(2)]
    outs = pl.pallas_call(
        body, name=name,
        out_shape=sem_shapes + [pltpu.HBM(a.shape, a.dtype) for a in srcs + lands] + [_sds((8, LANES))],
        in_specs=[HBM] * (2 * n), out_specs=[SEM] * (2 * G) + [HBM] * (2 * n) + [pl.BlockSpec(memory_space=pltpu.VMEM)],
        input_output_aliases={i: 2 * G + i for i in range(2 * n)},
        compiler_params=pltpu.CompilerParams(has_side_effects=SIDE_EFFECT))(*[_in_hbm(a) for a in srcs + lands])
    sems, thru, token = outs[:2 * G], outs[2 * G:2 * G + 2 * n], outs[-1]
    states, off = [], 0
    for g, s in enumerate(sizes):
        states.append((sems[2 * g], sems[2 * g + 1], thru[off:off + s], thru[n + off:n + off + s]))
        off += s
    return states, token


def _exchange_wait(ops, state, after, *, gather, name):
    send_sems, recv_sems, srcs, lands = state
    n = len(ops)

    def body(*refs):
        for cp in _remote_copies(ops, gather, refs[:n], refs[n:2 * n], refs[2 * n], refs[2 * n + 1]):
            cp.wait_send()
            cp.wait_recv()

    outs = pl.pallas_call(
        body, name=name, out_shape=[pltpu.HBM(a.shape, a.dtype) for a in list(srcs) + list(lands)],
        in_specs=[HBM] * (2 * n) + [SEM, SEM, ANY], out_specs=[HBM] * (2 * n),
        input_output_aliases={i: i for i in range(2 * n)},
        compiler_params=pltpu.CompilerParams(has_side_effects=SIDE_EFFECT))(*srcs, *lands, send_sems, recv_sems, after)
    return outs[n:]


ADAM_ROWS = 256


def _row_tile(R, cap=ADAM_ROWS):
    best = R
    if R > cap:
        for d in range(16, cap + 1, 16):
            if R % d == 0:
                best = d
    return best


def _adamw(g_layers, w, m, v, *, name):
    L = len(g_layers)
    J, R, Wd = g_layers[0].shape
    assert w.shape == (L, R, Wd), (g_layers[0].shape, w.shape)
    tr = _row_tile(R)
    nrt = R // tr
    c1 = 1.0 / (1.0 - ADAM_B1 ** ADAM_STEP)
    c2 = 1.0 / (1.0 - ADAM_B2 ** ADAM_STEP)

    def body(*refs):
        g_refs = refs[:L]
        w_ref, m_ref, v_ref, go_ref, d_ref, mo_ref, vo_ref = refs[L:]
        layer = pl.program_id(0)
        g = None
        for l, g_ref in enumerate(g_refs):
            gl = g_ref[0].astype(F32)
            for j in range(1, J):
                gl = gl + g_ref[j].astype(F32)
            g = gl if g is None else jnp.where(layer == l, gl, g)
        mn = ADAM_B1 * m_ref[...] + (1.0 - ADAM_B1) * g
        vn = ADAM_B2 * v_ref[...] + (1.0 - ADAM_B2) * (g * g)
        go_ref[...] = g
        mo_ref[...] = mn
        vo_ref[...] = vn
        d_ref[...] = -ADAM_LR * ((mn * c1) / (jnp.sqrt(vn * c2) + ADAM_EPS) + ADAM_WD * w_ref[...])

    def g_spec(l):
        return pl.BlockSpec((J, tr, Wd), lambda ll, i: (0, jnp.where(ll == l, i, jnp.where(ll < l, 0, nrt - 1)), 0))

    row = pl.BlockSpec((None, tr, Wd), lambda ll, i: (ll, i, 0))
    out = _sds((L, R, Wd))
    return _call(body, name=name, out_shape=(out, out, out, out), grid=(L, nrt),
                 in_specs=[g_spec(l) for l in range(L)] + [row, row, row], out_specs=(row, row, row, row))(*g_layers, w, m, v)


def _sum_slabs(slabs, *, name):
    n = len(slabs)

    def body(*refs):
        for g_ref, o_ref in zip(refs[:n], refs[n:]):
            g = g_ref[0]
            for j in range(1, g_ref.shape[0]):
                g = g + g_ref[j]
            o_ref[...] = g

    return _call(body, name=name, out_shape=[_sds(s.shape[1:]) for s in slabs])(*slabs)


def kernel(x, positions, norm_mix, norm_ffn, norm_final, mix_w_in, pool_w, pool_scale, attn_sinks, mix_w_out, ssm_w_in, ssm_conv_w, ssm_conv_b, ssm_dt_bias, ssm_A_log, ssm_D, ssm_norm, ssm_w_out, ffn_w_up, ffn_conv_w, ffn_conv_b, ffn_w_down, loss_target, m_norm_mix, m_norm_ffn, m_norm_final, m_mix_w_in, m_pool_w, m_pool_scale, m_attn_sinks, m_mix_w_out, m_ssm_w_in, m_ssm_conv_w, m_ssm_conv_b, m_ssm_dt_bias, m_ssm_A_log, m_ssm_D, m_ssm_norm, m_ssm_w_out, m_ffn_w_up, m_ffn_conv_w, m_ffn_conv_b, m_ffn_w_down, v_norm_mix, v_norm_ffn, v_norm_final, v_mix_w_in, v_pool_w, v_pool_scale, v_attn_sinks, v_mix_w_out, v_ssm_w_in, v_ssm_conv_w, v_ssm_conv_b, v_ssm_dt_bias, v_ssm_A_log, v_ssm_D, v_ssm_norm, v_ssm_w_out, v_ffn_w_up, v_ffn_conv_w, v_ffn_conv_b, v_ffn_w_down):
    args = dict(locals())
    wl = {n: args[n] for n in WEIGHTS}
    ml = {n: args["m_" + n] for n in WEIGHTS}
    vl = {n: args["v_" + n] for n in WEIGHTS}
    F = ffn_w_down.shape[1] * N_DEV
    DI, CD, NH = ssm_norm.shape[1] * N_DEV, ssm_conv_b.shape[1] * N_DEV, ssm_dt_bias.shape[1]
    Kc, Kf = ssm_conv_w.shape[1], ffn_conv_w.shape[1]
    n_up = ffn_w_up.shape[2]
    col_sharded = ("mix_w_in", "ssm_w_in", "ffn_w_up")

    def tr(a):
        return jnp.swapaxes(a, -1, -2)

    def two(a):
        return a.reshape(-1, a.shape[-1])

    def pay(a):
        return a.astype(PAYLOAD)

    order = ("mix_in", "mix_out", "ffn0_up", "ffn0_down", "ssm", "ssm_out", "ffn1_up", "ffn1_down")
    gops = {
        "mix_in": [(pay(tr(mix_w_in)[0]), "rows")],
        "mix_out": [(pay(mix_w_out[0]), "rows"), (two(ssm_conv_w), "slab"), (ssm_conv_b, "slab"), (ssm_norm, "slab"),
                    (two(ffn_conv_w), "slab")],
        "ffn0_up": [(pay(tr(ffn_w_up)[0]), "rows")], "ffn0_down": [(pay(ffn_w_down[0]), "rows")],
        "ssm": [(pay(tr(ssm_w_in)[0]), "slab")], "ssm_out": [(pay(ssm_w_out[0]), "rows")],
        "ffn1_up": [(pay(tr(ffn_w_up)[1]), "rows")], "ffn1_down": [(pay(ffn_w_down[1]), "rows")],
    }
    lands, handed = _place_own([op for g in order for op in gops[g]], gather=True, name="gather_own")
    groups, off = [], 0
    for g in order:
        gops[g] = handed[off:off + len(gops[g])]
        groups.append((gops[g], lands[off:off + len(gops[g])]))
        off += len(gops[g])
    gstates, token = _exchange_start(groups, gather=True, name="gather_start")
    gstate = dict(zip(order, gstates))
    P = {n: wl[n] for n in REPLICATED}

    def need(g, after):
        got = _exchange_wait(gops[g], gstate[g], after, gather=True, name="gather_wait_" + g)
        if g == "mix_in":
            P["mix_w_inT"] = got[0]
        elif g == "mix_out":
            P.update(mix_w_out=got[0], ssm_conv_w=got[1].transpose(1, 0, 2).reshape(Kc, CD), ssm_conv_b=got[2].reshape(1, CD),
                     ssm_norm=got[3].reshape(1, DI), ffn_conv_w=got[4].transpose(1, 0, 2).reshape(2, Kf, 2 * F))
        elif g == "ssm":
            w1T = got[0].reshape(-1, got[0].shape[-1])
            P.update(ssm_w_inT=w1T, ssm_wdtT=jnp.pad(w1T[DI + CD:], ((0, LANES - NH), (0, 0))))
        elif g == "ssm_out":
            P["ssm_w_out"] = got[0]
        elif g.endswith("_up"):
            P["ffn_w_upT" + g[3]] = got[0]
        else:
            P["ffn_w_down" + g[3]] = got[0]

    sent = {}

    def emit(g, d):
        if g == "mix":
            ops = [(d["mix_w_inT"], "rows"), (d["mix_w_out"], "rows"),
                   (d["ffn_conv_w"].reshape(2 * Kf, N_DEV, n_up).transpose(1, 0, 2), "slab")]
        elif g == "ssm":
            ops = [(d["ssm_w_inT"].reshape(N_DEV, -1, d["ssm_w_inT"].shape[-1]), "slab"), (d["ssm_w_out"], "rows"),
                   (d["ssm_conv_w"].reshape(Kc, N_DEV, -1).transpose(1, 0, 2), "slab"),
                   (d["ssm_conv_b"].reshape(N_DEV, 1, -1), "slab"), (d["ssm_norm"].reshape(N_DEV, 1, -1), "slab")]
        else:
            ops = [(d["ffn_w_upT"], "rows"), (d["ffn_w_down"], "rows")]
        own, ops = _place_own(ops, gather=False, name="scatter_own_" + g)
        (state,), tok = _exchange_start([(ops, own)], gather=False, name="scatter_start_" + g)
        sent[g] = (ops, state)
        return tok

    loss_lanes, grad_x, G = _local_step(x[0], positions.reshape(-1, 1), loss_target[0], P, need, emit, after=token)

    res = {}

    def update(n, g_layers):
        L = len(g_layers)
        g_layers = [g.reshape(g.shape[0], -1, g.shape[-1]) for g in g_layers]
        shape = (L,) + g_layers[0].shape[1:]
        view = tr if n in col_sharded else (lambda a: a)
        outs = _adamw(g_layers, view(wl[n]).reshape(shape), view(ml[n]).reshape(shape), view(vl[n]).reshape(shape),
                      name="adamw_" + n)
        for kind, a in zip(("grad", "delta", "new_m", "new_v"), outs):
            res[kind, n] = view(a.reshape(view(wl[n]).shape))

    rep_ops = [(a, "slab") for a in (G["norm_mix"], G["norm_ffn"], G["norm_final"], G["pool_w"].reshape(-1, LANES),
                                     G["pool_scale"], G["ffn_conv_b"], G["attn_sinks_rows"],
                                     G["ssm_dt_bias_row"], G["ssm_A_log_row"], G["ssm_D_g"].reshape(SSM_G, LANES),
                                     loss_lanes)]
    rep_own, rep_ops = _place_own(rep_ops, gather=True, name="small_own")
    (rep_state,), rep_token = _exchange_start([(rep_ops, rep_own)], gather=True, name="small_start")

    recv = {g: _exchange_wait(sent[g][0], sent[g][1], rep_token, gather=False, name="scatter_wait_" + g)
            for g in ("ffn1", "ssm", "ffn0")}
    update("ssm_w_in", [recv["ssm"][0]])
    update("ssm_w_out", [recv["ssm"][1]])
    update("ssm_conv_w", [recv["ssm"][2]])
    update("ssm_conv_b", [recv["ssm"][3]])
    update("ssm_norm", [recv["ssm"][4]])
    update("ffn_w_up", [recv["ffn0"][0], recv["ffn1"][0]])
    update("ffn_w_down", [recv["ffn0"][1], recv["ffn1"][1]])
    recv["mix"] = _exchange_wait(sent["mix"][0], sent["mix"][1], res["new_v", "ffn_w_down"], gather=False,
                                 name="scatter_wait_mix")
    update("mix_w_in", [recv["mix"][0]])
    update("mix_w_out", [recv["mix"][1]])
    update("ffn_conv_w", [recv["mix"][2]])

    rep = _exchange_wait(rep_ops, rep_state, res["new_v", "mix_w_out"], gather=True, name="small_wait")
    for n, r in zip(("norm_mix", "norm_ffn", "norm_final", "pool_w", "pool_scale", "ffn_conv_b"), rep):
        update(n, [r])
    sinks_rows, bias_row, alog_row, d_g, loss_sum = _sum_slabs(rep[6:], name="sum_head_grads")
    update("attn_sinks", [sinks_rows[:, 0].reshape(1, 1, N_HEADS)])
    update("ssm_dt_bias", [bias_row[:, :NH][None]])
    update("ssm_A_log", [alog_row[:, :NH][None]])
    update("ssm_D", [_ungroup(d_g)[None]])
    loss = loss_sum[0, 0]

    return (loss, grad_x[None], *[res[k, n] for k in ("grad", "delta", "new_m", "new_v") for n in WEIGHTS])
```
